```python
import jax, jax.numpy as jnp
from jax import lax
import numpy as np

D_MODEL = 1024
BATCH = 8
SEQ = 8192
DEPTH = 1

N_META = 16
HEAD_DIM = 64
ATTN_HEADS = (D_MODEL // 2) // HEAD_DIM
KV_HEADS = 2
GQA_GROUP = ATTN_HEADS // KV_HEADS
ATTN_WIDTH = ATTN_HEADS * HEAD_DIM
KV_WIDTH = KV_HEADS * HEAD_DIM
LRU_WIDTH = D_MODEL // 2
LRU_BLOCKS = 8
LRU_BLOCK = LRU_WIDTH // LRU_BLOCKS
LRU_C = 8.0
CONV_WIDTH = 4
WINDOW = 128
BLOCK = 128
PAD = BLOCK - N_META
MIX_WIDTH = ATTN_WIDTH + LRU_WIDTH
IN_WIDTH = ATTN_WIDTH + 2 * KV_WIDTH + 2 * LRU_WIDTH
D_FF = 4 * D_MODEL
EPS = 1e-6
NEG = -1e30

kernel_name = "hymba_griffin_swa_sink_hybrid"


def rmsnorm(x, g):
    xf = x.astype(jnp.float32)
    y = xf * lax.rsqrt(jnp.mean(xf * xf, axis=-1, keepdims=True) + EPS)
    return (y * g.astype(jnp.float32)).astype(x.dtype)


def causal_depthwise_conv(x, w, b):
    c = x.shape[-1]
    y = lax.conv_general_dilated(
        x, w[:, None, :].astype(x.dtype), window_strides=(1,),
        padding=[(CONV_WIDTH - 1, 0)],
        dimension_numbers=("NWC", "WIO", "NWC"), feature_group_count=c)
    return y + b.astype(x.dtype)


def rg_lru(x, w_a, b_a, w_x, b_x, lam):
    bsz, t, _ = x.shape
    xb = x.reshape(bsz, t, LRU_BLOCKS, LRU_BLOCK)
    gate_r = jnp.einsum("btnc,ncd->btnd", xb, w_a).reshape(bsz, t, LRU_WIDTH) + b_a
    gate_i = jnp.einsum("btnc,ncd->btnd", xb, w_x).reshape(bsz, t, LRU_WIDTH) + b_x
    r = jax.nn.sigmoid(gate_r.astype(jnp.float32))
    i = jax.nn.sigmoid(gate_i.astype(jnp.float32))
    log_a = -LRU_C * r * jax.nn.softplus(-lam.astype(jnp.float32))
    a = jnp.exp(log_a)
    mult = jnp.sqrt(-jnp.expm1(2.0 * log_a))
    u = mult * (i * x.astype(jnp.float32))

    def combine(left, right):
        a_l, b_l = left
        a_r, b_r = right
        return a_l * a_r, a_r * b_l + b_r

    _, h = lax.associative_scan(combine, (a, u), axis=1)
    return h


def sliding_window_attention_with_sinks(q, k, v, sinks):
    bsz, t, _, _ = q.shape
    pad_cfg = ((0, 0), (PAD, 0), (0, 0), (0, 0))
    qp, kp, vp = jnp.pad(q, pad_cfg), jnp.pad(k, pad_cfg), jnp.pad(v, pad_cfg)
    tp = t + PAD
    nb = tp // BLOCK
    qb = qp.reshape(bsz, nb, BLOCK, KV_HEADS, GQA_GROUP, HEAD_DIM)
    kb = kp.reshape(bsz, nb, BLOCK, KV_HEADS, HEAD_DIM)
    vb = vp.reshape(bsz, nb, BLOCK, KV_HEADS, HEAD_DIM)
    blk_pad = ((0, 0), (1, 0), (0, 0), (0, 0), (0, 0))
    kk = jnp.concatenate([jnp.pad(kb, blk_pad)[:, :-1], kb], axis=2)
    vv = jnp.concatenate([jnp.pad(vb, blk_pad)[:, :-1], vb], axis=2)
    scale = HEAD_DIM ** -0.5
    s = jnp.einsum("bnqkgd,bnskd->bnkgqs", qb, kk,
                   preferred_element_type=jnp.float32) * scale
    blk = jnp.arange(nb)[:, None] * BLOCK
    qpos = blk + jnp.arange(BLOCK)[None, :]
    kpos = blk - BLOCK + jnp.arange(2 * BLOCK)[None, :]
    diff = qpos[:, :, None] - kpos[:, None, :]
    valid = (diff >= 0) & (diff < WINDOW) & (kpos[:, None, :] >= PAD)
    s = jnp.where(valid[None, :, None, None], s, NEG)
    sink = sinks.astype(jnp.float32).reshape(1, 1, KV_HEADS, GQA_GROUP, 1, 1)
    m = jnp.maximum(jnp.max(s, axis=-1, keepdims=True), sink)
    p = jnp.exp(s - m)
    denom = jnp.sum(p, axis=-1, keepdims=True) + jnp.exp(sink - m)
    o = jnp.einsum("bnkgqs,bnskd->bnqkgd", (p / denom).astype(v.dtype), vv)
    return o.reshape(bsz, tp, ATTN_WIDTH)[:, PAD:]


def _fwd_setup_inputs(seed: int = 0) -> dict:
    key = jax.random.key(seed)
    ks = jax.random.split(key, 20)
    f32 = jnp.float32

    def nrm(k, shape, scale):
        return jax.random.normal(k, shape, f32) * scale

    u = jax.random.uniform(ks[10], (DEPTH, LRU_WIDTH), f32, 0.9, 0.999)
    a0 = u ** (1.0 / LRU_C)
    lru_lambda = jnp.log(a0) - jnp.log1p(-a0)
    return {
        "x": nrm(ks[0], (BATCH, SEQ, D_MODEL), 1.0),
        "meta_tokens": nrm(ks[1], (N_META, D_MODEL), 1.0),
        "g_pre_mix": 1.0 + nrm(ks[2], (DEPTH, D_MODEL), 0.05),
        "w_in": nrm(ks[3], (DEPTH, D_MODEL, IN_WIDTH), D_MODEL ** -0.5),
        "conv_w": nrm(ks[4], (DEPTH, CONV_WIDTH, LRU_WIDTH), CONV_WIDTH ** -0.5),
        "conv_b": nrm(ks[5], (DEPTH, LRU_WIDTH), 0.01),
        "w_a": nrm(ks[6], (DEPTH, LRU_BLOCKS, LRU_BLOCK, LRU_BLOCK), LRU_BLOCK ** -0.5),
        "b_a": nrm(ks[7], (DEPTH, LRU_WIDTH), 0.01),
        "w_x": nrm(ks[8], (DEPTH, LRU_BLOCKS, LRU_BLOCK, LRU_BLOCK), LRU_BLOCK ** -0.5),
        "b_x": nrm(ks[9], (DEPTH, LRU_WIDTH), 0.01),
        "lru_lambda": lru_lambda,
        "attn_sinks": nrm(ks[11], (DEPTH, ATTN_HEADS), 0.5),
        "w_out": nrm(ks[12], (DEPTH, MIX_WIDTH, D_MODEL), MIX_WIDTH ** -0.5),
        "g_post_mix": 1.0 + nrm(ks[13], (DEPTH, D_MODEL), 0.05),
        "g_pre_ffn": 1.0 + nrm(ks[14], (DEPTH, D_MODEL), 0.05),
        "w_ff1": nrm(ks[15], (DEPTH, D_MODEL, D_FF), D_MODEL ** -0.5),
        "w_ff2": nrm(ks[16], (DEPTH, D_FF, D_MODEL), D_FF ** -0.5),
        "g_post_ffn": 1.0 + nrm(ks[17], (DEPTH, D_MODEL), 0.05),
    }


def _fwd_reference(x, meta_tokens, g_pre_mix, w_in, conv_w, conv_b, w_a, b_a, w_x, b_x,
              lru_lambda, attn_sinks, w_out, g_post_mix, g_pre_ffn, w_ff1, w_ff2,
              g_post_ffn):
    bsz = x.shape[0]
    meta = jnp.broadcast_to(meta_tokens[None].astype(x.dtype), (bsz, N_META, D_MODEL))
    h = jnp.concatenate([meta, x], axis=1)
    t = h.shape[1]
    split_at = [ATTN_WIDTH, ATTN_WIDTH + KV_WIDTH, ATTN_WIDTH + 2 * KV_WIDTH,
                ATTN_WIDTH + 2 * KV_WIDTH + LRU_WIDTH]
    for l in range(DEPTH):
        u = rmsnorm(h, g_pre_mix[l])
        z = u @ w_in[l]
        q, k, v, xr, yr = jnp.split(z, split_at, axis=-1)
        attn = sliding_window_attention_with_sinks(
            q.reshape(bsz, t, ATTN_HEADS, HEAD_DIM),
            k.reshape(bsz, t, KV_HEADS, HEAD_DIM),
            v.reshape(bsz, t, KV_HEADS, HEAD_DIM),
            attn_sinks[l])
        xr = causal_depthwise_conv(xr, conv_w[l], conv_b[l])
        hr = rg_lru(xr, w_a[l], b_a[l], w_x[l], b_x[l], lru_lambda[l])
        rec = (jax.nn.gelu(yr.astype(jnp.float32)) * hr).astype(h.dtype)
        mix = jnp.concatenate([attn.astype(h.dtype), rec], axis=-1) @ w_out[l]
        h = h + rmsnorm(mix, g_post_mix[l])
        u = rmsnorm(h, g_pre_ffn[l])
        f = jnp.square(jax.nn.relu(u @ w_ff1[l])) @ w_ff2[l]
        h = h + rmsnorm(f, g_post_ffn[l])
    return h[:, N_META:]


import jax as _jax
import jax.numpy as _jnp

TWIN_FORMAT = 'train_step'
FWD_PARAMS = ['x', 'meta_tokens', 'g_pre_mix', 'w_in', 'conv_w', 'conv_b', 'w_a', 'b_a', 'w_x', 'b_x', 'lru_lambda', 'attn_sinks', 'w_out', 'g_post_mix', 'g_pre_ffn', 'w_ff1', 'w_ff2', 'g_post_ffn']
TWIN_WEIGHTS = ['meta_tokens', 'g_pre_mix', 'w_in', 'conv_w', 'conv_b', 'w_a', 'b_a', 'w_x', 'b_x', 'lru_lambda', 'attn_sinks', 'w_out', 'g_post_mix', 'g_pre_ffn', 'w_ff1', 'w_ff2', 'g_post_ffn']
TWIN_DIFF_INPUT = 'x'
TWIN_INPUTS = ['x', 'meta_tokens', 'g_pre_mix', 'w_in', 'conv_w', 'conv_b', 'w_a', 'b_a', 'w_x', 'b_x', 'lru_lambda', 'attn_sinks', 'w_out', 'g_post_mix', 'g_pre_ffn', 'w_ff1', 'w_ff2', 'g_post_ffn', 'loss_target', 'm_meta_tokens', 'm_g_pre_mix', 'm_w_in', 'm_conv_w', 'm_conv_b', 'm_w_a', 'm_b_a', 'm_w_x', 'm_b_x', 'm_lru_lambda', 'm_attn_sinks', 'm_w_out', 'm_g_post_mix', 'm_g_pre_ffn', 'm_w_ff1', 'm_w_ff2', 'm_g_post_ffn', 'v_meta_tokens', 'v_g_pre_mix', 'v_w_in', 'v_conv_w', 'v_conv_b', 'v_w_a', 'v_b_a', 'v_w_x', 'v_b_x', 'v_lru_lambda', 'v_attn_sinks', 'v_w_out', 'v_g_post_mix', 'v_g_pre_ffn', 'v_w_ff1', 'v_w_ff2', 'v_g_post_ffn']
TWIN_OUTPUTS = ['loss', 'grad_x', 'grad_meta_tokens', 'grad_g_pre_mix', 'grad_w_in', 'grad_conv_w', 'grad_conv_b', 'grad_w_a', 'grad_b_a', 'grad_w_x', 'grad_b_x', 'grad_lru_lambda', 'grad_attn_sinks', 'grad_w_out', 'grad_g_post_mix', 'grad_g_pre_ffn', 'grad_w_ff1', 'grad_w_ff2', 'grad_g_post_ffn', 'delta_meta_tokens', 'delta_g_pre_mix', 'delta_w_in', 'delta_conv_w', 'delta_conv_b', 'delta_w_a', 'delta_b_a', 'delta_w_x', 'delta_b_x', 'delta_lru_lambda', 'delta_attn_sinks', 'delta_w_out', 'delta_g_post_mix', 'delta_g_pre_ffn', 'delta_w_ff1', 'delta_w_ff2', 'delta_g_post_ffn', 'new_m_meta_tokens', 'new_m_g_pre_mix', 'new_m_w_in', 'new_m_conv_w', 'new_m_conv_b', 'new_m_w_a', 'new_m_b_a', 'new_m_w_x', 'new_m_b_x', 'new_m_lru_lambda', 'new_m_attn_sinks', 'new_m_w_out', 'new_m_g_post_mix', 'new_m_g_pre_ffn', 'new_m_w_ff1', 'new_m_w_ff2', 'new_m_g_post_ffn', 'new_v_meta_tokens', 'new_v_g_pre_mix', 'new_v_w_in', 'new_v_conv_w', 'new_v_conv_b', 'new_v_w_a', 'new_v_b_a', 'new_v_w_x', 'new_v_b_x', 'new_v_lru_lambda', 'new_v_attn_sinks', 'new_v_w_out', 'new_v_g_post_mix', 'new_v_g_pre_ffn', 'new_v_w_ff1', 'new_v_w_ff2', 'new_v_g_post_ffn']
TWIN_LEAF_KINDS = {'loss': 'loss', 'grad_x': 'grad_x', 'grad_meta_tokens': 'grad_w', 'grad_g_pre_mix': 'grad_w', 'grad_w_in': 'grad_w', 'grad_conv_w': 'grad_w', 'grad_conv_b': 'grad_w', 'grad_w_a': 'grad_w', 'grad_b_a': 'grad_w', 'grad_w_x': 'grad_w', 'grad_b_x': 'grad_w', 'grad_lru_lambda': 'grad_w', 'grad_attn_sinks': 'grad_w', 'grad_w_out': 'grad_w', 'grad_g_post_mix': 'grad_w', 'grad_g_pre_ffn': 'grad_w', 'grad_w_ff1': 'grad_w', 'grad_w_ff2': 'grad_w', 'grad_g_post_ffn': 'grad_w', 'delta_meta_tokens': 'delta_w', 'delta_g_pre_mix': 'delta_w', 'delta_w_in': 'delta_w', 'delta_conv_w': 'delta_w', 'delta_conv_b': 'delta_w', 'delta_w_a': 'delta_w', 'delta_b_a': 'delta_w', 'delta_w_x': 'delta_w', 'delta_b_x': 'delta_w', 'delta_lru_lambda': 'delta_w', 'delta_attn_sinks': 'delta_w', 'delta_w_out': 'delta_w', 'delta_g_post_mix': 'delta_w', 'delta_g_pre_ffn': 'delta_w', 'delta_w_ff1': 'delta_w', 'delta_w_ff2': 'delta_w', 'delta_g_post_ffn': 'delta_w', 'new_m_meta_tokens': 'new_m', 'new_m_g_pre_mix': 'new_m', 'new_m_w_in': 'new_m', 'new_m_conv_w': 'new_m', 'new_m_conv_b': 'new_m', 'new_m_w_a': 'new_m', 'new_m_b_a': 'new_m', 'new_m_w_x': 'new_m', 'new_m_b_x': 'new_m', 'new_m_lru_lambda': 'new_m', 'new_m_attn_sinks': 'new_m', 'new_m_w_out': 'new_m', 'new_m_g_post_mix': 'new_m', 'new_m_g_pre_ffn': 'new_m', 'new_m_w_ff1': 'new_m', 'new_m_w_ff2': 'new_m', 'new_m_g_post_ffn': 'new_m', 'new_v_meta_tokens': 'new_v', 'new_v_g_pre_mix': 'new_v', 'new_v_w_in': 'new_v', 'new_v_conv_w': 'new_v', 'new_v_conv_b': 'new_v', 'new_v_w_a': 'new_v', 'new_v_b_a': 'new_v', 'new_v_w_x': 'new_v', 'new_v_b_x': 'new_v', 'new_v_lru_lambda': 'new_v', 'new_v_attn_sinks': 'new_v', 'new_v_w_out': 'new_v', 'new_v_g_post_mix': 'new_v', 'new_v_g_pre_ffn': 'new_v', 'new_v_w_ff1': 'new_v', 'new_v_w_ff2': 'new_v', 'new_v_g_post_ffn': 'new_v'}


def _forward(args):
    return _fwd_reference(*[args[k] for k in FWD_PARAMS])


def _output_shape():
    def fwd():
        inp = _fwd_setup_inputs(0)
        return _fwd_reference(*[inp[k] for k in FWD_PARAMS])
    out = _jax.eval_shape(fwd)
    return out.shape, out.dtype

N_MICROBATCH = 1
ADAM_LR = 0.001
ADAM_B1 = 0.9
ADAM_B2 = 0.999
ADAM_EPS = 1e-08
ADAM_WD = 0.01
ADAM_STEP = 10
PER_EXAMPLE_BATCH_AXIS = {'x': 0, 'loss_target': 0}
SHARED_INPUTS = []
_WEIGHT_DTYPES = {'meta_tokens': _jnp.float32, 'g_pre_mix': _jnp.float32, 'w_in': _jnp.float32, 'conv_w': _jnp.float32, 'conv_b': _jnp.float32, 'w_a': _jnp.float32, 'b_a': _jnp.float32, 'w_x': _jnp.float32, 'b_x': _jnp.float32, 'lru_lambda': _jnp.float32, 'attn_sinks': _jnp.float32, 'w_out': _jnp.float32, 'g_post_mix': _jnp.float32, 'g_pre_ffn': _jnp.float32, 'w_ff1': _jnp.float32, 'w_ff2': _jnp.float32, 'g_post_ffn': _jnp.float32}
MOMENT_SCALE = {'meta_tokens': 1.430605e-01, 'g_pre_mix': 1.344561e+00, 'w_in': 9.098162e-01, 'conv_w': 8.219465e+00, 'conv_b': 1.064817e+02, 'w_a': 3.490966e+00, 'b_a': 2.070705e+00, 'w_x': 6.435255e+00, 'b_x': 2.091139e+00, 'lru_lambda': 2.997304e+00, 'attn_sinks': 1.367995e-01, 'w_out': 7.889409e+00, 'g_post_mix': 6.522209e+01, 'g_pre_ffn': 3.383274e+00, 'w_ff1': 1.711455e+00, 'w_ff2': 7.376575e+00, 'g_post_ffn': 6.683739e+01}


def _to_microbatches(a, axis):
    t = _jnp.moveaxis(a, axis, 0)
    t = t.reshape((N_MICROBATCH, t.shape[0] // N_MICROBATCH) + t.shape[1:])
    return _jnp.moveaxis(t, 1, axis + 1)


def setup_inputs(seed: int = 0) -> dict:
    inp = _fwd_setup_inputs(seed)
    key = _jax.random.fold_in(_jax.random.key(seed), 7919)
    shape, _ = _output_shape()
    out = dict(inp)
    out["loss_target"] = _jax.random.normal(_jax.random.fold_in(key, 0), shape, _jnp.float32)
    for i, name in enumerate(TWIN_WEIGHTS):
        w = inp[name].astype(_jnp.float32)
        if MOMENT_SCALE is None:
            s = _jnp.sqrt(_jnp.mean(_jnp.square(w)) + 1e-30)
        else:
            s = MOMENT_SCALE[name]
        km, kv = _jax.random.split(_jax.random.fold_in(key, i + 1))
        out[name] = w
        out["m_" + name] = s * _jax.random.normal(km, w.shape, _jnp.float32)
        out["v_" + name] = (s * s) * _jax.random.uniform(kv, w.shape, _jnp.float32, 0.5, 1.5)
    if N_MICROBATCH > 1:
        for name, axis in PER_EXAMPLE_BATCH_AXIS.items():
            out[name] = _to_microbatches(out[name], axis)
    return {'x': out['x'], 'meta_tokens': out['meta_tokens'], 'g_pre_mix': out['g_pre_mix'], 'w_in': out['w_in'], 'conv_w': out['conv_w'], 'conv_b': out['conv_b'], 'w_a': out['w_a'], 'b_a': out['b_a'], 'w_x': out['w_x'], 'b_x': out['b_x'], 'lru_lambda': out['lru_lambda'], 'attn_sinks': out['attn_sinks'], 'w_out': out['w_out'], 'g_post_mix': out['g_post_mix'], 'g_pre_ffn': out['g_pre_ffn'], 'w_ff1': out['w_ff1'], 'w_ff2': out['w_ff2'], 'g_post_ffn': out['g_post_ffn'], 'loss_target': out['loss_target'], 'm_meta_tokens': out['m_meta_tokens'], 'm_g_pre_mix': out['m_g_pre_mix'], 'm_w_in': out['m_w_in'], 'm_conv_w': out['m_conv_w'], 'm_conv_b': out['m_conv_b'], 'm_w_a': out['m_w_a'], 'm_b_a': out['m_b_a'], 'm_w_x': out['m_w_x'], 'm_b_x': out['m_b_x'], 'm_lru_lambda': out['m_lru_lambda'], 'm_attn_sinks': out['m_attn_sinks'], 'm_w_out': out['m_w_out'], 'm_g_post_mix': out['m_g_post_mix'], 'm_g_pre_ffn': out['m_g_pre_ffn'], 'm_w_ff1': out['m_w_ff1'], 'm_w_ff2': out['m_w_ff2'], 'm_g_post_ffn': out['m_g_post_ffn'], 'v_meta_tokens': out['v_meta_tokens'], 'v_g_pre_mix': out['v_g_pre_mix'], 'v_w_in': out['v_w_in'], 'v_conv_w': out['v_conv_w'], 'v_conv_b': out['v_conv_b'], 'v_w_a': out['v_w_a'], 'v_b_a': out['v_b_a'], 'v_w_x': out['v_w_x'], 'v_b_x': out['v_b_x'], 'v_lru_lambda': out['v_lru_lambda'], 'v_attn_sinks': out['v_attn_sinks'], 'v_w_out': out['v_w_out'], 'v_g_post_mix': out['v_g_post_mix'], 'v_g_pre_ffn': out['v_g_pre_ffn'], 'v_w_ff1': out['v_w_ff1'], 'v_w_ff2': out['v_w_ff2'], 'v_g_post_ffn': out['v_g_post_ffn']}


def _loss(weights, diff, rest, loss_target):
    with _jax.named_scope("forward"):
        args = {**rest, TWIN_DIFF_INPUT: diff, **{k: w.astype(_WEIGHT_DTYPES[k]) for k, w in weights.items()}}
        y = _forward(args)
    with _jax.named_scope("loss_head"):
        err = _jnp.square(y.astype(_jnp.float32) - loss_target)
        return 0.5 * _jnp.sum(_jnp.mean(err, axis=-1)) if err.ndim else 0.5 * err


def _adamw(w, g, m, v):
    m = ADAM_B1 * m + (1.0 - ADAM_B1) * g
    v = ADAM_B2 * v + (1.0 - ADAM_B2) * _jnp.square(g)
    m_hat = m / (1.0 - ADAM_B1 ** ADAM_STEP)
    v_hat = v / (1.0 - ADAM_B2 ** ADAM_STEP)
    delta = -ADAM_LR * (m_hat / (_jnp.sqrt(v_hat) + ADAM_EPS) + ADAM_WD * w)
    return delta, m, v


def reference(x, meta_tokens, g_pre_mix, w_in, conv_w, conv_b, w_a, b_a, w_x, b_x, lru_lambda, attn_sinks, w_out, g_post_mix, g_pre_ffn, w_ff1, w_ff2, g_post_ffn, loss_target, m_meta_tokens, m_g_pre_mix, m_w_in, m_conv_w, m_conv_b, m_w_a, m_b_a, m_w_x, m_b_x, m_lru_lambda, m_attn_sinks, m_w_out, m_g_post_mix, m_g_pre_ffn, m_w_ff1, m_w_ff2, m_g_post_ffn, v_meta_tokens, v_g_pre_mix, v_w_in, v_conv_w, v_conv_b, v_w_a, v_b_a, v_w_x, v_b_x, v_lru_lambda, v_attn_sinks, v_w_out, v_g_post_mix, v_g_pre_ffn, v_w_ff1, v_w_ff2, v_g_post_ffn):
    given = dict(x=x, meta_tokens=meta_tokens, g_pre_mix=g_pre_mix, w_in=w_in, conv_w=conv_w, conv_b=conv_b, w_a=w_a, b_a=b_a, w_x=w_x, b_x=b_x, lru_lambda=lru_lambda, attn_sinks=attn_sinks, w_out=w_out, g_post_mix=g_post_mix, g_pre_ffn=g_pre_ffn, w_ff1=w_ff1, w_ff2=w_ff2, g_post_ffn=g_post_ffn, loss_target=loss_target, m_meta_tokens=m_meta_tokens, m_g_pre_mix=m_g_pre_mix, m_w_in=m_w_in, m_conv_w=m_conv_w, m_conv_b=m_conv_b, m_w_a=m_w_a, m_b_a=m_b_a, m_w_x=m_w_x, m_b_x=m_b_x, m_lru_lambda=m_lru_lambda, m_attn_sinks=m_attn_sinks, m_w_out=m_w_out, m_g_post_mix=m_g_post_mix, m_g_pre_ffn=m_g_pre_ffn, m_w_ff1=m_w_ff1, m_w_ff2=m_w_ff2, m_g_post_ffn=m_g_post_ffn, v_meta_tokens=v_meta_tokens, v_g_pre_mix=v_g_pre_mix, v_w_in=v_w_in, v_conv_w=v_conv_w, v_conv_b=v_conv_b, v_w_a=v_w_a, v_b_a=v_b_a, v_w_x=v_w_x, v_b_x=v_b_x, v_lru_lambda=v_lru_lambda, v_attn_sinks=v_attn_sinks, v_w_out=v_w_out, v_g_post_mix=v_g_post_mix, v_g_pre_ffn=v_g_pre_ffn, v_w_ff1=v_w_ff1, v_w_ff2=v_w_ff2, v_g_post_ffn=v_g_post_ffn)
    weights = {n: given[n] for n in TWIN_WEIGHTS}
    shared = {n: given[n] for n in SHARED_INPUTS}
    per_example = {n: given[n] for n in ['x']}
    grad_fn = _jax.value_and_grad(_loss, argnums=(0, 1))

    def one_microbatch(ex, loss_target):
        ex = dict(ex)
        diff = ex.pop(TWIN_DIFF_INPUT)
        return grad_fn(weights, diff, {**shared, **ex}, loss_target)

    if N_MICROBATCH == 1:
        loss, (grad_w, grad_x) = one_microbatch(per_example, given["loss_target"])
    else:
        def body(carry, xs):
            loss_sum, grad_sum = carry
            l_k, (gw_k, gx_k) = one_microbatch(xs[0], xs[1])
            with _jax.named_scope("update"):
                return (loss_sum + l_k, _jax.tree.map(_jnp.add, grad_sum, gw_k)), gx_k

        init = (_jnp.zeros((), _jnp.float32), _jax.tree.map(_jnp.zeros_like, weights))
        (loss, grad_w), grad_x = _jax.lax.scan(body, init, (per_example, given["loss_target"]))
    with _jax.named_scope("update"):
        delta_w, new_m, new_v = {}, {}, {}
        for n in TWIN_WEIGHTS:
            delta_w[n], new_m[n], new_v[n] = _adamw(weights[n], grad_w[n], given["m_" + n], given["v_" + n])
    return (loss, grad_x, *[grad_w[n] for n in TWIN_WEIGHTS], *[delta_w[n] for n in TWIN_WEIGHTS],
            *[new_m[n] for n in TWIN_WEIGHTS], *[new_v[n] for n in TWIN_WEIGHTS])
```

```python
import functools

import jax
import jax.numpy as jnp
from jax import lax
from jax.experimental import pallas as pl
from jax.experimental.pallas import tpu as pltpu

F32 = jnp.float32
BF16 = jnp.bfloat16

D_MODEL = 1024
N_META = 16
HEAD_DIM = 64
ATTN_HEADS = 8
KV_HEADS = 2
GQA_GROUP = ATTN_HEADS // KV_HEADS
ATTN_WIDTH = ATTN_HEADS * HEAD_DIM
KV_WIDTH = KV_HEADS * HEAD_DIM
QKV_WIDTH = ATTN_WIDTH + 2 * KV_WIDTH
LRU_WIDTH = 512
LRU_BLOCKS = 8
LRU_BLOCK = 64
LRU_HALF = 256
LRU_C = 8.0
CONV_WIDTH = 4
BLOCK = 128
PAD_ROWS = BLOCK - N_META
IN_WIDTH = QKV_WIDTH + 2 * LRU_WIDTH
D_FF = 4096
EPS = 1e-6
NEG = -1e30
N_DEV = 8
FF_CHUNK = D_FF // N_DEV
SUBLANES = 8
LANES = 128

ADAM_LR = 0.001
ADAM_B1 = 0.9
ADAM_B2 = 0.999
ADAM_EPS = 1e-08
ADAM_WD = 0.01
ADAM_STEP = 10

VMEM_LIMIT = 56 * 1024 * 1024


def _row_tile(rows):
    for t in (640, 512, 256, 128):
        if rows % t == 0:
            return t
    raise ValueError(rows)


def _rec_tile(rows):
    for t in (320, 256, 128):
        if rows % t == 0:
            return t
    raise ValueError(rows)


def _params(semantics):
    return pltpu.CompilerParams(dimension_semantics=semantics, vmem_limit_bytes=VMEM_LIMIT)


def _mm(a, b):
    return lax.dot_general(a, b, (((1,), (0,)), ((), ())), preferred_element_type=F32)


def _mm_nt(a, b):
    return lax.dot_general(a, b, (((1,), (1,)), ((), ())), preferred_element_type=F32)


def _mm_tn(a, b):
    return lax.dot_general(a, b, (((0,), (0,)), ((), ())), preferred_element_type=F32)


def _rms_fwd(x, g):
    rstd = lax.rsqrt(jnp.mean(x * x, axis=-1, keepdims=True) + EPS)
    xhat = x * rstd
    return xhat * g, xhat, rstd


def _rms_bwd(dy, xhat, rstd, g):
    dyg = dy * g
    c = jnp.mean(dyg * xhat, axis=-1, keepdims=True)
    dx = rstd * (dyg - xhat * c)
    dg = jnp.sum(dy * xhat, axis=0, keepdims=True)
    return dx, dg


def _sigmoid(x):
    return 1.0 / (1.0 + jnp.exp(-x))


def _log1p(x):
    u = 1.0 + x
    return jnp.where(u == 1.0, x, jnp.log(u) * x / (u - 1.0))


def _expm1(x):
    u = jnp.exp(x)
    um1 = u - 1.0
    return jnp.where(u == 1.0, x, jnp.where(um1 == -1.0, -1.0, um1 * x / jnp.log(u)))


def _softplus(x):
    return jnp.maximum(x, 0.0) + _log1p(jnp.exp(-jnp.abs(x)))


GELU_C = 0.7978845608028654
GELU_K = 0.044715


def _gelu(x):
    t = jnp.tanh(GELU_C * (x + GELU_K * x * x * x))
    return 0.5 * x * (1.0 + t), t


def _gelu_grad(x, t):
    return 0.5 * (1.0 + t) + 0.5 * x * (1.0 - t * t) * GELU_C * (1.0 + 3.0 * GELU_K * x * x)


def _full(shape):
    return pl.BlockSpec(shape, lambda *_: (0,) * len(shape))


def _exchange(arrays, modes, name):
    na = len(arrays)

    def body(*refs):
        ins, outs = refs[:na], refs[na:2 * na]
        send_sems, recv_sems, local_sems = refs[2 * na:]
        x, y, c = lax.axis_index("x"), lax.axis_index("y"), lax.axis_index("c")
        me = 4 * x + 2 * y + c

        def block(a, dev):
            return ins[a] if modes[a] == "gather" else ins[a].at[dev]

        local = [pltpu.make_async_copy(block(a, me), outs[a].at[me], local_sems.at[a]) for a in range(na)]
        for cp in local:
            cp.start()
        peers = []
        for k in range(1, N_DEV):
            px = jnp.bitwise_xor(x, (k >> 2) & 1)
            py = jnp.bitwise_xor(y, (k >> 1) & 1)
            pc = jnp.bitwise_xor(c, k & 1)
            peers.append((px, py, pc, 4 * px + 2 * py + pc))
        sends = []
        for a in range(na):
            for k, (px, py, pc, peer) in enumerate(peers):
                cp = pltpu.make_async_remote_copy(
                    src_ref=block(a, peer), dst_ref=outs[a].at[me],
                    send_sem=send_sems.at[a, k], recv_sem=recv_sems.at[a, k],
                    device_id=(px, py, pc), device_id_type=pl.DeviceIdType.MESH)
                cp.start()
                sends.append(cp)
        for a in range(na):
            for k, (px, py, pc, peer) in enumerate(peers):
                pltpu.make_async_remote_copy(
                    src_ref=block(a, peer), dst_ref=outs[a].at[peer],
                    send_sem=send_sems.at[a, k], recv_sem=recv_sems.at[a, k],
                    device_id=(px, py, pc), device_id_type=pl.DeviceIdType.MESH).wait_recv()
        for cp in sends:
            cp.wait_send()
        for cp in local:
            cp.wait()

    out_shape = []
    for arr, mode in zip(arrays, modes):
        shape = (N_DEV,) + arr.shape if mode == "gather" else arr.shape
        out_shape.append(jax.ShapeDtypeStruct(shape, arr.dtype))
    return pl.pallas_call(
        body, name=name, out_shape=out_shape,
        in_specs=[pl.BlockSpec(memory_space=pl.ANY)] * na,
        out_specs=[pl.BlockSpec(memory_space=pl.ANY)] * na,
        scratch_shapes=[pltpu.SemaphoreType.DMA((na, N_DEV - 1)), pltpu.SemaphoreType.DMA((na, N_DEV - 1)),
                        pltpu.SemaphoreType.DMA((na,))],
        compiler_params=pltpu.CompilerParams(has_side_effects=True),
    )(*arrays)


def _in_proj_fwd(h0, g1, w_in):
    rows = h0.shape[0]
    tm = _row_tile(rows)

    def body(h_ref, g_ref, w_ref, qkv_ref, zrec_ref):
        u, _, _ = _rms_fwd(h_ref[...], g_ref[...])
        z = _mm(u.astype(BF16), w_ref[...])
        qkv_ref[...] = z[:, :QKV_WIDTH].astype(BF16)
        zrec_ref[...] = z[:, QKV_WIDTH:]

    return pl.pallas_call(
        body, name="in_proj_fwd", grid=(rows // tm,),
        in_specs=[pl.BlockSpec((tm, D_MODEL), lambda i: (i, 0)), _full((1, D_MODEL)), _full((D_MODEL, IN_WIDTH))],
        out_specs=[pl.BlockSpec((tm, QKV_WIDTH), lambda i: (i, 0)), pl.BlockSpec((tm, 2 * LRU_WIDTH), lambda i: (i, 0))],
        out_shape=[jax.ShapeDtypeStruct((rows, QKV_WIDTH), BF16), jax.ShapeDtypeStruct((rows, 2 * LRU_WIDTH), F32)],
        compiler_params=_params(("parallel",)),
    )(h0, g1, w_in)


def _attn_mask(n):
    r = lax.broadcasted_iota(jnp.int32, (GQA_GROUP * BLOCK, 2 * BLOCK), 0) & (BLOCK - 1)
    col = lax.broadcasted_iota(jnp.int32, (GQA_GROUP * BLOCK, 2 * BLOCK), 1)
    kpos = (n - 1) * BLOCK + col
    return (col > r) & (col <= r + BLOCK) & (kpos >= PAD_ROWS)


def _attn_probs(q4, k2, valid, sink_col):
    s = _mm_nt(q4, k2) * (HEAD_DIM ** -0.5)
    s = jnp.where(valid, s, NEG)
    m = jnp.maximum(jnp.max(s, axis=-1, keepdims=True), sink_col)
    p = jnp.exp(s - m)
    es = jnp.exp(sink_col - m)
    inv = 1.0 / (jnp.sum(p, axis=-1, keepdims=True) + es)
    return p * inv, es * inv


def _heads(ref, first, count):
    return jnp.concatenate([ref[:, (first + g) * HEAD_DIM:(first + g + 1) * HEAD_DIM] for g in range(count)], axis=0)


def _sink_col(sink_ref, kv):
    return jnp.concatenate([jnp.full((BLOCK, 1), sink_ref[0, kv * GQA_GROUP + g], F32) for g in range(GQA_GROUP)], axis=0)


def _attn_fwd(qkv, sinks):
    rows = qkv.shape[0]
    nb = rows // BLOCK
    k_col, v_col = ATTN_WIDTH // KV_WIDTH, ATTN_WIDTH // KV_WIDTH + 1

    def body(sink_ref, q_ref, kp_ref, kc_ref, vp_ref, vc_ref, o_ref):
        valid = _attn_mask(pl.program_id(0))
        outs = []
        for kv in range(KV_HEADS):
            sl = slice(kv * HEAD_DIM, (kv + 1) * HEAD_DIM)
            k2 = jnp.concatenate([kp_ref[:, sl], kc_ref[:, sl]], axis=0)
            v2 = jnp.concatenate([vp_ref[:, sl], vc_ref[:, sl]], axis=0)
            q4 = _heads(q_ref, kv * GQA_GROUP, GQA_GROUP)
            pn, _ = _attn_probs(q4, k2, valid, _sink_col(sink_ref, kv))
            o4 = _mm(pn.astype(BF16), v2)
            outs += [o4[g * BLOCK:(g + 1) * BLOCK] for g in range(GQA_GROUP)]
        o_ref[...] = jnp.concatenate(outs, axis=1).astype(BF16)

    prev = lambda n: jnp.maximum(n - 1, 0)
    return pl.pallas_call(
        body, name="attn_fwd", grid=(nb,),
        in_specs=[pl.BlockSpec(memory_space=pltpu.SMEM),
                  pl.BlockSpec((BLOCK, ATTN_WIDTH), lambda n: (n, 0)),
                  pl.BlockSpec((BLOCK, KV_WIDTH), lambda n: (prev(n), k_col)),
                  pl.BlockSpec((BLOCK, KV_WIDTH), lambda n: (n, k_col)),
                  pl.BlockSpec((BLOCK, KV_WIDTH), lambda n: (prev(n), v_col)),
                  pl.BlockSpec((BLOCK, KV_WIDTH), lambda n: (n, v_col))],
        out_specs=pl.BlockSpec((BLOCK, ATTN_WIDTH), lambda n: (n, 0)),
        out_shape=jax.ShapeDtypeStruct((rows, ATTN_WIDTH), BF16),
        compiler_params=_params(("parallel",)),
    )(sinks, qkv, qkv, qkv, qkv, qkv)


def _conv_taps(xbuf, tm):
    return [xbuf[pl.ds(SUBLANES - (CONV_WIDTH - 1 - j), tm), :] for j in range(CONV_WIDTH)]


def _lru_gates(xc, wa_ref, ba_ref, wx_ref, bx_ref, lam_ref):
    halves = [xc[:, h * LRU_HALF:(h + 1) * LRU_HALF].astype(BF16) for h in range(2)]
    gate_r = jnp.concatenate([_mm(halves[h], wa_ref[h]) for h in range(2)], axis=1) + ba_ref[...]
    gate_i = jnp.concatenate([_mm(halves[h], wx_ref[h]) for h in range(2)], axis=1) + bx_ref[...]
    r = _sigmoid(gate_r)
    ig = _sigmoid(gate_i)
    log_a = (-LRU_C) * r * _softplus(-lam_ref[...])
    a = jnp.exp(log_a)
    mult = jnp.sqrt(-_expm1(2.0 * log_a))
    return halves, r, ig, a, mult


def _scan_tile(a_ref, u_ref, out_ref, carry, tm, reverse):
    row = lax.broadcasted_iota(jnp.int32, (SUBLANES, LRU_WIDTH), 0)
    groups = tm // SUBLANES

    def step(j, prev):
        jj = groups - 1 - j if reverse else j
        o = pl.multiple_of(jj * SUBLANES, SUBLANES)
        a = a_ref[pl.ds(o, SUBLANES), :]
        u = u_ref[pl.ds(o, SUBLANES), :]
        for s in (1, 2, 4):
            shift = SUBLANES - s if reverse else s
            keep = (row < SUBLANES - s) if reverse else (row >= s)
            u = jnp.where(keep, a * pltpu.roll(u, shift, 0) + u, u)
            a = jnp.where(keep, a * pltpu.roll(a, shift, 0), a)
        out = a * prev + u
        out_ref[pl.ds(o, SUBLANES), :] = out
        return out[0:1, :] if reverse else out[SUBLANES - 1:SUBLANES, :]

    return lax.fori_loop(0, groups, step, carry)


def _rec_fwd(zrec, conv_w, conv_b, wa_bd, b_a, wx_bd, b_x, lam):
    rows = zrec.shape[0]
    tm = _rec_tile(rows)

    def body(xr_ref, yr_ref, cw_ref, cb_ref, wa_ref, ba_ref, wx_ref, bx_ref, lam_ref, rec_ref, h_ref,
             xbuf, a_s, u_s, carry):
        i = pl.program_id(0)

        @pl.when(i == 0)
        def _():
            xbuf[0:SUBLANES, :] = jnp.zeros((SUBLANES, LRU_WIDTH), F32)
            carry[...] = jnp.zeros_like(carry)

        @pl.when(i > 0)
        def _():
            xbuf[0:SUBLANES, :] = xbuf[tm:tm + SUBLANES, :]

        xbuf[SUBLANES:SUBLANES + tm, :] = xr_ref[...]
        taps = _conv_taps(xbuf, tm)
        xc = cb_ref[...] + sum(cw_ref[j:j + 1, :] * taps[j] for j in range(CONV_WIDTH))
        _, r, ig, a, mult = _lru_gates(xc, wa_ref, ba_ref, wx_ref, bx_ref, lam_ref)
        grow = i * tm + lax.broadcasted_iota(jnp.int32, (tm, LRU_WIDTH), 0)
        a_s[...] = a
        u_s[...] = jnp.where(grow >= PAD_ROWS, mult * (ig * xc), 0.0)
        carry[0:1, :] = _scan_tile(a_s, u_s, h_ref, carry[0:1, :], tm, reverse=False)
        gel, _ = _gelu(yr_ref[...])
        rec_ref[...] = (gel * h_ref[...]).astype(BF16)

    vec = _full((1, LRU_WIDTH))
    bd = _full((2, LRU_HALF, LRU_HALF))
    return pl.pallas_call(
        body, name="rec_fwd", grid=(rows // tm,),
        in_specs=[pl.BlockSpec((tm, LRU_WIDTH), lambda i: (i, 0)), pl.BlockSpec((tm, LRU_WIDTH), lambda i: (i, 1)),
                  _full((CONV_WIDTH, LRU_WIDTH)), vec, bd, vec, bd, vec, vec],
        out_specs=[pl.BlockSpec((tm, LRU_WIDTH), lambda i: (i, 0)), pl.BlockSpec((tm, LRU_WIDTH), lambda i: (i, 0))],
        out_shape=[jax.ShapeDtypeStruct((rows, LRU_WIDTH), BF16), jax.ShapeDtypeStruct((rows, LRU_WIDTH), F32)],
        scratch_shapes=[pltpu.VMEM((tm + SUBLANES, LRU_WIDTH), F32), pltpu.VMEM((tm, LRU_WIDTH), F32),
                        pltpu.VMEM((tm, LRU_WIDTH), F32), pltpu.VMEM((SUBLANES, LRU_WIDTH), F32)],
        compiler_params=_params(("arbitrary",)),
    )(zrec, zrec, conv_w, conv_b, wa_bd, b_a, wx_bd, b_x, lam)


def _out_proj_fwd(attn, rec, w_out, h0, g2):
    rows = h0.shape[0]
    tm = _row_tile(rows)

    def body(attn_ref, rec_ref, w_ref, h_ref, g_ref, mix_ref, h1_ref):
        mix = _mm(attn_ref[...], w_ref[0:ATTN_WIDTH, :]) + _mm(rec_ref[...], w_ref[ATTN_WIDTH:, :])
        y, _, _ = _rms_fwd(mix, g_ref[...])
        mix_ref[...] = mix
        h1_ref[...] = h_ref[...] + y

    half = pl.BlockSpec((tm, ATTN_WIDTH), lambda i: (i, 0))
    wide = pl.BlockSpec((tm, D_MODEL), lambda i: (i, 0))
    return pl.pallas_call(
        body, name="out_proj_fwd", grid=(rows // tm,),
        in_specs=[half, half, _full((D_MODEL, D_MODEL)), wide, _full((1, D_MODEL))],
        out_specs=[wide, wide],
        out_shape=[jax.ShapeDtypeStruct((rows, D_MODEL), F32)] * 2,
        compiler_params=_params(("parallel",)),
    )(attn, rec, w_out, h0, g2)


def _ffn_up(h1, g3, w1g):
    rows = h1.shape[0]
    tm = _row_tile(rows)

    def body(h_ref, g_ref, w_ref, act_ref, u_ref):
        @pl.when(pl.program_id(1) == 0)
        def _():
            u, _, _ = _rms_fwd(h_ref[...], g_ref[...])
            u_ref[...] = u.astype(BF16)

        a1 = jnp.maximum(_mm(u_ref[...], w_ref[...]), 0.0)
        act_ref[...] = (a1 * a1).astype(BF16)

    return pl.pallas_call(
        body, name="ffn_up", grid=(rows // tm, N_DEV),
        in_specs=[pl.BlockSpec((tm, D_MODEL), lambda i, j: (i, 0)), _full((1, D_MODEL)),
                  pl.BlockSpec((None, D_MODEL, FF_CHUNK), lambda i, j: (j, 0, 0))],
        out_specs=[pl.BlockSpec((tm, FF_CHUNK), lambda i, j: (i, j)), pl.BlockSpec((tm, D_MODEL), lambda i, j: (i, 0))],
        out_shape=[jax.ShapeDtypeStruct((rows, D_FF), BF16), jax.ShapeDtypeStruct((rows, D_MODEL), BF16)],
        compiler_params=_params(("parallel", "arbitrary")),
    )(h1, g3, w1g)


def _ffn_down_loss(act, w2g, h1, target, g4):
    rows = h1.shape[0]
    tm = _row_tile(rows)

    def body(act_ref, w_ref, h_ref, t_ref, g_ref, dy_ref, df_ref, dg_ref, loss_ref, acc):
        i, j = pl.program_id(0), pl.program_id(1)

        @pl.when(j == 0)
        def _():
            acc[...] = jnp.zeros_like(acc)

        acc[...] += _mm(act_ref[...], w_ref[...])

        @pl.when((i == 0) & (j == 0))
        def _():
            dg_ref[...] = jnp.zeros_like(dg_ref)
            loss_ref[...] = jnp.zeros_like(loss_ref)

        @pl.when(j == N_DEV - 1)
        def _():
            g = g_ref[...]
            y, fhat, rstd = _rms_fwd(acc[...], g)
            grow = i * tm + lax.broadcasted_iota(jnp.int32, (tm, D_MODEL), 0)
            err = jnp.where(grow >= BLOCK, h_ref[...] + y - t_ref[...], 0.0)
            loss_ref[...] += (0.5 / D_MODEL) * jnp.sum(err * err)
            dy = err * (1.0 / D_MODEL)
            df, dg = _rms_bwd(dy, fhat, rstd, g)
            dy_ref[...] = dy
            df_ref[...] = df.astype(BF16)
            dg_ref[...] += dg

    wide = pl.BlockSpec((tm, D_MODEL), lambda i, j: (i, 0))
    return pl.pallas_call(
        body, name="ffn_down_loss", grid=(rows // tm, N_DEV),
        in_specs=[pl.BlockSpec((tm, FF_CHUNK), lambda i, j: (i, j)),
                  pl.BlockSpec((None, FF_CHUNK, D_MODEL), lambda i, j: (j, 0, 0)),
                  wide, wide, _full((1, D_MODEL))],
        out_specs=[wide, wide, _full((1, D_MODEL)), _full((SUBLANES, LANES))],
        out_shape=[jax.ShapeDtypeStruct((rows, D_MODEL), F32), jax.ShapeDtypeStruct((rows, D_MODEL), BF16),
                   jax.ShapeDtypeStruct((1, D_MODEL), F32), jax.ShapeDtypeStruct((SUBLANES, LANES), F32)],
        scratch_shapes=[pltpu.VMEM((tm, D_MODEL), F32)],
        compiler_params=_params(("arbitrary", "arbitrary")),
    )(act, w2g, h1, target, g4)


def _ffn_bwd_act(df, w2g, act, w1g, h1, dy, g3):
    rows = h1.shape[0]
    tm = _row_tile(rows)

    def body(df_ref, w2_ref, act_ref, w1_ref, h_ref, dy_ref, g_ref, da_ref, dh_ref, dg_ref, acc):
        i, j = pl.program_id(0), pl.program_id(1)

        @pl.when(j == 0)
        def _():
            acc[...] = jnp.zeros_like(acc)

        dact = _mm_nt(df_ref[...], w2_ref[...])
        da = (dact * (2.0 * jnp.sqrt(act_ref[...].astype(F32)))).astype(BF16)
        da_ref[...] = da
        acc[...] += _mm_nt(da, w1_ref[...])

        @pl.when((i == 0) & (j == 0))
        def _():
            dg_ref[...] = jnp.zeros_like(dg_ref)

        @pl.when(j == N_DEV - 1)
        def _():
            g = g_ref[...]
            _, xhat, rstd = _rms_fwd(h_ref[...], g)
            dx, dg = _rms_bwd(acc[...], xhat, rstd, g)
            dh_ref[...] = dy_ref[...] + dx
            dg_ref[...] += dg

    wide = pl.BlockSpec((tm, D_MODEL), lambda i, j: (i, 0))
    chunk = pl.BlockSpec((tm, FF_CHUNK), lambda i, j: (i, j))
    return pl.pallas_call(
        body, name="ffn_bwd_act", grid=(rows // tm, N_DEV),
        in_specs=[wide, pl.BlockSpec((None, FF_CHUNK, D_MODEL), lambda i, j: (j, 0, 0)), chunk,
                  pl.BlockSpec((None, D_MODEL, FF_CHUNK), lambda i, j: (j, 0, 0)), wide, wide, _full((1, D_MODEL))],
        out_specs=[chunk, wide, _full((1, D_MODEL))],
        out_shape=[jax.ShapeDtypeStruct((rows, D_FF), BF16), jax.ShapeDtypeStruct((rows, D_MODEL), F32),
                   jax.ShapeDtypeStruct((1, D_MODEL), F32)],
        scratch_shapes=[pltpu.VMEM((tm, D_MODEL), F32)],
        compiler_params=_params(("arbitrary", "arbitrary")),
    )(df, w2g, act, w1g, h1, dy, g3)


def _ffn_bwd_weights(u2, da, act, df):
    rows = u2.shape[0]
    tm = _row_tile(rows)

    def body(u_ref, da_ref, act_ref, df_ref, dw1_ref, dw2_ref):
        @pl.when(pl.program_id(1) == 0)
        def _():
            dw1_ref[...] = jnp.zeros_like(dw1_ref)
            dw2_ref[...] = jnp.zeros_like(dw2_ref)

        dw1_ref[...] += _mm_tn(u_ref[...], da_ref[...])
        dw2_ref[...] += _mm_tn(act_ref[...], df_ref[...])

    wide = pl.BlockSpec((tm, D_MODEL), lambda j, i: (i, 0))
    chunk = pl.BlockSpec((tm, FF_CHUNK), lambda j, i: (i, j))
    return pl.pallas_call(
        body, name="ffn_bwd_weights", grid=(N_DEV, rows // tm),
        in_specs=[wide, chunk, chunk, wide],
        out_specs=[pl.BlockSpec((None, D_MODEL, FF_CHUNK), lambda j, i: (j, 0, 0)),
                   pl.BlockSpec((None, FF_CHUNK, D_MODEL), lambda j, i: (j, 0, 0))],
        out_shape=[jax.ShapeDtypeStruct((N_DEV, D_MODEL, FF_CHUNK), F32),
                   jax.ShapeDtypeStruct((N_DEV, FF_CHUNK, D_MODEL), F32)],
        compiler_params=_params(("parallel", "arbitrary")),
    )(u2, da, act, df)


def _out_proj_bwd(dh1, mix, g2, w_out, attn, rec):
    rows = dh1.shape[0]
    tm = _row_tile(rows)

    def body(dh_ref, mix_ref, g_ref, w_ref, attn_ref, rec_ref, dattn_ref, drec_ref, dw_ref, dg_ref):
        @pl.when(pl.program_id(0) == 0)
        def _():
            dw_ref[...] = jnp.zeros_like(dw_ref)
            dg_ref[...] = jnp.zeros_like(dg_ref)

        g = g_ref[...]
        _, xhat, rstd = _rms_fwd(mix_ref[...], g)
        dmix, dg = _rms_bwd(dh_ref[...], xhat, rstd, g)
        dmix = dmix.astype(BF16)
        dg_ref[...] += dg
        din = _mm_nt(dmix, w_ref[...])
        dattn_ref[...] = din[:, :ATTN_WIDTH].astype(BF16)
        drec_ref[...] = din[:, ATTN_WIDTH:]
        dw_ref[0:ATTN_WIDTH, :] += _mm_tn(attn_ref[...], dmix)
        dw_ref[ATTN_WIDTH:, :] += _mm_tn(rec_ref[...], dmix)

    half = pl.BlockSpec((tm, ATTN_WIDTH), lambda i: (i, 0))
    wide = pl.BlockSpec((tm, D_MODEL), lambda i: (i, 0))
    return pl.pallas_call(
        body, name="out_proj_bwd", grid=(rows // tm,),
        in_specs=[wide, wide, _full((1, D_MODEL)), _full((D_MODEL, D_MODEL)), half, half],
        out_specs=[half, half, _full((D_MODEL, D_MODEL)), _full((1, D_MODEL))],
        out_shape=[jax.ShapeDtypeStruct((rows, ATTN_WIDTH), BF16), jax.ShapeDtypeStruct((rows, LRU_WIDTH), F32),
                   jax.ShapeDtypeStruct((D_MODEL, D_MODEL), F32), jax.ShapeDtypeStruct((1, D_MODEL), F32)],
        compiler_params=_params(("arbitrary",)),
    )(dh1, mix, g2, w_out, attn, rec)


def _attn_bwd(qkv, dattn, sinks):
    rows = qkv.shape[0]
    nb = rows // BLOCK
    k_col, v_col = ATTN_WIDTH // KV_WIDTH, ATTN_WIDTH // KV_WIDTH + 1

    def body(sink_ref, q_ref, do_ref, kp_ref, kc_ref, vp_ref, vc_ref, dq_ref, dkv_ref, dsink_ref, dk_c, dv_c):
        n = pl.program_id(0)

        @pl.when(n == 0)
        def _():
            dk_c[...] = jnp.zeros_like(dk_c)
            dv_c[...] = jnp.zeros_like(dv_c)
            dsink_ref[...] = jnp.zeros_like(dsink_ref)

        @pl.when(n < nb)
        def _():
            valid = _attn_mask(n)
            dq_parts, dk_parts, dv_parts, dsink_rows = [], [], [], []
            for kv in range(KV_HEADS):
                sl = slice(kv * HEAD_DIM, (kv + 1) * HEAD_DIM)
                k2 = jnp.concatenate([kp_ref[:, sl], kc_ref[:, sl]], axis=0)
                v2 = jnp.concatenate([vp_ref[:, sl], vc_ref[:, sl]], axis=0)
                q4 = _heads(q_ref, kv * GQA_GROUP, GQA_GROUP)
                do4 = _heads(do_ref, kv * GQA_GROUP, GQA_GROUP)
                pn, psink = _attn_probs(q4, k2, valid, _sink_col(sink_ref, kv))
                dpn = _mm_nt(do4, v2)
                delta = jnp.sum(pn * dpn, axis=-1, keepdims=True)
                ds = ((pn * (dpn - delta)) * (HEAD_DIM ** -0.5)).astype(BF16)
                dq4 = _mm(ds, k2)
                dq_parts += [dq4[g * BLOCK:(g + 1) * BLOCK] for g in range(GQA_GROUP)]
                dk_parts.append(_mm_tn(ds, q4))
                dv_parts.append(_mm_tn(pn.astype(BF16), do4))
                sd = psink * delta
                for g in range(GQA_GROUP):
                    dsink_rows.append(jnp.full((1, LANES), -jnp.sum(sd[g * BLOCK:(g + 1) * BLOCK]), F32))
            dq_ref[...] = jnp.concatenate(dq_parts, axis=1).astype(BF16)
            dsink_ref[...] += jnp.concatenate(dsink_rows, axis=0)
            dk2 = jnp.concatenate(dk_parts, axis=1)
            dv2 = jnp.concatenate(dv_parts, axis=1)
            dkv_ref[:, 0:KV_WIDTH] = (dk_c[...] + dk2[0:BLOCK]).astype(BF16)
            dkv_ref[:, KV_WIDTH:] = (dv_c[...] + dv2[0:BLOCK]).astype(BF16)
            dk_c[...] = dk2[BLOCK:]
            dv_c[...] = dv2[BLOCK:]

        @pl.when(n == nb)
        def _():
            dkv_ref[:, 0:KV_WIDTH] = dk_c[...].astype(BF16)
            dkv_ref[:, KV_WIDTH:] = dv_c[...].astype(BF16)

    cur = lambda n: jnp.minimum(n, nb - 1)
    prev = lambda n: jnp.maximum(jnp.minimum(n, nb - 1) - 1, 0)
    return pl.pallas_call(
        body, name="attn_bwd", grid=(nb + 1,),
        in_specs=[pl.BlockSpec(memory_space=pltpu.SMEM),
                  pl.BlockSpec((BLOCK, ATTN_WIDTH), lambda n: (cur(n), 0)),
                  pl.BlockSpec((BLOCK, ATTN_WIDTH), lambda n: (cur(n), 0)),
                  pl.BlockSpec((BLOCK, KV_WIDTH), lambda n: (prev(n), k_col)),
                  pl.BlockSpec((BLOCK, KV_WIDTH), lambda n: (cur(n), k_col)),
                  pl.BlockSpec((BLOCK, KV_WIDTH), lambda n: (prev(n), v_col)),
                  pl.BlockSpec((BLOCK, KV_WIDTH), lambda n: (cur(n), v_col))],
        out_specs=[pl.BlockSpec((BLOCK, ATTN_WIDTH), lambda n: (cur(n), 0)),
                   pl.BlockSpec((BLOCK, 2 * KV_WIDTH), lambda n: (jnp.maximum(n - 1, 0), 0)),
                   _full((ATTN_HEADS, LANES))],
        out_shape=[jax.ShapeDtypeStruct((rows, ATTN_WIDTH), BF16), jax.ShapeDtypeStruct((rows, 2 * KV_WIDTH), BF16),
                   jax.ShapeDtypeStruct((ATTN_HEADS, LANES), F32)],
        scratch_shapes=[pltpu.VMEM((BLOCK, KV_WIDTH), F32), pltpu.VMEM((BLOCK, KV_WIDTH), F32)],
        compiler_params=_params(("arbitrary",)),
    )(sinks, qkv, dattn, qkv, qkv, qkv, qkv)


ROW_CONV_B, ROW_B_A, ROW_B_X, ROW_LAMBDA = 4, 5, 6, 7


def _rec_bwd(drec, zrec, h, conv_w, conv_b, wa_bd, b_a, wx_bd, b_x, lam):
    rows = zrec.shape[0]
    tm = _rec_tile(rows)
    nt = rows // tm
    per = tm // SUBLANES

    def body(drec_ref, xr_ref, yr_ref, h_ref, xhalo_ref, hhalo_ref, cw_ref, cb_ref, wa_ref, ba_ref, wx_ref, bx_ref,
             lam_ref, drz_ref, small_ref, dwa_ref, dwx_ref, xbuf, hbuf, abuf, u_s, g_s, dbuf, carry):
        s = pl.program_id(0)
        i = nt - 1 - s

        @pl.when(s == 0)
        def _():
            small_ref[...] = jnp.zeros_like(small_ref)
            dwa_ref[...] = jnp.zeros_like(dwa_ref)
            dwx_ref[...] = jnp.zeros_like(dwx_ref)
            carry[...] = jnp.zeros_like(carry)
            abuf[tm:tm + SUBLANES, :] = jnp.zeros((SUBLANES, LRU_WIDTH), F32)
            dbuf[tm:tm + SUBLANES, :] = jnp.zeros((SUBLANES, LRU_WIDTH), F32)

        first = i == 0
        xbuf[0:SUBLANES, :] = jnp.where(first, 0.0, xhalo_ref[...])
        hbuf[0:SUBLANES, :] = jnp.where(first, 0.0, hhalo_ref[...])
        xbuf[SUBLANES:SUBLANES + tm, :] = xr_ref[...]
        hbuf[SUBLANES:SUBLANES + tm, :] = h_ref[...]

        taps = _conv_taps(xbuf, tm)
        xc = cb_ref[...] + sum(cw_ref[j:j + 1, :] * taps[j] for j in range(CONV_WIDTH))
        halves, r, ig, a, mult = _lru_gates(xc, wa_ref, ba_ref, wx_ref, bx_ref, lam_ref)

        yr = yr_ref[...]
        gel, t = _gelu(yr)
        drec_t = drec_ref[...]
        dyr = drec_t * h_ref[...] * _gelu_grad(yr, t)

        abuf[0:tm, :] = a
        u_s[...] = drec_t * gel
        a_next = abuf[pl.ds(1, tm), :]
        abuf[0:tm, :] = a_next
        carry[0:1, :] = _scan_tile(abuf, u_s, g_s, carry[0:1, :], tm, reverse=True)
        abuf[tm:tm + 1, :] = a[0:1, :]
        g = g_s[...]

        grow = i * tm + lax.broadcasted_iota(jnp.int32, (tm, LRU_WIDTH), 0)
        du = jnp.where(grow >= PAD_ROWS, g, 0.0)
        da = g * hbuf[pl.ds(SUBLANES - 1, tm), :]
        dmult = du * (ig * xc)
        dig = du * (mult * xc)
        dxc = du * (mult * ig)
        a2 = a * a
        dlog_a = da * a - dmult * (a2 / mult)
        sp = _softplus(-lam_ref[...])
        dgr = (dlog_a * (-LRU_C) * sp) * (r * (1.0 - r))
        dgi = dig * (ig * (1.0 - ig))
        dlam = jnp.sum(dlog_a * r, axis=0, keepdims=True) * (LRU_C * _sigmoid(-lam_ref[...]))
        dgr_b = [dgr[:, hh * LRU_HALF:(hh + 1) * LRU_HALF].astype(BF16) for hh in range(2)]
        dgi_b = [dgi[:, hh * LRU_HALF:(hh + 1) * LRU_HALF].astype(BF16) for hh in range(2)]
        dxc = dxc + jnp.concatenate(
            [_mm_nt(dgr_b[hh], wa_ref[hh]) + _mm_nt(dgi_b[hh], wx_ref[hh]) for hh in range(2)], axis=1)
        for hh in range(2):
            dwa_ref[hh] += _mm_tn(halves[hh], dgr_b[hh])
            dwx_ref[hh] += _mm_tn(halves[hh], dgi_b[hh])

        dbuf[0:tm, :] = dxc
        dxr = sum(cw_ref[j:j + 1, :] * dbuf[pl.ds(CONV_WIDTH - 1 - j, tm), :] for j in range(CONV_WIDTH))
        dbuf[tm:tm + SUBLANES, :] = dxc[0:SUBLANES, :]
        drz_ref[:, 0:LRU_WIDTH] = dxr.astype(BF16)
        drz_ref[:, LRU_WIDTH:] = dyr.astype(BF16)

        upd = [jnp.sum(dxc * taps[j], axis=0, keepdims=True) for j in range(CONV_WIDTH)]
        upd += [jnp.sum(dxc, axis=0, keepdims=True), jnp.sum(dgr, axis=0, keepdims=True),
                jnp.sum(dgi, axis=0, keepdims=True), dlam]
        small_ref[...] += jnp.concatenate(upd, axis=0)

    rev = lambda s: nt - 1 - s
    halo = lambda s: jnp.maximum(rev(s) * per - 1, 0)
    tile0 = pl.BlockSpec((tm, LRU_WIDTH), lambda s: (rev(s), 0))
    tile1 = pl.BlockSpec((tm, LRU_WIDTH), lambda s: (rev(s), 1))
    halo0 = pl.BlockSpec((SUBLANES, LRU_WIDTH), lambda s: (halo(s), 0))
    vec = _full((1, LRU_WIDTH))
    bd = _full((2, LRU_HALF, LRU_HALF))
    big = pltpu.VMEM((tm + SUBLANES, LRU_WIDTH), F32)
    tile = pltpu.VMEM((tm, LRU_WIDTH), F32)
    return pl.pallas_call(
        body, name="rec_bwd", grid=(nt,),
        in_specs=[tile0, tile0, tile1, tile0, halo0, halo0, _full((CONV_WIDTH, LRU_WIDTH)), vec, bd, vec, bd, vec, vec],
        out_specs=[pl.BlockSpec((tm, 2 * LRU_WIDTH), lambda s: (rev(s), 0)), _full((SUBLANES, LRU_WIDTH)), bd, bd],
        out_shape=[jax.ShapeDtypeStruct((rows, 2 * LRU_WIDTH), BF16), jax.ShapeDtypeStruct((SUBLANES, LRU_WIDTH), F32),
                   jax.ShapeDtypeStruct((2, LRU_HALF, LRU_HALF), F32), jax.ShapeDtypeStruct((2, LRU_HALF, LRU_HALF), F32)],
        scratch_shapes=[big, big, big, tile, tile, big, pltpu.VMEM((SUBLANES, LRU_WIDTH), F32)],
        compiler_params=_params(("arbitrary",)),
    )(drec, zrec, zrec, h, zrec, h, conv_w, conv_b, wa_bd, b_a, wx_bd, b_x, lam)


def _in_proj_bwd(h0, g1, dh1, dq, dkv, drz, w_in):
    rows = h0.shape[0]
    tm = _row_tile(rows)
    cuts = (0, ATTN_WIDTH, QKV_WIDTH, IN_WIDTH)

    def body(h_ref, g_ref, dh1_ref, dq_ref, dkv_ref, drz_ref, w_ref, dh0_ref, dw_ref, dg_ref):
        @pl.when(pl.program_id(0) == 0)
        def _():
            dw_ref[...] = jnp.zeros_like(dw_ref)
            dg_ref[...] = jnp.zeros_like(dg_ref)

        g = g_ref[...]
        u, xhat, rstd = _rms_fwd(h_ref[...], g)
        u = u.astype(BF16)
        parts = (dq_ref[...], dkv_ref[...], drz_ref[...])
        du = sum(_mm_nt(parts[p], w_ref[:, cuts[p]:cuts[p + 1]]) for p in range(3))
        dx, dg = _rms_bwd(du, xhat, rstd, g)
        dh0_ref[...] = dh1_ref[...] + dx
        dg_ref[...] += dg
        for p in range(3):
            dw_ref[:, cuts[p]:cuts[p + 1]] += _mm_tn(u, parts[p])

    wide = pl.BlockSpec((tm, D_MODEL), lambda i: (i, 0))
    return pl.pallas_call(
        body, name="in_proj_bwd", grid=(rows // tm,),
        in_specs=[wide, _full((1, D_MODEL)), wide, pl.BlockSpec((tm, ATTN_WIDTH), lambda i: (i, 0)),
                  pl.BlockSpec((tm, 2 * KV_WIDTH), lambda i: (i, 0)), pl.BlockSpec((tm, 2 * LRU_WIDTH), lambda i: (i, 0)),
                  _full((D_MODEL, IN_WIDTH))],
        out_specs=[wide, _full((D_MODEL, IN_WIDTH)), _full((1, D_MODEL))],
        out_shape=[jax.ShapeDtypeStruct((rows, D_MODEL), F32), jax.ShapeDtypeStruct((D_MODEL, IN_WIDTH), F32),
                   jax.ShapeDtypeStruct((1, D_MODEL), F32)],
        compiler_params=_params(("arbitrary",)),
    )(h0, g1, dh1, dq, dkv, drz, w_in)


def _adamw(w, m, v, parts, name):
    rows, cols = w.shape
    tr = next((t for t in (256, 128) if rows % t == 0), rows)

    def body(w_ref, m_ref, v_ref, p_ref, g_ref, d_ref, nm_ref, nv_ref):
        g = p_ref[0].astype(F32)
        for s in range(1, N_DEV):
            g = g + p_ref[s].astype(F32)
        nm = ADAM_B1 * m_ref[...] + (1.0 - ADAM_B1) * g
        nv = ADAM_B2 * v_ref[...] + (1.0 - ADAM_B2) * (g * g)
        m_hat = nm / (1.0 - ADAM_B1 ** ADAM_STEP)
        v_hat = nv / (1.0 - ADAM_B2 ** ADAM_STEP)
        g_ref[...] = g
        d_ref[...] = (-ADAM_LR) * (m_hat / (jnp.sqrt(v_hat) + ADAM_EPS) + ADAM_WD * w_ref[...])
        nm_ref[...] = nm
        nv_ref[...] = nv

    blk = pl.BlockSpec((tr, cols), lambda i: (i, 0))
    return pl.pallas_call(
        body, name=name, grid=(rows // tr,),
        in_specs=[blk, blk, blk, pl.BlockSpec((N_DEV, tr, cols), lambda i: (0, i, 0))],
        out_specs=[blk] * 4,
        out_shape=[jax.ShapeDtypeStruct((rows, cols), F32)] * 4,
        compiler_params=_params(("parallel",)),
    )(w, m, v, parts)


def _cols_from_shards(g):
    return jnp.transpose(g, (1, 0, 2)).reshape(g.shape[1], N_DEV * g.shape[2])


def _cols_to_shards(a):
    r, c = a.shape
    return jnp.transpose(a.reshape(r, N_DEV, c // N_DEV), (1, 0, 2))


def _block_diag(w):
    per = LRU_HALF // LRU_BLOCK
    w = w.reshape(2, per, LRU_BLOCK, LRU_BLOCK)
    eye = jnp.eye(per, dtype=w.dtype)
    return (w[:, :, :, None, :] * eye[None, :, None, :, None]).reshape(2, LRU_HALF, LRU_HALF)


def _block_diag_extract(t):
    per = LRU_HALF // LRU_BLOCK
    t = t.reshape(2, per, LRU_BLOCK, per, LRU_BLOCK)
    return jnp.stack([t[:, b, :, b, :] for b in range(per)], axis=1).reshape(LRU_BLOCKS, LRU_BLOCK, LRU_BLOCK)


SMALL_NAMES = ("g_pre_mix", "conv_b", "w_a", "b_a", "w_x", "b_x", "lru_lambda", "attn_sinks",
               "g_post_mix", "g_pre_ffn", "g_post_ffn")


def _pack_small(vals):
    flat = []
    for name in SMALL_NAMES:
        a = vals[name].reshape(-1)
        flat.append(jnp.pad(a, (0, (-a.shape[0]) % LANES)))
    flat = jnp.concatenate(flat)
    rows = flat.shape[0] // LANES
    return jnp.pad(flat.reshape(rows, LANES), ((0, (-rows) % SUBLANES), (0, 0)))


def _unpack_small(packed, like):
    flat = packed.reshape(-1)
    out, at = {}, 0
    for name in SMALL_NAMES:
        n = like[name].size
        out[name] = flat[at:at + n].reshape(like[name].shape)
        at += n + (-n) % LANES
    return out


def kernel(x, meta_tokens, g_pre_mix, w_in, conv_w, conv_b, w_a, b_a, w_x, b_x, lru_lambda, attn_sinks, w_out, g_post_mix, g_pre_ffn, w_ff1, w_ff2, g_post_ffn, loss_target, m_meta_tokens, m_g_pre_mix, m_w_in, m_conv_w, m_conv_b, m_w_a, m_b_a, m_w_x, m_b_x, m_lru_lambda, m_attn_sinks, m_w_out, m_g_post_mix, m_g_pre_ffn, m_w_ff1, m_w_ff2, m_g_post_ffn, v_meta_tokens, v_g_pre_mix, v_w_in, v_conv_w, v_conv_b, v_w_a, v_b_a, v_w_x, v_b_x, v_lru_lambda, v_attn_sinks, v_w_out, v_g_post_mix, v_g_pre_ffn, v_w_ff1, v_w_ff2, v_g_post_ffn):
    weights = dict(meta_tokens=meta_tokens, g_pre_mix=g_pre_mix, w_in=w_in, conv_w=conv_w, conv_b=conv_b, w_a=w_a,
                   b_a=b_a, w_x=w_x, b_x=b_x, lru_lambda=lru_lambda, attn_sinks=attn_sinks, w_out=w_out,
                   g_post_mix=g_post_mix, g_pre_ffn=g_pre_ffn, w_ff1=w_ff1, w_ff2=w_ff2, g_post_ffn=g_post_ffn)
    mom_m = dict(meta_tokens=m_meta_tokens, g_pre_mix=m_g_pre_mix, w_in=m_w_in, conv_w=m_conv_w, conv_b=m_conv_b,
                 w_a=m_w_a, b_a=m_b_a, w_x=m_w_x, b_x=m_b_x, lru_lambda=m_lru_lambda, attn_sinks=m_attn_sinks,
                 w_out=m_w_out, g_post_mix=m_g_post_mix, g_pre_ffn=m_g_pre_ffn, w_ff1=m_w_ff1, w_ff2=m_w_ff2,
                 g_post_ffn=m_g_post_ffn)
    mom_v = dict(meta_tokens=v_meta_tokens, g_pre_mix=v_g_pre_mix, w_in=v_w_in, conv_w=v_conv_w, conv_b=v_conv_b,
                 w_a=v_w_a, b_a=v_b_a, w_x=v_w_x, b_x=v_b_x, lru_lambda=v_lru_lambda, attn_sinks=v_attn_sinks,
                 w_out=v_w_out, g_post_mix=v_g_post_mix, g_pre_ffn=v_g_pre_ffn, w_ff1=v_w_ff1, w_ff2=v_w_ff2,
                 g_post_ffn=v_g_post_ffn)
    order = list(weights)

    shards = [w_in[0].astype(BF16), w_out[0].astype(BF16), w_ff1[0].astype(BF16), w_ff2[0].astype(BF16),
              meta_tokens, conv_w[0]]
    g_win, g_wout, w1g, w2g, g_meta, g_cw = _exchange(shards, ["gather"] * len(shards), "gather_weights")
    w_in_full = _cols_from_shards(g_win)
    w_out_full = g_wout.reshape(D_MODEL, D_MODEL)
    meta_full = _cols_from_shards(g_meta)
    conv_w_full = _cols_from_shards(g_cw)

    seq = x.shape[1]
    rows = BLOCK + seq
    h0 = jnp.concatenate([jnp.zeros((PAD_ROWS, D_MODEL), F32), meta_full, x[0]], axis=0)
    target = jnp.pad(loss_target[0], ((BLOCK, 0), (0, 0)))
    wa_bd = _block_diag(w_a[0]).astype(BF16)
    wx_bd = _block_diag(w_x[0]).astype(BF16)

    qkv, zrec = _in_proj_fwd(h0, g_pre_mix, w_in_full)
    attn = _attn_fwd(qkv, attn_sinks)
    rec, h_lru = _rec_fwd(zrec, conv_w_full, conv_b, wa_bd, b_a, wx_bd, b_x, lru_lambda)
    mix, h1 = _out_proj_fwd(attn, rec, w_out_full, h0, g_post_mix)
    act, u2 = _ffn_up(h1, g_pre_ffn, w1g)
    dy, df, dg_post_ffn, loss_acc = _ffn_down_loss(act, w2g, h1, target, g_post_ffn)

    da1, dh1, dg_pre_ffn = _ffn_bwd_act(df, w2g, act, w1g, h1, dy, g_pre_ffn)
    dw1g, dw2g = _ffn_bwd_weights(u2, da1, act, df)
    dattn, drec, dw_out, dg_post_mix = _out_proj_bwd(dh1, mix, g_post_mix, w_out_full, attn, rec)
    dq, dkv, dsinks = _attn_bwd(qkv, dattn, attn_sinks)
    drz, rec_small, dwa_bd, dwx_bd = _rec_bwd(drec, zrec, h_lru, conv_w_full, conv_b, wa_bd, b_a, wx_bd, b_x, lru_lambda)
    dh0, dw_in, dg_pre_mix = _in_proj_bwd(h0, g_pre_mix, dh1, dq, dkv, drz, w_in_full)

    small_grads = dict(
        g_pre_mix=dg_pre_mix, conv_b=rec_small[ROW_CONV_B], w_a=_block_diag_extract(dwa_bd), b_a=rec_small[ROW_B_A],
        w_x=_block_diag_extract(dwx_bd), b_x=rec_small[ROW_B_X], lru_lambda=rec_small[ROW_LAMBDA],
        attn_sinks=dsinks[:, 0], g_post_mix=dg_post_mix, g_pre_ffn=dg_pre_ffn, g_post_ffn=dg_post_ffn)
    partial = [
        _cols_to_shards(dw_in).astype(BF16),
        dw_out.reshape(N_DEV, D_MODEL // N_DEV, D_MODEL).astype(BF16),
        dw1g.astype(BF16),
        dw2g.astype(BF16),
        _cols_to_shards(dh0[PAD_ROWS:BLOCK]),
        _cols_to_shards(rec_small[0:CONV_WIDTH]),
        _pack_small(small_grads),
    ]
    modes = ["scatter"] * 6 + ["gather"]
    p_win, p_wout, p_w1, p_w2, p_meta, p_cw, p_small = _exchange(partial, modes, "exchange_grads")

    res = {}
    res["w_in"] = _adamw(w_in[0], m_w_in[0], v_w_in[0], p_win, "adamw_w_in")
    res["w_out"] = _adamw(w_out[0], m_w_out[0], v_w_out[0], p_wout, "adamw_w_out")
    res["w_ff1"] = _adamw(w_ff1[0], m_w_ff1[0], v_w_ff1[0], p_w1, "adamw_w_ff1")
    res["w_ff2"] = _adamw(w_ff2[0], m_w_ff2[0], v_w_ff2[0], p_w2, "adamw_w_ff2")
    res["meta_tokens"] = _adamw(meta_tokens, m_meta_tokens, v_meta_tokens, p_meta, "adamw_meta")
    res["conv_w"] = _adamw(conv_w[0], m_conv_w[0], v_conv_w[0], p_cw, "adamw_conv_w")
    small = _adamw(_pack_small(weights), _pack_small(mom_m), _pack_small(mom_v), p_small, "adamw_small")
    small = [_unpack_small(t, weights) for t in small]
    for name in SMALL_NAMES:
        res[name] = tuple(t[name] for t in small)
    for name in ("w_in", "w_out", "w_ff1", "w_ff2", "conv_w"):
        res[name] = tuple(t[None] for t in res[name])

    loss = lax.psum(loss_acc[0, 0], ("x", "y", "c"))
    grad_x = dh0[BLOCK:][None]
    outs = [loss, grad_x]
    for k in range(4):
        outs += [res[name][k] for name in order]
    return tuple(outs)
```

```python
import functools

import jax
import jax.numpy as jnp
from jax import lax
from jax.experimental import pallas as pl
from jax.experimental.pallas import tpu as pltpu

F32 = jnp.float32
BF16 = jnp.bfloat16

D_MODEL = 1024
N_META = 16
HEAD_DIM = 64
ATTN_HEADS = 8
KV_HEADS = 2
GQA_GROUP = ATTN_HEADS // KV_HEADS
ATTN_WIDTH = ATTN_HEADS * HEAD_DIM
KV_WIDTH = KV_HEADS * HEAD_DIM
QKV_WIDTH = ATTN_WIDTH + 2 * KV_WIDTH
LRU_WIDTH = 512
LRU_BLOCKS = 8
LRU_BLOCK = 64
LRU_HALF = 256
LRU_C = 8.0
CONV_WIDTH = 4
BLOCK = 128
PAD_ROWS = BLOCK - N_META
IN_WIDTH = QKV_WIDTH + 2 * LRU_WIDTH
D_FF = 4096
EPS = 1e-6
NEG = -1e30
N_DEV = 8
FF_CHUNK = D_FF // N_DEV
SUBLANES = 8
LANES = 128

ADAM_LR = 0.001
ADAM_B1 = 0.9
ADAM_B2 = 0.999
ADAM_EPS = 1e-08
ADAM_WD = 0.01
ADAM_STEP = 10

VMEM_LIMIT = 56 * 1024 * 1024


def _row_tile(rows):
    for t in (640, 512, 256, 128):
        if rows % t == 0:
            return t
    raise ValueError(rows)


def _big_tile(rows):
    for t in (1664, 1024, 512, 256, 128):
        if rows % t == 0:
            return t
    raise ValueError(rows)


def _rec_tile(rows):
    for t in (320, 256, 128):
        if rows % t == 0:
            return t
    raise ValueError(rows)


def _params(semantics):
    return pltpu.CompilerParams(dimension_semantics=semantics, vmem_limit_bytes=VMEM_LIMIT)


def _mm(a, b):
    return lax.dot_general(a, b, (((1,), (0,)), ((), ())), preferred_element_type=F32)


def _mm_nt(a, b):
    return lax.dot_general(a, b, (((1,), (1,)), ((), ())), preferred_element_type=F32)


def _mm_tn(a, b):
    return lax.dot_general(a, b, (((0,), (0,)), ((), ())), preferred_element_type=F32)


def _rms_fwd(x, g):
    rstd = lax.rsqrt(jnp.mean(x * x, axis=-1, keepdims=True) + EPS)
    xhat = x * rstd
    return xhat * g, xhat, rstd


def _rms_bwd(dy, xhat, rstd, g):
    dyg = dy * g
    c = jnp.mean(dyg * xhat, axis=-1, keepdims=True)
    dx = rstd * (dyg - xhat * c)
    dg = jnp.sum(dy * xhat, axis=0, keepdims=True)
    return dx, dg


def _sigmoid(x):
    return 1.0 / (1.0 + jnp.exp(-x))


def _log1p(x):
    u = 1.0 + x
    return jnp.where(u == 1.0, x, jnp.log(u) * x / (u - 1.0))


def _expm1(x):
    u = jnp.exp(x)
    um1 = u - 1.0
    return jnp.where(u == 1.0, x, jnp.where(um1 == -1.0, -1.0, um1 * x / jnp.log(u)))


def _softplus(x):
    return jnp.maximum(x, 0.0) + _log1p(jnp.exp(-jnp.abs(x)))


GELU_C = 0.7978845608028654
GELU_K = 0.044715


def _gelu(x):
    t = jnp.tanh(GELU_C * (x + GELU_K * x * x * x))
    return 0.5 * x * (1.0 + t), t


def _gelu_grad(x, t):
    return 0.5 * (1.0 + t) + 0.5 * x * (1.0 - t * t) * GELU_C * (1.0 + 3.0 * GELU_K * x * x)


def _full(shape):
    return pl.BlockSpec(shape, lambda *_: (0,) * len(shape))


def _resident(shape):
    return pl.BlockSpec(shape, lambda *_: (0,) * len(shape), pipeline_mode=pl.Buffered(1))


def _exchange(arrays, modes, name):
    na = len(arrays)

    def body(*refs):
        ins, outs = refs[:na], refs[na:2 * na]
        send_sems, recv_sems, local_sems = refs[2 * na:]
        x, y, c = lax.axis_index("x"), lax.axis_index("y"), lax.axis_index("c")
        me = 4 * x + 2 * y + c

        def block(a, dev):
            return ins[a] if modes[a] == "gather" else ins[a].at[dev]

        local = [pltpu.make_async_copy(block(a, me), outs[a].at[me], local_sems.at[a]) for a in range(na)]
        for cp in local:
            cp.start()
        peers = []
        for k in range(1, N_DEV):
            px = jnp.bitwise_xor(x, (k >> 2) & 1)
            py = jnp.bitwise_xor(y, (k >> 1) & 1)
            pc = jnp.bitwise_xor(c, k & 1)
            peers.append((px, py, pc, 4 * px + 2 * py + pc))
        sends = []
        for a in range(na):
            for k, (px, py, pc, peer) in enumerate(peers):
                cp = pltpu.make_async_remote_copy(
                    src_ref=block(a, peer), dst_ref=outs[a].at[me],
                    send_sem=send_sems.at[a, k], recv_sem=recv_sems.at[a, k],
                    device_id=(px, py, pc), device_id_type=pl.DeviceIdType.MESH)
                cp.start()
                sends.append(cp)
        for a in range(na):
            for k, (px, py, pc, peer) in enumerate(peers):
                pltpu.make_async_remote_copy(
                    src_ref=block(a, peer), dst_ref=outs[a].at[peer],
                    send_sem=send_sems.at[a, k], recv_sem=recv_sems.at[a, k],
                    device_id=(px, py, pc), device_id_type=pl.DeviceIdType.MESH).wait_recv()
        for cp in sends:
            cp.wait_send()
        for cp in local:
            cp.wait()

    out_shape = []
    for arr, mode in zip(arrays, modes):
        shape = (N_DEV,) + arr.shape if mode == "gather" else arr.shape
        out_shape.append(jax.ShapeDtypeStruct(shape, arr.dtype))
    return pl.pallas_call(
        body, name=name, out_shape=out_shape,
        in_specs=[pl.BlockSpec(memory_space=pl.ANY)] * na,
        out_specs=[pl.BlockSpec(memory_space=pl.ANY)] * na,
        scratch_shapes=[pltpu.SemaphoreType.DMA((na, N_DEV - 1)), pltpu.SemaphoreType.DMA((na, N_DEV - 1)),
                        pltpu.SemaphoreType.DMA((na,))],
        compiler_params=pltpu.CompilerParams(has_side_effects=True),
    )(*arrays)


def _in_proj_fwd(h0, g1, w_in):
    rows = h0.shape[0]
    tm = _row_tile(rows)

    def body(h_ref, g_ref, w_ref, qkv_ref, zrec_ref, u_ref):
        u, _, _ = _rms_fwd(h_ref[...], g_ref[...])
        u = u.astype(BF16)
        u_ref[...] = u
        z = _mm(u, w_ref[...])
        qkv_ref[...] = z[:, :QKV_WIDTH].astype(BF16)
        zrec_ref[...] = z[:, QKV_WIDTH:]

    wide = pl.BlockSpec((tm, D_MODEL), lambda i: (i, 0))
    return pl.pallas_call(
        body, name="in_proj_fwd", grid=(rows // tm,),
        in_specs=[wide, _full((1, D_MODEL)), _resident((D_MODEL, IN_WIDTH))],
        out_specs=[pl.BlockSpec((tm, QKV_WIDTH), lambda i: (i, 0)), pl.BlockSpec((tm, 2 * LRU_WIDTH), lambda i: (i, 0)), wide],
        out_shape=[jax.ShapeDtypeStruct((rows, QKV_WIDTH), BF16), jax.ShapeDtypeStruct((rows, 2 * LRU_WIDTH), F32),
                   jax.ShapeDtypeStruct((rows, D_MODEL), BF16)],
        compiler_params=_params(("parallel",)),
    )(h0, g1, w_in)


N_BIAS = 3


def _attn_bias():
    key = lax.broadcasted_iota(jnp.int32, (2 * BLOCK, GQA_GROUP * BLOCK), 0)
    r = lax.broadcasted_iota(jnp.int32, (2 * BLOCK, GQA_GROUP * BLOCK), 1) & (BLOCK - 1)
    band = (key > r) & (key <= r + BLOCK)
    out = [jnp.where(band & ((n - 1) * BLOCK + key >= PAD_ROWS), 0.0, NEG) for n in range(N_BIAS)]
    return jnp.stack(out).astype(F32)


def _attn_probs(k2, q4, bias, sink_row):
    s = _mm_nt(k2, q4) * (HEAD_DIM ** -0.5) + bias
    m = jnp.maximum(jnp.max(s, axis=0, keepdims=True), sink_row)
    p = jnp.exp(s - m)
    es = jnp.exp(sink_row - m)
    inv = 1.0 / (jnp.sum(p, axis=0, keepdims=True) + es)
    return p * inv, es * inv


def _heads(ref, first, count):
    return jnp.concatenate([ref[:, (first + g) * HEAD_DIM:(first + g + 1) * HEAD_DIM] for g in range(count)], axis=0)


def _sink_row(sink_ref, kv):
    g = lax.broadcasted_iota(jnp.int32, (1, GQA_GROUP * BLOCK), 1) // BLOCK
    row = jnp.full((1, GQA_GROUP * BLOCK), sink_ref[0, kv * GQA_GROUP], F32)
    for i in range(1, GQA_GROUP):
        row = jnp.where(g == i, sink_ref[0, kv * GQA_GROUP + i], row)
    return row


def _from_head_major(pieces):
    return jnp.concatenate(pieces, axis=0).T


def _attn_fwd(qkv, sinks, bias):
    rows = qkv.shape[0]
    nb = rows // BLOCK
    k_col, v_col = ATTN_WIDTH // KV_WIDTH, ATTN_WIDTH // KV_WIDTH + 1

    def body(sink_ref, bias_ref, q_ref, kp_ref, kc_ref, vp_ref, vc_ref, o_ref):
        bias_t = bias_ref[...]
        pieces = []
        for kv in range(KV_HEADS):
            sl = slice(kv * HEAD_DIM, (kv + 1) * HEAD_DIM)
            k2 = jnp.concatenate([kp_ref[:, sl], kc_ref[:, sl]], axis=0)
            v2 = jnp.concatenate([vp_ref[:, sl], vc_ref[:, sl]], axis=0)
            q4 = _heads(q_ref, kv * GQA_GROUP, GQA_GROUP)
            pn, _ = _attn_probs(k2, q4, bias_t, _sink_row(sink_ref, kv))
            ot = _mm_tn(v2, pn.astype(BF16))
            pieces += [ot[:, g * BLOCK:(g + 1) * BLOCK] for g in range(GQA_GROUP)]
        o_ref[...] = _from_head_major(pieces).astype(BF16)

    prev = lambda n: jnp.maximum(n - 1, 0)
    return pl.pallas_call(
        body, name="attn_fwd", grid=(nb,),
        in_specs=[pl.BlockSpec(memory_space=pltpu.SMEM),
                  pl.BlockSpec((None, 2 * BLOCK, GQA_GROUP * BLOCK), lambda n: (jnp.minimum(n, N_BIAS - 1), 0, 0)),
                  pl.BlockSpec((BLOCK, ATTN_WIDTH), lambda n: (n, 0)),
                  pl.BlockSpec((BLOCK, KV_WIDTH), lambda n: (prev(n), k_col)),
                  pl.BlockSpec((BLOCK, KV_WIDTH), lambda n: (n, k_col)),
                  pl.BlockSpec((BLOCK, KV_WIDTH), lambda n: (prev(n), v_col)),
                  pl.BlockSpec((BLOCK, KV_WIDTH), lambda n: (n, v_col))],
        out_specs=pl.BlockSpec((BLOCK, ATTN_WIDTH), lambda n: (n, 0)),
        out_shape=jax.ShapeDtypeStruct((rows, ATTN_WIDTH), BF16),
        compiler_params=_params(("parallel",)),
    )(sinks, bias, qkv, qkv, qkv, qkv, qkv)


def _conv_taps(xbuf, tm):
    return [xbuf[pl.ds(SUBLANES - (CONV_WIDTH - 1 - j), tm), :] for j in range(CONV_WIDTH)]


def _lru_gates(xc, wa_ref, ba_ref, wx_ref, bx_ref, lam_ref):
    halves = [xc[:, h * LRU_HALF:(h + 1) * LRU_HALF].astype(BF16) for h in range(2)]
    gate_r = jnp.concatenate([_mm(halves[h], wa_ref[h]) for h in range(2)], axis=1) + ba_ref[...]
    gate_i = jnp.concatenate([_mm(halves[h], wx_ref[h]) for h in range(2)], axis=1) + bx_ref[...]
    r = _sigmoid(gate_r)
    ig = _sigmoid(gate_i)
    log_a = (-LRU_C) * r * _softplus(-lam_ref[...])
    a = jnp.exp(log_a)
    mult = jnp.sqrt(-_expm1(2.0 * log_a))
    return halves, r, ig, a, mult


def _scan_tile(a_ref, u_ref, out_ref, carry, tm, reverse):
    row = lax.broadcasted_iota(jnp.int32, (SUBLANES, LRU_WIDTH), 0)
    groups = tm // SUBLANES

    def step(j, prev):
        jj = groups - 1 - j if reverse else j
        o = pl.multiple_of(jj * SUBLANES, SUBLANES)
        a = a_ref[pl.ds(o, SUBLANES), :]
        u = u_ref[pl.ds(o, SUBLANES), :]
        for s in (1, 2, 4):
            shift = SUBLANES - s if reverse else s
            keep = (row < SUBLANES - s) if reverse else (row >= s)
            u = jnp.where(keep, a * pltpu.roll(u, shift, 0) + u, u)
            a = jnp.where(keep, a * pltpu.roll(a, shift, 0), a)
        out = a * prev + u
        out_ref[pl.ds(o, SUBLANES), :] = out
        return out[0:1, :] if reverse else out[SUBLANES - 1:SUBLANES, :]

    return lax.fori_loop(0, groups, step, carry)


def _rec_fwd(zrec, conv_w, conv_b, wa_bd, b_a, wx_bd, b_x, lam):
    rows = zrec.shape[0]
    tm = _rec_tile(rows)

    def body(xr_ref, yr_ref, cw_ref, cb_ref, wa_ref, ba_ref, wx_ref, bx_ref, lam_ref, rec_ref, h_ref,
             xbuf, a_s, u_s, carry):
        i = pl.program_id(0)

        @pl.when(i == 0)
        def _():
            xbuf[0:SUBLANES, :] = jnp.zeros((SUBLANES, LRU_WIDTH), F32)
            carry[...] = jnp.zeros_like(carry)

        @pl.when(i > 0)
        def _():
            xbuf[0:SUBLANES, :] = xbuf[tm:tm + SUBLANES, :]

        xbuf[SUBLANES:SUBLANES + tm, :] = xr_ref[...]
        taps = _conv_taps(xbuf, tm)
        xc = cb_ref[...] + sum(cw_ref[j:j + 1, :] * taps[j] for j in range(CONV_WIDTH))
        _, r, ig, a, mult = _lru_gates(xc, wa_ref, ba_ref, wx_ref, bx_ref, lam_ref)
        grow = i * tm + lax.broadcasted_iota(jnp.int32, (tm, LRU_WIDTH), 0)
        a_s[...] = a
        u_s[...] = jnp.where(grow >= PAD_ROWS, mult * (ig * xc), 0.0)
        carry[0:1, :] = _scan_tile(a_s, u_s, h_ref, carry[0:1, :], tm, reverse=False)
        gel, _ = _gelu(yr_ref[...])
        rec_ref[...] = (gel * h_ref[...]).astype(BF16)

    vec = _full((1, LRU_WIDTH))
    bd = _full((2, LRU_HALF, LRU_HALF))
    return pl.pallas_call(
        body, name="rec_fwd", grid=(rows // tm,),
        in_specs=[pl.BlockSpec((tm, LRU_WIDTH), lambda i: (i, 0)), pl.BlockSpec((tm, LRU_WIDTH), lambda i: (i, 1)),
                  _full((CONV_WIDTH, LRU_WIDTH)), vec, bd, vec, bd, vec, vec],
        out_specs=[pl.BlockSpec((tm, LRU_WIDTH), lambda i: (i, 0)), pl.BlockSpec((tm, LRU_WIDTH), lambda i: (i, 0))],
        out_shape=[jax.ShapeDtypeStruct((rows, LRU_WIDTH), BF16), jax.ShapeDtypeStruct((rows, LRU_WIDTH), F32)],
        scratch_shapes=[pltpu.VMEM((tm + SUBLANES, LRU_WIDTH), F32), pltpu.VMEM((tm, LRU_WIDTH), F32),
                        pltpu.VMEM((tm, LRU_WIDTH), F32), pltpu.VMEM((SUBLANES, LRU_WIDTH), F32)],
        compiler_params=_params(("arbitrary",)),
    )(zrec, zrec, conv_w, conv_b, wa_bd, b_a, wx_bd, b_x, lam)


def _out_proj_fwd(attn, rec, w_out, h0, g2):
    rows = h0.shape[0]
    tm = _row_tile(rows)

    def body(attn_ref, rec_ref, w_ref, h_ref, g_ref, mix_ref, h1_ref):
        mix = _mm(attn_ref[...], w_ref[0:ATTN_WIDTH, :]) + _mm(rec_ref[...], w_ref[ATTN_WIDTH:, :])
        y, _, _ = _rms_fwd(mix, g_ref[...])
        mix_ref[...] = mix
        h1_ref[...] = h_ref[...] + y

    half = pl.BlockSpec((tm, ATTN_WIDTH), lambda i: (i, 0))
    wide = pl.BlockSpec((tm, D_MODEL), lambda i: (i, 0))
    return pl.pallas_call(
        body, name="out_proj_fwd", grid=(rows // tm,),
        in_specs=[half, half, _resident((D_MODEL, D_MODEL)), wide, _full((1, D_MODEL))],
        out_specs=[wide, wide],
        out_shape=[jax.ShapeDtypeStruct((rows, D_MODEL), F32)] * 2,
        compiler_params=_params(("parallel",)),
    )(attn, rec, w_out, h0, g2)


FF_COLS = 1024


def _ffn_up(h1, g3, w1):
    rows = h1.shape[0]
    tm = _row_tile(rows)

    def body(h_ref, g_ref, w_ref, act_ref, u_ref):
        u, _, _ = _rms_fwd(h_ref[...], g_ref[...])
        u = u.astype(BF16)
        u_ref[...] = u
        for c in range(0, D_FF, FF_COLS):
            a1 = jnp.maximum(_mm(u, w_ref[:, c:c + FF_COLS]), 0.0)
            act_ref[:, c:c + FF_COLS] = (a1 * a1).astype(BF16)

    wide = pl.BlockSpec((tm, D_MODEL), lambda i: (i, 0))
    return pl.pallas_call(
        body, name="ffn_up", grid=(rows // tm,),
        in_specs=[wide, _full((1, D_MODEL)), _resident((D_MODEL, D_FF))],
        out_specs=[pl.BlockSpec((tm, D_FF), lambda i: (i, 0)), wide],
        out_shape=[jax.ShapeDtypeStruct((rows, D_FF), BF16), jax.ShapeDtypeStruct((rows, D_MODEL), BF16)],
        compiler_params=_params(("parallel",)),
    )(h1, g3, w1)


def _ffn_down_loss(act, w2, h1, target, g4):
    rows = h1.shape[0]
    tm = _row_tile(rows)

    def body(act_ref, w_ref, h_ref, t_ref, g_ref, dy_ref, df_ref, dg_ref, loss_ref):
        i = pl.program_id(0)

        @pl.when(i == 0)
        def _():
            dg_ref[...] = jnp.zeros_like(dg_ref)
            loss_ref[...] = jnp.zeros_like(loss_ref)

        g = g_ref[...]
        y, fhat, rstd = _rms_fwd(_mm(act_ref[...], w_ref[...]), g)
        grow = i * tm + lax.broadcasted_iota(jnp.int32, (tm, D_MODEL), 0)
        err = jnp.where(grow >= BLOCK, h_ref[...] + y - t_ref[...], 0.0)
        loss_ref[...] += (0.5 / D_MODEL) * jnp.sum(err * err)
        dy = err * (1.0 / D_MODEL)
        df, dg = _rms_bwd(dy, fhat, rstd, g)
        dy_ref[...] = dy
        df_ref[...] = df.astype(BF16)
        dg_ref[...] += dg

    wide = pl.BlockSpec((tm, D_MODEL), lambda i: (i, 0))
    return pl.pallas_call(
        body, name="ffn_down_loss", grid=(rows // tm,),
        in_specs=[pl.BlockSpec((tm, D_FF), lambda i: (i, 0)), _resident((D_FF, D_MODEL)), wide, wide, _full((1, D_MODEL))],
        out_specs=[wide, wide, _full((1, D_MODEL)), _full((SUBLANES, LANES))],
        out_shape=[jax.ShapeDtypeStruct((rows, D_MODEL), F32), jax.ShapeDtypeStruct((rows, D_MODEL), BF16),
                   jax.ShapeDtypeStruct((1, D_MODEL), F32), jax.ShapeDtypeStruct((SUBLANES, LANES), F32)],
        compiler_params=_params(("arbitrary",)),
    )(act, w2, h1, target, g4)


def _ffn_bwd_act(df, w2t, act):
    rows = df.shape[0]
    tm = _row_tile(rows)

    def body(df_ref, w_ref, act_ref, da_ref):
        df_t = df_ref[...]
        for c in range(0, D_FF, FF_COLS):
            dact = _mm(df_t, w_ref[:, c:c + FF_COLS])
            da_ref[:, c:c + FF_COLS] = (dact * (2.0 * jnp.sqrt(act_ref[:, c:c + FF_COLS].astype(F32)))).astype(BF16)

    hidden = pl.BlockSpec((tm, D_FF), lambda i: (i, 0))
    return pl.pallas_call(
        body, name="ffn_bwd_act", grid=(rows // tm,),
        in_specs=[pl.BlockSpec((tm, D_MODEL), lambda i: (i, 0)), _resident((D_MODEL, D_FF)), hidden],
        out_specs=hidden,
        out_shape=jax.ShapeDtypeStruct((rows, D_FF), BF16),
        compiler_params=_params(("parallel",)),
    )(df, w2t, act)


def _ffn_bwd_x(da, w1t, h1, dy, g3):
    rows = h1.shape[0]
    tm = _row_tile(rows)

    def body(da_ref, w_ref, h_ref, dy_ref, g_ref, dh_ref, dg_ref):
        @pl.when(pl.program_id(0) == 0)
        def _():
            dg_ref[...] = jnp.zeros_like(dg_ref)

        g = g_ref[...]
        _, xhat, rstd = _rms_fwd(h_ref[...], g)
        dx, dg = _rms_bwd(_mm(da_ref[...], w_ref[...]), xhat, rstd, g)
        dh_ref[...] = dy_ref[...] + dx
        dg_ref[...] += dg

    wide = pl.BlockSpec((tm, D_MODEL), lambda i: (i, 0))
    return pl.pallas_call(
        body, name="ffn_bwd_x", grid=(rows // tm,),
        in_specs=[pl.BlockSpec((tm, D_FF), lambda i: (i, 0)), _resident((D_FF, D_MODEL)), wide, wide, _full((1, D_MODEL))],
        out_specs=[wide, _full((1, D_MODEL))],
        out_shape=[jax.ShapeDtypeStruct((rows, D_MODEL), F32), jax.ShapeDtypeStruct((1, D_MODEL), F32)],
        compiler_params=_params(("arbitrary",)),
    )(da, w1t, h1, dy, g3)


def _ffn_bwd_weights(u2, da, act, df):
    rows = u2.shape[0]
    tb = _big_tile(rows)

    def body(u_ref, da_ref, act_ref, df_ref, dw1_ref, dw2_ref):
        @pl.when(pl.program_id(1) == 0)
        def _():
            dw1_ref[...] = jnp.zeros_like(dw1_ref)
            dw2_ref[...] = jnp.zeros_like(dw2_ref)

        dw1_ref[...] += _mm_tn(u_ref[...], da_ref[...])
        dw2_ref[...] += _mm_tn(act_ref[...], df_ref[...])

    wide = pl.BlockSpec((tb, D_MODEL), lambda j, i: (i, 0))
    chunk = pl.BlockSpec((tb, FF_COLS), lambda j, i: (i, j))
    return pl.pallas_call(
        body, name="ffn_bwd_weights", grid=(D_FF // FF_COLS, rows // tb),
        in_specs=[wide, chunk, chunk, wide],
        out_specs=[pl.BlockSpec((D_MODEL, FF_COLS), lambda j, i: (0, j)), pl.BlockSpec((FF_COLS, D_MODEL), lambda j, i: (j, 0))],
        out_shape=[jax.ShapeDtypeStruct((D_MODEL, D_FF), F32), jax.ShapeDtypeStruct((D_FF, D_MODEL), F32)],
        compiler_params=_params(("parallel", "arbitrary")),
    )(u2, da, act, df)


def _out_proj_bwd(dh1, mix, g2, w_out_t, attn, rec):
    rows = dh1.shape[0]
    tm = _row_tile(rows)

    def body(dh_ref, mix_ref, g_ref, w_ref, attn_ref, rec_ref, dattn_ref, drec_ref, dw_ref, dg_ref):
        @pl.when(pl.program_id(0) == 0)
        def _():
            dw_ref[...] = jnp.zeros_like(dw_ref)
            dg_ref[...] = jnp.zeros_like(dg_ref)

        g = g_ref[...]
        _, xhat, rstd = _rms_fwd(mix_ref[...], g)
        dmix, dg = _rms_bwd(dh_ref[...], xhat, rstd, g)
        dmix = dmix.astype(BF16)
        dg_ref[...] += dg
        din = _mm(dmix, w_ref[...])
        dattn_ref[...] = din[:, :ATTN_WIDTH].astype(BF16)
        drec_ref[...] = din[:, ATTN_WIDTH:]
        dw_ref[0:ATTN_WIDTH, :] += _mm_tn(attn_ref[...], dmix)
        dw_ref[ATTN_WIDTH:, :] += _mm_tn(rec_ref[...], dmix)

    half = pl.BlockSpec((tm, ATTN_WIDTH), lambda i: (i, 0))
    wide = pl.BlockSpec((tm, D_MODEL), lambda i: (i, 0))
    return pl.pallas_call(
        body, name="out_proj_bwd", grid=(rows // tm,),
        in_specs=[wide, wide, _full((1, D_MODEL)), _resident((D_MODEL, D_MODEL)), half, half],
        out_specs=[half, half, _full((D_MODEL, D_MODEL)), _full((1, D_MODEL))],
        out_shape=[jax.ShapeDtypeStruct((rows, ATTN_WIDTH), BF16), jax.ShapeDtypeStruct((rows, LRU_WIDTH), F32),
                   jax.ShapeDtypeStruct((D_MODEL, D_MODEL), F32), jax.ShapeDtypeStruct((1, D_MODEL), F32)],
        compiler_params=_params(("arbitrary",)),
    )(dh1, mix, g2, w_out_t, attn, rec)


def _attn_bwd(qkv, dattn, sinks, bias):
    rows = qkv.shape[0]
    nb = rows // BLOCK
    k_col, v_col = ATTN_WIDTH // KV_WIDTH, ATTN_WIDTH // KV_WIDTH + 1

    def body(sink_ref, bias_ref, q_ref, do_ref, kp_ref, kc_ref, vp_ref, vc_ref, dq_ref, dkv_ref, dsink_ref, dk_c, dv_c):
        n = pl.program_id(0)

        @pl.when(n == 0)
        def _():
            dk_c[...] = jnp.zeros_like(dk_c)
            dv_c[...] = jnp.zeros_like(dv_c)
            dsink_ref[...] = jnp.zeros_like(dsink_ref)

        @pl.when(n < nb)
        def _():
            bias_t = bias_ref[...]
            dq_parts, dk_parts, dv_parts, dsink_rows = [], [], [], []
            for kv in range(KV_HEADS):
                sl = slice(kv * HEAD_DIM, (kv + 1) * HEAD_DIM)
                k2 = jnp.concatenate([kp_ref[:, sl], kc_ref[:, sl]], axis=0)
                v2 = jnp.concatenate([vp_ref[:, sl], vc_ref[:, sl]], axis=0)
                q4 = _heads(q_ref, kv * GQA_GROUP, GQA_GROUP)
                do4 = _heads(do_ref, kv * GQA_GROUP, GQA_GROUP)
                pn, psink = _attn_probs(k2, q4, bias_t, _sink_row(sink_ref, kv))
                dpn = _mm_nt(v2, do4)
                delta = jnp.sum(pn * dpn, axis=0, keepdims=True)
                ds = ((pn * (dpn - delta)) * (HEAD_DIM ** -0.5)).astype(BF16)
                dqt = _mm_tn(k2, ds)
                dq_parts += [dqt[:, g * BLOCK:(g + 1) * BLOCK] for g in range(GQA_GROUP)]
                dk_parts.append(_mm(ds, q4))
                dv_parts.append(_mm(pn.astype(BF16), do4))
                sd = psink * delta
                for g in range(GQA_GROUP):
                    dsink_rows.append(jnp.full((1, LANES), -jnp.sum(sd[:, g * BLOCK:(g + 1) * BLOCK]), F32))
            dq_ref[...] = _from_head_major(dq_parts).astype(BF16)
            dsink_ref[...] += jnp.concatenate(dsink_rows, axis=0)
            dk2 = jnp.concatenate(dk_parts, axis=1)
            dv2 = jnp.concatenate(dv_parts, axis=1)
            dkv_ref[:, 0:KV_WIDTH] = (dk_c[...] + dk2[0:BLOCK]).astype(BF16)
            dkv_ref[:, KV_WIDTH:] = (dv_c[...] + dv2[0:BLOCK]).astype(BF16)
            dk_c[...] = dk2[BLOCK:]
            dv_c[...] = dv2[BLOCK:]

        @pl.when(n == nb)
        def _():
            dkv_ref[:, 0:KV_WIDTH] = dk_c[...].astype(BF16)
            dkv_ref[:, KV_WIDTH:] = dv_c[...].astype(BF16)

    cur = lambda n: jnp.minimum(n, nb - 1)
    prev = lambda n: jnp.maximum(jnp.minimum(n, nb - 1) - 1, 0)
    return pl.pallas_call(
        body, name="attn_bwd", grid=(nb + 1,),
        in_specs=[pl.BlockSpec(memory_space=pltpu.SMEM),
                  pl.BlockSpec((None, 2 * BLOCK, GQA_GROUP * BLOCK), lambda n: (jnp.minimum(n, N_BIAS - 1), 0, 0)),
                  pl.BlockSpec((BLOCK, ATTN_WIDTH), lambda n: (cur(n), 0)),
                  pl.BlockSpec((BLOCK, ATTN_WIDTH), lambda n: (cur(n), 0)),
                  pl.BlockSpec((BLOCK, KV_WIDTH), lambda n: (prev(n), k_col)),
                  pl.BlockSpec((BLOCK, KV_WIDTH), lambda n: (cur(n), k_col)),
                  pl.BlockSpec((BLOCK, KV_WIDTH), lambda n: (prev(n), v_col)),
                  pl.BlockSpec((BLOCK, KV_WIDTH), lambda n: (cur(n), v_col))],
        out_specs=[pl.BlockSpec((BLOCK, ATTN_WIDTH), lambda n: (cur(n), 0)),
                   pl.BlockSpec((BLOCK, 2 * KV_WIDTH), lambda n: (jnp.maximum(n - 1, 0), 0)),
                   _full((ATTN_HEADS, LANES))],
        out_shape=[jax.ShapeDtypeStruct((rows, ATTN_WIDTH), BF16), jax.ShapeDtypeStruct((rows, 2 * KV_WIDTH), BF16),
                   jax.ShapeDtypeStruct((ATTN_HEADS, LANES), F32)],
        scratch_shapes=[pltpu.VMEM((BLOCK, KV_WIDTH), F32), pltpu.VMEM((BLOCK, KV_WIDTH), F32)],
        compiler_params=_params(("arbitrary",)),
    )(sinks, bias, qkv, dattn, qkv, qkv, qkv, qkv)


ROW_CONV_B, ROW_B_A, ROW_B_X, ROW_LAMBDA = 4, 5, 6, 7


def _rec_bwd(drec, zrec, h, conv_w, conv_b, wa_bd, b_a, wx_bd, b_x, lam):
    rows = zrec.shape[0]
    tm = _rec_tile(rows)
    nt = rows // tm
    per = tm // SUBLANES

    def body(drec_ref, xr_ref, yr_ref, h_ref, xhalo_ref, hhalo_ref, cw_ref, cb_ref, wa_ref, ba_ref, wx_ref, bx_ref,
             lam_ref, drz_ref, small_ref, dwa_ref, dwx_ref, xbuf, hbuf, abuf, u_s, g_s, dbuf, carry):
        s = pl.program_id(0)
        i = nt - 1 - s

        @pl.when(s == 0)
        def _():
            small_ref[...] = jnp.zeros_like(small_ref)
            dwa_ref[...] = jnp.zeros_like(dwa_ref)
            dwx_ref[...] = jnp.zeros_like(dwx_ref)
            carry[...] = jnp.zeros_like(carry)
            abuf[tm:tm + SUBLANES, :] = jnp.zeros((SUBLANES, LRU_WIDTH), F32)
            dbuf[tm:tm + SUBLANES, :] = jnp.zeros((SUBLANES, LRU_WIDTH), F32)

        first = i == 0
        xbuf[0:SUBLANES, :] = jnp.where(first, 0.0, xhalo_ref[...])
        hbuf[0:SUBLANES, :] = jnp.where(first, 0.0, hhalo_ref[...])
        xbuf[SUBLANES:SUBLANES + tm, :] = xr_ref[...]
        hbuf[SUBLANES:SUBLANES + tm, :] = h_ref[...]

        taps = _conv_taps(xbuf, tm)
        xc = cb_ref[...] + sum(cw_ref[j:j + 1, :] * taps[j] for j in range(CONV_WIDTH))
        halves, r, ig, a, mult = _lru_gates(xc, wa_ref, ba_ref, wx_ref, bx_ref, lam_ref)

        yr = yr_ref[...]
        gel, t = _gelu(yr)
        drec_t = drec_ref[...]
        dyr = drec_t * h_ref[...] * _gelu_grad(yr, t)

        abuf[0:tm, :] = a
        u_s[...] = drec_t * gel
        a_next = abuf[pl.ds(1, tm), :]
        abuf[0:tm, :] = a_next
        carry[0:1, :] = _scan_tile(abuf, u_s, g_s, carry[0:1, :], tm, reverse=True)
        abuf[tm:tm + 1, :] = a[0:1, :]
        g = g_s[...]

        grow = i * tm + lax.broadcasted_iota(jnp.int32, (tm, LRU_WIDTH), 0)
        du = jnp.where(grow >= PAD_ROWS, g, 0.0)
        da = g * hbuf[pl.ds(SUBLANES - 1, tm), :]
        dmult = du * (ig * xc)
        dig = du * (mult * xc)
        dxc = du * (mult * ig)
        a2 = a * a
        dlog_a = da * a - dmult * (a2 / mult)
        sp = _softplus(-lam_ref[...])
        dgr = (dlog_a * (-LRU_C) * sp) * (r * (1.0 - r))
        dgi = dig * (ig * (1.0 - ig))
        dlam = jnp.sum(dlog_a * r, axis=0, keepdims=True) * (LRU_C * _sigmoid(-lam_ref[...]))
        dgr_b = [dgr[:, hh * LRU_HALF:(hh + 1) * LRU_HALF].astype(BF16) for hh in range(2)]
        dgi_b = [dgi[:, hh * LRU_HALF:(hh + 1) * LRU_HALF].astype(BF16) for hh in range(2)]
        dxc = dxc + jnp.concatenate(
            [_mm_nt(dgr_b[hh], wa_ref[hh]) + _mm_nt(dgi_b[hh], wx_ref[hh]) for hh in range(2)], axis=1)
        for hh in range(2):
            dwa_ref[hh] += _mm_tn(halves[hh], dgr_b[hh])
            dwx_ref[hh] += _mm_tn(halves[hh], dgi_b[hh])

        dbuf[0:tm, :] = dxc
        dxr = sum(cw_ref[j:j + 1, :] * dbuf[pl.ds(CONV_WIDTH - 1 - j, tm), :] for j in range(CONV_WIDTH))
        dbuf[tm:tm + SUBLANES, :] = dxc[0:SUBLANES, :]
        drz_ref[:, 0:LRU_WIDTH] = dxr.astype(BF16)
        drz_ref[:, LRU_WIDTH:] = dyr.astype(BF16)

        upd = [jnp.sum(dxc * taps[j], axis=0, keepdims=True) for j in range(CONV_WIDTH)]
        upd += [jnp.sum(dxc, axis=0, keepdims=True), jnp.sum(dgr, axis=0, keepdims=True),
                jnp.sum(dgi, axis=0, keepdims=True), dlam]
        small_ref[...] += jnp.concatenate(upd, axis=0)

    rev = lambda s: nt - 1 - s
    halo = lambda s: jnp.maximum(rev(s) * per - 1, 0)
    tile0 = pl.BlockSpec((tm, LRU_WIDTH), lambda s: (rev(s), 0))
    tile1 = pl.BlockSpec((tm, LRU_WIDTH), lambda s: (rev(s), 1))
    halo0 = pl.BlockSpec((SUBLANES, LRU_WIDTH), lambda s: (halo(s), 0))
    vec = _full((1, LRU_WIDTH))
    bd = _full((2, LRU_HALF, LRU_HALF))
    big = pltpu.VMEM((tm + SUBLANES, LRU_WIDTH), F32)
    tile = pltpu.VMEM((tm, LRU_WIDTH), F32)
    return pl.pallas_call(
        body, name="rec_bwd", grid=(nt,),
        in_specs=[tile0, tile0, tile1, tile0, halo0, halo0, _full((CONV_WIDTH, LRU_WIDTH)), vec, bd, vec, bd, vec, vec],
        out_specs=[pl.BlockSpec((tm, 2 * LRU_WIDTH), lambda s: (rev(s), 0)), _full((SUBLANES, LRU_WIDTH)), bd, bd],
        out_shape=[jax.ShapeDtypeStruct((rows, 2 * LRU_WIDTH), BF16), jax.ShapeDtypeStruct((SUBLANES, LRU_WIDTH), F32),
                   jax.ShapeDtypeStruct((2, LRU_HALF, LRU_HALF), F32), jax.ShapeDtypeStruct((2, LRU_HALF, LRU_HALF), F32)],
        scratch_shapes=[big, big, big, tile, tile, big, pltpu.VMEM((SUBLANES, LRU_WIDTH), F32)],
        compiler_params=_params(("arbitrary",)),
    )(drec, zrec, zrec, h, zrec, h, conv_w, conv_b, wa_bd, b_a, wx_bd, b_x, lam)


DZ_CUTS = (0, ATTN_WIDTH, QKV_WIDTH, IN_WIDTH)


def _dz_specs(tm):
    return [pl.BlockSpec((tm, DZ_CUTS[p + 1] - DZ_CUTS[p]), lambda i: (i, 0)) for p in range(3)]


def _in_proj_bwd_x(h0, g1, dh1, dq, dkv, drz, w_in_t):
    rows = h0.shape[0]
    tm = _row_tile(rows)

    def body(h_ref, g_ref, dh1_ref, dq_ref, dkv_ref, drz_ref, w_ref, dh0_ref, dg_ref):
        @pl.when(pl.program_id(0) == 0)
        def _():
            dg_ref[...] = jnp.zeros_like(dg_ref)

        g = g_ref[...]
        _, xhat, rstd = _rms_fwd(h_ref[...], g)
        parts = (dq_ref[...], dkv_ref[...], drz_ref[...])
        du = sum(_mm(parts[p], w_ref[DZ_CUTS[p]:DZ_CUTS[p + 1], :]) for p in range(3))
        dx, dg = _rms_bwd(du, xhat, rstd, g)
        dh0_ref[...] = dh1_ref[...] + dx
        dg_ref[...] += dg

    wide = pl.BlockSpec((tm, D_MODEL), lambda i: (i, 0))
    return pl.pallas_call(
        body, name="in_proj_bwd_x", grid=(rows // tm,),
        in_specs=[wide, _full((1, D_MODEL)), wide] + _dz_specs(tm) + [_resident((IN_WIDTH, D_MODEL))],
        out_specs=[wide, _full((1, D_MODEL))],
        out_shape=[jax.ShapeDtypeStruct((rows, D_MODEL), F32), jax.ShapeDtypeStruct((1, D_MODEL), F32)],
        compiler_params=_params(("arbitrary",)),
    )(h0, g1, dh1, dq, dkv, drz, w_in_t)


def _in_proj_bwd_w(u1, dq, dkv, drz):
    rows = u1.shape[0]
    tb = _big_tile(rows)

    def body(u_ref, dq_ref, dkv_ref, drz_ref, dw_ref):
        @pl.when(pl.program_id(0) == 0)
        def _():
            dw_ref[...] = jnp.zeros_like(dw_ref)

        u = u_ref[...]
        for p, ref in enumerate((dq_ref, dkv_ref, drz_ref)):
            dw_ref[:, DZ_CUTS[p]:DZ_CUTS[p + 1]] += _mm_tn(u, ref[...])

    return pl.pallas_call(
        body, name="in_proj_bwd_w", grid=(rows // tb,),
        in_specs=[pl.BlockSpec((tb, D_MODEL), lambda i: (i, 0))] + _dz_specs(tb),
        out_specs=_full((D_MODEL, IN_WIDTH)),
        out_shape=jax.ShapeDtypeStruct((D_MODEL, IN_WIDTH), F32),
        compiler_params=_params(("arbitrary",)),
    )(u1, dq, dkv, drz)


def _adamw(w, m, v, parts, name):
    rows, cols = w.shape
    tr = next((t for t in (256, 128) if rows % t == 0), rows)

    def body(w_ref, m_ref, v_ref, p_ref, g_ref, d_ref, nm_ref, nv_ref):
        g = p_ref[0].astype(F32)
        for s in range(1, N_DEV):
            g = g + p_ref[s].astype(F32)
        nm = ADAM_B1 * m_ref[...] + (1.0 - ADAM_B1) * g
        nv = ADAM_B2 * v_ref[...] + (1.0 - ADAM_B2) * (g * g)
        m_hat = nm / (1.0 - ADAM_B1 ** ADAM_STEP)
        v_hat = nv / (1.0 - ADAM_B2 ** ADAM_STEP)
        g_ref[...] = g
        d_ref[...] = (-ADAM_LR) * (m_hat / (jnp.sqrt(v_hat) + ADAM_EPS) + ADAM_WD * w_ref[...])
        nm_ref[...] = nm
        nv_ref[...] = nv

    blk = pl.BlockSpec((tr, cols), lambda i: (i, 0))
    return pl.pallas_call(
        body, name=name, grid=(rows // tr,),
        in_specs=[blk, blk, blk, pl.BlockSpec((N_DEV, tr, cols), lambda i: (0, i, 0))],
        out_specs=[blk] * 4,
        out_shape=[jax.ShapeDtypeStruct((rows, cols), F32)] * 4,
        compiler_params=_params(("parallel",)),
    )(w, m, v, parts)


def _cols_from_shards(g):
    return jnp.transpose(g, (1, 0, 2)).reshape(g.shape[1], N_DEV * g.shape[2])


def _cols_to_shards(a):
    r, c = a.shape
    return jnp.transpose(a.reshape(r, N_DEV, c // N_DEV), (1, 0, 2))


def _block_diag(w):
    per = LRU_HALF // LRU_BLOCK
    w = w.reshape(2, per, LRU_BLOCK, LRU_BLOCK)
    eye = jnp.eye(per, dtype=w.dtype)
    return (w[:, :, :, None, :] * eye[None, :, None, :, None]).reshape(2, LRU_HALF, LRU_HALF)


def _block_diag_extract(t):
    per = LRU_HALF // LRU_BLOCK
    t = t.reshape(2, per, LRU_BLOCK, per, LRU_BLOCK)
    return jnp.stack([t[:, b, :, b, :] for b in range(per)], axis=1).reshape(LRU_BLOCKS, LRU_BLOCK, LRU_BLOCK)


SMALL_NAMES = ("g_pre_mix", "conv_b", "w_a", "b_a", "w_x", "b_x", "lru_lambda", "attn_sinks",
               "g_post_mix", "g_pre_ffn", "g_post_ffn")


def _pack_small(vals):
    flat = []
    for name in SMALL_NAMES:
        a = vals[name].reshape(-1)
        flat.append(jnp.pad(a, (0, (-a.shape[0]) % LANES)))
    flat = jnp.concatenate(flat)
    rows = flat.shape[0] // LANES
    return jnp.pad(flat.reshape(rows, LANES), ((0, (-rows) % SUBLANES), (0, 0)))


def _unpack_small(packed, like):
    flat = packed.reshape(-1)
    out, at = {}, 0
    for name in SMALL_NAMES:
        n = like[name].size
        out[name] = flat[at:at + n].reshape(like[name].shape)
        at += n + (-n) % LANES
    return out


def kernel(x, meta_tokens, g_pre_mix, w_in, conv_w, conv_b, w_a, b_a, w_x, b_x, lru_lambda, attn_sinks, w_out, g_post_mix, g_pre_ffn, w_ff1, w_ff2, g_post_ffn, loss_target, m_meta_tokens, m_g_pre_mix, m_w_in, m_conv_w, m_conv_b, m_w_a, m_b_a, m_w_x, m_b_x, m_lru_lambda, m_attn_sinks, m_w_out, m_g_post_mix, m_g_pre_ffn, m_w_ff1, m_w_ff2, m_g_post_ffn, v_meta_tokens, v_g_pre_mix, v_w_in, v_conv_w, v_conv_b, v_w_a, v_b_a, v_w_x, v_b_x, v_lru_lambda, v_attn_sinks, v_w_out, v_g_post_mix, v_g_pre_ffn, v_w_ff1, v_w_ff2, v_g_post_ffn):
    weights = dict(meta_tokens=meta_tokens, g_pre_mix=g_pre_mix, w_in=w_in, conv_w=conv_w, conv_b=conv_b, w_a=w_a,
                   b_a=b_a, w_x=w_x, b_x=b_x, lru_lambda=lru_lambda, attn_sinks=attn_sinks, w_out=w_out,
                   g_post_mix=g_post_mix, g_pre_ffn=g_pre_ffn, w_ff1=w_ff1, w_ff2=w_ff2, g_post_ffn=g_post_ffn)
    mom_m = dict(meta_tokens=m_meta_tokens, g_pre_mix=m_g_pre_mix, w_in=m_w_in, conv_w=m_conv_w, conv_b=m_conv_b,
                 w_a=m_w_a, b_a=m_b_a, w_x=m_w_x, b_x=m_b_x, lru_lambda=m_lru_lambda, attn_sinks=m_attn_sinks,
                 w_out=m_w_out, g_post_mix=m_g_post_mix, g_pre_ffn=m_g_pre_ffn, w_ff1=m_w_ff1, w_ff2=m_w_ff2,
                 g_post_ffn=m_g_post_ffn)
    mom_v = dict(meta_tokens=v_meta_tokens, g_pre_mix=v_g_pre_mix, w_in=v_w_in, conv_w=v_conv_w, conv_b=v_conv_b,
                 w_a=v_w_a, b_a=v_b_a, w_x=v_w_x, b_x=v_b_x, lru_lambda=v_lru_lambda, attn_sinks=v_attn_sinks,
                 w_out=v_w_out, g_post_mix=v_g_post_mix, g_pre_ffn=v_g_pre_ffn, w_ff1=v_w_ff1, w_ff2=v_w_ff2,
                 g_post_ffn=v_g_post_ffn)
    order = list(weights)

    shards = [w_in[0].astype(BF16), w_out[0].astype(BF16), w_ff1[0].astype(BF16), w_ff2[0].astype(BF16),
              meta_tokens, conv_w[0]]
    g_win, g_wout, w1g, w2g, g_meta, g_cw = _exchange(shards, ["gather"] * len(shards), "gather_weights")
    w_in_full = _cols_from_shards(g_win)
    w_out_full = g_wout.reshape(D_MODEL, D_MODEL)
    meta_full = _cols_from_shards(g_meta)
    conv_w_full = _cols_from_shards(g_cw)

    seq = x.shape[1]
    rows = BLOCK + seq
    h0 = jnp.concatenate([jnp.zeros((PAD_ROWS, D_MODEL), F32), meta_full, x[0]], axis=0)
    target = jnp.pad(loss_target[0], ((BLOCK, 0), (0, 0)))
    wa_bd = _block_diag(w_a[0]).astype(BF16)
    wx_bd = _block_diag(w_x[0]).astype(BF16)
    w1 = _cols_from_shards(w1g)
    w2 = w2g.reshape(D_FF, D_MODEL)
    bias = _attn_bias()

    qkv, zrec, u1 = _in_proj_fwd(h0, g_pre_mix, w_in_full)
    attn = _attn_fwd(qkv, attn_sinks, bias)
    rec, h_lru = _rec_fwd(zrec, conv_w_full, conv_b, wa_bd, b_a, wx_bd, b_x, lru_lambda)
    mix, h1 = _out_proj_fwd(attn, rec, w_out_full, h0, g_post_mix)
    act, u2 = _ffn_up(h1, g_pre_ffn, w1)
    dy, df, dg_post_ffn, loss_acc = _ffn_down_loss(act, w2, h1, target, g_post_ffn)

    da1 = _ffn_bwd_act(df, w2.T, act)
    dh1, dg_pre_ffn = _ffn_bwd_x(da1, w1.T, h1, dy, g_pre_ffn)
    dw1, dw2 = _ffn_bwd_weights(u2, da1, act, df)
    dattn, drec, dw_out, dg_post_mix = _out_proj_bwd(dh1, mix, g_post_mix, w_out_full.T, attn, rec)
    dq, dkv, dsinks = _attn_bwd(qkv, dattn, attn_sinks, bias)
    drz, rec_small, dwa_bd, dwx_bd = _rec_bwd(drec, zrec, h_lru, conv_w_full, conv_b, wa_bd, b_a, wx_bd, b_x, lru_lambda)
    dh0, dg_pre_mix = _in_proj_bwd_x(h0, g_pre_mix, dh1, dq, dkv, drz, w_in_full.T)
    dw_in = _in_proj_bwd_w(u1, dq, dkv, drz)

    small_grads = dict(
        g_pre_mix=dg_pre_mix, conv_b=rec_small[ROW_CONV_B], w_a=_block_diag_extract(dwa_bd), b_a=rec_small[ROW_B_A],
        w_x=_block_diag_extract(dwx_bd), b_x=rec_small[ROW_B_X], lru_lambda=rec_small[ROW_LAMBDA],
        attn_sinks=dsinks[:, 0], g_post_mix=dg_post_mix, g_pre_ffn=dg_pre_ffn, g_post_ffn=dg_post_ffn)
    partial = [
        _cols_to_shards(dw_in).astype(BF16),
        dw_out.reshape(N_DEV, D_MODEL // N_DEV, D_MODEL).astype(BF16),
        _cols_to_shards(dw1).astype(BF16),
        dw2.reshape(N_DEV, FF_CHUNK, D_MODEL).astype(BF16),
        _cols_to_shards(dh0[PAD_ROWS:BLOCK]),
        _cols_to_shards(rec_small[0:CONV_WIDTH]),
        _pack_small(small_grads),
    ]
    modes = ["scatter"] * 6 + ["gather"]
    p_win, p_wout, p_w1, p_w2, p_meta, p_cw, p_small = _exchange(partial, modes, "exchange_grads")

    res = {}
    res["w_in"] = _adamw(w_in[0], m_w_in[0], v_w_in[0], p_win, "adamw_w_in")
    res["w_out"] = _adamw(w_out[0], m_w_out[0], v_w_out[0], p_wout, "adamw_w_out")
    res["w_ff1"] = _adamw(w_ff1[0], m_w_ff1[0], v_w_ff1[0], p_w1, "adamw_w_ff1")
    res["w_ff2"] = _adamw(w_ff2[0], m_w_ff2[0], v_w_ff2[0], p_w2, "adamw_w_ff2")
    res["meta_tokens"] = _adamw(meta_tokens, m_meta_tokens, v_meta_tokens, p_meta, "adamw_meta")
    res["conv_w"] = _adamw(conv_w[0], m_conv_w[0], v_conv_w[0], p_cw, "adamw_conv_w")
    small = _adamw(_pack_small(weights), _pack_small(mom_m), _pack_small(mom_v), p_small, "adamw_small")
    small = [_unpack_small(t, weights) for t in small]
    for name in SMALL_NAMES:
        res[name] = tuple(t[name] for t in small)
    for name in ("w_in", "w_out", "w_ff1", "w_ff2", "conv_w"):
        res[name] = tuple(t[None] for t in res[name])

    loss = lax.psum(loss_acc[0, 0], ("x", "y", "c"))
    grad_x = dh0[BLOCK:][None]
    outs = [loss, grad_x]
    for k in range(4):
        outs += [res[name][k] for name in order]
    return tuple(outs)
```

```python
import jax
import jax.numpy as jnp
from jax import lax
from jax.experimental import pallas as pl
from jax.experimental.pallas import tpu as pltpu

F32 = jnp.float32
BF16 = jnp.bfloat16

D_MODEL = 1024
N_META = 16
HEAD_DIM = 64
ATTN_HEADS = 8
KV_HEADS = 2
GQA_GROUP = ATTN_HEADS // KV_HEADS
ATTN_WIDTH = ATTN_HEADS * HEAD_DIM
KV_WIDTH = KV_HEADS * HEAD_DIM
QKV_WIDTH = ATTN_WIDTH + 2 * KV_WIDTH
LRU_WIDTH = 512
LRU_BLOCKS = 8
LRU_BLOCK = 64
LRU_HALF = 256
LRU_C = 8.0
CONV_WIDTH = 4
BLOCK = 128
PAD_ROWS = BLOCK - N_META
IN_WIDTH = QKV_WIDTH + 2 * LRU_WIDTH
D_FF = 4096
EPS = 1e-6
NEG = -1e30
N_DEV = 8
FF_CHUNK = D_FF // N_DEV
SUBLANES = 8
LANES = 128

ADAM_LR = 0.001
ADAM_B1 = 0.9
ADAM_B2 = 0.999
ADAM_EPS = 1e-08
ADAM_WD = 0.01
ADAM_STEP = 10

VMEM_LIMIT = 56 * 1024 * 1024


def _row_tile(rows):
    for t in (640, 512, 256, 128):
        if rows % t == 0:
            return t
    raise ValueError(rows)


def _big_tile(rows):
    for t in (1664, 1024, 512, 256, 128):
        if rows % t == 0:
            return t
    raise ValueError(rows)


def _rec_tile(rows):
    for t in (320, 256, 128):
        if rows % t == 0:
            return t
    raise ValueError(rows)


def _params(semantics):
    return pltpu.CompilerParams(dimension_semantics=semantics, vmem_limit_bytes=VMEM_LIMIT)


def _mm(a, b):
    return lax.dot_general(a, b, (((1,), (0,)), ((), ())), preferred_element_type=F32)


def _mm_nt(a, b):
    return lax.dot_general(a, b, (((1,), (1,)), ((), ())), preferred_element_type=F32)


def _mm_tn(a, b):
    return lax.dot_general(a, b, (((0,), (0,)), ((), ())), preferred_element_type=F32)


def _rms_fwd(x, g):
    rstd = lax.rsqrt(jnp.mean(x * x, axis=-1, keepdims=True) + EPS)
    xhat = x * rstd
    return xhat * g, xhat, rstd


def _rms_bwd(dy, xhat, rstd, g):
    dyg = dy * g
    c = jnp.mean(dyg * xhat, axis=-1, keepdims=True)
    dx = rstd * (dyg - xhat * c)
    dg = jnp.sum(dy * xhat, axis=0, keepdims=True)
    return dx, dg


def _sigmoid(x):
    return 1.0 / (1.0 + jnp.exp(-x))


def _log1p(x):
    u = 1.0 + x
    return jnp.where(u == 1.0, x, jnp.log(u) * x / (u - 1.0))


def _expm1(x):
    u = jnp.exp(x)
    um1 = u - 1.0
    return jnp.where(u == 1.0, x, jnp.where(um1 == -1.0, -1.0, um1 * x / jnp.log(u)))


def _softplus(x):
    return jnp.maximum(x, 0.0) + _log1p(jnp.exp(-jnp.abs(x)))


GELU_C = 0.7978845608028654
GELU_K = 0.044715


def _gelu(x):
    t = jnp.tanh(GELU_C * (x + GELU_K * x * x * x))
    return 0.5 * x * (1.0 + t), t


def _gelu_grad(x, t):
    return 0.5 * (1.0 + t) + 0.5 * x * (1.0 - t * t) * GELU_C * (1.0 + 3.0 * GELU_K * x * x)


def _full(shape):
    return pl.BlockSpec(shape, lambda *_: (0,) * len(shape))


def _resident(shape):
    return pl.BlockSpec(shape, lambda *_: (0,) * len(shape), pipeline_mode=pl.Buffered(1))


def _exchange_copies(ins, outs, sems, modes):
    send_sems, recv_sems, local_sems = sems
    x, y, c = lax.axis_index("x"), lax.axis_index("y"), lax.axis_index("c")
    me = 4 * x + 2 * y + c

    def block(a, dev):
        return ins[a] if modes[a] == "gather" else ins[a].at[dev]

    local = [pltpu.make_async_copy(block(a, me), outs[a].at[me], local_sems.at[a]) for a in range(len(ins))]
    sends, recvs = [], []
    for a in range(len(ins)):
        for k in range(N_DEV - 1):
            bits = k + 1
            px = jnp.bitwise_xor(x, (bits >> 2) & 1)
            py = jnp.bitwise_xor(y, (bits >> 1) & 1)
            pc = jnp.bitwise_xor(c, bits & 1)
            peer = 4 * px + 2 * py + pc
            common = dict(src_ref=block(a, peer), send_sem=send_sems.at[a, k], recv_sem=recv_sems.at[a, k],
                          device_id=(px, py, pc), device_id_type=pl.DeviceIdType.MESH)
            sends.append(pltpu.make_async_remote_copy(dst_ref=outs[a].at[me], **common))
            recvs.append(pltpu.make_async_remote_copy(dst_ref=outs[a].at[peer], **common))
    return local, sends, recvs


def _exchange_start(ins, outs, sems, modes):
    local, sends, _ = _exchange_copies(ins, outs, sems, modes)
    for cp in local + sends:
        cp.start()


def _exchange_wait(ins, outs, sems, modes):
    local, sends, recvs = _exchange_copies(ins, outs, sems, modes)
    for cp in recvs:
        cp.wait_recv()
    for cp in sends:
        cp.wait_send()
    for cp in local:
        cp.wait()


def _exchange_shapes(arrays, modes):
    return [jax.ShapeDtypeStruct((N_DEV,) + a.shape if mode == "gather" else a.shape, a.dtype)
            for a, mode in zip(arrays, modes)]


def _exchange_sems(na):
    return [pltpu.SemaphoreType.DMA((na, N_DEV - 1)), pltpu.SemaphoreType.DMA((na, N_DEV - 1)),
            pltpu.SemaphoreType.DMA((na,))]


ANY_SPACE = pl.BlockSpec(memory_space=pl.ANY)


def _exchange(arrays, modes, name):
    na = len(arrays)

    def body(*refs):
        ins, outs, sems = refs[:na], refs[na:2 * na], refs[2 * na:]
        _exchange_start(ins, outs, sems, modes)
        _exchange_wait(ins, outs, sems, modes)

    return pl.pallas_call(
        body, name=name, out_shape=_exchange_shapes(arrays, modes),
        in_specs=[ANY_SPACE] * na, out_specs=[ANY_SPACE] * na, scratch_shapes=_exchange_sems(na),
        compiler_params=pltpu.CompilerParams(has_side_effects=True),
    )(*arrays)


def _hosting_call(body, name, steps, in_specs, out_specs, out_shape, scratch_shapes, args, arrays, modes):
    n_in, n_out, n_scr, na = len(in_specs), len(out_specs), len(scratch_shapes), len(arrays)

    def hosting_body(*refs):
        cuts = [0]
        for n in (n_in, na, n_out, na, n_scr, 3):
            cuts.append(cuts[-1] + n)
        ins, x_ins, outs, x_outs, scr, sems = (refs[cuts[p]:cuts[p + 1]] for p in range(6))
        step = pl.program_id(0)

        @pl.when(step == 0)
        def _():
            _exchange_start(x_ins, x_outs, sems, modes)

        body(*ins, *outs, *scr)

        @pl.when(step == steps - 1)
        def _():
            _exchange_wait(x_ins, x_outs, sems, modes)

    res = pl.pallas_call(
        hosting_body, name=name, grid=(steps,),
        in_specs=list(in_specs) + [ANY_SPACE] * na, out_specs=list(out_specs) + [ANY_SPACE] * na,
        out_shape=list(out_shape) + _exchange_shapes(arrays, modes),
        scratch_shapes=list(scratch_shapes) + _exchange_sems(na),
        compiler_params=_params(("arbitrary",)),
    )(*args, *arrays)
    return res[:n_out], res[n_out:]


def _in_proj_fwd(h0, g1, w_in, carried, modes):
    rows = h0.shape[0]
    tm = _row_tile(rows)

    def body(h_ref, g_ref, w_ref, qkv_ref, zrec_ref, u_ref):
        u, _, _ = _rms_fwd(h_ref[...], g_ref[...])
        u = u.astype(BF16)
        u_ref[...] = u
        z = _mm(u, w_ref[...])
        qkv_ref[...] = z[:, :QKV_WIDTH].astype(BF16)
        zrec_ref[...] = z[:, QKV_WIDTH:]

    wide = pl.BlockSpec((tm, D_MODEL), lambda i: (i, 0))
    return _hosting_call(
        body, "in_proj_fwd", rows // tm,
        [wide, _full((1, D_MODEL)), _resident((D_MODEL, IN_WIDTH))],
        [pl.BlockSpec((tm, QKV_WIDTH), lambda i: (i, 0)), pl.BlockSpec((tm, 2 * LRU_WIDTH), lambda i: (i, 0)), wide],
        [jax.ShapeDtypeStruct((rows, QKV_WIDTH), BF16), jax.ShapeDtypeStruct((rows, 2 * LRU_WIDTH), F32),
         jax.ShapeDtypeStruct((rows, D_MODEL), BF16)],
        [], (h0, g1, w_in), carried, modes)


N_BIAS = 3


def _attn_bias():
    key = lax.broadcasted_iota(jnp.int32, (2 * BLOCK, GQA_GROUP * BLOCK), 0)
    r = lax.broadcasted_iota(jnp.int32, (2 * BLOCK, GQA_GROUP * BLOCK), 1) & (BLOCK - 1)
    band = (key > r) & (key <= r + BLOCK)
    out = [jnp.where(band & ((n - 1) * BLOCK + key >= PAD_ROWS), 0.0, NEG) for n in range(N_BIAS)]
    return jnp.stack(out).astype(F32)


def _attn_probs(k2, q4, bias, sink_row):
    s = _mm_nt(k2, q4) * (HEAD_DIM ** -0.5) + bias
    m = jnp.maximum(jnp.max(s, axis=0, keepdims=True), sink_row)
    p = jnp.exp(s - m)
    es = jnp.exp(sink_row - m)
    inv = 1.0 / (jnp.sum(p, axis=0, keepdims=True) + es)
    return p * inv, es * inv


def _heads(ref, first, count):
    return jnp.concatenate([ref[:, (first + g) * HEAD_DIM:(first + g + 1) * HEAD_DIM] for g in range(count)], axis=0)


def _sink_row(sink_ref, kv):
    g = lax.broadcasted_iota(jnp.int32, (1, GQA_GROUP * BLOCK), 1) // BLOCK
    row = jnp.full((1, GQA_GROUP * BLOCK), sink_ref[0, kv * GQA_GROUP], F32)
    for i in range(1, GQA_GROUP):
        row = jnp.where(g == i, sink_ref[0, kv * GQA_GROUP + i], row)
    return row


def _from_head_major(pieces):
    return jnp.concatenate(pieces, axis=0).T


def _attn_fwd(qkv, sinks, bias, carried, modes):
    rows = qkv.shape[0]
    nb = rows // BLOCK
    k_col, v_col = ATTN_WIDTH // KV_WIDTH, ATTN_WIDTH // KV_WIDTH + 1

    def body(sink_ref, bias_ref, q_ref, kp_ref, kc_ref, vp_ref, vc_ref, o_ref):
        bias_t = bias_ref[...]
        pieces = []
        for kv in range(KV_HEADS):
            sl = slice(kv * HEAD_DIM, (kv + 1) * HEAD_DIM)
            k2 = jnp.concatenate([kp_ref[:, sl], kc_ref[:, sl]], axis=0)
            v2 = jnp.concatenate([vp_ref[:, sl], vc_ref[:, sl]], axis=0)
            q4 = _heads(q_ref, kv * GQA_GROUP, GQA_GROUP)
            pn, _ = _attn_probs(k2, q4, bias_t, _sink_row(sink_ref, kv))
            ot = _mm_tn(v2, pn.astype(BF16))
            pieces += [ot[:, g * BLOCK:(g + 1) * BLOCK] for g in range(GQA_GROUP)]
        o_ref[...] = _from_head_major(pieces).astype(BF16)

    prev = lambda n: jnp.maximum(n - 1, 0)
    return _hosting_call(
        body, "attn_fwd", nb,
        [pl.BlockSpec(memory_space=pltpu.SMEM),
         pl.BlockSpec((None, 2 * BLOCK, GQA_GROUP * BLOCK), lambda n: (jnp.minimum(n, N_BIAS - 1), 0, 0)),
         pl.BlockSpec((BLOCK, ATTN_WIDTH), lambda n: (n, 0)),
         pl.BlockSpec((BLOCK, KV_WIDTH), lambda n: (prev(n), k_col)),
         pl.BlockSpec((BLOCK, KV_WIDTH), lambda n: (n, k_col)),
         pl.BlockSpec((BLOCK, KV_WIDTH), lambda n: (prev(n), v_col)),
         pl.BlockSpec((BLOCK, KV_WIDTH), lambda n: (n, v_col))],
        [pl.BlockSpec((BLOCK, ATTN_WIDTH), lambda n: (n, 0))],
        [jax.ShapeDtypeStruct((rows, ATTN_WIDTH), BF16)],
        [], (sinks, bias, qkv, qkv, qkv, qkv, qkv), carried, modes)


def _conv_taps(xbuf, tm):
    return [xbuf[pl.ds(SUBLANES - (CONV_WIDTH - 1 - j), tm), :] for j in range(CONV_WIDTH)]


def _lru_gates(xc, wa_ref, ba_ref, wx_ref, bx_ref, lam_ref):
    halves = [xc[:, h * LRU_HALF:(h + 1) * LRU_HALF].astype(BF16) for h in range(2)]
    gate_r = jnp.concatenate([_mm(halves[h], wa_ref[h]) for h in range(2)], axis=1) + ba_ref[...]
    gate_i = jnp.concatenate([_mm(halves[h], wx_ref[h]) for h in range(2)], axis=1) + bx_ref[...]
    r = _sigmoid(gate_r)
    ig = _sigmoid(gate_i)
    log_a = (-LRU_C) * r * _softplus(-lam_ref[...])
    a = jnp.exp(log_a)
    mult = jnp.sqrt(-_expm1(2.0 * log_a))
    return halves, r, ig, a, mult


def _scan_tile(a_ref, u_ref, out_ref, carry, tm, reverse):
    row = lax.broadcasted_iota(jnp.int32, (SUBLANES, LRU_WIDTH), 0)
    groups = tm // SUBLANES

    def step(j, prev):
        jj = groups - 1 - j if reverse else j
        o = pl.multiple_of(jj * SUBLANES, SUBLANES)
        a = a_ref[pl.ds(o, SUBLANES), :]
        u = u_ref[pl.ds(o, SUBLANES), :]
        for s in (1, 2, 4):
            shift = SUBLANES - s if reverse else s
            keep = (row < SUBLANES - s) if reverse else (row >= s)
            u = jnp.where(keep, a * pltpu.roll(u, shift, 0) + u, u)
            a = jnp.where(keep, a * pltpu.roll(a, shift, 0), a)
        out = a * prev + u
        out_ref[pl.ds(o, SUBLANES), :] = out
        return out[0:1, :] if reverse else out[SUBLANES - 1:SUBLANES, :]

    return lax.fori_loop(0, groups, step, carry)


def _rec_fwd(zrec, conv_w, conv_b, wa_bd, b_a, wx_bd, b_x, lam, carried, modes):
    rows = zrec.shape[0]
    tm = _rec_tile(rows)

    def body(xr_ref, yr_ref, cw_ref, cb_ref, wa_ref, ba_ref, wx_ref, bx_ref, lam_ref, rec_ref, h_ref,
             xbuf, a_s, u_s, carry):
        i = pl.program_id(0)

        @pl.when(i == 0)
        def _():
            xbuf[0:SUBLANES, :] = jnp.zeros((SUBLANES, LRU_WIDTH), F32)
            carry[...] = jnp.zeros_like(carry)

        @pl.when(i > 0)
        def _():
            xbuf[0:SUBLANES, :] = xbuf[tm:tm + SUBLANES, :]

        xbuf[SUBLANES:SUBLANES + tm, :] = xr_ref[...]
        taps = _conv_taps(xbuf, tm)
        xc = cb_ref[...] + sum(cw_ref[j:j + 1, :] * taps[j] for j in range(CONV_WIDTH))
        _, r, ig, a, mult = _lru_gates(xc, wa_ref, ba_ref, wx_ref, bx_ref, lam_ref)
        grow = i * tm + lax.broadcasted_iota(jnp.int32, (tm, LRU_WIDTH), 0)
        a_s[...] = a
        u_s[...] = jnp.where(grow >= PAD_ROWS, mult * (ig * xc), 0.0)
        carry[0:1, :] = _scan_tile(a_s, u_s, h_ref, carry[0:1, :], tm, reverse=False)
        gel, _ = _gelu(yr_ref[...])
        rec_ref[...] = (gel * h_ref[...]).astype(BF16)

    vec = _full((1, LRU_WIDTH))
    bd = _full((2, LRU_HALF, LRU_HALF))
    return _hosting_call(
        body, "rec_fwd", rows // tm,
        [pl.BlockSpec((tm, LRU_WIDTH), lambda i: (i, 0)), pl.BlockSpec((tm, LRU_WIDTH), lambda i: (i, 1)),
         _full((CONV_WIDTH, LRU_WIDTH)), vec, bd, vec, bd, vec, vec],
        [pl.BlockSpec((tm, LRU_WIDTH), lambda i: (i, 0)), pl.BlockSpec((tm, LRU_WIDTH), lambda i: (i, 0))],
        [jax.ShapeDtypeStruct((rows, LRU_WIDTH), BF16), jax.ShapeDtypeStruct((rows, LRU_WIDTH), F32)],
        [pltpu.VMEM((tm + SUBLANES, LRU_WIDTH), F32), pltpu.VMEM((tm, LRU_WIDTH), F32),
         pltpu.VMEM((tm, LRU_WIDTH), F32), pltpu.VMEM((SUBLANES, LRU_WIDTH), F32)],
        (zrec, zrec, conv_w, conv_b, wa_bd, b_a, wx_bd, b_x, lam), carried, modes)


def _out_proj_fwd(attn, rec, w_out, h0, g2):
    rows = h0.shape[0]
    tm = _row_tile(rows)

    def body(attn_ref, rec_ref, w_ref, h_ref, g_ref, mix_ref, h1_ref):
        mix = _mm(attn_ref[...], w_ref[0:ATTN_WIDTH, :]) + _mm(rec_ref[...], w_ref[ATTN_WIDTH:, :])
        y, _, _ = _rms_fwd(mix, g_ref[...])
        mix_ref[...] = mix
        h1_ref[...] = h_ref[...] + y

    half = pl.BlockSpec((tm, ATTN_WIDTH), lambda i: (i, 0))
    wide = pl.BlockSpec((tm, D_MODEL), lambda i: (i, 0))
    return pl.pallas_call(
        body, name="out_proj_fwd", grid=(rows // tm,),
        in_specs=[half, half, _resident((D_MODEL, D_MODEL)), wide, _full((1, D_MODEL))],
        out_specs=[wide, wide],
        out_shape=[jax.ShapeDtypeStruct((rows, D_MODEL), F32)] * 2,
        compiler_params=_params(("parallel",)),
    )(attn, rec, w_out, h0, g2)


FF_COLS = 1024


def _ffn_up(h1, g3, w1):
    rows = h1.shape[0]
    tm = _row_tile(rows)

    def body(h_ref, g_ref, w_ref, act_ref, u_ref):
        u, _, _ = _rms_fwd(h_ref[...], g_ref[...])
        u = u.astype(BF16)
        u_ref[...] = u
        for c in range(0, D_FF, FF_COLS):
            a1 = jnp.maximum(_mm(u, w_ref[:, c:c + FF_COLS]), 0.0)
            act_ref[:, c:c + FF_COLS] = (a1 * a1).astype(BF16)

    wide = pl.BlockSpec((tm, D_MODEL), lambda i: (i, 0))
    return pl.pallas_call(
        body, name="ffn_up", grid=(rows // tm,),
        in_specs=[wide, _full((1, D_MODEL)), _resident((D_MODEL, D_FF))],
        out_specs=[pl.BlockSpec((tm, D_FF), lambda i: (i, 0)), wide],
        out_shape=[jax.ShapeDtypeStruct((rows, D_FF), BF16), jax.ShapeDtypeStruct((rows, D_MODEL), BF16)],
        compiler_params=_params(("parallel",)),
    )(h1, g3, w1)


def _ffn_down_loss(act, w2, h1, target, g4):
    rows = h1.shape[0]
    tm = _row_tile(rows)

    def body(act_ref, w_ref, h_ref, t_ref, g_ref, dy_ref, df_ref, dg_ref, loss_ref):
        i = pl.program_id(0)

        @pl.when(i == 0)
        def _():
            dg_ref[...] = jnp.zeros_like(dg_ref)
            loss_ref[...] = jnp.zeros_like(loss_ref)

        g = g_ref[...]
        y, fhat, rstd = _rms_fwd(_mm(act_ref[...], w_ref[...]), g)
        grow = i * tm + lax.broadcasted_iota(jnp.int32, (tm, D_MODEL), 0)
        err = jnp.where(grow >= BLOCK, h_ref[...] + y - t_ref[...], 0.0)
        loss_ref[...] += (0.5 / D_MODEL) * jnp.sum(err * err)
        dy = err * (1.0 / D_MODEL)
        df, dg = _rms_bwd(dy, fhat, rstd, g)
        dy_ref[...] = dy
        df_ref[...] = df.astype(BF16)
        dg_ref[...] += dg

    wide = pl.BlockSpec((tm, D_MODEL), lambda i: (i, 0))
    return pl.pallas_call(
        body, name="ffn_down_loss", grid=(rows // tm,),
        in_specs=[pl.BlockSpec((tm, D_FF), lambda i: (i, 0)), _resident((D_FF, D_MODEL)), wide, wide, _full((1, D_MODEL))],
        out_specs=[wide, wide, _full((1, D_MODEL)), _full((SUBLANES, LANES))],
        out_shape=[jax.ShapeDtypeStruct((rows, D_MODEL), F32), jax.ShapeDtypeStruct((rows, D_MODEL), BF16),
                   jax.ShapeDtypeStruct((1, D_MODEL), F32), jax.ShapeDtypeStruct((SUBLANES, LANES), F32)],
        compiler_params=_params(("arbitrary",)),
    )(act, w2, h1, target, g4)


def _ffn_bwd_act(df, w2t, act):
    rows = df.shape[0]
    tm = _row_tile(rows)

    def body(df_ref, w_ref, act_ref, da_ref):
        df_t = df_ref[...]
        for c in range(0, D_FF, FF_COLS):
            dact = _mm(df_t, w_ref[:, c:c + FF_COLS])
            da_ref[:, c:c + FF_COLS] = (dact * (2.0 * jnp.sqrt(act_ref[:, c:c + FF_COLS].astype(F32)))).astype(BF16)

    hidden = pl.BlockSpec((tm, D_FF), lambda i: (i, 0))
    return pl.pallas_call(
        body, name="ffn_bwd_act", grid=(rows // tm,),
        in_specs=[pl.BlockSpec((tm, D_MODEL), lambda i: (i, 0)), _resident((D_MODEL, D_FF)), hidden],
        out_specs=hidden,
        out_shape=jax.ShapeDtypeStruct((rows, D_FF), BF16),
        compiler_params=_params(("parallel",)),
    )(df, w2t, act)


def _ffn_bwd_x(da, w1t, h1, dy, g3, carried, modes):
    rows = h1.shape[0]
    tm = _row_tile(rows)

    def body(da_ref, w_ref, h_ref, dy_ref, g_ref, dh_ref, dg_ref):
        @pl.when(pl.program_id(0) == 0)
        def _():
            dg_ref[...] = jnp.zeros_like(dg_ref)

        g = g_ref[...]
        _, xhat, rstd = _rms_fwd(h_ref[...], g)
        dx, dg = _rms_bwd(_mm(da_ref[...], w_ref[...]), xhat, rstd, g)
        dh_ref[...] = dy_ref[...] + dx
        dg_ref[...] += dg

    wide = pl.BlockSpec((tm, D_MODEL), lambda i: (i, 0))
    return _hosting_call(
        body, "ffn_bwd_x", rows // tm,
        [pl.BlockSpec((tm, D_FF), lambda i: (i, 0)), _resident((D_FF, D_MODEL)), wide, wide, _full((1, D_MODEL))],
        [wide, _full((1, D_MODEL))],
        [jax.ShapeDtypeStruct((rows, D_MODEL), F32), jax.ShapeDtypeStruct((1, D_MODEL), F32)],
        [], (da, w1t, h1, dy, g3), carried, modes)


def _ffn_bwd_weights(u2, da, act, df):
    rows = u2.shape[0]
    tb = _big_tile(rows)

    def body(u_ref, da_ref, act_ref, df_ref, dw1_ref, dw2_ref):
        @pl.when(pl.program_id(1) == 0)
        def _():
            dw1_ref[...] = jnp.zeros_like(dw1_ref)
            dw2_ref[...] = jnp.zeros_like(dw2_ref)

        dw1_ref[...] += _mm_tn(u_ref[...], da_ref[...])
        dw2_ref[...] += _mm_tn(act_ref[...], df_ref[...])

    wide = pl.BlockSpec((tb, D_MODEL), lambda j, i: (i, 0))
    chunk = pl.BlockSpec((tb, FF_COLS), lambda j, i: (i, j))
    return pl.pallas_call(
        body, name="ffn_bwd_weights", grid=(D_FF // FF_COLS, rows // tb),
        in_specs=[wide, chunk, chunk, wide],
        out_specs=[pl.BlockSpec((D_MODEL, FF_COLS), lambda j, i: (0, j)), pl.BlockSpec((FF_COLS, D_MODEL), lambda j, i: (j, 0))],
        out_shape=[jax.ShapeDtypeStruct((D_MODEL, D_FF), F32), jax.ShapeDtypeStruct((D_FF, D_MODEL), F32)],
        compiler_params=_params(("parallel", "arbitrary")),
    )(u2, da, act, df)


def _out_proj_bwd(dh1, mix, g2, w_out_t, attn, rec):
    rows = dh1.shape[0]
    tm = _row_tile(rows)

    def body(dh_ref, mix_ref, g_ref, w_ref, attn_ref, rec_ref, dattn_ref, drec_ref, dw_ref, dg_ref):
        @pl.when(pl.program_id(0) == 0)
        def _():
            dw_ref[...] = jnp.zeros_like(dw_ref)
            dg_ref[...] = jnp.zeros_like(dg_ref)

        g = g_ref[...]
        _, xhat, rstd = _rms_fwd(mix_ref[...], g)
        dmix, dg = _rms_bwd(dh_ref[...], xhat, rstd, g)
        dmix = dmix.astype(BF16)
        dg_ref[...] += dg
        din = _mm(dmix, w_ref[...])
        dattn_ref[...] = din[:, :ATTN_WIDTH].astype(BF16)
        drec_ref[...] = din[:, ATTN_WIDTH:]
        dw_ref[0:ATTN_WIDTH, :] += _mm_tn(attn_ref[...], dmix)
        dw_ref[ATTN_WIDTH:, :] += _mm_tn(rec_ref[...], dmix)

    half = pl.BlockSpec((tm, ATTN_WIDTH), lambda i: (i, 0))
    wide = pl.BlockSpec((tm, D_MODEL), lambda i: (i, 0))
    return pl.pallas_call(
        body, name="out_proj_bwd", grid=(rows // tm,),
        in_specs=[wide, wide, _full((1, D_MODEL)), _resident((D_MODEL, D_MODEL)), half, half],
        out_specs=[half, half, _full((D_MODEL, D_MODEL)), _full((1, D_MODEL))],
        out_shape=[jax.ShapeDtypeStruct((rows, ATTN_WIDTH), BF16), jax.ShapeDtypeStruct((rows, LRU_WIDTH), F32),
                   jax.ShapeDtypeStruct((D_MODEL, D_MODEL), F32), jax.ShapeDtypeStruct((1, D_MODEL), F32)],
        compiler_params=_params(("arbitrary",)),
    )(dh1, mix, g2, w_out_t, attn, rec)


def _attn_bwd(qkv, dattn, sinks, bias, carried, modes):
    rows = qkv.shape[0]
    nb = rows // BLOCK
    k_col, v_col = ATTN_WIDTH // KV_WIDTH, ATTN_WIDTH // KV_WIDTH + 1

    def body(sink_ref, bias_ref, q_ref, do_ref, kp_ref, kc_ref, vp_ref, vc_ref, dq_ref, dkv_ref, dsink_ref, dk_c, dv_c):
        n = pl.program_id(0)

        @pl.when(n == 0)
        def _():
            dk_c[...] = jnp.zeros_like(dk_c)
            dv_c[...] = jnp.zeros_like(dv_c)
            dsink_ref[...] = jnp.zeros_like(dsink_ref)

        @pl.when(n < nb)
        def _():
            bias_t = bias_ref[...]
            dq_parts, dk_parts, dv_parts, dsink_rows = [], [], [], []
            for kv in range(KV_HEADS):
                sl = slice(kv * HEAD_DIM, (kv + 1) * HEAD_DIM)
                k2 = jnp.concatenate([kp_ref[:, sl], kc_ref[:, sl]], axis=0)
                v2 = jnp.concatenate([vp_ref[:, sl], vc_ref[:, sl]], axis=0)
                q4 = _heads(q_ref, kv * GQA_GROUP, GQA_GROUP)
                do4 = _heads(do_ref, kv * GQA_GROUP, GQA_GROUP)
                pn, psink = _attn_probs(k2, q4, bias_t, _sink_row(sink_ref, kv))
                dpn = _mm_nt(v2, do4)
                delta = jnp.sum(pn * dpn, axis=0, keepdims=True)
                ds = ((pn * (dpn - delta)) * (HEAD_DIM ** -0.5)).astype(BF16)
                dqt = _mm_tn(k2, ds)
                dq_parts += [dqt[:, g * BLOCK:(g + 1) * BLOCK] for g in range(GQA_GROUP)]
                dk_parts.append(_mm(ds, q4))
                dv_parts.append(_mm(pn.astype(BF16), do4))
                sd = psink * delta
                for g in range(GQA_GROUP):
                    dsink_rows.append(jnp.full((1, LANES), -jnp.sum(sd[:, g * BLOCK:(g + 1) * BLOCK]), F32))
            dq_ref[...] = _from_head_major(dq_parts).astype(BF16)
            dsink_ref[...] += jnp.concatenate(dsink_rows, axis=0)
            dk2 = jnp.concatenate(dk_parts, axis=1)
            dv2 = jnp.concatenate(dv_parts, axis=1)
            dkv_ref[:, 0:KV_WIDTH] = (dk_c[...] + dk2[0:BLOCK]).astype(BF16)
            dkv_ref[:, KV_WIDTH:] = (dv_c[...] + dv2[0:BLOCK]).astype(BF16)
            dk_c[...] = dk2[BLOCK:]
            dv_c[...] = dv2[BLOCK:]

        @pl.when(n == nb)
        def _():
            dkv_ref[:, 0:KV_WIDTH] = dk_c[...].astype(BF16)
            dkv_ref[:, KV_WIDTH:] = dv_c[...].astype(BF16)

    cur = lambda n: jnp.minimum(n, nb - 1)
    prev = lambda n: jnp.maximum(jnp.minimum(n, nb - 1) - 1, 0)
    return _hosting_call(
        body, "attn_bwd", nb + 1,
        [pl.BlockSpec(memory_space=pltpu.SMEM),
         pl.BlockSpec((None, 2 * BLOCK, GQA_GROUP * BLOCK), lambda n: (jnp.minimum(n, N_BIAS - 1), 0, 0)),
         pl.BlockSpec((BLOCK, ATTN_WIDTH), lambda n: (cur(n), 0)),
         pl.BlockSpec((BLOCK, ATTN_WIDTH), lambda n: (cur(n), 0)),
         pl.BlockSpec((BLOCK, KV_WIDTH), lambda n: (prev(n), k_col)),
         pl.BlockSpec((BLOCK, KV_WIDTH), lambda n: (cur(n), k_col)),
         pl.BlockSpec((BLOCK, KV_WIDTH), lambda n: (prev(n), v_col)),
         pl.BlockSpec((BLOCK, KV_WIDTH), lambda n: (cur(n), v_col))],
        [pl.BlockSpec((BLOCK, ATTN_WIDTH), lambda n: (cur(n), 0)),
         pl.BlockSpec((BLOCK, 2 * KV_WIDTH), lambda n: (jnp.maximum(n - 1, 0), 0)),
         _full((ATTN_HEADS, LANES))],
        [jax.ShapeDtypeStruct((rows, ATTN_WIDTH), BF16), jax.ShapeDtypeStruct((rows, 2 * KV_WIDTH), BF16),
         jax.ShapeDtypeStruct((ATTN_HEADS, LANES), F32)],
        [pltpu.VMEM((BLOCK, KV_WIDTH), F32), pltpu.VMEM((BLOCK, KV_WIDTH), F32)],
        (sinks, bias, qkv, dattn, qkv, qkv, qkv, qkv), carried, modes)


ROW_CONV_B, ROW_B_A, ROW_B_X, ROW_LAMBDA = 4, 5, 6, 7


def _rec_bwd(drec, zrec, h, conv_w, conv_b, wa_bd, b_a, wx_bd, b_x, lam, carried, modes):
    rows = zrec.shape[0]
    tm = _rec_tile(rows)
    nt = rows // tm
    per = tm // SUBLANES

    def body(drec_ref, xr_ref, yr_ref, h_ref, xhalo_ref, hhalo_ref, cw_ref, cb_ref, wa_ref, ba_ref, wx_ref, bx_ref,
             lam_ref, drz_ref, small_ref, dwa_ref, dwx_ref, xbuf, hbuf, abuf, u_s, g_s, dbuf, carry):
        s = pl.program_id(0)
        i = nt - 1 - s

        @pl.when(s == 0)
        def _():
            small_ref[...] = jnp.zeros_like(small_ref)
            dwa_ref[...] = jnp.zeros_like(dwa_ref)
            dwx_ref[...] = jnp.zeros_like(dwx_ref)
            carry[...] = jnp.zeros_like(carry)
            abuf[tm:tm + SUBLANES, :] = jnp.zeros((SUBLANES, LRU_WIDTH), F32)
            dbuf[tm:tm + SUBLANES, :] = jnp.zeros((SUBLANES, LRU_WIDTH), F32)

        first = i == 0
        xbuf[0:SUBLANES, :] = jnp.where(first, 0.0, xhalo_ref[...])
        hbuf[0:SUBLANES, :] = jnp.where(first, 0.0, hhalo_ref[...])
        xbuf[SUBLANES:SUBLANES + tm, :] = xr_ref[...]
        hbuf[SUBLANES:SUBLANES + tm, :] = h_ref[...]

        taps = _conv_taps(xbuf, tm)
        xc = cb_ref[...] + sum(cw_ref[j:j + 1, :] * taps[j] for j in range(CONV_WIDTH))
        halves, r, ig, a, mult = _lru_gates(xc, wa_ref, ba_ref, wx_ref, bx_ref, lam_ref)

        yr = yr_ref[...]
        gel, t = _gelu(yr)
        drec_t = drec_ref[...]
        dyr = drec_t * h_ref[...] * _gelu_grad(yr, t)

        abuf[0:tm, :] = a
        u_s[...] = drec_t * gel
        a_next = abuf[pl.ds(1, tm), :]
        abuf[0:tm, :] = a_next
        carry[0:1, :] = _scan_tile(abuf, u_s, g_s, carry[0:1, :], tm, reverse=True)
        abuf[tm:tm + 1, :] = a[0:1, :]
        g = g_s[...]

        grow = i * tm + lax.broadcasted_iota(jnp.int32, (tm, LRU_WIDTH), 0)
        du = jnp.where(grow >= PAD_ROWS, g, 0.0)
        da = g * hbuf[pl.ds(SUBLANES - 1, tm), :]
        dmult = du * (ig * xc)
        dig = du * (mult * xc)
        dxc = du * (mult * ig)
        a2 = a * a
        dlog_a = da * a - dmult * (a2 / mult)
        sp = _softplus(-lam_ref[...])
        dgr = (dlog_a * (-LRU_C) * sp) * (r * (1.0 - r))
        dgi = dig * (ig * (1.0 - ig))
        dlam = jnp.sum(dlog_a * r, axis=0, keepdims=True) * (LRU_C * _sigmoid(-lam_ref[...]))
        dgr_b = [dgr[:, hh * LRU_HALF:(hh + 1) * LRU_HALF].astype(BF16) for hh in range(2)]
        dgi_b = [dgi[:, hh * LRU_HALF:(hh + 1) * LRU_HALF].astype(BF16) for hh in range(2)]
        dxc = dxc + jnp.concatenate(
            [_mm_nt(dgr_b[hh], wa_ref[hh]) + _mm_nt(dgi_b[hh], wx_ref[hh]) for hh in range(2)], axis=1)
        for hh in range(2):
            dwa_ref[hh] += _mm_tn(halves[hh], dgr_b[hh])
            dwx_ref[hh] += _mm_tn(halves[hh], dgi_b[hh])

        dbuf[0:tm, :] = dxc
        dxr = sum(cw_ref[j:j + 1, :] * dbuf[pl.ds(CONV_WIDTH - 1 - j, tm), :] for j in range(CONV_WIDTH))
        dbuf[tm:tm + SUBLANES, :] = dxc[0:SUBLANES, :]
        drz_ref[:, 0:LRU_WIDTH] = dxr.astype(BF16)
        drz_ref[:, LRU_WIDTH:] = dyr.astype(BF16)

        upd = [jnp.sum(dxc * taps[j], axis=0, keepdims=True) for j in range(CONV_WIDTH)]
        upd += [jnp.sum(dxc, axis=0, keepdims=True), jnp.sum(dgr, axis=0, keepdims=True),
                jnp.sum(dgi, axis=0, keepdims=True), dlam]
        small_ref[...] += jnp.concatenate(upd, axis=0)

    rev = lambda s: nt - 1 - s
    halo = lambda s: jnp.maximum(rev(s) * per - 1, 0)
    tile0 = pl.BlockSpec((tm, LRU_WIDTH), lambda s: (rev(s), 0))
    tile1 = pl.BlockSpec((tm, LRU_WIDTH), lambda s: (rev(s), 1))
    halo0 = pl.BlockSpec((SUBLANES, LRU_WIDTH), lambda s: (halo(s), 0))
    vec = _full((1, LRU_WIDTH))
    bd = _full((2, LRU_HALF, LRU_HALF))
    big = pltpu.VMEM((tm + SUBLANES, LRU_WIDTH), F32)
    tile = pltpu.VMEM((tm, LRU_WIDTH), F32)
    return _hosting_call(
        body, "rec_bwd", nt,
        [tile0, tile0, tile1, tile0, halo0, halo0, _full((CONV_WIDTH, LRU_WIDTH)), vec, bd, vec, bd, vec, vec],
        [pl.BlockSpec((tm, 2 * LRU_WIDTH), lambda s: (rev(s), 0)), _full((SUBLANES, LRU_WIDTH)), bd, bd],
        [jax.ShapeDtypeStruct((rows, 2 * LRU_WIDTH), BF16), jax.ShapeDtypeStruct((SUBLANES, LRU_WIDTH), F32),
         jax.ShapeDtypeStruct((2, LRU_HALF, LRU_HALF), F32), jax.ShapeDtypeStruct((2, LRU_HALF, LRU_HALF), F32)],
        [big, big, big, tile, tile, big, pltpu.VMEM((SUBLANES, LRU_WIDTH), F32)],
        (drec, zrec, zrec, h, zrec, h, conv_w, conv_b, wa_bd, b_a, wx_bd, b_x, lam), carried, modes)


DZ_CUTS = (0, ATTN_WIDTH, QKV_WIDTH, IN_WIDTH)


def _dz_specs(tm):
    return [pl.BlockSpec((tm, DZ_CUTS[p + 1] - DZ_CUTS[p]), lambda i: (i, 0)) for p in range(3)]


def _in_proj_bwd_x(h0, g1, dh1, dq, dkv, drz, w_in_t, carried, modes):
    rows = h0.shape[0]
    tm = _row_tile(rows)

    def body(h_ref, g_ref, dh1_ref, dq_ref, dkv_ref, drz_ref, w_ref, dh0_ref, dg_ref):
        @pl.when(pl.program_id(0) == 0)
        def _():
            dg_ref[...] = jnp.zeros_like(dg_ref)

        g = g_ref[...]
        _, xhat, rstd = _rms_fwd(h_ref[...], g)
        parts = (dq_ref[...], dkv_ref[...], drz_ref[...])
        du = sum(_mm(parts[p], w_ref[DZ_CUTS[p]:DZ_CUTS[p + 1], :]) for p in range(3))
        dx, dg = _rms_bwd(du, xhat, rstd, g)
        dh0_ref[...] = dh1_ref[...] + dx
        dg_ref[...] += dg

    wide = pl.BlockSpec((tm, D_MODEL), lambda i: (i, 0))
    return _hosting_call(
        body, "in_proj_bwd_x", rows // tm,
        [wide, _full((1, D_MODEL)), wide] + _dz_specs(tm) + [_resident((IN_WIDTH, D_MODEL))],
        [wide, _full((1, D_MODEL))],
        [jax.ShapeDtypeStruct((rows, D_MODEL), F32), jax.ShapeDtypeStruct((1, D_MODEL), F32)],
        [], (h0, g1, dh1, dq, dkv, drz, w_in_t), carried, modes)


def _in_proj_bwd_w(u1, dq, dkv, drz):
    rows = u1.shape[0]
    tb = _big_tile(rows)

    def body(u_ref, dq_ref, dkv_ref, drz_ref, dw_ref):
        @pl.when(pl.program_id(0) == 0)
        def _():
            dw_ref[...] = jnp.zeros_like(dw_ref)

        u = u_ref[...]
        for p, ref in enumerate((dq_ref, dkv_ref, drz_ref)):
            dw_ref[:, DZ_CUTS[p]:DZ_CUTS[p + 1]] += _mm_tn(u, ref[...])

    return pl.pallas_call(
        body, name="in_proj_bwd_w", grid=(rows // tb,),
        in_specs=[pl.BlockSpec((tb, D_MODEL), lambda i: (i, 0))] + _dz_specs(tb),
        out_specs=_full((D_MODEL, IN_WIDTH)),
        out_shape=jax.ShapeDtypeStruct((D_MODEL, IN_WIDTH), F32),
        compiler_params=_params(("arbitrary",)),
    )(u1, dq, dkv, drz)


def _adamw(w, m, v, parts, name):
    rows, cols = w.shape
    tr = next((t for t in (256, 128) if rows % t == 0), rows)

    def body(w_ref, m_ref, v_ref, p_ref, g_ref, d_ref, nm_ref, nv_ref):
        g = p_ref[0].astype(F32)
        for s in range(1, N_DEV):
            g = g + p_ref[s].astype(F32)
        nm = ADAM_B1 * m_ref[...] + (1.0 - ADAM_B1) * g
        nv = ADAM_B2 * v_ref[...] + (1.0 - ADAM_B2) * (g * g)
        m_hat = nm / (1.0 - ADAM_B1 ** ADAM_STEP)
        v_hat = nv / (1.0 - ADAM_B2 ** ADAM_STEP)
        g_ref[...] = g
        d_ref[...] = (-ADAM_LR) * (m_hat / (jnp.sqrt(v_hat) + ADAM_EPS) + ADAM_WD * w_ref[...])
        nm_ref[...] = nm
        nv_ref[...] = nv

    blk = pl.BlockSpec((tr, cols), lambda i: (i, 0))
    return pl.pallas_call(
        body, name=name, grid=(rows // tr,),
        in_specs=[blk, blk, blk, pl.BlockSpec((N_DEV, tr, cols), lambda i: (0, i, 0))],
        out_specs=[blk] * 4,
        out_shape=[jax.ShapeDtypeStruct((rows, cols), F32)] * 4,
        compiler_params=_params(("parallel",)),
    )(w, m, v, parts)


def _cols_from_shards(g):
    return jnp.transpose(g, (1, 0, 2)).reshape(g.shape[1], N_DEV * g.shape[2])


def _cols_to_shards(a):
    r, c = a.shape
    return jnp.transpose(a.reshape(r, N_DEV, c // N_DEV), (1, 0, 2))


def _block_diag(w):
    per = LRU_HALF // LRU_BLOCK
    w = w.reshape(2, per, LRU_BLOCK, LRU_BLOCK)
    eye = jnp.eye(per, dtype=w.dtype)
    return (w[:, :, :, None, :] * eye[None, :, None, :, None]).reshape(2, LRU_HALF, LRU_HALF)


def _block_diag_extract(t):
    per = LRU_HALF // LRU_BLOCK
    t = t.reshape(2, per, LRU_BLOCK, per, LRU_BLOCK)
    return jnp.stack([t[:, b, :, b, :] for b in range(per)], axis=1).reshape(LRU_BLOCKS, LRU_BLOCK, LRU_BLOCK)


SMALL_NAMES = ("conv_b", "w_a", "b_a", "w_x", "b_x", "lru_lambda", "attn_sinks",
               "g_post_mix", "g_pre_ffn", "g_post_ffn")


def _pack_small(vals):
    flat = []
    for name in SMALL_NAMES:
        a = vals[name].reshape(-1)
        flat.append(jnp.pad(a, (0, (-a.shape[0]) % LANES)))
    flat = jnp.concatenate(flat)
    rows = flat.shape[0] // LANES
    return jnp.pad(flat.reshape(rows, LANES), ((0, (-rows) % SUBLANES), (0, 0)))


def _unpack_small(packed, like):
    flat = packed.reshape(-1)
    out, at = {}, 0
    for name in SMALL_NAMES:
        n = like[name].size
        out[name] = flat[at:at + n].reshape(like[name].shape)
        at += n + (-n) % LANES
    return out


def kernel(x, meta_tokens, g_pre_mix, w_in, conv_w, conv_b, w_a, b_a, w_x, b_x, lru_lambda, attn_sinks, w_out, g_post_mix, g_pre_ffn, w_ff1, w_ff2, g_post_ffn, loss_target, m_meta_tokens, m_g_pre_mix, m_w_in, m_conv_w, m_conv_b, m_w_a, m_b_a, m_w_x, m_b_x, m_lru_lambda, m_attn_sinks, m_w_out, m_g_post_mix, m_g_pre_ffn, m_w_ff1, m_w_ff2, m_g_post_ffn, v_meta_tokens, v_g_pre_mix, v_w_in, v_conv_w, v_conv_b, v_w_a, v_b_a, v_w_x, v_b_x, v_lru_lambda, v_attn_sinks, v_w_out, v_g_post_mix, v_g_pre_ffn, v_w_ff1, v_w_ff2, v_g_post_ffn):
    weights = dict(meta_tokens=meta_tokens, g_pre_mix=g_pre_mix, w_in=w_in, conv_w=conv_w, conv_b=conv_b, w_a=w_a,
                   b_a=b_a, w_x=w_x, b_x=b_x, lru_lambda=lru_lambda, attn_sinks=attn_sinks, w_out=w_out,
                   g_post_mix=g_post_mix, g_pre_ffn=g_pre_ffn, w_ff1=w_ff1, w_ff2=w_ff2, g_post_ffn=g_post_ffn)
    mom_m = dict(meta_tokens=m_meta_tokens, g_pre_mix=m_g_pre_mix, w_in=m_w_in, conv_w=m_conv_w, conv_b=m_conv_b,
                 w_a=m_w_a, b_a=m_b_a, w_x=m_w_x, b_x=m_b_x, lru_lambda=m_lru_lambda, attn_sinks=m_attn_sinks,
                 w_out=m_w_out, g_post_mix=m_g_post_mix, g_pre_ffn=m_g_pre_ffn, w_ff1=m_w_ff1, w_ff2=m_w_ff2,
                 g_post_ffn=m_g_post_ffn)
    mom_v = dict(meta_tokens=v_meta_tokens, g_pre_mix=v_g_pre_mix, w_in=v_w_in, conv_w=v_conv_w, conv_b=v_conv_b,
                 w_a=v_w_a, b_a=v_b_a, w_x=v_w_x, b_x=v_b_x, lru_lambda=v_lru_lambda, attn_sinks=v_attn_sinks,
                 w_out=v_w_out, g_post_mix=v_g_post_mix, g_pre_ffn=v_g_pre_ffn, w_ff1=v_w_ff1, w_ff2=v_w_ff2,
                 g_post_ffn=v_g_post_ffn)
    order = list(weights)

    (g_win, g_meta, g_cw) = _exchange([w_in[0].astype(BF16), meta_tokens, conv_w[0]], ["gather"] * 3, "gather_first")
    w_in_full = _cols_from_shards(g_win)
    meta_full = _cols_from_shards(g_meta)
    conv_w_full = _cols_from_shards(g_cw)

    seq = x.shape[1]
    rows = BLOCK + seq
    h0 = jnp.concatenate([jnp.zeros((PAD_ROWS, D_MODEL), F32), meta_full, x[0]], axis=0)
    target = jnp.pad(loss_target[0], ((BLOCK, 0), (0, 0)))
    wa_bd = _block_diag(w_a[0]).astype(BF16)
    wx_bd = _block_diag(w_x[0]).astype(BF16)
    bias = _attn_bias()

    (qkv, zrec, u1), (g_wout,) = _in_proj_fwd(h0, g_pre_mix, w_in_full, [w_out[0].astype(BF16)], ["gather"])
    (attn,), (w1g,) = _attn_fwd(qkv, attn_sinks, bias, [w_ff1[0].astype(BF16)], ["gather"])
    (rec, h_lru), (w2g,) = _rec_fwd(zrec, conv_w_full, conv_b, wa_bd, b_a, wx_bd, b_x, lru_lambda,
                                     [w_ff2[0].astype(BF16)], ["gather"])
    w_out_full = g_wout.reshape(D_MODEL, D_MODEL)
    w1 = _cols_from_shards(w1g)
    w2 = w2g.reshape(D_FF, D_MODEL)
    mix, h1 = _out_proj_fwd(attn, rec, w_out_full, h0, g_post_mix)
    act, u2 = _ffn_up(h1, g_pre_ffn, w1)
    dy, df, dg_post_ffn, loss_acc = _ffn_down_loss(act, w2, h1, target, g_post_ffn)

    da1 = _ffn_bwd_act(df, w2.T, act)
    dw1, dw2 = _ffn_bwd_weights(u2, da1, act, df)
    (dh1, dg_pre_ffn), (p_w1,) = _ffn_bwd_x(da1, w1.T, h1, dy, g_pre_ffn, [_cols_to_shards(dw1).astype(BF16)], ["scatter"])
    dattn, drec, dw_out, dg_post_mix = _out_proj_bwd(dh1, mix, g_post_mix, w_out_full.T, attn, rec)
    (dq, dkv, dsinks), (p_w2,) = _attn_bwd(qkv, dattn, attn_sinks, bias,
                                            [dw2.reshape(N_DEV, FF_CHUNK, D_MODEL).astype(BF16)], ["scatter"])
    (drz, rec_small, dwa_bd, dwx_bd), (p_wout,) = _rec_bwd(
        drec, zrec, h_lru, conv_w_full, conv_b, wa_bd, b_a, wx_bd, b_x, lru_lambda,
        [dw_out.reshape(N_DEV, D_MODEL // N_DEV, D_MODEL).astype(BF16)], ["scatter"])
    dw_in = _in_proj_bwd_w(u1, dq, dkv, drz)
    small_grads = dict(
        conv_b=rec_small[ROW_CONV_B], w_a=_block_diag_extract(dwa_bd), b_a=rec_small[ROW_B_A],
        w_x=_block_diag_extract(dwx_bd), b_x=rec_small[ROW_B_X], lru_lambda=rec_small[ROW_LAMBDA],
        attn_sinks=dsinks[:, 0], g_post_mix=dg_post_mix, g_pre_ffn=dg_pre_ffn, g_post_ffn=dg_post_ffn)
    (dh0, dg_pre_mix), (p_win, p_cw, p_small) = _in_proj_bwd_x(
        h0, g_pre_mix, dh1, dq, dkv, drz, w_in_full.T,
        [_cols_to_shards(dw_in).astype(BF16), _cols_to_shards(rec_small[0:CONV_WIDTH]), _pack_small(small_grads)],
        ["scatter", "scatter", "gather"])
    p_meta, p_gpm = _exchange([_cols_to_shards(dh0[PAD_ROWS:BLOCK]), dg_pre_mix], ["scatter", "gather"], "exchange_last")

    res = {}
    res["g_pre_mix"] = _adamw(g_pre_mix, m_g_pre_mix, v_g_pre_mix, p_gpm, "adamw_g_pre_mix")
    res["w_in"] = _adamw(w_in[0], m_w_in[0], v_w_in[0], p_win, "adamw_w_in")
    res["w_out"] = _adamw(w_out[0], m_w_out[0], v_w_out[0], p_wout, "adamw_w_out")
    res["w_ff1"] = _adamw(w_ff1[0], m_w_ff1[0], v_w_ff1[0], p_w1, "adamw_w_ff1")
    res["w_ff2"] = _adamw(w_ff2[0], m_w_ff2[0], v_w_ff2[0], p_w2, "adamw_w_ff2")
    res["meta_tokens"] = _adamw(meta_tokens, m_meta_tokens, v_meta_tokens, p_meta, "adamw_meta")
    res["conv_w"] = _adamw(conv_w[0], m_conv_w[0], v_conv_w[0], p_cw, "adamw_conv_w")
    small = _adamw(_pack_small(weights), _pack_small(mom_m), _pack_small(mom_v), p_small, "adamw_small")
    small = [_unpack_small(t, weights) for t in small]
    for name in SMALL_NAMES:
        res[name] = tuple(t[name] for t in small)
    for name in ("w_in", "w_out", "w_ff1", "w_ff2", "conv_w"):
        res[name] = tuple(t[None] for t in res[name])

    loss = lax.psum(loss_acc[0, 0], ("x", "y", "c"))
    grad_x = dh0[BLOCK:][None]
    outs = [loss, grad_x]
    for k in range(4):
        outs += [res[name][k] for name in order]
    return tuple(outs)
```

```python
import jax
import jax.numpy as jnp
from jax import lax
from jax.experimental import pallas as pl
from jax.experimental.pallas import tpu as pltpu

F32 = jnp.float32
BF16 = jnp.bfloat16

D_MODEL = 1024
N_META = 16
HEAD_DIM = 64
ATTN_HEADS = 8
KV_HEADS = 2
GQA_GROUP = ATTN_HEADS // KV_HEADS
ATTN_WIDTH = ATTN_HEADS * HEAD_DIM
KV_WIDTH = KV_HEADS * HEAD_DIM
QKV_WIDTH = ATTN_WIDTH + 2 * KV_WIDTH
LRU_WIDTH = 512
LRU_BLOCKS = 8
LRU_BLOCK = 64
LRU_HALF = 256
LRU_C = 8.0
CONV_WIDTH = 4
BLOCK = 128
PAD_ROWS = BLOCK - N_META
IN_WIDTH = QKV_WIDTH + 2 * LRU_WIDTH
D_FF = 4096
EPS = 1e-6
NEG = -1e30
N_DEV = 8
FF_CHUNK = D_FF // N_DEV
SUBLANES = 8
LANES = 128

ADAM_LR = 0.001
ADAM_B1 = 0.9
ADAM_B2 = 0.999
ADAM_EPS = 1e-08
ADAM_WD = 0.01
ADAM_STEP = 10

VMEM_LIMIT = 56 * 1024 * 1024


def _row_tile(rows):
    for t in (640, 512, 256, 128):
        if rows % t == 0:
            return t
    raise ValueError(rows)


def _big_tile(rows):
    for t in (1664, 1024, 512, 256, 128):
        if rows % t == 0:
            return t
    raise ValueError(rows)


def _rec_tile(rows):
    for t in (320, 256, 128):
        if rows % t == 0:
            return t
    raise ValueError(rows)


def _params(semantics):
    return pltpu.CompilerParams(dimension_semantics=semantics, vmem_limit_bytes=VMEM_LIMIT)


def _mm(a, b):
    return lax.dot_general(a, b, (((1,), (0,)), ((), ())), preferred_element_type=F32)


def _mm_nt(a, b):
    return lax.dot_general(a, b, (((1,), (1,)), ((), ())), preferred_element_type=F32)


def _mm_tn(a, b):
    return lax.dot_general(a, b, (((0,), (0,)), ((), ())), preferred_element_type=F32)


def _rms_fwd(x, g):
    rstd = lax.rsqrt(jnp.mean(x * x, axis=-1, keepdims=True) + EPS)
    xhat = x * rstd
    return xhat * g, xhat, rstd


def _rms_bwd(dy, xhat, rstd, g):
    dyg = dy * g
    c = jnp.mean(dyg * xhat, axis=-1, keepdims=True)
    dx = rstd * (dyg - xhat * c)
    dg = jnp.sum(dy * xhat, axis=0, keepdims=True)
    return dx, dg


def _sigmoid(x):
    return 0.5 * jnp.tanh(0.5 * x) + 0.5


def _log1p(x):
    u = 1.0 + x
    return jnp.where(u == 1.0, x, jnp.log(u) * x / (u - 1.0))


def _one_minus_sq_exp(x, ex):
    return -jnp.tanh(x) * (1.0 + ex * ex)


def _sqrt_pos(y):
    r = lax.rsqrt(y)
    r = r * (1.5 - 0.5 * y * r * r)
    return jnp.where(y > 0.0, y * r, 0.0), r


def _softplus(x):
    return jnp.maximum(x, 0.0) + _log1p(jnp.exp(-jnp.abs(x)))


GELU_C = 0.7978845608028654
GELU_K = 0.044715


def _gelu(x):
    t = jnp.tanh(GELU_C * (x + GELU_K * x * x * x))
    return 0.5 * x * (1.0 + t), t


def _gelu_grad(x, t):
    return 0.5 * (1.0 + t) + 0.5 * x * (1.0 - t * t) * GELU_C * (1.0 + 3.0 * GELU_K * x * x)


def _full(shape):
    return pl.BlockSpec(shape, lambda *_: (0,) * len(shape))


def _resident(shape):
    return pl.BlockSpec(shape, lambda *_: (0,) * len(shape), pipeline_mode=pl.Buffered(1))


def _exchange_copies(ins, outs, sems, modes):
    send_sems, recv_sems, local_sems = sems
    x, y, c = lax.axis_index("x"), lax.axis_index("y"), lax.axis_index("c")
    me = 4 * x + 2 * y + c

    def block(a, dev):
        return ins[a] if modes[a] == "gather" else ins[a].at[dev]

    local = [pltpu.make_async_copy(block(a, me), outs[a].at[me], local_sems.at[a]) for a in range(len(ins))]
    sends, recvs = [], []
    for a in range(len(ins)):
        for k in range(N_DEV - 1):
            bits = k + 1
            px = jnp.bitwise_xor(x, (bits >> 2) & 1)
            py = jnp.bitwise_xor(y, (bits >> 1) & 1)
            pc = jnp.bitwise_xor(c, bits & 1)
            peer = 4 * px + 2 * py + pc
            common = dict(src_ref=block(a, peer), send_sem=send_sems.at[a, k], recv_sem=recv_sems.at[a, k],
                          device_id=(px, py, pc), device_id_type=pl.DeviceIdType.MESH)
            sends.append(pltpu.make_async_remote_copy(dst_ref=outs[a].at[me], **common))
            recvs.append(pltpu.make_async_remote_copy(dst_ref=outs[a].at[peer], **common))
    return local, sends, recvs


def _exchange_start(ins, outs, sems, modes):
    local, sends, _ = _exchange_copies(ins, outs, sems, modes)
    for cp in local + sends:
        cp.start()


def _exchange_wait(ins, outs, sems, modes):
    local, sends, recvs = _exchange_copies(ins, outs, sems, modes)
    for cp in recvs:
        cp.wait_recv()
    for cp in sends:
        cp.wait_send()
    for cp in local:
        cp.wait()


def _exchange_shapes(arrays, modes):
    return [jax.ShapeDtypeStruct((N_DEV,) + a.shape if mode == "gather" else a.shape, a.dtype)
            for a, mode in zip(arrays, modes)]


def _exchange_sems(na):
    return [pltpu.SemaphoreType.DMA((na, N_DEV - 1)), pltpu.SemaphoreType.DMA((na, N_DEV - 1)),
            pltpu.SemaphoreType.DMA((na,))]


ANY_SPACE = pl.BlockSpec(memory_space=pl.ANY)


def _exchange(arrays, modes, name):
    na = len(arrays)

    def body(*refs):
        ins, outs, sems = refs[:na], refs[na:2 * na], refs[2 * na:]
        _exchange_start(ins, outs, sems, modes)
        _exchange_wait(ins, outs, sems, modes)

    return pl.pallas_call(
        body, name=name, out_shape=_exchange_shapes(arrays, modes),
        in_specs=[ANY_SPACE] * na, out_specs=[ANY_SPACE] * na, scratch_shapes=_exchange_sems(na),
        compiler_params=pltpu.CompilerParams(has_side_effects=True),
    )(*arrays)


def _hosting_call(body, name, steps, in_specs, out_specs, out_shape, scratch_shapes, args, arrays, modes):
    n_in, n_out, n_scr, na = len(in_specs), len(out_specs), len(scratch_shapes), len(arrays)

    def hosting_body(*refs):
        cuts = [0]
        for n in (n_in, na, n_out, na, n_scr, 3):
            cuts.append(cuts[-1] + n)
        ins, x_ins, outs, x_outs, scr, sems = (refs[cuts[p]:cuts[p + 1]] for p in range(6))
        step = pl.program_id(0)

        @pl.when(step == 0)
        def _():
            _exchange_start(x_ins, x_outs, sems, modes)

        body(*ins, *outs, *scr)

        @pl.when(step == steps - 1)
        def _():
            _exchange_wait(x_ins, x_outs, sems, modes)

    res = pl.pallas_call(
        hosting_body, name=name, grid=(steps,),
        in_specs=list(in_specs) + [ANY_SPACE] * na, out_specs=list(out_specs) + [ANY_SPACE] * na,
        out_shape=list(out_shape) + _exchange_shapes(arrays, modes),
        scratch_shapes=list(scratch_shapes) + _exchange_sems(na),
        compiler_params=_params(("arbitrary",)),
    )(*args, *arrays)
    return res[:n_out], res[n_out:]


def _frame_rows(src_hbm, buf, sem, i, steps, tm):
    def first():
        return pltpu.make_async_copy(src_hbm.at[pl.ds(0, tm - BLOCK)], buf.at[0, pl.ds(BLOCK, tm - BLOCK)], sem.at[0])

    def later(t, slot):
        return pltpu.make_async_copy(src_hbm.at[pl.ds(pl.multiple_of(t * tm - BLOCK, BLOCK), tm)], buf.at[slot], sem.at[slot])

    slot = i % 2

    @pl.when(i == 0)
    def _():
        first().start()

    @pl.when(i + 1 < steps)
    def _():
        later(i + 1, 1 - slot).start()

    @pl.when(i == 0)
    def _():
        first().wait()

    @pl.when(i > 0)
    def _():
        later(i, slot).wait()

    return slot


def _frame_scratch(tm):
    return [pltpu.VMEM((2, tm, D_MODEL), F32), pltpu.SemaphoreType.DMA((2,))]


def _in_proj_fwd(head, x, g1, w_in, carried, modes):
    rows = BLOCK + x.shape[0]
    tm = _row_tile(rows)
    steps = rows // tm

    def body(head_ref, g_ref, w_ref, x_hbm, qkv_ref, zrec_ref, u_ref, h_ref, buf, sem):
        i = pl.program_id(0)
        slot = _frame_rows(x_hbm, buf, sem, i, steps, tm)

        @pl.when(i == 0)
        def _():
            buf[0, 0:BLOCK, :] = head_ref[...]

        h = buf[slot]
        h_ref[...] = h
        u, _, _ = _rms_fwd(h, g_ref[...])
        u = u.astype(BF16)
        u_ref[...] = u
        z = _mm(u, w_ref[...])
        qkv_ref[...] = z[:, :QKV_WIDTH].astype(BF16)
        zrec_ref[...] = z[:, QKV_WIDTH:]

    wide = pl.BlockSpec((tm, D_MODEL), lambda i: (i, 0))
    return _hosting_call(
        body, "in_proj_fwd", steps,
        [_full((BLOCK, D_MODEL)), _full((1, D_MODEL)), _resident((D_MODEL, IN_WIDTH)), ANY_SPACE],
        [pl.BlockSpec((tm, QKV_WIDTH), lambda i: (i, 0)), pl.BlockSpec((tm, 2 * LRU_WIDTH), lambda i: (i, 0)), wide, wide],
        [jax.ShapeDtypeStruct((rows, QKV_WIDTH), BF16), jax.ShapeDtypeStruct((rows, 2 * LRU_WIDTH), F32),
         jax.ShapeDtypeStruct((rows, D_MODEL), BF16), jax.ShapeDtypeStruct((rows, D_MODEL), F32)],
        _frame_scratch(tm), (head, g1, w_in, x), carried, modes)


N_BIAS = 3


def _attn_bias():
    key = lax.broadcasted_iota(jnp.int32, (2 * BLOCK, GQA_GROUP * BLOCK), 0)
    r = lax.broadcasted_iota(jnp.int32, (2 * BLOCK, GQA_GROUP * BLOCK), 1) & (BLOCK - 1)
    band = (key > r) & (key <= r + BLOCK)
    out = [jnp.where(band & ((n - 1) * BLOCK + key >= PAD_ROWS), 0.0, NEG) for n in range(N_BIAS)]
    return jnp.stack(out).astype(F32)


def _attn_probs(k2, q4, bias, sink_row):
    s = _mm_nt(k2, q4) * (HEAD_DIM ** -0.5) + bias
    m = jnp.maximum(jnp.max(s, axis=0, keepdims=True), sink_row)
    p = jnp.exp(s - m)
    es = jnp.exp(sink_row - m)
    inv = 1.0 / (jnp.sum(p, axis=0, keepdims=True) + es)
    return p * inv, es * inv


def _heads(ref, first, count):
    return jnp.concatenate([ref[:, (first + g) * HEAD_DIM:(first + g + 1) * HEAD_DIM] for g in range(count)], axis=0)


def _sink_row(sink_ref, kv):
    g = lax.broadcasted_iota(jnp.int32, (1, GQA_GROUP * BLOCK), 1) // BLOCK
    row = jnp.full((1, GQA_GROUP * BLOCK), sink_ref[0, kv * GQA_GROUP], F32)
    for i in range(1, GQA_GROUP):
        row = jnp.where(g == i, sink_ref[0, kv * GQA_GROUP + i], row)
    return row


def _from_head_major(pieces):
    return jnp.concatenate(pieces, axis=0).T


def _attn_fwd(qkv, sinks, bias, carried, modes):
    rows = qkv.shape[0]
    nb = rows // BLOCK
    k_col, v_col = ATTN_WIDTH // KV_WIDTH, ATTN_WIDTH // KV_WIDTH + 1

    def body(sink_ref, bias_ref, q_ref, kp_ref, kc_ref, vp_ref, vc_ref, o_ref):
        bias_t = bias_ref[...]
        pieces = []
        for kv in range(KV_HEADS):
            sl = slice(kv * HEAD_DIM, (kv + 1) * HEAD_DIM)
            k2 = jnp.concatenate([kp_ref[:, sl], kc_ref[:, sl]], axis=0)
            v2 = jnp.concatenate([vp_ref[:, sl], vc_ref[:, sl]], axis=0)
            q4 = _heads(q_ref, kv * GQA_GROUP, GQA_GROUP)
            pn, _ = _attn_probs(k2, q4, bias_t, _sink_row(sink_ref, kv))
            ot = _mm_tn(v2, pn.astype(BF16))
            pieces += [ot[:, g * BLOCK:(g + 1) * BLOCK] for g in range(GQA_GROUP)]
        o_ref[...] = _from_head_major(pieces).astype(BF16)

    prev = lambda n: jnp.maximum(n - 1, 0)
    return _hosting_call(
        body, "attn_fwd", nb,
        [pl.BlockSpec(memory_space=pltpu.SMEM),
         pl.BlockSpec((None, 2 * BLOCK, GQA_GROUP * BLOCK), lambda n: (jnp.minimum(n, N_BIAS - 1), 0, 0)),
         pl.BlockSpec((BLOCK, ATTN_WIDTH), lambda n: (n, 0)),
         pl.BlockSpec((BLOCK, KV_WIDTH), lambda n: (prev(n), k_col)),
         pl.BlockSpec((BLOCK, KV_WIDTH), lambda n: (n, k_col)),
         pl.BlockSpec((BLOCK, KV_WIDTH), lambda n: (prev(n), v_col)),
         pl.BlockSpec((BLOCK, KV_WIDTH), lambda n: (n, v_col))],
        [pl.BlockSpec((BLOCK, ATTN_WIDTH), lambda n: (n, 0))],
        [jax.ShapeDtypeStruct((rows, ATTN_WIDTH), BF16)],
        [], (sinks, bias, qkv, qkv, qkv, qkv, qkv), carried, modes)


def _conv_taps(xbuf, tm):
    return [xbuf[pl.ds(SUBLANES - (CONV_WIDTH - 1 - j), tm), :] for j in range(CONV_WIDTH)]


def _lru_gates(xc, wa_ref, ba_ref, wx_ref, bx_ref, lam_ref):
    halves = [xc[:, h * LRU_HALF:(h + 1) * LRU_HALF].astype(BF16) for h in range(2)]
    gate_r = jnp.concatenate([_mm(halves[h], wa_ref[h]) for h in range(2)], axis=1) + ba_ref[...]
    gate_i = jnp.concatenate([_mm(halves[h], wx_ref[h]) for h in range(2)], axis=1) + bx_ref[...]
    r = _sigmoid(gate_r)
    ig = _sigmoid(gate_i)
    log_a = (-LRU_C) * r * _softplus(-lam_ref[...])
    a = jnp.exp(log_a)
    mult, inv_mult = _sqrt_pos(_one_minus_sq_exp(log_a, a))
    return halves, r, ig, a, mult, inv_mult


def _scan_tile(a_ref, u_ref, out_ref, carry, tm, reverse):
    row = lax.broadcasted_iota(jnp.int32, (SUBLANES, LRU_WIDTH), 0)
    groups = tm // SUBLANES

    def step(j, prev):
        jj = groups - 1 - j if reverse else j
        o = pl.multiple_of(jj * SUBLANES, SUBLANES)
        a = a_ref[pl.ds(o, SUBLANES), :]
        u = u_ref[pl.ds(o, SUBLANES), :]
        for s in (1, 2, 4):
            shift = SUBLANES - s if reverse else s
            keep = (row < SUBLANES - s) if reverse else (row >= s)
            u = jnp.where(keep, a * pltpu.roll(u, shift, 0) + u, u)
            a = jnp.where(keep, a * pltpu.roll(a, shift, 0), a)
        out = a * prev + u
        out_ref[pl.ds(o, SUBLANES), :] = out
        return out[0:1, :] if reverse else out[SUBLANES - 1:SUBLANES, :]

    return lax.fori_loop(0, groups, step, carry)


def _rec_fwd(zrec, conv_w, conv_b, wa_bd, b_a, wx_bd, b_x, lam, carried, modes):
    rows = zrec.shape[0]
    tm = _rec_tile(rows)

    def body(xr_ref, yr_ref, cw_ref, cb_ref, wa_ref, ba_ref, wx_ref, bx_ref, lam_ref, rec_ref, h_ref, xc_ref,
             xbuf, a_s, u_s, carry):
        i = pl.program_id(0)

        @pl.when(i == 0)
        def _():
            xbuf[0:SUBLANES, :] = jnp.zeros((SUBLANES, LRU_WIDTH), F32)
            carry[...] = jnp.zeros_like(carry)

        @pl.when(i > 0)
        def _():
            xbuf[0:SUBLANES, :] = xbuf[tm:tm + SUBLANES, :]

        xbuf[SUBLANES:SUBLANES + tm, :] = xr_ref[...]
        taps = _conv_taps(xbuf, tm)
        xc = cb_ref[...] + sum(cw_ref[j:j + 1, :] * taps[j] for j in range(CONV_WIDTH))
        xc_ref[...] = xc
        _, r, ig, a, mult, _ = _lru_gates(xc, wa_ref, ba_ref, wx_ref, bx_ref, lam_ref)
        grow = i * tm + lax.broadcasted_iota(jnp.int32, (tm, LRU_WIDTH), 0)
        a_s[...] = a
        u_s[...] = jnp.where(grow >= PAD_ROWS, mult * (ig * xc), 0.0)
        carry[0:1, :] = _scan_tile(a_s, u_s, h_ref, carry[0:1, :], tm, reverse=False)
        gel, _ = _gelu(yr_ref[...])
        rec_ref[...] = (gel * h_ref[...]).astype(BF16)

    vec = _full((1, LRU_WIDTH))
    bd = _full((2, LRU_HALF, LRU_HALF))
    return _hosting_call(
        body, "rec_fwd", rows // tm,
        [pl.BlockSpec((tm, LRU_WIDTH), lambda i: (i, 0)), pl.BlockSpec((tm, LRU_WIDTH), lambda i: (i, 1)),
         _full((CONV_WIDTH, LRU_WIDTH)), vec, bd, vec, bd, vec, vec],
        [pl.BlockSpec((tm, LRU_WIDTH), lambda i: (i, 0))] * 3,
        [jax.ShapeDtypeStruct((rows, LRU_WIDTH), BF16), jax.ShapeDtypeStruct((rows, LRU_WIDTH), F32),
         jax.ShapeDtypeStruct((rows, LRU_WIDTH), F32)],
        [pltpu.VMEM((tm + SUBLANES, LRU_WIDTH), F32), pltpu.VMEM((tm, LRU_WIDTH), F32),
         pltpu.VMEM((tm, LRU_WIDTH), F32), pltpu.VMEM((SUBLANES, LRU_WIDTH), F32)],
        (zrec, zrec, conv_w, conv_b, wa_bd, b_a, wx_bd, b_x, lam), carried, modes)


def _out_proj_fwd(attn, rec, w_out, h0, g2):
    rows = h0.shape[0]
    tm = _row_tile(rows)

    def body(attn_ref, rec_ref, w_ref, h_ref, g_ref, mix_ref, h1_ref):
        mix = _mm(attn_ref[...], w_ref[0:ATTN_WIDTH, :]) + _mm(rec_ref[...], w_ref[ATTN_WIDTH:, :])
        y, _, _ = _rms_fwd(mix, g_ref[...])
        mix_ref[...] = mix
        h1_ref[...] = h_ref[...] + y

    half = pl.BlockSpec((tm, ATTN_WIDTH), lambda i: (i, 0))
    wide = pl.BlockSpec((tm, D_MODEL), lambda i: (i, 0))
    return pl.pallas_call(
        body, name="out_proj_fwd", grid=(rows // tm,),
        in_specs=[half, half, _resident((D_MODEL, D_MODEL)), wide, _full((1, D_MODEL))],
        out_specs=[wide, wide],
        out_shape=[jax.ShapeDtypeStruct((rows, D_MODEL), F32)] * 2,
        compiler_params=_params(("parallel",)),
    )(attn, rec, w_out, h0, g2)


FF_COLS = 1024


def _ffn_up(h1, g3, w1g):
    rows = h1.shape[0]
    tm = _row_tile(rows)

    def body(h_ref, g_ref, w_ref, act_ref, u_ref):
        u, _, _ = _rms_fwd(h_ref[...], g_ref[...])
        u = u.astype(BF16)
        u_ref[...] = u
        for d in range(N_DEV):
            a1 = jnp.maximum(_mm(u, w_ref[d]), 0.0)
            act_ref[:, d * FF_CHUNK:(d + 1) * FF_CHUNK] = (a1 * a1).astype(BF16)

    wide = pl.BlockSpec((tm, D_MODEL), lambda i: (i, 0))
    return pl.pallas_call(
        body, name="ffn_up", grid=(rows // tm,),
        in_specs=[wide, _full((1, D_MODEL)), _resident((N_DEV, D_MODEL, FF_CHUNK))],
        out_specs=[pl.BlockSpec((tm, D_FF), lambda i: (i, 0)), wide],
        out_shape=[jax.ShapeDtypeStruct((rows, D_FF), BF16), jax.ShapeDtypeStruct((rows, D_MODEL), BF16)],
        compiler_params=_params(("parallel",)),
    )(h1, g3, w1g)


def _ffn_down_loss(act, w2, h1, target, g4):
    rows = h1.shape[0]
    tm = _row_tile(rows)
    steps = rows // tm

    def body(act_ref, w_ref, h_ref, g_ref, t_hbm, dy_ref, df_ref, dg_ref, loss_ref, buf, sem):
        i = pl.program_id(0)
        slot = _frame_rows(t_hbm, buf, sem, i, steps, tm)

        @pl.when(i == 0)
        def _():
            dg_ref[...] = jnp.zeros_like(dg_ref)
            loss_ref[...] = jnp.zeros_like(loss_ref)
            buf[0, 0:BLOCK, :] = jnp.zeros((BLOCK, D_MODEL), F32)

        g = g_ref[...]
        y, fhat, rstd = _rms_fwd(_mm(act_ref[...], w_ref[...]), g)
        grow = i * tm + lax.broadcasted_iota(jnp.int32, (tm, D_MODEL), 0)
        err = jnp.where(grow >= BLOCK, h_ref[...] + y - buf[slot], 0.0)
        loss_ref[...] += (0.5 / D_MODEL) * jnp.sum(err * err)
        dy = err * (1.0 / D_MODEL)
        df, dg = _rms_bwd(dy, fhat, rstd, g)
        dy_ref[...] = dy
        df_ref[...] = df.astype(BF16)
        dg_ref[...] += dg

    wide = pl.BlockSpec((tm, D_MODEL), lambda i: (i, 0))
    return pl.pallas_call(
        body, name="ffn_down_loss", grid=(steps,),
        in_specs=[pl.BlockSpec((tm, D_FF), lambda i: (i, 0)), _resident((D_FF, D_MODEL)), wide, _full((1, D_MODEL)), ANY_SPACE],
        out_specs=[wide, wide, _full((1, D_MODEL)), _full((SUBLANES, LANES))],
        out_shape=[jax.ShapeDtypeStruct((rows, D_MODEL), F32), jax.ShapeDtypeStruct((rows, D_MODEL), BF16),
                   jax.ShapeDtypeStruct((1, D_MODEL), F32), jax.ShapeDtypeStruct((SUBLANES, LANES), F32)],
        scratch_shapes=_frame_scratch(tm),
        compiler_params=_params(("arbitrary",)),
    )(act, w2, h1, g4, target)


def _ffn_bwd_act(df, w2tg, act):
    rows = df.shape[0]
    tm = _row_tile(rows)

    def body(df_ref, w_ref, act_ref, da_ref):
        df_t = df_ref[...]
        for d in range(N_DEV):
            cols = slice(d * FF_CHUNK, (d + 1) * FF_CHUNK)
            dact = _mm(df_t, w_ref[d])
            relu_a1, _ = _sqrt_pos(act_ref[:, cols].astype(F32))
            da_ref[:, cols] = (dact * (2.0 * relu_a1)).astype(BF16)

    hidden = pl.BlockSpec((tm, D_FF), lambda i: (i, 0))
    return pl.pallas_call(
        body, name="ffn_bwd_act", grid=(rows // tm,),
        in_specs=[pl.BlockSpec((tm, D_MODEL), lambda i: (i, 0)), _resident((N_DEV, D_MODEL, FF_CHUNK)), hidden],
        out_specs=hidden,
        out_shape=jax.ShapeDtypeStruct((rows, D_FF), BF16),
        compiler_params=_params(("parallel",)),
    )(df, w2tg, act)


def _ffn_bwd_x(da, w1t, h1, dy, g3, carried, modes):
    rows = h1.shape[0]
    tm = _row_tile(rows)

    def body(da_ref, w_ref, h_ref, dy_ref, g_ref, dh_ref, dg_ref):
        @pl.when(pl.program_id(0) == 0)
        def _():
            dg_ref[...] = jnp.zeros_like(dg_ref)

        g = g_ref[...]
        _, xhat, rstd = _rms_fwd(h_ref[...], g)
        dx, dg = _rms_bwd(_mm(da_ref[...], w_ref[...]), xhat, rstd, g)
        dh_ref[...] = dy_ref[...] + dx
        dg_ref[...] += dg

    wide = pl.BlockSpec((tm, D_MODEL), lambda i: (i, 0))
    return _hosting_call(
        body, "ffn_bwd_x", rows // tm,
        [pl.BlockSpec((tm, D_FF), lambda i: (i, 0)), _resident((D_FF, D_MODEL)), wide, wide, _full((1, D_MODEL))],
        [wide, _full((1, D_MODEL))],
        [jax.ShapeDtypeStruct((rows, D_MODEL), F32), jax.ShapeDtypeStruct((1, D_MODEL), F32)],
        [], (da, w1t, h1, dy, g3), carried, modes)


def _ffn_bwd_weights(u2, da, act, df):
    rows = u2.shape[0]
    tb = _big_tile(rows)
    steps = rows // tb
    per = FF_COLS // FF_CHUNK

    def body(u_ref, da_ref, act_ref, df_ref, dw1_ref, dw2_ref, acc1, acc2):
        i = pl.program_id(1)

        @pl.when(i == 0)
        def _():
            acc1[...] = jnp.zeros_like(acc1)
            acc2[...] = jnp.zeros_like(acc2)

        acc1[...] += _mm_tn(u_ref[...], da_ref[...])
        acc2[...] += _mm_tn(act_ref[...], df_ref[...])

        @pl.when(i == steps - 1)
        def _():
            for p in range(per):
                dw1_ref[p] = acc1[:, p * FF_CHUNK:(p + 1) * FF_CHUNK].astype(BF16)
                dw2_ref[p] = acc2[p * FF_CHUNK:(p + 1) * FF_CHUNK, :].astype(BF16)

    wide = pl.BlockSpec((tb, D_MODEL), lambda j, i: (i, 0))
    chunk = pl.BlockSpec((tb, FF_COLS), lambda j, i: (i, j))
    return pl.pallas_call(
        body, name="ffn_bwd_weights", grid=(D_FF // FF_COLS, steps),
        in_specs=[wide, chunk, chunk, wide],
        out_specs=[pl.BlockSpec((per, D_MODEL, FF_CHUNK), lambda j, i: (j, 0, 0)),
                   pl.BlockSpec((per, FF_CHUNK, D_MODEL), lambda j, i: (j, 0, 0))],
        out_shape=[jax.ShapeDtypeStruct((N_DEV, D_MODEL, FF_CHUNK), BF16), jax.ShapeDtypeStruct((N_DEV, FF_CHUNK, D_MODEL), BF16)],
        scratch_shapes=[pltpu.VMEM((D_MODEL, FF_COLS), F32), pltpu.VMEM((FF_COLS, D_MODEL), F32)],
        compiler_params=_params(("parallel", "arbitrary")),
    )(u2, da, act, df)


def _out_proj_bwd(dh1, mix, g2, w_out_t, attn, rec):
    rows = dh1.shape[0]
    tm = _row_tile(rows)
    steps = rows // tm

    def body(dh_ref, mix_ref, g_ref, w_ref, attn_ref, rec_ref, dattn_ref, drec_ref, dw_ref, dg_ref, acc):
        i = pl.program_id(0)

        @pl.when(i == 0)
        def _():
            acc[...] = jnp.zeros_like(acc)
            dg_ref[...] = jnp.zeros_like(dg_ref)

        g = g_ref[...]
        _, xhat, rstd = _rms_fwd(mix_ref[...], g)
        dmix, dg = _rms_bwd(dh_ref[...], xhat, rstd, g)
        dmix = dmix.astype(BF16)
        dg_ref[...] += dg
        din = _mm(dmix, w_ref[...])
        dattn_ref[...] = din[:, :ATTN_WIDTH].astype(BF16)
        drec_ref[...] = din[:, ATTN_WIDTH:]
        acc[0:ATTN_WIDTH, :] += _mm_tn(attn_ref[...], dmix)
        acc[ATTN_WIDTH:, :] += _mm_tn(rec_ref[...], dmix)

        @pl.when(i == steps - 1)
        def _():
            dw_ref[...] = acc[...].astype(BF16)

    half = pl.BlockSpec((tm, ATTN_WIDTH), lambda i: (i, 0))
    wide = pl.BlockSpec((tm, D_MODEL), lambda i: (i, 0))
    return pl.pallas_call(
        body, name="out_proj_bwd", grid=(steps,),
        in_specs=[wide, wide, _full((1, D_MODEL)), _resident((D_MODEL, D_MODEL)), half, half],
        out_specs=[half, half, _full((D_MODEL, D_MODEL)), _full((1, D_MODEL))],
        out_shape=[jax.ShapeDtypeStruct((rows, ATTN_WIDTH), BF16), jax.ShapeDtypeStruct((rows, LRU_WIDTH), F32),
                   jax.ShapeDtypeStruct((D_MODEL, D_MODEL), BF16), jax.ShapeDtypeStruct((1, D_MODEL), F32)],
        scratch_shapes=[pltpu.VMEM((D_MODEL, D_MODEL), F32)],
        compiler_params=_params(("arbitrary",)),
    )(dh1, mix, g2, w_out_t, attn, rec)


def _attn_bwd(qkv, dattn, sinks, bias, carried, modes):
    rows = qkv.shape[0]
    nb = rows // BLOCK
    k_col, v_col = ATTN_WIDTH // KV_WIDTH, ATTN_WIDTH // KV_WIDTH + 1

    def body(sink_ref, bias_ref, q_ref, do_ref, kp_ref, kc_ref, vp_ref, vc_ref, dq_ref, dkv_ref, dsink_ref, dk_c, dv_c):
        n = pl.program_id(0)

        @pl.when(n == 0)
        def _():
            dk_c[...] = jnp.zeros_like(dk_c)
            dv_c[...] = jnp.zeros_like(dv_c)
            dsink_ref[...] = jnp.zeros_like(dsink_ref)

        @pl.when(n < nb)
        def _():
            bias_t = bias_ref[...]
            dq_parts, dk_parts, dv_parts, dsink_rows = [], [], [], []
            for kv in range(KV_HEADS):
                sl = slice(kv * HEAD_DIM, (kv + 1) * HEAD_DIM)
                k2 = jnp.concatenate([kp_ref[:, sl], kc_ref[:, sl]], axis=0)
                v2 = jnp.concatenate([vp_ref[:, sl], vc_ref[:, sl]], axis=0)
                q4 = _heads(q_ref, kv * GQA_GROUP, GQA_GROUP)
                do4 = _heads(do_ref, kv * GQA_GROUP, GQA_GROUP)
                pn, psink = _attn_probs(k2, q4, bias_t, _sink_row(sink_ref, kv))
                dpn = _mm_nt(v2, do4)
                delta = jnp.sum(pn * dpn, axis=0, keepdims=True)
                ds = ((pn * (dpn - delta)) * (HEAD_DIM ** -0.5)).astype(BF16)
                dqt = _mm_tn(k2, ds)
                dq_parts += [dqt[:, g * BLOCK:(g + 1) * BLOCK] for g in range(GQA_GROUP)]
                dk_parts.append(_mm(ds, q4))
                dv_parts.append(_mm(pn.astype(BF16), do4))
                sd = psink * delta
                for g in range(GQA_GROUP):
                    dsink_rows.append(jnp.full((1, LANES), -jnp.sum(sd[:, g * BLOCK:(g + 1) * BLOCK]), F32))
            dq_ref[...] = _from_head_major(dq_parts).astype(BF16)
            dsink_ref[...] += jnp.concatenate(dsink_rows, axis=0)
            dk2 = jnp.concatenate(dk_parts, axis=1)
            dv2 = jnp.concatenate(dv_parts, axis=1)
            dkv_ref[:, 0:KV_WIDTH] = (dk_c[...] + dk2[0:BLOCK]).astype(BF16)
            dkv_ref[:, KV_WIDTH:] = (dv_c[...] + dv2[0:BLOCK]).astype(BF16)
            dk_c[...] = dk2[BLOCK:]
            dv_c[...] = dv2[BLOCK:]

        @pl.when(n == nb)
        def _():
            dkv_ref[:, 0:KV_WIDTH] = dk_c[...].astype(BF16)
            dkv_ref[:, KV_WIDTH:] = dv_c[...].astype(BF16)

    cur = lambda n: jnp.minimum(n, nb - 1)
    prev = lambda n: jnp.maximum(jnp.minimum(n, nb - 1) - 1, 0)
    return _hosting_call(
        body, "attn_bwd", nb + 1,
        [pl.BlockSpec(memory_space=pltpu.SMEM),
         pl.BlockSpec((None, 2 * BLOCK, GQA_GROUP * BLOCK), lambda n: (jnp.minimum(n, N_BIAS - 1), 0, 0)),
         pl.BlockSpec((BLOCK, ATTN_WIDTH), lambda n: (cur(n), 0)),
         pl.BlockSpec((BLOCK, ATTN_WIDTH), lambda n: (cur(n), 0)),
         pl.BlockSpec((BLOCK, KV_WIDTH), lambda n: (prev(n), k_col)),
         pl.BlockSpec((BLOCK, KV_WIDTH), lambda n: (cur(n), k_col)),
         pl.BlockSpec((BLOCK, KV_WIDTH), lambda n: (prev(n), v_col)),
         pl.BlockSpec((BLOCK, KV_WIDTH), lambda n: (cur(n), v_col))],
        [pl.BlockSpec((BLOCK, ATTN_WIDTH), lambda n: (cur(n), 0)),
         pl.BlockSpec((BLOCK, 2 * KV_WIDTH), lambda n: (jnp.maximum(n - 1, 0), 0)),
         _full((ATTN_HEADS, LANES))],
        [jax.ShapeDtypeStruct((rows, ATTN_WIDTH), BF16), jax.ShapeDtypeStruct((rows, 2 * KV_WIDTH), BF16),
         jax.ShapeDtypeStruct((ATTN_HEADS, LANES), F32)],
        [pltpu.VMEM((BLOCK, KV_WIDTH), F32), pltpu.VMEM((BLOCK, KV_WIDTH), F32)],
        (sinks, bias, qkv, dattn, qkv, qkv, qkv, qkv), carried, modes)


ROW_CONV_B, ROW_B_A, ROW_B_X, ROW_LAMBDA = 4, 5, 6, 7


def _rec_bwd(drec, zrec, h, xc_all, conv_w, wa_bd, b_a, wx_bd, b_x, lam, carried, modes):
    rows = zrec.shape[0]
    tm = _rec_tile(rows)
    nt = rows // tm
    per = tm // SUBLANES

    def body(drec_ref, xr_ref, yr_ref, h_ref, xc_ref, hhalo_ref, cw_ref, wa_ref, ba_ref, wx_ref, bx_ref,
             lam_ref, drz_ref, small_ref, dwa_ref, dwx_ref, hbuf, abuf, u_s, g_s, dbuf, carry):
        s = pl.program_id(0)
        i = nt - 1 - s

        @pl.when(s == 0)
        def _():
            small_ref[...] = jnp.zeros_like(small_ref)
            dwa_ref[...] = jnp.zeros_like(dwa_ref)
            dwx_ref[...] = jnp.zeros_like(dwx_ref)
            carry[...] = jnp.zeros_like(carry)
            abuf[tm:tm + SUBLANES, :] = jnp.zeros((SUBLANES, LRU_WIDTH), F32)
            dbuf[tm:tm + SUBLANES, :] = jnp.zeros((SUBLANES, LRU_WIDTH), F32)

        hbuf[0:SUBLANES, :] = jnp.where(i == 0, 0.0, hhalo_ref[...])
        hbuf[SUBLANES:SUBLANES + tm, :] = h_ref[...]

        xc = xc_ref[...]
        halves, r, ig, a, mult, inv_mult = _lru_gates(xc, wa_ref, ba_ref, wx_ref, bx_ref, lam_ref)

        yr = yr_ref[...]
        gel, t = _gelu(yr)
        drec_t = drec_ref[...]
        dyr = drec_t * h_ref[...] * _gelu_grad(yr, t)

        abuf[0:tm, :] = a
        u_s[...] = drec_t * gel
        a_next = abuf[pl.ds(1, tm), :]
        abuf[0:tm, :] = a_next
        carry[0:1, :] = _scan_tile(abuf, u_s, g_s, carry[0:1, :], tm, reverse=True)
        abuf[tm:tm + 1, :] = a[0:1, :]
        g = g_s[...]

        grow = i * tm + lax.broadcasted_iota(jnp.int32, (tm, LRU_WIDTH), 0)
        du = jnp.where(grow >= PAD_ROWS, g, 0.0)
        da = g * hbuf[pl.ds(SUBLANES - 1, tm), :]
        dmult = du * (ig * xc)
        dig = du * (mult * xc)
        dxc = du * (mult * ig)
        dlog_a = da * a - dmult * (a * a * inv_mult)
        sp = _softplus(-lam_ref[...])
        dgr = (dlog_a * (-LRU_C) * sp) * (r * (1.0 - r))
        dgi = dig * (ig * (1.0 - ig))
        dlam = jnp.sum(dlog_a * r, axis=0, keepdims=True) * (LRU_C * _sigmoid(-lam_ref[...]))
        dgr_b = [dgr[:, hh * LRU_HALF:(hh + 1) * LRU_HALF].astype(BF16) for hh in range(2)]
        dgi_b = [dgi[:, hh * LRU_HALF:(hh + 1) * LRU_HALF].astype(BF16) for hh in range(2)]
        dxc = dxc + jnp.concatenate(
            [_mm_nt(dgr_b[hh], wa_ref[hh]) + _mm_nt(dgi_b[hh], wx_ref[hh]) for hh in range(2)], axis=1)
        for hh in range(2):
            dwa_ref[hh] += _mm_tn(halves[hh], dgr_b[hh])
            dwx_ref[hh] += _mm_tn(halves[hh], dgi_b[hh])

        dbuf[0:tm, :] = dxc
        ahead = [dbuf[pl.ds(CONV_WIDTH - 1 - j, tm), :] for j in range(CONV_WIDTH)]
        dxr = sum(cw_ref[j:j + 1, :] * ahead[j] for j in range(CONV_WIDTH))
        dbuf[tm:tm + SUBLANES, :] = dxc[0:SUBLANES, :]
        drz_ref[:, 0:LRU_WIDTH] = dxr.astype(BF16)
        drz_ref[:, LRU_WIDTH:] = dyr.astype(BF16)

        xr = xr_ref[...]
        upd = [jnp.sum(xr * ahead[j], axis=0, keepdims=True) for j in range(CONV_WIDTH)]
        upd += [jnp.sum(dxc, axis=0, keepdims=True), jnp.sum(dgr, axis=0, keepdims=True),
                jnp.sum(dgi, axis=0, keepdims=True), dlam]
        small_ref[...] += jnp.concatenate(upd, axis=0)

    rev = lambda s: nt - 1 - s
    halo = lambda s: jnp.maximum(rev(s) * per - 1, 0)
    tile0 = pl.BlockSpec((tm, LRU_WIDTH), lambda s: (rev(s), 0))
    tile1 = pl.BlockSpec((tm, LRU_WIDTH), lambda s: (rev(s), 1))
    halo0 = pl.BlockSpec((SUBLANES, LRU_WIDTH), lambda s: (halo(s), 0))
    vec = _full((1, LRU_WIDTH))
    bd = _full((2, LRU_HALF, LRU_HALF))
    big = pltpu.VMEM((tm + SUBLANES, LRU_WIDTH), F32)
    tile = pltpu.VMEM((tm, LRU_WIDTH), F32)
    return _hosting_call(
        body, "rec_bwd", nt,
        [tile0, tile0, tile1, tile0, tile0, halo0, _full((CONV_WIDTH, LRU_WIDTH)), bd, vec, bd, vec, vec],
        [pl.BlockSpec((tm, 2 * LRU_WIDTH), lambda s: (rev(s), 0)), _full((SUBLANES, LRU_WIDTH)), bd, bd],
        [jax.ShapeDtypeStruct((rows, 2 * LRU_WIDTH), BF16), jax.ShapeDtypeStruct((SUBLANES, LRU_WIDTH), F32),
         jax.ShapeDtypeStruct((2, LRU_HALF, LRU_HALF), F32), jax.ShapeDtypeStruct((2, LRU_HALF, LRU_HALF), F32)],
        [big, big, tile, tile, big, pltpu.VMEM((SUBLANES, LRU_WIDTH), F32)],
        (drec, zrec, zrec, h, xc_all, h, conv_w, wa_bd, b_a, wx_bd, b_x, lam), carried, modes)


DZ_CUTS = (0, ATTN_WIDTH, QKV_WIDTH, IN_WIDTH)


def _dz_specs(tm):
    return [pl.BlockSpec((tm, DZ_CUTS[p + 1] - DZ_CUTS[p]), lambda i: (i, 0)) for p in range(3)]


def _in_proj_bwd_x(h0, g1, dh1, dq, dkv, drz, w_in_t, carried, modes):
    rows = h0.shape[0]
    tm = _row_tile(rows)

    def body(h_ref, g_ref, dh1_ref, dq_ref, dkv_ref, drz_ref, w_ref, dh0_ref, dg_ref):
        @pl.when(pl.program_id(0) == 0)
        def _():
            dg_ref[...] = jnp.zeros_like(dg_ref)

        g = g_ref[...]
        _, xhat, rstd = _rms_fwd(h_ref[...], g)
        parts = (dq_ref[...], dkv_ref[...], drz_ref[...])
        du = sum(_mm(parts[p], w_ref[DZ_CUTS[p]:DZ_CUTS[p + 1], :]) for p in range(3))
        dx, dg = _rms_bwd(du, xhat, rstd, g)
        dh0_ref[...] = dh1_ref[...] + dx
        dg_ref[...] += dg

    wide = pl.BlockSpec((tm, D_MODEL), lambda i: (i, 0))
    return _hosting_call(
        body, "in_proj_bwd_x", rows // tm,
        [wide, _full((1, D_MODEL)), wide] + _dz_specs(tm) + [_resident((IN_WIDTH, D_MODEL))],
        [wide, _full((1, D_MODEL))],
        [jax.ShapeDtypeStruct((rows, D_MODEL), F32), jax.ShapeDtypeStruct((1, D_MODEL), F32)],
        [], (h0, g1, dh1, dq, dkv, drz, w_in_t), carried, modes)


def _in_proj_bwd_w(u1, dq, dkv, drz):
    rows = u1.shape[0]
    tb = _big_tile(rows)

    def body(u_ref, dq_ref, dkv_ref, drz_ref, dw_ref):
        @pl.when(pl.program_id(0) == 0)
        def _():
            dw_ref[...] = jnp.zeros_like(dw_ref)

        u = u_ref[...]
        for p, ref in enumerate((dq_ref, dkv_ref, drz_ref)):
            dw_ref[:, DZ_CUTS[p]:DZ_CUTS[p + 1]] += _mm_tn(u, ref[...])

    return pl.pallas_call(
        body, name="in_proj_bwd_w", grid=(rows // tb,),
        in_specs=[pl.BlockSpec((tb, D_MODEL), lambda i: (i, 0))] + _dz_specs(tb),
        out_specs=_full((D_MODEL, IN_WIDTH)),
        out_shape=jax.ShapeDtypeStruct((D_MODEL, IN_WIDTH), F32),
        compiler_params=_params(("arbitrary",)),
    )(u1, dq, dkv, drz)


def _adamw(w, m, v, parts, name):
    rows, cols = w.shape
    tr = next((t for t in (256, 128) if rows % t == 0), rows)

    def body(w_ref, m_ref, v_ref, p_ref, g_ref, d_ref, nm_ref, nv_ref):
        g = p_ref[0].astype(F32)
        for s in range(1, N_DEV):
            g = g + p_ref[s].astype(F32)
        nm = ADAM_B1 * m_ref[...] + (1.0 - ADAM_B1) * g
        nv = ADAM_B2 * v_ref[...] + (1.0 - ADAM_B2) * (g * g)
        m_hat = nm / (1.0 - ADAM_B1 ** ADAM_STEP)
        v_hat = nv / (1.0 - ADAM_B2 ** ADAM_STEP)
        g_ref[...] = g
        d_ref[...] = (-ADAM_LR) * (m_hat / (jnp.sqrt(v_hat) + ADAM_EPS) + ADAM_WD * w_ref[...])
        nm_ref[...] = nm
        nv_ref[...] = nv

    blk = pl.BlockSpec((tr, cols), lambda i: (i, 0))
    return pl.pallas_call(
        body, name=name, grid=(rows // tr,),
        in_specs=[blk, blk, blk, pl.BlockSpec((N_DEV, tr, cols), lambda i: (0, i, 0))],
        out_specs=[blk] * 4,
        out_shape=[jax.ShapeDtypeStruct((rows, cols), F32)] * 4,
        compiler_params=_params(("parallel",)),
    )(w, m, v, parts)


def _cols_from_shards(g):
    return jnp.transpose(g, (1, 0, 2)).reshape(g.shape[1], N_DEV * g.shape[2])


def _cols_to_shards(a):
    r, c = a.shape
    return jnp.transpose(a.reshape(r, N_DEV, c // N_DEV), (1, 0, 2))


def _block_diag(w):
    per = LRU_HALF // LRU_BLOCK
    w = w.reshape(2, per, LRU_BLOCK, LRU_BLOCK)
    eye = jnp.eye(per, dtype=w.dtype)
    return (w[:, :, :, None, :] * eye[None, :, None, :, None]).reshape(2, LRU_HALF, LRU_HALF)


def _block_diag_extract(t):
    per = LRU_HALF // LRU_BLOCK
    t = t.reshape(2, per, LRU_BLOCK, per, LRU_BLOCK)
    return jnp.stack([t[:, b, :, b, :] for b in range(per)], axis=1).reshape(LRU_BLOCKS, LRU_BLOCK, LRU_BLOCK)


SMALL_NAMES = ("conv_b", "w_a", "b_a", "w_x", "b_x", "lru_lambda", "attn_sinks",
               "g_post_mix", "g_pre_ffn", "g_post_ffn")


def _pack_small(vals):
    flat = []
    for name in SMALL_NAMES:
        a = vals[name].reshape(-1)
        flat.append(jnp.pad(a, (0, (-a.shape[0]) % LANES)))
    flat = jnp.concatenate(flat)
    rows = flat.shape[0] // LANES
    return jnp.pad(flat.reshape(rows, LANES), ((0, (-rows) % SUBLANES), (0, 0)))


def _unpack_small(packed, like):
    flat = packed.reshape(-1)
    out, at = {}, 0
    for name in SMALL_NAMES:
        n = like[name].size
        out[name] = flat[at:at + n].reshape(like[name].shape)
        at += n + (-n) % LANES
    return out


def kernel(x, meta_tokens, g_pre_mix, w_in, conv_w, conv_b, w_a, b_a, w_x, b_x, lru_lambda, attn_sinks, w_out, g_post_mix, g_pre_ffn, w_ff1, w_ff2, g_post_ffn, loss_target, m_meta_tokens, m_g_pre_mix, m_w_in, m_conv_w, m_conv_b, m_w_a, m_b_a, m_w_x, m_b_x, m_lru_lambda, m_attn_sinks, m_w_out, m_g_post_mix, m_g_pre_ffn, m_w_ff1, m_w_ff2, m_g_post_ffn, v_meta_tokens, v_g_pre_mix, v_w_in, v_conv_w, v_conv_b, v_w_a, v_b_a, v_w_x, v_b_x, v_lru_lambda, v_attn_sinks, v_w_out, v_g_post_mix, v_g_pre_ffn, v_w_ff1, v_w_ff2, v_g_post_ffn):
    weights = dict(meta_tokens=meta_tokens, g_pre_mix=g_pre_mix, w_in=w_in, conv_w=conv_w, conv_b=conv_b, w_a=w_a,
                   b_a=b_a, w_x=w_x, b_x=b_x, lru_lambda=lru_lambda, attn_sinks=attn_sinks, w_out=w_out,
                   g_post_mix=g_post_mix, g_pre_ffn=g_pre_ffn, w_ff1=w_ff1, w_ff2=w_ff2, g_post_ffn=g_post_ffn)
    mom_m = dict(meta_tokens=m_meta_tokens, g_pre_mix=m_g_pre_mix, w_in=m_w_in, conv_w=m_conv_w, conv_b=m_conv_b,
                 w_a=m_w_a, b_a=m_b_a, w_x=m_w_x, b_x=m_b_x, lru_lambda=m_lru_lambda, attn_sinks=m_attn_sinks,
                 w_out=m_w_out, g_post_mix=m_g_post_mix, g_pre_ffn=m_g_pre_ffn, w_ff1=m_w_ff1, w_ff2=m_w_ff2,
                 g_post_ffn=m_g_post_ffn)
    mom_v = dict(meta_tokens=v_meta_tokens, g_pre_mix=v_g_pre_mix, w_in=v_w_in, conv_w=v_conv_w, conv_b=v_conv_b,
                 w_a=v_w_a, b_a=v_b_a, w_x=v_w_x, b_x=v_b_x, lru_lambda=v_lru_lambda, attn_sinks=v_attn_sinks,
                 w_out=v_w_out, g_post_mix=v_g_post_mix, g_pre_ffn=v_g_pre_ffn, w_ff1=v_w_ff1, w_ff2=v_w_ff2,
                 g_post_ffn=v_g_post_ffn)
    order = list(weights)

    (g_win, g_meta, g_cw) = _exchange([w_in[0].astype(BF16), meta_tokens, conv_w[0]], ["gather"] * 3, "gather_first")
    w_in_full = _cols_from_shards(g_win)
    meta_full = _cols_from_shards(g_meta)
    conv_w_full = _cols_from_shards(g_cw)

    head = jnp.concatenate([jnp.zeros((PAD_ROWS, D_MODEL), F32), meta_full], axis=0)
    wa_bd = _block_diag(w_a[0]).astype(BF16)
    wx_bd = _block_diag(w_x[0]).astype(BF16)
    bias = _attn_bias()

    (qkv, zrec, u1, h0), (g_wout,) = _in_proj_fwd(head, x[0], g_pre_mix, w_in_full, [w_out[0].astype(BF16)], ["gather"])
    (attn,), (w1g,) = _attn_fwd(qkv, attn_sinks, bias, [w_ff1[0].astype(BF16)], ["gather"])
    (rec, h_lru, xc), (w2g,) = _rec_fwd(zrec, conv_w_full, conv_b, wa_bd, b_a, wx_bd, b_x, lru_lambda,
                                         [w_ff2[0].astype(BF16)], ["gather"])
    w_out_full = g_wout.reshape(D_MODEL, D_MODEL)
    mix, h1 = _out_proj_fwd(attn, rec, w_out_full, h0, g_post_mix)
    act, u2 = _ffn_up(h1, g_pre_ffn, w1g)
    dy, df, dg_post_ffn, loss_acc = _ffn_down_loss(act, w2g.reshape(D_FF, D_MODEL), h1, loss_target[0], g_post_ffn)

    da1 = _ffn_bwd_act(df, jnp.transpose(w2g, (0, 2, 1)), act)
    dw1g, dw2g = _ffn_bwd_weights(u2, da1, act, df)
    w1t = jnp.transpose(w1g, (0, 2, 1)).reshape(D_FF, D_MODEL)
    (dh1, dg_pre_ffn), (p_w1,) = _ffn_bwd_x(da1, w1t, h1, dy, g_pre_ffn, [dw1g], ["scatter"])
    dattn, drec, dw_out, dg_post_mix = _out_proj_bwd(dh1, mix, g_post_mix, w_out_full.T, attn, rec)
    (dq, dkv, dsinks), (p_w2,) = _attn_bwd(qkv, dattn, attn_sinks, bias, [dw2g], ["scatter"])
    (drz, rec_small, dwa_bd, dwx_bd), (p_wout,) = _rec_bwd(
        drec, zrec, h_lru, xc, conv_w_full, wa_bd, b_a, wx_bd, b_x, lru_lambda,
        [dw_out.reshape(N_DEV, D_MODEL // N_DEV, D_MODEL)], ["scatter"])
    dw_in = _in_proj_bwd_w(u1, dq, dkv, drz)
    small_grads = dict(
        conv_b=rec_small[ROW_CONV_B], w_a=_block_diag_extract(dwa_bd), b_a=rec_small[ROW_B_A],
        w_x=_block_diag_extract(dwx_bd), b_x=rec_small[ROW_B_X], lru_lambda=rec_small[ROW_LAMBDA],
        attn_sinks=dsinks[:, 0], g_post_mix=dg_post_mix, g_pre_ffn=dg_pre_ffn, g_post_ffn=dg_post_ffn)
    (dh0, dg_pre_mix), (p_win, p_cw, p_small) = _in_proj_bwd_x(
        h0, g_pre_mix, dh1, dq, dkv, drz, w_in_full.T,
        [_cols_to_shards(dw_in).astype(BF16), _cols_to_shards(rec_small[0:CONV_WIDTH]), _pack_small(small_grads)],
        ["scatter", "scatter", "gather"])
    p_meta, p_gpm = _exchange([_cols_to_shards(dh0[PAD_ROWS:BLOCK]), dg_pre_mix], ["scatter", "gather"], "exchange_last")

    res = {}
    res["g_pre_mix"] = _adamw(g_pre_mix, m_g_pre_mix, v_g_pre_mix, p_gpm, "adamw_g_pre_mix")
    res["w_in"] = _adamw(w_in[0], m_w_in[0], v_w_in[0], p_win, "adamw_w_in")
    res["w_out"] = _adamw(w_out[0], m_w_out[0], v_w_out[0], p_wout, "adamw_w_out")
    res["w_ff1"] = _adamw(w_ff1[0], m_w_ff1[0], v_w_ff1[0], p_w1, "adamw_w_ff1")
    res["w_ff2"] = _adamw(w_ff2[0], m_w_ff2[0], v_w_ff2[0], p_w2, "adamw_w_ff2")
    res["meta_tokens"] = _adamw(meta_tokens, m_meta_tokens, v_meta_tokens, p_meta, "adamw_meta")
    res["conv_w"] = _adamw(conv_w[0], m_conv_w[0], v_conv_w[0], p_cw, "adamw_conv_w")
    small = _adamw(_pack_small(weights), _pack_small(mom_m), _pack_small(mom_v), p_small, "adamw_small")
    small = [_unpack_small(t, weights) for t in small]
    for name in SMALL_NAMES:
        res[name] = tuple(t[name] for t in small)
    for name in ("w_in", "w_out", "w_ff1", "w_ff2", "conv_w"):
        res[name] = tuple(t[None] for t in res[name])

    loss = lax.psum(loss_acc[0, 0], ("x", "y", "c"))
    grad_x = dh0[BLOCK:][None]
    outs = [loss, grad_x]
    for k in range(4):
        outs += [res[name][k] for name in order]
    return tuple(outs)
```

```python
import jax
import jax.numpy as jnp
from jax import lax
from jax.experimental import pallas as pl
from jax.experimental.pallas import tpu as pltpu

F32 = jnp.float32
BF16 = jnp.bfloat16

D_MODEL = 1024
N_META = 16
HEAD_DIM = 64
ATTN_HEADS = 8
KV_HEADS = 2
GQA_GROUP = ATTN_HEADS // KV_HEADS
ATTN_WIDTH = ATTN_HEADS * HEAD_DIM
KV_WIDTH = KV_HEADS * HEAD_DIM
QKV_WIDTH = ATTN_WIDTH + 2 * KV_WIDTH
LRU_WIDTH = 512
LRU_BLOCKS = 8
LRU_BLOCK = 64
LRU_HALF = 256
LRU_C = 8.0
CONV_WIDTH = 4
BLOCK = 128
PAD_ROWS = BLOCK - N_META
IN_WIDTH = QKV_WIDTH + 2 * LRU_WIDTH
D_FF = 4096
EPS = 1e-6
NEG = -1e30
N_DEV = 8
FF_CHUNK = D_FF // N_DEV
SUBLANES = 8
LANES = 128

ADAM_LR = 0.001
ADAM_B1 = 0.9
ADAM_B2 = 0.999
ADAM_EPS = 1e-08
ADAM_WD = 0.01
ADAM_STEP = 10

VMEM_LIMIT = 56 * 1024 * 1024


def _row_tile(rows):
    for t in (640, 512, 256, 128):
        if rows % t == 0:
            return t
    raise ValueError(rows)


def _big_tile(rows):
    for t in (1664, 1024, 512, 256, 128):
        if rows % t == 0:
            return t
    raise ValueError(rows)


def _rec_tile(rows):
    for t in (320, 256, 128):
        if rows % t == 0:
            return t
    raise ValueError(rows)


def _params(semantics):
    return pltpu.CompilerParams(dimension_semantics=semantics, vmem_limit_bytes=VMEM_LIMIT)


def _mm(a, b):
    return lax.dot_general(a, b, (((1,), (0,)), ((), ())), preferred_element_type=F32)


def _mm_nt(a, b):
    return lax.dot_general(a, b, (((1,), (1,)), ((), ())), preferred_element_type=F32)


def _mm_tn(a, b):
    return lax.dot_general(a, b, (((0,), (0,)), ((), ())), preferred_element_type=F32)


def _rms_fwd(x, g):
    rstd = lax.rsqrt(jnp.mean(x * x, axis=-1, keepdims=True) + EPS)
    xhat = x * rstd
    return xhat * g, xhat, rstd


def _rms_bwd(dy, xhat, rstd, g):
    dyg = dy * g
    c = jnp.mean(dyg * xhat, axis=-1, keepdims=True)
    dx = rstd * (dyg - xhat * c)
    dg = jnp.sum(dy * xhat, axis=0, keepdims=True)
    return dx, dg


def _sigmoid(x):
    return 0.5 * jnp.tanh(0.5 * x) + 0.5


def _log1p(x):
    u = 1.0 + x
    return jnp.where(u == 1.0, x, jnp.log(u) * x / (u - 1.0))


def _one_minus_sq_exp(x, ex):
    return -jnp.tanh(x) * (1.0 + ex * ex)


def _sqrt_pos(y):
    r = lax.rsqrt(y)
    r = r * (1.5 - 0.5 * y * r * r)
    return jnp.where(y > 0.0, y * r, 0.0), r


def _softplus(x):
    return jnp.maximum(x, 0.0) + _log1p(jnp.exp(-jnp.abs(x)))


GELU_C = 0.7978845608028654
GELU_K = 0.044715


def _gelu(x):
    t = jnp.tanh(GELU_C * (x + GELU_K * x * x * x))
    return 0.5 * x * (1.0 + t), t


def _gelu_grad(x, t):
    return 0.5 * (1.0 + t) + 0.5 * x * (1.0 - t * t) * GELU_C * (1.0 + 3.0 * GELU_K * x * x)


def _full(shape):
    return pl.BlockSpec(shape, lambda *_: (0,) * len(shape))


def _resident(shape):
    return pl.BlockSpec(shape, lambda *_: (0,) * len(shape), pipeline_mode=pl.Buffered(1))


def _exchange_copies(ins, outs, sems, modes):
    send_sems, recv_sems, local_sems = sems
    x, y, c = lax.axis_index("x"), lax.axis_index("y"), lax.axis_index("c")
    me = 4 * x + 2 * y + c

    def block(a, dev):
        return ins[a] if modes[a] == "gather" else ins[a].at[dev]

    local = [pltpu.make_async_copy(block(a, me), outs[a].at[me], local_sems.at[a]) for a in range(len(ins))]
    sends, recvs = [], []
    for a in range(len(ins)):
        for k in range(N_DEV - 1):
            bits = k + 1
            px = jnp.bitwise_xor(x, (bits >> 2) & 1)
            py = jnp.bitwise_xor(y, (bits >> 1) & 1)
            pc = jnp.bitwise_xor(c, bits & 1)
            peer = 4 * px + 2 * py + pc
            common = dict(src_ref=block(a, peer), send_sem=send_sems.at[a, k], recv_sem=recv_sems.at[a, k],
                          device_id=(px, py, pc), device_id_type=pl.DeviceIdType.MESH)
            sends.append(pltpu.make_async_remote_copy(dst_ref=outs[a].at[me], **common))
            recvs.append(pltpu.make_async_remote_copy(dst_ref=outs[a].at[peer], **common))
    return local, sends, recvs


def _exchange_start(ins, outs, sems, modes):
    local, sends, _ = _exchange_copies(ins, outs, sems, modes)
    for cp in local + sends:
        cp.start()


def _exchange_wait(ins, outs, sems, modes):
    local, sends, recvs = _exchange_copies(ins, outs, sems, modes)
    for cp in recvs:
        cp.wait_recv()
    for cp in sends:
        cp.wait_send()
    for cp in local:
        cp.wait()


def _exchange_shapes(arrays, modes):
    return [jax.ShapeDtypeStruct((N_DEV,) + a.shape if mode == "gather" else a.shape, a.dtype)
            for a, mode in zip(arrays, modes)]


def _exchange_sems(na):
    return [pltpu.SemaphoreType.DMA((na, N_DEV - 1)), pltpu.SemaphoreType.DMA((na, N_DEV - 1)),
            pltpu.SemaphoreType.DMA((na,))]


ANY_SPACE = pl.BlockSpec(memory_space=pl.ANY)


def _exchange(arrays, modes, name):
    na = len(arrays)

    def body(*refs):
        ins, outs, sems = refs[:na], refs[na:2 * na], refs[2 * na:]
        _exchange_start(ins, outs, sems, modes)
        _exchange_wait(ins, outs, sems, modes)

    return pl.pallas_call(
        body, name=name, out_shape=_exchange_shapes(arrays, modes),
        in_specs=[ANY_SPACE] * na, out_specs=[ANY_SPACE] * na, scratch_shapes=_exchange_sems(na),
        compiler_params=pltpu.CompilerParams(has_side_effects=True),
    )(*arrays)


def _hosting_call(body, name, steps, in_specs, out_specs, out_shape, scratch_shapes, args, arrays, modes):
    n_in, n_out, n_scr, na = len(in_specs), len(out_specs), len(scratch_shapes), len(arrays)

    def hosting_body(*refs):
        cuts = [0]
        for n in (n_in, na, n_out, na, n_scr, 3):
            cuts.append(cuts[-1] + n)
        ins, x_ins, outs, x_outs, scr, sems = (refs[cuts[p]:cuts[p + 1]] for p in range(6))
        step = pl.program_id(0)

        @pl.when(step == 0)
        def _():
            _exchange_start(x_ins, x_outs, sems, modes)

        body(*ins, *outs, *scr)

        @pl.when(step == steps - 1)
        def _():
            _exchange_wait(x_ins, x_outs, sems, modes)

    res = pl.pallas_call(
        hosting_body, name=name, grid=(steps,),
        in_specs=list(in_specs) + [ANY_SPACE] * na, out_specs=list(out_specs) + [ANY_SPACE] * na,
        out_shape=list(out_shape) + _exchange_shapes(arrays, modes),
        scratch_shapes=list(scratch_shapes) + _exchange_sems(na),
        compiler_params=_params(("arbitrary",)),
    )(*args, *arrays)
    return res[:n_out], res[n_out:]


def _frame_rows(src_hbm, buf, sem, i, steps, tm):
    def first():
        return pltpu.make_async_copy(src_hbm.at[pl.ds(0, tm - BLOCK)], buf.at[0, pl.ds(BLOCK, tm - BLOCK)], sem.at[0])

    def later(t, slot):
        return pltpu.make_async_copy(src_hbm.at[pl.ds(pl.multiple_of(t * tm - BLOCK, BLOCK), tm)], buf.at[slot], sem.at[slot])

    slot = i % 2

    @pl.when(i == 0)
    def _():
        first().start()

    @pl.when(i + 1 < steps)
    def _():
        later(i + 1, 1 - slot).start()

    @pl.when(i == 0)
    def _():
        first().wait()

    @pl.when(i > 0)
    def _():
        later(i, slot).wait()

    return slot


def _frame_scratch(tm):
    return [pltpu.VMEM((2, tm, D_MODEL), F32), pltpu.SemaphoreType.DMA((2,))]


def _in_proj_fwd(head, x, g1, w_in, carried, modes):
    rows = BLOCK + x.shape[0]
    tm = _row_tile(rows)
    steps = rows // tm

    def body(head_ref, g_ref, w_ref, x_hbm, qkv_ref, zrec_ref, u_ref, h_ref, buf, sem):
        i = pl.program_id(0)
        slot = _frame_rows(x_hbm, buf, sem, i, steps, tm)

        @pl.when(i == 0)
        def _():
            buf[0, 0:BLOCK, :] = head_ref[...]

        h = buf[slot]
        h_ref[...] = h
        u, _, _ = _rms_fwd(h, g_ref[...])
        u = u.astype(BF16)
        u_ref[...] = u
        z = _mm(u, w_ref[...])
        qkv_ref[...] = z[:, :QKV_WIDTH].astype(BF16)
        zrec_ref[...] = z[:, QKV_WIDTH:]

    wide = pl.BlockSpec((tm, D_MODEL), lambda i: (i, 0))
    return _hosting_call(
        body, "in_proj_fwd", steps,
        [_full((BLOCK, D_MODEL)), _full((1, D_MODEL)), _resident((D_MODEL, IN_WIDTH)), ANY_SPACE],
        [pl.BlockSpec((tm, QKV_WIDTH), lambda i: (i, 0)), pl.BlockSpec((tm, 2 * LRU_WIDTH), lambda i: (i, 0)), wide, wide],
        [jax.ShapeDtypeStruct((rows, QKV_WIDTH), BF16), jax.ShapeDtypeStruct((rows, 2 * LRU_WIDTH), F32),
         jax.ShapeDtypeStruct((rows, D_MODEL), BF16), jax.ShapeDtypeStruct((rows, D_MODEL), F32)],
        _frame_scratch(tm), (head, g1, w_in, x), carried, modes)


N_BIAS = 3


def _attn_bias():
    key = lax.broadcasted_iota(jnp.int32, (2 * BLOCK, GQA_GROUP * BLOCK), 0)
    r = lax.broadcasted_iota(jnp.int32, (2 * BLOCK, GQA_GROUP * BLOCK), 1) & (BLOCK - 1)
    band = (key > r) & (key <= r + BLOCK)
    out = [jnp.where(band & ((n - 1) * BLOCK + key >= PAD_ROWS), 0.0, NEG) for n in range(N_BIAS)]
    return jnp.stack(out).astype(F32)


def _attn_probs(k2, q4, bias, sink_row):
    s = _mm_nt(k2, q4) * (HEAD_DIM ** -0.5) + bias
    m = jnp.maximum(jnp.max(s, axis=0, keepdims=True), sink_row)
    p = jnp.exp(s - m)
    es = jnp.exp(sink_row - m)
    inv = 1.0 / (jnp.sum(p, axis=0, keepdims=True) + es)
    return p * inv, es * inv


def _heads(ref, first, count):
    return jnp.concatenate([ref[:, (first + g) * HEAD_DIM:(first + g + 1) * HEAD_DIM] for g in range(count)], axis=0)


def _sink_row(sink_ref, kv):
    g = lax.broadcasted_iota(jnp.int32, (1, GQA_GROUP * BLOCK), 1) // BLOCK
    row = jnp.full((1, GQA_GROUP * BLOCK), sink_ref[0, kv * GQA_GROUP], F32)
    for i in range(1, GQA_GROUP):
        row = jnp.where(g == i, sink_ref[0, kv * GQA_GROUP + i], row)
    return row


def _from_head_major(pieces):
    return jnp.concatenate(pieces, axis=0).T


def _attn_fwd(qkv, sinks, bias, carried, modes):
    rows = qkv.shape[0]
    nb = rows // BLOCK
    k_col, v_col = ATTN_WIDTH // KV_WIDTH, ATTN_WIDTH // KV_WIDTH + 1

    def body(sink_ref, bias_ref, q_ref, kp_ref, kc_ref, vp_ref, vc_ref, o_ref):
        bias_t = bias_ref[...]
        pieces = []
        for kv in range(KV_HEADS):
            sl = slice(kv * HEAD_DIM, (kv + 1) * HEAD_DIM)
            k2 = jnp.concatenate([kp_ref[:, sl], kc_ref[:, sl]], axis=0)
            v2 = jnp.concatenate([vp_ref[:, sl], vc_ref[:, sl]], axis=0)
            q4 = _heads(q_ref, kv * GQA_GROUP, GQA_GROUP)
            pn, _ = _attn_probs(k2, q4, bias_t, _sink_row(sink_ref, kv))
            ot = _mm_tn(v2, pn.astype(BF16))
            pieces += [ot[:, g * BLOCK:(g + 1) * BLOCK] for g in range(GQA_GROUP)]
        o_ref[...] = _from_head_major(pieces).astype(BF16)

    prev = lambda n: jnp.maximum(n - 1, 0)
    return _hosting_call(
        body, "attn_fwd", nb,
        [pl.BlockSpec(memory_space=pltpu.SMEM),
         pl.BlockSpec((None, 2 * BLOCK, GQA_GROUP * BLOCK), lambda n: (jnp.minimum(n, N_BIAS - 1), 0, 0)),
         pl.BlockSpec((BLOCK, ATTN_WIDTH), lambda n: (n, 0)),
         pl.BlockSpec((BLOCK, KV_WIDTH), lambda n: (prev(n), k_col)),
         pl.BlockSpec((BLOCK, KV_WIDTH), lambda n: (n, k_col)),
         pl.BlockSpec((BLOCK, KV_WIDTH), lambda n: (prev(n), v_col)),
         pl.BlockSpec((BLOCK, KV_WIDTH), lambda n: (n, v_col))],
        [pl.BlockSpec((BLOCK, ATTN_WIDTH), lambda n: (n, 0))],
        [jax.ShapeDtypeStruct((rows, ATTN_WIDTH), BF16)],
        [], (sinks, bias, qkv, qkv, qkv, qkv, qkv), carried, modes)


def _conv_taps(xbuf, tm):
    return [xbuf[pl.ds(SUBLANES - (CONV_WIDTH - 1 - j), tm), :] for j in range(CONV_WIDTH)]


def _lru_gates(xc, wa_ref, ba_ref, wx_ref, bx_ref, lam_ref):
    halves = [xc[:, h * LRU_HALF:(h + 1) * LRU_HALF].astype(BF16) for h in range(2)]
    gate_r = jnp.concatenate([_mm(halves[h], wa_ref[h]) for h in range(2)], axis=1) + ba_ref[...]
    gate_i = jnp.concatenate([_mm(halves[h], wx_ref[h]) for h in range(2)], axis=1) + bx_ref[...]
    r = _sigmoid(gate_r)
    ig = _sigmoid(gate_i)
    log_a = (-LRU_C) * r * _softplus(-lam_ref[...])
    a = jnp.exp(log_a)
    mult, inv_mult = _sqrt_pos(_one_minus_sq_exp(log_a, a))
    return halves, r, ig, a, mult, inv_mult


def _scan_tile(a_ref, u_ref, out_ref, carry, tm, reverse):
    row = lax.broadcasted_iota(jnp.int32, (SUBLANES, LRU_WIDTH), 0)
    groups = tm // SUBLANES

    def step(j, prev):
        jj = groups - 1 - j if reverse else j
        o = pl.multiple_of(jj * SUBLANES, SUBLANES)
        a = a_ref[pl.ds(o, SUBLANES), :]
        u = u_ref[pl.ds(o, SUBLANES), :]
        for s in (1, 2, 4):
            shift = SUBLANES - s if reverse else s
            keep = (row < SUBLANES - s) if reverse else (row >= s)
            u = jnp.where(keep, a * pltpu.roll(u, shift, 0) + u, u)
            a = jnp.where(keep, a * pltpu.roll(a, shift, 0), a)
        out = a * prev + u
        out_ref[pl.ds(o, SUBLANES), :] = out
        return out[0:1, :] if reverse else out[SUBLANES - 1:SUBLANES, :]

    return lax.fori_loop(0, groups, step, carry)


def _rec_fwd(zrec, conv_w, conv_b, wa_bd, b_a, wx_bd, b_x, lam, carried, modes):
    rows = zrec.shape[0]
    tm = _rec_tile(rows)

    def body(xr_ref, yr_ref, cw_ref, cb_ref, wa_ref, ba_ref, wx_ref, bx_ref, lam_ref, rec_ref, h_ref, xc_ref,
             xbuf, a_s, u_s, carry):
        i = pl.program_id(0)

        @pl.when(i == 0)
        def _():
            xbuf[0:SUBLANES, :] = jnp.zeros((SUBLANES, LRU_WIDTH), F32)
            carry[...] = jnp.zeros_like(carry)

        @pl.when(i > 0)
        def _():
            xbuf[0:SUBLANES, :] = xbuf[tm:tm + SUBLANES, :]

        xbuf[SUBLANES:SUBLANES + tm, :] = xr_ref[...]
        taps = _conv_taps(xbuf, tm)
        xc = cb_ref[...] + sum(cw_ref[j:j + 1, :] * taps[j] for j in range(CONV_WIDTH))
        xc_ref[...] = xc
        _, r, ig, a, mult, _ = _lru_gates(xc, wa_ref, ba_ref, wx_ref, bx_ref, lam_ref)
        grow = i * tm + lax.broadcasted_iota(jnp.int32, (tm, LRU_WIDTH), 0)
        a_s[...] = a
        u_s[...] = jnp.where(grow >= PAD_ROWS, mult * (ig * xc), 0.0)
        carry[0:1, :] = _scan_tile(a_s, u_s, h_ref, carry[0:1, :], tm, reverse=False)
        gel, _ = _gelu(yr_ref[...])
        rec_ref[...] = (gel * h_ref[...]).astype(BF16)

    vec = _full((1, LRU_WIDTH))
    bd = _full((2, LRU_HALF, LRU_HALF))
    return _hosting_call(
        body, "rec_fwd", rows // tm,
        [pl.BlockSpec((tm, LRU_WIDTH), lambda i: (i, 0)), pl.BlockSpec((tm, LRU_WIDTH), lambda i: (i, 1)),
         _full((CONV_WIDTH, LRU_WIDTH)), vec, bd, vec, bd, vec, vec],
        [pl.BlockSpec((tm, LRU_WIDTH), lambda i: (i, 0))] * 3,
        [jax.ShapeDtypeStruct((rows, LRU_WIDTH), BF16), jax.ShapeDtypeStruct((rows, LRU_WIDTH), F32),
         jax.ShapeDtypeStruct((rows, LRU_WIDTH), F32)],
        [pltpu.VMEM((tm + SUBLANES, LRU_WIDTH), F32), pltpu.VMEM((tm, LRU_WIDTH), F32),
         pltpu.VMEM((tm, LRU_WIDTH), F32), pltpu.VMEM((SUBLANES, LRU_WIDTH), F32)],
        (zrec, zrec, conv_w, conv_b, wa_bd, b_a, wx_bd, b_x, lam), carried, modes)


def _out_proj_fwd(attn, rec, w_out, h0, g2):
    rows = h0.shape[0]
    tm = _row_tile(rows)

    def body(attn_ref, rec_ref, w_ref, h_ref, g_ref, mix_ref, h1_ref):
        mix = _mm(attn_ref[...], w_ref[0:ATTN_WIDTH, :]) + _mm(rec_ref[...], w_ref[ATTN_WIDTH:, :])
        y, _, _ = _rms_fwd(mix, g_ref[...])
        mix_ref[...] = mix
        h1_ref[...] = h_ref[...] + y

    half = pl.BlockSpec((tm, ATTN_WIDTH), lambda i: (i, 0))
    wide = pl.BlockSpec((tm, D_MODEL), lambda i: (i, 0))
    return pl.pallas_call(
        body, name="out_proj_fwd", grid=(rows // tm,),
        in_specs=[half, half, _resident((D_MODEL, D_MODEL)), wide, _full((1, D_MODEL))],
        out_specs=[wide, wide],
        out_shape=[jax.ShapeDtypeStruct((rows, D_MODEL), F32)] * 2,
        compiler_params=_params(("parallel",)),
    )(attn, rec, w_out, h0, g2)


FF_COLS = 1024


FF_HALF = FF_CHUNK // 2


def _ffn_up(h1, g3, w1_halves, carried, modes):
    rows = h1.shape[0]
    tm = _row_tile(rows)

    def body(h_ref, g_ref, wa_ref, wb_ref, act_ref, u_ref):
        u, _, _ = _rms_fwd(h_ref[...], g_ref[...])
        u = u.astype(BF16)
        u_ref[...] = u
        for d in range(N_DEV):
            for half, w_ref in enumerate((wa_ref, wb_ref)):
                c = d * FF_CHUNK + half * FF_HALF
                a1 = jnp.maximum(_mm(u, w_ref[d]), 0.0)
                act_ref[:, c:c + FF_HALF] = (a1 * a1).astype(BF16)

    wide = pl.BlockSpec((tm, D_MODEL), lambda i: (i, 0))
    return _hosting_call(
        body, "ffn_up", rows // tm,
        [wide, _full((1, D_MODEL))] + [_resident((N_DEV, D_MODEL, FF_HALF))] * 2,
        [pl.BlockSpec((tm, D_FF), lambda i: (i, 0)), wide],
        [jax.ShapeDtypeStruct((rows, D_FF), BF16), jax.ShapeDtypeStruct((rows, D_MODEL), BF16)],
        [], (h1, g3, *w1_halves), carried, modes)


def _ffn_down_loss(act, w2, h1, target, g4):
    rows = h1.shape[0]
    tm = _row_tile(rows)
    steps = rows // tm

    def body(act_ref, w_ref, h_ref, g_ref, t_hbm, dy_ref, df_ref, dg_ref, loss_ref, buf, sem):
        i = pl.program_id(0)
        slot = _frame_rows(t_hbm, buf, sem, i, steps, tm)

        @pl.when(i == 0)
        def _():
            dg_ref[...] = jnp.zeros_like(dg_ref)
            loss_ref[...] = jnp.zeros_like(loss_ref)
            buf[0, 0:BLOCK, :] = jnp.zeros((BLOCK, D_MODEL), F32)

        g = g_ref[...]
        y, fhat, rstd = _rms_fwd(_mm(act_ref[...], w_ref[...]), g)
        grow = i * tm + lax.broadcasted_iota(jnp.int32, (tm, D_MODEL), 0)
        err = jnp.where(grow >= BLOCK, h_ref[...] + y - buf[slot], 0.0)
        loss_ref[...] += (0.5 / D_MODEL) * jnp.sum(err * err)
        dy = err * (1.0 / D_MODEL)
        df, dg = _rms_bwd(dy, fhat, rstd, g)
        dy_ref[...] = dy
        df_ref[...] = df.astype(BF16)
        dg_ref[...] += dg

    wide = pl.BlockSpec((tm, D_MODEL), lambda i: (i, 0))
    return pl.pallas_call(
        body, name="ffn_down_loss", grid=(steps,),
        in_specs=[pl.BlockSpec((tm, D_FF), lambda i: (i, 0)), _resident((D_FF, D_MODEL)), wide, _full((1, D_MODEL)), ANY_SPACE],
        out_specs=[wide, wide, _full((1, D_MODEL)), _full((SUBLANES, LANES))],
        out_shape=[jax.ShapeDtypeStruct((rows, D_MODEL), F32), jax.ShapeDtypeStruct((rows, D_MODEL), BF16),
                   jax.ShapeDtypeStruct((1, D_MODEL), F32), jax.ShapeDtypeStruct((SUBLANES, LANES), F32)],
        scratch_shapes=_frame_scratch(tm),
        compiler_params=_params(("arbitrary",)),
    )(act, w2, h1, g4, target)


def _ffn_bwd_act(df, w2tg, act):
    rows = df.shape[0]
    tm = _row_tile(rows)

    def body(df_ref, w_ref, act_ref, da_ref):
        df_t = df_ref[...]
        for d in range(N_DEV):
            cols = slice(d * FF_CHUNK, (d + 1) * FF_CHUNK)
            dact = _mm(df_t, w_ref[d])
            relu_a1, _ = _sqrt_pos(act_ref[:, cols].astype(F32))
            da_ref[:, cols] = (dact * (2.0 * relu_a1)).astype(BF16)

    hidden = pl.BlockSpec((tm, D_FF), lambda i: (i, 0))
    return pl.pallas_call(
        body, name="ffn_bwd_act", grid=(rows // tm,),
        in_specs=[pl.BlockSpec((tm, D_MODEL), lambda i: (i, 0)), _resident((N_DEV, D_MODEL, FF_CHUNK)), hidden],
        out_specs=hidden,
        out_shape=jax.ShapeDtypeStruct((rows, D_FF), BF16),
        compiler_params=_params(("parallel",)),
    )(df, w2tg, act)


def _ffn_bwd_x(da, w1t, h1, dy, g3, carried, modes):
    rows = h1.shape[0]
    tm = _row_tile(rows)

    def body(da_ref, w_ref, h_ref, dy_ref, g_ref, dh_ref, dg_ref):
        @pl.when(pl.program_id(0) == 0)
        def _():
            dg_ref[...] = jnp.zeros_like(dg_ref)

        g = g_ref[...]
        _, xhat, rstd = _rms_fwd(h_ref[...], g)
        dx, dg = _rms_bwd(_mm(da_ref[...], w_ref[...]), xhat, rstd, g)
        dh_ref[...] = dy_ref[...] + dx
        dg_ref[...] += dg

    wide = pl.BlockSpec((tm, D_MODEL), lambda i: (i, 0))
    return _hosting_call(
        body, "ffn_bwd_x", rows // tm,
        [pl.BlockSpec((tm, D_FF), lambda i: (i, 0)), _resident((D_FF, D_MODEL)), wide, wide, _full((1, D_MODEL))],
        [wide, _full((1, D_MODEL))],
        [jax.ShapeDtypeStruct((rows, D_MODEL), F32), jax.ShapeDtypeStruct((1, D_MODEL), F32)],
        [], (da, w1t, h1, dy, g3), carried, modes)


def _ffn_bwd_weights(u2, da, act, df):
    rows = u2.shape[0]
    tb = _big_tile(rows)
    steps = rows // tb
    per = FF_COLS // FF_CHUNK

    def body(u_ref, da_ref, act_ref, df_ref, dw1a_ref, dw1b_ref, dw2_ref, acc1, acc2):
        i = pl.program_id(1)

        @pl.when(i == 0)
        def _():
            acc1[...] = jnp.zeros_like(acc1)
            acc2[...] = jnp.zeros_like(acc2)

        acc1[...] += _mm_tn(u_ref[...], da_ref[...])
        acc2[...] += _mm_tn(act_ref[...], df_ref[...])

        @pl.when(i == steps - 1)
        def _():
            for p in range(per):
                c = p * FF_CHUNK
                dw1a_ref[p] = acc1[:, c:c + FF_HALF].astype(BF16)
                dw1b_ref[p] = acc1[:, c + FF_HALF:c + FF_CHUNK].astype(BF16)
                dw2_ref[p] = acc2[c:c + FF_CHUNK, :].astype(BF16)

    wide = pl.BlockSpec((tb, D_MODEL), lambda j, i: (i, 0))
    chunk = pl.BlockSpec((tb, FF_COLS), lambda j, i: (i, j))
    half = pl.BlockSpec((per, D_MODEL, FF_HALF), lambda j, i: (j, 0, 0))
    return pl.pallas_call(
        body, name="ffn_bwd_weights", grid=(D_FF // FF_COLS, steps),
        in_specs=[wide, chunk, chunk, wide],
        out_specs=[half, half, pl.BlockSpec((per, FF_CHUNK, D_MODEL), lambda j, i: (j, 0, 0))],
        out_shape=[jax.ShapeDtypeStruct((N_DEV, D_MODEL, FF_HALF), BF16)] * 2
                  + [jax.ShapeDtypeStruct((N_DEV, FF_CHUNK, D_MODEL), BF16)],
        scratch_shapes=[pltpu.VMEM((D_MODEL, FF_COLS), F32), pltpu.VMEM((FF_COLS, D_MODEL), F32)],
        compiler_params=_params(("parallel", "arbitrary")),
    )(u2, da, act, df)


def _out_proj_bwd(dh1, mix, g2, w_out_t, attn, rec, carried, modes):
    rows = dh1.shape[0]
    tm = _row_tile(rows)
    steps = rows // tm

    def body(dh_ref, mix_ref, g_ref, w_ref, attn_ref, rec_ref, dattn_ref, drec_ref, dw_ref, dg_ref, acc):
        i = pl.program_id(0)

        @pl.when(i == 0)
        def _():
            acc[...] = jnp.zeros_like(acc)
            dg_ref[...] = jnp.zeros_like(dg_ref)

        g = g_ref[...]
        _, xhat, rstd = _rms_fwd(mix_ref[...], g)
        dmix, dg = _rms_bwd(dh_ref[...], xhat, rstd, g)
        dmix = dmix.astype(BF16)
        dg_ref[...] += dg
        din = _mm(dmix, w_ref[...])
        dattn_ref[...] = din[:, :ATTN_WIDTH].astype(BF16)
        drec_ref[...] = din[:, ATTN_WIDTH:]
        acc[0:ATTN_WIDTH, :] += _mm_tn(attn_ref[...], dmix)
        acc[ATTN_WIDTH:, :] += _mm_tn(rec_ref[...], dmix)

        @pl.when(i == steps - 1)
        def _():
            dw_ref[...] = acc[...].astype(BF16)

    half = pl.BlockSpec((tm, ATTN_WIDTH), lambda i: (i, 0))
    wide = pl.BlockSpec((tm, D_MODEL), lambda i: (i, 0))
    return _hosting_call(
        body, "out_proj_bwd", steps,
        [wide, wide, _full((1, D_MODEL)), _resident((D_MODEL, D_MODEL)), half, half],
        [half, half, _full((D_MODEL, D_MODEL)), _full((1, D_MODEL))],
        [jax.ShapeDtypeStruct((rows, ATTN_WIDTH), BF16), jax.ShapeDtypeStruct((rows, LRU_WIDTH), F32),
         jax.ShapeDtypeStruct((D_MODEL, D_MODEL), BF16), jax.ShapeDtypeStruct((1, D_MODEL), F32)],
        [pltpu.VMEM((D_MODEL, D_MODEL), F32)],
        (dh1, mix, g2, w_out_t, attn, rec), carried, modes)


def _attn_bwd(qkv, dattn, sinks, bias, carried, modes):
    rows = qkv.shape[0]
    nb = rows // BLOCK
    k_col, v_col = ATTN_WIDTH // KV_WIDTH, ATTN_WIDTH // KV_WIDTH + 1

    def body(sink_ref, bias_ref, q_ref, do_ref, kp_ref, kc_ref, vp_ref, vc_ref, dq_ref, dkv_ref, dsink_ref, dk_c, dv_c):
        n = pl.program_id(0)

        @pl.when(n == 0)
        def _():
            dk_c[...] = jnp.zeros_like(dk_c)
            dv_c[...] = jnp.zeros_like(dv_c)
            dsink_ref[...] = jnp.zeros_like(dsink_ref)

        @pl.when(n < nb)
        def _():
            bias_t = bias_ref[...]
            dq_parts, dk_parts, dv_parts, dsink_rows = [], [], [], []
            for kv in range(KV_HEADS):
                sl = slice(kv * HEAD_DIM, (kv + 1) * HEAD_DIM)
                k2 = jnp.concatenate([kp_ref[:, sl], kc_ref[:, sl]], axis=0)
                v2 = jnp.concatenate([vp_ref[:, sl], vc_ref[:, sl]], axis=0)
                q4 = _heads(q_ref, kv * GQA_GROUP, GQA_GROUP)
                do4 = _heads(do_ref, kv * GQA_GROUP, GQA_GROUP)
                pn, psink = _attn_probs(k2, q4, bias_t, _sink_row(sink_ref, kv))
                dpn = _mm_nt(v2, do4)
                delta = jnp.sum(pn * dpn, axis=0, keepdims=True)
                ds = ((pn * (dpn - delta)) * (HEAD_DIM ** -0.5)).astype(BF16)
                dqt = _mm_tn(k2, ds)
                dq_parts += [dqt[:, g * BLOCK:(g + 1) * BLOCK] for g in range(GQA_GROUP)]
                dk_parts.append(_mm(ds, q4))
                dv_parts.append(_mm(pn.astype(BF16), do4))
                sd = psink * delta
                for g in range(GQA_GROUP):
                    dsink_rows.append(jnp.full((1, LANES), -jnp.sum(sd[:, g * BLOCK:(g + 1) * BLOCK]), F32))
            dq_ref[...] = _from_head_major(dq_parts).astype(BF16)
            dsink_ref[...] += jnp.concatenate(dsink_rows, axis=0)
            dk2 = jnp.concatenate(dk_parts, axis=1)
            dv2 = jnp.concatenate(dv_parts, axis=1)
            dkv_ref[:, 0:KV_WIDTH] = (dk_c[...] + dk2[0:BLOCK]).astype(BF16)
            dkv_ref[:, KV_WIDTH:] = (dv_c[...] + dv2[0:BLOCK]).astype(BF16)
            dk_c[...] = dk2[BLOCK:]
            dv_c[...] = dv2[BLOCK:]

        @pl.when(n == nb)
        def _():
            dkv_ref[:, 0:KV_WIDTH] = dk_c[...].astype(BF16)
            dkv_ref[:, KV_WIDTH:] = dv_c[...].astype(BF16)

    cur = lambda n: jnp.minimum(n, nb - 1)
    prev = lambda n: jnp.maximum(jnp.minimum(n, nb - 1) - 1, 0)
    return _hosting_call(
        body, "attn_bwd", nb + 1,
        [pl.BlockSpec(memory_space=pltpu.SMEM),
         pl.BlockSpec((None, 2 * BLOCK, GQA_GROUP * BLOCK), lambda n: (jnp.minimum(n, N_BIAS - 1), 0, 0)),
         pl.BlockSpec((BLOCK, ATTN_WIDTH), lambda n: (cur(n), 0)),
         pl.BlockSpec((BLOCK, ATTN_WIDTH), lambda n: (cur(n), 0)),
         pl.BlockSpec((BLOCK, KV_WIDTH), lambda n: (prev(n), k_col)),
         pl.BlockSpec((BLOCK, KV_WIDTH), lambda n: (cur(n), k_col)),
         pl.BlockSpec((BLOCK, KV_WIDTH), lambda n: (prev(n), v_col)),
         pl.BlockSpec((BLOCK, KV_WIDTH), lambda n: (cur(n), v_col))],
        [pl.BlockSpec((BLOCK, ATTN_WIDTH), lambda n: (cur(n), 0)),
         pl.BlockSpec((BLOCK, 2 * KV_WIDTH), lambda n: (jnp.maximum(n - 1, 0), 0)),
         _full((ATTN_HEADS, LANES))],
        [jax.ShapeDtypeStruct((rows, ATTN_WIDTH), BF16), jax.ShapeDtypeStruct((rows, 2 * KV_WIDTH), BF16),
         jax.ShapeDtypeStruct((ATTN_HEADS, LANES), F32)],
        [pltpu.VMEM((BLOCK, KV_WIDTH), F32), pltpu.VMEM((BLOCK, KV_WIDTH), F32)],
        (sinks, bias, qkv, dattn, qkv, qkv, qkv, qkv), carried, modes)


ROW_CONV_B, ROW_B_A, ROW_B_X, ROW_LAMBDA = 4, 5, 6, 7


def _rec_bwd(drec, zrec, h, xc_all, conv_w, wa_bd, b_a, wx_bd, b_x, lam, carried, modes):
    rows = zrec.shape[0]
    tm = _rec_tile(rows)
    nt = rows // tm
    per = tm // SUBLANES

    def body(drec_ref, xr_ref, yr_ref, h_ref, xc_ref, hhalo_ref, cw_ref, wa_ref, ba_ref, wx_ref, bx_ref,
             lam_ref, drz_ref, small_ref, dwa_ref, dwx_ref, hbuf, abuf, u_s, g_s, dbuf, carry):
        s = pl.program_id(0)
        i = nt - 1 - s

        @pl.when(s == 0)
        def _():
            small_ref[...] = jnp.zeros_like(small_ref)
            dwa_ref[...] = jnp.zeros_like(dwa_ref)
            dwx_ref[...] = jnp.zeros_like(dwx_ref)
            carry[...] = jnp.zeros_like(carry)
            abuf[tm:tm + SUBLANES, :] = jnp.zeros((SUBLANES, LRU_WIDTH), F32)
            dbuf[tm:tm + SUBLANES, :] = jnp.zeros((SUBLANES, LRU_WIDTH), F32)

        hbuf[0:SUBLANES, :] = jnp.where(i == 0, 0.0, hhalo_ref[...])
        hbuf[SUBLANES:SUBLANES + tm, :] = h_ref[...]

        xc = xc_ref[...]
        halves, r, ig, a, mult, inv_mult = _lru_gates(xc, wa_ref, ba_ref, wx_ref, bx_ref, lam_ref)

        yr = yr_ref[...]
        gel, t = _gelu(yr)
        drec_t = drec_ref[...]
        dyr = drec_t * h_ref[...] * _gelu_grad(yr, t)

        abuf[0:tm, :] = a
        u_s[...] = drec_t * gel
        a_next = abuf[pl.ds(1, tm), :]
        abuf[0:tm, :] = a_next
        carry[0:1, :] = _scan_tile(abuf, u_s, g_s, carry[0:1, :], tm, reverse=True)
        abuf[tm:tm + 1, :] = a[0:1, :]
        g = g_s[...]

        grow = i * tm + lax.broadcasted_iota(jnp.int32, (tm, LRU_WIDTH), 0)
        du = jnp.where(grow >= PAD_ROWS, g, 0.0)
        da = g * hbuf[pl.ds(SUBLANES - 1, tm), :]
        dmult = du * (ig * xc)
        dig = du * (mult * xc)
        dxc = du * (mult * ig)
        dlog_a = da * a - dmult * (a * a * inv_mult)
        sp = _softplus(-lam_ref[...])
        dgr = (dlog_a * (-LRU_C) * sp) * (r * (1.0 - r))
        dgi = dig * (ig * (1.0 - ig))
        dlam = jnp.sum(dlog_a * r, axis=0, keepdims=True) * (LRU_C * _sigmoid(-lam_ref[...]))
        dgr_b = [dgr[:, hh * LRU_HALF:(hh + 1) * LRU_HALF].astype(BF16) for hh in range(2)]
        dgi_b = [dgi[:, hh * LRU_HALF:(hh + 1) * LRU_HALF].astype(BF16) for hh in range(2)]
        dxc = dxc + jnp.concatenate(
            [_mm_nt(dgr_b[hh], wa_ref[hh]) + _mm_nt(dgi_b[hh], wx_ref[hh]) for hh in range(2)], axis=1)
        for hh in range(2):
            dwa_ref[hh] += _mm_tn(halves[hh], dgr_b[hh])
            dwx_ref[hh] += _mm_tn(halves[hh], dgi_b[hh])

        dbuf[0:tm, :] = dxc
        ahead = [dbuf[pl.ds(CONV_WIDTH - 1 - j, tm), :] for j in range(CONV_WIDTH)]
        dxr = sum(cw_ref[j:j + 1, :] * ahead[j] for j in range(CONV_WIDTH))
        dbuf[tm:tm + SUBLANES, :] = dxc[0:SUBLANES, :]
        drz_ref[:, 0:LRU_WIDTH] = dxr.astype(BF16)
        drz_ref[:, LRU_WIDTH:] = dyr.astype(BF16)

        xr = xr_ref[...]
        upd = [jnp.sum(xr * ahead[j], axis=0, keepdims=True) for j in range(CONV_WIDTH)]
        upd += [jnp.sum(dxc, axis=0, keepdims=True), jnp.sum(dgr, axis=0, keepdims=True),
                jnp.sum(dgi, axis=0, keepdims=True), dlam]
        small_ref[...] += jnp.concatenate(upd, axis=0)

    rev = lambda s: nt - 1 - s
    halo = lambda s: jnp.maximum(rev(s) * per - 1, 0)
    tile0 = pl.BlockSpec((tm, LRU_WIDTH), lambda s: (rev(s), 0))
    tile1 = pl.BlockSpec((tm, LRU_WIDTH), lambda s: (rev(s), 1))
    halo0 = pl.BlockSpec((SUBLANES, LRU_WIDTH), lambda s: (halo(s), 0))
    vec = _full((1, LRU_WIDTH))
    bd = _full((2, LRU_HALF, LRU_HALF))
    big = pltpu.VMEM((tm + SUBLANES, LRU_WIDTH), F32)
    tile = pltpu.VMEM((tm, LRU_WIDTH), F32)
    return _hosting_call(
        body, "rec_bwd", nt,
        [tile0, tile0, tile1, tile0, tile0, halo0, _full((CONV_WIDTH, LRU_WIDTH)), bd, vec, bd, vec, vec],
        [pl.BlockSpec((tm, 2 * LRU_WIDTH), lambda s: (rev(s), 0)), _full((SUBLANES, LRU_WIDTH)), bd, bd],
        [jax.ShapeDtypeStruct((rows, 2 * LRU_WIDTH), BF16), jax.ShapeDtypeStruct((SUBLANES, LRU_WIDTH), F32),
         jax.ShapeDtypeStruct((2, LRU_HALF, LRU_HALF), F32), jax.ShapeDtypeStruct((2, LRU_HALF, LRU_HALF), F32)],
        [big, big, tile, tile, big, pltpu.VMEM((SUBLANES, LRU_WIDTH), F32)],
        (drec, zrec, zrec, h, xc_all, h, conv_w, wa_bd, b_a, wx_bd, b_x, lam), carried, modes)


DZ_CUTS = (0, ATTN_WIDTH, QKV_WIDTH, IN_WIDTH)


def _dz_specs(tm):
    return [pl.BlockSpec((tm, DZ_CUTS[p + 1] - DZ_CUTS[p]), lambda i: (i, 0)) for p in range(3)]


def _in_proj_bwd_x(h0, g1, dh1, dq, dkv, drz, w_in_t, carried, modes):
    rows = h0.shape[0]
    tm = _row_tile(rows)

    def body(h_ref, g_ref, dh1_ref, dq_ref, dkv_ref, drz_ref, w_ref, dh0_ref, dg_ref):
        @pl.when(pl.program_id(0) == 0)
        def _():
            dg_ref[...] = jnp.zeros_like(dg_ref)

        g = g_ref[...]
        _, xhat, rstd = _rms_fwd(h_ref[...], g)
        parts = (dq_ref[...], dkv_ref[...], drz_ref[...])
        du = sum(_mm(parts[p], w_ref[DZ_CUTS[p]:DZ_CUTS[p + 1], :]) for p in range(3))
        dx, dg = _rms_bwd(du, xhat, rstd, g)
        dh0_ref[...] = dh1_ref[...] + dx
        dg_ref[...] += dg

    wide = pl.BlockSpec((tm, D_MODEL), lambda i: (i, 0))
    return _hosting_call(
        body, "in_proj_bwd_x", rows // tm,
        [wide, _full((1, D_MODEL)), wide] + _dz_specs(tm) + [_resident((IN_WIDTH, D_MODEL))],
        [wide, _full((1, D_MODEL))],
        [jax.ShapeDtypeStruct((rows, D_MODEL), F32), jax.ShapeDtypeStruct((1, D_MODEL), F32)],
        [], (h0, g1, dh1, dq, dkv, drz, w_in_t), carried, modes)


def _in_proj_bwd_w(u1, dq, dkv, drz, carried, modes):
    rows = u1.shape[0]
    tb = _big_tile(rows)

    def body(u_ref, dq_ref, dkv_ref, drz_ref, dw_ref):
        @pl.when(pl.program_id(0) == 0)
        def _():
            dw_ref[...] = jnp.zeros_like(dw_ref)

        u = u_ref[...]
        for p, ref in enumerate((dq_ref, dkv_ref, drz_ref)):
            dw_ref[:, DZ_CUTS[p]:DZ_CUTS[p + 1]] += _mm_tn(u, ref[...])

    return _hosting_call(
        body, "in_proj_bwd_w", rows // tb,
        [pl.BlockSpec((tb, D_MODEL), lambda i: (i, 0))] + _dz_specs(tb),
        [_full((D_MODEL, IN_WIDTH))],
        [jax.ShapeDtypeStruct((D_MODEL, IN_WIDTH), F32)],
        [], (u1, dq, dkv, drz), carried, modes)


def _adamw(w, m, v, parts, name):
    rows, cols = w.shape
    tr = next((t for t in (256, 128) if rows % t == 0), rows)
    parts = parts if isinstance(parts, (list, tuple)) else [parts]

    def body(w_ref, m_ref, v_ref, *refs):
        p_refs, (g_ref, d_ref, nm_ref, nv_ref) = refs[:len(parts)], refs[len(parts):]

        def total(p_ref):
            g = p_ref[0].astype(F32)
            for s in range(1, N_DEV):
                g = g + p_ref[s].astype(F32)
            return g

        g = jnp.concatenate([total(p_ref) for p_ref in p_refs], axis=1) if len(parts) > 1 else total(p_refs[0])
        nm = ADAM_B1 * m_ref[...] + (1.0 - ADAM_B1) * g
        nv = ADAM_B2 * v_ref[...] + (1.0 - ADAM_B2) * (g * g)
        m_hat = nm / (1.0 - ADAM_B1 ** ADAM_STEP)
        v_hat = nv / (1.0 - ADAM_B2 ** ADAM_STEP)
        g_ref[...] = g
        d_ref[...] = (-ADAM_LR) * (m_hat / (jnp.sqrt(v_hat) + ADAM_EPS) + ADAM_WD * w_ref[...])
        nm_ref[...] = nm
        nv_ref[...] = nv

    blk = pl.BlockSpec((tr, cols), lambda i: (i, 0))
    return pl.pallas_call(
        body, name=name, grid=(rows // tr,),
        in_specs=[blk, blk, blk] + [pl.BlockSpec((N_DEV, tr, p.shape[2]), lambda i: (0, i, 0)) for p in parts],
        out_specs=[blk] * 4,
        out_shape=[jax.ShapeDtypeStruct((rows, cols), F32)] * 4,
        compiler_params=_params(("parallel",)),
    )(w, m, v, *parts)


def _cols_from_shards(g):
    return jnp.transpose(g, (1, 0, 2)).reshape(g.shape[1], N_DEV * g.shape[2])


def _cols_to_shards(a):
    r, c = a.shape
    return jnp.transpose(a.reshape(r, N_DEV, c // N_DEV), (1, 0, 2))


def _block_diag(w):
    per = LRU_HALF // LRU_BLOCK
    w = w.reshape(2, per, LRU_BLOCK, LRU_BLOCK)
    eye = jnp.eye(per, dtype=w.dtype)
    return (w[:, :, :, None, :] * eye[None, :, None, :, None]).reshape(2, LRU_HALF, LRU_HALF)


def _block_diag_extract(t):
    per = LRU_HALF // LRU_BLOCK
    t = t.reshape(2, per, LRU_BLOCK, per, LRU_BLOCK)
    return jnp.stack([t[:, b, :, b, :] for b in range(per)], axis=1).reshape(LRU_BLOCKS, LRU_BLOCK, LRU_BLOCK)


SMALL_NAMES = ("conv_b", "w_a", "b_a", "w_x", "b_x", "lru_lambda", "attn_sinks",
               "g_post_mix", "g_pre_ffn", "g_post_ffn")


def _pack_small(vals):
    flat = []
    for name in SMALL_NAMES:
        a = vals[name].reshape(-1)
        flat.append(jnp.pad(a, (0, (-a.shape[0]) % LANES)))
    flat = jnp.concatenate(flat)
    rows = flat.shape[0] // LANES
    return jnp.pad(flat.reshape(rows, LANES), ((0, (-rows) % SUBLANES), (0, 0)))


def _unpack_small(packed, like):
    flat = packed.reshape(-1)
    out, at = {}, 0
    for name in SMALL_NAMES:
        n = like[name].size
        out[name] = flat[at:at + n].reshape(like[name].shape)
        at += n + (-n) % LANES
    return out


def kernel(x, meta_tokens, g_pre_mix, w_in, conv_w, conv_b, w_a, b_a, w_x, b_x, lru_lambda, attn_sinks, w_out, g_post_mix, g_pre_ffn, w_ff1, w_ff2, g_post_ffn, loss_target, m_meta_tokens, m_g_pre_mix, m_w_in, m_conv_w, m_conv_b, m_w_a, m_b_a, m_w_x, m_b_x, m_lru_lambda, m_attn_sinks, m_w_out, m_g_post_mix, m_g_pre_ffn, m_w_ff1, m_w_ff2, m_g_post_ffn, v_meta_tokens, v_g_pre_mix, v_w_in, v_conv_w, v_conv_b, v_w_a, v_b_a, v_w_x, v_b_x, v_lru_lambda, v_attn_sinks, v_w_out, v_g_post_mix, v_g_pre_ffn, v_w_ff1, v_w_ff2, v_g_post_ffn):
    weights = dict(meta_tokens=meta_tokens, g_pre_mix=g_pre_mix, w_in=w_in, conv_w=conv_w, conv_b=conv_b, w_a=w_a,
                   b_a=b_a, w_x=w_x, b_x=b_x, lru_lambda=lru_lambda, attn_sinks=attn_sinks, w_out=w_out,
                   g_post_mix=g_post_mix, g_pre_ffn=g_pre_ffn, w_ff1=w_ff1, w_ff2=w_ff2, g_post_ffn=g_post_ffn)
    mom_m = dict(meta_tokens=m_meta_tokens, g_pre_mix=m_g_pre_mix, w_in=m_w_in, conv_w=m_conv_w, conv_b=m_conv_b,
                 w_a=m_w_a, b_a=m_b_a, w_x=m_w_x, b_x=m_b_x, lru_lambda=m_lru_lambda, attn_sinks=m_attn_sinks,
                 w_out=m_w_out, g_post_mix=m_g_post_mix, g_pre_ffn=m_g_pre_ffn, w_ff1=m_w_ff1, w_ff2=m_w_ff2,
                 g_post_ffn=m_g_post_ffn)
    mom_v = dict(meta_tokens=v_meta_tokens, g_pre_mix=v_g_pre_mix, w_in=v_w_in, conv_w=v_conv_w, conv_b=v_conv_b,
                 w_a=v_w_a, b_a=v_b_a, w_x=v_w_x, b_x=v_b_x, lru_lambda=v_lru_lambda, attn_sinks=v_attn_sinks,
                 w_out=v_w_out, g_post_mix=v_g_post_mix, g_pre_ffn=v_g_pre_ffn, w_ff1=v_w_ff1, w_ff2=v_w_ff2,
                 g_post_ffn=v_g_post_ffn)
    order = list(weights)

    (g_win, g_meta, g_cw) = _exchange([w_in[0].astype(BF16), meta_tokens, conv_w[0]], ["gather"] * 3, "gather_first")
    w_in_full = _cols_from_shards(g_win)
    meta_full = _cols_from_shards(g_meta)
    conv_w_full = _cols_from_shards(g_cw)

    head = jnp.concatenate([jnp.zeros((PAD_ROWS, D_MODEL), F32), meta_full], axis=0)
    wa_bd = _block_diag(w_a[0]).astype(BF16)
    wx_bd = _block_diag(w_x[0]).astype(BF16)
    bias = _attn_bias()

    w1_shard = w_ff1[0].astype(BF16)
    (qkv, zrec, u1, h0), (g_wout,) = _in_proj_fwd(head, x[0], g_pre_mix, w_in_full, [w_out[0].astype(BF16)], ["gather"])
    (attn,), (w1a,) = _attn_fwd(qkv, attn_sinks, bias, [w1_shard[:, :FF_HALF]], ["gather"])
    (rec, h_lru, xc), (w1b,) = _rec_fwd(zrec, conv_w_full, conv_b, wa_bd, b_a, wx_bd, b_x, lru_lambda,
                                         [w1_shard[:, FF_HALF:]], ["gather"])
    w_out_full = g_wout.reshape(D_MODEL, D_MODEL)
    mix, h1 = _out_proj_fwd(attn, rec, w_out_full, h0, g_post_mix)
    (act, u2), (w2g,) = _ffn_up(h1, g_pre_ffn, (w1a, w1b), [w_ff2[0].astype(BF16)], ["gather"])
    dy, df, dg_post_ffn, loss_acc = _ffn_down_loss(act, w2g.reshape(D_FF, D_MODEL), h1, loss_target[0], g_post_ffn)

    da1 = _ffn_bwd_act(df, jnp.transpose(w2g, (0, 2, 1)), act)
    dw1a, dw1b, dw2g = _ffn_bwd_weights(u2, da1, act, df)
    w1t = jnp.concatenate([jnp.transpose(w1a, (0, 2, 1)), jnp.transpose(w1b, (0, 2, 1))], axis=1).reshape(D_FF, D_MODEL)
    (dh1, dg_pre_ffn), (p_w1a,) = _ffn_bwd_x(da1, w1t, h1, dy, g_pre_ffn, [dw1a], ["scatter"])
    (dattn, drec, dw_out, dg_post_mix), (p_w1b,) = _out_proj_bwd(dh1, mix, g_post_mix, w_out_full.T, attn, rec,
                                                                [dw1b], ["scatter"])
    (dq, dkv, dsinks), (p_w2,) = _attn_bwd(qkv, dattn, attn_sinks, bias, [dw2g], ["scatter"])
    (drz, rec_small, dwa_bd, dwx_bd), (p_wout,) = _rec_bwd(
        drec, zrec, h_lru, xc, conv_w_full, wa_bd, b_a, wx_bd, b_x, lru_lambda,
        [dw_out.reshape(N_DEV, D_MODEL // N_DEV, D_MODEL)], ["scatter"])
    small_grads = dict(
        conv_b=rec_small[ROW_CONV_B], w_a=_block_diag_extract(dwa_bd), b_a=rec_small[ROW_B_A],
        w_x=_block_diag_extract(dwx_bd), b_x=rec_small[ROW_B_X], lru_lambda=rec_small[ROW_LAMBDA],
        attn_sinks=dsinks[:, 0], g_post_mix=dg_post_mix, g_pre_ffn=dg_pre_ffn, g_post_ffn=dg_post_ffn)
    (dw_in,), (p_cw, p_small) = _in_proj_bwd_w(
        u1, dq, dkv, drz, [_cols_to_shards(rec_small[0:CONV_WIDTH]), _pack_small(small_grads)], ["scatter", "gather"])
    (dh0, dg_pre_mix), (p_win,) = _in_proj_bwd_x(
        h0, g_pre_mix, dh1, dq, dkv, drz, w_in_full.T, [_cols_to_shards(dw_in).astype(BF16)], ["scatter"])
    p_meta, p_gpm = _exchange([_cols_to_shards(dh0[PAD_ROWS:BLOCK]), dg_pre_mix], ["scatter", "gather"], "exchange_last")

    res = {}
    res["g_pre_mix"] = _adamw(g_pre_mix, m_g_pre_mix, v_g_pre_mix, p_gpm, "adamw_g_pre_mix")
    res["w_in"] = _adamw(w_in[0], m_w_in[0], v_w_in[0], p_win, "adamw_w_in")
    res["w_out"] = _adamw(w_out[0], m_w_out[0], v_w_out[0], p_wout, "adamw_w_out")
    res["w_ff1"] = _adamw(w_ff1[0], m_w_ff1[0], v_w_ff1[0], [p_w1a, p_w1b], "adamw_w_ff1")
    res["w_ff2"] = _adamw(w_ff2[0], m_w_ff2[0], v_w_ff2[0], p_w2, "adamw_w_ff2")
    res["meta_tokens"] = _adamw(meta_tokens, m_meta_tokens, v_meta_tokens, p_meta, "adamw_meta")
    res["conv_w"] = _adamw(conv_w[0], m_conv_w[0], v_conv_w[0], p_cw, "adamw_conv_w")
    small = _adamw(_pack_small(weights), _pack_small(mom_m), _pack_small(mom_v), p_small, "adamw_small")
    small = [_unpack_small(t, weights) for t in small]
    for name in SMALL_NAMES:
        res[name] = tuple(t[name] for t in small)
    for name in ("w_in", "w_out", "w_ff1", "w_ff2", "conv_w"):
        res[name] = tuple(t[None] for t in res[name])

    loss = lax.psum(loss_acc[0, 0], ("x", "y", "c"))
    grad_x = dh0[BLOCK:][None]
    outs = [loss, grad_x]
    for k in range(4):
        outs += [res[name][k] for name in order]
    return tuple(outs)
```

```python
import jax
import jax.numpy as jnp
import numpy as np
from jax import lax
from jax.experimental import pallas as pl
from jax.experimental.pallas import tpu as pltpu

F32 = jnp.float32
BF16 = jnp.bfloat16

D_MODEL = 1024
N_META = 16
HEAD_DIM = 64
ATTN_HEADS = 8
KV_HEADS = 2
GQA_GROUP = ATTN_HEADS // KV_HEADS
ATTN_WIDTH = ATTN_HEADS * HEAD_DIM
KV_WIDTH = KV_HEADS * HEAD_DIM
QKV_WIDTH = ATTN_WIDTH + 2 * KV_WIDTH
LRU_WIDTH = 512
LRU_BLOCKS = 8
LRU_BLOCK = 64
LRU_HALF = 256
LRU_C = 8.0
CONV_WIDTH = 4
BLOCK = 128
PAD_ROWS = BLOCK - N_META
IN_WIDTH = QKV_WIDTH + 2 * LRU_WIDTH
D_FF = 4096
EPS = 1e-6
NEG = -1e30
N_DEV = 8
FF_CHUNK = D_FF // N_DEV
SUBLANES = 8
LANES = 128

ADAM_LR = 0.001
ADAM_B1 = 0.9
ADAM_B2 = 0.999
ADAM_EPS = 1e-08
ADAM_WD = 0.01
ADAM_STEP = 10

VMEM_LIMIT = 56 * 1024 * 1024


def _row_tile(rows):
    for t in (640, 512, 256, 128):
        if rows % t == 0:
            return t
    raise ValueError(rows)


def _big_tile(rows):
    for t in (1664, 1024, 512, 256, 128):
        if rows % t == 0:
            return t
    raise ValueError(rows)


def _rec_tile(rows):
    for t in (320, 256, 128):
        if rows % t == 0:
            return t
    raise ValueError(rows)


def _params(semantics):
    return pltpu.CompilerParams(dimension_semantics=semantics, vmem_limit_bytes=VMEM_LIMIT)


def _mm(a, b):
    return lax.dot_general(a, b, (((1,), (0,)), ((), ())), preferred_element_type=F32)


def _mm_nt(a, b):
    return lax.dot_general(a, b, (((1,), (1,)), ((), ())), preferred_element_type=F32)


def _mm_tn(a, b):
    return lax.dot_general(a, b, (((0,), (0,)), ((), ())), preferred_element_type=F32)


def _rms_fwd(x, g):
    rstd = lax.rsqrt(jnp.mean(x * x, axis=-1, keepdims=True) + EPS)
    xhat = x * rstd
    return xhat * g, xhat, rstd


def _rms_bwd(dy, xhat, rstd, g):
    dyg = dy * g
    c = jnp.mean(dyg * xhat, axis=-1, keepdims=True)
    dx = rstd * (dyg - xhat * c)
    dg = jnp.sum(dy * xhat, axis=0, keepdims=True)
    return dx, dg


def _sigmoid(x):
    return 0.5 * jnp.tanh(0.5 * x) + 0.5


def _log1p(x):
    u = 1.0 + x
    return jnp.where(u == 1.0, x, jnp.log(u) * x / (u - 1.0))


def _one_minus_sq_exp(x, ex):
    return -jnp.tanh(x) * (1.0 + ex * ex)


TINY = 1e-30


def _sqrt_pos(y):
    r = lax.rsqrt(jnp.maximum(y, TINY))
    return y * r, r


def _softplus(x):
    return jnp.maximum(x, 0.0) + _log1p(jnp.exp(-jnp.abs(x)))


GELU_C = 0.7978845608028654
GELU_K = 0.044715


def _gelu(x):
    t = jnp.tanh(GELU_C * (x + GELU_K * x * x * x))
    return 0.5 * x * (1.0 + t), t


def _gelu_grad(x, t):
    return 0.5 * (1.0 + t) + 0.5 * x * (1.0 - t * t) * GELU_C * (1.0 + 3.0 * GELU_K * x * x)


def _full(shape):
    return pl.BlockSpec(shape, lambda *_: (0,) * len(shape))


def _resident(shape):
    return pl.BlockSpec(shape, lambda *_: (0,) * len(shape), pipeline_mode=pl.Buffered(1))


def _exchange_copies(ins, outs, sems, modes):
    send_sems, recv_sems, local_sems = sems
    x, y, c = lax.axis_index("x"), lax.axis_index("y"), lax.axis_index("c")
    me = 4 * x + 2 * y + c

    def block(a, dev):
        return ins[a] if modes[a] == "gather" else ins[a].at[dev]

    local = [pltpu.make_async_copy(block(a, me), outs[a].at[me], local_sems.at[a]) for a in range(len(ins))]
    sends, recvs = [], []
    for a in range(len(ins)):
        for k in range(N_DEV - 1):
            bits = k + 1
            px = jnp.bitwise_xor(x, (bits >> 2) & 1)
            py = jnp.bitwise_xor(y, (bits >> 1) & 1)
            pc = jnp.bitwise_xor(c, bits & 1)
            peer = 4 * px + 2 * py + pc
            common = dict(src_ref=block(a, peer), send_sem=send_sems.at[a, k], recv_sem=recv_sems.at[a, k],
                          device_id=(px, py, pc), device_id_type=pl.DeviceIdType.MESH)
            sends.append(pltpu.make_async_remote_copy(dst_ref=outs[a].at[me], **common))
            recvs.append(pltpu.make_async_remote_copy(dst_ref=outs[a].at[peer], **common))
    return local, sends, recvs


def _exchange_start(ins, outs, sems, modes):
    local, sends, _ = _exchange_copies(ins, outs, sems, modes)
    for cp in local + sends:
        cp.start()


def _exchange_wait(ins, outs, sems, modes):
    local, sends, recvs = _exchange_copies(ins, outs, sems, modes)
    for cp in recvs:
        cp.wait_recv()
    for cp in sends:
        cp.wait_send()
    for cp in local:
        cp.wait()


def _exchange_shapes(arrays, modes):
    return [jax.ShapeDtypeStruct((N_DEV,) + a.shape if mode == "gather" else a.shape, a.dtype)
            for a, mode in zip(arrays, modes)]


def _exchange_sems(na):
    return [pltpu.SemaphoreType.DMA((na, N_DEV - 1)), pltpu.SemaphoreType.DMA((na, N_DEV - 1)),
            pltpu.SemaphoreType.DMA((na,))]


ANY_SPACE = pl.BlockSpec(memory_space=pl.ANY)


def _exchange(arrays, modes, name):
    na = len(arrays)

    def body(*refs):
        ins, outs, sems = refs[:na], refs[na:2 * na], refs[2 * na:]
        _exchange_start(ins, outs, sems, modes)
        _exchange_wait(ins, outs, sems, modes)

    return pl.pallas_call(
        body, name=name, out_shape=_exchange_shapes(arrays, modes),
        in_specs=[ANY_SPACE] * na, out_specs=[ANY_SPACE] * na, scratch_shapes=_exchange_sems(na),
        compiler_params=pltpu.CompilerParams(has_side_effects=True),
    )(*arrays)


def _gather_two_level(arrays, name):
    na = len(arrays)

    def body(*refs):
        ins, outs = refs[:na], refs[na:2 * na]
        send_sems, recv_sems, local_sems = refs[2 * na:]
        x, y, c = lax.axis_index("x"), lax.axis_index("y"), lax.axis_index("c")
        me, sibling = (x, y, c), (x, y, 1 - c)
        chips = [(1 - x, y), (x, 1 - y), (1 - x, 1 - y)]

        def copy(a, k, block, to, src=None):
            slot = outs[a].at[4 * block[0] + 2 * block[1] + block[2]]
            return pltpu.make_async_remote_copy(
                src_ref=slot if src is None else src, dst_ref=slot, send_sem=send_sems.at[a, k],
                recv_sem=recv_sems.at[a, k], device_id=to, device_id_type=pl.DeviceIdType.MESH)

        local = [pltpu.make_async_copy(ins[a], outs[a].at[4 * x + 2 * y + c], local_sems.at[a]) for a in range(na)]
        first = []
        for a in range(na):
            first.append(copy(a, 0, me, sibling, src=ins[a]))
            first += [copy(a, 1 + j, me, (*chip, c), src=ins[a]) for j, chip in enumerate(chips)]
        for cp in local + first:
            cp.start()
        passed = []
        for j, chip in enumerate(chips):
            for a in range(na):
                copy(a, 1 + j, (*chip, c), me).wait_recv()
                passed.append(copy(a, 4 + j, (*chip, c), sibling))
                passed[-1].start()
        for a in range(na):
            copy(a, 0, sibling, me).wait_recv()
            for j, chip in enumerate(chips):
                copy(a, 4 + j, (*chip, 1 - c), me).wait_recv()
        for cp in first + passed:
            cp.wait_send()
        for cp in local:
            cp.wait()

    return pl.pallas_call(
        body, name=name, out_shape=_exchange_shapes(arrays, ["gather"] * na),
        in_specs=[ANY_SPACE] * na, out_specs=[ANY_SPACE] * na, scratch_shapes=_exchange_sems(na),
        compiler_params=pltpu.CompilerParams(has_side_effects=True),
    )(*arrays)


def _hosting_call(body, name, steps, in_specs, out_specs, out_shape, scratch_shapes, args, arrays, modes):
    n_in, n_out, n_scr, na = len(in_specs), len(out_specs), len(scratch_shapes), len(arrays)

    def hosting_body(*refs):
        cuts = [0]
        for n in (n_in, na, n_out, na, n_scr, 3):
            cuts.append(cuts[-1] + n)
        ins, x_ins, outs, x_outs, scr, sems = (refs[cuts[p]:cuts[p + 1]] for p in range(6))
        step = pl.program_id(0)

        @pl.when(step == 0)
        def _():
            _exchange_start(x_ins, x_outs, sems, modes)

        body(*ins, *outs, *scr)

        @pl.when(step == steps - 1)
        def _():
            _exchange_wait(x_ins, x_outs, sems, modes)

    res = pl.pallas_call(
        hosting_body, name=name, grid=(steps,),
        in_specs=list(in_specs) + [ANY_SPACE] * na, out_specs=list(out_specs) + [ANY_SPACE] * na,
        out_shape=list(out_shape) + _exchange_shapes(arrays, modes),
        scratch_shapes=list(scratch_shapes) + _exchange_sems(na),
        compiler_params=_params(("arbitrary",)),
    )(*args, *arrays)
    return res[:n_out], res[n_out:]


def _frame_rows(src_hbm, buf, sem, i, steps, tm):
    def first():
        return pltpu.make_async_copy(src_hbm.at[pl.ds(0, tm - BLOCK)], buf.at[0, pl.ds(BLOCK, tm - BLOCK)], sem.at[0])

    def later(t, slot):
        return pltpu.make_async_copy(src_hbm.at[pl.ds(pl.multiple_of(t * tm - BLOCK, BLOCK), tm)], buf.at[slot], sem.at[slot])

    slot = i % 2

    @pl.when(i == 0)
    def _():
        first().start()

    @pl.when(i + 1 < steps)
    def _():
        later(i + 1, 1 - slot).start()

    @pl.when(i == 0)
    def _():
        first().wait()

    @pl.when(i > 0)
    def _():
        later(i, slot).wait()

    return slot


def _frame_scratch(tm):
    return [pltpu.VMEM((2, tm, D_MODEL), F32), pltpu.SemaphoreType.DMA((2,))]


def _in_proj_fwd(head, x, g1, w_in, carried, modes):
    rows = BLOCK + x.shape[0]
    tm = _row_tile(rows)
    steps = rows // tm

    def body(head_ref, g_ref, w_ref, x_hbm, qkv_ref, zrec_ref, u_ref, h_ref, buf, sem):
        i = pl.program_id(0)
        slot = _frame_rows(x_hbm, buf, sem, i, steps, tm)

        @pl.when(i == 0)
        def _():
            buf[0, 0:BLOCK, :] = head_ref[...]

        h = buf[slot]
        h_ref[...] = h
        u, _, _ = _rms_fwd(h, g_ref[...])
        u = u.astype(BF16)
        u_ref[...] = u
        z = _mm(u, w_ref[...])
        qkv_ref[...] = z[:, :QKV_WIDTH].astype(BF16)
        zrec_ref[...] = z[:, QKV_WIDTH:]

    wide = pl.BlockSpec((tm, D_MODEL), lambda i: (i, 0))
    return _hosting_call(
        body, "in_proj_fwd", steps,
        [_full((BLOCK, D_MODEL)), _full((1, D_MODEL)), _resident((D_MODEL, IN_WIDTH)), ANY_SPACE],
        [pl.BlockSpec((tm, QKV_WIDTH), lambda i: (i, 0)), pl.BlockSpec((tm, 2 * LRU_WIDTH), lambda i: (i, 0)), wide, wide],
        [jax.ShapeDtypeStruct((rows, QKV_WIDTH), BF16), jax.ShapeDtypeStruct((rows, 2 * LRU_WIDTH), F32),
         jax.ShapeDtypeStruct((rows, D_MODEL), BF16), jax.ShapeDtypeStruct((rows, D_MODEL), F32)],
        _frame_scratch(tm), (head, g1, w_in, x), carried, modes)


N_BIAS = 3


def _attn_bias():
    key = np.arange(2 * BLOCK)[:, None]
    r = np.arange(GQA_GROUP * BLOCK)[None, :] % BLOCK
    band = (key > r) & (key <= r + BLOCK)
    out = [np.where(band & ((n - 1) * BLOCK + key >= PAD_ROWS), 0.0, NEG) for n in range(N_BIAS)]
    return jnp.asarray(np.stack(out), F32)


def _attn_probs(k2, q4, bias, sink_row):
    s = _mm_nt(k2, q4) * (HEAD_DIM ** -0.5) + bias
    m = jnp.maximum(jnp.max(s, axis=0, keepdims=True), sink_row)
    p = jnp.exp(s - m)
    es = jnp.exp(sink_row - m)
    inv = 1.0 / (jnp.sum(p, axis=0, keepdims=True) + es)
    return p * inv, es * inv


def _heads(ref, first, count):
    return jnp.concatenate([ref[:, (first + g) * HEAD_DIM:(first + g + 1) * HEAD_DIM] for g in range(count)], axis=0)


def _sink_row(sink_ref, kv):
    g = lax.broadcasted_iota(jnp.int32, (1, GQA_GROUP * BLOCK), 1) // BLOCK
    row = jnp.full((1, GQA_GROUP * BLOCK), sink_ref[0, kv * GQA_GROUP], F32)
    for i in range(1, GQA_GROUP):
        row = jnp.where(g == i, sink_ref[0, kv * GQA_GROUP + i], row)
    return row


def _from_head_major(pieces):
    return jnp.concatenate(pieces, axis=0).T


def _attn_fwd(qkv, sinks, bias, carried, modes):
    rows = qkv.shape[0]
    nb = rows // BLOCK
    k_col, v_col = ATTN_WIDTH // KV_WIDTH, ATTN_WIDTH // KV_WIDTH + 1

    def body(sink_ref, bias_ref, q_ref, kp_ref, kc_ref, vp_ref, vc_ref, o_ref):
        bias_t = bias_ref[...]
        pieces = []
        for kv in range(KV_HEADS):
            sl = slice(kv * HEAD_DIM, (kv + 1) * HEAD_DIM)
            k2 = jnp.concatenate([kp_ref[:, sl], kc_ref[:, sl]], axis=0)
            v2 = jnp.concatenate([vp_ref[:, sl], vc_ref[:, sl]], axis=0)
            q4 = _heads(q_ref, kv * GQA_GROUP, GQA_GROUP)
            pn, _ = _attn_probs(k2, q4, bias_t, _sink_row(sink_ref, kv))
            ot = _mm_tn(v2, pn.astype(BF16))
            pieces += [ot[:, g * BLOCK:(g + 1) * BLOCK] for g in range(GQA_GROUP)]
        o_ref[...] = _from_head_major(pieces).astype(BF16)

    prev = lambda n: jnp.maximum(n - 1, 0)
    return _hosting_call(
        body, "attn_fwd", nb,
        [pl.BlockSpec(memory_space=pltpu.SMEM),
         pl.BlockSpec((None, 2 * BLOCK, GQA_GROUP * BLOCK), lambda n: (jnp.minimum(n, N_BIAS - 1), 0, 0)),
         pl.BlockSpec((BLOCK, ATTN_WIDTH), lambda n: (n, 0)),
         pl.BlockSpec((BLOCK, KV_WIDTH), lambda n: (prev(n), k_col)),
         pl.BlockSpec((BLOCK, KV_WIDTH), lambda n: (n, k_col)),
         pl.BlockSpec((BLOCK, KV_WIDTH), lambda n: (prev(n), v_col)),
         pl.BlockSpec((BLOCK, KV_WIDTH), lambda n: (n, v_col))],
        [pl.BlockSpec((BLOCK, ATTN_WIDTH), lambda n: (n, 0))],
        [jax.ShapeDtypeStruct((rows, ATTN_WIDTH), BF16)],
        [], (sinks, bias, qkv, qkv, qkv, qkv, qkv), carried, modes)


def _conv_taps(xbuf, tm):
    return [xbuf[pl.ds(SUBLANES - (CONV_WIDTH - 1 - j), tm), :] for j in range(CONV_WIDTH)]


def _lru_gates(xc, wa_ref, ba_ref, wx_ref, bx_ref, lam_ref):
    halves = [xc[:, h * LRU_HALF:(h + 1) * LRU_HALF].astype(BF16) for h in range(2)]
    gate_r = jnp.concatenate([_mm(halves[h], wa_ref[h]) for h in range(2)], axis=1) + ba_ref[...]
    gate_i = jnp.concatenate([_mm(halves[h], wx_ref[h]) for h in range(2)], axis=1) + bx_ref[...]
    r = _sigmoid(gate_r)
    ig = _sigmoid(gate_i)
    log_a = (-LRU_C) * r * _softplus(-lam_ref[...])
    a = jnp.exp(log_a)
    mult, inv_mult = _sqrt_pos(_one_minus_sq_exp(log_a, a))
    return halves, r, ig, a, mult, inv_mult


def _scan_tile(a_ref, u_ref, out_ref, carry, tm, reverse):
    row = lax.broadcasted_iota(jnp.int32, (SUBLANES, LRU_WIDTH), 0)
    groups = tm // SUBLANES

    def step(j, prev):
        jj = groups - 1 - j if reverse else j
        o = pl.multiple_of(jj * SUBLANES, SUBLANES)
        a = a_ref[pl.ds(o, SUBLANES), :]
        u = u_ref[pl.ds(o, SUBLANES), :]
        for s in (1, 2, 4):
            shift = SUBLANES - s if reverse else s
            keep = (row < SUBLANES - s) if reverse else (row >= s)
            u = jnp.where(keep, a * pltpu.roll(u, shift, 0) + u, u)
            a = jnp.where(keep, a * pltpu.roll(a, shift, 0), a)
        out = a * prev + u
        out_ref[pl.ds(o, SUBLANES), :] = out
        return out[0:1, :] if reverse else out[SUBLANES - 1:SUBLANES, :]

    return lax.fori_loop(0, groups, step, carry)


def _rec_fwd(zrec, conv_w, conv_b, wa_bd, b_a, wx_bd, b_x, lam, carried, modes):
    rows = zrec.shape[0]
    tm = _rec_tile(rows)

    def body(xr_ref, yr_ref, cw_ref, cb_ref, wa_ref, ba_ref, wx_ref, bx_ref, lam_ref, rec_ref, h_ref, xc_ref,
             xbuf, a_s, u_s, carry):
        i = pl.program_id(0)

        @pl.when(i == 0)
        def _():
            xbuf[0:SUBLANES, :] = jnp.zeros((SUBLANES, LRU_WIDTH), F32)
            carry[...] = jnp.zeros_like(carry)

        @pl.when(i > 0)
        def _():
            xbuf[0:SUBLANES, :] = xbuf[tm:tm + SUBLANES, :]

        xbuf[SUBLANES:SUBLANES + tm, :] = xr_ref[...]
        taps = _conv_taps(xbuf, tm)
        xc = cb_ref[...] + sum(cw_ref[j:j + 1, :] * taps[j] for j in range(CONV_WIDTH))
        xc_ref[...] = xc
        _, r, ig, a, mult, _ = _lru_gates(xc, wa_ref, ba_ref, wx_ref, bx_ref, lam_ref)
        grow = i * tm + lax.broadcasted_iota(jnp.int32, (tm, LRU_WIDTH), 0)
        a_s[...] = a
        u_s[...] = jnp.where(grow >= PAD_ROWS, mult * (ig * xc), 0.0)
        carry[0:1, :] = _scan_tile(a_s, u_s, h_ref, carry[0:1, :], tm, reverse=False)
        gel, _ = _gelu(yr_ref[...])
        rec_ref[...] = (gel * h_ref[...]).astype(BF16)

    vec = _full((1, LRU_WIDTH))
    bd = _full((2, LRU_HALF, LRU_HALF))
    return _hosting_call(
        body, "rec_fwd", rows // tm,
        [pl.BlockSpec((tm, LRU_WIDTH), lambda i: (i, 0)), pl.BlockSpec((tm, LRU_WIDTH), lambda i: (i, 1)),
         _full((CONV_WIDTH, LRU_WIDTH)), vec, bd, vec, bd, vec, vec],
        [pl.BlockSpec((tm, LRU_WIDTH), lambda i: (i, 0))] * 3,
        [jax.ShapeDtypeStruct((rows, LRU_WIDTH), BF16), jax.ShapeDtypeStruct((rows, LRU_WIDTH), F32),
         jax.ShapeDtypeStruct((rows, LRU_WIDTH), F32)],
        [pltpu.VMEM((tm + SUBLANES, LRU_WIDTH), F32), pltpu.VMEM((tm, LRU_WIDTH), F32),
         pltpu.VMEM((tm, LRU_WIDTH), F32), pltpu.VMEM((SUBLANES, LRU_WIDTH), F32)],
        (zrec, zrec, conv_w, conv_b, wa_bd, b_a, wx_bd, b_x, lam), carried, modes)


def _out_proj_fwd(attn, rec, w_out, h0, g2, carried, modes):
    rows = h0.shape[0]
    tm = _row_tile(rows)

    def body(attn_ref, rec_ref, w_ref, h_ref, g_ref, mix_ref, h1_ref):
        mix = _mm(attn_ref[...], w_ref[0:ATTN_WIDTH, :]) + _mm(rec_ref[...], w_ref[ATTN_WIDTH:, :])
        y, _, _ = _rms_fwd(mix, g_ref[...])
        mix_ref[...] = mix
        h1_ref[...] = h_ref[...] + y

    half = pl.BlockSpec((tm, ATTN_WIDTH), lambda i: (i, 0))
    wide = pl.BlockSpec((tm, D_MODEL), lambda i: (i, 0))
    return _hosting_call(
        body, "out_proj_fwd", rows // tm,
        [half, half, _resident((D_MODEL, D_MODEL)), wide, _full((1, D_MODEL))],
        [wide, wide],
        [jax.ShapeDtypeStruct((rows, D_MODEL), F32)] * 2,
        [], (attn, rec, w_out, h0, g2), carried, modes)


FF_COLS = 1024
FF_HALF = FF_CHUNK // 2


def _hidden_at(d, half):
    return half * (D_FF // 2) + d * FF_HALF


def _ffn_up(h1, g3, w1_halves, carried, modes):
    rows = h1.shape[0]
    tm = _row_tile(rows)

    def body(h_ref, g_ref, wa_ref, wb_ref, act_ref, u_ref):
        u, _, _ = _rms_fwd(h_ref[...], g_ref[...])
        u = u.astype(BF16)
        u_ref[...] = u
        for half, w_ref in enumerate((wa_ref, wb_ref)):
            for d in range(N_DEV):
                c = _hidden_at(d, half)
                a1 = jnp.maximum(_mm(u, w_ref[d]), 0.0)
                act_ref[:, c:c + FF_HALF] = (a1 * a1).astype(BF16)

    wide = pl.BlockSpec((tm, D_MODEL), lambda i: (i, 0))
    return _hosting_call(
        body, "ffn_up", rows // tm,
        [wide, _full((1, D_MODEL))] + [_resident((N_DEV, D_MODEL, FF_HALF))] * 2,
        [pl.BlockSpec((tm, D_FF), lambda i: (i, 0)), wide],
        [jax.ShapeDtypeStruct((rows, D_FF), BF16), jax.ShapeDtypeStruct((rows, D_MODEL), BF16)],
        [], (h1, g3, *w1_halves), carried, modes)


def _ffn_down_loss(act, w2_halves, h1, target, g4):
    rows = h1.shape[0]
    tm = _row_tile(rows)
    steps = rows // tm
    kh = D_FF // 2

    def body(act_ref, wa_ref, wb_ref, h_ref, g_ref, t_hbm, dy_ref, df_ref, dg_ref, loss_ref, buf, sem):
        i = pl.program_id(0)
        slot = _frame_rows(t_hbm, buf, sem, i, steps, tm)

        @pl.when(i == 0)
        def _():
            dg_ref[...] = jnp.zeros_like(dg_ref)
            loss_ref[...] = jnp.zeros_like(loss_ref)
            buf[0, 0:BLOCK, :] = jnp.zeros((BLOCK, D_MODEL), F32)

        g = g_ref[...]
        f = _mm(act_ref[:, :kh], wa_ref[...]) + _mm(act_ref[:, kh:], wb_ref[...])
        y, fhat, rstd = _rms_fwd(f, g)
        grow = i * tm + lax.broadcasted_iota(jnp.int32, (tm, D_MODEL), 0)
        err = jnp.where(grow >= BLOCK, h_ref[...] + y - buf[slot], 0.0)
        loss_ref[...] += (0.5 / D_MODEL) * jnp.sum(err * err)
        dy = err * (1.0 / D_MODEL)
        df, dg = _rms_bwd(dy, fhat, rstd, g)
        dy_ref[...] = dy
        df_ref[...] = df.astype(BF16)
        dg_ref[...] += dg

    wide = pl.BlockSpec((tm, D_MODEL), lambda i: (i, 0))
    return pl.pallas_call(
        body, name="ffn_down_loss", grid=(steps,),
        in_specs=[pl.BlockSpec((tm, D_FF), lambda i: (i, 0)), _resident((kh, D_MODEL)), _resident((kh, D_MODEL)), wide,
                  _full((1, D_MODEL)), ANY_SPACE],
        out_specs=[wide, wide, _full((1, D_MODEL)), _full((SUBLANES, LANES))],
        out_shape=[jax.ShapeDtypeStruct((rows, D_MODEL), F32), jax.ShapeDtypeStruct((rows, D_MODEL), BF16),
                   jax.ShapeDtypeStruct((1, D_MODEL), F32), jax.ShapeDtypeStruct((SUBLANES, LANES), F32)],
        scratch_shapes=_frame_scratch(tm),
        compiler_params=_params(("arbitrary",)),
    )(act, *w2_halves, h1, g4, target)


def _ffn_bwd_act(df, w2t_halves, act):
    rows = df.shape[0]
    tm = _row_tile(rows)

    def body(df_ref, wa_ref, wb_ref, act_ref, da_ref):
        df_t = df_ref[...]
        for half, w_ref in enumerate((wa_ref, wb_ref)):
            for d in range(N_DEV):
                cols = slice(_hidden_at(d, half), _hidden_at(d, half) + FF_HALF)
                dact = _mm(df_t, w_ref[d])
                relu_a1, _ = _sqrt_pos(act_ref[:, cols].astype(F32))
                da_ref[:, cols] = (dact * (2.0 * relu_a1)).astype(BF16)

    hidden = pl.BlockSpec((tm, D_FF), lambda i: (i, 0))
    return pl.pallas_call(
        body, name="ffn_bwd_act", grid=(rows // tm,),
        in_specs=[pl.BlockSpec((tm, D_MODEL), lambda i: (i, 0))] + [_resident((N_DEV, D_MODEL, FF_HALF))] * 2 + [hidden],
        out_specs=hidden,
        out_shape=jax.ShapeDtypeStruct((rows, D_FF), BF16),
        compiler_params=_params(("parallel",)),
    )(df, *w2t_halves, act)


def _ffn_bwd_x(da, w1t, h1, dy, g3, carried, modes):
    rows = h1.shape[0]
    tm = _row_tile(rows)

    def body(da_ref, w_ref, h_ref, dy_ref, g_ref, dh_ref, dg_ref):
        @pl.when(pl.program_id(0) == 0)
        def _():
            dg_ref[...] = jnp.zeros_like(dg_ref)

        g = g_ref[...]
        _, xhat, rstd = _rms_fwd(h_ref[...], g)
        dx, dg = _rms_bwd(_mm(da_ref[...], w_ref[...]), xhat, rstd, g)
        dh_ref[...] = dy_ref[...] + dx
        dg_ref[...] += dg

    wide = pl.BlockSpec((tm, D_MODEL), lambda i: (i, 0))
    return _hosting_call(
        body, "ffn_bwd_x", rows // tm,
        [pl.BlockSpec((tm, D_FF), lambda i: (i, 0)), _resident((D_FF, D_MODEL)), wide, wide, _full((1, D_MODEL))],
        [wide, _full((1, D_MODEL))],
        [jax.ShapeDtypeStruct((rows, D_MODEL), F32), jax.ShapeDtypeStruct((1, D_MODEL), F32)],
        [], (da, w1t, h1, dy, g3), carried, modes)


def _ffn_bwd_weights(u2, da, act, df):
    rows = u2.shape[0]
    tb = _big_tile(rows)
    steps = rows // tb
    per = FF_COLS // FF_HALF

    def body(u_ref, da_ref, act_ref, df_ref, dw1_ref, dw2_ref, acc1, acc2):
        i = pl.program_id(1)

        @pl.when(i == 0)
        def _():
            acc1[...] = jnp.zeros_like(acc1)
            acc2[...] = jnp.zeros_like(acc2)

        acc1[...] += _mm_tn(u_ref[...], da_ref[...])
        acc2[...] += _mm_tn(act_ref[...], df_ref[...])

        @pl.when(i == steps - 1)
        def _():
            for p in range(per):
                c = p * FF_HALF
                dw1_ref[p] = acc1[:, c:c + FF_HALF].astype(BF16)
                dw2_ref[p] = acc2[c:c + FF_HALF, :].astype(BF16)

    wide = pl.BlockSpec((tb, D_MODEL), lambda j, i: (i, 0))
    chunk = pl.BlockSpec((tb, FF_COLS), lambda j, i: (i, j))
    return pl.pallas_call(
        body, name="ffn_bwd_weights", grid=(D_FF // FF_COLS, steps),
        in_specs=[wide, chunk, chunk, wide],
        out_specs=[pl.BlockSpec((None, per, D_MODEL, FF_HALF), lambda j, i: (j // 2, j % 2, 0, 0)),
                   pl.BlockSpec((per, FF_HALF, D_MODEL), lambda j, i: (j % 2, j // 2, 0))],
        out_shape=[jax.ShapeDtypeStruct((2, N_DEV, D_MODEL, FF_HALF), BF16),
                   jax.ShapeDtypeStruct((N_DEV, FF_CHUNK, D_MODEL), BF16)],
        scratch_shapes=[pltpu.VMEM((D_MODEL, FF_COLS), F32), pltpu.VMEM((FF_COLS, D_MODEL), F32)],
        compiler_params=_params(("parallel", "arbitrary")),
    )(u2, da, act, df)


def _out_proj_bwd(dh1, mix, g2, w_out_t, attn, rec, carried, modes):
    rows = dh1.shape[0]
    tm = _row_tile(rows)
    steps = rows // tm

    def body(dh_ref, mix_ref, g_ref, w_ref, attn_ref, rec_ref, dattn_ref, drec_ref, dw_ref, dg_ref, acc):
        i = pl.program_id(0)

        @pl.when(i == 0)
        def _():
            acc[...] = jnp.zeros_like(acc)
            dg_ref[...] = jnp.zeros_like(dg_ref)

        g = g_ref[...]
        _, xhat, rstd = _rms_fwd(mix_ref[...], g)
        dmix, dg = _rms_bwd(dh_ref[...], xhat, rstd, g)
        dmix = dmix.astype(BF16)
        dg_ref[...] += dg
        din = _mm(dmix, w_ref[...])
        dattn_ref[...] = din[:, :ATTN_WIDTH].astype(BF16)
        drec_ref[...] = din[:, ATTN_WIDTH:]
        acc[0:ATTN_WIDTH, :] += _mm_tn(attn_ref[...], dmix)
        acc[ATTN_WIDTH:, :] += _mm_tn(rec_ref[...], dmix)

        @pl.when(i == steps - 1)
        def _():
            dw_ref[...] = acc[...].astype(BF16)

    half = pl.BlockSpec((tm, ATTN_WIDTH), lambda i: (i, 0))
    wide = pl.BlockSpec((tm, D_MODEL), lambda i: (i, 0))
    return _hosting_call(
        body, "out_proj_bwd", steps,
        [wide, wide, _full((1, D_MODEL)), _resident((D_MODEL, D_MODEL)), half, half],
        [half, half, _full((D_MODEL, D_MODEL)), _full((1, D_MODEL))],
        [jax.ShapeDtypeStruct((rows, ATTN_WIDTH), BF16), jax.ShapeDtypeStruct((rows, LRU_WIDTH), F32),
         jax.ShapeDtypeStruct((D_MODEL, D_MODEL), BF16), jax.ShapeDtypeStruct((1, D_MODEL), F32)],
        [pltpu.VMEM((D_MODEL, D_MODEL), F32)],
        (dh1, mix, g2, w_out_t, attn, rec), carried, modes)


def _attn_bwd(qkv, dattn, sinks, bias, carried, modes):
    rows = qkv.shape[0]
    nb = rows // BLOCK
    k_col, v_col = ATTN_WIDTH // KV_WIDTH, ATTN_WIDTH // KV_WIDTH + 1

    def body(sink_ref, bias_ref, q_ref, do_ref, kp_ref, kc_ref, vp_ref, vc_ref, dq_ref, dkv_ref, dsink_ref, dk_c, dv_c):
        n = pl.program_id(0)

        @pl.when(n == 0)
        def _():
            dk_c[...] = jnp.zeros_like(dk_c)
            dv_c[...] = jnp.zeros_like(dv_c)
            dsink_ref[...] = jnp.zeros_like(dsink_ref)

        @pl.when(n < nb)
        def _():
            bias_t = bias_ref[...]
            dq_parts, dk_parts, dv_parts, dsink_rows = [], [], [], []
            for kv in range(KV_HEADS):
                sl = slice(kv * HEAD_DIM, (kv + 1) * HEAD_DIM)
                k2 = jnp.concatenate([kp_ref[:, sl], kc_ref[:, sl]], axis=0)
                v2 = jnp.concatenate([vp_ref[:, sl], vc_ref[:, sl]], axis=0)
                q4 = _heads(q_ref, kv * GQA_GROUP, GQA_GROUP)
                do4 = _heads(do_ref, kv * GQA_GROUP, GQA_GROUP)
                pn, psink = _attn_probs(k2, q4, bias_t, _sink_row(sink_ref, kv))
                dpn = _mm_nt(v2, do4)
                delta = jnp.sum(pn * dpn, axis=0, keepdims=True)
                ds = ((pn * (dpn - delta)) * (HEAD_DIM ** -0.5)).astype(BF16)
                dqt = _mm_tn(k2, ds)
                dq_parts += [dqt[:, g * BLOCK:(g + 1) * BLOCK] for g in range(GQA_GROUP)]
                dk_parts.append(_mm(ds, q4))
                dv_parts.append(_mm(pn.astype(BF16), do4))
                sd = psink * delta
                for g in range(GQA_GROUP):
                    dsink_rows.append(jnp.full((1, LANES), -jnp.sum(sd[:, g * BLOCK:(g + 1) * BLOCK]), F32))
            dq_ref[...] = _from_head_major(dq_parts).astype(BF16)
            dsink_ref[...] += jnp.concatenate(dsink_rows, axis=0)
            dk2 = jnp.concatenate(dk_parts, axis=1)
            dv2 = jnp.concatenate(dv_parts, axis=1)
            dkv_ref[:, 0:KV_WIDTH] = (dk_c[...] + dk2[0:BLOCK]).astype(BF16)
            dkv_ref[:, KV_WIDTH:] = (dv_c[...] + dv2[0:BLOCK]).astype(BF16)
            dk_c[...] = dk2[BLOCK:]
            dv_c[...] = dv2[BLOCK:]

        @pl.when(n == nb)
        def _():
            dkv_ref[:, 0:KV_WIDTH] = dk_c[...].astype(BF16)
            dkv_ref[:, KV_WIDTH:] = dv_c[...].astype(BF16)

    cur = lambda n: jnp.minimum(n, nb - 1)
    prev = lambda n: jnp.maximum(jnp.minimum(n, nb - 1) - 1, 0)
    return _hosting_call(
        body, "attn_bwd", nb + 1,
        [pl.BlockSpec(memory_space=pltpu.SMEM),
         pl.BlockSpec((None, 2 * BLOCK, GQA_GROUP * BLOCK), lambda n: (jnp.minimum(n, N_BIAS - 1), 0, 0)),
         pl.BlockSpec((BLOCK, ATTN_WIDTH), lambda n: (cur(n), 0)),
         pl.BlockSpec((BLOCK, ATTN_WIDTH), lambda n: (cur(n), 0)),
         pl.BlockSpec((BLOCK, KV_WIDTH), lambda n: (prev(n), k_col)),
         pl.BlockSpec((BLOCK, KV_WIDTH), lambda n: (cur(n), k_col)),
         pl.BlockSpec((BLOCK, KV_WIDTH), lambda n: (prev(n), v_col)),
         pl.BlockSpec((BLOCK, KV_WIDTH), lambda n: (cur(n), v_col))],
        [pl.BlockSpec((BLOCK, ATTN_WIDTH), lambda n: (cur(n), 0)),
         pl.BlockSpec((BLOCK, 2 * KV_WIDTH), lambda n: (jnp.maximum(n - 1, 0), 0)),
         _full((ATTN_HEADS, LANES))],
        [jax.ShapeDtypeStruct((rows, ATTN_WIDTH), BF16), jax.ShapeDtypeStruct((rows, 2 * KV_WIDTH), BF16),
         jax.ShapeDtypeStruct((ATTN_HEADS, LANES), F32)],
        [pltpu.VMEM((BLOCK, KV_WIDTH), F32), pltpu.VMEM((BLOCK, KV_WIDTH), F32)],
        (sinks, bias, qkv, dattn, qkv, qkv, qkv, qkv), carried, modes)


ROW_CONV_B, ROW_B_A, ROW_B_X, ROW_LAMBDA = 4, 5, 6, 7


def _rec_bwd(drec, zrec, h, xc_all, conv_w, wa_bd, b_a, wx_bd, b_x, lam, carried, modes):
    rows = zrec.shape[0]
    tm = _rec_tile(rows)
    nt = rows // tm
    per = tm // SUBLANES

    def body(drec_ref, xr_ref, yr_ref, h_ref, xc_ref, hhalo_ref, cw_ref, wa_ref, ba_ref, wx_ref, bx_ref,
             lam_ref, drz_ref, small_ref, dwa_ref, dwx_ref, hbuf, abuf, u_s, g_s, dbuf, carry):
        s = pl.program_id(0)
        i = nt - 1 - s

        @pl.when(s == 0)
        def _():
            small_ref[...] = jnp.zeros_like(small_ref)
            dwa_ref[...] = jnp.zeros_like(dwa_ref)
            dwx_ref[...] = jnp.zeros_like(dwx_ref)
            carry[...] = jnp.zeros_like(carry)
            abuf[tm:tm + SUBLANES, :] = jnp.zeros((SUBLANES, LRU_WIDTH), F32)
            dbuf[tm:tm + SUBLANES, :] = jnp.zeros((SUBLANES, LRU_WIDTH), F32)

        hbuf[0:SUBLANES, :] = jnp.where(i == 0, 0.0, hhalo_ref[...])
        hbuf[SUBLANES:SUBLANES + tm, :] = h_ref[...]

        xc = xc_ref[...]
        halves, r, ig, a, mult, inv_mult = _lru_gates(xc, wa_ref, ba_ref, wx_ref, bx_ref, lam_ref)

        yr = yr_ref[...]
        gel, t = _gelu(yr)
        drec_t = drec_ref[...]
        dyr = drec_t * h_ref[...] * _gelu_grad(yr, t)

        abuf[0:tm, :] = a
        u_s[...] = drec_t * gel
        a_next = abuf[pl.ds(1, tm), :]
        abuf[0:tm, :] = a_next
        carry[0:1, :] = _scan_tile(abuf, u_s, g_s, carry[0:1, :], tm, reverse=True)
        abuf[tm:tm + 1, :] = a[0:1, :]
        g = g_s[...]

        grow = i * tm + lax.broadcasted_iota(jnp.int32, (tm, LRU_WIDTH), 0)
        du = jnp.where(grow >= PAD_ROWS, g, 0.0)
        da = g * hbuf[pl.ds(SUBLANES - 1, tm), :]
        dmult = du * (ig * xc)
        dig = du * (mult * xc)
        dxc = du * (mult * ig)
        dlog_a = da * a - dmult * (a * a * inv_mult)
        sp = _softplus(-lam_ref[...])
        dgr = (dlog_a * (-LRU_C) * sp) * (r * (1.0 - r))
        dgi = dig * (ig * (1.0 - ig))
        dlam = jnp.sum(dlog_a * r, axis=0, keepdims=True) * (LRU_C * _sigmoid(-lam_ref[...]))
        dgr_b = [dgr[:, hh * LRU_HALF:(hh + 1) * LRU_HALF].astype(BF16) for hh in range(2)]
        dgi_b = [dgi[:, hh * LRU_HALF:(hh + 1) * LRU_HALF].astype(BF16) for hh in range(2)]
        dxc = dxc + jnp.concatenate(
            [_mm_nt(dgr_b[hh], wa_ref[hh]) + _mm_nt(dgi_b[hh], wx_ref[hh]) for hh in range(2)], axis=1)
        for hh in range(2):
            dwa_ref[hh] += _mm_tn(halves[hh], dgr_b[hh])
            dwx_ref[hh] += _mm_tn(halves[hh], dgi_b[hh])

        dbuf[0:tm, :] = dxc
        ahead = [dbuf[pl.ds(CONV_WIDTH - 1 - j, tm), :] for j in range(CONV_WIDTH)]
        dxr = sum(cw_ref[j:j + 1, :] * ahead[j] for j in range(CONV_WIDTH))
        dbuf[tm:tm + SUBLANES, :] = dxc[0:SUBLANES, :]
        drz_ref[:, 0:LRU_WIDTH] = dxr.astype(BF16)
        drz_ref[:, LRU_WIDTH:] = dyr.astype(BF16)

        xr = xr_ref[...]
        upd = [jnp.sum(xr * ahead[j], axis=0, keepdims=True) for j in range(CONV_WIDTH)]
        upd += [jnp.sum(dxc, axis=0, keepdims=True), jnp.sum(dgr, axis=0, keepdims=True),
                jnp.sum(dgi, axis=0, keepdims=True), dlam]
        small_ref[...] += jnp.concatenate(upd, axis=0)

    rev = lambda s: nt - 1 - s
    halo = lambda s: jnp.maximum(rev(s) * per - 1, 0)
    tile0 = pl.BlockSpec((tm, LRU_WIDTH), lambda s: (rev(s), 0))
    tile1 = pl.BlockSpec((tm, LRU_WIDTH), lambda s: (rev(s), 1))
    halo0 = pl.BlockSpec((SUBLANES, LRU_WIDTH), lambda s: (halo(s), 0))
    vec = _full((1, LRU_WIDTH))
    bd = _full((2, LRU_HALF, LRU_HALF))
    big = pltpu.VMEM((tm + SUBLANES, LRU_WIDTH), F32)
    tile = pltpu.VMEM((tm, LRU_WIDTH), F32)
    return _hosting_call(
        body, "rec_bwd", nt,
        [tile0, tile0, tile1, tile0, tile0, halo0, _full((CONV_WIDTH, LRU_WIDTH)), bd, vec, bd, vec, vec],
        [pl.BlockSpec((tm, 2 * LRU_WIDTH), lambda s: (rev(s), 0)), _full((SUBLANES, LRU_WIDTH)), bd, bd],
        [jax.ShapeDtypeStruct((rows, 2 * LRU_WIDTH), BF16), jax.ShapeDtypeStruct((SUBLANES, LRU_WIDTH), F32),
         jax.ShapeDtypeStruct((2, LRU_HALF, LRU_HALF), F32), jax.ShapeDtypeStruct((2, LRU_HALF, LRU_HALF), F32)],
        [big, big, tile, tile, big, pltpu.VMEM((SUBLANES, LRU_WIDTH), F32)],
        (drec, zrec, zrec, h, xc_all, h, conv_w, wa_bd, b_a, wx_bd, b_x, lam), carried, modes)


DZ_CUTS = (0, ATTN_WIDTH, QKV_WIDTH, IN_WIDTH)


def _dz_specs(tm):
    return [pl.BlockSpec((tm, DZ_CUTS[p + 1] - DZ_CUTS[p]), lambda i: (i, 0)) for p in range(3)]


def _in_proj_bwd_x(h0, g1, dh1, dq, dkv, drz, w_in_t, carried, modes):
    rows = h0.shape[0]
    tm = _row_tile(rows)

    def body(h_ref, g_ref, dh1_ref, dq_ref, dkv_ref, drz_ref, w_ref, dh0_ref, dg_ref):
        @pl.when(pl.program_id(0) == 0)
        def _():
            dg_ref[...] = jnp.zeros_like(dg_ref)

        g = g_ref[...]
        _, xhat, rstd = _rms_fwd(h_ref[...], g)
        parts = (dq_ref[...], dkv_ref[...], drz_ref[...])
        du = sum(_mm(parts[p], w_ref[DZ_CUTS[p]:DZ_CUTS[p + 1], :]) for p in range(3))
        dx, dg = _rms_bwd(du, xhat, rstd, g)
        dh0_ref[...] = dh1_ref[...] + dx
        dg_ref[...] += dg

    wide = pl.BlockSpec((tm, D_MODEL), lambda i: (i, 0))
    return _hosting_call(
        body, "in_proj_bwd_x", rows // tm,
        [wide, _full((1, D_MODEL)), wide] + _dz_specs(tm) + [_resident((IN_WIDTH, D_MODEL))],
        [wide, _full((1, D_MODEL))],
        [jax.ShapeDtypeStruct((rows, D_MODEL), F32), jax.ShapeDtypeStruct((1, D_MODEL), F32)],
        [], (h0, g1, dh1, dq, dkv, drz, w_in_t), carried, modes)


def _in_proj_bwd_w(u1, dq, dkv, drz, carried, modes):
    rows = u1.shape[0]
    tb = _big_tile(rows)

    def body(u_ref, dq_ref, dkv_ref, drz_ref, dw_ref):
        @pl.when(pl.program_id(0) == 0)
        def _():
            dw_ref[...] = jnp.zeros_like(dw_ref)

        u = u_ref[...]
        for p, ref in enumerate((dq_ref, dkv_ref, drz_ref)):
            dw_ref[:, DZ_CUTS[p]:DZ_CUTS[p + 1]] += _mm_tn(u, ref[...])

    return _hosting_call(
        body, "in_proj_bwd_w", rows // tb,
        [pl.BlockSpec((tb, D_MODEL), lambda i: (i, 0))] + _dz_specs(tb),
        [_full((D_MODEL, IN_WIDTH))],
        [jax.ShapeDtypeStruct((D_MODEL, IN_WIDTH), F32)],
        [], (u1, dq, dkv, drz), carried, modes)


def _adamw(w, m, v, parts, name):
    rows, cols = w.shape
    tr = next((t for t in (256, 128) if rows % t == 0), rows)
    parts = parts if isinstance(parts, (list, tuple)) else [parts]

    def body(w_ref, m_ref, v_ref, *refs):
        p_refs, (g_ref, d_ref, nm_ref, nv_ref) = refs[:len(parts)], refs[len(parts):]

        def total(p_ref):
            g = p_ref[0].astype(F32)
            for s in range(1, N_DEV):
                g = g + p_ref[s].astype(F32)
            return g

        g = jnp.concatenate([total(p_ref) for p_ref in p_refs], axis=1) if len(parts) > 1 else total(p_refs[0])
        nm = ADAM_B1 * m_ref[...] + (1.0 - ADAM_B1) * g
        nv = ADAM_B2 * v_ref[...] + (1.0 - ADAM_B2) * (g * g)
        m_hat = nm / (1.0 - ADAM_B1 ** ADAM_STEP)
        v_hat = nv / (1.0 - ADAM_B2 ** ADAM_STEP)
        g_ref[...] = g
        d_ref[...] = (-ADAM_LR) * (m_hat / (jnp.sqrt(v_hat) + ADAM_EPS) + ADAM_WD * w_ref[...])
        nm_ref[...] = nm
        nv_ref[...] = nv

    blk = pl.BlockSpec((tr, cols), lambda i: (i, 0))
    return pl.pallas_call(
        body, name=name, grid=(rows // tr,),
        in_specs=[blk, blk, blk] + [pl.BlockSpec((N_DEV, tr, p.shape[2]), lambda i: (0, i, 0)) for p in parts],
        out_specs=[blk] * 4,
        out_shape=[jax.ShapeDtypeStruct((rows, cols), F32)] * 4,
        compiler_params=_params(("parallel",)),
    )(w, m, v, *parts)


def _cols_from_shards(g):
    return jnp.transpose(g, (1, 0, 2)).reshape(g.shape[1], N_DEV * g.shape[2])


def _cols_to_shards(a):
    r, c = a.shape
    return jnp.transpose(a.reshape(r, N_DEV, c // N_DEV), (1, 0, 2))


def _block_diag(w):
    per = LRU_HALF // LRU_BLOCK
    w = w.reshape(2, per, LRU_BLOCK, LRU_BLOCK)
    eye = jnp.eye(per, dtype=w.dtype)
    return (w[:, :, :, None, :] * eye[None, :, None, :, None]).reshape(2, LRU_HALF, LRU_HALF)


def _block_diag_extract(t):
    per = LRU_HALF // LRU_BLOCK
    t = t.reshape(2, per, LRU_BLOCK, per, LRU_BLOCK)
    return jnp.stack([t[:, b, :, b, :] for b in range(per)], axis=1).reshape(LRU_BLOCKS, LRU_BLOCK, LRU_BLOCK)


SMALL_NAMES = ("conv_b", "w_a", "b_a", "w_x", "b_x", "lru_lambda", "attn_sinks",
               "g_post_mix", "g_pre_ffn", "g_post_ffn")


def _pack_small(vals):
    flat = []
    for name in SMALL_NAMES:
        a = vals[name].reshape(-1)
        flat.append(jnp.pad(a, (0, (-a.shape[0]) % LANES)))
    flat = jnp.concatenate(flat)
    rows = flat.shape[0] // LANES
    return jnp.pad(flat.reshape(rows, LANES), ((0, (-rows) % SUBLANES), (0, 0)))


def _unpack_small(packed, like):
    flat = packed.reshape(-1)
    out, at = {}, 0
    for name in SMALL_NAMES:
        n = like[name].size
        out[name] = flat[at:at + n].reshape(like[name].shape)
        at += n + (-n) % LANES
    return out


def kernel(x, meta_tokens, g_pre_mix, w_in, conv_w, conv_b, w_a, b_a, w_x, b_x, lru_lambda, attn_sinks, w_out, g_post_mix, g_pre_ffn, w_ff1, w_ff2, g_post_ffn, loss_target, m_meta_tokens, m_g_pre_mix, m_w_in, m_conv_w, m_conv_b, m_w_a, m_b_a, m_w_x, m_b_x, m_lru_lambda, m_attn_sinks, m_w_out, m_g_post_mix, m_g_pre_ffn, m_w_ff1, m_w_ff2, m_g_post_ffn, v_meta_tokens, v_g_pre_mix, v_w_in, v_conv_w, v_conv_b, v_w_a, v_b_a, v_w_x, v_b_x, v_lru_lambda, v_attn_sinks, v_w_out, v_g_post_mix, v_g_pre_ffn, v_w_ff1, v_w_ff2, v_g_post_ffn):
    weights = dict(meta_tokens=meta_tokens, g_pre_mix=g_pre_mix, w_in=w_in, conv_w=conv_w, conv_b=conv_b, w_a=w_a,
                   b_a=b_a, w_x=w_x, b_x=b_x, lru_lambda=lru_lambda, attn_sinks=attn_sinks, w_out=w_out,
                   g_post_mix=g_post_mix, g_pre_ffn=g_pre_ffn, w_ff1=w_ff1, w_ff2=w_ff2, g_post_ffn=g_post_ffn)
    mom_m = dict(meta_tokens=m_meta_tokens, g_pre_mix=m_g_pre_mix, w_in=m_w_in, conv_w=m_conv_w, conv_b=m_conv_b,
                 w_a=m_w_a, b_a=m_b_a, w_x=m_w_x, b_x=m_b_x, lru_lambda=m_lru_lambda, attn_sinks=m_attn_sinks,
                 w_out=m_w_out, g_post_mix=m_g_post_mix, g_pre_ffn=m_g_pre_ffn, w_ff1=m_w_ff1, w_ff2=m_w_ff2,
                 g_post_ffn=m_g_post_ffn)
    mom_v = dict(meta_tokens=v_meta_tokens, g_pre_mix=v_g_pre_mix, w_in=v_w_in, conv_w=v_conv_w, conv_b=v_conv_b,
                 w_a=v_w_a, b_a=v_b_a, w_x=v_w_x, b_x=v_b_x, lru_lambda=v_lru_lambda, attn_sinks=v_attn_sinks,
                 w_out=v_w_out, g_post_mix=v_g_post_mix, g_pre_ffn=v_g_pre_ffn, w_ff1=v_w_ff1, w_ff2=v_w_ff2,
                 g_post_ffn=v_g_post_ffn)
    order = list(weights)

    (g_win, g_meta, g_cw) = _gather_two_level([w_in[0].astype(BF16), meta_tokens, conv_w[0]], "gather_first")
    w_in_full = _cols_from_shards(g_win)
    meta_full = _cols_from_shards(g_meta)
    conv_w_full = _cols_from_shards(g_cw)

    head = jnp.concatenate([jnp.zeros((PAD_ROWS, D_MODEL), F32), meta_full], axis=0)
    wa_bd = _block_diag(w_a[0]).astype(BF16)
    wx_bd = _block_diag(w_x[0]).astype(BF16)
    bias = _attn_bias()

    w1_shard = w_ff1[0].astype(BF16)
    (qkv, zrec, u1, h0), (g_wout,) = _in_proj_fwd(head, x[0], g_pre_mix, w_in_full, [w_out[0].astype(BF16)], ["gather"])
    (attn,), (w1a,) = _attn_fwd(qkv, attn_sinks, bias, [w1_shard[:, :FF_HALF]], ["gather"])
    (rec, h_lru, xc), (w1b,) = _rec_fwd(zrec, conv_w_full, conv_b, wa_bd, b_a, wx_bd, b_x, lru_lambda,
                                         [w1_shard[:, FF_HALF:]], ["gather"])
    w_out_full = g_wout.reshape(D_MODEL, D_MODEL)
    w2_shard = w_ff2[0].astype(BF16)
    (mix, h1), (w2a,) = _out_proj_fwd(attn, rec, w_out_full, h0, g_post_mix, [w2_shard[:FF_HALF]], ["gather"])
    (act, u2), (w2b,) = _ffn_up(h1, g_pre_ffn, (w1a, w1b), [w2_shard[FF_HALF:]], ["gather"])
    w2_halves = [w.reshape(D_FF // 2, D_MODEL) for w in (w2a, w2b)]
    dy, df, dg_post_ffn, loss_acc = _ffn_down_loss(act, w2_halves, h1, loss_target[0], g_post_ffn)

    da1 = _ffn_bwd_act(df, [jnp.transpose(w, (0, 2, 1)) for w in (w2a, w2b)], act)
    dw1h, dw2g = _ffn_bwd_weights(u2, da1, act, df)
    w1t = jnp.concatenate([jnp.transpose(w, (0, 2, 1)).reshape(D_FF // 2, D_MODEL) for w in (w1a, w1b)], axis=0)
    (dh1, dg_pre_ffn), (p_w1a,) = _ffn_bwd_x(da1, w1t, h1, dy, g_pre_ffn, [dw1h[0]], ["scatter"])
    (dattn, drec, dw_out, dg_post_mix), (p_w1b,) = _out_proj_bwd(dh1, mix, g_post_mix, w_out_full.T, attn, rec,
                                                                [dw1h[1]], ["scatter"])
    (dq, dkv, dsinks), (p_w2,) = _attn_bwd(qkv, dattn, attn_sinks, bias, [dw2g], ["scatter"])
    (drz, rec_small, dwa_bd, dwx_bd), (p_wout,) = _rec_bwd(
        drec, zrec, h_lru, xc, conv_w_full, wa_bd, b_a, wx_bd, b_x, lru_lambda,
        [dw_out.reshape(N_DEV, D_MODEL // N_DEV, D_MODEL)], ["scatter"])
    small_grads = dict(
        conv_b=rec_small[ROW_CONV_B], w_a=_block_diag_extract(dwa_bd), b_a=rec_small[ROW_B_A],
        w_x=_block_diag_extract(dwx_bd), b_x=rec_small[ROW_B_X], lru_lambda=rec_small[ROW_LAMBDA],
        attn_sinks=dsinks[:, 0], g_post_mix=dg_post_mix, g_pre_ffn=dg_pre_ffn, g_post_ffn=dg_post_ffn)
    (dw_in,), (p_cw, p_small) = _in_proj_bwd_w(
        u1, dq, dkv, drz, [_cols_to_shards(rec_small[0:CONV_WIDTH]), _pack_small(small_grads)], ["scatter", "gather"])
    (dh0, dg_pre_mix), (p_win,) = _in_proj_bwd_x(
        h0, g_pre_mix, dh1, dq, dkv, drz, w_in_full.T, [_cols_to_shards(dw_in).astype(BF16)], ["scatter"])
    p_meta, p_gpm = _exchange([_cols_to_shards(dh0[PAD_ROWS:BLOCK]), dg_pre_mix], ["scatter", "gather"], "exchange_last")

    res = {}
    res["g_pre_mix"] = _adamw(g_pre_mix, m_g_pre_mix, v_g_pre_mix, p_gpm, "adamw_g_pre_mix")
    res["w_in"] = _adamw(w_in[0], m_w_in[0], v_w_in[0], p_win, "adamw_w_in")
    res["w_out"] = _adamw(w_out[0], m_w_out[0], v_w_out[0], p_wout, "adamw_w_out")
    res["w_ff1"] = _adamw(w_ff1[0], m_w_ff1[0], v_w_ff1[0], [p_w1a, p_w1b], "adamw_w_ff1")
    res["w_ff2"] = _adamw(w_ff2[0], m_w_ff2[0], v_w_ff2[0], p_w2, "adamw_w_ff2")
    res["meta_tokens"] = _adamw(meta_tokens, m_meta_tokens, v_meta_tokens, p_meta, "adamw_meta")
    res["conv_w"] = _adamw(conv_w[0], m_conv_w[0], v_conv_w[0], p_cw, "adamw_conv_w")
    small = _adamw(_pack_small(weights), _pack_small(mom_m), _pack_small(mom_v), p_small, "adamw_small")
    small = [_unpack_small(t, weights) for t in small]
    for name in SMALL_NAMES:
        res[name] = tuple(t[name] for t in small)
    for name in ("w_in", "w_out", "w_ff1", "w_ff2", "conv_w"):
        res[name] = tuple(t[None] for t in res[name])

    loss = lax.psum(loss_acc[0, 0], ("x", "y", "c"))
    grad_x = dh0[BLOCK:][None]
    outs = [loss, grad_x]
    for k in range(4):
        outs += [res[name][k] for name in order]
    return tuple(outs)
```

```python
import jax
import jax.numpy as jnp
import numpy as np
from jax import lax
from jax.experimental import pallas as pl
from jax.experimental.pallas import tpu as pltpu

F32 = jnp.float32
BF16 = jnp.bfloat16

D_MODEL = 1024
N_META = 16
HEAD_DIM = 64
ATTN_HEADS = 8
KV_HEADS = 2
GQA_GROUP = ATTN_HEADS // KV_HEADS
ATTN_WIDTH = ATTN_HEADS * HEAD_DIM
KV_WIDTH = KV_HEADS * HEAD_DIM
QKV_WIDTH = ATTN_WIDTH + 2 * KV_WIDTH
LRU_WIDTH = 512
LRU_BLOCKS = 8
LRU_BLOCK = 64
LRU_HALF = 256
LRU_C = 8.0
CONV_WIDTH = 4
BLOCK = 128
PAD_ROWS = BLOCK - N_META
IN_WIDTH = QKV_WIDTH + 2 * LRU_WIDTH
D_FF = 4096
EPS = 1e-6
NEG = -1e30
N_DEV = 8
FF_CHUNK = D_FF // N_DEV
SUBLANES = 8
LANES = 128

ADAM_LR = 0.001
ADAM_B1 = 0.9
ADAM_B2 = 0.999
ADAM_EPS = 1e-08
ADAM_WD = 0.01
ADAM_STEP = 10

VMEM_LIMIT = 56 * 1024 * 1024


def _row_tile(rows):
    for t in (640, 512, 256, 128):
        if rows % t == 0:
            return t
    raise ValueError(rows)


def _big_tile(rows):
    for t in (1664, 1024, 512, 256, 128):
        if rows % t == 0:
            return t
    raise ValueError(rows)


def _rec_tile(rows):
    for t in (320, 256, 128):
        if rows % t == 0:
            return t
    raise ValueError(rows)


def _params(semantics):
    return pltpu.CompilerParams(dimension_semantics=semantics, vmem_limit_bytes=VMEM_LIMIT)


def _mm(a, b):
    return lax.dot_general(a, b, (((1,), (0,)), ((), ())), preferred_element_type=F32)


def _mm_nt(a, b):
    return lax.dot_general(a, b, (((1,), (1,)), ((), ())), preferred_element_type=F32)


def _mm_tn(a, b):
    return lax.dot_general(a, b, (((0,), (0,)), ((), ())), preferred_element_type=F32)


def _rms_fwd(x, g):
    rstd = lax.rsqrt(jnp.mean(x * x, axis=-1, keepdims=True) + EPS)
    xhat = x * rstd
    return xhat * g, xhat, rstd


def _rms_bwd(dy, xhat, rstd, g):
    dyg = dy * g
    c = jnp.mean(dyg * xhat, axis=-1, keepdims=True)
    dx = rstd * (dyg - xhat * c)
    dg = jnp.sum(dy * xhat, axis=0, keepdims=True)
    return dx, dg


def _sigmoid(x):
    return 0.5 * jnp.tanh(0.5 * x) + 0.5


def _log1p(x):
    u = 1.0 + x
    return jnp.where(u == 1.0, x, jnp.log(u) * x / (u - 1.0))


def _one_minus_sq_exp(x, ex):
    return -jnp.tanh(x) * (1.0 + ex * ex)


TINY = 1e-30


def _sqrt_pos(y):
    r = lax.rsqrt(jnp.maximum(y, TINY))
    return y * r, r


def _softplus(x):
    return jnp.maximum(x, 0.0) + _log1p(jnp.exp(-jnp.abs(x)))


GELU_C = 0.7978845608028654
GELU_K = 0.044715


def _gelu(x):
    t = jnp.tanh(GELU_C * (x + GELU_K * x * x * x))
    return 0.5 * x * (1.0 + t), t


def _gelu_grad(x, t):
    return 0.5 * (1.0 + t) + 0.5 * x * (1.0 - t * t) * GELU_C * (1.0 + 3.0 * GELU_K * x * x)


def _full(shape):
    return pl.BlockSpec(shape, lambda *_: (0,) * len(shape))


def _resident(shape):
    return pl.BlockSpec(shape, lambda *_: (0,) * len(shape), pipeline_mode=pl.Buffered(1))


def _exchange_copies(ins, outs, sems, modes):
    send_sems, recv_sems, local_sems = sems
    x, y, c = lax.axis_index("x"), lax.axis_index("y"), lax.axis_index("c")
    me = 4 * x + 2 * y + c

    def block(a, dev):
        return ins[a] if modes[a] == "gather" else ins[a].at[dev]

    local = [pltpu.make_async_copy(block(a, me), outs[a].at[me], local_sems.at[a]) for a in range(len(ins))]
    sends, recvs = [], []
    for a in range(len(ins)):
        for k in range(N_DEV - 1):
            bits = k + 1
            px = jnp.bitwise_xor(x, (bits >> 2) & 1)
            py = jnp.bitwise_xor(y, (bits >> 1) & 1)
            pc = jnp.bitwise_xor(c, bits & 1)
            peer = 4 * px + 2 * py + pc
            common = dict(src_ref=block(a, peer), send_sem=send_sems.at[a, k], recv_sem=recv_sems.at[a, k],
                          device_id=(px, py, pc), device_id_type=pl.DeviceIdType.MESH)
            sends.append(pltpu.make_async_remote_copy(dst_ref=outs[a].at[me], **common))
            recvs.append(pltpu.make_async_remote_copy(dst_ref=outs[a].at[peer], **common))
    return local, sends, recvs


def _exchange_start(ins, outs, sems, modes):
    local, sends, _ = _exchange_copies(ins, outs, sems, modes)
    for cp in local + sends:
        cp.start()


def _exchange_wait(ins, outs, sems, modes):
    local, sends, recvs = _exchange_copies(ins, outs, sems, modes)
    for cp in recvs:
        cp.wait_recv()
    for cp in sends:
        cp.wait_send()
    for cp in local:
        cp.wait()


def _exchange_shapes(arrays, modes):
    return [jax.ShapeDtypeStruct((N_DEV,) + a.shape if mode == "gather" else a.shape, a.dtype)
            for a, mode in zip(arrays, modes)]


def _exchange_sems(na):
    return [pltpu.SemaphoreType.DMA((na, N_DEV - 1)), pltpu.SemaphoreType.DMA((na, N_DEV - 1)),
            pltpu.SemaphoreType.DMA((na,))]


ANY_SPACE = pl.BlockSpec(memory_space=pl.ANY)


def _exchange(arrays, modes, name):
    na = len(arrays)

    def body(*refs):
        ins, outs, sems = refs[:na], refs[na:2 * na], refs[2 * na:]
        _exchange_start(ins, outs, sems, modes)
        _exchange_wait(ins, outs, sems, modes)

    return pl.pallas_call(
        body, name=name, out_shape=_exchange_shapes(arrays, modes),
        in_specs=[ANY_SPACE] * na, out_specs=[ANY_SPACE] * na, scratch_shapes=_exchange_sems(na),
        compiler_params=pltpu.CompilerParams(has_side_effects=True),
    )(*arrays)


def _gather_two_level(arrays, name):
    na = len(arrays)

    def body(*refs):
        ins, outs = refs[:na], refs[na:2 * na]
        send_sems, recv_sems, local_sems = refs[2 * na:]
        x, y, c = lax.axis_index("x"), lax.axis_index("y"), lax.axis_index("c")
        me, sibling = (x, y, c), (x, y, 1 - c)
        chips = [(1 - x, y), (x, 1 - y), (1 - x, 1 - y)]

        def copy(a, k, block, to, src=None):
            slot = outs[a].at[4 * block[0] + 2 * block[1] + block[2]]
            return pltpu.make_async_remote_copy(
                src_ref=slot if src is None else src, dst_ref=slot, send_sem=send_sems.at[a, k],
                recv_sem=recv_sems.at[a, k], device_id=to, device_id_type=pl.DeviceIdType.MESH)

        local = [pltpu.make_async_copy(ins[a], outs[a].at[4 * x + 2 * y + c], local_sems.at[a]) for a in range(na)]
        first = []
        for a in range(na):
            first.append(copy(a, 0, me, sibling, src=ins[a]))
            first += [copy(a, 1 + j, me, (*chip, c), src=ins[a]) for j, chip in enumerate(chips)]
        for cp in local + first:
            cp.start()
        passed = []
        for j, chip in enumerate(chips):
            for a in range(na):
                copy(a, 1 + j, (*chip, c), me).wait_recv()
                passed.append(copy(a, 4 + j, (*chip, c), sibling))
                passed[-1].start()
        for a in range(na):
            copy(a, 0, sibling, me).wait_recv()
            for j, chip in enumerate(chips):
                copy(a, 4 + j, (*chip, 1 - c), me).wait_recv()
        for cp in first + passed:
            cp.wait_send()
        for cp in local:
            cp.wait()

    return pl.pallas_call(
        body, name=name, out_shape=_exchange_shapes(arrays, ["gather"] * na),
        in_specs=[ANY_SPACE] * na, out_specs=[ANY_SPACE] * na, scratch_shapes=_exchange_sems(na),
        compiler_params=pltpu.CompilerParams(has_side_effects=True),
    )(*arrays)


def _hosting_call(body, name, steps, in_specs, out_specs, out_shape, scratch_shapes, args, arrays, modes):
    n_in, n_out, n_scr, na = len(in_specs), len(out_specs), len(scratch_shapes), len(arrays)

    def hosting_body(*refs):
        cuts = [0]
        for n in (n_in, na, n_out, na, n_scr, 3):
            cuts.append(cuts[-1] + n)
        ins, x_ins, outs, x_outs, scr, sems = (refs[cuts[p]:cuts[p + 1]] for p in range(6))
        step = pl.program_id(0)

        @pl.when(step == 0)
        def _():
            _exchange_start(x_ins, x_outs, sems, modes)

        body(*ins, *outs, *scr)

        @pl.when(step == steps - 1)
        def _():
            _exchange_wait(x_ins, x_outs, sems, modes)

    res = pl.pallas_call(
        hosting_body, name=name, grid=(steps,),
        in_specs=list(in_specs) + [ANY_SPACE] * na, out_specs=list(out_specs) + [ANY_SPACE] * na,
        out_shape=list(out_shape) + _exchange_shapes(arrays, modes),
        scratch_shapes=list(scratch_shapes) + _exchange_sems(na),
        compiler_params=_params(("arbitrary",)),
    )(*args, *arrays)
    return res[:n_out], res[n_out:]


def _frame_rows(src_hbm, buf, sem, i, steps, tm):
    def first():
        return pltpu.make_async_copy(src_hbm.at[pl.ds(0, tm - BLOCK)], buf.at[0, pl.ds(BLOCK, tm - BLOCK)], sem.at[0])

    def later(t, slot):
        return pltpu.make_async_copy(src_hbm.at[pl.ds(pl.multiple_of(t * tm - BLOCK, BLOCK), tm)], buf.at[slot], sem.at[slot])

    slot = i % 2

    @pl.when(i == 0)
    def _():
        first().start()

    @pl.when(i + 1 < steps)
    def _():
        later(i + 1, 1 - slot).start()

    @pl.when(i == 0)
    def _():
        first().wait()

    @pl.when(i > 0)
    def _():
        later(i, slot).wait()

    return slot


def _frame_scratch(tm):
    return [pltpu.VMEM((2, tm, D_MODEL), F32), pltpu.SemaphoreType.DMA((2,))]


def _in_proj_fwd(head, x, g1, w_in, carried, modes):
    rows = BLOCK + x.shape[0]
    tm = _row_tile(rows)
    steps = rows // tm

    def body(head_ref, g_ref, w_ref, x_hbm, qkv_ref, zrec_ref, u_ref, h_ref, buf, sem):
        i = pl.program_id(0)
        slot = _frame_rows(x_hbm, buf, sem, i, steps, tm)

        @pl.when(i == 0)
        def _():
            buf[0, 0:BLOCK, :] = head_ref[...]

        h = buf[slot]
        h_ref[...] = h
        u, _, _ = _rms_fwd(h, g_ref[...])
        u = u.astype(BF16)
        u_ref[...] = u
        z = _mm(u, w_ref[...])
        qkv_ref[...] = z[:, :QKV_WIDTH].astype(BF16)
        zrec_ref[...] = z[:, QKV_WIDTH:]

    wide = pl.BlockSpec((tm, D_MODEL), lambda i: (i, 0))
    return _hosting_call(
        body, "in_proj_fwd", steps,
        [_full((BLOCK, D_MODEL)), _full((1, D_MODEL)), _resident((D_MODEL, IN_WIDTH)), ANY_SPACE],
        [pl.BlockSpec((tm, QKV_WIDTH), lambda i: (i, 0)), pl.BlockSpec((tm, 2 * LRU_WIDTH), lambda i: (i, 0)), wide, wide],
        [jax.ShapeDtypeStruct((rows, QKV_WIDTH), BF16), jax.ShapeDtypeStruct((rows, 2 * LRU_WIDTH), F32),
         jax.ShapeDtypeStruct((rows, D_MODEL), BF16), jax.ShapeDtypeStruct((rows, D_MODEL), F32)],
        _frame_scratch(tm), (head, g1, w_in, x), carried, modes)


N_BIAS = 3


def _attn_bias():
    key = np.arange(2 * BLOCK)[:, None]
    r = np.arange(GQA_GROUP * BLOCK)[None, :] % BLOCK
    band = (key > r) & (key <= r + BLOCK)
    out = [np.where(band & ((n - 1) * BLOCK + key >= PAD_ROWS), 0.0, NEG) for n in range(N_BIAS)]
    return jnp.asarray(np.stack(out), F32)


def _attn_probs(k2, q4, bias, sink_row):
    s = _mm_nt(k2, q4) * (HEAD_DIM ** -0.5) + bias
    m = jnp.maximum(jnp.max(s, axis=0, keepdims=True), sink_row)
    p = jnp.exp(s - m)
    es = jnp.exp(sink_row - m)
    inv = 1.0 / (jnp.sum(p, axis=0, keepdims=True) + es)
    return p * inv, es * inv


def _heads(ref, rows, first, count):
    return jnp.concatenate([ref[rows, (first + g) * HEAD_DIM:(first + g + 1) * HEAD_DIM] for g in range(count)], axis=0)


def _keys_of_block(prev_ref, cur_ref, b, kv):
    sl = slice(kv * HEAD_DIM, (kv + 1) * HEAD_DIM)
    before = prev_ref[:, sl] if b == 0 else cur_ref[(b - 1) * BLOCK:b * BLOCK, sl]
    return jnp.concatenate([before, cur_ref[b * BLOCK:(b + 1) * BLOCK, sl]], axis=0)


def _bias_of_block(bias_ref, block):
    return bias_ref[jnp.minimum(block, N_BIAS - 1)]


def _sink_row(sink_ref, kv):
    g = lax.broadcasted_iota(jnp.int32, (1, GQA_GROUP * BLOCK), 1) // BLOCK
    row = jnp.full((1, GQA_GROUP * BLOCK), sink_ref[0, kv * GQA_GROUP], F32)
    for i in range(1, GQA_GROUP):
        row = jnp.where(g == i, sink_ref[0, kv * GQA_GROUP + i], row)
    return row


def _from_head_major(pieces):
    return jnp.concatenate(pieces, axis=0).T


def _attn_specs(tm, tile_of):
    nbt = tm // BLOCK
    k_col, v_col = ATTN_WIDTH // KV_WIDTH, ATTN_WIDTH // KV_WIDTH + 1
    before = lambda i: jnp.maximum(tile_of(i) * nbt - 1, 0)
    return [pl.BlockSpec((tm, ATTN_WIDTH), lambda i: (tile_of(i), 0)),
            pl.BlockSpec((BLOCK, KV_WIDTH), lambda i: (before(i), k_col)),
            pl.BlockSpec((tm, KV_WIDTH), lambda i: (tile_of(i), k_col)),
            pl.BlockSpec((BLOCK, KV_WIDTH), lambda i: (before(i), v_col)),
            pl.BlockSpec((tm, KV_WIDTH), lambda i: (tile_of(i), v_col))]


def _attn_fwd(qkv, sinks, bias, carried, modes):
    rows = qkv.shape[0]
    tm = _row_tile(rows)
    nbt = tm // BLOCK

    def body(sink_ref, bias_ref, q_ref, kp_ref, kc_ref, vp_ref, vc_ref, o_ref):
        i = pl.program_id(0)
        for b in range(nbt):
            blk = slice(b * BLOCK, (b + 1) * BLOCK)
            bias_t = _bias_of_block(bias_ref, i * nbt + b)
            pieces = []
            for kv in range(KV_HEADS):
                k2 = _keys_of_block(kp_ref, kc_ref, b, kv)
                v2 = _keys_of_block(vp_ref, vc_ref, b, kv)
                q4 = _heads(q_ref, blk, kv * GQA_GROUP, GQA_GROUP)
                pn, _ = _attn_probs(k2, q4, bias_t, _sink_row(sink_ref, kv))
                ot = _mm_tn(v2, pn.astype(BF16))
                pieces += [ot[:, g * BLOCK:(g + 1) * BLOCK] for g in range(GQA_GROUP)]
            o_ref[blk, :] = _from_head_major(pieces).astype(BF16)

    return _hosting_call(
        body, "attn_fwd", rows // tm,
        [pl.BlockSpec(memory_space=pltpu.SMEM), _resident((N_BIAS, 2 * BLOCK, GQA_GROUP * BLOCK))]
        + _attn_specs(tm, lambda i: i),
        [pl.BlockSpec((tm, ATTN_WIDTH), lambda i: (i, 0))],
        [jax.ShapeDtypeStruct((rows, ATTN_WIDTH), BF16)],
        [], (sinks, bias, qkv, qkv, qkv, qkv, qkv), carried, modes)


def _conv_taps(xbuf, tm):
    return [xbuf[pl.ds(SUBLANES - (CONV_WIDTH - 1 - j), tm), :] for j in range(CONV_WIDTH)]


def _lru_gates(xc, wa_ref, ba_ref, wx_ref, bx_ref, lam_ref):
    halves = [xc[:, h * LRU_HALF:(h + 1) * LRU_HALF].astype(BF16) for h in range(2)]
    gate_r = jnp.concatenate([_mm(halves[h], wa_ref[h]) for h in range(2)], axis=1) + ba_ref[...]
    gate_i = jnp.concatenate([_mm(halves[h], wx_ref[h]) for h in range(2)], axis=1) + bx_ref[...]
    r = _sigmoid(gate_r)
    ig = _sigmoid(gate_i)
    log_a = (-LRU_C) * r * _softplus(-lam_ref[...])
    a = jnp.exp(log_a)
    mult, inv_mult = _sqrt_pos(_one_minus_sq_exp(log_a, a))
    return halves, r, ig, a, mult, inv_mult


def _scan_tile(a_ref, u_ref, out_ref, carry, tm, reverse):
    row = lax.broadcasted_iota(jnp.int32, (SUBLANES, LRU_WIDTH), 0)
    groups = tm // SUBLANES

    def step(j, prev):
        jj = groups - 1 - j if reverse else j
        o = pl.multiple_of(jj * SUBLANES, SUBLANES)
        a = a_ref[pl.ds(o, SUBLANES), :]
        u = u_ref[pl.ds(o, SUBLANES), :]
        for s in (1, 2, 4):
            shift = SUBLANES - s if reverse else s
            keep = (row < SUBLANES - s) if reverse else (row >= s)
            u = jnp.where(keep, a * pltpu.roll(u, shift, 0) + u, u)
            a = jnp.where(keep, a * pltpu.roll(a, shift, 0), a)
        out = a * prev + u
        out_ref[pl.ds(o, SUBLANES), :] = out
        return out[0:1, :] if reverse else out[SUBLANES - 1:SUBLANES, :]

    return lax.fori_loop(0, groups, step, carry)


def _rec_fwd(zrec, conv_w, conv_b, wa_bd, b_a, wx_bd, b_x, lam, carried, modes):
    rows = zrec.shape[0]
    tm = _rec_tile(rows)

    def body(xr_ref, yr_ref, cw_ref, cb_ref, wa_ref, ba_ref, wx_ref, bx_ref, lam_ref, rec_ref, h_ref, xc_ref,
             xbuf, a_s, u_s, carry):
        i = pl.program_id(0)

        @pl.when(i == 0)
        def _():
            xbuf[0:SUBLANES, :] = jnp.zeros((SUBLANES, LRU_WIDTH), F32)
            carry[...] = jnp.zeros_like(carry)

        @pl.when(i > 0)
        def _():
            xbuf[0:SUBLANES, :] = xbuf[tm:tm + SUBLANES, :]

        xbuf[SUBLANES:SUBLANES + tm, :] = xr_ref[...]
        taps = _conv_taps(xbuf, tm)
        xc = cb_ref[...] + sum(cw_ref[j:j + 1, :] * taps[j] for j in range(CONV_WIDTH))
        xc_ref[...] = xc
        _, r, ig, a, mult, _ = _lru_gates(xc, wa_ref, ba_ref, wx_ref, bx_ref, lam_ref)
        grow = i * tm + lax.broadcasted_iota(jnp.int32, (tm, LRU_WIDTH), 0)
        a_s[...] = a
        u_s[...] = jnp.where(grow >= PAD_ROWS, mult * (ig * xc), 0.0)
        carry[0:1, :] = _scan_tile(a_s, u_s, h_ref, carry[0:1, :], tm, reverse=False)
        gel, _ = _gelu(yr_ref[...])
        rec_ref[...] = (gel * h_ref[...]).astype(BF16)

    vec = _full((1, LRU_WIDTH))
    bd = _full((2, LRU_HALF, LRU_HALF))
    return _hosting_call(
        body, "rec_fwd", rows // tm,
        [pl.BlockSpec((tm, LRU_WIDTH), lambda i: (i, 0)), pl.BlockSpec((tm, LRU_WIDTH), lambda i: (i, 1)),
         _full((CONV_WIDTH, LRU_WIDTH)), vec, bd, vec, bd, vec, vec],
        [pl.BlockSpec((tm, LRU_WIDTH), lambda i: (i, 0))] * 3,
        [jax.ShapeDtypeStruct((rows, LRU_WIDTH), BF16), jax.ShapeDtypeStruct((rows, LRU_WIDTH), F32),
         jax.ShapeDtypeStruct((rows, LRU_WIDTH), F32)],
        [pltpu.VMEM((tm + SUBLANES, LRU_WIDTH), F32), pltpu.VMEM((tm, LRU_WIDTH), F32),
         pltpu.VMEM((tm, LRU_WIDTH), F32), pltpu.VMEM((SUBLANES, LRU_WIDTH), F32)],
        (zrec, zrec, conv_w, conv_b, wa_bd, b_a, wx_bd, b_x, lam), carried, modes)


def _out_proj_fwd(attn, rec, w_out, h0, g2, carried, modes):
    rows = h0.shape[0]
    tm = _row_tile(rows)

    def body(attn_ref, rec_ref, w_ref, h_ref, g_ref, mix_ref, h1_ref):
        mix = _mm(attn_ref[...], w_ref[0:ATTN_WIDTH, :]) + _mm(rec_ref[...], w_ref[ATTN_WIDTH:, :])
        y, _, _ = _rms_fwd(mix, g_ref[...])
        mix_ref[...] = mix
        h1_ref[...] = h_ref[...] + y

    half = pl.BlockSpec((tm, ATTN_WIDTH), lambda i: (i, 0))
    wide = pl.BlockSpec((tm, D_MODEL), lambda i: (i, 0))
    return _hosting_call(
        body, "out_proj_fwd", rows // tm,
        [half, half, _resident((D_MODEL, D_MODEL)), wide, _full((1, D_MODEL))],
        [wide, wide],
        [jax.ShapeDtypeStruct((rows, D_MODEL), F32)] * 2,
        [], (attn, rec, w_out, h0, g2), carried, modes)


FF_COLS = 1024
FF_HALF = FF_CHUNK // 2


def _hidden_at(d, half):
    return half * (D_FF // 2) + d * FF_HALF


def _ffn_up(h1, g3, w1_halves, carried, modes):
    rows = h1.shape[0]
    tm = _row_tile(rows)

    def body(h_ref, g_ref, wa_ref, wb_ref, act_ref, u_ref):
        u, _, _ = _rms_fwd(h_ref[...], g_ref[...])
        u = u.astype(BF16)
        u_ref[...] = u
        for half, w_ref in enumerate((wa_ref, wb_ref)):
            for d in range(N_DEV):
                c = _hidden_at(d, half)
                a1 = jnp.maximum(_mm(u, w_ref[d]), 0.0)
                act_ref[:, c:c + FF_HALF] = (a1 * a1).astype(BF16)

    wide = pl.BlockSpec((tm, D_MODEL), lambda i: (i, 0))
    return _hosting_call(
        body, "ffn_up", rows // tm,
        [wide, _full((1, D_MODEL))] + [_resident((N_DEV, D_MODEL, FF_HALF))] * 2,
        [pl.BlockSpec((tm, D_FF), lambda i: (i, 0)), wide],
        [jax.ShapeDtypeStruct((rows, D_FF), BF16), jax.ShapeDtypeStruct((rows, D_MODEL), BF16)],
        [], (h1, g3, *w1_halves), carried, modes)


def _ffn_down_loss(act, w2_halves, h1, target, g4):
    rows = h1.shape[0]
    tm = _row_tile(rows)
    steps = rows // tm
    kh = D_FF // 2

    def body(act_ref, wa_ref, wb_ref, h_ref, g_ref, t_hbm, dy_ref, df_ref, dg_ref, loss_ref, buf, sem):
        i = pl.program_id(0)
        slot = _frame_rows(t_hbm, buf, sem, i, steps, tm)

        @pl.when(i == 0)
        def _():
            dg_ref[...] = jnp.zeros_like(dg_ref)
            loss_ref[...] = jnp.zeros_like(loss_ref)
            buf[0, 0:BLOCK, :] = jnp.zeros((BLOCK, D_MODEL), F32)

        g = g_ref[...]
        f = _mm(act_ref[:, :kh], wa_ref[...]) + _mm(act_ref[:, kh:], wb_ref[...])
        y, fhat, rstd = _rms_fwd(f, g)
        grow = i * tm + lax.broadcasted_iota(jnp.int32, (tm, D_MODEL), 0)
        err = jnp.where(grow >= BLOCK, h_ref[...] + y - buf[slot], 0.0)
        loss_ref[...] += (0.5 / D_MODEL) * jnp.sum(err * err)
        dy = err * (1.0 / D_MODEL)
        df, dg = _rms_bwd(dy, fhat, rstd, g)
        dy_ref[...] = dy
        df_ref[...] = df.astype(BF16)
        dg_ref[...] += dg

    wide = pl.BlockSpec((tm, D_MODEL), lambda i: (i, 0))
    return pl.pallas_call(
        body, name="ffn_down_loss", grid=(steps,),
        in_specs=[pl.BlockSpec((tm, D_FF), lambda i: (i, 0)), _resident((kh, D_MODEL)), _resident((kh, D_MODEL)), wide,
                  _full((1, D_MODEL)), ANY_SPACE],
        out_specs=[wide, wide, _full((1, D_MODEL)), _full((SUBLANES, LANES))],
        out_shape=[jax.ShapeDtypeStruct((rows, D_MODEL), F32), jax.ShapeDtypeStruct((rows, D_MODEL), BF16),
                   jax.ShapeDtypeStruct((1, D_MODEL), F32), jax.ShapeDtypeStruct((SUBLANES, LANES), F32)],
        scratch_shapes=_frame_scratch(tm),
        compiler_params=_params(("arbitrary",)),
    )(act, *w2_halves, h1, g4, target)


def _ffn_bwd_act(df, w2t_halves, act):
    rows = df.shape[0]
    tm = _row_tile(rows)

    def body(df_ref, wa_ref, wb_ref, act_ref, da_ref):
        df_t = df_ref[...]
        for half, w_ref in enumerate((wa_ref, wb_ref)):
            for d in range(N_DEV):
                cols = slice(_hidden_at(d, half), _hidden_at(d, half) + FF_HALF)
                dact = _mm(df_t, w_ref[d])
                relu_a1, _ = _sqrt_pos(act_ref[:, cols].astype(F32))
                da_ref[:, cols] = (dact * (2.0 * relu_a1)).astype(BF16)

    hidden = pl.BlockSpec((tm, D_FF), lambda i: (i, 0))
    return pl.pallas_call(
        body, name="ffn_bwd_act", grid=(rows // tm,),
        in_specs=[pl.BlockSpec((tm, D_MODEL), lambda i: (i, 0))] + [_resident((N_DEV, D_MODEL, FF_HALF))] * 2 + [hidden],
        out_specs=hidden,
        out_shape=jax.ShapeDtypeStruct((rows, D_FF), BF16),
        compiler_params=_params(("parallel",)),
    )(df, *w2t_halves, act)


def _ffn_bwd_x(da, w1t, h1, dy, g3, carried, modes):
    rows = h1.shape[0]
    tm = _row_tile(rows)

    def body(da_ref, w_ref, h_ref, dy_ref, g_ref, dh_ref, dg_ref):
        @pl.when(pl.program_id(0) == 0)
        def _():
            dg_ref[...] = jnp.zeros_like(dg_ref)

        g = g_ref[...]
        _, xhat, rstd = _rms_fwd(h_ref[...], g)
        dx, dg = _rms_bwd(_mm(da_ref[...], w_ref[...]), xhat, rstd, g)
        dh_ref[...] = dy_ref[...] + dx
        dg_ref[...] += dg

    wide = pl.BlockSpec((tm, D_MODEL), lambda i: (i, 0))
    return _hosting_call(
        body, "ffn_bwd_x", rows // tm,
        [pl.BlockSpec((tm, D_FF), lambda i: (i, 0)), _resident((D_FF, D_MODEL)), wide, wide, _full((1, D_MODEL))],
        [wide, _full((1, D_MODEL))],
        [jax.ShapeDtypeStruct((rows, D_MODEL), F32), jax.ShapeDtypeStruct((1, D_MODEL), F32)],
        [], (da, w1t, h1, dy, g3), carried, modes)


def _ffn_bwd_weights(u2, da, act, df):
    rows = u2.shape[0]
    tb = _big_tile(rows)
    steps = rows // tb
    per = FF_COLS // FF_HALF

    def body(u_ref, da_ref, act_ref, df_ref, dw1_ref, dw2_ref, acc1, acc2):
        i = pl.program_id(1)

        @pl.when(i == 0)
        def _():
            acc1[...] = jnp.zeros_like(acc1)
            acc2[...] = jnp.zeros_like(acc2)

        acc1[...] += _mm_tn(u_ref[...], da_ref[...])
        acc2[...] += _mm_tn(act_ref[...], df_ref[...])

        @pl.when(i == steps - 1)
        def _():
            for p in range(per):
                c = p * FF_HALF
                dw1_ref[p] = acc1[:, c:c + FF_HALF].astype(BF16)
                dw2_ref[p] = acc2[c:c + FF_HALF, :].astype(BF16)

    wide = pl.BlockSpec((tb, D_MODEL), lambda j, i: (i, 0))
    chunk = pl.BlockSpec((tb, FF_COLS), lambda j, i: (i, j))
    return pl.pallas_call(
        body, name="ffn_bwd_weights", grid=(D_FF // FF_COLS, steps),
        in_specs=[wide, chunk, chunk, wide],
        out_specs=[pl.BlockSpec((None, per, D_MODEL, FF_HALF), lambda j, i: (j // 2, j % 2, 0, 0)),
                   pl.BlockSpec((per, FF_HALF, D_MODEL), lambda j, i: (j % 2, j // 2, 0))],
        out_shape=[jax.ShapeDtypeStruct((2, N_DEV, D_MODEL, FF_HALF), BF16),
                   jax.ShapeDtypeStruct((N_DEV, FF_CHUNK, D_MODEL), BF16)],
        scratch_shapes=[pltpu.VMEM((D_MODEL, FF_COLS), F32), pltpu.VMEM((FF_COLS, D_MODEL), F32)],
        compiler_params=_params(("parallel", "arbitrary")),
    )(u2, da, act, df)


def _out_proj_bwd(dh1, mix, g2, w_out_t, attn, rec, carried, modes):
    rows = dh1.shape[0]
    tm = _row_tile(rows)
    steps = rows // tm

    def body(dh_ref, mix_ref, g_ref, w_ref, attn_ref, rec_ref, dattn_ref, drec_ref, dw_ref, dg_ref, acc):
        i = pl.program_id(0)

        @pl.when(i == 0)
        def _():
            acc[...] = jnp.zeros_like(acc)
            dg_ref[...] = jnp.zeros_like(dg_ref)

        g = g_ref[...]
        _, xhat, rstd = _rms_fwd(mix_ref[...], g)
        dmix, dg = _rms_bwd(dh_ref[...], xhat, rstd, g)
        dmix = dmix.astype(BF16)
        dg_ref[...] += dg
        din = _mm(dmix, w_ref[...])
        dattn_ref[...] = din[:, :ATTN_WIDTH].astype(BF16)
        drec_ref[...] = din[:, ATTN_WIDTH:]
        acc[0:ATTN_WIDTH, :] += _mm_tn(attn_ref[...], dmix)
        acc[ATTN_WIDTH:, :] += _mm_tn(rec_ref[...], dmix)

        @pl.when(i == steps - 1)
        def _():
            dw_ref[...] = acc[...].astype(BF16)

    half = pl.BlockSpec((tm, ATTN_WIDTH), lambda i: (i, 0))
    wide = pl.BlockSpec((tm, D_MODEL), lambda i: (i, 0))
    return _hosting_call(
        body, "out_proj_bwd", steps,
        [wide, wide, _full((1, D_MODEL)), _resident((D_MODEL, D_MODEL)), half, half],
        [half, half, _full((D_MODEL, D_MODEL)), _full((1, D_MODEL))],
        [jax.ShapeDtypeStruct((rows, ATTN_WIDTH), BF16), jax.ShapeDtypeStruct((rows, LRU_WIDTH), F32),
         jax.ShapeDtypeStruct((D_MODEL, D_MODEL), BF16), jax.ShapeDtypeStruct((1, D_MODEL), F32)],
        [pltpu.VMEM((D_MODEL, D_MODEL), F32)],
        (dh1, mix, g2, w_out_t, attn, rec), carried, modes)


def _attn_bwd(qkv, dattn, sinks, bias, carried, modes):
    rows = qkv.shape[0]
    tm = _row_tile(rows)
    nbt, nt = tm // BLOCK, rows // tm

    def body(sink_ref, bias_ref, do_ref, q_ref, kp_ref, kc_ref, vp_ref, vc_ref, dq_ref, dkv_ref, dsink_ref, dk_c, dv_c):
        i = pl.program_id(0)

        @pl.when(i == 0)
        def _():
            dk_c[...] = jnp.zeros_like(dk_c)
            dv_c[...] = jnp.zeros_like(dv_c)
            dsink_ref[...] = jnp.zeros_like(dsink_ref)

        @pl.when(i < nt)
        def _():
            dk_late, dv_late = dk_c[...], dv_c[...]
            dsink_rows = [jnp.zeros((1, LANES), F32)] * ATTN_HEADS
            for b in range(nbt):
                blk = slice(b * BLOCK, (b + 1) * BLOCK)
                bias_t = _bias_of_block(bias_ref, i * nbt + b)
                dq_parts, dk_parts, dv_parts = [], [], []
                for kv in range(KV_HEADS):
                    k2 = _keys_of_block(kp_ref, kc_ref, b, kv)
                    v2 = _keys_of_block(vp_ref, vc_ref, b, kv)
                    q4 = _heads(q_ref, blk, kv * GQA_GROUP, GQA_GROUP)
                    do4 = _heads(do_ref, blk, kv * GQA_GROUP, GQA_GROUP)
                    pn, psink = _attn_probs(k2, q4, bias_t, _sink_row(sink_ref, kv))
                    dpn = _mm_nt(v2, do4)
                    delta = jnp.sum(pn * dpn, axis=0, keepdims=True)
                    ds = ((pn * (dpn - delta)) * (HEAD_DIM ** -0.5)).astype(BF16)
                    dqt = _mm_tn(k2, ds)
                    dq_parts += [dqt[:, g * BLOCK:(g + 1) * BLOCK] for g in range(GQA_GROUP)]
                    dk_parts.append(_mm(ds, q4))
                    dv_parts.append(_mm(pn.astype(BF16), do4))
                    sd = psink * delta
                    for g in range(GQA_GROUP):
                        h = kv * GQA_GROUP + g
                        dsink_rows[h] = dsink_rows[h] - jnp.sum(sd[:, g * BLOCK:(g + 1) * BLOCK])
                dq_ref[blk, :] = _from_head_major(dq_parts).astype(BF16)
                dk2 = jnp.concatenate(dk_parts, axis=1)
                dv2 = jnp.concatenate(dv_parts, axis=1)
                dkv_ref[blk, 0:KV_WIDTH] = (dk_late + dk2[0:BLOCK]).astype(BF16)
                dkv_ref[blk, KV_WIDTH:] = (dv_late + dv2[0:BLOCK]).astype(BF16)
                dk_late, dv_late = dk2[BLOCK:], dv2[BLOCK:]
            dk_c[...] = dk_late
            dv_c[...] = dv_late
            dsink_ref[...] += jnp.concatenate(dsink_rows, axis=0)

        @pl.when(i == nt)
        def _():
            dkv_ref[...] = jnp.zeros_like(dkv_ref)
            dkv_ref[0:BLOCK, 0:KV_WIDTH] = dk_c[...].astype(BF16)
            dkv_ref[0:BLOCK, KV_WIDTH:] = dv_c[...].astype(BF16)

    tile_of = lambda i: jnp.minimum(i, nt - 1)
    tile = pl.BlockSpec((tm, ATTN_WIDTH), lambda i: (tile_of(i), 0))
    return _hosting_call(
        body, "attn_bwd", nt + 1,
        [pl.BlockSpec(memory_space=pltpu.SMEM), _resident((N_BIAS, 2 * BLOCK, GQA_GROUP * BLOCK)), tile]
        + _attn_specs(tm, tile_of),
        [tile, pl.BlockSpec((tm, 2 * KV_WIDTH), lambda i: (i, 0)), _full((ATTN_HEADS, LANES))],
        [jax.ShapeDtypeStruct((rows, ATTN_WIDTH), BF16), jax.ShapeDtypeStruct((rows + tm, 2 * KV_WIDTH), BF16),
         jax.ShapeDtypeStruct((ATTN_HEADS, LANES), F32)],
        [pltpu.VMEM((BLOCK, KV_WIDTH), F32), pltpu.VMEM((BLOCK, KV_WIDTH), F32)],
        (sinks, bias, dattn, qkv, qkv, qkv, qkv, qkv), carried, modes)


ROW_CONV_B, ROW_B_A, ROW_B_X, ROW_LAMBDA = 4, 5, 6, 7


def _rec_bwd(drec, zrec, h, xc_all, conv_w, wa_bd, b_a, wx_bd, b_x, lam, carried, modes):
    rows = zrec.shape[0]
    tm = _rec_tile(rows)
    nt = rows // tm
    per = tm // SUBLANES

    def body(drec_ref, xr_ref, yr_ref, h_ref, xc_ref, hhalo_ref, cw_ref, wa_ref, ba_ref, wx_ref, bx_ref,
             lam_ref, drz_ref, small_ref, dwa_ref, dwx_ref, hbuf, abuf, u_s, g_s, dbuf, carry):
        s = pl.program_id(0)
        i = nt - 1 - s

        @pl.when(s == 0)
        def _():
            small_ref[...] = jnp.zeros_like(small_ref)
            dwa_ref[...] = jnp.zeros_like(dwa_ref)
            dwx_ref[...] = jnp.zeros_like(dwx_ref)
            carry[...] = jnp.zeros_like(carry)
            abuf[tm:tm + SUBLANES, :] = jnp.zeros((SUBLANES, LRU_WIDTH), F32)
            dbuf[tm:tm + SUBLANES, :] = jnp.zeros((SUBLANES, LRU_WIDTH), F32)

        hbuf[0:SUBLANES, :] = jnp.where(i == 0, 0.0, hhalo_ref[...])
        hbuf[SUBLANES:SUBLANES + tm, :] = h_ref[...]

        xc = xc_ref[...]
        halves, r, ig, a, mult, inv_mult = _lru_gates(xc, wa_ref, ba_ref, wx_ref, bx_ref, lam_ref)

        yr = yr_ref[...]
        gel, t = _gelu(yr)
        drec_t = drec_ref[...]
        dyr = drec_t * h_ref[...] * _gelu_grad(yr, t)

        abuf[0:tm, :] = a
        u_s[...] = drec_t * gel
        a_next = abuf[pl.ds(1, tm), :]
        abuf[0:tm, :] = a_next
        carry[0:1, :] = _scan_tile(abuf, u_s, g_s, carry[0:1, :], tm, reverse=True)
        abuf[tm:tm + 1, :] = a[0:1, :]
        g = g_s[...]

        grow = i * tm + lax.broadcasted_iota(jnp.int32, (tm, LRU_WIDTH), 0)
        du = jnp.where(grow >= PAD_ROWS, g, 0.0)
        da = g * hbuf[pl.ds(SUBLANES - 1, tm), :]
        dmult = du * (ig * xc)
        dig = du * (mult * xc)
        dxc = du * (mult * ig)
        dlog_a = da * a - dmult * (a * a * inv_mult)
        sp = _softplus(-lam_ref[...])
        dgr = (dlog_a * (-LRU_C) * sp) * (r * (1.0 - r))
        dgi = dig * (ig * (1.0 - ig))
        dlam = jnp.sum(dlog_a * r, axis=0, keepdims=True) * (LRU_C * _sigmoid(-lam_ref[...]))
        dgr_b = [dgr[:, hh * LRU_HALF:(hh + 1) * LRU_HALF].astype(BF16) for hh in range(2)]
        dgi_b = [dgi[:, hh * LRU_HALF:(hh + 1) * LRU_HALF].astype(BF16) for hh in range(2)]
        dxc = dxc + jnp.concatenate(
            [_mm_nt(dgr_b[hh], wa_ref[hh]) + _mm_nt(dgi_b[hh], wx_ref[hh]) for hh in range(2)], axis=1)
        for hh in range(2):
            dwa_ref[hh] += _mm_tn(halves[hh], dgr_b[hh])
            dwx_ref[hh] += _mm_tn(halves[hh], dgi_b[hh])

        dbuf[0:tm, :] = dxc
        ahead = [dbuf[pl.ds(CONV_WIDTH - 1 - j, tm), :] for j in range(CONV_WIDTH)]
        dxr = sum(cw_ref[j:j + 1, :] * ahead[j] for j in range(CONV_WIDTH))
        dbuf[tm:tm + SUBLANES, :] = dxc[0:SUBLANES, :]
        drz_ref[:, 0:LRU_WIDTH] = dxr.astype(BF16)
        drz_ref[:, LRU_WIDTH:] = dyr.astype(BF16)

        xr = xr_ref[...]
        upd = [jnp.sum(xr * ahead[j], axis=0, keepdims=True) for j in range(CONV_WIDTH)]
        upd += [jnp.sum(dxc, axis=0, keepdims=True), jnp.sum(dgr, axis=0, keepdims=True),
                jnp.sum(dgi, axis=0, keepdims=True), dlam]
        small_ref[...] += jnp.concatenate(upd, axis=0)

    rev = lambda s: nt - 1 - s
    halo = lambda s: jnp.maximum(rev(s) * per - 1, 0)
    tile0 = pl.BlockSpec((tm, LRU_WIDTH), lambda s: (rev(s), 0))
    tile1 = pl.BlockSpec((tm, LRU_WIDTH), lambda s: (rev(s), 1))
    halo0 = pl.BlockSpec((SUBLANES, LRU_WIDTH), lambda s: (halo(s), 0))
    vec = _full((1, LRU_WIDTH))
    bd = _full((2, LRU_HALF, LRU_HALF))
    big = pltpu.VMEM((tm + SUBLANES, LRU_WIDTH), F32)
    tile = pltpu.VMEM((tm, LRU_WIDTH), F32)
    return _hosting_call(
        body, "rec_bwd", nt,
        [tile0, tile0, tile1, tile0, tile0, halo0, _full((CONV_WIDTH, LRU_WIDTH)), bd, vec, bd, vec, vec],
        [pl.BlockSpec((tm, 2 * LRU_WIDTH), lambda s: (rev(s), 0)), _full((SUBLANES, LRU_WIDTH)), bd, bd],
        [jax.ShapeDtypeStruct((rows, 2 * LRU_WIDTH), BF16), jax.ShapeDtypeStruct((SUBLANES, LRU_WIDTH), F32),
         jax.ShapeDtypeStruct((2, LRU_HALF, LRU_HALF), F32), jax.ShapeDtypeStruct((2, LRU_HALF, LRU_HALF), F32)],
        [big, big, tile, tile, big, pltpu.VMEM((SUBLANES, LRU_WIDTH), F32)],
        (drec, zrec, zrec, h, xc_all, h, conv_w, wa_bd, b_a, wx_bd, b_x, lam), carried, modes)


DZ_CUTS = (0, ATTN_WIDTH, QKV_WIDTH, IN_WIDTH)


def _dz_specs(tm):
    return [pl.BlockSpec((tm, DZ_CUTS[p + 1] - DZ_CUTS[p]), lambda i: (i, 0)) for p in range(3)]


def _in_proj_bwd_x(h0, g1, dh1, dq, dkv, drz, w_in_t, carried, modes):
    rows = h0.shape[0]
    tm = _row_tile(rows)

    def body(h_ref, g_ref, dh1_ref, dq_ref, dkv_ref, drz_ref, w_ref, dh0_ref, dg_ref):
        @pl.when(pl.program_id(0) == 0)
        def _():
            dg_ref[...] = jnp.zeros_like(dg_ref)

        g = g_ref[...]
        _, xhat, rstd = _rms_fwd(h_ref[...], g)
        parts = (dq_ref[...], dkv_ref[...], drz_ref[...])
        du = sum(_mm(parts[p], w_ref[DZ_CUTS[p]:DZ_CUTS[p + 1], :]) for p in range(3))
        dx, dg = _rms_bwd(du, xhat, rstd, g)
        dh0_ref[...] = dh1_ref[...] + dx
        dg_ref[...] += dg

    wide = pl.BlockSpec((tm, D_MODEL), lambda i: (i, 0))
    return _hosting_call(
        body, "in_proj_bwd_x", rows // tm,
        [wide, _full((1, D_MODEL)), wide] + _dz_specs(tm) + [_resident((IN_WIDTH, D_MODEL))],
        [wide, _full((1, D_MODEL))],
        [jax.ShapeDtypeStruct((rows, D_MODEL), F32), jax.ShapeDtypeStruct((1, D_MODEL), F32)],
        [], (h0, g1, dh1, dq, dkv, drz, w_in_t), carried, modes)


def _in_proj_bwd_w(u1, dq, dkv, drz, carried, modes):
    rows = u1.shape[0]
    tb = _big_tile(rows)

    def body(u_ref, dq_ref, dkv_ref, drz_ref, dw_ref):
        @pl.when(pl.program_id(0) == 0)
        def _():
            dw_ref[...] = jnp.zeros_like(dw_ref)

        u = u_ref[...]
        for p, ref in enumerate((dq_ref, dkv_ref, drz_ref)):
            dw_ref[:, DZ_CUTS[p]:DZ_CUTS[p + 1]] += _mm_tn(u, ref[...])

    return _hosting_call(
        body, "in_proj_bwd_w", rows // tb,
        [pl.BlockSpec((tb, D_MODEL), lambda i: (i, 0))] + _dz_specs(tb),
        [_full((D_MODEL, IN_WIDTH))],
        [jax.ShapeDtypeStruct((D_MODEL, IN_WIDTH), F32)],
        [], (u1, dq, dkv, drz), carried, modes)


def _adamw(w, m, v, parts, name):
    rows, cols = w.shape
    tr = next((t for t in (256, 128) if rows % t == 0), rows)
    parts = parts if isinstance(parts, (list, tuple)) else [parts]

    def body(w_ref, m_ref, v_ref, *refs):
        p_refs, (g_ref, d_ref, nm_ref, nv_ref) = refs[:len(parts)], refs[len(parts):]

        def total(p_ref):
            g = p_ref[0].astype(F32)
            for s in range(1, N_DEV):
                g = g + p_ref[s].astype(F32)
            return g

        g = jnp.concatenate([total(p_ref) for p_ref in p_refs], axis=1) if len(parts) > 1 else total(p_refs[0])
        nm = ADAM_B1 * m_ref[...] + (1.0 - ADAM_B1) * g
        nv = ADAM_B2 * v_ref[...] + (1.0 - ADAM_B2) * (g * g)
        m_hat = nm / (1.0 - ADAM_B1 ** ADAM_STEP)
        v_hat = nv / (1.0 - ADAM_B2 ** ADAM_STEP)
        g_ref[...] = g
        d_ref[...] = (-ADAM_LR) * (m_hat / (jnp.sqrt(v_hat) + ADAM_EPS) + ADAM_WD * w_ref[...])
        nm_ref[...] = nm
        nv_ref[...] = nv

    blk = pl.BlockSpec((tr, cols), lambda i: (i, 0))
    return pl.pallas_call(
        body, name=name, grid=(rows // tr,),
        in_specs=[blk, blk, blk] + [pl.BlockSpec((N_DEV, tr, p.shape[2]), lambda i: (0, i, 0)) for p in parts],
        out_specs=[blk] * 4,
        out_shape=[jax.ShapeDtypeStruct((rows, cols), F32)] * 4,
        compiler_params=_params(("parallel",)),
    )(w, m, v, *parts)


def _cols_from_shards(g):
    return jnp.transpose(g, (1, 0, 2)).reshape(g.shape[1], N_DEV * g.shape[2])


def _cols_to_shards(a):
    r, c = a.shape
    return jnp.transpose(a.reshape(r, N_DEV, c // N_DEV), (1, 0, 2))


def _block_diag(w):
    per = LRU_HALF // LRU_BLOCK
    w = w.reshape(2, per, LRU_BLOCK, LRU_BLOCK)
    eye = jnp.eye(per, dtype=w.dtype)
    return (w[:, :, :, None, :] * eye[None, :, None, :, None]).reshape(2, LRU_HALF, LRU_HALF)


def _block_diag_extract(t):
    per = LRU_HALF // LRU_BLOCK
    t = t.reshape(2, per, LRU_BLOCK, per, LRU_BLOCK)
    return jnp.stack([t[:, b, :, b, :] for b in range(per)], axis=1).reshape(LRU_BLOCKS, LRU_BLOCK, LRU_BLOCK)


SMALL_NAMES = ("conv_b", "w_a", "b_a", "w_x", "b_x", "lru_lambda", "attn_sinks",
               "g_post_mix", "g_pre_ffn", "g_post_ffn")


def _pack_small(vals):
    flat = []
    for name in SMALL_NAMES:
        a = vals[name].reshape(-1)
        flat.append(jnp.pad(a, (0, (-a.shape[0]) % LANES)))
    flat = jnp.concatenate(flat)
    rows = flat.shape[0] // LANES
    return jnp.pad(flat.reshape(rows, LANES), ((0, (-rows) % SUBLANES), (0, 0)))


def _unpack_small(packed, like):
    flat = packed.reshape(-1)
    out, at = {}, 0
    for name in SMALL_NAMES:
        n = like[name].size
        out[name] = flat[at:at + n].reshape(like[name].shape)
        at += n + (-n) % LANES
    return out


def kernel(x, meta_tokens, g_pre_mix, w_in, conv_w, conv_b, w_a, b_a, w_x, b_x, lru_lambda, attn_sinks, w_out, g_post_mix, g_pre_ffn, w_ff1, w_ff2, g_post_ffn, loss_target, m_meta_tokens, m_g_pre_mix, m_w_in, m_conv_w, m_conv_b, m_w_a, m_b_a, m_w_x, m_b_x, m_lru_lambda, m_attn_sinks, m_w_out, m_g_post_mix, m_g_pre_ffn, m_w_ff1, m_w_ff2, m_g_post_ffn, v_meta_tokens, v_g_pre_mix, v_w_in, v_conv_w, v_conv_b, v_w_a, v_b_a, v_w_x, v_b_x, v_lru_lambda, v_attn_sinks, v_w_out, v_g_post_mix, v_g_pre_ffn, v_w_ff1, v_w_ff2, v_g_post_ffn):
    weights = dict(meta_tokens=meta_tokens, g_pre_mix=g_pre_mix, w_in=w_in, conv_w=conv_w, conv_b=conv_b, w_a=w_a,
                   b_a=b_a, w_x=w_x, b_x=b_x, lru_lambda=lru_lambda, attn_sinks=attn_sinks, w_out=w_out,
                   g_post_mix=g_post_mix, g_pre_ffn=g_pre_ffn, w_ff1=w_ff1, w_ff2=w_ff2, g_post_ffn=g_post_ffn)
    mom_m = dict(meta_tokens=m_meta_tokens, g_pre_mix=m_g_pre_mix, w_in=m_w_in, conv_w=m_conv_w, conv_b=m_conv_b,
                 w_a=m_w_a, b_a=m_b_a, w_x=m_w_x, b_x=m_b_x, lru_lambda=m_lru_lambda, attn_sinks=m_attn_sinks,
                 w_out=m_w_out, g_post_mix=m_g_post_mix, g_pre_ffn=m_g_pre_ffn, w_ff1=m_w_ff1, w_ff2=m_w_ff2,
                 g_post_ffn=m_g_post_ffn)
    mom_v = dict(meta_tokens=v_meta_tokens, g_pre_mix=v_g_pre_mix, w_in=v_w_in, conv_w=v_conv_w, conv_b=v_conv_b,
                 w_a=v_w_a, b_a=v_b_a, w_x=v_w_x, b_x=v_b_x, lru_lambda=v_lru_lambda, attn_sinks=v_attn_sinks,
                 w_out=v_w_out, g_post_mix=v_g_post_mix, g_pre_ffn=v_g_pre_ffn, w_ff1=v_w_ff1, w_ff2=v_w_ff2,
                 g_post_ffn=v_g_post_ffn)
    order = list(weights)

    (g_win, g_meta, g_cw) = _gather_two_level([w_in[0].astype(BF16), meta_tokens, conv_w[0]], "gather_first")
    w_in_full = _cols_from_shards(g_win)
    meta_full = _cols_from_shards(g_meta)
    conv_w_full = _cols_from_shards(g_cw)

    head = jnp.concatenate([jnp.zeros((PAD_ROWS, D_MODEL), F32), meta_full], axis=0)
    wa_bd = _block_diag(w_a[0]).astype(BF16)
    wx_bd = _block_diag(w_x[0]).astype(BF16)
    bias = _attn_bias()

    w1_shard = w_ff1[0].astype(BF16)
    (qkv, zrec, u1, h0), (g_wout,) = _in_proj_fwd(head, x[0], g_pre_mix, w_in_full, [w_out[0].astype(BF16)], ["gather"])
    (attn,), (w1a,) = _attn_fwd(qkv, attn_sinks, bias, [w1_shard[:, :FF_HALF]], ["gather"])
    (rec, h_lru, xc), (w1b,) = _rec_fwd(zrec, conv_w_full, conv_b, wa_bd, b_a, wx_bd, b_x, lru_lambda,
                                         [w1_shard[:, FF_HALF:]], ["gather"])
    w_out_full = g_wout.reshape(D_MODEL, D_MODEL)
    w2_shard = w_ff2[0].astype(BF16)
    (mix, h1), (w2a,) = _out_proj_fwd(attn, rec, w_out_full, h0, g_post_mix, [w2_shard[:FF_HALF]], ["gather"])
    (act, u2), (w2b,) = _ffn_up(h1, g_pre_ffn, (w1a, w1b), [w2_shard[FF_HALF:]], ["gather"])
    w2_halves = [w.reshape(D_FF // 2, D_MODEL) for w in (w2a, w2b)]
    dy, df, dg_post_ffn, loss_acc = _ffn_down_loss(act, w2_halves, h1, loss_target[0], g_post_ffn)

    da1 = _ffn_bwd_act(df, [jnp.transpose(w, (0, 2, 1)) for w in (w2a, w2b)], act)
    dw1h, dw2g = _ffn_bwd_weights(u2, da1, act, df)
    w1t = jnp.concatenate([jnp.transpose(w, (0, 2, 1)).reshape(D_FF // 2, D_MODEL) for w in (w1a, w1b)], axis=0)
    (dh1, dg_pre_ffn), (p_w1a,) = _ffn_bwd_x(da1, w1t, h1, dy, g_pre_ffn, [dw1h[0]], ["scatter"])
    (dattn, drec, dw_out, dg_post_mix), (p_w1b,) = _out_proj_bwd(dh1, mix, g_post_mix, w_out_full.T, attn, rec,
                                                                [dw1h[1]], ["scatter"])
    (dq, dkv_late, dsinks), (p_w2,) = _attn_bwd(qkv, dattn, attn_sinks, bias, [dw2g], ["scatter"])
    dkv = dkv_late[BLOCK:BLOCK + qkv.shape[0]]
    (drz, rec_small, dwa_bd, dwx_bd), (p_wout,) = _rec_bwd(
        drec, zrec, h_lru, xc, conv_w_full, wa_bd, b_a, wx_bd, b_x, lru_lambda,
        [dw_out.reshape(N_DEV, D_MODEL // N_DEV, D_MODEL)], ["scatter"])
    small_grads = dict(
        conv_b=rec_small[ROW_CONV_B], w_a=_block_diag_extract(dwa_bd), b_a=rec_small[ROW_B_A],
        w_x=_block_diag_extract(dwx_bd), b_x=rec_small[ROW_B_X], lru_lambda=rec_small[ROW_LAMBDA],
        attn_sinks=dsinks[:, 0], g_post_mix=dg_post_mix, g_pre_ffn=dg_pre_ffn, g_post_ffn=dg_post_ffn)
    (dw_in,), (p_cw, p_small) = _in_proj_bwd_w(
        u1, dq, dkv, drz, [_cols_to_shards(rec_small[0:CONV_WIDTH]), _pack_small(small_grads)], ["scatter", "gather"])
    (dh0, dg_pre_mix), (p_win,) = _in_proj_bwd_x(
        h0, g_pre_mix, dh1, dq, dkv, drz, w_in_full.T, [_cols_to_shards(dw_in).astype(BF16)], ["scatter"])
    p_meta, p_gpm = _exchange([_cols_to_shards(dh0[PAD_ROWS:BLOCK]), dg_pre_mix], ["scatter", "gather"], "exchange_last")

    res = {}
    res["g_pre_mix"] = _adamw(g_pre_mix, m_g_pre_mix, v_g_pre_mix, p_gpm, "adamw_g_pre_mix")
    res["w_in"] = _adamw(w_in[0], m_w_in[0], v_w_in[0], p_win, "adamw_w_in")
    res["w_out"] = _adamw(w_out[0], m_w_out[0], v_w_out[0], p_wout, "adamw_w_out")
    res["w_ff1"] = _adamw(w_ff1[0], m_w_ff1[0], v_w_ff1[0], [p_w1a, p_w1b], "adamw_w_ff1")
    res["w_ff2"] = _adamw(w_ff2[0], m_w_ff2[0], v_w_ff2[0], p_w2, "adamw_w_ff2")
    res["meta_tokens"] = _adamw(meta_tokens, m_meta_tokens, v_meta_tokens, p_meta, "adamw_meta")
    res["conv_w"] = _adamw(conv_w[0], m_conv_w[0], v_conv_w[0], p_cw, "adamw_conv_w")
    small = _adamw(_pack_small(weights), _pack_small(mom_m), _pack_small(mom_v), p_small, "adamw_small")
    small = [_unpack_small(t, weights) for t in small]
    for name in SMALL_NAMES:
        res[name] = tuple(t[name] for t in small)
    for name in ("w_in", "w_out", "w_ff1", "w_ff2", "conv_w"):
        res[name] = tuple(t[None] for t in res[name])

    loss = lax.psum(loss_acc[0, 0], ("x", "y", "c"))
    grad_x = dh0[BLOCK:][None]
    outs = [loss, grad_x]
    for k in range(4):
        outs += [res[name][k] for name in order]
    return tuple(outs)
```

```python
import jax
import jax.numpy as jnp
import numpy as np
from jax import lax
from jax.experimental import pallas as pl
from jax.experimental.pallas import tpu as pltpu

F32 = jnp.float32
BF16 = jnp.bfloat16

D_MODEL = 1024
N_META = 16
HEAD_DIM = 64
ATTN_HEADS = 8
KV_HEADS = 2
GQA_GROUP = ATTN_HEADS // KV_HEADS
ATTN_WIDTH = ATTN_HEADS * HEAD_DIM
KV_WIDTH = KV_HEADS * HEAD_DIM
QKV_WIDTH = ATTN_WIDTH + 2 * KV_WIDTH
LRU_WIDTH = 512
LRU_BLOCKS = 8
LRU_BLOCK = 64
LRU_HALF = 256
LRU_C = 8.0
CONV_WIDTH = 4
BLOCK = 128
PAD_ROWS = BLOCK - N_META
IN_WIDTH = QKV_WIDTH + 2 * LRU_WIDTH
D_FF = 4096
EPS = 1e-6
NEG = -1e30
N_DEV = 8
FF_CHUNK = D_FF // N_DEV
SUBLANES = 8
LANES = 128

ADAM_LR = 0.001
ADAM_B1 = 0.9
ADAM_B2 = 0.999
ADAM_EPS = 1e-08
ADAM_WD = 0.01
ADAM_STEP = 10

VMEM_LIMIT = 56 * 1024 * 1024


def _row_tile(rows):
    for t in (640, 512, 256, 128):
        if rows % t == 0:
            return t
    raise ValueError(rows)


def _big_tile(rows):
    for t in (1664, 1024, 512, 256, 128):
        if rows % t == 0:
            return t
    raise ValueError(rows)


def _rec_tile(rows):
    for t in (320, 256, 128):
        if rows % t == 0:
            return t
    raise ValueError(rows)


def _params(semantics):
    return pltpu.CompilerParams(dimension_semantics=semantics, vmem_limit_bytes=VMEM_LIMIT)


def _mm(a, b):
    return lax.dot_general(a, b, (((1,), (0,)), ((), ())), preferred_element_type=F32)


def _mm_nt(a, b):
    return lax.dot_general(a, b, (((1,), (1,)), ((), ())), preferred_element_type=F32)


def _mm_tn(a, b):
    return lax.dot_general(a, b, (((0,), (0,)), ((), ())), preferred_element_type=F32)


def _rms_fwd(x, g):
    rstd = lax.rsqrt(jnp.mean(x * x, axis=-1, keepdims=True) + EPS)
    xhat = x * rstd
    return xhat * g, xhat, rstd


def _rms_bwd(dy, xhat, rstd, g):
    dyg = dy * g
    c = jnp.mean(dyg * xhat, axis=-1, keepdims=True)
    dx = rstd * (dyg - xhat * c)
    dg = jnp.sum(dy * xhat, axis=0, keepdims=True)
    return dx, dg


def _sigmoid(x):
    return 0.5 * jnp.tanh(0.5 * x) + 0.5


def _log1p(x):
    u = 1.0 + x
    return jnp.where(u == 1.0, x, jnp.log(u) * x / (u - 1.0))


def _one_minus_sq_exp(x, ex):
    return -jnp.tanh(x) * (1.0 + ex * ex)


TINY = 1e-30


def _sqrt_pos(y):
    r = lax.rsqrt(jnp.maximum(y, TINY))
    return y * r, r


def _softplus(x):
    return jnp.maximum(x, 0.0) + _log1p(jnp.exp(-jnp.abs(x)))


GELU_C = 0.7978845608028654
GELU_K = 0.044715


def _gelu(x):
    t = jnp.tanh(GELU_C * (x + GELU_K * x * x * x))
    return 0.5 * x * (1.0 + t), t


def _gelu_grad(x, t):
    return 0.5 * (1.0 + t) + 0.5 * x * (1.0 - t * t) * GELU_C * (1.0 + 3.0 * GELU_K * x * x)


def _full(shape):
    return pl.BlockSpec(shape, lambda *_: (0,) * len(shape))


def _resident(shape):
    return pl.BlockSpec(shape, lambda *_: (0,) * len(shape), pipeline_mode=pl.Buffered(1))


def _exchange_copies(ins, outs, sems, modes):
    send_sems, recv_sems, local_sems = sems
    x, y, c = lax.axis_index("x"), lax.axis_index("y"), lax.axis_index("c")
    me = 4 * x + 2 * y + c

    def block(a, dev):
        return ins[a] if modes[a] == "gather" else ins[a].at[dev]

    local = [pltpu.make_async_copy(block(a, me), outs[a].at[me], local_sems.at[a]) for a in range(len(ins))]
    sends, recvs = [], []
    for a in range(len(ins)):
        for k in range(N_DEV - 1):
            bits = k + 1
            px = jnp.bitwise_xor(x, (bits >> 2) & 1)
            py = jnp.bitwise_xor(y, (bits >> 1) & 1)
            pc = jnp.bitwise_xor(c, bits & 1)
            peer = 4 * px + 2 * py + pc
            common = dict(src_ref=block(a, peer), send_sem=send_sems.at[a, k], recv_sem=recv_sems.at[a, k],
                          device_id=(px, py, pc), device_id_type=pl.DeviceIdType.MESH)
            sends.append(pltpu.make_async_remote_copy(dst_ref=outs[a].at[me], **common))
            recvs.append(pltpu.make_async_remote_copy(dst_ref=outs[a].at[peer], **common))
    return local, sends, recvs


def _exchange_start(ins, outs, sems, modes):
    local, sends, _ = _exchange_copies(ins, outs, sems, modes)
    for cp in local + sends:
        cp.start()


def _exchange_wait(ins, outs, sems, modes):
    local, sends, recvs = _exchange_copies(ins, outs, sems, modes)
    for cp in recvs:
        cp.wait_recv()
    for cp in sends:
        cp.wait_send()
    for cp in local:
        cp.wait()


def _exchange_shapes(arrays, modes):
    return [jax.ShapeDtypeStruct((N_DEV,) + a.shape if mode == "gather" else a.shape, a.dtype)
            for a, mode in zip(arrays, modes)]


def _exchange_sems(na):
    return [pltpu.SemaphoreType.DMA((na, N_DEV - 1)), pltpu.SemaphoreType.DMA((na, N_DEV - 1)),
            pltpu.SemaphoreType.DMA((na,))]


ANY_SPACE = pl.BlockSpec(memory_space=pl.ANY)


def _exchange(arrays, modes, name):
    na = len(arrays)

    def body(*refs):
        ins, outs, sems = refs[:na], refs[na:2 * na], refs[2 * na:]
        _exchange_start(ins, outs, sems, modes)
        _exchange_wait(ins, outs, sems, modes)

    return pl.pallas_call(
        body, name=name, out_shape=_exchange_shapes(arrays, modes),
        in_specs=[ANY_SPACE] * na, out_specs=[ANY_SPACE] * na, scratch_shapes=_exchange_sems(na),
        compiler_params=pltpu.CompilerParams(has_side_effects=True),
    )(*arrays)


def _gather_two_level(arrays, name):
    na = len(arrays)

    def body(*refs):
        ins, outs = refs[:na], refs[na:2 * na]
        send_sems, recv_sems, local_sems = refs[2 * na:]
        x, y, c = lax.axis_index("x"), lax.axis_index("y"), lax.axis_index("c")
        me, sibling = (x, y, c), (x, y, 1 - c)
        chips = [(1 - x, y), (x, 1 - y), (1 - x, 1 - y)]

        def copy(a, k, block, to, src=None):
            slot = outs[a].at[4 * block[0] + 2 * block[1] + block[2]]
            return pltpu.make_async_remote_copy(
                src_ref=slot if src is None else src, dst_ref=slot, send_sem=send_sems.at[a, k],
                recv_sem=recv_sems.at[a, k], device_id=to, device_id_type=pl.DeviceIdType.MESH)

        local = [pltpu.make_async_copy(ins[a], outs[a].at[4 * x + 2 * y + c], local_sems.at[a]) for a in range(na)]
        first = []
        for a in range(na):
            first.append(copy(a, 0, me, sibling, src=ins[a]))
            first += [copy(a, 1 + j, me, (*chip, c), src=ins[a]) for j, chip in enumerate(chips)]
        for cp in local + first:
            cp.start()
        passed = []
        for j, chip in enumerate(chips):
            for a in range(na):
                copy(a, 1 + j, (*chip, c), me).wait_recv()
                passed.append(copy(a, 4 + j, (*chip, c), sibling))
                passed[-1].start()
        for a in range(na):
            copy(a, 0, sibling, me).wait_recv()
            for j, chip in enumerate(chips):
                copy(a, 4 + j, (*chip, 1 - c), me).wait_recv()
        for cp in first + passed:
            cp.wait_send()
        for cp in local:
            cp.wait()

    return pl.pallas_call(
        body, name=name, out_shape=_exchange_shapes(arrays, ["gather"] * na),
        in_specs=[ANY_SPACE] * na, out_specs=[ANY_SPACE] * na, scratch_shapes=_exchange_sems(na),
        compiler_params=pltpu.CompilerParams(has_side_effects=True),
    )(*arrays)


def _hosting_call(body, name, steps, in_specs, out_specs, out_shape, scratch_shapes, args, arrays, modes):
    n_in, n_out, n_scr, na = len(in_specs), len(out_specs), len(scratch_shapes), len(arrays)

    def hosting_body(*refs):
        cuts = [0]
        for n in (n_in, na, n_out, na, n_scr, 3):
            cuts.append(cuts[-1] + n)
        ins, x_ins, outs, x_outs, scr, sems = (refs[cuts[p]:cuts[p + 1]] for p in range(6))
        step = pl.program_id(0)

        @pl.when(step == 0)
        def _():
            _exchange_start(x_ins, x_outs, sems, modes)

        body(*ins, *outs, *scr)

        @pl.when(step == steps - 1)
        def _():
            _exchange_wait(x_ins, x_outs, sems, modes)

    res = pl.pallas_call(
        hosting_body, name=name, grid=(steps,),
        in_specs=list(in_specs) + [ANY_SPACE] * na, out_specs=list(out_specs) + [ANY_SPACE] * na,
        out_shape=list(out_shape) + _exchange_shapes(arrays, modes),
        scratch_shapes=list(scratch_shapes) + _exchange_sems(na),
        compiler_params=_params(("arbitrary",)),
    )(*args, *arrays)
    return res[:n_out], res[n_out:]


def _frame_rows(src_hbm, buf, sem, i, steps, tm):
    def first():
        return pltpu.make_async_copy(src_hbm.at[pl.ds(0, tm - BLOCK)], buf.at[0, pl.ds(BLOCK, tm - BLOCK)], sem.at[0])

    def later(t, slot):
        return pltpu.make_async_copy(src_hbm.at[pl.ds(pl.multiple_of(t * tm - BLOCK, BLOCK), tm)], buf.at[slot], sem.at[slot])

    slot = i % 2

    @pl.when(i == 0)
    def _():
        first().start()

    @pl.when(i + 1 < steps)
    def _():
        later(i + 1, 1 - slot).start()

    @pl.when(i == 0)
    def _():
        first().wait()

    @pl.when(i > 0)
    def _():
        later(i, slot).wait()

    return slot


def _frame_scratch(tm):
    return [pltpu.VMEM((2, tm, D_MODEL), F32), pltpu.SemaphoreType.DMA((2,))]


def _in_proj_fwd(head, x, g1, w_in, carried, modes):
    rows = BLOCK + x.shape[0]
    tm = _row_tile(rows)
    steps = rows // tm

    def body(head_ref, g_ref, w_ref, x_hbm, qkv_ref, zrec_ref, u_ref, h_ref, buf, sem):
        i = pl.program_id(0)
        slot = _frame_rows(x_hbm, buf, sem, i, steps, tm)

        @pl.when(i == 0)
        def _():
            buf[0, 0:BLOCK, :] = head_ref[...]

        h = buf[slot]
        h_ref[...] = h
        u, _, _ = _rms_fwd(h, g_ref[...])
        u = u.astype(BF16)
        u_ref[...] = u
        z = _mm(u, w_ref[...])
        qkv_ref[...] = z[:, :QKV_WIDTH].astype(BF16)
        zrec_ref[...] = z[:, QKV_WIDTH:]

    wide = pl.BlockSpec((tm, D_MODEL), lambda i: (i, 0))
    return _hosting_call(
        body, "in_proj_fwd", steps,
        [_full((BLOCK, D_MODEL)), _full((1, D_MODEL)), _resident((D_MODEL, IN_WIDTH)), ANY_SPACE],
        [pl.BlockSpec((tm, QKV_WIDTH), lambda i: (i, 0)), pl.BlockSpec((tm, 2 * LRU_WIDTH), lambda i: (i, 0)), wide, wide],
        [jax.ShapeDtypeStruct((rows, QKV_WIDTH), BF16), jax.ShapeDtypeStruct((rows, 2 * LRU_WIDTH), F32),
         jax.ShapeDtypeStruct((rows, D_MODEL), BF16), jax.ShapeDtypeStruct((rows, D_MODEL), F32)],
        _frame_scratch(tm), (head, g1, w_in, x), carried, modes)


N_BIAS = 3


def _attn_bias():
    key = np.arange(2 * BLOCK)[:, None]
    r = np.arange(GQA_GROUP * BLOCK)[None, :] % BLOCK
    band = (key > r) & (key <= r + BLOCK)
    out = [np.where(band & ((n - 1) * BLOCK + key >= PAD_ROWS), 0.0, NEG) for n in range(N_BIAS)]
    return jnp.asarray(np.stack(out), F32)


def _attn_probs(k2, q4, bias, sink_row):
    s = _mm_nt(k2, q4) * (HEAD_DIM ** -0.5) + bias
    m = jnp.maximum(jnp.max(s, axis=0, keepdims=True), sink_row)
    p = jnp.exp(s - m)
    es = jnp.exp(sink_row - m)
    inv = 1.0 / (jnp.sum(p, axis=0, keepdims=True) + es)
    return p * inv, es * inv


def _heads(ref, rows, first, count):
    return jnp.concatenate([ref[rows, (first + g) * HEAD_DIM:(first + g + 1) * HEAD_DIM] for g in range(count)], axis=0)


def _keys_of_block(prev_ref, cur_ref, b, kv):
    sl = slice(kv * HEAD_DIM, (kv + 1) * HEAD_DIM)
    before = prev_ref[:, sl] if b == 0 else cur_ref[(b - 1) * BLOCK:b * BLOCK, sl]
    return jnp.concatenate([before, cur_ref[b * BLOCK:(b + 1) * BLOCK, sl]], axis=0)


def _bias_of_block(bias_ref, block):
    return bias_ref[jnp.minimum(block, N_BIAS - 1)]


def _sink_row(sink_ref, kv):
    g = lax.broadcasted_iota(jnp.int32, (1, GQA_GROUP * BLOCK), 1) // BLOCK
    row = jnp.full((1, GQA_GROUP * BLOCK), sink_ref[0, kv * GQA_GROUP], F32)
    for i in range(1, GQA_GROUP):
        row = jnp.where(g == i, sink_ref[0, kv * GQA_GROUP + i], row)
    return row


def _from_head_major(pieces):
    return jnp.concatenate(pieces, axis=0).T


def _attn_specs(tm, tile_of):
    nbt = tm // BLOCK
    k_col, v_col = ATTN_WIDTH // KV_WIDTH, ATTN_WIDTH // KV_WIDTH + 1
    before = lambda i: jnp.maximum(tile_of(i) * nbt - 1, 0)
    return [pl.BlockSpec((tm, ATTN_WIDTH), lambda i: (tile_of(i), 0)),
            pl.BlockSpec((BLOCK, KV_WIDTH), lambda i: (before(i), k_col)),
            pl.BlockSpec((tm, KV_WIDTH), lambda i: (tile_of(i), k_col)),
            pl.BlockSpec((BLOCK, KV_WIDTH), lambda i: (before(i), v_col)),
            pl.BlockSpec((tm, KV_WIDTH), lambda i: (tile_of(i), v_col))]


def _attn_fwd(qkv, sinks, bias, carried, modes):
    rows = qkv.shape[0]
    tm = _row_tile(rows)
    nbt = tm // BLOCK

    def body(sink_ref, bias_ref, q_ref, kp_ref, kc_ref, vp_ref, vc_ref, o_ref):
        i = pl.program_id(0)
        for b in range(nbt):
            blk = slice(b * BLOCK, (b + 1) * BLOCK)
            bias_t = _bias_of_block(bias_ref, i * nbt + b)
            pieces = []
            for kv in range(KV_HEADS):
                k2 = _keys_of_block(kp_ref, kc_ref, b, kv)
                v2 = _keys_of_block(vp_ref, vc_ref, b, kv)
                q4 = _heads(q_ref, blk, kv * GQA_GROUP, GQA_GROUP)
                pn, _ = _attn_probs(k2, q4, bias_t, _sink_row(sink_ref, kv))
                ot = _mm_tn(v2, pn.astype(BF16))
                pieces += [ot[:, g * BLOCK:(g + 1) * BLOCK] for g in range(GQA_GROUP)]
            o_ref[blk, :] = _from_head_major(pieces).astype(BF16)

    return _hosting_call(
        body, "attn_fwd", rows // tm,
        [pl.BlockSpec(memory_space=pltpu.SMEM), _resident((N_BIAS, 2 * BLOCK, GQA_GROUP * BLOCK))]
        + _attn_specs(tm, lambda i: i),
        [pl.BlockSpec((tm, ATTN_WIDTH), lambda i: (i, 0))],
        [jax.ShapeDtypeStruct((rows, ATTN_WIDTH), BF16)],
        [], (sinks, bias, qkv, qkv, qkv, qkv, qkv), carried, modes)


def _conv_taps(xbuf, tm):
    return [xbuf[pl.ds(SUBLANES - (CONV_WIDTH - 1 - j), tm), :] for j in range(CONV_WIDTH)]


def _lru_halves(xc):
    return [xc[:, h * LRU_HALF:(h + 1) * LRU_HALF].astype(BF16) for h in range(2)]


def _lru_gates(xc, wa_ref, ba_ref, wx_ref, bx_ref, lam_ref):
    halves = _lru_halves(xc)
    gate_r = jnp.concatenate([_mm(halves[h], wa_ref[h]) for h in range(2)], axis=1) + ba_ref[...]
    gate_i = jnp.concatenate([_mm(halves[h], wx_ref[h]) for h in range(2)], axis=1) + bx_ref[...]
    r = _sigmoid(gate_r)
    ig = _sigmoid(gate_i)
    log_a = (-LRU_C) * r * _softplus(-lam_ref[...])
    a = jnp.exp(log_a)
    mult, _ = _sqrt_pos(_one_minus_sq_exp(log_a, a))
    return r, ig, a, mult


KEPT_XC, KEPT_A, KEPT_MULT, KEPT_R, KEPT_I, N_KEPT = 0, 1, 2, 3, 4, 5


def _scan_tile(a_ref, u_ref, out_ref, carry, tm, reverse):
    row = lax.broadcasted_iota(jnp.int32, (SUBLANES, LRU_WIDTH), 0)
    groups = tm // SUBLANES

    def step(j, prev):
        jj = groups - 1 - j if reverse else j
        o = pl.multiple_of(jj * SUBLANES, SUBLANES)
        a = a_ref[pl.ds(o, SUBLANES), :]
        u = u_ref[pl.ds(o, SUBLANES), :]
        for s in (1, 2, 4):
            shift = SUBLANES - s if reverse else s
            keep = (row < SUBLANES - s) if reverse else (row >= s)
            u = jnp.where(keep, a * pltpu.roll(u, shift, 0) + u, u)
            a = jnp.where(keep, a * pltpu.roll(a, shift, 0), a)
        out = a * prev + u
        out_ref[pl.ds(o, SUBLANES), :] = out
        return out[0:1, :] if reverse else out[SUBLANES - 1:SUBLANES, :]

    return lax.fori_loop(0, groups, step, carry)


def _rec_fwd(zrec, conv_w, conv_b, wa_bd, b_a, wx_bd, b_x, lam, carried, modes):
    rows = zrec.shape[0]
    tm = _rec_tile(rows)

    def body(xr_ref, yr_ref, cw_ref, cb_ref, wa_ref, ba_ref, wx_ref, bx_ref, lam_ref, rec_ref, h_ref, kept_ref,
             xbuf, a_s, u_s, carry):
        i = pl.program_id(0)

        @pl.when(i == 0)
        def _():
            xbuf[0:SUBLANES, :] = jnp.zeros((SUBLANES, LRU_WIDTH), F32)
            carry[...] = jnp.zeros_like(carry)

        @pl.when(i > 0)
        def _():
            xbuf[0:SUBLANES, :] = xbuf[tm:tm + SUBLANES, :]

        xbuf[SUBLANES:SUBLANES + tm, :] = xr_ref[...]
        taps = _conv_taps(xbuf, tm)
        xc = cb_ref[...] + sum(cw_ref[j:j + 1, :] * taps[j] for j in range(CONV_WIDTH))
        r, ig, a, mult = _lru_gates(xc, wa_ref, ba_ref, wx_ref, bx_ref, lam_ref)
        for k, val in ((KEPT_XC, xc), (KEPT_A, a), (KEPT_MULT, mult), (KEPT_R, r), (KEPT_I, ig)):
            kept_ref[:, k * LRU_WIDTH:(k + 1) * LRU_WIDTH] = val
        grow = i * tm + lax.broadcasted_iota(jnp.int32, (tm, LRU_WIDTH), 0)
        a_s[...] = a
        u_s[...] = jnp.where(grow >= PAD_ROWS, mult * (ig * xc), 0.0)
        carry[0:1, :] = _scan_tile(a_s, u_s, h_ref, carry[0:1, :], tm, reverse=False)
        gel, _ = _gelu(yr_ref[...])
        rec_ref[...] = (gel * h_ref[...]).astype(BF16)

    vec = _full((1, LRU_WIDTH))
    bd = _full((2, LRU_HALF, LRU_HALF))
    return _hosting_call(
        body, "rec_fwd", rows // tm,
        [pl.BlockSpec((tm, LRU_WIDTH), lambda i: (i, 0)), pl.BlockSpec((tm, LRU_WIDTH), lambda i: (i, 1)),
         _full((CONV_WIDTH, LRU_WIDTH)), vec, bd, vec, bd, vec, vec],
        [pl.BlockSpec((tm, LRU_WIDTH), lambda i: (i, 0))] * 2 + [pl.BlockSpec((tm, N_KEPT * LRU_WIDTH), lambda i: (i, 0))],
        [jax.ShapeDtypeStruct((rows, LRU_WIDTH), BF16), jax.ShapeDtypeStruct((rows, LRU_WIDTH), F32),
         jax.ShapeDtypeStruct((rows, N_KEPT * LRU_WIDTH), F32)],
        [pltpu.VMEM((tm + SUBLANES, LRU_WIDTH), F32), pltpu.VMEM((tm, LRU_WIDTH), F32),
         pltpu.VMEM((tm, LRU_WIDTH), F32), pltpu.VMEM((SUBLANES, LRU_WIDTH), F32)],
        (zrec, zrec, conv_w, conv_b, wa_bd, b_a, wx_bd, b_x, lam), carried, modes)


def _out_proj_fwd(attn, rec, w_out, h0, g2, carried, modes):
    rows = h0.shape[0]
    tm = _row_tile(rows)

    def body(attn_ref, rec_ref, w_ref, h_ref, g_ref, mix_ref, h1_ref):
        mix = _mm(attn_ref[...], w_ref[0:ATTN_WIDTH, :]) + _mm(rec_ref[...], w_ref[ATTN_WIDTH:, :])
        y, _, _ = _rms_fwd(mix, g_ref[...])
        mix_ref[...] = mix
        h1_ref[...] = h_ref[...] + y

    half = pl.BlockSpec((tm, ATTN_WIDTH), lambda i: (i, 0))
    wide = pl.BlockSpec((tm, D_MODEL), lambda i: (i, 0))
    return _hosting_call(
        body, "out_proj_fwd", rows // tm,
        [half, half, _resident((D_MODEL, D_MODEL)), wide, _full((1, D_MODEL))],
        [wide, wide],
        [jax.ShapeDtypeStruct((rows, D_MODEL), F32)] * 2,
        [], (attn, rec, w_out, h0, g2), carried, modes)


FF_COLS = 1024
FF_HALF = FF_CHUNK // 2


def _hidden_at(d, half):
    return half * (D_FF // 2) + d * FF_HALF


def _ffn_up(h1, g3, w1_halves, carried, modes):
    rows = h1.shape[0]
    tm = _row_tile(rows)

    def body(h_ref, g_ref, wa_ref, wb_ref, act_ref, u_ref):
        u, _, _ = _rms_fwd(h_ref[...], g_ref[...])
        u = u.astype(BF16)
        u_ref[...] = u
        for half, w_ref in enumerate((wa_ref, wb_ref)):
            for d in range(N_DEV):
                c = _hidden_at(d, half)
                a1 = jnp.maximum(_mm(u, w_ref[d]), 0.0)
                act_ref[:, c:c + FF_HALF] = (a1 * a1).astype(BF16)

    wide = pl.BlockSpec((tm, D_MODEL), lambda i: (i, 0))
    return _hosting_call(
        body, "ffn_up", rows // tm,
        [wide, _full((1, D_MODEL))] + [_resident((N_DEV, D_MODEL, FF_HALF))] * 2,
        [pl.BlockSpec((tm, D_FF), lambda i: (i, 0)), wide],
        [jax.ShapeDtypeStruct((rows, D_FF), BF16), jax.ShapeDtypeStruct((rows, D_MODEL), BF16)],
        [], (h1, g3, *w1_halves), carried, modes)


def _ffn_down_loss(act, w2_halves, h1, target, g4):
    rows = h1.shape[0]
    tm = _row_tile(rows)
    steps = rows // tm
    kh = D_FF // 2

    def body(act_ref, wa_ref, wb_ref, h_ref, g_ref, t_hbm, dy_ref, df_ref, dg_ref, loss_ref, buf, sem):
        i = pl.program_id(0)
        slot = _frame_rows(t_hbm, buf, sem, i, steps, tm)

        @pl.when(i == 0)
        def _():
            dg_ref[...] = jnp.zeros_like(dg_ref)
            loss_ref[...] = jnp.zeros_like(loss_ref)
            buf[0, 0:BLOCK, :] = jnp.zeros((BLOCK, D_MODEL), F32)

        g = g_ref[...]
        f = _mm(act_ref[:, :kh], wa_ref[...]) + _mm(act_ref[:, kh:], wb_ref[...])
        y, fhat, rstd = _rms_fwd(f, g)
        grow = i * tm + lax.broadcasted_iota(jnp.int32, (tm, D_MODEL), 0)
        err = jnp.where(grow >= BLOCK, h_ref[...] + y - buf[slot], 0.0)
        loss_ref[...] += (0.5 / D_MODEL) * jnp.sum(err * err)
        dy = err * (1.0 / D_MODEL)
        df, dg = _rms_bwd(dy, fhat, rstd, g)
        dy_ref[...] = dy
        df_ref[...] = df.astype(BF16)
        dg_ref[...] += dg

    wide = pl.BlockSpec((tm, D_MODEL), lambda i: (i, 0))
    return pl.pallas_call(
        body, name="ffn_down_loss", grid=(steps,),
        in_specs=[pl.BlockSpec((tm, D_FF), lambda i: (i, 0)), _resident((kh, D_MODEL)), _resident((kh, D_MODEL)), wide,
                  _full((1, D_MODEL)), ANY_SPACE],
        out_specs=[wide, wide, _full((1, D_MODEL)), _full((SUBLANES, LANES))],
        out_shape=[jax.ShapeDtypeStruct((rows, D_MODEL), F32), jax.ShapeDtypeStruct((rows, D_MODEL), BF16),
                   jax.ShapeDtypeStruct((1, D_MODEL), F32), jax.ShapeDtypeStruct((SUBLANES, LANES), F32)],
        scratch_shapes=_frame_scratch(tm),
        compiler_params=_params(("arbitrary",)),
    )(act, *w2_halves, h1, g4, target)


def _ffn_bwd_act(df, w2t_halves, act):
    rows = df.shape[0]
    tm = _row_tile(rows)

    def body(df_ref, wa_ref, wb_ref, act_ref, da_ref):
        df_t = df_ref[...]
        for half, w_ref in enumerate((wa_ref, wb_ref)):
            for d in range(N_DEV):
                cols = slice(_hidden_at(d, half), _hidden_at(d, half) + FF_HALF)
                dact = _mm(df_t, w_ref[d])
                relu_a1, _ = _sqrt_pos(act_ref[:, cols].astype(F32))
                da_ref[:, cols] = (dact * (2.0 * relu_a1)).astype(BF16)

    hidden = pl.BlockSpec((tm, D_FF), lambda i: (i, 0))
    return pl.pallas_call(
        body, name="ffn_bwd_act", grid=(rows // tm,),
        in_specs=[pl.BlockSpec((tm, D_MODEL), lambda i: (i, 0))] + [_resident((N_DEV, D_MODEL, FF_HALF))] * 2 + [hidden],
        out_specs=hidden,
        out_shape=jax.ShapeDtypeStruct((rows, D_FF), BF16),
        compiler_params=_params(("parallel",)),
    )(df, *w2t_halves, act)


def _ffn_bwd_x(da, w1t, h1, dy, g3, carried, modes):
    rows = h1.shape[0]
    tm = _row_tile(rows)

    def body(da_ref, w_ref, h_ref, dy_ref, g_ref, dh_ref, dg_ref):
        @pl.when(pl.program_id(0) == 0)
        def _():
            dg_ref[...] = jnp.zeros_like(dg_ref)

        g = g_ref[...]
        _, xhat, rstd = _rms_fwd(h_ref[...], g)
        dx, dg = _rms_bwd(_mm(da_ref[...], w_ref[...]), xhat, rstd, g)
        dh_ref[...] = dy_ref[...] + dx
        dg_ref[...] += dg

    wide = pl.BlockSpec((tm, D_MODEL), lambda i: (i, 0))
    return _hosting_call(
        body, "ffn_bwd_x", rows // tm,
        [pl.BlockSpec((tm, D_FF), lambda i: (i, 0)), _resident((D_FF, D_MODEL)), wide, wide, _full((1, D_MODEL))],
        [wide, _full((1, D_MODEL))],
        [jax.ShapeDtypeStruct((rows, D_MODEL), F32), jax.ShapeDtypeStruct((1, D_MODEL), F32)],
        [], (da, w1t, h1, dy, g3), carried, modes)


def _ffn_bwd_weights(u2, da, act, df):
    rows = u2.shape[0]
    tb = _big_tile(rows)
    steps = rows // tb
    per = FF_COLS // FF_HALF

    def body(u_ref, da_ref, act_ref, df_ref, dw1_ref, dw2_ref, acc1, acc2):
        i = pl.program_id(1)

        @pl.when(i == 0)
        def _():
            acc1[...] = jnp.zeros_like(acc1)
            acc2[...] = jnp.zeros_like(acc2)

        acc1[...] += _mm_tn(u_ref[...], da_ref[...])
        acc2[...] += _mm_tn(act_ref[...], df_ref[...])

        @pl.when(i == steps - 1)
        def _():
            for p in range(per):
                c = p * FF_HALF
                dw1_ref[p] = acc1[:, c:c + FF_HALF].astype(BF16)
                dw2_ref[p] = acc2[c:c + FF_HALF, :].astype(BF16)

    wide = pl.BlockSpec((tb, D_MODEL), lambda j, i: (i, 0))
    chunk = pl.BlockSpec((tb, FF_COLS), lambda j, i: (i, j))
    return pl.pallas_call(
        body, name="ffn_bwd_weights", grid=(D_FF // FF_COLS, steps),
        in_specs=[wide, chunk, chunk, wide],
        out_specs=[pl.BlockSpec((None, per, D_MODEL, FF_HALF), lambda j, i: (j // 2, j % 2, 0, 0)),
                   pl.BlockSpec((per, FF_HALF, D_MODEL), lambda j, i: (j % 2, j // 2, 0))],
        out_shape=[jax.ShapeDtypeStruct((2, N_DEV, D_MODEL, FF_HALF), BF16),
                   jax.ShapeDtypeStruct((N_DEV, FF_CHUNK, D_MODEL), BF16)],
        scratch_shapes=[pltpu.VMEM((D_MODEL, FF_COLS), F32), pltpu.VMEM((FF_COLS, D_MODEL), F32)],
        compiler_params=_params(("parallel", "arbitrary")),
    )(u2, da, act, df)


def _out_proj_bwd(dh1, mix, g2, w_out_t, attn, rec, carried, modes):
    rows = dh1.shape[0]
    tm = _row_tile(rows)
    steps = rows // tm

    def body(dh_ref, mix_ref, g_ref, w_ref, attn_ref, rec_ref, dattn_ref, drec_ref, dw_ref, dg_ref, acc):
        i = pl.program_id(0)

        @pl.when(i == 0)
        def _():
            acc[...] = jnp.zeros_like(acc)
            dg_ref[...] = jnp.zeros_like(dg_ref)

        g = g_ref[...]
        _, xhat, rstd = _rms_fwd(mix_ref[...], g)
        dmix, dg = _rms_bwd(dh_ref[...], xhat, rstd, g)
        dmix = dmix.astype(BF16)
        dg_ref[...] += dg
        din = _mm(dmix, w_ref[...])
        dattn_ref[...] = din[:, :ATTN_WIDTH].astype(BF16)
        drec_ref[...] = din[:, ATTN_WIDTH:]
        acc[0:ATTN_WIDTH, :] += _mm_tn(attn_ref[...], dmix)
        acc[ATTN_WIDTH:, :] += _mm_tn(rec_ref[...], dmix)

        @pl.when(i == steps - 1)
        def _():
            dw_ref[...] = acc[...].astype(BF16)

    half = pl.BlockSpec((tm, ATTN_WIDTH), lambda i: (i, 0))
    wide = pl.BlockSpec((tm, D_MODEL), lambda i: (i, 0))
    return _hosting_call(
        body, "out_proj_bwd", steps,
        [wide, wide, _full((1, D_MODEL)), _resident((D_MODEL, D_MODEL)), half, half],
        [half, half, _full((D_MODEL, D_MODEL)), _full((1, D_MODEL))],
        [jax.ShapeDtypeStruct((rows, ATTN_WIDTH), BF16), jax.ShapeDtypeStruct((rows, LRU_WIDTH), F32),
         jax.ShapeDtypeStruct((D_MODEL, D_MODEL), BF16), jax.ShapeDtypeStruct((1, D_MODEL), F32)],
        [pltpu.VMEM((D_MODEL, D_MODEL), F32)],
        (dh1, mix, g2, w_out_t, attn, rec), carried, modes)


def _attn_bwd(qkv, dattn, sinks, bias, carried, modes):
    rows = qkv.shape[0]
    tm = _row_tile(rows)
    nbt, nt = tm // BLOCK, rows // tm

    def body(sink_ref, bias_ref, do_ref, q_ref, kp_ref, kc_ref, vp_ref, vc_ref, dq_ref, dkv_ref, dsink_ref, dk_c, dv_c):
        i = pl.program_id(0)

        @pl.when(i == 0)
        def _():
            dk_c[...] = jnp.zeros_like(dk_c)
            dv_c[...] = jnp.zeros_like(dv_c)
            dsink_ref[...] = jnp.zeros_like(dsink_ref)

        @pl.when(i < nt)
        def _():
            dk_late, dv_late = dk_c[...], dv_c[...]
            dsink_rows = [jnp.zeros((1, LANES), F32)] * ATTN_HEADS
            for b in range(nbt):
                blk = slice(b * BLOCK, (b + 1) * BLOCK)
                bias_t = _bias_of_block(bias_ref, i * nbt + b)
                dq_parts, dk_parts, dv_parts = [], [], []
                for kv in range(KV_HEADS):
                    k2 = _keys_of_block(kp_ref, kc_ref, b, kv)
                    v2 = _keys_of_block(vp_ref, vc_ref, b, kv)
                    q4 = _heads(q_ref, blk, kv * GQA_GROUP, GQA_GROUP)
                    do4 = _heads(do_ref, blk, kv * GQA_GROUP, GQA_GROUP)
                    pn, psink = _attn_probs(k2, q4, bias_t, _sink_row(sink_ref, kv))
                    dpn = _mm_nt(v2, do4)
                    delta = jnp.sum(pn * dpn, axis=0, keepdims=True)
                    ds = ((pn * (dpn - delta)) * (HEAD_DIM ** -0.5)).astype(BF16)
                    dqt = _mm_tn(k2, ds)
                    dq_parts += [dqt[:, g * BLOCK:(g + 1) * BLOCK] for g in range(GQA_GROUP)]
                    dk_parts.append(_mm(ds, q4))
                    dv_parts.append(_mm(pn.astype(BF16), do4))
                    sd = psink * delta
                    for g in range(GQA_GROUP):
                        h = kv * GQA_GROUP + g
                        dsink_rows[h] = dsink_rows[h] - jnp.sum(sd[:, g * BLOCK:(g + 1) * BLOCK])
                dq_ref[blk, :] = _from_head_major(dq_parts).astype(BF16)
                dk2 = jnp.concatenate(dk_parts, axis=1)
                dv2 = jnp.concatenate(dv_parts, axis=1)
                dkv_ref[blk, 0:KV_WIDTH] = (dk_late + dk2[0:BLOCK]).astype(BF16)
                dkv_ref[blk, KV_WIDTH:] = (dv_late + dv2[0:BLOCK]).astype(BF16)
                dk_late, dv_late = dk2[BLOCK:], dv2[BLOCK:]
            dk_c[...] = dk_late
            dv_c[...] = dv_late
            dsink_ref[...] += jnp.concatenate(dsink_rows, axis=0)

        @pl.when(i == nt)
        def _():
            dkv_ref[...] = jnp.zeros_like(dkv_ref)
            dkv_ref[0:BLOCK, 0:KV_WIDTH] = dk_c[...].astype(BF16)
            dkv_ref[0:BLOCK, KV_WIDTH:] = dv_c[...].astype(BF16)

    tile_of = lambda i: jnp.minimum(i, nt - 1)
    tile = pl.BlockSpec((tm, ATTN_WIDTH), lambda i: (tile_of(i), 0))
    return _hosting_call(
        body, "attn_bwd", nt + 1,
        [pl.BlockSpec(memory_space=pltpu.SMEM), _resident((N_BIAS, 2 * BLOCK, GQA_GROUP * BLOCK)), tile]
        + _attn_specs(tm, tile_of),
        [tile, pl.BlockSpec((tm, 2 * KV_WIDTH), lambda i: (i, 0)), _full((ATTN_HEADS, LANES))],
        [jax.ShapeDtypeStruct((rows, ATTN_WIDTH), BF16), jax.ShapeDtypeStruct((rows + tm, 2 * KV_WIDTH), BF16),
         jax.ShapeDtypeStruct((ATTN_HEADS, LANES), F32)],
        [pltpu.VMEM((BLOCK, KV_WIDTH), F32), pltpu.VMEM((BLOCK, KV_WIDTH), F32)],
        (sinks, bias, dattn, qkv, qkv, qkv, qkv, qkv), carried, modes)


ROW_CONV_B, ROW_B_A, ROW_B_X, ROW_LAMBDA = 4, 5, 6, 7


def _rec_bwd(drec, zrec, h, kept, conv_w, wa_bd, wx_bd, lam, carried, modes):
    rows = zrec.shape[0]
    tm = _rec_tile(rows)
    nt = rows // tm
    per = tm // SUBLANES

    def body(drec_ref, xr_ref, yr_ref, h_ref, xc_ref, a_ref, mult_ref, r_ref, ig_ref, hhalo_ref, cw_ref, wa_ref, wx_ref,
             lam_ref, drz_ref, small_ref, dwa_ref, dwx_ref, hbuf, abuf, u_s, g_s, dbuf, carry):
        s = pl.program_id(0)
        i = nt - 1 - s

        @pl.when(s == 0)
        def _():
            small_ref[...] = jnp.zeros_like(small_ref)
            dwa_ref[...] = jnp.zeros_like(dwa_ref)
            dwx_ref[...] = jnp.zeros_like(dwx_ref)
            carry[...] = jnp.zeros_like(carry)
            abuf[tm:tm + SUBLANES, :] = jnp.zeros((SUBLANES, LRU_WIDTH), F32)
            dbuf[tm:tm + SUBLANES, :] = jnp.zeros((SUBLANES, LRU_WIDTH), F32)

        hbuf[0:SUBLANES, :] = jnp.where(i == 0, 0.0, hhalo_ref[...])
        hbuf[SUBLANES:SUBLANES + tm, :] = h_ref[...]

        xc, a, mult, r, ig = xc_ref[...], a_ref[...], mult_ref[...], r_ref[...], ig_ref[...]
        halves = _lru_halves(xc)
        inv_mult = pl.reciprocal(mult, approx=True)

        yr = yr_ref[...]
        gel, t = _gelu(yr)
        drec_t = drec_ref[...]
        dyr = drec_t * h_ref[...] * _gelu_grad(yr, t)

        abuf[0:tm, :] = a
        u_s[...] = drec_t * gel
        a_next = abuf[pl.ds(1, tm), :]
        abuf[0:tm, :] = a_next
        carry[0:1, :] = _scan_tile(abuf, u_s, g_s, carry[0:1, :], tm, reverse=True)
        abuf[tm:tm + 1, :] = a[0:1, :]
        g = g_s[...]

        grow = i * tm + lax.broadcasted_iota(jnp.int32, (tm, LRU_WIDTH), 0)
        du = jnp.where(grow >= PAD_ROWS, g, 0.0)
        da = g * hbuf[pl.ds(SUBLANES - 1, tm), :]
        dmult = du * (ig * xc)
        dig = du * (mult * xc)
        dxc = du * (mult * ig)
        dlog_a = da * a - dmult * (a * a * inv_mult)
        sp = _softplus(-lam_ref[...])
        dgr = (dlog_a * (-LRU_C) * sp) * (r * (1.0 - r))
        dgi = dig * (ig * (1.0 - ig))
        dlam = jnp.sum(dlog_a * r, axis=0, keepdims=True) * (LRU_C * _sigmoid(-lam_ref[...]))
        dgr_b = [dgr[:, hh * LRU_HALF:(hh + 1) * LRU_HALF].astype(BF16) for hh in range(2)]
        dgi_b = [dgi[:, hh * LRU_HALF:(hh + 1) * LRU_HALF].astype(BF16) for hh in range(2)]
        dxc = dxc + jnp.concatenate(
            [_mm_nt(dgr_b[hh], wa_ref[hh]) + _mm_nt(dgi_b[hh], wx_ref[hh]) for hh in range(2)], axis=1)
        for hh in range(2):
            dwa_ref[hh] += _mm_tn(halves[hh], dgr_b[hh])
            dwx_ref[hh] += _mm_tn(halves[hh], dgi_b[hh])

        dbuf[0:tm, :] = dxc
        ahead = [dbuf[pl.ds(CONV_WIDTH - 1 - j, tm), :] for j in range(CONV_WIDTH)]
        dxr = sum(cw_ref[j:j + 1, :] * ahead[j] for j in range(CONV_WIDTH))
        dbuf[tm:tm + SUBLANES, :] = dxc[0:SUBLANES, :]
        drz_ref[:, 0:LRU_WIDTH] = dxr.astype(BF16)
        drz_ref[:, LRU_WIDTH:] = dyr.astype(BF16)

        xr = xr_ref[...]
        upd = [jnp.sum(xr * ahead[j], axis=0, keepdims=True) for j in range(CONV_WIDTH)]
        upd += [jnp.sum(dxc, axis=0, keepdims=True), jnp.sum(dgr, axis=0, keepdims=True),
                jnp.sum(dgi, axis=0, keepdims=True), dlam]
        small_ref[...] += jnp.concatenate(upd, axis=0)

    rev = lambda s: nt - 1 - s
    halo = lambda s: jnp.maximum(rev(s) * per - 1, 0)
    cols = lambda k: pl.BlockSpec((tm, LRU_WIDTH), lambda s: (rev(s), k))
    halo0 = pl.BlockSpec((SUBLANES, LRU_WIDTH), lambda s: (halo(s), 0))
    bd = _full((2, LRU_HALF, LRU_HALF))
    big = pltpu.VMEM((tm + SUBLANES, LRU_WIDTH), F32)
    tile = pltpu.VMEM((tm, LRU_WIDTH), F32)
    kept_cols = [cols(k) for k in (KEPT_XC, KEPT_A, KEPT_MULT, KEPT_R, KEPT_I)]
    return _hosting_call(
        body, "rec_bwd", nt,
        [cols(0), cols(0), cols(1), cols(0)] + kept_cols
        + [halo0, _full((CONV_WIDTH, LRU_WIDTH)), bd, bd, _full((1, LRU_WIDTH))],
        [pl.BlockSpec((tm, 2 * LRU_WIDTH), lambda s: (rev(s), 0)), _full((SUBLANES, LRU_WIDTH)), bd, bd],
        [jax.ShapeDtypeStruct((rows, 2 * LRU_WIDTH), BF16), jax.ShapeDtypeStruct((SUBLANES, LRU_WIDTH), F32),
         jax.ShapeDtypeStruct((2, LRU_HALF, LRU_HALF), F32), jax.ShapeDtypeStruct((2, LRU_HALF, LRU_HALF), F32)],
        [big, big, tile, tile, big, pltpu.VMEM((SUBLANES, LRU_WIDTH), F32)],
        (drec, zrec, zrec, h) + (kept,) * N_KEPT + (h, conv_w, wa_bd, wx_bd, lam), carried, modes)


DZ_CUTS = (0, ATTN_WIDTH, QKV_WIDTH, IN_WIDTH)


def _dz_specs(tm):
    return [pl.BlockSpec((tm, DZ_CUTS[p + 1] - DZ_CUTS[p]), lambda i: (i, 0)) for p in range(3)]


def _in_proj_bwd_x(h0, g1, dh1, dq, dkv, drz, w_in_t, carried, modes):
    rows = h0.shape[0]
    tm = _row_tile(rows)

    def body(h_ref, g_ref, dh1_ref, dq_ref, dkv_ref, drz_ref, w_ref, dh0_ref, dg_ref):
        @pl.when(pl.program_id(0) == 0)
        def _():
            dg_ref[...] = jnp.zeros_like(dg_ref)

        g = g_ref[...]
        _, xhat, rstd = _rms_fwd(h_ref[...], g)
        parts = (dq_ref[...], dkv_ref[...], drz_ref[...])
        du = sum(_mm(parts[p], w_ref[DZ_CUTS[p]:DZ_CUTS[p + 1], :]) for p in range(3))
        dx, dg = _rms_bwd(du, xhat, rstd, g)
        dh0_ref[...] = dh1_ref[...] + dx
        dg_ref[...] += dg

    wide = pl.BlockSpec((tm, D_MODEL), lambda i: (i, 0))
    return _hosting_call(
        body, "in_proj_bwd_x", rows // tm,
        [wide, _full((1, D_MODEL)), wide] + _dz_specs(tm) + [_resident((IN_WIDTH, D_MODEL))],
        [wide, _full((1, D_MODEL))],
        [jax.ShapeDtypeStruct((rows, D_MODEL), F32), jax.ShapeDtypeStruct((1, D_MODEL), F32)],
        [], (h0, g1, dh1, dq, dkv, drz, w_in_t), carried, modes)


def _in_proj_bwd_w(u1, dq, dkv, drz, carried, modes):
    rows = u1.shape[0]
    tb = _big_tile(rows)

    def body(u_ref, dq_ref, dkv_ref, drz_ref, dw_ref):
        @pl.when(pl.program_id(0) == 0)
        def _():
            dw_ref[...] = jnp.zeros_like(dw_ref)

        u = u_ref[...]
        for p, ref in enumerate((dq_ref, dkv_ref, drz_ref)):
            dw_ref[:, DZ_CUTS[p]:DZ_CUTS[p + 1]] += _mm_tn(u, ref[...])

    return _hosting_call(
        body, "in_proj_bwd_w", rows // tb,
        [pl.BlockSpec((tb, D_MODEL), lambda i: (i, 0))] + _dz_specs(tb),
        [_full((D_MODEL, IN_WIDTH))],
        [jax.ShapeDtypeStruct((D_MODEL, IN_WIDTH), F32)],
        [], (u1, dq, dkv, drz), carried, modes)


def _adamw_math(w, m, v, g):
    nm = ADAM_B1 * m + (1.0 - ADAM_B1) * g
    nv = ADAM_B2 * v + (1.0 - ADAM_B2) * (g * g)
    m_hat = nm / (1.0 - ADAM_B1 ** ADAM_STEP)
    v_hat = nv / (1.0 - ADAM_B2 ** ADAM_STEP)
    return (-ADAM_LR) * (m_hat / (jnp.sqrt(v_hat) + ADAM_EPS) + ADAM_WD * w), nm, nv


SMALL_NAMES = ("conv_b", "b_a", "b_x", "lru_lambda", "attn_sinks", "g_post_mix", "g_pre_ffn", "g_post_ffn",
               "w_a", "w_x")
PACK_WIDTH = 1024


def _rows_view(a):
    return (a.size // PACK_WIDTH, PACK_WIDTH) if a.size >= PACK_WIDTH else (1, a.size)


def _pack_rows(vals):
    single = [name for name in SMALL_NAMES if _rows_view(vals[name])[0] == 1]
    assert len(single) == SUBLANES and SMALL_NAMES[:SUBLANES] == tuple(single)
    row = lax.broadcasted_iota(jnp.int32, (SUBLANES, PACK_WIDTH), 0)
    first = jnp.zeros((SUBLANES, PACK_WIDTH), F32)
    for k, name in enumerate(single):
        a = vals[name].reshape(1, -1)
        first = jnp.where(row == k, jnp.pad(a, ((0, 0), (0, PACK_WIDTH - a.shape[1]))), first)
    return jnp.concatenate([first] + [vals[name].reshape(_rows_view(vals[name])) for name in SMALL_NAMES[SUBLANES:]], axis=0)


def _adamw_small(weights, mom_m, mom_v, parts, loss_parts):
    n = len(SMALL_NAMES)
    views = [_rows_view(weights[name]) for name in SMALL_NAMES]

    def body(*refs):
        w_refs, m_refs, v_refs = refs[:n], refs[n:2 * n], refs[2 * n:3 * n]
        p_ref, l_ref, loss_ref = refs[3 * n], refs[3 * n + 1], refs[3 * n + 2]
        outs = refs[3 * n + 3:]
        row = 0
        for k, (nr, c) in enumerate(views):
            g = p_ref[0, row:row + nr, 0:c]
            for s in range(1, N_DEV):
                g = g + p_ref[s, row:row + nr, 0:c]
            g_ref, d_ref, nm_ref, nv_ref = outs[4 * k:4 * k + 4]
            g_ref[...] = g
            d_ref[...], nm_ref[...], nv_ref[...] = _adamw_math(w_refs[k][...], m_refs[k][...], v_refs[k][...], g)
            row += nr
        total = l_ref[0]
        for s in range(1, N_DEV):
            total = total + l_ref[s]
        loss_ref[...] = total

    args = [src[name].reshape(view) for src in (weights, mom_m, mom_v) for name, view in zip(SMALL_NAMES, views)]
    res = pl.pallas_call(
        body, name="adamw_small",
        out_shape=[jax.ShapeDtypeStruct(loss_parts.shape[1:], F32)]
                  + [jax.ShapeDtypeStruct(view, F32) for view in views for _ in range(4)],
        compiler_params=pltpu.CompilerParams(vmem_limit_bytes=VMEM_LIMIT),
    )(*args, parts, loss_parts)
    out = {name: tuple(t.reshape(weights[name].shape) for t in res[1 + 4 * k:5 + 4 * k]) for k, name in enumerate(SMALL_NAMES)}
    return res[0], out


def _adamw(w, m, v, parts, name):
    rows, cols = w.shape
    tr = next((t for t in (256, 128) if rows % t == 0), rows)
    parts = parts if isinstance(parts, (list, tuple)) else [parts]

    def body(w_ref, m_ref, v_ref, *refs):
        p_refs, (g_ref, d_ref, nm_ref, nv_ref) = refs[:len(parts)], refs[len(parts):]

        def total(p_ref):
            g = p_ref[0].astype(F32)
            for s in range(1, N_DEV):
                g = g + p_ref[s].astype(F32)
            return g

        g = jnp.concatenate([total(p_ref) for p_ref in p_refs], axis=1) if len(parts) > 1 else total(p_refs[0])
        g_ref[...] = g
        d_ref[...], nm_ref[...], nv_ref[...] = _adamw_math(w_ref[...], m_ref[...], v_ref[...], g)

    blk = pl.BlockSpec((tr, cols), lambda i: (i, 0))
    return pl.pallas_call(
        body, name=name, grid=(rows // tr,),
        in_specs=[blk, blk, blk] + [pl.BlockSpec((N_DEV, tr, p.shape[2]), lambda i: (0, i, 0)) for p in parts],
        out_specs=[blk] * 4,
        out_shape=[jax.ShapeDtypeStruct((rows, cols), F32)] * 4,
        compiler_params=_params(("parallel",)),
    )(w, m, v, *parts)


def _cols_from_shards(g):
    return jnp.transpose(g, (1, 0, 2)).reshape(g.shape[1], N_DEV * g.shape[2])


def _cols_to_shards(a):
    r, c = a.shape
    return jnp.transpose(a.reshape(r, N_DEV, c // N_DEV), (1, 0, 2))


def _block_diag(w):
    per = LRU_HALF // LRU_BLOCK
    w = w.reshape(2, per, LRU_BLOCK, LRU_BLOCK)
    eye = jnp.eye(per, dtype=w.dtype)
    return (w[:, :, :, None, :] * eye[None, :, None, :, None]).reshape(2, LRU_HALF, LRU_HALF)


def _block_diag_extract(t):
    per = LRU_HALF // LRU_BLOCK
    t = t.reshape(2, per, LRU_BLOCK, per, LRU_BLOCK)
    return jnp.stack([t[:, b, :, b, :] for b in range(per)], axis=1).reshape(LRU_BLOCKS, LRU_BLOCK, LRU_BLOCK)


def kernel(x, meta_tokens, g_pre_mix, w_in, conv_w, conv_b, w_a, b_a, w_x, b_x, lru_lambda, attn_sinks, w_out, g_post_mix, g_pre_ffn, w_ff1, w_ff2, g_post_ffn, loss_target, m_meta_tokens, m_g_pre_mix, m_w_in, m_conv_w, m_conv_b, m_w_a, m_b_a, m_w_x, m_b_x, m_lru_lambda, m_attn_sinks, m_w_out, m_g_post_mix, m_g_pre_ffn, m_w_ff1, m_w_ff2, m_g_post_ffn, v_meta_tokens, v_g_pre_mix, v_w_in, v_conv_w, v_conv_b, v_w_a, v_b_a, v_w_x, v_b_x, v_lru_lambda, v_attn_sinks, v_w_out, v_g_post_mix, v_g_pre_ffn, v_w_ff1, v_w_ff2, v_g_post_ffn):
    weights = dict(meta_tokens=meta_tokens, g_pre_mix=g_pre_mix, w_in=w_in, conv_w=conv_w, conv_b=conv_b, w_a=w_a,
                   b_a=b_a, w_x=w_x, b_x=b_x, lru_lambda=lru_lambda, attn_sinks=attn_sinks, w_out=w_out,
                   g_post_mix=g_post_mix, g_pre_ffn=g_pre_ffn, w_ff1=w_ff1, w_ff2=w_ff2, g_post_ffn=g_post_ffn)
    mom_m = dict(meta_tokens=m_meta_tokens, g_pre_mix=m_g_pre_mix, w_in=m_w_in, conv_w=m_conv_w, conv_b=m_conv_b,
                 w_a=m_w_a, b_a=m_b_a, w_x=m_w_x, b_x=m_b_x, lru_lambda=m_lru_lambda, attn_sinks=m_attn_sinks,
                 w_out=m_w_out, g_post_mix=m_g_post_mix, g_pre_ffn=m_g_pre_ffn, w_ff1=m_w_ff1, w_ff2=m_w_ff2,
                 g_post_ffn=m_g_post_ffn)
    mom_v = dict(meta_tokens=v_meta_tokens, g_pre_mix=v_g_pre_mix, w_in=v_w_in, conv_w=v_conv_w, conv_b=v_conv_b,
                 w_a=v_w_a, b_a=v_b_a, w_x=v_w_x, b_x=v_b_x, lru_lambda=v_lru_lambda, attn_sinks=v_attn_sinks,
                 w_out=v_w_out, g_post_mix=v_g_post_mix, g_pre_ffn=v_g_pre_ffn, w_ff1=v_w_ff1, w_ff2=v_w_ff2,
                 g_post_ffn=v_g_post_ffn)
    order = list(weights)

    (g_win, g_meta, g_cw) = _gather_two_level([w_in[0].astype(BF16), meta_tokens, conv_w[0]], "gather_first")
    w_in_full = _cols_from_shards(g_win)
    meta_full = _cols_from_shards(g_meta)
    conv_w_full = _cols_from_shards(g_cw)

    head = jnp.concatenate([jnp.zeros((PAD_ROWS, D_MODEL), F32), meta_full], axis=0)
    wa_bd = _block_diag(w_a[0]).astype(BF16)
    wx_bd = _block_diag(w_x[0]).astype(BF16)
    bias = _attn_bias()

    w1_shard = w_ff1[0].astype(BF16)
    (qkv, zrec, u1, h0), (g_wout,) = _in_proj_fwd(head, x[0], g_pre_mix, w_in_full, [w_out[0].astype(BF16)], ["gather"])
    (attn,), (w1a,) = _attn_fwd(qkv, attn_sinks, bias, [w1_shard[:, :FF_HALF]], ["gather"])
    (rec, h_lru, kept), (w1b,) = _rec_fwd(zrec, conv_w_full, conv_b, wa_bd, b_a, wx_bd, b_x, lru_lambda,
                                         [w1_shard[:, FF_HALF:]], ["gather"])
    w_out_full = g_wout.reshape(D_MODEL, D_MODEL)
    w2_shard = w_ff2[0].astype(BF16)
    (mix, h1), (w2a,) = _out_proj_fwd(attn, rec, w_out_full, h0, g_post_mix, [w2_shard[:FF_HALF]], ["gather"])
    (act, u2), (w2b,) = _ffn_up(h1, g_pre_ffn, (w1a, w1b), [w2_shard[FF_HALF:]], ["gather"])
    w2_halves = [w.reshape(D_FF // 2, D_MODEL) for w in (w2a, w2b)]
    dy, df, dg_post_ffn, loss_acc = _ffn_down_loss(act, w2_halves, h1, loss_target[0], g_post_ffn)

    da1 = _ffn_bwd_act(df, [jnp.transpose(w, (0, 2, 1)) for w in (w2a, w2b)], act)
    dw1h, dw2g = _ffn_bwd_weights(u2, da1, act, df)
    w1t = jnp.concatenate([jnp.transpose(w, (0, 2, 1)).reshape(D_FF // 2, D_MODEL) for w in (w1a, w1b)], axis=0)
    (dh1, dg_pre_ffn), (p_w1a,) = _ffn_bwd_x(da1, w1t, h1, dy, g_pre_ffn, [dw1h[0]], ["scatter"])
    (dattn, drec, dw_out, dg_post_mix), (p_w1b,) = _out_proj_bwd(dh1, mix, g_post_mix, w_out_full.T, attn, rec,
                                                                [dw1h[1]], ["scatter"])
    (dq, dkv_late, dsinks), (p_w2,) = _attn_bwd(qkv, dattn, attn_sinks, bias, [dw2g], ["scatter"])
    dkv = dkv_late[BLOCK:BLOCK + qkv.shape[0]]
    (drz, rec_small, dwa_bd, dwx_bd), (p_wout,) = _rec_bwd(
        drec, zrec, h_lru, kept, conv_w_full, wa_bd, wx_bd, lru_lambda,
        [dw_out.reshape(N_DEV, D_MODEL // N_DEV, D_MODEL)], ["scatter"])
    small_grads = dict(
        conv_b=rec_small[ROW_CONV_B], w_a=_block_diag_extract(dwa_bd), b_a=rec_small[ROW_B_A],
        w_x=_block_diag_extract(dwx_bd), b_x=rec_small[ROW_B_X], lru_lambda=rec_small[ROW_LAMBDA],
        attn_sinks=dsinks[:, 0], g_post_mix=dg_post_mix, g_pre_ffn=dg_pre_ffn, g_post_ffn=dg_post_ffn)
    (dw_in,), (p_cw, p_small) = _in_proj_bwd_w(
        u1, dq, dkv, drz, [_cols_to_shards(rec_small[0:CONV_WIDTH]), _pack_rows(small_grads)], ["scatter", "gather"])
    (dh0, dg_pre_mix), (p_win,) = _in_proj_bwd_x(
        h0, g_pre_mix, dh1, dq, dkv, drz, w_in_full.T, [_cols_to_shards(dw_in).astype(BF16)], ["scatter"])
    p_meta, p_gpm, p_loss = _exchange([_cols_to_shards(dh0[PAD_ROWS:BLOCK]), dg_pre_mix, loss_acc],
                                      ["scatter", "gather", "gather"], "exchange_last")

    res = {}
    res["g_pre_mix"] = _adamw(g_pre_mix, m_g_pre_mix, v_g_pre_mix, p_gpm, "adamw_g_pre_mix")
    res["w_in"] = _adamw(w_in[0], m_w_in[0], v_w_in[0], p_win, "adamw_w_in")
    res["w_out"] = _adamw(w_out[0], m_w_out[0], v_w_out[0], p_wout, "adamw_w_out")
    res["w_ff1"] = _adamw(w_ff1[0], m_w_ff1[0], v_w_ff1[0], [p_w1a, p_w1b], "adamw_w_ff1")
    res["w_ff2"] = _adamw(w_ff2[0], m_w_ff2[0], v_w_ff2[0], p_w2, "adamw_w_ff2")
    res["meta_tokens"] = _adamw(meta_tokens, m_meta_tokens, v_meta_tokens, p_meta, "adamw_meta")
    res["conv_w"] = _adamw(conv_w[0], m_conv_w[0], v_conv_w[0], p_cw, "adamw_conv_w")
    for name in ("w_in", "w_out", "w_ff1", "w_ff2", "conv_w"):
        res[name] = tuple(t[None] for t in res[name])
    loss_total, small = _adamw_small(weights, mom_m, mom_v, p_small, p_loss)
    res.update(small)

    grad_x = dh0[BLOCK:][None]
    outs = [loss_total[0, 0], grad_x]
    for k in range(4):
        outs += [res[name][k] for name in order]
    return tuple(outs)
```

```python
import jax
import jax.numpy as jnp
import numpy as np
from jax import lax
from jax.experimental import pallas as pl
from jax.experimental.pallas import tpu as pltpu

F32 = jnp.float32
BF16 = jnp.bfloat16

D_MODEL = 1024
N_META = 16
HEAD_DIM = 64
ATTN_HEADS = 8
KV_HEADS = 2
GQA_GROUP = ATTN_HEADS // KV_HEADS
ATTN_WIDTH = ATTN_HEADS * HEAD_DIM
KV_WIDTH = KV_HEADS * HEAD_DIM
QKV_WIDTH = ATTN_WIDTH + 2 * KV_WIDTH
LRU_WIDTH = 512
LRU_BLOCKS = 8
LRU_BLOCK = 64
LRU_HALF = 256
LRU_C = 8.0
CONV_WIDTH = 4
BLOCK = 128
PAD_ROWS = BLOCK - N_META
IN_WIDTH = QKV_WIDTH + 2 * LRU_WIDTH
D_FF = 4096
EPS = 1e-6
NEG = -1e30
N_DEV = 8
FF_CHUNK = D_FF // N_DEV
SUBLANES = 8
LANES = 128

ADAM_LR = 0.001
ADAM_B1 = 0.9
ADAM_B2 = 0.999
ADAM_EPS = 1e-08
ADAM_WD = 0.01
ADAM_STEP = 10

VMEM_LIMIT = 56 * 1024 * 1024


def _row_tile(rows):
    for t in (640, 512, 256, 128):
        if rows % t == 0:
            return t
    raise ValueError(rows)


def _big_tile(rows):
    for t in (1664, 1024, 512, 256, 128):
        if rows % t == 0:
            return t
    raise ValueError(rows)


def _rec_tile(rows):
    for t in (320, 256, 128):
        if rows % t == 0:
            return t
    raise ValueError(rows)


def _params(semantics):
    return pltpu.CompilerParams(dimension_semantics=semantics, vmem_limit_bytes=VMEM_LIMIT)


def _mm(a, b):
    return lax.dot_general(a, b, (((1,), (0,)), ((), ())), preferred_element_type=F32)


def _mm_nt(a, b):
    return lax.dot_general(a, b, (((1,), (1,)), ((), ())), preferred_element_type=F32)


def _mm_tn(a, b):
    return lax.dot_general(a, b, (((0,), (0,)), ((), ())), preferred_element_type=F32)


def _rms_fwd(x, g):
    rstd = lax.rsqrt(jnp.mean(x * x, axis=-1, keepdims=True) + EPS)
    xhat = x * rstd
    return xhat * g, xhat, rstd


def _rms_bwd(dy, xhat, rstd, g):
    dyg = dy * g
    c = jnp.mean(dyg * xhat, axis=-1, keepdims=True)
    dx = rstd * (dyg - xhat * c)
    dg = jnp.sum(dy * xhat, axis=0, keepdims=True)
    return dx, dg


def _sigmoid(x):
    return 0.5 * jnp.tanh(0.5 * x) + 0.5


def _log1p(x):
    u = 1.0 + x
    return jnp.where(u == 1.0, x, jnp.log(u) * x / (u - 1.0))


def _one_minus_sq_exp(x, ex):
    return -jnp.tanh(x) * (1.0 + ex * ex)


TINY = 1e-30


def _sqrt_pos(y):
    r = lax.rsqrt(jnp.maximum(y, TINY))
    return y * r, r


def _softplus(x):
    return jnp.maximum(x, 0.0) + _log1p(jnp.exp(-jnp.abs(x)))


GELU_C = 0.7978845608028654
GELU_K = 0.044715


def _gelu(x):
    t = jnp.tanh(GELU_C * (x + GELU_K * x * x * x))
    return 0.5 * x * (1.0 + t), t


def _gelu_grad(x, t):
    return 0.5 * (1.0 + t) + 0.5 * x * (1.0 - t * t) * GELU_C * (1.0 + 3.0 * GELU_K * x * x)


def _full(shape):
    return pl.BlockSpec(shape, lambda *_: (0,) * len(shape))


def _resident(shape):
    return pl.BlockSpec(shape, lambda *_: (0,) * len(shape), pipeline_mode=pl.Buffered(1))


def _exchange_copies(ins, outs, sems, modes):
    send_sems, recv_sems, local_sems = sems
    x, y, c = lax.axis_index("x"), lax.axis_index("y"), lax.axis_index("c")
    me = 4 * x + 2 * y + c

    def block(a, dev):
        return ins[a] if modes[a] == "gather" else ins[a].at[dev]

    local = [pltpu.make_async_copy(block(a, me), outs[a].at[me], local_sems.at[a]) for a in range(len(ins))]
    sends, recvs = [], []
    for a in range(len(ins)):
        for k in range(N_DEV - 1):
            bits = k + 1
            px = jnp.bitwise_xor(x, (bits >> 2) & 1)
            py = jnp.bitwise_xor(y, (bits >> 1) & 1)
            pc = jnp.bitwise_xor(c, bits & 1)
            peer = 4 * px + 2 * py + pc
            common = dict(src_ref=block(a, peer), send_sem=send_sems.at[a, k], recv_sem=recv_sems.at[a, k],
                          device_id=(px, py, pc), device_id_type=pl.DeviceIdType.MESH)
            sends.append(pltpu.make_async_remote_copy(dst_ref=outs[a].at[me], **common))
            recvs.append(pltpu.make_async_remote_copy(dst_ref=outs[a].at[peer], **common))
    return local, sends, recvs


def _exchange_start(ins, outs, sems, modes):
    local, sends, _ = _exchange_copies(ins, outs, sems, modes)
    for cp in local + sends:
        cp.start()


def _exchange_wait(ins, outs, sems, modes):
    local, sends, recvs = _exchange_copies(ins, outs, sems, modes)
    for cp in recvs:
        cp.wait_recv()
    for cp in sends:
        cp.wait_send()
    for cp in local:
        cp.wait()


def _exchange_shapes(arrays, modes):
    return [jax.ShapeDtypeStruct((N_DEV,) + a.shape if mode == "gather" else a.shape, a.dtype)
            for a, mode in zip(arrays, modes)]


def _exchange_sems(na):
    return [pltpu.SemaphoreType.DMA((na, N_DEV - 1)), pltpu.SemaphoreType.DMA((na, N_DEV - 1)),
            pltpu.SemaphoreType.DMA((na,))]


ANY_SPACE = pl.BlockSpec(memory_space=pl.ANY)


def _exchange(arrays, modes, name):
    na = len(arrays)

    def body(*refs):
        ins, outs, sems = refs[:na], refs[na:2 * na], refs[2 * na:]
        _exchange_start(ins, outs, sems, modes)
        _exchange_wait(ins, outs, sems, modes)

    return pl.pallas_call(
        body, name=name, out_shape=_exchange_shapes(arrays, modes),
        in_specs=[ANY_SPACE] * na, out_specs=[ANY_SPACE] * na, scratch_shapes=_exchange_sems(na),
        compiler_params=pltpu.CompilerParams(has_side_effects=True),
    )(*arrays)


def _gather_two_level(arrays, name):
    na = len(arrays)

    def body(*refs):
        ins, outs = refs[:na], refs[na:2 * na]
        send_sems, recv_sems, local_sems = refs[2 * na:]
        x, y, c = lax.axis_index("x"), lax.axis_index("y"), lax.axis_index("c")
        me, sibling = (x, y, c), (x, y, 1 - c)
        chips = [(1 - x, y), (x, 1 - y), (1 - x, 1 - y)]

        def copy(a, k, block, to, src=None):
            slot = outs[a].at[4 * block[0] + 2 * block[1] + block[2]]
            return pltpu.make_async_remote_copy(
                src_ref=slot if src is None else src, dst_ref=slot, send_sem=send_sems.at[a, k],
                recv_sem=recv_sems.at[a, k], device_id=to, device_id_type=pl.DeviceIdType.MESH)

        local = [pltpu.make_async_copy(ins[a], outs[a].at[4 * x + 2 * y + c], local_sems.at[a]) for a in range(na)]
        first = []
        for a in range(na):
            first.append(copy(a, 0, me, sibling, src=ins[a]))
            first += [copy(a, 1 + j, me, (*chip, c), src=ins[a]) for j, chip in enumerate(chips)]
        for cp in local + first:
            cp.start()
        passed = []
        for j, chip in enumerate(chips):
            for a in range(na):
                copy(a, 1 + j, (*chip, c), me).wait_recv()
                passed.append(copy(a, 4 + j, (*chip, c), sibling))
                passed[-1].start()
        for a in range(na):
            copy(a, 0, sibling, me).wait_recv()
            for j, chip in enumerate(chips):
                copy(a, 4 + j, (*chip, 1 - c), me).wait_recv()
        for cp in first + passed:
            cp.wait_send()
        for cp in local:
            cp.wait()

    return pl.pallas_call(
        body, name=name, out_shape=_exchange_shapes(arrays, ["gather"] * na),
        in_specs=[ANY_SPACE] * na, out_specs=[ANY_SPACE] * na, scratch_shapes=_exchange_sems(na),
        compiler_params=pltpu.CompilerParams(has_side_effects=True),
    )(*arrays)


def _hosting_call(body, name, steps, in_specs, out_specs, out_shape, scratch_shapes, args, arrays, modes):
    n_in, n_out, n_scr, na = len(in_specs), len(out_specs), len(scratch_shapes), len(arrays)

    def hosting_body(*refs):
        cuts = [0]
        for n in (n_in, na, n_out, na, n_scr, 3):
            cuts.append(cuts[-1] + n)
        ins, x_ins, outs, x_outs, scr, sems = (refs[cuts[p]:cuts[p + 1]] for p in range(6))
        step = pl.program_id(0)

        @pl.when(step == 0)
        def _():
            _exchange_start(x_ins, x_outs, sems, modes)

        body(*ins, *outs, *scr)

        @pl.when(step == steps - 1)
        def _():
            _exchange_wait(x_ins, x_outs, sems, modes)

    res = pl.pallas_call(
        hosting_body, name=name, grid=(steps,),
        in_specs=list(in_specs) + [ANY_SPACE] * na, out_specs=list(out_specs) + [ANY_SPACE] * na,
        out_shape=list(out_shape) + _exchange_shapes(arrays, modes),
        scratch_shapes=list(scratch_shapes) + _exchange_sems(na),
        compiler_params=_params(("arbitrary",)),
    )(*args, *arrays)
    return res[:n_out], res[n_out:]


def _staggered(i, steps, matmul, row_work, products):
    @pl.when(i == 0)
    def _():
        matmul(products[0])

    for parity in (0, 1):
        @pl.when((i > 0) & (i < steps) & (i % 2 == parity))
        def _():
            matmul(products[parity])
            row_work(products[1 - parity], 1 - parity)

    @pl.when(i == steps)
    def _():
        row_work(products[(steps - 1) % 2], (steps - 1) % 2)


def _frame_rows(src_hbm, buf, sem, i, steps, tm):
    def first():
        return pltpu.make_async_copy(src_hbm.at[pl.ds(0, tm - BLOCK)], buf.at[0, pl.ds(BLOCK, tm - BLOCK)], sem.at[0])

    def later(t, slot):
        return pltpu.make_async_copy(src_hbm.at[pl.ds(pl.multiple_of(t * tm - BLOCK, BLOCK), tm)], buf.at[slot], sem.at[slot])

    slot = i % 2

    @pl.when(i == 0)
    def _():
        first().start()

    @pl.when(i + 1 < steps)
    def _():
        later(i + 1, 1 - slot).start()

    @pl.when(i == 0)
    def _():
        first().wait()

    @pl.when(i > 0)
    def _():
        later(i, slot).wait()

    return slot


def _frame_scratch(tm):
    return [pltpu.VMEM((2, tm, D_MODEL), F32), pltpu.SemaphoreType.DMA((2,))]


def _in_proj_fwd(head, x, g1, w_in, carried, modes):
    rows = BLOCK + x.shape[0]
    tm = _row_tile(rows)
    steps = rows // tm

    def body(head_ref, g_ref, w_ref, x_hbm, qkv_ref, zrec_ref, u_ref, h_ref, buf, sem):
        i = pl.program_id(0)
        slot = _frame_rows(x_hbm, buf, sem, i, steps, tm)

        @pl.when(i == 0)
        def _():
            buf[0, 0:BLOCK, :] = head_ref[...]

        h = buf[slot]
        h_ref[...] = h
        u, _, _ = _rms_fwd(h, g_ref[...])
        u = u.astype(BF16)
        u_ref[...] = u
        z = _mm(u, w_ref[...])
        qkv_ref[...] = z[:, :QKV_WIDTH].astype(BF16)
        zrec_ref[...] = z[:, QKV_WIDTH:]

    wide = pl.BlockSpec((tm, D_MODEL), lambda i: (i, 0))
    return _hosting_call(
        body, "in_proj_fwd", steps,
        [_full((BLOCK, D_MODEL)), _full((1, D_MODEL)), _resident((D_MODEL, IN_WIDTH)), ANY_SPACE],
        [pl.BlockSpec((tm, QKV_WIDTH), lambda i: (i, 0)), pl.BlockSpec((tm, 2 * LRU_WIDTH), lambda i: (i, 0)), wide, wide],
        [jax.ShapeDtypeStruct((rows, QKV_WIDTH), BF16), jax.ShapeDtypeStruct((rows, 2 * LRU_WIDTH), F32),
         jax.ShapeDtypeStruct((rows, D_MODEL), BF16), jax.ShapeDtypeStruct((rows, D_MODEL), F32)],
        _frame_scratch(tm), (head, g1, w_in, x), carried, modes)


N_BIAS = 3


def _attn_bias():
    key = np.arange(2 * BLOCK)[:, None]
    r = np.arange(GQA_GROUP * BLOCK)[None, :] % BLOCK
    band = (key > r) & (key <= r + BLOCK)
    out = [np.where(band & ((n - 1) * BLOCK + key >= PAD_ROWS), 0.0, NEG) for n in range(N_BIAS)]
    return jnp.asarray(np.stack(out), F32)


def _attn_probs(k2, q4, bias, sink_row):
    s = _mm_nt(k2, q4) * (HEAD_DIM ** -0.5) + bias
    m = jnp.maximum(jnp.max(s, axis=0, keepdims=True), sink_row)
    p = jnp.exp(s - m)
    es = jnp.exp(sink_row - m)
    inv = 1.0 / (jnp.sum(p, axis=0, keepdims=True) + es)
    return p * inv, es * inv


def _heads(ref, rows, first, count):
    return jnp.concatenate([ref[rows, (first + g) * HEAD_DIM:(first + g + 1) * HEAD_DIM] for g in range(count)], axis=0)


def _keys_of_block(prev_ref, cur_ref, b, kv):
    sl = slice(kv * HEAD_DIM, (kv + 1) * HEAD_DIM)
    before = prev_ref[:, sl] if b == 0 else cur_ref[(b - 1) * BLOCK:b * BLOCK, sl]
    return jnp.concatenate([before, cur_ref[b * BLOCK:(b + 1) * BLOCK, sl]], axis=0)


def _bias_of_block(bias_ref, block):
    return bias_ref[jnp.minimum(block, N_BIAS - 1)]


def _sink_row(sink_ref, kv):
    g = lax.broadcasted_iota(jnp.int32, (1, GQA_GROUP * BLOCK), 1) // BLOCK
    row = jnp.full((1, GQA_GROUP * BLOCK), sink_ref[0, kv * GQA_GROUP], F32)
    for i in range(1, GQA_GROUP):
        row = jnp.where(g == i, sink_ref[0, kv * GQA_GROUP + i], row)
    return row


def _from_head_major(pieces):
    return jnp.concatenate(pieces, axis=0).T


def _attn_specs(tm, tile_of):
    nbt = tm // BLOCK
    k_col, v_col = ATTN_WIDTH // KV_WIDTH, ATTN_WIDTH // KV_WIDTH + 1
    before = lambda i: jnp.maximum(tile_of(i) * nbt - 1, 0)
    return [pl.BlockSpec((tm, ATTN_WIDTH), lambda i: (tile_of(i), 0)),
            pl.BlockSpec((BLOCK, KV_WIDTH), lambda i: (before(i), k_col)),
            pl.BlockSpec((tm, KV_WIDTH), lambda i: (tile_of(i), k_col)),
            pl.BlockSpec((BLOCK, KV_WIDTH), lambda i: (before(i), v_col)),
            pl.BlockSpec((tm, KV_WIDTH), lambda i: (tile_of(i), v_col))]


def _attn_fwd(qkv, sinks, bias, carried, modes):
    rows = qkv.shape[0]
    tm = _row_tile(rows)
    nbt = tm // BLOCK

    def body(sink_ref, bias_ref, q_ref, kp_ref, kc_ref, vp_ref, vc_ref, o_ref):
        i = pl.program_id(0)
        for b in range(nbt):
            blk = slice(b * BLOCK, (b + 1) * BLOCK)
            bias_t = _bias_of_block(bias_ref, i * nbt + b)
            pieces = []
            for kv in range(KV_HEADS):
                k2 = _keys_of_block(kp_ref, kc_ref, b, kv)
                v2 = _keys_of_block(vp_ref, vc_ref, b, kv)
                q4 = _heads(q_ref, blk, kv * GQA_GROUP, GQA_GROUP)
                pn, _ = _attn_probs(k2, q4, bias_t, _sink_row(sink_ref, kv))
                ot = _mm_tn(v2, pn.astype(BF16))
                pieces += [ot[:, g * BLOCK:(g + 1) * BLOCK] for g in range(GQA_GROUP)]
            o_ref[blk, :] = _from_head_major(pieces).astype(BF16)

    return _hosting_call(
        body, "attn_fwd", rows // tm,
        [pl.BlockSpec(memory_space=pltpu.SMEM), _resident((N_BIAS, 2 * BLOCK, GQA_GROUP * BLOCK))]
        + _attn_specs(tm, lambda i: i),
        [pl.BlockSpec((tm, ATTN_WIDTH), lambda i: (i, 0))],
        [jax.ShapeDtypeStruct((rows, ATTN_WIDTH), BF16)],
        [], (sinks, bias, qkv, qkv, qkv, qkv, qkv), carried, modes)


def _conv_taps(xbuf, tm):
    return [xbuf[pl.ds(SUBLANES - (CONV_WIDTH - 1 - j), tm), :] for j in range(CONV_WIDTH)]


def _lru_halves(xc):
    return [xc[:, h * LRU_HALF:(h + 1) * LRU_HALF].astype(BF16) for h in range(2)]


def _lru_gates(xc, wa_ref, ba_ref, wx_ref, bx_ref, lam_ref):
    halves = _lru_halves(xc)
    gate_r = jnp.concatenate([_mm(halves[h], wa_ref[h]) for h in range(2)], axis=1) + ba_ref[...]
    gate_i = jnp.concatenate([_mm(halves[h], wx_ref[h]) for h in range(2)], axis=1) + bx_ref[...]
    r = _sigmoid(gate_r)
    ig = _sigmoid(gate_i)
    log_a = (-LRU_C) * r * _softplus(-lam_ref[...])
    a = jnp.exp(log_a)
    mult, _ = _sqrt_pos(_one_minus_sq_exp(log_a, a))
    return r, ig, a, mult


KEPT_XC, KEPT_A, KEPT_MULT, KEPT_R, KEPT_I, N_KEPT = 0, 1, 2, 3, 4, 5


def _scan_tile(a_ref, u_ref, out_ref, carry, tm, reverse):
    row = lax.broadcasted_iota(jnp.int32, (SUBLANES, LRU_WIDTH), 0)
    groups = tm // SUBLANES

    def step(j, prev):
        jj = groups - 1 - j if reverse else j
        o = pl.multiple_of(jj * SUBLANES, SUBLANES)
        a = a_ref[pl.ds(o, SUBLANES), :]
        u = u_ref[pl.ds(o, SUBLANES), :]
        for s in (1, 2, 4):
            shift = SUBLANES - s if reverse else s
            keep = (row < SUBLANES - s) if reverse else (row >= s)
            u = jnp.where(keep, a * pltpu.roll(u, shift, 0) + u, u)
            a = jnp.where(keep, a * pltpu.roll(a, shift, 0), a)
        out = a * prev + u
        out_ref[pl.ds(o, SUBLANES), :] = out
        return out[0:1, :] if reverse else out[SUBLANES - 1:SUBLANES, :]

    return lax.fori_loop(0, groups, step, carry)


def _rec_fwd(zrec, conv_w, conv_b, wa_bd, b_a, wx_bd, b_x, lam, carried, modes):
    rows = zrec.shape[0]
    tm = _rec_tile(rows)

    def body(xr_ref, yr_ref, cw_ref, cb_ref, wa_ref, ba_ref, wx_ref, bx_ref, lam_ref, rec_ref, h_ref, kept_ref,
             xbuf, a_s, u_s, carry):
        i = pl.program_id(0)

        @pl.when(i == 0)
        def _():
            xbuf[0:SUBLANES, :] = jnp.zeros((SUBLANES, LRU_WIDTH), F32)
            carry[...] = jnp.zeros_like(carry)

        @pl.when(i > 0)
        def _():
            xbuf[0:SUBLANES, :] = xbuf[tm:tm + SUBLANES, :]

        xbuf[SUBLANES:SUBLANES + tm, :] = xr_ref[...]
        taps = _conv_taps(xbuf, tm)
        xc = cb_ref[...] + sum(cw_ref[j:j + 1, :] * taps[j] for j in range(CONV_WIDTH))
        r, ig, a, mult = _lru_gates(xc, wa_ref, ba_ref, wx_ref, bx_ref, lam_ref)
        for k, val in ((KEPT_XC, xc), (KEPT_A, a), (KEPT_MULT, mult), (KEPT_R, r), (KEPT_I, ig)):
            kept_ref[:, k * LRU_WIDTH:(k + 1) * LRU_WIDTH] = val
        grow = i * tm + lax.broadcasted_iota(jnp.int32, (tm, LRU_WIDTH), 0)
        a_s[...] = a
        u_s[...] = jnp.where(grow >= PAD_ROWS, mult * (ig * xc), 0.0)
        carry[0:1, :] = _scan_tile(a_s, u_s, h_ref, carry[0:1, :], tm, reverse=False)
        gel, _ = _gelu(yr_ref[...])
        rec_ref[...] = (gel * h_ref[...]).astype(BF16)

    vec = _full((1, LRU_WIDTH))
    bd = _full((2, LRU_HALF, LRU_HALF))
    return _hosting_call(
        body, "rec_fwd", rows // tm,
        [pl.BlockSpec((tm, LRU_WIDTH), lambda i: (i, 0)), pl.BlockSpec((tm, LRU_WIDTH), lambda i: (i, 1)),
         _full((CONV_WIDTH, LRU_WIDTH)), vec, bd, vec, bd, vec, vec],
        [pl.BlockSpec((tm, LRU_WIDTH), lambda i: (i, 0))] * 2 + [pl.BlockSpec((tm, N_KEPT * LRU_WIDTH), lambda i: (i, 0))],
        [jax.ShapeDtypeStruct((rows, LRU_WIDTH), BF16), jax.ShapeDtypeStruct((rows, LRU_WIDTH), F32),
         jax.ShapeDtypeStruct((rows, N_KEPT * LRU_WIDTH), F32)],
        [pltpu.VMEM((tm + SUBLANES, LRU_WIDTH), F32), pltpu.VMEM((tm, LRU_WIDTH), F32),
         pltpu.VMEM((tm, LRU_WIDTH), F32), pltpu.VMEM((SUBLANES, LRU_WIDTH), F32)],
        (zrec, zrec, conv_w, conv_b, wa_bd, b_a, wx_bd, b_x, lam), carried, modes)


def _out_proj_fwd(attn, rec, w_out, h0, g2, carried, modes):
    rows = h0.shape[0]
    tm = _row_tile(rows)

    def body(attn_ref, rec_ref, w_ref, h_ref, g_ref, mix_ref, h1_ref):
        mix = _mm(attn_ref[...], w_ref[0:ATTN_WIDTH, :]) + _mm(rec_ref[...], w_ref[ATTN_WIDTH:, :])
        y, _, _ = _rms_fwd(mix, g_ref[...])
        mix_ref[...] = mix
        h1_ref[...] = h_ref[...] + y

    half = pl.BlockSpec((tm, ATTN_WIDTH), lambda i: (i, 0))
    wide = pl.BlockSpec((tm, D_MODEL), lambda i: (i, 0))
    return _hosting_call(
        body, "out_proj_fwd", rows // tm,
        [half, half, _resident((D_MODEL, D_MODEL)), wide, _full((1, D_MODEL))],
        [wide, wide],
        [jax.ShapeDtypeStruct((rows, D_MODEL), F32)] * 2,
        [], (attn, rec, w_out, h0, g2), carried, modes)


FF_COLS = 1024
FF_HALF = FF_CHUNK // 2


def _hidden_at(d, half):
    return half * (D_FF // 2) + d * FF_HALF


def _ffn_up(h1, g3, w1_halves, carried, modes):
    rows = h1.shape[0]
    tm = _row_tile(rows)

    def body(h_ref, g_ref, wa_ref, wb_ref, act_ref, u_ref):
        u, _, _ = _rms_fwd(h_ref[...], g_ref[...])
        u = u.astype(BF16)
        u_ref[...] = u
        for half, w_ref in enumerate((wa_ref, wb_ref)):
            for d in range(N_DEV):
                c = _hidden_at(d, half)
                a1 = jnp.maximum(_mm(u, w_ref[d]), 0.0)
                act_ref[:, c:c + FF_HALF] = (a1 * a1).astype(BF16)

    wide = pl.BlockSpec((tm, D_MODEL), lambda i: (i, 0))
    return _hosting_call(
        body, "ffn_up", rows // tm,
        [wide, _full((1, D_MODEL))] + [_resident((N_DEV, D_MODEL, FF_HALF))] * 2,
        [pl.BlockSpec((tm, D_FF), lambda i: (i, 0)), wide],
        [jax.ShapeDtypeStruct((rows, D_FF), BF16), jax.ShapeDtypeStruct((rows, D_MODEL), BF16)],
        [], (h1, g3, *w1_halves), carried, modes)


def _ffn_down_loss(act, w2_halves, h1, target, g4):
    rows = h1.shape[0]
    tm = _row_tile(rows)
    steps = rows // tm
    kh = D_FF // 2

    def body(act_ref, wa_ref, wb_ref, h_ref, g_ref, t_hbm, dy_ref, df_ref, dg_ref, loss_ref, buf, sem, f_even, f_odd):
        i = pl.program_id(0)
        t = i - 1

        @pl.when(i == 0)
        def _():
            dg_ref[...] = jnp.zeros_like(dg_ref)
            loss_ref[...] = jnp.zeros_like(loss_ref)
            buf[0, 0:BLOCK, :] = jnp.zeros((BLOCK, D_MODEL), F32)

        @pl.when(i > 0)
        def _():
            _frame_rows(t_hbm, buf, sem, t, steps, tm)

        def matmul(f_ref):
            f_ref[...] = _mm(act_ref[:, :kh], wa_ref[...]) + _mm(act_ref[:, kh:], wb_ref[...])

        def row_work(f_ref, slot):
            g = g_ref[...]
            y, fhat, rstd = _rms_fwd(f_ref[...], g)
            grow = t * tm + lax.broadcasted_iota(jnp.int32, (tm, D_MODEL), 0)
            err = jnp.where(grow >= BLOCK, h_ref[...] + y - buf[slot], 0.0)
            loss_ref[...] += (0.5 / D_MODEL) * jnp.sum(err * err)
            dy = err * (1.0 / D_MODEL)
            df, dg = _rms_bwd(dy, fhat, rstd, g)
            dy_ref[...] = dy
            df_ref[...] = df.astype(BF16)
            dg_ref[...] += dg

        _staggered(i, steps, matmul, row_work, (f_even, f_odd))

    behind = pl.BlockSpec((tm, D_MODEL), lambda i: (jnp.maximum(i - 1, 0), 0))
    product = pltpu.VMEM((tm, D_MODEL), F32)
    return pl.pallas_call(
        body, name="ffn_down_loss", grid=(steps + 1,),
        in_specs=[pl.BlockSpec((tm, D_FF), lambda i: (jnp.minimum(i, steps - 1), 0)), _resident((kh, D_MODEL)),
                  _resident((kh, D_MODEL)), behind, _full((1, D_MODEL)), ANY_SPACE],
        out_specs=[behind, behind, _full((1, D_MODEL)), _full((SUBLANES, LANES))],
        out_shape=[jax.ShapeDtypeStruct((rows, D_MODEL), F32), jax.ShapeDtypeStruct((rows, D_MODEL), BF16),
                   jax.ShapeDtypeStruct((1, D_MODEL), F32), jax.ShapeDtypeStruct((SUBLANES, LANES), F32)],
        scratch_shapes=_frame_scratch(tm) + [product, product],
        compiler_params=_params(("arbitrary",)),
    )(act, *w2_halves, h1, g4, target)


def _ffn_bwd_act(df, w2t_halves, act):
    rows = df.shape[0]
    tm = _row_tile(rows)

    def body(df_ref, wa_ref, wb_ref, act_ref, da_ref):
        df_t = df_ref[...]
        for half, w_ref in enumerate((wa_ref, wb_ref)):
            for d in range(N_DEV):
                cols = slice(_hidden_at(d, half), _hidden_at(d, half) + FF_HALF)
                dact = _mm(df_t, w_ref[d])
                relu_a1, _ = _sqrt_pos(act_ref[:, cols].astype(F32))
                da_ref[:, cols] = (dact * (2.0 * relu_a1)).astype(BF16)

    hidden = pl.BlockSpec((tm, D_FF), lambda i: (i, 0))
    return pl.pallas_call(
        body, name="ffn_bwd_act", grid=(rows // tm,),
        in_specs=[pl.BlockSpec((tm, D_MODEL), lambda i: (i, 0))] + [_resident((N_DEV, D_MODEL, FF_HALF))] * 2 + [hidden],
        out_specs=hidden,
        out_shape=jax.ShapeDtypeStruct((rows, D_FF), BF16),
        compiler_params=_params(("parallel",)),
    )(df, *w2t_halves, act)


def _ffn_bwd_x(da, w1t, h1, dy, g3, carried, modes):
    rows = h1.shape[0]
    tm = _row_tile(rows)

    steps = rows // tm

    def body(da_ref, w_ref, h_ref, dy_ref, g_ref, dh_ref, dg_ref, p_even, p_odd):
        i = pl.program_id(0)

        @pl.when(i == 0)
        def _():
            dg_ref[...] = jnp.zeros_like(dg_ref)

        def matmul(p_ref):
            p_ref[...] = _mm(da_ref[...], w_ref[...])

        def row_work(p_ref, _):
            g = g_ref[...]
            _, xhat, rstd = _rms_fwd(h_ref[...], g)
            dx, dg = _rms_bwd(p_ref[...], xhat, rstd, g)
            dh_ref[...] = dy_ref[...] + dx
            dg_ref[...] += dg

        _staggered(i, steps, matmul, row_work, (p_even, p_odd))

    behind = pl.BlockSpec((tm, D_MODEL), lambda i: (jnp.maximum(i - 1, 0), 0))
    product = pltpu.VMEM((tm, D_MODEL), F32)
    return _hosting_call(
        body, "ffn_bwd_x", steps + 1,
        [pl.BlockSpec((tm, D_FF), lambda i: (jnp.minimum(i, steps - 1), 0)), _resident((D_FF, D_MODEL)), behind, behind,
         _full((1, D_MODEL))],
        [behind, _full((1, D_MODEL))],
        [jax.ShapeDtypeStruct((rows, D_MODEL), F32), jax.ShapeDtypeStruct((1, D_MODEL), F32)],
        [product, product], (da, w1t, h1, dy, g3), carried, modes)


def _ffn_bwd_weights(u2, da, act, df):
    rows = u2.shape[0]
    tb = _big_tile(rows)
    steps = rows // tb
    per = FF_COLS // FF_HALF

    def body(u_ref, da_ref, act_ref, df_ref, dw1_ref, dw2_ref, acc1, acc2):
        i = pl.program_id(1)

        @pl.when(i == 0)
        def _():
            acc1[...] = jnp.zeros_like(acc1)
            acc2[...] = jnp.zeros_like(acc2)

        acc1[...] += _mm_tn(u_ref[...], da_ref[...])
        acc2[...] += _mm_tn(act_ref[...], df_ref[...])

        @pl.when(i == steps - 1)
        def _():
            for p in range(per):
                c = p * FF_HALF
                dw1_ref[p] = acc1[:, c:c + FF_HALF].astype(BF16)
                dw2_ref[p] = acc2[c:c + FF_HALF, :].astype(BF16)

    wide = pl.BlockSpec((tb, D_MODEL), lambda j, i: (i, 0))
    chunk = pl.BlockSpec((tb, FF_COLS), lambda j, i: (i, j))
    return pl.pallas_call(
        body, name="ffn_bwd_weights", grid=(D_FF // FF_COLS, steps),
        in_specs=[wide, chunk, chunk, wide],
        out_specs=[pl.BlockSpec((None, per, D_MODEL, FF_HALF), lambda j, i: (j // 2, j % 2, 0, 0)),
                   pl.BlockSpec((per, FF_HALF, D_MODEL), lambda j, i: (j % 2, j // 2, 0))],
        out_shape=[jax.ShapeDtypeStruct((2, N_DEV, D_MODEL, FF_HALF), BF16),
                   jax.ShapeDtypeStruct((N_DEV, FF_CHUNK, D_MODEL), BF16)],
        scratch_shapes=[pltpu.VMEM((D_MODEL, FF_COLS), F32), pltpu.VMEM((FF_COLS, D_MODEL), F32)],
        compiler_params=_params(("parallel", "arbitrary")),
    )(u2, da, act, df)


def _out_proj_bwd(dh1, mix, g2, w_out_t, attn, rec, carried, modes):
    rows = dh1.shape[0]
    tm = _row_tile(rows)
    steps = rows // tm

    def body(dh_ref, mix_ref, g_ref, w_ref, attn_ref, rec_ref, dattn_ref, drec_ref, dw_ref, dg_ref, acc):
        i = pl.program_id(0)

        @pl.when(i == 0)
        def _():
            acc[...] = jnp.zeros_like(acc)
            dg_ref[...] = jnp.zeros_like(dg_ref)

        g = g_ref[...]
        _, xhat, rstd = _rms_fwd(mix_ref[...], g)
        dmix, dg = _rms_bwd(dh_ref[...], xhat, rstd, g)
        dmix = dmix.astype(BF16)
        dg_ref[...] += dg
        din = _mm(dmix, w_ref[...])
        dattn_ref[...] = din[:, :ATTN_WIDTH].astype(BF16)
        drec_ref[...] = din[:, ATTN_WIDTH:]
        acc[0:ATTN_WIDTH, :] += _mm_tn(attn_ref[...], dmix)
        acc[ATTN_WIDTH:, :] += _mm_tn(rec_ref[...], dmix)

        @pl.when(i == steps - 1)
        def _():
            dw_ref[...] = acc[...].astype(BF16)

    half = pl.BlockSpec((tm, ATTN_WIDTH), lambda i: (i, 0))
    wide = pl.BlockSpec((tm, D_MODEL), lambda i: (i, 0))
    return _hosting_call(
        body, "out_proj_bwd", steps,
        [wide, wide, _full((1, D_MODEL)), _resident((D_MODEL, D_MODEL)), half, half],
        [half, half, _full((D_MODEL, D_MODEL)), _full((1, D_MODEL))],
        [jax.ShapeDtypeStruct((rows, ATTN_WIDTH), BF16), jax.ShapeDtypeStruct((rows, LRU_WIDTH), F32),
         jax.ShapeDtypeStruct((D_MODEL, D_MODEL), BF16), jax.ShapeDtypeStruct((1, D_MODEL), F32)],
        [pltpu.VMEM((D_MODEL, D_MODEL), F32)],
        (dh1, mix, g2, w_out_t, attn, rec), carried, modes)


def _attn_bwd(qkv, dattn, sinks, bias, carried, modes):
    rows = qkv.shape[0]
    tm = _row_tile(rows)
    nbt, nt = tm // BLOCK, rows // tm

    def body(sink_ref, bias_ref, do_ref, q_ref, kp_ref, kc_ref, vp_ref, vc_ref, dq_ref, dkv_ref, dsink_ref, dk_c, dv_c):
        i = pl.program_id(0)

        @pl.when(i == 0)
        def _():
            dk_c[...] = jnp.zeros_like(dk_c)
            dv_c[...] = jnp.zeros_like(dv_c)
            dsink_ref[...] = jnp.zeros_like(dsink_ref)

        @pl.when(i < nt)
        def _():
            dk_late, dv_late = dk_c[...], dv_c[...]
            dsink_rows = [jnp.zeros((1, LANES), F32)] * ATTN_HEADS
            for b in range(nbt):
                blk = slice(b * BLOCK, (b + 1) * BLOCK)
                bias_t = _bias_of_block(bias_ref, i * nbt + b)
                dq_parts, dk_parts, dv_parts = [], [], []
                for kv in range(KV_HEADS):
                    k2 = _keys_of_block(kp_ref, kc_ref, b, kv)
                    v2 = _keys_of_block(vp_ref, vc_ref, b, kv)
                    q4 = _heads(q_ref, blk, kv * GQA_GROUP, GQA_GROUP)
                    do4 = _heads(do_ref, blk, kv * GQA_GROUP, GQA_GROUP)
                    pn, psink = _attn_probs(k2, q4, bias_t, _sink_row(sink_ref, kv))
                    dpn = _mm_nt(v2, do4)
                    delta = jnp.sum(pn * dpn, axis=0, keepdims=True)
                    ds = ((pn * (dpn - delta)) * (HEAD_DIM ** -0.5)).astype(BF16)
                    dqt = _mm_tn(k2, ds)
                    dq_parts += [dqt[:, g * BLOCK:(g + 1) * BLOCK] for g in range(GQA_GROUP)]
                    dk_parts.append(_mm(ds, q4))
                    dv_parts.append(_mm(pn.astype(BF16), do4))
                    sd = psink * delta
                    for g in range(GQA_GROUP):
                        h = kv * GQA_GROUP + g
                        dsink_rows[h] = dsink_rows[h] - jnp.sum(sd[:, g * BLOCK:(g + 1) * BLOCK])
                dq_ref[blk, :] = _from_head_major(dq_parts).astype(BF16)
                dk2 = jnp.concatenate(dk_parts, axis=1)
                dv2 = jnp.concatenate(dv_parts, axis=1)
                dkv_ref[blk, 0:KV_WIDTH] = (dk_late + dk2[0:BLOCK]).astype(BF16)
                dkv_ref[blk, KV_WIDTH:] = (dv_late + dv2[0:BLOCK]).astype(BF16)
                dk_late, dv_late = dk2[BLOCK:], dv2[BLOCK:]
            dk_c[...] = dk_late
            dv_c[...] = dv_late
            dsink_ref[...] += jnp.concatenate(dsink_rows, axis=0)

        @pl.when(i == nt)
        def _():
            dkv_ref[...] = jnp.zeros_like(dkv_ref)
            dkv_ref[0:BLOCK, 0:KV_WIDTH] = dk_c[...].astype(BF16)
            dkv_ref[0:BLOCK, KV_WIDTH:] = dv_c[...].astype(BF16)

    tile_of = lambda i: jnp.minimum(i, nt - 1)
    tile = pl.BlockSpec((tm, ATTN_WIDTH), lambda i: (tile_of(i), 0))
    return _hosting_call(
        body, "attn_bwd", nt + 1,
        [pl.BlockSpec(memory_space=pltpu.SMEM), _resident((N_BIAS, 2 * BLOCK, GQA_GROUP * BLOCK)), tile]
        + _attn_specs(tm, tile_of),
        [tile, pl.BlockSpec((tm, 2 * KV_WIDTH), lambda i: (i, 0)), _full((ATTN_HEADS, LANES))],
        [jax.ShapeDtypeStruct((rows, ATTN_WIDTH), BF16), jax.ShapeDtypeStruct((rows + tm, 2 * KV_WIDTH), BF16),
         jax.ShapeDtypeStruct((ATTN_HEADS, LANES), F32)],
        [pltpu.VMEM((BLOCK, KV_WIDTH), F32), pltpu.VMEM((BLOCK, KV_WIDTH), F32)],
        (sinks, bias, dattn, qkv, qkv, qkv, qkv, qkv), carried, modes)


ROW_CONV_B, ROW_B_A, ROW_B_X, ROW_LAMBDA = 4, 5, 6, 7


def _rec_bwd(drec, zrec, h, kept, conv_w, wa_bd, wx_bd, lam, carried, modes):
    rows = zrec.shape[0]
    tm = _rec_tile(rows)
    nt = rows // tm
    per = tm // SUBLANES

    def body(drec_ref, xr_ref, yr_ref, h_ref, xc_ref, a_ref, mult_ref, r_ref, ig_ref, hhalo_ref, cw_ref, wa_ref, wx_ref,
             lam_ref, drz_ref, small_ref, dwa_ref, dwx_ref, hbuf, abuf, u_s, g_s, dbuf, carry):
        s = pl.program_id(0)
        i = nt - 1 - s

        @pl.when(s == 0)
        def _():
            small_ref[...] = jnp.zeros_like(small_ref)
            dwa_ref[...] = jnp.zeros_like(dwa_ref)
            dwx_ref[...] = jnp.zeros_like(dwx_ref)
            carry[...] = jnp.zeros_like(carry)
            abuf[tm:tm + SUBLANES, :] = jnp.zeros((SUBLANES, LRU_WIDTH), F32)
            dbuf[tm:tm + SUBLANES, :] = jnp.zeros((SUBLANES, LRU_WIDTH), F32)

        hbuf[0:SUBLANES, :] = jnp.where(i == 0, 0.0, hhalo_ref[...])
        hbuf[SUBLANES:SUBLANES + tm, :] = h_ref[...]

        xc, a, mult, r, ig = xc_ref[...], a_ref[...], mult_ref[...], r_ref[...], ig_ref[...]
        halves = _lru_halves(xc)
        inv_mult = pl.reciprocal(mult, approx=True)

        yr = yr_ref[...]
        gel, t = _gelu(yr)
        drec_t = drec_ref[...]
        dyr = drec_t * h_ref[...] * _gelu_grad(yr, t)

        abuf[0:tm, :] = a
        u_s[...] = drec_t * gel
        a_next = abuf[pl.ds(1, tm), :]
        abuf[0:tm, :] = a_next
        carry[0:1, :] = _scan_tile(abuf, u_s, g_s, carry[0:1, :], tm, reverse=True)
        abuf[tm:tm + 1, :] = a[0:1, :]
        g = g_s[...]

        grow = i * tm + lax.broadcasted_iota(jnp.int32, (tm, LRU_WIDTH), 0)
        du = jnp.where(grow >= PAD_ROWS, g, 0.0)
        da = g * hbuf[pl.ds(SUBLANES - 1, tm), :]
        dmult = du * (ig * xc)
        dig = du * (mult * xc)
        dxc = du * (mult * ig)
        dlog_a = da * a - dmult * (a * a * inv_mult)
        sp = _softplus(-lam_ref[...])
        dgr = (dlog_a * (-LRU_C) * sp) * (r * (1.0 - r))
        dgi = dig * (ig * (1.0 - ig))
        dlam = jnp.sum(dlog_a * r, axis=0, keepdims=True) * (LRU_C * _sigmoid(-lam_ref[...]))
        dgr_b = [dgr[:, hh * LRU_HALF:(hh + 1) * LRU_HALF].astype(BF16) for hh in range(2)]
        dgi_b = [dgi[:, hh * LRU_HALF:(hh + 1) * LRU_HALF].astype(BF16) for hh in range(2)]
        dxc = dxc + jnp.concatenate(
            [_mm_nt(dgr_b[hh], wa_ref[hh]) + _mm_nt(dgi_b[hh], wx_ref[hh]) for hh in range(2)], axis=1)
        for hh in range(2):
            dwa_ref[hh] += _mm_tn(halves[hh], dgr_b[hh])
            dwx_ref[hh] += _mm_tn(halves[hh], dgi_b[hh])

        dbuf[0:tm, :] = dxc
        ahead = [dbuf[pl.ds(CONV_WIDTH - 1 - j, tm), :] for j in range(CONV_WIDTH)]
        dxr = sum(cw_ref[j:j + 1, :] * ahead[j] for j in range(CONV_WIDTH))
        dbuf[tm:tm + SUBLANES, :] = dxc[0:SUBLANES, :]
        drz_ref[:, 0:LRU_WIDTH] = dxr.astype(BF16)
        drz_ref[:, LRU_WIDTH:] = dyr.astype(BF16)

        xr = xr_ref[...]
        upd = [jnp.sum(xr * ahead[j], axis=0, keepdims=True) for j in range(CONV_WIDTH)]
        upd += [jnp.sum(dxc, axis=0, keepdims=True), jnp.sum(dgr, axis=0, keepdims=True),
                jnp.sum(dgi, axis=0, keepdims=True), dlam]
        small_ref[...] += jnp.concatenate(upd, axis=0)

    rev = lambda s: nt - 1 - s
    halo = lambda s: jnp.maximum(rev(s) * per - 1, 0)
    cols = lambda k: pl.BlockSpec((tm, LRU_WIDTH), lambda s: (rev(s), k))
    halo0 = pl.BlockSpec((SUBLANES, LRU_WIDTH), lambda s: (halo(s), 0))
    bd = _full((2, LRU_HALF, LRU_HALF))
    big = pltpu.VMEM((tm + SUBLANES, LRU_WIDTH), F32)
    tile = pltpu.VMEM((tm, LRU_WIDTH), F32)
    kept_cols = [cols(k) for k in (KEPT_XC, KEPT_A, KEPT_MULT, KEPT_R, KEPT_I)]
    return _hosting_call(
        body, "rec_bwd", nt,
        [cols(0), cols(0), cols(1), cols(0)] + kept_cols
        + [halo0, _full((CONV_WIDTH, LRU_WIDTH)), bd, bd, _full((1, LRU_WIDTH))],
        [pl.BlockSpec((tm, 2 * LRU_WIDTH), lambda s: (rev(s), 0)), _full((SUBLANES, LRU_WIDTH)), bd, bd],
        [jax.ShapeDtypeStruct((rows, 2 * LRU_WIDTH), BF16), jax.ShapeDtypeStruct((SUBLANES, LRU_WIDTH), F32),
         jax.ShapeDtypeStruct((2, LRU_HALF, LRU_HALF), F32), jax.ShapeDtypeStruct((2, LRU_HALF, LRU_HALF), F32)],
        [big, big, tile, tile, big, pltpu.VMEM((SUBLANES, LRU_WIDTH), F32)],
        (drec, zrec, zrec, h) + (kept,) * N_KEPT + (h, conv_w, wa_bd, wx_bd, lam), carried, modes)


DZ_CUTS = (0, ATTN_WIDTH, QKV_WIDTH, IN_WIDTH)


def _dz_specs(tm):
    return [pl.BlockSpec((tm, DZ_CUTS[p + 1] - DZ_CUTS[p]), lambda i: (i, 0)) for p in range(3)]


def _in_proj_bwd_x(h0, g1, dh1, dq, dkv, drz, w_in_t, carried, modes):
    rows = h0.shape[0]
    tm = _row_tile(rows)

    def body(h_ref, g_ref, dh1_ref, dq_ref, dkv_ref, drz_ref, w_ref, dh0_ref, dg_ref):
        @pl.when(pl.program_id(0) == 0)
        def _():
            dg_ref[...] = jnp.zeros_like(dg_ref)

        g = g_ref[...]
        _, xhat, rstd = _rms_fwd(h_ref[...], g)
        parts = (dq_ref[...], dkv_ref[...], drz_ref[...])
        du = sum(_mm(parts[p], w_ref[DZ_CUTS[p]:DZ_CUTS[p + 1], :]) for p in range(3))
        dx, dg = _rms_bwd(du, xhat, rstd, g)
        dh0_ref[...] = dh1_ref[...] + dx
        dg_ref[...] += dg

    wide = pl.BlockSpec((tm, D_MODEL), lambda i: (i, 0))
    return _hosting_call(
        body, "in_proj_bwd_x", rows // tm,
        [wide, _full((1, D_MODEL)), wide] + _dz_specs(tm) + [_resident((IN_WIDTH, D_MODEL))],
        [wide, _full((1, D_MODEL))],
        [jax.ShapeDtypeStruct((rows, D_MODEL), F32), jax.ShapeDtypeStruct((1, D_MODEL), F32)],
        [], (h0, g1, dh1, dq, dkv, drz, w_in_t), carried, modes)


def _in_proj_bwd_w(u1, dq, dkv, drz, carried, modes):
    rows = u1.shape[0]
    tb = _big_tile(rows)

    def body(u_ref, dq_ref, dkv_ref, drz_ref, dw_ref):
        @pl.when(pl.program_id(0) == 0)
        def _():
            dw_ref[...] = jnp.zeros_like(dw_ref)

        u = u_ref[...]
        for p, ref in enumerate((dq_ref, dkv_ref, drz_ref)):
            dw_ref[:, DZ_CUTS[p]:DZ_CUTS[p + 1]] += _mm_tn(u, ref[...])

    return _hosting_call(
        body, "in_proj_bwd_w", rows // tb,
        [pl.BlockSpec((tb, D_MODEL), lambda i: (i, 0))] + _dz_specs(tb),
        [_full((D_MODEL, IN_WIDTH))],
        [jax.ShapeDtypeStruct((D_MODEL, IN_WIDTH), F32)],
        [], (u1, dq, dkv, drz), carried, modes)


def _adamw_math(w, m, v, g):
    nm = ADAM_B1 * m + (1.0 - ADAM_B1) * g
    nv = ADAM_B2 * v + (1.0 - ADAM_B2) * (g * g)
    m_hat = nm / (1.0 - ADAM_B1 ** ADAM_STEP)
    v_hat = nv / (1.0 - ADAM_B2 ** ADAM_STEP)
    return (-ADAM_LR) * (m_hat / (jnp.sqrt(v_hat) + ADAM_EPS) + ADAM_WD * w), nm, nv


SMALL_NAMES = ("conv_b", "b_a", "b_x", "lru_lambda", "attn_sinks", "g_post_mix", "g_pre_ffn", "g_post_ffn",
               "w_a", "w_x")
PACK_WIDTH = 1024


def _rows_view(a):
    return (a.size // PACK_WIDTH, PACK_WIDTH) if a.size >= PACK_WIDTH else (1, a.size)


def _pack_rows(vals):
    single = [name for name in SMALL_NAMES if _rows_view(vals[name])[0] == 1]
    assert len(single) == SUBLANES and SMALL_NAMES[:SUBLANES] == tuple(single)
    row = lax.broadcasted_iota(jnp.int32, (SUBLANES, PACK_WIDTH), 0)
    first = jnp.zeros((SUBLANES, PACK_WIDTH), F32)
    for k, name in enumerate(single):
        a = vals[name].reshape(1, -1)
        first = jnp.where(row == k, jnp.pad(a, ((0, 0), (0, PACK_WIDTH - a.shape[1]))), first)
    return jnp.concatenate([first] + [vals[name].reshape(_rows_view(vals[name])) for name in SMALL_NAMES[SUBLANES:]], axis=0)


def _adamw_small(weights, mom_m, mom_v, parts, loss_parts):
    n = len(SMALL_NAMES)
    views = [_rows_view(weights[name]) for name in SMALL_NAMES]

    def body(*refs):
        w_refs, m_refs, v_refs = refs[:n], refs[n:2 * n], refs[2 * n:3 * n]
        p_ref, l_ref, loss_ref = refs[3 * n], refs[3 * n + 1], refs[3 * n + 2]
        outs = refs[3 * n + 3:]
        row = 0
        for k, (nr, c) in enumerate(views):
            g = p_ref[0, row:row + nr, 0:c]
            for s in range(1, N_DEV):
                g = g + p_ref[s, row:row + nr, 0:c]
            g_ref, d_ref, nm_ref, nv_ref = outs[4 * k:4 * k + 4]
            g_ref[...] = g
            d_ref[...], nm_ref[...], nv_ref[...] = _adamw_math(w_refs[k][...], m_refs[k][...], v_refs[k][...], g)
            row += nr
        total = l_ref[0]
        for s in range(1, N_DEV):
            total = total + l_ref[s]
        loss_ref[...] = total

    args = [src[name].reshape(view) for src in (weights, mom_m, mom_v) for name, view in zip(SMALL_NAMES, views)]
    res = pl.pallas_call(
        body, name="adamw_small",
        out_shape=[jax.ShapeDtypeStruct(loss_parts.shape[1:], F32)]
                  + [jax.ShapeDtypeStruct(view, F32) for view in views for _ in range(4)],
        compiler_params=pltpu.CompilerParams(vmem_limit_bytes=VMEM_LIMIT),
    )(*args, parts, loss_parts)
    out = {name: tuple(t.reshape(weights[name].shape) for t in res[1 + 4 * k:5 + 4 * k]) for k, name in enumerate(SMALL_NAMES)}
    return res[0], out


def _adamw(w, m, v, parts, name):
    rows, cols = w.shape
    tr = next((t for t in (256, 128) if rows % t == 0), rows)
    parts = parts if isinstance(parts, (list, tuple)) else [parts]

    def body(w_ref, m_ref, v_ref, *refs):
        p_refs, (g_ref, d_ref, nm_ref, nv_ref) = refs[:len(parts)], refs[len(parts):]

        def total(p_ref):
            g = p_ref[0].astype(F32)
            for s in range(1, N_DEV):
                g = g + p_ref[s].astype(F32)
            return g

        g = jnp.concatenate([total(p_ref) for p_ref in p_refs], axis=1) if len(parts) > 1 else total(p_refs[0])
        g_ref[...] = g
        d_ref[...], nm_ref[...], nv_ref[...] = _adamw_math(w_ref[...], m_ref[...], v_ref[...], g)

    blk = pl.BlockSpec((tr, cols), lambda i: (i, 0))
    return pl.pallas_call(
        body, name=name, grid=(rows // tr,),
        in_specs=[blk, blk, blk] + [pl.BlockSpec((N_DEV, tr, p.shape[2]), lambda i: (0, i, 0)) for p in parts],
        out_specs=[blk] * 4,
        out_shape=[jax.ShapeDtypeStruct((rows, cols), F32)] * 4,
        compiler_params=_params(("parallel",)),
    )(w, m, v, *parts)


def _cols_from_shards(g):
    return jnp.transpose(g, (1, 0, 2)).reshape(g.shape[1], N_DEV * g.shape[2])


def _cols_to_shards(a):
    r, c = a.shape
    return jnp.transpose(a.reshape(r, N_DEV, c // N_DEV), (1, 0, 2))


def _block_diag(w):
    per = LRU_HALF // LRU_BLOCK
    w = w.reshape(2, per, LRU_BLOCK, LRU_BLOCK)
    eye = jnp.eye(per, dtype=w.dtype)
    return (w[:, :, :, None, :] * eye[None, :, None, :, None]).reshape(2, LRU_HALF, LRU_HALF)


def _block_diag_extract(t):
    per = LRU_HALF // LRU_BLOCK
    t = t.reshape(2, per, LRU_BLOCK, per, LRU_BLOCK)
    return jnp.stack([t[:, b, :, b, :] for b in range(per)], axis=1).reshape(LRU_BLOCKS, LRU_BLOCK, LRU_BLOCK)


def kernel(x, meta_tokens, g_pre_mix, w_in, conv_w, conv_b, w_a, b_a, w_x, b_x, lru_lambda, attn_sinks, w_out, g_post_mix, g_pre_ffn, w_ff1, w_ff2, g_post_ffn, loss_target, m_meta_tokens, m_g_pre_mix, m_w_in, m_conv_w, m_conv_b, m_w_a, m_b_a, m_w_x, m_b_x, m_lru_lambda, m_attn_sinks, m_w_out, m_g_post_mix, m_g_pre_ffn, m_w_ff1, m_w_ff2, m_g_post_ffn, v_meta_tokens, v_g_pre_mix, v_w_in, v_conv_w, v_conv_b, v_w_a, v_b_a, v_w_x, v_b_x, v_lru_lambda, v_attn_sinks, v_w_out, v_g_post_mix, v_g_pre_ffn, v_w_ff1, v_w_ff2, v_g_post_ffn):
    weights = dict(meta_tokens=meta_tokens, g_pre_mix=g_pre_mix, w_in=w_in, conv_w=conv_w, conv_b=conv_b, w_a=w_a,
                   b_a=b_a, w_x=w_x, b_x=b_x, lru_lambda=lru_lambda, attn_sinks=attn_sinks, w_out=w_out,
                   g_post_mix=g_post_mix, g_pre_ffn=g_pre_ffn, w_ff1=w_ff1, w_ff2=w_ff2, g_post_ffn=g_post_ffn)
    mom_m = dict(meta_tokens=m_meta_tokens, g_pre_mix=m_g_pre_mix, w_in=m_w_in, conv_w=m_conv_w, conv_b=m_conv_b,
                 w_a=m_w_a, b_a=m_b_a, w_x=m_w_x, b_x=m_b_x, lru_lambda=m_lru_lambda, attn_sinks=m_attn_sinks,
                 w_out=m_w_out, g_post_mix=m_g_post_mix, g_pre_ffn=m_g_pre_ffn, w_ff1=m_w_ff1, w_ff2=m_w_ff2,
                 g_post_ffn=m_g_post_ffn)
    mom_v = dict(meta_tokens=v_meta_tokens, g_pre_mix=v_g_pre_mix, w_in=v_w_in, conv_w=v_conv_w, conv_b=v_conv_b,
                 w_a=v_w_a, b_a=v_b_a, w_x=v_w_x, b_x=v_b_x, lru_lambda=v_lru_lambda, attn_sinks=v_attn_sinks,
                 w_out=v_w_out, g_post_mix=v_g_post_mix, g_pre_ffn=v_g_pre_ffn, w_ff1=v_w_ff1, w_ff2=v_w_ff2,
                 g_post_ffn=v_g_post_ffn)
    order = list(weights)

    (g_win, g_meta, g_cw) = _gather_two_level([w_in[0].astype(BF16), meta_tokens, conv_w[0]], "gather_first")
    w_in_full = _cols_from_shards(g_win)
    meta_full = _cols_from_shards(g_meta)
    conv_w_full = _cols_from_shards(g_cw)

    head = jnp.concatenate([jnp.zeros((PAD_ROWS, D_MODEL), F32), meta_full], axis=0)
    wa_bd = _block_diag(w_a[0]).astype(BF16)
    wx_bd = _block_diag(w_x[0]).astype(BF16)
    bias = _attn_bias()

    w1_shard = w_ff1[0].astype(BF16)
    (qkv, zrec, u1, h0), (g_wout,) = _in_proj_fwd(head, x[0], g_pre_mix, w_in_full, [w_out[0].astype(BF16)], ["gather"])
    (attn,), (w1a,) = _attn_fwd(qkv, attn_sinks, bias, [w1_shard[:, :FF_HALF]], ["gather"])
    (rec, h_lru, kept), (w1b,) = _rec_fwd(zrec, conv_w_full, conv_b, wa_bd, b_a, wx_bd, b_x, lru_lambda,
                                         [w1_shard[:, FF_HALF:]], ["gather"])
    w_out_full = g_wout.reshape(D_MODEL, D_MODEL)
    w2_shard = w_ff2[0].astype(BF16)
    (mix, h1), (w2a,) = _out_proj_fwd(attn, rec, w_out_full, h0, g_post_mix, [w2_shard[:FF_HALF]], ["gather"])
    (act, u2), (w2b,) = _ffn_up(h1, g_pre_ffn, (w1a, w1b), [w2_shard[FF_HALF:]], ["gather"])
    w2_halves = [w.reshape(D_FF // 2, D_MODEL) for w in (w2a, w2b)]
    dy, df, dg_post_ffn, loss_acc = _ffn_down_loss(act, w2_halves, h1, loss_target[0], g_post_ffn)

    da1 = _ffn_bwd_act(df, [jnp.transpose(w, (0, 2, 1)) for w in (w2a, w2b)], act)
    dw1h, dw2g = _ffn_bwd_weights(u2, da1, act, df)
    w1t = jnp.concatenate([jnp.transpose(w, (0, 2, 1)).reshape(D_FF // 2, D_MODEL) for w in (w1a, w1b)], axis=0)
    (dh1, dg_pre_ffn), (p_w1a,) = _ffn_bwd_x(da1, w1t, h1, dy, g_pre_ffn, [dw1h[0]], ["scatter"])
    (dattn, drec, dw_out, dg_post_mix), (p_w1b,) = _out_proj_bwd(dh1, mix, g_post_mix, w_out_full.T, attn, rec,
                                                                [dw1h[1]], ["scatter"])
    (dq, dkv_late, dsinks), (p_w2,) = _attn_bwd(qkv, dattn, attn_sinks, bias, [dw2g], ["scatter"])
    dkv = dkv_late[BLOCK:BLOCK + qkv.shape[0]]
    (drz, rec_small, dwa_bd, dwx_bd), (p_wout,) = _rec_bwd(
        drec, zrec, h_lru, kept, conv_w_full, wa_bd, wx_bd, lru_lambda,
        [dw_out.reshape(N_DEV, D_MODEL // N_DEV, D_MODEL)], ["scatter"])
    small_grads = dict(
        conv_b=rec_small[ROW_CONV_B], w_a=_block_diag_extract(dwa_bd), b_a=rec_small[ROW_B_A],
        w_x=_block_diag_extract(dwx_bd), b_x=rec_small[ROW_B_X], lru_lambda=rec_small[ROW_LAMBDA],
        attn_sinks=dsinks[:, 0], g_post_mix=dg_post_mix, g_pre_ffn=dg_pre_ffn, g_post_ffn=dg_post_ffn)
    (dw_in,), (p_cw, p_small) = _in_proj_bwd_w(
        u1, dq, dkv, drz, [_cols_to_shards(rec_small[0:CONV_WIDTH]), _pack_rows(small_grads)], ["scatter", "gather"])
    (dh0, dg_pre_mix), (p_win,) = _in_proj_bwd_x(
        h0, g_pre_mix, dh1, dq, dkv, drz, w_in_full.T, [_cols_to_shards(dw_in).astype(BF16)], ["scatter"])
    p_meta, p_gpm, p_loss = _exchange([_cols_to_shards(dh0[PAD_ROWS:BLOCK]), dg_pre_mix, loss_acc],
                                      ["scatter", "gather", "gather"], "exchange_last")

    res = {}
    res["g_pre_mix"] = _adamw(g_pre_mix, m_g_pre_mix, v_g_pre_mix, p_gpm, "adamw_g_pre_mix")
    res["w_in"] = _adamw(w_in[0], m_w_in[0], v_w_in[0], p_win, "adamw_w_in")
    res["w_out"] = _adamw(w_out[0], m_w_out[0], v_w_out[0], p_wout, "adamw_w_out")
    res["w_ff1"] = _adamw(w_ff1[0], m_w_ff1[0], v_w_ff1[0], [p_w1a, p_w1b], "adamw_w_ff1")
    res["w_ff2"] = _adamw(w_ff2[0], m_w_ff2[0], v_w_ff2[0], p_w2, "adamw_w_ff2")
    res["meta_tokens"] = _adamw(meta_tokens, m_meta_tokens, v_meta_tokens, p_meta, "adamw_meta")
    res["conv_w"] = _adamw(conv_w[0], m_conv_w[0], v_conv_w[0], p_cw, "adamw_conv_w")
    for name in ("w_in", "w_out", "w_ff1", "w_ff2", "conv_w"):
        res[name] = tuple(t[None] for t in res[name])
    loss_total, small = _adamw_small(weights, mom_m, mom_v, p_small, p_loss)
    res.update(small)

    grad_x = dh0[BLOCK:][None]
    outs = [loss_total[0, 0], grad_x]
    for k in range(4):
        outs += [res[name][k] for name in order]
    return tuple(outs)
```

```python
import jax
import jax.numpy as jnp
import numpy as np
from jax import lax
from jax.experimental import pallas as pl
from jax.experimental.pallas import tpu as pltpu

F32 = jnp.float32
BF16 = jnp.bfloat16

D_MODEL = 1024
N_META = 16
HEAD_DIM = 64
ATTN_HEADS = 8
KV_HEADS = 2
GQA_GROUP = ATTN_HEADS // KV_HEADS
ATTN_WIDTH = ATTN_HEADS * HEAD_DIM
KV_WIDTH = KV_HEADS * HEAD_DIM
QKV_WIDTH = ATTN_WIDTH + 2 * KV_WIDTH
LRU_WIDTH = 512
LRU_BLOCKS = 8
LRU_BLOCK = 64
LRU_HALF = 256
LRU_C = 8.0
CONV_WIDTH = 4
BLOCK = 128
PAD_ROWS = BLOCK - N_META
IN_WIDTH = QKV_WIDTH + 2 * LRU_WIDTH
D_FF = 4096
EPS = 1e-6
NEG = -1e30
N_DEV = 8
FF_CHUNK = D_FF // N_DEV
SUBLANES = 8
LANES = 128

ADAM_LR = 0.001
ADAM_B1 = 0.9
ADAM_B2 = 0.999
ADAM_EPS = 1e-08
ADAM_WD = 0.01
ADAM_STEP = 10

VMEM_LIMIT = 56 * 1024 * 1024


def _row_tile(rows):
    for t in (640, 512, 256, 128):
        if rows % t == 0:
            return t
    raise ValueError(rows)


def _big_tile(rows):
    for t in (1664, 1024, 512, 256, 128):
        if rows % t == 0:
            return t
    raise ValueError(rows)


def _rec_tile(rows):
    for t in (320, 256, 128):
        if rows % t == 0:
            return t
    raise ValueError(rows)


def _params(semantics):
    return pltpu.CompilerParams(dimension_semantics=semantics, vmem_limit_bytes=VMEM_LIMIT)


def _mm(a, b):
    return lax.dot_general(a, b, (((1,), (0,)), ((), ())), preferred_element_type=F32)


def _mm_nt(a, b):
    return lax.dot_general(a, b, (((1,), (1,)), ((), ())), preferred_element_type=F32)


def _mm_tn(a, b):
    return lax.dot_general(a, b, (((0,), (0,)), ((), ())), preferred_element_type=F32)


def _rms_fwd(x, g):
    rstd = lax.rsqrt(jnp.mean(x * x, axis=-1, keepdims=True) + EPS)
    xhat = x * rstd
    return xhat * g, xhat, rstd


def _rms_bwd(dy, xhat, rstd, g):
    dyg = dy * g
    c = jnp.mean(dyg * xhat, axis=-1, keepdims=True)
    dx = rstd * (dyg - xhat * c)
    dg = jnp.sum(dy * xhat, axis=0, keepdims=True)
    return dx, dg


def _sigmoid(x):
    return 0.5 * jnp.tanh(0.5 * x) + 0.5


def _log1p(x):
    u = 1.0 + x
    return jnp.where(u == 1.0, x, jnp.log(u) * x / (u - 1.0))


def _one_minus_sq_exp(x, ex):
    return -jnp.tanh(x) * (1.0 + ex * ex)


TINY = 1e-30


def _sqrt_pos(y):
    r = lax.rsqrt(jnp.maximum(y, TINY))
    return y * r, r


def _softplus(x):
    return jnp.maximum(x, 0.0) + _log1p(jnp.exp(-jnp.abs(x)))


GELU_C = 0.7978845608028654
GELU_K = 0.044715


def _gelu(x):
    t = jnp.tanh(GELU_C * (x + GELU_K * x * x * x))
    return 0.5 * x * (1.0 + t), t


def _gelu_grad(x, t):
    return 0.5 * (1.0 + t) + 0.5 * x * (1.0 - t * t) * GELU_C * (1.0 + 3.0 * GELU_K * x * x)


def _full(shape):
    return pl.BlockSpec(shape, lambda *_: (0,) * len(shape))


def _resident(shape):
    return pl.BlockSpec(shape, lambda *_: (0,) * len(shape), pipeline_mode=pl.Buffered(1))


def _exchange_copies(ins, outs, sems, modes):
    send_sems, recv_sems, local_sems = sems
    x, y, c = lax.axis_index("x"), lax.axis_index("y"), lax.axis_index("c")
    me = 4 * x + 2 * y + c

    def block(a, dev):
        return ins[a] if modes[a] == "gather" else ins[a].at[dev]

    local = [pltpu.make_async_copy(block(a, me), outs[a].at[me], local_sems.at[a]) for a in range(len(ins))]
    sends, recvs = [], []
    for a in range(len(ins)):
        for k in range(N_DEV - 1):
            bits = k + 1
            px = jnp.bitwise_xor(x, (bits >> 2) & 1)
            py = jnp.bitwise_xor(y, (bits >> 1) & 1)
            pc = jnp.bitwise_xor(c, bits & 1)
            peer = 4 * px + 2 * py + pc
            common = dict(src_ref=block(a, peer), send_sem=send_sems.at[a, k], recv_sem=recv_sems.at[a, k],
                          device_id=(px, py, pc), device_id_type=pl.DeviceIdType.MESH)
            sends.append(pltpu.make_async_remote_copy(dst_ref=outs[a].at[me], **common))
            recvs.append(pltpu.make_async_remote_copy(dst_ref=outs[a].at[peer], **common))
    return local, sends, recvs


def _exchange_start(ins, outs, sems, modes):
    local, sends, _ = _exchange_copies(ins, outs, sems, modes)
    for cp in local + sends:
        cp.start()


def _exchange_wait(ins, outs, sems, modes):
    local, sends, recvs = _exchange_copies(ins, outs, sems, modes)
    for cp in recvs:
        cp.wait_recv()
    for cp in sends:
        cp.wait_send()
    for cp in local:
        cp.wait()


def _exchange_shapes(arrays, modes):
    return [jax.ShapeDtypeStruct((N_DEV,) + a.shape if mode == "gather" else a.shape, a.dtype)
            for a, mode in zip(arrays, modes)]


def _exchange_sems(na):
    return [pltpu.SemaphoreType.DMA((na, N_DEV - 1)), pltpu.SemaphoreType.DMA((na, N_DEV - 1)),
            pltpu.SemaphoreType.DMA((na,))]


ANY_SPACE = pl.BlockSpec(memory_space=pl.ANY)


def _exchange(arrays, modes, name):
    na = len(arrays)

    def body(*refs):
        ins, outs, sems = refs[:na], refs[na:2 * na], refs[2 * na:]
        _exchange_start(ins, outs, sems, modes)
        _exchange_wait(ins, outs, sems, modes)

    return pl.pallas_call(
        body, name=name, out_shape=_exchange_shapes(arrays, modes),
        in_specs=[ANY_SPACE] * na, out_specs=[ANY_SPACE] * na, scratch_shapes=_exchange_sems(na),
        compiler_params=pltpu.CompilerParams(has_side_effects=True),
    )(*arrays)


def _gather_two_level(arrays, name):
    na = len(arrays)

    def body(*refs):
        ins, outs = refs[:na], refs[na:2 * na]
        send_sems, recv_sems, local_sems = refs[2 * na:]
        x, y, c = lax.axis_index("x"), lax.axis_index("y"), lax.axis_index("c")
        me, sibling = (x, y, c), (x, y, 1 - c)
        chips = [(1 - x, y), (x, 1 - y), (1 - x, 1 - y)]

        def copy(a, k, block, to, src=None):
            slot = outs[a].at[4 * block[0] + 2 * block[1] + block[2]]
            return pltpu.make_async_remote_copy(
                src_ref=slot if src is None else src, dst_ref=slot, send_sem=send_sems.at[a, k],
                recv_sem=recv_sems.at[a, k], device_id=to, device_id_type=pl.DeviceIdType.MESH)

        local = [pltpu.make_async_copy(ins[a], outs[a].at[4 * x + 2 * y + c], local_sems.at[a]) for a in range(na)]
        first = []
        for a in range(na):
            first.append(copy(a, 0, me, sibling, src=ins[a]))
            first += [copy(a, 1 + j, me, (*chip, c), src=ins[a]) for j, chip in enumerate(chips)]
        for cp in local + first:
            cp.start()
        passed = []
        for j, chip in enumerate(chips):
            for a in range(na):
                copy(a, 1 + j, (*chip, c), me).wait_recv()
                passed.append(copy(a, 4 + j, (*chip, c), sibling))
                passed[-1].start()
        for a in range(na):
            copy(a, 0, sibling, me).wait_recv()
            for j, chip in enumerate(chips):
                copy(a, 4 + j, (*chip, 1 - c), me).wait_recv()
        for cp in first + passed:
            cp.wait_send()
        for cp in local:
            cp.wait()

    return pl.pallas_call(
        body, name=name, out_shape=_exchange_shapes(arrays, ["gather"] * na),
        in_specs=[ANY_SPACE] * na, out_specs=[ANY_SPACE] * na, scratch_shapes=_exchange_sems(na),
        compiler_params=pltpu.CompilerParams(has_side_effects=True),
    )(*arrays)


def _hosting_call(body, name, steps, in_specs, out_specs, out_shape, scratch_shapes, args, arrays, modes):
    n_in, n_out, n_scr, na = len(in_specs), len(out_specs), len(scratch_shapes), len(arrays)

    def hosting_body(*refs):
        cuts = [0]
        for n in (n_in, na, n_out, na, n_scr, 3):
            cuts.append(cuts[-1] + n)
        ins, x_ins, outs, x_outs, scr, sems = (refs[cuts[p]:cuts[p + 1]] for p in range(6))
        step = pl.program_id(0)

        @pl.when(step == 0)
        def _():
            _exchange_start(x_ins, x_outs, sems, modes)

        body(*ins, *outs, *scr)

        @pl.when(step == steps - 1)
        def _():
            _exchange_wait(x_ins, x_outs, sems, modes)

    res = pl.pallas_call(
        hosting_body, name=name, grid=(steps,),
        in_specs=list(in_specs) + [ANY_SPACE] * na, out_specs=list(out_specs) + [ANY_SPACE] * na,
        out_shape=list(out_shape) + _exchange_shapes(arrays, modes),
        scratch_shapes=list(scratch_shapes) + _exchange_sems(na),
        compiler_params=_params(("arbitrary",)),
    )(*args, *arrays)
    return res[:n_out], res[n_out:]


def _frame_rows(src_hbm, buf, sem, i, steps, tm):
    def first():
        return pltpu.make_async_copy(src_hbm.at[pl.ds(0, tm - BLOCK)], buf.at[0, pl.ds(BLOCK, tm - BLOCK)], sem.at[0])

    def later(t, slot):
        return pltpu.make_async_copy(src_hbm.at[pl.ds(pl.multiple_of(t * tm - BLOCK, BLOCK), tm)], buf.at[slot], sem.at[slot])

    slot = i % 2

    @pl.when(i == 0)
    def _():
        first().start()

    @pl.when(i + 1 < steps)
    def _():
        later(i + 1, 1 - slot).start()

    @pl.when(i == 0)
    def _():
        first().wait()

    @pl.when(i > 0)
    def _():
        later(i, slot).wait()

    return slot


def _frame_scratch(tm):
    return [pltpu.VMEM((2, tm, D_MODEL), F32), pltpu.SemaphoreType.DMA((2,))]


def _h0_tile(head_ref, x_hbm, buf, sem, i, steps, tm):
    slot = _frame_rows(x_hbm, buf, sem, i, steps, tm)

    @pl.when(i == 0)
    def _():
        buf[0, 0:BLOCK, :] = head_ref[...]

    return buf[slot]


def _in_proj_fwd(head, x, g1, w_in, carried, modes):
    rows = BLOCK + x.shape[0]
    tm = _row_tile(rows)
    steps = rows // tm

    def body(head_ref, g_ref, w_ref, x_hbm, qkv_ref, zrec_ref, u_ref, buf, sem):
        h = _h0_tile(head_ref, x_hbm, buf, sem, pl.program_id(0), steps, tm)
        u, _, _ = _rms_fwd(h, g_ref[...])
        u = u.astype(BF16)
        u_ref[...] = u
        z = _mm(u, w_ref[...])
        qkv_ref[...] = z[:, :QKV_WIDTH].astype(BF16)
        zrec_ref[...] = z[:, QKV_WIDTH:]

    wide = pl.BlockSpec((tm, D_MODEL), lambda i: (i, 0))
    return _hosting_call(
        body, "in_proj_fwd", steps,
        [_full((BLOCK, D_MODEL)), _full((1, D_MODEL)), _resident((D_MODEL, IN_WIDTH)), ANY_SPACE],
        [pl.BlockSpec((tm, QKV_WIDTH), lambda i: (i, 0)), pl.BlockSpec((tm, 2 * LRU_WIDTH), lambda i: (i, 0)), wide],
        [jax.ShapeDtypeStruct((rows, QKV_WIDTH), BF16), jax.ShapeDtypeStruct((rows, 2 * LRU_WIDTH), F32),
         jax.ShapeDtypeStruct((rows, D_MODEL), BF16)],
        _frame_scratch(tm), (head, g1, w_in, x), carried, modes)


N_BIAS = 3


def _attn_bias():
    key = np.arange(2 * BLOCK)[:, None]
    r = np.arange(GQA_GROUP * BLOCK)[None, :] % BLOCK
    band = (key > r) & (key <= r + BLOCK)
    out = [np.where(band & ((n - 1) * BLOCK + key >= PAD_ROWS), 0.0, NEG) for n in range(N_BIAS)]
    return jnp.asarray(np.stack(out), F32)


def _attn_probs(k2, q4, bias, sink_row):
    s = _mm_nt(k2, q4) * (HEAD_DIM ** -0.5) + bias
    m = jnp.maximum(jnp.max(s, axis=0, keepdims=True), sink_row)
    p = jnp.exp(s - m)
    es = jnp.exp(sink_row - m)
    inv = 1.0 / (jnp.sum(p, axis=0, keepdims=True) + es)
    return p * inv, es * inv


def _heads(ref, rows, first, count):
    return jnp.concatenate([ref[rows, (first + g) * HEAD_DIM:(first + g + 1) * HEAD_DIM] for g in range(count)], axis=0)


def _keys_of_block(prev_ref, cur_ref, b, kv):
    sl = slice(kv * HEAD_DIM, (kv + 1) * HEAD_DIM)
    before = prev_ref[:, sl] if b == 0 else cur_ref[(b - 1) * BLOCK:b * BLOCK, sl]
    return jnp.concatenate([before, cur_ref[b * BLOCK:(b + 1) * BLOCK, sl]], axis=0)


def _bias_of_block(bias_ref, block):
    return bias_ref[jnp.minimum(block, N_BIAS - 1)]


def _sink_row(sink_ref, kv):
    g = lax.broadcasted_iota(jnp.int32, (1, GQA_GROUP * BLOCK), 1) // BLOCK
    row = jnp.full((1, GQA_GROUP * BLOCK), sink_ref[0, kv * GQA_GROUP], F32)
    for i in range(1, GQA_GROUP):
        row = jnp.where(g == i, sink_ref[0, kv * GQA_GROUP + i], row)
    return row


def _from_head_major(pieces):
    return jnp.concatenate(pieces, axis=0).T


def _attn_specs(tm, tile_of):
    nbt = tm // BLOCK
    k_col, v_col = ATTN_WIDTH // KV_WIDTH, ATTN_WIDTH // KV_WIDTH + 1
    before = lambda i: jnp.maximum(tile_of(i) * nbt - 1, 0)
    return [pl.BlockSpec((tm, ATTN_WIDTH), lambda i: (tile_of(i), 0)),
            pl.BlockSpec((BLOCK, KV_WIDTH), lambda i: (before(i), k_col)),
            pl.BlockSpec((tm, KV_WIDTH), lambda i: (tile_of(i), k_col)),
            pl.BlockSpec((BLOCK, KV_WIDTH), lambda i: (before(i), v_col)),
            pl.BlockSpec((tm, KV_WIDTH), lambda i: (tile_of(i), v_col))]


def _attn_fwd(qkv, sinks, bias, carried, modes):
    rows = qkv.shape[0]
    tm = _row_tile(rows)
    nbt = tm // BLOCK

    def body(sink_ref, bias_ref, q_ref, kp_ref, kc_ref, vp_ref, vc_ref, o_ref):
        i = pl.program_id(0)
        for b in range(nbt):
            blk = slice(b * BLOCK, (b + 1) * BLOCK)
            bias_t = _bias_of_block(bias_ref, i * nbt + b)
            pieces = []
            for kv in range(KV_HEADS):
                k2 = _keys_of_block(kp_ref, kc_ref, b, kv)
                v2 = _keys_of_block(vp_ref, vc_ref, b, kv)
                q4 = _heads(q_ref, blk, kv * GQA_GROUP, GQA_GROUP)
                pn, _ = _attn_probs(k2, q4, bias_t, _sink_row(sink_ref, kv))
                ot = _mm_tn(v2, pn.astype(BF16))
                pieces += [ot[:, g * BLOCK:(g + 1) * BLOCK] for g in range(GQA_GROUP)]
            o_ref[blk, :] = _from_head_major(pieces).astype(BF16)

    return _hosting_call(
        body, "attn_fwd", rows // tm,
        [pl.BlockSpec(memory_space=pltpu.SMEM), _resident((N_BIAS, 2 * BLOCK, GQA_GROUP * BLOCK))]
        + _attn_specs(tm, lambda i: i),
        [pl.BlockSpec((tm, ATTN_WIDTH), lambda i: (i, 0))],
        [jax.ShapeDtypeStruct((rows, ATTN_WIDTH), BF16)],
        [], (sinks, bias, qkv, qkv, qkv, qkv, qkv), carried, modes)


def _conv_taps(xbuf, tm):
    return [xbuf[pl.ds(SUBLANES - (CONV_WIDTH - 1 - j), tm), :] for j in range(CONV_WIDTH)]


def _lru_halves(xc):
    return [xc[:, h * LRU_HALF:(h + 1) * LRU_HALF].astype(BF16) for h in range(2)]


def _lru_gates(xc, wa_ref, ba_ref, wx_ref, bx_ref, lam_ref):
    halves = _lru_halves(xc)
    gate_r = jnp.concatenate([_mm(halves[h], wa_ref[h]) for h in range(2)], axis=1) + ba_ref[...]
    gate_i = jnp.concatenate([_mm(halves[h], wx_ref[h]) for h in range(2)], axis=1) + bx_ref[...]
    r = _sigmoid(gate_r)
    ig = _sigmoid(gate_i)
    log_a = (-LRU_C) * r * _softplus(-lam_ref[...])
    a = jnp.exp(log_a)
    mult, _ = _sqrt_pos(_one_minus_sq_exp(log_a, a))
    return r, ig, a, mult


KEPT_XC, KEPT_A, KEPT_MULT, KEPT_R, KEPT_I, N_KEPT = 0, 1, 2, 3, 4, 5


def _scan_tile(a_ref, u_ref, out_ref, carry, tm, reverse):
    row = lax.broadcasted_iota(jnp.int32, (SUBLANES, LRU_WIDTH), 0)
    groups = tm // SUBLANES

    def step(j, prev):
        jj = groups - 1 - j if reverse else j
        o = pl.multiple_of(jj * SUBLANES, SUBLANES)
        a = a_ref[pl.ds(o, SUBLANES), :]
        u = u_ref[pl.ds(o, SUBLANES), :]
        for s in (1, 2, 4):
            shift = SUBLANES - s if reverse else s
            keep = (row < SUBLANES - s) if reverse else (row >= s)
            u = jnp.where(keep, a * pltpu.roll(u, shift, 0) + u, u)
            a = jnp.where(keep, a * pltpu.roll(a, shift, 0), a)
        out = a * prev + u
        out_ref[pl.ds(o, SUBLANES), :] = out
        return out[0:1, :] if reverse else out[SUBLANES - 1:SUBLANES, :]

    return lax.fori_loop(0, groups, step, carry)


def _rec_fwd(zrec, conv_w, conv_b, wa_bd, b_a, wx_bd, b_x, lam, carried, modes):
    rows = zrec.shape[0]
    tm = _row_tile(rows)

    def body(xr_ref, yr_ref, cw_ref, cb_ref, wa_ref, ba_ref, wx_ref, bx_ref, lam_ref, rec_ref, h_ref, kept_ref,
             xbuf, a_s, u_s, carry):
        i = pl.program_id(0)

        @pl.when(i == 0)
        def _():
            xbuf[0:SUBLANES, :] = jnp.zeros((SUBLANES, LRU_WIDTH), F32)
            carry[...] = jnp.zeros_like(carry)

        @pl.when(i > 0)
        def _():
            xbuf[0:SUBLANES, :] = xbuf[tm:tm + SUBLANES, :]

        xbuf[SUBLANES:SUBLANES + tm, :] = xr_ref[...]
        taps = _conv_taps(xbuf, tm)
        xc = cb_ref[...] + sum(cw_ref[j:j + 1, :] * taps[j] for j in range(CONV_WIDTH))
        r, ig, a, mult = _lru_gates(xc, wa_ref, ba_ref, wx_ref, bx_ref, lam_ref)
        for k, val in ((KEPT_XC, xc), (KEPT_A, a), (KEPT_MULT, mult), (KEPT_R, r), (KEPT_I, ig)):
            kept_ref[:, k * LRU_WIDTH:(k + 1) * LRU_WIDTH] = val
        grow = i * tm + lax.broadcasted_iota(jnp.int32, (tm, LRU_WIDTH), 0)
        a_s[...] = a
        u_s[...] = jnp.where(grow >= PAD_ROWS, mult * (ig * xc), 0.0)
        carry[0:1, :] = _scan_tile(a_s, u_s, h_ref, carry[0:1, :], tm, reverse=False)
        gel, _ = _gelu(yr_ref[...])
        rec_ref[...] = (gel * h_ref[...]).astype(BF16)

    vec = _full((1, LRU_WIDTH))
    bd = _full((2, LRU_HALF, LRU_HALF))
    return _hosting_call(
        body, "rec_fwd", rows // tm,
        [pl.BlockSpec((tm, LRU_WIDTH), lambda i: (i, 0)), pl.BlockSpec((tm, LRU_WIDTH), lambda i: (i, 1)),
         _full((CONV_WIDTH, LRU_WIDTH)), vec, bd, vec, bd, vec, vec],
        [pl.BlockSpec((tm, LRU_WIDTH), lambda i: (i, 0))] * 2 + [pl.BlockSpec((tm, N_KEPT * LRU_WIDTH), lambda i: (i, 0))],
        [jax.ShapeDtypeStruct((rows, LRU_WIDTH), BF16), jax.ShapeDtypeStruct((rows, LRU_WIDTH), F32),
         jax.ShapeDtypeStruct((rows, N_KEPT * LRU_WIDTH), F32)],
        [pltpu.VMEM((tm + SUBLANES, LRU_WIDTH), F32), pltpu.VMEM((tm, LRU_WIDTH), F32),
         pltpu.VMEM((tm, LRU_WIDTH), F32), pltpu.VMEM((SUBLANES, LRU_WIDTH), F32)],
        (zrec, zrec, conv_w, conv_b, wa_bd, b_a, wx_bd, b_x, lam), carried, modes)


def _out_proj_fwd(attn, rec, w_out, head, x, g2, carried, modes):
    rows = attn.shape[0]
    tm = _row_tile(rows)
    steps = rows // tm

    def body(attn_ref, rec_ref, w_ref, head_ref, g_ref, x_hbm, mix_ref, h1_ref, buf, sem):
        h0 = _h0_tile(head_ref, x_hbm, buf, sem, pl.program_id(0), steps, tm)
        mix = _mm(attn_ref[...], w_ref[0:ATTN_WIDTH, :]) + _mm(rec_ref[...], w_ref[ATTN_WIDTH:, :])
        y, _, _ = _rms_fwd(mix, g_ref[...])
        mix_ref[...] = mix
        h1_ref[...] = h0 + y

    half = pl.BlockSpec((tm, ATTN_WIDTH), lambda i: (i, 0))
    wide = pl.BlockSpec((tm, D_MODEL), lambda i: (i, 0))
    return _hosting_call(
        body, "out_proj_fwd", steps,
        [half, half, _resident((D_MODEL, D_MODEL)), _full((BLOCK, D_MODEL)), _full((1, D_MODEL)), ANY_SPACE],
        [wide, wide],
        [jax.ShapeDtypeStruct((rows, D_MODEL), F32)] * 2,
        _frame_scratch(tm), (attn, rec, w_out, head, g2, x), carried, modes)


FF_COLS = 1024
FF_HALF = FF_CHUNK // 2


def _hidden_at(d, half):
    return half * (D_FF // 2) + d * FF_HALF


def _ffn_up(h1, g3, w1_halves, carried, modes):
    rows = h1.shape[0]
    tm = _row_tile(rows)

    def body(h_ref, g_ref, wa_ref, wb_ref, act_ref, u_ref):
        u, _, _ = _rms_fwd(h_ref[...], g_ref[...])
        u = u.astype(BF16)
        u_ref[...] = u
        for half, w_ref in enumerate((wa_ref, wb_ref)):
            for d in range(N_DEV):
                c = _hidden_at(d, half)
                a1 = jnp.maximum(_mm(u, w_ref[d]), 0.0)
                act_ref[:, c:c + FF_HALF] = (a1 * a1).astype(BF16)

    wide = pl.BlockSpec((tm, D_MODEL), lambda i: (i, 0))
    return _hosting_call(
        body, "ffn_up", rows // tm,
        [wide, _full((1, D_MODEL))] + [_resident((N_DEV, D_MODEL, FF_HALF))] * 2,
        [pl.BlockSpec((tm, D_FF), lambda i: (i, 0)), wide],
        [jax.ShapeDtypeStruct((rows, D_FF), BF16), jax.ShapeDtypeStruct((rows, D_MODEL), BF16)],
        [], (h1, g3, *w1_halves), carried, modes)


def _ffn_down_loss(act, w2_halves, h1, target, g4):
    rows = h1.shape[0]
    tm = _row_tile(rows)
    steps = rows // tm
    kh = D_FF // 2

    def body(act_ref, wa_ref, wb_ref, h_ref, g_ref, t_hbm, dy_ref, df_ref, dg_ref, loss_ref, buf, sem):
        i = pl.program_id(0)
        slot = _frame_rows(t_hbm, buf, sem, i, steps, tm)

        @pl.when(i == 0)
        def _():
            dg_ref[...] = jnp.zeros_like(dg_ref)
            loss_ref[...] = jnp.zeros_like(loss_ref)
            buf[0, 0:BLOCK, :] = jnp.zeros((BLOCK, D_MODEL), F32)

        g = g_ref[...]
        f = _mm(act_ref[:, :kh], wa_ref[...]) + _mm(act_ref[:, kh:], wb_ref[...])
        y, fhat, rstd = _rms_fwd(f, g)
        grow = i * tm + lax.broadcasted_iota(jnp.int32, (tm, D_MODEL), 0)
        err = jnp.where(grow >= BLOCK, h_ref[...] + y - buf[slot], 0.0)
        loss_ref[...] += (0.5 / D_MODEL) * jnp.sum(err * err)
        dy = err * (1.0 / D_MODEL)
        df, dg = _rms_bwd(dy, fhat, rstd, g)
        dy_ref[...] = dy
        df_ref[...] = df.astype(BF16)
        dg_ref[...] += dg

    wide = pl.BlockSpec((tm, D_MODEL), lambda i: (i, 0))
    return pl.pallas_call(
        body, name="ffn_down_loss", grid=(steps,),
        in_specs=[pl.BlockSpec((tm, D_FF), lambda i: (i, 0)), _resident((kh, D_MODEL)), _resident((kh, D_MODEL)), wide,
                  _full((1, D_MODEL)), ANY_SPACE],
        out_specs=[wide, wide, _full((1, D_MODEL)), _full((SUBLANES, LANES))],
        out_shape=[jax.ShapeDtypeStruct((rows, D_MODEL), F32), jax.ShapeDtypeStruct((rows, D_MODEL), BF16),
                   jax.ShapeDtypeStruct((1, D_MODEL), F32), jax.ShapeDtypeStruct((SUBLANES, LANES), F32)],
        scratch_shapes=_frame_scratch(tm),
        compiler_params=_params(("arbitrary",)),
    )(act, *w2_halves, h1, g4, target)


def _ffn_bwd_act(df, w2t_halves, act):
    rows = df.shape[0]
    tm = _row_tile(rows)

    def body(df_ref, wa_ref, wb_ref, act_ref, da_ref):
        df_t = df_ref[...]
        for half, w_ref in enumerate((wa_ref, wb_ref)):
            for d in range(N_DEV):
                cols = slice(_hidden_at(d, half), _hidden_at(d, half) + FF_HALF)
                dact = _mm(df_t, w_ref[d])
                relu_a1, _ = _sqrt_pos(act_ref[:, cols].astype(F32))
                da_ref[:, cols] = (dact * (2.0 * relu_a1)).astype(BF16)

    hidden = pl.BlockSpec((tm, D_FF), lambda i: (i, 0))
    return pl.pallas_call(
        body, name="ffn_bwd_act", grid=(rows // tm,),
        in_specs=[pl.BlockSpec((tm, D_MODEL), lambda i: (i, 0))] + [_resident((N_DEV, D_MODEL, FF_HALF))] * 2 + [hidden],
        out_specs=hidden,
        out_shape=jax.ShapeDtypeStruct((rows, D_FF), BF16),
        compiler_params=_params(("parallel",)),
    )(df, *w2t_halves, act)


def _ffn_bwd_x(da, w1t, h1, dy, g3, carried, modes):
    rows = h1.shape[0]
    tm = _row_tile(rows)

    def body(da_ref, w_ref, h_ref, dy_ref, g_ref, dh_ref, dg_ref):
        @pl.when(pl.program_id(0) == 0)
        def _():
            dg_ref[...] = jnp.zeros_like(dg_ref)

        g = g_ref[...]
        _, xhat, rstd = _rms_fwd(h_ref[...], g)
        dx, dg = _rms_bwd(_mm(da_ref[...], w_ref[...]), xhat, rstd, g)
        dh_ref[...] = dy_ref[...] + dx
        dg_ref[...] += dg

    wide = pl.BlockSpec((tm, D_MODEL), lambda i: (i, 0))
    return _hosting_call(
        body, "ffn_bwd_x", rows // tm,
        [pl.BlockSpec((tm, D_FF), lambda i: (i, 0)), _resident((D_FF, D_MODEL)), wide, wide, _full((1, D_MODEL))],
        [wide, _full((1, D_MODEL))],
        [jax.ShapeDtypeStruct((rows, D_MODEL), F32), jax.ShapeDtypeStruct((1, D_MODEL), F32)],
        [], (da, w1t, h1, dy, g3), carried, modes)


def _ffn_bwd_weights(u2, da, act, df):
    rows = u2.shape[0]
    tb = _big_tile(rows)
    steps = rows // tb
    per = FF_COLS // FF_HALF

    def body(u_ref, da_ref, act_ref, df_ref, dw1_ref, dw2_ref, acc1, acc2):
        i = pl.program_id(1)

        @pl.when(i == 0)
        def _():
            acc1[...] = jnp.zeros_like(acc1)
            acc2[...] = jnp.zeros_like(acc2)

        acc1[...] += _mm_tn(u_ref[...], da_ref[...])
        acc2[...] += _mm_tn(act_ref[...], df_ref[...])

        @pl.when(i == steps - 1)
        def _():
            for p in range(per):
                c = p * FF_HALF
                dw1_ref[p] = acc1[:, c:c + FF_HALF].astype(BF16)
                dw2_ref[p] = acc2[c:c + FF_HALF, :].astype(BF16)

    wide = pl.BlockSpec((tb, D_MODEL), lambda j, i: (i, 0))
    chunk = pl.BlockSpec((tb, FF_COLS), lambda j, i: (i, j))
    return pl.pallas_call(
        body, name="ffn_bwd_weights", grid=(D_FF // FF_COLS, steps),
        in_specs=[wide, chunk, chunk, wide],
        out_specs=[pl.BlockSpec((None, per, D_MODEL, FF_HALF), lambda j, i: (j // 2, j % 2, 0, 0)),
                   pl.BlockSpec((per, FF_HALF, D_MODEL), lambda j, i: (j % 2, j // 2, 0))],
        out_shape=[jax.ShapeDtypeStruct((2, N_DEV, D_MODEL, FF_HALF), BF16),
                   jax.ShapeDtypeStruct((N_DEV, FF_CHUNK, D_MODEL), BF16)],
        scratch_shapes=[pltpu.VMEM((D_MODEL, FF_COLS), F32), pltpu.VMEM((FF_COLS, D_MODEL), F32)],
        compiler_params=_params(("parallel", "arbitrary")),
    )(u2, da, act, df)


def _out_proj_bwd(dh1, mix, g2, w_out_t, attn, rec, carried, modes):
    rows = dh1.shape[0]
    tm = _row_tile(rows)
    steps = rows // tm

    def body(dh_ref, mix_ref, g_ref, w_ref, attn_ref, rec_ref, dattn_ref, drec_ref, dw_ref, dg_ref, acc):
        i = pl.program_id(0)

        @pl.when(i == 0)
        def _():
            acc[...] = jnp.zeros_like(acc)
            dg_ref[...] = jnp.zeros_like(dg_ref)

        g = g_ref[...]
        _, xhat, rstd = _rms_fwd(mix_ref[...], g)
        dmix, dg = _rms_bwd(dh_ref[...], xhat, rstd, g)
        dmix = dmix.astype(BF16)
        dg_ref[...] += dg
        din = _mm(dmix, w_ref[...])
        dattn_ref[...] = din[:, :ATTN_WIDTH].astype(BF16)
        drec_ref[...] = din[:, ATTN_WIDTH:]
        acc[0:ATTN_WIDTH, :] += _mm_tn(attn_ref[...], dmix)
        acc[ATTN_WIDTH:, :] += _mm_tn(rec_ref[...], dmix)

        @pl.when(i == steps - 1)
        def _():
            dw_ref[...] = acc[...].astype(BF16)

    half = pl.BlockSpec((tm, ATTN_WIDTH), lambda i: (i, 0))
    wide = pl.BlockSpec((tm, D_MODEL), lambda i: (i, 0))
    return _hosting_call(
        body, "out_proj_bwd", steps,
        [wide, wide, _full((1, D_MODEL)), _resident((D_MODEL, D_MODEL)), half, half],
        [half, half, _full((D_MODEL, D_MODEL)), _full((1, D_MODEL))],
        [jax.ShapeDtypeStruct((rows, ATTN_WIDTH), BF16), jax.ShapeDtypeStruct((rows, LRU_WIDTH), F32),
         jax.ShapeDtypeStruct((D_MODEL, D_MODEL), BF16), jax.ShapeDtypeStruct((1, D_MODEL), F32)],
        [pltpu.VMEM((D_MODEL, D_MODEL), F32)],
        (dh1, mix, g2, w_out_t, attn, rec), carried, modes)


def _attn_bwd(qkv, dattn, sinks, bias, carried, modes):
    rows = qkv.shape[0]
    tm = _row_tile(rows)
    nbt, nt = tm // BLOCK, rows // tm

    def body(sink_ref, bias_ref, do_ref, q_ref, kp_ref, kc_ref, vp_ref, vc_ref, dq_ref, dkv_ref, dsink_ref, dk_c, dv_c):
        i = pl.program_id(0)

        @pl.when(i == 0)
        def _():
            dk_c[...] = jnp.zeros_like(dk_c)
            dv_c[...] = jnp.zeros_like(dv_c)
            dsink_ref[...] = jnp.zeros_like(dsink_ref)

        @pl.when(i < nt)
        def _():
            dk_late, dv_late = dk_c[...], dv_c[...]
            dsink_rows = [jnp.zeros((1, LANES), F32)] * ATTN_HEADS
            for b in range(nbt):
                blk = slice(b * BLOCK, (b + 1) * BLOCK)
                bias_t = _bias_of_block(bias_ref, i * nbt + b)
                dq_parts, dk_parts, dv_parts = [], [], []
                for kv in range(KV_HEADS):
                    k2 = _keys_of_block(kp_ref, kc_ref, b, kv)
                    v2 = _keys_of_block(vp_ref, vc_ref, b, kv)
                    q4 = _heads(q_ref, blk, kv * GQA_GROUP, GQA_GROUP)
                    do4 = _heads(do_ref, blk, kv * GQA_GROUP, GQA_GROUP)
                    pn, psink = _attn_probs(k2, q4, bias_t, _sink_row(sink_ref, kv))
                    dpn = _mm_nt(v2, do4)
                    delta = jnp.sum(pn * dpn, axis=0, keepdims=True)
                    ds = ((pn * (dpn - delta)) * (HEAD_DIM ** -0.5)).astype(BF16)
                    dqt = _mm_tn(k2, ds)
                    dq_parts += [dqt[:, g * BLOCK:(g + 1) * BLOCK] for g in range(GQA_GROUP)]
                    dk_parts.append(_mm(ds, q4))
                    dv_parts.append(_mm(pn.astype(BF16), do4))
                    sd = psink * delta
                    for g in range(GQA_GROUP):
                        h = kv * GQA_GROUP + g
                        dsink_rows[h] = dsink_rows[h] - jnp.sum(sd[:, g * BLOCK:(g + 1) * BLOCK])
                dq_ref[blk, :] = _from_head_major(dq_parts).astype(BF16)
                dk2 = jnp.concatenate(dk_parts, axis=1)
                dv2 = jnp.concatenate(dv_parts, axis=1)
                dkv_ref[blk, 0:KV_WIDTH] = (dk_late + dk2[0:BLOCK]).astype(BF16)
                dkv_ref[blk, KV_WIDTH:] = (dv_late + dv2[0:BLOCK]).astype(BF16)
                dk_late, dv_late = dk2[BLOCK:], dv2[BLOCK:]
            dk_c[...] = dk_late
            dv_c[...] = dv_late
            dsink_ref[...] += jnp.concatenate(dsink_rows, axis=0)

        @pl.when(i == nt)
        def _():
            dkv_ref[...] = jnp.zeros_like(dkv_ref)
            dkv_ref[0:BLOCK, 0:KV_WIDTH] = dk_c[...].astype(BF16)
            dkv_ref[0:BLOCK, KV_WIDTH:] = dv_c[...].astype(BF16)

    tile_of = lambda i: jnp.minimum(i, nt - 1)
    tile = pl.BlockSpec((tm, ATTN_WIDTH), lambda i: (tile_of(i), 0))
    return _hosting_call(
        body, "attn_bwd", nt + 1,
        [pl.BlockSpec(memory_space=pltpu.SMEM), _resident((N_BIAS, 2 * BLOCK, GQA_GROUP * BLOCK)), tile]
        + _attn_specs(tm, tile_of),
        [tile, pl.BlockSpec((tm, 2 * KV_WIDTH), lambda i: (i, 0)), _full((ATTN_HEADS, LANES))],
        [jax.ShapeDtypeStruct((rows, ATTN_WIDTH), BF16), jax.ShapeDtypeStruct((rows + tm, 2 * KV_WIDTH), BF16),
         jax.ShapeDtypeStruct((ATTN_HEADS, LANES), F32)],
        [pltpu.VMEM((BLOCK, KV_WIDTH), F32), pltpu.VMEM((BLOCK, KV_WIDTH), F32)],
        (sinks, bias, dattn, qkv, qkv, qkv, qkv, qkv), carried, modes)


ROW_CONV_B, ROW_B_A, ROW_B_X, ROW_LAMBDA = 4, 5, 6, 7


def _rec_bwd(drec, zrec, h, kept, conv_w, wa_bd, wx_bd, lam, carried, modes):
    rows = zrec.shape[0]
    tm = _rec_tile(rows)
    nt = rows // tm
    per = tm // SUBLANES

    def body(drec_ref, xr_ref, yr_ref, h_ref, xc_ref, a_ref, mult_ref, r_ref, ig_ref, hhalo_ref, cw_ref, wa_ref, wx_ref,
             lam_ref, drz_ref, small_ref, dwa_ref, dwx_ref, hbuf, abuf, u_s, g_s, dbuf, carry):
        s = pl.program_id(0)
        i = nt - 1 - s

        @pl.when(s == 0)
        def _():
            small_ref[...] = jnp.zeros_like(small_ref)
            dwa_ref[...] = jnp.zeros_like(dwa_ref)
            dwx_ref[...] = jnp.zeros_like(dwx_ref)
            carry[...] = jnp.zeros_like(carry)
            abuf[tm:tm + SUBLANES, :] = jnp.zeros((SUBLANES, LRU_WIDTH), F32)
            dbuf[tm:tm + SUBLANES, :] = jnp.zeros((SUBLANES, LRU_WIDTH), F32)

        hbuf[0:SUBLANES, :] = jnp.where(i == 0, 0.0, hhalo_ref[...])
        hbuf[SUBLANES:SUBLANES + tm, :] = h_ref[...]

        xc, a, mult, r, ig = xc_ref[...], a_ref[...], mult_ref[...], r_ref[...], ig_ref[...]
        halves = _lru_halves(xc)
        inv_mult = pl.reciprocal(mult, approx=True)

        yr = yr_ref[...]
        gel, t = _gelu(yr)
        drec_t = drec_ref[...]
        dyr = drec_t * h_ref[...] * _gelu_grad(yr, t)

        abuf[0:tm, :] = a
        u_s[...] = drec_t * gel
        a_next = abuf[pl.ds(1, tm), :]
        abuf[0:tm, :] = a_next
        carry[0:1, :] = _scan_tile(abuf, u_s, g_s, carry[0:1, :], tm, reverse=True)
        abuf[tm:tm + 1, :] = a[0:1, :]
        g = g_s[...]

        grow = i * tm + lax.broadcasted_iota(jnp.int32, (tm, LRU_WIDTH), 0)
        du = jnp.where(grow >= PAD_ROWS, g, 0.0)
        da = g * hbuf[pl.ds(SUBLANES - 1, tm), :]
        dmult = du * (ig * xc)
        dig = du * (mult * xc)
        dxc = du * (mult * ig)
        dlog_a = da * a - dmult * (a * a * inv_mult)
        sp = _softplus(-lam_ref[...])
        dgr = (dlog_a * (-LRU_C) * sp) * (r * (1.0 - r))
        dgi = dig * (ig * (1.0 - ig))
        dlam = jnp.sum(dlog_a * r, axis=0, keepdims=True) * (LRU_C * _sigmoid(-lam_ref[...]))
        dgr_b = [dgr[:, hh * LRU_HALF:(hh + 1) * LRU_HALF].astype(BF16) for hh in range(2)]
        dgi_b = [dgi[:, hh * LRU_HALF:(hh + 1) * LRU_HALF].astype(BF16) for hh in range(2)]
        dxc = dxc + jnp.concatenate(
            [_mm_nt(dgr_b[hh], wa_ref[hh]) + _mm_nt(dgi_b[hh], wx_ref[hh]) for hh in range(2)], axis=1)
        for hh in range(2):
            dwa_ref[hh] += _mm_tn(halves[hh], dgr_b[hh])
            dwx_ref[hh] += _mm_tn(halves[hh], dgi_b[hh])

        dbuf[0:tm, :] = dxc
        ahead = [dbuf[pl.ds(CONV_WIDTH - 1 - j, tm), :] for j in range(CONV_WIDTH)]
        dxr = sum(cw_ref[j:j + 1, :] * ahead[j] for j in range(CONV_WIDTH))
        dbuf[tm:tm + SUBLANES, :] = dxc[0:SUBLANES, :]
        drz_ref[:, 0:LRU_WIDTH] = dxr.astype(BF16)
        drz_ref[:, LRU_WIDTH:] = dyr.astype(BF16)

        xr = xr_ref[...]
        upd = [jnp.sum(xr * ahead[j], axis=0, keepdims=True) for j in range(CONV_WIDTH)]
        upd += [jnp.sum(dxc, axis=0, keepdims=True), jnp.sum(dgr, axis=0, keepdims=True),
                jnp.sum(dgi, axis=0, keepdims=True), dlam]
        small_ref[...] += jnp.concatenate(upd, axis=0)

    rev = lambda s: nt - 1 - s
    halo = lambda s: jnp.maximum(rev(s) * per - 1, 0)
    cols = lambda k: pl.BlockSpec((tm, LRU_WIDTH), lambda s: (rev(s), k))
    halo0 = pl.BlockSpec((SUBLANES, LRU_WIDTH), lambda s: (halo(s), 0))
    bd = _full((2, LRU_HALF, LRU_HALF))
    big = pltpu.VMEM((tm + SUBLANES, LRU_WIDTH), F32)
    tile = pltpu.VMEM((tm, LRU_WIDTH), F32)
    kept_cols = [cols(k) for k in (KEPT_XC, KEPT_A, KEPT_MULT, KEPT_R, KEPT_I)]
    return _hosting_call(
        body, "rec_bwd", nt,
        [cols(0), cols(0), cols(1), cols(0)] + kept_cols
        + [halo0, _full((CONV_WIDTH, LRU_WIDTH)), bd, bd, _full((1, LRU_WIDTH))],
        [pl.BlockSpec((tm, 2 * LRU_WIDTH), lambda s: (rev(s), 0)), _full((SUBLANES, LRU_WIDTH)), bd, bd],
        [jax.ShapeDtypeStruct((rows, 2 * LRU_WIDTH), BF16), jax.ShapeDtypeStruct((SUBLANES, LRU_WIDTH), F32),
         jax.ShapeDtypeStruct((2, LRU_HALF, LRU_HALF), F32), jax.ShapeDtypeStruct((2, LRU_HALF, LRU_HALF), F32)],
        [big, big, tile, tile, big, pltpu.VMEM((SUBLANES, LRU_WIDTH), F32)],
        (drec, zrec, zrec, h) + (kept,) * N_KEPT + (h, conv_w, wa_bd, wx_bd, lam), carried, modes)


DZ_CUTS = (0, ATTN_WIDTH, QKV_WIDTH, IN_WIDTH)


def _dz_specs(tm):
    return [pl.BlockSpec((tm, DZ_CUTS[p + 1] - DZ_CUTS[p]), lambda i: (i, 0)) for p in range(3)]


def _in_proj_bwd_x(head, x, g1, dh1, dq, dkv, drz, w_in_t, carried, modes):
    rows = dh1.shape[0]
    tm = _row_tile(rows)
    steps = rows // tm

    def body(head_ref, g_ref, dh1_ref, dq_ref, dkv_ref, drz_ref, w_ref, x_hbm, dh0_ref, dg_ref, buf, sem):
        i = pl.program_id(0)
        h0 = _h0_tile(head_ref, x_hbm, buf, sem, i, steps, tm)

        @pl.when(i == 0)
        def _():
            dg_ref[...] = jnp.zeros_like(dg_ref)

        g = g_ref[...]
        _, xhat, rstd = _rms_fwd(h0, g)
        parts = (dq_ref[...], dkv_ref[...], drz_ref[...])
        du = sum(_mm(parts[p], w_ref[DZ_CUTS[p]:DZ_CUTS[p + 1], :]) for p in range(3))
        dx, dg = _rms_bwd(du, xhat, rstd, g)
        dh0_ref[...] = dh1_ref[...] + dx
        dg_ref[...] += dg

    wide = pl.BlockSpec((tm, D_MODEL), lambda i: (i, 0))
    return _hosting_call(
        body, "in_proj_bwd_x", steps,
        [_full((BLOCK, D_MODEL)), _full((1, D_MODEL)), wide] + _dz_specs(tm) + [_resident((IN_WIDTH, D_MODEL)), ANY_SPACE],
        [wide, _full((1, D_MODEL))],
        [jax.ShapeDtypeStruct((rows, D_MODEL), F32), jax.ShapeDtypeStruct((1, D_MODEL), F32)],
        _frame_scratch(tm), (head, g1, dh1, dq, dkv, drz, w_in_t, x), carried, modes)


def _in_proj_bwd_w(u1, dq, dkv, drz, carried, modes):
    rows = u1.shape[0]
    tb = _big_tile(rows)

    def body(u_ref, dq_ref, dkv_ref, drz_ref, dw_ref):
        @pl.when(pl.program_id(0) == 0)
        def _():
            dw_ref[...] = jnp.zeros_like(dw_ref)

        u = u_ref[...]
        for p, ref in enumerate((dq_ref, dkv_ref, drz_ref)):
            dw_ref[:, DZ_CUTS[p]:DZ_CUTS[p + 1]] += _mm_tn(u, ref[...])

    return _hosting_call(
        body, "in_proj_bwd_w", rows // tb,
        [pl.BlockSpec((tb, D_MODEL), lambda i: (i, 0))] + _dz_specs(tb),
        [_full((D_MODEL, IN_WIDTH))],
        [jax.ShapeDtypeStruct((D_MODEL, IN_WIDTH), F32)],
        [], (u1, dq, dkv, drz), carried, modes)


def _adamw_math(w, m, v, g):
    nm = ADAM_B1 * m + (1.0 - ADAM_B1) * g
    nv = ADAM_B2 * v + (1.0 - ADAM_B2) * (g * g)
    m_hat = nm / (1.0 - ADAM_B1 ** ADAM_STEP)
    v_hat = nv / (1.0 - ADAM_B2 ** ADAM_STEP)
    return (-ADAM_LR) * (m_hat / (jnp.sqrt(v_hat) + ADAM_EPS) + ADAM_WD * w), nm, nv


SMALL_NAMES = ("conv_b", "b_a", "b_x", "lru_lambda", "attn_sinks", "g_post_mix", "g_pre_ffn", "g_post_ffn",
               "w_a", "w_x")
PACK_WIDTH = 1024


def _rows_view(a):
    return (a.size // PACK_WIDTH, PACK_WIDTH) if a.size >= PACK_WIDTH else (1, a.size)


def _pack_rows(vals):
    single = [name for name in SMALL_NAMES if _rows_view(vals[name])[0] == 1]
    assert len(single) == SUBLANES and SMALL_NAMES[:SUBLANES] == tuple(single)
    row = lax.broadcasted_iota(jnp.int32, (SUBLANES, PACK_WIDTH), 0)
    first = jnp.zeros((SUBLANES, PACK_WIDTH), F32)
    for k, name in enumerate(single):
        a = vals[name].reshape(1, -1)
        first = jnp.where(row == k, jnp.pad(a, ((0, 0), (0, PACK_WIDTH - a.shape[1]))), first)
    return jnp.concatenate([first] + [vals[name].reshape(_rows_view(vals[name])) for name in SMALL_NAMES[SUBLANES:]], axis=0)


def _adamw_small(weights, mom_m, mom_v, parts, loss_parts):
    n = len(SMALL_NAMES)
    views = [_rows_view(weights[name]) for name in SMALL_NAMES]

    def body(*refs):
        w_refs, m_refs, v_refs = refs[:n], refs[n:2 * n], refs[2 * n:3 * n]
        p_ref, l_ref, loss_ref = refs[3 * n], refs[3 * n + 1], refs[3 * n + 2]
        outs = refs[3 * n + 3:]
        row = 0
        for k, (nr, c) in enumerate(views):
            g = p_ref[0, row:row + nr, 0:c]
            for s in range(1, N_DEV):
                g = g + p_ref[s, row:row + nr, 0:c]
            g_ref, d_ref, nm_ref, nv_ref = outs[4 * k:4 * k + 4]
            g_ref[...] = g
            d_ref[...], nm_ref[...], nv_ref[...] = _adamw_math(w_refs[k][...], m_refs[k][...], v_refs[k][...], g)
            row += nr
        total = l_ref[0]
        for s in range(1, N_DEV):
            total = total + l_ref[s]
        loss_ref[...] = total

    args = [src[name].reshape(view) for src in (weights, mom_m, mom_v) for name, view in zip(SMALL_NAMES, views)]
    res = pl.pallas_call(
        body, name="adamw_small",
        out_shape=[jax.ShapeDtypeStruct(loss_parts.shape[1:], F32)]
                  + [jax.ShapeDtypeStruct(view, F32) for view in views for _ in range(4)],
        compiler_params=pltpu.CompilerParams(vmem_limit_bytes=VMEM_LIMIT),
    )(*args, parts, loss_parts)
    out = {name: tuple(t.reshape(weights[name].shape) for t in res[1 + 4 * k:5 + 4 * k]) for k, name in enumerate(SMALL_NAMES)}
    return res[0], out


def _adamw(w, m, v, parts, name):
    rows, cols = w.shape
    tr = next((t for t in (256, 128) if rows % t == 0), rows)
    parts = parts if isinstance(parts, (list, tuple)) else [parts]

    def body(w_ref, m_ref, v_ref, *refs):
        p_refs, (g_ref, d_ref, nm_ref, nv_ref) = refs[:len(parts)], refs[len(parts):]

        def total(p_ref):
            g = p_ref[0].astype(F32)
            for s in range(1, N_DEV):
                g = g + p_ref[s].astype(F32)
            return g

        g = jnp.concatenate([total(p_ref) for p_ref in p_refs], axis=1) if len(parts) > 1 else total(p_refs[0])
        g_ref[...] = g
        d_ref[...], nm_ref[...], nv_ref[...] = _adamw_math(w_ref[...], m_ref[...], v_ref[...], g)

    blk = pl.BlockSpec((tr, cols), lambda i: (i, 0))
    return pl.pallas_call(
        body, name=name, grid=(rows // tr,),
        in_specs=[blk, blk, blk] + [pl.BlockSpec((N_DEV, tr, p.shape[2]), lambda i: (0, i, 0)) for p in parts],
        out_specs=[blk] * 4,
        out_shape=[jax.ShapeDtypeStruct((rows, cols), F32)] * 4,
        compiler_params=_params(("parallel",)),
    )(w, m, v, *parts)


def _cols_from_shards(g):
    return jnp.transpose(g, (1, 0, 2)).reshape(g.shape[1], N_DEV * g.shape[2])


def _cols_to_shards(a):
    r, c = a.shape
    return jnp.transpose(a.reshape(r, N_DEV, c // N_DEV), (1, 0, 2))


def _block_diag(w):
    per = LRU_HALF // LRU_BLOCK
    w = w.reshape(2, per, LRU_BLOCK, LRU_BLOCK)
    eye = jnp.eye(per, dtype=w.dtype)
    return (w[:, :, :, None, :] * eye[None, :, None, :, None]).reshape(2, LRU_HALF, LRU_HALF)


def _block_diag_extract(t):
    per = LRU_HALF // LRU_BLOCK
    t = t.reshape(2, per, LRU_BLOCK, per, LRU_BLOCK)
    return jnp.stack([t[:, b, :, b, :] for b in range(per)], axis=1).reshape(LRU_BLOCKS, LRU_BLOCK, LRU_BLOCK)


def kernel(x, meta_tokens, g_pre_mix, w_in, conv_w, conv_b, w_a, b_a, w_x, b_x, lru_lambda, attn_sinks, w_out, g_post_mix, g_pre_ffn, w_ff1, w_ff2, g_post_ffn, loss_target, m_meta_tokens, m_g_pre_mix, m_w_in, m_conv_w, m_conv_b, m_w_a, m_b_a, m_w_x, m_b_x, m_lru_lambda, m_attn_sinks, m_w_out, m_g_post_mix, m_g_pre_ffn, m_w_ff1, m_w_ff2, m_g_post_ffn, v_meta_tokens, v_g_pre_mix, v_w_in, v_conv_w, v_conv_b, v_w_a, v_b_a, v_w_x, v_b_x, v_lru_lambda, v_attn_sinks, v_w_out, v_g_post_mix, v_g_pre_ffn, v_w_ff1, v_w_ff2, v_g_post_ffn):
    weights = dict(meta_tokens=meta_tokens, g_pre_mix=g_pre_mix, w_in=w_in, conv_w=conv_w, conv_b=conv_b, w_a=w_a,
                   b_a=b_a, w_x=w_x, b_x=b_x, lru_lambda=lru_lambda, attn_sinks=attn_sinks, w_out=w_out,
                   g_post_mix=g_post_mix, g_pre_ffn=g_pre_ffn, w_ff1=w_ff1, w_ff2=w_ff2, g_post_ffn=g_post_ffn)
    mom_m = dict(meta_tokens=m_meta_tokens, g_pre_mix=m_g_pre_mix, w_in=m_w_in, conv_w=m_conv_w, conv_b=m_conv_b,
                 w_a=m_w_a, b_a=m_b_a, w_x=m_w_x, b_x=m_b_x, lru_lambda=m_lru_lambda, attn_sinks=m_attn_sinks,
                 w_out=m_w_out, g_post_mix=m_g_post_mix, g_pre_ffn=m_g_pre_ffn, w_ff1=m_w_ff1, w_ff2=m_w_ff2,
                 g_post_ffn=m_g_post_ffn)
    mom_v = dict(meta_tokens=v_meta_tokens, g_pre_mix=v_g_pre_mix, w_in=v_w_in, conv_w=v_conv_w, conv_b=v_conv_b,
                 w_a=v_w_a, b_a=v_b_a, w_x=v_w_x, b_x=v_b_x, lru_lambda=v_lru_lambda, attn_sinks=v_attn_sinks,
                 w_out=v_w_out, g_post_mix=v_g_post_mix, g_pre_ffn=v_g_pre_ffn, w_ff1=v_w_ff1, w_ff2=v_w_ff2,
                 g_post_ffn=v_g_post_ffn)
    order = list(weights)

    (g_win, g_meta, g_cw) = _gather_two_level([w_in[0].astype(BF16), meta_tokens, conv_w[0]], "gather_first")
    w_in_full = _cols_from_shards(g_win)
    meta_full = _cols_from_shards(g_meta)
    conv_w_full = _cols_from_shards(g_cw)

    head = jnp.concatenate([jnp.zeros((PAD_ROWS, D_MODEL), F32), meta_full], axis=0)
    wa_bd = _block_diag(w_a[0]).astype(BF16)
    wx_bd = _block_diag(w_x[0]).astype(BF16)
    bias = _attn_bias()

    w1_shard = w_ff1[0].astype(BF16)
    (qkv, zrec, u1), (g_wout,) = _in_proj_fwd(head, x[0], g_pre_mix, w_in_full, [w_out[0].astype(BF16)], ["gather"])
    (attn,), (w1a,) = _attn_fwd(qkv, attn_sinks, bias, [w1_shard[:, :FF_HALF]], ["gather"])
    (rec, h_lru, kept), (w1b,) = _rec_fwd(zrec, conv_w_full, conv_b, wa_bd, b_a, wx_bd, b_x, lru_lambda,
                                         [w1_shard[:, FF_HALF:]], ["gather"])
    w_out_full = g_wout.reshape(D_MODEL, D_MODEL)
    w2_shard = w_ff2[0].astype(BF16)
    (mix, h1), (w2a,) = _out_proj_fwd(attn, rec, w_out_full, head, x[0], g_post_mix, [w2_shard[:FF_HALF]], ["gather"])
    (act, u2), (w2b,) = _ffn_up(h1, g_pre_ffn, (w1a, w1b), [w2_shard[FF_HALF:]], ["gather"])
    w2_halves = [w.reshape(D_FF // 2, D_MODEL) for w in (w2a, w2b)]
    dy, df, dg_post_ffn, loss_acc = _ffn_down_loss(act, w2_halves, h1, loss_target[0], g_post_ffn)

    da1 = _ffn_bwd_act(df, [jnp.transpose(w, (0, 2, 1)) for w in (w2a, w2b)], act)
    dw1h, dw2g = _ffn_bwd_weights(u2, da1, act, df)
    w1t = jnp.concatenate([jnp.transpose(w, (0, 2, 1)).reshape(D_FF // 2, D_MODEL) for w in (w1a, w1b)], axis=0)
    (dh1, dg_pre_ffn), (p_w1a,) = _ffn_bwd_x(da1, w1t, h1, dy, g_pre_ffn, [dw1h[0]], ["scatter"])
    (dattn, drec, dw_out, dg_post_mix), (p_w1b,) = _out_proj_bwd(dh1, mix, g_post_mix, w_out_full.T, attn, rec,
                                                                [dw1h[1]], ["scatter"])
    (dq, dkv_late, dsinks), (p_w2,) = _attn_bwd(qkv, dattn, attn_sinks, bias, [dw2g], ["scatter"])
    dkv = dkv_late[BLOCK:BLOCK + qkv.shape[0]]
    (drz, rec_small, dwa_bd, dwx_bd), (p_wout,) = _rec_bwd(
        drec, zrec, h_lru, kept, conv_w_full, wa_bd, wx_bd, lru_lambda,
        [dw_out.reshape(N_DEV, D_MODEL // N_DEV, D_MODEL)], ["scatter"])
    small_grads = dict(
        conv_b=rec_small[ROW_CONV_B], w_a=_block_diag_extract(dwa_bd), b_a=rec_small[ROW_B_A],
        w_x=_block_diag_extract(dwx_bd), b_x=rec_small[ROW_B_X], lru_lambda=rec_small[ROW_LAMBDA],
        attn_sinks=dsinks[:, 0], g_post_mix=dg_post_mix, g_pre_ffn=dg_pre_ffn, g_post_ffn=dg_post_ffn)
    (dw_in,), (p_cw, p_small) = _in_proj_bwd_w(
        u1, dq, dkv, drz, [_cols_to_shards(rec_small[0:CONV_WIDTH]), _pack_rows(small_grads)], ["scatter", "gather"])
    (dh0, dg_pre_mix), (p_win,) = _in_proj_bwd_x(
        head, x[0], g_pre_mix, dh1, dq, dkv, drz, w_in_full.T, [_cols_to_shards(dw_in).astype(BF16)], ["scatter"])
    p_meta, p_gpm, p_loss = _exchange([_cols_to_shards(dh0[PAD_ROWS:BLOCK]), dg_pre_mix, loss_acc],
                                      ["scatter", "gather", "gather"], "exchange_last")

    res = {}
    res["g_pre_mix"] = _adamw(g_pre_mix, m_g_pre_mix, v_g_pre_mix, p_gpm, "adamw_g_pre_mix")
    res["w_in"] = _adamw(w_in[0], m_w_in[0], v_w_in[0], p_win, "adamw_w_in")
    res["w_out"] = _adamw(w_out[0], m_w_out[0], v_w_out[0], p_wout, "adamw_w_out")
    res["w_ff1"] = _adamw(w_ff1[0], m_w_ff1[0], v_w_ff1[0], [p_w1a, p_w1b], "adamw_w_ff1")
    res["w_ff2"] = _adamw(w_ff2[0], m_w_ff2[0], v_w_ff2[0], p_w2, "adamw_w_ff2")
    res["meta_tokens"] = _adamw(meta_tokens, m_meta_tokens, v_meta_tokens, p_meta, "adamw_meta")
    res["conv_w"] = _adamw(conv_w[0], m_conv_w[0], v_conv_w[0], p_cw, "adamw_conv_w")
    for name in ("w_in", "w_out", "w_ff1", "w_ff2", "conv_w"):
        res[name] = tuple(t[None] for t in res[name])
    loss_total, small = _adamw_small(weights, mom_m, mom_v, p_small, p_loss)
    res.update(small)

    grad_x = dh0[BLOCK:][None]
    outs = [loss_total[0, 0], grad_x]
    for k in range(4):
        outs += [res[name][k] for name in order]
    return tuple(outs)
```

```python
import jax
import jax.numpy as jnp
import numpy as np
from jax import lax
from jax.experimental import pallas as pl
from jax.experimental.pallas import tpu as pltpu

F32 = jnp.float32
BF16 = jnp.bfloat16

D_MODEL = 1024
N_META = 16
HEAD_DIM = 64
ATTN_HEADS = 8
KV_HEADS = 2
GQA_GROUP = ATTN_HEADS // KV_HEADS
ATTN_WIDTH = ATTN_HEADS * HEAD_DIM
KV_WIDTH = KV_HEADS * HEAD_DIM
QKV_WIDTH = ATTN_WIDTH + 2 * KV_WIDTH
LRU_WIDTH = 512
LRU_BLOCKS = 8
LRU_BLOCK = 64
LRU_HALF = 256
LRU_C = 8.0
CONV_WIDTH = 4
BLOCK = 128
PAD_ROWS = BLOCK - N_META
IN_WIDTH = QKV_WIDTH + 2 * LRU_WIDTH
D_FF = 4096
EPS = 1e-6
NEG = -1e30
N_DEV = 8
FF_CHUNK = D_FF // N_DEV
SUBLANES = 8
LANES = 128

ADAM_LR = 0.001
ADAM_B1 = 0.9
ADAM_B2 = 0.999
ADAM_EPS = 1e-08
ADAM_WD = 0.01
ADAM_STEP = 10

VMEM_LIMIT = 56 * 1024 * 1024


def _row_tile(rows):
    for t in (640, 512, 256, 128):
        if rows % t == 0:
            return t
    raise ValueError(rows)


def _big_tile(rows):
    for t in (1664, 1024, 512, 256, 128):
        if rows % t == 0:
            return t
    raise ValueError(rows)


def _rec_tile(rows):
    for t in (320, 256, 128):
        if rows % t == 0:
            return t
    raise ValueError(rows)


def _params(semantics):
    return pltpu.CompilerParams(dimension_semantics=semantics, vmem_limit_bytes=VMEM_LIMIT)


def _mm(a, b):
    return lax.dot_general(a, b, (((1,), (0,)), ((), ())), preferred_element_type=F32)


def _mm_nt(a, b):
    return lax.dot_general(a, b, (((1,), (1,)), ((), ())), preferred_element_type=F32)


def _mm_tn(a, b):
    return lax.dot_general(a, b, (((0,), (0,)), ((), ())), preferred_element_type=F32)


def _rms_fwd(x, g):
    rstd = lax.rsqrt(jnp.mean(x * x, axis=-1, keepdims=True) + EPS)
    xhat = x * rstd
    return xhat * g, xhat, rstd


def _rms_bwd(dy, xhat, rstd, g):
    dyg = dy * g
    c = jnp.mean(dyg * xhat, axis=-1, keepdims=True)
    dx = rstd * (dyg - xhat * c)
    dg = jnp.sum(dy * xhat, axis=0, keepdims=True)
    return dx, dg


def _sigmoid(x):
    return 0.5 * jnp.tanh(0.5 * x) + 0.5


def _log1p(x):
    u = 1.0 + x
    return jnp.where(u == 1.0, x, jnp.log(u) * x / (u - 1.0))


def _one_minus_sq_exp(x, ex):
    return -jnp.tanh(x) * (1.0 + ex * ex)


TINY = 1e-30


def _sqrt_pos(y):
    r = lax.rsqrt(jnp.maximum(y, TINY))
    return y * r, r


def _softplus(x):
    return jnp.maximum(x, 0.0) + _log1p(jnp.exp(-jnp.abs(x)))


GELU_C = 0.7978845608028654
GELU_K = 0.044715


def _gelu(x):
    t = jnp.tanh(GELU_C * (x + GELU_K * x * x * x))
    return 0.5 * x * (1.0 + t), t


def _gelu_grad(x, t):
    return 0.5 * (1.0 + t) + 0.5 * x * (1.0 - t * t) * GELU_C * (1.0 + 3.0 * GELU_K * x * x)


def _full(shape):
    return pl.BlockSpec(shape, lambda *_: (0,) * len(shape))


def _resident(shape):
    return pl.BlockSpec(shape, lambda *_: (0,) * len(shape), pipeline_mode=pl.Buffered(1))


def _exchange_copies(ins, outs, sems, modes):
    send_sems, recv_sems, local_sems = sems
    x, y, c = lax.axis_index("x"), lax.axis_index("y"), lax.axis_index("c")
    me = 4 * x + 2 * y + c

    def block(a, dev):
        return ins[a] if modes[a] == "gather" else ins[a].at[dev]

    local = [pltpu.make_async_copy(block(a, me), outs[a].at[me], local_sems.at[a]) for a in range(len(ins))]
    sends, recvs = [], []
    for a in range(len(ins)):
        for k in range(N_DEV - 1):
            bits = k + 1
            px = jnp.bitwise_xor(x, (bits >> 2) & 1)
            py = jnp.bitwise_xor(y, (bits >> 1) & 1)
            pc = jnp.bitwise_xor(c, bits & 1)
            peer = 4 * px + 2 * py + pc
            common = dict(src_ref=block(a, peer), send_sem=send_sems.at[a, k], recv_sem=recv_sems.at[a, k],
                          device_id=(px, py, pc), device_id_type=pl.DeviceIdType.MESH)
            sends.append(pltpu.make_async_remote_copy(dst_ref=outs[a].at[me], **common))
            recvs.append(pltpu.make_async_remote_copy(dst_ref=outs[a].at[peer], **common))
    return local, sends, recvs


def _exchange_start(ins, outs, sems, modes):
    local, sends, _ = _exchange_copies(ins, outs, sems, modes)
    for cp in local + sends:
        cp.start()


def _exchange_wait(ins, outs, sems, modes):
    local, sends, recvs = _exchange_copies(ins, outs, sems, modes)
    for cp in recvs:
        cp.wait_recv()
    for cp in sends:
        cp.wait_send()
    for cp in local:
        cp.wait()


def _exchange_shapes(arrays, modes):
    return [jax.ShapeDtypeStruct((N_DEV,) + a.shape if mode == "gather" else a.shape, a.dtype)
            for a, mode in zip(arrays, modes)]


def _exchange_sems(na):
    return [pltpu.SemaphoreType.DMA((na, N_DEV - 1)), pltpu.SemaphoreType.DMA((na, N_DEV - 1)),
            pltpu.SemaphoreType.DMA((na,))]


ANY_SPACE = pl.BlockSpec(memory_space=pl.ANY)


def _exchange(arrays, modes, name):
    na = len(arrays)

    def body(*refs):
        ins, outs, sems = refs[:na], refs[na:2 * na], refs[2 * na:]
        _exchange_start(ins, outs, sems, modes)
        _exchange_wait(ins, outs, sems, modes)

    return pl.pallas_call(
        body, name=name, out_shape=_exchange_shapes(arrays, modes),
        in_specs=[ANY_SPACE] * na, out_specs=[ANY_SPACE] * na, scratch_shapes=_exchange_sems(na),
        compiler_params=pltpu.CompilerParams(has_side_effects=True),
    )(*arrays)


def _gather_two_level(arrays, name):
    na = len(arrays)

    def body(*refs):
        ins, outs = refs[:na], refs[na:2 * na]
        send_sems, recv_sems, local_sems = refs[2 * na:]
        x, y, c = lax.axis_index("x"), lax.axis_index("y"), lax.axis_index("c")
        me, sibling = (x, y, c), (x, y, 1 - c)
        chips = [(1 - x, y), (x, 1 - y), (1 - x, 1 - y)]

        def copy(a, k, block, to, src=None):
            slot = outs[a].at[4 * block[0] + 2 * block[1] + block[2]]
            return pltpu.make_async_remote_copy(
                src_ref=slot if src is None else src, dst_ref=slot, send_sem=send_sems.at[a, k],
                recv_sem=recv_sems.at[a, k], device_id=to, device_id_type=pl.DeviceIdType.MESH)

        local = [pltpu.make_async_copy(ins[a], outs[a].at[4 * x + 2 * y + c], local_sems.at[a]) for a in range(na)]
        first = []
        for a in range(na):
            first.append(copy(a, 0, me, sibling, src=ins[a]))
            first += [copy(a, 1 + j, me, (*chip, c), src=ins[a]) for j, chip in enumerate(chips)]
        for cp in local + first:
            cp.start()
        passed = []
        for j, chip in enumerate(chips):
            for a in range(na):
                copy(a, 1 + j, (*chip, c), me).wait_recv()
                passed.append(copy(a, 4 + j, (*chip, c), sibling))
                passed[-1].start()
        for a in range(na):
            copy(a, 0, sibling, me).wait_recv()
            for j, chip in enumerate(chips):
                copy(a, 4 + j, (*chip, 1 - c), me).wait_recv()
        for cp in first + passed:
            cp.wait_send()
        for cp in local:
            cp.wait()

    return pl.pallas_call(
        body, name=name, out_shape=_exchange_shapes(arrays, ["gather"] * na),
        in_specs=[ANY_SPACE] * na, out_specs=[ANY_SPACE] * na, scratch_shapes=_exchange_sems(na),
        compiler_params=pltpu.CompilerParams(has_side_effects=True),
    )(*arrays)


def _hosting_call(body, name, steps, in_specs, out_specs, out_shape, scratch_shapes, args, arrays, modes):
    n_in, n_out, n_scr, na = len(in_specs), len(out_specs), len(scratch_shapes), len(arrays)

    def hosting_body(*refs):
        cuts = [0]
        for n in (n_in, na, n_out, na, n_scr, 3):
            cuts.append(cuts[-1] + n)
        ins, x_ins, outs, x_outs, scr, sems = (refs[cuts[p]:cuts[p + 1]] for p in range(6))
        step = pl.program_id(0)

        @pl.when(step == 0)
        def _():
            _exchange_start(x_ins, x_outs, sems, modes)

        body(*ins, *outs, *scr)

        @pl.when(step == steps - 1)
        def _():
            _exchange_wait(x_ins, x_outs, sems, modes)

    res = pl.pallas_call(
        hosting_body, name=name, grid=(steps,),
        in_specs=list(in_specs) + [ANY_SPACE] * na, out_specs=list(out_specs) + [ANY_SPACE] * na,
        out_shape=list(out_shape) + _exchange_shapes(arrays, modes),
        scratch_shapes=list(scratch_shapes) + _exchange_sems(na),
        compiler_params=_params(("arbitrary",)),
    )(*args, *arrays)
    return res[:n_out], res[n_out:]


def _frame_rows(src_hbm, buf, sem, i, steps, tm):
    def first():
        return pltpu.make_async_copy(src_hbm.at[pl.ds(0, tm - BLOCK)], buf.at[0, pl.ds(BLOCK, tm - BLOCK)], sem.at[0])

    def later(t, slot):
        return pltpu.make_async_copy(src_hbm.at[pl.ds(pl.multiple_of(t * tm - BLOCK, BLOCK), tm)], buf.at[slot], sem.at[slot])

    slot = i % 2

    @pl.when(i == 0)
    def _():
        first().start()

    @pl.when(i + 1 < steps)
    def _():
        later(i + 1, 1 - slot).start()

    @pl.when(i == 0)
    def _():
        first().wait()

    @pl.when(i > 0)
    def _():
        later(i, slot).wait()

    return slot


def _frame_scratch(tm):
    return [pltpu.VMEM((2, tm, D_MODEL), F32), pltpu.SemaphoreType.DMA((2,))]


def _h0_tile(head_ref, x_hbm, buf, sem, i, steps, tm):
    slot = _frame_rows(x_hbm, buf, sem, i, steps, tm)

    @pl.when(i == 0)
    def _():
        buf[0, 0:BLOCK, :] = head_ref[...]

    return buf[slot]


def _in_proj_fwd(head, x, g1, w_in, carried, modes):
    rows = BLOCK + x.shape[0]
    tm = _row_tile(rows)
    steps = rows // tm

    def body(head_ref, g_ref, w_ref, x_hbm, qkv_ref, zrec_ref, u_ref, buf, sem):
        h = _h0_tile(head_ref, x_hbm, buf, sem, pl.program_id(0), steps, tm)
        u, _, _ = _rms_fwd(h, g_ref[...])
        u = u.astype(BF16)
        u_ref[...] = u
        z = _mm(u, w_ref[...])
        qkv_ref[...] = z[:, :QKV_WIDTH].astype(BF16)
        zrec_ref[...] = z[:, QKV_WIDTH:]

    wide = pl.BlockSpec((tm, D_MODEL), lambda i: (i, 0))
    return _hosting_call(
        body, "in_proj_fwd", steps,
        [_full((BLOCK, D_MODEL)), _full((1, D_MODEL)), _resident((D_MODEL, IN_WIDTH)), ANY_SPACE],
        [pl.BlockSpec((tm, QKV_WIDTH), lambda i: (i, 0)), pl.BlockSpec((tm, 2 * LRU_WIDTH), lambda i: (i, 0)), wide],
        [jax.ShapeDtypeStruct((rows, QKV_WIDTH), BF16), jax.ShapeDtypeStruct((rows, 2 * LRU_WIDTH), F32),
         jax.ShapeDtypeStruct((rows, D_MODEL), BF16)],
        _frame_scratch(tm), (head, g1, w_in, x), carried, modes)


N_BIAS = 3


def _attn_bias():
    key = np.arange(2 * BLOCK)[:, None]
    r = np.arange(GQA_GROUP * BLOCK)[None, :] % BLOCK
    band = (key > r) & (key <= r + BLOCK)
    out = [np.where(band & ((n - 1) * BLOCK + key >= PAD_ROWS), 0.0, NEG) for n in range(N_BIAS)]
    return jnp.asarray(np.stack(out), F32)


def _attn_probs(k2, q4, bias, sink_row):
    s = _mm_nt(k2, q4) * (HEAD_DIM ** -0.5) + bias
    m = jnp.maximum(jnp.max(s, axis=0, keepdims=True), sink_row)
    p = jnp.exp(s - m)
    es = jnp.exp(sink_row - m)
    inv = 1.0 / (jnp.sum(p, axis=0, keepdims=True) + es)
    return p * inv, es * inv


def _heads(ref, rows, first, count):
    return jnp.concatenate([ref[rows, (first + g) * HEAD_DIM:(first + g + 1) * HEAD_DIM] for g in range(count)], axis=0)


def _keys_of_block(prev_ref, cur_ref, b, kv):
    sl = slice(kv * HEAD_DIM, (kv + 1) * HEAD_DIM)
    before = prev_ref[:, sl] if b == 0 else cur_ref[(b - 1) * BLOCK:b * BLOCK, sl]
    return jnp.concatenate([before, cur_ref[b * BLOCK:(b + 1) * BLOCK, sl]], axis=0)


def _bias_of_block(bias_ref, block):
    return bias_ref[jnp.minimum(block, N_BIAS - 1)]


def _sink_row(sink_ref, kv):
    g = lax.broadcasted_iota(jnp.int32, (1, GQA_GROUP * BLOCK), 1) // BLOCK
    row = jnp.full((1, GQA_GROUP * BLOCK), sink_ref[0, kv * GQA_GROUP], F32)
    for i in range(1, GQA_GROUP):
        row = jnp.where(g == i, sink_ref[0, kv * GQA_GROUP + i], row)
    return row


def _from_head_major(pieces):
    return jnp.concatenate(pieces, axis=0).T


def _attn_specs(tm, tile_of):
    nbt = tm // BLOCK
    k_col, v_col = ATTN_WIDTH // KV_WIDTH, ATTN_WIDTH // KV_WIDTH + 1
    before = lambda i: jnp.maximum(tile_of(i) * nbt - 1, 0)
    return [pl.BlockSpec((tm, ATTN_WIDTH), lambda i: (tile_of(i), 0)),
            pl.BlockSpec((BLOCK, KV_WIDTH), lambda i: (before(i), k_col)),
            pl.BlockSpec((tm, KV_WIDTH), lambda i: (tile_of(i), k_col)),
            pl.BlockSpec((BLOCK, KV_WIDTH), lambda i: (before(i), v_col)),
            pl.BlockSpec((tm, KV_WIDTH), lambda i: (tile_of(i), v_col))]


def _attn_fwd(qkv, sinks, bias, carried, modes):
    rows = qkv.shape[0]
    tm = _row_tile(rows)
    nbt = tm // BLOCK

    def body(sink_ref, bias_ref, q_ref, kp_ref, kc_ref, vp_ref, vc_ref, o_ref):
        i = pl.program_id(0)
        for b in range(nbt):
            blk = slice(b * BLOCK, (b + 1) * BLOCK)
            bias_t = _bias_of_block(bias_ref, i * nbt + b)
            pieces = []
            for kv in range(KV_HEADS):
                k2 = _keys_of_block(kp_ref, kc_ref, b, kv)
                v2 = _keys_of_block(vp_ref, vc_ref, b, kv)
                q4 = _heads(q_ref, blk, kv * GQA_GROUP, GQA_GROUP)
                pn, _ = _attn_probs(k2, q4, bias_t, _sink_row(sink_ref, kv))
                ot = _mm_tn(v2, pn.astype(BF16))
                pieces += [ot[:, g * BLOCK:(g + 1) * BLOCK] for g in range(GQA_GROUP)]
            o_ref[blk, :] = _from_head_major(pieces).astype(BF16)

    return _hosting_call(
        body, "attn_fwd", rows // tm,
        [pl.BlockSpec(memory_space=pltpu.SMEM), _resident((N_BIAS, 2 * BLOCK, GQA_GROUP * BLOCK))]
        + _attn_specs(tm, lambda i: i),
        [pl.BlockSpec((tm, ATTN_WIDTH), lambda i: (i, 0))],
        [jax.ShapeDtypeStruct((rows, ATTN_WIDTH), BF16)],
        [], (sinks, bias, qkv, qkv, qkv, qkv, qkv), carried, modes)


def _conv_taps(xbuf, tm):
    return [xbuf[pl.ds(SUBLANES - (CONV_WIDTH - 1 - j), tm), :] for j in range(CONV_WIDTH)]


def _lru_halves(xc):
    return [xc[:, h * LRU_HALF:(h + 1) * LRU_HALF].astype(BF16) for h in range(2)]


def _lru_gates(xc, wa_ref, ba_ref, wx_ref, bx_ref, lam_ref):
    halves = _lru_halves(xc)
    gate_r = jnp.concatenate([_mm(halves[h], wa_ref[h]) for h in range(2)], axis=1) + ba_ref[...]
    gate_i = jnp.concatenate([_mm(halves[h], wx_ref[h]) for h in range(2)], axis=1) + bx_ref[...]
    r = _sigmoid(gate_r)
    ig = _sigmoid(gate_i)
    log_a = (-LRU_C) * r * _softplus(-lam_ref[...])
    a = jnp.exp(log_a)
    mult, _ = _sqrt_pos(_one_minus_sq_exp(log_a, a))
    return r, ig, a, mult


KEPT_XC, KEPT_A, KEPT_MULT, KEPT_R, KEPT_I, N_KEPT = 0, 1, 2, 3, 4, 5


def _scan_tile(a_ref, u_ref, out_ref, carry, tm, reverse):
    row = lax.broadcasted_iota(jnp.int32, (SUBLANES, LRU_WIDTH), 0)
    groups = tm // SUBLANES

    def step(j, prev):
        jj = groups - 1 - j if reverse else j
        o = pl.multiple_of(jj * SUBLANES, SUBLANES)
        a = a_ref[pl.ds(o, SUBLANES), :]
        u = u_ref[pl.ds(o, SUBLANES), :]
        for s in (1, 2, 4):
            shift = SUBLANES - s if reverse else s
            keep = (row < SUBLANES - s) if reverse else (row >= s)
            u = jnp.where(keep, a * pltpu.roll(u, shift, 0) + u, u)
            a = jnp.where(keep, a * pltpu.roll(a, shift, 0), a)
        out = a * prev + u
        out_ref[pl.ds(o, SUBLANES), :] = out
        return out[0:1, :] if reverse else out[SUBLANES - 1:SUBLANES, :]

    return lax.fori_loop(0, groups, step, carry)


def _rec_fwd(zrec, conv_w, conv_b, wa_bd, b_a, wx_bd, b_x, lam, carried, modes):
    rows = zrec.shape[0]
    tm = _row_tile(rows)

    def body(xr_ref, yr_ref, cw_ref, cb_ref, wa_ref, ba_ref, wx_ref, bx_ref, lam_ref, rec_ref, h_ref, kept_ref,
             xbuf, a_s, u_s, carry):
        i = pl.program_id(0)

        @pl.when(i == 0)
        def _():
            xbuf[0:SUBLANES, :] = jnp.zeros((SUBLANES, LRU_WIDTH), F32)
            carry[...] = jnp.zeros_like(carry)

        @pl.when(i > 0)
        def _():
            xbuf[0:SUBLANES, :] = xbuf[tm:tm + SUBLANES, :]

        xbuf[SUBLANES:SUBLANES + tm, :] = xr_ref[...]
        taps = _conv_taps(xbuf, tm)
        xc = cb_ref[...] + sum(cw_ref[j:j + 1, :] * taps[j] for j in range(CONV_WIDTH))
        r, ig, a, mult = _lru_gates(xc, wa_ref, ba_ref, wx_ref, bx_ref, lam_ref)
        for k, val in ((KEPT_XC, xc), (KEPT_A, a), (KEPT_MULT, mult), (KEPT_R, r), (KEPT_I, ig)):
            kept_ref[:, k * LRU_WIDTH:(k + 1) * LRU_WIDTH] = val
        grow = i * tm + lax.broadcasted_iota(jnp.int32, (tm, LRU_WIDTH), 0)
        a_s[...] = a
        u_s[...] = jnp.where(grow >= PAD_ROWS, mult * (ig * xc), 0.0)
        carry[0:1, :] = _scan_tile(a_s, u_s, h_ref, carry[0:1, :], tm, reverse=False)
        gel, _ = _gelu(yr_ref[...])
        rec_ref[...] = (gel * h_ref[...]).astype(BF16)

    vec = _full((1, LRU_WIDTH))
    bd = _full((2, LRU_HALF, LRU_HALF))
    return _hosting_call(
        body, "rec_fwd", rows // tm,
        [pl.BlockSpec((tm, LRU_WIDTH), lambda i: (i, 0)), pl.BlockSpec((tm, LRU_WIDTH), lambda i: (i, 1)),
         _full((CONV_WIDTH, LRU_WIDTH)), vec, bd, vec, bd, vec, vec],
        [pl.BlockSpec((tm, LRU_WIDTH), lambda i: (i, 0))] * 2 + [pl.BlockSpec((tm, N_KEPT * LRU_WIDTH), lambda i: (i, 0))],
        [jax.ShapeDtypeStruct((rows, LRU_WIDTH), BF16), jax.ShapeDtypeStruct((rows, LRU_WIDTH), F32),
         jax.ShapeDtypeStruct((rows, N_KEPT * LRU_WIDTH), F32)],
        [pltpu.VMEM((tm + SUBLANES, LRU_WIDTH), F32), pltpu.VMEM((tm, LRU_WIDTH), F32),
         pltpu.VMEM((tm, LRU_WIDTH), F32), pltpu.VMEM((SUBLANES, LRU_WIDTH), F32)],
        (zrec, zrec, conv_w, conv_b, wa_bd, b_a, wx_bd, b_x, lam), carried, modes)


def _out_proj_fwd(attn, rec, w_out, head, x, g2, carried, modes):
    rows = attn.shape[0]
    tm = _row_tile(rows)
    steps = rows // tm

    def body(attn_ref, rec_ref, w_ref, head_ref, g_ref, x_hbm, mix_ref, h1_ref, buf, sem):
        h0 = _h0_tile(head_ref, x_hbm, buf, sem, pl.program_id(0), steps, tm)
        mix = _mm(attn_ref[...], w_ref[0:ATTN_WIDTH, :]) + _mm(rec_ref[...], w_ref[ATTN_WIDTH:, :])
        y, _, _ = _rms_fwd(mix, g_ref[...])
        mix_ref[...] = mix
        h1_ref[...] = h0 + y

    half = pl.BlockSpec((tm, ATTN_WIDTH), lambda i: (i, 0))
    wide = pl.BlockSpec((tm, D_MODEL), lambda i: (i, 0))
    return _hosting_call(
        body, "out_proj_fwd", steps,
        [half, half, _resident((D_MODEL, D_MODEL)), _full((BLOCK, D_MODEL)), _full((1, D_MODEL)), ANY_SPACE],
        [wide, wide],
        [jax.ShapeDtypeStruct((rows, D_MODEL), F32)] * 2,
        _frame_scratch(tm), (attn, rec, w_out, head, g2, x), carried, modes)


FF_COLS = 1024
FF_HALF = FF_CHUNK // 2


def _hidden_at(d, half):
    return half * (D_FF // 2) + d * FF_HALF


def _ffn_up(h1, g3, w1_halves, carried, modes):
    rows = h1.shape[0]
    tm = _row_tile(rows)

    def body(h_ref, g_ref, wa_ref, wb_ref, act_ref, u_ref):
        u, _, _ = _rms_fwd(h_ref[...], g_ref[...])
        u = u.astype(BF16)
        u_ref[...] = u
        for half, w_ref in enumerate((wa_ref, wb_ref)):
            for d in range(N_DEV):
                c = _hidden_at(d, half)
                a1 = jnp.maximum(_mm(u, w_ref[d]), 0.0)
                act_ref[:, c:c + FF_HALF] = (a1 * a1).astype(BF16)

    wide = pl.BlockSpec((tm, D_MODEL), lambda i: (i, 0))
    return _hosting_call(
        body, "ffn_up", rows // tm,
        [wide, _full((1, D_MODEL))] + [_resident((N_DEV, D_MODEL, FF_HALF))] * 2,
        [pl.BlockSpec((tm, D_FF), lambda i: (i, 0)), wide],
        [jax.ShapeDtypeStruct((rows, D_FF), BF16), jax.ShapeDtypeStruct((rows, D_MODEL), BF16)],
        [], (h1, g3, *w1_halves), carried, modes)


def _ffn_down_loss(act, w2_halves, h1, target, g4):
    rows = h1.shape[0]
    tm = _row_tile(rows)
    steps = rows // tm
    kh = D_FF // 2

    def body(act_ref, wa_ref, wb_ref, h_ref, g_ref, t_hbm, dy_ref, df_ref, dg_ref, loss_ref, buf, sem):
        i = pl.program_id(0)
        slot = _frame_rows(t_hbm, buf, sem, i, steps, tm)

        @pl.when(i == 0)
        def _():
            dg_ref[...] = jnp.zeros_like(dg_ref)
            loss_ref[...] = jnp.zeros_like(loss_ref)
            buf[0, 0:BLOCK, :] = jnp.zeros((BLOCK, D_MODEL), F32)

        g = g_ref[...]
        f = _mm(act_ref[:, :kh], wa_ref[...]) + _mm(act_ref[:, kh:], wb_ref[...])
        y, fhat, rstd = _rms_fwd(f, g)
        grow = i * tm + lax.broadcasted_iota(jnp.int32, (tm, D_MODEL), 0)
        err = jnp.where(grow >= BLOCK, h_ref[...] + y - buf[slot], 0.0)
        loss_ref[...] += (0.5 / D_MODEL) * jnp.sum(err * err)
        dy = err * (1.0 / D_MODEL)
        df, dg = _rms_bwd(dy, fhat, rstd, g)
        dy_ref[...] = dy
        df_ref[...] = df.astype(BF16)
        dg_ref[...] += dg

    wide = pl.BlockSpec((tm, D_MODEL), lambda i: (i, 0))
    return pl.pallas_call(
        body, name="ffn_down_loss", grid=(steps,),
        in_specs=[pl.BlockSpec((tm, D_FF), lambda i: (i, 0)), _resident((kh, D_MODEL)), _resident((kh, D_MODEL)), wide,
                  _full((1, D_MODEL)), ANY_SPACE],
        out_specs=[wide, wide, _full((1, D_MODEL)), _full((SUBLANES, LANES))],
        out_shape=[jax.ShapeDtypeStruct((rows, D_MODEL), F32), jax.ShapeDtypeStruct((rows, D_MODEL), BF16),
                   jax.ShapeDtypeStruct((1, D_MODEL), F32), jax.ShapeDtypeStruct((SUBLANES, LANES), F32)],
        scratch_shapes=_frame_scratch(tm),
        compiler_params=_params(("arbitrary",)),
    )(act, *w2_halves, h1, g4, target)


def _ffn_bwd_act(df, w2t_halves, act):
    rows = df.shape[0]
    tm = _row_tile(rows)

    def body(df_ref, wa_ref, wb_ref, act_ref, da_ref):
        df_t = df_ref[...]
        for half, w_ref in enumerate((wa_ref, wb_ref)):
            for d in range(N_DEV):
                cols = slice(_hidden_at(d, half), _hidden_at(d, half) + FF_HALF)
                dact = _mm(df_t, w_ref[d])
                relu_a1, _ = _sqrt_pos(act_ref[:, cols].astype(F32))
                da_ref[:, cols] = (dact * (2.0 * relu_a1)).astype(BF16)

    hidden = pl.BlockSpec((tm, D_FF), lambda i: (i, 0))
    return pl.pallas_call(
        body, name="ffn_bwd_act", grid=(rows // tm,),
        in_specs=[pl.BlockSpec((tm, D_MODEL), lambda i: (i, 0))] + [_resident((N_DEV, D_MODEL, FF_HALF))] * 2 + [hidden],
        out_specs=hidden,
        out_shape=jax.ShapeDtypeStruct((rows, D_FF), BF16),
        compiler_params=_params(("parallel",)),
    )(df, *w2t_halves, act)


def _ffn_bwd_x(da, w1t, h1, dy, g3, carried, modes):
    rows = h1.shape[0]
    tm = _row_tile(rows)

    def body(da_ref, w_ref, h_ref, dy_ref, g_ref, dh_ref, dg_ref):
        @pl.when(pl.program_id(0) == 0)
        def _():
            dg_ref[...] = jnp.zeros_like(dg_ref)

        g = g_ref[...]
        _, xhat, rstd = _rms_fwd(h_ref[...], g)
        dx, dg = _rms_bwd(_mm(da_ref[...], w_ref[...]), xhat, rstd, g)
        dh_ref[...] = dy_ref[...] + dx
        dg_ref[...] += dg

    wide = pl.BlockSpec((tm, D_MODEL), lambda i: (i, 0))
    return _hosting_call(
        body, "ffn_bwd_x", rows // tm,
        [pl.BlockSpec((tm, D_FF), lambda i: (i, 0)), _resident((D_FF, D_MODEL)), wide, wide, _full((1, D_MODEL))],
        [wide, _full((1, D_MODEL))],
        [jax.ShapeDtypeStruct((rows, D_MODEL), F32), jax.ShapeDtypeStruct((1, D_MODEL), F32)],
        [], (da, w1t, h1, dy, g3), carried, modes)


def _ffn_bwd_weights(u2, da, act, df):
    rows = u2.shape[0]
    tb = _big_tile(rows)
    steps = rows // tb
    per = FF_COLS // FF_HALF

    def body(u_ref, da_ref, act_ref, df_ref, dw1_ref, dw2_ref, acc1, acc2):
        i = pl.program_id(1)

        @pl.when(i == 0)
        def _():
            acc1[...] = jnp.zeros_like(acc1)
            acc2[...] = jnp.zeros_like(acc2)

        acc1[...] += _mm_tn(u_ref[...], da_ref[...])
        acc2[...] += _mm_tn(act_ref[...], df_ref[...])

        @pl.when(i == steps - 1)
        def _():
            for p in range(per):
                c = p * FF_HALF
                dw1_ref[p] = acc1[:, c:c + FF_HALF].astype(BF16)
                dw2_ref[p] = acc2[c:c + FF_HALF, :].astype(BF16)

    wide = pl.BlockSpec((tb, D_MODEL), lambda j, i: (i, 0))
    chunk = pl.BlockSpec((tb, FF_COLS), lambda j, i: (i, j))
    return pl.pallas_call(
        body, name="ffn_bwd_weights", grid=(D_FF // FF_COLS, steps),
        in_specs=[wide, chunk, chunk, wide],
        out_specs=[pl.BlockSpec((None, per, D_MODEL, FF_HALF), lambda j, i: (j // 2, j % 2, 0, 0)),
                   pl.BlockSpec((per, FF_HALF, D_MODEL), lambda j, i: (j % 2, j // 2, 0))],
        out_shape=[jax.ShapeDtypeStruct((2, N_DEV, D_MODEL, FF_HALF), BF16),
                   jax.ShapeDtypeStruct((N_DEV, FF_CHUNK, D_MODEL), BF16)],
        scratch_shapes=[pltpu.VMEM((D_MODEL, FF_COLS), F32), pltpu.VMEM((FF_COLS, D_MODEL), F32)],
        compiler_params=_params(("parallel", "arbitrary")),
    )(u2, da, act, df)


def _out_proj_bwd(dh1, mix, g2, w_out_t, attn, rec, carried, modes):
    rows = dh1.shape[0]
    tm = _row_tile(rows)
    steps = rows // tm

    def body(dh_ref, mix_ref, g_ref, w_ref, attn_ref, rec_ref, dattn_ref, drec_ref, dw_ref, dg_ref, acc):
        i = pl.program_id(0)

        @pl.when(i == 0)
        def _():
            acc[...] = jnp.zeros_like(acc)
            dg_ref[...] = jnp.zeros_like(dg_ref)

        g = g_ref[...]
        _, xhat, rstd = _rms_fwd(mix_ref[...], g)
        dmix, dg = _rms_bwd(dh_ref[...], xhat, rstd, g)
        dmix = dmix.astype(BF16)
        dg_ref[...] += dg
        din = _mm(dmix, w_ref[...])
        dattn_ref[...] = din[:, :ATTN_WIDTH].astype(BF16)
        drec_ref[...] = din[:, ATTN_WIDTH:]
        acc[0:ATTN_WIDTH, :] += _mm_tn(attn_ref[...], dmix)
        acc[ATTN_WIDTH:, :] += _mm_tn(rec_ref[...], dmix)

        @pl.when(i == steps - 1)
        def _():
            dw_ref[...] = acc[...].astype(BF16)

    half = pl.BlockSpec((tm, ATTN_WIDTH), lambda i: (i, 0))
    wide = pl.BlockSpec((tm, D_MODEL), lambda i: (i, 0))
    return _hosting_call(
        body, "out_proj_bwd", steps,
        [wide, wide, _full((1, D_MODEL)), _resident((D_MODEL, D_MODEL)), half, half],
        [half, half, _full((D_MODEL, D_MODEL)), _full((1, D_MODEL))],
        [jax.ShapeDtypeStruct((rows, ATTN_WIDTH), BF16), jax.ShapeDtypeStruct((rows, LRU_WIDTH), F32),
         jax.ShapeDtypeStruct((D_MODEL, D_MODEL), BF16), jax.ShapeDtypeStruct((1, D_MODEL), F32)],
        [pltpu.VMEM((D_MODEL, D_MODEL), F32)],
        (dh1, mix, g2, w_out_t, attn, rec), carried, modes)


def _attn_bwd(qkv, dattn, sinks, bias, carried, modes):
    rows = qkv.shape[0]
    tm = _row_tile(rows)
    nbt, nt = tm // BLOCK, rows // tm

    def body(sink_ref, bias_ref, do_ref, q_ref, kp_ref, kc_ref, vp_ref, vc_ref, dq_ref, dkv_ref, dsink_ref, dk_c, dv_c):
        i = pl.program_id(0)

        @pl.when(i == 0)
        def _():
            dk_c[...] = jnp.zeros_like(dk_c)
            dv_c[...] = jnp.zeros_like(dv_c)
            dsink_ref[...] = jnp.zeros_like(dsink_ref)

        @pl.when(i < nt)
        def _():
            dk_late, dv_late = dk_c[...], dv_c[...]
            dsink_rows = [jnp.zeros((1, LANES), F32)] * ATTN_HEADS
            for b in range(nbt):
                blk = slice(b * BLOCK, (b + 1) * BLOCK)
                bias_t = _bias_of_block(bias_ref, i * nbt + b)
                dq_parts, dk_parts, dv_parts = [], [], []
                for kv in range(KV_HEADS):
                    k2 = _keys_of_block(kp_ref, kc_ref, b, kv)
                    v2 = _keys_of_block(vp_ref, vc_ref, b, kv)
                    q4 = _heads(q_ref, blk, kv * GQA_GROUP, GQA_GROUP)
                    do4 = _heads(do_ref, blk, kv * GQA_GROUP, GQA_GROUP)
                    pn, psink = _attn_probs(k2, q4, bias_t, _sink_row(sink_ref, kv))
                    dpn = _mm_nt(v2, do4)
                    delta = jnp.sum(pn * dpn, axis=0, keepdims=True)
                    ds = ((pn * (dpn - delta)) * (HEAD_DIM ** -0.5)).astype(BF16)
                    dqt = _mm_tn(k2, ds)
                    dq_parts += [dqt[:, g * BLOCK:(g + 1) * BLOCK] for g in range(GQA_GROUP)]
                    dk_parts.append(_mm(ds, q4))
                    dv_parts.append(_mm(pn.astype(BF16), do4))
                    sd = psink * delta
                    for g in range(GQA_GROUP):
                        h = kv * GQA_GROUP + g
                        dsink_rows[h] = dsink_rows[h] - jnp.sum(sd[:, g * BLOCK:(g + 1) * BLOCK])
                dq_ref[blk, :] = _from_head_major(dq_parts).astype(BF16)
                dk2 = jnp.concatenate(dk_parts, axis=1)
                dv2 = jnp.concatenate(dv_parts, axis=1)
                dkv_ref[blk, 0:KV_WIDTH] = (dk_late + dk2[0:BLOCK]).astype(BF16)
                dkv_ref[blk, KV_WIDTH:] = (dv_late + dv2[0:BLOCK]).astype(BF16)
                dk_late, dv_late = dk2[BLOCK:], dv2[BLOCK:]
            dk_c[...] = dk_late
            dv_c[...] = dv_late
            dsink_ref[...] += jnp.concatenate(dsink_rows, axis=0)

        @pl.when(i == nt)
        def _():
            dkv_ref[...] = jnp.zeros_like(dkv_ref)
            dkv_ref[0:BLOCK, 0:KV_WIDTH] = dk_c[...].astype(BF16)
            dkv_ref[0:BLOCK, KV_WIDTH:] = dv_c[...].astype(BF16)

    tile_of = lambda i: jnp.minimum(i, nt - 1)
    tile = pl.BlockSpec((tm, ATTN_WIDTH), lambda i: (tile_of(i), 0))
    return _hosting_call(
        body, "attn_bwd", nt + 1,
        [pl.BlockSpec(memory_space=pltpu.SMEM), _resident((N_BIAS, 2 * BLOCK, GQA_GROUP * BLOCK)), tile]
        + _attn_specs(tm, tile_of),
        [tile, pl.BlockSpec((tm, 2 * KV_WIDTH), lambda i: (i, 0)), _full((ATTN_HEADS, LANES))],
        [jax.ShapeDtypeStruct((rows, ATTN_WIDTH), BF16), jax.ShapeDtypeStruct((rows + tm, 2 * KV_WIDTH), BF16),
         jax.ShapeDtypeStruct((ATTN_HEADS, LANES), F32)],
        [pltpu.VMEM((BLOCK, KV_WIDTH), F32), pltpu.VMEM((BLOCK, KV_WIDTH), F32)],
        (sinks, bias, dattn, qkv, qkv, qkv, qkv, qkv), carried, modes)


ROW_CONV_B, ROW_B_A, ROW_B_X, ROW_LAMBDA = 4, 5, 6, 7


def _rec_bwd(drec, zrec, h, kept, conv_w, wa_bd, wx_bd, lam, carried, modes):
    rows = zrec.shape[0]
    tm = _rec_tile(rows)
    nt = rows // tm
    per = tm // SUBLANES

    def body(drec_ref, xr_ref, yr_ref, h_ref, xc_ref, a_ref, mult_ref, r_ref, ig_ref, hhalo_ref, cw_ref, wa_ref, wx_ref,
             lam_ref, drz_ref, small_ref, dwa_ref, dwx_ref, hbuf, abuf, u_s, g_s, dbuf, carry):
        s = pl.program_id(0)
        i = nt - 1 - s

        @pl.when(s == 0)
        def _():
            small_ref[...] = jnp.zeros_like(small_ref)
            dwa_ref[...] = jnp.zeros_like(dwa_ref)
            dwx_ref[...] = jnp.zeros_like(dwx_ref)
            carry[...] = jnp.zeros_like(carry)
            abuf[tm:tm + SUBLANES, :] = jnp.zeros((SUBLANES, LRU_WIDTH), F32)
            dbuf[tm:tm + SUBLANES, :] = jnp.zeros((SUBLANES, LRU_WIDTH), F32)

        hbuf[0:SUBLANES, :] = jnp.where(i == 0, 0.0, hhalo_ref[...])
        hbuf[SUBLANES:SUBLANES + tm, :] = h_ref[...]

        xc, a, mult, r, ig = xc_ref[...], a_ref[...], mult_ref[...], r_ref[...], ig_ref[...]
        halves = _lru_halves(xc)
        inv_mult = pl.reciprocal(mult, approx=True)

        yr = yr_ref[...]
        gel, t = _gelu(yr)
        drec_t = drec_ref[...]
        dyr = drec_t * h_ref[...] * _gelu_grad(yr, t)

        abuf[0:tm, :] = a
        u_s[...] = drec_t * gel
        a_next = abuf[pl.ds(1, tm), :]
        abuf[0:tm, :] = a_next
        carry[0:1, :] = _scan_tile(abuf, u_s, g_s, carry[0:1, :], tm, reverse=True)
        abuf[tm:tm + 1, :] = a[0:1, :]
        g = g_s[...]

        grow = i * tm + lax.broadcasted_iota(jnp.int32, (tm, LRU_WIDTH), 0)
        du = jnp.where(grow >= PAD_ROWS, g, 0.0)
        da = g * hbuf[pl.ds(SUBLANES - 1, tm), :]
        dmult = du * (ig * xc)
        dig = du * (mult * xc)
        dxc = du * (mult * ig)
        dlog_a = da * a - dmult * (a * a * inv_mult)
        sp = _softplus(-lam_ref[...])
        dgr = (dlog_a * (-LRU_C) * sp) * (r * (1.0 - r))
        dgi = dig * (ig * (1.0 - ig))
        dlam = jnp.sum(dlog_a * r, axis=0, keepdims=True) * (LRU_C * _sigmoid(-lam_ref[...]))
        dgr_b = [dgr[:, hh * LRU_HALF:(hh + 1) * LRU_HALF].astype(BF16) for hh in range(2)]
        dgi_b = [dgi[:, hh * LRU_HALF:(hh + 1) * LRU_HALF].astype(BF16) for hh in range(2)]
        dxc = dxc + jnp.concatenate(
            [_mm_nt(dgr_b[hh], wa_ref[hh]) + _mm_nt(dgi_b[hh], wx_ref[hh]) for hh in range(2)], axis=1)
        for hh in range(2):
            dwa_ref[hh] += _mm_tn(halves[hh], dgr_b[hh])
            dwx_ref[hh] += _mm_tn(halves[hh], dgi_b[hh])

        dbuf[0:tm, :] = dxc
        ahead = [dbuf[pl.ds(CONV_WIDTH - 1 - j, tm), :] for j in range(CONV_WIDTH)]
        dxr = sum(cw_ref[j:j + 1, :] * ahead[j] for j in range(CONV_WIDTH))
        dbuf[tm:tm + SUBLANES, :] = dxc[0:SUBLANES, :]
        drz_ref[:, 0:LRU_WIDTH] = dxr.astype(BF16)
        drz_ref[:, LRU_WIDTH:] = dyr.astype(BF16)

        xr = xr_ref[...]
        upd = [jnp.sum(xr * ahead[j], axis=0, keepdims=True) for j in range(CONV_WIDTH)]
        upd += [jnp.sum(dxc, axis=0, keepdims=True), jnp.sum(dgr, axis=0, keepdims=True),
                jnp.sum(dgi, axis=0, keepdims=True), dlam]
        small_ref[...] += jnp.concatenate(upd, axis=0)

    rev = lambda s: nt - 1 - s
    halo = lambda s: jnp.maximum(rev(s) * per - 1, 0)
    cols = lambda k: pl.BlockSpec((tm, LRU_WIDTH), lambda s: (rev(s), k))
    halo0 = pl.BlockSpec((SUBLANES, LRU_WIDTH), lambda s: (halo(s), 0))
    bd = _full((2, LRU_HALF, LRU_HALF))
    big = pltpu.VMEM((tm + SUBLANES, LRU_WIDTH), F32)
    tile = pltpu.VMEM((tm, LRU_WIDTH), F32)
    kept_cols = [cols(k) for k in (KEPT_XC, KEPT_A, KEPT_MULT, KEPT_R, KEPT_I)]
    return _hosting_call(
        body, "rec_bwd", nt,
        [cols(0), cols(0), cols(1), cols(0)] + kept_cols
        + [halo0, _full((CONV_WIDTH, LRU_WIDTH)), bd, bd, _full((1, LRU_WIDTH))],
        [pl.BlockSpec((tm, 2 * LRU_WIDTH), lambda s: (rev(s), 0)), _full((SUBLANES, LRU_WIDTH)), bd, bd],
        [jax.ShapeDtypeStruct((rows, 2 * LRU_WIDTH), BF16), jax.ShapeDtypeStruct((SUBLANES, LRU_WIDTH), F32),
         jax.ShapeDtypeStruct((2, LRU_HALF, LRU_HALF), F32), jax.ShapeDtypeStruct((2, LRU_HALF, LRU_HALF), F32)],
        [big, big, tile, tile, big, pltpu.VMEM((SUBLANES, LRU_WIDTH), F32)],
        (drec, zrec, zrec, h) + (kept,) * N_KEPT + (h, conv_w, wa_bd, wx_bd, lam), carried, modes)


DZ_CUTS = (0, ATTN_WIDTH, QKV_WIDTH, IN_WIDTH)


def _dz_specs(tm):
    return [pl.BlockSpec((tm, DZ_CUTS[p + 1] - DZ_CUTS[p]), lambda i: (i, 0)) for p in range(3)]


def _in_proj_bwd_x(head, x, g1, dh1, dq, dkv, drz, w_in_t, carried, modes):
    rows = dh1.shape[0]
    tm = _row_tile(rows)
    steps = rows // tm

    def body(head_ref, g_ref, dh1_ref, dq_ref, dkv_ref, drz_ref, w_ref, x_hbm, dh0_ref, dg_ref, buf, sem):
        i = pl.program_id(0)
        h0 = _h0_tile(head_ref, x_hbm, buf, sem, i, steps, tm)

        @pl.when(i == 0)
        def _():
            dg_ref[...] = jnp.zeros_like(dg_ref)

        g = g_ref[...]
        _, xhat, rstd = _rms_fwd(h0, g)
        parts = (dq_ref[...], dkv_ref[...], drz_ref[...])
        du = sum(_mm(parts[p], w_ref[DZ_CUTS[p]:DZ_CUTS[p + 1], :]) for p in range(3))
        dx, dg = _rms_bwd(du, xhat, rstd, g)
        dh0_ref[...] = dh1_ref[...] + dx
        dg_ref[...] += dg

    wide = pl.BlockSpec((tm, D_MODEL), lambda i: (i, 0))
    return _hosting_call(
        body, "in_proj_bwd_x", steps,
        [_full((BLOCK, D_MODEL)), _full((1, D_MODEL)), wide] + _dz_specs(tm) + [_resident((IN_WIDTH, D_MODEL)), ANY_SPACE],
        [wide, _full((1, D_MODEL))],
        [jax.ShapeDtypeStruct((rows, D_MODEL), F32), jax.ShapeDtypeStruct((1, D_MODEL), F32)],
        _frame_scratch(tm), (head, g1, dh1, dq, dkv, drz, w_in_t, x), carried, modes)


def _in_proj_bwd_w(u1, dq, dkv, drz, carried, modes):
    rows = u1.shape[0]
    tb = _big_tile(rows)

    def body(u_ref, dq_ref, dkv_ref, drz_ref, dw_ref):
        @pl.when(pl.program_id(0) == 0)
        def _():
            dw_ref[...] = jnp.zeros_like(dw_ref)

        u = u_ref[...]
        for p, ref in enumerate((dq_ref, dkv_ref, drz_ref)):
            dw_ref[:, DZ_CUTS[p]:DZ_CUTS[p + 1]] += _mm_tn(u, ref[...])

    return _hosting_call(
        body, "in_proj_bwd_w", rows // tb,
        [pl.BlockSpec((tb, D_MODEL), lambda i: (i, 0))] + _dz_specs(tb),
        [_full((D_MODEL, IN_WIDTH))],
        [jax.ShapeDtypeStruct((D_MODEL, IN_WIDTH), F32)],
        [], (u1, dq, dkv, drz), carried, modes)


def _adamw_math(w, m, v, g):
    nm = ADAM_B1 * m + (1.0 - ADAM_B1) * g
    nv = ADAM_B2 * v + (1.0 - ADAM_B2) * (g * g)
    m_hat = nm / (1.0 - ADAM_B1 ** ADAM_STEP)
    v_hat = nv / (1.0 - ADAM_B2 ** ADAM_STEP)
    return (-ADAM_LR) * (m_hat / (jnp.sqrt(v_hat) + ADAM_EPS) + ADAM_WD * w), nm, nv


SMALL_NAMES = ("conv_b", "b_a", "b_x", "lru_lambda", "attn_sinks", "g_post_mix", "g_pre_ffn", "g_post_ffn")
PACK_WIDTH = 1024


def _pack_rows(vals):
    assert len(SMALL_NAMES) == SUBLANES
    row = lax.broadcasted_iota(jnp.int32, (SUBLANES, PACK_WIDTH), 0)
    tile = jnp.zeros((SUBLANES, PACK_WIDTH), F32)
    for k, name in enumerate(SMALL_NAMES):
        a = vals[name].reshape(1, -1)
        tile = jnp.where(row == k, jnp.pad(a, ((0, 0), (0, PACK_WIDTH - a.shape[1]))), tile)
    return tile


def _adamw_small(weights, mom_m, mom_v, parts, loss_parts):
    n = len(SMALL_NAMES)
    views = [(1, weights[name].size) for name in SMALL_NAMES]

    def body(*refs):
        w_refs, m_refs, v_refs = refs[:n], refs[n:2 * n], refs[2 * n:3 * n]
        p_ref, l_ref, loss_ref = refs[3 * n], refs[3 * n + 1], refs[3 * n + 2]
        outs = refs[3 * n + 3:]
        for k, (_, c) in enumerate(views):
            g = p_ref[0, k:k + 1, 0:c]
            for s in range(1, N_DEV):
                g = g + p_ref[s, k:k + 1, 0:c]
            g_ref, d_ref, nm_ref, nv_ref = outs[4 * k:4 * k + 4]
            g_ref[...] = g
            d_ref[...], nm_ref[...], nv_ref[...] = _adamw_math(w_refs[k][...], m_refs[k][...], v_refs[k][...], g)
        total = l_ref[0]
        for s in range(1, N_DEV):
            total = total + l_ref[s]
        loss_ref[...] = total

    args = [src[name].reshape(view) for src in (weights, mom_m, mom_v) for name, view in zip(SMALL_NAMES, views)]
    res = pl.pallas_call(
        body, name="adamw_small",
        out_shape=[jax.ShapeDtypeStruct(loss_parts.shape[1:], F32)]
                  + [jax.ShapeDtypeStruct(view, F32) for view in views for _ in range(4)],
        compiler_params=pltpu.CompilerParams(vmem_limit_bytes=VMEM_LIMIT),
    )(*args, parts, loss_parts)
    out = {name: tuple(t.reshape(weights[name].shape) for t in res[1 + 4 * k:5 + 4 * k]) for k, name in enumerate(SMALL_NAMES)}
    return res[0], out


def _adamw(w, m, v, parts, name):
    rows, cols = w.shape
    tr = next((t for t in (256, 128) if rows % t == 0), rows)
    parts = parts if isinstance(parts, (list, tuple)) else [parts]

    def body(w_ref, m_ref, v_ref, *refs):
        p_refs, (g_ref, d_ref, nm_ref, nv_ref) = refs[:len(parts)], refs[len(parts):]

        def total(p_ref):
            g = p_ref[0].astype(F32)
            for s in range(1, N_DEV):
                g = g + p_ref[s].astype(F32)
            return g

        g = jnp.concatenate([total(p_ref) for p_ref in p_refs], axis=1) if len(parts) > 1 else total(p_refs[0])
        g_ref[...] = g
        d_ref[...], nm_ref[...], nv_ref[...] = _adamw_math(w_ref[...], m_ref[...], v_ref[...], g)

    blk = pl.BlockSpec((tr, cols), lambda i: (i, 0))
    return pl.pallas_call(
        body, name=name, grid=(rows // tr,),
        in_specs=[blk, blk, blk] + [pl.BlockSpec((N_DEV, tr, p.shape[2]), lambda i: (0, i, 0)) for p in parts],
        out_specs=[blk] * 4,
        out_shape=[jax.ShapeDtypeStruct((rows, cols), F32)] * 4,
        compiler_params=_params(("parallel",)),
    )(w, m, v, *parts)


def _cols_from_shards(g):
    return jnp.transpose(g, (1, 0, 2)).reshape(g.shape[1], N_DEV * g.shape[2])


def _cols_to_shards(a):
    r, c = a.shape
    return jnp.transpose(a.reshape(r, N_DEV, c // N_DEV), (1, 0, 2))


def _block_diag(w):
    per = LRU_HALF // LRU_BLOCK
    w = w.reshape(2, per, LRU_BLOCK, LRU_BLOCK)
    eye = jnp.eye(per, dtype=w.dtype)
    return (w[:, :, :, None, :] * eye[None, :, None, :, None]).reshape(2, LRU_HALF, LRU_HALF)


def _block_diag_extract(t):
    per = LRU_HALF // LRU_BLOCK
    t = t.reshape(2, per, LRU_BLOCK, per, LRU_BLOCK)
    return jnp.stack([t[:, b, :, b, :] for b in range(per)], axis=1).reshape(LRU_BLOCKS, LRU_BLOCK, LRU_BLOCK)


def kernel(x, meta_tokens, g_pre_mix, w_in, conv_w, conv_b, w_a, b_a, w_x, b_x, lru_lambda, attn_sinks, w_out, g_post_mix, g_pre_ffn, w_ff1, w_ff2, g_post_ffn, loss_target, m_meta_tokens, m_g_pre_mix, m_w_in, m_conv_w, m_conv_b, m_w_a, m_b_a, m_w_x, m_b_x, m_lru_lambda, m_attn_sinks, m_w_out, m_g_post_mix, m_g_pre_ffn, m_w_ff1, m_w_ff2, m_g_post_ffn, v_meta_tokens, v_g_pre_mix, v_w_in, v_conv_w, v_conv_b, v_w_a, v_b_a, v_w_x, v_b_x, v_lru_lambda, v_attn_sinks, v_w_out, v_g_post_mix, v_g_pre_ffn, v_w_ff1, v_w_ff2, v_g_post_ffn):
    weights = dict(meta_tokens=meta_tokens, g_pre_mix=g_pre_mix, w_in=w_in, conv_w=conv_w, conv_b=conv_b, w_a=w_a,
                   b_a=b_a, w_x=w_x, b_x=b_x, lru_lambda=lru_lambda, attn_sinks=attn_sinks, w_out=w_out,
                   g_post_mix=g_post_mix, g_pre_ffn=g_pre_ffn, w_ff1=w_ff1, w_ff2=w_ff2, g_post_ffn=g_post_ffn)
    mom_m = dict(meta_tokens=m_meta_tokens, g_pre_mix=m_g_pre_mix, w_in=m_w_in, conv_w=m_conv_w, conv_b=m_conv_b,
                 w_a=m_w_a, b_a=m_b_a, w_x=m_w_x, b_x=m_b_x, lru_lambda=m_lru_lambda, attn_sinks=m_attn_sinks,
                 w_out=m_w_out, g_post_mix=m_g_post_mix, g_pre_ffn=m_g_pre_ffn, w_ff1=m_w_ff1, w_ff2=m_w_ff2,
                 g_post_ffn=m_g_post_ffn)
    mom_v = dict(meta_tokens=v_meta_tokens, g_pre_mix=v_g_pre_mix, w_in=v_w_in, conv_w=v_conv_w, conv_b=v_conv_b,
                 w_a=v_w_a, b_a=v_b_a, w_x=v_w_x, b_x=v_b_x, lru_lambda=v_lru_lambda, attn_sinks=v_attn_sinks,
                 w_out=v_w_out, g_post_mix=v_g_post_mix, g_pre_ffn=v_g_pre_ffn, w_ff1=v_w_ff1, w_ff2=v_w_ff2,
                 g_post_ffn=v_g_post_ffn)
    order = list(weights)

    (g_win, g_meta, g_cw) = _gather_two_level([w_in[0].astype(BF16), meta_tokens, conv_w[0]], "gather_first")
    w_in_full = _cols_from_shards(g_win)
    meta_full = _cols_from_shards(g_meta)
    conv_w_full = _cols_from_shards(g_cw)

    head = jnp.concatenate([jnp.zeros((PAD_ROWS, D_MODEL), F32), meta_full], axis=0)
    wa_bd = _block_diag(w_a[0]).astype(BF16)
    wx_bd = _block_diag(w_x[0]).astype(BF16)
    bias = _attn_bias()

    w1_shard = w_ff1[0].astype(BF16)
    (qkv, zrec, u1), (g_wout,) = _in_proj_fwd(head, x[0], g_pre_mix, w_in_full, [w_out[0].astype(BF16)], ["gather"])
    (attn,), (w1a,) = _attn_fwd(qkv, attn_sinks, bias, [w1_shard[:, :FF_HALF]], ["gather"])
    (rec, h_lru, kept), (w1b,) = _rec_fwd(zrec, conv_w_full, conv_b, wa_bd, b_a, wx_bd, b_x, lru_lambda,
                                         [w1_shard[:, FF_HALF:]], ["gather"])
    w_out_full = g_wout.reshape(D_MODEL, D_MODEL)
    w2_shard = w_ff2[0].astype(BF16)
    (mix, h1), (w2a,) = _out_proj_fwd(attn, rec, w_out_full, head, x[0], g_post_mix, [w2_shard[:FF_HALF]], ["gather"])
    (act, u2), (w2b,) = _ffn_up(h1, g_pre_ffn, (w1a, w1b), [w2_shard[FF_HALF:]], ["gather"])
    w2_halves = [w.reshape(D_FF // 2, D_MODEL) for w in (w2a, w2b)]
    dy, df, dg_post_ffn, loss_acc = _ffn_down_loss(act, w2_halves, h1, loss_target[0], g_post_ffn)

    da1 = _ffn_bwd_act(df, [jnp.transpose(w, (0, 2, 1)) for w in (w2a, w2b)], act)
    dw1h, dw2g = _ffn_bwd_weights(u2, da1, act, df)
    w1t = jnp.concatenate([jnp.transpose(w, (0, 2, 1)).reshape(D_FF // 2, D_MODEL) for w in (w1a, w1b)], axis=0)
    (dh1, dg_pre_ffn), (p_w1a,) = _ffn_bwd_x(da1, w1t, h1, dy, g_pre_ffn, [dw1h[0]], ["scatter"])
    (dattn, drec, dw_out, dg_post_mix), (p_w1b,) = _out_proj_bwd(dh1, mix, g_post_mix, w_out_full.T, attn, rec,
                                                                [dw1h[1]], ["scatter"])
    (dq, dkv_late, dsinks), (p_w2,) = _attn_bwd(qkv, dattn, attn_sinks, bias, [dw2g], ["scatter"])
    dkv = dkv_late[BLOCK:BLOCK + qkv.shape[0]]
    (drz, rec_small, dwa_bd, dwx_bd), (p_wout,) = _rec_bwd(
        drec, zrec, h_lru, kept, conv_w_full, wa_bd, wx_bd, lru_lambda,
        [dw_out.reshape(N_DEV, D_MODEL // N_DEV, D_MODEL)], ["scatter"])
    small_grads = dict(
        conv_b=rec_small[ROW_CONV_B], b_a=rec_small[ROW_B_A], b_x=rec_small[ROW_B_X], lru_lambda=rec_small[ROW_LAMBDA],
        attn_sinks=dsinks[:, 0], g_post_mix=dg_post_mix, g_pre_ffn=dg_pre_ffn, g_post_ffn=dg_post_ffn)
    gate_rows = (LRU_BLOCKS * LRU_BLOCK, LRU_BLOCK)
    (dw_in,), (p_cw, p_small, p_wa, p_wx) = _in_proj_bwd_w(
        u1, dq, dkv, drz,
        [_cols_to_shards(rec_small[0:CONV_WIDTH]), _pack_rows(small_grads),
         _block_diag_extract(dwa_bd).reshape(gate_rows), _block_diag_extract(dwx_bd).reshape(gate_rows)],
        ["scatter", "gather", "gather", "gather"])
    (dh0, dg_pre_mix), (p_win,) = _in_proj_bwd_x(
        head, x[0], g_pre_mix, dh1, dq, dkv, drz, w_in_full.T, [_cols_to_shards(dw_in).astype(BF16)], ["scatter"])
    p_meta, p_gpm, p_loss = _exchange([_cols_to_shards(dh0[PAD_ROWS:BLOCK]), dg_pre_mix, loss_acc],
                                      ["scatter", "gather", "gather"], "exchange_last")

    res = {}
    res["g_pre_mix"] = _adamw(g_pre_mix, m_g_pre_mix, v_g_pre_mix, p_gpm, "adamw_g_pre_mix")
    res["w_in"] = _adamw(w_in[0], m_w_in[0], v_w_in[0], p_win, "adamw_w_in")
    res["w_out"] = _adamw(w_out[0], m_w_out[0], v_w_out[0], p_wout, "adamw_w_out")
    res["w_ff1"] = _adamw(w_ff1[0], m_w_ff1[0], v_w_ff1[0], [p_w1a, p_w1b], "adamw_w_ff1")
    res["w_ff2"] = _adamw(w_ff2[0], m_w_ff2[0], v_w_ff2[0], p_w2, "adamw_w_ff2")
    res["meta_tokens"] = _adamw(meta_tokens, m_meta_tokens, v_meta_tokens, p_meta, "adamw_meta")
    res["conv_w"] = _adamw(conv_w[0], m_conv_w[0], v_conv_w[0], p_cw, "adamw_conv_w")
    for name in ("w_in", "w_out", "w_ff1", "w_ff2", "conv_w"):
        res[name] = tuple(t[None] for t in res[name])
    for name, parts in (("w_a", p_wa), ("w_x", p_wx)):
        gate = _adamw(*(src[name].reshape(gate_rows) for src in (weights, mom_m, mom_v)), parts, "adamw_" + name)
        res[name] = tuple(t.reshape(weights[name].shape) for t in gate)
    loss_total, small = _adamw_small(weights, mom_m, mom_v, p_small, p_loss)
    res.update(small)

    grad_x = dh0[BLOCK:][None]
    outs = [loss_total[0, 0], grad_x]
    for k in range(4):
        outs += [res[name][k] for name in order]
    return tuple(outs)
```

```python
import jax
import jax.numpy as jnp
import numpy as np
from jax import lax
from jax.experimental import pallas as pl
from jax.experimental.pallas import tpu as pltpu

F32 = jnp.float32
BF16 = jnp.bfloat16

D_MODEL = 1024
N_META = 16
HEAD_DIM = 64
ATTN_HEADS = 8
KV_HEADS = 2
GQA_GROUP = ATTN_HEADS // KV_HEADS
ATTN_WIDTH = ATTN_HEADS * HEAD_DIM
KV_WIDTH = KV_HEADS * HEAD_DIM
QKV_WIDTH = ATTN_WIDTH + 2 * KV_WIDTH
LRU_WIDTH = 512
LRU_BLOCKS = 8
LRU_BLOCK = 64
LRU_HALF = 256
LRU_C = 8.0
CONV_WIDTH = 4
BLOCK = 128
PAD_ROWS = BLOCK - N_META
IN_WIDTH = QKV_WIDTH + 2 * LRU_WIDTH
D_FF = 4096
EPS = 1e-6
NEG = -1e30
N_DEV = 8
FF_CHUNK = D_FF // N_DEV
SUBLANES = 8
LANES = 128

ADAM_LR = 0.001
ADAM_B1 = 0.9
ADAM_B2 = 0.999
ADAM_EPS = 1e-08
ADAM_WD = 0.01
ADAM_STEP = 10

VMEM_LIMIT = 56 * 1024 * 1024


def _row_tile(rows):
    for t in (640, 512, 256, 128):
        if rows % t == 0:
            return t
    raise ValueError(rows)


def _big_tile(rows):
    for t in (1664, 1024, 512, 256, 128):
        if rows % t == 0:
            return t
    raise ValueError(rows)


def _rec_tile(rows):
    for t in (320, 256, 128):
        if rows % t == 0:
            return t
    raise ValueError(rows)


def _params(semantics):
    return pltpu.CompilerParams(dimension_semantics=semantics, vmem_limit_bytes=VMEM_LIMIT)


def _mm(a, b):
    return lax.dot_general(a, b, (((1,), (0,)), ((), ())), preferred_element_type=F32)


def _mm_nt(a, b):
    return lax.dot_general(a, b, (((1,), (1,)), ((), ())), preferred_element_type=F32)


def _mm_tn(a, b):
    return lax.dot_general(a, b, (((0,), (0,)), ((), ())), preferred_element_type=F32)


def _rms_fwd(x, g):
    rstd = lax.rsqrt(jnp.mean(x * x, axis=-1, keepdims=True) + EPS)
    xhat = x * rstd
    return xhat * g, xhat, rstd


def _rms_bwd(dy, xhat, rstd, g):
    dyg = dy * g
    c = jnp.mean(dyg * xhat, axis=-1, keepdims=True)
    dx = rstd * (dyg - xhat * c)
    dg = jnp.sum(dy * xhat, axis=0, keepdims=True)
    return dx, dg


def _sigmoid(x):
    return 0.5 * jnp.tanh(0.5 * x) + 0.5


def _log1p(x):
    u = 1.0 + x
    return jnp.where(u == 1.0, x, jnp.log(u) * x / (u - 1.0))


def _one_minus_sq_exp(x, ex):
    return -jnp.tanh(x) * (1.0 + ex * ex)


TINY = 1e-30


def _sqrt_pos(y):
    r = lax.rsqrt(jnp.maximum(y, TINY))
    return y * r, r


def _softplus(x):
    return jnp.maximum(x, 0.0) + _log1p(jnp.exp(-jnp.abs(x)))


GELU_C = 0.7978845608028654
GELU_K = 0.044715


def _gelu(x):
    t = jnp.tanh(GELU_C * (x + GELU_K * x * x * x))
    return 0.5 * x * (1.0 + t), t


def _gelu_grad(x, t):
    return 0.5 * (1.0 + t) + 0.5 * x * (1.0 - t * t) * GELU_C * (1.0 + 3.0 * GELU_K * x * x)


def _full(shape):
    return pl.BlockSpec(shape, lambda *_: (0,) * len(shape))


def _resident(shape):
    return pl.BlockSpec(shape, lambda *_: (0,) * len(shape), pipeline_mode=pl.Buffered(1))


def _exchange_copies(ins, outs, sems, modes):
    send_sems, recv_sems, local_sems = sems
    x, y, c = lax.axis_index("x"), lax.axis_index("y"), lax.axis_index("c")
    me = 4 * x + 2 * y + c

    def block(a, dev):
        return ins[a] if modes[a] == "gather" else ins[a].at[dev]

    local = [pltpu.make_async_copy(block(a, me), outs[a].at[me], local_sems.at[a]) for a in range(len(ins))]
    sends, recvs = [], []
    for a in range(len(ins)):
        for k in range(N_DEV - 1):
            bits = k + 1
            px = jnp.bitwise_xor(x, (bits >> 2) & 1)
            py = jnp.bitwise_xor(y, (bits >> 1) & 1)
            pc = jnp.bitwise_xor(c, bits & 1)
            peer = 4 * px + 2 * py + pc
            common = dict(src_ref=block(a, peer), send_sem=send_sems.at[a, k], recv_sem=recv_sems.at[a, k],
                          device_id=(px, py, pc), device_id_type=pl.DeviceIdType.MESH)
            sends.append(pltpu.make_async_remote_copy(dst_ref=outs[a].at[me], **common))
            recvs.append(pltpu.make_async_remote_copy(dst_ref=outs[a].at[peer], **common))
    return local, sends, recvs


def _exchange_start(ins, outs, sems, modes):
    local, sends, _ = _exchange_copies(ins, outs, sems, modes)
    for cp in local + sends:
        cp.start()


def _exchange_wait(ins, outs, sems, modes):
    local, sends, recvs = _exchange_copies(ins, outs, sems, modes)
    for cp in recvs:
        cp.wait_recv()
    for cp in sends:
        cp.wait_send()
    for cp in local:
        cp.wait()


def _exchange_shapes(arrays, modes):
    return [jax.ShapeDtypeStruct((N_DEV,) + a.shape if mode == "gather" else a.shape, a.dtype)
            for a, mode in zip(arrays, modes)]


def _exchange_sems(na):
    return [pltpu.SemaphoreType.DMA((na, N_DEV - 1)), pltpu.SemaphoreType.DMA((na, N_DEV - 1)),
            pltpu.SemaphoreType.DMA((na,))]


ANY_SPACE = pl.BlockSpec(memory_space=pl.ANY)


def _exchange(arrays, modes, name):
    na = len(arrays)

    def body(*refs):
        ins, outs, sems = refs[:na], refs[na:2 * na], refs[2 * na:]
        _exchange_start(ins, outs, sems, modes)
        _exchange_wait(ins, outs, sems, modes)

    return pl.pallas_call(
        body, name=name, out_shape=_exchange_shapes(arrays, modes),
        in_specs=[ANY_SPACE] * na, out_specs=[ANY_SPACE] * na, scratch_shapes=_exchange_sems(na),
        compiler_params=pltpu.CompilerParams(has_side_effects=True),
    )(*arrays)


def _gather_two_level(arrays, name):
    na = len(arrays)

    def body(*refs):
        ins, outs = refs[:na], refs[na:2 * na]
        send_sems, recv_sems, local_sems = refs[2 * na:]
        x, y, c = lax.axis_index("x"), lax.axis_index("y"), lax.axis_index("c")
        me, sibling = (x, y, c), (x, y, 1 - c)
        chips = [(1 - x, y), (x, 1 - y), (1 - x, 1 - y)]

        def copy(a, k, block, to, src=None):
            slot = outs[a].at[4 * block[0] + 2 * block[1] + block[2]]
            return pltpu.make_async_remote_copy(
                src_ref=slot if src is None else src, dst_ref=slot, send_sem=send_sems.at[a, k],
                recv_sem=recv_sems.at[a, k], device_id=to, device_id_type=pl.DeviceIdType.MESH)

        local = [pltpu.make_async_copy(ins[a], outs[a].at[4 * x + 2 * y + c], local_sems.at[a]) for a in range(na)]
        first = []
        for a in range(na):
            first.append(copy(a, 0, me, sibling, src=ins[a]))
            first += [copy(a, 1 + j, me, (*chip, c), src=ins[a]) for j, chip in enumerate(chips)]
        for cp in local + first:
            cp.start()
        passed = []
        for j, chip in enumerate(chips):
            for a in range(na):
                copy(a, 1 + j, (*chip, c), me).wait_recv()
                passed.append(copy(a, 4 + j, (*chip, c), sibling))
                passed[-1].start()
        for a in range(na):
            copy(a, 0, sibling, me).wait_recv()
            for j, chip in enumerate(chips):
                copy(a, 4 + j, (*chip, 1 - c), me).wait_recv()
        for cp in first + passed:
            cp.wait_send()
        for cp in local:
            cp.wait()

    return pl.pallas_call(
        body, name=name, out_shape=_exchange_shapes(arrays, ["gather"] * na),
        in_specs=[ANY_SPACE] * na, out_specs=[ANY_SPACE] * na, scratch_shapes=_exchange_sems(na),
        compiler_params=pltpu.CompilerParams(has_side_effects=True),
    )(*arrays)


def _hosting_call(body, name, steps, in_specs, out_specs, out_shape, scratch_shapes, args, arrays, modes):
    n_in, n_out, n_scr, na = len(in_specs), len(out_specs), len(scratch_shapes), len(arrays)

    def hosting_body(*refs):
        cuts = [0]
        for n in (n_in, na, n_out, na, n_scr, 3):
            cuts.append(cuts[-1] + n)
        ins, x_ins, outs, x_outs, scr, sems = (refs[cuts[p]:cuts[p + 1]] for p in range(6))
        step = pl.program_id(0)

        @pl.when(step == 0)
        def _():
            _exchange_start(x_ins, x_outs, sems, modes)

        body(*ins, *outs, *scr)

        @pl.when(step == steps - 1)
        def _():
            _exchange_wait(x_ins, x_outs, sems, modes)

    res = pl.pallas_call(
        hosting_body, name=name, grid=(steps,),
        in_specs=list(in_specs) + [ANY_SPACE] * na, out_specs=list(out_specs) + [ANY_SPACE] * na,
        out_shape=list(out_shape) + _exchange_shapes(arrays, modes),
        scratch_shapes=list(scratch_shapes) + _exchange_sems(na),
        compiler_params=_params(("arbitrary",)),
    )(*args, *arrays)
    return res[:n_out], res[n_out:]


def _frame_rows(src_hbm, buf, sem, i, steps, tm):
    def first():
        return pltpu.make_async_copy(src_hbm.at[pl.ds(0, tm - BLOCK)], buf.at[0, pl.ds(BLOCK, tm - BLOCK)], sem.at[0])

    def later(t, slot):
        return pltpu.make_async_copy(src_hbm.at[pl.ds(pl.multiple_of(t * tm - BLOCK, BLOCK), tm)], buf.at[slot], sem.at[slot])

    slot = i % 2

    @pl.when(i == 0)
    def _():
        first().start()

    @pl.when(i + 1 < steps)
    def _():
        later(i + 1, 1 - slot).start()

    @pl.when(i == 0)
    def _():
        first().wait()

    @pl.when(i > 0)
    def _():
        later(i, slot).wait()

    return slot


def _frame_scratch(tm):
    return [pltpu.VMEM((2, tm, D_MODEL), F32), pltpu.SemaphoreType.DMA((2,))]


def _h0_tile(head_ref, x_hbm, buf, sem, i, steps, tm):
    slot = _frame_rows(x_hbm, buf, sem, i, steps, tm)

    @pl.when(i == 0)
    def _():
        buf[0, 0:BLOCK, :] = head_ref[...]

    return buf[slot]


def _in_proj_fwd(head, x, g1, w_in, carried, modes):
    rows = BLOCK + x.shape[0]
    tm = _row_tile(rows)
    steps = rows // tm

    def body(head_ref, g_ref, w_ref, x_hbm, qkv_ref, zrec_ref, u_ref, buf, sem):
        h = _h0_tile(head_ref, x_hbm, buf, sem, pl.program_id(0), steps, tm)
        u, _, _ = _rms_fwd(h, g_ref[...])
        u = u.astype(BF16)
        u_ref[...] = u
        z = _mm(u, w_ref[...])
        qkv_ref[...] = z[:, :QKV_WIDTH].astype(BF16)
        zrec_ref[...] = z[:, QKV_WIDTH:]

    wide = pl.BlockSpec((tm, D_MODEL), lambda i: (i, 0))
    return _hosting_call(
        body, "in_proj_fwd", steps,
        [_full((BLOCK, D_MODEL)), _full((1, D_MODEL)), _resident((D_MODEL, IN_WIDTH)), ANY_SPACE],
        [pl.BlockSpec((tm, QKV_WIDTH), lambda i: (i, 0)), pl.BlockSpec((tm, 2 * LRU_WIDTH), lambda i: (i, 0)), wide],
        [jax.ShapeDtypeStruct((rows, QKV_WIDTH), BF16), jax.ShapeDtypeStruct((rows, 2 * LRU_WIDTH), F32),
         jax.ShapeDtypeStruct((rows, D_MODEL), BF16)],
        _frame_scratch(tm), (head, g1, w_in, x), carried, modes)


N_BIAS = 3


def _attn_bias():
    key = np.arange(2 * BLOCK)[:, None]
    r = np.arange(GQA_GROUP * BLOCK)[None, :] % BLOCK
    band = (key > r) & (key <= r + BLOCK)
    out = [np.where(band & ((n - 1) * BLOCK + key >= PAD_ROWS), 0.0, NEG) for n in range(N_BIAS)]
    return jnp.asarray(np.stack(out), F32)


def _attn_probs(k2, q4, bias, sink_row):
    s = _mm_nt(k2, q4) * (HEAD_DIM ** -0.5) + bias
    m = jnp.maximum(jnp.max(s, axis=0, keepdims=True), sink_row)
    p = jnp.exp(s - m)
    es = jnp.exp(sink_row - m)
    inv = 1.0 / (jnp.sum(p, axis=0, keepdims=True) + es)
    return p * inv, es * inv


def _heads(ref, rows, first, count):
    return jnp.concatenate([ref[rows, (first + g) * HEAD_DIM:(first + g + 1) * HEAD_DIM] for g in range(count)], axis=0)


def _keys_of_block(prev_ref, cur_ref, b, kv):
    sl = slice(kv * HEAD_DIM, (kv + 1) * HEAD_DIM)
    before = prev_ref[:, sl] if b == 0 else cur_ref[(b - 1) * BLOCK:b * BLOCK, sl]
    return jnp.concatenate([before, cur_ref[b * BLOCK:(b + 1) * BLOCK, sl]], axis=0)


def _bias_of_block(bias_ref, block):
    return bias_ref[jnp.minimum(block, N_BIAS - 1)]


def _sink_row(sink_ref, kv):
    g = lax.broadcasted_iota(jnp.int32, (1, GQA_GROUP * BLOCK), 1) // BLOCK
    row = jnp.full((1, GQA_GROUP * BLOCK), sink_ref[0, kv * GQA_GROUP], F32)
    for i in range(1, GQA_GROUP):
        row = jnp.where(g == i, sink_ref[0, kv * GQA_GROUP + i], row)
    return row


def _from_head_major(pieces):
    return jnp.concatenate(pieces, axis=0).T


def _attn_specs(tm, tile_of):
    nbt = tm // BLOCK
    k_col, v_col = ATTN_WIDTH // KV_WIDTH, ATTN_WIDTH // KV_WIDTH + 1
    before = lambda i: jnp.maximum(tile_of(i) * nbt - 1, 0)
    return [pl.BlockSpec((tm, ATTN_WIDTH), lambda i: (tile_of(i), 0)),
            pl.BlockSpec((BLOCK, KV_WIDTH), lambda i: (before(i), k_col)),
            pl.BlockSpec((tm, KV_WIDTH), lambda i: (tile_of(i), k_col)),
            pl.BlockSpec((BLOCK, KV_WIDTH), lambda i: (before(i), v_col)),
            pl.BlockSpec((tm, KV_WIDTH), lambda i: (tile_of(i), v_col))]


def _attn_fwd(qkv, sinks, bias, carried, modes):
    rows = qkv.shape[0]
    tm = _row_tile(rows)
    nbt = tm // BLOCK

    def body(sink_ref, bias_ref, q_ref, kp_ref, kc_ref, vp_ref, vc_ref, o_ref):
        i = pl.program_id(0)
        for b in range(nbt):
            blk = slice(b * BLOCK, (b + 1) * BLOCK)
            bias_t = _bias_of_block(bias_ref, i * nbt + b)
            pieces = []
            for kv in range(KV_HEADS):
                k2 = _keys_of_block(kp_ref, kc_ref, b, kv)
                v2 = _keys_of_block(vp_ref, vc_ref, b, kv)
                q4 = _heads(q_ref, blk, kv * GQA_GROUP, GQA_GROUP)
                pn, _ = _attn_probs(k2, q4, bias_t, _sink_row(sink_ref, kv))
                ot = _mm_tn(v2, pn.astype(BF16))
                pieces += [ot[:, g * BLOCK:(g + 1) * BLOCK] for g in range(GQA_GROUP)]
            o_ref[blk, :] = _from_head_major(pieces).astype(BF16)

    return _hosting_call(
        body, "attn_fwd", rows // tm,
        [pl.BlockSpec(memory_space=pltpu.SMEM), _resident((N_BIAS, 2 * BLOCK, GQA_GROUP * BLOCK))]
        + _attn_specs(tm, lambda i: i),
        [pl.BlockSpec((tm, ATTN_WIDTH), lambda i: (i, 0))],
        [jax.ShapeDtypeStruct((rows, ATTN_WIDTH), BF16)],
        [], (sinks, bias, qkv, qkv, qkv, qkv, qkv), carried, modes)


def _conv_taps(xbuf, tm):
    return [xbuf[pl.ds(SUBLANES - (CONV_WIDTH - 1 - j), tm), :] for j in range(CONV_WIDTH)]


def _lru_halves(xc):
    return [xc[:, h * LRU_HALF:(h + 1) * LRU_HALF].astype(BF16) for h in range(2)]


def _lru_gates(xc, wa_ref, ba_ref, wx_ref, bx_ref, lam_ref):
    halves = _lru_halves(xc)
    gate_r = jnp.concatenate([_mm(halves[h], wa_ref[h]) for h in range(2)], axis=1) + ba_ref[...]
    gate_i = jnp.concatenate([_mm(halves[h], wx_ref[h]) for h in range(2)], axis=1) + bx_ref[...]
    r = _sigmoid(gate_r)
    ig = _sigmoid(gate_i)
    log_a = (-LRU_C) * r * _softplus(-lam_ref[...])
    a = jnp.exp(log_a)
    mult, _ = _sqrt_pos(_one_minus_sq_exp(log_a, a))
    return r, ig, a, mult


KEPT_XC, KEPT_A, KEPT_MULT, KEPT_R, KEPT_I, N_KEPT = 0, 1, 2, 3, 4, 5


def _scan_tile(a_ref, u_ref, out_ref, carry, tm, reverse):
    row = lax.broadcasted_iota(jnp.int32, (SUBLANES, LRU_WIDTH), 0)
    groups = tm // SUBLANES

    def step(j, prev):
        jj = groups - 1 - j if reverse else j
        o = pl.multiple_of(jj * SUBLANES, SUBLANES)
        a = a_ref[pl.ds(o, SUBLANES), :]
        u = u_ref[pl.ds(o, SUBLANES), :]
        for s in (1, 2, 4):
            shift = SUBLANES - s if reverse else s
            keep = (row < SUBLANES - s) if reverse else (row >= s)
            u = jnp.where(keep, a * pltpu.roll(u, shift, 0) + u, u)
            a = jnp.where(keep, a * pltpu.roll(a, shift, 0), a)
        out = a * prev + u
        out_ref[pl.ds(o, SUBLANES), :] = out
        return out[0:1, :] if reverse else out[SUBLANES - 1:SUBLANES, :]

    return lax.fori_loop(0, groups, step, carry)


def _rec_fwd(zrec, conv_w, conv_b, wa_bd, b_a, wx_bd, b_x, lam, carried, modes):
    rows = zrec.shape[0]
    tm = _row_tile(rows)

    def body(xr_ref, yr_ref, cw_ref, cb_ref, wa_ref, ba_ref, wx_ref, bx_ref, lam_ref, rec_ref, h_ref, kept_ref,
             xbuf, a_s, u_s, carry):
        i = pl.program_id(0)

        @pl.when(i == 0)
        def _():
            xbuf[0:SUBLANES, :] = jnp.zeros((SUBLANES, LRU_WIDTH), F32)
            carry[...] = jnp.zeros_like(carry)

        @pl.when(i > 0)
        def _():
            xbuf[0:SUBLANES, :] = xbuf[tm:tm + SUBLANES, :]

        xbuf[SUBLANES:SUBLANES + tm, :] = xr_ref[...]
        taps = _conv_taps(xbuf, tm)
        xc = cb_ref[...] + sum(cw_ref[j:j + 1, :] * taps[j] for j in range(CONV_WIDTH))
        r, ig, a, mult = _lru_gates(xc, wa_ref, ba_ref, wx_ref, bx_ref, lam_ref)
        for k, val in ((KEPT_XC, xc), (KEPT_A, a), (KEPT_MULT, mult), (KEPT_R, r), (KEPT_I, ig)):
            kept_ref[:, k * LRU_WIDTH:(k + 1) * LRU_WIDTH] = val
        grow = i * tm + lax.broadcasted_iota(jnp.int32, (tm, LRU_WIDTH), 0)
        a_s[...] = a
        u_s[...] = jnp.where(grow >= PAD_ROWS, mult * (ig * xc), 0.0)
        carry[0:1, :] = _scan_tile(a_s, u_s, h_ref, carry[0:1, :], tm, reverse=False)
        gel, _ = _gelu(yr_ref[...])
        rec_ref[...] = (gel * h_ref[...]).astype(BF16)

    vec = _full((1, LRU_WIDTH))
    bd = _full((2, LRU_HALF, LRU_HALF))
    return _hosting_call(
        body, "rec_fwd", rows // tm,
        [pl.BlockSpec((tm, LRU_WIDTH), lambda i: (i, 0)), pl.BlockSpec((tm, LRU_WIDTH), lambda i: (i, 1)),
         _full((CONV_WIDTH, LRU_WIDTH)), vec, bd, vec, bd, vec, vec],
        [pl.BlockSpec((tm, LRU_WIDTH), lambda i: (i, 0))] * 2 + [pl.BlockSpec((tm, N_KEPT * LRU_WIDTH), lambda i: (i, 0))],
        [jax.ShapeDtypeStruct((rows, LRU_WIDTH), BF16), jax.ShapeDtypeStruct((rows, LRU_WIDTH), F32),
         jax.ShapeDtypeStruct((rows, N_KEPT * LRU_WIDTH), F32)],
        [pltpu.VMEM((tm + SUBLANES, LRU_WIDTH), F32), pltpu.VMEM((tm, LRU_WIDTH), F32),
         pltpu.VMEM((tm, LRU_WIDTH), F32), pltpu.VMEM((SUBLANES, LRU_WIDTH), F32)],
        (zrec, zrec, conv_w, conv_b, wa_bd, b_a, wx_bd, b_x, lam), carried, modes)


def _out_proj_fwd(attn, rec, w_out, head, x, g2, carried, modes):
    rows = attn.shape[0]
    tm = _row_tile(rows)
    steps = rows // tm

    def body(attn_ref, rec_ref, w_ref, head_ref, g_ref, x_hbm, mix_ref, h1_ref, buf, sem):
        h0 = _h0_tile(head_ref, x_hbm, buf, sem, pl.program_id(0), steps, tm)
        mix = _mm(attn_ref[...], w_ref[0:ATTN_WIDTH, :]) + _mm(rec_ref[...], w_ref[ATTN_WIDTH:, :])
        y, _, _ = _rms_fwd(mix, g_ref[...])
        mix_ref[...] = mix
        h1_ref[...] = h0 + y

    half = pl.BlockSpec((tm, ATTN_WIDTH), lambda i: (i, 0))
    wide = pl.BlockSpec((tm, D_MODEL), lambda i: (i, 0))
    return _hosting_call(
        body, "out_proj_fwd", steps,
        [half, half, _resident((D_MODEL, D_MODEL)), _full((BLOCK, D_MODEL)), _full((1, D_MODEL)), ANY_SPACE],
        [wide, wide],
        [jax.ShapeDtypeStruct((rows, D_MODEL), F32)] * 2,
        _frame_scratch(tm), (attn, rec, w_out, head, g2, x), carried, modes)


FF_COLS = 1024
FF_HALF = FF_CHUNK // 2


def _hidden_at(d, half):
    return half * (D_FF // 2) + d * FF_HALF


def _ffn_up(h1, g3, w1_halves, carried, modes):
    rows = h1.shape[0]
    tm = _row_tile(rows)

    def body(h_ref, g_ref, wa_ref, wb_ref, act_ref, u_ref):
        u, _, _ = _rms_fwd(h_ref[...], g_ref[...])
        u = u.astype(BF16)
        u_ref[...] = u
        for half, w_ref in enumerate((wa_ref, wb_ref)):
            for d in range(N_DEV):
                c = _hidden_at(d, half)
                a1 = jnp.maximum(_mm(u, w_ref[d]), 0.0)
                act_ref[:, c:c + FF_HALF] = (a1 * a1).astype(BF16)

    wide = pl.BlockSpec((tm, D_MODEL), lambda i: (i, 0))
    return _hosting_call(
        body, "ffn_up", rows // tm,
        [wide, _full((1, D_MODEL))] + [_resident((N_DEV, D_MODEL, FF_HALF))] * 2,
        [pl.BlockSpec((tm, D_FF), lambda i: (i, 0)), wide],
        [jax.ShapeDtypeStruct((rows, D_FF), BF16), jax.ShapeDtypeStruct((rows, D_MODEL), BF16)],
        [], (h1, g3, *w1_halves), carried, modes)


def _ffn_down_loss(act, w2_halves, h1, target, g4, carried, modes):
    rows = h1.shape[0]
    tm = _row_tile(rows)
    steps = rows // tm
    kh = D_FF // 2

    def body(act_ref, wa_ref, wb_ref, h_ref, g_ref, t_hbm, dy_ref, df_ref, dg_ref, loss_ref, buf, sem):
        i = pl.program_id(0)
        slot = _frame_rows(t_hbm, buf, sem, i, steps, tm)

        @pl.when(i == 0)
        def _():
            dg_ref[...] = jnp.zeros_like(dg_ref)
            loss_ref[...] = jnp.zeros_like(loss_ref)
            buf[0, 0:BLOCK, :] = jnp.zeros((BLOCK, D_MODEL), F32)

        g = g_ref[...]
        f = _mm(act_ref[:, :kh], wa_ref[...]) + _mm(act_ref[:, kh:], wb_ref[...])
        y, fhat, rstd = _rms_fwd(f, g)
        grow = i * tm + lax.broadcasted_iota(jnp.int32, (tm, D_MODEL), 0)
        err = jnp.where(grow >= BLOCK, h_ref[...] + y - buf[slot], 0.0)
        loss_ref[...] += (0.5 / D_MODEL) * jnp.sum(err * err)
        dy = err * (1.0 / D_MODEL)
        df, dg = _rms_bwd(dy, fhat, rstd, g)
        dy_ref[...] = dy
        df_ref[...] = df.astype(BF16)
        dg_ref[...] += dg

    wide = pl.BlockSpec((tm, D_MODEL), lambda i: (i, 0))
    return _hosting_call(
        body, "ffn_down_loss", steps,
        [pl.BlockSpec((tm, D_FF), lambda i: (i, 0)), _resident((kh, D_MODEL)), _resident((kh, D_MODEL)), wide,
         _full((1, D_MODEL)), ANY_SPACE],
        [wide, wide, _full((1, D_MODEL)), _full((SUBLANES, LANES))],
        [jax.ShapeDtypeStruct((rows, D_MODEL), F32), jax.ShapeDtypeStruct((rows, D_MODEL), BF16),
         jax.ShapeDtypeStruct((1, D_MODEL), F32), jax.ShapeDtypeStruct((SUBLANES, LANES), F32)],
        _frame_scratch(tm), (act, *w2_halves, h1, g4, target), carried, modes)


def _ffn_bwd_act(df, w2t_halves, act, carried, modes):
    rows = df.shape[0]
    tm = _row_tile(rows)

    def body(df_ref, wa_ref, wb_ref, act_ref, da_ref):
        df_t = df_ref[...]
        for half, w_ref in enumerate((wa_ref, wb_ref)):
            for d in range(N_DEV):
                cols = slice(_hidden_at(d, half), _hidden_at(d, half) + FF_HALF)
                dact = _mm(df_t, w_ref[d])
                relu_a1, _ = _sqrt_pos(act_ref[:, cols].astype(F32))
                da_ref[:, cols] = (dact * (2.0 * relu_a1)).astype(BF16)

    hidden = pl.BlockSpec((tm, D_FF), lambda i: (i, 0))
    return _hosting_call(
        body, "ffn_bwd_act", rows // tm,
        [pl.BlockSpec((tm, D_MODEL), lambda i: (i, 0))] + [_resident((N_DEV, D_MODEL, FF_HALF))] * 2 + [hidden],
        [hidden],
        [jax.ShapeDtypeStruct((rows, D_FF), BF16)],
        [], (df, *w2t_halves, act), carried, modes)


def _ffn_bwd_x(da, w1t_halves, h1, dy, g3, carried, modes):
    rows = h1.shape[0]
    tm = _row_tile(rows)
    kh = D_FF // 2

    def body(da_ref, wa_ref, wb_ref, h_ref, dy_ref, g_ref, dh_ref, dg_ref):
        @pl.when(pl.program_id(0) == 0)
        def _():
            dg_ref[...] = jnp.zeros_like(dg_ref)

        g = g_ref[...]
        _, xhat, rstd = _rms_fwd(h_ref[...], g)
        du = _mm(da_ref[:, :kh], wa_ref[...]) + _mm(da_ref[:, kh:], wb_ref[...])
        dx, dg = _rms_bwd(du, xhat, rstd, g)
        dh_ref[...] = dy_ref[...] + dx
        dg_ref[...] += dg

    wide = pl.BlockSpec((tm, D_MODEL), lambda i: (i, 0))
    return _hosting_call(
        body, "ffn_bwd_x", rows // tm,
        [pl.BlockSpec((tm, D_FF), lambda i: (i, 0)), _resident((kh, D_MODEL)), _resident((kh, D_MODEL)), wide, wide,
         _full((1, D_MODEL))],
        [wide, _full((1, D_MODEL))],
        [jax.ShapeDtypeStruct((rows, D_MODEL), F32), jax.ShapeDtypeStruct((1, D_MODEL), F32)],
        [], (da, *w1t_halves, h1, dy, g3), carried, modes)


def _ffn_bwd_weights(u2, da, act, df):
    rows = u2.shape[0]
    tb = _big_tile(rows)
    steps = rows // tb
    per = FF_COLS // FF_HALF

    def body(u_ref, da_ref, act_ref, df_ref, dw1_ref, dw2_ref, acc1, acc2):
        i = pl.program_id(1)

        @pl.when(i == 0)
        def _():
            acc1[...] = jnp.zeros_like(acc1)
            acc2[...] = jnp.zeros_like(acc2)

        acc1[...] += _mm_tn(u_ref[...], da_ref[...])
        acc2[...] += _mm_tn(act_ref[...], df_ref[...])

        @pl.when(i == steps - 1)
        def _():
            for p in range(per):
                c = p * FF_HALF
                dw1_ref[p] = acc1[:, c:c + FF_HALF].astype(BF16)
                dw2_ref[p] = acc2[c:c + FF_HALF, :].astype(BF16)

    wide = pl.BlockSpec((tb, D_MODEL), lambda j, i: (i, 0))
    chunk = pl.BlockSpec((tb, FF_COLS), lambda j, i: (i, j))
    return pl.pallas_call(
        body, name="ffn_bwd_weights", grid=(D_FF // FF_COLS, steps),
        in_specs=[wide, chunk, chunk, wide],
        out_specs=[pl.BlockSpec((None, per, D_MODEL, FF_HALF), lambda j, i: (j // 2, j % 2, 0, 0)),
                   pl.BlockSpec((per, FF_HALF, D_MODEL), lambda j, i: (j % 2, j // 2, 0))],
        out_shape=[jax.ShapeDtypeStruct((2, N_DEV, D_MODEL, FF_HALF), BF16),
                   jax.ShapeDtypeStruct((N_DEV, FF_CHUNK, D_MODEL), BF16)],
        scratch_shapes=[pltpu.VMEM((D_MODEL, FF_COLS), F32), pltpu.VMEM((FF_COLS, D_MODEL), F32)],
        compiler_params=_params(("parallel", "arbitrary")),
    )(u2, da, act, df)


def _out_proj_bwd(dh1, mix, g2, w_out_t, attn, rec, carried, modes):
    rows = dh1.shape[0]
    tm = _row_tile(rows)
    steps = rows // tm

    def body(dh_ref, mix_ref, g_ref, w_ref, attn_ref, rec_ref, dattn_ref, drec_ref, dw_ref, dg_ref, acc):
        i = pl.program_id(0)

        @pl.when(i == 0)
        def _():
            acc[...] = jnp.zeros_like(acc)
            dg_ref[...] = jnp.zeros_like(dg_ref)

        g = g_ref[...]
        _, xhat, rstd = _rms_fwd(mix_ref[...], g)
        dmix, dg = _rms_bwd(dh_ref[...], xhat, rstd, g)
        dmix = dmix.astype(BF16)
        dg_ref[...] += dg
        din = _mm(dmix, w_ref[...])
        dattn_ref[...] = din[:, :ATTN_WIDTH].astype(BF16)
        drec_ref[...] = din[:, ATTN_WIDTH:]
        acc[0:ATTN_WIDTH, :] += _mm_tn(attn_ref[...], dmix)
        acc[ATTN_WIDTH:, :] += _mm_tn(rec_ref[...], dmix)

        @pl.when(i == steps - 1)
        def _():
            dw_ref[...] = acc[...].astype(BF16)

    half = pl.BlockSpec((tm, ATTN_WIDTH), lambda i: (i, 0))
    wide = pl.BlockSpec((tm, D_MODEL), lambda i: (i, 0))
    return _hosting_call(
        body, "out_proj_bwd", steps,
        [wide, wide, _full((1, D_MODEL)), _resident((D_MODEL, D_MODEL)), half, half],
        [half, half, _full((D_MODEL, D_MODEL)), _full((1, D_MODEL))],
        [jax.ShapeDtypeStruct((rows, ATTN_WIDTH), BF16), jax.ShapeDtypeStruct((rows, LRU_WIDTH), F32),
         jax.ShapeDtypeStruct((D_MODEL, D_MODEL), BF16), jax.ShapeDtypeStruct((1, D_MODEL), F32)],
        [pltpu.VMEM((D_MODEL, D_MODEL), F32)],
        (dh1, mix, g2, w_out_t, attn, rec), carried, modes)


def _attn_bwd(qkv, dattn, sinks, bias, carried, modes):
    rows = qkv.shape[0]
    tm = _row_tile(rows)
    nbt, nt = tm // BLOCK, rows // tm

    def body(sink_ref, bias_ref, do_ref, q_ref, kp_ref, kc_ref, vp_ref, vc_ref, dq_ref, dkv_ref, dsink_ref, dk_c, dv_c):
        i = pl.program_id(0)

        @pl.when(i == 0)
        def _():
            dk_c[...] = jnp.zeros_like(dk_c)
            dv_c[...] = jnp.zeros_like(dv_c)
            dsink_ref[...] = jnp.zeros_like(dsink_ref)

        @pl.when(i < nt)
        def _():
            dk_late, dv_late = dk_c[...], dv_c[...]
            dsink_rows = [jnp.zeros((1, LANES), F32)] * ATTN_HEADS
            for b in range(nbt):
                blk = slice(b * BLOCK, (b + 1) * BLOCK)
                bias_t = _bias_of_block(bias_ref, i * nbt + b)
                dq_parts, dk_parts, dv_parts = [], [], []
                for kv in range(KV_HEADS):
                    k2 = _keys_of_block(kp_ref, kc_ref, b, kv)
                    v2 = _keys_of_block(vp_ref, vc_ref, b, kv)
                    q4 = _heads(q_ref, blk, kv * GQA_GROUP, GQA_GROUP)
                    do4 = _heads(do_ref, blk, kv * GQA_GROUP, GQA_GROUP)
                    pn, psink = _attn_probs(k2, q4, bias_t, _sink_row(sink_ref, kv))
                    dpn = _mm_nt(v2, do4)
                    delta = jnp.sum(pn * dpn, axis=0, keepdims=True)
                    ds = ((pn * (dpn - delta)) * (HEAD_DIM ** -0.5)).astype(BF16)
                    dqt = _mm_tn(k2, ds)
                    dq_parts += [dqt[:, g * BLOCK:(g + 1) * BLOCK] for g in range(GQA_GROUP)]
                    dk_parts.append(_mm(ds, q4))
                    dv_parts.append(_mm(pn.astype(BF16), do4))
                    sd = psink * delta
                    for g in range(GQA_GROUP):
                        h = kv * GQA_GROUP + g
                        dsink_rows[h] = dsink_rows[h] - jnp.sum(sd[:, g * BLOCK:(g + 1) * BLOCK])
                dq_ref[blk, :] = _from_head_major(dq_parts).astype(BF16)
                dk2 = jnp.concatenate(dk_parts, axis=1)
                dv2 = jnp.concatenate(dv_parts, axis=1)
                dkv_ref[blk, 0:KV_WIDTH] = (dk_late + dk2[0:BLOCK]).astype(BF16)
                dkv_ref[blk, KV_WIDTH:] = (dv_late + dv2[0:BLOCK]).astype(BF16)
                dk_late, dv_late = dk2[BLOCK:], dv2[BLOCK:]
            dk_c[...] = dk_late
            dv_c[...] = dv_late
            dsink_ref[...] += jnp.concatenate(dsink_rows, axis=0)

        @pl.when(i == nt)
        def _():
            dkv_ref[...] = jnp.zeros_like(dkv_ref)
            dkv_ref[0:BLOCK, 0:KV_WIDTH] = dk_c[...].astype(BF16)
            dkv_ref[0:BLOCK, KV_WIDTH:] = dv_c[...].astype(BF16)

    tile_of = lambda i: jnp.minimum(i, nt - 1)
    tile = pl.BlockSpec((tm, ATTN_WIDTH), lambda i: (tile_of(i), 0))
    return _hosting_call(
        body, "attn_bwd", nt + 1,
        [pl.BlockSpec(memory_space=pltpu.SMEM), _resident((N_BIAS, 2 * BLOCK, GQA_GROUP * BLOCK)), tile]
        + _attn_specs(tm, tile_of),
        [tile, pl.BlockSpec((tm, 2 * KV_WIDTH), lambda i: (i, 0)), _full((ATTN_HEADS, LANES))],
        [jax.ShapeDtypeStruct((rows, ATTN_WIDTH), BF16), jax.ShapeDtypeStruct((rows + tm, 2 * KV_WIDTH), BF16),
         jax.ShapeDtypeStruct((ATTN_HEADS, LANES), F32)],
        [pltpu.VMEM((BLOCK, KV_WIDTH), F32), pltpu.VMEM((BLOCK, KV_WIDTH), F32)],
        (sinks, bias, dattn, qkv, qkv, qkv, qkv, qkv), carried, modes)


ROW_CONV_B, ROW_B_A, ROW_B_X, ROW_LAMBDA = 4, 5, 6, 7


def _rec_bwd(drec, zrec, h, kept, conv_w, wa_bd, wx_bd, lam, carried, modes):
    rows = zrec.shape[0]
    tm = _rec_tile(rows)
    nt = rows // tm
    per = tm // SUBLANES

    def body(drec_ref, xr_ref, yr_ref, h_ref, xc_ref, a_ref, mult_ref, r_ref, ig_ref, hhalo_ref, cw_ref, wa_ref, wx_ref,
             lam_ref, drz_ref, small_ref, dwa_ref, dwx_ref, hbuf, abuf, u_s, g_s, dbuf, carry):
        s = pl.program_id(0)
        i = nt - 1 - s

        @pl.when(s == 0)
        def _():
            small_ref[...] = jnp.zeros_like(small_ref)
            dwa_ref[...] = jnp.zeros_like(dwa_ref)
            dwx_ref[...] = jnp.zeros_like(dwx_ref)
            carry[...] = jnp.zeros_like(carry)
            abuf[tm:tm + SUBLANES, :] = jnp.zeros((SUBLANES, LRU_WIDTH), F32)
            dbuf[tm:tm + SUBLANES, :] = jnp.zeros((SUBLANES, LRU_WIDTH), F32)

        hbuf[0:SUBLANES, :] = jnp.where(i == 0, 0.0, hhalo_ref[...])
        hbuf[SUBLANES:SUBLANES + tm, :] = h_ref[...]

        xc, a, mult, r, ig = xc_ref[...], a_ref[...], mult_ref[...], r_ref[...], ig_ref[...]
        halves = _lru_halves(xc)
        inv_mult = pl.reciprocal(mult, approx=True)

        yr = yr_ref[...]
        gel, t = _gelu(yr)
        drec_t = drec_ref[...]
        dyr = drec_t * h_ref[...] * _gelu_grad(yr, t)

        abuf[0:tm, :] = a
        u_s[...] = drec_t * gel
        a_next = abuf[pl.ds(1, tm), :]
        abuf[0:tm, :] = a_next
        carry[0:1, :] = _scan_tile(abuf, u_s, g_s, carry[0:1, :], tm, reverse=True)
        abuf[tm:tm + 1, :] = a[0:1, :]
        g = g_s[...]

        grow = i * tm + lax.broadcasted_iota(jnp.int32, (tm, LRU_WIDTH), 0)
        du = jnp.where(grow >= PAD_ROWS, g, 0.0)
        da = g * hbuf[pl.ds(SUBLANES - 1, tm), :]
        dmult = du * (ig * xc)
        dig = du * (mult * xc)
        dxc = du * (mult * ig)
        dlog_a = da * a - dmult * (a * a * inv_mult)
        sp = _softplus(-lam_ref[...])
        dgr = (dlog_a * (-LRU_C) * sp) * (r * (1.0 - r))
        dgi = dig * (ig * (1.0 - ig))
        dlam = jnp.sum(dlog_a * r, axis=0, keepdims=True) * (LRU_C * _sigmoid(-lam_ref[...]))
        dgr_b = [dgr[:, hh * LRU_HALF:(hh + 1) * LRU_HALF].astype(BF16) for hh in range(2)]
        dgi_b = [dgi[:, hh * LRU_HALF:(hh + 1) * LRU_HALF].astype(BF16) for hh in range(2)]
        dxc = dxc + jnp.concatenate(
            [_mm_nt(dgr_b[hh], wa_ref[hh]) + _mm_nt(dgi_b[hh], wx_ref[hh]) for hh in range(2)], axis=1)
        for hh in range(2):
            dwa_ref[hh] += _mm_tn(halves[hh], dgr_b[hh])
            dwx_ref[hh] += _mm_tn(halves[hh], dgi_b[hh])

        dbuf[0:tm, :] = dxc
        ahead = [dbuf[pl.ds(CONV_WIDTH - 1 - j, tm), :] for j in range(CONV_WIDTH)]
        dxr = sum(cw_ref[j:j + 1, :] * ahead[j] for j in range(CONV_WIDTH))
        dbuf[tm:tm + SUBLANES, :] = dxc[0:SUBLANES, :]
        drz_ref[:, 0:LRU_WIDTH] = dxr.astype(BF16)
        drz_ref[:, LRU_WIDTH:] = dyr.astype(BF16)

        xr = xr_ref[...]
        upd = [jnp.sum(xr * ahead[j], axis=0, keepdims=True) for j in range(CONV_WIDTH)]
        upd += [jnp.sum(dxc, axis=0, keepdims=True), jnp.sum(dgr, axis=0, keepdims=True),
                jnp.sum(dgi, axis=0, keepdims=True), dlam]
        small_ref[...] += jnp.concatenate(upd, axis=0)

    rev = lambda s: nt - 1 - s
    halo = lambda s: jnp.maximum(rev(s) * per - 1, 0)
    cols = lambda k: pl.BlockSpec((tm, LRU_WIDTH), lambda s: (rev(s), k))
    halo0 = pl.BlockSpec((SUBLANES, LRU_WIDTH), lambda s: (halo(s), 0))
    bd = _full((2, LRU_HALF, LRU_HALF))
    big = pltpu.VMEM((tm + SUBLANES, LRU_WIDTH), F32)
    tile = pltpu.VMEM((tm, LRU_WIDTH), F32)
    kept_cols = [cols(k) for k in (KEPT_XC, KEPT_A, KEPT_MULT, KEPT_R, KEPT_I)]
    return _hosting_call(
        body, "rec_bwd", nt,
        [cols(0), cols(0), cols(1), cols(0)] + kept_cols
        + [halo0, _full((CONV_WIDTH, LRU_WIDTH)), bd, bd, _full((1, LRU_WIDTH))],
        [pl.BlockSpec((tm, 2 * LRU_WIDTH), lambda s: (rev(s), 0)), _full((SUBLANES, LRU_WIDTH)), bd, bd],
        [jax.ShapeDtypeStruct((rows, 2 * LRU_WIDTH), BF16), jax.ShapeDtypeStruct((SUBLANES, LRU_WIDTH), F32),
         jax.ShapeDtypeStruct((2, LRU_HALF, LRU_HALF), F32), jax.ShapeDtypeStruct((2, LRU_HALF, LRU_HALF), F32)],
        [big, big, tile, tile, big, pltpu.VMEM((SUBLANES, LRU_WIDTH), F32)],
        (drec, zrec, zrec, h) + (kept,) * N_KEPT + (h, conv_w, wa_bd, wx_bd, lam), carried, modes)


DZ_CUTS = (0, ATTN_WIDTH, QKV_WIDTH, IN_WIDTH)


def _dz_specs(tm):
    return [pl.BlockSpec((tm, DZ_CUTS[p + 1] - DZ_CUTS[p]), lambda i: (i, 0)) for p in range(3)]


def _in_proj_bwd_x(head, x, g1, dh1, dq, dkv, drz, w_in_t, carried, modes):
    rows = dh1.shape[0]
    tm = _row_tile(rows)
    steps = rows // tm

    def body(head_ref, g_ref, dh1_ref, dq_ref, dkv_ref, drz_ref, w_ref, x_hbm, dh0_ref, dg_ref, buf, sem):
        i = pl.program_id(0)
        h0 = _h0_tile(head_ref, x_hbm, buf, sem, i, steps, tm)

        @pl.when(i == 0)
        def _():
            dg_ref[...] = jnp.zeros_like(dg_ref)

        g = g_ref[...]
        _, xhat, rstd = _rms_fwd(h0, g)
        parts = (dq_ref[...], dkv_ref[...], drz_ref[...])
        du = sum(_mm(parts[p], w_ref[DZ_CUTS[p]:DZ_CUTS[p + 1], :]) for p in range(3))
        dx, dg = _rms_bwd(du, xhat, rstd, g)
        dh0_ref[...] = dh1_ref[...] + dx
        dg_ref[...] += dg

    wide = pl.BlockSpec((tm, D_MODEL), lambda i: (i, 0))
    return _hosting_call(
        body, "in_proj_bwd_x", steps,
        [_full((BLOCK, D_MODEL)), _full((1, D_MODEL)), wide] + _dz_specs(tm) + [_resident((IN_WIDTH, D_MODEL)), ANY_SPACE],
        [wide, _full((1, D_MODEL))],
        [jax.ShapeDtypeStruct((rows, D_MODEL), F32), jax.ShapeDtypeStruct((1, D_MODEL), F32)],
        _frame_scratch(tm), (head, g1, dh1, dq, dkv, drz, w_in_t, x), carried, modes)


def _in_proj_bwd_w(u1, dq, dkv, drz, carried, modes):
    rows = u1.shape[0]
    tb = _big_tile(rows)

    def body(u_ref, dq_ref, dkv_ref, drz_ref, dw_ref):
        @pl.when(pl.program_id(0) == 0)
        def _():
            dw_ref[...] = jnp.zeros_like(dw_ref)

        u = u_ref[...]
        for p, ref in enumerate((dq_ref, dkv_ref, drz_ref)):
            dw_ref[:, DZ_CUTS[p]:DZ_CUTS[p + 1]] += _mm_tn(u, ref[...])

    return _hosting_call(
        body, "in_proj_bwd_w", rows // tb,
        [pl.BlockSpec((tb, D_MODEL), lambda i: (i, 0))] + _dz_specs(tb),
        [_full((D_MODEL, IN_WIDTH))],
        [jax.ShapeDtypeStruct((D_MODEL, IN_WIDTH), F32)],
        [], (u1, dq, dkv, drz), carried, modes)


def _adamw_math(w, m, v, g):
    nm = ADAM_B1 * m + (1.0 - ADAM_B1) * g
    nv = ADAM_B2 * v + (1.0 - ADAM_B2) * (g * g)
    m_hat = nm / (1.0 - ADAM_B1 ** ADAM_STEP)
    v_hat = nv / (1.0 - ADAM_B2 ** ADAM_STEP)
    return (-ADAM_LR) * (m_hat / (jnp.sqrt(v_hat) + ADAM_EPS) + ADAM_WD * w), nm, nv


SMALL_NAMES = ("conv_b", "b_a", "b_x", "lru_lambda", "attn_sinks", "g_post_mix", "g_pre_ffn", "g_post_ffn")
PACK_WIDTH = 1024


def _pack_rows(vals):
    assert len(SMALL_NAMES) == SUBLANES
    row = lax.broadcasted_iota(jnp.int32, (SUBLANES, PACK_WIDTH), 0)
    tile = jnp.zeros((SUBLANES, PACK_WIDTH), F32)
    for k, name in enumerate(SMALL_NAMES):
        a = vals[name].reshape(1, -1)
        tile = jnp.where(row == k, jnp.pad(a, ((0, 0), (0, PACK_WIDTH - a.shape[1]))), tile)
    return tile


def _adamw_small(weights, mom_m, mom_v, parts, loss_parts):
    n = len(SMALL_NAMES)
    views = [(1, weights[name].size) for name in SMALL_NAMES]

    def body(*refs):
        w_refs, m_refs, v_refs = refs[:n], refs[n:2 * n], refs[2 * n:3 * n]
        p_ref, l_ref, loss_ref = refs[3 * n], refs[3 * n + 1], refs[3 * n + 2]
        outs = refs[3 * n + 3:]
        for k, (_, c) in enumerate(views):
            g = p_ref[0, k:k + 1, 0:c]
            for s in range(1, N_DEV):
                g = g + p_ref[s, k:k + 1, 0:c]
            g_ref, d_ref, nm_ref, nv_ref = outs[4 * k:4 * k + 4]
            g_ref[...] = g
            d_ref[...], nm_ref[...], nv_ref[...] = _adamw_math(w_refs[k][...], m_refs[k][...], v_refs[k][...], g)
        total = l_ref[0]
        for s in range(1, N_DEV):
            total = total + l_ref[s]
        loss_ref[...] = total

    args = [src[name].reshape(view) for src in (weights, mom_m, mom_v) for name, view in zip(SMALL_NAMES, views)]
    res = pl.pallas_call(
        body, name="adamw_small",
        out_shape=[jax.ShapeDtypeStruct(loss_parts.shape[1:], F32)]
                  + [jax.ShapeDtypeStruct(view, F32) for view in views for _ in range(4)],
        compiler_params=pltpu.CompilerParams(vmem_limit_bytes=VMEM_LIMIT),
    )(*args, parts, loss_parts)
    out = {name: tuple(t.reshape(weights[name].shape) for t in res[1 + 4 * k:5 + 4 * k]) for k, name in enumerate(SMALL_NAMES)}
    return res[0], out


def _adamw(w, m, v, parts, name):
    rows, cols = w.shape
    tr = next((t for t in (256, 128) if rows % t == 0), rows)
    parts = parts if isinstance(parts, (list, tuple)) else [parts]

    def body(w_ref, m_ref, v_ref, *refs):
        p_refs, (g_ref, d_ref, nm_ref, nv_ref) = refs[:len(parts)], refs[len(parts):]

        def total(p_ref):
            g = p_ref[0].astype(F32)
            for s in range(1, N_DEV):
                g = g + p_ref[s].astype(F32)
            return g

        g = jnp.concatenate([total(p_ref) for p_ref in p_refs], axis=1) if len(parts) > 1 else total(p_refs[0])
        g_ref[...] = g
        d_ref[...], nm_ref[...], nv_ref[...] = _adamw_math(w_ref[...], m_ref[...], v_ref[...], g)

    blk = pl.BlockSpec((tr, cols), lambda i: (i, 0))
    return pl.pallas_call(
        body, name=name, grid=(rows // tr,),
        in_specs=[blk, blk, blk] + [pl.BlockSpec((N_DEV, tr, p.shape[2]), lambda i: (0, i, 0)) for p in parts],
        out_specs=[blk] * 4,
        out_shape=[jax.ShapeDtypeStruct((rows, cols), F32)] * 4,
        compiler_params=_params(("parallel",)),
    )(w, m, v, *parts)


def _cols_from_shards(g):
    return jnp.transpose(g, (1, 0, 2)).reshape(g.shape[1], N_DEV * g.shape[2])


def _cols_to_shards(a):
    r, c = a.shape
    return jnp.transpose(a.reshape(r, N_DEV, c // N_DEV), (1, 0, 2))


def _block_diag(w):
    per = LRU_HALF // LRU_BLOCK
    w = w.reshape(2, per, LRU_BLOCK, LRU_BLOCK)
    eye = jnp.eye(per, dtype=w.dtype)
    return (w[:, :, :, None, :] * eye[None, :, None, :, None]).reshape(2, LRU_HALF, LRU_HALF)


def _block_diag_extract(t):
    per = LRU_HALF // LRU_BLOCK
    t = t.reshape(2, per, LRU_BLOCK, per, LRU_BLOCK)
    return jnp.stack([t[:, b, :, b, :] for b in range(per)], axis=1).reshape(LRU_BLOCKS, LRU_BLOCK, LRU_BLOCK)


def kernel(x, meta_tokens, g_pre_mix, w_in, conv_w, conv_b, w_a, b_a, w_x, b_x, lru_lambda, attn_sinks, w_out, g_post_mix, g_pre_ffn, w_ff1, w_ff2, g_post_ffn, loss_target, m_meta_tokens, m_g_pre_mix, m_w_in, m_conv_w, m_conv_b, m_w_a, m_b_a, m_w_x, m_b_x, m_lru_lambda, m_attn_sinks, m_w_out, m_g_post_mix, m_g_pre_ffn, m_w_ff1, m_w_ff2, m_g_post_ffn, v_meta_tokens, v_g_pre_mix, v_w_in, v_conv_w, v_conv_b, v_w_a, v_b_a, v_w_x, v_b_x, v_lru_lambda, v_attn_sinks, v_w_out, v_g_post_mix, v_g_pre_ffn, v_w_ff1, v_w_ff2, v_g_post_ffn):
    weights = dict(meta_tokens=meta_tokens, g_pre_mix=g_pre_mix, w_in=w_in, conv_w=conv_w, conv_b=conv_b, w_a=w_a,
                   b_a=b_a, w_x=w_x, b_x=b_x, lru_lambda=lru_lambda, attn_sinks=attn_sinks, w_out=w_out,
                   g_post_mix=g_post_mix, g_pre_ffn=g_pre_ffn, w_ff1=w_ff1, w_ff2=w_ff2, g_post_ffn=g_post_ffn)
    mom_m = dict(meta_tokens=m_meta_tokens, g_pre_mix=m_g_pre_mix, w_in=m_w_in, conv_w=m_conv_w, conv_b=m_conv_b,
                 w_a=m_w_a, b_a=m_b_a, w_x=m_w_x, b_x=m_b_x, lru_lambda=m_lru_lambda, attn_sinks=m_attn_sinks,
                 w_out=m_w_out, g_post_mix=m_g_post_mix, g_pre_ffn=m_g_pre_ffn, w_ff1=m_w_ff1, w_ff2=m_w_ff2,
                 g_post_ffn=m_g_post_ffn)
    mom_v = dict(meta_tokens=v_meta_tokens, g_pre_mix=v_g_pre_mix, w_in=v_w_in, conv_w=v_conv_w, conv_b=v_conv_b,
                 w_a=v_w_a, b_a=v_b_a, w_x=v_w_x, b_x=v_b_x, lru_lambda=v_lru_lambda, attn_sinks=v_attn_sinks,
                 w_out=v_w_out, g_post_mix=v_g_post_mix, g_pre_ffn=v_g_pre_ffn, w_ff1=v_w_ff1, w_ff2=v_w_ff2,
                 g_post_ffn=v_g_post_ffn)
    order = list(weights)

    (g_win, g_meta, g_cw) = _gather_two_level([w_in[0].astype(BF16), meta_tokens, conv_w[0]], "gather_first")
    w_in_full = _cols_from_shards(g_win)
    meta_full = _cols_from_shards(g_meta)
    conv_w_full = _cols_from_shards(g_cw)

    head = jnp.concatenate([jnp.zeros((PAD_ROWS, D_MODEL), F32), meta_full], axis=0)
    wa_bd = _block_diag(w_a[0]).astype(BF16)
    wx_bd = _block_diag(w_x[0]).astype(BF16)
    bias = _attn_bias()

    w1_shard = w_ff1[0].astype(BF16)
    (qkv, zrec, u1), (g_wout,) = _in_proj_fwd(head, x[0], g_pre_mix, w_in_full, [w_out[0].astype(BF16)], ["gather"])
    (attn,), (w1a,) = _attn_fwd(qkv, attn_sinks, bias, [w1_shard[:, :FF_HALF]], ["gather"])
    (rec, h_lru, kept), (w1b,) = _rec_fwd(zrec, conv_w_full, conv_b, wa_bd, b_a, wx_bd, b_x, lru_lambda,
                                         [w1_shard[:, FF_HALF:]], ["gather"])
    w_out_full = g_wout.reshape(D_MODEL, D_MODEL)
    w2_shard = w_ff2[0].astype(BF16)
    (mix, h1), (w2a,) = _out_proj_fwd(attn, rec, w_out_full, head, x[0], g_post_mix, [w2_shard[:FF_HALF]], ["gather"])
    (act, u2), (w2b,) = _ffn_up(h1, g_pre_ffn, (w1a, w1b), [w2_shard[FF_HALF:]], ["gather"])
    w2_halves = [w.reshape(D_FF // 2, D_MODEL) for w in (w2a, w2b)]
    (dy, df, dg_post_ffn, loss_acc), w2t_halves = _ffn_down_loss(
        act, w2_halves, h1, loss_target[0], g_post_ffn, [w2_shard[:FF_HALF].T, w2_shard[FF_HALF:].T], ["gather"] * 2)

    (da1,), w1t_gathered = _ffn_bwd_act(
        df, w2t_halves, act, [w1_shard[:, :FF_HALF].T, w1_shard[:, FF_HALF:].T], ["gather"] * 2)
    w1t_halves = [w.reshape(D_FF // 2, D_MODEL) for w in w1t_gathered]
    dw1h, dw2g = _ffn_bwd_weights(u2, da1, act, df)
    (dh1, dg_pre_ffn), (p_w1a,) = _ffn_bwd_x(da1, w1t_halves, h1, dy, g_pre_ffn, [dw1h[0]], ["scatter"])
    (dattn, drec, dw_out, dg_post_mix), (p_w1b,) = _out_proj_bwd(dh1, mix, g_post_mix, w_out_full.T, attn, rec,
                                                                [dw1h[1]], ["scatter"])
    (dq, dkv_late, dsinks), (p_w2,) = _attn_bwd(qkv, dattn, attn_sinks, bias, [dw2g], ["scatter"])
    dkv = dkv_late[BLOCK:BLOCK + qkv.shape[0]]
    (drz, rec_small, dwa_bd, dwx_bd), (p_wout,) = _rec_bwd(
        drec, zrec, h_lru, kept, conv_w_full, wa_bd, wx_bd, lru_lambda,
        [dw_out.reshape(N_DEV, D_MODEL // N_DEV, D_MODEL)], ["scatter"])
    small_grads = dict(
        conv_b=rec_small[ROW_CONV_B], b_a=rec_small[ROW_B_A], b_x=rec_small[ROW_B_X], lru_lambda=rec_small[ROW_LAMBDA],
        attn_sinks=dsinks[:, 0], g_post_mix=dg_post_mix, g_pre_ffn=dg_pre_ffn, g_post_ffn=dg_post_ffn)
    gate_rows = (LRU_BLOCKS * LRU_BLOCK, LRU_BLOCK)
    (dw_in,), (p_cw, p_small, p_wa, p_wx) = _in_proj_bwd_w(
        u1, dq, dkv, drz,
        [_cols_to_shards(rec_small[0:CONV_WIDTH]), _pack_rows(small_grads),
         _block_diag_extract(dwa_bd).reshape(gate_rows), _block_diag_extract(dwx_bd).reshape(gate_rows)],
        ["scatter", "gather", "gather", "gather"])
    (dh0, dg_pre_mix), (p_win,) = _in_proj_bwd_x(
        head, x[0], g_pre_mix, dh1, dq, dkv, drz, w_in_full.T, [_cols_to_shards(dw_in).astype(BF16)], ["scatter"])
    p_meta, p_gpm, p_loss = _exchange([_cols_to_shards(dh0[PAD_ROWS:BLOCK]), dg_pre_mix, loss_acc],
                                      ["scatter", "gather", "gather"], "exchange_last")

    res = {}
    res["g_pre_mix"] = _adamw(g_pre_mix, m_g_pre_mix, v_g_pre_mix, p_gpm, "adamw_g_pre_mix")
    res["w_in"] = _adamw(w_in[0], m_w_in[0], v_w_in[0], p_win, "adamw_w_in")
    res["w_out"] = _adamw(w_out[0], m_w_out[0], v_w_out[0], p_wout, "adamw_w_out")
    res["w_ff1"] = _adamw(w_ff1[0], m_w_ff1[0], v_w_ff1[0], [p_w1a, p_w1b], "adamw_w_ff1")
    res["w_ff2"] = _adamw(w_ff2[0], m_w_ff2[0], v_w_ff2[0], p_w2, "adamw_w_ff2")
    res["meta_tokens"] = _adamw(meta_tokens, m_meta_tokens, v_meta_tokens, p_meta, "adamw_meta")
    res["conv_w"] = _adamw(conv_w[0], m_conv_w[0], v_conv_w[0], p_cw, "adamw_conv_w")
    for name in ("w_in", "w_out", "w_ff1", "w_ff2", "conv_w"):
        res[name] = tuple(t[None] for t in res[name])
    for name, parts in (("w_a", p_wa), ("w_x", p_wx)):
        gate = _adamw(*(src[name].reshape(gate_rows) for src in (weights, mom_m, mom_v)), parts, "adamw_" + name)
        res[name] = tuple(t.reshape(weights[name].shape) for t in gate)
    loss_total, small = _adamw_small(weights, mom_m, mom_v, p_small, p_loss)
    res.update(small)

    grad_x = dh0[BLOCK:][None]
    outs = [loss_total[0, 0], grad_x]
    for k in range(4):
        outs += [res[name][k] for name in order]
    return tuple(outs)
```

```python
import jax
import jax.numpy as jnp
import numpy as np
from jax import lax
from jax.experimental import pallas as pl
from jax.experimental.pallas import tpu as pltpu

F32 = jnp.float32
BF16 = jnp.bfloat16

D_MODEL = 1024
N_META = 16
HEAD_DIM = 64
ATTN_HEADS = 8
KV_HEADS = 2
GQA_GROUP = ATTN_HEADS // KV_HEADS
ATTN_WIDTH = ATTN_HEADS * HEAD_DIM
KV_WIDTH = KV_HEADS * HEAD_DIM
QKV_WIDTH = ATTN_WIDTH + 2 * KV_WIDTH
LRU_WIDTH = 512
LRU_BLOCKS = 8
LRU_BLOCK = 64
LRU_HALF = 256
LRU_C = 8.0
CONV_WIDTH = 4
BLOCK = 128
PAD_ROWS = BLOCK - N_META
IN_WIDTH = QKV_WIDTH + 2 * LRU_WIDTH
D_FF = 4096
EPS = 1e-6
NEG = -1e30
N_DEV = 8
FF_CHUNK = D_FF // N_DEV
SUBLANES = 8
LANES = 128

ADAM_LR = 0.001
ADAM_B1 = 0.9
ADAM_B2 = 0.999
ADAM_EPS = 1e-08
ADAM_WD = 0.01
ADAM_STEP = 10

VMEM_LIMIT = 56 * 1024 * 1024


def _row_tile(rows):
    for t in (640, 512, 256, 128):
        if rows % t == 0:
            return t
    raise ValueError(rows)


def _big_tile(rows):
    for t in (1664, 1024, 512, 256, 128):
        if rows % t == 0:
            return t
    raise ValueError(rows)


def _rec_tile(rows):
    for t in (320, 256, 128):
        if rows % t == 0:
            return t
    raise ValueError(rows)


def _params(semantics):
    return pltpu.CompilerParams(dimension_semantics=semantics, vmem_limit_bytes=VMEM_LIMIT)


def _mm(a, b):
    return lax.dot_general(a, b, (((1,), (0,)), ((), ())), preferred_element_type=F32)


def _mm_nt(a, b):
    return lax.dot_general(a, b, (((1,), (1,)), ((), ())), preferred_element_type=F32)


def _mm_tn(a, b):
    return lax.dot_general(a, b, (((0,), (0,)), ((), ())), preferred_element_type=F32)


def _rms_fwd(x, g):
    rstd = lax.rsqrt(jnp.mean(x * x, axis=-1, keepdims=True) + EPS)
    xhat = x * rstd
    return xhat * g, xhat, rstd


def _rms_bwd(dy, xhat, rstd, g):
    dyg = dy * g
    c = jnp.mean(dyg * xhat, axis=-1, keepdims=True)
    dx = rstd * (dyg - xhat * c)
    dg = jnp.sum(dy * xhat, axis=0, keepdims=True)
    return dx, dg


def _sigmoid(x):
    return 0.5 * jnp.tanh(0.5 * x) + 0.5


def _log1p(x):
    u = 1.0 + x
    return jnp.where(u == 1.0, x, jnp.log(u) * x / (u - 1.0))


def _one_minus_sq_exp(x, ex):
    return -jnp.tanh(x) * (1.0 + ex * ex)


TINY = 1e-30


def _sqrt_pos(y):
    r = lax.rsqrt(jnp.maximum(y, TINY))
    return y * r, r


def _softplus(x):
    return jnp.maximum(x, 0.0) + _log1p(jnp.exp(-jnp.abs(x)))


GELU_C = 0.7978845608028654
GELU_K = 0.044715


def _gelu(x):
    t = jnp.tanh(GELU_C * (x + GELU_K * x * x * x))
    return 0.5 * x * (1.0 + t), t


def _gelu_grad(x, t):
    return 0.5 * (1.0 + t) + 0.5 * x * (1.0 - t * t) * GELU_C * (1.0 + 3.0 * GELU_K * x * x)


def _full(shape):
    return pl.BlockSpec(shape, lambda *_: (0,) * len(shape))


def _resident(shape):
    return pl.BlockSpec(shape, lambda *_: (0,) * len(shape), pipeline_mode=pl.Buffered(1))


def _exchange_copies(ins, outs, sems, modes):
    send_sems, recv_sems, local_sems = sems
    x, y, c = lax.axis_index("x"), lax.axis_index("y"), lax.axis_index("c")
    me = 4 * x + 2 * y + c

    def block(a, dev):
        return ins[a] if modes[a] == "gather" else ins[a].at[dev]

    local = [pltpu.make_async_copy(block(a, me), outs[a].at[me], local_sems.at[a]) for a in range(len(ins))]
    sends, recvs = [], []
    for a in range(len(ins)):
        for k in range(N_DEV - 1):
            bits = k + 1
            px = jnp.bitwise_xor(x, (bits >> 2) & 1)
            py = jnp.bitwise_xor(y, (bits >> 1) & 1)
            pc = jnp.bitwise_xor(c, bits & 1)
            peer = 4 * px + 2 * py + pc
            common = dict(src_ref=block(a, peer), send_sem=send_sems.at[a, k], recv_sem=recv_sems.at[a, k],
                          device_id=(px, py, pc), device_id_type=pl.DeviceIdType.MESH)
            sends.append(pltpu.make_async_remote_copy(dst_ref=outs[a].at[me], **common))
            recvs.append(pltpu.make_async_remote_copy(dst_ref=outs[a].at[peer], **common))
    return local, sends, recvs


def _exchange_start(ins, outs, sems, modes):
    local, sends, _ = _exchange_copies(ins, outs, sems, modes)
    for cp in local + sends:
        cp.start()


def _exchange_wait(ins, outs, sems, modes):
    local, sends, recvs = _exchange_copies(ins, outs, sems, modes)
    for cp in recvs:
        cp.wait_recv()
    for cp in sends:
        cp.wait_send()
    for cp in local:
        cp.wait()


def _exchange_shapes(arrays, modes):
    return [jax.ShapeDtypeStruct((N_DEV,) + a.shape if mode == "gather" else a.shape, a.dtype)
            for a, mode in zip(arrays, modes)]


def _exchange_sems(na):
    return [pltpu.SemaphoreType.DMA((na, N_DEV - 1)), pltpu.SemaphoreType.DMA((na, N_DEV - 1)),
            pltpu.SemaphoreType.DMA((na,))]


ANY_SPACE = pl.BlockSpec(memory_space=pl.ANY)


def _exchange(arrays, modes, name):
    na = len(arrays)

    def body(*refs):
        ins, outs, sems = refs[:na], refs[na:2 * na], refs[2 * na:]
        _exchange_start(ins, outs, sems, modes)
        _exchange_wait(ins, outs, sems, modes)

    return pl.pallas_call(
        body, name=name, out_shape=_exchange_shapes(arrays, modes),
        in_specs=[ANY_SPACE] * na, out_specs=[ANY_SPACE] * na, scratch_shapes=_exchange_sems(na),
        compiler_params=pltpu.CompilerParams(has_side_effects=True),
    )(*arrays)


def _gather_two_level(arrays, name):
    na = len(arrays)

    def body(*refs):
        ins, outs = refs[:na], refs[na:2 * na]
        send_sems, recv_sems, local_sems = refs[2 * na:]
        x, y, c = lax.axis_index("x"), lax.axis_index("y"), lax.axis_index("c")
        me, sibling = (x, y, c), (x, y, 1 - c)
        chips = [(1 - x, y), (x, 1 - y), (1 - x, 1 - y)]

        def copy(a, k, block, to, src=None):
            slot = outs[a].at[4 * block[0] + 2 * block[1] + block[2]]
            return pltpu.make_async_remote_copy(
                src_ref=slot if src is None else src, dst_ref=slot, send_sem=send_sems.at[a, k],
                recv_sem=recv_sems.at[a, k], device_id=to, device_id_type=pl.DeviceIdType.MESH)

        local = [pltpu.make_async_copy(ins[a], outs[a].at[4 * x + 2 * y + c], local_sems.at[a]) for a in range(na)]
        first = []
        for a in range(na):
            first.append(copy(a, 0, me, sibling, src=ins[a]))
            first += [copy(a, 1 + j, me, (*chip, c), src=ins[a]) for j, chip in enumerate(chips)]
        for cp in local + first:
            cp.start()
        passed = []
        for j, chip in enumerate(chips):
            for a in range(na):
                copy(a, 1 + j, (*chip, c), me).wait_recv()
                passed.append(copy(a, 4 + j, (*chip, c), sibling))
                passed[-1].start()
        for a in range(na):
            copy(a, 0, sibling, me).wait_recv()
            for j, chip in enumerate(chips):
                copy(a, 4 + j, (*chip, 1 - c), me).wait_recv()
        for cp in first + passed:
            cp.wait_send()
        for cp in local:
            cp.wait()

    return pl.pallas_call(
        body, name=name, out_shape=_exchange_shapes(arrays, ["gather"] * na),
        in_specs=[ANY_SPACE] * na, out_specs=[ANY_SPACE] * na, scratch_shapes=_exchange_sems(na),
        compiler_params=pltpu.CompilerParams(has_side_effects=True),
    )(*arrays)


def _hosting_call(body, name, steps, in_specs, out_specs, out_shape, scratch_shapes, args, arrays, modes):
    n_in, n_out, n_scr, na = len(in_specs), len(out_specs), len(scratch_shapes), len(arrays)
    grid = steps if isinstance(steps, tuple) else (steps,)

    def hosting_body(*refs):
        cuts = [0]
        for n in (n_in, na, n_out, na, n_scr, 3):
            cuts.append(cuts[-1] + n)
        ins, x_ins, outs, x_outs, scr, sems = (refs[cuts[p]:cuts[p + 1]] for p in range(6))
        first, last = True, True
        for axis, n in enumerate(grid):
            first = first & (pl.program_id(axis) == 0)
            last = last & (pl.program_id(axis) == n - 1)

        @pl.when(first)
        def _():
            _exchange_start(x_ins, x_outs, sems, modes)

        body(*ins, *outs, *scr)

        @pl.when(last)
        def _():
            _exchange_wait(x_ins, x_outs, sems, modes)

    res = pl.pallas_call(
        hosting_body, name=name, grid=grid,
        in_specs=list(in_specs) + [ANY_SPACE] * na, out_specs=list(out_specs) + [ANY_SPACE] * na,
        out_shape=list(out_shape) + _exchange_shapes(arrays, modes),
        scratch_shapes=list(scratch_shapes) + _exchange_sems(na),
        compiler_params=_params(("arbitrary",) * len(grid)),
    )(*args, *arrays)
    return res[:n_out], res[n_out:]


def _frame_rows(src_hbm, buf, sem, i, steps, tm):
    def first():
        return pltpu.make_async_copy(src_hbm.at[pl.ds(0, tm - BLOCK)], buf.at[0, pl.ds(BLOCK, tm - BLOCK)], sem.at[0])

    def later(t, slot):
        return pltpu.make_async_copy(src_hbm.at[pl.ds(pl.multiple_of(t * tm - BLOCK, BLOCK), tm)], buf.at[slot], sem.at[slot])

    slot = i % 2

    @pl.when(i == 0)
    def _():
        first().start()

    @pl.when(i + 1 < steps)
    def _():
        later(i + 1, 1 - slot).start()

    @pl.when(i == 0)
    def _():
        first().wait()

    @pl.when(i > 0)
    def _():
        later(i, slot).wait()

    return slot


def _frame_scratch(tm):
    return [pltpu.VMEM((2, tm, D_MODEL), F32), pltpu.SemaphoreType.DMA((2,))]


def _h0_tile(head_ref, x_hbm, buf, sem, i, steps, tm):
    slot = _frame_rows(x_hbm, buf, sem, i, steps, tm)

    @pl.when(i == 0)
    def _():
        buf[0, 0:BLOCK, :] = head_ref[...]

    return buf[slot]


def _in_proj_fwd(head, x, g1, w_in, carried, modes):
    rows = BLOCK + x.shape[0]
    tm = _row_tile(rows)
    steps = rows // tm

    def body(head_ref, g_ref, w_ref, x_hbm, qkv_ref, zrec_ref, u_ref, buf, sem):
        h = _h0_tile(head_ref, x_hbm, buf, sem, pl.program_id(0), steps, tm)
        u, _, _ = _rms_fwd(h, g_ref[...])
        u = u.astype(BF16)
        u_ref[...] = u
        z = _mm(u, w_ref[...])
        qkv_ref[...] = z[:, :QKV_WIDTH].astype(BF16)
        zrec_ref[...] = z[:, QKV_WIDTH:]

    wide = pl.BlockSpec((tm, D_MODEL), lambda i: (i, 0))
    return _hosting_call(
        body, "in_proj_fwd", steps,
        [_full((BLOCK, D_MODEL)), _full((1, D_MODEL)), _resident((D_MODEL, IN_WIDTH)), ANY_SPACE],
        [pl.BlockSpec((tm, QKV_WIDTH), lambda i: (i, 0)), pl.BlockSpec((tm, 2 * LRU_WIDTH), lambda i: (i, 0)), wide],
        [jax.ShapeDtypeStruct((rows, QKV_WIDTH), BF16), jax.ShapeDtypeStruct((rows, 2 * LRU_WIDTH), F32),
         jax.ShapeDtypeStruct((rows, D_MODEL), BF16)],
        _frame_scratch(tm), (head, g1, w_in, x), carried, modes)


N_BIAS = 3


def _attn_bias():
    key = np.arange(2 * BLOCK)[:, None]
    r = np.arange(GQA_GROUP * BLOCK)[None, :] % BLOCK
    band = (key > r) & (key <= r + BLOCK)
    out = [np.where(band & ((n - 1) * BLOCK + key >= PAD_ROWS), 0.0, NEG) for n in range(N_BIAS)]
    return jnp.asarray(np.stack(out), F32)


def _attn_probs(k2, q4, bias, sink_row):
    s = _mm_nt(k2, q4) * (HEAD_DIM ** -0.5) + bias
    m = jnp.maximum(jnp.max(s, axis=0, keepdims=True), sink_row)
    p = jnp.exp(s - m)
    es = jnp.exp(sink_row - m)
    inv = 1.0 / (jnp.sum(p, axis=0, keepdims=True) + es)
    return p * inv, es * inv


def _heads(ref, rows, first, count):
    return jnp.concatenate([ref[rows, (first + g) * HEAD_DIM:(first + g + 1) * HEAD_DIM] for g in range(count)], axis=0)


def _keys_of_block(prev_ref, cur_ref, b, kv):
    sl = slice(kv * HEAD_DIM, (kv + 1) * HEAD_DIM)
    before = prev_ref[:, sl] if b == 0 else cur_ref[(b - 1) * BLOCK:b * BLOCK, sl]
    return jnp.concatenate([before, cur_ref[b * BLOCK:(b + 1) * BLOCK, sl]], axis=0)


def _bias_of_block(bias_ref, block):
    return bias_ref[jnp.minimum(block, N_BIAS - 1)]


def _sink_row(sink_ref, kv):
    g = lax.broadcasted_iota(jnp.int32, (1, GQA_GROUP * BLOCK), 1) // BLOCK
    row = jnp.full((1, GQA_GROUP * BLOCK), sink_ref[0, kv * GQA_GROUP], F32)
    for i in range(1, GQA_GROUP):
        row = jnp.where(g == i, sink_ref[0, kv * GQA_GROUP + i], row)
    return row


def _from_head_major(pieces):
    return jnp.concatenate(pieces, axis=0).T


def _attn_specs(tm, tile_of):
    nbt = tm // BLOCK
    k_col, v_col = ATTN_WIDTH // KV_WIDTH, ATTN_WIDTH // KV_WIDTH + 1
    before = lambda i: jnp.maximum(tile_of(i) * nbt - 1, 0)
    return [pl.BlockSpec((tm, ATTN_WIDTH), lambda i: (tile_of(i), 0)),
            pl.BlockSpec((BLOCK, KV_WIDTH), lambda i: (before(i), k_col)),
            pl.BlockSpec((tm, KV_WIDTH), lambda i: (tile_of(i), k_col)),
            pl.BlockSpec((BLOCK, KV_WIDTH), lambda i: (before(i), v_col)),
            pl.BlockSpec((tm, KV_WIDTH), lambda i: (tile_of(i), v_col))]


def _attn_fwd(qkv, sinks, bias, carried, modes):
    rows = qkv.shape[0]
    tm = _row_tile(rows)
    nbt = tm // BLOCK

    def body(sink_ref, bias_ref, q_ref, kp_ref, kc_ref, vp_ref, vc_ref, o_ref):
        i = pl.program_id(0)
        for b in range(nbt):
            blk = slice(b * BLOCK, (b + 1) * BLOCK)
            bias_t = _bias_of_block(bias_ref, i * nbt + b)
            pieces = []
            for kv in range(KV_HEADS):
                k2 = _keys_of_block(kp_ref, kc_ref, b, kv)
                v2 = _keys_of_block(vp_ref, vc_ref, b, kv)
                q4 = _heads(q_ref, blk, kv * GQA_GROUP, GQA_GROUP)
                pn, _ = _attn_probs(k2, q4, bias_t, _sink_row(sink_ref, kv))
                ot = _mm_tn(v2, pn.astype(BF16))
                pieces += [ot[:, g * BLOCK:(g + 1) * BLOCK] for g in range(GQA_GROUP)]
            o_ref[blk, :] = _from_head_major(pieces).astype(BF16)

    return _hosting_call(
        body, "attn_fwd", rows // tm,
        [pl.BlockSpec(memory_space=pltpu.SMEM), _resident((N_BIAS, 2 * BLOCK, GQA_GROUP * BLOCK))]
        + _attn_specs(tm, lambda i: i),
        [pl.BlockSpec((tm, ATTN_WIDTH), lambda i: (i, 0))],
        [jax.ShapeDtypeStruct((rows, ATTN_WIDTH), BF16)],
        [], (sinks, bias, qkv, qkv, qkv, qkv, qkv), carried, modes)


def _conv_taps(xbuf, tm):
    return [xbuf[pl.ds(SUBLANES - (CONV_WIDTH - 1 - j), tm), :] for j in range(CONV_WIDTH)]


def _lru_halves(xc):
    return [xc[:, h * LRU_HALF:(h + 1) * LRU_HALF].astype(BF16) for h in range(2)]


def _lru_gates(xc, wa_ref, ba_ref, wx_ref, bx_ref, lam_ref):
    halves = _lru_halves(xc)
    gate_r = jnp.concatenate([_mm(halves[h], wa_ref[h]) for h in range(2)], axis=1) + ba_ref[...]
    gate_i = jnp.concatenate([_mm(halves[h], wx_ref[h]) for h in range(2)], axis=1) + bx_ref[...]
    r = _sigmoid(gate_r)
    ig = _sigmoid(gate_i)
    log_a = (-LRU_C) * r * _softplus(-lam_ref[...])
    a = jnp.exp(log_a)
    mult, _ = _sqrt_pos(_one_minus_sq_exp(log_a, a))
    return r, ig, a, mult


KEPT_XC, KEPT_A, KEPT_MULT, KEPT_R, KEPT_I, N_KEPT = 0, 1, 2, 3, 4, 5


def _scan_tile(a_ref, u_ref, out_ref, carry, tm, reverse):
    row = lax.broadcasted_iota(jnp.int32, (SUBLANES, LRU_WIDTH), 0)
    groups = tm // SUBLANES

    def step(j, prev):
        jj = groups - 1 - j if reverse else j
        o = pl.multiple_of(jj * SUBLANES, SUBLANES)
        a = a_ref[pl.ds(o, SUBLANES), :]
        u = u_ref[pl.ds(o, SUBLANES), :]
        for s in (1, 2, 4):
            shift = SUBLANES - s if reverse else s
            keep = (row < SUBLANES - s) if reverse else (row >= s)
            u = jnp.where(keep, a * pltpu.roll(u, shift, 0) + u, u)
            a = jnp.where(keep, a * pltpu.roll(a, shift, 0), a)
        out = a * prev + u
        out_ref[pl.ds(o, SUBLANES), :] = out
        return out[0:1, :] if reverse else out[SUBLANES - 1:SUBLANES, :]

    return lax.fori_loop(0, groups, step, carry)


def _rec_fwd(zrec, conv_w, conv_b, wa_bd, b_a, wx_bd, b_x, lam, carried, modes):
    rows = zrec.shape[0]
    tm = _row_tile(rows)

    def body(xr_ref, yr_ref, cw_ref, cb_ref, wa_ref, ba_ref, wx_ref, bx_ref, lam_ref, rec_ref, h_ref, kept_ref,
             xbuf, a_s, u_s, carry):
        i = pl.program_id(0)

        @pl.when(i == 0)
        def _():
            xbuf[0:SUBLANES, :] = jnp.zeros((SUBLANES, LRU_WIDTH), F32)
            carry[...] = jnp.zeros_like(carry)

        @pl.when(i > 0)
        def _():
            xbuf[0:SUBLANES, :] = xbuf[tm:tm + SUBLANES, :]

        xbuf[SUBLANES:SUBLANES + tm, :] = xr_ref[...]
        taps = _conv_taps(xbuf, tm)
        xc = cb_ref[...] + sum(cw_ref[j:j + 1, :] * taps[j] for j in range(CONV_WIDTH))
        r, ig, a, mult = _lru_gates(xc, wa_ref, ba_ref, wx_ref, bx_ref, lam_ref)
        for k, val in ((KEPT_XC, xc), (KEPT_A, a), (KEPT_MULT, mult), (KEPT_R, r), (KEPT_I, ig)):
            kept_ref[:, k * LRU_WIDTH:(k + 1) * LRU_WIDTH] = val
        grow = i * tm + lax.broadcasted_iota(jnp.int32, (tm, LRU_WIDTH), 0)
        a_s[...] = a
        u_s[...] = jnp.where(grow >= PAD_ROWS, mult * (ig * xc), 0.0)
        carry[0:1, :] = _scan_tile(a_s, u_s, h_ref, carry[0:1, :], tm, reverse=False)
        gel, _ = _gelu(yr_ref[...])
        rec_ref[...] = (gel * h_ref[...]).astype(BF16)

    vec = _full((1, LRU_WIDTH))
    bd = _full((2, LRU_HALF, LRU_HALF))
    return _hosting_call(
        body, "rec_fwd", rows // tm,
        [pl.BlockSpec((tm, LRU_WIDTH), lambda i: (i, 0)), pl.BlockSpec((tm, LRU_WIDTH), lambda i: (i, 1)),
         _full((CONV_WIDTH, LRU_WIDTH)), vec, bd, vec, bd, vec, vec],
        [pl.BlockSpec((tm, LRU_WIDTH), lambda i: (i, 0))] * 2 + [pl.BlockSpec((tm, N_KEPT * LRU_WIDTH), lambda i: (i, 0))],
        [jax.ShapeDtypeStruct((rows, LRU_WIDTH), BF16), jax.ShapeDtypeStruct((rows, LRU_WIDTH), F32),
         jax.ShapeDtypeStruct((rows, N_KEPT * LRU_WIDTH), F32)],
        [pltpu.VMEM((tm + SUBLANES, LRU_WIDTH), F32), pltpu.VMEM((tm, LRU_WIDTH), F32),
         pltpu.VMEM((tm, LRU_WIDTH), F32), pltpu.VMEM((SUBLANES, LRU_WIDTH), F32)],
        (zrec, zrec, conv_w, conv_b, wa_bd, b_a, wx_bd, b_x, lam), carried, modes)


def _out_proj_fwd(attn, rec, w_out, head, x, g2, carried, modes):
    rows = attn.shape[0]
    tm = _row_tile(rows)
    steps = rows // tm

    def body(attn_ref, rec_ref, w_ref, head_ref, g_ref, x_hbm, mix_ref, h1_ref, buf, sem):
        h0 = _h0_tile(head_ref, x_hbm, buf, sem, pl.program_id(0), steps, tm)
        mix = _mm(attn_ref[...], w_ref[0:ATTN_WIDTH, :]) + _mm(rec_ref[...], w_ref[ATTN_WIDTH:, :])
        y, _, _ = _rms_fwd(mix, g_ref[...])
        mix_ref[...] = mix
        h1_ref[...] = h0 + y

    half = pl.BlockSpec((tm, ATTN_WIDTH), lambda i: (i, 0))
    wide = pl.BlockSpec((tm, D_MODEL), lambda i: (i, 0))
    return _hosting_call(
        body, "out_proj_fwd", steps,
        [half, half, _resident((D_MODEL, D_MODEL)), _full((BLOCK, D_MODEL)), _full((1, D_MODEL)), ANY_SPACE],
        [wide, wide],
        [jax.ShapeDtypeStruct((rows, D_MODEL), F32)] * 2,
        _frame_scratch(tm), (attn, rec, w_out, head, g2, x), carried, modes)


FF_COLS = 1024
FF_HALF = FF_CHUNK // 2


def _hidden_at(d, half):
    return half * (D_FF // 2) + d * FF_HALF


def _ffn_up(h1, g3, w1_halves, carried, modes):
    rows = h1.shape[0]
    tm = _row_tile(rows)

    def body(h_ref, g_ref, wa_ref, wb_ref, act_ref, u_ref):
        u, _, _ = _rms_fwd(h_ref[...], g_ref[...])
        u = u.astype(BF16)
        u_ref[...] = u
        for half, w_ref in enumerate((wa_ref, wb_ref)):
            for d in range(N_DEV):
                c = _hidden_at(d, half)
                a1 = jnp.maximum(_mm(u, w_ref[d]), 0.0)
                act_ref[:, c:c + FF_HALF] = (a1 * a1).astype(BF16)

    wide = pl.BlockSpec((tm, D_MODEL), lambda i: (i, 0))
    return _hosting_call(
        body, "ffn_up", rows // tm,
        [wide, _full((1, D_MODEL))] + [_resident((N_DEV, D_MODEL, FF_HALF))] * 2,
        [pl.BlockSpec((tm, D_FF), lambda i: (i, 0)), wide],
        [jax.ShapeDtypeStruct((rows, D_FF), BF16), jax.ShapeDtypeStruct((rows, D_MODEL), BF16)],
        [], (h1, g3, *w1_halves), carried, modes)


def _ffn_down_loss(act, w2_halves, h1, target, g4, carried, modes):
    rows = h1.shape[0]
    tm = _row_tile(rows)
    steps = rows // tm
    kh = D_FF // 2

    def body(act_ref, wa_ref, wb_ref, h_ref, g_ref, t_hbm, dy_ref, df_ref, dg_ref, loss_ref, buf, sem):
        i = pl.program_id(0)
        slot = _frame_rows(t_hbm, buf, sem, i, steps, tm)

        @pl.when(i == 0)
        def _():
            dg_ref[...] = jnp.zeros_like(dg_ref)
            loss_ref[...] = jnp.zeros_like(loss_ref)
            buf[0, 0:BLOCK, :] = jnp.zeros((BLOCK, D_MODEL), F32)

        g = g_ref[...]
        f = _mm(act_ref[:, :kh], wa_ref[...]) + _mm(act_ref[:, kh:], wb_ref[...])
        y, fhat, rstd = _rms_fwd(f, g)
        grow = i * tm + lax.broadcasted_iota(jnp.int32, (tm, D_MODEL), 0)
        err = jnp.where(grow >= BLOCK, h_ref[...] + y - buf[slot], 0.0)
        loss_ref[...] += (0.5 / D_MODEL) * jnp.sum(err * err)
        dy = err * (1.0 / D_MODEL)
        df, dg = _rms_bwd(dy, fhat, rstd, g)
        dy_ref[...] = dy
        df_ref[...] = df.astype(BF16)
        dg_ref[...] += dg

    wide = pl.BlockSpec((tm, D_MODEL), lambda i: (i, 0))
    return _hosting_call(
        body, "ffn_down_loss", steps,
        [pl.BlockSpec((tm, D_FF), lambda i: (i, 0)), _resident((kh, D_MODEL)), _resident((kh, D_MODEL)), wide,
         _full((1, D_MODEL)), ANY_SPACE],
        [wide, wide, _full((1, D_MODEL)), _full((SUBLANES, LANES))],
        [jax.ShapeDtypeStruct((rows, D_MODEL), F32), jax.ShapeDtypeStruct((rows, D_MODEL), BF16),
         jax.ShapeDtypeStruct((1, D_MODEL), F32), jax.ShapeDtypeStruct((SUBLANES, LANES), F32)],
        _frame_scratch(tm), (act, *w2_halves, h1, g4, target), carried, modes)


def _ffn_bwd_act(df, w2t_halves, act, carried, modes):
    rows = df.shape[0]
    tm = _row_tile(rows)

    def body(df_ref, wa_ref, wb_ref, act_ref, da_ref):
        df_t = df_ref[...]
        for half, w_ref in enumerate((wa_ref, wb_ref)):
            for d in range(N_DEV):
                cols = slice(_hidden_at(d, half), _hidden_at(d, half) + FF_HALF)
                dact = _mm(df_t, w_ref[d])
                relu_a1, _ = _sqrt_pos(act_ref[:, cols].astype(F32))
                da_ref[:, cols] = (dact * (2.0 * relu_a1)).astype(BF16)

    hidden = pl.BlockSpec((tm, D_FF), lambda i: (i, 0))
    return _hosting_call(
        body, "ffn_bwd_act", rows // tm,
        [pl.BlockSpec((tm, D_MODEL), lambda i: (i, 0))] + [_resident((N_DEV, D_MODEL, FF_HALF))] * 2 + [hidden],
        [hidden],
        [jax.ShapeDtypeStruct((rows, D_FF), BF16)],
        [], (df, *w2t_halves, act), carried, modes)


def _ffn_bwd_x(da, w1t_halves, h1, dy, g3, carried, modes):
    rows = h1.shape[0]
    tm = _row_tile(rows)
    kh = D_FF // 2

    def body(da_ref, wa_ref, wb_ref, h_ref, dy_ref, g_ref, dh_ref, dg_ref):
        @pl.when(pl.program_id(0) == 0)
        def _():
            dg_ref[...] = jnp.zeros_like(dg_ref)

        g = g_ref[...]
        _, xhat, rstd = _rms_fwd(h_ref[...], g)
        du = _mm(da_ref[:, :kh], wa_ref[...]) + _mm(da_ref[:, kh:], wb_ref[...])
        dx, dg = _rms_bwd(du, xhat, rstd, g)
        dh_ref[...] = dy_ref[...] + dx
        dg_ref[...] += dg

    wide = pl.BlockSpec((tm, D_MODEL), lambda i: (i, 0))
    return _hosting_call(
        body, "ffn_bwd_x", rows // tm,
        [pl.BlockSpec((tm, D_FF), lambda i: (i, 0)), _resident((kh, D_MODEL)), _resident((kh, D_MODEL)), wide, wide,
         _full((1, D_MODEL))],
        [wide, _full((1, D_MODEL))],
        [jax.ShapeDtypeStruct((rows, D_MODEL), F32), jax.ShapeDtypeStruct((1, D_MODEL), F32)],
        [], (da, *w1t_halves, h1, dy, g3), carried, modes)


def _ffn_bwd_weights(u2, da, act, df, carried, modes):
    rows = u2.shape[0]
    tb = _big_tile(rows)
    steps = rows // tb
    per = FF_COLS // FF_HALF

    def body(u_ref, da_ref, act_ref, df_ref, dw1_ref, dw2_ref, acc1, acc2):
        i = pl.program_id(1)

        @pl.when(i == 0)
        def _():
            acc1[...] = jnp.zeros_like(acc1)
            acc2[...] = jnp.zeros_like(acc2)

        acc1[...] += _mm_tn(u_ref[...], da_ref[...])
        acc2[...] += _mm_tn(act_ref[...], df_ref[...])

        @pl.when(i == steps - 1)
        def _():
            for p in range(per):
                c = p * FF_HALF
                dw1_ref[p] = acc1[:, c:c + FF_HALF].astype(BF16)
                dw2_ref[p] = acc2[c:c + FF_HALF, :].astype(BF16)

    wide = pl.BlockSpec((tb, D_MODEL), lambda j, i: (i, 0))
    chunk = pl.BlockSpec((tb, FF_COLS), lambda j, i: (i, j))
    return _hosting_call(
        body, "ffn_bwd_weights", (D_FF // FF_COLS, steps),
        [wide, chunk, chunk, wide],
        [pl.BlockSpec((None, per, D_MODEL, FF_HALF), lambda j, i: (j // 2, j % 2, 0, 0)),
         pl.BlockSpec((per, FF_HALF, D_MODEL), lambda j, i: (j % 2, j // 2, 0))],
        [jax.ShapeDtypeStruct((2, N_DEV, D_MODEL, FF_HALF), BF16), jax.ShapeDtypeStruct((N_DEV, FF_CHUNK, D_MODEL), BF16)],
        [pltpu.VMEM((D_MODEL, FF_COLS), F32), pltpu.VMEM((FF_COLS, D_MODEL), F32)],
        (u2, da, act, df), carried, modes)


def _out_proj_bwd(dh1, mix, g2, w_out_t, attn, rec, carried, modes):
    rows = dh1.shape[0]
    tm = _row_tile(rows)
    steps = rows // tm

    def body(dh_ref, mix_ref, g_ref, w_ref, attn_ref, rec_ref, dattn_ref, drec_ref, dw_ref, dg_ref, acc):
        i = pl.program_id(0)

        @pl.when(i == 0)
        def _():
            acc[...] = jnp.zeros_like(acc)
            dg_ref[...] = jnp.zeros_like(dg_ref)

        g = g_ref[...]
        _, xhat, rstd = _rms_fwd(mix_ref[...], g)
        dmix, dg = _rms_bwd(dh_ref[...], xhat, rstd, g)
        dmix = dmix.astype(BF16)
        dg_ref[...] += dg
        din = _mm(dmix, w_ref[...])
        dattn_ref[...] = din[:, :ATTN_WIDTH].astype(BF16)
        drec_ref[...] = din[:, ATTN_WIDTH:]
        acc[0:ATTN_WIDTH, :] += _mm_tn(attn_ref[...], dmix)
        acc[ATTN_WIDTH:, :] += _mm_tn(rec_ref[...], dmix)

        @pl.when(i == steps - 1)
        def _():
            dw_ref[...] = acc[...].astype(BF16)

    half = pl.BlockSpec((tm, ATTN_WIDTH), lambda i: (i, 0))
    wide = pl.BlockSpec((tm, D_MODEL), lambda i: (i, 0))
    return _hosting_call(
        body, "out_proj_bwd", steps,
        [wide, wide, _full((1, D_MODEL)), _resident((D_MODEL, D_MODEL)), half, half],
        [half, half, _full((D_MODEL, D_MODEL)), _full((1, D_MODEL))],
        [jax.ShapeDtypeStruct((rows, ATTN_WIDTH), BF16), jax.ShapeDtypeStruct((rows, LRU_WIDTH), F32),
         jax.ShapeDtypeStruct((D_MODEL, D_MODEL), BF16), jax.ShapeDtypeStruct((1, D_MODEL), F32)],
        [pltpu.VMEM((D_MODEL, D_MODEL), F32)],
        (dh1, mix, g2, w_out_t, attn, rec), carried, modes)


def _attn_bwd(qkv, dattn, sinks, bias, carried, modes):
    rows = qkv.shape[0]
    tm = _row_tile(rows)
    nbt, nt = tm // BLOCK, rows // tm

    def body(sink_ref, bias_ref, do_ref, q_ref, kp_ref, kc_ref, vp_ref, vc_ref, dq_ref, dkv_ref, dsink_ref, dk_c, dv_c):
        i = pl.program_id(0)

        @pl.when(i == 0)
        def _():
            dk_c[...] = jnp.zeros_like(dk_c)
            dv_c[...] = jnp.zeros_like(dv_c)
            dsink_ref[...] = jnp.zeros_like(dsink_ref)

        @pl.when(i < nt)
        def _():
            dk_late, dv_late = dk_c[...], dv_c[...]
            dsink_rows = [jnp.zeros((1, LANES), F32)] * ATTN_HEADS
            for b in range(nbt):
                blk = slice(b * BLOCK, (b + 1) * BLOCK)
                bias_t = _bias_of_block(bias_ref, i * nbt + b)
                dq_parts, dk_parts, dv_parts = [], [], []
                for kv in range(KV_HEADS):
                    k2 = _keys_of_block(kp_ref, kc_ref, b, kv)
                    v2 = _keys_of_block(vp_ref, vc_ref, b, kv)
                    q4 = _heads(q_ref, blk, kv * GQA_GROUP, GQA_GROUP)
                    do4 = _heads(do_ref, blk, kv * GQA_GROUP, GQA_GROUP)
                    pn, psink = _attn_probs(k2, q4, bias_t, _sink_row(sink_ref, kv))
                    dpn = _mm_nt(v2, do4)
                    delta = jnp.sum(pn * dpn, axis=0, keepdims=True)
                    ds = ((pn * (dpn - delta)) * (HEAD_DIM ** -0.5)).astype(BF16)
                    dqt = _mm_tn(k2, ds)
                    dq_parts += [dqt[:, g * BLOCK:(g + 1) * BLOCK] for g in range(GQA_GROUP)]
                    dk_parts.append(_mm(ds, q4))
                    dv_parts.append(_mm(pn.astype(BF16), do4))
                    sd = psink * delta
                    for g in range(GQA_GROUP):
                        h = kv * GQA_GROUP + g
                        dsink_rows[h] = dsink_rows[h] - jnp.sum(sd[:, g * BLOCK:(g + 1) * BLOCK])
                dq_ref[blk, :] = _from_head_major(dq_parts).astype(BF16)
                dk2 = jnp.concatenate(dk_parts, axis=1)
                dv2 = jnp.concatenate(dv_parts, axis=1)
                dkv_ref[blk, 0:KV_WIDTH] = (dk_late + dk2[0:BLOCK]).astype(BF16)
                dkv_ref[blk, KV_WIDTH:] = (dv_late + dv2[0:BLOCK]).astype(BF16)
                dk_late, dv_late = dk2[BLOCK:], dv2[BLOCK:]
            dk_c[...] = dk_late
            dv_c[...] = dv_late
            dsink_ref[...] += jnp.concatenate(dsink_rows, axis=0)

        @pl.when(i == nt)
        def _():
            dkv_ref[...] = jnp.zeros_like(dkv_ref)
            dkv_ref[0:BLOCK, 0:KV_WIDTH] = dk_c[...].astype(BF16)
            dkv_ref[0:BLOCK, KV_WIDTH:] = dv_c[...].astype(BF16)

    tile_of = lambda i: jnp.minimum(i, nt - 1)
    tile = pl.BlockSpec((tm, ATTN_WIDTH), lambda i: (tile_of(i), 0))
    return _hosting_call(
        body, "attn_bwd", nt + 1,
        [pl.BlockSpec(memory_space=pltpu.SMEM), _resident((N_BIAS, 2 * BLOCK, GQA_GROUP * BLOCK)), tile]
        + _attn_specs(tm, tile_of),
        [tile, pl.BlockSpec((tm, 2 * KV_WIDTH), lambda i: (i, 0)), _full((ATTN_HEADS, LANES))],
        [jax.ShapeDtypeStruct((rows, ATTN_WIDTH), BF16), jax.ShapeDtypeStruct((rows + tm, 2 * KV_WIDTH), BF16),
         jax.ShapeDtypeStruct((ATTN_HEADS, LANES), F32)],
        [pltpu.VMEM((BLOCK, KV_WIDTH), F32), pltpu.VMEM((BLOCK, KV_WIDTH), F32)],
        (sinks, bias, dattn, qkv, qkv, qkv, qkv, qkv), carried, modes)


ROW_CONV_B, ROW_B_A, ROW_B_X, ROW_LAMBDA = 4, 5, 6, 7


def _rec_bwd(drec, zrec, h, kept, conv_w, wa_bd, wx_bd, lam, carried, modes):
    rows = zrec.shape[0]
    tm = _rec_tile(rows)
    nt = rows // tm
    per = tm // SUBLANES

    def body(drec_ref, xr_ref, yr_ref, h_ref, xc_ref, a_ref, mult_ref, r_ref, ig_ref, hhalo_ref, cw_ref, wa_ref, wx_ref,
             lam_ref, drz_ref, small_ref, dwa_ref, dwx_ref, hbuf, abuf, u_s, g_s, dbuf, carry):
        s = pl.program_id(0)
        i = nt - 1 - s

        @pl.when(s == 0)
        def _():
            small_ref[...] = jnp.zeros_like(small_ref)
            dwa_ref[...] = jnp.zeros_like(dwa_ref)
            dwx_ref[...] = jnp.zeros_like(dwx_ref)
            carry[...] = jnp.zeros_like(carry)
            abuf[tm:tm + SUBLANES, :] = jnp.zeros((SUBLANES, LRU_WIDTH), F32)
            dbuf[tm:tm + SUBLANES, :] = jnp.zeros((SUBLANES, LRU_WIDTH), F32)

        hbuf[0:SUBLANES, :] = jnp.where(i == 0, 0.0, hhalo_ref[...])
        hbuf[SUBLANES:SUBLANES + tm, :] = h_ref[...]

        xc, a, mult, r, ig = xc_ref[...], a_ref[...], mult_ref[...], r_ref[...], ig_ref[...]
        halves = _lru_halves(xc)
        inv_mult = pl.reciprocal(mult, approx=True)

        yr = yr_ref[...]
        gel, t = _gelu(yr)
        drec_t = drec_ref[...]
        dyr = drec_t * h_ref[...] * _gelu_grad(yr, t)

        abuf[0:tm, :] = a
        u_s[...] = drec_t * gel
        a_next = abuf[pl.ds(1, tm), :]
        abuf[0:tm, :] = a_next
        carry[0:1, :] = _scan_tile(abuf, u_s, g_s, carry[0:1, :], tm, reverse=True)
        abuf[tm:tm + 1, :] = a[0:1, :]
        g = g_s[...]

        grow = i * tm + lax.broadcasted_iota(jnp.int32, (tm, LRU_WIDTH), 0)
        du = jnp.where(grow >= PAD_ROWS, g, 0.0)
        da = g * hbuf[pl.ds(SUBLANES - 1, tm), :]
        dmult = du * (ig * xc)
        dig = du * (mult * xc)
        dxc = du * (mult * ig)
        dlog_a = da * a - dmult * (a * a * inv_mult)
        sp = _softplus(-lam_ref[...])
        dgr = (dlog_a * (-LRU_C) * sp) * (r * (1.0 - r))
        dgi = dig * (ig * (1.0 - ig))
        dlam = jnp.sum(dlog_a * r, axis=0, keepdims=True) * (LRU_C * _sigmoid(-lam_ref[...]))
        dgr_b = [dgr[:, hh * LRU_HALF:(hh + 1) * LRU_HALF].astype(BF16) for hh in range(2)]
        dgi_b = [dgi[:, hh * LRU_HALF:(hh + 1) * LRU_HALF].astype(BF16) for hh in range(2)]
        dxc = dxc + jnp.concatenate(
            [_mm_nt(dgr_b[hh], wa_ref[hh]) + _mm_nt(dgi_b[hh], wx_ref[hh]) for hh in range(2)], axis=1)
        for hh in range(2):
            dwa_ref[hh] += _mm_tn(halves[hh], dgr_b[hh])
            dwx_ref[hh] += _mm_tn(halves[hh], dgi_b[hh])

        dbuf[0:tm, :] = dxc
        ahead = [dbuf[pl.ds(CONV_WIDTH - 1 - j, tm), :] for j in range(CONV_WIDTH)]
        dxr = sum(cw_ref[j:j + 1, :] * ahead[j] for j in range(CONV_WIDTH))
        dbuf[tm:tm + SUBLANES, :] = dxc[0:SUBLANES, :]
        drz_ref[:, 0:LRU_WIDTH] = dxr.astype(BF16)
        drz_ref[:, LRU_WIDTH:] = dyr.astype(BF16)

        xr = xr_ref[...]
        upd = [jnp.sum(xr * ahead[j], axis=0, keepdims=True) for j in range(CONV_WIDTH)]
        upd += [jnp.sum(dxc, axis=0, keepdims=True), jnp.sum(dgr, axis=0, keepdims=True),
                jnp.sum(dgi, axis=0, keepdims=True), dlam]
        small_ref[...] += jnp.concatenate(upd, axis=0)

    rev = lambda s: nt - 1 - s
    halo = lambda s: jnp.maximum(rev(s) * per - 1, 0)
    cols = lambda k: pl.BlockSpec((tm, LRU_WIDTH), lambda s: (rev(s), k))
    halo0 = pl.BlockSpec((SUBLANES, LRU_WIDTH), lambda s: (halo(s), 0))
    bd = _full((2, LRU_HALF, LRU_HALF))
    big = pltpu.VMEM((tm + SUBLANES, LRU_WIDTH), F32)
    tile = pltpu.VMEM((tm, LRU_WIDTH), F32)
    kept_cols = [cols(k) for k in (KEPT_XC, KEPT_A, KEPT_MULT, KEPT_R, KEPT_I)]
    return _hosting_call(
        body, "rec_bwd", nt,
        [cols(0), cols(0), cols(1), cols(0)] + kept_cols
        + [halo0, _full((CONV_WIDTH, LRU_WIDTH)), bd, bd, _full((1, LRU_WIDTH))],
        [pl.BlockSpec((tm, 2 * LRU_WIDTH), lambda s: (rev(s), 0)), _full((SUBLANES, LRU_WIDTH)), bd, bd],
        [jax.ShapeDtypeStruct((rows, 2 * LRU_WIDTH), BF16), jax.ShapeDtypeStruct((SUBLANES, LRU_WIDTH), F32),
         jax.ShapeDtypeStruct((2, LRU_HALF, LRU_HALF), F32), jax.ShapeDtypeStruct((2, LRU_HALF, LRU_HALF), F32)],
        [big, big, tile, tile, big, pltpu.VMEM((SUBLANES, LRU_WIDTH), F32)],
        (drec, zrec, zrec, h) + (kept,) * N_KEPT + (h, conv_w, wa_bd, wx_bd, lam), carried, modes)


DZ_CUTS = (0, ATTN_WIDTH, QKV_WIDTH, IN_WIDTH)


def _dz_specs(tm):
    return [pl.BlockSpec((tm, DZ_CUTS[p + 1] - DZ_CUTS[p]), lambda i: (i, 0)) for p in range(3)]


def _in_proj_bwd_x(head, x, g1, dh1, dq, dkv, drz, w_in_t, carried, modes):
    rows = dh1.shape[0]
    tm = _row_tile(rows)
    steps = rows // tm

    def body(head_ref, g_ref, dh1_ref, dq_ref, dkv_ref, drz_ref, w_ref, x_hbm, dh0_ref, dg_ref, buf, sem):
        i = pl.program_id(0)
        h0 = _h0_tile(head_ref, x_hbm, buf, sem, i, steps, tm)

        @pl.when(i == 0)
        def _():
            dg_ref[...] = jnp.zeros_like(dg_ref)

        g = g_ref[...]
        _, xhat, rstd = _rms_fwd(h0, g)
        parts = (dq_ref[...], dkv_ref[...], drz_ref[...])
        du = sum(_mm(parts[p], w_ref[DZ_CUTS[p]:DZ_CUTS[p + 1], :]) for p in range(3))
        dx, dg = _rms_bwd(du, xhat, rstd, g)
        dh0_ref[...] = dh1_ref[...] + dx
        dg_ref[...] += dg

    wide = pl.BlockSpec((tm, D_MODEL), lambda i: (i, 0))
    return _hosting_call(
        body, "in_proj_bwd_x", steps,
        [_full((BLOCK, D_MODEL)), _full((1, D_MODEL)), wide] + _dz_specs(tm) + [_resident((IN_WIDTH, D_MODEL)), ANY_SPACE],
        [wide, _full((1, D_MODEL))],
        [jax.ShapeDtypeStruct((rows, D_MODEL), F32), jax.ShapeDtypeStruct((1, D_MODEL), F32)],
        _frame_scratch(tm), (head, g1, dh1, dq, dkv, drz, w_in_t, x), carried, modes)


def _in_proj_bwd_w(u1, dq, dkv, drz, carried, modes):
    rows = u1.shape[0]
    tb = _big_tile(rows)

    def body(u_ref, dq_ref, dkv_ref, drz_ref, dw_ref):
        @pl.when(pl.program_id(0) == 0)
        def _():
            dw_ref[...] = jnp.zeros_like(dw_ref)

        u = u_ref[...]
        for p, ref in enumerate((dq_ref, dkv_ref, drz_ref)):
            dw_ref[:, DZ_CUTS[p]:DZ_CUTS[p + 1]] += _mm_tn(u, ref[...])

    return _hosting_call(
        body, "in_proj_bwd_w", rows // tb,
        [pl.BlockSpec((tb, D_MODEL), lambda i: (i, 0))] + _dz_specs(tb),
        [_full((D_MODEL, IN_WIDTH))],
        [jax.ShapeDtypeStruct((D_MODEL, IN_WIDTH), F32)],
        [], (u1, dq, dkv, drz), carried, modes)


def _adamw_math(w, m, v, g):
    nm = ADAM_B1 * m + (1.0 - ADAM_B1) * g
    nv = ADAM_B2 * v + (1.0 - ADAM_B2) * (g * g)
    m_hat = nm / (1.0 - ADAM_B1 ** ADAM_STEP)
    v_hat = nv / (1.0 - ADAM_B2 ** ADAM_STEP)
    return (-ADAM_LR) * (m_hat / (jnp.sqrt(v_hat) + ADAM_EPS) + ADAM_WD * w), nm, nv


SMALL_NAMES = ("conv_b", "b_a", "b_x", "lru_lambda", "attn_sinks", "g_post_mix", "g_pre_ffn", "g_post_ffn")
PACK_WIDTH = 1024


def _pack_rows(vals):
    assert len(SMALL_NAMES) == SUBLANES
    row = lax.broadcasted_iota(jnp.int32, (SUBLANES, PACK_WIDTH), 0)
    tile = jnp.zeros((SUBLANES, PACK_WIDTH), F32)
    for k, name in enumerate(SMALL_NAMES):
        a = vals[name].reshape(1, -1)
        tile = jnp.where(row == k, jnp.pad(a, ((0, 0), (0, PACK_WIDTH - a.shape[1]))), tile)
    return tile


def _adamw_small(weights, mom_m, mom_v, parts, loss_parts):
    n = len(SMALL_NAMES)
    views = [(1, weights[name].size) for name in SMALL_NAMES]

    def body(*refs):
        w_refs, m_refs, v_refs = refs[:n], refs[n:2 * n], refs[2 * n:3 * n]
        p_ref, l_ref, loss_ref = refs[3 * n], refs[3 * n + 1], refs[3 * n + 2]
        outs = refs[3 * n + 3:]
        for k, (_, c) in enumerate(views):
            g = p_ref[0, k:k + 1, 0:c]
            for s in range(1, N_DEV):
                g = g + p_ref[s, k:k + 1, 0:c]
            g_ref, d_ref, nm_ref, nv_ref = outs[4 * k:4 * k + 4]
            g_ref[...] = g
            d_ref[...], nm_ref[...], nv_ref[...] = _adamw_math(w_refs[k][...], m_refs[k][...], v_refs[k][...], g)
        total = l_ref[0]
        for s in range(1, N_DEV):
            total = total + l_ref[s]
        loss_ref[...] = total

    args = [src[name].reshape(view) for src in (weights, mom_m, mom_v) for name, view in zip(SMALL_NAMES, views)]
    res = pl.pallas_call(
        body, name="adamw_small",
        out_shape=[jax.ShapeDtypeStruct(loss_parts.shape[1:], F32)]
                  + [jax.ShapeDtypeStruct(view, F32) for view in views for _ in range(4)],
        compiler_params=pltpu.CompilerParams(vmem_limit_bytes=VMEM_LIMIT),
    )(*args, parts, loss_parts)
    out = {name: tuple(t.reshape(weights[name].shape) for t in res[1 + 4 * k:5 + 4 * k]) for k, name in enumerate(SMALL_NAMES)}
    return res[0], out


def _adamw(w, m, v, parts, name):
    rows, cols = w.shape
    tr = next((t for t in (256, 128) if rows % t == 0), rows)
    parts = parts if isinstance(parts, (list, tuple)) else [parts]

    def body(w_ref, m_ref, v_ref, *refs):
        p_refs, (g_ref, d_ref, nm_ref, nv_ref) = refs[:len(parts)], refs[len(parts):]

        def total(p_ref):
            g = p_ref[0].astype(F32)
            for s in range(1, N_DEV):
                g = g + p_ref[s].astype(F32)
            return g

        g = jnp.concatenate([total(p_ref) for p_ref in p_refs], axis=1) if len(parts) > 1 else total(p_refs[0])
        g_ref[...] = g
        d_ref[...], nm_ref[...], nv_ref[...] = _adamw_math(w_ref[...], m_ref[...], v_ref[...], g)

    blk = pl.BlockSpec((tr, cols), lambda i: (i, 0))
    return pl.pallas_call(
        body, name=name, grid=(rows // tr,),
        in_specs=[blk, blk, blk] + [pl.BlockSpec((N_DEV, tr, p.shape[2]), lambda i: (0, i, 0)) for p in parts],
        out_specs=[blk] * 4,
        out_shape=[jax.ShapeDtypeStruct((rows, cols), F32)] * 4,
        compiler_params=_params(("parallel",)),
    )(w, m, v, *parts)


def _cols_from_shards(g):
    return jnp.transpose(g, (1, 0, 2)).reshape(g.shape[1], N_DEV * g.shape[2])


def _cols_to_shards(a):
    r, c = a.shape
    return jnp.transpose(a.reshape(r, N_DEV, c // N_DEV), (1, 0, 2))


def _block_diag(w):
    per = LRU_HALF // LRU_BLOCK
    w = w.reshape(2, per, LRU_BLOCK, LRU_BLOCK)
    eye = jnp.eye(per, dtype=w.dtype)
    return (w[:, :, :, None, :] * eye[None, :, None, :, None]).reshape(2, LRU_HALF, LRU_HALF)


def _block_diag_extract(t):
    per = LRU_HALF // LRU_BLOCK
    t = t.reshape(2, per, LRU_BLOCK, per, LRU_BLOCK)
    return jnp.stack([t[:, b, :, b, :] for b in range(per)], axis=1).reshape(LRU_BLOCKS, LRU_BLOCK, LRU_BLOCK)


def kernel(x, meta_tokens, g_pre_mix, w_in, conv_w, conv_b, w_a, b_a, w_x, b_x, lru_lambda, attn_sinks, w_out, g_post_mix, g_pre_ffn, w_ff1, w_ff2, g_post_ffn, loss_target, m_meta_tokens, m_g_pre_mix, m_w_in, m_conv_w, m_conv_b, m_w_a, m_b_a, m_w_x, m_b_x, m_lru_lambda, m_attn_sinks, m_w_out, m_g_post_mix, m_g_pre_ffn, m_w_ff1, m_w_ff2, m_g_post_ffn, v_meta_tokens, v_g_pre_mix, v_w_in, v_conv_w, v_conv_b, v_w_a, v_b_a, v_w_x, v_b_x, v_lru_lambda, v_attn_sinks, v_w_out, v_g_post_mix, v_g_pre_ffn, v_w_ff1, v_w_ff2, v_g_post_ffn):
    weights = dict(meta_tokens=meta_tokens, g_pre_mix=g_pre_mix, w_in=w_in, conv_w=conv_w, conv_b=conv_b, w_a=w_a,
                   b_a=b_a, w_x=w_x, b_x=b_x, lru_lambda=lru_lambda, attn_sinks=attn_sinks, w_out=w_out,
                   g_post_mix=g_post_mix, g_pre_ffn=g_pre_ffn, w_ff1=w_ff1, w_ff2=w_ff2, g_post_ffn=g_post_ffn)
    mom_m = dict(meta_tokens=m_meta_tokens, g_pre_mix=m_g_pre_mix, w_in=m_w_in, conv_w=m_conv_w, conv_b=m_conv_b,
                 w_a=m_w_a, b_a=m_b_a, w_x=m_w_x, b_x=m_b_x, lru_lambda=m_lru_lambda, attn_sinks=m_attn_sinks,
                 w_out=m_w_out, g_post_mix=m_g_post_mix, g_pre_ffn=m_g_pre_ffn, w_ff1=m_w_ff1, w_ff2=m_w_ff2,
                 g_post_ffn=m_g_post_ffn)
    mom_v = dict(meta_tokens=v_meta_tokens, g_pre_mix=v_g_pre_mix, w_in=v_w_in, conv_w=v_conv_w, conv_b=v_conv_b,
                 w_a=v_w_a, b_a=v_b_a, w_x=v_w_x, b_x=v_b_x, lru_lambda=v_lru_lambda, attn_sinks=v_attn_sinks,
                 w_out=v_w_out, g_post_mix=v_g_post_mix, g_pre_ffn=v_g_pre_ffn, w_ff1=v_w_ff1, w_ff2=v_w_ff2,
                 g_post_ffn=v_g_post_ffn)
    order = list(weights)

    (g_win, g_meta, g_cw) = _gather_two_level([w_in[0].astype(BF16), meta_tokens, conv_w[0]], "gather_first")
    w_in_full = _cols_from_shards(g_win)
    meta_full = _cols_from_shards(g_meta)
    conv_w_full = _cols_from_shards(g_cw)

    head = jnp.concatenate([jnp.zeros((PAD_ROWS, D_MODEL), F32), meta_full], axis=0)
    wa_bd = _block_diag(w_a[0]).astype(BF16)
    wx_bd = _block_diag(w_x[0]).astype(BF16)
    bias = _attn_bias()

    w1_shard = w_ff1[0].astype(BF16)
    (qkv, zrec, u1), (g_wout,) = _in_proj_fwd(head, x[0], g_pre_mix, w_in_full, [w_out[0].astype(BF16)], ["gather"])
    (attn,), (w1a,) = _attn_fwd(qkv, attn_sinks, bias, [w1_shard[:, :FF_HALF]], ["gather"])
    (rec, h_lru, kept), (w1b,) = _rec_fwd(zrec, conv_w_full, conv_b, wa_bd, b_a, wx_bd, b_x, lru_lambda,
                                         [w1_shard[:, FF_HALF:]], ["gather"])
    w_out_full = g_wout.reshape(D_MODEL, D_MODEL)
    w2_shard = w_ff2[0].astype(BF16)
    (mix, h1), (w2a,) = _out_proj_fwd(attn, rec, w_out_full, head, x[0], g_post_mix, [w2_shard[:FF_HALF]], ["gather"])
    (act, u2), (w2b,) = _ffn_up(h1, g_pre_ffn, (w1a, w1b), [w2_shard[FF_HALF:]], ["gather"])
    w2_halves = [w.reshape(D_FF // 2, D_MODEL) for w in (w2a, w2b)]
    (dy, df, dg_post_ffn, loss_acc), w2t_halves = _ffn_down_loss(
        act, w2_halves, h1, loss_target[0], g_post_ffn, [w2_shard[:FF_HALF].T, w2_shard[FF_HALF:].T], ["gather"] * 2)

    (da1,), (w1ta,) = _ffn_bwd_act(df, w2t_halves, act, [w1_shard[:, :FF_HALF].T], ["gather"])
    (dw1h, dw2g), (w1tb,) = _ffn_bwd_weights(u2, da1, act, df, [w1_shard[:, FF_HALF:].T], ["gather"])
    w1t_halves = [w.reshape(D_FF // 2, D_MODEL) for w in (w1ta, w1tb)]
    (dh1, dg_pre_ffn), (p_w1a,) = _ffn_bwd_x(da1, w1t_halves, h1, dy, g_pre_ffn, [dw1h[0]], ["scatter"])
    (dattn, drec, dw_out, dg_post_mix), (p_w1b,) = _out_proj_bwd(dh1, mix, g_post_mix, w_out_full.T, attn, rec,
                                                                [dw1h[1]], ["scatter"])
    (dq, dkv_late, dsinks), (p_w2,) = _attn_bwd(qkv, dattn, attn_sinks, bias, [dw2g], ["scatter"])
    dkv = dkv_late[BLOCK:BLOCK + qkv.shape[0]]
    (drz, rec_small, dwa_bd, dwx_bd), (p_wout,) = _rec_bwd(
        drec, zrec, h_lru, kept, conv_w_full, wa_bd, wx_bd, lru_lambda,
        [dw_out.reshape(N_DEV, D_MODEL // N_DEV, D_MODEL)], ["scatter"])
    small_grads = dict(
        conv_b=rec_small[ROW_CONV_B], b_a=rec_small[ROW_B_A], b_x=rec_small[ROW_B_X], lru_lambda=rec_small[ROW_LAMBDA],
        attn_sinks=dsinks[:, 0], g_post_mix=dg_post_mix, g_pre_ffn=dg_pre_ffn, g_post_ffn=dg_post_ffn)
    gate_rows = (LRU_BLOCKS * LRU_BLOCK, LRU_BLOCK)
    gate_dense = (LRU_BLOCKS * LRU_BLOCK * LRU_BLOCK // PACK_WIDTH, PACK_WIDTH)
    (dw_in,), (p_cw, p_small, p_wa, p_wx) = _in_proj_bwd_w(
        u1, dq, dkv, drz,
        [_cols_to_shards(rec_small[0:CONV_WIDTH]), _pack_rows(small_grads),
         _block_diag_extract(dwa_bd).reshape(gate_dense), _block_diag_extract(dwx_bd).reshape(gate_dense)],
        ["scatter", "gather", "gather", "gather"])
    p_wa, p_wx = (p.reshape((N_DEV,) + gate_rows) for p in (p_wa, p_wx))
    (dh0, dg_pre_mix), (p_win,) = _in_proj_bwd_x(
        head, x[0], g_pre_mix, dh1, dq, dkv, drz, w_in_full.T, [_cols_to_shards(dw_in).astype(BF16)], ["scatter"])
    p_meta, p_gpm, p_loss = _exchange([_cols_to_shards(dh0[PAD_ROWS:BLOCK]), dg_pre_mix, loss_acc],
                                      ["scatter", "gather", "gather"], "exchange_last")

    res = {}
    res["g_pre_mix"] = _adamw(g_pre_mix, m_g_pre_mix, v_g_pre_mix, p_gpm, "adamw_g_pre_mix")
    res["w_in"] = _adamw(w_in[0], m_w_in[0], v_w_in[0], p_win, "adamw_w_in")
    res["w_out"] = _adamw(w_out[0], m_w_out[0], v_w_out[0], p_wout, "adamw_w_out")
    res["w_ff1"] = _adamw(w_ff1[0], m_w_ff1[0], v_w_ff1[0], [p_w1a, p_w1b], "adamw_w_ff1")
    res["w_ff2"] = _adamw(w_ff2[0], m_w_ff2[0], v_w_ff2[0], p_w2, "adamw_w_ff2")
    res["meta_tokens"] = _adamw(meta_tokens, m_meta_tokens, v_meta_tokens, p_meta, "adamw_meta")
    res["conv_w"] = _adamw(conv_w[0], m_conv_w[0], v_conv_w[0], p_cw, "adamw_conv_w")
    for name in ("w_in", "w_out", "w_ff1", "w_ff2", "conv_w"):
        res[name] = tuple(t[None] for t in res[name])
    for name, parts in (("w_a", p_wa), ("w_x", p_wx)):
        gate = _adamw(*(src[name].reshape(gate_rows) for src in (weights, mom_m, mom_v)), parts, "adamw_" + name)
        res[name] = tuple(t.reshape(weights[name].shape) for t in gate)
    loss_total, small = _adamw_small(weights, mom_m, mom_v, p_small, p_loss)
    res.update(small)

    grad_x = dh0[BLOCK:][None]
    outs = [loss_total[0, 0], grad_x]
    for k in range(4):
        outs += [res[name][k] for name in order]
    return tuple(outs)
```

```python
import jax
import jax.numpy as jnp
import numpy as np
from jax import lax
from jax.experimental import pallas as pl
from jax.experimental.pallas import tpu as pltpu

F32 = jnp.float32
BF16 = jnp.bfloat16

D_MODEL = 1024
N_META = 16
HEAD_DIM = 64
ATTN_HEADS = 8
KV_HEADS = 2
GQA_GROUP = ATTN_HEADS // KV_HEADS
ATTN_WIDTH = ATTN_HEADS * HEAD_DIM
KV_WIDTH = KV_HEADS * HEAD_DIM
QKV_WIDTH = ATTN_WIDTH + 2 * KV_WIDTH
LRU_WIDTH = 512
LRU_BLOCKS = 8
LRU_BLOCK = 64
LRU_HALF = 256
LRU_C = 8.0
CONV_WIDTH = 4
BLOCK = 128
PAD_ROWS = BLOCK - N_META
IN_WIDTH = QKV_WIDTH + 2 * LRU_WIDTH
D_FF = 4096
EPS = 1e-6
NEG = -1e30
N_DEV = 8
FF_CHUNK = D_FF // N_DEV
SUBLANES = 8
LANES = 128

ADAM_LR = 0.001
ADAM_B1 = 0.9
ADAM_B2 = 0.999
ADAM_EPS = 1e-08
ADAM_WD = 0.01
ADAM_STEP = 10

VMEM_LIMIT = 56 * 1024 * 1024


def _row_tile(rows):
    for t in (640, 512, 256, 128):
        if rows % t == 0:
            return t
    raise ValueError(rows)


def _wide_tile(rows):
    for t in (1040, 640, 512, 256, 128):
        if rows % t == 0:
            return t
    raise ValueError(rows)


def _big_tile(rows):
    for t in (1664, 1024, 512, 256, 128):
        if rows % t == 0:
            return t
    raise ValueError(rows)


def _rec_tile(rows):
    for t in (320, 256, 128):
        if rows % t == 0:
            return t
    raise ValueError(rows)


def _params(semantics):
    return pltpu.CompilerParams(dimension_semantics=semantics, vmem_limit_bytes=VMEM_LIMIT)


def _mm(a, b):
    return lax.dot_general(a, b, (((1,), (0,)), ((), ())), preferred_element_type=F32)


def _mm_nt(a, b):
    return lax.dot_general(a, b, (((1,), (1,)), ((), ())), preferred_element_type=F32)


def _mm_tn(a, b):
    return lax.dot_general(a, b, (((0,), (0,)), ((), ())), preferred_element_type=F32)


def _rms_fwd(x, g):
    rstd = lax.rsqrt(jnp.mean(x * x, axis=-1, keepdims=True) + EPS)
    xhat = x * rstd
    return xhat * g, xhat, rstd


def _rms_bwd(dy, xhat, rstd, g):
    dyg = dy * g
    c = jnp.mean(dyg * xhat, axis=-1, keepdims=True)
    dx = rstd * (dyg - xhat * c)
    dg = jnp.sum(dy * xhat, axis=0, keepdims=True)
    return dx, dg


def _sigmoid(x):
    return 0.5 * jnp.tanh(0.5 * x) + 0.5


def _log1p(x):
    u = 1.0 + x
    return jnp.where(u == 1.0, x, jnp.log(u) * x / (u - 1.0))


def _one_minus_sq_exp(x, ex):
    return -jnp.tanh(x) * (1.0 + ex * ex)


TINY = 1e-30


def _sqrt_pos(y):
    r = lax.rsqrt(jnp.maximum(y, TINY))
    return y * r, r


def _softplus(x):
    return jnp.maximum(x, 0.0) + _log1p(jnp.exp(-jnp.abs(x)))


GELU_C = 0.7978845608028654
GELU_K = 0.044715


def _gelu(x):
    t = jnp.tanh(GELU_C * (x + GELU_K * x * x * x))
    return 0.5 * x * (1.0 + t), t


def _gelu_grad(x, t):
    return 0.5 * (1.0 + t) + 0.5 * x * (1.0 - t * t) * GELU_C * (1.0 + 3.0 * GELU_K * x * x)


def _full(shape):
    return pl.BlockSpec(shape, lambda *_: (0,) * len(shape))


def _resident(shape):
    return pl.BlockSpec(shape, lambda *_: (0,) * len(shape), pipeline_mode=pl.Buffered(1))


def _exchange_copies(ins, outs, sems, modes):
    send_sems, recv_sems, local_sems = sems
    x, y, c = lax.axis_index("x"), lax.axis_index("y"), lax.axis_index("c")
    me = 4 * x + 2 * y + c

    def block(a, dev):
        return ins[a] if modes[a] == "gather" else ins[a].at[dev]

    local = [pltpu.make_async_copy(block(a, me), outs[a].at[me], local_sems.at[a]) for a in range(len(ins))]
    sends, recvs = [], []
    for a in range(len(ins)):
        for k in range(N_DEV - 1):
            bits = k + 1
            px = jnp.bitwise_xor(x, (bits >> 2) & 1)
            py = jnp.bitwise_xor(y, (bits >> 1) & 1)
            pc = jnp.bitwise_xor(c, bits & 1)
            peer = 4 * px + 2 * py + pc
            common = dict(src_ref=block(a, peer), send_sem=send_sems.at[a, k], recv_sem=recv_sems.at[a, k],
                          device_id=(px, py, pc), device_id_type=pl.DeviceIdType.MESH)
            sends.append(pltpu.make_async_remote_copy(dst_ref=outs[a].at[me], **common))
            recvs.append(pltpu.make_async_remote_copy(dst_ref=outs[a].at[peer], **common))
    return local, sends, recvs


def _exchange_start(ins, outs, sems, modes):
    local, sends, _ = _exchange_copies(ins, outs, sems, modes)
    for cp in local + sends:
        cp.start()


def _exchange_wait(ins, outs, sems, modes):
    local, sends, recvs = _exchange_copies(ins, outs, sems, modes)
    for cp in recvs:
        cp.wait_recv()
    for cp in sends:
        cp.wait_send()
    for cp in local:
        cp.wait()


def _exchange_shapes(arrays, modes):
    return [jax.ShapeDtypeStruct((N_DEV,) + a.shape if mode == "gather" else a.shape, a.dtype)
            for a, mode in zip(arrays, modes)]


def _exchange_sems(na):
    return [pltpu.SemaphoreType.DMA((na, N_DEV - 1)), pltpu.SemaphoreType.DMA((na, N_DEV - 1)),
            pltpu.SemaphoreType.DMA((na,))]


ANY_SPACE = pl.BlockSpec(memory_space=pl.ANY)


def _exchange(arrays, modes, name):
    na = len(arrays)

    def body(*refs):
        ins, outs, sems = refs[:na], refs[na:2 * na], refs[2 * na:]
        _exchange_start(ins, outs, sems, modes)
        _exchange_wait(ins, outs, sems, modes)

    return pl.pallas_call(
        body, name=name, out_shape=_exchange_shapes(arrays, modes),
        in_specs=[ANY_SPACE] * na, out_specs=[ANY_SPACE] * na, scratch_shapes=_exchange_sems(na),
        compiler_params=pltpu.CompilerParams(has_side_effects=True),
    )(*arrays)


def _gather_two_level(arrays, name):
    na = len(arrays)

    def body(*refs):
        ins, outs = refs[:na], refs[na:2 * na]
        send_sems, recv_sems, local_sems = refs[2 * na:]
        x, y, c = lax.axis_index("x"), lax.axis_index("y"), lax.axis_index("c")
        me, sibling = (x, y, c), (x, y, 1 - c)
        chips = [(1 - x, y), (x, 1 - y), (1 - x, 1 - y)]

        def copy(a, k, block, to, src=None):
            slot = outs[a].at[4 * block[0] + 2 * block[1] + block[2]]
            return pltpu.make_async_remote_copy(
                src_ref=slot if src is None else src, dst_ref=slot, send_sem=send_sems.at[a, k],
                recv_sem=recv_sems.at[a, k], device_id=to, device_id_type=pl.DeviceIdType.MESH)

        local = [pltpu.make_async_copy(ins[a], outs[a].at[4 * x + 2 * y + c], local_sems.at[a]) for a in range(na)]
        first = []
        for a in range(na):
            first.append(copy(a, 0, me, sibling, src=ins[a]))
            first += [copy(a, 1 + j, me, (*chip, c), src=ins[a]) for j, chip in enumerate(chips)]
        for cp in local + first:
            cp.start()
        passed = []
        for j, chip in enumerate(chips):
            for a in range(na):
                copy(a, 1 + j, (*chip, c), me).wait_recv()
                passed.append(copy(a, 4 + j, (*chip, c), sibling))
                passed[-1].start()
        for a in range(na):
            copy(a, 0, sibling, me).wait_recv()
            for j, chip in enumerate(chips):
                copy(a, 4 + j, (*chip, 1 - c), me).wait_recv()
        for cp in first + passed:
            cp.wait_send()
        for cp in local:
            cp.wait()

    return pl.pallas_call(
        body, name=name, out_shape=_exchange_shapes(arrays, ["gather"] * na),
        in_specs=[ANY_SPACE] * na, out_specs=[ANY_SPACE] * na, scratch_shapes=_exchange_sems(na),
        compiler_params=pltpu.CompilerParams(has_side_effects=True),
    )(*arrays)


def _hosting_call(body, name, steps, in_specs, out_specs, out_shape, scratch_shapes, args, arrays, modes):
    n_in, n_out, n_scr, na = len(in_specs), len(out_specs), len(scratch_shapes), len(arrays)
    grid = steps if isinstance(steps, tuple) else (steps,)

    def hosting_body(*refs):
        cuts = [0]
        for n in (n_in, na, n_out, na, n_scr, 3):
            cuts.append(cuts[-1] + n)
        ins, x_ins, outs, x_outs, scr, sems = (refs[cuts[p]:cuts[p + 1]] for p in range(6))
        first, last = True, True
        for axis, n in enumerate(grid):
            first = first & (pl.program_id(axis) == 0)
            last = last & (pl.program_id(axis) == n - 1)

        @pl.when(first)
        def _():
            _exchange_start(x_ins, x_outs, sems, modes)

        body(*ins, *outs, *scr)

        @pl.when(last)
        def _():
            _exchange_wait(x_ins, x_outs, sems, modes)

    res = pl.pallas_call(
        hosting_body, name=name, grid=grid,
        in_specs=list(in_specs) + [ANY_SPACE] * na, out_specs=list(out_specs) + [ANY_SPACE] * na,
        out_shape=list(out_shape) + _exchange_shapes(arrays, modes),
        scratch_shapes=list(scratch_shapes) + _exchange_sems(na),
        compiler_params=_params(("arbitrary",) * len(grid)),
    )(*args, *arrays)
    return res[:n_out], res[n_out:]


def _frame_rows(src_hbm, buf, sem, i, steps, tm):
    def first():
        return pltpu.make_async_copy(src_hbm.at[pl.ds(0, tm - BLOCK)], buf.at[0, pl.ds(BLOCK, tm - BLOCK)], sem.at[0])

    def later(t, slot):
        return pltpu.make_async_copy(src_hbm.at[pl.ds(pl.multiple_of(t * tm - BLOCK, SUBLANES), tm)], buf.at[slot], sem.at[slot])

    slot = i % 2

    @pl.when(i == 0)
    def _():
        first().start()

    @pl.when(i + 1 < steps)
    def _():
        later(i + 1, 1 - slot).start()

    @pl.when(i == 0)
    def _():
        first().wait()

    @pl.when(i > 0)
    def _():
        later(i, slot).wait()

    return slot


def _frame_scratch(tm):
    return [pltpu.VMEM((2, tm, D_MODEL), F32), pltpu.SemaphoreType.DMA((2,))]


def _h0_tile(head_ref, x_hbm, buf, sem, i, steps, tm):
    slot = _frame_rows(x_hbm, buf, sem, i, steps, tm)

    @pl.when(i == 0)
    def _():
        buf[0, 0:BLOCK, :] = head_ref[...]

    return buf[slot]


def _in_proj_fwd(head, x, g1, w_in, carried, modes):
    rows = BLOCK + x.shape[0]
    tm = _wide_tile(rows)
    steps = rows // tm

    def body(head_ref, g_ref, w_ref, x_hbm, qkv_ref, zrec_ref, u_ref, buf, sem):
        h = _h0_tile(head_ref, x_hbm, buf, sem, pl.program_id(0), steps, tm)
        u, _, _ = _rms_fwd(h, g_ref[...])
        u = u.astype(BF16)
        u_ref[...] = u
        z = _mm(u, w_ref[...])
        qkv_ref[...] = z[:, :QKV_WIDTH].astype(BF16)
        zrec_ref[...] = z[:, QKV_WIDTH:]

    wide = pl.BlockSpec((tm, D_MODEL), lambda i: (i, 0))
    return _hosting_call(
        body, "in_proj_fwd", steps,
        [_full((BLOCK, D_MODEL)), _full((1, D_MODEL)), _resident((D_MODEL, IN_WIDTH)), ANY_SPACE],
        [pl.BlockSpec((tm, QKV_WIDTH), lambda i: (i, 0)), pl.BlockSpec((tm, 2 * LRU_WIDTH), lambda i: (i, 0)), wide],
        [jax.ShapeDtypeStruct((rows, QKV_WIDTH), BF16), jax.ShapeDtypeStruct((rows, 2 * LRU_WIDTH), F32),
         jax.ShapeDtypeStruct((rows, D_MODEL), BF16)],
        _frame_scratch(tm), (head, g1, w_in, x), carried, modes)


N_BIAS = 3


def _attn_bias():
    key = np.arange(2 * BLOCK)[:, None]
    r = np.arange(GQA_GROUP * BLOCK)[None, :] % BLOCK
    band = (key > r) & (key <= r + BLOCK)
    out = [np.where(band & ((n - 1) * BLOCK + key >= PAD_ROWS), 0.0, NEG) for n in range(N_BIAS)]
    return jnp.asarray(np.stack(out), F32)


def _attn_probs(k2, q4, bias, sink_row):
    s = _mm_nt(k2, q4) * (HEAD_DIM ** -0.5) + bias
    m = jnp.maximum(jnp.max(s, axis=0, keepdims=True), sink_row)
    p = jnp.exp(s - m)
    es = jnp.exp(sink_row - m)
    inv = 1.0 / (jnp.sum(p, axis=0, keepdims=True) + es)
    return p * inv, es * inv


def _heads(ref, rows, first, count):
    return jnp.concatenate([ref[rows, (first + g) * HEAD_DIM:(first + g + 1) * HEAD_DIM] for g in range(count)], axis=0)


def _keys_of_block(prev_ref, cur_ref, b, kv):
    sl = slice(kv * HEAD_DIM, (kv + 1) * HEAD_DIM)
    before = prev_ref[:, sl] if b == 0 else cur_ref[(b - 1) * BLOCK:b * BLOCK, sl]
    return jnp.concatenate([before, cur_ref[b * BLOCK:(b + 1) * BLOCK, sl]], axis=0)


def _bias_of_block(bias_ref, block):
    return bias_ref[jnp.minimum(block, N_BIAS - 1)]


def _sink_row(sink_ref, kv):
    g = lax.broadcasted_iota(jnp.int32, (1, GQA_GROUP * BLOCK), 1) // BLOCK
    row = jnp.full((1, GQA_GROUP * BLOCK), sink_ref[0, kv * GQA_GROUP], F32)
    for i in range(1, GQA_GROUP):
        row = jnp.where(g == i, sink_ref[0, kv * GQA_GROUP + i], row)
    return row


def _from_head_major(pieces):
    return jnp.concatenate(pieces, axis=0).T


def _attn_specs(tm, tile_of):
    nbt = tm // BLOCK
    k_col, v_col = ATTN_WIDTH // KV_WIDTH, ATTN_WIDTH // KV_WIDTH + 1
    before = lambda i: jnp.maximum(tile_of(i) * nbt - 1, 0)
    return [pl.BlockSpec((tm, ATTN_WIDTH), lambda i: (tile_of(i), 0)),
            pl.BlockSpec((BLOCK, KV_WIDTH), lambda i: (before(i), k_col)),
            pl.BlockSpec((tm, KV_WIDTH), lambda i: (tile_of(i), k_col)),
            pl.BlockSpec((BLOCK, KV_WIDTH), lambda i: (before(i), v_col)),
            pl.BlockSpec((tm, KV_WIDTH), lambda i: (tile_of(i), v_col))]


def _attn_fwd(qkv, sinks, bias, carried, modes):
    rows = qkv.shape[0]
    tm = _row_tile(rows)
    nbt = tm // BLOCK

    def body(sink_ref, bias_ref, q_ref, kp_ref, kc_ref, vp_ref, vc_ref, o_ref):
        i = pl.program_id(0)
        for b in range(nbt):
            blk = slice(b * BLOCK, (b + 1) * BLOCK)
            bias_t = _bias_of_block(bias_ref, i * nbt + b)
            pieces = []
            for kv in range(KV_HEADS):
                k2 = _keys_of_block(kp_ref, kc_ref, b, kv)
                v2 = _keys_of_block(vp_ref, vc_ref, b, kv)
                q4 = _heads(q_ref, blk, kv * GQA_GROUP, GQA_GROUP)
                pn, _ = _attn_probs(k2, q4, bias_t, _sink_row(sink_ref, kv))
                ot = _mm_tn(v2, pn.astype(BF16))
                pieces += [ot[:, g * BLOCK:(g + 1) * BLOCK] for g in range(GQA_GROUP)]
            o_ref[blk, :] = _from_head_major(pieces).astype(BF16)

    return _hosting_call(
        body, "attn_fwd", rows // tm,
        [pl.BlockSpec(memory_space=pltpu.SMEM), _resident((N_BIAS, 2 * BLOCK, GQA_GROUP * BLOCK))]
        + _attn_specs(tm, lambda i: i),
        [pl.BlockSpec((tm, ATTN_WIDTH), lambda i: (i, 0))],
        [jax.ShapeDtypeStruct((rows, ATTN_WIDTH), BF16)],
        [], (sinks, bias, qkv, qkv, qkv, qkv, qkv), carried, modes)


def _conv_taps(xbuf, tm):
    return [xbuf[pl.ds(SUBLANES - (CONV_WIDTH - 1 - j), tm), :] for j in range(CONV_WIDTH)]


def _lru_halves(xc):
    return [xc[:, h * LRU_HALF:(h + 1) * LRU_HALF].astype(BF16) for h in range(2)]


def _lru_gates(xc, wa_ref, ba_ref, wx_ref, bx_ref, lam_ref):
    halves = _lru_halves(xc)
    gate_r = jnp.concatenate([_mm(halves[h], wa_ref[h]) for h in range(2)], axis=1) + ba_ref[...]
    gate_i = jnp.concatenate([_mm(halves[h], wx_ref[h]) for h in range(2)], axis=1) + bx_ref[...]
    r = _sigmoid(gate_r)
    ig = _sigmoid(gate_i)
    log_a = (-LRU_C) * r * _softplus(-lam_ref[...])
    a = jnp.exp(log_a)
    mult, _ = _sqrt_pos(_one_minus_sq_exp(log_a, a))
    return r, ig, a, mult


KEPT_XC, KEPT_A, KEPT_MULT, KEPT_R, KEPT_I, N_KEPT = 0, 1, 2, 3, 4, 5


def _scan_tile(a_ref, u_ref, out_ref, carry, tm, reverse):
    row = lax.broadcasted_iota(jnp.int32, (SUBLANES, LRU_WIDTH), 0)
    groups = tm // SUBLANES

    def step(j, prev):
        jj = groups - 1 - j if reverse else j
        o = pl.multiple_of(jj * SUBLANES, SUBLANES)
        a = a_ref[pl.ds(o, SUBLANES), :]
        u = u_ref[pl.ds(o, SUBLANES), :]
        for s in (1, 2, 4):
            shift = SUBLANES - s if reverse else s
            keep = (row < SUBLANES - s) if reverse else (row >= s)
            u = jnp.where(keep, a * pltpu.roll(u, shift, 0) + u, u)
            a = jnp.where(keep, a * pltpu.roll(a, shift, 0), a)
        out = a * prev + u
        out_ref[pl.ds(o, SUBLANES), :] = out
        return out[0:1, :] if reverse else out[SUBLANES - 1:SUBLANES, :]

    return lax.fori_loop(0, groups, step, carry)


def _rec_fwd(zrec, conv_w, conv_b, wa_bd, b_a, wx_bd, b_x, lam, carried, modes):
    rows = zrec.shape[0]
    tm = _row_tile(rows)

    def body(xr_ref, yr_ref, cw_ref, cb_ref, wa_ref, ba_ref, wx_ref, bx_ref, lam_ref, rec_ref, h_ref, kept_ref,
             xbuf, a_s, u_s, carry):
        i = pl.program_id(0)

        @pl.when(i == 0)
        def _():
            xbuf[0:SUBLANES, :] = jnp.zeros((SUBLANES, LRU_WIDTH), F32)
            carry[...] = jnp.zeros_like(carry)

        @pl.when(i > 0)
        def _():
            xbuf[0:SUBLANES, :] = xbuf[tm:tm + SUBLANES, :]

        xbuf[SUBLANES:SUBLANES + tm, :] = xr_ref[...]
        taps = _conv_taps(xbuf, tm)
        xc = cb_ref[...] + sum(cw_ref[j:j + 1, :] * taps[j] for j in range(CONV_WIDTH))
        r, ig, a, mult = _lru_gates(xc, wa_ref, ba_ref, wx_ref, bx_ref, lam_ref)
        for k, val in ((KEPT_XC, xc), (KEPT_A, a), (KEPT_MULT, mult), (KEPT_R, r), (KEPT_I, ig)):
            kept_ref[:, k * LRU_WIDTH:(k + 1) * LRU_WIDTH] = val
        grow = i * tm + lax.broadcasted_iota(jnp.int32, (tm, LRU_WIDTH), 0)
        a_s[...] = a
        u_s[...] = jnp.where(grow >= PAD_ROWS, mult * (ig * xc), 0.0)
        carry[0:1, :] = _scan_tile(a_s, u_s, h_ref, carry[0:1, :], tm, reverse=False)
        gel, _ = _gelu(yr_ref[...])
        rec_ref[...] = (gel * h_ref[...]).astype(BF16)

    vec = _full((1, LRU_WIDTH))
    bd = _full((2, LRU_HALF, LRU_HALF))
    return _hosting_call(
        body, "rec_fwd", rows // tm,
        [pl.BlockSpec((tm, LRU_WIDTH), lambda i: (i, 0)), pl.BlockSpec((tm, LRU_WIDTH), lambda i: (i, 1)),
         _full((CONV_WIDTH, LRU_WIDTH)), vec, bd, vec, bd, vec, vec],
        [pl.BlockSpec((tm, LRU_WIDTH), lambda i: (i, 0))] * 2 + [pl.BlockSpec((tm, N_KEPT * LRU_WIDTH), lambda i: (i, 0))],
        [jax.ShapeDtypeStruct((rows, LRU_WIDTH), BF16), jax.ShapeDtypeStruct((rows, LRU_WIDTH), F32),
         jax.ShapeDtypeStruct((rows, N_KEPT * LRU_WIDTH), F32)],
        [pltpu.VMEM((tm + SUBLANES, LRU_WIDTH), F32), pltpu.VMEM((tm, LRU_WIDTH), F32),
         pltpu.VMEM((tm, LRU_WIDTH), F32), pltpu.VMEM((SUBLANES, LRU_WIDTH), F32)],
        (zrec, zrec, conv_w, conv_b, wa_bd, b_a, wx_bd, b_x, lam), carried, modes)


def _out_proj_fwd(attn, rec, w_out, head, x, g2, carried, modes):
    rows = attn.shape[0]
    tm = _wide_tile(rows)
    steps = rows // tm

    def body(attn_ref, rec_ref, w_ref, head_ref, g_ref, x_hbm, mix_ref, h1_ref, buf, sem):
        h0 = _h0_tile(head_ref, x_hbm, buf, sem, pl.program_id(0), steps, tm)
        mix = _mm(attn_ref[...], w_ref[0:ATTN_WIDTH, :]) + _mm(rec_ref[...], w_ref[ATTN_WIDTH:, :])
        y, _, _ = _rms_fwd(mix, g_ref[...])
        mix_ref[...] = mix
        h1_ref[...] = h0 + y

    half = pl.BlockSpec((tm, ATTN_WIDTH), lambda i: (i, 0))
    wide = pl.BlockSpec((tm, D_MODEL), lambda i: (i, 0))
    return _hosting_call(
        body, "out_proj_fwd", steps,
        [half, half, _resident((D_MODEL, D_MODEL)), _full((BLOCK, D_MODEL)), _full((1, D_MODEL)), ANY_SPACE],
        [wide, wide],
        [jax.ShapeDtypeStruct((rows, D_MODEL), F32)] * 2,
        _frame_scratch(tm), (attn, rec, w_out, head, g2, x), carried, modes)


FF_COLS = 1024
FF_HALF = FF_CHUNK // 2


def _hidden_at(d, half):
    return half * (D_FF // 2) + d * FF_HALF


def _ffn_up(h1, g3, w1_halves, carried, modes):
    rows = h1.shape[0]
    tm = _wide_tile(rows)

    def body(h_ref, g_ref, wa_ref, wb_ref, act_ref, u_ref):
        u, _, _ = _rms_fwd(h_ref[...], g_ref[...])
        u = u.astype(BF16)
        u_ref[...] = u
        for half, w_ref in enumerate((wa_ref, wb_ref)):
            for d in range(N_DEV):
                c = _hidden_at(d, half)
                a1 = jnp.maximum(_mm(u, w_ref[d]), 0.0)
                act_ref[:, c:c + FF_HALF] = (a1 * a1).astype(BF16)

    wide = pl.BlockSpec((tm, D_MODEL), lambda i: (i, 0))
    return _hosting_call(
        body, "ffn_up", rows // tm,
        [wide, _full((1, D_MODEL))] + [_resident((N_DEV, D_MODEL, FF_HALF))] * 2,
        [pl.BlockSpec((tm, D_FF), lambda i: (i, 0)), wide],
        [jax.ShapeDtypeStruct((rows, D_FF), BF16), jax.ShapeDtypeStruct((rows, D_MODEL), BF16)],
        [], (h1, g3, *w1_halves), carried, modes)


def _ffn_down_loss(act, w2_halves, h1, target, g4, carried, modes):
    rows = h1.shape[0]
    tm = _row_tile(rows)
    steps = rows // tm
    kh = D_FF // 2

    def body(act_ref, wa_ref, wb_ref, h_ref, g_ref, t_hbm, dy_ref, df_ref, dg_ref, loss_ref, buf, sem):
        i = pl.program_id(0)
        slot = _frame_rows(t_hbm, buf, sem, i, steps, tm)

        @pl.when(i == 0)
        def _():
            dg_ref[...] = jnp.zeros_like(dg_ref)
            loss_ref[...] = jnp.zeros_like(loss_ref)
            buf[0, 0:BLOCK, :] = jnp.zeros((BLOCK, D_MODEL), F32)

        g = g_ref[...]
        f = _mm(act_ref[:, :kh], wa_ref[...]) + _mm(act_ref[:, kh:], wb_ref[...])
        y, fhat, rstd = _rms_fwd(f, g)
        grow = i * tm + lax.broadcasted_iota(jnp.int32, (tm, D_MODEL), 0)
        err = jnp.where(grow >= BLOCK, h_ref[...] + y - buf[slot], 0.0)
        loss_ref[...] += (0.5 / D_MODEL) * jnp.sum(err * err)
        dy = err * (1.0 / D_MODEL)
        df, dg = _rms_bwd(dy, fhat, rstd, g)
        dy_ref[...] = dy
        df_ref[...] = df.astype(BF16)
        dg_ref[...] += dg

    wide = pl.BlockSpec((tm, D_MODEL), lambda i: (i, 0))
    return _hosting_call(
        body, "ffn_down_loss", steps,
        [pl.BlockSpec((tm, D_FF), lambda i: (i, 0)), _resident((kh, D_MODEL)), _resident((kh, D_MODEL)), wide,
         _full((1, D_MODEL)), ANY_SPACE],
        [wide, wide, _full((1, D_MODEL)), _full((SUBLANES, LANES))],
        [jax.ShapeDtypeStruct((rows, D_MODEL), F32), jax.ShapeDtypeStruct((rows, D_MODEL), BF16),
         jax.ShapeDtypeStruct((1, D_MODEL), F32), jax.ShapeDtypeStruct((SUBLANES, LANES), F32)],
        _frame_scratch(tm), (act, *w2_halves, h1, g4, target), carried, modes)


def _ffn_bwd_act(df, w2t_halves, act, carried, modes):
    rows = df.shape[0]
    tm = _wide_tile(rows)

    def body(df_ref, wa_ref, wb_ref, act_ref, da_ref):
        df_t = df_ref[...]
        for half, w_ref in enumerate((wa_ref, wb_ref)):
            for d in range(N_DEV):
                cols = slice(_hidden_at(d, half), _hidden_at(d, half) + FF_HALF)
                dact = _mm(df_t, w_ref[d])
                relu_a1, _ = _sqrt_pos(act_ref[:, cols].astype(F32))
                da_ref[:, cols] = (dact * (2.0 * relu_a1)).astype(BF16)

    hidden = pl.BlockSpec((tm, D_FF), lambda i: (i, 0))
    return _hosting_call(
        body, "ffn_bwd_act", rows // tm,
        [pl.BlockSpec((tm, D_MODEL), lambda i: (i, 0))] + [_resident((N_DEV, D_MODEL, FF_HALF))] * 2 + [hidden],
        [hidden],
        [jax.ShapeDtypeStruct((rows, D_FF), BF16)],
        [], (df, *w2t_halves, act), carried, modes)


def _ffn_bwd_x(da, w1t_halves, h1, dy, g3, carried, modes):
    rows = h1.shape[0]
    tm = _row_tile(rows)
    kh = D_FF // 2

    def body(da_ref, wa_ref, wb_ref, h_ref, dy_ref, g_ref, dh_ref, dg_ref):
        @pl.when(pl.program_id(0) == 0)
        def _():
            dg_ref[...] = jnp.zeros_like(dg_ref)

        g = g_ref[...]
        _, xhat, rstd = _rms_fwd(h_ref[...], g)
        du = _mm(da_ref[:, :kh], wa_ref[...]) + _mm(da_ref[:, kh:], wb_ref[...])
        dx, dg = _rms_bwd(du, xhat, rstd, g)
        dh_ref[...] = dy_ref[...] + dx
        dg_ref[...] += dg

    wide = pl.BlockSpec((tm, D_MODEL), lambda i: (i, 0))
    return _hosting_call(
        body, "ffn_bwd_x", rows // tm,
        [pl.BlockSpec((tm, D_FF), lambda i: (i, 0)), _resident((kh, D_MODEL)), _resident((kh, D_MODEL)), wide, wide,
         _full((1, D_MODEL))],
        [wide, _full((1, D_MODEL))],
        [jax.ShapeDtypeStruct((rows, D_MODEL), F32), jax.ShapeDtypeStruct((1, D_MODEL), F32)],
        [], (da, *w1t_halves, h1, dy, g3), carried, modes)


def _ffn_bwd_weights(u2, da, act, df, carried, modes):
    rows = u2.shape[0]
    tb = _big_tile(rows)
    steps = rows // tb
    per = FF_COLS // FF_HALF

    def body(u_ref, da_ref, act_ref, df_ref, dw1_ref, dw2_ref, acc1, acc2):
        i = pl.program_id(1)

        @pl.when(i == 0)
        def _():
            acc1[...] = jnp.zeros_like(acc1)
            acc2[...] = jnp.zeros_like(acc2)

        acc1[...] += _mm_tn(u_ref[...], da_ref[...])
        acc2[...] += _mm_tn(act_ref[...], df_ref[...])

        @pl.when(i == steps - 1)
        def _():
            for p in range(per):
                c = p * FF_HALF
                dw1_ref[p] = acc1[:, c:c + FF_HALF].astype(BF16)
                dw2_ref[p] = acc2[c:c + FF_HALF, :].astype(BF16)

    wide = pl.BlockSpec((tb, D_MODEL), lambda j, i: (i, 0))
    chunk = pl.BlockSpec((tb, FF_COLS), lambda j, i: (i, j))
    return _hosting_call(
        body, "ffn_bwd_weights", (D_FF // FF_COLS, steps),
        [wide, chunk, chunk, wide],
        [pl.BlockSpec((None, per, D_MODEL, FF_HALF), lambda j, i: (j // 2, j % 2, 0, 0)),
         pl.BlockSpec((per, FF_HALF, D_MODEL), lambda j, i: (j % 2, j // 2, 0))],
        [jax.ShapeDtypeStruct((2, N_DEV, D_MODEL, FF_HALF), BF16), jax.ShapeDtypeStruct((N_DEV, FF_CHUNK, D_MODEL), BF16)],
        [pltpu.VMEM((D_MODEL, FF_COLS), F32), pltpu.VMEM((FF_COLS, D_MODEL), F32)],
        (u2, da, act, df), carried, modes)


def _out_proj_bwd(dh1, mix, g2, w_out_t, attn, rec, carried, modes):
    rows = dh1.shape[0]
    tm = _row_tile(rows)
    steps = rows // tm

    def body(dh_ref, mix_ref, g_ref, w_ref, attn_ref, rec_ref, dattn_ref, drec_ref, dw_ref, dg_ref, acc):
        i = pl.program_id(0)

        @pl.when(i == 0)
        def _():
            acc[...] = jnp.zeros_like(acc)
            dg_ref[...] = jnp.zeros_like(dg_ref)

        g = g_ref[...]
        _, xhat, rstd = _rms_fwd(mix_ref[...], g)
        dmix, dg = _rms_bwd(dh_ref[...], xhat, rstd, g)
        dmix = dmix.astype(BF16)
        dg_ref[...] += dg
        din = _mm(dmix, w_ref[...])
        dattn_ref[...] = din[:, :ATTN_WIDTH].astype(BF16)
        drec_ref[...] = din[:, ATTN_WIDTH:]
        acc[0:ATTN_WIDTH, :] += _mm_tn(attn_ref[...], dmix)
        acc[ATTN_WIDTH:, :] += _mm_tn(rec_ref[...], dmix)

        @pl.when(i == steps - 1)
        def _():
            dw_ref[...] = acc[...].astype(BF16)

    half = pl.BlockSpec((tm, ATTN_WIDTH), lambda i: (i, 0))
    wide = pl.BlockSpec((tm, D_MODEL), lambda i: (i, 0))
    return _hosting_call(
        body, "out_proj_bwd", steps,
        [wide, wide, _full((1, D_MODEL)), _resident((D_MODEL, D_MODEL)), half, half],
        [half, half, _full((D_MODEL, D_MODEL)), _full((1, D_MODEL))],
        [jax.ShapeDtypeStruct((rows, ATTN_WIDTH), BF16), jax.ShapeDtypeStruct((rows, LRU_WIDTH), F32),
         jax.ShapeDtypeStruct((D_MODEL, D_MODEL), BF16), jax.ShapeDtypeStruct((1, D_MODEL), F32)],
        [pltpu.VMEM((D_MODEL, D_MODEL), F32)],
        (dh1, mix, g2, w_out_t, attn, rec), carried, modes)


def _attn_bwd(qkv, dattn, sinks, bias, carried, modes):
    rows = qkv.shape[0]
    tm = _row_tile(rows)
    nbt, nt = tm // BLOCK, rows // tm

    def body(sink_ref, bias_ref, do_ref, q_ref, kp_ref, kc_ref, vp_ref, vc_ref, dq_ref, dkv_ref, dsink_ref, dk_c, dv_c):
        i = pl.program_id(0)

        @pl.when(i == 0)
        def _():
            dk_c[...] = jnp.zeros_like(dk_c)
            dv_c[...] = jnp.zeros_like(dv_c)
            dsink_ref[...] = jnp.zeros_like(dsink_ref)

        @pl.when(i < nt)
        def _():
            dk_late, dv_late = dk_c[...], dv_c[...]
            dsink_rows = [jnp.zeros((1, LANES), F32)] * ATTN_HEADS
            for b in range(nbt):
                blk = slice(b * BLOCK, (b + 1) * BLOCK)
                bias_t = _bias_of_block(bias_ref, i * nbt + b)
                dq_parts, dk_parts, dv_parts = [], [], []
                for kv in range(KV_HEADS):
                    k2 = _keys_of_block(kp_ref, kc_ref, b, kv)
                    v2 = _keys_of_block(vp_ref, vc_ref, b, kv)
                    q4 = _heads(q_ref, blk, kv * GQA_GROUP, GQA_GROUP)
                    do4 = _heads(do_ref, blk, kv * GQA_GROUP, GQA_GROUP)
                    pn, psink = _attn_probs(k2, q4, bias_t, _sink_row(sink_ref, kv))
                    dpn = _mm_nt(v2, do4)
                    delta = jnp.sum(pn * dpn, axis=0, keepdims=True)
                    ds = ((pn * (dpn - delta)) * (HEAD_DIM ** -0.5)).astype(BF16)
                    dqt = _mm_tn(k2, ds)
                    dq_parts += [dqt[:, g * BLOCK:(g + 1) * BLOCK] for g in range(GQA_GROUP)]
                    dk_parts.append(_mm(ds, q4))
                    dv_parts.append(_mm(pn.astype(BF16), do4))
                    sd = psink * delta
                    for g in range(GQA_GROUP):
                        h = kv * GQA_GROUP + g
                        dsink_rows[h] = dsink_rows[h] - jnp.sum(sd[:, g * BLOCK:(g + 1) * BLOCK])
                dq_ref[blk, :] = _from_head_major(dq_parts).astype(BF16)
                dk2 = jnp.concatenate(dk_parts, axis=1)
                dv2 = jnp.concatenate(dv_parts, axis=1)
                dkv_ref[blk, 0:KV_WIDTH] = (dk_late + dk2[0:BLOCK]).astype(BF16)
                dkv_ref[blk, KV_WIDTH:] = (dv_late + dv2[0:BLOCK]).astype(BF16)
                dk_late, dv_late = dk2[BLOCK:], dv2[BLOCK:]
            dk_c[...] = dk_late
            dv_c[...] = dv_late
            dsink_ref[...] += jnp.concatenate(dsink_rows, axis=0)

        @pl.when(i == nt)
        def _():
            dkv_ref[...] = jnp.zeros_like(dkv_ref)
            dkv_ref[0:BLOCK, 0:KV_WIDTH] = dk_c[...].astype(BF16)
            dkv_ref[0:BLOCK, KV_WIDTH:] = dv_c[...].astype(BF16)

    tile_of = lambda i: jnp.minimum(i, nt - 1)
    tile = pl.BlockSpec((tm, ATTN_WIDTH), lambda i: (tile_of(i), 0))
    return _hosting_call(
        body, "attn_bwd", nt + 1,
        [pl.BlockSpec(memory_space=pltpu.SMEM), _resident((N_BIAS, 2 * BLOCK, GQA_GROUP * BLOCK)), tile]
        + _attn_specs(tm, tile_of),
        [tile, pl.BlockSpec((tm, 2 * KV_WIDTH), lambda i: (i, 0)), _full((ATTN_HEADS, LANES))],
        [jax.ShapeDtypeStruct((rows, ATTN_WIDTH), BF16), jax.ShapeDtypeStruct((rows + tm, 2 * KV_WIDTH), BF16),
         jax.ShapeDtypeStruct((ATTN_HEADS, LANES), F32)],
        [pltpu.VMEM((BLOCK, KV_WIDTH), F32), pltpu.VMEM((BLOCK, KV_WIDTH), F32)],
        (sinks, bias, dattn, qkv, qkv, qkv, qkv, qkv), carried, modes)


ROW_CONV_B, ROW_B_A, ROW_B_X, ROW_LAMBDA = 4, 5, 6, 7


def _rec_bwd(drec, zrec, h, kept, conv_w, wa_bd, wx_bd, lam, carried, modes):
    rows = zrec.shape[0]
    tm = _rec_tile(rows)
    nt = rows // tm
    per = tm // SUBLANES

    def body(drec_ref, xr_ref, yr_ref, h_ref, xc_ref, a_ref, mult_ref, r_ref, ig_ref, hhalo_ref, cw_ref, wa_ref, wx_ref,
             lam_ref, drz_ref, small_ref, dwa_ref, dwx_ref, hbuf, abuf, u_s, g_s, dbuf, carry):
        s = pl.program_id(0)
        i = nt - 1 - s

        @pl.when(s == 0)
        def _():
            small_ref[...] = jnp.zeros_like(small_ref)
            dwa_ref[...] = jnp.zeros_like(dwa_ref)
            dwx_ref[...] = jnp.zeros_like(dwx_ref)
            carry[...] = jnp.zeros_like(carry)
            abuf[tm:tm + SUBLANES, :] = jnp.zeros((SUBLANES, LRU_WIDTH), F32)
            dbuf[tm:tm + SUBLANES, :] = jnp.zeros((SUBLANES, LRU_WIDTH), F32)

        hbuf[0:SUBLANES, :] = jnp.where(i == 0, 0.0, hhalo_ref[...])
        hbuf[SUBLANES:SUBLANES + tm, :] = h_ref[...]

        xc, a, mult, r, ig = xc_ref[...], a_ref[...], mult_ref[...], r_ref[...], ig_ref[...]
        halves = _lru_halves(xc)
        inv_mult = pl.reciprocal(mult, approx=True)

        yr = yr_ref[...]
        gel, t = _gelu(yr)
        drec_t = drec_ref[...]
        dyr = drec_t * h_ref[...] * _gelu_grad(yr, t)

        abuf[0:tm, :] = a
        u_s[...] = drec_t * gel
        a_next = abuf[pl.ds(1, tm), :]
        abuf[0:tm, :] = a_next
        carry[0:1, :] = _scan_tile(abuf, u_s, g_s, carry[0:1, :], tm, reverse=True)
        abuf[tm:tm + 1, :] = a[0:1, :]
        g = g_s[...]

        grow = i * tm + lax.broadcasted_iota(jnp.int32, (tm, LRU_WIDTH), 0)
        du = jnp.where(grow >= PAD_ROWS, g, 0.0)
        da = g * hbuf[pl.ds(SUBLANES - 1, tm), :]
        dmult = du * (ig * xc)
        dig = du * (mult * xc)
        dxc = du * (mult * ig)
        dlog_a = da * a - dmult * (a * a * inv_mult)
        sp = _softplus(-lam_ref[...])
        dgr = (dlog_a * (-LRU_C) * sp) * (r * (1.0 - r))
        dgi = dig * (ig * (1.0 - ig))
        dlam = jnp.sum(dlog_a * r, axis=0, keepdims=True) * (LRU_C * _sigmoid(-lam_ref[...]))
        dgr_b = [dgr[:, hh * LRU_HALF:(hh + 1) * LRU_HALF].astype(BF16) for hh in range(2)]
        dgi_b = [dgi[:, hh * LRU_HALF:(hh + 1) * LRU_HALF].astype(BF16) for hh in range(2)]
        dxc = dxc + jnp.concatenate(
            [_mm_nt(dgr_b[hh], wa_ref[hh]) + _mm_nt(dgi_b[hh], wx_ref[hh]) for hh in range(2)], axis=1)
        for hh in range(2):
            dwa_ref[hh] += _mm_tn(halves[hh], dgr_b[hh])
            dwx_ref[hh] += _mm_tn(halves[hh], dgi_b[hh])

        dbuf[0:tm, :] = dxc
        ahead = [dbuf[pl.ds(CONV_WIDTH - 1 - j, tm), :] for j in range(CONV_WIDTH)]
        dxr = sum(cw_ref[j:j + 1, :] * ahead[j] for j in range(CONV_WIDTH))
        dbuf[tm:tm + SUBLANES, :] = dxc[0:SUBLANES, :]
        drz_ref[:, 0:LRU_WIDTH] = dxr.astype(BF16)
        drz_ref[:, LRU_WIDTH:] = dyr.astype(BF16)

        xr = xr_ref[...]
        upd = [jnp.sum(xr * ahead[j], axis=0, keepdims=True) for j in range(CONV_WIDTH)]
        upd += [jnp.sum(dxc, axis=0, keepdims=True), jnp.sum(dgr, axis=0, keepdims=True),
                jnp.sum(dgi, axis=0, keepdims=True), dlam]
        small_ref[...] += jnp.concatenate(upd, axis=0)

    rev = lambda s: nt - 1 - s
    halo = lambda s: jnp.maximum(rev(s) * per - 1, 0)
    cols = lambda k: pl.BlockSpec((tm, LRU_WIDTH), lambda s: (rev(s), k))
    halo0 = pl.BlockSpec((SUBLANES, LRU_WIDTH), lambda s: (halo(s), 0))
    bd = _full((2, LRU_HALF, LRU_HALF))
    big = pltpu.VMEM((tm + SUBLANES, LRU_WIDTH), F32)
    tile = pltpu.VMEM((tm, LRU_WIDTH), F32)
    kept_cols = [cols(k) for k in (KEPT_XC, KEPT_A, KEPT_MULT, KEPT_R, KEPT_I)]
    return _hosting_call(
        body, "rec_bwd", nt,
        [cols(0), cols(0), cols(1), cols(0)] + kept_cols
        + [halo0, _full((CONV_WIDTH, LRU_WIDTH)), bd, bd, _full((1, LRU_WIDTH))],
        [pl.BlockSpec((tm, 2 * LRU_WIDTH), lambda s: (rev(s), 0)), _full((SUBLANES, LRU_WIDTH)), bd, bd],
        [jax.ShapeDtypeStruct((rows, 2 * LRU_WIDTH), BF16), jax.ShapeDtypeStruct((SUBLANES, LRU_WIDTH), F32),
         jax.ShapeDtypeStruct((2, LRU_HALF, LRU_HALF), F32), jax.ShapeDtypeStruct((2, LRU_HALF, LRU_HALF), F32)],
        [big, big, tile, tile, big, pltpu.VMEM((SUBLANES, LRU_WIDTH), F32)],
        (drec, zrec, zrec, h) + (kept,) * N_KEPT + (h, conv_w, wa_bd, wx_bd, lam), carried, modes)


DZ_CUTS = (0, ATTN_WIDTH, QKV_WIDTH, IN_WIDTH)


def _dz_specs(tm):
    return [pl.BlockSpec((tm, DZ_CUTS[p + 1] - DZ_CUTS[p]), lambda i: (i, 0)) for p in range(3)]


def _in_proj_bwd_x(head, x, g1, dh1, dq, dkv, drz, w_in_t, carried, modes):
    rows = dh1.shape[0]
    tm = _row_tile(rows)
    steps = rows // tm

    def body(head_ref, g_ref, dh1_ref, dq_ref, dkv_ref, drz_ref, w_ref, x_hbm, dh0_ref, dg_ref, buf, sem):
        i = pl.program_id(0)
        h0 = _h0_tile(head_ref, x_hbm, buf, sem, i, steps, tm)

        @pl.when(i == 0)
        def _():
            dg_ref[...] = jnp.zeros_like(dg_ref)

        g = g_ref[...]
        _, xhat, rstd = _rms_fwd(h0, g)
        parts = (dq_ref[...], dkv_ref[...], drz_ref[...])
        du = sum(_mm(parts[p], w_ref[DZ_CUTS[p]:DZ_CUTS[p + 1], :]) for p in range(3))
        dx, dg = _rms_bwd(du, xhat, rstd, g)
        dh0_ref[...] = dh1_ref[...] + dx
        dg_ref[...] += dg

    wide = pl.BlockSpec((tm, D_MODEL), lambda i: (i, 0))
    return _hosting_call(
        body, "in_proj_bwd_x", steps,
        [_full((BLOCK, D_MODEL)), _full((1, D_MODEL)), wide] + _dz_specs(tm) + [_resident((IN_WIDTH, D_MODEL)), ANY_SPACE],
        [wide, _full((1, D_MODEL))],
        [jax.ShapeDtypeStruct((rows, D_MODEL), F32), jax.ShapeDtypeStruct((1, D_MODEL), F32)],
        _frame_scratch(tm), (head, g1, dh1, dq, dkv, drz, w_in_t, x), carried, modes)


def _in_proj_bwd_w(u1, dq, dkv, drz, carried, modes):
    rows = u1.shape[0]
    tb = _big_tile(rows)
    steps = rows // tb

    def body(u_ref, dq_ref, dkv_ref, drz_ref, dw_ref, acc):
        i = pl.program_id(0)

        @pl.when(i == 0)
        def _():
            acc[...] = jnp.zeros_like(acc)

        u = u_ref[...]
        for p, ref in enumerate((dq_ref, dkv_ref, drz_ref)):
            acc[:, DZ_CUTS[p]:DZ_CUTS[p + 1]] += _mm_tn(u, ref[...])

        @pl.when(i == steps - 1)
        def _():
            dw_ref[...] = acc[...].astype(BF16)

    return _hosting_call(
        body, "in_proj_bwd_w", steps,
        [pl.BlockSpec((tb, D_MODEL), lambda i: (i, 0))] + _dz_specs(tb),
        [_full((D_MODEL, IN_WIDTH))],
        [jax.ShapeDtypeStruct((D_MODEL, IN_WIDTH), BF16)],
        [pltpu.VMEM((D_MODEL, IN_WIDTH), F32)], (u1, dq, dkv, drz), carried, modes)


def _adamw_math(w, m, v, g):
    nm = ADAM_B1 * m + (1.0 - ADAM_B1) * g
    nv = ADAM_B2 * v + (1.0 - ADAM_B2) * (g * g)
    m_hat = nm / (1.0 - ADAM_B1 ** ADAM_STEP)
    v_hat = nv / (1.0 - ADAM_B2 ** ADAM_STEP)
    return (-ADAM_LR) * (m_hat / (jnp.sqrt(v_hat) + ADAM_EPS) + ADAM_WD * w), nm, nv


SMALL_NAMES = ("conv_b", "b_a", "b_x", "lru_lambda", "attn_sinks", "g_post_mix", "g_pre_ffn", "g_post_ffn")
PACK_WIDTH = 1024


def _pack_rows(vals):
    assert len(SMALL_NAMES) == SUBLANES
    row = lax.broadcasted_iota(jnp.int32, (SUBLANES, PACK_WIDTH), 0)
    tile = jnp.zeros((SUBLANES, PACK_WIDTH), F32)
    for k, name in enumerate(SMALL_NAMES):
        a = vals[name].reshape(1, -1)
        tile = jnp.where(row == k, jnp.pad(a, ((0, 0), (0, PACK_WIDTH - a.shape[1]))), tile)
    return tile


def _adamw_small(weights, mom_m, mom_v, parts, loss_parts):
    n = len(SMALL_NAMES)
    views = [(1, weights[name].size) for name in SMALL_NAMES]

    def body(*refs):
        w_refs, m_refs, v_refs = refs[:n], refs[n:2 * n], refs[2 * n:3 * n]
        p_ref, l_ref, loss_ref = refs[3 * n], refs[3 * n + 1], refs[3 * n + 2]
        outs = refs[3 * n + 3:]
        for k, (_, c) in enumerate(views):
            g = p_ref[0, k:k + 1, 0:c]
            for s in range(1, N_DEV):
                g = g + p_ref[s, k:k + 1, 0:c]
            g_ref, d_ref, nm_ref, nv_ref = outs[4 * k:4 * k + 4]
            g_ref[...] = g
            d_ref[...], nm_ref[...], nv_ref[...] = _adamw_math(w_refs[k][...], m_refs[k][...], v_refs[k][...], g)
        total = l_ref[0]
        for s in range(1, N_DEV):
            total = total + l_ref[s]
        loss_ref[...] = total

    args = [src[name].reshape(view) for src in (weights, mom_m, mom_v) for name, view in zip(SMALL_NAMES, views)]
    res = pl.pallas_call(
        body, name="adamw_small",
        out_shape=[jax.ShapeDtypeStruct(loss_parts.shape[1:], F32)]
                  + [jax.ShapeDtypeStruct(view, F32) for view in views for _ in range(4)],
        compiler_params=pltpu.CompilerParams(vmem_limit_bytes=VMEM_LIMIT),
    )(*args, parts, loss_parts)
    out = {name: tuple(t.reshape(weights[name].shape) for t in res[1 + 4 * k:5 + 4 * k]) for k, name in enumerate(SMALL_NAMES)}
    return res[0], out


def _adamw(w, m, v, parts, name):
    rows, cols = w.shape
    tr = next((t for t in (256, 128) if rows % t == 0), rows)
    parts = parts if isinstance(parts, (list, tuple)) else [parts]

    def body(w_ref, m_ref, v_ref, *refs):
        p_refs, (g_ref, d_ref, nm_ref, nv_ref) = refs[:len(parts)], refs[len(parts):]

        def total(p_ref):
            g = p_ref[0].astype(F32)
            for s in range(1, N_DEV):
                g = g + p_ref[s].astype(F32)
            return g

        g = jnp.concatenate([total(p_ref) for p_ref in p_refs], axis=1) if len(parts) > 1 else total(p_refs[0])
        g_ref[...] = g
        d_ref[...], nm_ref[...], nv_ref[...] = _adamw_math(w_ref[...], m_ref[...], v_ref[...], g)

    blk = pl.BlockSpec((tr, cols), lambda i: (i, 0))
    return pl.pallas_call(
        body, name=name, grid=(rows // tr,),
        in_specs=[blk, blk, blk] + [pl.BlockSpec((N_DEV, tr, p.shape[2]), lambda i: (0, i, 0)) for p in parts],
        out_specs=[blk] * 4,
        out_shape=[jax.ShapeDtypeStruct((rows, cols), F32)] * 4,
        compiler_params=_params(("parallel",)),
    )(w, m, v, *parts)


def _cols_from_shards(g):
    return jnp.transpose(g, (1, 0, 2)).reshape(g.shape[1], N_DEV * g.shape[2])


def _cols_to_shards(a):
    r, c = a.shape
    return jnp.transpose(a.reshape(r, N_DEV, c // N_DEV), (1, 0, 2))


def _block_diag(w):
    per = LRU_HALF // LRU_BLOCK
    w = w.reshape(2, per, LRU_BLOCK, LRU_BLOCK)
    eye = jnp.eye(per, dtype=w.dtype)
    return (w[:, :, :, None, :] * eye[None, :, None, :, None]).reshape(2, LRU_HALF, LRU_HALF)


def _block_diag_extract(t):
    per = LRU_HALF // LRU_BLOCK
    t = t.reshape(2, per, LRU_BLOCK, per, LRU_BLOCK)
    return jnp.stack([t[:, b, :, b, :] for b in range(per)], axis=1).reshape(LRU_BLOCKS, LRU_BLOCK, LRU_BLOCK)


def kernel(x, meta_tokens, g_pre_mix, w_in, conv_w, conv_b, w_a, b_a, w_x, b_x, lru_lambda, attn_sinks, w_out, g_post_mix, g_pre_ffn, w_ff1, w_ff2, g_post_ffn, loss_target, m_meta_tokens, m_g_pre_mix, m_w_in, m_conv_w, m_conv_b, m_w_a, m_b_a, m_w_x, m_b_x, m_lru_lambda, m_attn_sinks, m_w_out, m_g_post_mix, m_g_pre_ffn, m_w_ff1, m_w_ff2, m_g_post_ffn, v_meta_tokens, v_g_pre_mix, v_w_in, v_conv_w, v_conv_b, v_w_a, v_b_a, v_w_x, v_b_x, v_lru_lambda, v_attn_sinks, v_w_out, v_g_post_mix, v_g_pre_ffn, v_w_ff1, v_w_ff2, v_g_post_ffn):
    weights = dict(meta_tokens=meta_tokens, g_pre_mix=g_pre_mix, w_in=w_in, conv_w=conv_w, conv_b=conv_b, w_a=w_a,
                   b_a=b_a, w_x=w_x, b_x=b_x, lru_lambda=lru_lambda, attn_sinks=attn_sinks, w_out=w_out,
                   g_post_mix=g_post_mix, g_pre_ffn=g_pre_ffn, w_ff1=w_ff1, w_ff2=w_ff2, g_post_ffn=g_post_ffn)
    mom_m = dict(meta_tokens=m_meta_tokens, g_pre_mix=m_g_pre_mix, w_in=m_w_in, conv_w=m_conv_w, conv_b=m_conv_b,
                 w_a=m_w_a, b_a=m_b_a, w_x=m_w_x, b_x=m_b_x, lru_lambda=m_lru_lambda, attn_sinks=m_attn_sinks,
                 w_out=m_w_out, g_post_mix=m_g_post_mix, g_pre_ffn=m_g_pre_ffn, w_ff1=m_w_ff1, w_ff2=m_w_ff2,
                 g_post_ffn=m_g_post_ffn)
    mom_v = dict(meta_tokens=v_meta_tokens, g_pre_mix=v_g_pre_mix, w_in=v_w_in, conv_w=v_conv_w, conv_b=v_conv_b,
                 w_a=v_w_a, b_a=v_b_a, w_x=v_w_x, b_x=v_b_x, lru_lambda=v_lru_lambda, attn_sinks=v_attn_sinks,
                 w_out=v_w_out, g_post_mix=v_g_post_mix, g_pre_ffn=v_g_pre_ffn, w_ff1=v_w_ff1, w_ff2=v_w_ff2,
                 g_post_ffn=v_g_post_ffn)
    order = list(weights)

    (g_win, g_meta, g_cw) = _gather_two_level([w_in[0].astype(BF16), meta_tokens, conv_w[0]], "gather_first")
    w_in_full = _cols_from_shards(g_win)
    meta_full = _cols_from_shards(g_meta)
    conv_w_full = _cols_from_shards(g_cw)

    head = jnp.concatenate([jnp.zeros((PAD_ROWS, D_MODEL), F32), meta_full], axis=0)
    wa_bd = _block_diag(w_a[0]).astype(BF16)
    wx_bd = _block_diag(w_x[0]).astype(BF16)
    bias = _attn_bias()

    w1_shard = w_ff1[0].astype(BF16)
    (qkv, zrec, u1), (g_wout,) = _in_proj_fwd(head, x[0], g_pre_mix, w_in_full, [w_out[0].astype(BF16)], ["gather"])
    (attn,), (w1a,) = _attn_fwd(qkv, attn_sinks, bias, [w1_shard[:, :FF_HALF]], ["gather"])
    (rec, h_lru, kept), (w1b,) = _rec_fwd(zrec, conv_w_full, conv_b, wa_bd, b_a, wx_bd, b_x, lru_lambda,
                                         [w1_shard[:, FF_HALF:]], ["gather"])
    w_out_full = g_wout.reshape(D_MODEL, D_MODEL)
    w2_shard = w_ff2[0].astype(BF16)
    (mix, h1), (w2a,) = _out_proj_fwd(attn, rec, w_out_full, head, x[0], g_post_mix, [w2_shard[:FF_HALF]], ["gather"])
    (act, u2), (w2b,) = _ffn_up(h1, g_pre_ffn, (w1a, w1b), [w2_shard[FF_HALF:]], ["gather"])
    w2_halves = [w.reshape(D_FF // 2, D_MODEL) for w in (w2a, w2b)]
    (dy, df, dg_post_ffn, loss_acc), w2t_halves = _ffn_down_loss(
        act, w2_halves, h1, loss_target[0], g_post_ffn, [w2_shard[:FF_HALF].T, w2_shard[FF_HALF:].T], ["gather"] * 2)

    (da1,), (w1ta,) = _ffn_bwd_act(df, w2t_halves, act, [w1_shard[:, :FF_HALF].T], ["gather"])
    (dw1h, dw2g), (w1tb,) = _ffn_bwd_weights(u2, da1, act, df, [w1_shard[:, FF_HALF:].T], ["gather"])
    w1t_halves = [w.reshape(D_FF // 2, D_MODEL) for w in (w1ta, w1tb)]
    (dh1, dg_pre_ffn), (p_w1a,) = _ffn_bwd_x(da1, w1t_halves, h1, dy, g_pre_ffn, [dw1h[0]], ["scatter"])
    (dattn, drec, dw_out, dg_post_mix), (p_w1b,) = _out_proj_bwd(dh1, mix, g_post_mix, w_out_full.T, attn, rec,
                                                                [dw1h[1]], ["scatter"])
    (dq, dkv_late, dsinks), (p_w2,) = _attn_bwd(qkv, dattn, attn_sinks, bias, [dw2g], ["scatter"])
    dkv = dkv_late[BLOCK:BLOCK + qkv.shape[0]]
    (drz, rec_small, dwa_bd, dwx_bd), (p_wout,) = _rec_bwd(
        drec, zrec, h_lru, kept, conv_w_full, wa_bd, wx_bd, lru_lambda,
        [dw_out.reshape(N_DEV, D_MODEL // N_DEV, D_MODEL)], ["scatter"])
    small_grads = dict(
        conv_b=rec_small[ROW_CONV_B], b_a=rec_small[ROW_B_A], b_x=rec_small[ROW_B_X], lru_lambda=rec_small[ROW_LAMBDA],
        attn_sinks=dsinks[:, 0], g_post_mix=dg_post_mix, g_pre_ffn=dg_pre_ffn, g_post_ffn=dg_post_ffn)
    gate_rows = (LRU_BLOCKS * LRU_BLOCK, LRU_BLOCK)
    gate_dense = (LRU_BLOCKS * LRU_BLOCK * LRU_BLOCK // PACK_WIDTH, PACK_WIDTH)
    (dw_in,), (p_cw, p_small, p_wa, p_wx) = _in_proj_bwd_w(
        u1, dq, dkv, drz,
        [_cols_to_shards(rec_small[0:CONV_WIDTH]), _pack_rows(small_grads),
         _block_diag_extract(dwa_bd).reshape(gate_dense), _block_diag_extract(dwx_bd).reshape(gate_dense)],
        ["scatter", "gather", "gather", "gather"])
    p_wa, p_wx = (p.reshape((N_DEV,) + gate_rows) for p in (p_wa, p_wx))
    (dh0, dg_pre_mix), (p_win,) = _in_proj_bwd_x(
        head, x[0], g_pre_mix, dh1, dq, dkv, drz, w_in_full.T, [_cols_to_shards(dw_in)], ["scatter"])
    p_meta, p_gpm, p_loss = _exchange([_cols_to_shards(dh0[PAD_ROWS:BLOCK]), dg_pre_mix, loss_acc],
                                      ["scatter", "gather", "gather"], "exchange_last")

    res = {}
    res["g_pre_mix"] = _adamw(g_pre_mix, m_g_pre_mix, v_g_pre_mix, p_gpm, "adamw_g_pre_mix")
    res["w_in"] = _adamw(w_in[0], m_w_in[0], v_w_in[0], p_win, "adamw_w_in")
    res["w_out"] = _adamw(w_out[0], m_w_out[0], v_w_out[0], p_wout, "adamw_w_out")
    res["w_ff1"] = _adamw(w_ff1[0], m_w_ff1[0], v_w_ff1[0], [p_w1a, p_w1b], "adamw_w_ff1")
    res["w_ff2"] = _adamw(w_ff2[0], m_w_ff2[0], v_w_ff2[0], p_w2, "adamw_w_ff2")
    res["meta_tokens"] = _adamw(meta_tokens, m_meta_tokens, v_meta_tokens, p_meta, "adamw_meta")
    res["conv_w"] = _adamw(conv_w[0], m_conv_w[0], v_conv_w[0], p_cw, "adamw_conv_w")
    for name in ("w_in", "w_out", "w_ff1", "w_ff2", "conv_w"):
        res[name] = tuple(t[None] for t in res[name])
    for name, parts in (("w_a", p_wa), ("w_x", p_wx)):
        gate = _adamw(*(src[name].reshape(gate_rows) for src in (weights, mom_m, mom_v)), parts, "adamw_" + name)
        res[name] = tuple(t.reshape(weights[name].shape) for t in gate)
    loss_total, small = _adamw_small(weights, mom_m, mom_v, p_small, p_loss)
    res.update(small)

    grad_x = dh0[BLOCK:][None]
    outs = [loss_total[0, 0], grad_x]
    for k in range(4):
        outs += [res[name][k] for name in order]
    return tuple(outs)
```

```python
import jax
import jax.numpy as jnp
import numpy as np
from jax import lax
from jax.experimental import pallas as pl
from jax.experimental.pallas import tpu as pltpu

F32 = jnp.float32
BF16 = jnp.bfloat16

D_MODEL = 1024
N_META = 16
HEAD_DIM = 64
ATTN_HEADS = 8
KV_HEADS = 2
GQA_GROUP = ATTN_HEADS // KV_HEADS
ATTN_WIDTH = ATTN_HEADS * HEAD_DIM
KV_WIDTH = KV_HEADS * HEAD_DIM
QKV_WIDTH = ATTN_WIDTH + 2 * KV_WIDTH
LRU_WIDTH = 512
LRU_BLOCKS = 8
LRU_BLOCK = 64
LRU_HALF = 256
LRU_C = 8.0
CONV_WIDTH = 4
BLOCK = 128
PAD_ROWS = BLOCK - N_META
IN_WIDTH = QKV_WIDTH + 2 * LRU_WIDTH
D_FF = 4096
EPS = 1e-6
NEG = -1e30
N_DEV = 8
FF_CHUNK = D_FF // N_DEV
SUBLANES = 8
LANES = 128

ADAM_LR = 0.001
ADAM_B1 = 0.9
ADAM_B2 = 0.999
ADAM_EPS = 1e-08
ADAM_WD = 0.01
ADAM_STEP = 10

VMEM_LIMIT = 56 * 1024 * 1024


def _row_tile(rows):
    for t in (640, 512, 256, 128):
        if rows % t == 0:
            return t
    raise ValueError(rows)


def _big_tile(rows):
    for t in (1664, 1024, 512, 256, 128):
        if rows % t == 0:
            return t
    raise ValueError(rows)


def _rec_tile(rows):
    for t in (416, 256, 128):
        if rows % t == 0:
            return t
    raise ValueError(rows)


def _params(semantics):
    return pltpu.CompilerParams(dimension_semantics=semantics, vmem_limit_bytes=VMEM_LIMIT)


def _mm(a, b):
    return lax.dot_general(a, b, (((1,), (0,)), ((), ())), preferred_element_type=F32)


def _mm_nt(a, b):
    return lax.dot_general(a, b, (((1,), (1,)), ((), ())), preferred_element_type=F32)


def _mm_tn(a, b):
    return lax.dot_general(a, b, (((0,), (0,)), ((), ())), preferred_element_type=F32)


def _rms_fwd(x, g):
    rstd = lax.rsqrt(jnp.mean(x * x, axis=-1, keepdims=True) + EPS)
    xhat = x * rstd
    return xhat * g, xhat, rstd


def _rms_bwd(dy, xhat, rstd, g):
    dyg = dy * g
    c = jnp.mean(dyg * xhat, axis=-1, keepdims=True)
    dx = rstd * (dyg - xhat * c)
    dg = jnp.sum(dy * xhat, axis=0, keepdims=True)
    return dx, dg


def _sigmoid(x):
    return 0.5 * jnp.tanh(0.5 * x) + 0.5


def _log1p(x):
    u = 1.0 + x
    return jnp.where(u == 1.0, x, jnp.log(u) * x / (u - 1.0))


def _one_minus_sq_exp(x, ex):
    return -jnp.tanh(x) * (1.0 + ex * ex)


TINY = 1e-30


def _sqrt_pos(y):
    r = lax.rsqrt(jnp.maximum(y, TINY))
    return y * r, r


def _softplus(x):
    return jnp.maximum(x, 0.0) + _log1p(jnp.exp(-jnp.abs(x)))


GELU_C = 0.7978845608028654
GELU_K = 0.044715


def _gelu(x):
    t = jnp.tanh(GELU_C * (x + GELU_K * x * x * x))
    return 0.5 * x * (1.0 + t), t


def _gelu_grad(x, t):
    return 0.5 * (1.0 + t) + 0.5 * x * (1.0 - t * t) * GELU_C * (1.0 + 3.0 * GELU_K * x * x)


def _full(shape):
    return pl.BlockSpec(shape, lambda *_: (0,) * len(shape))


def _resident(shape):
    return pl.BlockSpec(shape, lambda *_: (0,) * len(shape), pipeline_mode=pl.Buffered(1))


def _exchange_copies(ins, outs, sems, modes):
    send_sems, recv_sems, local_sems = sems
    x, y, c = lax.axis_index("x"), lax.axis_index("y"), lax.axis_index("c")
    me = 4 * x + 2 * y + c

    def block(a, dev):
        return ins[a] if modes[a] == "gather" else ins[a].at[dev]

    local = [pltpu.make_async_copy(block(a, me), outs[a].at[me], local_sems.at[a]) for a in range(len(ins))]
    sends, recvs = [], []
    for a in range(len(ins)):
        for k in range(N_DEV - 1):
            bits = k + 1
            px = jnp.bitwise_xor(x, (bits >> 2) & 1)
            py = jnp.bitwise_xor(y, (bits >> 1) & 1)
            pc = jnp.bitwise_xor(c, bits & 1)
            peer = 4 * px + 2 * py + pc
            common = dict(src_ref=block(a, peer), send_sem=send_sems.at[a, k], recv_sem=recv_sems.at[a, k],
                          device_id=(px, py, pc), device_id_type=pl.DeviceIdType.MESH)
            sends.append(pltpu.make_async_remote_copy(dst_ref=outs[a].at[me], **common))
            recvs.append(pltpu.make_async_remote_copy(dst_ref=outs[a].at[peer], **common))
    return local, sends, recvs


def _exchange_start(ins, outs, sems, modes):
    local, sends, _ = _exchange_copies(ins, outs, sems, modes)
    for cp in local + sends:
        cp.start()


def _exchange_wait(ins, outs, sems, modes):
    local, sends, recvs = _exchange_copies(ins, outs, sems, modes)
    for cp in recvs:
        cp.wait_recv()
    for cp in sends:
        cp.wait_send()
    for cp in local:
        cp.wait()


def _exchange_shapes(arrays, modes):
    return [jax.ShapeDtypeStruct((N_DEV,) + a.shape if mode == "gather" else a.shape, a.dtype)
            for a, mode in zip(arrays, modes)]


def _exchange_sems(na):
    return [pltpu.SemaphoreType.DMA((na, N_DEV - 1)), pltpu.SemaphoreType.DMA((na, N_DEV - 1)),
            pltpu.SemaphoreType.DMA((na,))]


ANY_SPACE = pl.BlockSpec(memory_space=pl.ANY)


def _exchange(arrays, modes, name):
    na = len(arrays)

    def body(*refs):
        ins, outs, sems = refs[:na], refs[na:2 * na], refs[2 * na:]
        _exchange_start(ins, outs, sems, modes)
        _exchange_wait(ins, outs, sems, modes)

    return pl.pallas_call(
        body, name=name, out_shape=_exchange_shapes(arrays, modes),
        in_specs=[ANY_SPACE] * na, out_specs=[ANY_SPACE] * na, scratch_shapes=_exchange_sems(na),
        compiler_params=pltpu.CompilerParams(has_side_effects=True),
    )(*arrays)


def _gather_two_level(arrays, name):
    na = len(arrays)

    def body(*refs):
        ins, outs = refs[:na], refs[na:2 * na]
        send_sems, recv_sems, local_sems = refs[2 * na:]
        x, y, c = lax.axis_index("x"), lax.axis_index("y"), lax.axis_index("c")
        me, sibling = (x, y, c), (x, y, 1 - c)
        chips = [(1 - x, y), (x, 1 - y), (1 - x, 1 - y)]

        def copy(a, k, block, to, src=None):
            slot = outs[a].at[4 * block[0] + 2 * block[1] + block[2]]
            return pltpu.make_async_remote_copy(
                src_ref=slot if src is None else src, dst_ref=slot, send_sem=send_sems.at[a, k],
                recv_sem=recv_sems.at[a, k], device_id=to, device_id_type=pl.DeviceIdType.MESH)

        local = [pltpu.make_async_copy(ins[a], outs[a].at[4 * x + 2 * y + c], local_sems.at[a]) for a in range(na)]
        first = []
        for a in range(na):
            first.append(copy(a, 0, me, sibling, src=ins[a]))
            first += [copy(a, 1 + j, me, (*chip, c), src=ins[a]) for j, chip in enumerate(chips)]
        for cp in local + first:
            cp.start()
        passed = []
        for j, chip in enumerate(chips):
            for a in range(na):
                copy(a, 1 + j, (*chip, c), me).wait_recv()
                passed.append(copy(a, 4 + j, (*chip, c), sibling))
                passed[-1].start()
        for a in range(na):
            copy(a, 0, sibling, me).wait_recv()
            for j, chip in enumerate(chips):
                copy(a, 4 + j, (*chip, 1 - c), me).wait_recv()
        for cp in first + passed:
            cp.wait_send()
        for cp in local:
            cp.wait()

    return pl.pallas_call(
        body, name=name, out_shape=_exchange_shapes(arrays, ["gather"] * na),
        in_specs=[ANY_SPACE] * na, out_specs=[ANY_SPACE] * na, scratch_shapes=_exchange_sems(na),
        compiler_params=pltpu.CompilerParams(has_side_effects=True),
    )(*arrays)


def _hosting_call(body, name, steps, in_specs, out_specs, out_shape, scratch_shapes, args, arrays, modes):
    n_in, n_out, n_scr, na = len(in_specs), len(out_specs), len(scratch_shapes), len(arrays)
    grid = steps if isinstance(steps, tuple) else (steps,)

    def hosting_body(*refs):
        cuts = [0]
        for n in (n_in, na, n_out, na, n_scr, 3):
            cuts.append(cuts[-1] + n)
        ins, x_ins, outs, x_outs, scr, sems = (refs[cuts[p]:cuts[p + 1]] for p in range(6))
        first, last = True, True
        for axis, n in enumerate(grid):
            first = first & (pl.program_id(axis) == 0)
            last = last & (pl.program_id(axis) == n - 1)

        @pl.when(first)
        def _():
            _exchange_start(x_ins, x_outs, sems, modes)

        body(*ins, *outs, *scr)

        @pl.when(last)
        def _():
            _exchange_wait(x_ins, x_outs, sems, modes)

    res = pl.pallas_call(
        hosting_body, name=name, grid=grid,
        in_specs=list(in_specs) + [ANY_SPACE] * na, out_specs=list(out_specs) + [ANY_SPACE] * na,
        out_shape=list(out_shape) + _exchange_shapes(arrays, modes),
        scratch_shapes=list(scratch_shapes) + _exchange_sems(na),
        compiler_params=_params(("arbitrary",) * len(grid)),
    )(*args, *arrays)
    return res[:n_out], res[n_out:]


def _frame_rows(src_hbm, buf, sem, i, steps, tm):
    def first():
        return pltpu.make_async_copy(src_hbm.at[pl.ds(0, tm - BLOCK)], buf.at[0, pl.ds(BLOCK, tm - BLOCK)], sem.at[0])

    def later(t, slot):
        return pltpu.make_async_copy(src_hbm.at[pl.ds(pl.multiple_of(t * tm - BLOCK, SUBLANES), tm)], buf.at[slot], sem.at[slot])

    slot = i % 2

    @pl.when(i == 0)
    def _():
        first().start()

    @pl.when(i + 1 < steps)
    def _():
        later(i + 1, 1 - slot).start()

    @pl.when(i == 0)
    def _():
        first().wait()

    @pl.when(i > 0)
    def _():
        later(i, slot).wait()

    return slot


def _frame_scratch(tm):
    return [pltpu.VMEM((2, tm, D_MODEL), F32), pltpu.SemaphoreType.DMA((2,))]


def _h0_tile(head_ref, x_hbm, buf, sem, i, steps, tm):
    slot = _frame_rows(x_hbm, buf, sem, i, steps, tm)

    @pl.when(i == 0)
    def _():
        buf[0, 0:BLOCK, :] = head_ref[...]

    return buf[slot]


def _in_proj_fwd(head, x, g1, w_in, carried, modes):
    rows = BLOCK + x.shape[0]
    tm = _row_tile(rows)
    steps = rows // tm

    def body(head_ref, g_ref, w_ref, x_hbm, qkv_ref, zrec_ref, u_ref, buf, sem):
        h = _h0_tile(head_ref, x_hbm, buf, sem, pl.program_id(0), steps, tm)
        u, _, _ = _rms_fwd(h, g_ref[...])
        u = u.astype(BF16)
        u_ref[...] = u
        z = _mm(u, w_ref[...])
        qkv_ref[...] = z[:, :QKV_WIDTH].astype(BF16)
        zrec_ref[...] = z[:, QKV_WIDTH:]

    wide = pl.BlockSpec((tm, D_MODEL), lambda i: (i, 0))
    return _hosting_call(
        body, "in_proj_fwd", steps,
        [_full((BLOCK, D_MODEL)), _full((1, D_MODEL)), _resident((D_MODEL, IN_WIDTH)), ANY_SPACE],
        [pl.BlockSpec((tm, QKV_WIDTH), lambda i: (i, 0)), pl.BlockSpec((tm, 2 * LRU_WIDTH), lambda i: (i, 0)), wide],
        [jax.ShapeDtypeStruct((rows, QKV_WIDTH), BF16), jax.ShapeDtypeStruct((rows, 2 * LRU_WIDTH), F32),
         jax.ShapeDtypeStruct((rows, D_MODEL), BF16)],
        _frame_scratch(tm), (head, g1, w_in, x), carried, modes)


N_BIAS = 3


def _attn_bias():
    key = np.arange(2 * BLOCK)[:, None]
    r = np.arange(GQA_GROUP * BLOCK)[None, :] % BLOCK
    band = (key > r) & (key <= r + BLOCK)
    out = [np.where(band & ((n - 1) * BLOCK + key >= PAD_ROWS), 0.0, NEG) for n in range(N_BIAS)]
    return jnp.asarray(np.stack(out), F32)


def _attn_probs(k2, q4, bias, sink_row):
    s = _mm_nt(k2, q4) * (HEAD_DIM ** -0.5) + bias
    m = jnp.maximum(jnp.max(s, axis=0, keepdims=True), sink_row)
    p = jnp.exp(s - m)
    es = jnp.exp(sink_row - m)
    inv = 1.0 / (jnp.sum(p, axis=0, keepdims=True) + es)
    return p * inv, es * inv


def _heads(ref, rows, first, count):
    return jnp.concatenate([ref[rows, (first + g) * HEAD_DIM:(first + g + 1) * HEAD_DIM] for g in range(count)], axis=0)


def _keys_of_block(prev_ref, cur_ref, b, kv):
    sl = slice(kv * HEAD_DIM, (kv + 1) * HEAD_DIM)
    before = prev_ref[:, sl] if b == 0 else cur_ref[(b - 1) * BLOCK:b * BLOCK, sl]
    return jnp.concatenate([before, cur_ref[b * BLOCK:(b + 1) * BLOCK, sl]], axis=0)


def _bias_of_block(bias_ref, block):
    return bias_ref[jnp.minimum(block, N_BIAS - 1)]


def _sink_row(sink_ref, kv):
    g = lax.broadcasted_iota(jnp.int32, (1, GQA_GROUP * BLOCK), 1) // BLOCK
    row = jnp.full((1, GQA_GROUP * BLOCK), sink_ref[0, kv * GQA_GROUP], F32)
    for i in range(1, GQA_GROUP):
        row = jnp.where(g == i, sink_ref[0, kv * GQA_GROUP + i], row)
    return row


def _from_head_major(pieces):
    return jnp.concatenate(pieces, axis=0).T


def _attn_specs(tm, tile_of):
    nbt = tm // BLOCK
    k_col, v_col = ATTN_WIDTH // KV_WIDTH, ATTN_WIDTH // KV_WIDTH + 1
    before = lambda i: jnp.maximum(tile_of(i) * nbt - 1, 0)
    return [pl.BlockSpec((tm, ATTN_WIDTH), lambda i: (tile_of(i), 0)),
            pl.BlockSpec((BLOCK, KV_WIDTH), lambda i: (before(i), k_col)),
            pl.BlockSpec((tm, KV_WIDTH), lambda i: (tile_of(i), k_col)),
            pl.BlockSpec((BLOCK, KV_WIDTH), lambda i: (before(i), v_col)),
            pl.BlockSpec((tm, KV_WIDTH), lambda i: (tile_of(i), v_col))]


def _attn_fwd(qkv, sinks, bias, carried, modes):
    rows = qkv.shape[0]
    tm = _row_tile(rows)
    nbt = tm // BLOCK

    def body(sink_ref, bias_ref, q_ref, kp_ref, kc_ref, vp_ref, vc_ref, o_ref):
        i = pl.program_id(0)
        for b in range(nbt):
            blk = slice(b * BLOCK, (b + 1) * BLOCK)
            bias_t = _bias_of_block(bias_ref, i * nbt + b)
            pieces = []
            for kv in range(KV_HEADS):
                k2 = _keys_of_block(kp_ref, kc_ref, b, kv)
                v2 = _keys_of_block(vp_ref, vc_ref, b, kv)
                q4 = _heads(q_ref, blk, kv * GQA_GROUP, GQA_GROUP)
                pn, _ = _attn_probs(k2, q4, bias_t, _sink_row(sink_ref, kv))
                ot = _mm_tn(v2, pn.astype(BF16))
                pieces += [ot[:, g * BLOCK:(g + 1) * BLOCK] for g in range(GQA_GROUP)]
            o_ref[blk, :] = _from_head_major(pieces).astype(BF16)

    return _hosting_call(
        body, "attn_fwd", rows // tm,
        [pl.BlockSpec(memory_space=pltpu.SMEM), _resident((N_BIAS, 2 * BLOCK, GQA_GROUP * BLOCK))]
        + _attn_specs(tm, lambda i: i),
        [pl.BlockSpec((tm, ATTN_WIDTH), lambda i: (i, 0))],
        [jax.ShapeDtypeStruct((rows, ATTN_WIDTH), BF16)],
        [], (sinks, bias, qkv, qkv, qkv, qkv, qkv), carried, modes)


def _conv_taps(xbuf, tm):
    return [xbuf[pl.ds(SUBLANES - (CONV_WIDTH - 1 - j), tm), :] for j in range(CONV_WIDTH)]


def _lru_halves(xc):
    return [xc[:, h * LRU_HALF:(h + 1) * LRU_HALF].astype(BF16) for h in range(2)]


def _lru_gates(xc, wa_ref, ba_ref, wx_ref, bx_ref, lam_ref):
    halves = _lru_halves(xc)
    gate_r = jnp.concatenate([_mm(halves[h], wa_ref[h]) for h in range(2)], axis=1) + ba_ref[...]
    gate_i = jnp.concatenate([_mm(halves[h], wx_ref[h]) for h in range(2)], axis=1) + bx_ref[...]
    r = _sigmoid(gate_r)
    ig = _sigmoid(gate_i)
    log_a = (-LRU_C) * r * _softplus(-lam_ref[...])
    a = jnp.exp(log_a)
    mult, _ = _sqrt_pos(_one_minus_sq_exp(log_a, a))
    return r, ig, a, mult


KEPT_XC, KEPT_A, KEPT_MULT, KEPT_R, KEPT_I, N_KEPT = 0, 1, 2, 3, 4, 5


def _scan_tile(a_ref, u_ref, out_ref, carry, tm, reverse):
    row = lax.broadcasted_iota(jnp.int32, (SUBLANES, LRU_WIDTH), 0)
    groups = tm // SUBLANES

    def step(j, prev):
        jj = groups - 1 - j if reverse else j
        o = pl.multiple_of(jj * SUBLANES, SUBLANES)
        a = a_ref[pl.ds(o, SUBLANES), :]
        u = u_ref[pl.ds(o, SUBLANES), :]
        for s in (1, 2, 4):
            shift = SUBLANES - s if reverse else s
            keep = (row < SUBLANES - s) if reverse else (row >= s)
            u = jnp.where(keep, a * pltpu.roll(u, shift, 0) + u, u)
            a = jnp.where(keep, a * pltpu.roll(a, shift, 0), a)
        out = a * prev + u
        out_ref[pl.ds(o, SUBLANES), :] = out
        return out[0:1, :] if reverse else out[SUBLANES - 1:SUBLANES, :]

    return lax.fori_loop(0, groups, step, carry)


def _rec_fwd(zrec, conv_w, conv_b, wa_bd, b_a, wx_bd, b_x, lam, carried, modes):
    rows = zrec.shape[0]
    tm = _row_tile(rows)

    def body(xr_ref, yr_ref, cw_ref, cb_ref, wa_ref, ba_ref, wx_ref, bx_ref, lam_ref, rec_ref, h_ref, kept_ref,
             xbuf, a_s, u_s, carry):
        i = pl.program_id(0)

        @pl.when(i == 0)
        def _():
            xbuf[0:SUBLANES, :] = jnp.zeros((SUBLANES, LRU_WIDTH), F32)
            carry[...] = jnp.zeros_like(carry)

        @pl.when(i > 0)
        def _():
            xbuf[0:SUBLANES, :] = xbuf[tm:tm + SUBLANES, :]

        xbuf[SUBLANES:SUBLANES + tm, :] = xr_ref[...]
        taps = _conv_taps(xbuf, tm)
        xc = cb_ref[...] + sum(cw_ref[j:j + 1, :] * taps[j] for j in range(CONV_WIDTH))
        r, ig, a, mult = _lru_gates(xc, wa_ref, ba_ref, wx_ref, bx_ref, lam_ref)
        for k, val in ((KEPT_XC, xc), (KEPT_A, a), (KEPT_MULT, mult), (KEPT_R, r), (KEPT_I, ig)):
            kept_ref[:, k * LRU_WIDTH:(k + 1) * LRU_WIDTH] = val
        grow = i * tm + lax.broadcasted_iota(jnp.int32, (tm, LRU_WIDTH), 0)
        a_s[...] = a
        u_s[...] = jnp.where(grow >= PAD_ROWS, mult * (ig * xc), 0.0)
        carry[0:1, :] = _scan_tile(a_s, u_s, h_ref, carry[0:1, :], tm, reverse=False)
        gel, _ = _gelu(yr_ref[...])
        rec_ref[...] = (gel * h_ref[...]).astype(BF16)

    vec = _full((1, LRU_WIDTH))
    bd = _full((2, LRU_HALF, LRU_HALF))
    return _hosting_call(
        body, "rec_fwd", rows // tm,
        [pl.BlockSpec((tm, LRU_WIDTH), lambda i: (i, 0)), pl.BlockSpec((tm, LRU_WIDTH), lambda i: (i, 1)),
         _full((CONV_WIDTH, LRU_WIDTH)), vec, bd, vec, bd, vec, vec],
        [pl.BlockSpec((tm, LRU_WIDTH), lambda i: (i, 0))] * 2 + [pl.BlockSpec((tm, N_KEPT * LRU_WIDTH), lambda i: (i, 0))],
        [jax.ShapeDtypeStruct((rows, LRU_WIDTH), BF16), jax.ShapeDtypeStruct((rows, LRU_WIDTH), F32),
         jax.ShapeDtypeStruct((rows, N_KEPT * LRU_WIDTH), F32)],
        [pltpu.VMEM((tm + SUBLANES, LRU_WIDTH), F32), pltpu.VMEM((tm, LRU_WIDTH), F32),
         pltpu.VMEM((tm, LRU_WIDTH), F32), pltpu.VMEM((SUBLANES, LRU_WIDTH), F32)],
        (zrec, zrec, conv_w, conv_b, wa_bd, b_a, wx_bd, b_x, lam), carried, modes)


def _out_proj_fwd(attn, rec, w_out, head, x, g2, carried, modes):
    rows = attn.shape[0]
    tm = _row_tile(rows)
    steps = rows // tm

    def body(attn_ref, rec_ref, w_ref, head_ref, g_ref, x_hbm, mix_ref, h1_ref, buf, sem):
        h0 = _h0_tile(head_ref, x_hbm, buf, sem, pl.program_id(0), steps, tm)
        mix = _mm(attn_ref[...], w_ref[0:ATTN_WIDTH, :]) + _mm(rec_ref[...], w_ref[ATTN_WIDTH:, :])
        y, _, _ = _rms_fwd(mix, g_ref[...])
        mix_ref[...] = mix
        h1_ref[...] = h0 + y

    half = pl.BlockSpec((tm, ATTN_WIDTH), lambda i: (i, 0))
    wide = pl.BlockSpec((tm, D_MODEL), lambda i: (i, 0))
    return _hosting_call(
        body, "out_proj_fwd", steps,
        [half, half, _resident((D_MODEL, D_MODEL)), _full((BLOCK, D_MODEL)), _full((1, D_MODEL)), ANY_SPACE],
        [wide, wide],
        [jax.ShapeDtypeStruct((rows, D_MODEL), F32)] * 2,
        _frame_scratch(tm), (attn, rec, w_out, head, g2, x), carried, modes)


FF_COLS = 1024
FF_HALF = FF_CHUNK // 2


def _hidden_at(d, half):
    return half * (D_FF // 2) + d * FF_HALF


def _ffn_up(h1, g3, w1_halves, carried, modes):
    rows = h1.shape[0]
    tm = _row_tile(rows)

    def body(h_ref, g_ref, wa_ref, wb_ref, act_ref, u_ref):
        u, _, _ = _rms_fwd(h_ref[...], g_ref[...])
        u = u.astype(BF16)
        u_ref[...] = u
        for half, w_ref in enumerate((wa_ref, wb_ref)):
            for d in range(N_DEV):
                c = _hidden_at(d, half)
                a1 = jnp.maximum(_mm(u, w_ref[d]), 0.0)
                act_ref[:, c:c + FF_HALF] = (a1 * a1).astype(BF16)

    wide = pl.BlockSpec((tm, D_MODEL), lambda i: (i, 0))
    return _hosting_call(
        body, "ffn_up", rows // tm,
        [wide, _full((1, D_MODEL))] + [_resident((N_DEV, D_MODEL, FF_HALF))] * 2,
        [pl.BlockSpec((tm, D_FF), lambda i: (i, 0)), wide],
        [jax.ShapeDtypeStruct((rows, D_FF), BF16), jax.ShapeDtypeStruct((rows, D_MODEL), BF16)],
        [], (h1, g3, *w1_halves), carried, modes)


def _ffn_down_loss(act, w2_halves, h1, target, g4, carried, modes):
    rows = h1.shape[0]
    tm = _row_tile(rows)
    steps = rows // tm
    kh = D_FF // 2

    def body(act_ref, wa_ref, wb_ref, h_ref, g_ref, t_hbm, dy_ref, df_ref, dg_ref, loss_ref, buf, sem):
        i = pl.program_id(0)
        slot = _frame_rows(t_hbm, buf, sem, i, steps, tm)

        @pl.when(i == 0)
        def _():
            dg_ref[...] = jnp.zeros_like(dg_ref)
            loss_ref[...] = jnp.zeros_like(loss_ref)
            buf[0, 0:BLOCK, :] = jnp.zeros((BLOCK, D_MODEL), F32)

        g = g_ref[...]
        f = _mm(act_ref[:, :kh], wa_ref[...]) + _mm(act_ref[:, kh:], wb_ref[...])
        y, fhat, rstd = _rms_fwd(f, g)
        grow = i * tm + lax.broadcasted_iota(jnp.int32, (tm, D_MODEL), 0)
        err = jnp.where(grow >= BLOCK, h_ref[...] + y - buf[slot], 0.0)
        loss_ref[...] += (0.5 / D_MODEL) * jnp.sum(err * err)
        dy = err * (1.0 / D_MODEL)
        df, dg = _rms_bwd(dy, fhat, rstd, g)
        dy_ref[...] = dy
        df_ref[...] = df.astype(BF16)
        dg_ref[...] += dg

    wide = pl.BlockSpec((tm, D_MODEL), lambda i: (i, 0))
    return _hosting_call(
        body, "ffn_down_loss", steps,
        [pl.BlockSpec((tm, D_FF), lambda i: (i, 0)), _resident((kh, D_MODEL)), _resident((kh, D_MODEL)), wide,
         _full((1, D_MODEL)), ANY_SPACE],
        [wide, wide, _full((1, D_MODEL)), _full((SUBLANES, LANES))],
        [jax.ShapeDtypeStruct((rows, D_MODEL), F32), jax.ShapeDtypeStruct((rows, D_MODEL), BF16),
         jax.ShapeDtypeStruct((1, D_MODEL), F32), jax.ShapeDtypeStruct((SUBLANES, LANES), F32)],
        _frame_scratch(tm), (act, *w2_halves, h1, g4, target), carried, modes)


def _ffn_bwd_act(df, w2t_halves, act, carried, modes):
    rows = df.shape[0]
    tm = _row_tile(rows)

    def body(df_ref, wa_ref, wb_ref, act_ref, da_ref):
        df_t = df_ref[...]
        for half, w_ref in enumerate((wa_ref, wb_ref)):
            for d in range(N_DEV):
                cols = slice(_hidden_at(d, half), _hidden_at(d, half) + FF_HALF)
                dact = _mm(df_t, w_ref[d])
                relu_a1, _ = _sqrt_pos(act_ref[:, cols].astype(F32))
                da_ref[:, cols] = (dact * (2.0 * relu_a1)).astype(BF16)

    hidden = pl.BlockSpec((tm, D_FF), lambda i: (i, 0))
    return _hosting_call(
        body, "ffn_bwd_act", rows // tm,
        [pl.BlockSpec((tm, D_MODEL), lambda i: (i, 0))] + [_resident((N_DEV, D_MODEL, FF_HALF))] * 2 + [hidden],
        [hidden],
        [jax.ShapeDtypeStruct((rows, D_FF), BF16)],
        [], (df, *w2t_halves, act), carried, modes)


def _ffn_bwd_x(da, w1t_halves, h1, dy, g3, carried, modes):
    rows = h1.shape[0]
    tm = _row_tile(rows)
    kh = D_FF // 2

    def body(da_ref, wa_ref, wb_ref, h_ref, dy_ref, g_ref, dh_ref, dg_ref):
        @pl.when(pl.program_id(0) == 0)
        def _():
            dg_ref[...] = jnp.zeros_like(dg_ref)

        g = g_ref[...]
        _, xhat, rstd = _rms_fwd(h_ref[...], g)
        du = _mm(da_ref[:, :kh], wa_ref[...]) + _mm(da_ref[:, kh:], wb_ref[...])
        dx, dg = _rms_bwd(du, xhat, rstd, g)
        dh_ref[...] = dy_ref[...] + dx
        dg_ref[...] += dg

    wide = pl.BlockSpec((tm, D_MODEL), lambda i: (i, 0))
    return _hosting_call(
        body, "ffn_bwd_x", rows // tm,
        [pl.BlockSpec((tm, D_FF), lambda i: (i, 0)), _resident((kh, D_MODEL)), _resident((kh, D_MODEL)), wide, wide,
         _full((1, D_MODEL))],
        [wide, _full((1, D_MODEL))],
        [jax.ShapeDtypeStruct((rows, D_MODEL), F32), jax.ShapeDtypeStruct((1, D_MODEL), F32)],
        [], (da, *w1t_halves, h1, dy, g3), carried, modes)


def _ffn_bwd_weights(u2, da, act, df, carried, modes):
    rows = u2.shape[0]
    tb = _big_tile(rows)
    steps = rows // tb
    per = FF_COLS // FF_HALF

    def body(u_ref, da_ref, act_ref, df_ref, dw1_ref, dw2_ref, acc1, acc2):
        i = pl.program_id(1)

        @pl.when(i == 0)
        def _():
            acc1[...] = jnp.zeros_like(acc1)
            acc2[...] = jnp.zeros_like(acc2)

        acc1[...] += _mm_tn(u_ref[...], da_ref[...])
        acc2[...] += _mm_tn(act_ref[...], df_ref[...])

        @pl.when(i == steps - 1)
        def _():
            for p in range(per):
                c = p * FF_HALF
                dw1_ref[p] = acc1[:, c:c + FF_HALF].astype(BF16)
                dw2_ref[p] = acc2[c:c + FF_HALF, :].astype(BF16)

    wide = pl.BlockSpec((tb, D_MODEL), lambda j, i: (i, 0))
    chunk = pl.BlockSpec((tb, FF_COLS), lambda j, i: (i, j))
    return _hosting_call(
        body, "ffn_bwd_weights", (D_FF // FF_COLS, steps),
        [wide, chunk, chunk, wide],
        [pl.BlockSpec((None, per, D_MODEL, FF_HALF), lambda j, i: (j // 2, j % 2, 0, 0)),
         pl.BlockSpec((per, FF_HALF, D_MODEL), lambda j, i: (j % 2, j // 2, 0))],
        [jax.ShapeDtypeStruct((2, N_DEV, D_MODEL, FF_HALF), BF16), jax.ShapeDtypeStruct((N_DEV, FF_CHUNK, D_MODEL), BF16)],
        [pltpu.VMEM((D_MODEL, FF_COLS), F32), pltpu.VMEM((FF_COLS, D_MODEL), F32)],
        (u2, da, act, df), carried, modes)


def _out_proj_bwd(dh1, mix, g2, w_out_t, attn, rec, carried, modes):
    rows = dh1.shape[0]
    tm = _row_tile(rows)
    steps = rows // tm

    def body(dh_ref, mix_ref, g_ref, w_ref, attn_ref, rec_ref, dattn_ref, drec_ref, dw_ref, dg_ref, acc):
        i = pl.program_id(0)

        @pl.when(i == 0)
        def _():
            acc[...] = jnp.zeros_like(acc)
            dg_ref[...] = jnp.zeros_like(dg_ref)

        g = g_ref[...]
        _, xhat, rstd = _rms_fwd(mix_ref[...], g)
        dmix, dg = _rms_bwd(dh_ref[...], xhat, rstd, g)
        dmix = dmix.astype(BF16)
        dg_ref[...] += dg
        din = _mm(dmix, w_ref[...])
        dattn_ref[...] = din[:, :ATTN_WIDTH].astype(BF16)
        drec_ref[...] = din[:, ATTN_WIDTH:]
        acc[0:ATTN_WIDTH, :] += _mm_tn(attn_ref[...], dmix)
        acc[ATTN_WIDTH:, :] += _mm_tn(rec_ref[...], dmix)

        @pl.when(i == steps - 1)
        def _():
            dw_ref[...] = acc[...].astype(BF16)

    half = pl.BlockSpec((tm, ATTN_WIDTH), lambda i: (i, 0))
    wide = pl.BlockSpec((tm, D_MODEL), lambda i: (i, 0))
    return _hosting_call(
        body, "out_proj_bwd", steps,
        [wide, wide, _full((1, D_MODEL)), _resident((D_MODEL, D_MODEL)), half, half],
        [half, half, _full((D_MODEL, D_MODEL)), _full((1, D_MODEL))],
        [jax.ShapeDtypeStruct((rows, ATTN_WIDTH), BF16), jax.ShapeDtypeStruct((rows, LRU_WIDTH), F32),
         jax.ShapeDtypeStruct((D_MODEL, D_MODEL), BF16), jax.ShapeDtypeStruct((1, D_MODEL), F32)],
        [pltpu.VMEM((D_MODEL, D_MODEL), F32)],
        (dh1, mix, g2, w_out_t, attn, rec), carried, modes)


def _attn_bwd(qkv, dattn, sinks, bias, carried, modes):
    rows = qkv.shape[0]
    tm = _row_tile(rows)
    nbt, nt = tm // BLOCK, rows // tm

    def body(sink_ref, bias_ref, do_ref, q_ref, kp_ref, kc_ref, vp_ref, vc_ref, dq_ref, dkv_ref, dsink_ref, dk_c, dv_c):
        i = pl.program_id(0)

        @pl.when(i == 0)
        def _():
            dk_c[...] = jnp.zeros_like(dk_c)
            dv_c[...] = jnp.zeros_like(dv_c)
            dsink_ref[...] = jnp.zeros_like(dsink_ref)

        @pl.when(i < nt)
        def _():
            dk_late, dv_late = dk_c[...], dv_c[...]
            dsink_rows = [jnp.zeros((1, LANES), F32)] * ATTN_HEADS
            for b in range(nbt):
                blk = slice(b * BLOCK, (b + 1) * BLOCK)
                bias_t = _bias_of_block(bias_ref, i * nbt + b)
                dq_parts, dk_parts, dv_parts = [], [], []
                for kv in range(KV_HEADS):
                    k2 = _keys_of_block(kp_ref, kc_ref, b, kv)
                    v2 = _keys_of_block(vp_ref, vc_ref, b, kv)
                    q4 = _heads(q_ref, blk, kv * GQA_GROUP, GQA_GROUP)
                    do4 = _heads(do_ref, blk, kv * GQA_GROUP, GQA_GROUP)
                    pn, psink = _attn_probs(k2, q4, bias_t, _sink_row(sink_ref, kv))
                    dpn = _mm_nt(v2, do4)
                    delta = jnp.sum(pn * dpn, axis=0, keepdims=True)
                    ds = ((pn * (dpn - delta)) * (HEAD_DIM ** -0.5)).astype(BF16)
                    dqt = _mm_tn(k2, ds)
                    dq_parts += [dqt[:, g * BLOCK:(g + 1) * BLOCK] for g in range(GQA_GROUP)]
                    dk_parts.append(_mm(ds, q4))
                    dv_parts.append(_mm(pn.astype(BF16), do4))
                    sd = psink * delta
                    for g in range(GQA_GROUP):
                        h = kv * GQA_GROUP + g
                        dsink_rows[h] = dsink_rows[h] - jnp.sum(sd[:, g * BLOCK:(g + 1) * BLOCK])
                dq_ref[blk, :] = _from_head_major(dq_parts).astype(BF16)
                dk2 = jnp.concatenate(dk_parts, axis=1)
                dv2 = jnp.concatenate(dv_parts, axis=1)
                dkv_ref[blk, 0:KV_WIDTH] = (dk_late + dk2[0:BLOCK]).astype(BF16)
                dkv_ref[blk, KV_WIDTH:] = (dv_late + dv2[0:BLOCK]).astype(BF16)
                dk_late, dv_late = dk2[BLOCK:], dv2[BLOCK:]
            dk_c[...] = dk_late
            dv_c[...] = dv_late
            dsink_ref[...] += jnp.concatenate(dsink_rows, axis=0)

        @pl.when(i == nt)
        def _():
            dkv_ref[...] = jnp.zeros_like(dkv_ref)
            dkv_ref[0:BLOCK, 0:KV_WIDTH] = dk_c[...].astype(BF16)
            dkv_ref[0:BLOCK, KV_WIDTH:] = dv_c[...].astype(BF16)

    tile_of = lambda i: jnp.minimum(i, nt - 1)
    tile = pl.BlockSpec((tm, ATTN_WIDTH), lambda i: (tile_of(i), 0))
    return _hosting_call(
        body, "attn_bwd", nt + 1,
        [pl.BlockSpec(memory_space=pltpu.SMEM), _resident((N_BIAS, 2 * BLOCK, GQA_GROUP * BLOCK)), tile]
        + _attn_specs(tm, tile_of),
        [tile, pl.BlockSpec((tm, 2 * KV_WIDTH), lambda i: (i, 0)), _full((ATTN_HEADS, LANES))],
        [jax.ShapeDtypeStruct((rows, ATTN_WIDTH), BF16), jax.ShapeDtypeStruct((rows + tm, 2 * KV_WIDTH), BF16),
         jax.ShapeDtypeStruct((ATTN_HEADS, LANES), F32)],
        [pltpu.VMEM((BLOCK, KV_WIDTH), F32), pltpu.VMEM((BLOCK, KV_WIDTH), F32)],
        (sinks, bias, dattn, qkv, qkv, qkv, qkv, qkv), carried, modes)


ROW_CONV_B, ROW_B_A, ROW_B_X, ROW_LAMBDA = 4, 5, 6, 7


def _rec_bwd(drec, zrec, h, kept, conv_w, wa_bd, wx_bd, lam, carried, modes):
    rows = zrec.shape[0]
    tm = _rec_tile(rows)
    nt = rows // tm
    per = tm // SUBLANES

    def body(drec_ref, xr_ref, yr_ref, h_ref, xc_ref, a_ref, mult_ref, r_ref, ig_ref, hhalo_ref, cw_ref, wa_ref, wx_ref,
             lam_ref, drz_ref, small_ref, dwa_ref, dwx_ref, hbuf, abuf, u_s, g_s, dbuf, carry):
        s = pl.program_id(0)
        i = nt - 1 - s

        @pl.when(s == 0)
        def _():
            small_ref[...] = jnp.zeros_like(small_ref)
            dwa_ref[...] = jnp.zeros_like(dwa_ref)
            dwx_ref[...] = jnp.zeros_like(dwx_ref)
            carry[...] = jnp.zeros_like(carry)
            abuf[tm:tm + SUBLANES, :] = jnp.zeros((SUBLANES, LRU_WIDTH), F32)
            dbuf[tm:tm + SUBLANES, :] = jnp.zeros((SUBLANES, LRU_WIDTH), F32)

        hbuf[0:SUBLANES, :] = jnp.where(i == 0, 0.0, hhalo_ref[...])
        hbuf[SUBLANES:SUBLANES + tm, :] = h_ref[...]

        yr = yr_ref[...]
        gel, t = _gelu(yr)
        drz_ref[:, LRU_WIDTH:] = (drec_ref[...] * h_ref[...] * _gelu_grad(yr, t)).astype(BF16)
        u_s[...] = drec_ref[...] * gel

        abuf[0:tm, :] = a_ref[...]
        a_next = abuf[pl.ds(1, tm), :]
        abuf[0:tm, :] = a_next
        carry[0:1, :] = _scan_tile(abuf, u_s, g_s, carry[0:1, :], tm, reverse=True)
        abuf[tm:tm + 1, :] = a_ref[0:1, :]

        grow = i * tm + lax.broadcasted_iota(jnp.int32, (tm, LRU_WIDTH), 0)
        du = jnp.where(grow >= PAD_ROWS, g_s[...], 0.0)
        dbuf[0:tm, :] = du * (mult_ref[...] * ig_ref[...])
        ig = ig_ref[...]
        dgi = (du * (mult_ref[...] * xc_ref[...])) * (ig * (1.0 - ig))
        dgi_b = [dgi[:, hh * LRU_HALF:(hh + 1) * LRU_HALF].astype(BF16) for hh in range(2)]
        sum_dgi = jnp.sum(dgi, axis=0, keepdims=True)
        a = a_ref[...]
        dlog_a = (g_s[...] * hbuf[pl.ds(SUBLANES - 1, tm), :]) * a \
            - (du * (ig * xc_ref[...])) * (a * a * pl.reciprocal(mult_ref[...], approx=True))
        r = r_ref[...]
        dlam = jnp.sum(dlog_a * r, axis=0, keepdims=True) * (LRU_C * _sigmoid(-lam_ref[...]))
        dgr = (dlog_a * ((-LRU_C) * _softplus(-lam_ref[...]))) * (r * (1.0 - r))
        dgr_b = [dgr[:, hh * LRU_HALF:(hh + 1) * LRU_HALF].astype(BF16) for hh in range(2)]
        sum_dgr = jnp.sum(dgr, axis=0, keepdims=True)

        halves = _lru_halves(xc_ref[...])
        for hh in range(2):
            dwa_ref[hh] += _mm_tn(halves[hh], dgr_b[hh])
            dwx_ref[hh] += _mm_tn(halves[hh], dgi_b[hh])
        dxc = dbuf[0:tm, :] + jnp.concatenate(
            [_mm_nt(dgr_b[hh], wa_ref[hh]) + _mm_nt(dgi_b[hh], wx_ref[hh]) for hh in range(2)], axis=1)

        dbuf[0:tm, :] = dxc
        sum_dxc = jnp.sum(dxc, axis=0, keepdims=True)
        ahead = [dbuf[pl.ds(CONV_WIDTH - 1 - j, tm), :] for j in range(CONV_WIDTH)]
        drz_ref[:, 0:LRU_WIDTH] = sum(cw_ref[j:j + 1, :] * ahead[j] for j in range(CONV_WIDTH)).astype(BF16)
        upd = [jnp.sum(xr_ref[...] * ahead[j], axis=0, keepdims=True) for j in range(CONV_WIDTH)]
        dbuf[tm:tm + SUBLANES, :] = dbuf[0:SUBLANES, :]
        small_ref[...] += jnp.concatenate(upd + [sum_dxc, sum_dgr, sum_dgi, dlam], axis=0)

    rev = lambda s: nt - 1 - s
    halo = lambda s: jnp.maximum(rev(s) * per - 1, 0)
    cols = lambda k: pl.BlockSpec((tm, LRU_WIDTH), lambda s: (rev(s), k))
    halo0 = pl.BlockSpec((SUBLANES, LRU_WIDTH), lambda s: (halo(s), 0))
    bd = _full((2, LRU_HALF, LRU_HALF))
    big = pltpu.VMEM((tm + SUBLANES, LRU_WIDTH), F32)
    tile = pltpu.VMEM((tm, LRU_WIDTH), F32)
    kept_cols = [cols(k) for k in (KEPT_XC, KEPT_A, KEPT_MULT, KEPT_R, KEPT_I)]
    return _hosting_call(
        body, "rec_bwd", nt,
        [cols(0), cols(0), cols(1), cols(0)] + kept_cols
        + [halo0, _full((CONV_WIDTH, LRU_WIDTH)), bd, bd, _full((1, LRU_WIDTH))],
        [pl.BlockSpec((tm, 2 * LRU_WIDTH), lambda s: (rev(s), 0)), _full((SUBLANES, LRU_WIDTH)), bd, bd],
        [jax.ShapeDtypeStruct((rows, 2 * LRU_WIDTH), BF16), jax.ShapeDtypeStruct((SUBLANES, LRU_WIDTH), F32),
         jax.ShapeDtypeStruct((2, LRU_HALF, LRU_HALF), F32), jax.ShapeDtypeStruct((2, LRU_HALF, LRU_HALF), F32)],
        [big, big, tile, tile, big, pltpu.VMEM((SUBLANES, LRU_WIDTH), F32)],
        (drec, zrec, zrec, h) + (kept,) * N_KEPT + (h, conv_w, wa_bd, wx_bd, lam), carried, modes)


DZ_CUTS = (0, ATTN_WIDTH, QKV_WIDTH, IN_WIDTH)


def _dz_specs(tm):
    return [pl.BlockSpec((tm, DZ_CUTS[p + 1] - DZ_CUTS[p]), lambda i: (i, 0)) for p in range(3)]


def _in_proj_bwd_x(head, x, g1, dh1, dq, dkv, drz, w_in_t, carried, modes):
    rows = dh1.shape[0]
    tm = _row_tile(rows)
    steps = rows // tm

    def body(head_ref, g_ref, dh1_ref, dq_ref, dkv_ref, drz_ref, w_ref, x_hbm, dh0_ref, dg_ref, buf, sem):
        i = pl.program_id(0)
        h0 = _h0_tile(head_ref, x_hbm, buf, sem, i, steps, tm)

        @pl.when(i == 0)
        def _():
            dg_ref[...] = jnp.zeros_like(dg_ref)

        g = g_ref[...]
        _, xhat, rstd = _rms_fwd(h0, g)
        parts = (dq_ref[...], dkv_ref[...], drz_ref[...])
        du = sum(_mm(parts[p], w_ref[DZ_CUTS[p]:DZ_CUTS[p + 1], :]) for p in range(3))
        dx, dg = _rms_bwd(du, xhat, rstd, g)
        dh0_ref[...] = dh1_ref[...] + dx
        dg_ref[...] += dg

    wide = pl.BlockSpec((tm, D_MODEL), lambda i: (i, 0))
    return _hosting_call(
        body, "in_proj_bwd_x", steps,
        [_full((BLOCK, D_MODEL)), _full((1, D_MODEL)), wide] + _dz_specs(tm) + [_resident((IN_WIDTH, D_MODEL)), ANY_SPACE],
        [wide, _full((1, D_MODEL))],
        [jax.ShapeDtypeStruct((rows, D_MODEL), F32), jax.ShapeDtypeStruct((1, D_MODEL), F32)],
        _frame_scratch(tm), (head, g1, dh1, dq, dkv, drz, w_in_t, x), carried, modes)


def _in_proj_bwd_w(u1, dq, dkv, drz, carried, modes):
    rows = u1.shape[0]
    tb = _big_tile(rows)
    steps = rows // tb

    def body(u_ref, dq_ref, dkv_ref, drz_ref, dw_ref, acc):
        i = pl.program_id(0)

        @pl.when(i == 0)
        def _():
            acc[...] = jnp.zeros_like(acc)

        u = u_ref[...]
        for p, ref in enumerate((dq_ref, dkv_ref, drz_ref)):
            acc[:, DZ_CUTS[p]:DZ_CUTS[p + 1]] += _mm_tn(u, ref[...])

        @pl.when(i == steps - 1)
        def _():
            dw_ref[...] = acc[...].astype(BF16)

    return _hosting_call(
        body, "in_proj_bwd_w", steps,
        [pl.BlockSpec((tb, D_MODEL), lambda i: (i, 0))] + _dz_specs(tb),
        [_full((D_MODEL, IN_WIDTH))],
        [jax.ShapeDtypeStruct((D_MODEL, IN_WIDTH), BF16)],
        [pltpu.VMEM((D_MODEL, IN_WIDTH), F32)], (u1, dq, dkv, drz), carried, modes)


def _adamw_math(w, m, v, g):
    nm = ADAM_B1 * m + (1.0 - ADAM_B1) * g
    nv = ADAM_B2 * v + (1.0 - ADAM_B2) * (g * g)
    m_hat = nm / (1.0 - ADAM_B1 ** ADAM_STEP)
    v_hat = nv / (1.0 - ADAM_B2 ** ADAM_STEP)
    return (-ADAM_LR) * (m_hat / (jnp.sqrt(v_hat) + ADAM_EPS) + ADAM_WD * w), nm, nv


SMALL_NAMES = ("conv_b", "b_a", "b_x", "lru_lambda", "attn_sinks", "g_post_mix", "g_pre_ffn", "g_post_ffn")
PACK_WIDTH = 1024


def _pack_rows(vals):
    assert len(SMALL_NAMES) == SUBLANES
    row = lax.broadcasted_iota(jnp.int32, (SUBLANES, PACK_WIDTH), 0)
    tile = jnp.zeros((SUBLANES, PACK_WIDTH), F32)
    for k, name in enumerate(SMALL_NAMES):
        a = vals[name].reshape(1, -1)
        tile = jnp.where(row == k, jnp.pad(a, ((0, 0), (0, PACK_WIDTH - a.shape[1]))), tile)
    return tile


def _adamw_small(weights, mom_m, mom_v, parts, loss_parts):
    n = len(SMALL_NAMES)
    views = [(1, weights[name].size) for name in SMALL_NAMES]

    def body(*refs):
        w_refs, m_refs, v_refs = refs[:n], refs[n:2 * n], refs[2 * n:3 * n]
        p_ref, l_ref, loss_ref = refs[3 * n], refs[3 * n + 1], refs[3 * n + 2]
        outs = refs[3 * n + 3:]
        for k, (_, c) in enumerate(views):
            g = p_ref[0, k:k + 1, 0:c]
            for s in range(1, N_DEV):
                g = g + p_ref[s, k:k + 1, 0:c]
            g_ref, d_ref, nm_ref, nv_ref = outs[4 * k:4 * k + 4]
            g_ref[...] = g
            d_ref[...], nm_ref[...], nv_ref[...] = _adamw_math(w_refs[k][...], m_refs[k][...], v_refs[k][...], g)
        total = l_ref[0]
        for s in range(1, N_DEV):
            total = total + l_ref[s]
        loss_ref[...] = total

    args = [src[name].reshape(view) for src in (weights, mom_m, mom_v) for name, view in zip(SMALL_NAMES, views)]
    res = pl.pallas_call(
        body, name="adamw_small",
        out_shape=[jax.ShapeDtypeStruct(loss_parts.shape[1:], F32)]
                  + [jax.ShapeDtypeStruct(view, F32) for view in views for _ in range(4)],
        compiler_params=pltpu.CompilerParams(vmem_limit_bytes=VMEM_LIMIT),
    )(*args, parts, loss_parts)
    out = {name: tuple(t.reshape(weights[name].shape) for t in res[1 + 4 * k:5 + 4 * k]) for k, name in enumerate(SMALL_NAMES)}
    return res[0], out


def _adamw(w, m, v, parts, name):
    rows, cols = w.shape
    tr = next((t for t in (256, 128) if rows % t == 0), rows)
    parts = parts if isinstance(parts, (list, tuple)) else [parts]

    def body(w_ref, m_ref, v_ref, *refs):
        p_refs, (g_ref, d_ref, nm_ref, nv_ref) = refs[:len(parts)], refs[len(parts):]

        def total(p_ref):
            g = p_ref[0].astype(F32)
            for s in range(1, N_DEV):
                g = g + p_ref[s].astype(F32)
            return g

        g = jnp.concatenate([total(p_ref) for p_ref in p_refs], axis=1) if len(parts) > 1 else total(p_refs[0])
        g_ref[...] = g
        d_ref[...], nm_ref[...], nv_ref[...] = _adamw_math(w_ref[...], m_ref[...], v_ref[...], g)

    blk = pl.BlockSpec((tr, cols), lambda i: (i, 0))
    return pl.pallas_call(
        body, name=name, grid=(rows // tr,),
        in_specs=[blk, blk, blk] + [pl.BlockSpec((N_DEV, tr, p.shape[2]), lambda i: (0, i, 0)) for p in parts],
        out_specs=[blk] * 4,
        out_shape=[jax.ShapeDtypeStruct((rows, cols), F32)] * 4,
        compiler_params=_params(("parallel",)),
    )(w, m, v, *parts)


def _cols_from_shards(g):
    return jnp.transpose(g, (1, 0, 2)).reshape(g.shape[1], N_DEV * g.shape[2])


def _cols_to_shards(a):
    r, c = a.shape
    return jnp.transpose(a.reshape(r, N_DEV, c // N_DEV), (1, 0, 2))


def _block_diag(w):
    per = LRU_HALF // LRU_BLOCK
    w = w.reshape(2, per, LRU_BLOCK, LRU_BLOCK)
    eye = jnp.eye(per, dtype=w.dtype)
    return (w[:, :, :, None, :] * eye[None, :, None, :, None]).reshape(2, LRU_HALF, LRU_HALF)


def _block_diag_extract(t):
    per = LRU_HALF // LRU_BLOCK
    t = t.reshape(2, per, LRU_BLOCK, per, LRU_BLOCK)
    return jnp.stack([t[:, b, :, b, :] for b in range(per)], axis=1).reshape(LRU_BLOCKS, LRU_BLOCK, LRU_BLOCK)


def kernel(x, meta_tokens, g_pre_mix, w_in, conv_w, conv_b, w_a, b_a, w_x, b_x, lru_lambda, attn_sinks, w_out, g_post_mix, g_pre_ffn, w_ff1, w_ff2, g_post_ffn, loss_target, m_meta_tokens, m_g_pre_mix, m_w_in, m_conv_w, m_conv_b, m_w_a, m_b_a, m_w_x, m_b_x, m_lru_lambda, m_attn_sinks, m_w_out, m_g_post_mix, m_g_pre_ffn, m_w_ff1, m_w_ff2, m_g_post_ffn, v_meta_tokens, v_g_pre_mix, v_w_in, v_conv_w, v_conv_b, v_w_a, v_b_a, v_w_x, v_b_x, v_lru_lambda, v_attn_sinks, v_w_out, v_g_post_mix, v_g_pre_ffn, v_w_ff1, v_w_ff2, v_g_post_ffn):
    weights = dict(meta_tokens=meta_tokens, g_pre_mix=g_pre_mix, w_in=w_in, conv_w=conv_w, conv_b=conv_b, w_a=w_a,
                   b_a=b_a, w_x=w_x, b_x=b_x, lru_lambda=lru_lambda, attn_sinks=attn_sinks, w_out=w_out,
                   g_post_mix=g_post_mix, g_pre_ffn=g_pre_ffn, w_ff1=w_ff1, w_ff2=w_ff2, g_post_ffn=g_post_ffn)
    mom_m = dict(meta_tokens=m_meta_tokens, g_pre_mix=m_g_pre_mix, w_in=m_w_in, conv_w=m_conv_w, conv_b=m_conv_b,
                 w_a=m_w_a, b_a=m_b_a, w_x=m_w_x, b_x=m_b_x, lru_lambda=m_lru_lambda, attn_sinks=m_attn_sinks,
                 w_out=m_w_out, g_post_mix=m_g_post_mix, g_pre_ffn=m_g_pre_ffn, w_ff1=m_w_ff1, w_ff2=m_w_ff2,
                 g_post_ffn=m_g_post_ffn)
    mom_v = dict(meta_tokens=v_meta_tokens, g_pre_mix=v_g_pre_mix, w_in=v_w_in, conv_w=v_conv_w, conv_b=v_conv_b,
                 w_a=v_w_a, b_a=v_b_a, w_x=v_w_x, b_x=v_b_x, lru_lambda=v_lru_lambda, attn_sinks=v_attn_sinks,
                 w_out=v_w_out, g_post_mix=v_g_post_mix, g_pre_ffn=v_g_pre_ffn, w_ff1=v_w_ff1, w_ff2=v_w_ff2,
                 g_post_ffn=v_g_post_ffn)
    order = list(weights)

    (g_win, g_meta, g_cw) = _gather_two_level([w_in[0].astype(BF16), meta_tokens, conv_w[0]], "gather_first")
    w_in_full = _cols_from_shards(g_win)
    meta_full = _cols_from_shards(g_meta)
    conv_w_full = _cols_from_shards(g_cw)

    head = jnp.concatenate([jnp.zeros((PAD_ROWS, D_MODEL), F32), meta_full], axis=0)
    wa_bd = _block_diag(w_a[0]).astype(BF16)
    wx_bd = _block_diag(w_x[0]).astype(BF16)
    bias = _attn_bias()

    w1_shard = w_ff1[0].astype(BF16)
    (qkv, zrec, u1), (g_wout,) = _in_proj_fwd(head, x[0], g_pre_mix, w_in_full, [w_out[0].astype(BF16)], ["gather"])
    (attn,), (w1a,) = _attn_fwd(qkv, attn_sinks, bias, [w1_shard[:, :FF_HALF]], ["gather"])
    (rec, h_lru, kept), (w1b,) = _rec_fwd(zrec, conv_w_full, conv_b, wa_bd, b_a, wx_bd, b_x, lru_lambda,
                                         [w1_shard[:, FF_HALF:]], ["gather"])
    w_out_full = g_wout.reshape(D_MODEL, D_MODEL)
    w2_shard = w_ff2[0].astype(BF16)
    (mix, h1), (w2a,) = _out_proj_fwd(attn, rec, w_out_full, head, x[0], g_post_mix, [w2_shard[:FF_HALF]], ["gather"])
    (act, u2), (w2b,) = _ffn_up(h1, g_pre_ffn, (w1a, w1b), [w2_shard[FF_HALF:]], ["gather"])
    w2_halves = [w.reshape(D_FF // 2, D_MODEL) for w in (w2a, w2b)]
    (dy, df, dg_post_ffn, loss_acc), w2t_halves = _ffn_down_loss(
        act, w2_halves, h1, loss_target[0], g_post_ffn, [w2_shard[:FF_HALF].T, w2_shard[FF_HALF:].T], ["gather"] * 2)

    (da1,), (w1ta,) = _ffn_bwd_act(df, w2t_halves, act, [w1_shard[:, :FF_HALF].T], ["gather"])
    (dw1h, dw2g), (w1tb,) = _ffn_bwd_weights(u2, da1, act, df, [w1_shard[:, FF_HALF:].T], ["gather"])
    w1t_halves = [w.reshape(D_FF // 2, D_MODEL) for w in (w1ta, w1tb)]
    (dh1, dg_pre_ffn), (p_w1a,) = _ffn_bwd_x(da1, w1t_halves, h1, dy, g_pre_ffn, [dw1h[0]], ["scatter"])
    (dattn, drec, dw_out, dg_post_mix), (p_w1b,) = _out_proj_bwd(dh1, mix, g_post_mix, w_out_full.T, attn, rec,
                                                                [dw1h[1]], ["scatter"])
    (dq, dkv_late, dsinks), (p_w2,) = _attn_bwd(qkv, dattn, attn_sinks, bias, [dw2g], ["scatter"])
    dkv = dkv_late[BLOCK:BLOCK + qkv.shape[0]]
    (drz, rec_small, dwa_bd, dwx_bd), (p_wout,) = _rec_bwd(
        drec, zrec, h_lru, kept, conv_w_full, wa_bd, wx_bd, lru_lambda,
        [dw_out.reshape(N_DEV, D_MODEL // N_DEV, D_MODEL)], ["scatter"])
    small_grads = dict(
        conv_b=rec_small[ROW_CONV_B], b_a=rec_small[ROW_B_A], b_x=rec_small[ROW_B_X], lru_lambda=rec_small[ROW_LAMBDA],
        attn_sinks=dsinks[:, 0], g_post_mix=dg_post_mix, g_pre_ffn=dg_pre_ffn, g_post_ffn=dg_post_ffn)
    gate_rows = (LRU_BLOCKS * LRU_BLOCK, LRU_BLOCK)
    gate_dense = (LRU_BLOCKS * LRU_BLOCK * LRU_BLOCK // PACK_WIDTH, PACK_WIDTH)
    (dw_in,), (p_cw, p_small, p_wa, p_wx) = _in_proj_bwd_w(
        u1, dq, dkv, drz,
        [_cols_to_shards(rec_small[0:CONV_WIDTH]), _pack_rows(small_grads),
         _block_diag_extract(dwa_bd).reshape(gate_dense), _block_diag_extract(dwx_bd).reshape(gate_dense)],
        ["scatter", "gather", "gather", "gather"])
    p_wa, p_wx = (p.reshape((N_DEV,) + gate_rows) for p in (p_wa, p_wx))
    (dh0, dg_pre_mix), (p_win,) = _in_proj_bwd_x(
        head, x[0], g_pre_mix, dh1, dq, dkv, drz, w_in_full.T, [_cols_to_shards(dw_in)], ["scatter"])
    p_meta, p_gpm, p_loss = _exchange([_cols_to_shards(dh0[PAD_ROWS:BLOCK]), dg_pre_mix, loss_acc],
                                      ["scatter", "gather", "gather"], "exchange_last")

    res = {}
    res["g_pre_mix"] = _adamw(g_pre_mix, m_g_pre_mix, v_g_pre_mix, p_gpm, "adamw_g_pre_mix")
    res["w_in"] = _adamw(w_in[0], m_w_in[0], v_w_in[0], p_win, "adamw_w_in")
    res["w_out"] = _adamw(w_out[0], m_w_out[0], v_w_out[0], p_wout, "adamw_w_out")
    res["w_ff1"] = _adamw(w_ff1[0], m_w_ff1[0], v_w_ff1[0], [p_w1a, p_w1b], "adamw_w_ff1")
    res["w_ff2"] = _adamw(w_ff2[0], m_w_ff2[0], v_w_ff2[0], p_w2, "adamw_w_ff2")
    res["meta_tokens"] = _adamw(meta_tokens, m_meta_tokens, v_meta_tokens, p_meta, "adamw_meta")
    res["conv_w"] = _adamw(conv_w[0], m_conv_w[0], v_conv_w[0], p_cw, "adamw_conv_w")
    for name in ("w_in", "w_out", "w_ff1", "w_ff2", "conv_w"):
        res[name] = tuple(t[None] for t in res[name])
    for name, parts in (("w_a", p_wa), ("w_x", p_wx)):
        gate = _adamw(*(src[name].reshape(gate_rows) for src in (weights, mom_m, mom_v)), parts, "adamw_" + name)
        res[name] = tuple(t.reshape(weights[name].shape) for t in gate)
    loss_total, small = _adamw_small(weights, mom_m, mom_v, p_small, p_loss)
    res.update(small)

    grad_x = dh0[BLOCK:][None]
    outs = [loss_total[0, 0], grad_x]
    for k in range(4):
        outs += [res[name][k] for name in order]
    return tuple(outs)
```

```python
import jax
import jax.numpy as jnp
import numpy as np
from jax import lax
from jax.experimental import pallas as pl
from jax.experimental.pallas import tpu as pltpu

F32 = jnp.float32
BF16 = jnp.bfloat16

D_MODEL = 1024
N_META = 16
HEAD_DIM = 64
ATTN_HEADS = 8
KV_HEADS = 2
GQA_GROUP = ATTN_HEADS // KV_HEADS
ATTN_WIDTH = ATTN_HEADS * HEAD_DIM
KV_WIDTH = KV_HEADS * HEAD_DIM
QKV_WIDTH = ATTN_WIDTH + 2 * KV_WIDTH
LRU_WIDTH = 512
LRU_BLOCKS = 8
LRU_BLOCK = 64
LRU_HALF = 256
LRU_C = 8.0
CONV_WIDTH = 4
BLOCK = 128
PAD_ROWS = BLOCK - N_META
IN_WIDTH = QKV_WIDTH + 2 * LRU_WIDTH
D_FF = 4096
EPS = 1e-6
NEG = -1e30
N_DEV = 8
FF_CHUNK = D_FF // N_DEV
SUBLANES = 8
LANES = 128

ADAM_LR = 0.001
ADAM_B1 = 0.9
ADAM_B2 = 0.999
ADAM_EPS = 1e-08
ADAM_WD = 0.01
ADAM_STEP = 10

VMEM_LIMIT = 56 * 1024 * 1024


def _row_tile(rows):
    for t in (640, 512, 256, 128):
        if rows % t == 0:
            return t
    raise ValueError(rows)


def _big_tile(rows):
    for t in (1664, 1024, 512, 256, 128):
        if rows % t == 0:
            return t
    raise ValueError(rows)


def _rec_tile(rows):
    for t in (416, 256, 128):
        if rows % t == 0:
            return t
    raise ValueError(rows)


def _params(semantics):
    return pltpu.CompilerParams(dimension_semantics=semantics, vmem_limit_bytes=VMEM_LIMIT)


def _mm(a, b):
    return lax.dot_general(a, b, (((1,), (0,)), ((), ())), preferred_element_type=F32)


def _mm_nt(a, b):
    return lax.dot_general(a, b, (((1,), (1,)), ((), ())), preferred_element_type=F32)


def _mm_tn(a, b):
    return lax.dot_general(a, b, (((0,), (0,)), ((), ())), preferred_element_type=F32)


def _rms_fwd(x, g):
    rstd = lax.rsqrt(jnp.mean(x * x, axis=-1, keepdims=True) + EPS)
    xhat = x * rstd
    return xhat * g, xhat, rstd


def _rms_bwd(dy, xhat, rstd, g):
    dyg = dy * g
    c = jnp.mean(dyg * xhat, axis=-1, keepdims=True)
    dx = rstd * (dyg - xhat * c)
    dg = jnp.sum(dy * xhat, axis=0, keepdims=True)
    return dx, dg


def _sigmoid(x):
    return 0.5 * jnp.tanh(0.5 * x) + 0.5


def _log1p(x):
    u = 1.0 + x
    return jnp.where(u == 1.0, x, jnp.log(u) * x / (u - 1.0))


def _one_minus_sq_exp(x, ex):
    return -jnp.tanh(x) * (1.0 + ex * ex)


TINY = 1e-30


def _sqrt_pos(y):
    r = lax.rsqrt(jnp.maximum(y, TINY))
    return y * r, r


def _softplus(x):
    return jnp.maximum(x, 0.0) + _log1p(jnp.exp(-jnp.abs(x)))


GELU_C = 0.7978845608028654
GELU_K = 0.044715


def _gelu(x):
    t = jnp.tanh(GELU_C * (x + GELU_K * x * x * x))
    return 0.5 * x * (1.0 + t), t


def _gelu_grad(x, t):
    return 0.5 * (1.0 + t) + 0.5 * x * (1.0 - t * t) * GELU_C * (1.0 + 3.0 * GELU_K * x * x)


def _full(shape):
    return pl.BlockSpec(shape, lambda *_: (0,) * len(shape))


def _resident(shape):
    return pl.BlockSpec(shape, lambda *_: (0,) * len(shape), pipeline_mode=pl.Buffered(1))


def _exchange_copies(ins, outs, sems, modes):
    send_sems, recv_sems, local_sems = sems
    x, y, c = lax.axis_index("x"), lax.axis_index("y"), lax.axis_index("c")
    me = 4 * x + 2 * y + c

    def block(a, dev):
        return ins[a] if modes[a] == "gather" else ins[a].at[dev]

    local = [pltpu.make_async_copy(block(a, me), outs[a].at[me], local_sems.at[a]) for a in range(len(ins))]
    sends, recvs = [], []
    for a in range(len(ins)):
        for k in range(N_DEV - 1):
            bits = k + 1
            px = jnp.bitwise_xor(x, (bits >> 2) & 1)
            py = jnp.bitwise_xor(y, (bits >> 1) & 1)
            pc = jnp.bitwise_xor(c, bits & 1)
            peer = 4 * px + 2 * py + pc
            common = dict(src_ref=block(a, peer), send_sem=send_sems.at[a, k], recv_sem=recv_sems.at[a, k],
                          device_id=(px, py, pc), device_id_type=pl.DeviceIdType.MESH)
            sends.append(pltpu.make_async_remote_copy(dst_ref=outs[a].at[me], **common))
            recvs.append(pltpu.make_async_remote_copy(dst_ref=outs[a].at[peer], **common))
    return local, sends, recvs


def _exchange_start(ins, outs, sems, modes):
    local, sends, _ = _exchange_copies(ins, outs, sems, modes)
    for cp in local + sends:
        cp.start()


def _exchange_wait(ins, outs, sems, modes):
    local, sends, recvs = _exchange_copies(ins, outs, sems, modes)
    for cp in recvs:
        cp.wait_recv()
    for cp in sends:
        cp.wait_send()
    for cp in local:
        cp.wait()


def _exchange_shapes(arrays, modes):
    return [jax.ShapeDtypeStruct((N_DEV,) + a.shape if mode == "gather" else a.shape, a.dtype)
            for a, mode in zip(arrays, modes)]


def _exchange_sems(na):
    return [pltpu.SemaphoreType.DMA((na, N_DEV - 1)), pltpu.SemaphoreType.DMA((na, N_DEV - 1)),
            pltpu.SemaphoreType.DMA((na,))]


ANY_SPACE = pl.BlockSpec(memory_space=pl.ANY)


def _exchange(arrays, modes, name):
    na = len(arrays)

    def body(*refs):
        ins, outs, sems = refs[:na], refs[na:2 * na], refs[2 * na:]
        _exchange_start(ins, outs, sems, modes)
        _exchange_wait(ins, outs, sems, modes)

    return pl.pallas_call(
        body, name=name, out_shape=_exchange_shapes(arrays, modes),
        in_specs=[ANY_SPACE] * na, out_specs=[ANY_SPACE] * na, scratch_shapes=_exchange_sems(na),
        compiler_params=pltpu.CompilerParams(has_side_effects=True),
    )(*arrays)


def _gather_two_level(arrays, name):
    na = len(arrays)

    def body(*refs):
        ins, outs = refs[:na], refs[na:2 * na]
        send_sems, recv_sems, local_sems = refs[2 * na:]
        x, y, c = lax.axis_index("x"), lax.axis_index("y"), lax.axis_index("c")
        me, sibling = (x, y, c), (x, y, 1 - c)
        chips = [(1 - x, y), (x, 1 - y), (1 - x, 1 - y)]

        def copy(a, k, block, to, src=None):
            slot = outs[a].at[4 * block[0] + 2 * block[1] + block[2]]
            return pltpu.make_async_remote_copy(
                src_ref=slot if src is None else src, dst_ref=slot, send_sem=send_sems.at[a, k],
                recv_sem=recv_sems.at[a, k], device_id=to, device_id_type=pl.DeviceIdType.MESH)

        local = [pltpu.make_async_copy(ins[a], outs[a].at[4 * x + 2 * y + c], local_sems.at[a]) for a in range(na)]
        first = []
        for a in range(na):
            first.append(copy(a, 0, me, sibling, src=ins[a]))
            first += [copy(a, 1 + j, me, (*chip, c), src=ins[a]) for j, chip in enumerate(chips)]
        for cp in local + first:
            cp.start()
        passed = []
        for j, chip in enumerate(chips):
            for a in range(na):
                copy(a, 1 + j, (*chip, c), me).wait_recv()
                passed.append(copy(a, 4 + j, (*chip, c), sibling))
                passed[-1].start()
        for a in range(na):
            copy(a, 0, sibling, me).wait_recv()
            for j, chip in enumerate(chips):
                copy(a, 4 + j, (*chip, 1 - c), me).wait_recv()
        for cp in first + passed:
            cp.wait_send()
        for cp in local:
            cp.wait()

    return pl.pallas_call(
        body, name=name, out_shape=_exchange_shapes(arrays, ["gather"] * na),
        in_specs=[ANY_SPACE] * na, out_specs=[ANY_SPACE] * na, scratch_shapes=_exchange_sems(na),
        compiler_params=pltpu.CompilerParams(has_side_effects=True),
    )(*arrays)


def _hosting_call(body, name, steps, in_specs, out_specs, out_shape, scratch_shapes, args, arrays, modes):
    n_in, n_out, n_scr, na = len(in_specs), len(out_specs), len(scratch_shapes), len(arrays)
    grid = steps if isinstance(steps, tuple) else (steps,)

    def hosting_body(*refs):
        cuts = [0]
        for n in (n_in, na, n_out, na, n_scr, 3):
            cuts.append(cuts[-1] + n)
        ins, x_ins, outs, x_outs, scr, sems = (refs[cuts[p]:cuts[p + 1]] for p in range(6))
        first, last = True, True
        for axis, n in enumerate(grid):
            first = first & (pl.program_id(axis) == 0)
            last = last & (pl.program_id(axis) == n - 1)

        @pl.when(first)
        def _():
            _exchange_start(x_ins, x_outs, sems, modes)

        body(*ins, *outs, *scr)

        @pl.when(last)
        def _():
            _exchange_wait(x_ins, x_outs, sems, modes)

    res = pl.pallas_call(
        hosting_body, name=name, grid=grid,
        in_specs=list(in_specs) + [ANY_SPACE] * na, out_specs=list(out_specs) + [ANY_SPACE] * na,
        out_shape=list(out_shape) + _exchange_shapes(arrays, modes),
        scratch_shapes=list(scratch_shapes) + _exchange_sems(na),
        compiler_params=_params(("arbitrary",) * len(grid)),
    )(*args, *arrays)
    return res[:n_out], res[n_out:]


def _frame_rows(src_hbm, buf, sem, i, steps, tm):
    def first():
        return pltpu.make_async_copy(src_hbm.at[pl.ds(0, tm - BLOCK)], buf.at[0, pl.ds(BLOCK, tm - BLOCK)], sem.at[0])

    def later(t, slot):
        return pltpu.make_async_copy(src_hbm.at[pl.ds(pl.multiple_of(t * tm - BLOCK, SUBLANES), tm)], buf.at[slot], sem.at[slot])

    slot = i % 2

    @pl.when(i == 0)
    def _():
        first().start()

    @pl.when(i + 1 < steps)
    def _():
        later(i + 1, 1 - slot).start()

    @pl.when(i == 0)
    def _():
        first().wait()

    @pl.when(i > 0)
    def _():
        later(i, slot).wait()

    return slot


def _frame_scratch(tm):
    return [pltpu.VMEM((2, tm, D_MODEL), F32), pltpu.SemaphoreType.DMA((2,))]


def _h0_tile(head_ref, x_hbm, buf, sem, i, steps, tm):
    slot = _frame_rows(x_hbm, buf, sem, i, steps, tm)

    @pl.when(i == 0)
    def _():
        buf[0, 0:BLOCK, :] = head_ref[...]

    return buf[slot]


def _in_proj_fwd(head, x, g1, w_in, carried, modes):
    rows = BLOCK + x.shape[0]
    tm = _row_tile(rows)
    steps = rows // tm

    def body(head_ref, g_ref, w_ref, x_hbm, qkv_ref, zrec_ref, u_ref, buf, sem):
        h = _h0_tile(head_ref, x_hbm, buf, sem, pl.program_id(0), steps, tm)
        u, _, _ = _rms_fwd(h, g_ref[...])
        u = u.astype(BF16)
        u_ref[...] = u
        z = _mm(u, w_ref[...])
        qkv_ref[...] = z[:, :QKV_WIDTH].astype(BF16)
        zrec_ref[...] = z[:, QKV_WIDTH:]

    wide = pl.BlockSpec((tm, D_MODEL), lambda i: (i, 0))
    return _hosting_call(
        body, "in_proj_fwd", steps,
        [_full((BLOCK, D_MODEL)), _full((1, D_MODEL)), _resident((D_MODEL, IN_WIDTH)), ANY_SPACE],
        [pl.BlockSpec((tm, QKV_WIDTH), lambda i: (i, 0)), pl.BlockSpec((tm, 2 * LRU_WIDTH), lambda i: (i, 0)), wide],
        [jax.ShapeDtypeStruct((rows, QKV_WIDTH), BF16), jax.ShapeDtypeStruct((rows, 2 * LRU_WIDTH), F32),
         jax.ShapeDtypeStruct((rows, D_MODEL), BF16)],
        _frame_scratch(tm), (head, g1, w_in, x), carried, modes)


N_BIAS = 3


def _attn_bias():
    key = np.arange(2 * BLOCK)[:, None]
    r = np.arange(GQA_GROUP * BLOCK)[None, :] % BLOCK
    band = (key > r) & (key <= r + BLOCK)
    out = [np.where(band & ((n - 1) * BLOCK + key >= PAD_ROWS), 0.0, NEG) for n in range(N_BIAS)]
    return jnp.asarray(np.stack(out), F32)


def _attn_probs(k2, q4, bias, sink_row):
    s = _mm_nt(k2, q4) * (HEAD_DIM ** -0.5) + bias
    m = jnp.maximum(jnp.max(s, axis=0, keepdims=True), sink_row)
    p = jnp.exp(s - m)
    es = jnp.exp(sink_row - m)
    inv = 1.0 / (jnp.sum(p, axis=0, keepdims=True) + es)
    return p * inv, es * inv


def _heads(ref, rows, first, count):
    return jnp.concatenate([ref[rows, (first + g) * HEAD_DIM:(first + g + 1) * HEAD_DIM] for g in range(count)], axis=0)


def _keys_of_block(prev_ref, cur_ref, b, kv):
    sl = slice(kv * HEAD_DIM, (kv + 1) * HEAD_DIM)
    before = prev_ref[:, sl] if b == 0 else cur_ref[(b - 1) * BLOCK:b * BLOCK, sl]
    return jnp.concatenate([before, cur_ref[b * BLOCK:(b + 1) * BLOCK, sl]], axis=0)


def _bias_of_block(bias_ref, block):
    return bias_ref[jnp.minimum(block, N_BIAS - 1)]


def _sink_row(sink_ref, kv):
    g = lax.broadcasted_iota(jnp.int32, (1, GQA_GROUP * BLOCK), 1) // BLOCK
    row = jnp.full((1, GQA_GROUP * BLOCK), sink_ref[0, kv * GQA_GROUP], F32)
    for i in range(1, GQA_GROUP):
        row = jnp.where(g == i, sink_ref[0, kv * GQA_GROUP + i], row)
    return row


def _from_head_major(pieces):
    return jnp.concatenate(pieces, axis=0).T


def _attn_specs(tm, tile_of):
    nbt = tm // BLOCK
    k_col, v_col = ATTN_WIDTH // KV_WIDTH, ATTN_WIDTH // KV_WIDTH + 1
    before = lambda i: jnp.maximum(tile_of(i) * nbt - 1, 0)
    return [pl.BlockSpec((tm, ATTN_WIDTH), lambda i: (tile_of(i), 0)),
            pl.BlockSpec((BLOCK, KV_WIDTH), lambda i: (before(i), k_col)),
            pl.BlockSpec((tm, KV_WIDTH), lambda i: (tile_of(i), k_col)),
            pl.BlockSpec((BLOCK, KV_WIDTH), lambda i: (before(i), v_col)),
            pl.BlockSpec((tm, KV_WIDTH), lambda i: (tile_of(i), v_col))]


def _attn_fwd(qkv, sinks, bias, carried, modes):
    rows = qkv.shape[0]
    tm = _row_tile(rows)
    nbt = tm // BLOCK

    def body(sink_ref, bias_ref, q_ref, kp_ref, kc_ref, vp_ref, vc_ref, o_ref):
        i = pl.program_id(0)
        for b in range(nbt):
            blk = slice(b * BLOCK, (b + 1) * BLOCK)
            bias_t = _bias_of_block(bias_ref, i * nbt + b)
            pieces = []
            for kv in range(KV_HEADS):
                k2 = _keys_of_block(kp_ref, kc_ref, b, kv)
                v2 = _keys_of_block(vp_ref, vc_ref, b, kv)
                q4 = _heads(q_ref, blk, kv * GQA_GROUP, GQA_GROUP)
                pn, _ = _attn_probs(k2, q4, bias_t, _sink_row(sink_ref, kv))
                ot = _mm_tn(v2, pn.astype(BF16))
                pieces += [ot[:, g * BLOCK:(g + 1) * BLOCK] for g in range(GQA_GROUP)]
            o_ref[blk, :] = _from_head_major(pieces).astype(BF16)

    return _hosting_call(
        body, "attn_fwd", rows // tm,
        [pl.BlockSpec(memory_space=pltpu.SMEM), _resident((N_BIAS, 2 * BLOCK, GQA_GROUP * BLOCK))]
        + _attn_specs(tm, lambda i: i),
        [pl.BlockSpec((tm, ATTN_WIDTH), lambda i: (i, 0))],
        [jax.ShapeDtypeStruct((rows, ATTN_WIDTH), BF16)],
        [], (sinks, bias, qkv, qkv, qkv, qkv, qkv), carried, modes)


def _conv_taps(xbuf, tm):
    return [xbuf[pl.ds(SUBLANES - (CONV_WIDTH - 1 - j), tm), :] for j in range(CONV_WIDTH)]


def _lru_halves(xc):
    return [xc[:, h * LRU_HALF:(h + 1) * LRU_HALF].astype(BF16) for h in range(2)]


KEPT_XC, KEPT_A, KEPT_MULT, KEPT_R, KEPT_I, N_KEPT = 0, 1, 2, 3, 4, 5


def _kept(k):
    return slice(k * LRU_WIDTH, (k + 1) * LRU_WIDTH)


def _rec_fwd(zrec, conv_w, conv_b, wa_bd, b_a, wx_bd, b_x, lam, carried, modes):
    rows = zrec.shape[0]
    tm = _row_tile(rows)
    groups = tm // SUBLANES

    def body(xr_ref, yr_ref, cw_ref, cb_ref, wa_ref, ba_ref, wx_ref, bx_ref, lam_ref, rec_ref, h_ref, kept_ref,
             xbuf, gr_s, gi_s, rec_s, carry):
        i = pl.program_id(0)

        @pl.when(i == 0)
        def _():
            xbuf[0:SUBLANES, :] = jnp.zeros((SUBLANES, LRU_WIDTH), F32)
            carry[...] = jnp.zeros_like(carry)

        @pl.when(i > 0)
        def _():
            xbuf[0:SUBLANES, :] = xbuf[tm:tm + SUBLANES, :]

        xbuf[SUBLANES:SUBLANES + tm, :] = xr_ref[...]
        taps = _conv_taps(xbuf, tm)
        xc = cb_ref[...] + sum(cw_ref[j:j + 1, :] * taps[j] for j in range(CONV_WIDTH))
        kept_ref[:, _kept(KEPT_XC)] = xc
        halves = _lru_halves(xc)
        gr_s[...] = jnp.concatenate([_mm(halves[h], wa_ref[h]) for h in range(2)], axis=1)
        gi_s[...] = jnp.concatenate([_mm(halves[h], wx_ref[h]) for h in range(2)], axis=1)

        row = lax.broadcasted_iota(jnp.int32, (SUBLANES, LRU_WIDTH), 0)
        b_r, b_i, log_a_scale = ba_ref[...], bx_ref[...], (-LRU_C) * _softplus(-lam_ref[...])

        def group(j, h_before):
            o = pl.multiple_of(j * SUBLANES, SUBLANES)
            rows8 = pl.ds(o, SUBLANES)
            xc_t = kept_ref[rows8, _kept(KEPT_XC)]
            r = _sigmoid(gr_s[rows8, :] + b_r)
            ig = _sigmoid(gi_s[rows8, :] + b_i)
            log_a = r * log_a_scale
            a = jnp.exp(log_a)
            mult, _ = _sqrt_pos(_one_minus_sq_exp(log_a, a))
            for k, val in ((KEPT_A, a), (KEPT_MULT, mult), (KEPT_R, r), (KEPT_I, ig)):
                kept_ref[rows8, _kept(k)] = val
            u = jnp.where(i * tm + o + row >= PAD_ROWS, mult * (ig * xc_t), 0.0)
            coef = a
            for sft in (1, 2, 4):
                keep = row >= sft
                u = jnp.where(keep, coef * pltpu.roll(u, sft, 0) + u, u)
                coef = jnp.where(keep, coef * pltpu.roll(coef, sft, 0), coef)
            h_t = coef * h_before + u
            h_ref[rows8, :] = h_t
            gel, _ = _gelu(yr_ref[rows8, :])
            rec_s[rows8, :] = gel * h_t
            return h_t[SUBLANES - 1:SUBLANES, :]

        carry[0:1, :] = lax.fori_loop(0, groups, group, carry[0:1, :], unroll=2)
        rec_ref[...] = rec_s[...].astype(BF16)

    vec = _full((1, LRU_WIDTH))
    bd = _full((2, LRU_HALF, LRU_HALF))
    return _hosting_call(
        body, "rec_fwd", rows // tm,
        [pl.BlockSpec((tm, LRU_WIDTH), lambda i: (i, 0)), pl.BlockSpec((tm, LRU_WIDTH), lambda i: (i, 1)),
         _full((CONV_WIDTH, LRU_WIDTH)), vec, bd, vec, bd, vec, vec],
        [pl.BlockSpec((tm, LRU_WIDTH), lambda i: (i, 0))] * 2 + [pl.BlockSpec((tm, N_KEPT * LRU_WIDTH), lambda i: (i, 0))],
        [jax.ShapeDtypeStruct((rows, LRU_WIDTH), BF16), jax.ShapeDtypeStruct((rows, LRU_WIDTH), F32),
         jax.ShapeDtypeStruct((rows, N_KEPT * LRU_WIDTH), F32)],
        [pltpu.VMEM((tm + SUBLANES, LRU_WIDTH), F32)] + [pltpu.VMEM((tm, LRU_WIDTH), F32)] * 3
        + [pltpu.VMEM((SUBLANES, LRU_WIDTH), F32)],
        (zrec, zrec, conv_w, conv_b, wa_bd, b_a, wx_bd, b_x, lam), carried, modes)


def _out_proj_fwd(attn, rec, w_out, head, x, g2, carried, modes):
    rows = attn.shape[0]
    tm = _row_tile(rows)
    steps = rows // tm

    def body(attn_ref, rec_ref, w_ref, head_ref, g_ref, x_hbm, mix_ref, h1_ref, buf, sem):
        h0 = _h0_tile(head_ref, x_hbm, buf, sem, pl.program_id(0), steps, tm)
        mix = _mm(attn_ref[...], w_ref[0:ATTN_WIDTH, :]) + _mm(rec_ref[...], w_ref[ATTN_WIDTH:, :])
        y, _, _ = _rms_fwd(mix, g_ref[...])
        mix_ref[...] = mix
        h1_ref[...] = h0 + y

    half = pl.BlockSpec((tm, ATTN_WIDTH), lambda i: (i, 0))
    wide = pl.BlockSpec((tm, D_MODEL), lambda i: (i, 0))
    return _hosting_call(
        body, "out_proj_fwd", steps,
        [half, half, _resident((D_MODEL, D_MODEL)), _full((BLOCK, D_MODEL)), _full((1, D_MODEL)), ANY_SPACE],
        [wide, wide],
        [jax.ShapeDtypeStruct((rows, D_MODEL), F32)] * 2,
        _frame_scratch(tm), (attn, rec, w_out, head, g2, x), carried, modes)


FF_COLS = 1024
FF_HALF = FF_CHUNK // 2


def _hidden_at(d, half):
    return half * (D_FF // 2) + d * FF_HALF


def _ffn_up(h1, g3, w1_halves, carried, modes):
    rows = h1.shape[0]
    tm = _row_tile(rows)

    def body(h_ref, g_ref, wa_ref, wb_ref, act_ref, u_ref):
        u, _, _ = _rms_fwd(h_ref[...], g_ref[...])
        u = u.astype(BF16)
        u_ref[...] = u
        for half, w_ref in enumerate((wa_ref, wb_ref)):
            for d in range(N_DEV):
                c = _hidden_at(d, half)
                a1 = jnp.maximum(_mm(u, w_ref[d]), 0.0)
                act_ref[:, c:c + FF_HALF] = (a1 * a1).astype(BF16)

    wide = pl.BlockSpec((tm, D_MODEL), lambda i: (i, 0))
    return _hosting_call(
        body, "ffn_up", rows // tm,
        [wide, _full((1, D_MODEL))] + [_resident((N_DEV, D_MODEL, FF_HALF))] * 2,
        [pl.BlockSpec((tm, D_FF), lambda i: (i, 0)), wide],
        [jax.ShapeDtypeStruct((rows, D_FF), BF16), jax.ShapeDtypeStruct((rows, D_MODEL), BF16)],
        [], (h1, g3, *w1_halves), carried, modes)


def _ffn_down_loss(act, w2_halves, h1, target, g4, carried, modes):
    rows = h1.shape[0]
    tm = _row_tile(rows)
    steps = rows // tm
    kh = D_FF // 2

    def body(act_ref, wa_ref, wb_ref, h_ref, g_ref, t_hbm, dy_ref, df_ref, dg_ref, loss_ref, buf, sem):
        i = pl.program_id(0)
        slot = _frame_rows(t_hbm, buf, sem, i, steps, tm)

        @pl.when(i == 0)
        def _():
            dg_ref[...] = jnp.zeros_like(dg_ref)
            loss_ref[...] = jnp.zeros_like(loss_ref)
            buf[0, 0:BLOCK, :] = jnp.zeros((BLOCK, D_MODEL), F32)

        g = g_ref[...]
        f = _mm(act_ref[:, :kh], wa_ref[...]) + _mm(act_ref[:, kh:], wb_ref[...])
        y, fhat, rstd = _rms_fwd(f, g)
        grow = i * tm + lax.broadcasted_iota(jnp.int32, (tm, D_MODEL), 0)
        err = jnp.where(grow >= BLOCK, h_ref[...] + y - buf[slot], 0.0)
        loss_ref[...] += (0.5 / D_MODEL) * jnp.sum(err * err)
        dy = err * (1.0 / D_MODEL)
        df, dg = _rms_bwd(dy, fhat, rstd, g)
        dy_ref[...] = dy
        df_ref[...] = df.astype(BF16)
        dg_ref[...] += dg

    wide = pl.BlockSpec((tm, D_MODEL), lambda i: (i, 0))
    return _hosting_call(
        body, "ffn_down_loss", steps,
        [pl.BlockSpec((tm, D_FF), lambda i: (i, 0)), _resident((kh, D_MODEL)), _resident((kh, D_MODEL)), wide,
         _full((1, D_MODEL)), ANY_SPACE],
        [wide, wide, _full((1, D_MODEL)), _full((SUBLANES, LANES))],
        [jax.ShapeDtypeStruct((rows, D_MODEL), F32), jax.ShapeDtypeStruct((rows, D_MODEL), BF16),
         jax.ShapeDtypeStruct((1, D_MODEL), F32), jax.ShapeDtypeStruct((SUBLANES, LANES), F32)],
        _frame_scratch(tm), (act, *w2_halves, h1, g4, target), carried, modes)


def _ffn_bwd_act(df, w2t_halves, act, carried, modes):
    rows = df.shape[0]
    tm = _row_tile(rows)

    def body(df_ref, wa_ref, wb_ref, act_ref, da_ref):
        df_t = df_ref[...]
        for half, w_ref in enumerate((wa_ref, wb_ref)):
            for d in range(N_DEV):
                cols = slice(_hidden_at(d, half), _hidden_at(d, half) + FF_HALF)
                dact = _mm(df_t, w_ref[d])
                relu_a1, _ = _sqrt_pos(act_ref[:, cols].astype(F32))
                da_ref[:, cols] = (dact * (2.0 * relu_a1)).astype(BF16)

    hidden = pl.BlockSpec((tm, D_FF), lambda i: (i, 0))
    return _hosting_call(
        body, "ffn_bwd_act", rows // tm,
        [pl.BlockSpec((tm, D_MODEL), lambda i: (i, 0))] + [_resident((N_DEV, D_MODEL, FF_HALF))] * 2 + [hidden],
        [hidden],
        [jax.ShapeDtypeStruct((rows, D_FF), BF16)],
        [], (df, *w2t_halves, act), carried, modes)


def _ffn_bwd_x(da, w1t_halves, h1, dy, g3, carried, modes):
    rows = h1.shape[0]
    tm = _row_tile(rows)
    kh = D_FF // 2

    def body(da_ref, wa_ref, wb_ref, h_ref, dy_ref, g_ref, dh_ref, dg_ref):
        @pl.when(pl.program_id(0) == 0)
        def _():
            dg_ref[...] = jnp.zeros_like(dg_ref)

        g = g_ref[...]
        _, xhat, rstd = _rms_fwd(h_ref[...], g)
        du = _mm(da_ref[:, :kh], wa_ref[...]) + _mm(da_ref[:, kh:], wb_ref[...])
        dx, dg = _rms_bwd(du, xhat, rstd, g)
        dh_ref[...] = dy_ref[...] + dx
        dg_ref[...] += dg

    wide = pl.BlockSpec((tm, D_MODEL), lambda i: (i, 0))
    return _hosting_call(
        body, "ffn_bwd_x", rows // tm,
        [pl.BlockSpec((tm, D_FF), lambda i: (i, 0)), _resident((kh, D_MODEL)), _resident((kh, D_MODEL)), wide, wide,
         _full((1, D_MODEL))],
        [wide, _full((1, D_MODEL))],
        [jax.ShapeDtypeStruct((rows, D_MODEL), F32), jax.ShapeDtypeStruct((1, D_MODEL), F32)],
        [], (da, *w1t_halves, h1, dy, g3), carried, modes)


def _ffn_bwd_weights(u2, da, act, df, carried, modes):
    rows = u2.shape[0]
    tb = _big_tile(rows)
    steps = rows // tb
    per = FF_COLS // FF_HALF

    def body(u_ref, da_ref, act_ref, df_ref, dw1_ref, dw2_ref, acc1, acc2):
        i = pl.program_id(1)

        @pl.when(i == 0)
        def _():
            acc1[...] = jnp.zeros_like(acc1)
            acc2[...] = jnp.zeros_like(acc2)

        acc1[...] += _mm_tn(u_ref[...], da_ref[...])
        acc2[...] += _mm_tn(act_ref[...], df_ref[...])

        @pl.when(i == steps - 1)
        def _():
            for p in range(per):
                c = p * FF_HALF
                dw1_ref[p] = acc1[:, c:c + FF_HALF].astype(BF16)
                dw2_ref[p] = acc2[c:c + FF_HALF, :].astype(BF16)

    wide = pl.BlockSpec((tb, D_MODEL), lambda j, i: (i, 0))
    chunk = pl.BlockSpec((tb, FF_COLS), lambda j, i: (i, j))
    return _hosting_call(
        body, "ffn_bwd_weights", (D_FF // FF_COLS, steps),
        [wide, chunk, chunk, wide],
        [pl.BlockSpec((None, per, D_MODEL, FF_HALF), lambda j, i: (j // 2, j % 2, 0, 0)),
         pl.BlockSpec((per, FF_HALF, D_MODEL), lambda j, i: (j % 2, j // 2, 0))],
        [jax.ShapeDtypeStruct((2, N_DEV, D_MODEL, FF_HALF), BF16), jax.ShapeDtypeStruct((N_DEV, FF_CHUNK, D_MODEL), BF16)],
        [pltpu.VMEM((D_MODEL, FF_COLS), F32), pltpu.VMEM((FF_COLS, D_MODEL), F32)],
        (u2, da, act, df), carried, modes)


def _out_proj_bwd(dh1, mix, g2, w_out_t, attn, rec, carried, modes):
    rows = dh1.shape[0]
    tm = _row_tile(rows)
    steps = rows // tm

    def body(dh_ref, mix_ref, g_ref, w_ref, attn_ref, rec_ref, dattn_ref, drec_ref, dw_ref, dg_ref, acc):
        i = pl.program_id(0)

        @pl.when(i == 0)
        def _():
            acc[...] = jnp.zeros_like(acc)
            dg_ref[...] = jnp.zeros_like(dg_ref)

        g = g_ref[...]
        _, xhat, rstd = _rms_fwd(mix_ref[...], g)
        dmix, dg = _rms_bwd(dh_ref[...], xhat, rstd, g)
        dmix = dmix.astype(BF16)
        dg_ref[...] += dg
        din = _mm(dmix, w_ref[...])
        dattn_ref[...] = din[:, :ATTN_WIDTH].astype(BF16)
        drec_ref[...] = din[:, ATTN_WIDTH:]
        acc[0:ATTN_WIDTH, :] += _mm_tn(attn_ref[...], dmix)
        acc[ATTN_WIDTH:, :] += _mm_tn(rec_ref[...], dmix)

        @pl.when(i == steps - 1)
        def _():
            dw_ref[...] = acc[...].astype(BF16)

    half = pl.BlockSpec((tm, ATTN_WIDTH), lambda i: (i, 0))
    wide = pl.BlockSpec((tm, D_MODEL), lambda i: (i, 0))
    return _hosting_call(
        body, "out_proj_bwd", steps,
        [wide, wide, _full((1, D_MODEL)), _resident((D_MODEL, D_MODEL)), half, half],
        [half, half, _full((D_MODEL, D_MODEL)), _full((1, D_MODEL))],
        [jax.ShapeDtypeStruct((rows, ATTN_WIDTH), BF16), jax.ShapeDtypeStruct((rows, LRU_WIDTH), F32),
         jax.ShapeDtypeStruct((D_MODEL, D_MODEL), BF16), jax.ShapeDtypeStruct((1, D_MODEL), F32)],
        [pltpu.VMEM((D_MODEL, D_MODEL), F32)],
        (dh1, mix, g2, w_out_t, attn, rec), carried, modes)


def _attn_bwd(qkv, dattn, sinks, bias, carried, modes):
    rows = qkv.shape[0]
    tm = _row_tile(rows)
    nbt, nt = tm // BLOCK, rows // tm

    def body(sink_ref, bias_ref, do_ref, q_ref, kp_ref, kc_ref, vp_ref, vc_ref, dq_ref, dkv_ref, dsink_ref, dk_c, dv_c):
        i = pl.program_id(0)

        @pl.when(i == 0)
        def _():
            dk_c[...] = jnp.zeros_like(dk_c)
            dv_c[...] = jnp.zeros_like(dv_c)
            dsink_ref[...] = jnp.zeros_like(dsink_ref)

        @pl.when(i < nt)
        def _():
            dk_late, dv_late = dk_c[...], dv_c[...]
            dsink_rows = [jnp.zeros((1, LANES), F32)] * ATTN_HEADS
            for b in range(nbt):
                blk = slice(b * BLOCK, (b + 1) * BLOCK)
                bias_t = _bias_of_block(bias_ref, i * nbt + b)
                dq_parts, dk_parts, dv_parts = [], [], []
                for kv in range(KV_HEADS):
                    k2 = _keys_of_block(kp_ref, kc_ref, b, kv)
                    v2 = _keys_of_block(vp_ref, vc_ref, b, kv)
                    q4 = _heads(q_ref, blk, kv * GQA_GROUP, GQA_GROUP)
                    do4 = _heads(do_ref, blk, kv * GQA_GROUP, GQA_GROUP)
                    pn, psink = _attn_probs(k2, q4, bias_t, _sink_row(sink_ref, kv))
                    dpn = _mm_nt(v2, do4)
                    delta = jnp.sum(pn * dpn, axis=0, keepdims=True)
                    ds = ((pn * (dpn - delta)) * (HEAD_DIM ** -0.5)).astype(BF16)
                    dqt = _mm_tn(k2, ds)
                    dq_parts += [dqt[:, g * BLOCK:(g + 1) * BLOCK] for g in range(GQA_GROUP)]
                    dk_parts.append(_mm(ds, q4))
                    dv_parts.append(_mm(pn.astype(BF16), do4))
                    sd = psink * delta
                    for g in range(GQA_GROUP):
                        h = kv * GQA_GROUP + g
                        dsink_rows[h] = dsink_rows[h] - jnp.sum(sd[:, g * BLOCK:(g + 1) * BLOCK])
                dq_ref[blk, :] = _from_head_major(dq_parts).astype(BF16)
                dk2 = jnp.concatenate(dk_parts, axis=1)
                dv2 = jnp.concatenate(dv_parts, axis=1)
                dkv_ref[blk, 0:KV_WIDTH] = (dk_late + dk2[0:BLOCK]).astype(BF16)
                dkv_ref[blk, KV_WIDTH:] = (dv_late + dv2[0:BLOCK]).astype(BF16)
                dk_late, dv_late = dk2[BLOCK:], dv2[BLOCK:]
            dk_c[...] = dk_late
            dv_c[...] = dv_late
            dsink_ref[...] += jnp.concatenate(dsink_rows, axis=0)

        @pl.when(i == nt)
        def _():
            dkv_ref[...] = jnp.zeros_like(dkv_ref)
            dkv_ref[0:BLOCK, 0:KV_WIDTH] = dk_c[...].astype(BF16)
            dkv_ref[0:BLOCK, KV_WIDTH:] = dv_c[...].astype(BF16)

    tile_of = lambda i: jnp.minimum(i, nt - 1)
    tile = pl.BlockSpec((tm, ATTN_WIDTH), lambda i: (tile_of(i), 0))
    return _hosting_call(
        body, "attn_bwd", nt + 1,
        [pl.BlockSpec(memory_space=pltpu.SMEM), _resident((N_BIAS, 2 * BLOCK, GQA_GROUP * BLOCK)), tile]
        + _attn_specs(tm, tile_of),
        [tile, pl.BlockSpec((tm, 2 * KV_WIDTH), lambda i: (i, 0)), _full((ATTN_HEADS, LANES))],
        [jax.ShapeDtypeStruct((rows, ATTN_WIDTH), BF16), jax.ShapeDtypeStruct((rows + tm, 2 * KV_WIDTH), BF16),
         jax.ShapeDtypeStruct((ATTN_HEADS, LANES), F32)],
        [pltpu.VMEM((BLOCK, KV_WIDTH), F32), pltpu.VMEM((BLOCK, KV_WIDTH), F32)],
        (sinks, bias, dattn, qkv, qkv, qkv, qkv, qkv), carried, modes)


ROW_CONV_B, ROW_B_A, ROW_B_X, ROW_LAMBDA = 4, 5, 6, 7


def _rec_bwd(drec, zrec, h, kept, conv_w, wa_bd, wx_bd, lam, carried, modes):
    rows = zrec.shape[0]
    tm = _rec_tile(rows)
    nt = rows // tm
    per = tm // SUBLANES

    def body(drec_ref, xr_ref, yr_ref, h_ref, xc_ref, a_ref, mult_ref, r_ref, ig_ref, hhalo_ref, cw_ref, wa_ref, wx_ref,
             lam_ref, drz_ref, small_ref, dwa_ref, dwx_ref, hbuf, dbuf, dgr_s, dgi_s, dyr_s, carry):
        s = pl.program_id(0)
        i = nt - 1 - s

        @pl.when(s == 0)
        def _():
            small_ref[...] = jnp.zeros_like(small_ref)
            dwa_ref[...] = jnp.zeros_like(dwa_ref)
            dwx_ref[...] = jnp.zeros_like(dwx_ref)
            carry[...] = jnp.zeros_like(carry)
            dbuf[tm:tm + SUBLANES, :] = jnp.zeros((SUBLANES, LRU_WIDTH), F32)

        hbuf[0:SUBLANES, :] = jnp.where(i == 0, 0.0, hhalo_ref[...])
        hbuf[SUBLANES:SUBLANES + tm, :] = h_ref[...]

        row = lax.broadcasted_iota(jnp.int32, (SUBLANES, LRU_WIDTH), 0)
        log_a_scale = (-LRU_C) * _softplus(-lam_ref[...])
        zeros = jnp.zeros((SUBLANES, LRU_WIDTH), F32)

        def group(k, state):
            g_later, a_later, sum_dgr, sum_dgi, sum_lam = state
            o = pl.multiple_of((per - 1 - k) * SUBLANES, SUBLANES)
            rows8 = pl.ds(o, SUBLANES)
            yr, drec_t, h_t, a = yr_ref[rows8, :], drec_ref[rows8, :], h_ref[rows8, :], a_ref[rows8, :]
            gel, t = _gelu(yr)
            dyr_s[rows8, :] = drec_t * h_t * _gelu_grad(yr, t)
            u = drec_t * gel
            coef = jnp.where(row == SUBLANES - 1, a_later, pltpu.roll(a, SUBLANES - 1, 0))
            for sft in (1, 2, 4):
                keep = row < SUBLANES - sft
                u = jnp.where(keep, coef * pltpu.roll(u, SUBLANES - sft, 0) + u, u)
                coef = jnp.where(keep, coef * pltpu.roll(coef, SUBLANES - sft, 0), coef)
            g = coef * g_later + u
            du = jnp.where(i * tm + o + row >= PAD_ROWS, g, 0.0)
            h_before = jnp.where(row == 0, hbuf[rows8, :][SUBLANES - 1:SUBLANES, :], pltpu.roll(h_t, 1, 0))
            xc, mult, r, ig = xc_ref[rows8, :], mult_ref[rows8, :], r_ref[rows8, :], ig_ref[rows8, :]
            dbuf[rows8, :] = du * (mult * ig)
            dgi = (du * (mult * xc)) * (ig * (1.0 - ig))
            dgi_s[rows8, :] = dgi
            dlog_a = (g * h_before) * a - (du * (ig * xc)) * (a * a * pl.reciprocal(mult, approx=True))
            dgr = (dlog_a * log_a_scale) * (r * (1.0 - r))
            dgr_s[rows8, :] = dgr
            return g[0:1, :], a[0:1, :], sum_dgr + dgr, sum_dgi + dgi, sum_lam + dlog_a * r

        state = lax.fori_loop(0, per, group, (carry[0:1, :], carry[1:2, :], zeros, zeros, zeros), unroll=2)
        carry[0:1, :], carry[1:2, :] = state[0], state[1]
        sum_dgr, sum_dgi, sum_lam = (jnp.sum(v, axis=0, keepdims=True) for v in state[2:])
        dlam = sum_lam * (LRU_C * _sigmoid(-lam_ref[...]))

        dgr_b = [dgr_s[:, hh * LRU_HALF:(hh + 1) * LRU_HALF].astype(BF16) for hh in range(2)]
        dgi_b = [dgi_s[:, hh * LRU_HALF:(hh + 1) * LRU_HALF].astype(BF16) for hh in range(2)]
        halves = _lru_halves(xc_ref[...])
        for hh in range(2):
            dwa_ref[hh] += _mm_tn(halves[hh], dgr_b[hh])
            dwx_ref[hh] += _mm_tn(halves[hh], dgi_b[hh])
        dxc = dbuf[0:tm, :] + jnp.concatenate(
            [_mm_nt(dgr_b[hh], wa_ref[hh]) + _mm_nt(dgi_b[hh], wx_ref[hh]) for hh in range(2)], axis=1)

        dbuf[0:tm, :] = dxc
        sum_dxc = jnp.sum(dxc, axis=0, keepdims=True)
        ahead = [dbuf[pl.ds(CONV_WIDTH - 1 - j, tm), :] for j in range(CONV_WIDTH)]
        drz_ref[:, 0:LRU_WIDTH] = sum(cw_ref[j:j + 1, :] * ahead[j] for j in range(CONV_WIDTH)).astype(BF16)
        drz_ref[:, LRU_WIDTH:] = dyr_s[...].astype(BF16)
        upd = [jnp.sum(xr_ref[...] * ahead[j], axis=0, keepdims=True) for j in range(CONV_WIDTH)]
        dbuf[tm:tm + SUBLANES, :] = dbuf[0:SUBLANES, :]
        small_ref[...] += jnp.concatenate(upd + [sum_dxc, sum_dgr, sum_dgi, dlam], axis=0)

    rev = lambda s: nt - 1 - s
    halo = lambda s: jnp.maximum(rev(s) * per - 1, 0)
    cols = lambda k: pl.BlockSpec((tm, LRU_WIDTH), lambda s: (rev(s), k))
    halo0 = pl.BlockSpec((SUBLANES, LRU_WIDTH), lambda s: (halo(s), 0))
    bd = _full((2, LRU_HALF, LRU_HALF))
    big = pltpu.VMEM((tm + SUBLANES, LRU_WIDTH), F32)
    tile = pltpu.VMEM((tm, LRU_WIDTH), F32)
    kept_cols = [cols(k) for k in (KEPT_XC, KEPT_A, KEPT_MULT, KEPT_R, KEPT_I)]
    return _hosting_call(
        body, "rec_bwd", nt,
        [cols(0), cols(0), cols(1), cols(0)] + kept_cols
        + [halo0, _full((CONV_WIDTH, LRU_WIDTH)), bd, bd, _full((1, LRU_WIDTH))],
        [pl.BlockSpec((tm, 2 * LRU_WIDTH), lambda s: (rev(s), 0)), _full((SUBLANES, LRU_WIDTH)), bd, bd],
        [jax.ShapeDtypeStruct((rows, 2 * LRU_WIDTH), BF16), jax.ShapeDtypeStruct((SUBLANES, LRU_WIDTH), F32),
         jax.ShapeDtypeStruct((2, LRU_HALF, LRU_HALF), F32), jax.ShapeDtypeStruct((2, LRU_HALF, LRU_HALF), F32)],
        [big, big, tile, tile, tile, pltpu.VMEM((SUBLANES, LRU_WIDTH), F32)],
        (drec, zrec, zrec, h) + (kept,) * N_KEPT + (h, conv_w, wa_bd, wx_bd, lam), carried, modes)


DZ_CUTS = (0, ATTN_WIDTH, QKV_WIDTH, IN_WIDTH)


def _dz_specs(tm):
    return [pl.BlockSpec((tm, DZ_CUTS[p + 1] - DZ_CUTS[p]), lambda i: (i, 0)) for p in range(3)]


def _in_proj_bwd_x(head, x, g1, dh1, dq, dkv, drz, w_in_t, carried, modes):
    rows = dh1.shape[0]
    tm = _row_tile(rows)
    steps = rows // tm

    def body(head_ref, g_ref, dh1_ref, dq_ref, dkv_ref, drz_ref, w_ref, x_hbm, dh0_ref, dg_ref, buf, sem):
        i = pl.program_id(0)
        h0 = _h0_tile(head_ref, x_hbm, buf, sem, i, steps, tm)

        @pl.when(i == 0)
        def _():
            dg_ref[...] = jnp.zeros_like(dg_ref)

        g = g_ref[...]
        _, xhat, rstd = _rms_fwd(h0, g)
        parts = (dq_ref[...], dkv_ref[...], drz_ref[...])
        du = sum(_mm(parts[p], w_ref[DZ_CUTS[p]:DZ_CUTS[p + 1], :]) for p in range(3))
        dx, dg = _rms_bwd(du, xhat, rstd, g)
        dh0_ref[...] = dh1_ref[...] + dx
        dg_ref[...] += dg

    wide = pl.BlockSpec((tm, D_MODEL), lambda i: (i, 0))
    return _hosting_call(
        body, "in_proj_bwd_x", steps,
        [_full((BLOCK, D_MODEL)), _full((1, D_MODEL)), wide] + _dz_specs(tm) + [_resident((IN_WIDTH, D_MODEL)), ANY_SPACE],
        [wide, _full((1, D_MODEL))],
        [jax.ShapeDtypeStruct((rows, D_MODEL), F32), jax.ShapeDtypeStruct((1, D_MODEL), F32)],
        _frame_scratch(tm), (head, g1, dh1, dq, dkv, drz, w_in_t, x), carried, modes)


def _in_proj_bwd_w(u1, dq, dkv, drz, carried, modes):
    rows = u1.shape[0]
    tb = _big_tile(rows)
    steps = rows // tb

    def body(u_ref, dq_ref, dkv_ref, drz_ref, dw_ref, acc):
        i = pl.program_id(0)

        @pl.when(i == 0)
        def _():
            acc[...] = jnp.zeros_like(acc)

        u = u_ref[...]
        for p, ref in enumerate((dq_ref, dkv_ref, drz_ref)):
            acc[:, DZ_CUTS[p]:DZ_CUTS[p + 1]] += _mm_tn(u, ref[...])

        @pl.when(i == steps - 1)
        def _():
            dw_ref[...] = acc[...].astype(BF16)

    return _hosting_call(
        body, "in_proj_bwd_w", steps,
        [pl.BlockSpec((tb, D_MODEL), lambda i: (i, 0))] + _dz_specs(tb),
        [_full((D_MODEL, IN_WIDTH))],
        [jax.ShapeDtypeStruct((D_MODEL, IN_WIDTH), BF16)],
        [pltpu.VMEM((D_MODEL, IN_WIDTH), F32)], (u1, dq, dkv, drz), carried, modes)


def _adamw_math(w, m, v, g):
    nm = ADAM_B1 * m + (1.0 - ADAM_B1) * g
    nv = ADAM_B2 * v + (1.0 - ADAM_B2) * (g * g)
    m_hat = nm / (1.0 - ADAM_B1 ** ADAM_STEP)
    v_hat = nv / (1.0 - ADAM_B2 ** ADAM_STEP)
    return (-ADAM_LR) * (m_hat / (jnp.sqrt(v_hat) + ADAM_EPS) + ADAM_WD * w), nm, nv


SMALL_NAMES = ("conv_b", "b_a", "b_x", "lru_lambda", "attn_sinks", "g_post_mix", "g_pre_ffn", "g_post_ffn")
PACK_WIDTH = 1024


def _pack_rows(vals):
    assert len(SMALL_NAMES) == SUBLANES
    row = lax.broadcasted_iota(jnp.int32, (SUBLANES, PACK_WIDTH), 0)
    tile = jnp.zeros((SUBLANES, PACK_WIDTH), F32)
    for k, name in enumerate(SMALL_NAMES):
        a = vals[name].reshape(1, -1)
        tile = jnp.where(row == k, jnp.pad(a, ((0, 0), (0, PACK_WIDTH - a.shape[1]))), tile)
    return tile


def _adamw_small(weights, mom_m, mom_v, parts, loss_parts):
    n = len(SMALL_NAMES)
    views = [(1, weights[name].size) for name in SMALL_NAMES]

    def body(*refs):
        w_refs, m_refs, v_refs = refs[:n], refs[n:2 * n], refs[2 * n:3 * n]
        p_ref, l_ref, loss_ref = refs[3 * n], refs[3 * n + 1], refs[3 * n + 2]
        outs = refs[3 * n + 3:]
        for k, (_, c) in enumerate(views):
            g = p_ref[0, k:k + 1, 0:c]
            for s in range(1, N_DEV):
                g = g + p_ref[s, k:k + 1, 0:c]
            g_ref, d_ref, nm_ref, nv_ref = outs[4 * k:4 * k + 4]
            g_ref[...] = g
            d_ref[...], nm_ref[...], nv_ref[...] = _adamw_math(w_refs[k][...], m_refs[k][...], v_refs[k][...], g)
        total = l_ref[0]
        for s in range(1, N_DEV):
            total = total + l_ref[s]
        loss_ref[...] = total

    args = [src[name].reshape(view) for src in (weights, mom_m, mom_v) for name, view in zip(SMALL_NAMES, views)]
    res = pl.pallas_call(
        body, name="adamw_small",
        out_shape=[jax.ShapeDtypeStruct(loss_parts.shape[1:], F32)]
                  + [jax.ShapeDtypeStruct(view, F32) for view in views for _ in range(4)],
        compiler_params=pltpu.CompilerParams(vmem_limit_bytes=VMEM_LIMIT),
    )(*args, parts, loss_parts)
    out = {name: tuple(t.reshape(weights[name].shape) for t in res[1 + 4 * k:5 + 4 * k]) for k, name in enumerate(SMALL_NAMES)}
    return res[0], out


def _adamw(w, m, v, parts, name):
    rows, cols = w.shape
    tr = next((t for t in (256, 128) if rows % t == 0), rows)
    parts = parts if isinstance(parts, (list, tuple)) else [parts]

    def body(w_ref, m_ref, v_ref, *refs):
        p_refs, (g_ref, d_ref, nm_ref, nv_ref) = refs[:len(parts)], refs[len(parts):]

        def total(p_ref):
            g = p_ref[0].astype(F32)
            for s in range(1, N_DEV):
                g = g + p_ref[s].astype(F32)
            return g

        g = jnp.concatenate([total(p_ref) for p_ref in p_refs], axis=1) if len(parts) > 1 else total(p_refs[0])
        g_ref[...] = g
        d_ref[...], nm_ref[...], nv_ref[...] = _adamw_math(w_ref[...], m_ref[...], v_ref[...], g)

    blk = pl.BlockSpec((tr, cols), lambda i: (i, 0))
    return pl.pallas_call(
        body, name=name, grid=(rows // tr,),
        in_specs=[blk, blk, blk] + [pl.BlockSpec((N_DEV, tr, p.shape[2]), lambda i: (0, i, 0)) for p in parts],
        out_specs=[blk] * 4,
        out_shape=[jax.ShapeDtypeStruct((rows, cols), F32)] * 4,
        compiler_params=_params(("parallel",)),
    )(w, m, v, *parts)


def _cols_from_shards(g):
    return jnp.transpose(g, (1, 0, 2)).reshape(g.shape[1], N_DEV * g.shape[2])


def _cols_to_shards(a):
    r, c = a.shape
    return jnp.transpose(a.reshape(r, N_DEV, c // N_DEV), (1, 0, 2))


def _block_diag(w):
    per = LRU_HALF // LRU_BLOCK
    w = w.reshape(2, per, LRU_BLOCK, LRU_BLOCK)
    eye = jnp.eye(per, dtype=w.dtype)
    return (w[:, :, :, None, :] * eye[None, :, None, :, None]).reshape(2, LRU_HALF, LRU_HALF)


def _block_diag_extract(t):
    per = LRU_HALF // LRU_BLOCK
    t = t.reshape(2, per, LRU_BLOCK, per, LRU_BLOCK)
    return jnp.stack([t[:, b, :, b, :] for b in range(per)], axis=1).reshape(LRU_BLOCKS, LRU_BLOCK, LRU_BLOCK)


def kernel(x, meta_tokens, g_pre_mix, w_in, conv_w, conv_b, w_a, b_a, w_x, b_x, lru_lambda, attn_sinks, w_out, g_post_mix, g_pre_ffn, w_ff1, w_ff2, g_post_ffn, loss_target, m_meta_tokens, m_g_pre_mix, m_w_in, m_conv_w, m_conv_b, m_w_a, m_b_a, m_w_x, m_b_x, m_lru_lambda, m_attn_sinks, m_w_out, m_g_post_mix, m_g_pre_ffn, m_w_ff1, m_w_ff2, m_g_post_ffn, v_meta_tokens, v_g_pre_mix, v_w_in, v_conv_w, v_conv_b, v_w_a, v_b_a, v_w_x, v_b_x, v_lru_lambda, v_attn_sinks, v_w_out, v_g_post_mix, v_g_pre_ffn, v_w_ff1, v_w_ff2, v_g_post_ffn):
    weights = dict(meta_tokens=meta_tokens, g_pre_mix=g_pre_mix, w_in=w_in, conv_w=conv_w, conv_b=conv_b, w_a=w_a,
                   b_a=b_a, w_x=w_x, b_x=b_x, lru_lambda=lru_lambda, attn_sinks=attn_sinks, w_out=w_out,
                   g_post_mix=g_post_mix, g_pre_ffn=g_pre_ffn, w_ff1=w_ff1, w_ff2=w_ff2, g_post_ffn=g_post_ffn)
    mom_m = dict(meta_tokens=m_meta_tokens, g_pre_mix=m_g_pre_mix, w_in=m_w_in, conv_w=m_conv_w, conv_b=m_conv_b,
                 w_a=m_w_a, b_a=m_b_a, w_x=m_w_x, b_x=m_b_x, lru_lambda=m_lru_lambda, attn_sinks=m_attn_sinks,
                 w_out=m_w_out, g_post_mix=m_g_post_mix, g_pre_ffn=m_g_pre_ffn, w_ff1=m_w_ff1, w_ff2=m_w_ff2,
                 g_post_ffn=m_g_post_ffn)
    mom_v = dict(meta_tokens=v_meta_tokens, g_pre_mix=v_g_pre_mix, w_in=v_w_in, conv_w=v_conv_w, conv_b=v_conv_b,
                 w_a=v_w_a, b_a=v_b_a, w_x=v_w_x, b_x=v_b_x, lru_lambda=v_lru_lambda, attn_sinks=v_attn_sinks,
                 w_out=v_w_out, g_post_mix=v_g_post_mix, g_pre_ffn=v_g_pre_ffn, w_ff1=v_w_ff1, w_ff2=v_w_ff2,
                 g_post_ffn=v_g_post_ffn)
    order = list(weights)

    (g_win, g_meta, g_cw) = _gather_two_level([w_in[0].astype(BF16), meta_tokens, conv_w[0]], "gather_first")
    w_in_full = _cols_from_shards(g_win)
    meta_full = _cols_from_shards(g_meta)
    conv_w_full = _cols_from_shards(g_cw)

    head = jnp.concatenate([jnp.zeros((PAD_ROWS, D_MODEL), F32), meta_full], axis=0)
    wa_bd = _block_diag(w_a[0]).astype(BF16)
    wx_bd = _block_diag(w_x[0]).astype(BF16)
    bias = _attn_bias()

    w1_shard = w_ff1[0].astype(BF16)
    (qkv, zrec, u1), (g_wout,) = _in_proj_fwd(head, x[0], g_pre_mix, w_in_full, [w_out[0].astype(BF16)], ["gather"])
    (attn,), (w1a,) = _attn_fwd(qkv, attn_sinks, bias, [w1_shard[:, :FF_HALF]], ["gather"])
    (rec, h_lru, kept), (w1b,) = _rec_fwd(zrec, conv_w_full, conv_b, wa_bd, b_a, wx_bd, b_x, lru_lambda,
                                         [w1_shard[:, FF_HALF:]], ["gather"])
    w_out_full = g_wout.reshape(D_MODEL, D_MODEL)
    w2_shard = w_ff2[0].astype(BF16)
    (mix, h1), (w2a,) = _out_proj_fwd(attn, rec, w_out_full, head, x[0], g_post_mix, [w2_shard[:FF_HALF]], ["gather"])
    (act, u2), (w2b,) = _ffn_up(h1, g_pre_ffn, (w1a, w1b), [w2_shard[FF_HALF:]], ["gather"])
    w2_halves = [w.reshape(D_FF // 2, D_MODEL) for w in (w2a, w2b)]
    (dy, df, dg_post_ffn, loss_acc), w2t_halves = _ffn_down_loss(
        act, w2_halves, h1, loss_target[0], g_post_ffn, [w2_shard[:FF_HALF].T, w2_shard[FF_HALF:].T], ["gather"] * 2)

    (da1,), (w1ta,) = _ffn_bwd_act(df, w2t_halves, act, [w1_shard[:, :FF_HALF].T], ["gather"])
    (dw1h, dw2g), (w1tb,) = _ffn_bwd_weights(u2, da1, act, df, [w1_shard[:, FF_HALF:].T], ["gather"])
    w1t_halves = [w.reshape(D_FF // 2, D_MODEL) for w in (w1ta, w1tb)]
    (dh1, dg_pre_ffn), (p_w1a,) = _ffn_bwd_x(da1, w1t_halves, h1, dy, g_pre_ffn, [dw1h[0]], ["scatter"])
    (dattn, drec, dw_out, dg_post_mix), (p_w1b,) = _out_proj_bwd(dh1, mix, g_post_mix, w_out_full.T, attn, rec,
                                                                [dw1h[1]], ["scatter"])
    (dq, dkv_late, dsinks), (p_w2,) = _attn_bwd(qkv, dattn, attn_sinks, bias, [dw2g], ["scatter"])
    dkv = dkv_late[BLOCK:BLOCK + qkv.shape[0]]
    (drz, rec_small, dwa_bd, dwx_bd), (p_wout,) = _rec_bwd(
        drec, zrec, h_lru, kept, conv_w_full, wa_bd, wx_bd, lru_lambda,
        [dw_out.reshape(N_DEV, D_MODEL // N_DEV, D_MODEL)], ["scatter"])
    small_grads = dict(
        conv_b=rec_small[ROW_CONV_B], b_a=rec_small[ROW_B_A], b_x=rec_small[ROW_B_X], lru_lambda=rec_small[ROW_LAMBDA],
        attn_sinks=dsinks[:, 0], g_post_mix=dg_post_mix, g_pre_ffn=dg_pre_ffn, g_post_ffn=dg_post_ffn)
    gate_rows = (LRU_BLOCKS * LRU_BLOCK, LRU_BLOCK)
    gate_dense = (LRU_BLOCKS * LRU_BLOCK * LRU_BLOCK // PACK_WIDTH, PACK_WIDTH)
    (dw_in,), (p_cw, p_small, p_wa, p_wx) = _in_proj_bwd_w(
        u1, dq, dkv, drz,
        [_cols_to_shards(rec_small[0:CONV_WIDTH]), _pack_rows(small_grads),
         _block_diag_extract(dwa_bd).reshape(gate_dense), _block_diag_extract(dwx_bd).reshape(gate_dense)],
        ["scatter", "gather", "gather", "gather"])
    p_wa, p_wx = (p.reshape((N_DEV,) + gate_rows) for p in (p_wa, p_wx))
    (dh0, dg_pre_mix), (p_win,) = _in_proj_bwd_x(
        head, x[0], g_pre_mix, dh1, dq, dkv, drz, w_in_full.T, [_cols_to_shards(dw_in)], ["scatter"])
    p_meta, p_gpm, p_loss = _exchange([_cols_to_shards(dh0[PAD_ROWS:BLOCK]), dg_pre_mix, loss_acc],
                                      ["scatter", "gather", "gather"], "exchange_last")

    res = {}
    res["g_pre_mix"] = _adamw(g_pre_mix, m_g_pre_mix, v_g_pre_mix, p_gpm, "adamw_g_pre_mix")
    res["w_in"] = _adamw(w_in[0], m_w_in[0], v_w_in[0], p_win, "adamw_w_in")
    res["w_out"] = _adamw(w_out[0], m_w_out[0], v_w_out[0], p_wout, "adamw_w_out")
    res["w_ff1"] = _adamw(w_ff1[0], m_w_ff1[0], v_w_ff1[0], [p_w1a, p_w1b], "adamw_w_ff1")
    res["w_ff2"] = _adamw(w_ff2[0], m_w_ff2[0], v_w_ff2[0], p_w2, "adamw_w_ff2")
    res["meta_tokens"] = _adamw(meta_tokens, m_meta_tokens, v_meta_tokens, p_meta, "adamw_meta")
    res["conv_w"] = _adamw(conv_w[0], m_conv_w[0], v_conv_w[0], p_cw, "adamw_conv_w")
    for name in ("w_in", "w_out", "w_ff1", "w_ff2", "conv_w"):
        res[name] = tuple(t[None] for t in res[name])
    for name, parts in (("w_a", p_wa), ("w_x", p_wx)):
        gate = _adamw(*(src[name].reshape(gate_rows) for src in (weights, mom_m, mom_v)), parts, "adamw_" + name)
        res[name] = tuple(t.reshape(weights[name].shape) for t in gate)
    loss_total, small = _adamw_small(weights, mom_m, mom_v, p_small, p_loss)
    res.update(small)

    grad_x = dh0[BLOCK:][None]
    outs = [loss_total[0, 0], grad_x]
    for k in range(4):
        outs += [res[name][k] for name in order]
    return tuple(outs)
```

```python
import jax
import jax.numpy as jnp
import numpy as np
from jax import lax
from jax.experimental import pallas as pl
from jax.experimental.pallas import tpu as pltpu

F32 = jnp.float32
BF16 = jnp.bfloat16

D_MODEL = 1024
N_META = 16
HEAD_DIM = 64
ATTN_HEADS = 8
KV_HEADS = 2
GQA_GROUP = ATTN_HEADS // KV_HEADS
ATTN_WIDTH = ATTN_HEADS * HEAD_DIM
KV_WIDTH = KV_HEADS * HEAD_DIM
QKV_WIDTH = ATTN_WIDTH + 2 * KV_WIDTH
LRU_WIDTH = 512
LRU_BLOCKS = 8
LRU_BLOCK = 64
LRU_HALF = 256
LRU_C = 8.0
CONV_WIDTH = 4
BLOCK = 128
PAD_ROWS = BLOCK - N_META
IN_WIDTH = QKV_WIDTH + 2 * LRU_WIDTH
D_FF = 4096
EPS = 1e-6
NEG = -1e30
N_DEV = 8
FF_CHUNK = D_FF // N_DEV
SUBLANES = 8
LANES = 128

ADAM_LR = 0.001
ADAM_B1 = 0.9
ADAM_B2 = 0.999
ADAM_EPS = 1e-08
ADAM_WD = 0.01
ADAM_STEP = 10

VMEM_LIMIT = 56 * 1024 * 1024


def _row_tile(rows):
    for t in (640, 512, 256, 128):
        if rows % t == 0:
            return t
    raise ValueError(rows)


def _big_tile(rows):
    for t in (1664, 1024, 512, 256, 128):
        if rows % t == 0:
            return t
    raise ValueError(rows)


def _rec_tile(rows):
    for t in (416, 256, 128):
        if rows % t == 0:
            return t
    raise ValueError(rows)


def _params(semantics):
    return pltpu.CompilerParams(dimension_semantics=semantics, vmem_limit_bytes=VMEM_LIMIT)


def _mm(a, b):
    return lax.dot_general(a, b, (((1,), (0,)), ((), ())), preferred_element_type=F32)


def _mm_nt(a, b):
    return lax.dot_general(a, b, (((1,), (1,)), ((), ())), preferred_element_type=F32)


def _mm_tn(a, b):
    return lax.dot_general(a, b, (((0,), (0,)), ((), ())), preferred_element_type=F32)


def _rms_fwd(x, g):
    rstd = lax.rsqrt(jnp.mean(x * x, axis=-1, keepdims=True) + EPS)
    xhat = x * rstd
    return xhat * g, xhat, rstd


def _rms_bwd(dy, xhat, rstd, g):
    dyg = dy * g
    c = jnp.mean(dyg * xhat, axis=-1, keepdims=True)
    dx = rstd * (dyg - xhat * c)
    dg = jnp.sum(dy * xhat, axis=0, keepdims=True)
    return dx, dg


def _sigmoid(x):
    return 0.5 * jnp.tanh(0.5 * x) + 0.5


def _log1p(x):
    u = 1.0 + x
    return jnp.where(u == 1.0, x, jnp.log(u) * x / (u - 1.0))


def _one_minus_sq_exp(x, ex):
    return -jnp.tanh(x) * (1.0 + ex * ex)


TINY = 1e-30


def _sqrt_pos(y):
    r = lax.rsqrt(jnp.maximum(y, TINY))
    return y * r, r


def _softplus(x):
    return jnp.maximum(x, 0.0) + _log1p(jnp.exp(-jnp.abs(x)))


GELU_C = 0.7978845608028654
GELU_K = 0.044715


def _gelu(x):
    t = jnp.tanh(GELU_C * (x + GELU_K * x * x * x))
    return 0.5 * x * (1.0 + t), t


def _gelu_grad(x, t):
    return 0.5 * (1.0 + t) + 0.5 * x * (1.0 - t * t) * GELU_C * (1.0 + 3.0 * GELU_K * x * x)


def _full(shape):
    return pl.BlockSpec(shape, lambda *_: (0,) * len(shape))


def _resident(shape):
    return pl.BlockSpec(shape, lambda *_: (0,) * len(shape), pipeline_mode=pl.Buffered(1))


def _exchange_copies(ins, outs, sems, modes):
    send_sems, recv_sems, local_sems = sems
    x, y, c = lax.axis_index("x"), lax.axis_index("y"), lax.axis_index("c")
    me = 4 * x + 2 * y + c

    def block(a, dev):
        return ins[a] if modes[a] == "gather" else ins[a].at[dev]

    local = [pltpu.make_async_copy(block(a, me), outs[a].at[me], local_sems.at[a]) for a in range(len(ins))]
    sends, recvs = [], []
    for a in range(len(ins)):
        for k in range(N_DEV - 1):
            bits = k + 1
            px = jnp.bitwise_xor(x, (bits >> 2) & 1)
            py = jnp.bitwise_xor(y, (bits >> 1) & 1)
            pc = jnp.bitwise_xor(c, bits & 1)
            peer = 4 * px + 2 * py + pc
            common = dict(src_ref=block(a, peer), send_sem=send_sems.at[a, k], recv_sem=recv_sems.at[a, k],
                          device_id=(px, py, pc), device_id_type=pl.DeviceIdType.MESH)
            sends.append(pltpu.make_async_remote_copy(dst_ref=outs[a].at[me], **common))
            recvs.append(pltpu.make_async_remote_copy(dst_ref=outs[a].at[peer], **common))
    return local, sends, recvs


def _exchange_start(ins, outs, sems, modes):
    local, sends, _ = _exchange_copies(ins, outs, sems, modes)
    for cp in local + sends:
        cp.start()


def _exchange_wait(ins, outs, sems, modes):
    local, sends, recvs = _exchange_copies(ins, outs, sems, modes)
    for cp in recvs:
        cp.wait_recv()
    for cp in sends:
        cp.wait_send()
    for cp in local:
        cp.wait()


def _exchange_shapes(arrays, modes):
    return [jax.ShapeDtypeStruct((N_DEV,) + a.shape if mode == "gather" else a.shape, a.dtype)
            for a, mode in zip(arrays, modes)]


def _exchange_sems(na):
    return [pltpu.SemaphoreType.DMA((na, N_DEV - 1)), pltpu.SemaphoreType.DMA((na, N_DEV - 1)),
            pltpu.SemaphoreType.DMA((na,))]


ANY_SPACE = pl.BlockSpec(memory_space=pl.ANY)


def _exchange(arrays, modes, name):
    na = len(arrays)

    def body(*refs):
        ins, outs, sems = refs[:na], refs[na:2 * na], refs[2 * na:]
        _exchange_start(ins, outs, sems, modes)
        _exchange_wait(ins, outs, sems, modes)

    return pl.pallas_call(
        body, name=name, out_shape=_exchange_shapes(arrays, modes),
        in_specs=[ANY_SPACE] * na, out_specs=[ANY_SPACE] * na, scratch_shapes=_exchange_sems(na),
        compiler_params=pltpu.CompilerParams(has_side_effects=True),
    )(*arrays)


def _gather_two_level(arrays, name):
    na = len(arrays)

    def body(*refs):
        ins, outs = refs[:na], refs[na:2 * na]
        send_sems, recv_sems, local_sems = refs[2 * na:]
        x, y, c = lax.axis_index("x"), lax.axis_index("y"), lax.axis_index("c")
        me, sibling = (x, y, c), (x, y, 1 - c)
        chips = [(1 - x, y), (x, 1 - y), (1 - x, 1 - y)]

        def copy(a, k, block, to, src=None):
            slot = outs[a].at[4 * block[0] + 2 * block[1] + block[2]]
            return pltpu.make_async_remote_copy(
                src_ref=slot if src is None else src, dst_ref=slot, send_sem=send_sems.at[a, k],
                recv_sem=recv_sems.at[a, k], device_id=to, device_id_type=pl.DeviceIdType.MESH)

        local = [pltpu.make_async_copy(ins[a], outs[a].at[4 * x + 2 * y + c], local_sems.at[a]) for a in range(na)]
        first = []
        for a in range(na):
            first.append(copy(a, 0, me, sibling, src=ins[a]))
            first += [copy(a, 1 + j, me, (*chip, c), src=ins[a]) for j, chip in enumerate(chips)]
        for cp in local + first:
            cp.start()
        passed = []
        for j, chip in enumerate(chips):
            for a in range(na):
                copy(a, 1 + j, (*chip, c), me).wait_recv()
                passed.append(copy(a, 4 + j, (*chip, c), sibling))
                passed[-1].start()
        for a in range(na):
            copy(a, 0, sibling, me).wait_recv()
            for j, chip in enumerate(chips):
                copy(a, 4 + j, (*chip, 1 - c), me).wait_recv()
        for cp in first + passed:
            cp.wait_send()
        for cp in local:
            cp.wait()

    return pl.pallas_call(
        body, name=name, out_shape=_exchange_shapes(arrays, ["gather"] * na),
        in_specs=[ANY_SPACE] * na, out_specs=[ANY_SPACE] * na, scratch_shapes=_exchange_sems(na),
        compiler_params=pltpu.CompilerParams(has_side_effects=True),
    )(*arrays)


def _hosting_call(body, name, steps, in_specs, out_specs, out_shape, scratch_shapes, args, arrays, modes):
    n_in, n_out, n_scr, na = len(in_specs), len(out_specs), len(scratch_shapes), len(arrays)
    grid = steps if isinstance(steps, tuple) else (steps,)

    def hosting_body(*refs):
        cuts = [0]
        for n in (n_in, na, n_out, na, n_scr, 3):
            cuts.append(cuts[-1] + n)
        ins, x_ins, outs, x_outs, scr, sems = (refs[cuts[p]:cuts[p + 1]] for p in range(6))
        first, last = True, True
        for axis, n in enumerate(grid):
            first = first & (pl.program_id(axis) == 0)
            last = last & (pl.program_id(axis) == n - 1)

        @pl.when(first)
        def _():
            _exchange_start(x_ins, x_outs, sems, modes)

        body(*ins, *outs, *scr)

        @pl.when(last)
        def _():
            _exchange_wait(x_ins, x_outs, sems, modes)

    res = pl.pallas_call(
        hosting_body, name=name, grid=grid,
        in_specs=list(in_specs) + [ANY_SPACE] * na, out_specs=list(out_specs) + [ANY_SPACE] * na,
        out_shape=list(out_shape) + _exchange_shapes(arrays, modes),
        scratch_shapes=list(scratch_shapes) + _exchange_sems(na),
        compiler_params=_params(("arbitrary",) * len(grid)),
    )(*args, *arrays)
    return res[:n_out], res[n_out:]


def _frame_rows(src_hbm, buf, sem, i, steps, tm):
    def first():
        return pltpu.make_async_copy(src_hbm.at[pl.ds(0, tm - BLOCK)], buf.at[0, pl.ds(BLOCK, tm - BLOCK)], sem.at[0])

    def later(t, slot):
        return pltpu.make_async_copy(src_hbm.at[pl.ds(pl.multiple_of(t * tm - BLOCK, SUBLANES), tm)], buf.at[slot], sem.at[slot])

    slot = i % 2

    @pl.when(i == 0)
    def _():
        first().start()

    @pl.when(i + 1 < steps)
    def _():
        later(i + 1, 1 - slot).start()

    @pl.when(i == 0)
    def _():
        first().wait()

    @pl.when(i > 0)
    def _():
        later(i, slot).wait()

    return slot


def _frame_scratch(tm):
    return [pltpu.VMEM((2, tm, D_MODEL), F32), pltpu.SemaphoreType.DMA((2,))]


def _h0_tile(head_ref, x_hbm, buf, sem, i, steps, tm):
    slot = _frame_rows(x_hbm, buf, sem, i, steps, tm)

    @pl.when(i == 0)
    def _():
        buf[0, 0:BLOCK, :] = head_ref[...]

    return buf[slot]


def _in_proj_fwd(head, x, g1, w_in, carried, modes):
    rows = BLOCK + x.shape[0]
    tm = _row_tile(rows)
    steps = rows // tm

    def body(head_ref, g_ref, w_ref, x_hbm, qkv_ref, zrec_ref, u_ref, buf, sem):
        h = _h0_tile(head_ref, x_hbm, buf, sem, pl.program_id(0), steps, tm)
        u, _, _ = _rms_fwd(h, g_ref[...])
        u = u.astype(BF16)
        u_ref[...] = u
        z = _mm(u, w_ref[...])
        qkv_ref[...] = z[:, :QKV_WIDTH].astype(BF16)
        zrec_ref[...] = z[:, QKV_WIDTH:]

    wide = pl.BlockSpec((tm, D_MODEL), lambda i: (i, 0))
    return _hosting_call(
        body, "in_proj_fwd", steps,
        [_full((BLOCK, D_MODEL)), _full((1, D_MODEL)), _resident((D_MODEL, IN_WIDTH)), ANY_SPACE],
        [pl.BlockSpec((tm, QKV_WIDTH), lambda i: (i, 0)), pl.BlockSpec((tm, 2 * LRU_WIDTH), lambda i: (i, 0)), wide],
        [jax.ShapeDtypeStruct((rows, QKV_WIDTH), BF16), jax.ShapeDtypeStruct((rows, 2 * LRU_WIDTH), F32),
         jax.ShapeDtypeStruct((rows, D_MODEL), BF16)],
        _frame_scratch(tm), (head, g1, w_in, x), carried, modes)


N_BIAS = 3


def _attn_bias():
    key = np.arange(2 * BLOCK)[:, None]
    r = np.arange(GQA_GROUP * BLOCK)[None, :] % BLOCK
    band = (key > r) & (key <= r + BLOCK)
    out = [np.where(band & ((n - 1) * BLOCK + key >= PAD_ROWS), 0.0, NEG) for n in range(N_BIAS)]
    return jnp.asarray(np.stack(out), F32)


def _attn_probs(k2, q4, bias, sink_row):
    s = _mm_nt(k2, q4) * (HEAD_DIM ** -0.5) + bias
    m = jnp.maximum(jnp.max(s, axis=0, keepdims=True), sink_row)
    p = jnp.exp(s - m)
    es = jnp.exp(sink_row - m)
    inv = 1.0 / (jnp.sum(p, axis=0, keepdims=True) + es)
    return p * inv, es * inv


def _heads(ref, rows, first, count):
    return jnp.concatenate([ref[rows, (first + g) * HEAD_DIM:(first + g + 1) * HEAD_DIM] for g in range(count)], axis=0)


def _keys_of_block(prev_ref, cur_ref, b, kv):
    sl = slice(kv * HEAD_DIM, (kv + 1) * HEAD_DIM)
    before = prev_ref[:, sl] if b == 0 else cur_ref[(b - 1) * BLOCK:b * BLOCK, sl]
    return jnp.concatenate([before, cur_ref[b * BLOCK:(b + 1) * BLOCK, sl]], axis=0)


def _bias_of_block(bias_ref, block):
    return bias_ref[jnp.minimum(block, N_BIAS - 1)]


def _sink_row(sink_ref, kv):
    g = lax.broadcasted_iota(jnp.int32, (1, GQA_GROUP * BLOCK), 1) // BLOCK
    row = jnp.full((1, GQA_GROUP * BLOCK), sink_ref[0, kv * GQA_GROUP], F32)
    for i in range(1, GQA_GROUP):
        row = jnp.where(g == i, sink_ref[0, kv * GQA_GROUP + i], row)
    return row


def _from_head_major(pieces):
    return jnp.concatenate(pieces, axis=0).T


def _attn_specs(tm, tile_of):
    nbt = tm // BLOCK
    k_col, v_col = ATTN_WIDTH // KV_WIDTH, ATTN_WIDTH // KV_WIDTH + 1
    before = lambda i: jnp.maximum(tile_of(i) * nbt - 1, 0)
    return [pl.BlockSpec((tm, ATTN_WIDTH), lambda i: (tile_of(i), 0)),
            pl.BlockSpec((BLOCK, KV_WIDTH), lambda i: (before(i), k_col)),
            pl.BlockSpec((tm, KV_WIDTH), lambda i: (tile_of(i), k_col)),
            pl.BlockSpec((BLOCK, KV_WIDTH), lambda i: (before(i), v_col)),
            pl.BlockSpec((tm, KV_WIDTH), lambda i: (tile_of(i), v_col))]


def _attn_fwd(qkv, sinks, bias, carried, modes):
    rows = qkv.shape[0]
    tm = _row_tile(rows)
    nbt = tm // BLOCK

    def body(sink_ref, bias_ref, q_ref, kp_ref, kc_ref, vp_ref, vc_ref, o_ref):
        i = pl.program_id(0)
        for b in range(nbt):
            blk = slice(b * BLOCK, (b + 1) * BLOCK)
            bias_t = _bias_of_block(bias_ref, i * nbt + b)
            pieces = []
            for kv in range(KV_HEADS):
                k2 = _keys_of_block(kp_ref, kc_ref, b, kv)
                v2 = _keys_of_block(vp_ref, vc_ref, b, kv)
                q4 = _heads(q_ref, blk, kv * GQA_GROUP, GQA_GROUP)
                pn, _ = _attn_probs(k2, q4, bias_t, _sink_row(sink_ref, kv))
                ot = _mm_tn(v2, pn.astype(BF16))
                pieces += [ot[:, g * BLOCK:(g + 1) * BLOCK] for g in range(GQA_GROUP)]
            o_ref[blk, :] = _from_head_major(pieces).astype(BF16)

    return _hosting_call(
        body, "attn_fwd", rows // tm,
        [pl.BlockSpec(memory_space=pltpu.SMEM), _resident((N_BIAS, 2 * BLOCK, GQA_GROUP * BLOCK))]
        + _attn_specs(tm, lambda i: i),
        [pl.BlockSpec((tm, ATTN_WIDTH), lambda i: (i, 0))],
        [jax.ShapeDtypeStruct((rows, ATTN_WIDTH), BF16)],
        [], (sinks, bias, qkv, qkv, qkv, qkv, qkv), carried, modes)


def _conv_taps(xbuf, tm):
    return [xbuf[pl.ds(SUBLANES - (CONV_WIDTH - 1 - j), tm), :] for j in range(CONV_WIDTH)]


def _lru_halves(xc):
    return [xc[:, h * LRU_HALF:(h + 1) * LRU_HALF].astype(BF16) for h in range(2)]


def _lru_gates(xc, wa_ref, ba_ref, wx_ref, bx_ref, lam_ref):
    halves = _lru_halves(xc)
    gate_r = jnp.concatenate([_mm(halves[h], wa_ref[h]) for h in range(2)], axis=1) + ba_ref[...]
    gate_i = jnp.concatenate([_mm(halves[h], wx_ref[h]) for h in range(2)], axis=1) + bx_ref[...]
    r = _sigmoid(gate_r)
    ig = _sigmoid(gate_i)
    log_a = (-LRU_C) * r * _softplus(-lam_ref[...])
    a = jnp.exp(log_a)
    mult, _ = _sqrt_pos(_one_minus_sq_exp(log_a, a))
    return r, ig, a, mult


KEPT_XC, KEPT_A, KEPT_MULT, KEPT_R, KEPT_I, N_KEPT = 0, 1, 2, 3, 4, 5


def _scan_tile(a_ref, u_ref, out_ref, carry, tm):
    row = lax.broadcasted_iota(jnp.int32, (SUBLANES, LRU_WIDTH), 0)

    def step(j, before):
        o = pl.multiple_of(j * SUBLANES, SUBLANES)
        a = a_ref[pl.ds(o, SUBLANES), :]
        u = u_ref[pl.ds(o, SUBLANES), :]
        for s in (1, 2, 4):
            keep = row >= s
            u = jnp.where(keep, a * pltpu.roll(u, s, 0) + u, u)
            a = jnp.where(keep, a * pltpu.roll(a, s, 0), a)
        out = a * before + u
        out_ref[pl.ds(o, SUBLANES), :] = out
        return out[SUBLANES - 1:SUBLANES, :]

    return lax.fori_loop(0, tm // SUBLANES, step, carry)


def _rec_fwd(zrec, conv_w, conv_b, wa_bd, b_a, wx_bd, b_x, lam, carried, modes):
    rows = zrec.shape[0]
    tm = _row_tile(rows)

    def body(xr_ref, yr_ref, cw_ref, cb_ref, wa_ref, ba_ref, wx_ref, bx_ref, lam_ref, rec_ref, h_ref, kept_ref,
             xbuf, a_s, u_s, carry):
        i = pl.program_id(0)

        @pl.when(i == 0)
        def _():
            xbuf[0:SUBLANES, :] = jnp.zeros((SUBLANES, LRU_WIDTH), F32)
            carry[...] = jnp.zeros_like(carry)

        @pl.when(i > 0)
        def _():
            xbuf[0:SUBLANES, :] = xbuf[tm:tm + SUBLANES, :]

        xbuf[SUBLANES:SUBLANES + tm, :] = xr_ref[...]
        taps = _conv_taps(xbuf, tm)
        xc = cb_ref[...] + sum(cw_ref[j:j + 1, :] * taps[j] for j in range(CONV_WIDTH))
        r, ig, a, mult = _lru_gates(xc, wa_ref, ba_ref, wx_ref, bx_ref, lam_ref)
        for k, val in ((KEPT_XC, xc), (KEPT_A, a), (KEPT_MULT, mult), (KEPT_R, r), (KEPT_I, ig)):
            kept_ref[:, k * LRU_WIDTH:(k + 1) * LRU_WIDTH] = val
        grow = i * tm + lax.broadcasted_iota(jnp.int32, (tm, LRU_WIDTH), 0)
        a_s[...] = a
        u_s[...] = jnp.where(grow >= PAD_ROWS, mult * (ig * xc), 0.0)
        carry[0:1, :] = _scan_tile(a_s, u_s, h_ref, carry[0:1, :], tm)
        gel, _ = _gelu(yr_ref[...])
        rec_ref[...] = (gel * h_ref[...]).astype(BF16)

    vec = _full((1, LRU_WIDTH))
    bd = _full((2, LRU_HALF, LRU_HALF))
    return _hosting_call(
        body, "rec_fwd", rows // tm,
        [pl.BlockSpec((tm, LRU_WIDTH), lambda i: (i, 0)), pl.BlockSpec((tm, LRU_WIDTH), lambda i: (i, 1)),
         _full((CONV_WIDTH, LRU_WIDTH)), vec, bd, vec, bd, vec, vec],
        [pl.BlockSpec((tm, LRU_WIDTH), lambda i: (i, 0))] * 2 + [pl.BlockSpec((tm, N_KEPT * LRU_WIDTH), lambda i: (i, 0))],
        [jax.ShapeDtypeStruct((rows, LRU_WIDTH), BF16), jax.ShapeDtypeStruct((rows, LRU_WIDTH), F32),
         jax.ShapeDtypeStruct((rows, N_KEPT * LRU_WIDTH), F32)],
        [pltpu.VMEM((tm + SUBLANES, LRU_WIDTH), F32), pltpu.VMEM((tm, LRU_WIDTH), F32),
         pltpu.VMEM((tm, LRU_WIDTH), F32), pltpu.VMEM((SUBLANES, LRU_WIDTH), F32)],
        (zrec, zrec, conv_w, conv_b, wa_bd, b_a, wx_bd, b_x, lam), carried, modes)


def _out_proj_fwd(attn, rec, w_out, head, x, g2, carried, modes):
    rows = attn.shape[0]
    tm = _row_tile(rows)
    steps = rows // tm

    def body(attn_ref, rec_ref, w_ref, head_ref, g_ref, x_hbm, mix_ref, h1_ref, buf, sem):
        h0 = _h0_tile(head_ref, x_hbm, buf, sem, pl.program_id(0), steps, tm)
        mix = _mm(attn_ref[...], w_ref[0:ATTN_WIDTH, :]) + _mm(rec_ref[...], w_ref[ATTN_WIDTH:, :])
        y, _, _ = _rms_fwd(mix, g_ref[...])
        mix_ref[...] = mix
        h1_ref[...] = h0 + y

    half = pl.BlockSpec((tm, ATTN_WIDTH), lambda i: (i, 0))
    wide = pl.BlockSpec((tm, D_MODEL), lambda i: (i, 0))
    return _hosting_call(
        body, "out_proj_fwd", steps,
        [half, half, _resident((D_MODEL, D_MODEL)), _full((BLOCK, D_MODEL)), _full((1, D_MODEL)), ANY_SPACE],
        [wide, wide],
        [jax.ShapeDtypeStruct((rows, D_MODEL), F32)] * 2,
        _frame_scratch(tm), (attn, rec, w_out, head, g2, x), carried, modes)


FF_COLS = 1024
FF_HALF = FF_CHUNK // 2


def _hidden_at(d, half):
    return half * (D_FF // 2) + d * FF_HALF


def _ffn_up(h1, g3, w1_halves, carried, modes):
    rows = h1.shape[0]
    tm = _row_tile(rows)

    def body(h_ref, g_ref, wa_ref, wb_ref, act_ref, u_ref):
        u, _, _ = _rms_fwd(h_ref[...], g_ref[...])
        u = u.astype(BF16)
        u_ref[...] = u
        for half, w_ref in enumerate((wa_ref, wb_ref)):
            for d in range(N_DEV):
                c = _hidden_at(d, half)
                a1 = jnp.maximum(_mm(u, w_ref[d]), 0.0)
                act_ref[:, c:c + FF_HALF] = (a1 * a1).astype(BF16)

    wide = pl.BlockSpec((tm, D_MODEL), lambda i: (i, 0))
    return _hosting_call(
        body, "ffn_up", rows // tm,
        [wide, _full((1, D_MODEL))] + [_resident((N_DEV, D_MODEL, FF_HALF))] * 2,
        [pl.BlockSpec((tm, D_FF), lambda i: (i, 0)), wide],
        [jax.ShapeDtypeStruct((rows, D_FF), BF16), jax.ShapeDtypeStruct((rows, D_MODEL), BF16)],
        [], (h1, g3, *w1_halves), carried, modes)


def _ffn_down_loss(act, w2_halves, h1, target, g4, carried, modes):
    rows = h1.shape[0]
    tm = _row_tile(rows)
    steps = rows // tm
    kh = D_FF // 2

    def body(act_ref, wa_ref, wb_ref, h_ref, g_ref, t_hbm, dy_ref, df_ref, dg_ref, loss_ref, buf, sem):
        i = pl.program_id(0)
        slot = _frame_rows(t_hbm, buf, sem, i, steps, tm)

        @pl.when(i == 0)
        def _():
            dg_ref[...] = jnp.zeros_like(dg_ref)
            loss_ref[...] = jnp.zeros_like(loss_ref)
            buf[0, 0:BLOCK, :] = jnp.zeros((BLOCK, D_MODEL), F32)

        g = g_ref[...]
        f = _mm(act_ref[:, :kh], wa_ref[...]) + _mm(act_ref[:, kh:], wb_ref[...])
        y, fhat, rstd = _rms_fwd(f, g)
        grow = i * tm + lax.broadcasted_iota(jnp.int32, (tm, D_MODEL), 0)
        err = jnp.where(grow >= BLOCK, h_ref[...] + y - buf[slot], 0.0)
        loss_ref[...] += (0.5 / D_MODEL) * jnp.sum(err * err)
        dy = err * (1.0 / D_MODEL)
        df, dg = _rms_bwd(dy, fhat, rstd, g)
        dy_ref[...] = dy
        df_ref[...] = df.astype(BF16)
        dg_ref[...] += dg

    wide = pl.BlockSpec((tm, D_MODEL), lambda i: (i, 0))
    return _hosting_call(
        body, "ffn_down_loss", steps,
        [pl.BlockSpec((tm, D_FF), lambda i: (i, 0)), _resident((kh, D_MODEL)), _resident((kh, D_MODEL)), wide,
         _full((1, D_MODEL)), ANY_SPACE],
        [wide, wide, _full((1, D_MODEL)), _full((SUBLANES, LANES))],
        [jax.ShapeDtypeStruct((rows, D_MODEL), F32), jax.ShapeDtypeStruct((rows, D_MODEL), BF16),
         jax.ShapeDtypeStruct((1, D_MODEL), F32), jax.ShapeDtypeStruct((SUBLANES, LANES), F32)],
        _frame_scratch(tm), (act, *w2_halves, h1, g4, target), carried, modes)


def _ffn_bwd_act(df, w2t_halves, act, carried, modes):
    rows = df.shape[0]
    tm = _row_tile(rows)

    def body(df_ref, wa_ref, wb_ref, act_ref, da_ref):
        df_t = df_ref[...]
        for half, w_ref in enumerate((wa_ref, wb_ref)):
            for d in range(N_DEV):
                cols = slice(_hidden_at(d, half), _hidden_at(d, half) + FF_HALF)
                dact = _mm(df_t, w_ref[d])
                relu_a1, _ = _sqrt_pos(act_ref[:, cols].astype(F32))
                da_ref[:, cols] = (dact * (2.0 * relu_a1)).astype(BF16)

    hidden = pl.BlockSpec((tm, D_FF), lambda i: (i, 0))
    return _hosting_call(
        body, "ffn_bwd_act", rows // tm,
        [pl.BlockSpec((tm, D_MODEL), lambda i: (i, 0))] + [_resident((N_DEV, D_MODEL, FF_HALF))] * 2 + [hidden],
        [hidden],
        [jax.ShapeDtypeStruct((rows, D_FF), BF16)],
        [], (df, *w2t_halves, act), carried, modes)


def _ffn_bwd_x(da, w1t_halves, h1, dy, g3, carried, modes):
    rows = h1.shape[0]
    tm = _row_tile(rows)
    kh = D_FF // 2

    def body(da_ref, wa_ref, wb_ref, h_ref, dy_ref, g_ref, dh_ref, dg_ref):
        @pl.when(pl.program_id(0) == 0)
        def _():
            dg_ref[...] = jnp.zeros_like(dg_ref)

        g = g_ref[...]
        _, xhat, rstd = _rms_fwd(h_ref[...], g)
        du = _mm(da_ref[:, :kh], wa_ref[...]) + _mm(da_ref[:, kh:], wb_ref[...])
        dx, dg = _rms_bwd(du, xhat, rstd, g)
        dh_ref[...] = dy_ref[...] + dx
        dg_ref[...] += dg

    wide = pl.BlockSpec((tm, D_MODEL), lambda i: (i, 0))
    return _hosting_call(
        body, "ffn_bwd_x", rows // tm,
        [pl.BlockSpec((tm, D_FF), lambda i: (i, 0)), _resident((kh, D_MODEL)), _resident((kh, D_MODEL)), wide, wide,
         _full((1, D_MODEL))],
        [wide, _full((1, D_MODEL))],
        [jax.ShapeDtypeStruct((rows, D_MODEL), F32), jax.ShapeDtypeStruct((1, D_MODEL), F32)],
        [], (da, *w1t_halves, h1, dy, g3), carried, modes)


def _ffn_bwd_weights(u2, da, act, df, carried, modes):
    rows = u2.shape[0]
    tb = _big_tile(rows)
    steps = rows // tb
    per = FF_COLS // FF_HALF

    def body(u_ref, da_ref, act_ref, df_ref, dw1_ref, dw2_ref, acc1, acc2):
        i = pl.program_id(1)

        @pl.when(i == 0)
        def _():
            acc1[...] = jnp.zeros_like(acc1)
            acc2[...] = jnp.zeros_like(acc2)

        acc1[...] += _mm_tn(u_ref[...], da_ref[...])
        acc2[...] += _mm_tn(act_ref[...], df_ref[...])

        @pl.when(i == steps - 1)
        def _():
            for p in range(per):
                c = p * FF_HALF
                dw1_ref[p] = acc1[:, c:c + FF_HALF].astype(BF16)
                dw2_ref[p] = acc2[c:c + FF_HALF, :].astype(BF16)

    wide = pl.BlockSpec((tb, D_MODEL), lambda j, i: (i, 0))
    chunk = pl.BlockSpec((tb, FF_COLS), lambda j, i: (i, j))
    return _hosting_call(
        body, "ffn_bwd_weights", (D_FF // FF_COLS, steps),
        [wide, chunk, chunk, wide],
        [pl.BlockSpec((None, per, D_MODEL, FF_HALF), lambda j, i: (j // 2, j % 2, 0, 0)),
         pl.BlockSpec((per, FF_HALF, D_MODEL), lambda j, i: (j % 2, j // 2, 0))],
        [jax.ShapeDtypeStruct((2, N_DEV, D_MODEL, FF_HALF), BF16), jax.ShapeDtypeStruct((N_DEV, FF_CHUNK, D_MODEL), BF16)],
        [pltpu.VMEM((D_MODEL, FF_COLS), F32), pltpu.VMEM((FF_COLS, D_MODEL), F32)],
        (u2, da, act, df), carried, modes)


def _out_proj_bwd(dh1, mix, g2, w_out_t, attn, rec, carried, modes):
    rows = dh1.shape[0]
    tm = _row_tile(rows)
    steps = rows // tm

    def body(dh_ref, mix_ref, g_ref, w_ref, attn_ref, rec_ref, dattn_ref, drec_ref, dw_ref, dg_ref, acc):
        i = pl.program_id(0)

        @pl.when(i == 0)
        def _():
            acc[...] = jnp.zeros_like(acc)
            dg_ref[...] = jnp.zeros_like(dg_ref)

        g = g_ref[...]
        _, xhat, rstd = _rms_fwd(mix_ref[...], g)
        dmix, dg = _rms_bwd(dh_ref[...], xhat, rstd, g)
        dmix = dmix.astype(BF16)
        dg_ref[...] += dg
        din = _mm(dmix, w_ref[...])
        dattn_ref[...] = din[:, :ATTN_WIDTH].astype(BF16)
        drec_ref[...] = din[:, ATTN_WIDTH:]
        acc[0:ATTN_WIDTH, :] += _mm_tn(attn_ref[...], dmix)
        acc[ATTN_WIDTH:, :] += _mm_tn(rec_ref[...], dmix)

        @pl.when(i == steps - 1)
        def _():
            dw_ref[...] = acc[...].astype(BF16)

    half = pl.BlockSpec((tm, ATTN_WIDTH), lambda i: (i, 0))
    wide = pl.BlockSpec((tm, D_MODEL), lambda i: (i, 0))
    return _hosting_call(
        body, "out_proj_bwd", steps,
        [wide, wide, _full((1, D_MODEL)), _resident((D_MODEL, D_MODEL)), half, half],
        [half, half, _full((D_MODEL, D_MODEL)), _full((1, D_MODEL))],
        [jax.ShapeDtypeStruct((rows, ATTN_WIDTH), BF16), jax.ShapeDtypeStruct((rows, LRU_WIDTH), F32),
         jax.ShapeDtypeStruct((D_MODEL, D_MODEL), BF16), jax.ShapeDtypeStruct((1, D_MODEL), F32)],
        [pltpu.VMEM((D_MODEL, D_MODEL), F32)],
        (dh1, mix, g2, w_out_t, attn, rec), carried, modes)


def _attn_bwd(qkv, dattn, sinks, bias, carried, modes):
    rows = qkv.shape[0]
    tm = _row_tile(rows)
    nbt, nt = tm // BLOCK, rows // tm

    def body(sink_ref, bias_ref, do_ref, q_ref, kp_ref, kc_ref, vp_ref, vc_ref, dq_ref, dkv_ref, dsink_ref, dk_c, dv_c):
        i = pl.program_id(0)

        @pl.when(i == 0)
        def _():
            dk_c[...] = jnp.zeros_like(dk_c)
            dv_c[...] = jnp.zeros_like(dv_c)
            dsink_ref[...] = jnp.zeros_like(dsink_ref)

        @pl.when(i < nt)
        def _():
            dk_late, dv_late = dk_c[...], dv_c[...]
            dsink_rows = [jnp.zeros((1, LANES), F32)] * ATTN_HEADS
            for b in range(nbt):
                blk = slice(b * BLOCK, (b + 1) * BLOCK)
                bias_t = _bias_of_block(bias_ref, i * nbt + b)
                dq_parts, dk_parts, dv_parts = [], [], []
                for kv in range(KV_HEADS):
                    k2 = _keys_of_block(kp_ref, kc_ref, b, kv)
                    v2 = _keys_of_block(vp_ref, vc_ref, b, kv)
                    q4 = _heads(q_ref, blk, kv * GQA_GROUP, GQA_GROUP)
                    do4 = _heads(do_ref, blk, kv * GQA_GROUP, GQA_GROUP)
                    pn, psink = _attn_probs(k2, q4, bias_t, _sink_row(sink_ref, kv))
                    dpn = _mm_nt(v2, do4)
                    delta = jnp.sum(pn * dpn, axis=0, keepdims=True)
                    ds = ((pn * (dpn - delta)) * (HEAD_DIM ** -0.5)).astype(BF16)
                    dqt = _mm_tn(k2, ds)
                    dq_parts += [dqt[:, g * BLOCK:(g + 1) * BLOCK] for g in range(GQA_GROUP)]
                    dk_parts.append(_mm(ds, q4))
                    dv_parts.append(_mm(pn.astype(BF16), do4))
                    sd = psink * delta
                    for g in range(GQA_GROUP):
                        h = kv * GQA_GROUP + g
                        dsink_rows[h] = dsink_rows[h] - jnp.sum(sd[:, g * BLOCK:(g + 1) * BLOCK])
                dq_ref[blk, :] = _from_head_major(dq_parts).astype(BF16)
                dk2 = jnp.concatenate(dk_parts, axis=1)
                dv2 = jnp.concatenate(dv_parts, axis=1)
                dkv_ref[blk, 0:KV_WIDTH] = (dk_late + dk2[0:BLOCK]).astype(BF16)
                dkv_ref[blk, KV_WIDTH:] = (dv_late + dv2[0:BLOCK]).astype(BF16)
                dk_late, dv_late = dk2[BLOCK:], dv2[BLOCK:]
            dk_c[...] = dk_late
            dv_c[...] = dv_late
            dsink_ref[...] += jnp.concatenate(dsink_rows, axis=0)

        @pl.when(i == nt)
        def _():
            dkv_ref[...] = jnp.zeros_like(dkv_ref)
            dkv_ref[0:BLOCK, 0:KV_WIDTH] = dk_c[...].astype(BF16)
            dkv_ref[0:BLOCK, KV_WIDTH:] = dv_c[...].astype(BF16)

    tile_of = lambda i: jnp.minimum(i, nt - 1)
    tile = pl.BlockSpec((tm, ATTN_WIDTH), lambda i: (tile_of(i), 0))
    return _hosting_call(
        body, "attn_bwd", nt + 1,
        [pl.BlockSpec(memory_space=pltpu.SMEM), _resident((N_BIAS, 2 * BLOCK, GQA_GROUP * BLOCK)), tile]
        + _attn_specs(tm, tile_of),
        [tile, pl.BlockSpec((tm, 2 * KV_WIDTH), lambda i: (i, 0)), _full((ATTN_HEADS, LANES))],
        [jax.ShapeDtypeStruct((rows, ATTN_WIDTH), BF16), jax.ShapeDtypeStruct((rows + tm, 2 * KV_WIDTH), BF16),
         jax.ShapeDtypeStruct((ATTN_HEADS, LANES), F32)],
        [pltpu.VMEM((BLOCK, KV_WIDTH), F32), pltpu.VMEM((BLOCK, KV_WIDTH), F32)],
        (sinks, bias, dattn, qkv, qkv, qkv, qkv, qkv), carried, modes)


ROW_CONV_B, ROW_B_A, ROW_B_X, ROW_LAMBDA = 4, 5, 6, 7


def _rec_bwd(drec, zrec, h, kept, conv_w, wa_bd, wx_bd, lam, carried, modes):
    rows = zrec.shape[0]
    tm = _rec_tile(rows)
    nt = rows // tm
    per = tm // SUBLANES

    def body(drec_ref, xr_ref, yr_ref, h_ref, xc_ref, a_ref, mult_ref, r_ref, ig_ref, hhalo_ref, cw_ref, wa_ref, wx_ref,
             lam_ref, drz_ref, small_ref, dwa_ref, dwx_ref, hbuf, dbuf, dgr_s, dgi_s, dyr_s, carry):
        s = pl.program_id(0)
        i = nt - 1 - s

        @pl.when(s == 0)
        def _():
            small_ref[...] = jnp.zeros_like(small_ref)
            dwa_ref[...] = jnp.zeros_like(dwa_ref)
            dwx_ref[...] = jnp.zeros_like(dwx_ref)
            carry[...] = jnp.zeros_like(carry)
            dbuf[tm:tm + SUBLANES, :] = jnp.zeros((SUBLANES, LRU_WIDTH), F32)

        hbuf[0:SUBLANES, :] = jnp.where(i == 0, 0.0, hhalo_ref[...])
        hbuf[SUBLANES:SUBLANES + tm, :] = h_ref[...]

        row = lax.broadcasted_iota(jnp.int32, (SUBLANES, LRU_WIDTH), 0)
        log_a_scale = (-LRU_C) * _softplus(-lam_ref[...])
        zeros = jnp.zeros((SUBLANES, LRU_WIDTH), F32)

        def group(k, state):
            g_later, a_later, sum_dgr, sum_dgi, sum_lam = state
            o = pl.multiple_of((per - 1 - k) * SUBLANES, SUBLANES)
            rows8 = pl.ds(o, SUBLANES)
            yr, drec_t, h_t, a = yr_ref[rows8, :], drec_ref[rows8, :], h_ref[rows8, :], a_ref[rows8, :]
            gel, t = _gelu(yr)
            dyr_s[rows8, :] = drec_t * h_t * _gelu_grad(yr, t)
            u = drec_t * gel
            coef = jnp.where(row == SUBLANES - 1, a_later, pltpu.roll(a, SUBLANES - 1, 0))
            for sft in (1, 2, 4):
                keep = row < SUBLANES - sft
                u = jnp.where(keep, coef * pltpu.roll(u, SUBLANES - sft, 0) + u, u)
                coef = jnp.where(keep, coef * pltpu.roll(coef, SUBLANES - sft, 0), coef)
            g = coef * g_later + u
            du = jnp.where(i * tm + o + row >= PAD_ROWS, g, 0.0)
            h_before = jnp.where(row == 0, hbuf[rows8, :][SUBLANES - 1:SUBLANES, :], pltpu.roll(h_t, 1, 0))
            xc, mult, r, ig = xc_ref[rows8, :], mult_ref[rows8, :], r_ref[rows8, :], ig_ref[rows8, :]
            dbuf[rows8, :] = du * (mult * ig)
            dgi = (du * (mult * xc)) * (ig * (1.0 - ig))
            dgi_s[rows8, :] = dgi
            dlog_a = (g * h_before) * a - (du * (ig * xc)) * (a * a * pl.reciprocal(mult, approx=True))
            dgr = (dlog_a * log_a_scale) * (r * (1.0 - r))
            dgr_s[rows8, :] = dgr
            return g[0:1, :], a[0:1, :], sum_dgr + dgr, sum_dgi + dgi, sum_lam + dlog_a * r

        state = lax.fori_loop(0, per, group, (carry[0:1, :], carry[1:2, :], zeros, zeros, zeros))
        carry[0:1, :], carry[1:2, :] = state[0], state[1]
        sum_dgr, sum_dgi, sum_lam = (jnp.sum(v, axis=0, keepdims=True) for v in state[2:])
        dlam = sum_lam * (LRU_C * _sigmoid(-lam_ref[...]))

        dgr_b = [dgr_s[:, hh * LRU_HALF:(hh + 1) * LRU_HALF].astype(BF16) for hh in range(2)]
        dgi_b = [dgi_s[:, hh * LRU_HALF:(hh + 1) * LRU_HALF].astype(BF16) for hh in range(2)]
        halves = _lru_halves(xc_ref[...])
        for hh in range(2):
            dwa_ref[hh] += _mm_tn(halves[hh], dgr_b[hh])
            dwx_ref[hh] += _mm_tn(halves[hh], dgi_b[hh])
        dxc = dbuf[0:tm, :] + jnp.concatenate(
            [_mm_nt(dgr_b[hh], wa_ref[hh]) + _mm_nt(dgi_b[hh], wx_ref[hh]) for hh in range(2)], axis=1)

        dbuf[0:tm, :] = dxc
        sum_dxc = jnp.sum(dxc, axis=0, keepdims=True)
        ahead = [dbuf[pl.ds(CONV_WIDTH - 1 - j, tm), :] for j in range(CONV_WIDTH)]
        drz_ref[:, 0:LRU_WIDTH] = sum(cw_ref[j:j + 1, :] * ahead[j] for j in range(CONV_WIDTH)).astype(BF16)
        drz_ref[:, LRU_WIDTH:] = dyr_s[...].astype(BF16)
        upd = [jnp.sum(xr_ref[...] * ahead[j], axis=0, keepdims=True) for j in range(CONV_WIDTH)]
        dbuf[tm:tm + SUBLANES, :] = dbuf[0:SUBLANES, :]
        small_ref[...] += jnp.concatenate(upd + [sum_dxc, sum_dgr, sum_dgi, dlam], axis=0)

    rev = lambda s: nt - 1 - s
    halo = lambda s: jnp.maximum(rev(s) * per - 1, 0)
    cols = lambda k: pl.BlockSpec((tm, LRU_WIDTH), lambda s: (rev(s), k))
    halo0 = pl.BlockSpec((SUBLANES, LRU_WIDTH), lambda s: (halo(s), 0))
    bd = _full((2, LRU_HALF, LRU_HALF))
    big = pltpu.VMEM((tm + SUBLANES, LRU_WIDTH), F32)
    tile = pltpu.VMEM((tm, LRU_WIDTH), F32)
    kept_cols = [cols(k) for k in (KEPT_XC, KEPT_A, KEPT_MULT, KEPT_R, KEPT_I)]
    return _hosting_call(
        body, "rec_bwd", nt,
        [cols(0), cols(0), cols(1), cols(0)] + kept_cols
        + [halo0, _full((CONV_WIDTH, LRU_WIDTH)), bd, bd, _full((1, LRU_WIDTH))],
        [pl.BlockSpec((tm, 2 * LRU_WIDTH), lambda s: (rev(s), 0)), _full((SUBLANES, LRU_WIDTH)), bd, bd],
        [jax.ShapeDtypeStruct((rows, 2 * LRU_WIDTH), BF16), jax.ShapeDtypeStruct((SUBLANES, LRU_WIDTH), F32),
         jax.ShapeDtypeStruct((2, LRU_HALF, LRU_HALF), F32), jax.ShapeDtypeStruct((2, LRU_HALF, LRU_HALF), F32)],
        [big, big, tile, tile, tile, pltpu.VMEM((SUBLANES, LRU_WIDTH), F32)],
        (drec, zrec, zrec, h) + (kept,) * N_KEPT + (h, conv_w, wa_bd, wx_bd, lam), carried, modes)


DZ_CUTS = (0, ATTN_WIDTH, QKV_WIDTH, IN_WIDTH)


def _dz_specs(tm):
    return [pl.BlockSpec((tm, DZ_CUTS[p + 1] - DZ_CUTS[p]), lambda i: (i, 0)) for p in range(3)]


def _in_proj_bwd_x(head, x, g1, dh1, dq, dkv, drz, w_in_t, carried, modes):
    rows = dh1.shape[0]
    tm = _row_tile(rows)
    steps = rows // tm

    def body(head_ref, g_ref, dh1_ref, dq_ref, dkv_ref, drz_ref, w_ref, x_hbm, dh0_ref, dg_ref, buf, sem):
        i = pl.program_id(0)
        h0 = _h0_tile(head_ref, x_hbm, buf, sem, i, steps, tm)

        @pl.when(i == 0)
        def _():
            dg_ref[...] = jnp.zeros_like(dg_ref)

        g = g_ref[...]
        _, xhat, rstd = _rms_fwd(h0, g)
        parts = (dq_ref[...], dkv_ref[...], drz_ref[...])
        du = sum(_mm(parts[p], w_ref[DZ_CUTS[p]:DZ_CUTS[p + 1], :]) for p in range(3))
        dx, dg = _rms_bwd(du, xhat, rstd, g)
        dh0_ref[...] = dh1_ref[...] + dx
        dg_ref[...] += dg

    wide = pl.BlockSpec((tm, D_MODEL), lambda i: (i, 0))
    return _hosting_call(
        body, "in_proj_bwd_x", steps,
        [_full((BLOCK, D_MODEL)), _full((1, D_MODEL)), wide] + _dz_specs(tm) + [_resident((IN_WIDTH, D_MODEL)), ANY_SPACE],
        [wide, _full((1, D_MODEL))],
        [jax.ShapeDtypeStruct((rows, D_MODEL), F32), jax.ShapeDtypeStruct((1, D_MODEL), F32)],
        _frame_scratch(tm), (head, g1, dh1, dq, dkv, drz, w_in_t, x), carried, modes)


def _in_proj_bwd_w(u1, dq, dkv, drz, carried, modes):
    rows = u1.shape[0]
    tb = _big_tile(rows)
    steps = rows // tb

    def body(u_ref, dq_ref, dkv_ref, drz_ref, dw_ref, acc):
        i = pl.program_id(0)

        @pl.when(i == 0)
        def _():
            acc[...] = jnp.zeros_like(acc)

        u = u_ref[...]
        for p, ref in enumerate((dq_ref, dkv_ref, drz_ref)):
            acc[:, DZ_CUTS[p]:DZ_CUTS[p + 1]] += _mm_tn(u, ref[...])

        @pl.when(i == steps - 1)
        def _():
            dw_ref[...] = acc[...].astype(BF16)

    return _hosting_call(
        body, "in_proj_bwd_w", steps,
        [pl.BlockSpec((tb, D_MODEL), lambda i: (i, 0))] + _dz_specs(tb),
        [_full((D_MODEL, IN_WIDTH))],
        [jax.ShapeDtypeStruct((D_MODEL, IN_WIDTH), BF16)],
        [pltpu.VMEM((D_MODEL, IN_WIDTH), F32)], (u1, dq, dkv, drz), carried, modes)


def _adamw_math(w, m, v, g):
    nm = ADAM_B1 * m + (1.0 - ADAM_B1) * g
    nv = ADAM_B2 * v + (1.0 - ADAM_B2) * (g * g)
    m_hat = nm / (1.0 - ADAM_B1 ** ADAM_STEP)
    v_hat = nv / (1.0 - ADAM_B2 ** ADAM_STEP)
    return (-ADAM_LR) * (m_hat / (jnp.sqrt(v_hat) + ADAM_EPS) + ADAM_WD * w), nm, nv


SMALL_NAMES = ("conv_b", "b_a", "b_x", "lru_lambda", "attn_sinks", "g_post_mix", "g_pre_ffn", "g_post_ffn")
PACK_WIDTH = 1024


def _pack_rows(vals):
    assert len(SMALL_NAMES) == SUBLANES
    row = lax.broadcasted_iota(jnp.int32, (SUBLANES, PACK_WIDTH), 0)
    tile = jnp.zeros((SUBLANES, PACK_WIDTH), F32)
    for k, name in enumerate(SMALL_NAMES):
        a = vals[name].reshape(1, -1)
        tile = jnp.where(row == k, jnp.pad(a, ((0, 0), (0, PACK_WIDTH - a.shape[1]))), tile)
    return tile


def _adamw_small(weights, mom_m, mom_v, parts, loss_parts):
    n = len(SMALL_NAMES)
    views = [(1, weights[name].size) for name in SMALL_NAMES]

    def body(*refs):
        w_refs, m_refs, v_refs = refs[:n], refs[n:2 * n], refs[2 * n:3 * n]
        p_ref, l_ref, loss_ref = refs[3 * n], refs[3 * n + 1], refs[3 * n + 2]
        outs = refs[3 * n + 3:]
        for k, (_, c) in enumerate(views):
            g = p_ref[0, k:k + 1, 0:c]
            for s in range(1, N_DEV):
                g = g + p_ref[s, k:k + 1, 0:c]
            g_ref, d_ref, nm_ref, nv_ref = outs[4 * k:4 * k + 4]
            g_ref[...] = g
            d_ref[...], nm_ref[...], nv_ref[...] = _adamw_math(w_refs[k][...], m_refs[k][...], v_refs[k][...], g)
        total = l_ref[0]
        for s in range(1, N_DEV):
            total = total + l_ref[s]
        loss_ref[...] = total

    args = [src[name].reshape(view) for src in (weights, mom_m, mom_v) for name, view in zip(SMALL_NAMES, views)]
    res = pl.pallas_call(
        body, name="adamw_small",
        out_shape=[jax.ShapeDtypeStruct(loss_parts.shape[1:], F32)]
                  + [jax.ShapeDtypeStruct(view, F32) for view in views for _ in range(4)],
        compiler_params=pltpu.CompilerParams(vmem_limit_bytes=VMEM_LIMIT),
    )(*args, parts, loss_parts)
    out = {name: tuple(t.reshape(weights[name].shape) for t in res[1 + 4 * k:5 + 4 * k]) for k, name in enumerate(SMALL_NAMES)}
    return res[0], out


def _adamw(w, m, v, parts, name):
    rows, cols = w.shape
    tr = next((t for t in (256, 128) if rows % t == 0), rows)
    parts = parts if isinstance(parts, (list, tuple)) else [parts]

    def body(w_ref, m_ref, v_ref, *refs):
        p_refs, (g_ref, d_ref, nm_ref, nv_ref) = refs[:len(parts)], refs[len(parts):]

        def total(p_ref):
            g = p_ref[0].astype(F32)
            for s in range(1, N_DEV):
                g = g + p_ref[s].astype(F32)
            return g

        g = jnp.concatenate([total(p_ref) for p_ref in p_refs], axis=1) if len(parts) > 1 else total(p_refs[0])
        g_ref[...] = g
        d_ref[...], nm_ref[...], nv_ref[...] = _adamw_math(w_ref[...], m_ref[...], v_ref[...], g)

    blk = pl.BlockSpec((tr, cols), lambda i: (i, 0))
    return pl.pallas_call(
        body, name=name, grid=(rows // tr,),
        in_specs=[blk, blk, blk] + [pl.BlockSpec((N_DEV, tr, p.shape[2]), lambda i: (0, i, 0)) for p in parts],
        out_specs=[blk] * 4,
        out_shape=[jax.ShapeDtypeStruct((rows, cols), F32)] * 4,
        compiler_params=_params(("parallel",)),
    )(w, m, v, *parts)


def _cols_from_shards(g):
    return jnp.transpose(g, (1, 0, 2)).reshape(g.shape[1], N_DEV * g.shape[2])


def _cols_to_shards(a):
    r, c = a.shape
    return jnp.transpose(a.reshape(r, N_DEV, c // N_DEV), (1, 0, 2))


def _block_diag(w):
    per = LRU_HALF // LRU_BLOCK
    w = w.reshape(2, per, LRU_BLOCK, LRU_BLOCK)
    eye = jnp.eye(per, dtype=w.dtype)
    return (w[:, :, :, None, :] * eye[None, :, None, :, None]).reshape(2, LRU_HALF, LRU_HALF)


def _block_diag_extract(t):
    per = LRU_HALF // LRU_BLOCK
    t = t.reshape(2, per, LRU_BLOCK, per, LRU_BLOCK)
    return jnp.stack([t[:, b, :, b, :] for b in range(per)], axis=1).reshape(LRU_BLOCKS, LRU_BLOCK, LRU_BLOCK)


def kernel(x, meta_tokens, g_pre_mix, w_in, conv_w, conv_b, w_a, b_a, w_x, b_x, lru_lambda, attn_sinks, w_out, g_post_mix, g_pre_ffn, w_ff1, w_ff2, g_post_ffn, loss_target, m_meta_tokens, m_g_pre_mix, m_w_in, m_conv_w, m_conv_b, m_w_a, m_b_a, m_w_x, m_b_x, m_lru_lambda, m_attn_sinks, m_w_out, m_g_post_mix, m_g_pre_ffn, m_w_ff1, m_w_ff2, m_g_post_ffn, v_meta_tokens, v_g_pre_mix, v_w_in, v_conv_w, v_conv_b, v_w_a, v_b_a, v_w_x, v_b_x, v_lru_lambda, v_attn_sinks, v_w_out, v_g_post_mix, v_g_pre_ffn, v_w_ff1, v_w_ff2, v_g_post_ffn):
    weights = dict(meta_tokens=meta_tokens, g_pre_mix=g_pre_mix, w_in=w_in, conv_w=conv_w, conv_b=conv_b, w_a=w_a,
                   b_a=b_a, w_x=w_x, b_x=b_x, lru_lambda=lru_lambda, attn_sinks=attn_sinks, w_out=w_out,
                   g_post_mix=g_post_mix, g_pre_ffn=g_pre_ffn, w_ff1=w_ff1, w_ff2=w_ff2, g_post_ffn=g_post_ffn)
    mom_m = dict(meta_tokens=m_meta_tokens, g_pre_mix=m_g_pre_mix, w_in=m_w_in, conv_w=m_conv_w, conv_b=m_conv_b,
                 w_a=m_w_a, b_a=m_b_a, w_x=m_w_x, b_x=m_b_x, lru_lambda=m_lru_lambda, attn_sinks=m_attn_sinks,
                 w_out=m_w_out, g_post_mix=m_g_post_mix, g_pre_ffn=m_g_pre_ffn, w_ff1=m_w_ff1, w_ff2=m_w_ff2,
                 g_post_ffn=m_g_post_ffn)
    mom_v = dict(meta_tokens=v_meta_tokens, g_pre_mix=v_g_pre_mix, w_in=v_w_in, conv_w=v_conv_w, conv_b=v_conv_b,
                 w_a=v_w_a, b_a=v_b_a, w_x=v_w_x, b_x=v_b_x, lru_lambda=v_lru_lambda, attn_sinks=v_attn_sinks,
                 w_out=v_w_out, g_post_mix=v_g_post_mix, g_pre_ffn=v_g_pre_ffn, w_ff1=v_w_ff1, w_ff2=v_w_ff2,
                 g_post_ffn=v_g_post_ffn)
    order = list(weights)

    (g_win, g_meta, g_cw) = _gather_two_level([w_in[0].astype(BF16), meta_tokens, conv_w[0]], "gather_first")
    w_in_full = _cols_from_shards(g_win)
    meta_full = _cols_from_shards(g_meta)
    conv_w_full = _cols_from_shards(g_cw)

    head = jnp.concatenate([jnp.zeros((PAD_ROWS, D_MODEL), F32), meta_full], axis=0)
    wa_bd = _block_diag(w_a[0]).astype(BF16)
    wx_bd = _block_diag(w_x[0]).astype(BF16)
    bias = _attn_bias()

    w1_shard = w_ff1[0].astype(BF16)
    (qkv, zrec, u1), (g_wout,) = _in_proj_fwd(head, x[0], g_pre_mix, w_in_full, [w_out[0].astype(BF16)], ["gather"])
    (attn,), (w1a,) = _attn_fwd(qkv, attn_sinks, bias, [w1_shard[:, :FF_HALF]], ["gather"])
    (rec, h_lru, kept), (w1b,) = _rec_fwd(zrec, conv_w_full, conv_b, wa_bd, b_a, wx_bd, b_x, lru_lambda,
                                         [w1_shard[:, FF_HALF:]], ["gather"])
    w_out_full = g_wout.reshape(D_MODEL, D_MODEL)
    w2_shard = w_ff2[0].astype(BF16)
    (mix, h1), (w2a,) = _out_proj_fwd(attn, rec, w_out_full, head, x[0], g_post_mix, [w2_shard[:FF_HALF]], ["gather"])
    (act, u2), (w2b,) = _ffn_up(h1, g_pre_ffn, (w1a, w1b), [w2_shard[FF_HALF:]], ["gather"])
    w2_halves = [w.reshape(D_FF // 2, D_MODEL) for w in (w2a, w2b)]
    (dy, df, dg_post_ffn, loss_acc), w2t_halves = _ffn_down_loss(
        act, w2_halves, h1, loss_target[0], g_post_ffn, [w2_shard[:FF_HALF].T, w2_shard[FF_HALF:].T], ["gather"] * 2)

    (da1,), (w1ta,) = _ffn_bwd_act(df, w2t_halves, act, [w1_shard[:, :FF_HALF].T], ["gather"])
    (dw1h, dw2g), (w1tb,) = _ffn_bwd_weights(u2, da1, act, df, [w1_shard[:, FF_HALF:].T], ["gather"])
    w1t_halves = [w.reshape(D_FF // 2, D_MODEL) for w in (w1ta, w1tb)]
    (dh1, dg_pre_ffn), (p_w1a,) = _ffn_bwd_x(da1, w1t_halves, h1, dy, g_pre_ffn, [dw1h[0]], ["scatter"])
    (dattn, drec, dw_out, dg_post_mix), (p_w1b,) = _out_proj_bwd(dh1, mix, g_post_mix, w_out_full.T, attn, rec,
                                                                [dw1h[1]], ["scatter"])
    (dq, dkv_late, dsinks), (p_w2,) = _attn_bwd(qkv, dattn, attn_sinks, bias, [dw2g], ["scatter"])
    dkv = dkv_late[BLOCK:BLOCK + qkv.shape[0]]
    (drz, rec_small, dwa_bd, dwx_bd), (p_wout,) = _rec_bwd(
        drec, zrec, h_lru, kept, conv_w_full, wa_bd, wx_bd, lru_lambda,
        [dw_out.reshape(N_DEV, D_MODEL // N_DEV, D_MODEL)], ["scatter"])
    small_grads = dict(
        conv_b=rec_small[ROW_CONV_B], b_a=rec_small[ROW_B_A], b_x=rec_small[ROW_B_X], lru_lambda=rec_small[ROW_LAMBDA],
        attn_sinks=dsinks[:, 0], g_post_mix=dg_post_mix, g_pre_ffn=dg_pre_ffn, g_post_ffn=dg_post_ffn)
    gate_rows = (LRU_BLOCKS * LRU_BLOCK, LRU_BLOCK)
    gate_dense = (LRU_BLOCKS * LRU_BLOCK * LRU_BLOCK // PACK_WIDTH, PACK_WIDTH)
    (dw_in,), (p_cw, p_small, p_wa, p_wx) = _in_proj_bwd_w(
        u1, dq, dkv, drz,
        [_cols_to_shards(rec_small[0:CONV_WIDTH]), _pack_rows(small_grads),
         _block_diag_extract(dwa_bd).reshape(gate_dense), _block_diag_extract(dwx_bd).reshape(gate_dense)],
        ["scatter", "gather", "gather", "gather"])
    p_wa, p_wx = (p.reshape((N_DEV,) + gate_rows) for p in (p_wa, p_wx))
    (dh0, dg_pre_mix), (p_win,) = _in_proj_bwd_x(
        head, x[0], g_pre_mix, dh1, dq, dkv, drz, w_in_full.T, [_cols_to_shards(dw_in)], ["scatter"])
    p_meta, p_gpm, p_loss = _exchange([_cols_to_shards(dh0[PAD_ROWS:BLOCK]), dg_pre_mix, loss_acc],
                                      ["scatter", "gather", "gather"], "exchange_last")

    res = {}
    res["g_pre_mix"] = _adamw(g_pre_mix, m_g_pre_mix, v_g_pre_mix, p_gpm, "adamw_g_pre_mix")
    res["w_in"] = _adamw(w_in[0], m_w_in[0], v_w_in[0], p_win, "adamw_w_in")
    res["w_out"] = _adamw(w_out[0], m_w_out[0], v_w_out[0], p_wout, "adamw_w_out")
    res["w_ff1"] = _adamw(w_ff1[0], m_w_ff1[0], v_w_ff1[0], [p_w1a, p_w1b], "adamw_w_ff1")
    res["w_ff2"] = _adamw(w_ff2[0], m_w_ff2[0], v_w_ff2[0], p_w2, "adamw_w_ff2")
    res["meta_tokens"] = _adamw(meta_tokens, m_meta_tokens, v_meta_tokens, p_meta, "adamw_meta")
    res["conv_w"] = _adamw(conv_w[0], m_conv_w[0], v_conv_w[0], p_cw, "adamw_conv_w")
    for name in ("w_in", "w_out", "w_ff1", "w_ff2", "conv_w"):
        res[name] = tuple(t[None] for t in res[name])
    for name, parts in (("w_a", p_wa), ("w_x", p_wx)):
        gate = _adamw(*(src[name].reshape(gate_rows) for src in (weights, mom_m, mom_v)), parts, "adamw_" + name)
        res[name] = tuple(t.reshape(weights[name].shape) for t in gate)
    loss_total, small = _adamw_small(weights, mom_m, mom_v, p_small, p_loss)
    res.update(small)

    grad_x = dh0[BLOCK:][None]
    outs = [loss_total[0, 0], grad_x]
    for k in range(4):
        outs += [res[name][k] for name in order]
    return tuple(outs)
```

```python
import jax
import jax.numpy as jnp
import numpy as np
from jax import lax
from jax.experimental import pallas as pl
from jax.experimental.pallas import tpu as pltpu

F32 = jnp.float32
BF16 = jnp.bfloat16

D_MODEL = 1024
N_META = 16
HEAD_DIM = 64
ATTN_HEADS = 8
KV_HEADS = 2
GQA_GROUP = ATTN_HEADS // KV_HEADS
ATTN_WIDTH = ATTN_HEADS * HEAD_DIM
KV_WIDTH = KV_HEADS * HEAD_DIM
QKV_WIDTH = ATTN_WIDTH + 2 * KV_WIDTH
LRU_WIDTH = 512
LRU_BLOCKS = 8
LRU_BLOCK = 64
LRU_HALF = 256
LRU_C = 8.0
CONV_WIDTH = 4
BLOCK = 128
PAD_ROWS = BLOCK - N_META
IN_WIDTH = QKV_WIDTH + 2 * LRU_WIDTH
D_FF = 4096
EPS = 1e-6
NEG = -1e30
N_DEV = 8
FF_CHUNK = D_FF // N_DEV
SUBLANES = 8
LANES = 128

ADAM_LR = 0.001
ADAM_B1 = 0.9
ADAM_B2 = 0.999
ADAM_EPS = 1e-08
ADAM_WD = 0.01
ADAM_STEP = 10

VMEM_LIMIT = 56 * 1024 * 1024


def _row_tile(rows):
    for t in (640, 512, 256, 128):
        if rows % t == 0:
            return t
    raise ValueError(rows)


def _big_tile(rows):
    for t in (1664, 1024, 512, 256, 128):
        if rows % t == 0:
            return t
    raise ValueError(rows)


def _rec_tile(rows):
    for t in (416, 256, 128):
        if rows % t == 0:
            return t
    raise ValueError(rows)


def _params(semantics):
    return pltpu.CompilerParams(dimension_semantics=semantics, vmem_limit_bytes=VMEM_LIMIT)


def _mm(a, b):
    return lax.dot_general(a, b, (((1,), (0,)), ((), ())), preferred_element_type=F32)


def _mm_nt(a, b):
    return lax.dot_general(a, b, (((1,), (1,)), ((), ())), preferred_element_type=F32)


def _mm_tn(a, b):
    return lax.dot_general(a, b, (((0,), (0,)), ((), ())), preferred_element_type=F32)


def _rms_fwd(x, g):
    rstd = lax.rsqrt(jnp.mean(x * x, axis=-1, keepdims=True) + EPS)
    xhat = x * rstd
    return xhat * g, xhat, rstd


def _rms_bwd(dy, xhat, rstd, g):
    dyg = dy * g
    c = jnp.mean(dyg * xhat, axis=-1, keepdims=True)
    dx = rstd * (dyg - xhat * c)
    dg = jnp.sum(dy * xhat, axis=0, keepdims=True)
    return dx, dg


def _sigmoid(x):
    return pl.reciprocal(1.0 + jnp.exp(-x), approx=True)


def _log1p(x):
    u = 1.0 + x
    return jnp.where(u == 1.0, x, jnp.log(u) * x / (u - 1.0))


def _one_minus_sq_exp(x, ex):
    return -jnp.tanh(x) * (1.0 + ex * ex)


TINY = 1e-30


def _sqrt_pos(y):
    r = lax.rsqrt(jnp.maximum(y, TINY))
    return y * r, r


def _softplus(x):
    return jnp.maximum(x, 0.0) + _log1p(jnp.exp(-jnp.abs(x)))


GELU_C = 0.7978845608028654
GELU_K = 0.044715


def _gelu(x):
    t = jnp.tanh(GELU_C * (x + GELU_K * x * x * x))
    return 0.5 * x * (1.0 + t), t


def _gelu_grad(x, t):
    return 0.5 * (1.0 + t) + 0.5 * x * (1.0 - t * t) * GELU_C * (1.0 + 3.0 * GELU_K * x * x)


def _full(shape):
    return pl.BlockSpec(shape, lambda *_: (0,) * len(shape))


def _resident(shape):
    return pl.BlockSpec(shape, lambda *_: (0,) * len(shape), pipeline_mode=pl.Buffered(1))


def _exchange_copies(ins, outs, sems, modes):
    send_sems, recv_sems, local_sems = sems
    x, y, c = lax.axis_index("x"), lax.axis_index("y"), lax.axis_index("c")
    me = 4 * x + 2 * y + c

    def block(a, dev):
        return ins[a] if modes[a] == "gather" else ins[a].at[dev]

    local = [pltpu.make_async_copy(block(a, me), outs[a].at[me], local_sems.at[a]) for a in range(len(ins))]
    sends, recvs = [], []
    for a in range(len(ins)):
        for k in range(N_DEV - 1):
            bits = k + 1
            px = jnp.bitwise_xor(x, (bits >> 2) & 1)
            py = jnp.bitwise_xor(y, (bits >> 1) & 1)
            pc = jnp.bitwise_xor(c, bits & 1)
            peer = 4 * px + 2 * py + pc
            common = dict(src_ref=block(a, peer), send_sem=send_sems.at[a, k], recv_sem=recv_sems.at[a, k],
                          device_id=(px, py, pc), device_id_type=pl.DeviceIdType.MESH)
            sends.append(pltpu.make_async_remote_copy(dst_ref=outs[a].at[me], **common))
            recvs.append(pltpu.make_async_remote_copy(dst_ref=outs[a].at[peer], **common))
    return local, sends, recvs


def _exchange_start(ins, outs, sems, modes):
    local, sends, _ = _exchange_copies(ins, outs, sems, modes)
    for cp in local + sends:
        cp.start()


def _exchange_wait(ins, outs, sems, modes):
    local, sends, recvs = _exchange_copies(ins, outs, sems, modes)
    for cp in recvs:
        cp.wait_recv()
    for cp in sends:
        cp.wait_send()
    for cp in local:
        cp.wait()


def _exchange_shapes(arrays, modes):
    return [jax.ShapeDtypeStruct((N_DEV,) + a.shape if mode == "gather" else a.shape, a.dtype)
            for a, mode in zip(arrays, modes)]


def _exchange_sems(na):
    return [pltpu.SemaphoreType.DMA((na, N_DEV - 1)), pltpu.SemaphoreType.DMA((na, N_DEV - 1)),
            pltpu.SemaphoreType.DMA((na,))]


ANY_SPACE = pl.BlockSpec(memory_space=pl.ANY)


def _exchange(arrays, modes, name):
    na = len(arrays)

    def body(*refs):
        ins, outs, sems = refs[:na], refs[na:2 * na], refs[2 * na:]
        _exchange_start(ins, outs, sems, modes)
        _exchange_wait(ins, outs, sems, modes)

    return pl.pallas_call(
        body, name=name, out_shape=_exchange_shapes(arrays, modes),
        in_specs=[ANY_SPACE] * na, out_specs=[ANY_SPACE] * na, scratch_shapes=_exchange_sems(na),
        compiler_params=pltpu.CompilerParams(has_side_effects=True),
    )(*arrays)


def _gather_two_level(arrays, name):
    na = len(arrays)

    def body(*refs):
        ins, outs = refs[:na], refs[na:2 * na]
        send_sems, recv_sems, local_sems = refs[2 * na:]
        x, y, c = lax.axis_index("x"), lax.axis_index("y"), lax.axis_index("c")
        me, sibling = (x, y, c), (x, y, 1 - c)
        chips = [(1 - x, y), (x, 1 - y), (1 - x, 1 - y)]

        def copy(a, k, block, to, src=None):
            slot = outs[a].at[4 * block[0] + 2 * block[1] + block[2]]
            return pltpu.make_async_remote_copy(
                src_ref=slot if src is None else src, dst_ref=slot, send_sem=send_sems.at[a, k],
                recv_sem=recv_sems.at[a, k], device_id=to, device_id_type=pl.DeviceIdType.MESH)

        local = [pltpu.make_async_copy(ins[a], outs[a].at[4 * x + 2 * y + c], local_sems.at[a]) for a in range(na)]
        first = []
        for a in range(na):
            first.append(copy(a, 0, me, sibling, src=ins[a]))
            first += [copy(a, 1 + j, me, (*chip, c), src=ins[a]) for j, chip in enumerate(chips)]
        for cp in local + first:
            cp.start()
        passed = []
        for j, chip in enumerate(chips):
            for a in range(na):
                copy(a, 1 + j, (*chip, c), me).wait_recv()
                passed.append(copy(a, 4 + j, (*chip, c), sibling))
                passed[-1].start()
        for a in range(na):
            copy(a, 0, sibling, me).wait_recv()
            for j, chip in enumerate(chips):
                copy(a, 4 + j, (*chip, 1 - c), me).wait_recv()
        for cp in first + passed:
            cp.wait_send()
        for cp in local:
            cp.wait()

    return pl.pallas_call(
        body, name=name, out_shape=_exchange_shapes(arrays, ["gather"] * na),
        in_specs=[ANY_SPACE] * na, out_specs=[ANY_SPACE] * na, scratch_shapes=_exchange_sems(na),
        compiler_params=pltpu.CompilerParams(has_side_effects=True),
    )(*arrays)


def _hosting_call(body, name, steps, in_specs, out_specs, out_shape, scratch_shapes, args, arrays, modes):
    n_in, n_out, n_scr, na = len(in_specs), len(out_specs), len(scratch_shapes), len(arrays)
    grid = steps if isinstance(steps, tuple) else (steps,)

    def hosting_body(*refs):
        cuts = [0]
        for n in (n_in, na, n_out, na, n_scr, 3):
            cuts.append(cuts[-1] + n)
        ins, x_ins, outs, x_outs, scr, sems = (refs[cuts[p]:cuts[p + 1]] for p in range(6))
        first, last = True, True
        for axis, n in enumerate(grid):
            first = first & (pl.program_id(axis) == 0)
            last = last & (pl.program_id(axis) == n - 1)

        @pl.when(first)
        def _():
            _exchange_start(x_ins, x_outs, sems, modes)

        body(*ins, *outs, *scr)

        @pl.when(last)
        def _():
            _exchange_wait(x_ins, x_outs, sems, modes)

    res = pl.pallas_call(
        hosting_body, name=name, grid=grid,
        in_specs=list(in_specs) + [ANY_SPACE] * na, out_specs=list(out_specs) + [ANY_SPACE] * na,
        out_shape=list(out_shape) + _exchange_shapes(arrays, modes),
        scratch_shapes=list(scratch_shapes) + _exchange_sems(na),
        compiler_params=_params(("arbitrary",) * len(grid)),
    )(*args, *arrays)
    return res[:n_out], res[n_out:]


def _frame_rows(src_hbm, buf, sem, i, steps, tm):
    def first():
        return pltpu.make_async_copy(src_hbm.at[pl.ds(0, tm - BLOCK)], buf.at[0, pl.ds(BLOCK, tm - BLOCK)], sem.at[0])

    def later(t, slot):
        return pltpu.make_async_copy(src_hbm.at[pl.ds(pl.multiple_of(t * tm - BLOCK, SUBLANES), tm)], buf.at[slot], sem.at[slot])

    slot = i % 2

    @pl.when(i == 0)
    def _():
        first().start()

    @pl.when(i + 1 < steps)
    def _():
        later(i + 1, 1 - slot).start()

    @pl.when(i == 0)
    def _():
        first().wait()

    @pl.when(i > 0)
    def _():
        later(i, slot).wait()

    return slot


def _frame_scratch(tm):
    return [pltpu.VMEM((2, tm, D_MODEL), F32), pltpu.SemaphoreType.DMA((2,))]


def _h0_tile(head_ref, x_hbm, buf, sem, i, steps, tm):
    slot = _frame_rows(x_hbm, buf, sem, i, steps, tm)

    @pl.when(i == 0)
    def _():
        buf[0, 0:BLOCK, :] = head_ref[...]

    return buf[slot]


def _in_proj_fwd(head, x, g1, w_in, carried, modes):
    rows = BLOCK + x.shape[0]
    tm = _row_tile(rows)
    steps = rows // tm

    def body(head_ref, g_ref, w_ref, x_hbm, qkv_ref, zrec_ref, u_ref, buf, sem):
        h = _h0_tile(head_ref, x_hbm, buf, sem, pl.program_id(0), steps, tm)
        u, _, _ = _rms_fwd(h, g_ref[...])
        u = u.astype(BF16)
        u_ref[...] = u
        z = _mm(u, w_ref[...])
        qkv_ref[...] = z[:, :QKV_WIDTH].astype(BF16)
        zrec_ref[...] = z[:, QKV_WIDTH:]

    wide = pl.BlockSpec((tm, D_MODEL), lambda i: (i, 0))
    return _hosting_call(
        body, "in_proj_fwd", steps,
        [_full((BLOCK, D_MODEL)), _full((1, D_MODEL)), _resident((D_MODEL, IN_WIDTH)), ANY_SPACE],
        [pl.BlockSpec((tm, QKV_WIDTH), lambda i: (i, 0)), pl.BlockSpec((tm, 2 * LRU_WIDTH), lambda i: (i, 0)), wide],
        [jax.ShapeDtypeStruct((rows, QKV_WIDTH), BF16), jax.ShapeDtypeStruct((rows, 2 * LRU_WIDTH), F32),
         jax.ShapeDtypeStruct((rows, D_MODEL), BF16)],
        _frame_scratch(tm), (head, g1, w_in, x), carried, modes)


N_BIAS = 3


def _attn_bias():
    key = np.arange(2 * BLOCK)[:, None]
    r = np.arange(GQA_GROUP * BLOCK)[None, :] % BLOCK
    band = (key > r) & (key <= r + BLOCK)
    out = [np.where(band & ((n - 1) * BLOCK + key >= PAD_ROWS), 0.0, NEG) for n in range(N_BIAS)]
    return jnp.asarray(np.stack(out), F32)


def _attn_probs(k2, q4, bias, sink_row):
    s = _mm_nt(k2, q4) * (HEAD_DIM ** -0.5) + bias
    m = jnp.maximum(jnp.max(s, axis=0, keepdims=True), sink_row)
    p = jnp.exp(s - m)
    es = jnp.exp(sink_row - m)
    inv = 1.0 / (jnp.sum(p, axis=0, keepdims=True) + es)
    return p * inv, es * inv


def _heads(ref, rows, first, count):
    return jnp.concatenate([ref[rows, (first + g) * HEAD_DIM:(first + g + 1) * HEAD_DIM] for g in range(count)], axis=0)


def _keys_of_block(prev_ref, cur_ref, b, kv):
    sl = slice(kv * HEAD_DIM, (kv + 1) * HEAD_DIM)
    before = prev_ref[:, sl] if b == 0 else cur_ref[(b - 1) * BLOCK:b * BLOCK, sl]
    return jnp.concatenate([before, cur_ref[b * BLOCK:(b + 1) * BLOCK, sl]], axis=0)


def _bias_of_block(bias_ref, block):
    return bias_ref[jnp.minimum(block, N_BIAS - 1)]


def _sink_row(sink_ref, kv):
    g = lax.broadcasted_iota(jnp.int32, (1, GQA_GROUP * BLOCK), 1) // BLOCK
    row = jnp.full((1, GQA_GROUP * BLOCK), sink_ref[0, kv * GQA_GROUP], F32)
    for i in range(1, GQA_GROUP):
        row = jnp.where(g == i, sink_ref[0, kv * GQA_GROUP + i], row)
    return row


def _from_head_major(pieces):
    return jnp.concatenate(pieces, axis=0).T


def _attn_specs(tm, tile_of):
    nbt = tm // BLOCK
    k_col, v_col = ATTN_WIDTH // KV_WIDTH, ATTN_WIDTH // KV_WIDTH + 1
    before = lambda i: jnp.maximum(tile_of(i) * nbt - 1, 0)
    return [pl.BlockSpec((tm, ATTN_WIDTH), lambda i: (tile_of(i), 0)),
            pl.BlockSpec((BLOCK, KV_WIDTH), lambda i: (before(i), k_col)),
            pl.BlockSpec((tm, KV_WIDTH), lambda i: (tile_of(i), k_col)),
            pl.BlockSpec((BLOCK, KV_WIDTH), lambda i: (before(i), v_col)),
            pl.BlockSpec((tm, KV_WIDTH), lambda i: (tile_of(i), v_col))]


def _attn_fwd(qkv, sinks, bias, carried, modes):
    rows = qkv.shape[0]
    tm = _row_tile(rows)
    nbt = tm // BLOCK

    def body(sink_ref, bias_ref, q_ref, kp_ref, kc_ref, vp_ref, vc_ref, o_ref):
        i = pl.program_id(0)
        for b in range(nbt):
            blk = slice(b * BLOCK, (b + 1) * BLOCK)
            bias_t = _bias_of_block(bias_ref, i * nbt + b)
            pieces = []
            for kv in range(KV_HEADS):
                k2 = _keys_of_block(kp_ref, kc_ref, b, kv)
                v2 = _keys_of_block(vp_ref, vc_ref, b, kv)
                q4 = _heads(q_ref, blk, kv * GQA_GROUP, GQA_GROUP)
                pn, _ = _attn_probs(k2, q4, bias_t, _sink_row(sink_ref, kv))
                ot = _mm_tn(v2, pn.astype(BF16))
                pieces += [ot[:, g * BLOCK:(g + 1) * BLOCK] for g in range(GQA_GROUP)]
            o_ref[blk, :] = _from_head_major(pieces).astype(BF16)

    return _hosting_call(
        body, "attn_fwd", rows // tm,
        [pl.BlockSpec(memory_space=pltpu.SMEM), _resident((N_BIAS, 2 * BLOCK, GQA_GROUP * BLOCK))]
        + _attn_specs(tm, lambda i: i),
        [pl.BlockSpec((tm, ATTN_WIDTH), lambda i: (i, 0))],
        [jax.ShapeDtypeStruct((rows, ATTN_WIDTH), BF16)],
        [], (sinks, bias, qkv, qkv, qkv, qkv, qkv), carried, modes)


def _conv_taps(xbuf, tm):
    return [xbuf[pl.ds(SUBLANES - (CONV_WIDTH - 1 - j), tm), :] for j in range(CONV_WIDTH)]


def _lru_halves(xc):
    return [xc[:, h * LRU_HALF:(h + 1) * LRU_HALF].astype(BF16) for h in range(2)]


def _lru_gates(xc, wa_ref, ba_ref, wx_ref, bx_ref, lam_ref):
    halves = _lru_halves(xc)
    gate_r = jnp.concatenate([_mm(halves[h], wa_ref[h]) for h in range(2)], axis=1) + ba_ref[...]
    gate_i = jnp.concatenate([_mm(halves[h], wx_ref[h]) for h in range(2)], axis=1) + bx_ref[...]
    r = _sigmoid(gate_r)
    ig = _sigmoid(gate_i)
    log_a = (-LRU_C) * r * _softplus(-lam_ref[...])
    a = jnp.exp(log_a)
    mult, _ = _sqrt_pos(_one_minus_sq_exp(log_a, a))
    return r, ig, a, mult


KEPT_XC, KEPT_A, KEPT_MULT, KEPT_R, KEPT_I, N_KEPT = 0, 1, 2, 3, 4, 5


def _scan_tile(a_ref, u_ref, out_ref, carry, tm):
    row = lax.broadcasted_iota(jnp.int32, (SUBLANES, LRU_WIDTH), 0)

    def step(j, before):
        o = pl.multiple_of(j * SUBLANES, SUBLANES)
        a = a_ref[pl.ds(o, SUBLANES), :]
        u = u_ref[pl.ds(o, SUBLANES), :]
        for s in (1, 2, 4):
            keep = row >= s
            u = jnp.where(keep, a * pltpu.roll(u, s, 0) + u, u)
            a = jnp.where(keep, a * pltpu.roll(a, s, 0), a)
        out = a * before + u
        out_ref[pl.ds(o, SUBLANES), :] = out
        return out[SUBLANES - 1:SUBLANES, :]

    return lax.fori_loop(0, tm // SUBLANES, step, carry)


def _rec_fwd(zrec, conv_w, conv_b, wa_bd, b_a, wx_bd, b_x, lam, carried, modes):
    rows = zrec.shape[0]
    tm = _row_tile(rows)

    def body(xr_ref, yr_ref, cw_ref, cb_ref, wa_ref, ba_ref, wx_ref, bx_ref, lam_ref, rec_ref, h_ref, kept_ref,
             xbuf, a_s, u_s, carry):
        i = pl.program_id(0)

        @pl.when(i == 0)
        def _():
            xbuf[0:SUBLANES, :] = jnp.zeros((SUBLANES, LRU_WIDTH), F32)
            carry[...] = jnp.zeros_like(carry)

        @pl.when(i > 0)
        def _():
            xbuf[0:SUBLANES, :] = xbuf[tm:tm + SUBLANES, :]

        xbuf[SUBLANES:SUBLANES + tm, :] = xr_ref[...]
        taps = _conv_taps(xbuf, tm)
        xc = cb_ref[...] + sum(cw_ref[j:j + 1, :] * taps[j] for j in range(CONV_WIDTH))
        r, ig, a, mult = _lru_gates(xc, wa_ref, ba_ref, wx_ref, bx_ref, lam_ref)
        for k, val in ((KEPT_XC, xc), (KEPT_A, a), (KEPT_MULT, mult), (KEPT_R, r), (KEPT_I, ig)):
            kept_ref[:, k * LRU_WIDTH:(k + 1) * LRU_WIDTH] = val
        grow = i * tm + lax.broadcasted_iota(jnp.int32, (tm, LRU_WIDTH), 0)
        a_s[...] = a
        u_s[...] = jnp.where(grow >= PAD_ROWS, mult * (ig * xc), 0.0)
        carry[0:1, :] = _scan_tile(a_s, u_s, h_ref, carry[0:1, :], tm)
        gel, _ = _gelu(yr_ref[...])
        rec_ref[...] = (gel * h_ref[...]).astype(BF16)

    vec = _full((1, LRU_WIDTH))
    bd = _full((2, LRU_HALF, LRU_HALF))
    return _hosting_call(
        body, "rec_fwd", rows // tm,
        [pl.BlockSpec((tm, LRU_WIDTH), lambda i: (i, 0)), pl.BlockSpec((tm, LRU_WIDTH), lambda i: (i, 1)),
         _full((CONV_WIDTH, LRU_WIDTH)), vec, bd, vec, bd, vec, vec],
        [pl.BlockSpec((tm, LRU_WIDTH), lambda i: (i, 0))] * 2 + [pl.BlockSpec((tm, N_KEPT * LRU_WIDTH), lambda i: (i, 0))],
        [jax.ShapeDtypeStruct((rows, LRU_WIDTH), BF16), jax.ShapeDtypeStruct((rows, LRU_WIDTH), F32),
         jax.ShapeDtypeStruct((rows, N_KEPT * LRU_WIDTH), F32)],
        [pltpu.VMEM((tm + SUBLANES, LRU_WIDTH), F32), pltpu.VMEM((tm, LRU_WIDTH), F32),
         pltpu.VMEM((tm, LRU_WIDTH), F32), pltpu.VMEM((SUBLANES, LRU_WIDTH), F32)],
        (zrec, zrec, conv_w, conv_b, wa_bd, b_a, wx_bd, b_x, lam), carried, modes)


def _out_proj_fwd(attn, rec, w_out, head, x, g2, carried, modes):
    rows = attn.shape[0]
    tm = _row_tile(rows)
    steps = rows // tm

    def body(attn_ref, rec_ref, w_ref, head_ref, g_ref, x_hbm, mix_ref, h1_ref, buf, sem):
        h0 = _h0_tile(head_ref, x_hbm, buf, sem, pl.program_id(0), steps, tm)
        mix = _mm(attn_ref[...], w_ref[0:ATTN_WIDTH, :]) + _mm(rec_ref[...], w_ref[ATTN_WIDTH:, :])
        y, _, _ = _rms_fwd(mix, g_ref[...])
        mix_ref[...] = mix
        h1_ref[...] = h0 + y

    half = pl.BlockSpec((tm, ATTN_WIDTH), lambda i: (i, 0))
    wide = pl.BlockSpec((tm, D_MODEL), lambda i: (i, 0))
    return _hosting_call(
        body, "out_proj_fwd", steps,
        [half, half, _resident((D_MODEL, D_MODEL)), _full((BLOCK, D_MODEL)), _full((1, D_MODEL)), ANY_SPACE],
        [wide, wide],
        [jax.ShapeDtypeStruct((rows, D_MODEL), F32)] * 2,
        _frame_scratch(tm), (attn, rec, w_out, head, g2, x), carried, modes)


FF_COLS = 1024
FF_HALF = FF_CHUNK // 2


def _hidden_at(d, half):
    return half * (D_FF // 2) + d * FF_HALF


def _ffn_up(h1, g3, w1_halves, carried, modes):
    rows = h1.shape[0]
    tm = _row_tile(rows)

    def body(h_ref, g_ref, wa_ref, wb_ref, act_ref, u_ref):
        u, _, _ = _rms_fwd(h_ref[...], g_ref[...])
        u = u.astype(BF16)
        u_ref[...] = u
        for half, w_ref in enumerate((wa_ref, wb_ref)):
            for d in range(N_DEV):
                c = _hidden_at(d, half)
                a1 = jnp.maximum(_mm(u, w_ref[d]), 0.0)
                act_ref[:, c:c + FF_HALF] = (a1 * a1).astype(BF16)

    wide = pl.BlockSpec((tm, D_MODEL), lambda i: (i, 0))
    return _hosting_call(
        body, "ffn_up", rows // tm,
        [wide, _full((1, D_MODEL))] + [_resident((N_DEV, D_MODEL, FF_HALF))] * 2,
        [pl.BlockSpec((tm, D_FF), lambda i: (i, 0)), wide],
        [jax.ShapeDtypeStruct((rows, D_FF), BF16), jax.ShapeDtypeStruct((rows, D_MODEL), BF16)],
        [], (h1, g3, *w1_halves), carried, modes)


def _ffn_down_loss(act, w2_halves, h1, target, g4, carried, modes):
    rows = h1.shape[0]
    tm = _row_tile(rows)
    steps = rows // tm
    kh = D_FF // 2

    def body(act_ref, wa_ref, wb_ref, h_ref, g_ref, t_hbm, dy_ref, df_ref, dg_ref, loss_ref, buf, sem):
        i = pl.program_id(0)
        slot = _frame_rows(t_hbm, buf, sem, i, steps, tm)

        @pl.when(i == 0)
        def _():
            dg_ref[...] = jnp.zeros_like(dg_ref)
            loss_ref[...] = jnp.zeros_like(loss_ref)
            buf[0, 0:BLOCK, :] = jnp.zeros((BLOCK, D_MODEL), F32)

        g = g_ref[...]
        f = _mm(act_ref[:, :kh], wa_ref[...]) + _mm(act_ref[:, kh:], wb_ref[...])
        y, fhat, rstd = _rms_fwd(f, g)
        grow = i * tm + lax.broadcasted_iota(jnp.int32, (tm, D_MODEL), 0)
        err = jnp.where(grow >= BLOCK, h_ref[...] + y - buf[slot], 0.0)
        loss_ref[...] += (0.5 / D_MODEL) * jnp.sum(err * err)
        dy = err * (1.0 / D_MODEL)
        df, dg = _rms_bwd(dy, fhat, rstd, g)
        dy_ref[...] = dy
        df_ref[...] = df.astype(BF16)
        dg_ref[...] += dg

    wide = pl.BlockSpec((tm, D_MODEL), lambda i: (i, 0))
    return _hosting_call(
        body, "ffn_down_loss", steps,
        [pl.BlockSpec((tm, D_FF), lambda i: (i, 0)), _resident((kh, D_MODEL)), _resident((kh, D_MODEL)), wide,
         _full((1, D_MODEL)), ANY_SPACE],
        [wide, wide, _full((1, D_MODEL)), _full((SUBLANES, LANES))],
        [jax.ShapeDtypeStruct((rows, D_MODEL), F32), jax.ShapeDtypeStruct((rows, D_MODEL), BF16),
         jax.ShapeDtypeStruct((1, D_MODEL), F32), jax.ShapeDtypeStruct((SUBLANES, LANES), F32)],
        _frame_scratch(tm), (act, *w2_halves, h1, g4, target), carried, modes)


def _ffn_bwd_act(df, w2t_halves, act, carried, modes):
    rows = df.shape[0]
    tm = _row_tile(rows)

    def body(df_ref, wa_ref, wb_ref, act_ref, da_ref):
        df_t = df_ref[...]
        for half, w_ref in enumerate((wa_ref, wb_ref)):
            for d in range(N_DEV):
                cols = slice(_hidden_at(d, half), _hidden_at(d, half) + FF_HALF)
                dact = _mm(df_t, w_ref[d])
                relu_a1, _ = _sqrt_pos(act_ref[:, cols].astype(F32))
                da_ref[:, cols] = (dact * (2.0 * relu_a1)).astype(BF16)

    hidden = pl.BlockSpec((tm, D_FF), lambda i: (i, 0))
    return _hosting_call(
        body, "ffn_bwd_act", rows // tm,
        [pl.BlockSpec((tm, D_MODEL), lambda i: (i, 0))] + [_resident((N_DEV, D_MODEL, FF_HALF))] * 2 + [hidden],
        [hidden],
        [jax.ShapeDtypeStruct((rows, D_FF), BF16)],
        [], (df, *w2t_halves, act), carried, modes)


def _ffn_bwd_x(da, w1t_halves, h1, dy, g3, carried, modes):
    rows = h1.shape[0]
    tm = _row_tile(rows)
    kh = D_FF // 2

    def body(da_ref, wa_ref, wb_ref, h_ref, dy_ref, g_ref, dh_ref, dg_ref):
        @pl.when(pl.program_id(0) == 0)
        def _():
            dg_ref[...] = jnp.zeros_like(dg_ref)

        g = g_ref[...]
        _, xhat, rstd = _rms_fwd(h_ref[...], g)
        du = _mm(da_ref[:, :kh], wa_ref[...]) + _mm(da_ref[:, kh:], wb_ref[...])
        dx, dg = _rms_bwd(du, xhat, rstd, g)
        dh_ref[...] = dy_ref[...] + dx
        dg_ref[...] += dg

    wide = pl.BlockSpec((tm, D_MODEL), lambda i: (i, 0))
    return _hosting_call(
        body, "ffn_bwd_x", rows // tm,
        [pl.BlockSpec((tm, D_FF), lambda i: (i, 0)), _resident((kh, D_MODEL)), _resident((kh, D_MODEL)), wide, wide,
         _full((1, D_MODEL))],
        [wide, _full((1, D_MODEL))],
        [jax.ShapeDtypeStruct((rows, D_MODEL), F32), jax.ShapeDtypeStruct((1, D_MODEL), F32)],
        [], (da, *w1t_halves, h1, dy, g3), carried, modes)


def _ffn_bwd_weights(u2, da, act, df, carried, modes):
    rows = u2.shape[0]
    tb = _big_tile(rows)
    steps = rows // tb
    per = FF_COLS // FF_HALF

    def body(u_ref, da_ref, act_ref, df_ref, dw1_ref, dw2_ref, acc1, acc2):
        i = pl.program_id(1)

        @pl.when(i == 0)
        def _():
            acc1[...] = jnp.zeros_like(acc1)
            acc2[...] = jnp.zeros_like(acc2)

        acc1[...] += _mm_tn(u_ref[...], da_ref[...])
        acc2[...] += _mm_tn(act_ref[...], df_ref[...])

        @pl.when(i == steps - 1)
        def _():
            for p in range(per):
                c = p * FF_HALF
                dw1_ref[p] = acc1[:, c:c + FF_HALF].astype(BF16)
                dw2_ref[p] = acc2[c:c + FF_HALF, :].astype(BF16)

    wide = pl.BlockSpec((tb, D_MODEL), lambda j, i: (i, 0))
    chunk = pl.BlockSpec((tb, FF_COLS), lambda j, i: (i, j))
    return _hosting_call(
        body, "ffn_bwd_weights", (D_FF // FF_COLS, steps),
        [wide, chunk, chunk, wide],
        [pl.BlockSpec((None, per, D_MODEL, FF_HALF), lambda j, i: (j // 2, j % 2, 0, 0)),
         pl.BlockSpec((per, FF_HALF, D_MODEL), lambda j, i: (j % 2, j // 2, 0))],
        [jax.ShapeDtypeStruct((2, N_DEV, D_MODEL, FF_HALF), BF16), jax.ShapeDtypeStruct((N_DEV, FF_CHUNK, D_MODEL), BF16)],
        [pltpu.VMEM((D_MODEL, FF_COLS), F32), pltpu.VMEM((FF_COLS, D_MODEL), F32)],
        (u2, da, act, df), carried, modes)


def _out_proj_bwd(dh1, mix, g2, w_out_t, attn, rec, carried, modes):
    rows = dh1.shape[0]
    tm = _row_tile(rows)
    steps = rows // tm

    def body(dh_ref, mix_ref, g_ref, w_ref, attn_ref, rec_ref, dattn_ref, drec_ref, dw_ref, dg_ref, acc):
        i = pl.program_id(0)

        @pl.when(i == 0)
        def _():
            acc[...] = jnp.zeros_like(acc)
            dg_ref[...] = jnp.zeros_like(dg_ref)

        g = g_ref[...]
        _, xhat, rstd = _rms_fwd(mix_ref[...], g)
        dmix, dg = _rms_bwd(dh_ref[...], xhat, rstd, g)
        dmix = dmix.astype(BF16)
        dg_ref[...] += dg
        din = _mm(dmix, w_ref[...])
        dattn_ref[...] = din[:, :ATTN_WIDTH].astype(BF16)
        drec_ref[...] = din[:, ATTN_WIDTH:]
        acc[0:ATTN_WIDTH, :] += _mm_tn(attn_ref[...], dmix)
        acc[ATTN_WIDTH:, :] += _mm_tn(rec_ref[...], dmix)

        @pl.when(i == steps - 1)
        def _():
            dw_ref[...] = acc[...].astype(BF16)

    half = pl.BlockSpec((tm, ATTN_WIDTH), lambda i: (i, 0))
    wide = pl.BlockSpec((tm, D_MODEL), lambda i: (i, 0))
    return _hosting_call(
        body, "out_proj_bwd", steps,
        [wide, wide, _full((1, D_MODEL)), _resident((D_MODEL, D_MODEL)), half, half],
        [half, half, _full((D_MODEL, D_MODEL)), _full((1, D_MODEL))],
        [jax.ShapeDtypeStruct((rows, ATTN_WIDTH), BF16), jax.ShapeDtypeStruct((rows, LRU_WIDTH), F32),
         jax.ShapeDtypeStruct((D_MODEL, D_MODEL), BF16), jax.ShapeDtypeStruct((1, D_MODEL), F32)],
        [pltpu.VMEM((D_MODEL, D_MODEL), F32)],
        (dh1, mix, g2, w_out_t, attn, rec), carried, modes)


def _attn_bwd(qkv, dattn, sinks, bias, carried, modes):
    rows = qkv.shape[0]
    tm = _row_tile(rows)
    nbt, nt = tm // BLOCK, rows // tm

    def body(sink_ref, bias_ref, do_ref, q_ref, kp_ref, kc_ref, vp_ref, vc_ref, dq_ref, dkv_ref, dsink_ref, dk_c, dv_c):
        i = pl.program_id(0)

        @pl.when(i == 0)
        def _():
            dk_c[...] = jnp.zeros_like(dk_c)
            dv_c[...] = jnp.zeros_like(dv_c)
            dsink_ref[...] = jnp.zeros_like(dsink_ref)

        @pl.when(i < nt)
        def _():
            dk_late, dv_late = dk_c[...], dv_c[...]
            dsink_rows = [jnp.zeros((1, LANES), F32)] * ATTN_HEADS
            for b in range(nbt):
                blk = slice(b * BLOCK, (b + 1) * BLOCK)
                bias_t = _bias_of_block(bias_ref, i * nbt + b)
                dq_parts, dk_parts, dv_parts = [], [], []
                for kv in range(KV_HEADS):
                    k2 = _keys_of_block(kp_ref, kc_ref, b, kv)
                    v2 = _keys_of_block(vp_ref, vc_ref, b, kv)
                    q4 = _heads(q_ref, blk, kv * GQA_GROUP, GQA_GROUP)
                    do4 = _heads(do_ref, blk, kv * GQA_GROUP, GQA_GROUP)
                    pn, psink = _attn_probs(k2, q4, bias_t, _sink_row(sink_ref, kv))
                    dpn = _mm_nt(v2, do4)
                    delta = jnp.sum(pn * dpn, axis=0, keepdims=True)
                    ds = ((pn * (dpn - delta)) * (HEAD_DIM ** -0.5)).astype(BF16)
                    dqt = _mm_tn(k2, ds)
                    dq_parts += [dqt[:, g * BLOCK:(g + 1) * BLOCK] for g in range(GQA_GROUP)]
                    dk_parts.append(_mm(ds, q4))
                    dv_parts.append(_mm(pn.astype(BF16), do4))
                    sd = psink * delta
                    for g in range(GQA_GROUP):
                        h = kv * GQA_GROUP + g
                        dsink_rows[h] = dsink_rows[h] - jnp.sum(sd[:, g * BLOCK:(g + 1) * BLOCK])
                dq_ref[blk, :] = _from_head_major(dq_parts).astype(BF16)
                dk2 = jnp.concatenate(dk_parts, axis=1)
                dv2 = jnp.concatenate(dv_parts, axis=1)
                dkv_ref[blk, 0:KV_WIDTH] = (dk_late + dk2[0:BLOCK]).astype(BF16)
                dkv_ref[blk, KV_WIDTH:] = (dv_late + dv2[0:BLOCK]).astype(BF16)
                dk_late, dv_late = dk2[BLOCK:], dv2[BLOCK:]
            dk_c[...] = dk_late
            dv_c[...] = dv_late
            dsink_ref[...] += jnp.concatenate(dsink_rows, axis=0)

        @pl.when(i == nt)
        def _():
            dkv_ref[...] = jnp.zeros_like(dkv_ref)
            dkv_ref[0:BLOCK, 0:KV_WIDTH] = dk_c[...].astype(BF16)
            dkv_ref[0:BLOCK, KV_WIDTH:] = dv_c[...].astype(BF16)

    tile_of = lambda i: jnp.minimum(i, nt - 1)
    tile = pl.BlockSpec((tm, ATTN_WIDTH), lambda i: (tile_of(i), 0))
    return _hosting_call(
        body, "attn_bwd", nt + 1,
        [pl.BlockSpec(memory_space=pltpu.SMEM), _resident((N_BIAS, 2 * BLOCK, GQA_GROUP * BLOCK)), tile]
        + _attn_specs(tm, tile_of),
        [tile, pl.BlockSpec((tm, 2 * KV_WIDTH), lambda i: (i, 0)), _full((ATTN_HEADS, LANES))],
        [jax.ShapeDtypeStruct((rows, ATTN_WIDTH), BF16), jax.ShapeDtypeStruct((rows + tm, 2 * KV_WIDTH), BF16),
         jax.ShapeDtypeStruct((ATTN_HEADS, LANES), F32)],
        [pltpu.VMEM((BLOCK, KV_WIDTH), F32), pltpu.VMEM((BLOCK, KV_WIDTH), F32)],
        (sinks, bias, dattn, qkv, qkv, qkv, qkv, qkv), carried, modes)


ROW_CONV_B, ROW_B_A, ROW_B_X, ROW_LAMBDA = 4, 5, 6, 7


def _rec_bwd(drec, zrec, h, kept, conv_w, wa_bd, wx_bd, lam, carried, modes):
    rows = zrec.shape[0]
    tm = _rec_tile(rows)
    nt = rows // tm
    per = tm // SUBLANES

    def body(drec_ref, xr_ref, yr_ref, h_ref, xc_ref, a_ref, mult_ref, r_ref, ig_ref, hhalo_ref, cw_ref, wa_ref, wx_ref,
             lam_ref, drz_ref, small_ref, dwa_ref, dwx_ref, hbuf, dbuf, dgr_s, dgi_s, dyr_s, carry):
        s = pl.program_id(0)
        i = nt - 1 - s

        @pl.when(s == 0)
        def _():
            small_ref[...] = jnp.zeros_like(small_ref)
            dwa_ref[...] = jnp.zeros_like(dwa_ref)
            dwx_ref[...] = jnp.zeros_like(dwx_ref)
            carry[...] = jnp.zeros_like(carry)
            dbuf[tm:tm + SUBLANES, :] = jnp.zeros((SUBLANES, LRU_WIDTH), F32)

        hbuf[0:SUBLANES, :] = jnp.where(i == 0, 0.0, hhalo_ref[...])
        hbuf[SUBLANES:SUBLANES + tm, :] = h_ref[...]

        row = lax.broadcasted_iota(jnp.int32, (SUBLANES, LRU_WIDTH), 0)
        log_a_scale = (-LRU_C) * _softplus(-lam_ref[...])
        zeros = jnp.zeros((SUBLANES, LRU_WIDTH), F32)

        def group(k, state):
            g_later, a_later, sum_dgr, sum_dgi, sum_lam = state
            o = pl.multiple_of((per - 1 - k) * SUBLANES, SUBLANES)
            rows8 = pl.ds(o, SUBLANES)
            yr, drec_t, h_t, a = yr_ref[rows8, :], drec_ref[rows8, :], h_ref[rows8, :], a_ref[rows8, :]
            gel, t = _gelu(yr)
            dyr_s[rows8, :] = drec_t * h_t * _gelu_grad(yr, t)
            u = drec_t * gel
            coef = jnp.where(row == SUBLANES - 1, a_later, pltpu.roll(a, SUBLANES - 1, 0))
            for sft in (1, 2, 4):
                keep = row < SUBLANES - sft
                u = jnp.where(keep, coef * pltpu.roll(u, SUBLANES - sft, 0) + u, u)
                coef = jnp.where(keep, coef * pltpu.roll(coef, SUBLANES - sft, 0), coef)
            g = coef * g_later + u
            du = jnp.where(i * tm + o + row >= PAD_ROWS, g, 0.0)
            h_before = jnp.where(row == 0, hbuf[rows8, :][SUBLANES - 1:SUBLANES, :], pltpu.roll(h_t, 1, 0))
            xc, mult, r, ig = xc_ref[rows8, :], mult_ref[rows8, :], r_ref[rows8, :], ig_ref[rows8, :]
            dbuf[rows8, :] = du * (mult * ig)
            dgi = (du * (mult * xc)) * (ig * (1.0 - ig))
            dgi_s[rows8, :] = dgi
            dlog_a = (g * h_before) * a - (du * (ig * xc)) * (a * a * pl.reciprocal(mult, approx=True))
            dgr = (dlog_a * log_a_scale) * (r * (1.0 - r))
            dgr_s[rows8, :] = dgr
            return g[0:1, :], a[0:1, :], sum_dgr + dgr, sum_dgi + dgi, sum_lam + dlog_a * r

        state = lax.fori_loop(0, per, group, (carry[0:1, :], carry[1:2, :], zeros, zeros, zeros))
        carry[0:1, :], carry[1:2, :] = state[0], state[1]
        sum_dgr, sum_dgi, sum_lam = (jnp.sum(v, axis=0, keepdims=True) for v in state[2:])
        dlam = sum_lam * (LRU_C * _sigmoid(-lam_ref[...]))

        dgr_b = [dgr_s[:, hh * LRU_HALF:(hh + 1) * LRU_HALF].astype(BF16) for hh in range(2)]
        dgi_b = [dgi_s[:, hh * LRU_HALF:(hh + 1) * LRU_HALF].astype(BF16) for hh in range(2)]
        halves = _lru_halves(xc_ref[...])
        for hh in range(2):
            dwa_ref[hh] += _mm_tn(halves[hh], dgr_b[hh])
            dwx_ref[hh] += _mm_tn(halves[hh], dgi_b[hh])
        dxc = dbuf[0:tm, :] + jnp.concatenate(
            [_mm_nt(dgr_b[hh], wa_ref[hh]) + _mm_nt(dgi_b[hh], wx_ref[hh]) for hh in range(2)], axis=1)

        dbuf[0:tm, :] = dxc
        sum_dxc = jnp.sum(dxc, axis=0, keepdims=True)
        ahead = [dbuf[pl.ds(CONV_WIDTH - 1 - j, tm), :] for j in range(CONV_WIDTH)]
        drz_ref[:, 0:LRU_WIDTH] = sum(cw_ref[j:j + 1, :] * ahead[j] for j in range(CONV_WIDTH)).astype(BF16)
        drz_ref[:, LRU_WIDTH:] = dyr_s[...].astype(BF16)
        upd = [jnp.sum(xr_ref[...] * ahead[j], axis=0, keepdims=True) for j in range(CONV_WIDTH)]
        dbuf[tm:tm + SUBLANES, :] = dbuf[0:SUBLANES, :]
        small_ref[...] += jnp.concatenate(upd + [sum_dxc, sum_dgr, sum_dgi, dlam], axis=0)

    rev = lambda s: nt - 1 - s
    halo = lambda s: jnp.maximum(rev(s) * per - 1, 0)
    cols = lambda k: pl.BlockSpec((tm, LRU_WIDTH), lambda s: (rev(s), k))
    halo0 = pl.BlockSpec((SUBLANES, LRU_WIDTH), lambda s: (halo(s), 0))
    bd = _full((2, LRU_HALF, LRU_HALF))
    big = pltpu.VMEM((tm + SUBLANES, LRU_WIDTH), F32)
    tile = pltpu.VMEM((tm, LRU_WIDTH), F32)
    kept_cols = [cols(k) for k in (KEPT_XC, KEPT_A, KEPT_MULT, KEPT_R, KEPT_I)]
    return _hosting_call(
        body, "rec_bwd", nt,
        [cols(0), cols(0), cols(1), cols(0)] + kept_cols
        + [halo0, _full((CONV_WIDTH, LRU_WIDTH)), bd, bd, _full((1, LRU_WIDTH))],
        [pl.BlockSpec((tm, 2 * LRU_WIDTH), lambda s: (rev(s), 0)), _full((SUBLANES, LRU_WIDTH)), bd, bd],
        [jax.ShapeDtypeStruct((rows, 2 * LRU_WIDTH), BF16), jax.ShapeDtypeStruct((SUBLANES, LRU_WIDTH), F32),
         jax.ShapeDtypeStruct((2, LRU_HALF, LRU_HALF), F32), jax.ShapeDtypeStruct((2, LRU_HALF, LRU_HALF), F32)],
        [big, big, tile, tile, tile, pltpu.VMEM((SUBLANES, LRU_WIDTH), F32)],
        (drec, zrec, zrec, h) + (kept,) * N_KEPT + (h, conv_w, wa_bd, wx_bd, lam), carried, modes)


DZ_CUTS = (0, ATTN_WIDTH, QKV_WIDTH, IN_WIDTH)


def _dz_specs(tm):
    return [pl.BlockSpec((tm, DZ_CUTS[p + 1] - DZ_CUTS[p]), lambda i: (i, 0)) for p in range(3)]


def _in_proj_bwd_x(head, x, g1, dh1, dq, dkv, drz, w_in_t, carried, modes):
    rows = dh1.shape[0]
    tm = _row_tile(rows)
    steps = rows // tm

    def body(head_ref, g_ref, dh1_ref, dq_ref, dkv_ref, drz_ref, w_ref, x_hbm, dh0_ref, dg_ref, buf, sem):
        i = pl.program_id(0)
        h0 = _h0_tile(head_ref, x_hbm, buf, sem, i, steps, tm)

        @pl.when(i == 0)
        def _():
            dg_ref[...] = jnp.zeros_like(dg_ref)

        g = g_ref[...]
        _, xhat, rstd = _rms_fwd(h0, g)
        parts = (dq_ref[...], dkv_ref[...], drz_ref[...])
        du = sum(_mm(parts[p], w_ref[DZ_CUTS[p]:DZ_CUTS[p + 1], :]) for p in range(3))
        dx, dg = _rms_bwd(du, xhat, rstd, g)
        dh0_ref[...] = dh1_ref[...] + dx
        dg_ref[...] += dg

    wide = pl.BlockSpec((tm, D_MODEL), lambda i: (i, 0))
    return _hosting_call(
        body, "in_proj_bwd_x", steps,
        [_full((BLOCK, D_MODEL)), _full((1, D_MODEL)), wide] + _dz_specs(tm) + [_resident((IN_WIDTH, D_MODEL)), ANY_SPACE],
        [wide, _full((1, D_MODEL))],
        [jax.ShapeDtypeStruct((rows, D_MODEL), F32), jax.ShapeDtypeStruct((1, D_MODEL), F32)],
        _frame_scratch(tm), (head, g1, dh1, dq, dkv, drz, w_in_t, x), carried, modes)


def _in_proj_bwd_w(u1, dq, dkv, drz, carried, modes):
    rows = u1.shape[0]
    tb = _big_tile(rows)
    steps = rows // tb

    def body(u_ref, dq_ref, dkv_ref, drz_ref, dw_ref, acc):
        i = pl.program_id(0)

        @pl.when(i == 0)
        def _():
            acc[...] = jnp.zeros_like(acc)

        u = u_ref[...]
        for p, ref in enumerate((dq_ref, dkv_ref, drz_ref)):
            acc[:, DZ_CUTS[p]:DZ_CUTS[p + 1]] += _mm_tn(u, ref[...])

        @pl.when(i == steps - 1)
        def _():
            dw_ref[...] = acc[...].astype(BF16)

    return _hosting_call(
        body, "in_proj_bwd_w", steps,
        [pl.BlockSpec((tb, D_MODEL), lambda i: (i, 0))] + _dz_specs(tb),
        [_full((D_MODEL, IN_WIDTH))],
        [jax.ShapeDtypeStruct((D_MODEL, IN_WIDTH), BF16)],
        [pltpu.VMEM((D_MODEL, IN_WIDTH), F32)], (u1, dq, dkv, drz), carried, modes)


def _adamw_math(w, m, v, g):
    nm = ADAM_B1 * m + (1.0 - ADAM_B1) * g
    nv = ADAM_B2 * v + (1.0 - ADAM_B2) * (g * g)
    m_hat = nm / (1.0 - ADAM_B1 ** ADAM_STEP)
    v_hat = nv / (1.0 - ADAM_B2 ** ADAM_STEP)
    return (-ADAM_LR) * (m_hat / (jnp.sqrt(v_hat) + ADAM_EPS) + ADAM_WD * w), nm, nv


SMALL_NAMES = ("conv_b", "b_a", "b_x", "lru_lambda", "attn_sinks", "g_post_mix", "g_pre_ffn", "g_post_ffn")
PACK_WIDTH = 1024


def _pack_rows(vals):
    assert len(SMALL_NAMES) == SUBLANES
    row = lax.broadcasted_iota(jnp.int32, (SUBLANES, PACK_WIDTH), 0)
    tile = jnp.zeros((SUBLANES, PACK_WIDTH), F32)
    for k, name in enumerate(SMALL_NAMES):
        a = vals[name].reshape(1, -1)
        tile = jnp.where(row == k, jnp.pad(a, ((0, 0), (0, PACK_WIDTH - a.shape[1]))), tile)
    return tile


def _adamw_small(weights, mom_m, mom_v, parts, loss_parts):
    n = len(SMALL_NAMES)
    views = [(1, weights[name].size) for name in SMALL_NAMES]

    def body(*refs):
        w_refs, m_refs, v_refs = refs[:n], refs[n:2 * n], refs[2 * n:3 * n]
        p_ref, l_ref, loss_ref = refs[3 * n], refs[3 * n + 1], refs[3 * n + 2]
        outs = refs[3 * n + 3:]
        for k, (_, c) in enumerate(views):
            g = p_ref[0, k:k + 1, 0:c]
            for s in range(1, N_DEV):
                g = g + p_ref[s, k:k + 1, 0:c]
            g_ref, d_ref, nm_ref, nv_ref = outs[4 * k:4 * k + 4]
            g_ref[...] = g
            d_ref[...], nm_ref[...], nv_ref[...] = _adamw_math(w_refs[k][...], m_refs[k][...], v_refs[k][...], g)
        total = l_ref[0]
        for s in range(1, N_DEV):
            total = total + l_ref[s]
        loss_ref[...] = total

    args = [src[name].reshape(view) for src in (weights, mom_m, mom_v) for name, view in zip(SMALL_NAMES, views)]
    res = pl.pallas_call(
        body, name="adamw_small",
        out_shape=[jax.ShapeDtypeStruct(loss_parts.shape[1:], F32)]
                  + [jax.ShapeDtypeStruct(view, F32) for view in views for _ in range(4)],
        compiler_params=pltpu.CompilerParams(vmem_limit_bytes=VMEM_LIMIT),
    )(*args, parts, loss_parts)
    out = {name: tuple(t.reshape(weights[name].shape) for t in res[1 + 4 * k:5 + 4 * k]) for k, name in enumerate(SMALL_NAMES)}
    return res[0], out


def _adamw(w, m, v, parts, name):
    rows, cols = w.shape
    tr = next((t for t in (256, 128) if rows % t == 0), rows)
    parts = parts if isinstance(parts, (list, tuple)) else [parts]

    def body(w_ref, m_ref, v_ref, *refs):
        p_refs, (g_ref, d_ref, nm_ref, nv_ref) = refs[:len(parts)], refs[len(parts):]

        def total(p_ref):
            g = p_ref[0].astype(F32)
            for s in range(1, N_DEV):
                g = g + p_ref[s].astype(F32)
            return g

        g = jnp.concatenate([total(p_ref) for p_ref in p_refs], axis=1) if len(parts) > 1 else total(p_refs[0])
        g_ref[...] = g
        d_ref[...], nm_ref[...], nv_ref[...] = _adamw_math(w_ref[...], m_ref[...], v_ref[...], g)

    blk = pl.BlockSpec((tr, cols), lambda i: (i, 0))
    return pl.pallas_call(
        body, name=name, grid=(rows // tr,),
        in_specs=[blk, blk, blk] + [pl.BlockSpec((N_DEV, tr, p.shape[2]), lambda i: (0, i, 0)) for p in parts],
        out_specs=[blk] * 4,
        out_shape=[jax.ShapeDtypeStruct((rows, cols), F32)] * 4,
        compiler_params=_params(("parallel",)),
    )(w, m, v, *parts)


def _cols_from_shards(g):
    return jnp.transpose(g, (1, 0, 2)).reshape(g.shape[1], N_DEV * g.shape[2])


def _cols_to_shards(a):
    r, c = a.shape
    return jnp.transpose(a.reshape(r, N_DEV, c // N_DEV), (1, 0, 2))


def _block_diag(w):
    per = LRU_HALF // LRU_BLOCK
    w = w.reshape(2, per, LRU_BLOCK, LRU_BLOCK)
    eye = jnp.eye(per, dtype=w.dtype)
    return (w[:, :, :, None, :] * eye[None, :, None, :, None]).reshape(2, LRU_HALF, LRU_HALF)


def _block_diag_extract(t):
    per = LRU_HALF // LRU_BLOCK
    t = t.reshape(2, per, LRU_BLOCK, per, LRU_BLOCK)
    return jnp.stack([t[:, b, :, b, :] for b in range(per)], axis=1).reshape(LRU_BLOCKS, LRU_BLOCK, LRU_BLOCK)


def kernel(x, meta_tokens, g_pre_mix, w_in, conv_w, conv_b, w_a, b_a, w_x, b_x, lru_lambda, attn_sinks, w_out, g_post_mix, g_pre_ffn, w_ff1, w_ff2, g_post_ffn, loss_target, m_meta_tokens, m_g_pre_mix, m_w_in, m_conv_w, m_conv_b, m_w_a, m_b_a, m_w_x, m_b_x, m_lru_lambda, m_attn_sinks, m_w_out, m_g_post_mix, m_g_pre_ffn, m_w_ff1, m_w_ff2, m_g_post_ffn, v_meta_tokens, v_g_pre_mix, v_w_in, v_conv_w, v_conv_b, v_w_a, v_b_a, v_w_x, v_b_x, v_lru_lambda, v_attn_sinks, v_w_out, v_g_post_mix, v_g_pre_ffn, v_w_ff1, v_w_ff2, v_g_post_ffn):
    weights = dict(meta_tokens=meta_tokens, g_pre_mix=g_pre_mix, w_in=w_in, conv_w=conv_w, conv_b=conv_b, w_a=w_a,
                   b_a=b_a, w_x=w_x, b_x=b_x, lru_lambda=lru_lambda, attn_sinks=attn_sinks, w_out=w_out,
                   g_post_mix=g_post_mix, g_pre_ffn=g_pre_ffn, w_ff1=w_ff1, w_ff2=w_ff2, g_post_ffn=g_post_ffn)
    mom_m = dict(meta_tokens=m_meta_tokens, g_pre_mix=m_g_pre_mix, w_in=m_w_in, conv_w=m_conv_w, conv_b=m_conv_b,
                 w_a=m_w_a, b_a=m_b_a, w_x=m_w_x, b_x=m_b_x, lru_lambda=m_lru_lambda, attn_sinks=m_attn_sinks,
                 w_out=m_w_out, g_post_mix=m_g_post_mix, g_pre_ffn=m_g_pre_ffn, w_ff1=m_w_ff1, w_ff2=m_w_ff2,
                 g_post_ffn=m_g_post_ffn)
    mom_v = dict(meta_tokens=v_meta_tokens, g_pre_mix=v_g_pre_mix, w_in=v_w_in, conv_w=v_conv_w, conv_b=v_conv_b,
                 w_a=v_w_a, b_a=v_b_a, w_x=v_w_x, b_x=v_b_x, lru_lambda=v_lru_lambda, attn_sinks=v_attn_sinks,
                 w_out=v_w_out, g_post_mix=v_g_post_mix, g_pre_ffn=v_g_pre_ffn, w_ff1=v_w_ff1, w_ff2=v_w_ff2,
                 g_post_ffn=v_g_post_ffn)
    order = list(weights)

    (g_win, g_meta, g_cw) = _gather_two_level([w_in[0].astype(BF16), meta_tokens, conv_w[0]], "gather_first")
    w_in_full = _cols_from_shards(g_win)
    meta_full = _cols_from_shards(g_meta)
    conv_w_full = _cols_from_shards(g_cw)

    head = jnp.concatenate([jnp.zeros((PAD_ROWS, D_MODEL), F32), meta_full], axis=0)
    wa_bd = _block_diag(w_a[0]).astype(BF16)
    wx_bd = _block_diag(w_x[0]).astype(BF16)
    bias = _attn_bias()

    w1_shard = w_ff1[0].astype(BF16)
    (qkv, zrec, u1), (g_wout,) = _in_proj_fwd(head, x[0], g_pre_mix, w_in_full, [w_out[0].astype(BF16)], ["gather"])
    (attn,), (w1a,) = _attn_fwd(qkv, attn_sinks, bias, [w1_shard[:, :FF_HALF]], ["gather"])
    (rec, h_lru, kept), (w1b,) = _rec_fwd(zrec, conv_w_full, conv_b, wa_bd, b_a, wx_bd, b_x, lru_lambda,
                                         [w1_shard[:, FF_HALF:]], ["gather"])
    w_out_full = g_wout.reshape(D_MODEL, D_MODEL)
    w2_shard = w_ff2[0].astype(BF16)
    (mix, h1), (w2a,) = _out_proj_fwd(attn, rec, w_out_full, head, x[0], g_post_mix, [w2_shard[:FF_HALF]], ["gather"])
    (act, u2), (w2b,) = _ffn_up(h1, g_pre_ffn, (w1a, w1b), [w2_shard[FF_HALF:]], ["gather"])
    w2_halves = [w.reshape(D_FF // 2, D_MODEL) for w in (w2a, w2b)]
    (dy, df, dg_post_ffn, loss_acc), w2t_halves = _ffn_down_loss(
        act, w2_halves, h1, loss_target[0], g_post_ffn, [w2_shard[:FF_HALF].T, w2_shard[FF_HALF:].T], ["gather"] * 2)

    (da1,), (w1ta,) = _ffn_bwd_act(df, w2t_halves, act, [w1_shard[:, :FF_HALF].T], ["gather"])
    (dw1h, dw2g), (w1tb,) = _ffn_bwd_weights(u2, da1, act, df, [w1_shard[:, FF_HALF:].T], ["gather"])
    w1t_halves = [w.reshape(D_FF // 2, D_MODEL) for w in (w1ta, w1tb)]
    (dh1, dg_pre_ffn), (p_w1a,) = _ffn_bwd_x(da1, w1t_halves, h1, dy, g_pre_ffn, [dw1h[0]], ["scatter"])
    (dattn, drec, dw_out, dg_post_mix), (p_w1b,) = _out_proj_bwd(dh1, mix, g_post_mix, w_out_full.T, attn, rec,
                                                                [dw1h[1]], ["scatter"])
    (dq, dkv_late, dsinks), (p_w2,) = _attn_bwd(qkv, dattn, attn_sinks, bias, [dw2g], ["scatter"])
    dkv = dkv_late[BLOCK:BLOCK + qkv.shape[0]]
    (drz, rec_small, dwa_bd, dwx_bd), (p_wout,) = _rec_bwd(
        drec, zrec, h_lru, kept, conv_w_full, wa_bd, wx_bd, lru_lambda,
        [dw_out.reshape(N_DEV, D_MODEL // N_DEV, D_MODEL)], ["scatter"])
    small_grads = dict(
        conv_b=rec_small[ROW_CONV_B], b_a=rec_small[ROW_B_A], b_x=rec_small[ROW_B_X], lru_lambda=rec_small[ROW_LAMBDA],
        attn_sinks=dsinks[:, 0], g_post_mix=dg_post_mix, g_pre_ffn=dg_pre_ffn, g_post_ffn=dg_post_ffn)
    gate_rows = (LRU_BLOCKS * LRU_BLOCK, LRU_BLOCK)
    gate_dense = (LRU_BLOCKS * LRU_BLOCK * LRU_BLOCK // PACK_WIDTH, PACK_WIDTH)
    (dw_in,), (p_cw, p_small, p_wa, p_wx) = _in_proj_bwd_w(
        u1, dq, dkv, drz,
        [_cols_to_shards(rec_small[0:CONV_WIDTH]), _pack_rows(small_grads),
         _block_diag_extract(dwa_bd).reshape(gate_dense), _block_diag_extract(dwx_bd).reshape(gate_dense)],
        ["scatter", "gather", "gather", "gather"])
    p_wa, p_wx = (p.reshape((N_DEV,) + gate_rows) for p in (p_wa, p_wx))
    (dh0, dg_pre_mix), (p_win,) = _in_proj_bwd_x(
        head, x[0], g_pre_mix, dh1, dq, dkv, drz, w_in_full.T, [_cols_to_shards(dw_in)], ["scatter"])
    p_meta, p_gpm, p_loss = _exchange([_cols_to_shards(dh0[PAD_ROWS:BLOCK]), dg_pre_mix, loss_acc],
                                      ["scatter", "gather", "gather"], "exchange_last")

    res = {}
    res["g_pre_mix"] = _adamw(g_pre_mix, m_g_pre_mix, v_g_pre_mix, p_gpm, "adamw_g_pre_mix")
    res["w_in"] = _adamw(w_in[0], m_w_in[0], v_w_in[0], p_win, "adamw_w_in")
    res["w_out"] = _adamw(w_out[0], m_w_out[0], v_w_out[0], p_wout, "adamw_w_out")
    res["w_ff1"] = _adamw(w_ff1[0], m_w_ff1[0], v_w_ff1[0], [p_w1a, p_w1b], "adamw_w_ff1")
    res["w_ff2"] = _adamw(w_ff2[0], m_w_ff2[0], v_w_ff2[0], p_w2, "adamw_w_ff2")
    res["meta_tokens"] = _adamw(meta_tokens, m_meta_tokens, v_meta_tokens, p_meta, "adamw_meta")
    res["conv_w"] = _adamw(conv_w[0], m_conv_w[0], v_conv_w[0], p_cw, "adamw_conv_w")
    for name in ("w_in", "w_out", "w_ff1", "w_ff2", "conv_w"):
        res[name] = tuple(t[None] for t in res[name])
    for name, parts in (("w_a", p_wa), ("w_x", p_wx)):
        gate = _adamw(*(src[name].reshape(gate_rows) for src in (weights, mom_m, mom_v)), parts, "adamw_" + name)
        res[name] = tuple(t.reshape(weights[name].shape) for t in gate)
    loss_total, small = _adamw_small(weights, mom_m, mom_v, p_small, p_loss)
    res.update(small)

    grad_x = dh0[BLOCK:][None]
    outs = [loss_total[0, 0], grad_x]
    for k in range(4):
        outs += [res[name][k] for name in order]
    return tuple(outs)
```

```python
import jax
import jax.numpy as jnp
import numpy as np
from jax import lax
from jax.experimental import pallas as pl
from jax.experimental.pallas import tpu as pltpu

F32 = jnp.float32
BF16 = jnp.bfloat16

D_MODEL = 1024
N_META = 16
HEAD_DIM = 64
ATTN_HEADS = 8
KV_HEADS = 2
GQA_GROUP = ATTN_HEADS // KV_HEADS
ATTN_WIDTH = ATTN_HEADS * HEAD_DIM
KV_WIDTH = KV_HEADS * HEAD_DIM
QKV_WIDTH = ATTN_WIDTH + 2 * KV_WIDTH
LRU_WIDTH = 512
LRU_BLOCKS = 8
LRU_BLOCK = 64
LRU_HALF = 256
LRU_C = 8.0
CONV_WIDTH = 4
BLOCK = 128
PAD_ROWS = BLOCK - N_META
IN_WIDTH = QKV_WIDTH + 2 * LRU_WIDTH
D_FF = 4096
EPS = 1e-6
NEG = -1e30
N_DEV = 8
FF_CHUNK = D_FF // N_DEV
SUBLANES = 8
LANES = 128

ADAM_LR = 0.001
ADAM_B1 = 0.9
ADAM_B2 = 0.999
ADAM_EPS = 1e-08
ADAM_WD = 0.01
ADAM_STEP = 10

VMEM_LIMIT = 56 * 1024 * 1024


def _row_tile(rows):
    for t in (640, 512, 256, 128):
        if rows % t == 0:
            return t
    raise ValueError(rows)


def _big_tile(rows):
    for t in (1664, 1024, 512, 256, 128):
        if rows % t == 0:
            return t
    raise ValueError(rows)


def _rec_tile(rows):
    for t in (640, 256, 128):
        if rows % t == 0:
            return t
    raise ValueError(rows)


def _params(semantics):
    return pltpu.CompilerParams(dimension_semantics=semantics, vmem_limit_bytes=VMEM_LIMIT)


def _mm(a, b):
    return lax.dot_general(a, b, (((1,), (0,)), ((), ())), preferred_element_type=F32)


def _mm_nt(a, b):
    return lax.dot_general(a, b, (((1,), (1,)), ((), ())), preferred_element_type=F32)


def _mm_tn(a, b):
    return lax.dot_general(a, b, (((0,), (0,)), ((), ())), preferred_element_type=F32)


def _rms_fwd(x, g):
    rstd = lax.rsqrt(jnp.mean(x * x, axis=-1, keepdims=True) + EPS)
    xhat = x * rstd
    return xhat * g, xhat, rstd


def _rms_bwd(dy, xhat, rstd, g):
    dyg = dy * g
    c = jnp.mean(dyg * xhat, axis=-1, keepdims=True)
    dx = rstd * (dyg - xhat * c)
    dg = jnp.sum(dy * xhat, axis=0, keepdims=True)
    return dx, dg


def _sigmoid(x):
    return pl.reciprocal(1.0 + jnp.exp(-x), approx=True)


def _log1p(x):
    u = 1.0 + x
    return jnp.where(u == 1.0, x, jnp.log(u) * x / (u - 1.0))


def _one_minus_sq_exp(x, ex):
    return -jnp.tanh(x) * (1.0 + ex * ex)


TINY = 1e-30


def _sqrt_pos(y):
    r = lax.rsqrt(jnp.maximum(y, TINY))
    return y * r, r


def _softplus(x):
    return jnp.maximum(x, 0.0) + _log1p(jnp.exp(-jnp.abs(x)))


GELU_C = 0.7978845608028654
GELU_K = 0.044715


def _gelu(x):
    t = jnp.tanh(GELU_C * (x + GELU_K * x * x * x))
    return 0.5 * x * (1.0 + t), t


def _gelu_grad(x, t):
    return 0.5 * (1.0 + t) + 0.5 * x * (1.0 - t * t) * GELU_C * (1.0 + 3.0 * GELU_K * x * x)


def _full(shape):
    return pl.BlockSpec(shape, lambda *_: (0,) * len(shape))


def _resident(shape):
    return pl.BlockSpec(shape, lambda *_: (0,) * len(shape), pipeline_mode=pl.Buffered(1))


def _exchange_copies(ins, outs, sems, modes):
    send_sems, recv_sems, local_sems = sems
    x, y, c = lax.axis_index("x"), lax.axis_index("y"), lax.axis_index("c")
    me = 4 * x + 2 * y + c

    def block(a, dev):
        return ins[a] if modes[a] == "gather" else ins[a].at[dev]

    local = [pltpu.make_async_copy(block(a, me), outs[a].at[me], local_sems.at[a]) for a in range(len(ins))]
    sends, recvs = [], []
    for a in range(len(ins)):
        for k in range(N_DEV - 1):
            bits = k + 1
            px = jnp.bitwise_xor(x, (bits >> 2) & 1)
            py = jnp.bitwise_xor(y, (bits >> 1) & 1)
            pc = jnp.bitwise_xor(c, bits & 1)
            peer = 4 * px + 2 * py + pc
            common = dict(src_ref=block(a, peer), send_sem=send_sems.at[a, k], recv_sem=recv_sems.at[a, k],
                          device_id=(px, py, pc), device_id_type=pl.DeviceIdType.MESH)
            sends.append(pltpu.make_async_remote_copy(dst_ref=outs[a].at[me], **common))
            recvs.append(pltpu.make_async_remote_copy(dst_ref=outs[a].at[peer], **common))
    return local, sends, recvs


def _exchange_start(ins, outs, sems, modes):
    local, sends, _ = _exchange_copies(ins, outs, sems, modes)
    for cp in local + sends:
        cp.start()


def _exchange_wait(ins, outs, sems, modes):
    local, sends, recvs = _exchange_copies(ins, outs, sems, modes)
    for cp in recvs:
        cp.wait_recv()
    for cp in sends:
        cp.wait_send()
    for cp in local:
        cp.wait()


def _exchange_shapes(arrays, modes):
    return [jax.ShapeDtypeStruct((N_DEV,) + a.shape if mode == "gather" else a.shape, a.dtype)
            for a, mode in zip(arrays, modes)]


def _exchange_sems(na):
    return [pltpu.SemaphoreType.DMA((na, N_DEV - 1)), pltpu.SemaphoreType.DMA((na, N_DEV - 1)),
            pltpu.SemaphoreType.DMA((na,))]


ANY_SPACE = pl.BlockSpec(memory_space=pl.ANY)


def _exchange(arrays, modes, name):
    na = len(arrays)

    def body(*refs):
        ins, outs, sems = refs[:na], refs[na:2 * na], refs[2 * na:]
        _exchange_start(ins, outs, sems, modes)
        _exchange_wait(ins, outs, sems, modes)

    return pl.pallas_call(
        body, name=name, out_shape=_exchange_shapes(arrays, modes),
        in_specs=[ANY_SPACE] * na, out_specs=[ANY_SPACE] * na, scratch_shapes=_exchange_sems(na),
        compiler_params=pltpu.CompilerParams(has_side_effects=True),
    )(*arrays)


def _gather_two_level(arrays, name):
    na = len(arrays)

    def body(*refs):
        ins, outs = refs[:na], refs[na:2 * na]
        send_sems, recv_sems, local_sems = refs[2 * na:]
        x, y, c = lax.axis_index("x"), lax.axis_index("y"), lax.axis_index("c")
        me, sibling = (x, y, c), (x, y, 1 - c)
        chips = [(1 - x, y), (x, 1 - y), (1 - x, 1 - y)]

        def copy(a, k, block, to, src=None):
            slot = outs[a].at[4 * block[0] + 2 * block[1] + block[2]]
            return pltpu.make_async_remote_copy(
                src_ref=slot if src is None else src, dst_ref=slot, send_sem=send_sems.at[a, k],
                recv_sem=recv_sems.at[a, k], device_id=to, device_id_type=pl.DeviceIdType.MESH)

        local = [pltpu.make_async_copy(ins[a], outs[a].at[4 * x + 2 * y + c], local_sems.at[a]) for a in range(na)]
        first = []
        for a in range(na):
            first.append(copy(a, 0, me, sibling, src=ins[a]))
            first += [copy(a, 1 + j, me, (*chip, c), src=ins[a]) for j, chip in enumerate(chips)]
        for cp in local + first:
            cp.start()
        passed = []
        for j, chip in enumerate(chips):
            for a in range(na):
                copy(a, 1 + j, (*chip, c), me).wait_recv()
                passed.append(copy(a, 4 + j, (*chip, c), sibling))
                passed[-1].start()
        for a in range(na):
            copy(a, 0, sibling, me).wait_recv()
            for j, chip in enumerate(chips):
                copy(a, 4 + j, (*chip, 1 - c), me).wait_recv()
        for cp in first + passed:
            cp.wait_send()
        for cp in local:
            cp.wait()

    return pl.pallas_call(
        body, name=name, out_shape=_exchange_shapes(arrays, ["gather"] * na),
        in_specs=[ANY_SPACE] * na, out_specs=[ANY_SPACE] * na, scratch_shapes=_exchange_sems(na),
        compiler_params=pltpu.CompilerParams(has_side_effects=True),
    )(*arrays)


def _hosting_call(body, name, steps, in_specs, out_specs, out_shape, scratch_shapes, args, arrays, modes):
    n_in, n_out, n_scr, na = len(in_specs), len(out_specs), len(scratch_shapes), len(arrays)
    grid = steps if isinstance(steps, tuple) else (steps,)

    def hosting_body(*refs):
        cuts = [0]
        for n in (n_in, na, n_out, na, n_scr, 3):
            cuts.append(cuts[-1] + n)
        ins, x_ins, outs, x_outs, scr, sems = (refs[cuts[p]:cuts[p + 1]] for p in range(6))
        first, last = True, True
        for axis, n in enumerate(grid):
            first = first & (pl.program_id(axis) == 0)
            last = last & (pl.program_id(axis) == n - 1)

        @pl.when(first)
        def _():
            _exchange_start(x_ins, x_outs, sems, modes)

        body(*ins, *outs, *scr)

        @pl.when(last)
        def _():
            _exchange_wait(x_ins, x_outs, sems, modes)

    res = pl.pallas_call(
        hosting_body, name=name, grid=grid,
        in_specs=list(in_specs) + [ANY_SPACE] * na, out_specs=list(out_specs) + [ANY_SPACE] * na,
        out_shape=list(out_shape) + _exchange_shapes(arrays, modes),
        scratch_shapes=list(scratch_shapes) + _exchange_sems(na),
        compiler_params=_params(("arbitrary",) * len(grid)),
    )(*args, *arrays)
    return res[:n_out], res[n_out:]


def _frame_rows(src_hbm, buf, sem, i, steps, tm):
    def first():
        return pltpu.make_async_copy(src_hbm.at[pl.ds(0, tm - BLOCK)], buf.at[0, pl.ds(BLOCK, tm - BLOCK)], sem.at[0])

    def later(t, slot):
        return pltpu.make_async_copy(src_hbm.at[pl.ds(pl.multiple_of(t * tm - BLOCK, SUBLANES), tm)], buf.at[slot], sem.at[slot])

    slot = i % 2

    @pl.when(i == 0)
    def _():
        first().start()

    @pl.when(i + 1 < steps)
    def _():
        later(i + 1, 1 - slot).start()

    @pl.when(i == 0)
    def _():
        first().wait()

    @pl.when(i > 0)
    def _():
        later(i, slot).wait()

    return slot


def _frame_scratch(tm):
    return [pltpu.VMEM((2, tm, D_MODEL), F32), pltpu.SemaphoreType.DMA((2,))]


def _h0_tile(head_ref, x_hbm, buf, sem, i, steps, tm):
    slot = _frame_rows(x_hbm, buf, sem, i, steps, tm)

    @pl.when(i == 0)
    def _():
        buf[0, 0:BLOCK, :] = head_ref[...]

    return buf[slot]


def _in_proj_fwd(head, x, g1, w_in, carried, modes):
    rows = BLOCK + x.shape[0]
    tm = _row_tile(rows)
    steps = rows // tm

    def body(head_ref, g_ref, w_ref, x_hbm, qkv_ref, zrec_ref, u_ref, buf, sem):
        h = _h0_tile(head_ref, x_hbm, buf, sem, pl.program_id(0), steps, tm)
        u, _, _ = _rms_fwd(h, g_ref[...])
        u = u.astype(BF16)
        u_ref[...] = u
        z = _mm(u, w_ref[...])
        qkv_ref[...] = z[:, :QKV_WIDTH].astype(BF16)
        zrec_ref[...] = z[:, QKV_WIDTH:]

    wide = pl.BlockSpec((tm, D_MODEL), lambda i: (i, 0))
    return _hosting_call(
        body, "in_proj_fwd", steps,
        [_full((BLOCK, D_MODEL)), _full((1, D_MODEL)), _resident((D_MODEL, IN_WIDTH)), ANY_SPACE],
        [pl.BlockSpec((tm, QKV_WIDTH), lambda i: (i, 0)), pl.BlockSpec((tm, 2 * LRU_WIDTH), lambda i: (i, 0)), wide],
        [jax.ShapeDtypeStruct((rows, QKV_WIDTH), BF16), jax.ShapeDtypeStruct((rows, 2 * LRU_WIDTH), F32),
         jax.ShapeDtypeStruct((rows, D_MODEL), BF16)],
        _frame_scratch(tm), (head, g1, w_in, x), carried, modes)


N_BIAS = 3


def _attn_bias():
    key = np.arange(2 * BLOCK)[:, None]
    r = np.arange(GQA_GROUP * BLOCK)[None, :] % BLOCK
    band = (key > r) & (key <= r + BLOCK)
    out = [np.where(band & ((n - 1) * BLOCK + key >= PAD_ROWS), 0.0, NEG) for n in range(N_BIAS)]
    return jnp.asarray(np.stack(out), F32)


def _attn_probs(k2, q4, bias, sink_row):
    s = _mm_nt(k2, q4) * (HEAD_DIM ** -0.5) + bias
    m = jnp.maximum(jnp.max(s, axis=0, keepdims=True), sink_row)
    p = jnp.exp(s - m)
    es = jnp.exp(sink_row - m)
    inv = 1.0 / (jnp.sum(p, axis=0, keepdims=True) + es)
    return p * inv, es * inv


def _heads(ref, rows, first, count):
    return jnp.concatenate([ref[rows, (first + g) * HEAD_DIM:(first + g + 1) * HEAD_DIM] for g in range(count)], axis=0)


def _keys_of_block(prev_ref, cur_ref, b, kv):
    sl = slice(kv * HEAD_DIM, (kv + 1) * HEAD_DIM)
    before = prev_ref[:, sl] if b == 0 else cur_ref[(b - 1) * BLOCK:b * BLOCK, sl]
    return jnp.concatenate([before, cur_ref[b * BLOCK:(b + 1) * BLOCK, sl]], axis=0)


def _bias_of_block(bias_ref, block):
    return bias_ref[jnp.minimum(block, N_BIAS - 1)]


def _sink_row(sink_ref, kv):
    g = lax.broadcasted_iota(jnp.int32, (1, GQA_GROUP * BLOCK), 1) // BLOCK
    row = jnp.full((1, GQA_GROUP * BLOCK), sink_ref[0, kv * GQA_GROUP], F32)
    for i in range(1, GQA_GROUP):
        row = jnp.where(g == i, sink_ref[0, kv * GQA_GROUP + i], row)
    return row


def _from_head_major(pieces):
    return jnp.concatenate(pieces, axis=0).T


def _attn_specs(tm, tile_of):
    nbt = tm // BLOCK
    k_col, v_col = ATTN_WIDTH // KV_WIDTH, ATTN_WIDTH // KV_WIDTH + 1
    before = lambda i: jnp.maximum(tile_of(i) * nbt - 1, 0)
    return [pl.BlockSpec((tm, ATTN_WIDTH), lambda i: (tile_of(i), 0)),
            pl.BlockSpec((BLOCK, KV_WIDTH), lambda i: (before(i), k_col)),
            pl.BlockSpec((tm, KV_WIDTH), lambda i: (tile_of(i), k_col)),
            pl.BlockSpec((BLOCK, KV_WIDTH), lambda i: (before(i), v_col)),
            pl.BlockSpec((tm, KV_WIDTH), lambda i: (tile_of(i), v_col))]


def _attn_fwd(qkv, sinks, bias, carried, modes):
    rows = qkv.shape[0]
    tm = _row_tile(rows)
    nbt = tm // BLOCK

    def body(sink_ref, bias_ref, q_ref, kp_ref, kc_ref, vp_ref, vc_ref, o_ref):
        i = pl.program_id(0)
        for b in range(nbt):
            blk = slice(b * BLOCK, (b + 1) * BLOCK)
            bias_t = _bias_of_block(bias_ref, i * nbt + b)
            pieces = []
            for kv in range(KV_HEADS):
                k2 = _keys_of_block(kp_ref, kc_ref, b, kv)
                v2 = _keys_of_block(vp_ref, vc_ref, b, kv)
                q4 = _heads(q_ref, blk, kv * GQA_GROUP, GQA_GROUP)
                pn, _ = _attn_probs(k2, q4, bias_t, _sink_row(sink_ref, kv))
                ot = _mm_tn(v2, pn.astype(BF16))
                pieces += [ot[:, g * BLOCK:(g + 1) * BLOCK] for g in range(GQA_GROUP)]
            o_ref[blk, :] = _from_head_major(pieces).astype(BF16)

    return _hosting_call(
        body, "attn_fwd", rows // tm,
        [pl.BlockSpec(memory_space=pltpu.SMEM), _resident((N_BIAS, 2 * BLOCK, GQA_GROUP * BLOCK))]
        + _attn_specs(tm, lambda i: i),
        [pl.BlockSpec((tm, ATTN_WIDTH), lambda i: (i, 0))],
        [jax.ShapeDtypeStruct((rows, ATTN_WIDTH), BF16)],
        [], (sinks, bias, qkv, qkv, qkv, qkv, qkv), carried, modes)


def _conv_taps(xbuf, tm):
    return [xbuf[pl.ds(SUBLANES - (CONV_WIDTH - 1 - j), tm), :] for j in range(CONV_WIDTH)]


def _lru_halves(xc):
    return [xc[:, h * LRU_HALF:(h + 1) * LRU_HALF].astype(BF16) for h in range(2)]


def _lru_gates(xc, wa_ref, ba_ref, wx_ref, bx_ref, lam_ref):
    halves = _lru_halves(xc)
    gate_r = jnp.concatenate([_mm(halves[h], wa_ref[h]) for h in range(2)], axis=1) + ba_ref[...]
    gate_i = jnp.concatenate([_mm(halves[h], wx_ref[h]) for h in range(2)], axis=1) + bx_ref[...]
    r = _sigmoid(gate_r)
    ig = _sigmoid(gate_i)
    log_a = (-LRU_C) * r * _softplus(-lam_ref[...])
    a = jnp.exp(log_a)
    mult, _ = _sqrt_pos(_one_minus_sq_exp(log_a, a))
    return r, ig, a, mult


KEPT_XC, KEPT_A, KEPT_MULT, KEPT_R, KEPT_I, N_KEPT = 0, 1, 2, 3, 4, 5


def _scan_tile(a_ref, u_ref, out_ref, carry, tm):
    row = lax.broadcasted_iota(jnp.int32, (SUBLANES, LRU_WIDTH), 0)

    def step(j, before):
        o = pl.multiple_of(j * SUBLANES, SUBLANES)
        a = a_ref[pl.ds(o, SUBLANES), :]
        u = u_ref[pl.ds(o, SUBLANES), :]
        for s in (1, 2, 4):
            keep = row >= s
            u = jnp.where(keep, a * pltpu.roll(u, s, 0) + u, u)
            a = jnp.where(keep, a * pltpu.roll(a, s, 0), a)
        out = a * before + u
        out_ref[pl.ds(o, SUBLANES), :] = out
        return out[SUBLANES - 1:SUBLANES, :]

    return lax.fori_loop(0, tm // SUBLANES, step, carry)


def _rec_fwd(zrec, conv_w, conv_b, wa_bd, b_a, wx_bd, b_x, lam, carried, modes):
    rows = zrec.shape[0]
    tm = _row_tile(rows)

    def body(xr_ref, yr_ref, cw_ref, cb_ref, wa_ref, ba_ref, wx_ref, bx_ref, lam_ref, rec_ref, h_ref, kept_ref,
             xbuf, a_s, u_s, carry):
        i = pl.program_id(0)

        @pl.when(i == 0)
        def _():
            xbuf[0:SUBLANES, :] = jnp.zeros((SUBLANES, LRU_WIDTH), F32)
            carry[...] = jnp.zeros_like(carry)

        @pl.when(i > 0)
        def _():
            xbuf[0:SUBLANES, :] = xbuf[tm:tm + SUBLANES, :]

        xbuf[SUBLANES:SUBLANES + tm, :] = xr_ref[...]
        taps = _conv_taps(xbuf, tm)
        xc = cb_ref[...] + sum(cw_ref[j:j + 1, :] * taps[j] for j in range(CONV_WIDTH))
        r, ig, a, mult = _lru_gates(xc, wa_ref, ba_ref, wx_ref, bx_ref, lam_ref)
        for k, val in ((KEPT_XC, xc), (KEPT_A, a), (KEPT_MULT, mult), (KEPT_R, r), (KEPT_I, ig)):
            kept_ref[:, k * LRU_WIDTH:(k + 1) * LRU_WIDTH] = val
        grow = i * tm + lax.broadcasted_iota(jnp.int32, (tm, LRU_WIDTH), 0)
        a_s[...] = a
        u_s[...] = jnp.where(grow >= PAD_ROWS, mult * (ig * xc), 0.0)
        carry[0:1, :] = _scan_tile(a_s, u_s, h_ref, carry[0:1, :], tm)
        gel, _ = _gelu(yr_ref[...])
        rec_ref[...] = (gel * h_ref[...]).astype(BF16)

    vec = _full((1, LRU_WIDTH))
    bd = _full((2, LRU_HALF, LRU_HALF))
    return _hosting_call(
        body, "rec_fwd", rows // tm,
        [pl.BlockSpec((tm, LRU_WIDTH), lambda i: (i, 0)), pl.BlockSpec((tm, LRU_WIDTH), lambda i: (i, 1)),
         _full((CONV_WIDTH, LRU_WIDTH)), vec, bd, vec, bd, vec, vec],
        [pl.BlockSpec((tm, LRU_WIDTH), lambda i: (i, 0))] * 2 + [pl.BlockSpec((tm, N_KEPT * LRU_WIDTH), lambda i: (i, 0))],
        [jax.ShapeDtypeStruct((rows, LRU_WIDTH), BF16), jax.ShapeDtypeStruct((rows, LRU_WIDTH), F32),
         jax.ShapeDtypeStruct((rows, N_KEPT * LRU_WIDTH), F32)],
        [pltpu.VMEM((tm + SUBLANES, LRU_WIDTH), F32), pltpu.VMEM((tm, LRU_WIDTH), F32),
         pltpu.VMEM((tm, LRU_WIDTH), F32), pltpu.VMEM((SUBLANES, LRU_WIDTH), F32)],
        (zrec, zrec, conv_w, conv_b, wa_bd, b_a, wx_bd, b_x, lam), carried, modes)


def _out_proj_fwd(attn, rec, w_out, head, x, g2, carried, modes):
    rows = attn.shape[0]
    tm = _row_tile(rows)
    steps = rows // tm

    def body(attn_ref, rec_ref, w_ref, head_ref, g_ref, x_hbm, mix_ref, h1_ref, buf, sem):
        h0 = _h0_tile(head_ref, x_hbm, buf, sem, pl.program_id(0), steps, tm)
        mix = _mm(attn_ref[...], w_ref[0:ATTN_WIDTH, :]) + _mm(rec_ref[...], w_ref[ATTN_WIDTH:, :])
        y, _, _ = _rms_fwd(mix, g_ref[...])
        mix_ref[...] = mix
        h1_ref[...] = h0 + y

    half = pl.BlockSpec((tm, ATTN_WIDTH), lambda i: (i, 0))
    wide = pl.BlockSpec((tm, D_MODEL), lambda i: (i, 0))
    return _hosting_call(
        body, "out_proj_fwd", steps,
        [half, half, _resident((D_MODEL, D_MODEL)), _full((BLOCK, D_MODEL)), _full((1, D_MODEL)), ANY_SPACE],
        [wide, wide],
        [jax.ShapeDtypeStruct((rows, D_MODEL), F32)] * 2,
        _frame_scratch(tm), (attn, rec, w_out, head, g2, x), carried, modes)


FF_COLS = 1024
FF_HALF = FF_CHUNK // 2


def _hidden_at(d, half):
    return half * (D_FF // 2) + d * FF_HALF


def _ffn_up(h1, g3, w1_halves, carried, modes):
    rows = h1.shape[0]
    tm = _row_tile(rows)

    def body(h_ref, g_ref, wa_ref, wb_ref, act_ref, u_ref):
        u, _, _ = _rms_fwd(h_ref[...], g_ref[...])
        u = u.astype(BF16)
        u_ref[...] = u
        for half, w_ref in enumerate((wa_ref, wb_ref)):
            for d in range(N_DEV):
                c = _hidden_at(d, half)
                a1 = jnp.maximum(_mm(u, w_ref[d]), 0.0)
                act_ref[:, c:c + FF_HALF] = (a1 * a1).astype(BF16)

    wide = pl.BlockSpec((tm, D_MODEL), lambda i: (i, 0))
    return _hosting_call(
        body, "ffn_up", rows // tm,
        [wide, _full((1, D_MODEL))] + [_resident((N_DEV, D_MODEL, FF_HALF))] * 2,
        [pl.BlockSpec((tm, D_FF), lambda i: (i, 0)), wide],
        [jax.ShapeDtypeStruct((rows, D_FF), BF16), jax.ShapeDtypeStruct((rows, D_MODEL), BF16)],
        [], (h1, g3, *w1_halves), carried, modes)


def _ffn_down_loss(act, w2_halves, h1, target, g4, carried, modes):
    rows = h1.shape[0]
    tm = _row_tile(rows)
    steps = rows // tm
    kh = D_FF // 2

    def body(act_ref, wa_ref, wb_ref, h_ref, g_ref, t_hbm, dy_ref, df_ref, dg_ref, loss_ref, buf, sem):
        i = pl.program_id(0)
        slot = _frame_rows(t_hbm, buf, sem, i, steps, tm)

        @pl.when(i == 0)
        def _():
            dg_ref[...] = jnp.zeros_like(dg_ref)
            loss_ref[...] = jnp.zeros_like(loss_ref)
            buf[0, 0:BLOCK, :] = jnp.zeros((BLOCK, D_MODEL), F32)

        g = g_ref[...]
        f = _mm(act_ref[:, :kh], wa_ref[...]) + _mm(act_ref[:, kh:], wb_ref[...])
        y, fhat, rstd = _rms_fwd(f, g)
        grow = i * tm + lax.broadcasted_iota(jnp.int32, (tm, D_MODEL), 0)
        err = jnp.where(grow >= BLOCK, h_ref[...] + y - buf[slot], 0.0)
        loss_ref[...] += (0.5 / D_MODEL) * jnp.sum(err * err)
        dy = err * (1.0 / D_MODEL)
        df, dg = _rms_bwd(dy, fhat, rstd, g)
        dy_ref[...] = dy
        df_ref[...] = df.astype(BF16)
        dg_ref[...] += dg

    wide = pl.BlockSpec((tm, D_MODEL), lambda i: (i, 0))
    return _hosting_call(
        body, "ffn_down_loss", steps,
        [pl.BlockSpec((tm, D_FF), lambda i: (i, 0)), _resident((kh, D_MODEL)), _resident((kh, D_MODEL)), wide,
         _full((1, D_MODEL)), ANY_SPACE],
        [wide, wide, _full((1, D_MODEL)), _full((SUBLANES, LANES))],
        [jax.ShapeDtypeStruct((rows, D_MODEL), F32), jax.ShapeDtypeStruct((rows, D_MODEL), BF16),
         jax.ShapeDtypeStruct((1, D_MODEL), F32), jax.ShapeDtypeStruct((SUBLANES, LANES), F32)],
        _frame_scratch(tm), (act, *w2_halves, h1, g4, target), carried, modes)


def _ffn_bwd_act(df, w2t_halves, act, carried, modes):
    rows = df.shape[0]
    tm = _row_tile(rows)

    def body(df_ref, wa_ref, wb_ref, act_ref, da_ref):
        df_t = df_ref[...]
        for half, w_ref in enumerate((wa_ref, wb_ref)):
            for d in range(N_DEV):
                cols = slice(_hidden_at(d, half), _hidden_at(d, half) + FF_HALF)
                dact = _mm(df_t, w_ref[d])
                relu_a1, _ = _sqrt_pos(act_ref[:, cols].astype(F32))
                da_ref[:, cols] = (dact * (2.0 * relu_a1)).astype(BF16)

    hidden = pl.BlockSpec((tm, D_FF), lambda i: (i, 0))
    return _hosting_call(
        body, "ffn_bwd_act", rows // tm,
        [pl.BlockSpec((tm, D_MODEL), lambda i: (i, 0))] + [_resident((N_DEV, D_MODEL, FF_HALF))] * 2 + [hidden],
        [hidden],
        [jax.ShapeDtypeStruct((rows, D_FF), BF16)],
        [], (df, *w2t_halves, act), carried, modes)


def _ffn_bwd_x(da, w1t_halves, h1, dy, g3, carried, modes):
    rows = h1.shape[0]
    tm = _row_tile(rows)
    kh = D_FF // 2

    def body(da_ref, wa_ref, wb_ref, h_ref, dy_ref, g_ref, dh_ref, dg_ref):
        @pl.when(pl.program_id(0) == 0)
        def _():
            dg_ref[...] = jnp.zeros_like(dg_ref)

        g = g_ref[...]
        _, xhat, rstd = _rms_fwd(h_ref[...], g)
        du = _mm(da_ref[:, :kh], wa_ref[...]) + _mm(da_ref[:, kh:], wb_ref[...])
        dx, dg = _rms_bwd(du, xhat, rstd, g)
        dh_ref[...] = dy_ref[...] + dx
        dg_ref[...] += dg

    wide = pl.BlockSpec((tm, D_MODEL), lambda i: (i, 0))
    return _hosting_call(
        body, "ffn_bwd_x", rows // tm,
        [pl.BlockSpec((tm, D_FF), lambda i: (i, 0)), _resident((kh, D_MODEL)), _resident((kh, D_MODEL)), wide, wide,
         _full((1, D_MODEL))],
        [wide, _full((1, D_MODEL))],
        [jax.ShapeDtypeStruct((rows, D_MODEL), F32), jax.ShapeDtypeStruct((1, D_MODEL), F32)],
        [], (da, *w1t_halves, h1, dy, g3), carried, modes)


def _ffn_bwd_weights(u2, da, act, df, carried, modes):
    rows = u2.shape[0]
    tb = _big_tile(rows)
    steps = rows // tb
    per = FF_COLS // FF_HALF

    def body(u_ref, da_ref, act_ref, df_ref, dw1_ref, dw2_ref, acc1, acc2):
        i = pl.program_id(1)

        @pl.when(i == 0)
        def _():
            acc1[...] = jnp.zeros_like(acc1)
            acc2[...] = jnp.zeros_like(acc2)

        acc1[...] += _mm_tn(u_ref[...], da_ref[...])
        acc2[...] += _mm_tn(act_ref[...], df_ref[...])

        @pl.when(i == steps - 1)
        def _():
            for p in range(per):
                c = p * FF_HALF
                dw1_ref[p] = acc1[:, c:c + FF_HALF].astype(BF16)
                dw2_ref[p] = acc2[c:c + FF_HALF, :].astype(BF16)

    wide = pl.BlockSpec((tb, D_MODEL), lambda j, i: (i, 0))
    chunk = pl.BlockSpec((tb, FF_COLS), lambda j, i: (i, j))
    return _hosting_call(
        body, "ffn_bwd_weights", (D_FF // FF_COLS, steps),
        [wide, chunk, chunk, wide],
        [pl.BlockSpec((None, per, D_MODEL, FF_HALF), lambda j, i: (j // 2, j % 2, 0, 0)),
         pl.BlockSpec((per, FF_HALF, D_MODEL), lambda j, i: (j % 2, j // 2, 0))],
        [jax.ShapeDtypeStruct((2, N_DEV, D_MODEL, FF_HALF), BF16), jax.ShapeDtypeStruct((N_DEV, FF_CHUNK, D_MODEL), BF16)],
        [pltpu.VMEM((D_MODEL, FF_COLS), F32), pltpu.VMEM((FF_COLS, D_MODEL), F32)],
        (u2, da, act, df), carried, modes)


def _out_proj_bwd(dh1, mix, g2, w_out_t, attn, rec, carried, modes):
    rows = dh1.shape[0]
    tm = _row_tile(rows)
    steps = rows // tm

    def body(dh_ref, mix_ref, g_ref, w_ref, attn_ref, rec_ref, dattn_ref, drec_ref, dw_ref, dg_ref, acc):
        i = pl.program_id(0)

        @pl.when(i == 0)
        def _():
            acc[...] = jnp.zeros_like(acc)
            dg_ref[...] = jnp.zeros_like(dg_ref)

        g = g_ref[...]
        _, xhat, rstd = _rms_fwd(mix_ref[...], g)
        dmix, dg = _rms_bwd(dh_ref[...], xhat, rstd, g)
        dmix = dmix.astype(BF16)
        dg_ref[...] += dg
        din = _mm(dmix, w_ref[...])
        dattn_ref[...] = din[:, :ATTN_WIDTH].astype(BF16)
        drec_ref[...] = din[:, ATTN_WIDTH:]
        acc[0:ATTN_WIDTH, :] += _mm_tn(attn_ref[...], dmix)
        acc[ATTN_WIDTH:, :] += _mm_tn(rec_ref[...], dmix)

        @pl.when(i == steps - 1)
        def _():
            dw_ref[...] = acc[...].astype(BF16)

    half = pl.BlockSpec((tm, ATTN_WIDTH), lambda i: (i, 0))
    wide = pl.BlockSpec((tm, D_MODEL), lambda i: (i, 0))
    return _hosting_call(
        body, "out_proj_bwd", steps,
        [wide, wide, _full((1, D_MODEL)), _resident((D_MODEL, D_MODEL)), half, half],
        [half, half, _full((D_MODEL, D_MODEL)), _full((1, D_MODEL))],
        [jax.ShapeDtypeStruct((rows, ATTN_WIDTH), BF16), jax.ShapeDtypeStruct((rows, LRU_WIDTH), F32),
         jax.ShapeDtypeStruct((D_MODEL, D_MODEL), BF16), jax.ShapeDtypeStruct((1, D_MODEL), F32)],
        [pltpu.VMEM((D_MODEL, D_MODEL), F32)],
        (dh1, mix, g2, w_out_t, attn, rec), carried, modes)


def _attn_bwd(qkv, dattn, sinks, bias, carried, modes):
    rows = qkv.shape[0]
    tm = _row_tile(rows)
    nbt, nt = tm // BLOCK, rows // tm

    def body(sink_ref, bias_ref, do_ref, q_ref, kp_ref, kc_ref, vp_ref, vc_ref, dq_ref, dkv_ref, dsink_ref, dk_c, dv_c):
        i = pl.program_id(0)

        @pl.when(i == 0)
        def _():
            dk_c[...] = jnp.zeros_like(dk_c)
            dv_c[...] = jnp.zeros_like(dv_c)
            dsink_ref[...] = jnp.zeros_like(dsink_ref)

        @pl.when(i < nt)
        def _():
            dk_late, dv_late = dk_c[...], dv_c[...]
            dsink_rows = [jnp.zeros((1, LANES), F32)] * ATTN_HEADS
            for b in range(nbt):
                blk = slice(b * BLOCK, (b + 1) * BLOCK)
                bias_t = _bias_of_block(bias_ref, i * nbt + b)
                dq_parts, dk_parts, dv_parts = [], [], []
                for kv in range(KV_HEADS):
                    k2 = _keys_of_block(kp_ref, kc_ref, b, kv)
                    v2 = _keys_of_block(vp_ref, vc_ref, b, kv)
                    q4 = _heads(q_ref, blk, kv * GQA_GROUP, GQA_GROUP)
                    do4 = _heads(do_ref, blk, kv * GQA_GROUP, GQA_GROUP)
                    pn, psink = _attn_probs(k2, q4, bias_t, _sink_row(sink_ref, kv))
                    dpn = _mm_nt(v2, do4)
                    delta = jnp.sum(pn * dpn, axis=0, keepdims=True)
                    ds = ((pn * (dpn - delta)) * (HEAD_DIM ** -0.5)).astype(BF16)
                    dqt = _mm_tn(k2, ds)
                    dq_parts += [dqt[:, g * BLOCK:(g + 1) * BLOCK] for g in range(GQA_GROUP)]
                    dk_parts.append(_mm(ds, q4))
                    dv_parts.append(_mm(pn.astype(BF16), do4))
                    sd = psink * delta
                    for g in range(GQA_GROUP):
                        h = kv * GQA_GROUP + g
                        dsink_rows[h] = dsink_rows[h] - jnp.sum(sd[:, g * BLOCK:(g + 1) * BLOCK])
                dq_ref[blk, :] = _from_head_major(dq_parts).astype(BF16)
                dk2 = jnp.concatenate(dk_parts, axis=1)
                dv2 = jnp.concatenate(dv_parts, axis=1)
                dkv_ref[blk, 0:KV_WIDTH] = (dk_late + dk2[0:BLOCK]).astype(BF16)
                dkv_ref[blk, KV_WIDTH:] = (dv_late + dv2[0:BLOCK]).astype(BF16)
                dk_late, dv_late = dk2[BLOCK:], dv2[BLOCK:]
            dk_c[...] = dk_late
            dv_c[...] = dv_late
            dsink_ref[...] += jnp.concatenate(dsink_rows, axis=0)

        @pl.when(i == nt)
        def _():
            dkv_ref[...] = jnp.zeros_like(dkv_ref)
            dkv_ref[0:BLOCK, 0:KV_WIDTH] = dk_c[...].astype(BF16)
            dkv_ref[0:BLOCK, KV_WIDTH:] = dv_c[...].astype(BF16)

    tile_of = lambda i: jnp.minimum(i, nt - 1)
    tile = pl.BlockSpec((tm, ATTN_WIDTH), lambda i: (tile_of(i), 0))
    return _hosting_call(
        body, "attn_bwd", nt + 1,
        [pl.BlockSpec(memory_space=pltpu.SMEM), _resident((N_BIAS, 2 * BLOCK, GQA_GROUP * BLOCK)), tile]
        + _attn_specs(tm, tile_of),
        [tile, pl.BlockSpec((tm, 2 * KV_WIDTH), lambda i: (i, 0)), _full((ATTN_HEADS, LANES))],
        [jax.ShapeDtypeStruct((rows, ATTN_WIDTH), BF16), jax.ShapeDtypeStruct((rows + tm, 2 * KV_WIDTH), BF16),
         jax.ShapeDtypeStruct((ATTN_HEADS, LANES), F32)],
        [pltpu.VMEM((BLOCK, KV_WIDTH), F32), pltpu.VMEM((BLOCK, KV_WIDTH), F32)],
        (sinks, bias, dattn, qkv, qkv, qkv, qkv, qkv), carried, modes)


ROW_CONV_B, ROW_B_A, ROW_B_X, ROW_LAMBDA = 4, 5, 6, 7


def _rec_bwd(drec, zrec, h, kept, conv_w, wa_bd, wx_bd, lam, carried, modes):
    rows = zrec.shape[0]
    tm = _rec_tile(rows)
    nt = rows // tm
    per = tm // SUBLANES

    def body(drec_ref, xr_ref, yr_ref, h_ref, xc_ref, a_ref, mult_ref, r_ref, ig_ref, hhalo_ref, cw_ref, wa_ref, wx_ref,
             lam_ref, drz_ref, small_ref, dwa_ref, dwx_ref, hbuf, dbuf, dgr_s, dgi_s, dyr_s, carry):
        s = pl.program_id(0)
        i = nt - 1 - s

        @pl.when(s == 0)
        def _():
            small_ref[...] = jnp.zeros_like(small_ref)
            dwa_ref[...] = jnp.zeros_like(dwa_ref)
            dwx_ref[...] = jnp.zeros_like(dwx_ref)
            carry[...] = jnp.zeros_like(carry)
            dbuf[tm:tm + SUBLANES, :] = jnp.zeros((SUBLANES, LRU_WIDTH), F32)

        hbuf[0:SUBLANES, :] = jnp.where(i == 0, 0.0, hhalo_ref[...])
        hbuf[SUBLANES:SUBLANES + tm, :] = h_ref[...]

        row = lax.broadcasted_iota(jnp.int32, (SUBLANES, LRU_WIDTH), 0)
        log_a_scale = (-LRU_C) * _softplus(-lam_ref[...])
        zeros = jnp.zeros((SUBLANES, LRU_WIDTH), F32)

        def group(k, state):
            g_later, a_later, sum_dgr, sum_dgi, sum_lam = state
            o = pl.multiple_of((per - 1 - k) * SUBLANES, SUBLANES)
            rows8 = pl.ds(o, SUBLANES)
            yr, drec_t, h_t, a = yr_ref[rows8, :], drec_ref[rows8, :], h_ref[rows8, :], a_ref[rows8, :]
            gel, t = _gelu(yr)
            dyr_s[rows8, :] = drec_t * h_t * _gelu_grad(yr, t)
            u = drec_t * gel
            coef = jnp.where(row == SUBLANES - 1, a_later, pltpu.roll(a, SUBLANES - 1, 0))
            for sft in (1, 2, 4):
                keep = row < SUBLANES - sft
                u = jnp.where(keep, coef * pltpu.roll(u, SUBLANES - sft, 0) + u, u)
                coef = jnp.where(keep, coef * pltpu.roll(coef, SUBLANES - sft, 0), coef)
            g = coef * g_later + u
            du = jnp.where(i * tm + o + row >= PAD_ROWS, g, 0.0)
            h_before = jnp.where(row == 0, hbuf[rows8, :][SUBLANES - 1:SUBLANES, :], pltpu.roll(h_t, 1, 0))
            xc, mult, r, ig = xc_ref[rows8, :], mult_ref[rows8, :], r_ref[rows8, :], ig_ref[rows8, :]
            dbuf[rows8, :] = du * (mult * ig)
            dgi = (du * (mult * xc)) * (ig * (1.0 - ig))
            dgi_s[rows8, :] = dgi
            dlog_a = (g * h_before) * a - (du * (ig * xc)) * (a * a * pl.reciprocal(mult, approx=True))
            dgr = (dlog_a * log_a_scale) * (r * (1.0 - r))
            dgr_s[rows8, :] = dgr
            return g[0:1, :], a[0:1, :], sum_dgr + dgr, sum_dgi + dgi, sum_lam + dlog_a * r

        state = lax.fori_loop(0, per, group, (carry[0:1, :], carry[1:2, :], zeros, zeros, zeros))
        carry[0:1, :], carry[1:2, :] = state[0], state[1]
        sum_dgr, sum_dgi, sum_lam = (jnp.sum(v, axis=0, keepdims=True) for v in state[2:])
        dlam = sum_lam * (LRU_C * _sigmoid(-lam_ref[...]))

        dgr_b = [dgr_s[:, hh * LRU_HALF:(hh + 1) * LRU_HALF].astype(BF16) for hh in range(2)]
        dgi_b = [dgi_s[:, hh * LRU_HALF:(hh + 1) * LRU_HALF].astype(BF16) for hh in range(2)]
        halves = _lru_halves(xc_ref[...])
        for hh in range(2):
            dwa_ref[hh] += _mm_tn(halves[hh], dgr_b[hh])
            dwx_ref[hh] += _mm_tn(halves[hh], dgi_b[hh])
        dxc = dbuf[0:tm, :] + jnp.concatenate(
            [_mm_nt(dgr_b[hh], wa_ref[hh]) + _mm_nt(dgi_b[hh], wx_ref[hh]) for hh in range(2)], axis=1)

        dbuf[0:tm, :] = dxc
        sum_dxc = jnp.sum(dxc, axis=0, keepdims=True)
        ahead = [dbuf[pl.ds(CONV_WIDTH - 1 - j, tm), :] for j in range(CONV_WIDTH)]
        drz_ref[:, 0:LRU_WIDTH] = sum(cw_ref[j:j + 1, :] * ahead[j] for j in range(CONV_WIDTH)).astype(BF16)
        drz_ref[:, LRU_WIDTH:] = dyr_s[...].astype(BF16)
        upd = [jnp.sum(xr_ref[...] * ahead[j], axis=0, keepdims=True) for j in range(CONV_WIDTH)]
        dbuf[tm:tm + SUBLANES, :] = dbuf[0:SUBLANES, :]
        small_ref[...] += jnp.concatenate(upd + [sum_dxc, sum_dgr, sum_dgi, dlam], axis=0)

    rev = lambda s: nt - 1 - s
    halo = lambda s: jnp.maximum(rev(s) * per - 1, 0)
    cols = lambda k: pl.BlockSpec((tm, LRU_WIDTH), lambda s: (rev(s), k))
    halo0 = pl.BlockSpec((SUBLANES, LRU_WIDTH), lambda s: (halo(s), 0))
    bd = _full((2, LRU_HALF, LRU_HALF))
    big = pltpu.VMEM((tm + SUBLANES, LRU_WIDTH), F32)
    tile = pltpu.VMEM((tm, LRU_WIDTH), F32)
    kept_cols = [cols(k) for k in (KEPT_XC, KEPT_A, KEPT_MULT, KEPT_R, KEPT_I)]
    return _hosting_call(
        body, "rec_bwd", nt,
        [cols(0), cols(0), cols(1), cols(0)] + kept_cols
        + [halo0, _full((CONV_WIDTH, LRU_WIDTH)), bd, bd, _full((1, LRU_WIDTH))],
        [pl.BlockSpec((tm, 2 * LRU_WIDTH), lambda s: (rev(s), 0)), _full((SUBLANES, LRU_WIDTH)), bd, bd],
        [jax.ShapeDtypeStruct((rows, 2 * LRU_WIDTH), BF16), jax.ShapeDtypeStruct((SUBLANES, LRU_WIDTH), F32),
         jax.ShapeDtypeStruct((2, LRU_HALF, LRU_HALF), F32), jax.ShapeDtypeStruct((2, LRU_HALF, LRU_HALF), F32)],
        [big, big, tile, tile, tile, pltpu.VMEM((SUBLANES, LRU_WIDTH), F32)],
        (drec, zrec, zrec, h) + (kept,) * N_KEPT + (h, conv_w, wa_bd, wx_bd, lam), carried, modes)


DZ_CUTS = (0, ATTN_WIDTH, QKV_WIDTH, IN_WIDTH)


def _dz_specs(tm):
    return [pl.BlockSpec((tm, DZ_CUTS[p + 1] - DZ_CUTS[p]), lambda i: (i, 0)) for p in range(3)]


def _in_proj_bwd_x(head, x, g1, dh1, dq, dkv, drz, w_in_t, carried, modes):
    rows = dh1.shape[0]
    tm = _row_tile(rows)
    steps = rows // tm

    def body(head_ref, g_ref, dh1_ref, dq_ref, dkv_ref, drz_ref, w_ref, x_hbm, dh0_ref, dg_ref, buf, sem):
        i = pl.program_id(0)
        h0 = _h0_tile(head_ref, x_hbm, buf, sem, i, steps, tm)

        @pl.when(i == 0)
        def _():
            dg_ref[...] = jnp.zeros_like(dg_ref)

        g = g_ref[...]
        _, xhat, rstd = _rms_fwd(h0, g)
        parts = (dq_ref[...], dkv_ref[...], drz_ref[...])
        du = sum(_mm(parts[p], w_ref[DZ_CUTS[p]:DZ_CUTS[p + 1], :]) for p in range(3))
        dx, dg = _rms_bwd(du, xhat, rstd, g)
        dh0_ref[...] = dh1_ref[...] + dx
        dg_ref[...] += dg

    wide = pl.BlockSpec((tm, D_MODEL), lambda i: (i, 0))
    return _hosting_call(
        body, "in_proj_bwd_x", steps,
        [_full((BLOCK, D_MODEL)), _full((1, D_MODEL)), wide] + _dz_specs(tm) + [_resident((IN_WIDTH, D_MODEL)), ANY_SPACE],
        [wide, _full((1, D_MODEL))],
        [jax.ShapeDtypeStruct((rows, D_MODEL), F32), jax.ShapeDtypeStruct((1, D_MODEL), F32)],
        _frame_scratch(tm), (head, g1, dh1, dq, dkv, drz, w_in_t, x), carried, modes)


def _in_proj_bwd_w(u1, dq, dkv, drz, carried, modes):
    rows = u1.shape[0]
    tb = _big_tile(rows)
    steps = rows // tb

    def body(u_ref, dq_ref, dkv_ref, drz_ref, dw_ref, acc):
        i = pl.program_id(0)

        @pl.when(i == 0)
        def _():
            acc[...] = jnp.zeros_like(acc)

        u = u_ref[...]
        for p, ref in enumerate((dq_ref, dkv_ref, drz_ref)):
            acc[:, DZ_CUTS[p]:DZ_CUTS[p + 1]] += _mm_tn(u, ref[...])

        @pl.when(i == steps - 1)
        def _():
            dw_ref[...] = acc[...].astype(BF16)

    return _hosting_call(
        body, "in_proj_bwd_w", steps,
        [pl.BlockSpec((tb, D_MODEL), lambda i: (i, 0))] + _dz_specs(tb),
        [_full((D_MODEL, IN_WIDTH))],
        [jax.ShapeDtypeStruct((D_MODEL, IN_WIDTH), BF16)],
        [pltpu.VMEM((D_MODEL, IN_WIDTH), F32)], (u1, dq, dkv, drz), carried, modes)


def _adamw_math(w, m, v, g):
    nm = ADAM_B1 * m + (1.0 - ADAM_B1) * g
    nv = ADAM_B2 * v + (1.0 - ADAM_B2) * (g * g)
    m_hat = nm / (1.0 - ADAM_B1 ** ADAM_STEP)
    v_hat = nv / (1.0 - ADAM_B2 ** ADAM_STEP)
    return (-ADAM_LR) * (m_hat / (jnp.sqrt(v_hat) + ADAM_EPS) + ADAM_WD * w), nm, nv


SMALL_NAMES = ("conv_b", "b_a", "b_x", "lru_lambda", "attn_sinks", "g_post_mix", "g_pre_ffn", "g_post_ffn")
PACK_WIDTH = 1024


def _pack_rows(vals):
    assert len(SMALL_NAMES) == SUBLANES
    row = lax.broadcasted_iota(jnp.int32, (SUBLANES, PACK_WIDTH), 0)
    tile = jnp.zeros((SUBLANES, PACK_WIDTH), F32)
    for k, name in enumerate(SMALL_NAMES):
        a = vals[name].reshape(1, -1)
        tile = jnp.where(row == k, jnp.pad(a, ((0, 0), (0, PACK_WIDTH - a.shape[1]))), tile)
    return tile


def _adamw_small(weights, mom_m, mom_v, parts, loss_parts):
    n = len(SMALL_NAMES)
    views = [(1, weights[name].size) for name in SMALL_NAMES]

    def body(*refs):
        w_refs, m_refs, v_refs = refs[:n], refs[n:2 * n], refs[2 * n:3 * n]
        p_ref, l_ref, loss_ref = refs[3 * n], refs[3 * n + 1], refs[3 * n + 2]
        outs = refs[3 * n + 3:]
        for k, (_, c) in enumerate(views):
            g = p_ref[0, k:k + 1, 0:c]
            for s in range(1, N_DEV):
                g = g + p_ref[s, k:k + 1, 0:c]
            g_ref, d_ref, nm_ref, nv_ref = outs[4 * k:4 * k + 4]
            g_ref[...] = g
            d_ref[...], nm_ref[...], nv_ref[...] = _adamw_math(w_refs[k][...], m_refs[k][...], v_refs[k][...], g)
        total = l_ref[0]
        for s in range(1, N_DEV):
            total = total + l_ref[s]
        loss_ref[...] = total

    args = [src[name].reshape(view) for src in (weights, mom_m, mom_v) for name, view in zip(SMALL_NAMES, views)]
    res = pl.pallas_call(
        body, name="adamw_small",
        out_shape=[jax.ShapeDtypeStruct(loss_parts.shape[1:], F32)]
                  + [jax.ShapeDtypeStruct(view, F32) for view in views for _ in range(4)],
        compiler_params=pltpu.CompilerParams(vmem_limit_bytes=VMEM_LIMIT),
    )(*args, parts, loss_parts)
    out = {name: tuple(t.reshape(weights[name].shape) for t in res[1 + 4 * k:5 + 4 * k]) for k, name in enumerate(SMALL_NAMES)}
    return res[0], out


def _adamw(w, m, v, parts, name):
    rows, cols = w.shape
    tr = next((t for t in (256, 128) if rows % t == 0), rows)
    parts = parts if isinstance(parts, (list, tuple)) else [parts]

    def body(w_ref, m_ref, v_ref, *refs):
        p_refs, (g_ref, d_ref, nm_ref, nv_ref) = refs[:len(parts)], refs[len(parts):]

        def total(p_ref):
            g = p_ref[0].astype(F32)
            for s in range(1, N_DEV):
                g = g + p_ref[s].astype(F32)
            return g

        g = jnp.concatenate([total(p_ref) for p_ref in p_refs], axis=1) if len(parts) > 1 else total(p_refs[0])
        g_ref[...] = g
        d_ref[...], nm_ref[...], nv_ref[...] = _adamw_math(w_ref[...], m_ref[...], v_ref[...], g)

    blk = pl.BlockSpec((tr, cols), lambda i: (i, 0))
    return pl.pallas_call(
        body, name=name, grid=(rows // tr,),
        in_specs=[blk, blk, blk] + [pl.BlockSpec((N_DEV, tr, p.shape[2]), lambda i: (0, i, 0)) for p in parts],
        out_specs=[blk] * 4,
        out_shape=[jax.ShapeDtypeStruct((rows, cols), F32)] * 4,
        compiler_params=_params(("parallel",)),
    )(w, m, v, *parts)


def _cols_from_shards(g):
    return jnp.transpose(g, (1, 0, 2)).reshape(g.shape[1], N_DEV * g.shape[2])


def _cols_to_shards(a):
    r, c = a.shape
    return jnp.transpose(a.reshape(r, N_DEV, c // N_DEV), (1, 0, 2))


def _block_diag(w):
    per = LRU_HALF // LRU_BLOCK
    w = w.reshape(2, per, LRU_BLOCK, LRU_BLOCK)
    eye = jnp.eye(per, dtype=w.dtype)
    return (w[:, :, :, None, :] * eye[None, :, None, :, None]).reshape(2, LRU_HALF, LRU_HALF)


def _block_diag_extract(t):
    per = LRU_HALF // LRU_BLOCK
    t = t.reshape(2, per, LRU_BLOCK, per, LRU_BLOCK)
    return jnp.stack([t[:, b, :, b, :] for b in range(per)], axis=1).reshape(LRU_BLOCKS, LRU_BLOCK, LRU_BLOCK)


def kernel(x, meta_tokens, g_pre_mix, w_in, conv_w, conv_b, w_a, b_a, w_x, b_x, lru_lambda, attn_sinks, w_out, g_post_mix, g_pre_ffn, w_ff1, w_ff2, g_post_ffn, loss_target, m_meta_tokens, m_g_pre_mix, m_w_in, m_conv_w, m_conv_b, m_w_a, m_b_a, m_w_x, m_b_x, m_lru_lambda, m_attn_sinks, m_w_out, m_g_post_mix, m_g_pre_ffn, m_w_ff1, m_w_ff2, m_g_post_ffn, v_meta_tokens, v_g_pre_mix, v_w_in, v_conv_w, v_conv_b, v_w_a, v_b_a, v_w_x, v_b_x, v_lru_lambda, v_attn_sinks, v_w_out, v_g_post_mix, v_g_pre_ffn, v_w_ff1, v_w_ff2, v_g_post_ffn):
    weights = dict(meta_tokens=meta_tokens, g_pre_mix=g_pre_mix, w_in=w_in, conv_w=conv_w, conv_b=conv_b, w_a=w_a,
                   b_a=b_a, w_x=w_x, b_x=b_x, lru_lambda=lru_lambda, attn_sinks=attn_sinks, w_out=w_out,
                   g_post_mix=g_post_mix, g_pre_ffn=g_pre_ffn, w_ff1=w_ff1, w_ff2=w_ff2, g_post_ffn=g_post_ffn)
    mom_m = dict(meta_tokens=m_meta_tokens, g_pre_mix=m_g_pre_mix, w_in=m_w_in, conv_w=m_conv_w, conv_b=m_conv_b,
                 w_a=m_w_a, b_a=m_b_a, w_x=m_w_x, b_x=m_b_x, lru_lambda=m_lru_lambda, attn_sinks=m_attn_sinks,
                 w_out=m_w_out, g_post_mix=m_g_post_mix, g_pre_ffn=m_g_pre_ffn, w_ff1=m_w_ff1, w_ff2=m_w_ff2,
                 g_post_ffn=m_g_post_ffn)
    mom_v = dict(meta_tokens=v_meta_tokens, g_pre_mix=v_g_pre_mix, w_in=v_w_in, conv_w=v_conv_w, conv_b=v_conv_b,
                 w_a=v_w_a, b_a=v_b_a, w_x=v_w_x, b_x=v_b_x, lru_lambda=v_lru_lambda, attn_sinks=v_attn_sinks,
                 w_out=v_w_out, g_post_mix=v_g_post_mix, g_pre_ffn=v_g_pre_ffn, w_ff1=v_w_ff1, w_ff2=v_w_ff2,
                 g_post_ffn=v_g_post_ffn)
    order = list(weights)

    (g_win, g_meta, g_cw) = _gather_two_level([w_in[0].astype(BF16), meta_tokens, conv_w[0]], "gather_first")
    w_in_full = _cols_from_shards(g_win)
    meta_full = _cols_from_shards(g_meta)
    conv_w_full = _cols_from_shards(g_cw)

    head = jnp.concatenate([jnp.zeros((PAD_ROWS, D_MODEL), F32), meta_full], axis=0)
    wa_bd = _block_diag(w_a[0]).astype(BF16)
    wx_bd = _block_diag(w_x[0]).astype(BF16)
    bias = _attn_bias()

    w1_shard = w_ff1[0].astype(BF16)
    (qkv, zrec, u1), (g_wout,) = _in_proj_fwd(head, x[0], g_pre_mix, w_in_full, [w_out[0].astype(BF16)], ["gather"])
    (attn,), (w1a,) = _attn_fwd(qkv, attn_sinks, bias, [w1_shard[:, :FF_HALF]], ["gather"])
    (rec, h_lru, kept), (w1b,) = _rec_fwd(zrec, conv_w_full, conv_b, wa_bd, b_a, wx_bd, b_x, lru_lambda,
                                         [w1_shard[:, FF_HALF:]], ["gather"])
    w_out_full = g_wout.reshape(D_MODEL, D_MODEL)
    w2_shard = w_ff2[0].astype(BF16)
    (mix, h1), (w2a,) = _out_proj_fwd(attn, rec, w_out_full, head, x[0], g_post_mix, [w2_shard[:FF_HALF]], ["gather"])
    (act, u2), (w2b,) = _ffn_up(h1, g_pre_ffn, (w1a, w1b), [w2_shard[FF_HALF:]], ["gather"])
    w2_halves = [w.reshape(D_FF // 2, D_MODEL) for w in (w2a, w2b)]
    (dy, df, dg_post_ffn, loss_acc), w2t_halves = _ffn_down_loss(
        act, w2_halves, h1, loss_target[0], g_post_ffn, [w2_shard[:FF_HALF].T, w2_shard[FF_HALF:].T], ["gather"] * 2)

    (da1,), (w1ta,) = _ffn_bwd_act(df, w2t_halves, act, [w1_shard[:, :FF_HALF].T], ["gather"])
    (dw1h, dw2g), (w1tb,) = _ffn_bwd_weights(u2, da1, act, df, [w1_shard[:, FF_HALF:].T], ["gather"])
    w1t_halves = [w.reshape(D_FF // 2, D_MODEL) for w in (w1ta, w1tb)]
    (dh1, dg_pre_ffn), (p_w1a,) = _ffn_bwd_x(da1, w1t_halves, h1, dy, g_pre_ffn, [dw1h[0]], ["scatter"])
    (dattn, drec, dw_out, dg_post_mix), (p_w1b,) = _out_proj_bwd(dh1, mix, g_post_mix, w_out_full.T, attn, rec,
                                                                [dw1h[1]], ["scatter"])
    (dq, dkv_late, dsinks), (p_w2,) = _attn_bwd(qkv, dattn, attn_sinks, bias, [dw2g], ["scatter"])
    dkv = dkv_late[BLOCK:BLOCK + qkv.shape[0]]
    (drz, rec_small, dwa_bd, dwx_bd), (p_wout,) = _rec_bwd(
        drec, zrec, h_lru, kept, conv_w_full, wa_bd, wx_bd, lru_lambda,
        [dw_out.reshape(N_DEV, D_MODEL // N_DEV, D_MODEL)], ["scatter"])
    small_grads = dict(
        conv_b=rec_small[ROW_CONV_B], b_a=rec_small[ROW_B_A], b_x=rec_small[ROW_B_X], lru_lambda=rec_small[ROW_LAMBDA],
        attn_sinks=dsinks[:, 0], g_post_mix=dg_post_mix, g_pre_ffn=dg_pre_ffn, g_post_ffn=dg_post_ffn)
    gate_rows = (LRU_BLOCKS * LRU_BLOCK, LRU_BLOCK)
    gate_dense = (LRU_BLOCKS * LRU_BLOCK * LRU_BLOCK // PACK_WIDTH, PACK_WIDTH)
    (dw_in,), (p_cw, p_small, p_wa, p_wx) = _in_proj_bwd_w(
        u1, dq, dkv, drz,
        [_cols_to_shards(rec_small[0:CONV_WIDTH]), _pack_rows(small_grads),
         _block_diag_extract(dwa_bd).reshape(gate_dense), _block_diag_extract(dwx_bd).reshape(gate_dense)],
        ["scatter", "gather", "gather", "gather"])
    p_wa, p_wx = (p.reshape((N_DEV,) + gate_rows) for p in (p_wa, p_wx))
    (dh0, dg_pre_mix), (p_win,) = _in_proj_bwd_x(
        head, x[0], g_pre_mix, dh1, dq, dkv, drz, w_in_full.T, [_cols_to_shards(dw_in)], ["scatter"])
    p_meta, p_gpm, p_loss = _exchange([_cols_to_shards(dh0[PAD_ROWS:BLOCK]), dg_pre_mix, loss_acc],
                                      ["scatter", "gather", "gather"], "exchange_last")

    res = {}
    res["g_pre_mix"] = _adamw(g_pre_mix, m_g_pre_mix, v_g_pre_mix, p_gpm, "adamw_g_pre_mix")
    res["w_in"] = _adamw(w_in[0], m_w_in[0], v_w_in[0], p_win, "adamw_w_in")
    res["w_out"] = _adamw(w_out[0], m_w_out[0], v_w_out[0], p_wout, "adamw_w_out")
    res["w_ff1"] = _adamw(w_ff1[0], m_w_ff1[0], v_w_ff1[0], [p_w1a, p_w1b], "adamw_w_ff1")
    res["w_ff2"] = _adamw(w_ff2[0], m_w_ff2[0], v_w_ff2[0], p_w2, "adamw_w_ff2")
    res["meta_tokens"] = _adamw(meta_tokens, m_meta_tokens, v_meta_tokens, p_meta, "adamw_meta")
    res["conv_w"] = _adamw(conv_w[0], m_conv_w[0], v_conv_w[0], p_cw, "adamw_conv_w")
    for name in ("w_in", "w_out", "w_ff1", "w_ff2", "conv_w"):
        res[name] = tuple(t[None] for t in res[name])
    for name, parts in (("w_a", p_wa), ("w_x", p_wx)):
        gate = _adamw(*(src[name].reshape(gate_rows) for src in (weights, mom_m, mom_v)), parts, "adamw_" + name)
        res[name] = tuple(t.reshape(weights[name].shape) for t in gate)
    loss_total, small = _adamw_small(weights, mom_m, mom_v, p_small, p_loss)
    res.update(small)

    grad_x = dh0[BLOCK:][None]
    outs = [loss_total[0, 0], grad_x]
    for k in range(4):
        outs += [res[name][k] for name in order]
    return tuple(outs)
```

```python
import jax
import jax.numpy as jnp
import numpy as np
from jax import lax
from jax.experimental import pallas as pl
from jax.experimental.pallas import tpu as pltpu

F32 = jnp.float32
BF16 = jnp.bfloat16

D_MODEL = 1024
N_META = 16
HEAD_DIM = 64
ATTN_HEADS = 8
KV_HEADS = 2
GQA_GROUP = ATTN_HEADS // KV_HEADS
ATTN_WIDTH = ATTN_HEADS * HEAD_DIM
KV_WIDTH = KV_HEADS * HEAD_DIM
QKV_WIDTH = ATTN_WIDTH + 2 * KV_WIDTH
LRU_WIDTH = 512
LRU_BLOCKS = 8
LRU_BLOCK = 64
LRU_HALF = 256
LRU_C = 8.0
CONV_WIDTH = 4
BLOCK = 128
PAD_ROWS = BLOCK - N_META
IN_WIDTH = QKV_WIDTH + 2 * LRU_WIDTH
D_FF = 4096
EPS = 1e-6
NEG = -1e30
N_DEV = 8
FF_CHUNK = D_FF // N_DEV
SUBLANES = 8
LANES = 128

ADAM_LR = 0.001
ADAM_B1 = 0.9
ADAM_B2 = 0.999
ADAM_EPS = 1e-08
ADAM_WD = 0.01
ADAM_STEP = 10

VMEM_LIMIT = 56 * 1024 * 1024


def _row_tile(rows):
    for t in (640, 512, 256, 128):
        if rows % t == 0:
            return t
    raise ValueError(rows)


def _big_tile(rows):
    for t in (1664, 1024, 512, 256, 128):
        if rows % t == 0:
            return t
    raise ValueError(rows)


def _rec_tile(rows):
    for t in (640, 256, 128):
        if rows % t == 0:
            return t
    raise ValueError(rows)


def _params(semantics):
    return pltpu.CompilerParams(dimension_semantics=semantics, vmem_limit_bytes=VMEM_LIMIT)


def _mm(a, b):
    return lax.dot_general(a, b, (((1,), (0,)), ((), ())), preferred_element_type=F32)


def _mm_nt(a, b):
    return lax.dot_general(a, b, (((1,), (1,)), ((), ())), preferred_element_type=F32)


def _mm_tn(a, b):
    return lax.dot_general(a, b, (((0,), (0,)), ((), ())), preferred_element_type=F32)


def _rms_fwd(x, g):
    rstd = lax.rsqrt(jnp.mean(x * x, axis=-1, keepdims=True) + EPS)
    xhat = x * rstd
    return xhat * g, xhat, rstd


def _rms_bwd(dy, xhat, rstd, g):
    dyg = dy * g
    c = jnp.mean(dyg * xhat, axis=-1, keepdims=True)
    dx = rstd * (dyg - xhat * c)
    dg = jnp.sum(dy * xhat, axis=0, keepdims=True)
    return dx, dg


def _sigmoid(x):
    return pl.reciprocal(1.0 + jnp.exp(-x), approx=True)


def _log1p(x):
    u = 1.0 + x
    return jnp.where(u == 1.0, x, jnp.log(u) * x / (u - 1.0))


def _one_minus_sq_exp(x, ex):
    return -jnp.tanh(x) * (1.0 + ex * ex)


TINY = 1e-30


def _sqrt_pos(y):
    r = lax.rsqrt(jnp.maximum(y, TINY))
    return y * r, r


def _softplus(x):
    return jnp.maximum(x, 0.0) + _log1p(jnp.exp(-jnp.abs(x)))


GELU_C = 0.7978845608028654
GELU_K = 0.044715


def _gelu(x):
    t = jnp.tanh(GELU_C * (x + GELU_K * x * x * x))
    return 0.5 * x * (1.0 + t), t


def _gelu_grad(x, t):
    return 0.5 * (1.0 + t) + 0.5 * x * (1.0 - t * t) * GELU_C * (1.0 + 3.0 * GELU_K * x * x)


def _full(shape):
    return pl.BlockSpec(shape, lambda *_: (0,) * len(shape))


def _resident(shape):
    return pl.BlockSpec(shape, lambda *_: (0,) * len(shape), pipeline_mode=pl.Buffered(1))


def _exchange_copies(ins, outs, sems, modes):
    send_sems, recv_sems, local_sems = sems
    x, y, c = lax.axis_index("x"), lax.axis_index("y"), lax.axis_index("c")
    me = 4 * x + 2 * y + c

    def block(a, dev):
        return ins[a] if modes[a] == "gather" else ins[a].at[dev]

    local = [pltpu.make_async_copy(block(a, me), outs[a].at[me], local_sems.at[a]) for a in range(len(ins))]
    sends, recvs = [], []
    for a in range(len(ins)):
        for k in range(N_DEV - 1):
            bits = k + 1
            px = jnp.bitwise_xor(x, (bits >> 2) & 1)
            py = jnp.bitwise_xor(y, (bits >> 1) & 1)
            pc = jnp.bitwise_xor(c, bits & 1)
            peer = 4 * px + 2 * py + pc
            common = dict(src_ref=block(a, peer), send_sem=send_sems.at[a, k], recv_sem=recv_sems.at[a, k],
                          device_id=(px, py, pc), device_id_type=pl.DeviceIdType.MESH)
            sends.append(pltpu.make_async_remote_copy(dst_ref=outs[a].at[me], **common))
            recvs.append(pltpu.make_async_remote_copy(dst_ref=outs[a].at[peer], **common))
    return local, sends, recvs


def _exchange_start(ins, outs, sems, modes):
    local, sends, _ = _exchange_copies(ins, outs, sems, modes)
    for cp in local + sends:
        cp.start()


def _exchange_wait(ins, outs, sems, modes):
    local, sends, recvs = _exchange_copies(ins, outs, sems, modes)
    for cp in recvs:
        cp.wait_recv()
    for cp in sends:
        cp.wait_send()
    for cp in local:
        cp.wait()


def _exchange_shapes(arrays, modes):
    return [jax.ShapeDtypeStruct((N_DEV,) + a.shape if mode == "gather" else a.shape, a.dtype)
            for a, mode in zip(arrays, modes)]


def _exchange_sems(na):
    return [pltpu.SemaphoreType.DMA((na, N_DEV - 1)), pltpu.SemaphoreType.DMA((na, N_DEV - 1)),
            pltpu.SemaphoreType.DMA((na,))]


ANY_SPACE = pl.BlockSpec(memory_space=pl.ANY)


def _exchange(arrays, modes, name):
    na = len(arrays)

    def body(*refs):
        ins, outs, sems = refs[:na], refs[na:2 * na], refs[2 * na:]
        _exchange_start(ins, outs, sems, modes)
        _exchange_wait(ins, outs, sems, modes)

    return pl.pallas_call(
        body, name=name, out_shape=_exchange_shapes(arrays, modes),
        in_specs=[ANY_SPACE] * na, out_specs=[ANY_SPACE] * na, scratch_shapes=_exchange_sems(na),
        compiler_params=pltpu.CompilerParams(has_side_effects=True),
    )(*arrays)


def _gather_two_level(arrays, name):
    na = len(arrays)

    def body(*refs):
        ins, outs = refs[:na], refs[na:2 * na]
        send_sems, recv_sems, local_sems = refs[2 * na:]
        x, y, c = lax.axis_index("x"), lax.axis_index("y"), lax.axis_index("c")
        me, sibling = (x, y, c), (x, y, 1 - c)
        chips = [(1 - x, y), (x, 1 - y), (1 - x, 1 - y)]

        def copy(a, k, block, to, src=None):
            slot = outs[a].at[4 * block[0] + 2 * block[1] + block[2]]
            return pltpu.make_async_remote_copy(
                src_ref=slot if src is None else src, dst_ref=slot, send_sem=send_sems.at[a, k],
                recv_sem=recv_sems.at[a, k], device_id=to, device_id_type=pl.DeviceIdType.MESH)

        local = [pltpu.make_async_copy(ins[a], outs[a].at[4 * x + 2 * y + c], local_sems.at[a]) for a in range(na)]
        first = []
        for a in range(na):
            first.append(copy(a, 0, me, sibling, src=ins[a]))
            first += [copy(a, 1 + j, me, (*chip, c), src=ins[a]) for j, chip in enumerate(chips)]
        for cp in local + first:
            cp.start()
        passed = []
        for j, chip in enumerate(chips):
            for a in range(na):
                copy(a, 1 + j, (*chip, c), me).wait_recv()
                passed.append(copy(a, 4 + j, (*chip, c), sibling))
                passed[-1].start()
        for a in range(na):
            copy(a, 0, sibling, me).wait_recv()
            for j, chip in enumerate(chips):
                copy(a, 4 + j, (*chip, 1 - c), me).wait_recv()
        for cp in first + passed:
            cp.wait_send()
        for cp in local:
            cp.wait()

    return pl.pallas_call(
        body, name=name, out_shape=_exchange_shapes(arrays, ["gather"] * na),
        in_specs=[ANY_SPACE] * na, out_specs=[ANY_SPACE] * na, scratch_shapes=_exchange_sems(na),
        compiler_params=pltpu.CompilerParams(has_side_effects=True),
    )(*arrays)


def _hosting_call(body, name, steps, in_specs, out_specs, out_shape, scratch_shapes, args, arrays, modes):
    n_in, n_out, n_scr, na = len(in_specs), len(out_specs), len(scratch_shapes), len(arrays)
    grid = steps if isinstance(steps, tuple) else (steps,)

    def hosting_body(*refs):
        cuts = [0]
        for n in (n_in, na, n_out, na, n_scr, 3):
            cuts.append(cuts[-1] + n)
        ins, x_ins, outs, x_outs, scr, sems = (refs[cuts[p]:cuts[p + 1]] for p in range(6))
        first, last = True, True
        for axis, n in enumerate(grid):
            first = first & (pl.program_id(axis) == 0)
            last = last & (pl.program_id(axis) == n - 1)

        @pl.when(first)
        def _():
            _exchange_start(x_ins, x_outs, sems, modes)

        body(*ins, *outs, *scr)

        @pl.when(last)
        def _():
            _exchange_wait(x_ins, x_outs, sems, modes)

    res = pl.pallas_call(
        hosting_body, name=name, grid=grid,
        in_specs=list(in_specs) + [ANY_SPACE] * na, out_specs=list(out_specs) + [ANY_SPACE] * na,
        out_shape=list(out_shape) + _exchange_shapes(arrays, modes),
        scratch_shapes=list(scratch_shapes) + _exchange_sems(na),
        compiler_params=_params(("arbitrary",) * len(grid)),
    )(*args, *arrays)
    return res[:n_out], res[n_out:]


def _frame_rows(src_hbm, buf, sem, i, steps, tm):
    def first():
        return pltpu.make_async_copy(src_hbm.at[pl.ds(0, tm - BLOCK)], buf.at[0, pl.ds(BLOCK, tm - BLOCK)], sem.at[0])

    def later(t, slot):
        return pltpu.make_async_copy(src_hbm.at[pl.ds(pl.multiple_of(t * tm - BLOCK, SUBLANES), tm)], buf.at[slot], sem.at[slot])

    slot = i % 2

    @pl.when(i == 0)
    def _():
        first().start()

    @pl.when(i + 1 < steps)
    def _():
        later(i + 1, 1 - slot).start()

    @pl.when(i == 0)
    def _():
        first().wait()

    @pl.when(i > 0)
    def _():
        later(i, slot).wait()

    return slot


def _frame_scratch(tm):
    return [pltpu.VMEM((2, tm, D_MODEL), F32), pltpu.SemaphoreType.DMA((2,))]


def _h0_tile(head_ref, x_hbm, buf, sem, i, steps, tm):
    slot = _frame_rows(x_hbm, buf, sem, i, steps, tm)

    @pl.when(i == 0)
    def _():
        buf[0, 0:BLOCK, :] = head_ref[...]

    return buf[slot]


def _in_proj_fwd(head, x, g1, w_in, carried, modes):
    rows = BLOCK + x.shape[0]
    tm = _row_tile(rows)
    steps = rows // tm

    def body(head_ref, g_ref, w_ref, x_hbm, qkv_ref, zrec_ref, u_ref, buf, sem):
        h = _h0_tile(head_ref, x_hbm, buf, sem, pl.program_id(0), steps, tm)
        u, _, _ = _rms_fwd(h, g_ref[...])
        u = u.astype(BF16)
        u_ref[...] = u
        z = _mm(u, w_ref[...])
        qkv_ref[...] = z[:, :QKV_WIDTH].astype(BF16)
        zrec_ref[...] = z[:, QKV_WIDTH:]

    wide = pl.BlockSpec((tm, D_MODEL), lambda i: (i, 0))
    return _hosting_call(
        body, "in_proj_fwd", steps,
        [_full((BLOCK, D_MODEL)), _full((1, D_MODEL)), _resident((D_MODEL, IN_WIDTH)), ANY_SPACE],
        [pl.BlockSpec((tm, QKV_WIDTH), lambda i: (i, 0)), pl.BlockSpec((tm, 2 * LRU_WIDTH), lambda i: (i, 0)), wide],
        [jax.ShapeDtypeStruct((rows, QKV_WIDTH), BF16), jax.ShapeDtypeStruct((rows, 2 * LRU_WIDTH), F32),
         jax.ShapeDtypeStruct((rows, D_MODEL), BF16)],
        _frame_scratch(tm), (head, g1, w_in, x), carried, modes)


N_BIAS = 3


def _attn_bias():
    key = np.arange(2 * BLOCK)[:, None]
    r = np.arange(GQA_GROUP * BLOCK)[None, :] % BLOCK
    band = (key > r) & (key <= r + BLOCK)
    out = [np.where(band & ((n - 1) * BLOCK + key >= PAD_ROWS), 0.0, NEG) for n in range(N_BIAS)]
    return jnp.asarray(np.stack(out), F32)


def _attn_probs(k2, q4, bias, sink_row):
    s = _mm_nt(k2, q4) * (HEAD_DIM ** -0.5) + bias
    m = jnp.maximum(jnp.max(s, axis=0, keepdims=True), sink_row)
    p = jnp.exp(s - m)
    es = jnp.exp(sink_row - m)
    inv = 1.0 / (jnp.sum(p, axis=0, keepdims=True) + es)
    return p * inv, es * inv


def _heads(ref, rows, first, count):
    return jnp.concatenate([ref[rows, (first + g) * HEAD_DIM:(first + g + 1) * HEAD_DIM] for g in range(count)], axis=0)


def _keys_of_block(prev_ref, cur_ref, b, kv):
    sl = slice(kv * HEAD_DIM, (kv + 1) * HEAD_DIM)
    before = prev_ref[:, sl] if b == 0 else cur_ref[(b - 1) * BLOCK:b * BLOCK, sl]
    return jnp.concatenate([before, cur_ref[b * BLOCK:(b + 1) * BLOCK, sl]], axis=0)


def _bias_of_block(bias_ref, block):
    return bias_ref[jnp.minimum(block, N_BIAS - 1)]


def _sink_row(sink_ref, kv):
    g = lax.broadcasted_iota(jnp.int32, (1, GQA_GROUP * BLOCK), 1) // BLOCK
    row = jnp.full((1, GQA_GROUP * BLOCK), sink_ref[0, kv * GQA_GROUP], F32)
    for i in range(1, GQA_GROUP):
        row = jnp.where(g == i, sink_ref[0, kv * GQA_GROUP + i], row)
    return row


def _from_head_major(pieces):
    return jnp.concatenate(pieces, axis=0).T


def _attn_specs(tm, tile_of):
    nbt = tm // BLOCK
    k_col, v_col = ATTN_WIDTH // KV_WIDTH, ATTN_WIDTH // KV_WIDTH + 1
    before = lambda i: jnp.maximum(tile_of(i) * nbt - 1, 0)
    return [pl.BlockSpec((tm, ATTN_WIDTH), lambda i: (tile_of(i), 0)),
            pl.BlockSpec((BLOCK, KV_WIDTH), lambda i: (before(i), k_col)),
            pl.BlockSpec((tm, KV_WIDTH), lambda i: (tile_of(i), k_col)),
            pl.BlockSpec((BLOCK, KV_WIDTH), lambda i: (before(i), v_col)),
            pl.BlockSpec((tm, KV_WIDTH), lambda i: (tile_of(i), v_col))]


def _attn_fwd(qkv, sinks, bias, carried, modes):
    rows = qkv.shape[0]
    tm = _big_tile(rows)
    nbt = tm // BLOCK

    def body(sink_ref, bias_ref, q_ref, kp_ref, kc_ref, vp_ref, vc_ref, o_ref):
        i = pl.program_id(0)
        for b in range(nbt):
            blk = slice(b * BLOCK, (b + 1) * BLOCK)
            bias_t = _bias_of_block(bias_ref, i * nbt + b)
            pieces = []
            for kv in range(KV_HEADS):
                k2 = _keys_of_block(kp_ref, kc_ref, b, kv)
                v2 = _keys_of_block(vp_ref, vc_ref, b, kv)
                q4 = _heads(q_ref, blk, kv * GQA_GROUP, GQA_GROUP)
                pn, _ = _attn_probs(k2, q4, bias_t, _sink_row(sink_ref, kv))
                ot = _mm_tn(v2, pn.astype(BF16))
                pieces += [ot[:, g * BLOCK:(g + 1) * BLOCK] for g in range(GQA_GROUP)]
            o_ref[blk, :] = _from_head_major(pieces).astype(BF16)

    return _hosting_call(
        body, "attn_fwd", rows // tm,
        [pl.BlockSpec(memory_space=pltpu.SMEM), _resident((N_BIAS, 2 * BLOCK, GQA_GROUP * BLOCK))]
        + _attn_specs(tm, lambda i: i),
        [pl.BlockSpec((tm, ATTN_WIDTH), lambda i: (i, 0))],
        [jax.ShapeDtypeStruct((rows, ATTN_WIDTH), BF16)],
        [], (sinks, bias, qkv, qkv, qkv, qkv, qkv), carried, modes)


def _conv_taps(xbuf, tm):
    return [xbuf[pl.ds(SUBLANES - (CONV_WIDTH - 1 - j), tm), :] for j in range(CONV_WIDTH)]


def _lru_halves(xc):
    return [xc[:, h * LRU_HALF:(h + 1) * LRU_HALF].astype(BF16) for h in range(2)]


def _lru_gates(xc, wa_ref, ba_ref, wx_ref, bx_ref, lam_ref):
    halves = _lru_halves(xc)
    gate_r = jnp.concatenate([_mm(halves[h], wa_ref[h]) for h in range(2)], axis=1) + ba_ref[...]
    gate_i = jnp.concatenate([_mm(halves[h], wx_ref[h]) for h in range(2)], axis=1) + bx_ref[...]
    r = _sigmoid(gate_r)
    ig = _sigmoid(gate_i)
    log_a = (-LRU_C) * r * _softplus(-lam_ref[...])
    a = jnp.exp(log_a)
    mult, _ = _sqrt_pos(_one_minus_sq_exp(log_a, a))
    return r, ig, a, mult


KEPT_XC, KEPT_A, KEPT_MULT, KEPT_R, KEPT_I, N_KEPT = 0, 1, 2, 3, 4, 5


def _scan_tile(a_ref, u_ref, out_ref, carry, tm):
    row = lax.broadcasted_iota(jnp.int32, (SUBLANES, LRU_WIDTH), 0)

    def step(j, before):
        o = pl.multiple_of(j * SUBLANES, SUBLANES)
        a = a_ref[pl.ds(o, SUBLANES), :]
        u = u_ref[pl.ds(o, SUBLANES), :]
        for s in (1, 2, 4):
            keep = row >= s
            u = jnp.where(keep, a * pltpu.roll(u, s, 0) + u, u)
            a = jnp.where(keep, a * pltpu.roll(a, s, 0), a)
        out = a * before + u
        out_ref[pl.ds(o, SUBLANES), :] = out
        return out[SUBLANES - 1:SUBLANES, :]

    return lax.fori_loop(0, tm // SUBLANES, step, carry)


def _rec_fwd(zrec, conv_w, conv_b, wa_bd, b_a, wx_bd, b_x, lam, carried, modes):
    rows = zrec.shape[0]
    tm = _row_tile(rows)

    def body(xr_ref, yr_ref, cw_ref, cb_ref, wa_ref, ba_ref, wx_ref, bx_ref, lam_ref, rec_ref, h_ref, kept_ref,
             xbuf, a_s, u_s, carry):
        i = pl.program_id(0)

        @pl.when(i == 0)
        def _():
            xbuf[0:SUBLANES, :] = jnp.zeros((SUBLANES, LRU_WIDTH), F32)
            carry[...] = jnp.zeros_like(carry)

        @pl.when(i > 0)
        def _():
            xbuf[0:SUBLANES, :] = xbuf[tm:tm + SUBLANES, :]

        xbuf[SUBLANES:SUBLANES + tm, :] = xr_ref[...]
        taps = _conv_taps(xbuf, tm)
        xc = cb_ref[...] + sum(cw_ref[j:j + 1, :] * taps[j] for j in range(CONV_WIDTH))
        r, ig, a, mult = _lru_gates(xc, wa_ref, ba_ref, wx_ref, bx_ref, lam_ref)
        for k, val in ((KEPT_XC, xc), (KEPT_A, a), (KEPT_MULT, mult), (KEPT_R, r), (KEPT_I, ig)):
            kept_ref[:, k * LRU_WIDTH:(k + 1) * LRU_WIDTH] = val
        grow = i * tm + lax.broadcasted_iota(jnp.int32, (tm, LRU_WIDTH), 0)
        a_s[...] = a
        u_s[...] = jnp.where(grow >= PAD_ROWS, mult * (ig * xc), 0.0)
        carry[0:1, :] = _scan_tile(a_s, u_s, h_ref, carry[0:1, :], tm)
        gel, _ = _gelu(yr_ref[...])
        rec_ref[...] = (gel * h_ref[...]).astype(BF16)

    vec = _full((1, LRU_WIDTH))
    bd = _full((2, LRU_HALF, LRU_HALF))
    return _hosting_call(
        body, "rec_fwd", rows // tm,
        [pl.BlockSpec((tm, LRU_WIDTH), lambda i: (i, 0)), pl.BlockSpec((tm, LRU_WIDTH), lambda i: (i, 1)),
         _full((CONV_WIDTH, LRU_WIDTH)), vec, bd, vec, bd, vec, vec],
        [pl.BlockSpec((tm, LRU_WIDTH), lambda i: (i, 0))] * 2 + [pl.BlockSpec((tm, N_KEPT * LRU_WIDTH), lambda i: (i, 0))],
        [jax.ShapeDtypeStruct((rows, LRU_WIDTH), BF16), jax.ShapeDtypeStruct((rows, LRU_WIDTH), F32),
         jax.ShapeDtypeStruct((rows, N_KEPT * LRU_WIDTH), F32)],
        [pltpu.VMEM((tm + SUBLANES, LRU_WIDTH), F32), pltpu.VMEM((tm, LRU_WIDTH), F32),
         pltpu.VMEM((tm, LRU_WIDTH), F32), pltpu.VMEM((SUBLANES, LRU_WIDTH), F32)],
        (zrec, zrec, conv_w, conv_b, wa_bd, b_a, wx_bd, b_x, lam), carried, modes)


def _out_proj_fwd(attn, rec, w_out, head, x, g2, carried, modes):
    rows = attn.shape[0]
    tm = _row_tile(rows)
    steps = rows // tm

    def body(attn_ref, rec_ref, w_ref, head_ref, g_ref, x_hbm, mix_ref, h1_ref, buf, sem):
        h0 = _h0_tile(head_ref, x_hbm, buf, sem, pl.program_id(0), steps, tm)
        mix = _mm(attn_ref[...], w_ref[0:ATTN_WIDTH, :]) + _mm(rec_ref[...], w_ref[ATTN_WIDTH:, :])
        y, _, _ = _rms_fwd(mix, g_ref[...])
        mix_ref[...] = mix
        h1_ref[...] = h0 + y

    half = pl.BlockSpec((tm, ATTN_WIDTH), lambda i: (i, 0))
    wide = pl.BlockSpec((tm, D_MODEL), lambda i: (i, 0))
    return _hosting_call(
        body, "out_proj_fwd", steps,
        [half, half, _resident((D_MODEL, D_MODEL)), _full((BLOCK, D_MODEL)), _full((1, D_MODEL)), ANY_SPACE],
        [wide, wide],
        [jax.ShapeDtypeStruct((rows, D_MODEL), F32)] * 2,
        _frame_scratch(tm), (attn, rec, w_out, head, g2, x), carried, modes)


FF_COLS = 1024
FF_HALF = FF_CHUNK // 2


def _hidden_at(d, half):
    return half * (D_FF // 2) + d * FF_HALF


def _ffn_up(h1, g3, w1_halves, carried, modes):
    rows = h1.shape[0]
    tm = _row_tile(rows)

    def body(h_ref, g_ref, wa_ref, wb_ref, act_ref, u_ref):
        u, _, _ = _rms_fwd(h_ref[...], g_ref[...])
        u = u.astype(BF16)
        u_ref[...] = u
        for half, w_ref in enumerate((wa_ref, wb_ref)):
            for d in range(N_DEV):
                c = _hidden_at(d, half)
                a1 = jnp.maximum(_mm(u, w_ref[d]), 0.0)
                act_ref[:, c:c + FF_HALF] = (a1 * a1).astype(BF16)

    wide = pl.BlockSpec((tm, D_MODEL), lambda i: (i, 0))
    return _hosting_call(
        body, "ffn_up", rows // tm,
        [wide, _full((1, D_MODEL))] + [_resident((N_DEV, D_MODEL, FF_HALF))] * 2,
        [pl.BlockSpec((tm, D_FF), lambda i: (i, 0)), wide],
        [jax.ShapeDtypeStruct((rows, D_FF), BF16), jax.ShapeDtypeStruct((rows, D_MODEL), BF16)],
        [], (h1, g3, *w1_halves), carried, modes)


def _ffn_down_loss(act, w2_halves, h1, target, g4, carried, modes):
    rows = h1.shape[0]
    tm = _row_tile(rows)
    steps = rows // tm
    kh = D_FF // 2

    def body(act_ref, wa_ref, wb_ref, h_ref, g_ref, t_hbm, dy_ref, df_ref, dg_ref, loss_ref, buf, sem):
        i = pl.program_id(0)
        slot = _frame_rows(t_hbm, buf, sem, i, steps, tm)

        @pl.when(i == 0)
        def _():
            dg_ref[...] = jnp.zeros_like(dg_ref)
            loss_ref[...] = jnp.zeros_like(loss_ref)
            buf[0, 0:BLOCK, :] = jnp.zeros((BLOCK, D_MODEL), F32)

        g = g_ref[...]
        f = _mm(act_ref[:, :kh], wa_ref[...]) + _mm(act_ref[:, kh:], wb_ref[...])
        y, fhat, rstd = _rms_fwd(f, g)
        grow = i * tm + lax.broadcasted_iota(jnp.int32, (tm, D_MODEL), 0)
        err = jnp.where(grow >= BLOCK, h_ref[...] + y - buf[slot], 0.0)
        loss_ref[...] += (0.5 / D_MODEL) * jnp.sum(err * err)
        dy = err * (1.0 / D_MODEL)
        df, dg = _rms_bwd(dy, fhat, rstd, g)
        dy_ref[...] = dy
        df_ref[...] = df.astype(BF16)
        dg_ref[...] += dg

    wide = pl.BlockSpec((tm, D_MODEL), lambda i: (i, 0))
    return _hosting_call(
        body, "ffn_down_loss", steps,
        [pl.BlockSpec((tm, D_FF), lambda i: (i, 0)), _resident((kh, D_MODEL)), _resident((kh, D_MODEL)), wide,
         _full((1, D_MODEL)), ANY_SPACE],
        [wide, wide, _full((1, D_MODEL)), _full((SUBLANES, LANES))],
        [jax.ShapeDtypeStruct((rows, D_MODEL), F32), jax.ShapeDtypeStruct((rows, D_MODEL), BF16),
         jax.ShapeDtypeStruct((1, D_MODEL), F32), jax.ShapeDtypeStruct((SUBLANES, LANES), F32)],
        _frame_scratch(tm), (act, *w2_halves, h1, g4, target), carried, modes)


def _ffn_bwd_act(df, w2t_halves, act, carried, modes):
    rows = df.shape[0]
    tm = _row_tile(rows)

    def body(df_ref, wa_ref, wb_ref, act_ref, da_ref):
        df_t = df_ref[...]
        for half, w_ref in enumerate((wa_ref, wb_ref)):
            for d in range(N_DEV):
                cols = slice(_hidden_at(d, half), _hidden_at(d, half) + FF_HALF)
                dact = _mm(df_t, w_ref[d])
                relu_a1, _ = _sqrt_pos(act_ref[:, cols].astype(F32))
                da_ref[:, cols] = (dact * (2.0 * relu_a1)).astype(BF16)

    hidden = pl.BlockSpec((tm, D_FF), lambda i: (i, 0))
    return _hosting_call(
        body, "ffn_bwd_act", rows // tm,
        [pl.BlockSpec((tm, D_MODEL), lambda i: (i, 0))] + [_resident((N_DEV, D_MODEL, FF_HALF))] * 2 + [hidden],
        [hidden],
        [jax.ShapeDtypeStruct((rows, D_FF), BF16)],
        [], (df, *w2t_halves, act), carried, modes)


def _ffn_bwd_x(da, w1t_halves, h1, dy, g3, carried, modes):
    rows = h1.shape[0]
    tm = _row_tile(rows)
    kh = D_FF // 2

    def body(da_ref, wa_ref, wb_ref, h_ref, dy_ref, g_ref, dh_ref, dg_ref):
        @pl.when(pl.program_id(0) == 0)
        def _():
            dg_ref[...] = jnp.zeros_like(dg_ref)

        g = g_ref[...]
        _, xhat, rstd = _rms_fwd(h_ref[...], g)
        du = _mm(da_ref[:, :kh], wa_ref[...]) + _mm(da_ref[:, kh:], wb_ref[...])
        dx, dg = _rms_bwd(du, xhat, rstd, g)
        dh_ref[...] = dy_ref[...] + dx
        dg_ref[...] += dg

    wide = pl.BlockSpec((tm, D_MODEL), lambda i: (i, 0))
    return _hosting_call(
        body, "ffn_bwd_x", rows // tm,
        [pl.BlockSpec((tm, D_FF), lambda i: (i, 0)), _resident((kh, D_MODEL)), _resident((kh, D_MODEL)), wide, wide,
         _full((1, D_MODEL))],
        [wide, _full((1, D_MODEL))],
        [jax.ShapeDtypeStruct((rows, D_MODEL), F32), jax.ShapeDtypeStruct((1, D_MODEL), F32)],
        [], (da, *w1t_halves, h1, dy, g3), carried, modes)


def _ffn_bwd_weights(u2, da, act, df, carried, modes):
    rows = u2.shape[0]
    tb = _big_tile(rows)
    steps = rows // tb
    per = FF_COLS // FF_HALF

    def body(u_ref, da_ref, act_ref, df_ref, dw1_ref, dw2_ref, acc1, acc2):
        i = pl.program_id(1)

        @pl.when(i == 0)
        def _():
            acc1[...] = jnp.zeros_like(acc1)
            acc2[...] = jnp.zeros_like(acc2)

        acc1[...] += _mm_tn(u_ref[...], da_ref[...])
        acc2[...] += _mm_tn(act_ref[...], df_ref[...])

        @pl.when(i == steps - 1)
        def _():
            for p in range(per):
                c = p * FF_HALF
                dw1_ref[p] = acc1[:, c:c + FF_HALF].astype(BF16)
                dw2_ref[p] = acc2[c:c + FF_HALF, :].astype(BF16)

    wide = pl.BlockSpec((tb, D_MODEL), lambda j, i: (i, 0))
    chunk = pl.BlockSpec((tb, FF_COLS), lambda j, i: (i, j))
    return _hosting_call(
        body, "ffn_bwd_weights", (D_FF // FF_COLS, steps),
        [wide, chunk, chunk, wide],
        [pl.BlockSpec((None, per, D_MODEL, FF_HALF), lambda j, i: (j // 2, j % 2, 0, 0)),
         pl.BlockSpec((per, FF_HALF, D_MODEL), lambda j, i: (j % 2, j // 2, 0))],
        [jax.ShapeDtypeStruct((2, N_DEV, D_MODEL, FF_HALF), BF16), jax.ShapeDtypeStruct((N_DEV, FF_CHUNK, D_MODEL), BF16)],
        [pltpu.VMEM((D_MODEL, FF_COLS), F32), pltpu.VMEM((FF_COLS, D_MODEL), F32)],
        (u2, da, act, df), carried, modes)


def _out_proj_bwd(dh1, mix, g2, w_out_t, attn, rec, carried, modes):
    rows = dh1.shape[0]
    tm = _row_tile(rows)
    steps = rows // tm

    def body(dh_ref, mix_ref, g_ref, w_ref, attn_ref, rec_ref, dattn_ref, drec_ref, dw_ref, dg_ref, acc):
        i = pl.program_id(0)

        @pl.when(i == 0)
        def _():
            acc[...] = jnp.zeros_like(acc)
            dg_ref[...] = jnp.zeros_like(dg_ref)

        g = g_ref[...]
        _, xhat, rstd = _rms_fwd(mix_ref[...], g)
        dmix, dg = _rms_bwd(dh_ref[...], xhat, rstd, g)
        dmix = dmix.astype(BF16)
        dg_ref[...] += dg
        din = _mm(dmix, w_ref[...])
        dattn_ref[...] = din[:, :ATTN_WIDTH].astype(BF16)
        drec_ref[...] = din[:, ATTN_WIDTH:]
        acc[0:ATTN_WIDTH, :] += _mm_tn(attn_ref[...], dmix)
        acc[ATTN_WIDTH:, :] += _mm_tn(rec_ref[...], dmix)

        @pl.when(i == steps - 1)
        def _():
            dw_ref[...] = acc[...].astype(BF16)

    half = pl.BlockSpec((tm, ATTN_WIDTH), lambda i: (i, 0))
    wide = pl.BlockSpec((tm, D_MODEL), lambda i: (i, 0))
    return _hosting_call(
        body, "out_proj_bwd", steps,
        [wide, wide, _full((1, D_MODEL)), _resident((D_MODEL, D_MODEL)), half, half],
        [half, half, _full((D_MODEL, D_MODEL)), _full((1, D_MODEL))],
        [jax.ShapeDtypeStruct((rows, ATTN_WIDTH), BF16), jax.ShapeDtypeStruct((rows, LRU_WIDTH), F32),
         jax.ShapeDtypeStruct((D_MODEL, D_MODEL), BF16), jax.ShapeDtypeStruct((1, D_MODEL), F32)],
        [pltpu.VMEM((D_MODEL, D_MODEL), F32)],
        (dh1, mix, g2, w_out_t, attn, rec), carried, modes)


def _attn_bwd(qkv, dattn, sinks, bias, carried, modes):
    rows = qkv.shape[0]
    tm = _big_tile(rows)
    nbt, nt = tm // BLOCK, rows // tm

    def body(sink_ref, bias_ref, do_ref, q_ref, kp_ref, kc_ref, vp_ref, vc_ref, dq_ref, dkv_ref, dsink_ref, dk_c, dv_c):
        i = pl.program_id(0)

        @pl.when(i == 0)
        def _():
            dk_c[...] = jnp.zeros_like(dk_c)
            dv_c[...] = jnp.zeros_like(dv_c)
            dsink_ref[...] = jnp.zeros_like(dsink_ref)

        @pl.when(i < nt)
        def _():
            dk_late, dv_late = dk_c[...], dv_c[...]
            dsink_rows = [jnp.zeros((1, LANES), F32)] * ATTN_HEADS
            for b in range(nbt):
                blk = slice(b * BLOCK, (b + 1) * BLOCK)
                bias_t = _bias_of_block(bias_ref, i * nbt + b)
                dq_parts, dk_parts, dv_parts = [], [], []
                for kv in range(KV_HEADS):
                    k2 = _keys_of_block(kp_ref, kc_ref, b, kv)
                    v2 = _keys_of_block(vp_ref, vc_ref, b, kv)
                    q4 = _heads(q_ref, blk, kv * GQA_GROUP, GQA_GROUP)
                    do4 = _heads(do_ref, blk, kv * GQA_GROUP, GQA_GROUP)
                    pn, psink = _attn_probs(k2, q4, bias_t, _sink_row(sink_ref, kv))
                    dpn = _mm_nt(v2, do4)
                    delta = jnp.sum(pn * dpn, axis=0, keepdims=True)
                    ds = ((pn * (dpn - delta)) * (HEAD_DIM ** -0.5)).astype(BF16)
                    dqt = _mm_tn(k2, ds)
                    dq_parts += [dqt[:, g * BLOCK:(g + 1) * BLOCK] for g in range(GQA_GROUP)]
                    dk_parts.append(_mm(ds, q4))
                    dv_parts.append(_mm(pn.astype(BF16), do4))
                    sd = psink * delta
                    for g in range(GQA_GROUP):
                        h = kv * GQA_GROUP + g
                        dsink_rows[h] = dsink_rows[h] - jnp.sum(sd[:, g * BLOCK:(g + 1) * BLOCK])
                dq_ref[blk, :] = _from_head_major(dq_parts).astype(BF16)
                dk2 = jnp.concatenate(dk_parts, axis=1)
                dv2 = jnp.concatenate(dv_parts, axis=1)
                dkv_ref[blk, 0:KV_WIDTH] = (dk_late + dk2[0:BLOCK]).astype(BF16)
                dkv_ref[blk, KV_WIDTH:] = (dv_late + dv2[0:BLOCK]).astype(BF16)
                dk_late, dv_late = dk2[BLOCK:], dv2[BLOCK:]
            dk_c[...] = dk_late
            dv_c[...] = dv_late
            dsink_ref[...] += jnp.concatenate(dsink_rows, axis=0)

        @pl.when(i == nt)
        def _():
            dkv_ref[...] = jnp.zeros_like(dkv_ref)
            dkv_ref[0:BLOCK, 0:KV_WIDTH] = dk_c[...].astype(BF16)
            dkv_ref[0:BLOCK, KV_WIDTH:] = dv_c[...].astype(BF16)

    tile_of = lambda i: jnp.minimum(i, nt - 1)
    tile = pl.BlockSpec((tm, ATTN_WIDTH), lambda i: (tile_of(i), 0))
    return _hosting_call(
        body, "attn_bwd", nt + 1,
        [pl.BlockSpec(memory_space=pltpu.SMEM), _resident((N_BIAS, 2 * BLOCK, GQA_GROUP * BLOCK)), tile]
        + _attn_specs(tm, tile_of),
        [tile, pl.BlockSpec((tm, 2 * KV_WIDTH), lambda i: (i, 0)), _full((ATTN_HEADS, LANES))],
        [jax.ShapeDtypeStruct((rows, ATTN_WIDTH), BF16), jax.ShapeDtypeStruct((rows + tm, 2 * KV_WIDTH), BF16),
         jax.ShapeDtypeStruct((ATTN_HEADS, LANES), F32)],
        [pltpu.VMEM((BLOCK, KV_WIDTH), F32), pltpu.VMEM((BLOCK, KV_WIDTH), F32)],
        (sinks, bias, dattn, qkv, qkv, qkv, qkv, qkv), carried, modes)


ROW_CONV_B, ROW_B_A, ROW_B_X, ROW_LAMBDA = 4, 5, 6, 7


def _rec_bwd(drec, zrec, h, kept, conv_w, wa_bd, wx_bd, lam, carried, modes):
    rows = zrec.shape[0]
    tm = _rec_tile(rows)
    nt = rows // tm
    per = tm // SUBLANES

    def body(drec_ref, xr_ref, yr_ref, h_ref, xc_ref, a_ref, mult_ref, r_ref, ig_ref, hhalo_ref, cw_ref, wa_ref, wx_ref,
             lam_ref, drz_ref, small_ref, dwa_ref, dwx_ref, hbuf, dbuf, dgr_s, dgi_s, dyr_s, carry):
        s = pl.program_id(0)
        i = nt - 1 - s

        @pl.when(s == 0)
        def _():
            small_ref[...] = jnp.zeros_like(small_ref)
            dwa_ref[...] = jnp.zeros_like(dwa_ref)
            dwx_ref[...] = jnp.zeros_like(dwx_ref)
            carry[...] = jnp.zeros_like(carry)
            dbuf[tm:tm + SUBLANES, :] = jnp.zeros((SUBLANES, LRU_WIDTH), F32)

        hbuf[0:SUBLANES, :] = jnp.where(i == 0, 0.0, hhalo_ref[...])
        hbuf[SUBLANES:SUBLANES + tm, :] = h_ref[...]

        row = lax.broadcasted_iota(jnp.int32, (SUBLANES, LRU_WIDTH), 0)
        log_a_scale = (-LRU_C) * _softplus(-lam_ref[...])
        zeros = jnp.zeros((SUBLANES, LRU_WIDTH), F32)

        def group(k, state):
            g_later, a_later, sum_dgr, sum_dgi, sum_lam = state
            o = pl.multiple_of((per - 1 - k) * SUBLANES, SUBLANES)
            rows8 = pl.ds(o, SUBLANES)
            yr, drec_t, h_t, a = yr_ref[rows8, :], drec_ref[rows8, :], h_ref[rows8, :], a_ref[rows8, :]
            gel, t = _gelu(yr)
            dyr_s[rows8, :] = drec_t * h_t * _gelu_grad(yr, t)
            u = drec_t * gel
            coef = jnp.where(row == SUBLANES - 1, a_later, pltpu.roll(a, SUBLANES - 1, 0))
            for sft in (1, 2, 4):
                keep = row < SUBLANES - sft
                u = jnp.where(keep, coef * pltpu.roll(u, SUBLANES - sft, 0) + u, u)
                coef = jnp.where(keep, coef * pltpu.roll(coef, SUBLANES - sft, 0), coef)
            g = coef * g_later + u
            du = jnp.where(i * tm + o + row >= PAD_ROWS, g, 0.0)
            h_before = jnp.where(row == 0, hbuf[rows8, :][SUBLANES - 1:SUBLANES, :], pltpu.roll(h_t, 1, 0))
            xc, mult, r, ig = xc_ref[rows8, :], mult_ref[rows8, :], r_ref[rows8, :], ig_ref[rows8, :]
            dbuf[rows8, :] = du * (mult * ig)
            dgi = (du * (mult * xc)) * (ig * (1.0 - ig))
            dgi_s[rows8, :] = dgi
            dlog_a = (g * h_before) * a - (du * (ig * xc)) * (a * a * pl.reciprocal(mult, approx=True))
            dgr = (dlog_a * log_a_scale) * (r * (1.0 - r))
            dgr_s[rows8, :] = dgr
            return g[0:1, :], a[0:1, :], sum_dgr + dgr, sum_dgi + dgi, sum_lam + dlog_a * r

        state = lax.fori_loop(0, per, group, (carry[0:1, :], carry[1:2, :], zeros, zeros, zeros))
        carry[0:1, :], carry[1:2, :] = state[0], state[1]
        sum_dgr, sum_dgi, sum_lam = (jnp.sum(v, axis=0, keepdims=True) for v in state[2:])
        dlam = sum_lam * (LRU_C * _sigmoid(-lam_ref[...]))

        dgr_b = [dgr_s[:, hh * LRU_HALF:(hh + 1) * LRU_HALF].astype(BF16) for hh in range(2)]
        dgi_b = [dgi_s[:, hh * LRU_HALF:(hh + 1) * LRU_HALF].astype(BF16) for hh in range(2)]
        halves = _lru_halves(xc_ref[...])
        for hh in range(2):
            dwa_ref[hh] += _mm_tn(halves[hh], dgr_b[hh])
            dwx_ref[hh] += _mm_tn(halves[hh], dgi_b[hh])
        dxc = dbuf[0:tm, :] + jnp.concatenate(
            [_mm_nt(dgr_b[hh], wa_ref[hh]) + _mm_nt(dgi_b[hh], wx_ref[hh]) for hh in range(2)], axis=1)

        dbuf[0:tm, :] = dxc
        sum_dxc = jnp.sum(dxc, axis=0, keepdims=True)
        ahead = [dbuf[pl.ds(CONV_WIDTH - 1 - j, tm), :] for j in range(CONV_WIDTH)]
        drz_ref[:, 0:LRU_WIDTH] = sum(cw_ref[j:j + 1, :] * ahead[j] for j in range(CONV_WIDTH)).astype(BF16)
        drz_ref[:, LRU_WIDTH:] = dyr_s[...].astype(BF16)
        upd = [jnp.sum(xr_ref[...] * ahead[j], axis=0, keepdims=True) for j in range(CONV_WIDTH)]
        dbuf[tm:tm + SUBLANES, :] = dbuf[0:SUBLANES, :]
        small_ref[...] += jnp.concatenate(upd + [sum_dxc, sum_dgr, sum_dgi, dlam], axis=0)

    rev = lambda s: nt - 1 - s
    halo = lambda s: jnp.maximum(rev(s) * per - 1, 0)
    cols = lambda k: pl.BlockSpec((tm, LRU_WIDTH), lambda s: (rev(s), k))
    halo0 = pl.BlockSpec((SUBLANES, LRU_WIDTH), lambda s: (halo(s), 0))
    bd = _full((2, LRU_HALF, LRU_HALF))
    big = pltpu.VMEM((tm + SUBLANES, LRU_WIDTH), F32)
    tile = pltpu.VMEM((tm, LRU_WIDTH), F32)
    kept_cols = [cols(k) for k in (KEPT_XC, KEPT_A, KEPT_MULT, KEPT_R, KEPT_I)]
    return _hosting_call(
        body, "rec_bwd", nt,
        [cols(0), cols(0), cols(1), cols(0)] + kept_cols
        + [halo0, _full((CONV_WIDTH, LRU_WIDTH)), bd, bd, _full((1, LRU_WIDTH))],
        [pl.BlockSpec((tm, 2 * LRU_WIDTH), lambda s: (rev(s), 0)), _full((SUBLANES, LRU_WIDTH)), bd, bd],
        [jax.ShapeDtypeStruct((rows, 2 * LRU_WIDTH), BF16), jax.ShapeDtypeStruct((SUBLANES, LRU_WIDTH), F32),
         jax.ShapeDtypeStruct((2, LRU_HALF, LRU_HALF), F32), jax.ShapeDtypeStruct((2, LRU_HALF, LRU_HALF), F32)],
        [big, big, tile, tile, tile, pltpu.VMEM((SUBLANES, LRU_WIDTH), F32)],
        (drec, zrec, zrec, h) + (kept,) * N_KEPT + (h, conv_w, wa_bd, wx_bd, lam), carried, modes)


DZ_CUTS = (0, ATTN_WIDTH, QKV_WIDTH, IN_WIDTH)


def _dz_specs(tm):
    return [pl.BlockSpec((tm, DZ_CUTS[p + 1] - DZ_CUTS[p]), lambda i: (i, 0)) for p in range(3)]


def _in_proj_bwd_x(head, x, g1, dh1, dq, dkv, drz, w_in_t, carried, modes):
    rows = dh1.shape[0]
    tm = _row_tile(rows)
    steps = rows // tm

    def body(head_ref, g_ref, dh1_ref, dq_ref, dkv_ref, drz_ref, w_ref, x_hbm, dh0_ref, dg_ref, buf, sem):
        i = pl.program_id(0)
        h0 = _h0_tile(head_ref, x_hbm, buf, sem, i, steps, tm)

        @pl.when(i == 0)
        def _():
            dg_ref[...] = jnp.zeros_like(dg_ref)

        g = g_ref[...]
        _, xhat, rstd = _rms_fwd(h0, g)
        parts = (dq_ref[...], dkv_ref[...], drz_ref[...])
        du = sum(_mm(parts[p], w_ref[DZ_CUTS[p]:DZ_CUTS[p + 1], :]) for p in range(3))
        dx, dg = _rms_bwd(du, xhat, rstd, g)
        dh0_ref[...] = dh1_ref[...] + dx
        dg_ref[...] += dg

    wide = pl.BlockSpec((tm, D_MODEL), lambda i: (i, 0))
    return _hosting_call(
        body, "in_proj_bwd_x", steps,
        [_full((BLOCK, D_MODEL)), _full((1, D_MODEL)), wide] + _dz_specs(tm) + [_resident((IN_WIDTH, D_MODEL)), ANY_SPACE],
        [wide, _full((1, D_MODEL))],
        [jax.ShapeDtypeStruct((rows, D_MODEL), F32), jax.ShapeDtypeStruct((1, D_MODEL), F32)],
        _frame_scratch(tm), (head, g1, dh1, dq, dkv, drz, w_in_t, x), carried, modes)


def _in_proj_bwd_w(u1, dq, dkv, drz, carried, modes):
    rows = u1.shape[0]
    tb = _big_tile(rows)
    steps = rows // tb

    def body(u_ref, dq_ref, dkv_ref, drz_ref, dw_ref, acc):
        i = pl.program_id(0)

        @pl.when(i == 0)
        def _():
            acc[...] = jnp.zeros_like(acc)

        u = u_ref[...]
        for p, ref in enumerate((dq_ref, dkv_ref, drz_ref)):
            acc[:, DZ_CUTS[p]:DZ_CUTS[p + 1]] += _mm_tn(u, ref[...])

        @pl.when(i == steps - 1)
        def _():
            dw_ref[...] = acc[...].astype(BF16)

    return _hosting_call(
        body, "in_proj_bwd_w", steps,
        [pl.BlockSpec((tb, D_MODEL), lambda i: (i, 0))] + _dz_specs(tb),
        [_full((D_MODEL, IN_WIDTH))],
        [jax.ShapeDtypeStruct((D_MODEL, IN_WIDTH), BF16)],
        [pltpu.VMEM((D_MODEL, IN_WIDTH), F32)], (u1, dq, dkv, drz), carried, modes)


def _adamw_math(w, m, v, g):
    nm = ADAM_B1 * m + (1.0 - ADAM_B1) * g
    nv = ADAM_B2 * v + (1.0 - ADAM_B2) * (g * g)
    m_hat = nm / (1.0 - ADAM_B1 ** ADAM_STEP)
    v_hat = nv / (1.0 - ADAM_B2 ** ADAM_STEP)
    return (-ADAM_LR) * (m_hat / (jnp.sqrt(v_hat) + ADAM_EPS) + ADAM_WD * w), nm, nv


SMALL_NAMES = ("conv_b", "b_a", "b_x", "lru_lambda", "attn_sinks", "g_post_mix", "g_pre_ffn", "g_post_ffn")
PACK_WIDTH = 1024


def _pack_rows(vals):
    assert len(SMALL_NAMES) == SUBLANES
    row = lax.broadcasted_iota(jnp.int32, (SUBLANES, PACK_WIDTH), 0)
    tile = jnp.zeros((SUBLANES, PACK_WIDTH), F32)
    for k, name in enumerate(SMALL_NAMES):
        a = vals[name].reshape(1, -1)
        tile = jnp.where(row == k, jnp.pad(a, ((0, 0), (0, PACK_WIDTH - a.shape[1]))), tile)
    return tile


def _adamw_small(weights, mom_m, mom_v, parts, loss_parts):
    n = len(SMALL_NAMES)
    views = [(1, weights[name].size) for name in SMALL_NAMES]

    def body(*refs):
        w_refs, m_refs, v_refs = refs[:n], refs[n:2 * n], refs[2 * n:3 * n]
        p_ref, l_ref, loss_ref = refs[3 * n], refs[3 * n + 1], refs[3 * n + 2]
        outs = refs[3 * n + 3:]
        for k, (_, c) in enumerate(views):
            g = p_ref[0, k:k + 1, 0:c]
            for s in range(1, N_DEV):
                g = g + p_ref[s, k:k + 1, 0:c]
            g_ref, d_ref, nm_ref, nv_ref = outs[4 * k:4 * k + 4]
            g_ref[...] = g
            d_ref[...], nm_ref[...], nv_ref[...] = _adamw_math(w_refs[k][...], m_refs[k][...], v_refs[k][...], g)
        total = l_ref[0]
        for s in range(1, N_DEV):
            total = total + l_ref[s]
        loss_ref[...] = total

    args = [src[name].reshape(view) for src in (weights, mom_m, mom_v) for name, view in zip(SMALL_NAMES, views)]
    res = pl.pallas_call(
        body, name="adamw_small",
        out_shape=[jax.ShapeDtypeStruct(loss_parts.shape[1:], F32)]
                  + [jax.ShapeDtypeStruct(view, F32) for view in views for _ in range(4)],
        compiler_params=pltpu.CompilerParams(vmem_limit_bytes=VMEM_LIMIT),
    )(*args, parts, loss_parts)
    out = {name: tuple(t.reshape(weights[name].shape) for t in res[1 + 4 * k:5 + 4 * k]) for k, name in enumerate(SMALL_NAMES)}
    return res[0], out


def _adamw(w, m, v, parts, name):
    rows, cols = w.shape
    tr = next((t for t in (256, 128) if rows % t == 0), rows)
    parts = parts if isinstance(parts, (list, tuple)) else [parts]

    def body(w_ref, m_ref, v_ref, *refs):
        p_refs, (g_ref, d_ref, nm_ref, nv_ref) = refs[:len(parts)], refs[len(parts):]

        def total(p_ref):
            g = p_ref[0].astype(F32)
            for s in range(1, N_DEV):
                g = g + p_ref[s].astype(F32)
            return g

        g = jnp.concatenate([total(p_ref) for p_ref in p_refs], axis=1) if len(parts) > 1 else total(p_refs[0])
        g_ref[...] = g
        d_ref[...], nm_ref[...], nv_ref[...] = _adamw_math(w_ref[...], m_ref[...], v_ref[...], g)

    blk = pl.BlockSpec((tr, cols), lambda i: (i, 0))
    return pl.pallas_call(
        body, name=name, grid=(rows // tr,),
        in_specs=[blk, blk, blk] + [pl.BlockSpec((N_DEV, tr, p.shape[2]), lambda i: (0, i, 0)) for p in parts],
        out_specs=[blk] * 4,
        out_shape=[jax.ShapeDtypeStruct((rows, cols), F32)] * 4,
        compiler_params=_params(("parallel",)),
    )(w, m, v, *parts)


def _cols_from_shards(g):
    return jnp.transpose(g, (1, 0, 2)).reshape(g.shape[1], N_DEV * g.shape[2])


def _cols_to_shards(a):
    r, c = a.shape
    return jnp.transpose(a.reshape(r, N_DEV, c // N_DEV), (1, 0, 2))


def _block_diag(w):
    per = LRU_HALF // LRU_BLOCK
    w = w.reshape(2, per, LRU_BLOCK, LRU_BLOCK)
    eye = jnp.eye(per, dtype=w.dtype)
    return (w[:, :, :, None, :] * eye[None, :, None, :, None]).reshape(2, LRU_HALF, LRU_HALF)


def _block_diag_extract(t):
    per = LRU_HALF // LRU_BLOCK
    t = t.reshape(2, per, LRU_BLOCK, per, LRU_BLOCK)
    return jnp.stack([t[:, b, :, b, :] for b in range(per)], axis=1).reshape(LRU_BLOCKS, LRU_BLOCK, LRU_BLOCK)


def kernel(x, meta_tokens, g_pre_mix, w_in, conv_w, conv_b, w_a, b_a, w_x, b_x, lru_lambda, attn_sinks, w_out, g_post_mix, g_pre_ffn, w_ff1, w_ff2, g_post_ffn, loss_target, m_meta_tokens, m_g_pre_mix, m_w_in, m_conv_w, m_conv_b, m_w_a, m_b_a, m_w_x, m_b_x, m_lru_lambda, m_attn_sinks, m_w_out, m_g_post_mix, m_g_pre_ffn, m_w_ff1, m_w_ff2, m_g_post_ffn, v_meta_tokens, v_g_pre_mix, v_w_in, v_conv_w, v_conv_b, v_w_a, v_b_a, v_w_x, v_b_x, v_lru_lambda, v_attn_sinks, v_w_out, v_g_post_mix, v_g_pre_ffn, v_w_ff1, v_w_ff2, v_g_post_ffn):
    weights = dict(meta_tokens=meta_tokens, g_pre_mix=g_pre_mix, w_in=w_in, conv_w=conv_w, conv_b=conv_b, w_a=w_a,
                   b_a=b_a, w_x=w_x, b_x=b_x, lru_lambda=lru_lambda, attn_sinks=attn_sinks, w_out=w_out,
                   g_post_mix=g_post_mix, g_pre_ffn=g_pre_ffn, w_ff1=w_ff1, w_ff2=w_ff2, g_post_ffn=g_post_ffn)
    mom_m = dict(meta_tokens=m_meta_tokens, g_pre_mix=m_g_pre_mix, w_in=m_w_in, conv_w=m_conv_w, conv_b=m_conv_b,
                 w_a=m_w_a, b_a=m_b_a, w_x=m_w_x, b_x=m_b_x, lru_lambda=m_lru_lambda, attn_sinks=m_attn_sinks,
                 w_out=m_w_out, g_post_mix=m_g_post_mix, g_pre_ffn=m_g_pre_ffn, w_ff1=m_w_ff1, w_ff2=m_w_ff2,
                 g_post_ffn=m_g_post_ffn)
    mom_v = dict(meta_tokens=v_meta_tokens, g_pre_mix=v_g_pre_mix, w_in=v_w_in, conv_w=v_conv_w, conv_b=v_conv_b,
                 w_a=v_w_a, b_a=v_b_a, w_x=v_w_x, b_x=v_b_x, lru_lambda=v_lru_lambda, attn_sinks=v_attn_sinks,
                 w_out=v_w_out, g_post_mix=v_g_post_mix, g_pre_ffn=v_g_pre_ffn, w_ff1=v_w_ff1, w_ff2=v_w_ff2,
                 g_post_ffn=v_g_post_ffn)
    order = list(weights)

    (g_win, g_meta, g_cw) = _gather_two_level([w_in[0].astype(BF16), meta_tokens, conv_w[0]], "gather_first")
    w_in_full = _cols_from_shards(g_win)
    meta_full = _cols_from_shards(g_meta)
    conv_w_full = _cols_from_shards(g_cw)

    head = jnp.concatenate([jnp.zeros((PAD_ROWS, D_MODEL), F32), meta_full], axis=0)
    wa_bd = _block_diag(w_a[0]).astype(BF16)
    wx_bd = _block_diag(w_x[0]).astype(BF16)
    bias = _attn_bias()

    w1_shard = w_ff1[0].astype(BF16)
    (qkv, zrec, u1), (g_wout,) = _in_proj_fwd(head, x[0], g_pre_mix, w_in_full, [w_out[0].astype(BF16)], ["gather"])
    (attn,), (w1a,) = _attn_fwd(qkv, attn_sinks, bias, [w1_shard[:, :FF_HALF]], ["gather"])
    (rec, h_lru, kept), (w1b,) = _rec_fwd(zrec, conv_w_full, conv_b, wa_bd, b_a, wx_bd, b_x, lru_lambda,
                                         [w1_shard[:, FF_HALF:]], ["gather"])
    w_out_full = g_wout.reshape(D_MODEL, D_MODEL)
    w2_shard = w_ff2[0].astype(BF16)
    (mix, h1), (w2a,) = _out_proj_fwd(attn, rec, w_out_full, head, x[0], g_post_mix, [w2_shard[:FF_HALF]], ["gather"])
    (act, u2), (w2b,) = _ffn_up(h1, g_pre_ffn, (w1a, w1b), [w2_shard[FF_HALF:]], ["gather"])
    w2_halves = [w.reshape(D_FF // 2, D_MODEL) for w in (w2a, w2b)]
    (dy, df, dg_post_ffn, loss_acc), w2t_halves = _ffn_down_loss(
        act, w2_halves, h1, loss_target[0], g_post_ffn, [w2_shard[:FF_HALF].T, w2_shard[FF_HALF:].T], ["gather"] * 2)

    (da1,), (w1ta,) = _ffn_bwd_act(df, w2t_halves, act, [w1_shard[:, :FF_HALF].T], ["gather"])
    (dw1h, dw2g), (w1tb,) = _ffn_bwd_weights(u2, da1, act, df, [w1_shard[:, FF_HALF:].T], ["gather"])
    w1t_halves = [w.reshape(D_FF // 2, D_MODEL) for w in (w1ta, w1tb)]
    (dh1, dg_pre_ffn), (p_w1a,) = _ffn_bwd_x(da1, w1t_halves, h1, dy, g_pre_ffn, [dw1h[0]], ["scatter"])
    (dattn, drec, dw_out, dg_post_mix), (p_w1b,) = _out_proj_bwd(dh1, mix, g_post_mix, w_out_full.T, attn, rec,
                                                                [dw1h[1]], ["scatter"])
    (dq, dkv_late, dsinks), (p_w2,) = _attn_bwd(qkv, dattn, attn_sinks, bias, [dw2g], ["scatter"])
    dkv = dkv_late[BLOCK:BLOCK + qkv.shape[0]]
    (drz, rec_small, dwa_bd, dwx_bd), (p_wout,) = _rec_bwd(
        drec, zrec, h_lru, kept, conv_w_full, wa_bd, wx_bd, lru_lambda,
        [dw_out.reshape(N_DEV, D_MODEL // N_DEV, D_MODEL)], ["scatter"])
    small_grads = dict(
        conv_b=rec_small[ROW_CONV_B], b_a=rec_small[ROW_B_A], b_x=rec_small[ROW_B_X], lru_lambda=rec_small[ROW_LAMBDA],
        attn_sinks=dsinks[:, 0], g_post_mix=dg_post_mix, g_pre_ffn=dg_pre_ffn, g_post_ffn=dg_post_ffn)
    gate_rows = (LRU_BLOCKS * LRU_BLOCK, LRU_BLOCK)
    gate_dense = (LRU_BLOCKS * LRU_BLOCK * LRU_BLOCK // PACK_WIDTH, PACK_WIDTH)
    (dw_in,), (p_cw, p_small, p_wa, p_wx) = _in_proj_bwd_w(
        u1, dq, dkv, drz,
        [_cols_to_shards(rec_small[0:CONV_WIDTH]), _pack_rows(small_grads),
         _block_diag_extract(dwa_bd).reshape(gate_dense), _block_diag_extract(dwx_bd).reshape(gate_dense)],
        ["scatter", "gather", "gather", "gather"])
    p_wa, p_wx = (p.reshape((N_DEV,) + gate_rows) for p in (p_wa, p_wx))
    (dh0, dg_pre_mix), (p_win,) = _in_proj_bwd_x(
        head, x[0], g_pre_mix, dh1, dq, dkv, drz, w_in_full.T, [_cols_to_shards(dw_in)], ["scatter"])
    p_meta, p_gpm, p_loss = _exchange([_cols_to_shards(dh0[PAD_ROWS:BLOCK]), dg_pre_mix, loss_acc],
                                      ["scatter", "gather", "gather"], "exchange_last")

    res = {}
    res["g_pre_mix"] = _adamw(g_pre_mix, m_g_pre_mix, v_g_pre_mix, p_gpm, "adamw_g_pre_mix")
    res["w_in"] = _adamw(w_in[0], m_w_in[0], v_w_in[0], p_win, "adamw_w_in")
    res["w_out"] = _adamw(w_out[0], m_w_out[0], v_w_out[0], p_wout, "adamw_w_out")
    res["w_ff1"] = _adamw(w_ff1[0], m_w_ff1[0], v_w_ff1[0], [p_w1a, p_w1b], "adamw_w_ff1")
    res["w_ff2"] = _adamw(w_ff2[0], m_w_ff2[0], v_w_ff2[0], p_w2, "adamw_w_ff2")
    res["meta_tokens"] = _adamw(meta_tokens, m_meta_tokens, v_meta_tokens, p_meta, "adamw_meta")
    res["conv_w"] = _adamw(conv_w[0], m_conv_w[0], v_conv_w[0], p_cw, "adamw_conv_w")
    for name in ("w_in", "w_out", "w_ff1", "w_ff2", "conv_w"):
        res[name] = tuple(t[None] for t in res[name])
    for name, parts in (("w_a", p_wa), ("w_x", p_wx)):
        gate = _adamw(*(src[name].reshape(gate_rows) for src in (weights, mom_m, mom_v)), parts, "adamw_" + name)
        res[name] = tuple(t.reshape(weights[name].shape) for t in gate)
    loss_total, small = _adamw_small(weights, mom_m, mom_v, p_small, p_loss)
    res.update(small)

    grad_x = dh0[BLOCK:][None]
    outs = [loss_total[0, 0], grad_x]
    for k in range(4):
        outs += [res[name][k] for name in order]
    return tuple(outs)
```

```python
import jax
import jax.numpy as jnp
import numpy as np
from jax import lax
from jax.experimental import pallas as pl
from jax.experimental.pallas import tpu as pltpu

F32 = jnp.float32
BF16 = jnp.bfloat16

D_MODEL = 1024
N_META = 16
HEAD_DIM = 64
ATTN_HEADS = 8
KV_HEADS = 2
GQA_GROUP = ATTN_HEADS // KV_HEADS
ATTN_WIDTH = ATTN_HEADS * HEAD_DIM
KV_WIDTH = KV_HEADS * HEAD_DIM
QKV_WIDTH = ATTN_WIDTH + 2 * KV_WIDTH
LRU_WIDTH = 512
LRU_BLOCKS = 8
LRU_BLOCK = 64
LRU_HALF = 256
LRU_C = 8.0
CONV_WIDTH = 4
BLOCK = 128
PAD_ROWS = BLOCK - N_META
IN_WIDTH = QKV_WIDTH + 2 * LRU_WIDTH
D_FF = 4096
EPS = 1e-6
NEG = -1e30
N_DEV = 8
FF_CHUNK = D_FF // N_DEV
SUBLANES = 8
LANES = 128

ADAM_LR = 0.001
ADAM_B1 = 0.9
ADAM_B2 = 0.999
ADAM_EPS = 1e-08
ADAM_WD = 0.01
ADAM_STEP = 10

VMEM_LIMIT = 56 * 1024 * 1024


def _row_tile(rows):
    for t in (640, 512, 256, 128):
        if rows % t == 0:
            return t
    raise ValueError(rows)


def _big_tile(rows):
    for t in (1664, 1024, 512, 256, 128):
        if rows % t == 0:
            return t
    raise ValueError(rows)


def _rec_tile(rows):
    for t in (640, 256, 128):
        if rows % t == 0:
            return t
    raise ValueError(rows)


def _params(semantics):
    return pltpu.CompilerParams(dimension_semantics=semantics, vmem_limit_bytes=VMEM_LIMIT)


def _mm(a, b):
    return lax.dot_general(a, b, (((1,), (0,)), ((), ())), preferred_element_type=F32)


def _mm_nt(a, b):
    return lax.dot_general(a, b, (((1,), (1,)), ((), ())), preferred_element_type=F32)


def _mm_tn(a, b):
    return lax.dot_general(a, b, (((0,), (0,)), ((), ())), preferred_element_type=F32)


def _rms_fwd(x, g):
    rstd = lax.rsqrt(jnp.mean(x * x, axis=-1, keepdims=True) + EPS)
    xhat = x * rstd
    return xhat * g, xhat, rstd


def _rms_bwd(dy, xhat, rstd, g):
    dyg = dy * g
    c = jnp.mean(dyg * xhat, axis=-1, keepdims=True)
    dx = rstd * (dyg - xhat * c)
    dg = jnp.sum(dy * xhat, axis=0, keepdims=True)
    return dx, dg


def _sigmoid(x):
    return pl.reciprocal(1.0 + jnp.exp(-x), approx=True)


def _log1p(x):
    u = 1.0 + x
    return jnp.where(u == 1.0, x, jnp.log(u) * x / (u - 1.0))


def _one_minus_sq_exp(x, ex):
    return -jnp.tanh(x) * (1.0 + ex * ex)


TINY = 1e-30


def _sqrt_pos(y):
    r = lax.rsqrt(jnp.maximum(y, TINY))
    return y * r, r


def _softplus(x):
    return jnp.maximum(x, 0.0) + _log1p(jnp.exp(-jnp.abs(x)))


GELU_C = 0.7978845608028654
GELU_K = 0.044715


def _gelu(x):
    t = jnp.tanh(GELU_C * (x + GELU_K * x * x * x))
    return 0.5 * x * (1.0 + t), t


def _gelu_grad(x, t):
    return 0.5 * (1.0 + t) + 0.5 * x * (1.0 - t * t) * GELU_C * (1.0 + 3.0 * GELU_K * x * x)


def _full(shape):
    return pl.BlockSpec(shape, lambda *_: (0,) * len(shape))


def _resident(shape):
    return pl.BlockSpec(shape, lambda *_: (0,) * len(shape), pipeline_mode=pl.Buffered(1))


def _exchange_copies(ins, outs, sems, modes):
    send_sems, recv_sems, local_sems = sems
    x, y, c = lax.axis_index("x"), lax.axis_index("y"), lax.axis_index("c")
    me = 4 * x + 2 * y + c

    def block(a, dev):
        return ins[a] if modes[a] == "gather" else ins[a].at[dev]

    local = [pltpu.make_async_copy(block(a, me), outs[a].at[me], local_sems.at[a]) for a in range(len(ins))]
    sends, recvs = [], []
    for a in range(len(ins)):
        for k in range(N_DEV - 1):
            bits = k + 1
            px = jnp.bitwise_xor(x, (bits >> 2) & 1)
            py = jnp.bitwise_xor(y, (bits >> 1) & 1)
            pc = jnp.bitwise_xor(c, bits & 1)
            peer = 4 * px + 2 * py + pc
            common = dict(src_ref=block(a, peer), send_sem=send_sems.at[a, k], recv_sem=recv_sems.at[a, k],
                          device_id=(px, py, pc), device_id_type=pl.DeviceIdType.MESH)
            sends.append(pltpu.make_async_remote_copy(dst_ref=outs[a].at[me], **common))
            recvs.append(pltpu.make_async_remote_copy(dst_ref=outs[a].at[peer], **common))
    return local, sends, recvs


def _exchange_start(ins, outs, sems, modes):
    local, sends, _ = _exchange_copies(ins, outs, sems, modes)
    for cp in local + sends:
        cp.start()


def _exchange_wait(ins, outs, sems, modes):
    local, sends, recvs = _exchange_copies(ins, outs, sems, modes)
    for cp in recvs:
        cp.wait_recv()
    for cp in sends:
        cp.wait_send()
    for cp in local:
        cp.wait()


def _exchange_shapes(arrays, modes):
    return [jax.ShapeDtypeStruct((N_DEV,) + a.shape if mode == "gather" else a.shape, a.dtype)
            for a, mode in zip(arrays, modes)]


def _exchange_sems(na):
    return [pltpu.SemaphoreType.DMA((na, N_DEV - 1)), pltpu.SemaphoreType.DMA((na, N_DEV - 1)),
            pltpu.SemaphoreType.DMA((na,))]


ANY_SPACE = pl.BlockSpec(memory_space=pl.ANY)


def _exchange(arrays, modes, name):
    na = len(arrays)

    def body(*refs):
        ins, outs, sems = refs[:na], refs[na:2 * na], refs[2 * na:]
        _exchange_start(ins, outs, sems, modes)
        _exchange_wait(ins, outs, sems, modes)

    return pl.pallas_call(
        body, name=name, out_shape=_exchange_shapes(arrays, modes),
        in_specs=[ANY_SPACE] * na, out_specs=[ANY_SPACE] * na, scratch_shapes=_exchange_sems(na),
        compiler_params=pltpu.CompilerParams(has_side_effects=True),
    )(*arrays)


def _gather_two_level(arrays, name):
    na = len(arrays)

    def body(*refs):
        ins, outs = refs[:na], refs[na:2 * na]
        send_sems, recv_sems, local_sems = refs[2 * na:]
        x, y, c = lax.axis_index("x"), lax.axis_index("y"), lax.axis_index("c")
        me, sibling = (x, y, c), (x, y, 1 - c)
        chips = [(1 - x, y), (x, 1 - y), (1 - x, 1 - y)]

        def copy(a, k, block, to, src=None):
            slot = outs[a].at[4 * block[0] + 2 * block[1] + block[2]]
            return pltpu.make_async_remote_copy(
                src_ref=slot if src is None else src, dst_ref=slot, send_sem=send_sems.at[a, k],
                recv_sem=recv_sems.at[a, k], device_id=to, device_id_type=pl.DeviceIdType.MESH)

        local = [pltpu.make_async_copy(ins[a], outs[a].at[4 * x + 2 * y + c], local_sems.at[a]) for a in range(na)]
        first = []
        for a in range(na):
            first.append(copy(a, 0, me, sibling, src=ins[a]))
            first += [copy(a, 1 + j, me, (*chip, c), src=ins[a]) for j, chip in enumerate(chips)]
        for cp in local + first:
            cp.start()
        passed = []
        for j, chip in enumerate(chips):
            for a in range(na):
                copy(a, 1 + j, (*chip, c), me).wait_recv()
                passed.append(copy(a, 4 + j, (*chip, c), sibling))
                passed[-1].start()
        for a in range(na):
            copy(a, 0, sibling, me).wait_recv()
            for j, chip in enumerate(chips):
                copy(a, 4 + j, (*chip, 1 - c), me).wait_recv()
        for cp in first + passed:
            cp.wait_send()
        for cp in local:
            cp.wait()

    return pl.pallas_call(
        body, name=name, out_shape=_exchange_shapes(arrays, ["gather"] * na),
        in_specs=[ANY_SPACE] * na, out_specs=[ANY_SPACE] * na, scratch_shapes=_exchange_sems(na),
        compiler_params=pltpu.CompilerParams(has_side_effects=True),
    )(*arrays)


def _hosting_call(body, name, steps, in_specs, out_specs, out_shape, scratch_shapes, args, arrays, modes):
    n_in, n_out, n_scr, na = len(in_specs), len(out_specs), len(scratch_shapes), len(arrays)
    grid = steps if isinstance(steps, tuple) else (steps,)

    def hosting_body(*refs):
        cuts = [0]
        for n in (n_in, na, n_out, na, n_scr, 3):
            cuts.append(cuts[-1] + n)
        ins, x_ins, outs, x_outs, scr, sems = (refs[cuts[p]:cuts[p + 1]] for p in range(6))
        first, last = True, True
        for axis, n in enumerate(grid):
            first = first & (pl.program_id(axis) == 0)
            last = last & (pl.program_id(axis) == n - 1)

        @pl.when(first)
        def _():
            _exchange_start(x_ins, x_outs, sems, modes)

        body(*ins, *outs, *scr)

        @pl.when(last)
        def _():
            _exchange_wait(x_ins, x_outs, sems, modes)

    res = pl.pallas_call(
        hosting_body, name=name, grid=grid,
        in_specs=list(in_specs) + [ANY_SPACE] * na, out_specs=list(out_specs) + [ANY_SPACE] * na,
        out_shape=list(out_shape) + _exchange_shapes(arrays, modes),
        scratch_shapes=list(scratch_shapes) + _exchange_sems(na),
        compiler_params=_params(("arbitrary",) * len(grid)),
    )(*args, *arrays)
    return res[:n_out], res[n_out:]


def _frame_rows(src_hbm, buf, sem, i, steps, tm):
    def first():
        return pltpu.make_async_copy(src_hbm.at[pl.ds(0, tm - BLOCK)], buf.at[0, pl.ds(BLOCK, tm - BLOCK)], sem.at[0])

    def later(t, slot):
        return pltpu.make_async_copy(src_hbm.at[pl.ds(pl.multiple_of(t * tm - BLOCK, SUBLANES), tm)], buf.at[slot], sem.at[slot])

    slot = i % 2

    @pl.when(i == 0)
    def _():
        first().start()

    @pl.when(i + 1 < steps)
    def _():
        later(i + 1, 1 - slot).start()

    @pl.when(i == 0)
    def _():
        first().wait()

    @pl.when(i > 0)
    def _():
        later(i, slot).wait()

    return slot


def _frame_scratch(tm):
    return [pltpu.VMEM((2, tm, D_MODEL), F32), pltpu.SemaphoreType.DMA((2,))]


def _h0_tile(head_ref, x_hbm, buf, sem, i, steps, tm):
    slot = _frame_rows(x_hbm, buf, sem, i, steps, tm)

    @pl.when(i == 0)
    def _():
        buf[0, 0:BLOCK, :] = head_ref[...]

    return buf[slot]


def _in_proj_fwd(head, x, g1, w_in, carried, modes):
    rows = BLOCK + x.shape[0]
    tm = _row_tile(rows)
    steps = rows // tm

    def body(head_ref, g_ref, w_ref, x_hbm, qkv_ref, zrec_ref, u_ref, buf, sem):
        h = _h0_tile(head_ref, x_hbm, buf, sem, pl.program_id(0), steps, tm)
        u, _, _ = _rms_fwd(h, g_ref[...])
        u = u.astype(BF16)
        u_ref[...] = u
        z = _mm(u, w_ref[...])
        qkv_ref[...] = z[:, :QKV_WIDTH].astype(BF16)
        zrec_ref[...] = z[:, QKV_WIDTH:]

    wide = pl.BlockSpec((tm, D_MODEL), lambda i: (i, 0))
    return _hosting_call(
        body, "in_proj_fwd", steps,
        [_full((BLOCK, D_MODEL)), _full((1, D_MODEL)), _resident((D_MODEL, IN_WIDTH)), ANY_SPACE],
        [pl.BlockSpec((tm, QKV_WIDTH), lambda i: (i, 0)), pl.BlockSpec((tm, 2 * LRU_WIDTH), lambda i: (i, 0)), wide],
        [jax.ShapeDtypeStruct((rows, QKV_WIDTH), BF16), jax.ShapeDtypeStruct((rows, 2 * LRU_WIDTH), F32),
         jax.ShapeDtypeStruct((rows, D_MODEL), BF16)],
        _frame_scratch(tm), (head, g1, w_in, x), carried, modes)


N_BIAS = 3


def _attn_bias():
    key = np.arange(2 * BLOCK)[:, None]
    r = np.arange(GQA_GROUP * BLOCK)[None, :] % BLOCK
    band = (key > r) & (key <= r + BLOCK)
    out = [np.where(band & ((n - 1) * BLOCK + key >= PAD_ROWS), 0.0, NEG) for n in range(N_BIAS)]
    return jnp.asarray(np.stack(out), F32)


def _attn_probs(k2, q4, bias, sink_row):
    s = _mm_nt(k2, q4) * (HEAD_DIM ** -0.5) + bias
    m = jnp.maximum(jnp.max(s, axis=0, keepdims=True), sink_row)
    p = jnp.exp(s - m)
    es = jnp.exp(sink_row - m)
    inv = 1.0 / (jnp.sum(p, axis=0, keepdims=True) + es)
    return p * inv, es * inv


def _heads(ref, rows, first, count):
    return jnp.concatenate([ref[rows, (first + g) * HEAD_DIM:(first + g + 1) * HEAD_DIM] for g in range(count)], axis=0)


def _keys_of_block(prev_ref, cur_ref, b, kv):
    sl = slice(kv * HEAD_DIM, (kv + 1) * HEAD_DIM)
    before = prev_ref[:, sl] if b == 0 else cur_ref[(b - 1) * BLOCK:b * BLOCK, sl]
    return jnp.concatenate([before, cur_ref[b * BLOCK:(b + 1) * BLOCK, sl]], axis=0)


def _bias_of_block(bias_ref, block):
    return bias_ref[jnp.minimum(block, N_BIAS - 1)]


def _sink_row(sink_ref, kv):
    g = lax.broadcasted_iota(jnp.int32, (1, GQA_GROUP * BLOCK), 1) // BLOCK
    row = jnp.full((1, GQA_GROUP * BLOCK), sink_ref[0, kv * GQA_GROUP], F32)
    for i in range(1, GQA_GROUP):
        row = jnp.where(g == i, sink_ref[0, kv * GQA_GROUP + i], row)
    return row


def _from_head_major(pieces):
    return jnp.concatenate(pieces, axis=0).T


def _attn_specs(tm, tile_of):
    nbt = tm // BLOCK
    k_col, v_col = ATTN_WIDTH // KV_WIDTH, ATTN_WIDTH // KV_WIDTH + 1
    before = lambda i: jnp.maximum(tile_of(i) * nbt - 1, 0)
    return [pl.BlockSpec((tm, ATTN_WIDTH), lambda i: (tile_of(i), 0)),
            pl.BlockSpec((BLOCK, KV_WIDTH), lambda i: (before(i), k_col)),
            pl.BlockSpec((tm, KV_WIDTH), lambda i: (tile_of(i), k_col)),
            pl.BlockSpec((BLOCK, KV_WIDTH), lambda i: (before(i), v_col)),
            pl.BlockSpec((tm, KV_WIDTH), lambda i: (tile_of(i), v_col))]


def _attn_fwd(qkv, sinks, bias, carried, modes):
    rows = qkv.shape[0]
    tm = _big_tile(rows)
    nbt = tm // BLOCK

    def body(sink_ref, bias_ref, q_ref, kp_ref, kc_ref, vp_ref, vc_ref, o_ref):
        i = pl.program_id(0)
        for b in range(nbt):
            blk = slice(b * BLOCK, (b + 1) * BLOCK)
            bias_t = _bias_of_block(bias_ref, i * nbt + b)
            pieces = []
            for kv in range(KV_HEADS):
                k2 = _keys_of_block(kp_ref, kc_ref, b, kv)
                v2 = _keys_of_block(vp_ref, vc_ref, b, kv)
                q4 = _heads(q_ref, blk, kv * GQA_GROUP, GQA_GROUP)
                pn, _ = _attn_probs(k2, q4, bias_t, _sink_row(sink_ref, kv))
                ot = _mm_tn(v2, pn.astype(BF16))
                pieces += [ot[:, g * BLOCK:(g + 1) * BLOCK] for g in range(GQA_GROUP)]
            o_ref[blk, :] = _from_head_major(pieces).astype(BF16)

    return _hosting_call(
        body, "attn_fwd", rows // tm,
        [pl.BlockSpec(memory_space=pltpu.SMEM), _resident((N_BIAS, 2 * BLOCK, GQA_GROUP * BLOCK))]
        + _attn_specs(tm, lambda i: i),
        [pl.BlockSpec((tm, ATTN_WIDTH), lambda i: (i, 0))],
        [jax.ShapeDtypeStruct((rows, ATTN_WIDTH), BF16)],
        [], (sinks, bias, qkv, qkv, qkv, qkv, qkv), carried, modes)


def _conv_taps(xbuf, tm):
    return [xbuf[pl.ds(SUBLANES - (CONV_WIDTH - 1 - j), tm), :] for j in range(CONV_WIDTH)]


def _lru_halves(xc):
    return [xc[:, h * LRU_HALF:(h + 1) * LRU_HALF].astype(BF16) for h in range(2)]


def _lru_gates(xc, wa_ref, ba_ref, wx_ref, bx_ref, lam_ref):
    halves = _lru_halves(xc)
    gate_r = jnp.concatenate([_mm(halves[h], wa_ref[h]) for h in range(2)], axis=1) + ba_ref[...]
    gate_i = jnp.concatenate([_mm(halves[h], wx_ref[h]) for h in range(2)], axis=1) + bx_ref[...]
    r = _sigmoid(gate_r)
    ig = _sigmoid(gate_i)
    log_a = (-LRU_C) * r * _softplus(-lam_ref[...])
    a = jnp.exp(log_a)
    mult, _ = _sqrt_pos(_one_minus_sq_exp(log_a, a))
    return r, ig, a, mult


KEPT_XC, KEPT_A, KEPT_MULT, KEPT_R, KEPT_I, N_KEPT = 0, 1, 2, 3, 4, 5


def _scan_tile(a_ref, u_ref, out_ref, carry, tm):
    row = lax.broadcasted_iota(jnp.int32, (SUBLANES, LRU_WIDTH), 0)

    def step(j, before):
        o = pl.multiple_of(j * SUBLANES, SUBLANES)
        a = a_ref[pl.ds(o, SUBLANES), :]
        u = u_ref[pl.ds(o, SUBLANES), :]
        for s in (1, 2, 4):
            keep = row >= s
            u = jnp.where(keep, a * pltpu.roll(u, s, 0) + u, u)
            a = jnp.where(keep, a * pltpu.roll(a, s, 0), a)
        out = a * before + u
        out_ref[pl.ds(o, SUBLANES), :] = out
        return out[SUBLANES - 1:SUBLANES, :]

    return lax.fori_loop(0, tm // SUBLANES, step, carry)


def _rec_fwd(zrec, conv_w, conv_b, wa_bd, b_a, wx_bd, b_x, lam, carried, modes):
    rows = zrec.shape[0]
    tm = _row_tile(rows)

    def body(xr_ref, yr_ref, cw_ref, cb_ref, wa_ref, ba_ref, wx_ref, bx_ref, lam_ref, rec_ref, h_ref, kept_ref,
             xbuf, a_s, u_s, carry):
        i = pl.program_id(0)

        @pl.when(i == 0)
        def _():
            xbuf[0:SUBLANES, :] = jnp.zeros((SUBLANES, LRU_WIDTH), F32)
            carry[...] = jnp.zeros_like(carry)

        @pl.when(i > 0)
        def _():
            xbuf[0:SUBLANES, :] = xbuf[tm:tm + SUBLANES, :]

        xbuf[SUBLANES:SUBLANES + tm, :] = xr_ref[...]
        taps = _conv_taps(xbuf, tm)
        xc = cb_ref[...] + sum(cw_ref[j:j + 1, :] * taps[j] for j in range(CONV_WIDTH))
        r, ig, a, mult = _lru_gates(xc, wa_ref, ba_ref, wx_ref, bx_ref, lam_ref)
        for k, val in ((KEPT_XC, xc), (KEPT_A, a), (KEPT_MULT, mult), (KEPT_R, r), (KEPT_I, ig)):
            kept_ref[:, k * LRU_WIDTH:(k + 1) * LRU_WIDTH] = val
        grow = i * tm + lax.broadcasted_iota(jnp.int32, (tm, LRU_WIDTH), 0)
        a_s[...] = a
        u_s[...] = jnp.where(grow >= PAD_ROWS, mult * (ig * xc), 0.0)
        carry[0:1, :] = _scan_tile(a_s, u_s, h_ref, carry[0:1, :], tm)
        gel, _ = _gelu(yr_ref[...])
        rec_ref[...] = (gel * h_ref[...]).astype(BF16)

    vec = _full((1, LRU_WIDTH))
    bd = _full((2, LRU_HALF, LRU_HALF))
    return _hosting_call(
        body, "rec_fwd", rows // tm,
        [pl.BlockSpec((tm, LRU_WIDTH), lambda i: (i, 0)), pl.BlockSpec((tm, LRU_WIDTH), lambda i: (i, 1)),
         _full((CONV_WIDTH, LRU_WIDTH)), vec, bd, vec, bd, vec, vec],
        [pl.BlockSpec((tm, LRU_WIDTH), lambda i: (i, 0))] * 2 + [pl.BlockSpec((tm, N_KEPT * LRU_WIDTH), lambda i: (i, 0))],
        [jax.ShapeDtypeStruct((rows, LRU_WIDTH), BF16), jax.ShapeDtypeStruct((rows, LRU_WIDTH), F32),
         jax.ShapeDtypeStruct((rows, N_KEPT * LRU_WIDTH), F32)],
        [pltpu.VMEM((tm + SUBLANES, LRU_WIDTH), F32), pltpu.VMEM((tm, LRU_WIDTH), F32),
         pltpu.VMEM((tm, LRU_WIDTH), F32), pltpu.VMEM((SUBLANES, LRU_WIDTH), F32)],
        (zrec, zrec, conv_w, conv_b, wa_bd, b_a, wx_bd, b_x, lam), carried, modes)


def _out_proj_fwd(attn, rec, w_out, head, x, g2, carried, modes):
    rows = attn.shape[0]
    tm = _row_tile(rows)
    steps = rows // tm

    def body(attn_ref, rec_ref, w_ref, head_ref, g_ref, x_hbm, mix_ref, h1_ref, buf, sem):
        h0 = _h0_tile(head_ref, x_hbm, buf, sem, pl.program_id(0), steps, tm)
        mix = _mm(attn_ref[...], w_ref[0:ATTN_WIDTH, :]) + _mm(rec_ref[...], w_ref[ATTN_WIDTH:, :])
        y, _, _ = _rms_fwd(mix, g_ref[...])
        mix_ref[...] = mix
        h1_ref[...] = h0 + y

    half = pl.BlockSpec((tm, ATTN_WIDTH), lambda i: (i, 0))
    wide = pl.BlockSpec((tm, D_MODEL), lambda i: (i, 0))
    return _hosting_call(
        body, "out_proj_fwd", steps,
        [half, half, _resident((D_MODEL, D_MODEL)), _full((BLOCK, D_MODEL)), _full((1, D_MODEL)), ANY_SPACE],
        [wide, wide],
        [jax.ShapeDtypeStruct((rows, D_MODEL), F32)] * 2,
        _frame_scratch(tm), (attn, rec, w_out, head, g2, x), carried, modes)


FF_COLS = 1024
FF_HALF = FF_CHUNK // 2


def _hidden_at(d, half):
    return half * (D_FF // 2) + d * FF_HALF


def _ffn_up(h1, g3, w1_halves, carried, modes):
    rows = h1.shape[0]
    tm = _row_tile(rows)

    def body(h_ref, g_ref, wa_ref, wb_ref, act_ref, u_ref):
        u, _, _ = _rms_fwd(h_ref[...], g_ref[...])
        u = u.astype(BF16)
        u_ref[...] = u
        for half, w_ref in enumerate((wa_ref, wb_ref)):
            for d in range(N_DEV):
                c = _hidden_at(d, half)
                a1 = jnp.maximum(_mm(u, w_ref[d]), 0.0)
                act_ref[:, c:c + FF_HALF] = (a1 * a1).astype(BF16)

    wide = pl.BlockSpec((tm, D_MODEL), lambda i: (i, 0))
    return _hosting_call(
        body, "ffn_up", rows // tm,
        [wide, _full((1, D_MODEL))] + [_resident((N_DEV, D_MODEL, FF_HALF))] * 2,
        [pl.BlockSpec((tm, D_FF), lambda i: (i, 0)), wide],
        [jax.ShapeDtypeStruct((rows, D_FF), BF16), jax.ShapeDtypeStruct((rows, D_MODEL), BF16)],
        [], (h1, g3, *w1_halves), carried, modes)


def _ffn_down_loss(act, w2_halves, h1, target, g4, carried, modes):
    rows = h1.shape[0]
    tm = _row_tile(rows)
    steps = rows // tm
    kh = D_FF // 2

    def body(act_ref, wa_ref, wb_ref, h_ref, g_ref, t_hbm, dy_ref, df_ref, dg_ref, loss_ref, buf, sem):
        i = pl.program_id(0)
        slot = _frame_rows(t_hbm, buf, sem, i, steps, tm)

        @pl.when(i == 0)
        def _():
            dg_ref[...] = jnp.zeros_like(dg_ref)
            loss_ref[...] = jnp.zeros_like(loss_ref)
            buf[0, 0:BLOCK, :] = jnp.zeros((BLOCK, D_MODEL), F32)

        g = g_ref[...]
        f = _mm(act_ref[:, :kh], wa_ref[...]) + _mm(act_ref[:, kh:], wb_ref[...])
        y, fhat, rstd = _rms_fwd(f, g)
        grow = i * tm + lax.broadcasted_iota(jnp.int32, (tm, D_MODEL), 0)
        err = jnp.where(grow >= BLOCK, h_ref[...] + y - buf[slot], 0.0)
        loss_ref[...] += (0.5 / D_MODEL) * jnp.sum(err * err)
        dy = err * (1.0 / D_MODEL)
        df, dg = _rms_bwd(dy, fhat, rstd, g)
        dy_ref[...] = dy
        df_ref[...] = df.astype(BF16)
        dg_ref[...] += dg

    wide = pl.BlockSpec((tm, D_MODEL), lambda i: (i, 0))
    return _hosting_call(
        body, "ffn_down_loss", steps,
        [pl.BlockSpec((tm, D_FF), lambda i: (i, 0)), _resident((kh, D_MODEL)), _resident((kh, D_MODEL)), wide,
         _full((1, D_MODEL)), ANY_SPACE],
        [wide, wide, _full((1, D_MODEL)), _full((SUBLANES, LANES))],
        [jax.ShapeDtypeStruct((rows, D_MODEL), F32), jax.ShapeDtypeStruct((rows, D_MODEL), BF16),
         jax.ShapeDtypeStruct((1, D_MODEL), F32), jax.ShapeDtypeStruct((SUBLANES, LANES), F32)],
        _frame_scratch(tm), (act, *w2_halves, h1, g4, target), carried, modes)


def _ffn_bwd_act(df, w2t_halves, act, carried, modes):
    rows = df.shape[0]
    tm = _row_tile(rows)

    def body(df_ref, wa_ref, wb_ref, act_ref, da_ref):
        df_t = df_ref[...]
        for half, w_ref in enumerate((wa_ref, wb_ref)):
            for d in range(N_DEV):
                cols = slice(_hidden_at(d, half), _hidden_at(d, half) + FF_HALF)
                dact = _mm(df_t, w_ref[d])
                relu_a1, _ = _sqrt_pos(act_ref[:, cols].astype(F32))
                da_ref[:, cols] = (dact * (2.0 * relu_a1)).astype(BF16)

    hidden = pl.BlockSpec((tm, D_FF), lambda i: (i, 0))
    return _hosting_call(
        body, "ffn_bwd_act", rows // tm,
        [pl.BlockSpec((tm, D_MODEL), lambda i: (i, 0))] + [_resident((N_DEV, D_MODEL, FF_HALF))] * 2 + [hidden],
        [hidden],
        [jax.ShapeDtypeStruct((rows, D_FF), BF16)],
        [], (df, *w2t_halves, act), carried, modes)


def _ffn_bwd_x(da, w1t_halves, h1, dy, g3, carried, modes):
    rows = h1.shape[0]
    tm = _row_tile(rows)
    kh = D_FF // 2

    def body(da_ref, wa_ref, wb_ref, h_ref, dy_ref, g_ref, dh_ref, dg_ref):
        @pl.when(pl.program_id(0) == 0)
        def _():
            dg_ref[...] = jnp.zeros_like(dg_ref)

        g = g_ref[...]
        _, xhat, rstd = _rms_fwd(h_ref[...], g)
        du = _mm(da_ref[:, :kh], wa_ref[...]) + _mm(da_ref[:, kh:], wb_ref[...])
        dx, dg = _rms_bwd(du, xhat, rstd, g)
        dh_ref[...] = dy_ref[...] + dx
        dg_ref[...] += dg

    wide = pl.BlockSpec((tm, D_MODEL), lambda i: (i, 0))
    return _hosting_call(
        body, "ffn_bwd_x", rows // tm,
        [pl.BlockSpec((tm, D_FF), lambda i: (i, 0)), _resident((kh, D_MODEL)), _resident((kh, D_MODEL)), wide, wide,
         _full((1, D_MODEL))],
        [wide, _full((1, D_MODEL))],
        [jax.ShapeDtypeStruct((rows, D_MODEL), F32), jax.ShapeDtypeStruct((1, D_MODEL), F32)],
        [], (da, *w1t_halves, h1, dy, g3), carried, modes)


def _ffn_bwd_weights(u2, da, act, df, carried, modes):
    rows = u2.shape[0]
    tb = _big_tile(rows)
    steps = rows // tb
    per = FF_COLS // FF_HALF

    def body(u_ref, da_ref, act_ref, df_ref, dw1_ref, dw2_ref, acc1, acc2):
        i = pl.program_id(1)

        @pl.when(i == 0)
        def _():
            acc1[...] = jnp.zeros_like(acc1)
            acc2[...] = jnp.zeros_like(acc2)

        acc1[...] += _mm_tn(u_ref[...], da_ref[...])
        acc2[...] += _mm_tn(act_ref[...], df_ref[...])

        @pl.when(i == steps - 1)
        def _():
            for p in range(per):
                c = p * FF_HALF
                dw1_ref[p] = acc1[:, c:c + FF_HALF].astype(BF16)
                dw2_ref[p] = acc2[c:c + FF_HALF, :].astype(BF16)

    wide = pl.BlockSpec((tb, D_MODEL), lambda j, i: (i, 0))
    chunk = pl.BlockSpec((tb, FF_COLS), lambda j, i: (i, j))
    return _hosting_call(
        body, "ffn_bwd_weights", (D_FF // FF_COLS, steps),
        [wide, chunk, chunk, wide],
        [pl.BlockSpec((None, per, D_MODEL, FF_HALF), lambda j, i: (j // 2, j % 2, 0, 0)),
         pl.BlockSpec((None, per, FF_HALF, D_MODEL), lambda j, i: (j // 2, j % 2, 0, 0))],
        [jax.ShapeDtypeStruct((2, N_DEV, D_MODEL, FF_HALF), BF16), jax.ShapeDtypeStruct((2, N_DEV, FF_HALF, D_MODEL), BF16)],
        [pltpu.VMEM((D_MODEL, FF_COLS), F32), pltpu.VMEM((FF_COLS, D_MODEL), F32)],
        (u2, da, act, df), carried, modes)


def _out_proj_bwd(dh1, mix, g2, w_out_t, attn, rec, carried, modes):
    rows = dh1.shape[0]
    tm = _row_tile(rows)
    steps = rows // tm

    def body(dh_ref, mix_ref, g_ref, w_ref, attn_ref, rec_ref, dattn_ref, drec_ref, dw_ref, dg_ref, acc):
        i = pl.program_id(0)

        @pl.when(i == 0)
        def _():
            acc[...] = jnp.zeros_like(acc)
            dg_ref[...] = jnp.zeros_like(dg_ref)

        g = g_ref[...]
        _, xhat, rstd = _rms_fwd(mix_ref[...], g)
        dmix, dg = _rms_bwd(dh_ref[...], xhat, rstd, g)
        dmix = dmix.astype(BF16)
        dg_ref[...] += dg
        din = _mm(dmix, w_ref[...])
        dattn_ref[...] = din[:, :ATTN_WIDTH].astype(BF16)
        drec_ref[...] = din[:, ATTN_WIDTH:]
        acc[0:ATTN_WIDTH, :] += _mm_tn(attn_ref[...], dmix)
        acc[ATTN_WIDTH:, :] += _mm_tn(rec_ref[...], dmix)

        @pl.when(i == steps - 1)
        def _():
            dw_ref[...] = acc[...].astype(BF16)

    half = pl.BlockSpec((tm, ATTN_WIDTH), lambda i: (i, 0))
    wide = pl.BlockSpec((tm, D_MODEL), lambda i: (i, 0))
    return _hosting_call(
        body, "out_proj_bwd", steps,
        [wide, wide, _full((1, D_MODEL)), _resident((D_MODEL, D_MODEL)), half, half],
        [half, half, _full((D_MODEL, D_MODEL)), _full((1, D_MODEL))],
        [jax.ShapeDtypeStruct((rows, ATTN_WIDTH), BF16), jax.ShapeDtypeStruct((rows, LRU_WIDTH), F32),
         jax.ShapeDtypeStruct((D_MODEL, D_MODEL), BF16), jax.ShapeDtypeStruct((1, D_MODEL), F32)],
        [pltpu.VMEM((D_MODEL, D_MODEL), F32)],
        (dh1, mix, g2, w_out_t, attn, rec), carried, modes)


def _attn_bwd(qkv, dattn, sinks, bias, carried, modes):
    rows = qkv.shape[0]
    tm = _big_tile(rows)
    nbt, nt = tm // BLOCK, rows // tm

    def body(sink_ref, bias_ref, do_ref, q_ref, kp_ref, kc_ref, vp_ref, vc_ref, dq_ref, dkv_ref, dsink_ref, dk_c, dv_c):
        i = pl.program_id(0)

        @pl.when(i == 0)
        def _():
            dk_c[...] = jnp.zeros_like(dk_c)
            dv_c[...] = jnp.zeros_like(dv_c)
            dsink_ref[...] = jnp.zeros_like(dsink_ref)

        @pl.when(i < nt)
        def _():
            dk_late, dv_late = dk_c[...], dv_c[...]
            dsink_rows = [jnp.zeros((1, LANES), F32)] * ATTN_HEADS
            for b in range(nbt):
                blk = slice(b * BLOCK, (b + 1) * BLOCK)
                bias_t = _bias_of_block(bias_ref, i * nbt + b)
                dq_parts, dk_parts, dv_parts = [], [], []
                for kv in range(KV_HEADS):
                    k2 = _keys_of_block(kp_ref, kc_ref, b, kv)
                    v2 = _keys_of_block(vp_ref, vc_ref, b, kv)
                    q4 = _heads(q_ref, blk, kv * GQA_GROUP, GQA_GROUP)
                    do4 = _heads(do_ref, blk, kv * GQA_GROUP, GQA_GROUP)
                    pn, psink = _attn_probs(k2, q4, bias_t, _sink_row(sink_ref, kv))
                    dpn = _mm_nt(v2, do4)
                    delta = jnp.sum(pn * dpn, axis=0, keepdims=True)
                    ds = ((pn * (dpn - delta)) * (HEAD_DIM ** -0.5)).astype(BF16)
                    dqt = _mm_tn(k2, ds)
                    dq_parts += [dqt[:, g * BLOCK:(g + 1) * BLOCK] for g in range(GQA_GROUP)]
                    dk_parts.append(_mm(ds, q4))
                    dv_parts.append(_mm(pn.astype(BF16), do4))
                    sd = psink * delta
                    for g in range(GQA_GROUP):
                        h = kv * GQA_GROUP + g
                        dsink_rows[h] = dsink_rows[h] - jnp.sum(sd[:, g * BLOCK:(g + 1) * BLOCK])
                dq_ref[blk, :] = _from_head_major(dq_parts).astype(BF16)
                dk2 = jnp.concatenate(dk_parts, axis=1)
                dv2 = jnp.concatenate(dv_parts, axis=1)
                dkv_ref[blk, 0:KV_WIDTH] = (dk_late + dk2[0:BLOCK]).astype(BF16)
                dkv_ref[blk, KV_WIDTH:] = (dv_late + dv2[0:BLOCK]).astype(BF16)
                dk_late, dv_late = dk2[BLOCK:], dv2[BLOCK:]
            dk_c[...] = dk_late
            dv_c[...] = dv_late
            dsink_ref[...] += jnp.concatenate(dsink_rows, axis=0)

        @pl.when(i == nt)
        def _():
            dkv_ref[...] = jnp.zeros_like(dkv_ref)
            dkv_ref[0:BLOCK, 0:KV_WIDTH] = dk_c[...].astype(BF16)
            dkv_ref[0:BLOCK, KV_WIDTH:] = dv_c[...].astype(BF16)

    tile_of = lambda i: jnp.minimum(i, nt - 1)
    tile = pl.BlockSpec((tm, ATTN_WIDTH), lambda i: (tile_of(i), 0))
    return _hosting_call(
        body, "attn_bwd", nt + 1,
        [pl.BlockSpec(memory_space=pltpu.SMEM), _resident((N_BIAS, 2 * BLOCK, GQA_GROUP * BLOCK)), tile]
        + _attn_specs(tm, tile_of),
        [tile, pl.BlockSpec((tm, 2 * KV_WIDTH), lambda i: (i, 0)), _full((ATTN_HEADS, LANES))],
        [jax.ShapeDtypeStruct((rows, ATTN_WIDTH), BF16), jax.ShapeDtypeStruct((rows + tm, 2 * KV_WIDTH), BF16),
         jax.ShapeDtypeStruct((ATTN_HEADS, LANES), F32)],
        [pltpu.VMEM((BLOCK, KV_WIDTH), F32), pltpu.VMEM((BLOCK, KV_WIDTH), F32)],
        (sinks, bias, dattn, qkv, qkv, qkv, qkv, qkv), carried, modes)


ROW_CONV_B, ROW_B_A, ROW_B_X, ROW_LAMBDA = 4, 5, 6, 7


def _rec_bwd(drec, zrec, h, kept, conv_w, wa_bd, wx_bd, lam, carried, modes):
    rows = zrec.shape[0]
    tm = _rec_tile(rows)
    nt = rows // tm
    per = tm // SUBLANES

    def body(drec_ref, xr_ref, yr_ref, h_ref, xc_ref, a_ref, mult_ref, r_ref, ig_ref, hhalo_ref, cw_ref, wa_ref, wx_ref,
             lam_ref, drz_ref, small_ref, dwa_ref, dwx_ref, hbuf, dbuf, dgr_s, dgi_s, dyr_s, carry):
        s = pl.program_id(0)
        i = nt - 1 - s

        @pl.when(s == 0)
        def _():
            small_ref[...] = jnp.zeros_like(small_ref)
            dwa_ref[...] = jnp.zeros_like(dwa_ref)
            dwx_ref[...] = jnp.zeros_like(dwx_ref)
            carry[...] = jnp.zeros_like(carry)
            dbuf[tm:tm + SUBLANES, :] = jnp.zeros((SUBLANES, LRU_WIDTH), F32)

        hbuf[0:SUBLANES, :] = jnp.where(i == 0, 0.0, hhalo_ref[...])
        hbuf[SUBLANES:SUBLANES + tm, :] = h_ref[...]

        row = lax.broadcasted_iota(jnp.int32, (SUBLANES, LRU_WIDTH), 0)
        log_a_scale = (-LRU_C) * _softplus(-lam_ref[...])
        zeros = jnp.zeros((SUBLANES, LRU_WIDTH), F32)

        def group(k, state):
            g_later, a_later, sum_dgr, sum_dgi, sum_lam = state
            o = pl.multiple_of((per - 1 - k) * SUBLANES, SUBLANES)
            rows8 = pl.ds(o, SUBLANES)
            yr, drec_t, h_t, a = yr_ref[rows8, :], drec_ref[rows8, :], h_ref[rows8, :], a_ref[rows8, :]
            gel, t = _gelu(yr)
            dyr_s[rows8, :] = drec_t * h_t * _gelu_grad(yr, t)
            u = drec_t * gel
            coef = jnp.where(row == SUBLANES - 1, a_later, pltpu.roll(a, SUBLANES - 1, 0))
            for sft in (1, 2, 4):
                keep = row < SUBLANES - sft
                u = jnp.where(keep, coef * pltpu.roll(u, SUBLANES - sft, 0) + u, u)
                coef = jnp.where(keep, coef * pltpu.roll(coef, SUBLANES - sft, 0), coef)
            g = coef * g_later + u
            du = jnp.where(i * tm + o + row >= PAD_ROWS, g, 0.0)
            h_before = jnp.where(row == 0, hbuf[rows8, :][SUBLANES - 1:SUBLANES, :], pltpu.roll(h_t, 1, 0))
            xc, mult, r, ig = xc_ref[rows8, :], mult_ref[rows8, :], r_ref[rows8, :], ig_ref[rows8, :]
            dbuf[rows8, :] = du * (mult * ig)
            dgi = (du * (mult * xc)) * (ig * (1.0 - ig))
            dgi_s[rows8, :] = dgi
            dlog_a = (g * h_before) * a - (du * (ig * xc)) * (a * a * pl.reciprocal(mult, approx=True))
            dgr = (dlog_a * log_a_scale) * (r * (1.0 - r))
            dgr_s[rows8, :] = dgr
            return g[0:1, :], a[0:1, :], sum_dgr + dgr, sum_dgi + dgi, sum_lam + dlog_a * r

        state = lax.fori_loop(0, per, group, (carry[0:1, :], carry[1:2, :], zeros, zeros, zeros))
        carry[0:1, :], carry[1:2, :] = state[0], state[1]
        sum_dgr, sum_dgi, sum_lam = (jnp.sum(v, axis=0, keepdims=True) for v in state[2:])
        dlam = sum_lam * (LRU_C * _sigmoid(-lam_ref[...]))

        dgr_b = [dgr_s[:, hh * LRU_HALF:(hh + 1) * LRU_HALF].astype(BF16) for hh in range(2)]
        dgi_b = [dgi_s[:, hh * LRU_HALF:(hh + 1) * LRU_HALF].astype(BF16) for hh in range(2)]
        halves = _lru_halves(xc_ref[...])
        for hh in range(2):
            dwa_ref[hh] += _mm_tn(halves[hh], dgr_b[hh])
            dwx_ref[hh] += _mm_tn(halves[hh], dgi_b[hh])
        dxc = dbuf[0:tm, :] + jnp.concatenate(
            [_mm_nt(dgr_b[hh], wa_ref[hh]) + _mm_nt(dgi_b[hh], wx_ref[hh]) for hh in range(2)], axis=1)

        dbuf[0:tm, :] = dxc
        sum_dxc = jnp.sum(dxc, axis=0, keepdims=True)
        ahead = [dbuf[pl.ds(CONV_WIDTH - 1 - j, tm), :] for j in range(CONV_WIDTH)]
        drz_ref[:, 0:LRU_WIDTH] = sum(cw_ref[j:j + 1, :] * ahead[j] for j in range(CONV_WIDTH)).astype(BF16)
        drz_ref[:, LRU_WIDTH:] = dyr_s[...].astype(BF16)
        upd = [jnp.sum(xr_ref[...] * ahead[j], axis=0, keepdims=True) for j in range(CONV_WIDTH)]
        dbuf[tm:tm + SUBLANES, :] = dbuf[0:SUBLANES, :]
        small_ref[...] += jnp.concatenate(upd + [sum_dxc, sum_dgr, sum_dgi, dlam], axis=0)

    rev = lambda s: nt - 1 - s
    halo = lambda s: jnp.maximum(rev(s) * per - 1, 0)
    cols = lambda k: pl.BlockSpec((tm, LRU_WIDTH), lambda s: (rev(s), k))
    halo0 = pl.BlockSpec((SUBLANES, LRU_WIDTH), lambda s: (halo(s), 0))
    bd = _full((2, LRU_HALF, LRU_HALF))
    big = pltpu.VMEM((tm + SUBLANES, LRU_WIDTH), F32)
    tile = pltpu.VMEM((tm, LRU_WIDTH), F32)
    kept_cols = [cols(k) for k in (KEPT_XC, KEPT_A, KEPT_MULT, KEPT_R, KEPT_I)]
    return _hosting_call(
        body, "rec_bwd", nt,
        [cols(0), cols(0), cols(1), cols(0)] + kept_cols
        + [halo0, _full((CONV_WIDTH, LRU_WIDTH)), bd, bd, _full((1, LRU_WIDTH))],
        [pl.BlockSpec((tm, 2 * LRU_WIDTH), lambda s: (rev(s), 0)), _full((SUBLANES, LRU_WIDTH)), bd, bd],
        [jax.ShapeDtypeStruct((rows, 2 * LRU_WIDTH), BF16), jax.ShapeDtypeStruct((SUBLANES, LRU_WIDTH), F32),
         jax.ShapeDtypeStruct((2, LRU_HALF, LRU_HALF), F32), jax.ShapeDtypeStruct((2, LRU_HALF, LRU_HALF), F32)],
        [big, big, tile, tile, tile, pltpu.VMEM((SUBLANES, LRU_WIDTH), F32)],
        (drec, zrec, zrec, h) + (kept,) * N_KEPT + (h, conv_w, wa_bd, wx_bd, lam), carried, modes)


DZ_CUTS = (0, ATTN_WIDTH, QKV_WIDTH, IN_WIDTH)


def _dz_specs(tm):
    return [pl.BlockSpec((tm, DZ_CUTS[p + 1] - DZ_CUTS[p]), lambda i: (i, 0)) for p in range(3)]


def _in_proj_bwd_x(head, x, g1, dh1, dq, dkv, drz, w_in_t, carried, modes):
    rows = dh1.shape[0]
    tm = _row_tile(rows)
    steps = rows // tm

    def body(head_ref, g_ref, dh1_ref, dq_ref, dkv_ref, drz_ref, w_ref, x_hbm, dh0_ref, dg_ref, buf, sem):
        i = pl.program_id(0)
        h0 = _h0_tile(head_ref, x_hbm, buf, sem, i, steps, tm)

        @pl.when(i == 0)
        def _():
            dg_ref[...] = jnp.zeros_like(dg_ref)

        g = g_ref[...]
        _, xhat, rstd = _rms_fwd(h0, g)
        parts = (dq_ref[...], dkv_ref[...], drz_ref[...])
        du = sum(_mm(parts[p], w_ref[DZ_CUTS[p]:DZ_CUTS[p + 1], :]) for p in range(3))
        dx, dg = _rms_bwd(du, xhat, rstd, g)
        dh0_ref[...] = dh1_ref[...] + dx
        dg_ref[...] += dg

    wide = pl.BlockSpec((tm, D_MODEL), lambda i: (i, 0))
    return _hosting_call(
        body, "in_proj_bwd_x", steps,
        [_full((BLOCK, D_MODEL)), _full((1, D_MODEL)), wide] + _dz_specs(tm) + [_resident((IN_WIDTH, D_MODEL)), ANY_SPACE],
        [wide, _full((1, D_MODEL))],
        [jax.ShapeDtypeStruct((rows, D_MODEL), F32), jax.ShapeDtypeStruct((1, D_MODEL), F32)],
        _frame_scratch(tm), (head, g1, dh1, dq, dkv, drz, w_in_t, x), carried, modes)


def _in_proj_bwd_w(u1, dq, dkv, drz, carried, modes):
    rows = u1.shape[0]
    tb = _big_tile(rows)
    steps = rows // tb

    def body(u_ref, dq_ref, dkv_ref, drz_ref, dw_ref, acc):
        i = pl.program_id(0)

        @pl.when(i == 0)
        def _():
            acc[...] = jnp.zeros_like(acc)

        u = u_ref[...]
        for p, ref in enumerate((dq_ref, dkv_ref, drz_ref)):
            acc[:, DZ_CUTS[p]:DZ_CUTS[p + 1]] += _mm_tn(u, ref[...])

        @pl.when(i == steps - 1)
        def _():
            dw_ref[...] = acc[...].astype(BF16)

    return _hosting_call(
        body, "in_proj_bwd_w", steps,
        [pl.BlockSpec((tb, D_MODEL), lambda i: (i, 0))] + _dz_specs(tb),
        [_full((D_MODEL, IN_WIDTH))],
        [jax.ShapeDtypeStruct((D_MODEL, IN_WIDTH), BF16)],
        [pltpu.VMEM((D_MODEL, IN_WIDTH), F32)], (u1, dq, dkv, drz), carried, modes)


def _adamw_math(w, m, v, g):
    nm = ADAM_B1 * m + (1.0 - ADAM_B1) * g
    nv = ADAM_B2 * v + (1.0 - ADAM_B2) * (g * g)
    m_hat = nm / (1.0 - ADAM_B1 ** ADAM_STEP)
    v_hat = nv / (1.0 - ADAM_B2 ** ADAM_STEP)
    return (-ADAM_LR) * (m_hat / (jnp.sqrt(v_hat) + ADAM_EPS) + ADAM_WD * w), nm, nv


SMALL_NAMES = ("conv_b", "b_a", "b_x", "lru_lambda", "attn_sinks", "g_post_mix", "g_pre_ffn", "g_post_ffn")
PACK_WIDTH = 1024


def _pack_rows(vals):
    assert len(SMALL_NAMES) == SUBLANES
    row = lax.broadcasted_iota(jnp.int32, (SUBLANES, PACK_WIDTH), 0)
    tile = jnp.zeros((SUBLANES, PACK_WIDTH), F32)
    for k, name in enumerate(SMALL_NAMES):
        a = vals[name].reshape(1, -1)
        tile = jnp.where(row == k, jnp.pad(a, ((0, 0), (0, PACK_WIDTH - a.shape[1]))), tile)
    return tile


def _adamw_small(weights, mom_m, mom_v, parts, loss_parts):
    n = len(SMALL_NAMES)
    views = [(1, weights[name].size) for name in SMALL_NAMES]

    def body(*refs):
        w_refs, m_refs, v_refs = refs[:n], refs[n:2 * n], refs[2 * n:3 * n]
        p_ref, l_ref, loss_ref = refs[3 * n], refs[3 * n + 1], refs[3 * n + 2]
        outs = refs[3 * n + 3:]
        for k, (_, c) in enumerate(views):
            g = p_ref[0, k:k + 1, 0:c]
            for s in range(1, N_DEV):
                g = g + p_ref[s, k:k + 1, 0:c]
            g_ref, d_ref, nm_ref, nv_ref = outs[4 * k:4 * k + 4]
            g_ref[...] = g
            d_ref[...], nm_ref[...], nv_ref[...] = _adamw_math(w_refs[k][...], m_refs[k][...], v_refs[k][...], g)
        total = l_ref[0]
        for s in range(1, N_DEV):
            total = total + l_ref[s]
        loss_ref[...] = total

    args = [src[name].reshape(view) for src in (weights, mom_m, mom_v) for name, view in zip(SMALL_NAMES, views)]
    res = pl.pallas_call(
        body, name="adamw_small",
        out_shape=[jax.ShapeDtypeStruct(loss_parts.shape[1:], F32)]
                  + [jax.ShapeDtypeStruct(view, F32) for view in views for _ in range(4)],
        compiler_params=pltpu.CompilerParams(vmem_limit_bytes=VMEM_LIMIT),
    )(*args, parts, loss_parts)
    out = {name: tuple(t.reshape(weights[name].shape) for t in res[1 + 4 * k:5 + 4 * k]) for k, name in enumerate(SMALL_NAMES)}
    return res[0], out


def _adamw(w, m, v, parts, name, by_rows=False):
    rows, cols = w.shape
    tr = next((t for t in (256, 128) if rows % t == 0), rows)
    parts = parts if isinstance(parts, (list, tuple)) else [parts]
    steps = rows // tr
    assert not by_rows or all(p.shape[1:] == (tr, cols) for p in parts) and len(parts) == steps

    def body(w_ref, m_ref, v_ref, *refs):
        p_refs, (g_ref, d_ref, nm_ref, nv_ref) = refs[:len(parts)], refs[len(parts):]

        def total(p_ref):
            g = p_ref[0].astype(F32)
            for s in range(1, N_DEV):
                g = g + p_ref[s].astype(F32)
            return g

        def update(g):
            g_ref[...] = g
            d_ref[...], nm_ref[...], nv_ref[...] = _adamw_math(w_ref[...], m_ref[...], v_ref[...], g)

        if by_rows:
            for k, p_ref in enumerate(p_refs):
                pl.when(pl.program_id(0) == k)(lambda p_ref=p_ref: update(total(p_ref)))
        else:
            update(jnp.concatenate([total(p_ref) for p_ref in p_refs], axis=1) if len(parts) > 1 else total(p_refs[0]))

    blk = pl.BlockSpec((tr, cols), lambda i: (i, 0))
    part_row = (lambda i: (0, 0, 0)) if by_rows else (lambda i: (0, i, 0))
    return pl.pallas_call(
        body, name=name, grid=(steps,),
        in_specs=[blk, blk, blk] + [pl.BlockSpec((N_DEV, tr, p.shape[2]), part_row) for p in parts],
        out_specs=[blk] * 4,
        out_shape=[jax.ShapeDtypeStruct((rows, cols), F32)] * 4,
        compiler_params=_params(("parallel",)),
    )(w, m, v, *parts)


def _cols_from_shards(g):
    return jnp.transpose(g, (1, 0, 2)).reshape(g.shape[1], N_DEV * g.shape[2])


def _cols_to_shards(a):
    r, c = a.shape
    return jnp.transpose(a.reshape(r, N_DEV, c // N_DEV), (1, 0, 2))


def _block_diag(w):
    per = LRU_HALF // LRU_BLOCK
    w = w.reshape(2, per, LRU_BLOCK, LRU_BLOCK)
    eye = jnp.eye(per, dtype=w.dtype)
    return (w[:, :, :, None, :] * eye[None, :, None, :, None]).reshape(2, LRU_HALF, LRU_HALF)


def _block_diag_extract(t):
    per = LRU_HALF // LRU_BLOCK
    t = t.reshape(2, per, LRU_BLOCK, per, LRU_BLOCK)
    return jnp.stack([t[:, b, :, b, :] for b in range(per)], axis=1).reshape(LRU_BLOCKS, LRU_BLOCK, LRU_BLOCK)


def kernel(x, meta_tokens, g_pre_mix, w_in, conv_w, conv_b, w_a, b_a, w_x, b_x, lru_lambda, attn_sinks, w_out, g_post_mix, g_pre_ffn, w_ff1, w_ff2, g_post_ffn, loss_target, m_meta_tokens, m_g_pre_mix, m_w_in, m_conv_w, m_conv_b, m_w_a, m_b_a, m_w_x, m_b_x, m_lru_lambda, m_attn_sinks, m_w_out, m_g_post_mix, m_g_pre_ffn, m_w_ff1, m_w_ff2, m_g_post_ffn, v_meta_tokens, v_g_pre_mix, v_w_in, v_conv_w, v_conv_b, v_w_a, v_b_a, v_w_x, v_b_x, v_lru_lambda, v_attn_sinks, v_w_out, v_g_post_mix, v_g_pre_ffn, v_w_ff1, v_w_ff2, v_g_post_ffn):
    weights = dict(meta_tokens=meta_tokens, g_pre_mix=g_pre_mix, w_in=w_in, conv_w=conv_w, conv_b=conv_b, w_a=w_a,
                   b_a=b_a, w_x=w_x, b_x=b_x, lru_lambda=lru_lambda, attn_sinks=attn_sinks, w_out=w_out,
                   g_post_mix=g_post_mix, g_pre_ffn=g_pre_ffn, w_ff1=w_ff1, w_ff2=w_ff2, g_post_ffn=g_post_ffn)
    mom_m = dict(meta_tokens=m_meta_tokens, g_pre_mix=m_g_pre_mix, w_in=m_w_in, conv_w=m_conv_w, conv_b=m_conv_b,
                 w_a=m_w_a, b_a=m_b_a, w_x=m_w_x, b_x=m_b_x, lru_lambda=m_lru_lambda, attn_sinks=m_attn_sinks,
                 w_out=m_w_out, g_post_mix=m_g_post_mix, g_pre_ffn=m_g_pre_ffn, w_ff1=m_w_ff1, w_ff2=m_w_ff2,
                 g_post_ffn=m_g_post_ffn)
    mom_v = dict(meta_tokens=v_meta_tokens, g_pre_mix=v_g_pre_mix, w_in=v_w_in, conv_w=v_conv_w, conv_b=v_conv_b,
                 w_a=v_w_a, b_a=v_b_a, w_x=v_w_x, b_x=v_b_x, lru_lambda=v_lru_lambda, attn_sinks=v_attn_sinks,
                 w_out=v_w_out, g_post_mix=v_g_post_mix, g_pre_ffn=v_g_pre_ffn, w_ff1=v_w_ff1, w_ff2=v_w_ff2,
                 g_post_ffn=v_g_post_ffn)
    order = list(weights)

    (g_win, g_meta, g_cw) = _gather_two_level([w_in[0].astype(BF16), meta_tokens, conv_w[0]], "gather_first")
    w_in_full = _cols_from_shards(g_win)
    meta_full = _cols_from_shards(g_meta)
    conv_w_full = _cols_from_shards(g_cw)

    head = jnp.concatenate([jnp.zeros((PAD_ROWS, D_MODEL), F32), meta_full], axis=0)
    wa_bd = _block_diag(w_a[0]).astype(BF16)
    wx_bd = _block_diag(w_x[0]).astype(BF16)
    bias = _attn_bias()

    w1_shard = w_ff1[0].astype(BF16)
    (qkv, zrec, u1), (g_wout,) = _in_proj_fwd(head, x[0], g_pre_mix, w_in_full, [w_out[0].astype(BF16)], ["gather"])
    (attn,), (w1a,) = _attn_fwd(qkv, attn_sinks, bias, [w1_shard[:, :FF_HALF]], ["gather"])
    (rec, h_lru, kept), (w1b,) = _rec_fwd(zrec, conv_w_full, conv_b, wa_bd, b_a, wx_bd, b_x, lru_lambda,
                                         [w1_shard[:, FF_HALF:]], ["gather"])
    w_out_full = g_wout.reshape(D_MODEL, D_MODEL)
    w2_shard = w_ff2[0].astype(BF16)
    (mix, h1), (w2a,) = _out_proj_fwd(attn, rec, w_out_full, head, x[0], g_post_mix, [w2_shard[:FF_HALF]], ["gather"])
    (act, u2), (w2b,) = _ffn_up(h1, g_pre_ffn, (w1a, w1b), [w2_shard[FF_HALF:]], ["gather"])
    w2_halves = [w.reshape(D_FF // 2, D_MODEL) for w in (w2a, w2b)]
    (dy, df, dg_post_ffn, loss_acc), w2t_halves = _ffn_down_loss(
        act, w2_halves, h1, loss_target[0], g_post_ffn, [w2_shard[:FF_HALF].T, w2_shard[FF_HALF:].T], ["gather"] * 2)

    (da1,), (w1ta,) = _ffn_bwd_act(df, w2t_halves, act, [w1_shard[:, :FF_HALF].T], ["gather"])
    (dw1h, dw2h), (w1tb,) = _ffn_bwd_weights(u2, da1, act, df, [w1_shard[:, FF_HALF:].T], ["gather"])
    w1t_halves = [w.reshape(D_FF // 2, D_MODEL) for w in (w1ta, w1tb)]
    (dh1, dg_pre_ffn), (p_w1a,) = _ffn_bwd_x(da1, w1t_halves, h1, dy, g_pre_ffn, [dw1h[0]], ["scatter"])
    (dattn, drec, dw_out, dg_post_mix), (p_w1b,) = _out_proj_bwd(dh1, mix, g_post_mix, w_out_full.T, attn, rec,
                                                                [dw1h[1]], ["scatter"])
    (dq, dkv_late, dsinks), (p_w2a,) = _attn_bwd(qkv, dattn, attn_sinks, bias, [dw2h[0]], ["scatter"])
    dkv = dkv_late[BLOCK:BLOCK + qkv.shape[0]]
    (drz, rec_small, dwa_bd, dwx_bd), (p_wout, p_w2b) = _rec_bwd(
        drec, zrec, h_lru, kept, conv_w_full, wa_bd, wx_bd, lru_lambda,
        [dw_out.reshape(N_DEV, D_MODEL // N_DEV, D_MODEL), dw2h[1]], ["scatter", "scatter"])
    small_grads = dict(
        conv_b=rec_small[ROW_CONV_B], b_a=rec_small[ROW_B_A], b_x=rec_small[ROW_B_X], lru_lambda=rec_small[ROW_LAMBDA],
        attn_sinks=dsinks[:, 0], g_post_mix=dg_post_mix, g_pre_ffn=dg_pre_ffn, g_post_ffn=dg_post_ffn)
    gate_rows = (LRU_BLOCKS * LRU_BLOCK, LRU_BLOCK)
    gate_dense = (LRU_BLOCKS * LRU_BLOCK * LRU_BLOCK // PACK_WIDTH, PACK_WIDTH)
    (dw_in,), (p_cw, p_small, p_wa, p_wx) = _in_proj_bwd_w(
        u1, dq, dkv, drz,
        [_cols_to_shards(rec_small[0:CONV_WIDTH]), _pack_rows(small_grads),
         _block_diag_extract(dwa_bd).reshape(gate_dense), _block_diag_extract(dwx_bd).reshape(gate_dense)],
        ["scatter", "gather", "gather", "gather"])
    p_wa, p_wx = (p.reshape((N_DEV,) + gate_rows) for p in (p_wa, p_wx))
    (dh0, dg_pre_mix), (p_win,) = _in_proj_bwd_x(
        head, x[0], g_pre_mix, dh1, dq, dkv, drz, w_in_full.T, [_cols_to_shards(dw_in)], ["scatter"])
    p_meta, p_gpm, p_loss = _exchange([_cols_to_shards(dh0[PAD_ROWS:BLOCK]), dg_pre_mix, loss_acc],
                                      ["scatter", "gather", "gather"], "exchange_last")

    res = {}
    res["g_pre_mix"] = _adamw(g_pre_mix, m_g_pre_mix, v_g_pre_mix, p_gpm, "adamw_g_pre_mix")
    res["w_in"] = _adamw(w_in[0], m_w_in[0], v_w_in[0], p_win, "adamw_w_in")
    res["w_out"] = _adamw(w_out[0], m_w_out[0], v_w_out[0], p_wout, "adamw_w_out")
    res["w_ff1"] = _adamw(w_ff1[0], m_w_ff1[0], v_w_ff1[0], [p_w1a, p_w1b], "adamw_w_ff1")
    res["w_ff2"] = _adamw(w_ff2[0], m_w_ff2[0], v_w_ff2[0], [p_w2a, p_w2b], "adamw_w_ff2", by_rows=True)
    res["meta_tokens"] = _adamw(meta_tokens, m_meta_tokens, v_meta_tokens, p_meta, "adamw_meta")
    res["conv_w"] = _adamw(conv_w[0], m_conv_w[0], v_conv_w[0], p_cw, "adamw_conv_w")
    for name in ("w_in", "w_out", "w_ff1", "w_ff2", "conv_w"):
        res[name] = tuple(t[None] for t in res[name])
    for name, parts in (("w_a", p_wa), ("w_x", p_wx)):
        gate = _adamw(*(src[name].reshape(gate_rows) for src in (weights, mom_m, mom_v)), parts, "adamw_" + name)
        res[name] = tuple(t.reshape(weights[name].shape) for t in gate)
    loss_total, small = _adamw_small(weights, mom_m, mom_v, p_small, p_loss)
    res.update(small)

    grad_x = dh0[BLOCK:][None]
    outs = [loss_total[0, 0], grad_x]
    for k in range(4):
        outs += [res[name][k] for name in order]
    return tuple(outs)
```

```python
import jax
import jax.numpy as jnp
import numpy as np
from jax import lax
from jax.experimental import pallas as pl
from jax.experimental.pallas import tpu as pltpu

F32 = jnp.float32
BF16 = jnp.bfloat16

D_MODEL = 1024
N_META = 16
HEAD_DIM = 64
ATTN_HEADS = 8
KV_HEADS = 2
GQA_GROUP = ATTN_HEADS // KV_HEADS
ATTN_WIDTH = ATTN_HEADS * HEAD_DIM
KV_WIDTH = KV_HEADS * HEAD_DIM
QKV_WIDTH = ATTN_WIDTH + 2 * KV_WIDTH
LRU_WIDTH = 512
LRU_BLOCKS = 8
LRU_BLOCK = 64
LRU_HALF = 256
LRU_C = 8.0
CONV_WIDTH = 4
BLOCK = 128
PAD_ROWS = BLOCK - N_META
IN_WIDTH = QKV_WIDTH + 2 * LRU_WIDTH
D_FF = 4096
EPS = 1e-6
NEG = -1e30
N_DEV = 8
FF_CHUNK = D_FF // N_DEV
SUBLANES = 8
LANES = 128

ADAM_LR = 0.001
ADAM_B1 = 0.9
ADAM_B2 = 0.999
ADAM_EPS = 1e-08
ADAM_WD = 0.01
ADAM_STEP = 10

VMEM_LIMIT = 56 * 1024 * 1024


def _row_tile(rows):
    for t in (640, 512, 256, 128):
        if rows % t == 0:
            return t
    raise ValueError(rows)


def _big_tile(rows):
    for t in (1664, 1024, 512, 256, 128):
        if rows % t == 0:
            return t
    raise ValueError(rows)


def _rec_tile(rows):
    for t in (640, 256, 128):
        if rows % t == 0:
            return t
    raise ValueError(rows)


def _params(semantics):
    return pltpu.CompilerParams(dimension_semantics=semantics, vmem_limit_bytes=VMEM_LIMIT)


def _mm(a, b):
    return lax.dot_general(a, b, (((1,), (0,)), ((), ())), preferred_element_type=F32)


def _mm_nt(a, b):
    return lax.dot_general(a, b, (((1,), (1,)), ((), ())), preferred_element_type=F32)


def _mm_tn(a, b):
    return lax.dot_general(a, b, (((0,), (0,)), ((), ())), preferred_element_type=F32)


def _rms_fwd(x, g):
    rstd = lax.rsqrt(jnp.mean(x * x, axis=-1, keepdims=True) + EPS)
    xhat = x * rstd
    return xhat * g, xhat, rstd


def _rms_bwd(dy, xhat, rstd, g):
    dyg = dy * g
    c = jnp.mean(dyg * xhat, axis=-1, keepdims=True)
    dx = rstd * (dyg - xhat * c)
    dg = jnp.sum(dy * xhat, axis=0, keepdims=True)
    return dx, dg


def _sigmoid(x):
    return pl.reciprocal(1.0 + jnp.exp(-x), approx=True)


def _log1p(x):
    u = 1.0 + x
    return jnp.where(u == 1.0, x, jnp.log(u) * x / (u - 1.0))


def _one_minus_sq_exp(x, ex):
    return -jnp.tanh(x) * (1.0 + ex * ex)


TINY = 1e-30


def _sqrt_pos(y):
    r = lax.rsqrt(jnp.maximum(y, TINY))
    return y * r, r


def _softplus(x):
    return jnp.maximum(x, 0.0) + _log1p(jnp.exp(-jnp.abs(x)))


GELU_C = 0.7978845608028654
GELU_K = 0.044715


def _gelu(x):
    t = jnp.tanh(GELU_C * (x + GELU_K * x * x * x))
    return 0.5 * x * (1.0 + t), t


def _gelu_grad(x, t):
    return 0.5 * (1.0 + t) + 0.5 * x * (1.0 - t * t) * GELU_C * (1.0 + 3.0 * GELU_K * x * x)


def _full(shape):
    return pl.BlockSpec(shape, lambda *_: (0,) * len(shape))


def _resident(shape):
    return pl.BlockSpec(shape, lambda *_: (0,) * len(shape), pipeline_mode=pl.Buffered(1))


def _exchange_copies(ins, outs, sems, modes):
    send_sems, recv_sems, local_sems = sems
    x, y, c = lax.axis_index("x"), lax.axis_index("y"), lax.axis_index("c")
    me = 4 * x + 2 * y + c

    def block(a, dev):
        return ins[a] if modes[a] == "gather" else ins[a].at[dev]

    local = [pltpu.make_async_copy(block(a, me), outs[a].at[me], local_sems.at[a]) for a in range(len(ins))]
    sends, recvs = [], []
    for a in range(len(ins)):
        for k in range(N_DEV - 1):
            bits = k + 1
            px = jnp.bitwise_xor(x, (bits >> 2) & 1)
            py = jnp.bitwise_xor(y, (bits >> 1) & 1)
            pc = jnp.bitwise_xor(c, bits & 1)
            peer = 4 * px + 2 * py + pc
            common = dict(src_ref=block(a, peer), send_sem=send_sems.at[a, k], recv_sem=recv_sems.at[a, k],
                          device_id=(px, py, pc), device_id_type=pl.DeviceIdType.MESH)
            sends.append(pltpu.make_async_remote_copy(dst_ref=outs[a].at[me], **common))
            recvs.append(pltpu.make_async_remote_copy(dst_ref=outs[a].at[peer], **common))
    return local, sends, recvs


def _exchange_start(ins, outs, sems, modes):
    local, sends, _ = _exchange_copies(ins, outs, sems, modes)
    for cp in local + sends:
        cp.start()


def _exchange_wait(ins, outs, sems, modes):
    local, sends, recvs = _exchange_copies(ins, outs, sems, modes)
    for cp in recvs:
        cp.wait_recv()
    for cp in sends:
        cp.wait_send()
    for cp in local:
        cp.wait()


def _exchange_shapes(arrays, modes):
    return [jax.ShapeDtypeStruct((N_DEV,) + a.shape if mode == "gather" else a.shape, a.dtype)
            for a, mode in zip(arrays, modes)]


def _exchange_sems(na):
    return [pltpu.SemaphoreType.DMA((na, N_DEV - 1)), pltpu.SemaphoreType.DMA((na, N_DEV - 1)),
            pltpu.SemaphoreType.DMA((na,))]


ANY_SPACE = pl.BlockSpec(memory_space=pl.ANY)


def _exchange(arrays, modes, name):
    na = len(arrays)

    def body(*refs):
        ins, outs, sems = refs[:na], refs[na:2 * na], refs[2 * na:]
        _exchange_start(ins, outs, sems, modes)
        _exchange_wait(ins, outs, sems, modes)

    return pl.pallas_call(
        body, name=name, out_shape=_exchange_shapes(arrays, modes),
        in_specs=[ANY_SPACE] * na, out_specs=[ANY_SPACE] * na, scratch_shapes=_exchange_sems(na),
        compiler_params=pltpu.CompilerParams(has_side_effects=True),
    )(*arrays)


def _gather_two_level(arrays, name):
    na = len(arrays)

    def body(*refs):
        ins, outs = refs[:na], refs[na:2 * na]
        send_sems, recv_sems, local_sems = refs[2 * na:]
        x, y, c = lax.axis_index("x"), lax.axis_index("y"), lax.axis_index("c")
        me, sibling = (x, y, c), (x, y, 1 - c)
        chips = [(1 - x, y), (x, 1 - y), (1 - x, 1 - y)]

        def copy(a, k, block, to, src=None):
            slot = outs[a].at[4 * block[0] + 2 * block[1] + block[2]]
            return pltpu.make_async_remote_copy(
                src_ref=slot if src is None else src, dst_ref=slot, send_sem=send_sems.at[a, k],
                recv_sem=recv_sems.at[a, k], device_id=to, device_id_type=pl.DeviceIdType.MESH)

        local = [pltpu.make_async_copy(ins[a], outs[a].at[4 * x + 2 * y + c], local_sems.at[a]) for a in range(na)]
        first = []
        for a in range(na):
            first.append(copy(a, 0, me, sibling, src=ins[a]))
            first += [copy(a, 1 + j, me, (*chip, c), src=ins[a]) for j, chip in enumerate(chips)]
        for cp in local + first:
            cp.start()
        passed = []
        for j, chip in enumerate(chips):
            for a in range(na):
                copy(a, 1 + j, (*chip, c), me).wait_recv()
                passed.append(copy(a, 4 + j, (*chip, c), sibling))
                passed[-1].start()
        for a in range(na):
            copy(a, 0, sibling, me).wait_recv()
            for j, chip in enumerate(chips):
                copy(a, 4 + j, (*chip, 1 - c), me).wait_recv()
        for cp in first + passed:
            cp.wait_send()
        for cp in local:
            cp.wait()

    return pl.pallas_call(
        body, name=name, out_shape=_exchange_shapes(arrays, ["gather"] * na),
        in_specs=[ANY_SPACE] * na, out_specs=[ANY_SPACE] * na, scratch_shapes=_exchange_sems(na),
        compiler_params=pltpu.CompilerParams(has_side_effects=True),
    )(*arrays)


def _hosting_call(body, name, steps, in_specs, out_specs, out_shape, scratch_shapes, args, arrays, modes):
    n_in, n_out, n_scr, na = len(in_specs), len(out_specs), len(scratch_shapes), len(arrays)
    grid = steps if isinstance(steps, tuple) else (steps,)

    def hosting_body(*refs):
        cuts = [0]
        for n in (n_in, na, n_out, na, n_scr, 3):
            cuts.append(cuts[-1] + n)
        ins, x_ins, outs, x_outs, scr, sems = (refs[cuts[p]:cuts[p + 1]] for p in range(6))
        first, last = True, True
        for axis, n in enumerate(grid):
            first = first & (pl.program_id(axis) == 0)
            last = last & (pl.program_id(axis) == n - 1)

        @pl.when(first)
        def _():
            _exchange_start(x_ins, x_outs, sems, modes)

        body(*ins, *outs, *scr)

        @pl.when(last)
        def _():
            _exchange_wait(x_ins, x_outs, sems, modes)

    res = pl.pallas_call(
        hosting_body, name=name, grid=grid,
        in_specs=list(in_specs) + [ANY_SPACE] * na, out_specs=list(out_specs) + [ANY_SPACE] * na,
        out_shape=list(out_shape) + _exchange_shapes(arrays, modes),
        scratch_shapes=list(scratch_shapes) + _exchange_sems(na),
        compiler_params=_params(("arbitrary",) * len(grid)),
    )(*args, *arrays)
    return res[:n_out], res[n_out:]


def _frame_rows(src_hbm, buf, sem, i, steps, tm):
    def first():
        return pltpu.make_async_copy(src_hbm.at[pl.ds(0, tm - BLOCK)], buf.at[0, pl.ds(BLOCK, tm - BLOCK)], sem.at[0])

    def later(t, slot):
        return pltpu.make_async_copy(src_hbm.at[pl.ds(pl.multiple_of(t * tm - BLOCK, SUBLANES), tm)], buf.at[slot], sem.at[slot])

    slot = i % 2

    @pl.when(i == 0)
    def _():
        first().start()

    @pl.when(i + 1 < steps)
    def _():
        later(i + 1, 1 - slot).start()

    @pl.when(i == 0)
    def _():
        first().wait()

    @pl.when(i > 0)
    def _():
        later(i, slot).wait()

    return slot


def _frame_scratch(tm):
    return [pltpu.VMEM((2, tm, D_MODEL), F32), pltpu.SemaphoreType.DMA((2,))]


def _h0_tile(head_ref, x_hbm, buf, sem, i, steps, tm):
    slot = _frame_rows(x_hbm, buf, sem, i, steps, tm)

    @pl.when(i == 0)
    def _():
        buf[0, 0:BLOCK, :] = head_ref[...]

    return buf[slot]


def _in_proj_fwd(head, x, g1, w_in, carried, modes):
    rows = BLOCK + x.shape[0]
    tm = _row_tile(rows)
    steps = rows // tm

    def body(head_ref, g_ref, w_ref, x_hbm, qkv_ref, zrec_ref, u_ref, buf, sem):
        h = _h0_tile(head_ref, x_hbm, buf, sem, pl.program_id(0), steps, tm)
        u, _, _ = _rms_fwd(h, g_ref[...])
        u = u.astype(BF16)
        u_ref[...] = u
        z = _mm(u, w_ref[...])
        qkv_ref[...] = z[:, :QKV_WIDTH].astype(BF16)
        zrec_ref[...] = z[:, QKV_WIDTH:]

    wide = pl.BlockSpec((tm, D_MODEL), lambda i: (i, 0))
    return _hosting_call(
        body, "in_proj_fwd", steps,
        [_full((BLOCK, D_MODEL)), _full((1, D_MODEL)), _resident((D_MODEL, IN_WIDTH)), ANY_SPACE],
        [pl.BlockSpec((tm, QKV_WIDTH), lambda i: (i, 0)), pl.BlockSpec((tm, 2 * LRU_WIDTH), lambda i: (i, 0)), wide],
        [jax.ShapeDtypeStruct((rows, QKV_WIDTH), BF16), jax.ShapeDtypeStruct((rows, 2 * LRU_WIDTH), F32),
         jax.ShapeDtypeStruct((rows, D_MODEL), BF16)],
        _frame_scratch(tm), (head, g1, w_in, x), carried, modes)


N_BIAS = 3


def _attn_bias():
    key = np.arange(2 * BLOCK)[:, None]
    r = np.arange(GQA_GROUP * BLOCK)[None, :] % BLOCK
    band = (key > r) & (key <= r + BLOCK)
    out = [np.where(band & ((n - 1) * BLOCK + key >= PAD_ROWS), 0.0, NEG) for n in range(N_BIAS)]
    return jnp.asarray(np.stack(out), F32)


def _attn_probs(k2, q4, bias, sink_row):
    s = _mm_nt(k2, q4) * (HEAD_DIM ** -0.5) + bias
    m = jnp.maximum(jnp.max(s, axis=0, keepdims=True), sink_row)
    p = jnp.exp(s - m)
    es = jnp.exp(sink_row - m)
    inv = 1.0 / (jnp.sum(p, axis=0, keepdims=True) + es)
    return p * inv, es * inv


def _heads(ref, rows, first, count):
    return jnp.concatenate([ref[rows, (first + g) * HEAD_DIM:(first + g + 1) * HEAD_DIM] for g in range(count)], axis=0)


def _keys_of_block(prev_ref, cur_ref, b, kv):
    sl = slice(kv * HEAD_DIM, (kv + 1) * HEAD_DIM)
    before = prev_ref[:, sl] if b == 0 else cur_ref[(b - 1) * BLOCK:b * BLOCK, sl]
    return jnp.concatenate([before, cur_ref[b * BLOCK:(b + 1) * BLOCK, sl]], axis=0)


def _bias_of_block(bias_ref, block):
    return bias_ref[jnp.minimum(block, N_BIAS - 1)]


def _sink_row(sink_ref, kv):
    g = lax.broadcasted_iota(jnp.int32, (1, GQA_GROUP * BLOCK), 1) // BLOCK
    row = jnp.full((1, GQA_GROUP * BLOCK), sink_ref[0, kv * GQA_GROUP], F32)
    for i in range(1, GQA_GROUP):
        row = jnp.where(g == i, sink_ref[0, kv * GQA_GROUP + i], row)
    return row


def _from_head_major(pieces):
    return jnp.concatenate(pieces, axis=0).T


def _attn_specs(tm, tile_of):
    nbt = tm // BLOCK
    k_col, v_col = ATTN_WIDTH // KV_WIDTH, ATTN_WIDTH // KV_WIDTH + 1
    before = lambda i: jnp.maximum(tile_of(i) * nbt - 1, 0)
    return [pl.BlockSpec((tm, ATTN_WIDTH), lambda i: (tile_of(i), 0)),
            pl.BlockSpec((BLOCK, KV_WIDTH), lambda i: (before(i), k_col)),
            pl.BlockSpec((tm, KV_WIDTH), lambda i: (tile_of(i), k_col)),
            pl.BlockSpec((BLOCK, KV_WIDTH), lambda i: (before(i), v_col)),
            pl.BlockSpec((tm, KV_WIDTH), lambda i: (tile_of(i), v_col))]


def _attn_fwd(qkv, sinks, bias, carried, modes):
    rows = qkv.shape[0]
    tm = _row_tile(rows)
    nbt = tm // BLOCK

    def body(sink_ref, bias_ref, q_ref, kp_ref, kc_ref, vp_ref, vc_ref, o_ref):
        i = pl.program_id(0)
        for b in range(nbt):
            blk = slice(b * BLOCK, (b + 1) * BLOCK)
            bias_t = _bias_of_block(bias_ref, i * nbt + b)
            pieces = []
            for kv in range(KV_HEADS):
                k2 = _keys_of_block(kp_ref, kc_ref, b, kv)
                v2 = _keys_of_block(vp_ref, vc_ref, b, kv)
                q4 = _heads(q_ref, blk, kv * GQA_GROUP, GQA_GROUP)
                pn, _ = _attn_probs(k2, q4, bias_t, _sink_row(sink_ref, kv))
                ot = _mm_tn(v2, pn.astype(BF16))
                pieces += [ot[:, g * BLOCK:(g + 1) * BLOCK] for g in range(GQA_GROUP)]
            o_ref[blk, :] = _from_head_major(pieces).astype(BF16)

    return _hosting_call(
        body, "attn_fwd", rows // tm,
        [pl.BlockSpec(memory_space=pltpu.SMEM), _resident((N_BIAS, 2 * BLOCK, GQA_GROUP * BLOCK))]
        + _attn_specs(tm, lambda i: i),
        [pl.BlockSpec((tm, ATTN_WIDTH), lambda i: (i, 0))],
        [jax.ShapeDtypeStruct((rows, ATTN_WIDTH), BF16)],
        [], (sinks, bias, qkv, qkv, qkv, qkv, qkv), carried, modes)


def _conv_taps(xbuf, tm):
    return [xbuf[pl.ds(SUBLANES - (CONV_WIDTH - 1 - j), tm), :] for j in range(CONV_WIDTH)]


def _lru_halves(xc):
    return [xc[:, h * LRU_HALF:(h + 1) * LRU_HALF].astype(BF16) for h in range(2)]


def _lru_gates(xc, wa_ref, ba_ref, wx_ref, bx_ref, lam_ref):
    halves = _lru_halves(xc)
    gate_r = jnp.concatenate([_mm(halves[h], wa_ref[h]) for h in range(2)], axis=1) + ba_ref[...]
    gate_i = jnp.concatenate([_mm(halves[h], wx_ref[h]) for h in range(2)], axis=1) + bx_ref[...]
    r = _sigmoid(gate_r)
    ig = _sigmoid(gate_i)
    log_a = (-LRU_C) * r * _softplus(-lam_ref[...])
    a = jnp.exp(log_a)
    mult, _ = _sqrt_pos(_one_minus_sq_exp(log_a, a))
    return r, ig, a, mult


KEPT_XC, KEPT_A, KEPT_MULT, KEPT_R, KEPT_I, N_KEPT = 0, 1, 2, 3, 4, 5


def _scan_tile(a_ref, u_ref, out_ref, carry, tm):
    row = lax.broadcasted_iota(jnp.int32, (SUBLANES, LRU_WIDTH), 0)

    def step(j, before):
        o = pl.multiple_of(j * SUBLANES, SUBLANES)
        a = a_ref[pl.ds(o, SUBLANES), :]
        u = u_ref[pl.ds(o, SUBLANES), :]
        for s in (1, 2, 4):
            keep = row >= s
            u = jnp.where(keep, a * pltpu.roll(u, s, 0) + u, u)
            a = jnp.where(keep, a * pltpu.roll(a, s, 0), a)
        out = a * before + u
        out_ref[pl.ds(o, SUBLANES), :] = out
        return out[SUBLANES - 1:SUBLANES, :]

    return lax.fori_loop(0, tm // SUBLANES, step, carry)


def _rec_fwd(zrec, conv_w, conv_b, wa_bd, b_a, wx_bd, b_x, lam, carried, modes):
    rows = zrec.shape[0]
    tm = _row_tile(rows)

    def body(xr_ref, yr_ref, cw_ref, cb_ref, wa_ref, ba_ref, wx_ref, bx_ref, lam_ref, rec_ref, h_ref, kept_ref,
             xbuf, a_s, u_s, carry):
        i = pl.program_id(0)

        @pl.when(i == 0)
        def _():
            xbuf[0:SUBLANES, :] = jnp.zeros((SUBLANES, LRU_WIDTH), F32)
            carry[...] = jnp.zeros_like(carry)

        @pl.when(i > 0)
        def _():
            xbuf[0:SUBLANES, :] = xbuf[tm:tm + SUBLANES, :]

        xbuf[SUBLANES:SUBLANES + tm, :] = xr_ref[...]
        taps = _conv_taps(xbuf, tm)
        xc = cb_ref[...] + sum(cw_ref[j:j + 1, :] * taps[j] for j in range(CONV_WIDTH))
        r, ig, a, mult = _lru_gates(xc, wa_ref, ba_ref, wx_ref, bx_ref, lam_ref)
        for k, val in ((KEPT_XC, xc), (KEPT_A, a), (KEPT_MULT, mult), (KEPT_R, r), (KEPT_I, ig)):
            kept_ref[:, k * LRU_WIDTH:(k + 1) * LRU_WIDTH] = val
        grow = i * tm + lax.broadcasted_iota(jnp.int32, (tm, LRU_WIDTH), 0)
        a_s[...] = a
        u_s[...] = jnp.where(grow >= PAD_ROWS, mult * (ig * xc), 0.0)
        carry[0:1, :] = _scan_tile(a_s, u_s, h_ref, carry[0:1, :], tm)
        gel, _ = _gelu(yr_ref[...])
        rec_ref[...] = (gel * h_ref[...]).astype(BF16)

    vec = _full((1, LRU_WIDTH))
    bd = _full((2, LRU_HALF, LRU_HALF))
    return _hosting_call(
        body, "rec_fwd", rows // tm,
        [pl.BlockSpec((tm, LRU_WIDTH), lambda i: (i, 0)), pl.BlockSpec((tm, LRU_WIDTH), lambda i: (i, 1)),
         _full((CONV_WIDTH, LRU_WIDTH)), vec, bd, vec, bd, vec, vec],
        [pl.BlockSpec((tm, LRU_WIDTH), lambda i: (i, 0))] * 2 + [pl.BlockSpec((tm, N_KEPT * LRU_WIDTH), lambda i: (i, 0))],
        [jax.ShapeDtypeStruct((rows, LRU_WIDTH), BF16), jax.ShapeDtypeStruct((rows, LRU_WIDTH), F32),
         jax.ShapeDtypeStruct((rows, N_KEPT * LRU_WIDTH), F32)],
        [pltpu.VMEM((tm + SUBLANES, LRU_WIDTH), F32), pltpu.VMEM((tm, LRU_WIDTH), F32),
         pltpu.VMEM((tm, LRU_WIDTH), F32), pltpu.VMEM((SUBLANES, LRU_WIDTH), F32)],
        (zrec, zrec, conv_w, conv_b, wa_bd, b_a, wx_bd, b_x, lam), carried, modes)


FF_COLS = 1024
FF_HALF = FF_CHUNK // 2


def _hidden_at(d, half):
    return half * (D_FF // 2) + d * FF_HALF


def _mix_and_ffn_up(attn, rec, w_out, head, x, g2, g3, w1_halves, carried, modes):
    rows = attn.shape[0]
    tm = _row_tile(rows)
    steps = rows // tm

    def body(attn_ref, rec_ref, w_ref, head_ref, g2_ref, g3_ref, wa_ref, wb_ref, x_hbm,
             mix_ref, h1_ref, act_ref, u_ref, buf, sem):
        h0 = _h0_tile(head_ref, x_hbm, buf, sem, pl.program_id(0), steps, tm)
        mix = _mm(attn_ref[...], w_ref[0:ATTN_WIDTH, :]) + _mm(rec_ref[...], w_ref[ATTN_WIDTH:, :])
        y, _, _ = _rms_fwd(mix, g2_ref[...])
        mix_ref[...] = mix
        h1_ref[...] = h0 + y
        u, _, _ = _rms_fwd(h1_ref[...], g3_ref[...])
        u = u.astype(BF16)
        u_ref[...] = u
        for half, w1_ref in enumerate((wa_ref, wb_ref)):
            for d in range(N_DEV):
                c = _hidden_at(d, half)
                a1 = jnp.maximum(_mm(u, w1_ref[d]), 0.0)
                act_ref[:, c:c + FF_HALF] = (a1 * a1).astype(BF16)

    half_in = pl.BlockSpec((tm, ATTN_WIDTH), lambda i: (i, 0))
    wide = pl.BlockSpec((tm, D_MODEL), lambda i: (i, 0))
    return _hosting_call(
        body, "mix_and_ffn_up", steps,
        [half_in, half_in, _resident((D_MODEL, D_MODEL)), _full((BLOCK, D_MODEL)), _full((1, D_MODEL)), _full((1, D_MODEL))]
        + [_resident((N_DEV, D_MODEL, FF_HALF))] * 2 + [ANY_SPACE],
        [wide, wide, pl.BlockSpec((tm, D_FF), lambda i: (i, 0)), wide],
        [jax.ShapeDtypeStruct((rows, D_MODEL), F32)] * 2
        + [jax.ShapeDtypeStruct((rows, D_FF), BF16), jax.ShapeDtypeStruct((rows, D_MODEL), BF16)],
        _frame_scratch(tm), (attn, rec, w_out, head, g2, g3, *w1_halves, x), carried, modes)


def _ffn_down_loss(act, w2_halves, h1, target, g4, carried, modes):
    rows = h1.shape[0]
    tm = _row_tile(rows)
    steps = rows // tm
    kh = D_FF // 2

    def body(act_ref, wa_ref, wb_ref, h_ref, g_ref, t_hbm, dy_ref, df_ref, dg_ref, loss_ref, buf, sem):
        i = pl.program_id(0)
        slot = _frame_rows(t_hbm, buf, sem, i, steps, tm)

        @pl.when(i == 0)
        def _():
            dg_ref[...] = jnp.zeros_like(dg_ref)
            loss_ref[...] = jnp.zeros_like(loss_ref)
            buf[0, 0:BLOCK, :] = jnp.zeros((BLOCK, D_MODEL), F32)

        g = g_ref[...]
        f = _mm(act_ref[:, :kh], wa_ref[...]) + _mm(act_ref[:, kh:], wb_ref[...])
        y, fhat, rstd = _rms_fwd(f, g)
        grow = i * tm + lax.broadcasted_iota(jnp.int32, (tm, D_MODEL), 0)
        err = jnp.where(grow >= BLOCK, h_ref[...] + y - buf[slot], 0.0)
        loss_ref[...] += (0.5 / D_MODEL) * jnp.sum(err * err)
        dy = err * (1.0 / D_MODEL)
        df, dg = _rms_bwd(dy, fhat, rstd, g)
        dy_ref[...] = dy
        df_ref[...] = df.astype(BF16)
        dg_ref[...] += dg

    wide = pl.BlockSpec((tm, D_MODEL), lambda i: (i, 0))
    return _hosting_call(
        body, "ffn_down_loss", steps,
        [pl.BlockSpec((tm, D_FF), lambda i: (i, 0)), _resident((kh, D_MODEL)), _resident((kh, D_MODEL)), wide,
         _full((1, D_MODEL)), ANY_SPACE],
        [wide, wide, _full((1, D_MODEL)), _full((SUBLANES, LANES))],
        [jax.ShapeDtypeStruct((rows, D_MODEL), F32), jax.ShapeDtypeStruct((rows, D_MODEL), BF16),
         jax.ShapeDtypeStruct((1, D_MODEL), F32), jax.ShapeDtypeStruct((SUBLANES, LANES), F32)],
        _frame_scratch(tm), (act, *w2_halves, h1, g4, target), carried, modes)


def _ffn_bwd_act(df, w2t_halves, act, carried, modes):
    rows = df.shape[0]
    tm = _row_tile(rows)

    def body(df_ref, wa_ref, wb_ref, act_ref, da_ref):
        df_t = df_ref[...]
        for half, w_ref in enumerate((wa_ref, wb_ref)):
            for d in range(N_DEV):
                cols = slice(_hidden_at(d, half), _hidden_at(d, half) + FF_HALF)
                dact = _mm(df_t, w_ref[d])
                relu_a1, _ = _sqrt_pos(act_ref[:, cols].astype(F32))
                da_ref[:, cols] = (dact * (2.0 * relu_a1)).astype(BF16)

    hidden = pl.BlockSpec((tm, D_FF), lambda i: (i, 0))
    return _hosting_call(
        body, "ffn_bwd_act", rows // tm,
        [pl.BlockSpec((tm, D_MODEL), lambda i: (i, 0))] + [_resident((N_DEV, D_MODEL, FF_HALF))] * 2 + [hidden],
        [hidden],
        [jax.ShapeDtypeStruct((rows, D_FF), BF16)],
        [], (df, *w2t_halves, act), carried, modes)


def _ffn_bwd_x(da, w1t_halves, h1, dy, g3, carried, modes):
    rows = h1.shape[0]
    tm = _row_tile(rows)
    kh = D_FF // 2

    def body(da_ref, wa_ref, wb_ref, h_ref, dy_ref, g_ref, dh_ref, dg_ref):
        @pl.when(pl.program_id(0) == 0)
        def _():
            dg_ref[...] = jnp.zeros_like(dg_ref)

        g = g_ref[...]
        _, xhat, rstd = _rms_fwd(h_ref[...], g)
        du = _mm(da_ref[:, :kh], wa_ref[...]) + _mm(da_ref[:, kh:], wb_ref[...])
        dx, dg = _rms_bwd(du, xhat, rstd, g)
        dh_ref[...] = dy_ref[...] + dx
        dg_ref[...] += dg

    wide = pl.BlockSpec((tm, D_MODEL), lambda i: (i, 0))
    return _hosting_call(
        body, "ffn_bwd_x", rows // tm,
        [pl.BlockSpec((tm, D_FF), lambda i: (i, 0)), _resident((kh, D_MODEL)), _resident((kh, D_MODEL)), wide, wide,
         _full((1, D_MODEL))],
        [wide, _full((1, D_MODEL))],
        [jax.ShapeDtypeStruct((rows, D_MODEL), F32), jax.ShapeDtypeStruct((1, D_MODEL), F32)],
        [], (da, *w1t_halves, h1, dy, g3), carried, modes)


def _ffn_bwd_weights(u2, da, act, df, carried, modes):
    rows = u2.shape[0]
    tb = _big_tile(rows)
    steps = rows // tb
    per = FF_COLS // FF_HALF

    def body(u_ref, da_ref, act_ref, df_ref, dw1_ref, dw2_ref, acc1, acc2):
        i = pl.program_id(1)

        @pl.when(i == 0)
        def _():
            acc1[...] = jnp.zeros_like(acc1)
            acc2[...] = jnp.zeros_like(acc2)

        acc1[...] += _mm_tn(u_ref[...], da_ref[...])
        acc2[...] += _mm_tn(act_ref[...], df_ref[...])

        @pl.when(i == steps - 1)
        def _():
            for p in range(per):
                c = p * FF_HALF
                dw1_ref[p] = acc1[:, c:c + FF_HALF].astype(BF16)
                dw2_ref[p] = acc2[c:c + FF_HALF, :].astype(BF16)

    wide = pl.BlockSpec((tb, D_MODEL), lambda j, i: (i, 0))
    chunk = pl.BlockSpec((tb, FF_COLS), lambda j, i: (i, j))
    return _hosting_call(
        body, "ffn_bwd_weights", (D_FF // FF_COLS, steps),
        [wide, chunk, chunk, wide],
        [pl.BlockSpec((None, per, D_MODEL, FF_HALF), lambda j, i: (j // 2, j % 2, 0, 0)),
         pl.BlockSpec((per, FF_HALF, D_MODEL), lambda j, i: (j % 2, j // 2, 0))],
        [jax.ShapeDtypeStruct((2, N_DEV, D_MODEL, FF_HALF), BF16), jax.ShapeDtypeStruct((N_DEV, FF_CHUNK, D_MODEL), BF16)],
        [pltpu.VMEM((D_MODEL, FF_COLS), F32), pltpu.VMEM((FF_COLS, D_MODEL), F32)],
        (u2, da, act, df), carried, modes)


def _out_proj_bwd(dh1, mix, g2, w_out_t, attn, rec, carried, modes):
    rows = dh1.shape[0]
    tm = _row_tile(rows)
    steps = rows // tm

    def body(dh_ref, mix_ref, g_ref, w_ref, attn_ref, rec_ref, dattn_ref, drec_ref, dw_ref, dg_ref, acc):
        i = pl.program_id(0)

        @pl.when(i == 0)
        def _():
            acc[...] = jnp.zeros_like(acc)
            dg_ref[...] = jnp.zeros_like(dg_ref)

        g = g_ref[...]
        _, xhat, rstd = _rms_fwd(mix_ref[...], g)
        dmix, dg = _rms_bwd(dh_ref[...], xhat, rstd, g)
        dmix = dmix.astype(BF16)
        dg_ref[...] += dg
        din = _mm(dmix, w_ref[...])
        dattn_ref[...] = din[:, :ATTN_WIDTH].astype(BF16)
        drec_ref[...] = din[:, ATTN_WIDTH:]
        acc[0:ATTN_WIDTH, :] += _mm_tn(attn_ref[...], dmix)
        acc[ATTN_WIDTH:, :] += _mm_tn(rec_ref[...], dmix)

        @pl.when(i == steps - 1)
        def _():
            dw_ref[...] = acc[...].astype(BF16)

    half = pl.BlockSpec((tm, ATTN_WIDTH), lambda i: (i, 0))
    wide = pl.BlockSpec((tm, D_MODEL), lambda i: (i, 0))
    return _hosting_call(
        body, "out_proj_bwd", steps,
        [wide, wide, _full((1, D_MODEL)), _resident((D_MODEL, D_MODEL)), half, half],
        [half, half, _full((D_MODEL, D_MODEL)), _full((1, D_MODEL))],
        [jax.ShapeDtypeStruct((rows, ATTN_WIDTH), BF16), jax.ShapeDtypeStruct((rows, LRU_WIDTH), F32),
         jax.ShapeDtypeStruct((D_MODEL, D_MODEL), BF16), jax.ShapeDtypeStruct((1, D_MODEL), F32)],
        [pltpu.VMEM((D_MODEL, D_MODEL), F32)],
        (dh1, mix, g2, w_out_t, attn, rec), carried, modes)


def _attn_bwd(qkv, dattn, sinks, bias, carried, modes):
    rows = qkv.shape[0]
    tm = _row_tile(rows)
    nbt, nt = tm // BLOCK, rows // tm

    def body(sink_ref, bias_ref, do_ref, q_ref, kp_ref, kc_ref, vp_ref, vc_ref, dq_ref, dkv_ref, dsink_ref, dk_c, dv_c):
        i = pl.program_id(0)

        @pl.when(i == 0)
        def _():
            dk_c[...] = jnp.zeros_like(dk_c)
            dv_c[...] = jnp.zeros_like(dv_c)
            dsink_ref[...] = jnp.zeros_like(dsink_ref)

        @pl.when(i < nt)
        def _():
            dk_late, dv_late = dk_c[...], dv_c[...]
            dsink_rows = [jnp.zeros((1, LANES), F32)] * ATTN_HEADS
            for b in range(nbt):
                blk = slice(b * BLOCK, (b + 1) * BLOCK)
                bias_t = _bias_of_block(bias_ref, i * nbt + b)
                dq_parts, dk_parts, dv_parts = [], [], []
                for kv in range(KV_HEADS):
                    k2 = _keys_of_block(kp_ref, kc_ref, b, kv)
                    v2 = _keys_of_block(vp_ref, vc_ref, b, kv)
                    q4 = _heads(q_ref, blk, kv * GQA_GROUP, GQA_GROUP)
                    do4 = _heads(do_ref, blk, kv * GQA_GROUP, GQA_GROUP)
                    pn, psink = _attn_probs(k2, q4, bias_t, _sink_row(sink_ref, kv))
                    dpn = _mm_nt(v2, do4)
                    delta = jnp.sum(pn * dpn, axis=0, keepdims=True)
                    ds = ((pn * (dpn - delta)) * (HEAD_DIM ** -0.5)).astype(BF16)
                    dqt = _mm_tn(k2, ds)
                    dq_parts += [dqt[:, g * BLOCK:(g + 1) * BLOCK] for g in range(GQA_GROUP)]
                    dk_parts.append(_mm(ds, q4))
                    dv_parts.append(_mm(pn.astype(BF16), do4))
                    sd = psink * delta
                    for g in range(GQA_GROUP):
                        h = kv * GQA_GROUP + g
                        dsink_rows[h] = dsink_rows[h] - jnp.sum(sd[:, g * BLOCK:(g + 1) * BLOCK])
                dq_ref[blk, :] = _from_head_major(dq_parts).astype(BF16)
                dk2 = jnp.concatenate(dk_parts, axis=1)
                dv2 = jnp.concatenate(dv_parts, axis=1)
                dkv_ref[blk, 0:KV_WIDTH] = (dk_late + dk2[0:BLOCK]).astype(BF16)
                dkv_ref[blk, KV_WIDTH:] = (dv_late + dv2[0:BLOCK]).astype(BF16)
                dk_late, dv_late = dk2[BLOCK:], dv2[BLOCK:]
            dk_c[...] = dk_late
            dv_c[...] = dv_late
            dsink_ref[...] += jnp.concatenate(dsink_rows, axis=0)

        @pl.when(i == nt)
        def _():
            dkv_ref[...] = jnp.zeros_like(dkv_ref)
            dkv_ref[0:BLOCK, 0:KV_WIDTH] = dk_c[...].astype(BF16)
            dkv_ref[0:BLOCK, KV_WIDTH:] = dv_c[...].astype(BF16)

    tile_of = lambda i: jnp.minimum(i, nt - 1)
    tile = pl.BlockSpec((tm, ATTN_WIDTH), lambda i: (tile_of(i), 0))
    return _hosting_call(
        body, "attn_bwd", nt + 1,
        [pl.BlockSpec(memory_space=pltpu.SMEM), _resident((N_BIAS, 2 * BLOCK, GQA_GROUP * BLOCK)), tile]
        + _attn_specs(tm, tile_of),
        [tile, pl.BlockSpec((tm, 2 * KV_WIDTH), lambda i: (i, 0)), _full((ATTN_HEADS, LANES))],
        [jax.ShapeDtypeStruct((rows, ATTN_WIDTH), BF16), jax.ShapeDtypeStruct((rows + tm, 2 * KV_WIDTH), BF16),
         jax.ShapeDtypeStruct((ATTN_HEADS, LANES), F32)],
        [pltpu.VMEM((BLOCK, KV_WIDTH), F32), pltpu.VMEM((BLOCK, KV_WIDTH), F32)],
        (sinks, bias, dattn, qkv, qkv, qkv, qkv, qkv), carried, modes)


ROW_CONV_B, ROW_B_A, ROW_B_X, ROW_LAMBDA = 4, 5, 6, 7


def _rec_bwd(drec, zrec, h, kept, conv_w, wa_bd, wx_bd, lam, carried, modes):
    rows = zrec.shape[0]
    tm = _rec_tile(rows)
    nt = rows // tm
    per = tm // SUBLANES

    def body(drec_ref, xr_ref, yr_ref, h_ref, xc_ref, a_ref, mult_ref, r_ref, ig_ref, hhalo_ref, cw_ref, wa_ref, wx_ref,
             lam_ref, drz_ref, small_ref, dwa_ref, dwx_ref, hbuf, dbuf, dgr_s, dgi_s, dyr_s, carry):
        s = pl.program_id(0)
        i = nt - 1 - s

        @pl.when(s == 0)
        def _():
            small_ref[...] = jnp.zeros_like(small_ref)
            dwa_ref[...] = jnp.zeros_like(dwa_ref)
            dwx_ref[...] = jnp.zeros_like(dwx_ref)
            carry[...] = jnp.zeros_like(carry)
            dbuf[tm:tm + SUBLANES, :] = jnp.zeros((SUBLANES, LRU_WIDTH), F32)

        hbuf[0:SUBLANES, :] = jnp.where(i == 0, 0.0, hhalo_ref[...])
        hbuf[SUBLANES:SUBLANES + tm, :] = h_ref[...]

        row = lax.broadcasted_iota(jnp.int32, (SUBLANES, LRU_WIDTH), 0)
        log_a_scale = (-LRU_C) * _softplus(-lam_ref[...])
        zeros = jnp.zeros((SUBLANES, LRU_WIDTH), F32)

        def group(k, state):
            g_later, a_later, sum_dgr, sum_dgi, sum_lam = state
            o = pl.multiple_of((per - 1 - k) * SUBLANES, SUBLANES)
            rows8 = pl.ds(o, SUBLANES)
            yr, drec_t, h_t, a = yr_ref[rows8, :], drec_ref[rows8, :], h_ref[rows8, :], a_ref[rows8, :]
            gel, t = _gelu(yr)
            dyr_s[rows8, :] = drec_t * h_t * _gelu_grad(yr, t)
            u = drec_t * gel
            coef = jnp.where(row == SUBLANES - 1, a_later, pltpu.roll(a, SUBLANES - 1, 0))
            for sft in (1, 2, 4):
                keep = row < SUBLANES - sft
                u = jnp.where(keep, coef * pltpu.roll(u, SUBLANES - sft, 0) + u, u)
                coef = jnp.where(keep, coef * pltpu.roll(coef, SUBLANES - sft, 0), coef)
            g = coef * g_later + u
            du = jnp.where(i * tm + o + row >= PAD_ROWS, g, 0.0)
            h_before = jnp.where(row == 0, hbuf[rows8, :][SUBLANES - 1:SUBLANES, :], pltpu.roll(h_t, 1, 0))
            xc, mult, r, ig = xc_ref[rows8, :], mult_ref[rows8, :], r_ref[rows8, :], ig_ref[rows8, :]
            dbuf[rows8, :] = du * (mult * ig)
            dgi = (du * (mult * xc)) * (ig * (1.0 - ig))
            dgi_s[rows8, :] = dgi
            dlog_a = (g * h_before) * a - (du * (ig * xc)) * (a * a * pl.reciprocal(mult, approx=True))
            dgr = (dlog_a * log_a_scale) * (r * (1.0 - r))
            dgr_s[rows8, :] = dgr
            return g[0:1, :], a[0:1, :], sum_dgr + dgr, sum_dgi + dgi, sum_lam + dlog_a * r

        state = lax.fori_loop(0, per, group, (carry[0:1, :], carry[1:2, :], zeros, zeros, zeros))
        carry[0:1, :], carry[1:2, :] = state[0], state[1]
        sum_dgr, sum_dgi, sum_lam = (jnp.sum(v, axis=0, keepdims=True) for v in state[2:])
        dlam = sum_lam * (LRU_C * _sigmoid(-lam_ref[...]))

        dgr_b = [dgr_s[:, hh * LRU_HALF:(hh + 1) * LRU_HALF].astype(BF16) for hh in range(2)]
        dgi_b = [dgi_s[:, hh * LRU_HALF:(hh + 1) * LRU_HALF].astype(BF16) for hh in range(2)]
        halves = _lru_halves(xc_ref[...])
        for hh in range(2):
            dwa_ref[hh] += _mm_tn(halves[hh], dgr_b[hh])
            dwx_ref[hh] += _mm_tn(halves[hh], dgi_b[hh])
        dxc = dbuf[0:tm, :] + jnp.concatenate(
            [_mm_nt(dgr_b[hh], wa_ref[hh]) + _mm_nt(dgi_b[hh], wx_ref[hh]) for hh in range(2)], axis=1)

        dbuf[0:tm, :] = dxc
        sum_dxc = jnp.sum(dxc, axis=0, keepdims=True)
        ahead = [dbuf[pl.ds(CONV_WIDTH - 1 - j, tm), :] for j in range(CONV_WIDTH)]
        drz_ref[:, 0:LRU_WIDTH] = sum(cw_ref[j:j + 1, :] * ahead[j] for j in range(CONV_WIDTH)).astype(BF16)
        drz_ref[:, LRU_WIDTH:] = dyr_s[...].astype(BF16)
        upd = [jnp.sum(xr_ref[...] * ahead[j], axis=0, keepdims=True) for j in range(CONV_WIDTH)]
        dbuf[tm:tm + SUBLANES, :] = dbuf[0:SUBLANES, :]
        small_ref[...] += jnp.concatenate(upd + [sum_dxc, sum_dgr, sum_dgi, dlam], axis=0)

    rev = lambda s: nt - 1 - s
    halo = lambda s: jnp.maximum(rev(s) * per - 1, 0)
    cols = lambda k: pl.BlockSpec((tm, LRU_WIDTH), lambda s: (rev(s), k))
    halo0 = pl.BlockSpec((SUBLANES, LRU_WIDTH), lambda s: (halo(s), 0))
    bd = _full((2, LRU_HALF, LRU_HALF))
    big = pltpu.VMEM((tm + SUBLANES, LRU_WIDTH), F32)
    tile = pltpu.VMEM((tm, LRU_WIDTH), F32)
    kept_cols = [cols(k) for k in (KEPT_XC, KEPT_A, KEPT_MULT, KEPT_R, KEPT_I)]
    return _hosting_call(
        body, "rec_bwd", nt,
        [cols(0), cols(0), cols(1), cols(0)] + kept_cols
        + [halo0, _full((CONV_WIDTH, LRU_WIDTH)), bd, bd, _full((1, LRU_WIDTH))],
        [pl.BlockSpec((tm, 2 * LRU_WIDTH), lambda s: (rev(s), 0)), _full((SUBLANES, LRU_WIDTH)), bd, bd],
        [jax.ShapeDtypeStruct((rows, 2 * LRU_WIDTH), BF16), jax.ShapeDtypeStruct((SUBLANES, LRU_WIDTH), F32),
         jax.ShapeDtypeStruct((2, LRU_HALF, LRU_HALF), F32), jax.ShapeDtypeStruct((2, LRU_HALF, LRU_HALF), F32)],
        [big, big, tile, tile, tile, pltpu.VMEM((SUBLANES, LRU_WIDTH), F32)],
        (drec, zrec, zrec, h) + (kept,) * N_KEPT + (h, conv_w, wa_bd, wx_bd, lam), carried, modes)


DZ_CUTS = (0, ATTN_WIDTH, QKV_WIDTH, IN_WIDTH)


def _dz_specs(tm):
    return [pl.BlockSpec((tm, DZ_CUTS[p + 1] - DZ_CUTS[p]), lambda i: (i, 0)) for p in range(3)]


def _in_proj_bwd_x(head, x, g1, dh1, dq, dkv, drz, w_in_t, carried, modes):
    rows = dh1.shape[0]
    tm = _row_tile(rows)
    steps = rows // tm

    def body(head_ref, g_ref, dh1_ref, dq_ref, dkv_ref, drz_ref, w_ref, x_hbm, dh0_ref, dg_ref, buf, sem):
        i = pl.program_id(0)
        h0 = _h0_tile(head_ref, x_hbm, buf, sem, i, steps, tm)

        @pl.when(i == 0)
        def _():
            dg_ref[...] = jnp.zeros_like(dg_ref)

        g = g_ref[...]
        _, xhat, rstd = _rms_fwd(h0, g)
        parts = (dq_ref[...], dkv_ref[...], drz_ref[...])
        du = sum(_mm(parts[p], w_ref[DZ_CUTS[p]:DZ_CUTS[p + 1], :]) for p in range(3))
        dx, dg = _rms_bwd(du, xhat, rstd, g)
        dh0_ref[...] = dh1_ref[...] + dx
        dg_ref[...] += dg

    wide = pl.BlockSpec((tm, D_MODEL), lambda i: (i, 0))
    return _hosting_call(
        body, "in_proj_bwd_x", steps,
        [_full((BLOCK, D_MODEL)), _full((1, D_MODEL)), wide] + _dz_specs(tm) + [_resident((IN_WIDTH, D_MODEL)), ANY_SPACE],
        [wide, _full((1, D_MODEL))],
        [jax.ShapeDtypeStruct((rows, D_MODEL), F32), jax.ShapeDtypeStruct((1, D_MODEL), F32)],
        _frame_scratch(tm), (head, g1, dh1, dq, dkv, drz, w_in_t, x), carried, modes)


def _in_proj_bwd_w(u1, dq, dkv, drz, carried, modes):
    rows = u1.shape[0]
    tb = _big_tile(rows)
    steps = rows // tb

    def body(u_ref, dq_ref, dkv_ref, drz_ref, dw_ref, acc):
        i = pl.program_id(0)

        @pl.when(i == 0)
        def _():
            acc[...] = jnp.zeros_like(acc)

        u = u_ref[...]
        for p, ref in enumerate((dq_ref, dkv_ref, drz_ref)):
            acc[:, DZ_CUTS[p]:DZ_CUTS[p + 1]] += _mm_tn(u, ref[...])

        @pl.when(i == steps - 1)
        def _():
            dw_ref[...] = acc[...].astype(BF16)

    return _hosting_call(
        body, "in_proj_bwd_w", steps,
        [pl.BlockSpec((tb, D_MODEL), lambda i: (i, 0))] + _dz_specs(tb),
        [_full((D_MODEL, IN_WIDTH))],
        [jax.ShapeDtypeStruct((D_MODEL, IN_WIDTH), BF16)],
        [pltpu.VMEM((D_MODEL, IN_WIDTH), F32)], (u1, dq, dkv, drz), carried, modes)


def _adamw_math(w, m, v, g):
    nm = ADAM_B1 * m + (1.0 - ADAM_B1) * g
    nv = ADAM_B2 * v + (1.0 - ADAM_B2) * (g * g)
    m_hat = nm / (1.0 - ADAM_B1 ** ADAM_STEP)
    v_hat = nv / (1.0 - ADAM_B2 ** ADAM_STEP)
    return (-ADAM_LR) * (m_hat / (jnp.sqrt(v_hat) + ADAM_EPS) + ADAM_WD * w), nm, nv


SMALL_NAMES = ("conv_b", "b_a", "b_x", "lru_lambda", "attn_sinks", "g_post_mix", "g_pre_ffn", "g_post_ffn")
PACK_WIDTH = 1024


def _pack_rows(vals):
    assert len(SMALL_NAMES) == SUBLANES
    row = lax.broadcasted_iota(jnp.int32, (SUBLANES, PACK_WIDTH), 0)
    tile = jnp.zeros((SUBLANES, PACK_WIDTH), F32)
    for k, name in enumerate(SMALL_NAMES):
        a = vals[name].reshape(1, -1)
        tile = jnp.where(row == k, jnp.pad(a, ((0, 0), (0, PACK_WIDTH - a.shape[1]))), tile)
    return tile


def _adamw_small(weights, mom_m, mom_v, parts, loss_parts):
    n = len(SMALL_NAMES)
    views = [(1, weights[name].size) for name in SMALL_NAMES]

    def body(*refs):
        w_refs, m_refs, v_refs = refs[:n], refs[n:2 * n], refs[2 * n:3 * n]
        p_ref, l_ref, loss_ref = refs[3 * n], refs[3 * n + 1], refs[3 * n + 2]
        outs = refs[3 * n + 3:]
        for k, (_, c) in enumerate(views):
            g = p_ref[0, k:k + 1, 0:c]
            for s in range(1, N_DEV):
                g = g + p_ref[s, k:k + 1, 0:c]
            g_ref, d_ref, nm_ref, nv_ref = outs[4 * k:4 * k + 4]
            g_ref[...] = g
            d_ref[...], nm_ref[...], nv_ref[...] = _adamw_math(w_refs[k][...], m_refs[k][...], v_refs[k][...], g)
        total = l_ref[0]
        for s in range(1, N_DEV):
            total = total + l_ref[s]
        loss_ref[...] = total

    args = [src[name].reshape(view) for src in (weights, mom_m, mom_v) for name, view in zip(SMALL_NAMES, views)]
    res = pl.pallas_call(
        body, name="adamw_small",
        out_shape=[jax.ShapeDtypeStruct(loss_parts.shape[1:], F32)]
                  + [jax.ShapeDtypeStruct(view, F32) for view in views for _ in range(4)],
        compiler_params=pltpu.CompilerParams(vmem_limit_bytes=VMEM_LIMIT),
    )(*args, parts, loss_parts)
    out = {name: tuple(t.reshape(weights[name].shape) for t in res[1 + 4 * k:5 + 4 * k]) for k, name in enumerate(SMALL_NAMES)}
    return res[0], out


def _adamw(w, m, v, parts, name):
    rows, cols = w.shape
    tr = next((t for t in (256, 128) if rows % t == 0), rows)
    parts = parts if isinstance(parts, (list, tuple)) else [parts]

    def body(w_ref, m_ref, v_ref, *refs):
        p_refs, (g_ref, d_ref, nm_ref, nv_ref) = refs[:len(parts)], refs[len(parts):]

        def total(p_ref):
            g = p_ref[0].astype(F32)
            for s in range(1, N_DEV):
                g = g + p_ref[s].astype(F32)
            return g

        g = jnp.concatenate([total(p_ref) for p_ref in p_refs], axis=1) if len(parts) > 1 else total(p_refs[0])
        g_ref[...] = g
        d_ref[...], nm_ref[...], nv_ref[...] = _adamw_math(w_ref[...], m_ref[...], v_ref[...], g)

    blk = pl.BlockSpec((tr, cols), lambda i: (i, 0))
    return pl.pallas_call(
        body, name=name, grid=(rows // tr,),
        in_specs=[blk, blk, blk] + [pl.BlockSpec((N_DEV, tr, p.shape[2]), lambda i: (0, i, 0)) for p in parts],
        out_specs=[blk] * 4,
        out_shape=[jax.ShapeDtypeStruct((rows, cols), F32)] * 4,
        compiler_params=_params(("parallel",)),
    )(w, m, v, *parts)


def _cols_from_shards(g):
    return jnp.transpose(g, (1, 0, 2)).reshape(g.shape[1], N_DEV * g.shape[2])


def _cols_to_shards(a):
    r, c = a.shape
    return jnp.transpose(a.reshape(r, N_DEV, c // N_DEV), (1, 0, 2))


def _block_diag(w):
    per = LRU_HALF // LRU_BLOCK
    w = w.reshape(2, per, LRU_BLOCK, LRU_BLOCK)
    eye = jnp.eye(per, dtype=w.dtype)
    return (w[:, :, :, None, :] * eye[None, :, None, :, None]).reshape(2, LRU_HALF, LRU_HALF)


def _block_diag_extract(t):
    per = LRU_HALF // LRU_BLOCK
    t = t.reshape(2, per, LRU_BLOCK, per, LRU_BLOCK)
    return jnp.stack([t[:, b, :, b, :] for b in range(per)], axis=1).reshape(LRU_BLOCKS, LRU_BLOCK, LRU_BLOCK)


def kernel(x, meta_tokens, g_pre_mix, w_in, conv_w, conv_b, w_a, b_a, w_x, b_x, lru_lambda, attn_sinks, w_out, g_post_mix, g_pre_ffn, w_ff1, w_ff2, g_post_ffn, loss_target, m_meta_tokens, m_g_pre_mix, m_w_in, m_conv_w, m_conv_b, m_w_a, m_b_a, m_w_x, m_b_x, m_lru_lambda, m_attn_sinks, m_w_out, m_g_post_mix, m_g_pre_ffn, m_w_ff1, m_w_ff2, m_g_post_ffn, v_meta_tokens, v_g_pre_mix, v_w_in, v_conv_w, v_conv_b, v_w_a, v_b_a, v_w_x, v_b_x, v_lru_lambda, v_attn_sinks, v_w_out, v_g_post_mix, v_g_pre_ffn, v_w_ff1, v_w_ff2, v_g_post_ffn):
    weights = dict(meta_tokens=meta_tokens, g_pre_mix=g_pre_mix, w_in=w_in, conv_w=conv_w, conv_b=conv_b, w_a=w_a,
                   b_a=b_a, w_x=w_x, b_x=b_x, lru_lambda=lru_lambda, attn_sinks=attn_sinks, w_out=w_out,
                   g_post_mix=g_post_mix, g_pre_ffn=g_pre_ffn, w_ff1=w_ff1, w_ff2=w_ff2, g_post_ffn=g_post_ffn)
    mom_m = dict(meta_tokens=m_meta_tokens, g_pre_mix=m_g_pre_mix, w_in=m_w_in, conv_w=m_conv_w, conv_b=m_conv_b,
                 w_a=m_w_a, b_a=m_b_a, w_x=m_w_x, b_x=m_b_x, lru_lambda=m_lru_lambda, attn_sinks=m_attn_sinks,
                 w_out=m_w_out, g_post_mix=m_g_post_mix, g_pre_ffn=m_g_pre_ffn, w_ff1=m_w_ff1, w_ff2=m_w_ff2,
                 g_post_ffn=m_g_post_ffn)
    mom_v = dict(meta_tokens=v_meta_tokens, g_pre_mix=v_g_pre_mix, w_in=v_w_in, conv_w=v_conv_w, conv_b=v_conv_b,
                 w_a=v_w_a, b_a=v_b_a, w_x=v_w_x, b_x=v_b_x, lru_lambda=v_lru_lambda, attn_sinks=v_attn_sinks,
                 w_out=v_w_out, g_post_mix=v_g_post_mix, g_pre_ffn=v_g_pre_ffn, w_ff1=v_w_ff1, w_ff2=v_w_ff2,
                 g_post_ffn=v_g_post_ffn)
    order = list(weights)

    (g_win, g_meta, g_cw) = _gather_two_level([w_in[0].astype(BF16), meta_tokens, conv_w[0]], "gather_first")
    w_in_full = _cols_from_shards(g_win)
    meta_full = _cols_from_shards(g_meta)
    conv_w_full = _cols_from_shards(g_cw)

    head = jnp.concatenate([jnp.zeros((PAD_ROWS, D_MODEL), F32), meta_full], axis=0)
    wa_bd = _block_diag(w_a[0]).astype(BF16)
    wx_bd = _block_diag(w_x[0]).astype(BF16)
    bias = _attn_bias()

    w1_shard = w_ff1[0].astype(BF16)
    (qkv, zrec, u1), (g_wout,) = _in_proj_fwd(head, x[0], g_pre_mix, w_in_full, [w_out[0].astype(BF16)], ["gather"])
    (attn,), (w1a,) = _attn_fwd(qkv, attn_sinks, bias, [w1_shard[:, :FF_HALF]], ["gather"])
    (rec, h_lru, kept), (w1b,) = _rec_fwd(zrec, conv_w_full, conv_b, wa_bd, b_a, wx_bd, b_x, lru_lambda,
                                         [w1_shard[:, FF_HALF:]], ["gather"])
    w_out_full = g_wout.reshape(D_MODEL, D_MODEL)
    w2_shard = w_ff2[0].astype(BF16)
    (mix, h1, act, u2), (w2a, w2b) = _mix_and_ffn_up(
        attn, rec, w_out_full, head, x[0], g_post_mix, g_pre_ffn, (w1a, w1b),
        [w2_shard[:FF_HALF], w2_shard[FF_HALF:]], ["gather"] * 2)
    w2_halves = [w.reshape(D_FF // 2, D_MODEL) for w in (w2a, w2b)]
    (dy, df, dg_post_ffn, loss_acc), w2t_halves = _ffn_down_loss(
        act, w2_halves, h1, loss_target[0], g_post_ffn, [w2_shard[:FF_HALF].T, w2_shard[FF_HALF:].T], ["gather"] * 2)

    (da1,), (w1ta,) = _ffn_bwd_act(df, w2t_halves, act, [w1_shard[:, :FF_HALF].T], ["gather"])
    (dw1h, dw2g), (w1tb,) = _ffn_bwd_weights(u2, da1, act, df, [w1_shard[:, FF_HALF:].T], ["gather"])
    w1t_halves = [w.reshape(D_FF // 2, D_MODEL) for w in (w1ta, w1tb)]
    (dh1, dg_pre_ffn), (p_w1a,) = _ffn_bwd_x(da1, w1t_halves, h1, dy, g_pre_ffn, [dw1h[0]], ["scatter"])
    (dattn, drec, dw_out, dg_post_mix), (p_w1b,) = _out_proj_bwd(dh1, mix, g_post_mix, w_out_full.T, attn, rec,
                                                                [dw1h[1]], ["scatter"])
    (dq, dkv_late, dsinks), (p_w2,) = _attn_bwd(qkv, dattn, attn_sinks, bias, [dw2g], ["scatter"])
    dkv = dkv_late[BLOCK:BLOCK + qkv.shape[0]]
    (drz, rec_small, dwa_bd, dwx_bd), (p_wout,) = _rec_bwd(
        drec, zrec, h_lru, kept, conv_w_full, wa_bd, wx_bd, lru_lambda,
        [dw_out.reshape(N_DEV, D_MODEL // N_DEV, D_MODEL)], ["scatter"])
    small_grads = dict(
        conv_b=rec_small[ROW_CONV_B], b_a=rec_small[ROW_B_A], b_x=rec_small[ROW_B_X], lru_lambda=rec_small[ROW_LAMBDA],
        attn_sinks=dsinks[:, 0], g_post_mix=dg_post_mix, g_pre_ffn=dg_pre_ffn, g_post_ffn=dg_post_ffn)
    gate_rows = (LRU_BLOCKS * LRU_BLOCK, LRU_BLOCK)
    gate_dense = (LRU_BLOCKS * LRU_BLOCK * LRU_BLOCK // PACK_WIDTH, PACK_WIDTH)
    (dw_in,), (p_cw, p_small, p_wa, p_wx) = _in_proj_bwd_w(
        u1, dq, dkv, drz,
        [_cols_to_shards(rec_small[0:CONV_WIDTH]), _pack_rows(small_grads),
         _block_diag_extract(dwa_bd).reshape(gate_dense), _block_diag_extract(dwx_bd).reshape(gate_dense)],
        ["scatter", "gather", "gather", "gather"])
    p_wa, p_wx = (p.reshape((N_DEV,) + gate_rows) for p in (p_wa, p_wx))
    (dh0, dg_pre_mix), (p_win,) = _in_proj_bwd_x(
        head, x[0], g_pre_mix, dh1, dq, dkv, drz, w_in_full.T, [_cols_to_shards(dw_in)], ["scatter"])
    p_meta, p_gpm, p_loss = _exchange([_cols_to_shards(dh0[PAD_ROWS:BLOCK]), dg_pre_mix, loss_acc],
                                      ["scatter", "gather", "gather"], "exchange_last")

    res = {}
    res["g_pre_mix"] = _adamw(g_pre_mix, m_g_pre_mix, v_g_pre_mix, p_gpm, "adamw_g_pre_mix")
    res["w_in"] = _adamw(w_in[0], m_w_in[0], v_w_in[0], p_win, "adamw_w_in")
    res["w_out"] = _adamw(w_out[0], m_w_out[0], v_w_out[0], p_wout, "adamw_w_out")
    res["w_ff1"] = _adamw(w_ff1[0], m_w_ff1[0], v_w_ff1[0], [p_w1a, p_w1b], "adamw_w_ff1")
    res["w_ff2"] = _adamw(w_ff2[0], m_w_ff2[0], v_w_ff2[0], p_w2, "adamw_w_ff2")
    res["meta_tokens"] = _adamw(meta_tokens, m_meta_tokens, v_meta_tokens, p_meta, "adamw_meta")
    res["conv_w"] = _adamw(conv_w[0], m_conv_w[0], v_conv_w[0], p_cw, "adamw_conv_w")
    for name in ("w_in", "w_out", "w_ff1", "w_ff2", "conv_w"):
        res[name] = tuple(t[None] for t in res[name])
    for name, parts in (("w_a", p_wa), ("w_x", p_wx)):
        gate = _adamw(*(src[name].reshape(gate_rows) for src in (weights, mom_m, mom_v)), parts, "adamw_" + name)
        res[name] = tuple(t.reshape(weights[name].shape) for t in gate)
    loss_total, small = _adamw_small(weights, mom_m, mom_v, p_small, p_loss)
    res.update(small)

    grad_x = dh0[BLOCK:][None]
    outs = [loss_total[0, 0], grad_x]
    for k in range(4):
        outs += [res[name][k] for name in order]
    return tuple(outs)
```

```python
import jax
import jax.numpy as jnp
import numpy as np
from jax import lax
from jax.experimental import pallas as pl
from jax.experimental.pallas import tpu as pltpu

F32 = jnp.float32
BF16 = jnp.bfloat16

D_MODEL = 1024
N_META = 16
HEAD_DIM = 64
ATTN_HEADS = 8
KV_HEADS = 2
GQA_GROUP = ATTN_HEADS // KV_HEADS
ATTN_WIDTH = ATTN_HEADS * HEAD_DIM
KV_WIDTH = KV_HEADS * HEAD_DIM
QKV_WIDTH = ATTN_WIDTH + 2 * KV_WIDTH
LRU_WIDTH = 512
LRU_BLOCKS = 8
LRU_BLOCK = 64
LRU_HALF = 256
LRU_C = 8.0
CONV_WIDTH = 4
BLOCK = 128
PAD_ROWS = BLOCK - N_META
IN_WIDTH = QKV_WIDTH + 2 * LRU_WIDTH
D_FF = 4096
EPS = 1e-6
NEG = -1e30
N_DEV = 8
FF_CHUNK = D_FF // N_DEV
SUBLANES = 8
LANES = 128

ADAM_LR = 0.001
ADAM_B1 = 0.9
ADAM_B2 = 0.999
ADAM_EPS = 1e-08
ADAM_WD = 0.01
ADAM_STEP = 10

VMEM_LIMIT = 56 * 1024 * 1024


def _row_tile(rows):
    for t in (640, 512, 256, 128):
        if rows % t == 0:
            return t
    raise ValueError(rows)


def _big_tile(rows):
    for t in (1664, 1024, 512, 256, 128):
        if rows % t == 0:
            return t
    raise ValueError(rows)


def _rec_tile(rows):
    for t in (640, 256, 128):
        if rows % t == 0:
            return t
    raise ValueError(rows)


def _params(semantics):
    return pltpu.CompilerParams(dimension_semantics=semantics, vmem_limit_bytes=VMEM_LIMIT)


def _mm(a, b):
    return lax.dot_general(a, b, (((1,), (0,)), ((), ())), preferred_element_type=F32)


def _mm_nt(a, b):
    return lax.dot_general(a, b, (((1,), (1,)), ((), ())), preferred_element_type=F32)


def _mm_tn(a, b):
    return lax.dot_general(a, b, (((0,), (0,)), ((), ())), preferred_element_type=F32)


def _rms_fwd(x, g):
    rstd = lax.rsqrt(jnp.mean(x * x, axis=-1, keepdims=True) + EPS)
    xhat = x * rstd
    return xhat * g, xhat, rstd


def _rms_bwd(dy, xhat, rstd, g):
    dyg = dy * g
    c = jnp.mean(dyg * xhat, axis=-1, keepdims=True)
    dx = rstd * (dyg - xhat * c)
    dg = jnp.sum(dy * xhat, axis=0, keepdims=True)
    return dx, dg


def _sigmoid(x):
    return pl.reciprocal(1.0 + jnp.exp(-x), approx=True)


def _log1p(x):
    u = 1.0 + x
    return jnp.where(u == 1.0, x, jnp.log(u) * x / (u - 1.0))


def _one_minus_sq_exp(x, ex):
    return -jnp.tanh(x) * (1.0 + ex * ex)


TINY = 1e-30


def _sqrt_pos(y):
    r = lax.rsqrt(jnp.maximum(y, TINY))
    return y * r, r


def _softplus(x):
    return jnp.maximum(x, 0.0) + _log1p(jnp.exp(-jnp.abs(x)))


GELU_C = 0.7978845608028654
GELU_K = 0.044715


def _gelu(x):
    t = jnp.tanh(GELU_C * (x + GELU_K * x * x * x))
    return 0.5 * x * (1.0 + t), t


def _gelu_grad(x, t):
    return 0.5 * (1.0 + t) + 0.5 * x * (1.0 - t * t) * GELU_C * (1.0 + 3.0 * GELU_K * x * x)


def _full(shape):
    return pl.BlockSpec(shape, lambda *_: (0,) * len(shape))


def _resident(shape):
    return pl.BlockSpec(shape, lambda *_: (0,) * len(shape), pipeline_mode=pl.Buffered(1))


def _exchange_copies(ins, outs, sems, modes):
    send_sems, recv_sems, local_sems = sems
    x, y, c = lax.axis_index("x"), lax.axis_index("y"), lax.axis_index("c")
    me = 4 * x + 2 * y + c

    def block(a, dev):
        return ins[a] if modes[a] == "gather" else ins[a].at[dev]

    local = [pltpu.make_async_copy(block(a, me), outs[a].at[me], local_sems.at[a]) for a in range(len(ins))]
    sends, recvs = [], []
    for a in range(len(ins)):
        for k in range(N_DEV - 1):
            bits = k + 1
            px = jnp.bitwise_xor(x, (bits >> 2) & 1)
            py = jnp.bitwise_xor(y, (bits >> 1) & 1)
            pc = jnp.bitwise_xor(c, bits & 1)
            peer = 4 * px + 2 * py + pc
            common = dict(src_ref=block(a, peer), send_sem=send_sems.at[a, k], recv_sem=recv_sems.at[a, k],
                          device_id=(px, py, pc), device_id_type=pl.DeviceIdType.MESH)
            sends.append(pltpu.make_async_remote_copy(dst_ref=outs[a].at[me], **common))
            recvs.append(pltpu.make_async_remote_copy(dst_ref=outs[a].at[peer], **common))
    return local, sends, recvs


def _exchange_start(ins, outs, sems, modes):
    local, sends, _ = _exchange_copies(ins, outs, sems, modes)
    for cp in local + sends:
        cp.start()


def _exchange_wait(ins, outs, sems, modes):
    local, sends, recvs = _exchange_copies(ins, outs, sems, modes)
    for cp in recvs:
        cp.wait_recv()
    for cp in sends:
        cp.wait_send()
    for cp in local:
        cp.wait()


def _exchange_shapes(arrays, modes):
    return [jax.ShapeDtypeStruct((N_DEV,) + a.shape if mode == "gather" else a.shape, a.dtype)
            for a, mode in zip(arrays, modes)]


def _exchange_sems(na):
    return [pltpu.SemaphoreType.DMA((na, N_DEV - 1)), pltpu.SemaphoreType.DMA((na, N_DEV - 1)),
            pltpu.SemaphoreType.DMA((na,))]


ANY_SPACE = pl.BlockSpec(memory_space=pl.ANY)


def _exchange(arrays, modes, name):
    na = len(arrays)

    def body(*refs):
        ins, outs, sems = refs[:na], refs[na:2 * na], refs[2 * na:]
        _exchange_start(ins, outs, sems, modes)
        _exchange_wait(ins, outs, sems, modes)

    return pl.pallas_call(
        body, name=name, out_shape=_exchange_shapes(arrays, modes),
        in_specs=[ANY_SPACE] * na, out_specs=[ANY_SPACE] * na, scratch_shapes=_exchange_sems(na),
        compiler_params=pltpu.CompilerParams(has_side_effects=True),
    )(*arrays)


def _gather_two_level(arrays, name):
    na = len(arrays)

    def body(*refs):
        ins, outs = refs[:na], refs[na:2 * na]
        send_sems, recv_sems, local_sems = refs[2 * na:]
        x, y, c = lax.axis_index("x"), lax.axis_index("y"), lax.axis_index("c")
        me, sibling = (x, y, c), (x, y, 1 - c)
        chips = [(1 - x, y), (x, 1 - y), (1 - x, 1 - y)]

        def copy(a, k, block, to, src=None):
            slot = outs[a].at[4 * block[0] + 2 * block[1] + block[2]]
            return pltpu.make_async_remote_copy(
                src_ref=slot if src is None else src, dst_ref=slot, send_sem=send_sems.at[a, k],
                recv_sem=recv_sems.at[a, k], device_id=to, device_id_type=pl.DeviceIdType.MESH)

        local = [pltpu.make_async_copy(ins[a], outs[a].at[4 * x + 2 * y + c], local_sems.at[a]) for a in range(na)]
        first = []
        for a in range(na):
            first.append(copy(a, 0, me, sibling, src=ins[a]))
            first += [copy(a, 1 + j, me, (*chip, c), src=ins[a]) for j, chip in enumerate(chips)]
        for cp in local + first:
            cp.start()
        passed = []
        for j, chip in enumerate(chips):
            for a in range(na):
                copy(a, 1 + j, (*chip, c), me).wait_recv()
                passed.append(copy(a, 4 + j, (*chip, c), sibling))
                passed[-1].start()
        for a in range(na):
            copy(a, 0, sibling, me).wait_recv()
            for j, chip in enumerate(chips):
                copy(a, 4 + j, (*chip, 1 - c), me).wait_recv()
        for cp in first + passed:
            cp.wait_send()
        for cp in local:
            cp.wait()

    return pl.pallas_call(
        body, name=name, out_shape=_exchange_shapes(arrays, ["gather"] * na),
        in_specs=[ANY_SPACE] * na, out_specs=[ANY_SPACE] * na, scratch_shapes=_exchange_sems(na),
        compiler_params=pltpu.CompilerParams(has_side_effects=True),
    )(*arrays)


def _hosting_call(body, name, steps, in_specs, out_specs, out_shape, scratch_shapes, args, arrays, modes):
    n_in, n_out, n_scr, na = len(in_specs), len(out_specs), len(scratch_shapes), len(arrays)
    grid = steps if isinstance(steps, tuple) else (steps,)

    def hosting_body(*refs):
        cuts = [0]
        for n in (n_in, na, n_out, na, n_scr, 3):
            cuts.append(cuts[-1] + n)
        ins, x_ins, outs, x_outs, scr, sems = (refs[cuts[p]:cuts[p + 1]] for p in range(6))
        first, last = True, True
        for axis, n in enumerate(grid):
            first = first & (pl.program_id(axis) == 0)
            last = last & (pl.program_id(axis) == n - 1)

        @pl.when(first)
        def _():
            _exchange_start(x_ins, x_outs, sems, modes)

        body(*ins, *outs, *scr)

        @pl.when(last)
        def _():
            _exchange_wait(x_ins, x_outs, sems, modes)

    res = pl.pallas_call(
        hosting_body, name=name, grid=grid,
        in_specs=list(in_specs) + [ANY_SPACE] * na, out_specs=list(out_specs) + [ANY_SPACE] * na,
        out_shape=list(out_shape) + _exchange_shapes(arrays, modes),
        scratch_shapes=list(scratch_shapes) + _exchange_sems(na),
        compiler_params=_params(("arbitrary",) * len(grid)),
    )(*args, *arrays)
    return res[:n_out], res[n_out:]


def _frame_rows(src_hbm, buf, sem, i, steps, tm):
    def first():
        return pltpu.make_async_copy(src_hbm.at[pl.ds(0, tm - BLOCK)], buf.at[0, pl.ds(BLOCK, tm - BLOCK)], sem.at[0])

    def later(t, slot):
        return pltpu.make_async_copy(src_hbm.at[pl.ds(pl.multiple_of(t * tm - BLOCK, SUBLANES), tm)], buf.at[slot], sem.at[slot])

    slot = i % 2

    @pl.when(i == 0)
    def _():
        first().start()

    @pl.when(i + 1 < steps)
    def _():
        later(i + 1, 1 - slot).start()

    @pl.when(i == 0)
    def _():
        first().wait()

    @pl.when(i > 0)
    def _():
        later(i, slot).wait()

    return slot


def _frame_scratch(tm):
    return [pltpu.VMEM((2, tm, D_MODEL), F32), pltpu.SemaphoreType.DMA((2,))]


def _h0_tile(head_ref, x_hbm, buf, sem, i, steps, tm):
    slot = _frame_rows(x_hbm, buf, sem, i, steps, tm)

    @pl.when(i == 0)
    def _():
        buf[0, 0:BLOCK, :] = head_ref[...]

    return buf[slot]


N_BIAS = 3


def _attn_bias():
    key = np.arange(2 * BLOCK)[:, None]
    r = np.arange(GQA_GROUP * BLOCK)[None, :] % BLOCK
    band = (key > r) & (key <= r + BLOCK)
    out = [np.where(band & ((n - 1) * BLOCK + key >= PAD_ROWS), 0.0, NEG) for n in range(N_BIAS)]
    return jnp.asarray(np.stack(out), F32)


def _attn_probs(k2, q4, bias, sink_row):
    s = _mm_nt(k2, q4) * (HEAD_DIM ** -0.5) + bias
    m = jnp.maximum(jnp.max(s, axis=0, keepdims=True), sink_row)
    p = jnp.exp(s - m)
    es = jnp.exp(sink_row - m)
    inv = 1.0 / (jnp.sum(p, axis=0, keepdims=True) + es)
    return p * inv, es * inv


def _heads(ref, rows, first, count):
    return jnp.concatenate([ref[rows, (first + g) * HEAD_DIM:(first + g + 1) * HEAD_DIM] for g in range(count)], axis=0)


def _keys_of_block(prev_ref, cur_ref, b, kv):
    sl = slice(kv * HEAD_DIM, (kv + 1) * HEAD_DIM)
    before = prev_ref[:, sl] if b == 0 else cur_ref[(b - 1) * BLOCK:b * BLOCK, sl]
    return jnp.concatenate([before, cur_ref[b * BLOCK:(b + 1) * BLOCK, sl]], axis=0)


def _bias_of_block(bias_ref, block):
    return bias_ref[jnp.minimum(block, N_BIAS - 1)]


def _sink_row(sink_ref, kv):
    g = lax.broadcasted_iota(jnp.int32, (1, GQA_GROUP * BLOCK), 1) // BLOCK
    row = jnp.full((1, GQA_GROUP * BLOCK), sink_ref[0, kv * GQA_GROUP], F32)
    for i in range(1, GQA_GROUP):
        row = jnp.where(g == i, sink_ref[0, kv * GQA_GROUP + i], row)
    return row


def _from_head_major(pieces):
    return jnp.concatenate(pieces, axis=0).T


def _attn_specs(tm, tile_of):
    nbt = tm // BLOCK
    k_col, v_col = ATTN_WIDTH // KV_WIDTH, ATTN_WIDTH // KV_WIDTH + 1
    before = lambda i: jnp.maximum(tile_of(i) * nbt - 1, 0)
    return [pl.BlockSpec((tm, ATTN_WIDTH), lambda i: (tile_of(i), 0)),
            pl.BlockSpec((BLOCK, KV_WIDTH), lambda i: (before(i), k_col)),
            pl.BlockSpec((tm, KV_WIDTH), lambda i: (tile_of(i), k_col)),
            pl.BlockSpec((BLOCK, KV_WIDTH), lambda i: (before(i), v_col)),
            pl.BlockSpec((tm, KV_WIDTH), lambda i: (tile_of(i), v_col))]


def _in_proj_attn_fwd(head, x, g1, w_in, sinks, bias, carried, modes):
    rows = BLOCK + x.shape[0]
    tm = _row_tile(rows)
    steps, nbt = rows // tm, tm // BLOCK

    def body(head_ref, g_ref, w_ref, sink_ref, bias_ref, x_hbm, qkv_ref, zrec_ref, u_ref, o_ref, buf, sem, kv_before):
        i = pl.program_id(0)
        h = _h0_tile(head_ref, x_hbm, buf, sem, i, steps, tm)
        u, _, _ = _rms_fwd(h, g_ref[...])
        u = u.astype(BF16)
        u_ref[...] = u
        z = _mm(u, w_ref[...])
        qkv_ref[...] = z[:, :QKV_WIDTH].astype(BF16)
        zrec_ref[...] = z[:, QKV_WIDTH:]

        @pl.when(i == 0)
        def _():
            kv_before[...] = jnp.zeros_like(kv_before)

        kc_ref, vc_ref = (qkv_ref.at[:, pl.ds(ATTN_WIDTH + c * KV_WIDTH, KV_WIDTH)] for c in range(2))
        kp_ref, vp_ref = (kv_before.at[:, pl.ds(c * KV_WIDTH, KV_WIDTH)] for c in range(2))
        for b in range(nbt):
            blk = slice(b * BLOCK, (b + 1) * BLOCK)
            bias_t = _bias_of_block(bias_ref, i * nbt + b)
            pieces = []
            for kv in range(KV_HEADS):
                k2 = _keys_of_block(kp_ref, kc_ref, b, kv)
                v2 = _keys_of_block(vp_ref, vc_ref, b, kv)
                q4 = _heads(qkv_ref, blk, kv * GQA_GROUP, GQA_GROUP)
                pn, _ = _attn_probs(k2, q4, bias_t, _sink_row(sink_ref, kv))
                ot = _mm_tn(v2, pn.astype(BF16))
                pieces += [ot[:, g * BLOCK:(g + 1) * BLOCK] for g in range(GQA_GROUP)]
            o_ref[blk, :] = _from_head_major(pieces).astype(BF16)
        kv_before[...] = qkv_ref[tm - BLOCK:tm, ATTN_WIDTH:]

    wide = pl.BlockSpec((tm, D_MODEL), lambda i: (i, 0))
    return _hosting_call(
        body, "in_proj_attn_fwd", steps,
        [_full((BLOCK, D_MODEL)), _full((1, D_MODEL)), _resident((D_MODEL, IN_WIDTH)), pl.BlockSpec(memory_space=pltpu.SMEM),
         _resident((N_BIAS, 2 * BLOCK, GQA_GROUP * BLOCK)), ANY_SPACE],
        [pl.BlockSpec((tm, QKV_WIDTH), lambda i: (i, 0)), pl.BlockSpec((tm, 2 * LRU_WIDTH), lambda i: (i, 0)), wide,
         pl.BlockSpec((tm, ATTN_WIDTH), lambda i: (i, 0))],
        [jax.ShapeDtypeStruct((rows, QKV_WIDTH), BF16), jax.ShapeDtypeStruct((rows, 2 * LRU_WIDTH), F32),
         jax.ShapeDtypeStruct((rows, D_MODEL), BF16), jax.ShapeDtypeStruct((rows, ATTN_WIDTH), BF16)],
        _frame_scratch(tm) + [pltpu.VMEM((BLOCK, 2 * KV_WIDTH), BF16)],
        (head, g1, w_in, sinks, bias, x), carried, modes)


def _conv_taps(xbuf, tm):
    return [xbuf[pl.ds(SUBLANES - (CONV_WIDTH - 1 - j), tm), :] for j in range(CONV_WIDTH)]


def _lru_halves(xc):
    return [xc[:, h * LRU_HALF:(h + 1) * LRU_HALF].astype(BF16) for h in range(2)]


def _lru_gates(xc, wa_ref, ba_ref, wx_ref, bx_ref, lam_ref):
    halves = _lru_halves(xc)
    gate_r = jnp.concatenate([_mm(halves[h], wa_ref[h]) for h in range(2)], axis=1) + ba_ref[...]
    gate_i = jnp.concatenate([_mm(halves[h], wx_ref[h]) for h in range(2)], axis=1) + bx_ref[...]
    r = _sigmoid(gate_r)
    ig = _sigmoid(gate_i)
    log_a = (-LRU_C) * r * _softplus(-lam_ref[...])
    a = jnp.exp(log_a)
    mult, _ = _sqrt_pos(_one_minus_sq_exp(log_a, a))
    return r, ig, a, mult


KEPT_XC, KEPT_A, KEPT_MULT, KEPT_R, KEPT_I, N_KEPT = 0, 1, 2, 3, 4, 5


def _scan_tile(a_ref, u_ref, out_ref, carry, tm):
    row = lax.broadcasted_iota(jnp.int32, (SUBLANES, LRU_WIDTH), 0)

    def step(j, before):
        o = pl.multiple_of(j * SUBLANES, SUBLANES)
        a = a_ref[pl.ds(o, SUBLANES), :]
        u = u_ref[pl.ds(o, SUBLANES), :]
        for s in (1, 2, 4):
            keep = row >= s
            u = jnp.where(keep, a * pltpu.roll(u, s, 0) + u, u)
            a = jnp.where(keep, a * pltpu.roll(a, s, 0), a)
        out = a * before + u
        out_ref[pl.ds(o, SUBLANES), :] = out
        return out[SUBLANES - 1:SUBLANES, :]

    return lax.fori_loop(0, tm // SUBLANES, step, carry)


def _rec_fwd(zrec, conv_w, conv_b, wa_bd, b_a, wx_bd, b_x, lam, carried, modes):
    rows = zrec.shape[0]
    tm = _row_tile(rows)

    def body(xr_ref, yr_ref, cw_ref, cb_ref, wa_ref, ba_ref, wx_ref, bx_ref, lam_ref, rec_ref, h_ref, kept_ref,
             xbuf, a_s, u_s, carry):
        i = pl.program_id(0)

        @pl.when(i == 0)
        def _():
            xbuf[0:SUBLANES, :] = jnp.zeros((SUBLANES, LRU_WIDTH), F32)
            carry[...] = jnp.zeros_like(carry)

        @pl.when(i > 0)
        def _():
            xbuf[0:SUBLANES, :] = xbuf[tm:tm + SUBLANES, :]

        xbuf[SUBLANES:SUBLANES + tm, :] = xr_ref[...]
        taps = _conv_taps(xbuf, tm)
        xc = cb_ref[...] + sum(cw_ref[j:j + 1, :] * taps[j] for j in range(CONV_WIDTH))
        r, ig, a, mult = _lru_gates(xc, wa_ref, ba_ref, wx_ref, bx_ref, lam_ref)
        for k, val in ((KEPT_XC, xc), (KEPT_A, a), (KEPT_MULT, mult), (KEPT_R, r), (KEPT_I, ig)):
            kept_ref[:, k * LRU_WIDTH:(k + 1) * LRU_WIDTH] = val
        grow = i * tm + lax.broadcasted_iota(jnp.int32, (tm, LRU_WIDTH), 0)
        a_s[...] = a
        u_s[...] = jnp.where(grow >= PAD_ROWS, mult * (ig * xc), 0.0)
        carry[0:1, :] = _scan_tile(a_s, u_s, h_ref, carry[0:1, :], tm)
        gel, _ = _gelu(yr_ref[...])
        rec_ref[...] = (gel * h_ref[...]).astype(BF16)

    vec = _full((1, LRU_WIDTH))
    bd = _full((2, LRU_HALF, LRU_HALF))
    return _hosting_call(
        body, "rec_fwd", rows // tm,
        [pl.BlockSpec((tm, LRU_WIDTH), lambda i: (i, 0)), pl.BlockSpec((tm, LRU_WIDTH), lambda i: (i, 1)),
         _full((CONV_WIDTH, LRU_WIDTH)), vec, bd, vec, bd, vec, vec],
        [pl.BlockSpec((tm, LRU_WIDTH), lambda i: (i, 0))] * 2 + [pl.BlockSpec((tm, N_KEPT * LRU_WIDTH), lambda i: (i, 0))],
        [jax.ShapeDtypeStruct((rows, LRU_WIDTH), BF16), jax.ShapeDtypeStruct((rows, LRU_WIDTH), F32),
         jax.ShapeDtypeStruct((rows, N_KEPT * LRU_WIDTH), F32)],
        [pltpu.VMEM((tm + SUBLANES, LRU_WIDTH), F32), pltpu.VMEM((tm, LRU_WIDTH), F32),
         pltpu.VMEM((tm, LRU_WIDTH), F32), pltpu.VMEM((SUBLANES, LRU_WIDTH), F32)],
        (zrec, zrec, conv_w, conv_b, wa_bd, b_a, wx_bd, b_x, lam), carried, modes)


FF_COLS = 1024
FF_HALF = FF_CHUNK // 2


def _hidden_at(d, half):
    return half * (D_FF // 2) + d * FF_HALF


def _mix_and_ffn_up(attn, rec, w_out, head, x, g2, g3, w1_halves, carried, modes):
    rows = attn.shape[0]
    tm = _row_tile(rows)
    steps = rows // tm

    def body(attn_ref, rec_ref, w_ref, head_ref, g2_ref, g3_ref, wa_ref, wb_ref, x_hbm,
             mix_ref, h1_ref, act_ref, u_ref, buf, sem):
        h0 = _h0_tile(head_ref, x_hbm, buf, sem, pl.program_id(0), steps, tm)
        mix = _mm(attn_ref[...], w_ref[0:ATTN_WIDTH, :]) + _mm(rec_ref[...], w_ref[ATTN_WIDTH:, :])
        y, _, _ = _rms_fwd(mix, g2_ref[...])
        mix_ref[...] = mix
        h1_ref[...] = h0 + y
        u, _, _ = _rms_fwd(h1_ref[...], g3_ref[...])
        u = u.astype(BF16)
        u_ref[...] = u
        for half, w1_ref in enumerate((wa_ref, wb_ref)):
            for d in range(N_DEV):
                c = _hidden_at(d, half)
                a1 = jnp.maximum(_mm(u, w1_ref[d]), 0.0)
                act_ref[:, c:c + FF_HALF] = (a1 * a1).astype(BF16)

    half_in = pl.BlockSpec((tm, ATTN_WIDTH), lambda i: (i, 0))
    wide = pl.BlockSpec((tm, D_MODEL), lambda i: (i, 0))
    return _hosting_call(
        body, "mix_and_ffn_up", steps,
        [half_in, half_in, _resident((D_MODEL, D_MODEL)), _full((BLOCK, D_MODEL)), _full((1, D_MODEL)), _full((1, D_MODEL))]
        + [_resident((N_DEV, D_MODEL, FF_HALF))] * 2 + [ANY_SPACE],
        [wide, wide, pl.BlockSpec((tm, D_FF), lambda i: (i, 0)), wide],
        [jax.ShapeDtypeStruct((rows, D_MODEL), F32)] * 2
        + [jax.ShapeDtypeStruct((rows, D_FF), BF16), jax.ShapeDtypeStruct((rows, D_MODEL), BF16)],
        _frame_scratch(tm), (attn, rec, w_out, head, g2, g3, *w1_halves, x), carried, modes)


def _ffn_down_loss(act, w2_halves, h1, target, g4, carried, modes):
    rows = h1.shape[0]
    tm = _row_tile(rows)
    steps = rows // tm
    kh = D_FF // 2

    def body(act_ref, wa_ref, wb_ref, h_ref, g_ref, t_hbm, dy_ref, df_ref, dg_ref, loss_ref, buf, sem):
        i = pl.program_id(0)
        slot = _frame_rows(t_hbm, buf, sem, i, steps, tm)

        @pl.when(i == 0)
        def _():
            dg_ref[...] = jnp.zeros_like(dg_ref)
            loss_ref[...] = jnp.zeros_like(loss_ref)
            buf[0, 0:BLOCK, :] = jnp.zeros((BLOCK, D_MODEL), F32)

        g = g_ref[...]
        f = _mm(act_ref[:, :kh], wa_ref[...]) + _mm(act_ref[:, kh:], wb_ref[...])
        y, fhat, rstd = _rms_fwd(f, g)
        grow = i * tm + lax.broadcasted_iota(jnp.int32, (tm, D_MODEL), 0)
        err = jnp.where(grow >= BLOCK, h_ref[...] + y - buf[slot], 0.0)
        loss_ref[...] += (0.5 / D_MODEL) * jnp.sum(err * err)
        dy = err * (1.0 / D_MODEL)
        df, dg = _rms_bwd(dy, fhat, rstd, g)
        dy_ref[...] = dy
        df_ref[...] = df.astype(BF16)
        dg_ref[...] += dg

    wide = pl.BlockSpec((tm, D_MODEL), lambda i: (i, 0))
    return _hosting_call(
        body, "ffn_down_loss", steps,
        [pl.BlockSpec((tm, D_FF), lambda i: (i, 0)), _resident((kh, D_MODEL)), _resident((kh, D_MODEL)), wide,
         _full((1, D_MODEL)), ANY_SPACE],
        [wide, wide, _full((1, D_MODEL)), _full((SUBLANES, LANES))],
        [jax.ShapeDtypeStruct((rows, D_MODEL), F32), jax.ShapeDtypeStruct((rows, D_MODEL), BF16),
         jax.ShapeDtypeStruct((1, D_MODEL), F32), jax.ShapeDtypeStruct((SUBLANES, LANES), F32)],
        _frame_scratch(tm), (act, *w2_halves, h1, g4, target), carried, modes)


def _ffn_bwd_act(df, w2t_halves, act, carried, modes):
    rows = df.shape[0]
    tm = _row_tile(rows)

    def body(df_ref, wa_ref, wb_ref, act_ref, da_ref):
        df_t = df_ref[...]
        for half, w_ref in enumerate((wa_ref, wb_ref)):
            for d in range(N_DEV):
                cols = slice(_hidden_at(d, half), _hidden_at(d, half) + FF_HALF)
                dact = _mm(df_t, w_ref[d])
                relu_a1, _ = _sqrt_pos(act_ref[:, cols].astype(F32))
                da_ref[:, cols] = (dact * (2.0 * relu_a1)).astype(BF16)

    hidden = pl.BlockSpec((tm, D_FF), lambda i: (i, 0))
    return _hosting_call(
        body, "ffn_bwd_act", rows // tm,
        [pl.BlockSpec((tm, D_MODEL), lambda i: (i, 0))] + [_resident((N_DEV, D_MODEL, FF_HALF))] * 2 + [hidden],
        [hidden],
        [jax.ShapeDtypeStruct((rows, D_FF), BF16)],
        [], (df, *w2t_halves, act), carried, modes)


def _ffn_bwd_x(da, w1t_halves, h1, dy, g3, carried, modes):
    rows = h1.shape[0]
    tm = _row_tile(rows)
    kh = D_FF // 2

    def body(da_ref, wa_ref, wb_ref, h_ref, dy_ref, g_ref, dh_ref, dg_ref):
        @pl.when(pl.program_id(0) == 0)
        def _():
            dg_ref[...] = jnp.zeros_like(dg_ref)

        g = g_ref[...]
        _, xhat, rstd = _rms_fwd(h_ref[...], g)
        du = _mm(da_ref[:, :kh], wa_ref[...]) + _mm(da_ref[:, kh:], wb_ref[...])
        dx, dg = _rms_bwd(du, xhat, rstd, g)
        dh_ref[...] = dy_ref[...] + dx
        dg_ref[...] += dg

    wide = pl.BlockSpec((tm, D_MODEL), lambda i: (i, 0))
    return _hosting_call(
        body, "ffn_bwd_x", rows // tm,
        [pl.BlockSpec((tm, D_FF), lambda i: (i, 0)), _resident((kh, D_MODEL)), _resident((kh, D_MODEL)), wide, wide,
         _full((1, D_MODEL))],
        [wide, _full((1, D_MODEL))],
        [jax.ShapeDtypeStruct((rows, D_MODEL), F32), jax.ShapeDtypeStruct((1, D_MODEL), F32)],
        [], (da, *w1t_halves, h1, dy, g3), carried, modes)


def _ffn_bwd_weights(u2, da, act, df, carried, modes):
    rows = u2.shape[0]
    tb = _big_tile(rows)
    steps = rows // tb
    per = FF_COLS // FF_HALF

    def body(u_ref, da_ref, act_ref, df_ref, dw1_ref, dw2_ref, acc1, acc2):
        i = pl.program_id(1)

        @pl.when(i == 0)
        def _():
            acc1[...] = jnp.zeros_like(acc1)
            acc2[...] = jnp.zeros_like(acc2)

        acc1[...] += _mm_tn(u_ref[...], da_ref[...])
        acc2[...] += _mm_tn(act_ref[...], df_ref[...])

        @pl.when(i == steps - 1)
        def _():
            for p in range(per):
                c = p * FF_HALF
                dw1_ref[p] = acc1[:, c:c + FF_HALF].astype(BF16)
                dw2_ref[p] = acc2[c:c + FF_HALF, :].astype(BF16)

    wide = pl.BlockSpec((tb, D_MODEL), lambda j, i: (i, 0))
    chunk = pl.BlockSpec((tb, FF_COLS), lambda j, i: (i, j))
    return _hosting_call(
        body, "ffn_bwd_weights", (D_FF // FF_COLS, steps),
        [wide, chunk, chunk, wide],
        [pl.BlockSpec((None, per, D_MODEL, FF_HALF), lambda j, i: (j // 2, j % 2, 0, 0)),
         pl.BlockSpec((per, FF_HALF, D_MODEL), lambda j, i: (j % 2, j // 2, 0))],
        [jax.ShapeDtypeStruct((2, N_DEV, D_MODEL, FF_HALF), BF16), jax.ShapeDtypeStruct((N_DEV, FF_CHUNK, D_MODEL), BF16)],
        [pltpu.VMEM((D_MODEL, FF_COLS), F32), pltpu.VMEM((FF_COLS, D_MODEL), F32)],
        (u2, da, act, df), carried, modes)


def _out_proj_bwd(dh1, mix, g2, w_out_t, attn, rec, carried, modes):
    rows = dh1.shape[0]
    tm = _row_tile(rows)
    steps = rows // tm

    def body(dh_ref, mix_ref, g_ref, w_ref, attn_ref, rec_ref, dattn_ref, drec_ref, dw_ref, dg_ref, acc):
        i = pl.program_id(0)

        @pl.when(i == 0)
        def _():
            acc[...] = jnp.zeros_like(acc)
            dg_ref[...] = jnp.zeros_like(dg_ref)

        g = g_ref[...]
        _, xhat, rstd = _rms_fwd(mix_ref[...], g)
        dmix, dg = _rms_bwd(dh_ref[...], xhat, rstd, g)
        dmix = dmix.astype(BF16)
        dg_ref[...] += dg
        din = _mm(dmix, w_ref[...])
        dattn_ref[...] = din[:, :ATTN_WIDTH].astype(BF16)
        drec_ref[...] = din[:, ATTN_WIDTH:]
        acc[0:ATTN_WIDTH, :] += _mm_tn(attn_ref[...], dmix)
        acc[ATTN_WIDTH:, :] += _mm_tn(rec_ref[...], dmix)

        @pl.when(i == steps - 1)
        def _():
            dw_ref[...] = acc[...].astype(BF16)

    half = pl.BlockSpec((tm, ATTN_WIDTH), lambda i: (i, 0))
    wide = pl.BlockSpec((tm, D_MODEL), lambda i: (i, 0))
    return _hosting_call(
        body, "out_proj_bwd", steps,
        [wide, wide, _full((1, D_MODEL)), _resident((D_MODEL, D_MODEL)), half, half],
        [half, half, _full((D_MODEL, D_MODEL)), _full((1, D_MODEL))],
        [jax.ShapeDtypeStruct((rows, ATTN_WIDTH), BF16), jax.ShapeDtypeStruct((rows, LRU_WIDTH), F32),
         jax.ShapeDtypeStruct((D_MODEL, D_MODEL), BF16), jax.ShapeDtypeStruct((1, D_MODEL), F32)],
        [pltpu.VMEM((D_MODEL, D_MODEL), F32)],
        (dh1, mix, g2, w_out_t, attn, rec), carried, modes)


def _attn_bwd(qkv, dattn, sinks, bias, carried, modes):
    rows = qkv.shape[0]
    tm = _row_tile(rows)
    nbt, nt = tm // BLOCK, rows // tm

    def body(sink_ref, bias_ref, do_ref, q_ref, kp_ref, kc_ref, vp_ref, vc_ref, dq_ref, dkv_ref, dsink_ref, dk_c, dv_c):
        i = pl.program_id(0)

        @pl.when(i == 0)
        def _():
            dk_c[...] = jnp.zeros_like(dk_c)
            dv_c[...] = jnp.zeros_like(dv_c)
            dsink_ref[...] = jnp.zeros_like(dsink_ref)

        @pl.when(i < nt)
        def _():
            dk_late, dv_late = dk_c[...], dv_c[...]
            dsink_rows = [jnp.zeros((1, LANES), F32)] * ATTN_HEADS
            for b in range(nbt):
                blk = slice(b * BLOCK, (b + 1) * BLOCK)
                bias_t = _bias_of_block(bias_ref, i * nbt + b)
                dq_parts, dk_parts, dv_parts = [], [], []
                for kv in range(KV_HEADS):
                    k2 = _keys_of_block(kp_ref, kc_ref, b, kv)
                    v2 = _keys_of_block(vp_ref, vc_ref, b, kv)
                    q4 = _heads(q_ref, blk, kv * GQA_GROUP, GQA_GROUP)
                    do4 = _heads(do_ref, blk, kv * GQA_GROUP, GQA_GROUP)
                    pn, psink = _attn_probs(k2, q4, bias_t, _sink_row(sink_ref, kv))
                    dpn = _mm_nt(v2, do4)
                    delta = jnp.sum(pn * dpn, axis=0, keepdims=True)
                    ds = ((pn * (dpn - delta)) * (HEAD_DIM ** -0.5)).astype(BF16)
                    dqt = _mm_tn(k2, ds)
                    dq_parts += [dqt[:, g * BLOCK:(g + 1) * BLOCK] for g in range(GQA_GROUP)]
                    dk_parts.append(_mm(ds, q4))
                    dv_parts.append(_mm(pn.astype(BF16), do4))
                    sd = psink * delta
                    for g in range(GQA_GROUP):
                        h = kv * GQA_GROUP + g
                        dsink_rows[h] = dsink_rows[h] - jnp.sum(sd[:, g * BLOCK:(g + 1) * BLOCK])
                dq_ref[blk, :] = _from_head_major(dq_parts).astype(BF16)
                dk2 = jnp.concatenate(dk_parts, axis=1)
                dv2 = jnp.concatenate(dv_parts, axis=1)
                dkv_ref[blk, 0:KV_WIDTH] = (dk_late + dk2[0:BLOCK]).astype(BF16)
                dkv_ref[blk, KV_WIDTH:] = (dv_late + dv2[0:BLOCK]).astype(BF16)
                dk_late, dv_late = dk2[BLOCK:], dv2[BLOCK:]
            dk_c[...] = dk_late
            dv_c[...] = dv_late
            dsink_ref[...] += jnp.concatenate(dsink_rows, axis=0)

        @pl.when(i == nt)
        def _():
            dkv_ref[...] = jnp.zeros_like(dkv_ref)
            dkv_ref[0:BLOCK, 0:KV_WIDTH] = dk_c[...].astype(BF16)
            dkv_ref[0:BLOCK, KV_WIDTH:] = dv_c[...].astype(BF16)

    tile_of = lambda i: jnp.minimum(i, nt - 1)
    tile = pl.BlockSpec((tm, ATTN_WIDTH), lambda i: (tile_of(i), 0))
    return _hosting_call(
        body, "attn_bwd", nt + 1,
        [pl.BlockSpec(memory_space=pltpu.SMEM), _resident((N_BIAS, 2 * BLOCK, GQA_GROUP * BLOCK)), tile]
        + _attn_specs(tm, tile_of),
        [tile, pl.BlockSpec((tm, 2 * KV_WIDTH), lambda i: (i, 0)), _full((ATTN_HEADS, LANES))],
        [jax.ShapeDtypeStruct((rows, ATTN_WIDTH), BF16), jax.ShapeDtypeStruct((rows + tm, 2 * KV_WIDTH), BF16),
         jax.ShapeDtypeStruct((ATTN_HEADS, LANES), F32)],
        [pltpu.VMEM((BLOCK, KV_WIDTH), F32), pltpu.VMEM((BLOCK, KV_WIDTH), F32)],
        (sinks, bias, dattn, qkv, qkv, qkv, qkv, qkv), carried, modes)


ROW_CONV_B, ROW_B_A, ROW_B_X, ROW_LAMBDA = 4, 5, 6, 7


def _rec_bwd(drec, zrec, h, kept, conv_w, wa_bd, wx_bd, lam, carried, modes):
    rows = zrec.shape[0]
    tm = _rec_tile(rows)
    nt = rows // tm
    per = tm // SUBLANES

    def body(drec_ref, xr_ref, yr_ref, h_ref, xc_ref, a_ref, mult_ref, r_ref, ig_ref, hhalo_ref, cw_ref, wa_ref, wx_ref,
             lam_ref, drz_ref, small_ref, dwa_ref, dwx_ref, hbuf, dbuf, dgr_s, dgi_s, dyr_s, carry):
        s = pl.program_id(0)
        i = nt - 1 - s

        @pl.when(s == 0)
        def _():
            small_ref[...] = jnp.zeros_like(small_ref)
            dwa_ref[...] = jnp.zeros_like(dwa_ref)
            dwx_ref[...] = jnp.zeros_like(dwx_ref)
            carry[...] = jnp.zeros_like(carry)
            dbuf[tm:tm + SUBLANES, :] = jnp.zeros((SUBLANES, LRU_WIDTH), F32)

        hbuf[0:SUBLANES, :] = jnp.where(i == 0, 0.0, hhalo_ref[...])
        hbuf[SUBLANES:SUBLANES + tm, :] = h_ref[...]

        row = lax.broadcasted_iota(jnp.int32, (SUBLANES, LRU_WIDTH), 0)
        log_a_scale = (-LRU_C) * _softplus(-lam_ref[...])
        zeros = jnp.zeros((SUBLANES, LRU_WIDTH), F32)

        def group(k, state):
            g_later, a_later, sum_dgr, sum_dgi, sum_lam = state
            o = pl.multiple_of((per - 1 - k) * SUBLANES, SUBLANES)
            rows8 = pl.ds(o, SUBLANES)
            yr, drec_t, h_t, a = yr_ref[rows8, :], drec_ref[rows8, :], h_ref[rows8, :], a_ref[rows8, :]
            gel, t = _gelu(yr)
            dyr_s[rows8, :] = drec_t * h_t * _gelu_grad(yr, t)
            u = drec_t * gel
            coef = jnp.where(row == SUBLANES - 1, a_later, pltpu.roll(a, SUBLANES - 1, 0))
            for sft in (1, 2, 4):
                keep = row < SUBLANES - sft
                u = jnp.where(keep, coef * pltpu.roll(u, SUBLANES - sft, 0) + u, u)
                coef = jnp.where(keep, coef * pltpu.roll(coef, SUBLANES - sft, 0), coef)
            g = coef * g_later + u
            du = jnp.where(i * tm + o + row >= PAD_ROWS, g, 0.0)
            h_before = jnp.where(row == 0, hbuf[rows8, :][SUBLANES - 1:SUBLANES, :], pltpu.roll(h_t, 1, 0))
            xc, mult, r, ig = xc_ref[rows8, :], mult_ref[rows8, :], r_ref[rows8, :], ig_ref[rows8, :]
            dbuf[rows8, :] = du * (mult * ig)
            dgi = (du * (mult * xc)) * (ig * (1.0 - ig))
            dgi_s[rows8, :] = dgi
            dlog_a = (g * h_before) * a - (du * (ig * xc)) * (a * a * pl.reciprocal(mult, approx=True))
            dgr = (dlog_a * log_a_scale) * (r * (1.0 - r))
            dgr_s[rows8, :] = dgr
            return g[0:1, :], a[0:1, :], sum_dgr + dgr, sum_dgi + dgi, sum_lam + dlog_a * r

        state = lax.fori_loop(0, per, group, (carry[0:1, :], carry[1:2, :], zeros, zeros, zeros))
        carry[0:1, :], carry[1:2, :] = state[0], state[1]
        sum_dgr, sum_dgi, sum_lam = (jnp.sum(v, axis=0, keepdims=True) for v in state[2:])
        dlam = sum_lam * (LRU_C * _sigmoid(-lam_ref[...]))

        dgr_b = [dgr_s[:, hh * LRU_HALF:(hh + 1) * LRU_HALF].astype(BF16) for hh in range(2)]
        dgi_b = [dgi_s[:, hh * LRU_HALF:(hh + 1) * LRU_HALF].astype(BF16) for hh in range(2)]
        halves = _lru_halves(xc_ref[...])
        for hh in range(2):
            dwa_ref[hh] += _mm_tn(halves[hh], dgr_b[hh])
            dwx_ref[hh] += _mm_tn(halves[hh], dgi_b[hh])
        dxc = dbuf[0:tm, :] + jnp.concatenate(
            [_mm_nt(dgr_b[hh], wa_ref[hh]) + _mm_nt(dgi_b[hh], wx_ref[hh]) for hh in range(2)], axis=1)

        dbuf[0:tm, :] = dxc
        sum_dxc = jnp.sum(dxc, axis=0, keepdims=True)
        ahead = [dbuf[pl.ds(CONV_WIDTH - 1 - j, tm), :] for j in range(CONV_WIDTH)]
        drz_ref[:, 0:LRU_WIDTH] = sum(cw_ref[j:j + 1, :] * ahead[j] for j in range(CONV_WIDTH)).astype(BF16)
        drz_ref[:, LRU_WIDTH:] = dyr_s[...].astype(BF16)
        upd = [jnp.sum(xr_ref[...] * ahead[j], axis=0, keepdims=True) for j in range(CONV_WIDTH)]
        dbuf[tm:tm + SUBLANES, :] = dbuf[0:SUBLANES, :]
        small_ref[...] += jnp.concatenate(upd + [sum_dxc, sum_dgr, sum_dgi, dlam], axis=0)

    rev = lambda s: nt - 1 - s
    halo = lambda s: jnp.maximum(rev(s) * per - 1, 0)
    cols = lambda k: pl.BlockSpec((tm, LRU_WIDTH), lambda s: (rev(s), k))
    halo0 = pl.BlockSpec((SUBLANES, LRU_WIDTH), lambda s: (halo(s), 0))
    bd = _full((2, LRU_HALF, LRU_HALF))
    big = pltpu.VMEM((tm + SUBLANES, LRU_WIDTH), F32)
    tile = pltpu.VMEM((tm, LRU_WIDTH), F32)
    kept_cols = [cols(k) for k in (KEPT_XC, KEPT_A, KEPT_MULT, KEPT_R, KEPT_I)]
    return _hosting_call(
        body, "rec_bwd", nt,
        [cols(0), cols(0), cols(1), cols(0)] + kept_cols
        + [halo0, _full((CONV_WIDTH, LRU_WIDTH)), bd, bd, _full((1, LRU_WIDTH))],
        [pl.BlockSpec((tm, 2 * LRU_WIDTH), lambda s: (rev(s), 0)), _full((SUBLANES, LRU_WIDTH)), bd, bd],
        [jax.ShapeDtypeStruct((rows, 2 * LRU_WIDTH), BF16), jax.ShapeDtypeStruct((SUBLANES, LRU_WIDTH), F32),
         jax.ShapeDtypeStruct((2, LRU_HALF, LRU_HALF), F32), jax.ShapeDtypeStruct((2, LRU_HALF, LRU_HALF), F32)],
        [big, big, tile, tile, tile, pltpu.VMEM((SUBLANES, LRU_WIDTH), F32)],
        (drec, zrec, zrec, h) + (kept,) * N_KEPT + (h, conv_w, wa_bd, wx_bd, lam), carried, modes)


DZ_CUTS = (0, ATTN_WIDTH, QKV_WIDTH, IN_WIDTH)


def _dz_specs(tm):
    return [pl.BlockSpec((tm, DZ_CUTS[p + 1] - DZ_CUTS[p]), lambda i: (i, 0)) for p in range(3)]


def _in_proj_bwd_x(head, x, g1, dh1, dq, dkv, drz, w_in_t, carried, modes):
    rows = dh1.shape[0]
    tm = _row_tile(rows)
    steps = rows // tm

    def body(head_ref, g_ref, dh1_ref, dq_ref, dkv_ref, drz_ref, w_ref, x_hbm, dh0_ref, dg_ref, buf, sem):
        i = pl.program_id(0)
        h0 = _h0_tile(head_ref, x_hbm, buf, sem, i, steps, tm)

        @pl.when(i == 0)
        def _():
            dg_ref[...] = jnp.zeros_like(dg_ref)

        g = g_ref[...]
        _, xhat, rstd = _rms_fwd(h0, g)
        parts = (dq_ref[...], dkv_ref[...], drz_ref[...])
        du = sum(_mm(parts[p], w_ref[DZ_CUTS[p]:DZ_CUTS[p + 1], :]) for p in range(3))
        dx, dg = _rms_bwd(du, xhat, rstd, g)
        dh0_ref[...] = dh1_ref[...] + dx
        dg_ref[...] += dg

    wide = pl.BlockSpec((tm, D_MODEL), lambda i: (i, 0))
    return _hosting_call(
        body, "in_proj_bwd_x", steps,
        [_full((BLOCK, D_MODEL)), _full((1, D_MODEL)), wide] + _dz_specs(tm) + [_resident((IN_WIDTH, D_MODEL)), ANY_SPACE],
        [wide, _full((1, D_MODEL))],
        [jax.ShapeDtypeStruct((rows, D_MODEL), F32), jax.ShapeDtypeStruct((1, D_MODEL), F32)],
        _frame_scratch(tm), (head, g1, dh1, dq, dkv, drz, w_in_t, x), carried, modes)


def _in_proj_bwd_w(u1, dq, dkv, drz, carried, modes):
    rows = u1.shape[0]
    tb = _big_tile(rows)
    steps = rows // tb

    def body(u_ref, dq_ref, dkv_ref, drz_ref, dw_ref, acc):
        i = pl.program_id(0)

        @pl.when(i == 0)
        def _():
            acc[...] = jnp.zeros_like(acc)

        u = u_ref[...]
        for p, ref in enumerate((dq_ref, dkv_ref, drz_ref)):
            acc[:, DZ_CUTS[p]:DZ_CUTS[p + 1]] += _mm_tn(u, ref[...])

        @pl.when(i == steps - 1)
        def _():
            dw_ref[...] = acc[...].astype(BF16)

    return _hosting_call(
        body, "in_proj_bwd_w", steps,
        [pl.BlockSpec((tb, D_MODEL), lambda i: (i, 0))] + _dz_specs(tb),
        [_full((D_MODEL, IN_WIDTH))],
        [jax.ShapeDtypeStruct((D_MODEL, IN_WIDTH), BF16)],
        [pltpu.VMEM((D_MODEL, IN_WIDTH), F32)], (u1, dq, dkv, drz), carried, modes)


def _adamw_math(w, m, v, g):
    nm = ADAM_B1 * m + (1.0 - ADAM_B1) * g
    nv = ADAM_B2 * v + (1.0 - ADAM_B2) * (g * g)
    m_hat = nm / (1.0 - ADAM_B1 ** ADAM_STEP)
    v_hat = nv / (1.0 - ADAM_B2 ** ADAM_STEP)
    return (-ADAM_LR) * (m_hat / (jnp.sqrt(v_hat) + ADAM_EPS) + ADAM_WD * w), nm, nv


SMALL_NAMES = ("conv_b", "b_a", "b_x", "lru_lambda", "attn_sinks", "g_post_mix", "g_pre_ffn", "g_post_ffn")
PACK_WIDTH = 1024


def _pack_rows(vals):
    assert len(SMALL_NAMES) == SUBLANES
    row = lax.broadcasted_iota(jnp.int32, (SUBLANES, PACK_WIDTH), 0)
    tile = jnp.zeros((SUBLANES, PACK_WIDTH), F32)
    for k, name in enumerate(SMALL_NAMES):
        a = vals[name].reshape(1, -1)
        tile = jnp.where(row == k, jnp.pad(a, ((0, 0), (0, PACK_WIDTH - a.shape[1]))), tile)
    return tile


def _adamw_small(weights, mom_m, mom_v, parts, loss_parts):
    n = len(SMALL_NAMES)
    views = [(1, weights[name].size) for name in SMALL_NAMES]

    def body(*refs):
        w_refs, m_refs, v_refs = refs[:n], refs[n:2 * n], refs[2 * n:3 * n]
        p_ref, l_ref, loss_ref = refs[3 * n], refs[3 * n + 1], refs[3 * n + 2]
        outs = refs[3 * n + 3:]
        for k, (_, c) in enumerate(views):
            g = p_ref[0, k:k + 1, 0:c]
            for s in range(1, N_DEV):
                g = g + p_ref[s, k:k + 1, 0:c]
            g_ref, d_ref, nm_ref, nv_ref = outs[4 * k:4 * k + 4]
            g_ref[...] = g
            d_ref[...], nm_ref[...], nv_ref[...] = _adamw_math(w_refs[k][...], m_refs[k][...], v_refs[k][...], g)
        total = l_ref[0]
        for s in range(1, N_DEV):
            total = total + l_ref[s]
        loss_ref[...] = total

    args = [src[name].reshape(view) for src in (weights, mom_m, mom_v) for name, view in zip(SMALL_NAMES, views)]
    res = pl.pallas_call(
        body, name="adamw_small",
        out_shape=[jax.ShapeDtypeStruct(loss_parts.shape[1:], F32)]
                  + [jax.ShapeDtypeStruct(view, F32) for view in views for _ in range(4)],
        compiler_params=pltpu.CompilerParams(vmem_limit_bytes=VMEM_LIMIT),
    )(*args, parts, loss_parts)
    out = {name: tuple(t.reshape(weights[name].shape) for t in res[1 + 4 * k:5 + 4 * k]) for k, name in enumerate(SMALL_NAMES)}
    return res[0], out


def _adamw(w, m, v, parts, name):
    rows, cols = w.shape
    tr = next((t for t in (256, 128) if rows % t == 0), rows)
    parts = parts if isinstance(parts, (list, tuple)) else [parts]

    def body(w_ref, m_ref, v_ref, *refs):
        p_refs, (g_ref, d_ref, nm_ref, nv_ref) = refs[:len(parts)], refs[len(parts):]

        def total(p_ref):
            g = p_ref[0].astype(F32)
            for s in range(1, N_DEV):
                g = g + p_ref[s].astype(F32)
            return g

        g = jnp.concatenate([total(p_ref) for p_ref in p_refs], axis=1) if len(parts) > 1 else total(p_refs[0])
        g_ref[...] = g
        d_ref[...], nm_ref[...], nv_ref[...] = _adamw_math(w_ref[...], m_ref[...], v_ref[...], g)

    blk = pl.BlockSpec((tr, cols), lambda i: (i, 0))
    return pl.pallas_call(
        body, name=name, grid=(rows // tr,),
        in_specs=[blk, blk, blk] + [pl.BlockSpec((N_DEV, tr, p.shape[2]), lambda i: (0, i, 0)) for p in parts],
        out_specs=[blk] * 4,
        out_shape=[jax.ShapeDtypeStruct((rows, cols), F32)] * 4,
        compiler_params=_params(("parallel",)),
    )(w, m, v, *parts)


def _cols_from_shards(g):
    return jnp.transpose(g, (1, 0, 2)).reshape(g.shape[1], N_DEV * g.shape[2])


def _cols_to_shards(a):
    r, c = a.shape
    return jnp.transpose(a.reshape(r, N_DEV, c // N_DEV), (1, 0, 2))


def _block_diag(w):
    per = LRU_HALF // LRU_BLOCK
    w = w.reshape(2, per, LRU_BLOCK, LRU_BLOCK)
    eye = jnp.eye(per, dtype=w.dtype)
    return (w[:, :, :, None, :] * eye[None, :, None, :, None]).reshape(2, LRU_HALF, LRU_HALF)


def _block_diag_extract(t):
    per = LRU_HALF // LRU_BLOCK
    t = t.reshape(2, per, LRU_BLOCK, per, LRU_BLOCK)
    return jnp.stack([t[:, b, :, b, :] for b in range(per)], axis=1).reshape(LRU_BLOCKS, LRU_BLOCK, LRU_BLOCK)


def kernel(x, meta_tokens, g_pre_mix, w_in, conv_w, conv_b, w_a, b_a, w_x, b_x, lru_lambda, attn_sinks, w_out, g_post_mix, g_pre_ffn, w_ff1, w_ff2, g_post_ffn, loss_target, m_meta_tokens, m_g_pre_mix, m_w_in, m_conv_w, m_conv_b, m_w_a, m_b_a, m_w_x, m_b_x, m_lru_lambda, m_attn_sinks, m_w_out, m_g_post_mix, m_g_pre_ffn, m_w_ff1, m_w_ff2, m_g_post_ffn, v_meta_tokens, v_g_pre_mix, v_w_in, v_conv_w, v_conv_b, v_w_a, v_b_a, v_w_x, v_b_x, v_lru_lambda, v_attn_sinks, v_w_out, v_g_post_mix, v_g_pre_ffn, v_w_ff1, v_w_ff2, v_g_post_ffn):
    weights = dict(meta_tokens=meta_tokens, g_pre_mix=g_pre_mix, w_in=w_in, conv_w=conv_w, conv_b=conv_b, w_a=w_a,
                   b_a=b_a, w_x=w_x, b_x=b_x, lru_lambda=lru_lambda, attn_sinks=attn_sinks, w_out=w_out,
                   g_post_mix=g_post_mix, g_pre_ffn=g_pre_ffn, w_ff1=w_ff1, w_ff2=w_ff2, g_post_ffn=g_post_ffn)
    mom_m = dict(meta_tokens=m_meta_tokens, g_pre_mix=m_g_pre_mix, w_in=m_w_in, conv_w=m_conv_w, conv_b=m_conv_b,
                 w_a=m_w_a, b_a=m_b_a, w_x=m_w_x, b_x=m_b_x, lru_lambda=m_lru_lambda, attn_sinks=m_attn_sinks,
                 w_out=m_w_out, g_post_mix=m_g_post_mix, g_pre_ffn=m_g_pre_ffn, w_ff1=m_w_ff1, w_ff2=m_w_ff2,
                 g_post_ffn=m_g_post_ffn)
    mom_v = dict(meta_tokens=v_meta_tokens, g_pre_mix=v_g_pre_mix, w_in=v_w_in, conv_w=v_conv_w, conv_b=v_conv_b,
                 w_a=v_w_a, b_a=v_b_a, w_x=v_w_x, b_x=v_b_x, lru_lambda=v_lru_lambda, attn_sinks=v_attn_sinks,
                 w_out=v_w_out, g_post_mix=v_g_post_mix, g_pre_ffn=v_g_pre_ffn, w_ff1=v_w_ff1, w_ff2=v_w_ff2,
                 g_post_ffn=v_g_post_ffn)
    order = list(weights)

    (g_win, g_meta, g_cw) = _gather_two_level([w_in[0].astype(BF16), meta_tokens, conv_w[0]], "gather_first")
    w_in_full = _cols_from_shards(g_win)
    meta_full = _cols_from_shards(g_meta)
    conv_w_full = _cols_from_shards(g_cw)

    head = jnp.concatenate([jnp.zeros((PAD_ROWS, D_MODEL), F32), meta_full], axis=0)
    wa_bd = _block_diag(w_a[0]).astype(BF16)
    wx_bd = _block_diag(w_x[0]).astype(BF16)
    bias = _attn_bias()

    w1_shard = w_ff1[0].astype(BF16)
    (qkv, zrec, u1, attn), (g_wout, w1a) = _in_proj_attn_fwd(
        head, x[0], g_pre_mix, w_in_full, attn_sinks, bias,
        [w_out[0].astype(BF16), w1_shard[:, :FF_HALF]], ["gather"] * 2)
    (rec, h_lru, kept), (w1b,) = _rec_fwd(zrec, conv_w_full, conv_b, wa_bd, b_a, wx_bd, b_x, lru_lambda,
                                         [w1_shard[:, FF_HALF:]], ["gather"])
    w_out_full = g_wout.reshape(D_MODEL, D_MODEL)
    w2_shard = w_ff2[0].astype(BF16)
    (mix, h1, act, u2), (w2a, w2b) = _mix_and_ffn_up(
        attn, rec, w_out_full, head, x[0], g_post_mix, g_pre_ffn, (w1a, w1b),
        [w2_shard[:FF_HALF], w2_shard[FF_HALF:]], ["gather"] * 2)
    w2_halves = [w.reshape(D_FF // 2, D_MODEL) for w in (w2a, w2b)]
    (dy, df, dg_post_ffn, loss_acc), w2t_halves = _ffn_down_loss(
        act, w2_halves, h1, loss_target[0], g_post_ffn, [w2_shard[:FF_HALF].T, w2_shard[FF_HALF:].T], ["gather"] * 2)

    (da1,), (w1ta,) = _ffn_bwd_act(df, w2t_halves, act, [w1_shard[:, :FF_HALF].T], ["gather"])
    (dw1h, dw2g), (w1tb,) = _ffn_bwd_weights(u2, da1, act, df, [w1_shard[:, FF_HALF:].T], ["gather"])
    w1t_halves = [w.reshape(D_FF // 2, D_MODEL) for w in (w1ta, w1tb)]
    (dh1, dg_pre_ffn), (p_w1a,) = _ffn_bwd_x(da1, w1t_halves, h1, dy, g_pre_ffn, [dw1h[0]], ["scatter"])
    (dattn, drec, dw_out, dg_post_mix), (p_w1b,) = _out_proj_bwd(dh1, mix, g_post_mix, w_out_full.T, attn, rec,
                                                                [dw1h[1]], ["scatter"])
    (dq, dkv_late, dsinks), (p_w2,) = _attn_bwd(qkv, dattn, attn_sinks, bias, [dw2g], ["scatter"])
    dkv = dkv_late[BLOCK:BLOCK + qkv.shape[0]]
    (drz, rec_small, dwa_bd, dwx_bd), (p_wout,) = _rec_bwd(
        drec, zrec, h_lru, kept, conv_w_full, wa_bd, wx_bd, lru_lambda,
        [dw_out.reshape(N_DEV, D_MODEL // N_DEV, D_MODEL)], ["scatter"])
    small_grads = dict(
        conv_b=rec_small[ROW_CONV_B], b_a=rec_small[ROW_B_A], b_x=rec_small[ROW_B_X], lru_lambda=rec_small[ROW_LAMBDA],
        attn_sinks=dsinks[:, 0], g_post_mix=dg_post_mix, g_pre_ffn=dg_pre_ffn, g_post_ffn=dg_post_ffn)
    gate_rows = (LRU_BLOCKS * LRU_BLOCK, LRU_BLOCK)
    gate_dense = (LRU_BLOCKS * LRU_BLOCK * LRU_BLOCK // PACK_WIDTH, PACK_WIDTH)
    (dw_in,), (p_cw, p_small, p_wa, p_wx) = _in_proj_bwd_w(
        u1, dq, dkv, drz,
        [_cols_to_shards(rec_small[0:CONV_WIDTH]), _pack_rows(small_grads),
         _block_diag_extract(dwa_bd).reshape(gate_dense), _block_diag_extract(dwx_bd).reshape(gate_dense)],
        ["scatter", "gather", "gather", "gather"])
    p_wa, p_wx = (p.reshape((N_DEV,) + gate_rows) for p in (p_wa, p_wx))
    (dh0, dg_pre_mix), (p_win,) = _in_proj_bwd_x(
        head, x[0], g_pre_mix, dh1, dq, dkv, drz, w_in_full.T, [_cols_to_shards(dw_in)], ["scatter"])
    p_meta, p_gpm, p_loss = _exchange([_cols_to_shards(dh0[PAD_ROWS:BLOCK]), dg_pre_mix, loss_acc],
                                      ["scatter", "gather", "gather"], "exchange_last")

    res = {}
    res["g_pre_mix"] = _adamw(g_pre_mix, m_g_pre_mix, v_g_pre_mix, p_gpm, "adamw_g_pre_mix")
    res["w_in"] = _adamw(w_in[0], m_w_in[0], v_w_in[0], p_win, "adamw_w_in")
    res["w_out"] = _adamw(w_out[0], m_w_out[0], v_w_out[0], p_wout, "adamw_w_out")
    res["w_ff1"] = _adamw(w_ff1[0], m_w_ff1[0], v_w_ff1[0], [p_w1a, p_w1b], "adamw_w_ff1")
    res["w_ff2"] = _adamw(w_ff2[0], m_w_ff2[0], v_w_ff2[0], p_w2, "adamw_w_ff2")
    res["meta_tokens"] = _adamw(meta_tokens, m_meta_tokens, v_meta_tokens, p_meta, "adamw_meta")
    res["conv_w"] = _adamw(conv_w[0], m_conv_w[0], v_conv_w[0], p_cw, "adamw_conv_w")
    for name in ("w_in", "w_out", "w_ff1", "w_ff2", "conv_w"):
        res[name] = tuple(t[None] for t in res[name])
    for name, parts in (("w_a", p_wa), ("w_x", p_wx)):
        gate = _adamw(*(src[name].reshape(gate_rows) for src in (weights, mom_m, mom_v)), parts, "adamw_" + name)
        res[name] = tuple(t.reshape(weights[name].shape) for t in gate)
    loss_total, small = _adamw_small(weights, mom_m, mom_v, p_small, p_loss)
    res.update(small)

    grad_x = dh0[BLOCK:][None]
    outs = [loss_total[0, 0], grad_x]
    for k in range(4):
        outs += [res[name][k] for name in order]
    return tuple(outs)
```

```python
import jax
import jax.numpy as jnp
import numpy as np
from jax import lax
from jax.experimental import pallas as pl
from jax.experimental.pallas import tpu as pltpu

F32 = jnp.float32
BF16 = jnp.bfloat16

D_MODEL = 1024
N_META = 16
HEAD_DIM = 64
ATTN_HEADS = 8
KV_HEADS = 2
GQA_GROUP = ATTN_HEADS // KV_HEADS
ATTN_WIDTH = ATTN_HEADS * HEAD_DIM
KV_WIDTH = KV_HEADS * HEAD_DIM
QKV_WIDTH = ATTN_WIDTH + 2 * KV_WIDTH
LRU_WIDTH = 512
LRU_BLOCKS = 8
LRU_BLOCK = 64
LRU_HALF = 256
LRU_C = 8.0
CONV_WIDTH = 4
BLOCK = 128
PAD_ROWS = BLOCK - N_META
IN_WIDTH = QKV_WIDTH + 2 * LRU_WIDTH
D_FF = 4096
EPS = 1e-6
NEG = -1e30
N_DEV = 8
FF_CHUNK = D_FF // N_DEV
SUBLANES = 8
LANES = 128

ADAM_LR = 0.001
ADAM_B1 = 0.9
ADAM_B2 = 0.999
ADAM_EPS = 1e-08
ADAM_WD = 0.01
ADAM_STEP = 10

VMEM_LIMIT = 56 * 1024 * 1024


def _row_tile(rows):
    for t in (640, 512, 256, 128):
        if rows % t == 0:
            return t
    raise ValueError(rows)


def _big_tile(rows):
    for t in (1664, 1024, 512, 256, 128):
        if rows % t == 0:
            return t
    raise ValueError(rows)


def _rec_tile(rows):
    for t in (640, 256, 128):
        if rows % t == 0:
            return t
    raise ValueError(rows)


def _params(semantics):
    return pltpu.CompilerParams(dimension_semantics=semantics, vmem_limit_bytes=VMEM_LIMIT)


def _mm(a, b):
    return lax.dot_general(a, b, (((1,), (0,)), ((), ())), preferred_element_type=F32)


def _mm_nt(a, b):
    return lax.dot_general(a, b, (((1,), (1,)), ((), ())), preferred_element_type=F32)


def _mm_tn(a, b):
    return lax.dot_general(a, b, (((0,), (0,)), ((), ())), preferred_element_type=F32)


def _rms_fwd(x, g):
    rstd = lax.rsqrt(jnp.mean(x * x, axis=-1, keepdims=True) + EPS)
    xhat = x * rstd
    return xhat * g, xhat, rstd


def _rms_bwd(dy, xhat, rstd, g):
    dyg = dy * g
    c = jnp.mean(dyg * xhat, axis=-1, keepdims=True)
    dx = rstd * (dyg - xhat * c)
    dg = jnp.sum(dy * xhat, axis=0, keepdims=True)
    return dx, dg


def _sigmoid(x):
    return pl.reciprocal(1.0 + jnp.exp(-x), approx=True)


def _log1p(x):
    u = 1.0 + x
    return jnp.where(u == 1.0, x, jnp.log(u) * x / (u - 1.0))


def _one_minus_sq_exp(x, ex):
    return -jnp.tanh(x) * (1.0 + ex * ex)


TINY = 1e-30


def _sqrt_pos(y):
    r = lax.rsqrt(jnp.maximum(y, TINY))
    return y * r, r


def _softplus(x):
    return jnp.maximum(x, 0.0) + _log1p(jnp.exp(-jnp.abs(x)))


GELU_C = 0.7978845608028654
GELU_K = 0.044715


def _gelu(x):
    t = jnp.tanh(GELU_C * (x + GELU_K * x * x * x))
    return 0.5 * x * (1.0 + t), t


def _gelu_grad(x, t):
    return 0.5 * (1.0 + t) + 0.5 * x * (1.0 - t * t) * GELU_C * (1.0 + 3.0 * GELU_K * x * x)


def _full(shape):
    return pl.BlockSpec(shape, lambda *_: (0,) * len(shape))


def _resident(shape):
    return pl.BlockSpec(shape, lambda *_: (0,) * len(shape), pipeline_mode=pl.Buffered(1))


def _exchange_copies(ins, outs, sems, modes):
    send_sems, recv_sems, local_sems = sems
    x, y, c = lax.axis_index("x"), lax.axis_index("y"), lax.axis_index("c")
    me = 4 * x + 2 * y + c

    def block(a, dev):
        return ins[a] if modes[a] == "gather" else ins[a].at[dev]

    local = [pltpu.make_async_copy(block(a, me), outs[a].at[me], local_sems.at[a]) for a in range(len(ins))]
    sends, recvs = [], []
    for a in range(len(ins)):
        for k in range(N_DEV - 1):
            bits = k + 1
            px = jnp.bitwise_xor(x, (bits >> 2) & 1)
            py = jnp.bitwise_xor(y, (bits >> 1) & 1)
            pc = jnp.bitwise_xor(c, bits & 1)
            peer = 4 * px + 2 * py + pc
            common = dict(src_ref=block(a, peer), send_sem=send_sems.at[a, k], recv_sem=recv_sems.at[a, k],
                          device_id=(px, py, pc), device_id_type=pl.DeviceIdType.MESH)
            sends.append(pltpu.make_async_remote_copy(dst_ref=outs[a].at[me], **common))
            recvs.append(pltpu.make_async_remote_copy(dst_ref=outs[a].at[peer], **common))
    return local, sends, recvs


def _exchange_start(ins, outs, sems, modes):
    local, sends, _ = _exchange_copies(ins, outs, sems, modes)
    for cp in local + sends:
        cp.start()


def _exchange_wait(ins, outs, sems, modes):
    local, sends, recvs = _exchange_copies(ins, outs, sems, modes)
    for cp in recvs:
        cp.wait_recv()
    for cp in sends:
        cp.wait_send()
    for cp in local:
        cp.wait()


def _exchange_shapes(arrays, modes):
    return [jax.ShapeDtypeStruct((N_DEV,) + a.shape if mode == "gather" else a.shape, a.dtype)
            for a, mode in zip(arrays, modes)]


def _exchange_sems(na):
    return [pltpu.SemaphoreType.DMA((na, N_DEV - 1)), pltpu.SemaphoreType.DMA((na, N_DEV - 1)),
            pltpu.SemaphoreType.DMA((na,))]


ANY_SPACE = pl.BlockSpec(memory_space=pl.ANY)


def _exchange(arrays, modes, name):
    na = len(arrays)

    def body(*refs):
        ins, outs, sems = refs[:na], refs[na:2 * na], refs[2 * na:]
        _exchange_start(ins, outs, sems, modes)
        _exchange_wait(ins, outs, sems, modes)

    return pl.pallas_call(
        body, name=name, out_shape=_exchange_shapes(arrays, modes),
        in_specs=[ANY_SPACE] * na, out_specs=[ANY_SPACE] * na, scratch_shapes=_exchange_sems(na),
        compiler_params=pltpu.CompilerParams(has_side_effects=True),
    )(*arrays)


def _gather_two_level(arrays, name):
    na = len(arrays)

    def body(*refs):
        ins, outs = refs[:na], refs[na:2 * na]
        send_sems, recv_sems, local_sems = refs[2 * na:]
        x, y, c = lax.axis_index("x"), lax.axis_index("y"), lax.axis_index("c")
        me, sibling = (x, y, c), (x, y, 1 - c)
        chips = [(1 - x, y), (x, 1 - y), (1 - x, 1 - y)]

        def copy(a, k, block, to, src=None):
            slot = outs[a].at[4 * block[0] + 2 * block[1] + block[2]]
            return pltpu.make_async_remote_copy(
                src_ref=slot if src is None else src, dst_ref=slot, send_sem=send_sems.at[a, k],
                recv_sem=recv_sems.at[a, k], device_id=to, device_id_type=pl.DeviceIdType.MESH)

        local = [pltpu.make_async_copy(ins[a], outs[a].at[4 * x + 2 * y + c], local_sems.at[a]) for a in range(na)]
        first = []
        for a in range(na):
            first.append(copy(a, 0, me, sibling, src=ins[a]))
            first += [copy(a, 1 + j, me, (*chip, c), src=ins[a]) for j, chip in enumerate(chips)]
        for cp in local + first:
            cp.start()
        passed = []
        for j, chip in enumerate(chips):
            for a in range(na):
                copy(a, 1 + j, (*chip, c), me).wait_recv()
                passed.append(copy(a, 4 + j, (*chip, c), sibling))
                passed[-1].start()
        for a in range(na):
            copy(a, 0, sibling, me).wait_recv()
            for j, chip in enumerate(chips):
                copy(a, 4 + j, (*chip, 1 - c), me).wait_recv()
        for cp in first + passed:
            cp.wait_send()
        for cp in local:
            cp.wait()

    return pl.pallas_call(
        body, name=name, out_shape=_exchange_shapes(arrays, ["gather"] * na),
        in_specs=[ANY_SPACE] * na, out_specs=[ANY_SPACE] * na, scratch_shapes=_exchange_sems(na),
        compiler_params=pltpu.CompilerParams(has_side_effects=True),
    )(*arrays)


def _hosting_call(body, name, steps, in_specs, out_specs, out_shape, scratch_shapes, args, arrays, modes):
    n_in, n_out, n_scr, na = len(in_specs), len(out_specs), len(scratch_shapes), len(arrays)
    grid = steps if isinstance(steps, tuple) else (steps,)

    def hosting_body(*refs):
        cuts = [0]
        for n in (n_in, na, n_out, na, n_scr, 3):
            cuts.append(cuts[-1] + n)
        ins, x_ins, outs, x_outs, scr, sems = (refs[cuts[p]:cuts[p + 1]] for p in range(6))
        first, last = True, True
        for axis, n in enumerate(grid):
            first = first & (pl.program_id(axis) == 0)
            last = last & (pl.program_id(axis) == n - 1)

        @pl.when(first)
        def _():
            _exchange_start(x_ins, x_outs, sems, modes)

        body(*ins, *outs, *scr)

        @pl.when(last)
        def _():
            _exchange_wait(x_ins, x_outs, sems, modes)

    res = pl.pallas_call(
        hosting_body, name=name, grid=grid,
        in_specs=list(in_specs) + [ANY_SPACE] * na, out_specs=list(out_specs) + [ANY_SPACE] * na,
        out_shape=list(out_shape) + _exchange_shapes(arrays, modes),
        scratch_shapes=list(scratch_shapes) + _exchange_sems(na),
        compiler_params=_params(("arbitrary",) * len(grid)),
    )(*args, *arrays)
    return res[:n_out], res[n_out:]


def _frame_rows(src_hbm, buf, sem, i, steps, tm):
    def first():
        return pltpu.make_async_copy(src_hbm.at[pl.ds(0, tm - BLOCK)], buf.at[0, pl.ds(BLOCK, tm - BLOCK)], sem.at[0])

    def later(t, slot):
        return pltpu.make_async_copy(src_hbm.at[pl.ds(pl.multiple_of(t * tm - BLOCK, SUBLANES), tm)], buf.at[slot], sem.at[slot])

    slot = i % 2

    @pl.when(i == 0)
    def _():
        first().start()

    @pl.when(i + 1 < steps)
    def _():
        later(i + 1, 1 - slot).start()

    @pl.when(i == 0)
    def _():
        first().wait()

    @pl.when(i > 0)
    def _():
        later(i, slot).wait()

    return slot


def _frame_scratch(tm):
    return [pltpu.VMEM((2, tm, D_MODEL), F32), pltpu.SemaphoreType.DMA((2,))]


def _h0_tile(head_ref, x_hbm, buf, sem, i, steps, tm):
    slot = _frame_rows(x_hbm, buf, sem, i, steps, tm)

    @pl.when(i == 0)
    def _():
        buf[0, 0:BLOCK, :] = head_ref[...]

    return buf[slot]


N_BIAS = 3


def _attn_bias():
    key = np.arange(2 * BLOCK)[:, None]
    r = np.arange(GQA_GROUP * BLOCK)[None, :] % BLOCK
    band = (key > r) & (key <= r + BLOCK)
    out = [np.where(band & ((n - 1) * BLOCK + key >= PAD_ROWS), 0.0, NEG) for n in range(N_BIAS)]
    return jnp.asarray(np.stack(out), F32)


def _attn_probs(k2, q4, bias, sink_row):
    s = _mm_nt(k2, q4) * (HEAD_DIM ** -0.5) + bias
    m = jnp.maximum(jnp.max(s, axis=0, keepdims=True), sink_row)
    p = jnp.exp(s - m)
    es = jnp.exp(sink_row - m)
    inv = 1.0 / (jnp.sum(p, axis=0, keepdims=True) + es)
    return p * inv, es * inv


def _heads(ref, rows, first, count):
    return jnp.concatenate([ref[rows, (first + g) * HEAD_DIM:(first + g + 1) * HEAD_DIM] for g in range(count)], axis=0)


def _keys_of_block(prev_ref, cur_ref, b, kv):
    sl = slice(kv * HEAD_DIM, (kv + 1) * HEAD_DIM)
    before = prev_ref[:, sl] if b == 0 else cur_ref[(b - 1) * BLOCK:b * BLOCK, sl]
    return jnp.concatenate([before, cur_ref[b * BLOCK:(b + 1) * BLOCK, sl]], axis=0)


def _bias_of_block(bias_ref, block):
    return bias_ref[jnp.minimum(block, N_BIAS - 1)]


def _sink_row(sink_ref, kv):
    g = lax.broadcasted_iota(jnp.int32, (1, GQA_GROUP * BLOCK), 1) // BLOCK
    row = jnp.full((1, GQA_GROUP * BLOCK), sink_ref[0, kv * GQA_GROUP], F32)
    for i in range(1, GQA_GROUP):
        row = jnp.where(g == i, sink_ref[0, kv * GQA_GROUP + i], row)
    return row


def _from_head_major(pieces):
    return jnp.concatenate(pieces, axis=0).T


def _attn_specs(tm, tile_of):
    nbt = tm // BLOCK
    k_col, v_col = ATTN_WIDTH // KV_WIDTH, ATTN_WIDTH // KV_WIDTH + 1
    before = lambda i: jnp.maximum(tile_of(i) * nbt - 1, 0)
    return [pl.BlockSpec((tm, ATTN_WIDTH), lambda i: (tile_of(i), 0)),
            pl.BlockSpec((BLOCK, KV_WIDTH), lambda i: (before(i), k_col)),
            pl.BlockSpec((tm, KV_WIDTH), lambda i: (tile_of(i), k_col)),
            pl.BlockSpec((BLOCK, KV_WIDTH), lambda i: (before(i), v_col)),
            pl.BlockSpec((tm, KV_WIDTH), lambda i: (tile_of(i), v_col))]


def _in_proj_attn_fwd(head, x, g1, w_in, sinks, bias, carried, modes):
    rows = BLOCK + x.shape[0]
    tm = _row_tile(rows)
    steps, nbt = rows // tm, tm // BLOCK

    def body(head_ref, g_ref, w_ref, sink_ref, bias_ref, x_hbm, qkv_ref, zrec_ref, u_ref, o_ref, buf, sem, kv_before):
        i = pl.program_id(0)
        h = _h0_tile(head_ref, x_hbm, buf, sem, i, steps, tm)
        u, _, _ = _rms_fwd(h, g_ref[...])
        u = u.astype(BF16)
        u_ref[...] = u
        z = _mm(u, w_ref[...])
        qkv_ref[...] = z[:, :QKV_WIDTH].astype(BF16)
        zrec_ref[...] = z[:, QKV_WIDTH:]

        @pl.when(i == 0)
        def _():
            kv_before[...] = jnp.zeros_like(kv_before)

        kc_ref, vc_ref = (qkv_ref.at[:, pl.ds(ATTN_WIDTH + c * KV_WIDTH, KV_WIDTH)] for c in range(2))
        kp_ref, vp_ref = (kv_before.at[:, pl.ds(c * KV_WIDTH, KV_WIDTH)] for c in range(2))
        for b in range(nbt):
            blk = slice(b * BLOCK, (b + 1) * BLOCK)
            bias_t = _bias_of_block(bias_ref, i * nbt + b)
            pieces = []
            for kv in range(KV_HEADS):
                k2 = _keys_of_block(kp_ref, kc_ref, b, kv)
                v2 = _keys_of_block(vp_ref, vc_ref, b, kv)
                q4 = _heads(qkv_ref, blk, kv * GQA_GROUP, GQA_GROUP)
                pn, _ = _attn_probs(k2, q4, bias_t, _sink_row(sink_ref, kv))
                ot = _mm_tn(v2, pn.astype(BF16))
                pieces += [ot[:, g * BLOCK:(g + 1) * BLOCK] for g in range(GQA_GROUP)]
            o_ref[blk, :] = _from_head_major(pieces).astype(BF16)
        kv_before[...] = qkv_ref[tm - BLOCK:tm, ATTN_WIDTH:]

    wide = pl.BlockSpec((tm, D_MODEL), lambda i: (i, 0))
    return _hosting_call(
        body, "in_proj_attn_fwd", steps,
        [_full((BLOCK, D_MODEL)), _full((1, D_MODEL)), _resident((D_MODEL, IN_WIDTH)), pl.BlockSpec(memory_space=pltpu.SMEM),
         _resident((N_BIAS, 2 * BLOCK, GQA_GROUP * BLOCK)), ANY_SPACE],
        [pl.BlockSpec((tm, QKV_WIDTH), lambda i: (i, 0)), pl.BlockSpec((tm, 2 * LRU_WIDTH), lambda i: (i, 0)), wide,
         pl.BlockSpec((tm, ATTN_WIDTH), lambda i: (i, 0))],
        [jax.ShapeDtypeStruct((rows, QKV_WIDTH), BF16), jax.ShapeDtypeStruct((rows, 2 * LRU_WIDTH), F32),
         jax.ShapeDtypeStruct((rows, D_MODEL), BF16), jax.ShapeDtypeStruct((rows, ATTN_WIDTH), BF16)],
        _frame_scratch(tm) + [pltpu.VMEM((BLOCK, 2 * KV_WIDTH), BF16)],
        (head, g1, w_in, sinks, bias, x), carried, modes)


def _conv_taps(xbuf, tm):
    return [xbuf[pl.ds(SUBLANES - (CONV_WIDTH - 1 - j), tm), :] for j in range(CONV_WIDTH)]


def _lru_halves(xc):
    return [xc[:, h * LRU_HALF:(h + 1) * LRU_HALF].astype(BF16) for h in range(2)]


def _lru_gates(xc, wa_ref, ba_ref, wx_ref, bx_ref, lam_ref):
    halves = _lru_halves(xc)
    gate_r = jnp.concatenate([_mm(halves[h], wa_ref[h]) for h in range(2)], axis=1) + ba_ref[...]
    gate_i = jnp.concatenate([_mm(halves[h], wx_ref[h]) for h in range(2)], axis=1) + bx_ref[...]
    r = _sigmoid(gate_r)
    ig = _sigmoid(gate_i)
    log_a = (-LRU_C) * r * _softplus(-lam_ref[...])
    a = jnp.exp(log_a)
    mult, _ = _sqrt_pos(_one_minus_sq_exp(log_a, a))
    return r, ig, a, mult


KEPT_XC, KEPT_A, KEPT_MULT, KEPT_R, KEPT_I, N_KEPT = 0, 1, 2, 3, 4, 5


def _scan_tile(a_ref, u_ref, out_ref, carry, tm):
    row = lax.broadcasted_iota(jnp.int32, (SUBLANES, LRU_WIDTH), 0)

    def step(j, before):
        o = pl.multiple_of(j * SUBLANES, SUBLANES)
        a = a_ref[pl.ds(o, SUBLANES), :]
        u = u_ref[pl.ds(o, SUBLANES), :]
        for s in (1, 2, 4):
            keep = row >= s
            u = jnp.where(keep, a * pltpu.roll(u, s, 0) + u, u)
            a = jnp.where(keep, a * pltpu.roll(a, s, 0), a)
        out = a * before + u
        out_ref[pl.ds(o, SUBLANES), :] = out
        return out[SUBLANES - 1:SUBLANES, :]

    return lax.fori_loop(0, tm // SUBLANES, step, carry)


def _rec_fwd(zrec, conv_w, conv_b, wa_bd, b_a, wx_bd, b_x, lam, carried, modes):
    rows = zrec.shape[0]
    tm = _row_tile(rows)

    def body(xr_ref, yr_ref, cw_ref, cb_ref, wa_ref, ba_ref, wx_ref, bx_ref, lam_ref, rec_ref, h_ref, kept_ref,
             xbuf, a_s, u_s, carry):
        i = pl.program_id(0)

        @pl.when(i == 0)
        def _():
            xbuf[0:SUBLANES, :] = jnp.zeros((SUBLANES, LRU_WIDTH), F32)
            carry[...] = jnp.zeros_like(carry)

        @pl.when(i > 0)
        def _():
            xbuf[0:SUBLANES, :] = xbuf[tm:tm + SUBLANES, :]

        xbuf[SUBLANES:SUBLANES + tm, :] = xr_ref[...]
        taps = _conv_taps(xbuf, tm)
        xc = cb_ref[...] + sum(cw_ref[j:j + 1, :] * taps[j] for j in range(CONV_WIDTH))
        r, ig, a, mult = _lru_gates(xc, wa_ref, ba_ref, wx_ref, bx_ref, lam_ref)
        for k, val in ((KEPT_XC, xc), (KEPT_A, a), (KEPT_MULT, mult), (KEPT_R, r), (KEPT_I, ig)):
            kept_ref[:, k * LRU_WIDTH:(k + 1) * LRU_WIDTH] = val
        grow = i * tm + lax.broadcasted_iota(jnp.int32, (tm, LRU_WIDTH), 0)
        a_s[...] = a
        u_s[...] = jnp.where(grow >= PAD_ROWS, mult * (ig * xc), 0.0)
        carry[0:1, :] = _scan_tile(a_s, u_s, h_ref, carry[0:1, :], tm)
        gel, _ = _gelu(yr_ref[...])
        rec_ref[...] = (gel * h_ref[...]).astype(BF16)

    vec = _full((1, LRU_WIDTH))
    bd = _full((2, LRU_HALF, LRU_HALF))
    return _hosting_call(
        body, "rec_fwd", rows // tm,
        [pl.BlockSpec((tm, LRU_WIDTH), lambda i: (i, 0)), pl.BlockSpec((tm, LRU_WIDTH), lambda i: (i, 1)),
         _full((CONV_WIDTH, LRU_WIDTH)), vec, bd, vec, bd, vec, vec],
        [pl.BlockSpec((tm, LRU_WIDTH), lambda i: (i, 0))] * 2 + [pl.BlockSpec((tm, N_KEPT * LRU_WIDTH), lambda i: (i, 0))],
        [jax.ShapeDtypeStruct((rows, LRU_WIDTH), BF16), jax.ShapeDtypeStruct((rows, LRU_WIDTH), F32),
         jax.ShapeDtypeStruct((rows, N_KEPT * LRU_WIDTH), F32)],
        [pltpu.VMEM((tm + SUBLANES, LRU_WIDTH), F32), pltpu.VMEM((tm, LRU_WIDTH), F32),
         pltpu.VMEM((tm, LRU_WIDTH), F32), pltpu.VMEM((SUBLANES, LRU_WIDTH), F32)],
        (zrec, zrec, conv_w, conv_b, wa_bd, b_a, wx_bd, b_x, lam), carried, modes)


FF_COLS = 1024
FF_HALF = FF_CHUNK // 2


def _hidden_at(d, half):
    return half * (D_FF // 2) + d * FF_HALF


def _mix_and_ffn_up(attn, rec, w_out, head, x, g2, g3, w1_halves, carried, modes):
    rows = attn.shape[0]
    tm = _row_tile(rows)
    steps = rows // tm

    def body(attn_ref, rec_ref, w_ref, head_ref, g2_ref, g3_ref, wa_ref, wb_ref, x_hbm,
             mix_ref, h1_ref, act_ref, u_ref, buf, sem):
        h0 = _h0_tile(head_ref, x_hbm, buf, sem, pl.program_id(0), steps, tm)
        mix = _mm(attn_ref[...], w_ref[0:ATTN_WIDTH, :]) + _mm(rec_ref[...], w_ref[ATTN_WIDTH:, :])
        y, _, _ = _rms_fwd(mix, g2_ref[...])
        mix_ref[...] = mix
        h1_ref[...] = h0 + y
        u, _, _ = _rms_fwd(h1_ref[...], g3_ref[...])
        u = u.astype(BF16)
        u_ref[...] = u
        for half, w1_ref in enumerate((wa_ref, wb_ref)):
            for d in range(N_DEV):
                c = _hidden_at(d, half)
                a1 = jnp.maximum(_mm(u, w1_ref[d]), 0.0)
                act_ref[:, c:c + FF_HALF] = (a1 * a1).astype(BF16)

    half_in = pl.BlockSpec((tm, ATTN_WIDTH), lambda i: (i, 0))
    wide = pl.BlockSpec((tm, D_MODEL), lambda i: (i, 0))
    return _hosting_call(
        body, "mix_and_ffn_up", steps,
        [half_in, half_in, _resident((D_MODEL, D_MODEL)), _full((BLOCK, D_MODEL)), _full((1, D_MODEL)), _full((1, D_MODEL))]
        + [_resident((N_DEV, D_MODEL, FF_HALF))] * 2 + [ANY_SPACE],
        [wide, wide, pl.BlockSpec((tm, D_FF), lambda i: (i, 0)), wide],
        [jax.ShapeDtypeStruct((rows, D_MODEL), F32)] * 2
        + [jax.ShapeDtypeStruct((rows, D_FF), BF16), jax.ShapeDtypeStruct((rows, D_MODEL), BF16)],
        _frame_scratch(tm), (attn, rec, w_out, head, g2, g3, *w1_halves, x), carried, modes)


def _ffn_down_loss(act, w2_halves, h1, target, g4, carried, modes):
    rows = h1.shape[0]
    tm = _row_tile(rows)
    steps = rows // tm
    kh = D_FF // 2

    def body(act_ref, wa_ref, wb_ref, h_ref, g_ref, t_hbm, dy_ref, df_ref, dg_ref, loss_ref, buf, sem):
        i = pl.program_id(0)
        slot = _frame_rows(t_hbm, buf, sem, i, steps, tm)

        @pl.when(i == 0)
        def _():
            dg_ref[...] = jnp.zeros_like(dg_ref)
            loss_ref[...] = jnp.zeros_like(loss_ref)
            buf[0, 0:BLOCK, :] = jnp.zeros((BLOCK, D_MODEL), F32)

        g = g_ref[...]
        f = _mm(act_ref[:, :kh], wa_ref[...]) + _mm(act_ref[:, kh:], wb_ref[...])
        y, fhat, rstd = _rms_fwd(f, g)
        grow = i * tm + lax.broadcasted_iota(jnp.int32, (tm, D_MODEL), 0)
        err = jnp.where(grow >= BLOCK, h_ref[...] + y - buf[slot], 0.0)
        loss_ref[...] += (0.5 / D_MODEL) * jnp.sum(err * err)
        dy = err * (1.0 / D_MODEL)
        df, dg = _rms_bwd(dy, fhat, rstd, g)
        dy_ref[...] = dy
        df_ref[...] = df.astype(BF16)
        dg_ref[...] += dg

    wide = pl.BlockSpec((tm, D_MODEL), lambda i: (i, 0))
    return _hosting_call(
        body, "ffn_down_loss", steps,
        [pl.BlockSpec((tm, D_FF), lambda i: (i, 0)), _resident((kh, D_MODEL)), _resident((kh, D_MODEL)), wide,
         _full((1, D_MODEL)), ANY_SPACE],
        [wide, wide, _full((1, D_MODEL)), _full((SUBLANES, LANES))],
        [jax.ShapeDtypeStruct((rows, D_MODEL), F32), jax.ShapeDtypeStruct((rows, D_MODEL), BF16),
         jax.ShapeDtypeStruct((1, D_MODEL), F32), jax.ShapeDtypeStruct((SUBLANES, LANES), F32)],
        _frame_scratch(tm), (act, *w2_halves, h1, g4, target), carried, modes)


def _ffn_bwd_act(df, w2t_halves, act, carried, modes):
    rows = df.shape[0]
    tm = _row_tile(rows)

    def body(df_ref, wa_ref, wb_ref, act_ref, da_ref):
        df_t = df_ref[...]
        for half, w_ref in enumerate((wa_ref, wb_ref)):
            for d in range(N_DEV):
                cols = slice(_hidden_at(d, half), _hidden_at(d, half) + FF_HALF)
                dact = _mm(df_t, w_ref[d])
                relu_a1, _ = _sqrt_pos(act_ref[:, cols].astype(F32))
                da_ref[:, cols] = (dact * (2.0 * relu_a1)).astype(BF16)

    hidden = pl.BlockSpec((tm, D_FF), lambda i: (i, 0))
    return _hosting_call(
        body, "ffn_bwd_act", rows // tm,
        [pl.BlockSpec((tm, D_MODEL), lambda i: (i, 0))] + [_resident((N_DEV, D_MODEL, FF_HALF))] * 2 + [hidden],
        [hidden],
        [jax.ShapeDtypeStruct((rows, D_FF), BF16)],
        [], (df, *w2t_halves, act), carried, modes)


def _ffn_bwd_x(da, w1t_halves, h1, dy, g3, carried, modes):
    rows = h1.shape[0]
    tm = _row_tile(rows)
    kh = D_FF // 2

    def body(da_ref, wa_ref, wb_ref, h_ref, dy_ref, g_ref, dh_ref, dg_ref):
        @pl.when(pl.program_id(0) == 0)
        def _():
            dg_ref[...] = jnp.zeros_like(dg_ref)

        g = g_ref[...]
        _, xhat, rstd = _rms_fwd(h_ref[...], g)
        du = _mm(da_ref[:, :kh], wa_ref[...]) + _mm(da_ref[:, kh:], wb_ref[...])
        dx, dg = _rms_bwd(du, xhat, rstd, g)
        dh_ref[...] = dy_ref[...] + dx
        dg_ref[...] += dg

    wide = pl.BlockSpec((tm, D_MODEL), lambda i: (i, 0))
    return _hosting_call(
        body, "ffn_bwd_x", rows // tm,
        [pl.BlockSpec((tm, D_FF), lambda i: (i, 0)), _resident((kh, D_MODEL)), _resident((kh, D_MODEL)), wide, wide,
         _full((1, D_MODEL))],
        [wide, _full((1, D_MODEL))],
        [jax.ShapeDtypeStruct((rows, D_MODEL), F32), jax.ShapeDtypeStruct((1, D_MODEL), F32)],
        [], (da, *w1t_halves, h1, dy, g3), carried, modes)


def _ffn_bwd_weights(u2, da, act, df, carried, modes):
    rows = u2.shape[0]
    tb = _big_tile(rows)
    steps = rows // tb
    per = FF_COLS // FF_HALF

    def body(u_ref, da_ref, act_ref, df_ref, dw1_ref, dw2_ref, acc1, acc2):
        i = pl.program_id(1)

        @pl.when(i == 0)
        def _():
            acc1[...] = jnp.zeros_like(acc1)
            acc2[...] = jnp.zeros_like(acc2)

        acc1[...] += _mm_tn(u_ref[...], da_ref[...])
        acc2[...] += _mm_tn(act_ref[...], df_ref[...])

        @pl.when(i == steps - 1)
        def _():
            for p in range(per):
                c = p * FF_HALF
                dw1_ref[p] = acc1[:, c:c + FF_HALF].astype(BF16)
                dw2_ref[p] = acc2[c:c + FF_HALF, :].astype(BF16)

    wide = pl.BlockSpec((tb, D_MODEL), lambda j, i: (i, 0))
    chunk = pl.BlockSpec((tb, FF_COLS), lambda j, i: (i, j))
    return _hosting_call(
        body, "ffn_bwd_weights", (D_FF // FF_COLS, steps),
        [wide, chunk, chunk, wide],
        [pl.BlockSpec((None, per, D_MODEL, FF_HALF), lambda j, i: (j // 2, j % 2, 0, 0)),
         pl.BlockSpec((per, FF_HALF, D_MODEL), lambda j, i: (j % 2, j // 2, 0))],
        [jax.ShapeDtypeStruct((2, N_DEV, D_MODEL, FF_HALF), BF16), jax.ShapeDtypeStruct((N_DEV, FF_CHUNK, D_MODEL), BF16)],
        [pltpu.VMEM((D_MODEL, FF_COLS), F32), pltpu.VMEM((FF_COLS, D_MODEL), F32)],
        (u2, da, act, df), carried, modes)


def _out_proj_bwd(dh1, mix, g2, w_out_t, attn, rec, carried, modes):
    rows = dh1.shape[0]
    tm = _row_tile(rows)
    steps = rows // tm

    def body(dh_ref, mix_ref, g_ref, w_ref, attn_ref, rec_ref, dattn_ref, drec_ref, dw_ref, dg_ref, acc):
        i = pl.program_id(0)

        @pl.when(i == 0)
        def _():
            acc[...] = jnp.zeros_like(acc)
            dg_ref[...] = jnp.zeros_like(dg_ref)

        g = g_ref[...]
        _, xhat, rstd = _rms_fwd(mix_ref[...], g)
        dmix, dg = _rms_bwd(dh_ref[...], xhat, rstd, g)
        dmix = dmix.astype(BF16)
        dg_ref[...] += dg
        din = _mm(dmix, w_ref[...])
        dattn_ref[...] = din[:, :ATTN_WIDTH].astype(BF16)
        drec_ref[...] = din[:, ATTN_WIDTH:]
        acc[0:ATTN_WIDTH, :] += _mm_tn(attn_ref[...], dmix)
        acc[ATTN_WIDTH:, :] += _mm_tn(rec_ref[...], dmix)

        @pl.when(i == steps - 1)
        def _():
            dw_ref[...] = acc[...].astype(BF16)

    half = pl.BlockSpec((tm, ATTN_WIDTH), lambda i: (i, 0))
    wide = pl.BlockSpec((tm, D_MODEL), lambda i: (i, 0))
    return _hosting_call(
        body, "out_proj_bwd", steps,
        [wide, wide, _full((1, D_MODEL)), _resident((D_MODEL, D_MODEL)), half, half],
        [half, half, _full((D_MODEL, D_MODEL)), _full((1, D_MODEL))],
        [jax.ShapeDtypeStruct((rows, ATTN_WIDTH), BF16), jax.ShapeDtypeStruct((rows, LRU_WIDTH), F32),
         jax.ShapeDtypeStruct((D_MODEL, D_MODEL), BF16), jax.ShapeDtypeStruct((1, D_MODEL), F32)],
        [pltpu.VMEM((D_MODEL, D_MODEL), F32)],
        (dh1, mix, g2, w_out_t, attn, rec), carried, modes)


def _attn_bwd(qkv, dattn, sinks, bias, carried, modes):
    rows = qkv.shape[0]
    tm = _row_tile(rows)
    nbt, nt = tm // BLOCK, rows // tm

    def body(sink_ref, bias_ref, do_ref, q_ref, kp_ref, kc_ref, vp_ref, vc_ref, dq_ref, dkv_ref, dsink_ref, dk_c, dv_c):
        i = pl.program_id(0)

        @pl.when(i == 0)
        def _():
            dk_c[...] = jnp.zeros_like(dk_c)
            dv_c[...] = jnp.zeros_like(dv_c)
            dsink_ref[...] = jnp.zeros_like(dsink_ref)

        @pl.when(i < nt)
        def _():
            dk_late, dv_late = dk_c[...], dv_c[...]
            dsink_rows = [jnp.zeros((1, LANES), F32)] * ATTN_HEADS
            for b in range(nbt):
                blk = slice(b * BLOCK, (b + 1) * BLOCK)
                bias_t = _bias_of_block(bias_ref, i * nbt + b)
                dq_parts, dk_parts, dv_parts = [], [], []
                for kv in range(KV_HEADS):
                    k2 = _keys_of_block(kp_ref, kc_ref, b, kv)
                    v2 = _keys_of_block(vp_ref, vc_ref, b, kv)
                    q4 = _heads(q_ref, blk, kv * GQA_GROUP, GQA_GROUP)
                    do4 = _heads(do_ref, blk, kv * GQA_GROUP, GQA_GROUP)
                    pn, psink = _attn_probs(k2, q4, bias_t, _sink_row(sink_ref, kv))
                    dpn = _mm_nt(v2, do4)
                    delta = jnp.sum(pn * dpn, axis=0, keepdims=True)
                    ds = ((pn * (dpn - delta)) * (HEAD_DIM ** -0.5)).astype(BF16)
                    dqt = _mm_tn(k2, ds)
                    dq_parts += [dqt[:, g * BLOCK:(g + 1) * BLOCK] for g in range(GQA_GROUP)]
                    dk_parts.append(_mm(ds, q4))
                    dv_parts.append(_mm(pn.astype(BF16), do4))
                    sd = psink * delta
                    for g in range(GQA_GROUP):
                        h = kv * GQA_GROUP + g
                        dsink_rows[h] = dsink_rows[h] - jnp.sum(sd[:, g * BLOCK:(g + 1) * BLOCK])
                dq_ref[blk, :] = _from_head_major(dq_parts).astype(BF16)
                dk2 = jnp.concatenate(dk_parts, axis=1)
                dv2 = jnp.concatenate(dv_parts, axis=1)
                dkv_ref[blk, 0:KV_WIDTH] = (dk_late + dk2[0:BLOCK]).astype(BF16)
                dkv_ref[blk, KV_WIDTH:] = (dv_late + dv2[0:BLOCK]).astype(BF16)
                dk_late, dv_late = dk2[BLOCK:], dv2[BLOCK:]
            dk_c[...] = dk_late
            dv_c[...] = dv_late
            dsink_ref[...] += jnp.concatenate(dsink_rows, axis=0)

        @pl.when(i == nt)
        def _():
            dkv_ref[...] = jnp.zeros_like(dkv_ref)
            dkv_ref[0:BLOCK, 0:KV_WIDTH] = dk_c[...].astype(BF16)
            dkv_ref[0:BLOCK, KV_WIDTH:] = dv_c[...].astype(BF16)

    tile_of = lambda i: jnp.minimum(i, nt - 1)
    tile = pl.BlockSpec((tm, ATTN_WIDTH), lambda i: (tile_of(i), 0))
    return _hosting_call(
        body, "attn_bwd", nt + 1,
        [pl.BlockSpec(memory_space=pltpu.SMEM), _resident((N_BIAS, 2 * BLOCK, GQA_GROUP * BLOCK)), tile]
        + _attn_specs(tm, tile_of),
        [tile, pl.BlockSpec((tm, 2 * KV_WIDTH), lambda i: (i, 0)), _full((ATTN_HEADS, LANES))],
        [jax.ShapeDtypeStruct((rows, ATTN_WIDTH), BF16), jax.ShapeDtypeStruct((rows + tm, 2 * KV_WIDTH), BF16),
         jax.ShapeDtypeStruct((ATTN_HEADS, LANES), F32)],
        [pltpu.VMEM((BLOCK, KV_WIDTH), F32), pltpu.VMEM((BLOCK, KV_WIDTH), F32)],
        (sinks, bias, dattn, qkv, qkv, qkv, qkv, qkv), carried, modes)


ROW_CONV_B, ROW_B_A, ROW_B_X, ROW_LAMBDA = 4, 5, 6, 7


def _rec_bwd(drec, zrec, h, kept, conv_w, wa_bd, wx_bd, lam, carried, modes):
    rows = zrec.shape[0]
    tm = _rec_tile(rows)
    nt = rows // tm
    per = tm // SUBLANES

    def body(drec_ref, xr_ref, yr_ref, h_ref, xc_ref, a_ref, mult_ref, r_ref, ig_ref, hhalo_ref, cw_ref, wa_ref, wx_ref,
             lam_ref, drz_ref, small_ref, dwa_ref, dwx_ref, hbuf, dbuf, dgr_s, dgi_s, dyr_s, carry):
        s = pl.program_id(0)
        i = nt - 1 - s

        @pl.when(s == 0)
        def _():
            small_ref[...] = jnp.zeros_like(small_ref)
            dwa_ref[...] = jnp.zeros_like(dwa_ref)
            dwx_ref[...] = jnp.zeros_like(dwx_ref)
            carry[...] = jnp.zeros_like(carry)
            dbuf[tm:tm + SUBLANES, :] = jnp.zeros((SUBLANES, LRU_WIDTH), F32)

        hbuf[0:SUBLANES, :] = jnp.where(i == 0, 0.0, hhalo_ref[...])
        hbuf[SUBLANES:SUBLANES + tm, :] = h_ref[...]

        row = lax.broadcasted_iota(jnp.int32, (SUBLANES, LRU_WIDTH), 0)
        log_a_scale = (-LRU_C) * _softplus(-lam_ref[...])
        zeros = jnp.zeros((SUBLANES, LRU_WIDTH), F32)

        def group(k, state):
            g_later, a_later, sum_dgr, sum_dgi, sum_lam = state
            o = pl.multiple_of((per - 1 - k) * SUBLANES, SUBLANES)
            rows8 = pl.ds(o, SUBLANES)
            yr, drec_t, h_t, a = yr_ref[rows8, :], drec_ref[rows8, :], h_ref[rows8, :], a_ref[rows8, :]
            gel, t = _gelu(yr)
            dyr_s[rows8, :] = drec_t * h_t * _gelu_grad(yr, t)
            u = drec_t * gel
            coef = jnp.where(row == SUBLANES - 1, a_later, pltpu.roll(a, SUBLANES - 1, 0))
            for sft in (1, 2, 4):
                keep = row < SUBLANES - sft
                u = jnp.where(keep, coef * pltpu.roll(u, SUBLANES - sft, 0) + u, u)
                coef = jnp.where(keep, coef * pltpu.roll(coef, SUBLANES - sft, 0), coef)
            g = coef * g_later + u
            du = jnp.where(i * tm + o + row >= PAD_ROWS, g, 0.0)
            h_before = jnp.where(row == 0, hbuf[rows8, :][SUBLANES - 1:SUBLANES, :], pltpu.roll(h_t, 1, 0))
            xc, mult, r, ig = xc_ref[rows8, :], mult_ref[rows8, :], r_ref[rows8, :], ig_ref[rows8, :]
            dbuf[rows8, :] = du * (mult * ig)
            dgi = (du * (mult * xc)) * (ig * (1.0 - ig))
            dgi_s[rows8, :] = dgi
            dlog_a = (g * h_before) * a - (du * (ig * xc)) * (a * a * pl.reciprocal(mult, approx=True))
            dgr = (dlog_a * log_a_scale) * (r * (1.0 - r))
            dgr_s[rows8, :] = dgr
            return g[0:1, :], a[0:1, :], sum_dgr + dgr, sum_dgi + dgi, sum_lam + dlog_a * r

        state = lax.fori_loop(0, per, group, (carry[0:1, :], carry[1:2, :], zeros, zeros, zeros))
        carry[0:1, :], carry[1:2, :] = state[0], state[1]
        sum_dgr, sum_dgi, sum_lam = (jnp.sum(v, axis=0, keepdims=True) for v in state[2:])
        dlam = sum_lam * (LRU_C * _sigmoid(-lam_ref[...]))

        dgr_b = [dgr_s[:, hh * LRU_HALF:(hh + 1) * LRU_HALF].astype(BF16) for hh in range(2)]
        dgi_b = [dgi_s[:, hh * LRU_HALF:(hh + 1) * LRU_HALF].astype(BF16) for hh in range(2)]
        halves = _lru_halves(xc_ref[...])
        for hh in range(2):
            dwa_ref[hh] += _mm_tn(halves[hh], dgr_b[hh])
            dwx_ref[hh] += _mm_tn(halves[hh], dgi_b[hh])
        dxc = dbuf[0:tm, :] + jnp.concatenate(
            [_mm_nt(dgr_b[hh], wa_ref[hh]) + _mm_nt(dgi_b[hh], wx_ref[hh]) for hh in range(2)], axis=1)

        dbuf[0:tm, :] = dxc
        sum_dxc = jnp.sum(dxc, axis=0, keepdims=True)
        ahead = [dbuf[pl.ds(CONV_WIDTH - 1 - j, tm), :] for j in range(CONV_WIDTH)]
        drz_ref[:, 0:LRU_WIDTH] = sum(cw_ref[j:j + 1, :] * ahead[j] for j in range(CONV_WIDTH)).astype(BF16)
        drz_ref[:, LRU_WIDTH:] = dyr_s[...].astype(BF16)
        upd = [jnp.sum(xr_ref[...] * ahead[j], axis=0, keepdims=True) for j in range(CONV_WIDTH)]
        dbuf[tm:tm + SUBLANES, :] = dbuf[0:SUBLANES, :]
        small_ref[...] += jnp.concatenate(upd + [sum_dxc, sum_dgr, sum_dgi, dlam], axis=0)

    rev = lambda s: nt - 1 - s
    halo = lambda s: jnp.maximum(rev(s) * per - 1, 0)
    cols = lambda k: pl.BlockSpec((tm, LRU_WIDTH), lambda s: (rev(s), k))
    halo0 = pl.BlockSpec((SUBLANES, LRU_WIDTH), lambda s: (halo(s), 0))
    bd = _full((2, LRU_HALF, LRU_HALF))
    big = pltpu.VMEM((tm + SUBLANES, LRU_WIDTH), F32)
    tile = pltpu.VMEM((tm, LRU_WIDTH), F32)
    kept_cols = [cols(k) for k in (KEPT_XC, KEPT_A, KEPT_MULT, KEPT_R, KEPT_I)]
    return _hosting_call(
        body, "rec_bwd", nt,
        [cols(0), cols(0), cols(1), cols(0)] + kept_cols
        + [halo0, _full((CONV_WIDTH, LRU_WIDTH)), bd, bd, _full((1, LRU_WIDTH))],
        [pl.BlockSpec((tm, 2 * LRU_WIDTH), lambda s: (rev(s), 0)), _full((SUBLANES, LRU_WIDTH)), bd, bd],
        [jax.ShapeDtypeStruct((rows, 2 * LRU_WIDTH), BF16), jax.ShapeDtypeStruct((SUBLANES, LRU_WIDTH), F32),
         jax.ShapeDtypeStruct((2, LRU_HALF, LRU_HALF), F32), jax.ShapeDtypeStruct((2, LRU_HALF, LRU_HALF), F32)],
        [big, big, tile, tile, tile, pltpu.VMEM((SUBLANES, LRU_WIDTH), F32)],
        (drec, zrec, zrec, h) + (kept,) * N_KEPT + (h, conv_w, wa_bd, wx_bd, lam), carried, modes)


DZ_CUTS = (0, ATTN_WIDTH, QKV_WIDTH, IN_WIDTH)


def _dz_specs(tm):
    return [pl.BlockSpec((tm, DZ_CUTS[p + 1] - DZ_CUTS[p]), lambda i: (i, 0)) for p in range(3)]


def _in_proj_bwd_x(head, x, g1, dh1, dq, dkv, drz, w_in_t, carried, modes):
    rows = dh1.shape[0]
    tm = _row_tile(rows)
    steps = rows // tm

    def body(head_ref, g_ref, dh1_ref, dq_ref, dkv_ref, drz_ref, w_ref, x_hbm, dh0_ref, dg_ref, buf, sem):
        i = pl.program_id(0)
        h0 = _h0_tile(head_ref, x_hbm, buf, sem, i, steps, tm)

        @pl.when(i == 0)
        def _():
            dg_ref[...] = jnp.zeros_like(dg_ref)

        g = g_ref[...]
        _, xhat, rstd = _rms_fwd(h0, g)
        parts = (dq_ref[...], dkv_ref[...], drz_ref[...])
        du = sum(_mm(parts[p], w_ref[DZ_CUTS[p]:DZ_CUTS[p + 1], :]) for p in range(3))
        dx, dg = _rms_bwd(du, xhat, rstd, g)
        dh0_ref[...] = dh1_ref[...] + dx
        dg_ref[...] += dg

    wide = pl.BlockSpec((tm, D_MODEL), lambda i: (i, 0))
    return _hosting_call(
        body, "in_proj_bwd_x", steps,
        [_full((BLOCK, D_MODEL)), _full((1, D_MODEL)), wide] + _dz_specs(tm) + [_resident((IN_WIDTH, D_MODEL)), ANY_SPACE],
        [wide, _full((1, D_MODEL))],
        [jax.ShapeDtypeStruct((rows, D_MODEL), F32), jax.ShapeDtypeStruct((1, D_MODEL), F32)],
        _frame_scratch(tm), (head, g1, dh1, dq, dkv, drz, w_in_t, x), carried, modes)


def _in_proj_bwd_w(u1, dq, dkv, drz, carried, modes):
    rows = u1.shape[0]
    tb = _big_tile(rows)
    steps = rows // tb

    def body(u_ref, dq_ref, dkv_ref, drz_ref, dw_ref, acc):
        i = pl.program_id(0)

        @pl.when(i == 0)
        def _():
            acc[...] = jnp.zeros_like(acc)

        u = u_ref[...]
        for p, ref in enumerate((dq_ref, dkv_ref, drz_ref)):
            acc[:, DZ_CUTS[p]:DZ_CUTS[p + 1]] += _mm_tn(u, ref[...])

        @pl.when(i == steps - 1)
        def _():
            dw_ref[...] = acc[...].astype(BF16)

    return _hosting_call(
        body, "in_proj_bwd_w", steps,
        [pl.BlockSpec((tb, D_MODEL), lambda i: (i, 0))] + _dz_specs(tb),
        [_full((D_MODEL, IN_WIDTH))],
        [jax.ShapeDtypeStruct((D_MODEL, IN_WIDTH), BF16)],
        [pltpu.VMEM((D_MODEL, IN_WIDTH), F32)], (u1, dq, dkv, drz), carried, modes)


def _adamw_math(w, m, v, g):
    nm = ADAM_B1 * m + (1.0 - ADAM_B1) * g
    nv = ADAM_B2 * v + (1.0 - ADAM_B2) * (g * g)
    m_hat = nm / (1.0 - ADAM_B1 ** ADAM_STEP)
    v_hat = nv / (1.0 - ADAM_B2 ** ADAM_STEP)
    return (-ADAM_LR) * (m_hat / (jnp.sqrt(v_hat) + ADAM_EPS) + ADAM_WD * w), nm, nv


SMALL_NAMES = ("conv_b", "b_a", "b_x", "lru_lambda", "attn_sinks", "g_post_mix", "g_pre_ffn", "g_post_ffn")
PACK_WIDTH = 1024


def _pack_rows(vals):
    assert len(SMALL_NAMES) == SUBLANES
    row = lax.broadcasted_iota(jnp.int32, (SUBLANES, PACK_WIDTH), 0)
    tile = jnp.zeros((SUBLANES, PACK_WIDTH), F32)
    for k, name in enumerate(SMALL_NAMES):
        a = vals[name].reshape(1, -1)
        tile = jnp.where(row == k, jnp.pad(a, ((0, 0), (0, PACK_WIDTH - a.shape[1]))), tile)
    return tile


def _adamw_small(weights, mom_m, mom_v, parts, loss_parts, others):
    names = list(SMALL_NAMES) + [name for name, _, _ in others]
    views = [(1, weights[name].size) for name in SMALL_NAMES] + [view for _, view, _ in others]
    n, n_pack = len(names), len(SMALL_NAMES)

    def body(*refs):
        w_refs, m_refs, v_refs = refs[:n], refs[n:2 * n], refs[2 * n:3 * n]
        p_ref, l_ref = refs[3 * n], refs[3 * n + 1]
        o_refs = refs[3 * n + 2:3 * n + 2 + len(others)]
        loss_ref, outs = refs[3 * n + 2 + len(others)], refs[3 * n + 3 + len(others):]
        for k, (_, c) in enumerate(views):
            if k < n_pack:
                g = p_ref[0, k:k + 1, 0:c]
                for s in range(1, N_DEV):
                    g = g + p_ref[s, k:k + 1, 0:c]
            else:
                g = o_refs[k - n_pack][0]
                for s in range(1, N_DEV):
                    g = g + o_refs[k - n_pack][s]
            g_ref, d_ref, nm_ref, nv_ref = outs[4 * k:4 * k + 4]
            g_ref[...] = g
            d_ref[...], nm_ref[...], nv_ref[...] = _adamw_math(w_refs[k][...], m_refs[k][...], v_refs[k][...], g)
        total = l_ref[0]
        for s in range(1, N_DEV):
            total = total + l_ref[s]
        loss_ref[...] = total

    args = [src[name].reshape(view) for src in (weights, mom_m, mom_v) for name, view in zip(names, views)]
    res = pl.pallas_call(
        body, name="adamw_small",
        out_shape=[jax.ShapeDtypeStruct(loss_parts.shape[1:], F32)]
                  + [jax.ShapeDtypeStruct(view, F32) for view in views for _ in range(4)],
        compiler_params=pltpu.CompilerParams(vmem_limit_bytes=VMEM_LIMIT),
    )(*args, parts, loss_parts, *[p for _, _, p in others])
    out = {name: tuple(t.reshape(weights[name].shape) for t in res[1 + 4 * k:5 + 4 * k]) for k, name in enumerate(names)}
    return res[0], out


def _adamw(w, m, v, parts, name):
    rows, cols = w.shape
    tr = next((t for t in (256, 128) if rows % t == 0), rows)
    parts = parts if isinstance(parts, (list, tuple)) else [parts]

    def body(w_ref, m_ref, v_ref, *refs):
        p_refs, (g_ref, d_ref, nm_ref, nv_ref) = refs[:len(parts)], refs[len(parts):]

        def total(p_ref):
            g = p_ref[0].astype(F32)
            for s in range(1, N_DEV):
                g = g + p_ref[s].astype(F32)
            return g

        g = jnp.concatenate([total(p_ref) for p_ref in p_refs], axis=1) if len(parts) > 1 else total(p_refs[0])
        g_ref[...] = g
        d_ref[...], nm_ref[...], nv_ref[...] = _adamw_math(w_ref[...], m_ref[...], v_ref[...], g)

    blk = pl.BlockSpec((tr, cols), lambda i: (i, 0))
    return pl.pallas_call(
        body, name=name, grid=(rows // tr,),
        in_specs=[blk, blk, blk] + [pl.BlockSpec((N_DEV, tr, p.shape[2]), lambda i: (0, i, 0)) for p in parts],
        out_specs=[blk] * 4,
        out_shape=[jax.ShapeDtypeStruct((rows, cols), F32)] * 4,
        compiler_params=_params(("parallel",)),
    )(w, m, v, *parts)


def _cols_from_shards(g):
    return jnp.transpose(g, (1, 0, 2)).reshape(g.shape[1], N_DEV * g.shape[2])


def _cols_to_shards(a):
    r, c = a.shape
    return jnp.transpose(a.reshape(r, N_DEV, c // N_DEV), (1, 0, 2))


def _block_diag(w):
    per = LRU_HALF // LRU_BLOCK
    w = w.reshape(2, per, LRU_BLOCK, LRU_BLOCK)
    eye = jnp.eye(per, dtype=w.dtype)
    return (w[:, :, :, None, :] * eye[None, :, None, :, None]).reshape(2, LRU_HALF, LRU_HALF)


def _block_diag_extract(t):
    per = LRU_HALF // LRU_BLOCK
    t = t.reshape(2, per, LRU_BLOCK, per, LRU_BLOCK)
    return jnp.stack([t[:, b, :, b, :] for b in range(per)], axis=1).reshape(LRU_BLOCKS, LRU_BLOCK, LRU_BLOCK)


def kernel(x, meta_tokens, g_pre_mix, w_in, conv_w, conv_b, w_a, b_a, w_x, b_x, lru_lambda, attn_sinks, w_out, g_post_mix, g_pre_ffn, w_ff1, w_ff2, g_post_ffn, loss_target, m_meta_tokens, m_g_pre_mix, m_w_in, m_conv_w, m_conv_b, m_w_a, m_b_a, m_w_x, m_b_x, m_lru_lambda, m_attn_sinks, m_w_out, m_g_post_mix, m_g_pre_ffn, m_w_ff1, m_w_ff2, m_g_post_ffn, v_meta_tokens, v_g_pre_mix, v_w_in, v_conv_w, v_conv_b, v_w_a, v_b_a, v_w_x, v_b_x, v_lru_lambda, v_attn_sinks, v_w_out, v_g_post_mix, v_g_pre_ffn, v_w_ff1, v_w_ff2, v_g_post_ffn):
    weights = dict(meta_tokens=meta_tokens, g_pre_mix=g_pre_mix, w_in=w_in, conv_w=conv_w, conv_b=conv_b, w_a=w_a,
                   b_a=b_a, w_x=w_x, b_x=b_x, lru_lambda=lru_lambda, attn_sinks=attn_sinks, w_out=w_out,
                   g_post_mix=g_post_mix, g_pre_ffn=g_pre_ffn, w_ff1=w_ff1, w_ff2=w_ff2, g_post_ffn=g_post_ffn)
    mom_m = dict(meta_tokens=m_meta_tokens, g_pre_mix=m_g_pre_mix, w_in=m_w_in, conv_w=m_conv_w, conv_b=m_conv_b,
                 w_a=m_w_a, b_a=m_b_a, w_x=m_w_x, b_x=m_b_x, lru_lambda=m_lru_lambda, attn_sinks=m_attn_sinks,
                 w_out=m_w_out, g_post_mix=m_g_post_mix, g_pre_ffn=m_g_pre_ffn, w_ff1=m_w_ff1, w_ff2=m_w_ff2,
                 g_post_ffn=m_g_post_ffn)
    mom_v = dict(meta_tokens=v_meta_tokens, g_pre_mix=v_g_pre_mix, w_in=v_w_in, conv_w=v_conv_w, conv_b=v_conv_b,
                 w_a=v_w_a, b_a=v_b_a, w_x=v_w_x, b_x=v_b_x, lru_lambda=v_lru_lambda, attn_sinks=v_attn_sinks,
                 w_out=v_w_out, g_post_mix=v_g_post_mix, g_pre_ffn=v_g_pre_ffn, w_ff1=v_w_ff1, w_ff2=v_w_ff2,
                 g_post_ffn=v_g_post_ffn)
    order = list(weights)

    (g_win, g_meta, g_cw) = _gather_two_level([w_in[0].astype(BF16), meta_tokens, conv_w[0]], "gather_first")
    w_in_full = _cols_from_shards(g_win)
    meta_full = _cols_from_shards(g_meta)
    conv_w_full = _cols_from_shards(g_cw)

    head = jnp.concatenate([jnp.zeros((PAD_ROWS, D_MODEL), F32), meta_full], axis=0)
    wa_bd = _block_diag(w_a[0]).astype(BF16)
    wx_bd = _block_diag(w_x[0]).astype(BF16)
    bias = _attn_bias()

    w1_shard = w_ff1[0].astype(BF16)
    (qkv, zrec, u1, attn), (g_wout, w1a) = _in_proj_attn_fwd(
        head, x[0], g_pre_mix, w_in_full, attn_sinks, bias,
        [w_out[0].astype(BF16), w1_shard[:, :FF_HALF]], ["gather"] * 2)
    (rec, h_lru, kept), (w1b,) = _rec_fwd(zrec, conv_w_full, conv_b, wa_bd, b_a, wx_bd, b_x, lru_lambda,
                                         [w1_shard[:, FF_HALF:]], ["gather"])
    w_out_full = g_wout.reshape(D_MODEL, D_MODEL)
    w2_shard = w_ff2[0].astype(BF16)
    (mix, h1, act, u2), (w2a, w2b) = _mix_and_ffn_up(
        attn, rec, w_out_full, head, x[0], g_post_mix, g_pre_ffn, (w1a, w1b),
        [w2_shard[:FF_HALF], w2_shard[FF_HALF:]], ["gather"] * 2)
    w2_halves = [w.reshape(D_FF // 2, D_MODEL) for w in (w2a, w2b)]
    (dy, df, dg_post_ffn, loss_acc), w2t_halves = _ffn_down_loss(
        act, w2_halves, h1, loss_target[0], g_post_ffn, [w2_shard[:FF_HALF].T, w2_shard[FF_HALF:].T], ["gather"] * 2)

    (da1,), (w1ta,) = _ffn_bwd_act(df, w2t_halves, act, [w1_shard[:, :FF_HALF].T], ["gather"])
    (dw1h, dw2g), (w1tb,) = _ffn_bwd_weights(u2, da1, act, df, [w1_shard[:, FF_HALF:].T], ["gather"])
    w1t_halves = [w.reshape(D_FF // 2, D_MODEL) for w in (w1ta, w1tb)]
    (dh1, dg_pre_ffn), (p_w1a,) = _ffn_bwd_x(da1, w1t_halves, h1, dy, g_pre_ffn, [dw1h[0]], ["scatter"])
    (dattn, drec, dw_out, dg_post_mix), (p_w1b,) = _out_proj_bwd(dh1, mix, g_post_mix, w_out_full.T, attn, rec,
                                                                [dw1h[1]], ["scatter"])
    (dq, dkv_late, dsinks), (p_w2,) = _attn_bwd(qkv, dattn, attn_sinks, bias, [dw2g], ["scatter"])
    dkv = dkv_late[BLOCK:BLOCK + qkv.shape[0]]
    (drz, rec_small, dwa_bd, dwx_bd), (p_wout,) = _rec_bwd(
        drec, zrec, h_lru, kept, conv_w_full, wa_bd, wx_bd, lru_lambda,
        [dw_out.reshape(N_DEV, D_MODEL // N_DEV, D_MODEL)], ["scatter"])
    small_grads = dict(
        conv_b=rec_small[ROW_CONV_B], b_a=rec_small[ROW_B_A], b_x=rec_small[ROW_B_X], lru_lambda=rec_small[ROW_LAMBDA],
        attn_sinks=dsinks[:, 0], g_post_mix=dg_post_mix, g_pre_ffn=dg_pre_ffn, g_post_ffn=dg_post_ffn)
    gate_rows = (LRU_BLOCKS * LRU_BLOCK, LRU_BLOCK)
    gate_dense = (LRU_BLOCKS * LRU_BLOCK * LRU_BLOCK // PACK_WIDTH, PACK_WIDTH)
    (dw_in,), (p_cw, p_small, p_wa, p_wx) = _in_proj_bwd_w(
        u1, dq, dkv, drz,
        [_cols_to_shards(rec_small[0:CONV_WIDTH]), _pack_rows(small_grads),
         _block_diag_extract(dwa_bd).reshape(gate_dense), _block_diag_extract(dwx_bd).reshape(gate_dense)],
        ["scatter", "gather", "gather", "gather"])
    p_wa, p_wx = (p.reshape((N_DEV,) + gate_rows) for p in (p_wa, p_wx))
    (dh0, dg_pre_mix), (p_win,) = _in_proj_bwd_x(
        head, x[0], g_pre_mix, dh1, dq, dkv, drz, w_in_full.T, [_cols_to_shards(dw_in)], ["scatter"])
    p_meta, p_gpm, p_loss = _exchange([_cols_to_shards(dh0[PAD_ROWS:BLOCK]), dg_pre_mix, loss_acc],
                                      ["scatter", "gather", "gather"], "exchange_last")

    res = {}
    res["w_in"] = _adamw(w_in[0], m_w_in[0], v_w_in[0], p_win, "adamw_w_in")
    res["w_out"] = _adamw(w_out[0], m_w_out[0], v_w_out[0], p_wout, "adamw_w_out")
    res["w_ff1"] = _adamw(w_ff1[0], m_w_ff1[0], v_w_ff1[0], [p_w1a, p_w1b], "adamw_w_ff1")
    res["w_ff2"] = _adamw(w_ff2[0], m_w_ff2[0], v_w_ff2[0], p_w2, "adamw_w_ff2")
    for name in ("w_in", "w_out", "w_ff1", "w_ff2"):
        res[name] = tuple(t[None] for t in res[name])
    others = [("g_pre_mix", g_pre_mix.shape, p_gpm), ("meta_tokens", meta_tokens.shape, p_meta),
              ("conv_w", conv_w.shape[1:], p_cw), ("w_a", gate_rows, p_wa), ("w_x", gate_rows, p_wx)]
    loss_total, small = _adamw_small(weights, mom_m, mom_v, p_small, p_loss, others)
    res.update(small)

    grad_x = dh0[BLOCK:][None]
    outs = [loss_total[0, 0], grad_x]
    for k in range(4):
        outs += [res[name][k] for name in order]
    return tuple(outs)
```

```python
import jax
import jax.numpy as jnp
import numpy as np
from jax import lax
from jax.experimental import pallas as pl
from jax.experimental.pallas import tpu as pltpu

F32 = jnp.float32
BF16 = jnp.bfloat16

D_MODEL = 1024
N_META = 16
HEAD_DIM = 64
ATTN_HEADS = 8
KV_HEADS = 2
GQA_GROUP = ATTN_HEADS // KV_HEADS
ATTN_WIDTH = ATTN_HEADS * HEAD_DIM
KV_WIDTH = KV_HEADS * HEAD_DIM
QKV_WIDTH = ATTN_WIDTH + 2 * KV_WIDTH
LRU_WIDTH = 512
LRU_BLOCKS = 8
LRU_BLOCK = 64
LRU_HALF = 256
LRU_C = 8.0
CONV_WIDTH = 4
BLOCK = 128
PAD_ROWS = BLOCK - N_META
IN_WIDTH = QKV_WIDTH + 2 * LRU_WIDTH
D_FF = 4096
EPS = 1e-6
NEG = -1e30
N_DEV = 8
FF_CHUNK = D_FF // N_DEV
SUBLANES = 8
LANES = 128

ADAM_LR = 0.001
ADAM_B1 = 0.9
ADAM_B2 = 0.999
ADAM_EPS = 1e-08
ADAM_WD = 0.01
ADAM_STEP = 10

VMEM_LIMIT = 56 * 1024 * 1024


def _row_tile(rows):
    for t in (640, 512, 256, 128):
        if rows % t == 0:
            return t
    raise ValueError(rows)


def _big_tile(rows):
    for t in (1664, 1024, 512, 256, 128):
        if rows % t == 0:
            return t
    raise ValueError(rows)


def _rec_tile(rows):
    for t in (640, 256, 128):
        if rows % t == 0:
            return t
    raise ValueError(rows)


def _params(semantics):
    return pltpu.CompilerParams(dimension_semantics=semantics, vmem_limit_bytes=VMEM_LIMIT)


def _mm(a, b):
    return lax.dot_general(a, b, (((1,), (0,)), ((), ())), preferred_element_type=F32)


def _mm_nt(a, b):
    return lax.dot_general(a, b, (((1,), (1,)), ((), ())), preferred_element_type=F32)


def _mm_tn(a, b):
    return lax.dot_general(a, b, (((0,), (0,)), ((), ())), preferred_element_type=F32)


def _rms_fwd(x, g):
    rstd = lax.rsqrt(jnp.mean(x * x, axis=-1, keepdims=True) + EPS)
    xhat = x * rstd
    return xhat * g, xhat, rstd


def _rms_bwd(dy, xhat, rstd, g):
    dyg = dy * g
    c = jnp.mean(dyg * xhat, axis=-1, keepdims=True)
    dx = rstd * (dyg - xhat * c)
    dg = jnp.sum(dy * xhat, axis=0, keepdims=True)
    return dx, dg


def _sigmoid(x):
    return pl.reciprocal(1.0 + jnp.exp(-x), approx=True)


def _log1p(x):
    u = 1.0 + x
    return jnp.where(u == 1.0, x, jnp.log(u) * x / (u - 1.0))


def _one_minus_sq_exp(x, ex):
    return -jnp.tanh(x) * (1.0 + ex * ex)


TINY = 1e-30


def _sqrt_pos(y):
    r = lax.rsqrt(jnp.maximum(y, TINY))
    return y * r, r


def _softplus(x):
    return jnp.maximum(x, 0.0) + _log1p(jnp.exp(-jnp.abs(x)))


GELU_C = 0.7978845608028654
GELU_K = 0.044715


def _gelu(x):
    t = jnp.tanh(GELU_C * (x + GELU_K * x * x * x))
    return 0.5 * x * (1.0 + t), t


def _gelu_grad(x, t):
    return 0.5 * (1.0 + t) + 0.5 * x * (1.0 - t * t) * GELU_C * (1.0 + 3.0 * GELU_K * x * x)


def _full(shape):
    return pl.BlockSpec(shape, lambda *_: (0,) * len(shape))


def _resident(shape):
    return pl.BlockSpec(shape, lambda *_: (0,) * len(shape), pipeline_mode=pl.Buffered(1))


def _exchange_copies(ins, outs, sems, modes):
    send_sems, recv_sems, local_sems = sems
    x, y, c = lax.axis_index("x"), lax.axis_index("y"), lax.axis_index("c")
    me = 4 * x + 2 * y + c

    def block(a, dev):
        return ins[a] if modes[a] == "gather" else ins[a].at[dev]

    local = [pltpu.make_async_copy(block(a, me), outs[a].at[me], local_sems.at[a]) for a in range(len(ins))]
    sends, recvs = [], []
    for a in range(len(ins)):
        for k in range(N_DEV - 1):
            bits = k + 1
            px = jnp.bitwise_xor(x, (bits >> 2) & 1)
            py = jnp.bitwise_xor(y, (bits >> 1) & 1)
            pc = jnp.bitwise_xor(c, bits & 1)
            peer = 4 * px + 2 * py + pc
            common = dict(src_ref=block(a, peer), send_sem=send_sems.at[a, k], recv_sem=recv_sems.at[a, k],
                          device_id=(px, py, pc), device_id_type=pl.DeviceIdType.MESH)
            sends.append(pltpu.make_async_remote_copy(dst_ref=outs[a].at[me], **common))
            recvs.append(pltpu.make_async_remote_copy(dst_ref=outs[a].at[peer], **common))
    return local, sends, recvs


def _exchange_start(ins, outs, sems, modes):
    local, sends, _ = _exchange_copies(ins, outs, sems, modes)
    for cp in local + sends:
        cp.start()


def _exchange_wait(ins, outs, sems, modes):
    local, sends, recvs = _exchange_copies(ins, outs, sems, modes)
    for cp in recvs:
        cp.wait_recv()
    for cp in sends:
        cp.wait_send()
    for cp in local:
        cp.wait()


def _exchange_shapes(arrays, modes):
    return [jax.ShapeDtypeStruct((N_DEV,) + a.shape if mode == "gather" else a.shape, a.dtype)
            for a, mode in zip(arrays, modes)]


def _exchange_sems(na):
    return [pltpu.SemaphoreType.DMA((na, N_DEV - 1)), pltpu.SemaphoreType.DMA((na, N_DEV - 1)),
            pltpu.SemaphoreType.DMA((na,))]


ANY_SPACE = pl.BlockSpec(memory_space=pl.ANY)


def _exchange(arrays, modes, name):
    na = len(arrays)

    def body(*refs):
        ins, outs, sems = refs[:na], refs[na:2 * na], refs[2 * na:]
        _exchange_start(ins, outs, sems, modes)
        _exchange_wait(ins, outs, sems, modes)

    return pl.pallas_call(
        body, name=name, out_shape=_exchange_shapes(arrays, modes),
        in_specs=[ANY_SPACE] * na, out_specs=[ANY_SPACE] * na, scratch_shapes=_exchange_sems(na),
        compiler_params=pltpu.CompilerParams(has_side_effects=True),
    )(*arrays)


def _gather_two_level(arrays, name):
    na = len(arrays)

    def body(*refs):
        ins, outs = refs[:na], refs[na:2 * na]
        send_sems, recv_sems, local_sems = refs[2 * na:]
        x, y, c = lax.axis_index("x"), lax.axis_index("y"), lax.axis_index("c")
        me, sibling = (x, y, c), (x, y, 1 - c)
        chips = [(1 - x, y), (x, 1 - y), (1 - x, 1 - y)]

        def copy(a, k, block, to, src=None):
            slot = outs[a].at[4 * block[0] + 2 * block[1] + block[2]]
            return pltpu.make_async_remote_copy(
                src_ref=slot if src is None else src, dst_ref=slot, send_sem=send_sems.at[a, k],
                recv_sem=recv_sems.at[a, k], device_id=to, device_id_type=pl.DeviceIdType.MESH)

        local = [pltpu.make_async_copy(ins[a], outs[a].at[4 * x + 2 * y + c], local_sems.at[a]) for a in range(na)]
        first = []
        for a in range(na):
            first.append(copy(a, 0, me, sibling, src=ins[a]))
            first += [copy(a, 1 + j, me, (*chip, c), src=ins[a]) for j, chip in enumerate(chips)]
        for cp in local + first:
            cp.start()
        passed = []
        for j, chip in enumerate(chips):
            for a in range(na):
                copy(a, 1 + j, (*chip, c), me).wait_recv()
                passed.append(copy(a, 4 + j, (*chip, c), sibling))
                passed[-1].start()
        for a in range(na):
            copy(a, 0, sibling, me).wait_recv()
            for j, chip in enumerate(chips):
                copy(a, 4 + j, (*chip, 1 - c), me).wait_recv()
        for cp in first + passed:
            cp.wait_send()
        for cp in local:
            cp.wait()

    return pl.pallas_call(
        body, name=name, out_shape=_exchange_shapes(arrays, ["gather"] * na),
        in_specs=[ANY_SPACE] * na, out_specs=[ANY_SPACE] * na, scratch_shapes=_exchange_sems(na),
        compiler_params=pltpu.CompilerParams(has_side_effects=True),
    )(*arrays)


def _hosting_call(body, name, steps, in_specs, out_specs, out_shape, scratch_shapes, args, arrays, modes):
    n_in, n_out, n_scr, na = len(in_specs), len(out_specs), len(scratch_shapes), len(arrays)
    grid = steps if isinstance(steps, tuple) else (steps,)

    def hosting_body(*refs):
        cuts = [0]
        for n in (n_in, na, n_out, na, n_scr, 3):
            cuts.append(cuts[-1] + n)
        ins, x_ins, outs, x_outs, scr, sems = (refs[cuts[p]:cuts[p + 1]] for p in range(6))
        first, last = True, True
        for axis, n in enumerate(grid):
            first = first & (pl.program_id(axis) == 0)
            last = last & (pl.program_id(axis) == n - 1)

        @pl.when(first)
        def _():
            _exchange_start(x_ins, x_outs, sems, modes)

        body(*ins, *outs, *scr)

        @pl.when(last)
        def _():
            _exchange_wait(x_ins, x_outs, sems, modes)

    res = pl.pallas_call(
        hosting_body, name=name, grid=grid,
        in_specs=list(in_specs) + [ANY_SPACE] * na, out_specs=list(out_specs) + [ANY_SPACE] * na,
        out_shape=list(out_shape) + _exchange_shapes(arrays, modes),
        scratch_shapes=list(scratch_shapes) + _exchange_sems(na),
        compiler_params=_params(("arbitrary",) * len(grid)),
    )(*args, *arrays)
    return res[:n_out], res[n_out:]


def _frame_rows(src_hbm, buf, sem, i, steps, tm):
    def first():
        return pltpu.make_async_copy(src_hbm.at[pl.ds(0, tm - BLOCK)], buf.at[0, pl.ds(BLOCK, tm - BLOCK)], sem.at[0])

    def later(t, slot):
        return pltpu.make_async_copy(src_hbm.at[pl.ds(pl.multiple_of(t * tm - BLOCK, SUBLANES), tm)], buf.at[slot], sem.at[slot])

    slot = i % 2

    @pl.when(i == 0)
    def _():
        first().start()

    @pl.when(i + 1 < steps)
    def _():
        later(i + 1, 1 - slot).start()

    @pl.when(i == 0)
    def _():
        first().wait()

    @pl.when(i > 0)
    def _():
        later(i, slot).wait()

    return slot


def _frame_scratch(tm):
    return [pltpu.VMEM((2, tm, D_MODEL), F32), pltpu.SemaphoreType.DMA((2,))]


def _h0_tile(head_ref, x_hbm, buf, sem, i, steps, tm):
    slot = _frame_rows(x_hbm, buf, sem, i, steps, tm)

    @pl.when(i == 0)
    def _():
        buf[0, 0:BLOCK, :] = head_ref[...]

    return buf[slot]


N_BIAS = 3


def _attn_bias():
    key = np.arange(2 * BLOCK)[:, None]
    r = np.arange(GQA_GROUP * BLOCK)[None, :] % BLOCK
    band = (key > r) & (key <= r + BLOCK)
    out = [np.where(band & ((n - 1) * BLOCK + key >= PAD_ROWS), 0.0, NEG) for n in range(N_BIAS)]
    return jnp.asarray(np.stack(out), F32)


def _attn_probs(k2, q4, bias, sink_row):
    s = _mm_nt(k2, q4) * (HEAD_DIM ** -0.5) + bias
    m = jnp.maximum(jnp.max(s, axis=0, keepdims=True), sink_row)
    p = jnp.exp(s - m)
    es = jnp.exp(sink_row - m)
    inv = 1.0 / (jnp.sum(p, axis=0, keepdims=True) + es)
    return p * inv, es * inv


def _heads(ref, rows, first, count):
    return jnp.concatenate([ref[rows, (first + g) * HEAD_DIM:(first + g + 1) * HEAD_DIM] for g in range(count)], axis=0)


def _keys_of_block(prev_ref, cur_ref, b, kv):
    sl = slice(kv * HEAD_DIM, (kv + 1) * HEAD_DIM)
    before = prev_ref[:, sl] if b == 0 else cur_ref[(b - 1) * BLOCK:b * BLOCK, sl]
    return jnp.concatenate([before, cur_ref[b * BLOCK:(b + 1) * BLOCK, sl]], axis=0)


def _bias_of_block(bias_ref, block):
    return bias_ref[jnp.minimum(block, N_BIAS - 1)]


def _sink_row(sink_ref, kv):
    g = lax.broadcasted_iota(jnp.int32, (1, GQA_GROUP * BLOCK), 1) // BLOCK
    row = jnp.full((1, GQA_GROUP * BLOCK), sink_ref[0, kv * GQA_GROUP], F32)
    for i in range(1, GQA_GROUP):
        row = jnp.where(g == i, sink_ref[0, kv * GQA_GROUP + i], row)
    return row


def _from_head_major(pieces):
    return jnp.concatenate(pieces, axis=0).T


def _attn_specs(tm, tile_of):
    nbt = tm // BLOCK
    k_col, v_col = ATTN_WIDTH // KV_WIDTH, ATTN_WIDTH // KV_WIDTH + 1
    before = lambda i: jnp.maximum(tile_of(i) * nbt - 1, 0)
    return [pl.BlockSpec((tm, ATTN_WIDTH), lambda i: (tile_of(i), 0)),
            pl.BlockSpec((BLOCK, KV_WIDTH), lambda i: (before(i), k_col)),
            pl.BlockSpec((tm, KV_WIDTH), lambda i: (tile_of(i), k_col)),
            pl.BlockSpec((BLOCK, KV_WIDTH), lambda i: (before(i), v_col)),
            pl.BlockSpec((tm, KV_WIDTH), lambda i: (tile_of(i), v_col))]


def _in_proj_attn_fwd(head, x, g1, w_in, sinks, bias, carried, modes):
    rows = BLOCK + x.shape[0]
    tm = _row_tile(rows)
    steps, nbt = rows // tm, tm // BLOCK

    def body(head_ref, g_ref, w_ref, sink_ref, bias_ref, x_hbm, qkv_ref, zrec_ref, u_ref, o_ref, buf, sem, kv_before):
        i = pl.program_id(0)
        h = _h0_tile(head_ref, x_hbm, buf, sem, i, steps, tm)
        u, _, _ = _rms_fwd(h, g_ref[...])
        u = u.astype(BF16)
        u_ref[...] = u
        z = _mm(u, w_ref[...])
        qkv_ref[...] = z[:, :QKV_WIDTH].astype(BF16)
        zrec_ref[...] = z[:, QKV_WIDTH:]

        @pl.when(i == 0)
        def _():
            kv_before[...] = jnp.zeros_like(kv_before)

        kc_ref, vc_ref = (qkv_ref.at[:, pl.ds(ATTN_WIDTH + c * KV_WIDTH, KV_WIDTH)] for c in range(2))
        kp_ref, vp_ref = (kv_before.at[:, pl.ds(c * KV_WIDTH, KV_WIDTH)] for c in range(2))
        for b in range(nbt):
            blk = slice(b * BLOCK, (b + 1) * BLOCK)
            bias_t = _bias_of_block(bias_ref, i * nbt + b)
            pieces = []
            for kv in range(KV_HEADS):
                k2 = _keys_of_block(kp_ref, kc_ref, b, kv)
                v2 = _keys_of_block(vp_ref, vc_ref, b, kv)
                q4 = _heads(qkv_ref, blk, kv * GQA_GROUP, GQA_GROUP)
                pn, _ = _attn_probs(k2, q4, bias_t, _sink_row(sink_ref, kv))
                ot = _mm_tn(v2, pn.astype(BF16))
                pieces += [ot[:, g * BLOCK:(g + 1) * BLOCK] for g in range(GQA_GROUP)]
            o_ref[blk, :] = _from_head_major(pieces).astype(BF16)
        kv_before[...] = qkv_ref[tm - BLOCK:tm, ATTN_WIDTH:]

    wide = pl.BlockSpec((tm, D_MODEL), lambda i: (i, 0))
    return _hosting_call(
        body, "in_proj_attn_fwd", steps,
        [_full((BLOCK, D_MODEL)), _full((1, D_MODEL)), _resident((D_MODEL, IN_WIDTH)), pl.BlockSpec(memory_space=pltpu.SMEM),
         _resident((N_BIAS, 2 * BLOCK, GQA_GROUP * BLOCK)), ANY_SPACE],
        [pl.BlockSpec((tm, QKV_WIDTH), lambda i: (i, 0)), pl.BlockSpec((tm, 2 * LRU_WIDTH), lambda i: (i, 0)), wide,
         pl.BlockSpec((tm, ATTN_WIDTH), lambda i: (i, 0))],
        [jax.ShapeDtypeStruct((rows, QKV_WIDTH), BF16), jax.ShapeDtypeStruct((rows, 2 * LRU_WIDTH), F32),
         jax.ShapeDtypeStruct((rows, D_MODEL), BF16), jax.ShapeDtypeStruct((rows, ATTN_WIDTH), BF16)],
        _frame_scratch(tm) + [pltpu.VMEM((BLOCK, 2 * KV_WIDTH), BF16)],
        (head, g1, w_in, sinks, bias, x), carried, modes)


def _conv_taps(xbuf, tm):
    return [xbuf[pl.ds(SUBLANES - (CONV_WIDTH - 1 - j), tm), :] for j in range(CONV_WIDTH)]


def _lru_halves(xc):
    return [xc[:, h * LRU_HALF:(h + 1) * LRU_HALF].astype(BF16) for h in range(2)]


def _lru_gates(xc, wa_ref, ba_ref, wx_ref, bx_ref, lam_ref):
    halves = _lru_halves(xc)
    gate_r = jnp.concatenate([_mm(halves[h], wa_ref[h]) for h in range(2)], axis=1) + ba_ref[...]
    gate_i = jnp.concatenate([_mm(halves[h], wx_ref[h]) for h in range(2)], axis=1) + bx_ref[...]
    r = _sigmoid(gate_r)
    ig = _sigmoid(gate_i)
    log_a = (-LRU_C) * r * _softplus(-lam_ref[...])
    a = jnp.exp(log_a)
    mult, _ = _sqrt_pos(_one_minus_sq_exp(log_a, a))
    return r, ig, a, mult


KEPT_XC, KEPT_A, KEPT_MULT, KEPT_R, KEPT_I, N_KEPT = 0, 1, 2, 3, 4, 5


def _scan_tile(a_ref, u_ref, out_ref, carry, tm):
    row = lax.broadcasted_iota(jnp.int32, (SUBLANES, LRU_WIDTH), 0)

    def step(j, before):
        o = pl.multiple_of(j * SUBLANES, SUBLANES)
        a = a_ref[pl.ds(o, SUBLANES), :]
        u = u_ref[pl.ds(o, SUBLANES), :]
        for s in (1, 2, 4):
            keep = row >= s
            u = jnp.where(keep, a * pltpu.roll(u, s, 0) + u, u)
            a = jnp.where(keep, a * pltpu.roll(a, s, 0), a)
        out = a * before + u
        out_ref[pl.ds(o, SUBLANES), :] = out
        return out[SUBLANES - 1:SUBLANES, :]

    return lax.fori_loop(0, tm // SUBLANES, step, carry)


def _rec_fwd(zrec, conv_w, conv_b, wa_bd, b_a, wx_bd, b_x, lam, carried, modes):
    rows = zrec.shape[0]
    tm = _row_tile(rows)

    def body(xr_ref, yr_ref, cw_ref, cb_ref, wa_ref, ba_ref, wx_ref, bx_ref, lam_ref, rec_ref, h_ref, kept_ref,
             xbuf, a_s, u_s, carry):
        i = pl.program_id(0)

        @pl.when(i == 0)
        def _():
            xbuf[0:SUBLANES, :] = jnp.zeros((SUBLANES, LRU_WIDTH), F32)
            carry[...] = jnp.zeros_like(carry)

        @pl.when(i > 0)
        def _():
            xbuf[0:SUBLANES, :] = xbuf[tm:tm + SUBLANES, :]

        xbuf[SUBLANES:SUBLANES + tm, :] = xr_ref[...]
        taps = _conv_taps(xbuf, tm)
        xc = cb_ref[...] + sum(cw_ref[j:j + 1, :] * taps[j] for j in range(CONV_WIDTH))
        r, ig, a, mult = _lru_gates(xc, wa_ref, ba_ref, wx_ref, bx_ref, lam_ref)
        for k, val in ((KEPT_XC, xc), (KEPT_A, a), (KEPT_MULT, mult), (KEPT_R, r), (KEPT_I, ig)):
            kept_ref[:, k * LRU_WIDTH:(k + 1) * LRU_WIDTH] = val
        grow = i * tm + lax.broadcasted_iota(jnp.int32, (tm, LRU_WIDTH), 0)
        a_s[...] = a
        u_s[...] = jnp.where(grow >= PAD_ROWS, mult * (ig * xc), 0.0)
        carry[0:1, :] = _scan_tile(a_s, u_s, h_ref, carry[0:1, :], tm)
        gel, _ = _gelu(yr_ref[...])
        rec_ref[...] = (gel * h_ref[...]).astype(BF16)

    vec = _full((1, LRU_WIDTH))
    bd = _full((2, LRU_HALF, LRU_HALF))
    return _hosting_call(
        body, "rec_fwd", rows // tm,
        [pl.BlockSpec((tm, LRU_WIDTH), lambda i: (i, 0)), pl.BlockSpec((tm, LRU_WIDTH), lambda i: (i, 1)),
         _full((CONV_WIDTH, LRU_WIDTH)), vec, bd, vec, bd, vec, vec],
        [pl.BlockSpec((tm, LRU_WIDTH), lambda i: (i, 0))] * 2 + [pl.BlockSpec((tm, N_KEPT * LRU_WIDTH), lambda i: (i, 0))],
        [jax.ShapeDtypeStruct((rows, LRU_WIDTH), BF16), jax.ShapeDtypeStruct((rows, LRU_WIDTH), F32),
         jax.ShapeDtypeStruct((rows, N_KEPT * LRU_WIDTH), F32)],
        [pltpu.VMEM((tm + SUBLANES, LRU_WIDTH), F32), pltpu.VMEM((tm, LRU_WIDTH), F32),
         pltpu.VMEM((tm, LRU_WIDTH), F32), pltpu.VMEM((SUBLANES, LRU_WIDTH), F32)],
        (zrec, zrec, conv_w, conv_b, wa_bd, b_a, wx_bd, b_x, lam), carried, modes)


FF_COLS = 1024
FF_HALF = FF_CHUNK // 2


def _hidden_at(d, half):
    return half * (D_FF // 2) + d * FF_HALF


def _mix_and_ffn_up(attn, rec, w_out, head, x, g2, g3, w1_halves, carried, modes):
    rows = attn.shape[0]
    tm = _row_tile(rows)
    steps = rows // tm

    def body(attn_ref, rec_ref, w_ref, head_ref, g2_ref, g3_ref, wa_ref, wb_ref, x_hbm,
             mix_ref, h1_ref, act_ref, u_ref, buf, sem):
        h0 = _h0_tile(head_ref, x_hbm, buf, sem, pl.program_id(0), steps, tm)
        mix = _mm(attn_ref[...], w_ref[0:ATTN_WIDTH, :]) + _mm(rec_ref[...], w_ref[ATTN_WIDTH:, :])
        y, _, _ = _rms_fwd(mix, g2_ref[...])
        mix_ref[...] = mix
        h1_ref[...] = h0 + y
        u, _, _ = _rms_fwd(h1_ref[...], g3_ref[...])
        u = u.astype(BF16)
        u_ref[...] = u
        for half, w1_ref in enumerate((wa_ref, wb_ref)):
            for d in range(N_DEV):
                c = _hidden_at(d, half)
                a1 = jnp.maximum(_mm(u, w1_ref[d]), 0.0)
                act_ref[:, c:c + FF_HALF] = (a1 * a1).astype(BF16)

    half_in = pl.BlockSpec((tm, ATTN_WIDTH), lambda i: (i, 0))
    wide = pl.BlockSpec((tm, D_MODEL), lambda i: (i, 0))
    return _hosting_call(
        body, "mix_and_ffn_up", steps,
        [half_in, half_in, _resident((D_MODEL, D_MODEL)), _full((BLOCK, D_MODEL)), _full((1, D_MODEL)), _full((1, D_MODEL))]
        + [_resident((N_DEV, D_MODEL, FF_HALF))] * 2 + [ANY_SPACE],
        [wide, wide, pl.BlockSpec((tm, D_FF), lambda i: (i, 0)), wide],
        [jax.ShapeDtypeStruct((rows, D_MODEL), F32)] * 2
        + [jax.ShapeDtypeStruct((rows, D_FF), BF16), jax.ShapeDtypeStruct((rows, D_MODEL), BF16)],
        _frame_scratch(tm), (attn, rec, w_out, head, g2, g3, *w1_halves, x), carried, modes)


def _ffn_down_loss(act, w2_halves, h1, target, g4, carried, modes):
    rows = h1.shape[0]
    tm = _row_tile(rows)
    steps = rows // tm
    kh = D_FF // 2

    def body(act_ref, wa_ref, wb_ref, h_ref, g_ref, t_hbm, dy_ref, df_ref, dg_ref, loss_ref, buf, sem):
        i = pl.program_id(0)
        slot = _frame_rows(t_hbm, buf, sem, i, steps, tm)

        @pl.when(i == 0)
        def _():
            dg_ref[...] = jnp.zeros_like(dg_ref)
            loss_ref[...] = jnp.zeros_like(loss_ref)
            buf[0, 0:BLOCK, :] = jnp.zeros((BLOCK, D_MODEL), F32)

        g = g_ref[...]
        f = _mm(act_ref[:, :kh], wa_ref[...]) + _mm(act_ref[:, kh:], wb_ref[...])
        y, fhat, rstd = _rms_fwd(f, g)
        grow = i * tm + lax.broadcasted_iota(jnp.int32, (tm, D_MODEL), 0)
        err = jnp.where(grow >= BLOCK, h_ref[...] + y - buf[slot], 0.0)
        loss_ref[...] += (0.5 / D_MODEL) * jnp.sum(err * err)
        dy = err * (1.0 / D_MODEL)
        df, dg = _rms_bwd(dy, fhat, rstd, g)
        dy_ref[...] = dy
        df_ref[...] = df.astype(BF16)
        dg_ref[...] += dg

    wide = pl.BlockSpec((tm, D_MODEL), lambda i: (i, 0))
    return _hosting_call(
        body, "ffn_down_loss", steps,
        [pl.BlockSpec((tm, D_FF), lambda i: (i, 0)), _resident((kh, D_MODEL)), _resident((kh, D_MODEL)), wide,
         _full((1, D_MODEL)), ANY_SPACE],
        [wide, wide, _full((1, D_MODEL)), _full((SUBLANES, LANES))],
        [jax.ShapeDtypeStruct((rows, D_MODEL), F32), jax.ShapeDtypeStruct((rows, D_MODEL), BF16),
         jax.ShapeDtypeStruct((1, D_MODEL), F32), jax.ShapeDtypeStruct((SUBLANES, LANES), F32)],
        _frame_scratch(tm), (act, *w2_halves, h1, g4, target), carried, modes)


def _ffn_bwd_act(df, w2t_halves, act, carried, modes):
    rows = df.shape[0]
    tm = _row_tile(rows)

    def body(df_ref, wa_ref, wb_ref, act_ref, da_ref):
        df_t = df_ref[...]
        for half, w_ref in enumerate((wa_ref, wb_ref)):
            for d in range(N_DEV):
                cols = slice(_hidden_at(d, half), _hidden_at(d, half) + FF_HALF)
                dact = _mm(df_t, w_ref[d])
                relu_a1, _ = _sqrt_pos(act_ref[:, cols].astype(F32))
                da_ref[:, cols] = (dact * (2.0 * relu_a1)).astype(BF16)

    hidden = pl.BlockSpec((tm, D_FF), lambda i: (i, 0))
    return _hosting_call(
        body, "ffn_bwd_act", rows // tm,
        [pl.BlockSpec((tm, D_MODEL), lambda i: (i, 0))] + [_resident((N_DEV, D_MODEL, FF_HALF))] * 2 + [hidden],
        [hidden],
        [jax.ShapeDtypeStruct((rows, D_FF), BF16)],
        [], (df, *w2t_halves, act), carried, modes)


def _ffn_bwd_x(da, w1t_halves, h1, dy, g3, carried, modes):
    rows = h1.shape[0]
    tm = _row_tile(rows)
    kh = D_FF // 2

    def body(da_ref, wa_ref, wb_ref, h_ref, dy_ref, g_ref, dh_ref, dg_ref):
        @pl.when(pl.program_id(0) == 0)
        def _():
            dg_ref[...] = jnp.zeros_like(dg_ref)

        g = g_ref[...]
        _, xhat, rstd = _rms_fwd(h_ref[...], g)
        du = _mm(da_ref[:, :kh], wa_ref[...]) + _mm(da_ref[:, kh:], wb_ref[...])
        dx, dg = _rms_bwd(du, xhat, rstd, g)
        dh_ref[...] = dy_ref[...] + dx
        dg_ref[...] += dg

    wide = pl.BlockSpec((tm, D_MODEL), lambda i: (i, 0))
    return _hosting_call(
        body, "ffn_bwd_x", rows // tm,
        [pl.BlockSpec((tm, D_FF), lambda i: (i, 0)), _resident((kh, D_MODEL)), _resident((kh, D_MODEL)), wide, wide,
         _full((1, D_MODEL))],
        [wide, _full((1, D_MODEL))],
        [jax.ShapeDtypeStruct((rows, D_MODEL), F32), jax.ShapeDtypeStruct((1, D_MODEL), F32)],
        [], (da, *w1t_halves, h1, dy, g3), carried, modes)


def _ffn_bwd_weights(u2, da, act, df, carried, modes):
    rows = u2.shape[0]
    tb = _big_tile(rows)
    steps = rows // tb
    per = FF_COLS // FF_HALF

    def body(u_ref, da_ref, act_ref, df_ref, dw1_ref, dw2_ref, acc1, acc2):
        i = pl.program_id(1)

        @pl.when(i == 0)
        def _():
            acc1[...] = jnp.zeros_like(acc1)
            acc2[...] = jnp.zeros_like(acc2)

        acc1[...] += _mm_tn(u_ref[...], da_ref[...])
        acc2[...] += _mm_tn(act_ref[...], df_ref[...])

        @pl.when(i == steps - 1)
        def _():
            for p in range(per):
                c = p * FF_HALF
                dw1_ref[p] = acc1[:, c:c + FF_HALF].astype(BF16)
                dw2_ref[p] = acc2[c:c + FF_HALF, :].astype(BF16)

    wide = pl.BlockSpec((tb, D_MODEL), lambda j, i: (i, 0))
    chunk = pl.BlockSpec((tb, FF_COLS), lambda j, i: (i, j))
    return _hosting_call(
        body, "ffn_bwd_weights", (D_FF // FF_COLS, steps),
        [wide, chunk, chunk, wide],
        [pl.BlockSpec((None, per, D_MODEL, FF_HALF), lambda j, i: (j // 2, j % 2, 0, 0)),
         pl.BlockSpec((per, FF_HALF, D_MODEL), lambda j, i: (j % 2, j // 2, 0))],
        [jax.ShapeDtypeStruct((2, N_DEV, D_MODEL, FF_HALF), BF16), jax.ShapeDtypeStruct((N_DEV, FF_CHUNK, D_MODEL), BF16)],
        [pltpu.VMEM((D_MODEL, FF_COLS), F32), pltpu.VMEM((FF_COLS, D_MODEL), F32)],
        (u2, da, act, df), carried, modes)


def _out_proj_bwd(dh1, mix, g2, w_out_t, attn, rec, carried, modes):
    rows = dh1.shape[0]
    tm = _row_tile(rows)
    steps = rows // tm

    def body(dh_ref, mix_ref, g_ref, w_ref, attn_ref, rec_ref, dattn_ref, drec_ref, dw_ref, dg_ref, acc):
        i = pl.program_id(0)

        @pl.when(i == 0)
        def _():
            acc[...] = jnp.zeros_like(acc)
            dg_ref[...] = jnp.zeros_like(dg_ref)

        g = g_ref[...]
        _, xhat, rstd = _rms_fwd(mix_ref[...], g)
        dmix, dg = _rms_bwd(dh_ref[...], xhat, rstd, g)
        dmix = dmix.astype(BF16)
        dg_ref[...] += dg
        din = _mm(dmix, w_ref[...])
        dattn_ref[...] = din[:, :ATTN_WIDTH].astype(BF16)
        drec_ref[...] = din[:, ATTN_WIDTH:]
        acc[0:ATTN_WIDTH, :] += _mm_tn(attn_ref[...], dmix)
        acc[ATTN_WIDTH:, :] += _mm_tn(rec_ref[...], dmix)

        @pl.when(i == steps - 1)
        def _():
            dw_ref[...] = acc[...].astype(BF16)

    half = pl.BlockSpec((tm, ATTN_WIDTH), lambda i: (i, 0))
    wide = pl.BlockSpec((tm, D_MODEL), lambda i: (i, 0))
    return _hosting_call(
        body, "out_proj_bwd", steps,
        [wide, wide, _full((1, D_MODEL)), _resident((D_MODEL, D_MODEL)), half, half],
        [half, half, _full((D_MODEL, D_MODEL)), _full((1, D_MODEL))],
        [jax.ShapeDtypeStruct((rows, ATTN_WIDTH), BF16), jax.ShapeDtypeStruct((rows, LRU_WIDTH), F32),
         jax.ShapeDtypeStruct((D_MODEL, D_MODEL), BF16), jax.ShapeDtypeStruct((1, D_MODEL), F32)],
        [pltpu.VMEM((D_MODEL, D_MODEL), F32)],
        (dh1, mix, g2, w_out_t, attn, rec), carried, modes)


def _attn_bwd(qkv, dattn, sinks, bias, carried, modes):
    rows = qkv.shape[0]
    tm = _row_tile(rows)
    nbt, nt = tm // BLOCK, rows // tm

    def body(sink_ref, bias_ref, do_ref, q_ref, kp_ref, kc_ref, vp_ref, vc_ref, dq_ref, dkv_hbm, dsink_ref,
             dk_c, dv_c, stage, wsem):
        i = pl.program_id(0)
        slot = i % 2

        def first_write(s):
            return pltpu.make_async_copy(stage.at[s, pl.ds(BLOCK, tm - BLOCK)], dkv_hbm.at[pl.ds(0, tm - BLOCK)], wsem.at[s])

        def tile_write(t, s):
            return pltpu.make_async_copy(stage.at[s], dkv_hbm.at[pl.ds(pl.multiple_of(t * tm - BLOCK, BLOCK), tm)], wsem.at[s])

        def last_write(s):
            return pltpu.make_async_copy(stage.at[s, pl.ds(0, BLOCK)], dkv_hbm.at[pl.ds(rows - BLOCK, BLOCK)], wsem.at[s])

        def wait_write(t, s):
            @pl.when(t == 0)
            def _():
                first_write(s).wait()

            @pl.when(t > 0)
            def _():
                tile_write(t, s).wait()

        @pl.when(i == 0)
        def _():
            dk_c[...] = jnp.zeros_like(dk_c)
            dv_c[...] = jnp.zeros_like(dv_c)
            dsink_ref[...] = jnp.zeros_like(dsink_ref)

        @pl.when(i >= 2)
        def _():
            wait_write(i - 2, slot)

        @pl.when(i < nt)
        def _():
            dk_late, dv_late = dk_c[...], dv_c[...]
            dsink_rows = [jnp.zeros((1, LANES), F32)] * ATTN_HEADS
            for b in range(nbt):
                blk = slice(b * BLOCK, (b + 1) * BLOCK)
                bias_t = _bias_of_block(bias_ref, i * nbt + b)
                dq_parts, dk_parts, dv_parts = [], [], []
                for kv in range(KV_HEADS):
                    k2 = _keys_of_block(kp_ref, kc_ref, b, kv)
                    v2 = _keys_of_block(vp_ref, vc_ref, b, kv)
                    q4 = _heads(q_ref, blk, kv * GQA_GROUP, GQA_GROUP)
                    do4 = _heads(do_ref, blk, kv * GQA_GROUP, GQA_GROUP)
                    pn, psink = _attn_probs(k2, q4, bias_t, _sink_row(sink_ref, kv))
                    dpn = _mm_nt(v2, do4)
                    delta = jnp.sum(pn * dpn, axis=0, keepdims=True)
                    ds = ((pn * (dpn - delta)) * (HEAD_DIM ** -0.5)).astype(BF16)
                    dqt = _mm_tn(k2, ds)
                    dq_parts += [dqt[:, g * BLOCK:(g + 1) * BLOCK] for g in range(GQA_GROUP)]
                    dk_parts.append(_mm(ds, q4))
                    dv_parts.append(_mm(pn.astype(BF16), do4))
                    sd = psink * delta
                    for g in range(GQA_GROUP):
                        h = kv * GQA_GROUP + g
                        dsink_rows[h] = dsink_rows[h] - jnp.sum(sd[:, g * BLOCK:(g + 1) * BLOCK])
                dq_ref[blk, :] = _from_head_major(dq_parts).astype(BF16)
                dk2 = jnp.concatenate(dk_parts, axis=1)
                dv2 = jnp.concatenate(dv_parts, axis=1)
                stage[slot, blk, 0:KV_WIDTH] = (dk_late + dk2[0:BLOCK]).astype(BF16)
                stage[slot, blk, KV_WIDTH:] = (dv_late + dv2[0:BLOCK]).astype(BF16)
                dk_late, dv_late = dk2[BLOCK:], dv2[BLOCK:]
            dk_c[...] = dk_late
            dv_c[...] = dv_late
            dsink_ref[...] += jnp.concatenate(dsink_rows, axis=0)

            @pl.when(i == 0)
            def _():
                first_write(slot).start()

            @pl.when(i > 0)
            def _():
                tile_write(i, slot).start()

        @pl.when(i == nt)
        def _():
            stage[slot, 0:BLOCK, 0:KV_WIDTH] = dk_c[...].astype(BF16)
            stage[slot, 0:BLOCK, KV_WIDTH:] = dv_c[...].astype(BF16)
            last_write(slot).start()
            wait_write(i - 1, 1 - slot)
            last_write(slot).wait()

    tile_of = lambda i: jnp.minimum(i, nt - 1)
    tile = pl.BlockSpec((tm, ATTN_WIDTH), lambda i: (tile_of(i), 0))
    return _hosting_call(
        body, "attn_bwd", nt + 1,
        [pl.BlockSpec(memory_space=pltpu.SMEM), _resident((N_BIAS, 2 * BLOCK, GQA_GROUP * BLOCK)), tile]
        + _attn_specs(tm, tile_of),
        [tile, ANY_SPACE, _full((ATTN_HEADS, LANES))],
        [jax.ShapeDtypeStruct((rows, ATTN_WIDTH), BF16), jax.ShapeDtypeStruct((rows, 2 * KV_WIDTH), BF16),
         jax.ShapeDtypeStruct((ATTN_HEADS, LANES), F32)],
        [pltpu.VMEM((BLOCK, KV_WIDTH), F32), pltpu.VMEM((BLOCK, KV_WIDTH), F32),
         pltpu.VMEM((2, tm, 2 * KV_WIDTH), BF16), pltpu.SemaphoreType.DMA((2,))],
        (sinks, bias, dattn, qkv, qkv, qkv, qkv, qkv), carried, modes)


ROW_CONV_B, ROW_B_A, ROW_B_X, ROW_LAMBDA = 4, 5, 6, 7


def _rec_bwd(drec, zrec, h, kept, conv_w, wa_bd, wx_bd, lam, carried, modes):
    rows = zrec.shape[0]
    tm = _rec_tile(rows)
    nt = rows // tm
    per = tm // SUBLANES

    def body(drec_ref, xr_ref, yr_ref, h_ref, xc_ref, a_ref, mult_ref, r_ref, ig_ref, hhalo_ref, cw_ref, wa_ref, wx_ref,
             lam_ref, drz_ref, small_ref, dwa_ref, dwx_ref, hbuf, dbuf, dgr_s, dgi_s, dyr_s, carry):
        s = pl.program_id(0)
        i = nt - 1 - s

        @pl.when(s == 0)
        def _():
            small_ref[...] = jnp.zeros_like(small_ref)
            dwa_ref[...] = jnp.zeros_like(dwa_ref)
            dwx_ref[...] = jnp.zeros_like(dwx_ref)
            carry[...] = jnp.zeros_like(carry)
            dbuf[tm:tm + SUBLANES, :] = jnp.zeros((SUBLANES, LRU_WIDTH), F32)

        hbuf[0:SUBLANES, :] = jnp.where(i == 0, 0.0, hhalo_ref[...])
        hbuf[SUBLANES:SUBLANES + tm, :] = h_ref[...]

        row = lax.broadcasted_iota(jnp.int32, (SUBLANES, LRU_WIDTH), 0)
        log_a_scale = (-LRU_C) * _softplus(-lam_ref[...])
        zeros = jnp.zeros((SUBLANES, LRU_WIDTH), F32)

        def group(k, state):
            g_later, a_later, sum_dgr, sum_dgi, sum_lam = state
            o = pl.multiple_of((per - 1 - k) * SUBLANES, SUBLANES)
            rows8 = pl.ds(o, SUBLANES)
            yr, drec_t, h_t, a = yr_ref[rows8, :], drec_ref[rows8, :], h_ref[rows8, :], a_ref[rows8, :]
            gel, t = _gelu(yr)
            dyr_s[rows8, :] = drec_t * h_t * _gelu_grad(yr, t)
            u = drec_t * gel
            coef = jnp.where(row == SUBLANES - 1, a_later, pltpu.roll(a, SUBLANES - 1, 0))
            for sft in (1, 2, 4):
                keep = row < SUBLANES - sft
                u = jnp.where(keep, coef * pltpu.roll(u, SUBLANES - sft, 0) + u, u)
                coef = jnp.where(keep, coef * pltpu.roll(coef, SUBLANES - sft, 0), coef)
            g = coef * g_later + u
            du = jnp.where(i * tm + o + row >= PAD_ROWS, g, 0.0)
            h_before = jnp.where(row == 0, hbuf[rows8, :][SUBLANES - 1:SUBLANES, :], pltpu.roll(h_t, 1, 0))
            xc, mult, r, ig = xc_ref[rows8, :], mult_ref[rows8, :], r_ref[rows8, :], ig_ref[rows8, :]
            dbuf[rows8, :] = du * (mult * ig)
            dgi = (du * (mult * xc)) * (ig * (1.0 - ig))
            dgi_s[rows8, :] = dgi
            dlog_a = (g * h_before) * a - (du * (ig * xc)) * (a * a * pl.reciprocal(mult, approx=True))
            dgr = (dlog_a * log_a_scale) * (r * (1.0 - r))
            dgr_s[rows8, :] = dgr
            return g[0:1, :], a[0:1, :], sum_dgr + dgr, sum_dgi + dgi, sum_lam + dlog_a * r

        state = lax.fori_loop(0, per, group, (carry[0:1, :], carry[1:2, :], zeros, zeros, zeros))
        carry[0:1, :], carry[1:2, :] = state[0], state[1]
        sum_dgr, sum_dgi, sum_lam = (jnp.sum(v, axis=0, keepdims=True) for v in state[2:])
        dlam = sum_lam * (LRU_C * _sigmoid(-lam_ref[...]))

        dgr_b = [dgr_s[:, hh * LRU_HALF:(hh + 1) * LRU_HALF].astype(BF16) for hh in range(2)]
        dgi_b = [dgi_s[:, hh * LRU_HALF:(hh + 1) * LRU_HALF].astype(BF16) for hh in range(2)]
        halves = _lru_halves(xc_ref[...])
        for hh in range(2):
            dwa_ref[hh] += _mm_tn(halves[hh], dgr_b[hh])
            dwx_ref[hh] += _mm_tn(halves[hh], dgi_b[hh])
        dxc = dbuf[0:tm, :] + jnp.concatenate(
            [_mm_nt(dgr_b[hh], wa_ref[hh]) + _mm_nt(dgi_b[hh], wx_ref[hh]) for hh in range(2)], axis=1)

        dbuf[0:tm, :] = dxc
        sum_dxc = jnp.sum(dxc, axis=0, keepdims=True)
        ahead = [dbuf[pl.ds(CONV_WIDTH - 1 - j, tm), :] for j in range(CONV_WIDTH)]
        drz_ref[:, 0:LRU_WIDTH] = sum(cw_ref[j:j + 1, :] * ahead[j] for j in range(CONV_WIDTH)).astype(BF16)
        drz_ref[:, LRU_WIDTH:] = dyr_s[...].astype(BF16)
        upd = [jnp.sum(xr_ref[...] * ahead[j], axis=0, keepdims=True) for j in range(CONV_WIDTH)]
        dbuf[tm:tm + SUBLANES, :] = dbuf[0:SUBLANES, :]
        small_ref[...] += jnp.concatenate(upd + [sum_dxc, sum_dgr, sum_dgi, dlam], axis=0)

    rev = lambda s: nt - 1 - s
    halo = lambda s: jnp.maximum(rev(s) * per - 1, 0)
    cols = lambda k: pl.BlockSpec((tm, LRU_WIDTH), lambda s: (rev(s), k))
    halo0 = pl.BlockSpec((SUBLANES, LRU_WIDTH), lambda s: (halo(s), 0))
    bd = _full((2, LRU_HALF, LRU_HALF))
    big = pltpu.VMEM((tm + SUBLANES, LRU_WIDTH), F32)
    tile = pltpu.VMEM((tm, LRU_WIDTH), F32)
    kept_cols = [cols(k) for k in (KEPT_XC, KEPT_A, KEPT_MULT, KEPT_R, KEPT_I)]
    return _hosting_call(
        body, "rec_bwd", nt,
        [cols(0), cols(0), cols(1), cols(0)] + kept_cols
        + [halo0, _full((CONV_WIDTH, LRU_WIDTH)), bd, bd, _full((1, LRU_WIDTH))],
        [pl.BlockSpec((tm, 2 * LRU_WIDTH), lambda s: (rev(s), 0)), _full((SUBLANES, LRU_WIDTH)), bd, bd],
        [jax.ShapeDtypeStruct((rows, 2 * LRU_WIDTH), BF16), jax.ShapeDtypeStruct((SUBLANES, LRU_WIDTH), F32),
         jax.ShapeDtypeStruct((2, LRU_HALF, LRU_HALF), F32), jax.ShapeDtypeStruct((2, LRU_HALF, LRU_HALF), F32)],
        [big, big, tile, tile, tile, pltpu.VMEM((SUBLANES, LRU_WIDTH), F32)],
        (drec, zrec, zrec, h) + (kept,) * N_KEPT + (h, conv_w, wa_bd, wx_bd, lam), carried, modes)


DZ_CUTS = (0, ATTN_WIDTH, QKV_WIDTH, IN_WIDTH)


def _dz_specs(tm):
    return [pl.BlockSpec((tm, DZ_CUTS[p + 1] - DZ_CUTS[p]), lambda i: (i, 0)) for p in range(3)]


def _in_proj_bwd_x(head, x, g1, dh1, dq, dkv, drz, w_in_t, carried, modes):
    rows = dh1.shape[0]
    tm = _row_tile(rows)
    steps = rows // tm

    def body(head_ref, g_ref, dh1_ref, dq_ref, dkv_ref, drz_ref, w_ref, x_hbm, dh0_ref, dg_ref, buf, sem):
        i = pl.program_id(0)
        h0 = _h0_tile(head_ref, x_hbm, buf, sem, i, steps, tm)

        @pl.when(i == 0)
        def _():
            dg_ref[...] = jnp.zeros_like(dg_ref)

        g = g_ref[...]
        _, xhat, rstd = _rms_fwd(h0, g)
        parts = (dq_ref[...], dkv_ref[...], drz_ref[...])
        du = sum(_mm(parts[p], w_ref[DZ_CUTS[p]:DZ_CUTS[p + 1], :]) for p in range(3))
        dx, dg = _rms_bwd(du, xhat, rstd, g)
        dh0_ref[...] = dh1_ref[...] + dx
        dg_ref[...] += dg

    wide = pl.BlockSpec((tm, D_MODEL), lambda i: (i, 0))
    return _hosting_call(
        body, "in_proj_bwd_x", steps,
        [_full((BLOCK, D_MODEL)), _full((1, D_MODEL)), wide] + _dz_specs(tm) + [_resident((IN_WIDTH, D_MODEL)), ANY_SPACE],
        [wide, _full((1, D_MODEL))],
        [jax.ShapeDtypeStruct((rows, D_MODEL), F32), jax.ShapeDtypeStruct((1, D_MODEL), F32)],
        _frame_scratch(tm), (head, g1, dh1, dq, dkv, drz, w_in_t, x), carried, modes)


def _in_proj_bwd_w(u1, dq, dkv, drz, carried, modes):
    rows = u1.shape[0]
    tb = _big_tile(rows)
    steps = rows // tb

    def body(u_ref, dq_ref, dkv_ref, drz_ref, dw_ref, acc):
        i = pl.program_id(0)

        @pl.when(i == 0)
        def _():
            acc[...] = jnp.zeros_like(acc)

        u = u_ref[...]
        for p, ref in enumerate((dq_ref, dkv_ref, drz_ref)):
            acc[:, DZ_CUTS[p]:DZ_CUTS[p + 1]] += _mm_tn(u, ref[...])

        @pl.when(i == steps - 1)
        def _():
            dw_ref[...] = acc[...].astype(BF16)

    return _hosting_call(
        body, "in_proj_bwd_w", steps,
        [pl.BlockSpec((tb, D_MODEL), lambda i: (i, 0))] + _dz_specs(tb),
        [_full((D_MODEL, IN_WIDTH))],
        [jax.ShapeDtypeStruct((D_MODEL, IN_WIDTH), BF16)],
        [pltpu.VMEM((D_MODEL, IN_WIDTH), F32)], (u1, dq, dkv, drz), carried, modes)


def _adamw_math(w, m, v, g):
    nm = ADAM_B1 * m + (1.0 - ADAM_B1) * g
    nv = ADAM_B2 * v + (1.0 - ADAM_B2) * (g * g)
    m_hat = nm / (1.0 - ADAM_B1 ** ADAM_STEP)
    v_hat = nv / (1.0 - ADAM_B2 ** ADAM_STEP)
    return (-ADAM_LR) * (m_hat / (jnp.sqrt(v_hat) + ADAM_EPS) + ADAM_WD * w), nm, nv


SMALL_NAMES = ("conv_b", "b_a", "b_x", "lru_lambda", "attn_sinks", "g_post_mix", "g_pre_ffn", "g_post_ffn")
PACK_WIDTH = 1024


def _pack_rows(vals):
    assert len(SMALL_NAMES) == SUBLANES
    row = lax.broadcasted_iota(jnp.int32, (SUBLANES, PACK_WIDTH), 0)
    tile = jnp.zeros((SUBLANES, PACK_WIDTH), F32)
    for k, name in enumerate(SMALL_NAMES):
        a = vals[name].reshape(1, -1)
        tile = jnp.where(row == k, jnp.pad(a, ((0, 0), (0, PACK_WIDTH - a.shape[1]))), tile)
    return tile


def _adamw_small(weights, mom_m, mom_v, parts, loss_parts, others):
    names = list(SMALL_NAMES) + [name for name, _, _ in others]
    views = [(1, weights[name].size) for name in SMALL_NAMES] + [view for _, view, _ in others]
    n, n_pack = len(names), len(SMALL_NAMES)

    def body(*refs):
        w_refs, m_refs, v_refs = refs[:n], refs[n:2 * n], refs[2 * n:3 * n]
        p_ref, l_ref = refs[3 * n], refs[3 * n + 1]
        o_refs = refs[3 * n + 2:3 * n + 2 + len(others)]
        loss_ref, outs = refs[3 * n + 2 + len(others)], refs[3 * n + 3 + len(others):]
        for k, (_, c) in enumerate(views):
            if k < n_pack:
                g = p_ref[0, k:k + 1, 0:c]
                for s in range(1, N_DEV):
                    g = g + p_ref[s, k:k + 1, 0:c]
            else:
                g = o_refs[k - n_pack][0]
                for s in range(1, N_DEV):
                    g = g + o_refs[k - n_pack][s]
            g_ref, d_ref, nm_ref, nv_ref = outs[4 * k:4 * k + 4]
            g_ref[...] = g
            d_ref[...], nm_ref[...], nv_ref[...] = _adamw_math(w_refs[k][...], m_refs[k][...], v_refs[k][...], g)
        total = l_ref[0]
        for s in range(1, N_DEV):
            total = total + l_ref[s]
        loss_ref[...] = total

    args = [src[name].reshape(view) for src in (weights, mom_m, mom_v) for name, view in zip(names, views)]
    res = pl.pallas_call(
        body, name="adamw_small",
        out_shape=[jax.ShapeDtypeStruct(loss_parts.shape[1:], F32)]
                  + [jax.ShapeDtypeStruct(view, F32) for view in views for _ in range(4)],
        compiler_params=pltpu.CompilerParams(vmem_limit_bytes=VMEM_LIMIT),
    )(*args, parts, loss_parts, *[p for _, _, p in others])
    out = {name: tuple(t.reshape(weights[name].shape) for t in res[1 + 4 * k:5 + 4 * k]) for k, name in enumerate(names)}
    return res[0], out


def _adamw(w, m, v, parts, name):
    rows, cols = w.shape
    tr = next((t for t in (256, 128) if rows % t == 0), rows)
    parts = parts if isinstance(parts, (list, tuple)) else [parts]

    def body(w_ref, m_ref, v_ref, *refs):
        p_refs, (g_ref, d_ref, nm_ref, nv_ref) = refs[:len(parts)], refs[len(parts):]

        def total(p_ref):
            g = p_ref[0].astype(F32)
            for s in range(1, N_DEV):
                g = g + p_ref[s].astype(F32)
            return g

        g = jnp.concatenate([total(p_ref) for p_ref in p_refs], axis=1) if len(parts) > 1 else total(p_refs[0])
        g_ref[...] = g
        d_ref[...], nm_ref[...], nv_ref[...] = _adamw_math(w_ref[...], m_ref[...], v_ref[...], g)

    blk = pl.BlockSpec((tr, cols), lambda i: (i, 0))
    return pl.pallas_call(
        body, name=name, grid=(rows // tr,),
        in_specs=[blk, blk, blk] + [pl.BlockSpec((N_DEV, tr, p.shape[2]), lambda i: (0, i, 0)) for p in parts],
        out_specs=[blk] * 4,
        out_shape=[jax.ShapeDtypeStruct((rows, cols), F32)] * 4,
        compiler_params=_params(("parallel",)),
    )(w, m, v, *parts)


def _cols_from_shards(g):
    return jnp.transpose(g, (1, 0, 2)).reshape(g.shape[1], N_DEV * g.shape[2])


def _cols_to_shards(a):
    r, c = a.shape
    return jnp.transpose(a.reshape(r, N_DEV, c // N_DEV), (1, 0, 2))


def _block_diag(w):
    per = LRU_HALF // LRU_BLOCK
    w = w.reshape(2, per, LRU_BLOCK, LRU_BLOCK)
    eye = jnp.eye(per, dtype=w.dtype)
    return (w[:, :, :, None, :] * eye[None, :, None, :, None]).reshape(2, LRU_HALF, LRU_HALF)


def _block_diag_extract(t):
    per = LRU_HALF // LRU_BLOCK
    t = t.reshape(2, per, LRU_BLOCK, per, LRU_BLOCK)
    return jnp.stack([t[:, b, :, b, :] for b in range(per)], axis=1).reshape(LRU_BLOCKS, LRU_BLOCK, LRU_BLOCK)


def kernel(x, meta_tokens, g_pre_mix, w_in, conv_w, conv_b, w_a, b_a, w_x, b_x, lru_lambda, attn_sinks, w_out, g_post_mix, g_pre_ffn, w_ff1, w_ff2, g_post_ffn, loss_target, m_meta_tokens, m_g_pre_mix, m_w_in, m_conv_w, m_conv_b, m_w_a, m_b_a, m_w_x, m_b_x, m_lru_lambda, m_attn_sinks, m_w_out, m_g_post_mix, m_g_pre_ffn, m_w_ff1, m_w_ff2, m_g_post_ffn, v_meta_tokens, v_g_pre_mix, v_w_in, v_conv_w, v_conv_b, v_w_a, v_b_a, v_w_x, v_b_x, v_lru_lambda, v_attn_sinks, v_w_out, v_g_post_mix, v_g_pre_ffn, v_w_ff1, v_w_ff2, v_g_post_ffn):
    weights = dict(meta_tokens=meta_tokens, g_pre_mix=g_pre_mix, w_in=w_in, conv_w=conv_w, conv_b=conv_b, w_a=w_a,
                   b_a=b_a, w_x=w_x, b_x=b_x, lru_lambda=lru_lambda, attn_sinks=attn_sinks, w_out=w_out,
                   g_post_mix=g_post_mix, g_pre_ffn=g_pre_ffn, w_ff1=w_ff1, w_ff2=w_ff2, g_post_ffn=g_post_ffn)
    mom_m = dict(meta_tokens=m_meta_tokens, g_pre_mix=m_g_pre_mix, w_in=m_w_in, conv_w=m_conv_w, conv_b=m_conv_b,
                 w_a=m_w_a, b_a=m_b_a, w_x=m_w_x, b_x=m_b_x, lru_lambda=m_lru_lambda, attn_sinks=m_attn_sinks,
                 w_out=m_w_out, g_post_mix=m_g_post_mix, g_pre_ffn=m_g_pre_ffn, w_ff1=m_w_ff1, w_ff2=m_w_ff2,
                 g_post_ffn=m_g_post_ffn)
    mom_v = dict(meta_tokens=v_meta_tokens, g_pre_mix=v_g_pre_mix, w_in=v_w_in, conv_w=v_conv_w, conv_b=v_conv_b,
                 w_a=v_w_a, b_a=v_b_a, w_x=v_w_x, b_x=v_b_x, lru_lambda=v_lru_lambda, attn_sinks=v_attn_sinks,
                 w_out=v_w_out, g_post_mix=v_g_post_mix, g_pre_ffn=v_g_pre_ffn, w_ff1=v_w_ff1, w_ff2=v_w_ff2,
                 g_post_ffn=v_g_post_ffn)
    order = list(weights)

    (g_win, g_meta, g_cw) = _gather_two_level([w_in[0].astype(BF16), meta_tokens, conv_w[0]], "gather_first")
    w_in_full = _cols_from_shards(g_win)
    meta_full = _cols_from_shards(g_meta)
    conv_w_full = _cols_from_shards(g_cw)

    head = jnp.concatenate([jnp.zeros((PAD_ROWS, D_MODEL), F32), meta_full], axis=0)
    wa_bd = _block_diag(w_a[0]).astype(BF16)
    wx_bd = _block_diag(w_x[0]).astype(BF16)
    bias = _attn_bias()

    w1_shard = w_ff1[0].astype(BF16)
    (qkv, zrec, u1, attn), (g_wout, w1a) = _in_proj_attn_fwd(
        head, x[0], g_pre_mix, w_in_full, attn_sinks, bias,
        [w_out[0].astype(BF16), w1_shard[:, :FF_HALF]], ["gather"] * 2)
    (rec, h_lru, kept), (w1b,) = _rec_fwd(zrec, conv_w_full, conv_b, wa_bd, b_a, wx_bd, b_x, lru_lambda,
                                         [w1_shard[:, FF_HALF:]], ["gather"])
    w_out_full = g_wout.reshape(D_MODEL, D_MODEL)
    w2_shard = w_ff2[0].astype(BF16)
    (mix, h1, act, u2), (w2a, w2b) = _mix_and_ffn_up(
        attn, rec, w_out_full, head, x[0], g_post_mix, g_pre_ffn, (w1a, w1b),
        [w2_shard[:FF_HALF], w2_shard[FF_HALF:]], ["gather"] * 2)
    w2_halves = [w.reshape(D_FF // 2, D_MODEL) for w in (w2a, w2b)]
    (dy, df, dg_post_ffn, loss_acc), w2t_halves = _ffn_down_loss(
        act, w2_halves, h1, loss_target[0], g_post_ffn, [w2_shard[:FF_HALF].T, w2_shard[FF_HALF:].T], ["gather"] * 2)

    (da1,), (w1ta,) = _ffn_bwd_act(df, w2t_halves, act, [w1_shard[:, :FF_HALF].T], ["gather"])
    (dw1h, dw2g), (w1tb,) = _ffn_bwd_weights(u2, da1, act, df, [w1_shard[:, FF_HALF:].T], ["gather"])
    w1t_halves = [w.reshape(D_FF // 2, D_MODEL) for w in (w1ta, w1tb)]
    (dh1, dg_pre_ffn), (p_w1a,) = _ffn_bwd_x(da1, w1t_halves, h1, dy, g_pre_ffn, [dw1h[0]], ["scatter"])
    (dattn, drec, dw_out, dg_post_mix), (p_w1b,) = _out_proj_bwd(dh1, mix, g_post_mix, w_out_full.T, attn, rec,
                                                                [dw1h[1]], ["scatter"])
    (dq, dkv, dsinks), (p_w2,) = _attn_bwd(qkv, dattn, attn_sinks, bias, [dw2g], ["scatter"])
    (drz, rec_small, dwa_bd, dwx_bd), (p_wout,) = _rec_bwd(
        drec, zrec, h_lru, kept, conv_w_full, wa_bd, wx_bd, lru_lambda,
        [dw_out.reshape(N_DEV, D_MODEL // N_DEV, D_MODEL)], ["scatter"])
    small_grads = dict(
        conv_b=rec_small[ROW_CONV_B], b_a=rec_small[ROW_B_A], b_x=rec_small[ROW_B_X], lru_lambda=rec_small[ROW_LAMBDA],
        attn_sinks=dsinks[:, 0], g_post_mix=dg_post_mix, g_pre_ffn=dg_pre_ffn, g_post_ffn=dg_post_ffn)
    gate_rows = (LRU_BLOCKS * LRU_BLOCK, LRU_BLOCK)
    gate_dense = (LRU_BLOCKS * LRU_BLOCK * LRU_BLOCK // PACK_WIDTH, PACK_WIDTH)
    (dw_in,), (p_cw, p_small, p_wa, p_wx) = _in_proj_bwd_w(
        u1, dq, dkv, drz,
        [_cols_to_shards(rec_small[0:CONV_WIDTH]), _pack_rows(small_grads),
         _block_diag_extract(dwa_bd).reshape(gate_dense), _block_diag_extract(dwx_bd).reshape(gate_dense)],
        ["scatter", "gather", "gather", "gather"])
    p_wa, p_wx = (p.reshape((N_DEV,) + gate_rows) for p in (p_wa, p_wx))
    (dh0, dg_pre_mix), (p_win,) = _in_proj_bwd_x(
        head, x[0], g_pre_mix, dh1, dq, dkv, drz, w_in_full.T, [_cols_to_shards(dw_in)], ["scatter"])
    p_meta, p_gpm, p_loss = _exchange([_cols_to_shards(dh0[PAD_ROWS:BLOCK]), dg_pre_mix, loss_acc],
                                      ["scatter", "gather", "gather"], "exchange_last")

    res = {}
    res["w_in"] = _adamw(w_in[0], m_w_in[0], v_w_in[0], p_win, "adamw_w_in")
    res["w_out"] = _adamw(w_out[0], m_w_out[0], v_w_out[0], p_wout, "adamw_w_out")
    res["w_ff1"] = _adamw(w_ff1[0], m_w_ff1[0], v_w_ff1[0], [p_w1a, p_w1b], "adamw_w_ff1")
    res["w_ff2"] = _adamw(w_ff2[0], m_w_ff2[0], v_w_ff2[0], p_w2, "adamw_w_ff2")
    for name in ("w_in", "w_out", "w_ff1", "w_ff2"):
        res[name] = tuple(t[None] for t in res[name])
    others = [("g_pre_mix", g_pre_mix.shape, p_gpm), ("meta_tokens", meta_tokens.shape, p_meta),
              ("conv_w", conv_w.shape[1:], p_cw), ("w_a", gate_rows, p_wa), ("w_x", gate_rows, p_wx)]
    loss_total, small = _adamw_small(weights, mom_m, mom_v, p_small, p_loss, others)
    res.update(small)

    grad_x = dh0[BLOCK:][None]
    outs = [loss_total[0, 0], grad_x]
    for k in range(4):
        outs += [res[name][k] for name in order]
    return tuple(outs)
```

```python
import jax
import jax.numpy as jnp
import numpy as np
from jax import lax
from jax.experimental import pallas as pl
from jax.experimental.pallas import tpu as pltpu

F32 = jnp.float32
BF16 = jnp.bfloat16

D_MODEL = 1024
N_META = 16
HEAD_DIM = 64
ATTN_HEADS = 8
KV_HEADS = 2
GQA_GROUP = ATTN_HEADS // KV_HEADS
ATTN_WIDTH = ATTN_HEADS * HEAD_DIM
KV_WIDTH = KV_HEADS * HEAD_DIM
QKV_WIDTH = ATTN_WIDTH + 2 * KV_WIDTH
LRU_WIDTH = 512
LRU_BLOCKS = 8
LRU_BLOCK = 64
LRU_HALF = 256
LRU_C = 8.0
CONV_WIDTH = 4
BLOCK = 128
PAD_ROWS = BLOCK - N_META
IN_WIDTH = QKV_WIDTH + 2 * LRU_WIDTH
D_FF = 4096
EPS = 1e-6
NEG = -1e30
N_DEV = 8
FF_CHUNK = D_FF // N_DEV
SUBLANES = 8
LANES = 128

ADAM_LR = 0.001
ADAM_B1 = 0.9
ADAM_B2 = 0.999
ADAM_EPS = 1e-08
ADAM_WD = 0.01
ADAM_STEP = 10

VMEM_LIMIT = 56 * 1024 * 1024


def _row_tile(rows):
    for t in (640, 512, 256, 128):
        if rows % t == 0:
            return t
    raise ValueError(rows)


def _big_tile(rows):
    for t in (1664, 1024, 512, 256, 128):
        if rows % t == 0:
            return t
    raise ValueError(rows)


def _rec_tile(rows):
    for t in (640, 256, 128):
        if rows % t == 0:
            return t
    raise ValueError(rows)


def _params(semantics):
    return pltpu.CompilerParams(dimension_semantics=semantics, vmem_limit_bytes=VMEM_LIMIT)


def _mm(a, b):
    return lax.dot_general(a, b, (((1,), (0,)), ((), ())), preferred_element_type=F32)


def _mm_nt(a, b):
    return lax.dot_general(a, b, (((1,), (1,)), ((), ())), preferred_element_type=F32)


def _mm_tn(a, b):
    return lax.dot_general(a, b, (((0,), (0,)), ((), ())), preferred_element_type=F32)


def _rms_fwd(x, g):
    rstd = lax.rsqrt(jnp.mean(x * x, axis=-1, keepdims=True) + EPS)
    xhat = x * rstd
    return xhat * g, xhat, rstd


def _rms_bwd(dy, xhat, rstd, g):
    dyg = dy * g
    c = jnp.mean(dyg * xhat, axis=-1, keepdims=True)
    dx = rstd * (dyg - xhat * c)
    dg = jnp.sum(dy * xhat, axis=0, keepdims=True)
    return dx, dg


def _sigmoid(x):
    return pl.reciprocal(1.0 + jnp.exp(-x), approx=True)


def _log1p(x):
    u = 1.0 + x
    return jnp.where(u == 1.0, x, jnp.log(u) * x / (u - 1.0))


def _one_minus_sq_exp(x, ex):
    return -jnp.tanh(x) * (1.0 + ex * ex)


TINY = 1e-30


def _sqrt_pos(y):
    r = lax.rsqrt(jnp.maximum(y, TINY))
    return y * r, r


def _softplus(x):
    return jnp.maximum(x, 0.0) + _log1p(jnp.exp(-jnp.abs(x)))


GELU_C = 0.7978845608028654
GELU_K = 0.044715


def _gelu(x):
    t = jnp.tanh(GELU_C * (x + GELU_K * x * x * x))
    return 0.5 * x * (1.0 + t), t


def _gelu_grad(x, t):
    return 0.5 * (1.0 + t) + 0.5 * x * (1.0 - t * t) * GELU_C * (1.0 + 3.0 * GELU_K * x * x)


def _full(shape):
    return pl.BlockSpec(shape, lambda *_: (0,) * len(shape))


def _resident(shape):
    return pl.BlockSpec(shape, lambda *_: (0,) * len(shape), pipeline_mode=pl.Buffered(1))


def _exchange_copies(ins, outs, sems, modes):
    send_sems, recv_sems, local_sems = sems
    x, y, c = lax.axis_index("x"), lax.axis_index("y"), lax.axis_index("c")
    me = 4 * x + 2 * y + c

    def block(a, dev):
        return ins[a] if modes[a] == "gather" else ins[a].at[dev]

    local = [pltpu.make_async_copy(block(a, me), outs[a].at[me], local_sems.at[a]) for a in range(len(ins))]
    sends, recvs = [], []
    for a in range(len(ins)):
        for k in range(N_DEV - 1):
            bits = k + 1
            px = jnp.bitwise_xor(x, (bits >> 2) & 1)
            py = jnp.bitwise_xor(y, (bits >> 1) & 1)
            pc = jnp.bitwise_xor(c, bits & 1)
            peer = 4 * px + 2 * py + pc
            common = dict(src_ref=block(a, peer), send_sem=send_sems.at[a, k], recv_sem=recv_sems.at[a, k],
                          device_id=(px, py, pc), device_id_type=pl.DeviceIdType.MESH)
            sends.append(pltpu.make_async_remote_copy(dst_ref=outs[a].at[me], **common))
            recvs.append(pltpu.make_async_remote_copy(dst_ref=outs[a].at[peer], **common))
    return local, sends, recvs


def _exchange_start(ins, outs, sems, modes):
    local, sends, _ = _exchange_copies(ins, outs, sems, modes)
    for cp in local + sends:
        cp.start()


def _exchange_wait(ins, outs, sems, modes):
    local, sends, recvs = _exchange_copies(ins, outs, sems, modes)
    for cp in recvs:
        cp.wait_recv()
    for cp in sends:
        cp.wait_send()
    for cp in local:
        cp.wait()


def _exchange_shapes(arrays, modes):
    return [jax.ShapeDtypeStruct((N_DEV,) + a.shape if mode == "gather" else a.shape, a.dtype)
            for a, mode in zip(arrays, modes)]


def _exchange_sems(na):
    return [pltpu.SemaphoreType.DMA((na, N_DEV - 1)), pltpu.SemaphoreType.DMA((na, N_DEV - 1)),
            pltpu.SemaphoreType.DMA((na,))]


ANY_SPACE = pl.BlockSpec(memory_space=pl.ANY)


def _exchange(arrays, modes, name):
    na = len(arrays)

    def body(*refs):
        ins, outs, sems = refs[:na], refs[na:2 * na], refs[2 * na:]
        _exchange_start(ins, outs, sems, modes)
        _exchange_wait(ins, outs, sems, modes)

    return pl.pallas_call(
        body, name=name, out_shape=_exchange_shapes(arrays, modes),
        in_specs=[ANY_SPACE] * na, out_specs=[ANY_SPACE] * na, scratch_shapes=_exchange_sems(na),
        compiler_params=pltpu.CompilerParams(has_side_effects=True),
    )(*arrays)


def _gather_two_level(arrays, name):
    na = len(arrays)

    def body(*refs):
        ins, outs = refs[:na], refs[na:2 * na]
        send_sems, recv_sems, local_sems = refs[2 * na:]
        x, y, c = lax.axis_index("x"), lax.axis_index("y"), lax.axis_index("c")
        me, sibling = (x, y, c), (x, y, 1 - c)
        chips = [(1 - x, y), (x, 1 - y), (1 - x, 1 - y)]

        def copy(a, k, block, to, src=None):
            slot = outs[a].at[4 * block[0] + 2 * block[1] + block[2]]
            return pltpu.make_async_remote_copy(
                src_ref=slot if src is None else src, dst_ref=slot, send_sem=send_sems.at[a, k],
                recv_sem=recv_sems.at[a, k], device_id=to, device_id_type=pl.DeviceIdType.MESH)

        local = [pltpu.make_async_copy(ins[a], outs[a].at[4 * x + 2 * y + c], local_sems.at[a]) for a in range(na)]
        first = []
        for a in range(na):
            first.append(copy(a, 0, me, sibling, src=ins[a]))
            first += [copy(a, 1 + j, me, (*chip, c), src=ins[a]) for j, chip in enumerate(chips)]
        for cp in local + first:
            cp.start()
        passed = []
        for j, chip in enumerate(chips):
            for a in range(na):
                copy(a, 1 + j, (*chip, c), me).wait_recv()
                passed.append(copy(a, 4 + j, (*chip, c), sibling))
                passed[-1].start()
        for a in range(na):
            copy(a, 0, sibling, me).wait_recv()
            for j, chip in enumerate(chips):
                copy(a, 4 + j, (*chip, 1 - c), me).wait_recv()
        for cp in first + passed:
            cp.wait_send()
        for cp in local:
            cp.wait()

    return pl.pallas_call(
        body, name=name, out_shape=_exchange_shapes(arrays, ["gather"] * na),
        in_specs=[ANY_SPACE] * na, out_specs=[ANY_SPACE] * na, scratch_shapes=_exchange_sems(na),
        compiler_params=pltpu.CompilerParams(has_side_effects=True),
    )(*arrays)


def _hosting_call(body, name, steps, in_specs, out_specs, out_shape, scratch_shapes, args, arrays, modes):
    n_in, n_out, n_scr, na = len(in_specs), len(out_specs), len(scratch_shapes), len(arrays)
    grid = steps if isinstance(steps, tuple) else (steps,)

    def hosting_body(*refs):
        cuts = [0]
        for n in (n_in, na, n_out, na, n_scr, 3):
            cuts.append(cuts[-1] + n)
        ins, x_ins, outs, x_outs, scr, sems = (refs[cuts[p]:cuts[p + 1]] for p in range(6))
        first, last = True, True
        for axis, n in enumerate(grid):
            first = first & (pl.program_id(axis) == 0)
            last = last & (pl.program_id(axis) == n - 1)

        @pl.when(first)
        def _():
            _exchange_start(x_ins, x_outs, sems, modes)

        body(*ins, *outs, *scr)

        @pl.when(last)
        def _():
            _exchange_wait(x_ins, x_outs, sems, modes)

    res = pl.pallas_call(
        hosting_body, name=name, grid=grid,
        in_specs=list(in_specs) + [ANY_SPACE] * na, out_specs=list(out_specs) + [ANY_SPACE] * na,
        out_shape=list(out_shape) + _exchange_shapes(arrays, modes),
        scratch_shapes=list(scratch_shapes) + _exchange_sems(na),
        compiler_params=_params(("arbitrary",) * len(grid)),
    )(*args, *arrays)
    return res[:n_out], res[n_out:]


def _frame_rows(src_hbm, buf, sem, i, steps, tm):
    def first():
        return pltpu.make_async_copy(src_hbm.at[pl.ds(0, tm - BLOCK)], buf.at[0, pl.ds(BLOCK, tm - BLOCK)], sem.at[0])

    def later(t, slot):
        return pltpu.make_async_copy(src_hbm.at[pl.ds(pl.multiple_of(t * tm - BLOCK, SUBLANES), tm)], buf.at[slot], sem.at[slot])

    slot = i % 2

    @pl.when(i == 0)
    def _():
        first().start()

    @pl.when(i + 1 < steps)
    def _():
        later(i + 1, 1 - slot).start()

    @pl.when(i == 0)
    def _():
        first().wait()

    @pl.when(i > 0)
    def _():
        later(i, slot).wait()

    return slot


def _frame_scratch(tm):
    return [pltpu.VMEM((2, tm, D_MODEL), F32), pltpu.SemaphoreType.DMA((2,))]


def _h0_tile(head_ref, x_hbm, buf, sem, i, steps, tm):
    slot = _frame_rows(x_hbm, buf, sem, i, steps, tm)

    @pl.when(i == 0)
    def _():
        buf[0, 0:BLOCK, :] = head_ref[...]

    return buf[slot]


N_BIAS = 3


def _attn_bias():
    key = np.arange(2 * BLOCK)[:, None]
    r = np.arange(GQA_GROUP * BLOCK)[None, :] % BLOCK
    band = (key > r) & (key <= r + BLOCK)
    out = [np.where(band & ((n - 1) * BLOCK + key >= PAD_ROWS), 0.0, NEG) for n in range(N_BIAS)]
    return jnp.asarray(np.stack(out), F32)


def _attn_probs(k2, q4, bias, sink_row):
    s = _mm_nt(k2, q4) * (HEAD_DIM ** -0.5) + bias
    m = jnp.maximum(jnp.max(s, axis=0, keepdims=True), sink_row)
    p = jnp.exp(s - m)
    es = jnp.exp(sink_row - m)
    inv = 1.0 / (jnp.sum(p, axis=0, keepdims=True) + es)
    return p * inv, es * inv


def _heads(ref, rows, first, count):
    return jnp.concatenate([ref[rows, (first + g) * HEAD_DIM:(first + g + 1) * HEAD_DIM] for g in range(count)], axis=0)


def _keys_of_block(prev_ref, cur_ref, b, kv):
    sl = slice(kv * HEAD_DIM, (kv + 1) * HEAD_DIM)
    before = prev_ref[:, sl] if b == 0 else cur_ref[(b - 1) * BLOCK:b * BLOCK, sl]
    return jnp.concatenate([before, cur_ref[b * BLOCK:(b + 1) * BLOCK, sl]], axis=0)


def _bias_of_block(bias_ref, block):
    return bias_ref[jnp.minimum(block, N_BIAS - 1)]


def _sink_row(sink_ref, kv):
    g = lax.broadcasted_iota(jnp.int32, (1, GQA_GROUP * BLOCK), 1) // BLOCK
    row = jnp.full((1, GQA_GROUP * BLOCK), sink_ref[0, kv * GQA_GROUP], F32)
    for i in range(1, GQA_GROUP):
        row = jnp.where(g == i, sink_ref[0, kv * GQA_GROUP + i], row)
    return row


def _from_head_major(pieces):
    return jnp.concatenate(pieces, axis=0).T


def _attn_specs(tm, tile_of):
    nbt = tm // BLOCK
    k_col, v_col = ATTN_WIDTH // KV_WIDTH, ATTN_WIDTH // KV_WIDTH + 1
    before = lambda i: jnp.maximum(tile_of(i) * nbt - 1, 0)
    return [pl.BlockSpec((tm, ATTN_WIDTH), lambda i: (tile_of(i), 0)),
            pl.BlockSpec((BLOCK, KV_WIDTH), lambda i: (before(i), k_col)),
            pl.BlockSpec((tm, KV_WIDTH), lambda i: (tile_of(i), k_col)),
            pl.BlockSpec((BLOCK, KV_WIDTH), lambda i: (before(i), v_col)),
            pl.BlockSpec((tm, KV_WIDTH), lambda i: (tile_of(i), v_col))]


def _in_proj_attn_fwd(head, x, g1, w_in, sinks, bias, carried, modes):
    rows = BLOCK + x.shape[0]
    tm = _row_tile(rows)
    steps, nbt = rows // tm, tm // BLOCK

    def body(head_ref, g_ref, w_ref, sink_ref, bias_ref, x_hbm, qkv_ref, zrec_ref, u_ref, o_ref, buf, sem, kv_before):
        i = pl.program_id(0)
        h = _h0_tile(head_ref, x_hbm, buf, sem, i, steps, tm)
        u, _, _ = _rms_fwd(h, g_ref[...])
        u = u.astype(BF16)
        u_ref[...] = u
        z = _mm(u, w_ref[...])
        qkv_ref[...] = z[:, :QKV_WIDTH].astype(BF16)
        zrec_ref[...] = z[:, QKV_WIDTH:]

        @pl.when(i == 0)
        def _():
            kv_before[...] = jnp.zeros_like(kv_before)

        kc_ref, vc_ref = (qkv_ref.at[:, pl.ds(ATTN_WIDTH + c * KV_WIDTH, KV_WIDTH)] for c in range(2))
        kp_ref, vp_ref = (kv_before.at[:, pl.ds(c * KV_WIDTH, KV_WIDTH)] for c in range(2))
        for b in range(nbt):
            blk = slice(b * BLOCK, (b + 1) * BLOCK)
            bias_t = _bias_of_block(bias_ref, i * nbt + b)
            pieces = []
            for kv in range(KV_HEADS):
                k2 = _keys_of_block(kp_ref, kc_ref, b, kv)
                v2 = _keys_of_block(vp_ref, vc_ref, b, kv)
                q4 = _heads(qkv_ref, blk, kv * GQA_GROUP, GQA_GROUP)
                pn, _ = _attn_probs(k2, q4, bias_t, _sink_row(sink_ref, kv))
                ot = _mm_tn(v2, pn.astype(BF16))
                pieces += [ot[:, g * BLOCK:(g + 1) * BLOCK] for g in range(GQA_GROUP)]
            o_ref[blk, :] = _from_head_major(pieces).astype(BF16)
        kv_before[...] = qkv_ref[tm - BLOCK:tm, ATTN_WIDTH:]

    wide = pl.BlockSpec((tm, D_MODEL), lambda i: (i, 0))
    return _hosting_call(
        body, "in_proj_attn_fwd", steps,
        [_full((BLOCK, D_MODEL)), _full((1, D_MODEL)), _resident((D_MODEL, IN_WIDTH)), pl.BlockSpec(memory_space=pltpu.SMEM),
         _resident((N_BIAS, 2 * BLOCK, GQA_GROUP * BLOCK)), ANY_SPACE],
        [pl.BlockSpec((tm, QKV_WIDTH), lambda i: (i, 0)), pl.BlockSpec((tm, 2 * LRU_WIDTH), lambda i: (i, 0)), wide,
         pl.BlockSpec((tm, ATTN_WIDTH), lambda i: (i, 0))],
        [jax.ShapeDtypeStruct((rows, QKV_WIDTH), BF16), jax.ShapeDtypeStruct((rows, 2 * LRU_WIDTH), F32),
         jax.ShapeDtypeStruct((rows, D_MODEL), BF16), jax.ShapeDtypeStruct((rows, ATTN_WIDTH), BF16)],
        _frame_scratch(tm) + [pltpu.VMEM((BLOCK, 2 * KV_WIDTH), BF16)],
        (head, g1, w_in, sinks, bias, x), carried, modes)


def _conv_taps(xbuf, tm):
    return [xbuf[pl.ds(SUBLANES - (CONV_WIDTH - 1 - j), tm), :] for j in range(CONV_WIDTH)]


def _lru_halves(xc):
    return [xc[:, h * LRU_HALF:(h + 1) * LRU_HALF].astype(BF16) for h in range(2)]


def _lru_gates(xc, wa_ref, ba_ref, wx_ref, bx_ref, lam_ref):
    halves = _lru_halves(xc)
    gate_r = jnp.concatenate([_mm(halves[h], wa_ref[h]) for h in range(2)], axis=1) + ba_ref[...]
    gate_i = jnp.concatenate([_mm(halves[h], wx_ref[h]) for h in range(2)], axis=1) + bx_ref[...]
    r = _sigmoid(gate_r)
    ig = _sigmoid(gate_i)
    log_a = (-LRU_C) * r * _softplus(-lam_ref[...])
    a = jnp.exp(log_a)
    mult, _ = _sqrt_pos(_one_minus_sq_exp(log_a, a))
    return r, ig, a, mult


KEPT_XC, KEPT_A, KEPT_MULT, KEPT_R, KEPT_I, N_KEPT = 0, 1, 2, 3, 4, 5


def _scan_tile(a_ref, u_ref, out_ref, carry, tm):
    row = lax.broadcasted_iota(jnp.int32, (SUBLANES, LRU_WIDTH), 0)

    def step(j, before):
        o = pl.multiple_of(j * SUBLANES, SUBLANES)
        a = a_ref[pl.ds(o, SUBLANES), :]
        u = u_ref[pl.ds(o, SUBLANES), :]
        for s in (1, 2, 4):
            keep = row >= s
            u = jnp.where(keep, a * pltpu.roll(u, s, 0) + u, u)
            a = jnp.where(keep, a * pltpu.roll(a, s, 0), a)
        out = a * before + u
        out_ref[pl.ds(o, SUBLANES), :] = out
        return out[SUBLANES - 1:SUBLANES, :]

    return lax.fori_loop(0, tm // SUBLANES, step, carry)


def _rec_fwd(zrec, conv_w, conv_b, wa_bd, b_a, wx_bd, b_x, lam, carried, modes):
    rows = zrec.shape[0]
    tm = _row_tile(rows)

    def body(xr_ref, yr_ref, cw_ref, cb_ref, wa_ref, ba_ref, wx_ref, bx_ref, lam_ref, rec_ref, h_ref, kept_ref,
             xbuf, a_s, u_s, carry):
        i = pl.program_id(0)

        @pl.when(i == 0)
        def _():
            xbuf[0:SUBLANES, :] = jnp.zeros((SUBLANES, LRU_WIDTH), F32)
            carry[...] = jnp.zeros_like(carry)

        @pl.when(i > 0)
        def _():
            xbuf[0:SUBLANES, :] = xbuf[tm:tm + SUBLANES, :]

        xbuf[SUBLANES:SUBLANES + tm, :] = xr_ref[...]
        taps = _conv_taps(xbuf, tm)
        xc = cb_ref[...] + sum(cw_ref[j:j + 1, :] * taps[j] for j in range(CONV_WIDTH))
        r, ig, a, mult = _lru_gates(xc, wa_ref, ba_ref, wx_ref, bx_ref, lam_ref)
        for k, val in ((KEPT_XC, xc), (KEPT_A, a), (KEPT_MULT, mult), (KEPT_R, r), (KEPT_I, ig)):
            kept_ref[:, k * LRU_WIDTH:(k + 1) * LRU_WIDTH] = val
        grow = i * tm + lax.broadcasted_iota(jnp.int32, (tm, LRU_WIDTH), 0)
        a_s[...] = a
        u_s[...] = jnp.where(grow >= PAD_ROWS, mult * (ig * xc), 0.0)
        carry[0:1, :] = _scan_tile(a_s, u_s, h_ref, carry[0:1, :], tm)
        gel, _ = _gelu(yr_ref[...])
        rec_ref[...] = (gel * h_ref[...]).astype(BF16)

    vec = _full((1, LRU_WIDTH))
    bd = _full((2, LRU_HALF, LRU_HALF))
    return _hosting_call(
        body, "rec_fwd", rows // tm,
        [pl.BlockSpec((tm, LRU_WIDTH), lambda i: (i, 0)), pl.BlockSpec((tm, LRU_WIDTH), lambda i: (i, 1)),
         _full((CONV_WIDTH, LRU_WIDTH)), vec, bd, vec, bd, vec, vec],
        [pl.BlockSpec((tm, LRU_WIDTH), lambda i: (i, 0))] * 2 + [pl.BlockSpec((tm, N_KEPT * LRU_WIDTH), lambda i: (i, 0))],
        [jax.ShapeDtypeStruct((rows, LRU_WIDTH), BF16), jax.ShapeDtypeStruct((rows, LRU_WIDTH), F32),
         jax.ShapeDtypeStruct((rows, N_KEPT * LRU_WIDTH), F32)],
        [pltpu.VMEM((tm + SUBLANES, LRU_WIDTH), F32), pltpu.VMEM((tm, LRU_WIDTH), F32),
         pltpu.VMEM((tm, LRU_WIDTH), F32), pltpu.VMEM((SUBLANES, LRU_WIDTH), F32)],
        (zrec, zrec, conv_w, conv_b, wa_bd, b_a, wx_bd, b_x, lam), carried, modes)


FF_COLS = 1024
FF_HALF = FF_CHUNK // 2


def _hidden_at(d, half):
    return half * (D_FF // 2) + d * FF_HALF


def _mix_and_ffn_up(attn, rec, w_out, head, x, g2, g3, w1_halves, carried, modes):
    rows = attn.shape[0]
    tm = _row_tile(rows)
    steps = rows // tm

    def body(attn_ref, rec_ref, w_ref, head_ref, g2_ref, g3_ref, wa_ref, wb_ref, x_hbm,
             mix_ref, h1_ref, act_ref, u_ref, buf, sem):
        h0 = _h0_tile(head_ref, x_hbm, buf, sem, pl.program_id(0), steps, tm)
        mix = _mm(attn_ref[...], w_ref[0:ATTN_WIDTH, :]) + _mm(rec_ref[...], w_ref[ATTN_WIDTH:, :])
        y, _, _ = _rms_fwd(mix, g2_ref[...])
        mix_ref[...] = mix
        h1_ref[...] = h0 + y
        u, _, _ = _rms_fwd(h1_ref[...], g3_ref[...])
        u = u.astype(BF16)
        u_ref[...] = u
        for half, w1_ref in enumerate((wa_ref, wb_ref)):
            for d in range(N_DEV):
                c = _hidden_at(d, half)
                a1 = jnp.maximum(_mm(u, w1_ref[d]), 0.0)
                act_ref[:, c:c + FF_HALF] = (a1 * a1).astype(BF16)

    half_in = pl.BlockSpec((tm, ATTN_WIDTH), lambda i: (i, 0))
    wide = pl.BlockSpec((tm, D_MODEL), lambda i: (i, 0))
    return _hosting_call(
        body, "mix_and_ffn_up", steps,
        [half_in, half_in, _resident((D_MODEL, D_MODEL)), _full((BLOCK, D_MODEL)), _full((1, D_MODEL)), _full((1, D_MODEL))]
        + [_resident((N_DEV, D_MODEL, FF_HALF))] * 2 + [ANY_SPACE],
        [wide, wide, pl.BlockSpec((tm, D_FF), lambda i: (i, 0)), wide],
        [jax.ShapeDtypeStruct((rows, D_MODEL), F32)] * 2
        + [jax.ShapeDtypeStruct((rows, D_FF), BF16), jax.ShapeDtypeStruct((rows, D_MODEL), BF16)],
        _frame_scratch(tm), (attn, rec, w_out, head, g2, g3, *w1_halves, x), carried, modes)


def _ffn_down_loss(act, w2_halves, h1, target, g4, carried, modes):
    rows = h1.shape[0]
    tm = _row_tile(rows)
    steps = rows // tm
    kh = D_FF // 2

    def body(act_ref, wa_ref, wb_ref, h_ref, g_ref, t_hbm, dy_ref, df_ref, dg_ref, loss_ref, buf, sem):
        i = pl.program_id(0)
        slot = _frame_rows(t_hbm, buf, sem, i, steps, tm)

        @pl.when(i == 0)
        def _():
            dg_ref[...] = jnp.zeros_like(dg_ref)
            loss_ref[...] = jnp.zeros_like(loss_ref)
            buf[0, 0:BLOCK, :] = jnp.zeros((BLOCK, D_MODEL), F32)

        g = g_ref[...]
        f = _mm(act_ref[:, :kh], wa_ref[...]) + _mm(act_ref[:, kh:], wb_ref[...])
        y, fhat, rstd = _rms_fwd(f, g)
        grow = i * tm + lax.broadcasted_iota(jnp.int32, (tm, D_MODEL), 0)
        err = jnp.where(grow >= BLOCK, h_ref[...] + y - buf[slot], 0.0)
        loss_ref[...] += (0.5 / D_MODEL) * jnp.sum(err * err)
        dy = err * (1.0 / D_MODEL)
        df, dg = _rms_bwd(dy, fhat, rstd, g)
        dy_ref[...] = dy
        df_ref[...] = df.astype(BF16)
        dg_ref[...] += dg

    wide = pl.BlockSpec((tm, D_MODEL), lambda i: (i, 0))
    return _hosting_call(
        body, "ffn_down_loss", steps,
        [pl.BlockSpec((tm, D_FF), lambda i: (i, 0)), _resident((kh, D_MODEL)), _resident((kh, D_MODEL)), wide,
         _full((1, D_MODEL)), ANY_SPACE],
        [wide, wide, _full((1, D_MODEL)), _full((SUBLANES, LANES))],
        [jax.ShapeDtypeStruct((rows, D_MODEL), F32), jax.ShapeDtypeStruct((rows, D_MODEL), BF16),
         jax.ShapeDtypeStruct((1, D_MODEL), F32), jax.ShapeDtypeStruct((SUBLANES, LANES), F32)],
        _frame_scratch(tm), (act, *w2_halves, h1, g4, target), carried, modes)


def _ffn_bwd_act(df, w2t_halves, act, carried, modes):
    rows = df.shape[0]
    tm = _row_tile(rows)

    def body(df_ref, wa_ref, wb_ref, act_ref, da_ref):
        df_t = df_ref[...]
        for half, w_ref in enumerate((wa_ref, wb_ref)):
            for d in range(N_DEV):
                cols = slice(_hidden_at(d, half), _hidden_at(d, half) + FF_HALF)
                dact = _mm(df_t, w_ref[d])
                relu_a1, _ = _sqrt_pos(act_ref[:, cols].astype(F32))
                da_ref[:, cols] = (dact * (2.0 * relu_a1)).astype(BF16)

    hidden = pl.BlockSpec((tm, D_FF), lambda i: (i, 0))
    return _hosting_call(
        body, "ffn_bwd_act", rows // tm,
        [pl.BlockSpec((tm, D_MODEL), lambda i: (i, 0))] + [_resident((N_DEV, D_MODEL, FF_HALF))] * 2 + [hidden],
        [hidden],
        [jax.ShapeDtypeStruct((rows, D_FF), BF16)],
        [], (df, *w2t_halves, act), carried, modes)


def _ffn_bwd_x(da, w1t_halves, h1, dy, g3, carried, modes):
    rows = h1.shape[0]
    tm = _row_tile(rows)
    kh = D_FF // 2

    def body(da_ref, wa_ref, wb_ref, h_ref, dy_ref, g_ref, dh_ref, dg_ref):
        @pl.when(pl.program_id(0) == 0)
        def _():
            dg_ref[...] = jnp.zeros_like(dg_ref)

        g = g_ref[...]
        _, xhat, rstd = _rms_fwd(h_ref[...], g)
        du = _mm(da_ref[:, :kh], wa_ref[...]) + _mm(da_ref[:, kh:], wb_ref[...])
        dx, dg = _rms_bwd(du, xhat, rstd, g)
        dh_ref[...] = dy_ref[...] + dx
        dg_ref[...] += dg

    wide = pl.BlockSpec((tm, D_MODEL), lambda i: (i, 0))
    return _hosting_call(
        body, "ffn_bwd_x", rows // tm,
        [pl.BlockSpec((tm, D_FF), lambda i: (i, 0)), _resident((kh, D_MODEL)), _resident((kh, D_MODEL)), wide, wide,
         _full((1, D_MODEL))],
        [wide, _full((1, D_MODEL))],
        [jax.ShapeDtypeStruct((rows, D_MODEL), F32), jax.ShapeDtypeStruct((1, D_MODEL), F32)],
        [], (da, *w1t_halves, h1, dy, g3), carried, modes)


def _ffn_bwd_weights(u2, da, act, df, carried, modes):
    rows = u2.shape[0]
    tb = _big_tile(rows)
    steps = rows // tb
    per = FF_COLS // FF_HALF

    def body(u_ref, da_ref, act_ref, df_ref, dw1_ref, dw2_ref, acc1, acc2):
        i = pl.program_id(1)

        @pl.when(i == 0)
        def _():
            acc1[...] = jnp.zeros_like(acc1)
            acc2[...] = jnp.zeros_like(acc2)

        acc1[...] += _mm_tn(u_ref[...], da_ref[...])
        acc2[...] += _mm_tn(act_ref[...], df_ref[...])

        @pl.when(i == steps - 1)
        def _():
            for p in range(per):
                c = p * FF_HALF
                dw1_ref[p] = acc1[:, c:c + FF_HALF].astype(BF16)
                dw2_ref[p] = acc2[c:c + FF_HALF, :].astype(BF16)

    wide = pl.BlockSpec((tb, D_MODEL), lambda j, i: (i, 0))
    chunk = pl.BlockSpec((tb, FF_COLS), lambda j, i: (i, j))
    return _hosting_call(
        body, "ffn_bwd_weights", (D_FF // FF_COLS, steps),
        [wide, chunk, chunk, wide],
        [pl.BlockSpec((None, per, D_MODEL, FF_HALF), lambda j, i: (j // 2, j % 2, 0, 0)),
         pl.BlockSpec((per, FF_HALF, D_MODEL), lambda j, i: (j % 2, j // 2, 0))],
        [jax.ShapeDtypeStruct((2, N_DEV, D_MODEL, FF_HALF), BF16), jax.ShapeDtypeStruct((N_DEV, FF_CHUNK, D_MODEL), BF16)],
        [pltpu.VMEM((D_MODEL, FF_COLS), F32), pltpu.VMEM((FF_COLS, D_MODEL), F32)],
        (u2, da, act, df), carried, modes)


def _out_proj_bwd(dh1, mix, g2, w_out_t, attn, rec, carried, modes):
    rows = dh1.shape[0]
    tm = _row_tile(rows)
    steps = rows // tm

    def body(dh_ref, mix_ref, g_ref, w_ref, attn_ref, rec_ref, dattn_ref, drec_ref, dw_ref, dg_ref, acc):
        i = pl.program_id(0)

        @pl.when(i == 0)
        def _():
            acc[...] = jnp.zeros_like(acc)
            dg_ref[...] = jnp.zeros_like(dg_ref)

        g = g_ref[...]
        _, xhat, rstd = _rms_fwd(mix_ref[...], g)
        dmix, dg = _rms_bwd(dh_ref[...], xhat, rstd, g)
        dmix = dmix.astype(BF16)
        dg_ref[...] += dg
        din = _mm(dmix, w_ref[...])
        dattn_ref[...] = din[:, :ATTN_WIDTH].astype(BF16)
        drec_ref[...] = din[:, ATTN_WIDTH:]
        acc[0:ATTN_WIDTH, :] += _mm_tn(attn_ref[...], dmix)
        acc[ATTN_WIDTH:, :] += _mm_tn(rec_ref[...], dmix)

        @pl.when(i == steps - 1)
        def _():
            dw_ref[...] = acc[...].astype(BF16)

    half = pl.BlockSpec((tm, ATTN_WIDTH), lambda i: (i, 0))
    wide = pl.BlockSpec((tm, D_MODEL), lambda i: (i, 0))
    return _hosting_call(
        body, "out_proj_bwd", steps,
        [wide, wide, _full((1, D_MODEL)), _resident((D_MODEL, D_MODEL)), half, half],
        [half, half, _full((D_MODEL, D_MODEL)), _full((1, D_MODEL))],
        [jax.ShapeDtypeStruct((rows, ATTN_WIDTH), BF16), jax.ShapeDtypeStruct((rows, LRU_WIDTH), F32),
         jax.ShapeDtypeStruct((D_MODEL, D_MODEL), BF16), jax.ShapeDtypeStruct((1, D_MODEL), F32)],
        [pltpu.VMEM((D_MODEL, D_MODEL), F32)],
        (dh1, mix, g2, w_out_t, attn, rec), carried, modes)


def _attn_bwd(qkv, dattn, sinks, bias, carried, modes):
    rows = qkv.shape[0]
    tm = _row_tile(rows)
    nbt, nt = tm // BLOCK, rows // tm

    def body(sink_ref, bias_ref, do_ref, q_ref, kp_ref, kc_ref, vp_ref, vc_ref, dq_ref, dkv_hbm, dsink_ref,
             dk_c, dv_c, stage, wsem):
        i = pl.program_id(0)
        slot = i % 2

        def first_write(s):
            return pltpu.make_async_copy(stage.at[s, pl.ds(BLOCK, tm - BLOCK)], dkv_hbm.at[pl.ds(0, tm - BLOCK)], wsem.at[s])

        def tile_write(t, s):
            return pltpu.make_async_copy(stage.at[s], dkv_hbm.at[pl.ds(pl.multiple_of(t * tm - BLOCK, BLOCK), tm)], wsem.at[s])

        def last_write(s):
            return pltpu.make_async_copy(stage.at[s, pl.ds(0, BLOCK)], dkv_hbm.at[pl.ds(rows - BLOCK, BLOCK)], wsem.at[s])

        def wait_write(t, s):
            @pl.when(t == 0)
            def _():
                first_write(s).wait()

            @pl.when(t > 0)
            def _():
                tile_write(t, s).wait()

        @pl.when(i == 0)
        def _():
            dk_c[...] = jnp.zeros_like(dk_c)
            dv_c[...] = jnp.zeros_like(dv_c)
            dsink_ref[...] = jnp.zeros_like(dsink_ref)

        @pl.when(i >= 2)
        def _():
            wait_write(i - 2, slot)

        @pl.when(i < nt)
        def _():
            dk_late, dv_late = dk_c[...], dv_c[...]
            dsink_rows = [jnp.zeros((1, LANES), F32)] * ATTN_HEADS
            for b in range(nbt):
                blk = slice(b * BLOCK, (b + 1) * BLOCK)
                bias_t = _bias_of_block(bias_ref, i * nbt + b)
                dq_parts, dk_parts, dv_parts = [], [], []
                for kv in range(KV_HEADS):
                    k2 = _keys_of_block(kp_ref, kc_ref, b, kv)
                    v2 = _keys_of_block(vp_ref, vc_ref, b, kv)
                    q4 = _heads(q_ref, blk, kv * GQA_GROUP, GQA_GROUP)
                    do4 = _heads(do_ref, blk, kv * GQA_GROUP, GQA_GROUP)
                    pn, psink = _attn_probs(k2, q4, bias_t, _sink_row(sink_ref, kv))
                    dpn = _mm_nt(v2, do4)
                    delta = jnp.sum(pn * dpn, axis=0, keepdims=True)
                    ds = ((pn * (dpn - delta)) * (HEAD_DIM ** -0.5)).astype(BF16)
                    dqt = _mm_tn(k2, ds)
                    dq_parts += [dqt[:, g * BLOCK:(g + 1) * BLOCK] for g in range(GQA_GROUP)]
                    dk_parts.append(_mm(ds, q4))
                    dv_parts.append(_mm(pn.astype(BF16), do4))
                    sd = psink * delta
                    for g in range(GQA_GROUP):
                        h = kv * GQA_GROUP + g
                        dsink_rows[h] = dsink_rows[h] - jnp.sum(sd[:, g * BLOCK:(g + 1) * BLOCK])
                dq_ref[blk, :] = _from_head_major(dq_parts).astype(BF16)
                dk2 = jnp.concatenate(dk_parts, axis=1)
                dv2 = jnp.concatenate(dv_parts, axis=1)
                stage[slot, blk, 0:KV_WIDTH] = (dk_late + dk2[0:BLOCK]).astype(BF16)
                stage[slot, blk, KV_WIDTH:] = (dv_late + dv2[0:BLOCK]).astype(BF16)
                dk_late, dv_late = dk2[BLOCK:], dv2[BLOCK:]
            dk_c[...] = dk_late
            dv_c[...] = dv_late
            dsink_ref[...] += jnp.concatenate(dsink_rows, axis=0)

            @pl.when(i == 0)
            def _():
                first_write(slot).start()

            @pl.when(i > 0)
            def _():
                tile_write(i, slot).start()

        @pl.when(i == nt)
        def _():
            stage[slot, 0:BLOCK, 0:KV_WIDTH] = dk_c[...].astype(BF16)
            stage[slot, 0:BLOCK, KV_WIDTH:] = dv_c[...].astype(BF16)
            last_write(slot).start()
            wait_write(i - 1, 1 - slot)
            last_write(slot).wait()

    tile_of = lambda i: jnp.minimum(i, nt - 1)
    tile = pl.BlockSpec((tm, ATTN_WIDTH), lambda i: (tile_of(i), 0))
    return _hosting_call(
        body, "attn_bwd", nt + 1,
        [pl.BlockSpec(memory_space=pltpu.SMEM), _resident((N_BIAS, 2 * BLOCK, GQA_GROUP * BLOCK)), tile]
        + _attn_specs(tm, tile_of),
        [tile, ANY_SPACE, _full((ATTN_HEADS, LANES))],
        [jax.ShapeDtypeStruct((rows, ATTN_WIDTH), BF16), jax.ShapeDtypeStruct((rows, 2 * KV_WIDTH), BF16),
         jax.ShapeDtypeStruct((ATTN_HEADS, LANES), F32)],
        [pltpu.VMEM((BLOCK, KV_WIDTH), F32), pltpu.VMEM((BLOCK, KV_WIDTH), F32),
         pltpu.VMEM((2, tm, 2 * KV_WIDTH), BF16), pltpu.SemaphoreType.DMA((2,))],
        (sinks, bias, dattn, qkv, qkv, qkv, qkv, qkv), carried, modes)


ROW_CONV_B, ROW_B_A, ROW_B_X, ROW_LAMBDA = 4, 5, 6, 7


def _rec_bwd(drec, zrec, h, kept, conv_w, wa_bd, wx_bd, lam, carried, modes):
    rows = zrec.shape[0]
    tm = _rec_tile(rows)
    nt = rows // tm
    per = tm // SUBLANES

    def body(drec_ref, xr_ref, yr_ref, h_ref, xc_ref, a_ref, mult_ref, r_ref, ig_ref, hhalo_ref, cw_ref, wa_ref, wx_ref,
             lam_ref, drz_ref, small_ref, dwa_ref, dwx_ref, hbuf, dbuf, dgr_s, dgi_s, dyr_s, carry):
        s = pl.program_id(0)
        i = nt - 1 - s

        @pl.when(s == 0)
        def _():
            small_ref[...] = jnp.zeros_like(small_ref)
            dwa_ref[...] = jnp.zeros_like(dwa_ref)
            dwx_ref[...] = jnp.zeros_like(dwx_ref)
            carry[...] = jnp.zeros_like(carry)
            dbuf[tm:tm + SUBLANES, :] = jnp.zeros((SUBLANES, LRU_WIDTH), F32)

        hbuf[0:SUBLANES, :] = jnp.where(i == 0, 0.0, hhalo_ref[...])
        hbuf[SUBLANES:SUBLANES + tm, :] = h_ref[...]

        row = lax.broadcasted_iota(jnp.int32, (SUBLANES, LRU_WIDTH), 0)
        log_a_scale = (-LRU_C) * _softplus(-lam_ref[...])
        zeros = jnp.zeros((SUBLANES, LRU_WIDTH), F32)

        def group(k, state):
            g_later, a_later, sum_dgr, sum_dgi, sum_lam = state
            o = pl.multiple_of((per - 1 - k) * SUBLANES, SUBLANES)
            rows8 = pl.ds(o, SUBLANES)
            yr, drec_t, h_t, a = yr_ref[rows8, :], drec_ref[rows8, :], h_ref[rows8, :], a_ref[rows8, :]
            gel, t = _gelu(yr)
            dyr_s[rows8, :] = drec_t * h_t * _gelu_grad(yr, t)
            u = drec_t * gel
            coef = jnp.where(row == SUBLANES - 1, a_later, pltpu.roll(a, SUBLANES - 1, 0))
            for sft in (1, 2, 4):
                keep = row < SUBLANES - sft
                u = jnp.where(keep, coef * pltpu.roll(u, SUBLANES - sft, 0) + u, u)
                coef = jnp.where(keep, coef * pltpu.roll(coef, SUBLANES - sft, 0), coef)
            g = coef * g_later + u
            du = jnp.where(i * tm + o + row >= PAD_ROWS, g, 0.0)
            h_before = jnp.where(row == 0, hbuf[rows8, :][SUBLANES - 1:SUBLANES, :], pltpu.roll(h_t, 1, 0))
            xc, mult, r, ig = xc_ref[rows8, :], mult_ref[rows8, :], r_ref[rows8, :], ig_ref[rows8, :]
            dbuf[rows8, :] = du * (mult * ig)
            dgi = (du * (mult * xc)) * (ig * (1.0 - ig))
            dgi_s[rows8, :] = dgi
            dlog_a = (g * h_before) * a - (du * (ig * xc)) * (a * a * pl.reciprocal(mult, approx=True))
            dgr = (dlog_a * log_a_scale) * (r * (1.0 - r))
            dgr_s[rows8, :] = dgr
            return g[0:1, :], a[0:1, :], sum_dgr + dgr, sum_dgi + dgi, sum_lam + dlog_a * r

        state = lax.fori_loop(0, per, group, (carry[0:1, :], carry[1:2, :], zeros, zeros, zeros))
        carry[0:1, :], carry[1:2, :] = state[0], state[1]
        sum_dgr, sum_dgi, sum_lam = (jnp.sum(v, axis=0, keepdims=True) for v in state[2:])
        dlam = sum_lam * (LRU_C * _sigmoid(-lam_ref[...]))

        dgr_b = [dgr_s[:, hh * LRU_HALF:(hh + 1) * LRU_HALF].astype(BF16) for hh in range(2)]
        dgi_b = [dgi_s[:, hh * LRU_HALF:(hh + 1) * LRU_HALF].astype(BF16) for hh in range(2)]
        halves = _lru_halves(xc_ref[...])
        for hh in range(2):
            dwa_ref[hh] += _mm_tn(halves[hh], dgr_b[hh])
            dwx_ref[hh] += _mm_tn(halves[hh], dgi_b[hh])
        dxc = dbuf[0:tm, :] + jnp.concatenate(
            [_mm_nt(dgr_b[hh], wa_ref[hh]) + _mm_nt(dgi_b[hh], wx_ref[hh]) for hh in range(2)], axis=1)

        dbuf[0:tm, :] = dxc
        sum_dxc = jnp.sum(dxc, axis=0, keepdims=True)
        ahead = [dbuf[pl.ds(CONV_WIDTH - 1 - j, tm), :] for j in range(CONV_WIDTH)]
        drz_ref[:, 0:LRU_WIDTH] = sum(cw_ref[j:j + 1, :] * ahead[j] for j in range(CONV_WIDTH)).astype(BF16)
        drz_ref[:, LRU_WIDTH:] = dyr_s[...].astype(BF16)
        upd = [jnp.sum(xr_ref[...] * ahead[j], axis=0, keepdims=True) for j in range(CONV_WIDTH)]
        dbuf[tm:tm + SUBLANES, :] = dbuf[0:SUBLANES, :]
        small_ref[...] += jnp.concatenate(upd + [sum_dxc, sum_dgr, sum_dgi, dlam], axis=0)

    rev = lambda s: nt - 1 - s
    halo = lambda s: jnp.maximum(rev(s) * per - 1, 0)
    cols = lambda k: pl.BlockSpec((tm, LRU_WIDTH), lambda s: (rev(s), k))
    halo0 = pl.BlockSpec((SUBLANES, LRU_WIDTH), lambda s: (halo(s), 0))
    bd = _full((2, LRU_HALF, LRU_HALF))
    big = pltpu.VMEM((tm + SUBLANES, LRU_WIDTH), F32)
    tile = pltpu.VMEM((tm, LRU_WIDTH), F32)
    kept_cols = [cols(k) for k in (KEPT_XC, KEPT_A, KEPT_MULT, KEPT_R, KEPT_I)]
    return _hosting_call(
        body, "rec_bwd", nt,
        [cols(0), cols(0), cols(1), cols(0)] + kept_cols
        + [halo0, _full((CONV_WIDTH, LRU_WIDTH)), bd, bd, _full((1, LRU_WIDTH))],
        [pl.BlockSpec((tm, 2 * LRU_WIDTH), lambda s: (rev(s), 0)), _full((SUBLANES, LRU_WIDTH)), bd, bd],
        [jax.ShapeDtypeStruct((rows, 2 * LRU_WIDTH), BF16), jax.ShapeDtypeStruct((SUBLANES, LRU_WIDTH), F32),
         jax.ShapeDtypeStruct((2, LRU_HALF, LRU_HALF), F32), jax.ShapeDtypeStruct((2, LRU_HALF, LRU_HALF), F32)],
        [big, big, tile, tile, tile, pltpu.VMEM((SUBLANES, LRU_WIDTH), F32)],
        (drec, zrec, zrec, h) + (kept,) * N_KEPT + (h, conv_w, wa_bd, wx_bd, lam), carried, modes)


DZ_CUTS = (0, ATTN_WIDTH, QKV_WIDTH, IN_WIDTH)


def _dz_specs(tm):
    return [pl.BlockSpec((tm, DZ_CUTS[p + 1] - DZ_CUTS[p]), lambda i: (i, 0)) for p in range(3)]


def _in_proj_bwd_x(head, x, g1, dh1, dq, dkv, drz, w_in_t, carried, modes):
    rows = dh1.shape[0]
    tm = _row_tile(rows)
    steps = rows // tm

    def body(head_ref, g_ref, dh1_ref, dq_ref, dkv_ref, drz_ref, w_ref, x_hbm, dhead_ref, dx_hbm, dg_ref,
             buf, sem, stage, wsem):
        i = pl.program_id(0)
        slot = i % 2
        h0 = _h0_tile(head_ref, x_hbm, buf, sem, i, steps, tm)

        def first_write(s):
            return pltpu.make_async_copy(stage.at[s, pl.ds(BLOCK, tm - BLOCK)], dx_hbm.at[pl.ds(0, tm - BLOCK)], wsem.at[s])

        def tile_write(t, s):
            return pltpu.make_async_copy(stage.at[s], dx_hbm.at[pl.ds(pl.multiple_of(t * tm - BLOCK, BLOCK), tm)], wsem.at[s])

        def wait_write(t, s):
            @pl.when(t == 0)
            def _():
                first_write(s).wait()

            @pl.when(t > 0)
            def _():
                tile_write(t, s).wait()

        @pl.when(i == 0)
        def _():
            dg_ref[...] = jnp.zeros_like(dg_ref)

        @pl.when(i >= 2)
        def _():
            wait_write(i - 2, slot)

        g = g_ref[...]
        _, xhat, rstd = _rms_fwd(h0, g)
        parts = (dq_ref[...], dkv_ref[...], drz_ref[...])
        du = sum(_mm(parts[p], w_ref[DZ_CUTS[p]:DZ_CUTS[p + 1], :]) for p in range(3))
        dx, dg = _rms_bwd(du, xhat, rstd, g)
        stage[slot] = dh1_ref[...] + dx
        dg_ref[...] += dg

        @pl.when(i == 0)
        def _():
            dhead_ref[...] = stage[slot, 0:BLOCK]
            first_write(slot).start()

        @pl.when(i > 0)
        def _():
            tile_write(i, slot).start()

        @pl.when(i == steps - 1)
        def _():
            if steps > 1:
                wait_write(i - 1, 1 - slot)
            wait_write(i, slot)

    wide = pl.BlockSpec((tm, D_MODEL), lambda i: (i, 0))
    return _hosting_call(
        body, "in_proj_bwd_x", steps,
        [_full((BLOCK, D_MODEL)), _full((1, D_MODEL)), wide] + _dz_specs(tm) + [_resident((IN_WIDTH, D_MODEL)), ANY_SPACE],
        [_full((BLOCK, D_MODEL)), ANY_SPACE, _full((1, D_MODEL))],
        [jax.ShapeDtypeStruct((BLOCK, D_MODEL), F32), jax.ShapeDtypeStruct((rows - BLOCK, D_MODEL), F32),
         jax.ShapeDtypeStruct((1, D_MODEL), F32)],
        _frame_scratch(tm) + [pltpu.VMEM((2, tm, D_MODEL), F32), pltpu.SemaphoreType.DMA((2,))],
        (head, g1, dh1, dq, dkv, drz, w_in_t, x), carried, modes)


def _in_proj_bwd_w(u1, dq, dkv, drz, carried, modes):
    rows = u1.shape[0]
    tb = _big_tile(rows)
    steps = rows // tb

    def body(u_ref, dq_ref, dkv_ref, drz_ref, dw_ref, acc):
        i = pl.program_id(0)

        @pl.when(i == 0)
        def _():
            acc[...] = jnp.zeros_like(acc)

        u = u_ref[...]
        for p, ref in enumerate((dq_ref, dkv_ref, drz_ref)):
            acc[:, DZ_CUTS[p]:DZ_CUTS[p + 1]] += _mm_tn(u, ref[...])

        @pl.when(i == steps - 1)
        def _():
            dw_ref[...] = acc[...].astype(BF16)

    return _hosting_call(
        body, "in_proj_bwd_w", steps,
        [pl.BlockSpec((tb, D_MODEL), lambda i: (i, 0))] + _dz_specs(tb),
        [_full((D_MODEL, IN_WIDTH))],
        [jax.ShapeDtypeStruct((D_MODEL, IN_WIDTH), BF16)],
        [pltpu.VMEM((D_MODEL, IN_WIDTH), F32)], (u1, dq, dkv, drz), carried, modes)


def _adamw_math(w, m, v, g):
    nm = ADAM_B1 * m + (1.0 - ADAM_B1) * g
    nv = ADAM_B2 * v + (1.0 - ADAM_B2) * (g * g)
    m_hat = nm / (1.0 - ADAM_B1 ** ADAM_STEP)
    v_hat = nv / (1.0 - ADAM_B2 ** ADAM_STEP)
    return (-ADAM_LR) * (m_hat / (jnp.sqrt(v_hat) + ADAM_EPS) + ADAM_WD * w), nm, nv


SMALL_NAMES = ("conv_b", "b_a", "b_x", "lru_lambda", "attn_sinks", "g_post_mix", "g_pre_ffn", "g_post_ffn")
PACK_WIDTH = 1024


def _pack_rows(vals):
    assert len(SMALL_NAMES) == SUBLANES
    row = lax.broadcasted_iota(jnp.int32, (SUBLANES, PACK_WIDTH), 0)
    tile = jnp.zeros((SUBLANES, PACK_WIDTH), F32)
    for k, name in enumerate(SMALL_NAMES):
        a = vals[name].reshape(1, -1)
        tile = jnp.where(row == k, jnp.pad(a, ((0, 0), (0, PACK_WIDTH - a.shape[1]))), tile)
    return tile


def _adamw_small(weights, mom_m, mom_v, parts, loss_parts, others):
    names = list(SMALL_NAMES) + [name for name, _, _ in others]
    views = [(1, weights[name].size) for name in SMALL_NAMES] + [view for _, view, _ in others]
    n, n_pack = len(names), len(SMALL_NAMES)

    def body(*refs):
        w_refs, m_refs, v_refs = refs[:n], refs[n:2 * n], refs[2 * n:3 * n]
        p_ref, l_ref = refs[3 * n], refs[3 * n + 1]
        o_refs = refs[3 * n + 2:3 * n + 2 + len(others)]
        loss_ref, outs = refs[3 * n + 2 + len(others)], refs[3 * n + 3 + len(others):]
        for k, (_, c) in enumerate(views):
            if k < n_pack:
                g = p_ref[0, k:k + 1, 0:c]
                for s in range(1, N_DEV):
                    g = g + p_ref[s, k:k + 1, 0:c]
            else:
                g = o_refs[k - n_pack][0]
                for s in range(1, N_DEV):
                    g = g + o_refs[k - n_pack][s]
            g_ref, d_ref, nm_ref, nv_ref = outs[4 * k:4 * k + 4]
            g_ref[...] = g
            d_ref[...], nm_ref[...], nv_ref[...] = _adamw_math(w_refs[k][...], m_refs[k][...], v_refs[k][...], g)
        total = l_ref[0]
        for s in range(1, N_DEV):
            total = total + l_ref[s]
        loss_ref[...] = total

    args = [src[name].reshape(view) for src in (weights, mom_m, mom_v) for name, view in zip(names, views)]
    res = pl.pallas_call(
        body, name="adamw_small",
        out_shape=[jax.ShapeDtypeStruct(loss_parts.shape[1:], F32)]
                  + [jax.ShapeDtypeStruct(view, F32) for view in views for _ in range(4)],
        compiler_params=pltpu.CompilerParams(vmem_limit_bytes=VMEM_LIMIT),
    )(*args, parts, loss_parts, *[p for _, _, p in others])
    out = {name: tuple(t.reshape(weights[name].shape) for t in res[1 + 4 * k:5 + 4 * k]) for k, name in enumerate(names)}
    return res[0], out


def _adamw(w, m, v, parts, name):
    rows, cols = w.shape
    tr = next((t for t in (256, 128) if rows % t == 0), rows)
    parts = parts if isinstance(parts, (list, tuple)) else [parts]

    def body(w_ref, m_ref, v_ref, *refs):
        p_refs, (g_ref, d_ref, nm_ref, nv_ref) = refs[:len(parts)], refs[len(parts):]

        def total(p_ref):
            g = p_ref[0].astype(F32)
            for s in range(1, N_DEV):
                g = g + p_ref[s].astype(F32)
            return g

        g = jnp.concatenate([total(p_ref) for p_ref in p_refs], axis=1) if len(parts) > 1 else total(p_refs[0])
        g_ref[...] = g
        d_ref[...], nm_ref[...], nv_ref[...] = _adamw_math(w_ref[...], m_ref[...], v_ref[...], g)

    blk = pl.BlockSpec((tr, cols), lambda i: (i, 0))
    return pl.pallas_call(
        body, name=name, grid=(rows // tr,),
        in_specs=[blk, blk, blk] + [pl.BlockSpec((N_DEV, tr, p.shape[2]), lambda i: (0, i, 0)) for p in parts],
        out_specs=[blk] * 4,
        out_shape=[jax.ShapeDtypeStruct((rows, cols), F32)] * 4,
        compiler_params=_params(("parallel",)),
    )(w, m, v, *parts)


def _cols_from_shards(g):
    return jnp.transpose(g, (1, 0, 2)).reshape(g.shape[1], N_DEV * g.shape[2])


def _cols_to_shards(a):
    r, c = a.shape
    return jnp.transpose(a.reshape(r, N_DEV, c // N_DEV), (1, 0, 2))


def _block_diag(w):
    per = LRU_HALF // LRU_BLOCK
    w = w.reshape(2, per, LRU_BLOCK, LRU_BLOCK)
    eye = jnp.eye(per, dtype=w.dtype)
    return (w[:, :, :, None, :] * eye[None, :, None, :, None]).reshape(2, LRU_HALF, LRU_HALF)


def _block_diag_extract(t):
    per = LRU_HALF // LRU_BLOCK
    t = t.reshape(2, per, LRU_BLOCK, per, LRU_BLOCK)
    return jnp.stack([t[:, b, :, b, :] for b in range(per)], axis=1).reshape(LRU_BLOCKS, LRU_BLOCK, LRU_BLOCK)


def kernel(x, meta_tokens, g_pre_mix, w_in, conv_w, conv_b, w_a, b_a, w_x, b_x, lru_lambda, attn_sinks, w_out, g_post_mix, g_pre_ffn, w_ff1, w_ff2, g_post_ffn, loss_target, m_meta_tokens, m_g_pre_mix, m_w_in, m_conv_w, m_conv_b, m_w_a, m_b_a, m_w_x, m_b_x, m_lru_lambda, m_attn_sinks, m_w_out, m_g_post_mix, m_g_pre_ffn, m_w_ff1, m_w_ff2, m_g_post_ffn, v_meta_tokens, v_g_pre_mix, v_w_in, v_conv_w, v_conv_b, v_w_a, v_b_a, v_w_x, v_b_x, v_lru_lambda, v_attn_sinks, v_w_out, v_g_post_mix, v_g_pre_ffn, v_w_ff1, v_w_ff2, v_g_post_ffn):
    weights = dict(meta_tokens=meta_tokens, g_pre_mix=g_pre_mix, w_in=w_in, conv_w=conv_w, conv_b=conv_b, w_a=w_a,
                   b_a=b_a, w_x=w_x, b_x=b_x, lru_lambda=lru_lambda, attn_sinks=attn_sinks, w_out=w_out,
                   g_post_mix=g_post_mix, g_pre_ffn=g_pre_ffn, w_ff1=w_ff1, w_ff2=w_ff2, g_post_ffn=g_post_ffn)
    mom_m = dict(meta_tokens=m_meta_tokens, g_pre_mix=m_g_pre_mix, w_in=m_w_in, conv_w=m_conv_w, conv_b=m_conv_b,
                 w_a=m_w_a, b_a=m_b_a, w_x=m_w_x, b_x=m_b_x, lru_lambda=m_lru_lambda, attn_sinks=m_attn_sinks,
                 w_out=m_w_out, g_post_mix=m_g_post_mix, g_pre_ffn=m_g_pre_ffn, w_ff1=m_w_ff1, w_ff2=m_w_ff2,
                 g_post_ffn=m_g_post_ffn)
    mom_v = dict(meta_tokens=v_meta_tokens, g_pre_mix=v_g_pre_mix, w_in=v_w_in, conv_w=v_conv_w, conv_b=v_conv_b,
                 w_a=v_w_a, b_a=v_b_a, w_x=v_w_x, b_x=v_b_x, lru_lambda=v_lru_lambda, attn_sinks=v_attn_sinks,
                 w_out=v_w_out, g_post_mix=v_g_post_mix, g_pre_ffn=v_g_pre_ffn, w_ff1=v_w_ff1, w_ff2=v_w_ff2,
                 g_post_ffn=v_g_post_ffn)
    order = list(weights)

    (g_win, g_meta, g_cw) = _gather_two_level([w_in[0].astype(BF16), meta_tokens, conv_w[0]], "gather_first")
    w_in_full = _cols_from_shards(g_win)
    meta_full = _cols_from_shards(g_meta)
    conv_w_full = _cols_from_shards(g_cw)

    head = jnp.concatenate([jnp.zeros((PAD_ROWS, D_MODEL), F32), meta_full], axis=0)
    wa_bd = _block_diag(w_a[0]).astype(BF16)
    wx_bd = _block_diag(w_x[0]).astype(BF16)
    bias = _attn_bias()

    w1_shard = w_ff1[0].astype(BF16)
    (qkv, zrec, u1, attn), (g_wout, w1a) = _in_proj_attn_fwd(
        head, x[0], g_pre_mix, w_in_full, attn_sinks, bias,
        [w_out[0].astype(BF16), w1_shard[:, :FF_HALF]], ["gather"] * 2)
    (rec, h_lru, kept), (w1b,) = _rec_fwd(zrec, conv_w_full, conv_b, wa_bd, b_a, wx_bd, b_x, lru_lambda,
                                         [w1_shard[:, FF_HALF:]], ["gather"])
    w_out_full = g_wout.reshape(D_MODEL, D_MODEL)
    w2_shard = w_ff2[0].astype(BF16)
    (mix, h1, act, u2), (w2a, w2b) = _mix_and_ffn_up(
        attn, rec, w_out_full, head, x[0], g_post_mix, g_pre_ffn, (w1a, w1b),
        [w2_shard[:FF_HALF], w2_shard[FF_HALF:]], ["gather"] * 2)
    w2_halves = [w.reshape(D_FF // 2, D_MODEL) for w in (w2a, w2b)]
    (dy, df, dg_post_ffn, loss_acc), w2t_halves = _ffn_down_loss(
        act, w2_halves, h1, loss_target[0], g_post_ffn, [w2_shard[:FF_HALF].T, w2_shard[FF_HALF:].T], ["gather"] * 2)

    (da1,), (w1ta,) = _ffn_bwd_act(df, w2t_halves, act, [w1_shard[:, :FF_HALF].T], ["gather"])
    (dw1h, dw2g), (w1tb,) = _ffn_bwd_weights(u2, da1, act, df, [w1_shard[:, FF_HALF:].T], ["gather"])
    w1t_halves = [w.reshape(D_FF // 2, D_MODEL) for w in (w1ta, w1tb)]
    (dh1, dg_pre_ffn), (p_w1a,) = _ffn_bwd_x(da1, w1t_halves, h1, dy, g_pre_ffn, [dw1h[0]], ["scatter"])
    (dattn, drec, dw_out, dg_post_mix), (p_w1b,) = _out_proj_bwd(dh1, mix, g_post_mix, w_out_full.T, attn, rec,
                                                                [dw1h[1]], ["scatter"])
    (dq, dkv, dsinks), (p_w2,) = _attn_bwd(qkv, dattn, attn_sinks, bias, [dw2g], ["scatter"])
    (drz, rec_small, dwa_bd, dwx_bd), (p_wout,) = _rec_bwd(
        drec, zrec, h_lru, kept, conv_w_full, wa_bd, wx_bd, lru_lambda,
        [dw_out.reshape(N_DEV, D_MODEL // N_DEV, D_MODEL)], ["scatter"])
    small_grads = dict(
        conv_b=rec_small[ROW_CONV_B], b_a=rec_small[ROW_B_A], b_x=rec_small[ROW_B_X], lru_lambda=rec_small[ROW_LAMBDA],
        attn_sinks=dsinks[:, 0], g_post_mix=dg_post_mix, g_pre_ffn=dg_pre_ffn, g_post_ffn=dg_post_ffn)
    gate_rows = (LRU_BLOCKS * LRU_BLOCK, LRU_BLOCK)
    gate_dense = (LRU_BLOCKS * LRU_BLOCK * LRU_BLOCK // PACK_WIDTH, PACK_WIDTH)
    (dw_in,), (p_cw, p_small, p_wa, p_wx) = _in_proj_bwd_w(
        u1, dq, dkv, drz,
        [_cols_to_shards(rec_small[0:CONV_WIDTH]), _pack_rows(small_grads),
         _block_diag_extract(dwa_bd).reshape(gate_dense), _block_diag_extract(dwx_bd).reshape(gate_dense)],
        ["scatter", "gather", "gather", "gather"])
    p_wa, p_wx = (p.reshape((N_DEV,) + gate_rows) for p in (p_wa, p_wx))
    (dhead, grad_rows, dg_pre_mix), (p_win,) = _in_proj_bwd_x(
        head, x[0], g_pre_mix, dh1, dq, dkv, drz, w_in_full.T, [_cols_to_shards(dw_in)], ["scatter"])
    p_meta, p_gpm, p_loss = _exchange([_cols_to_shards(dhead[PAD_ROWS:BLOCK]), dg_pre_mix, loss_acc],
                                      ["scatter", "gather", "gather"], "exchange_last")

    res = {}
    res["w_in"] = _adamw(w_in[0], m_w_in[0], v_w_in[0], p_win, "adamw_w_in")
    res["w_out"] = _adamw(w_out[0], m_w_out[0], v_w_out[0], p_wout, "adamw_w_out")
    res["w_ff1"] = _adamw(w_ff1[0], m_w_ff1[0], v_w_ff1[0], [p_w1a, p_w1b], "adamw_w_ff1")
    res["w_ff2"] = _adamw(w_ff2[0], m_w_ff2[0], v_w_ff2[0], p_w2, "adamw_w_ff2")
    for name in ("w_in", "w_out", "w_ff1", "w_ff2"):
        res[name] = tuple(t[None] for t in res[name])
    others = [("g_pre_mix", g_pre_mix.shape, p_gpm), ("meta_tokens", meta_tokens.shape, p_meta),
              ("conv_w", conv_w.shape[1:], p_cw), ("w_a", gate_rows, p_wa), ("w_x", gate_rows, p_wx)]
    loss_total, small = _adamw_small(weights, mom_m, mom_v, p_small, p_loss, others)
    res.update(small)

    grad_x = grad_rows[None]
    outs = [loss_total[0, 0], grad_x]
    for k in range(4):
        outs += [res[name][k] for name in order]
    return tuple(outs)
```

```python
import jax
import jax.numpy as jnp
import numpy as np
from jax import lax
from jax.experimental import pallas as pl
from jax.experimental.pallas import tpu as pltpu

F32 = jnp.float32
BF16 = jnp.bfloat16

D_MODEL = 1024
N_META = 16
HEAD_DIM = 64
ATTN_HEADS = 8
KV_HEADS = 2
GQA_GROUP = ATTN_HEADS // KV_HEADS
ATTN_WIDTH = ATTN_HEADS * HEAD_DIM
KV_WIDTH = KV_HEADS * HEAD_DIM
QKV_WIDTH = ATTN_WIDTH + 2 * KV_WIDTH
LRU_WIDTH = 512
LRU_BLOCKS = 8
LRU_BLOCK = 64
LRU_HALF = 256
LRU_C = 8.0
CONV_WIDTH = 4
BLOCK = 128
PAD_ROWS = BLOCK - N_META
IN_WIDTH = QKV_WIDTH + 2 * LRU_WIDTH
D_FF = 4096
EPS = 1e-6
NEG = -1e30
N_DEV = 8
FF_CHUNK = D_FF // N_DEV
SUBLANES = 8
LANES = 128

ADAM_LR = 0.001
ADAM_B1 = 0.9
ADAM_B2 = 0.999
ADAM_EPS = 1e-08
ADAM_WD = 0.01
ADAM_STEP = 10

VMEM_LIMIT = 56 * 1024 * 1024


def _row_tile(rows):
    for t in (640, 512, 256, 128):
        if rows % t == 0:
            return t
    raise ValueError(rows)


def _big_tile(rows):
    for t in (1664, 1024, 512, 256, 128):
        if rows % t == 0:
            return t
    raise ValueError(rows)


def _rec_tile(rows):
    for t in (640, 256, 128):
        if rows % t == 0:
            return t
    raise ValueError(rows)


def _params(semantics):
    return pltpu.CompilerParams(dimension_semantics=semantics, vmem_limit_bytes=VMEM_LIMIT)


def _mm(a, b):
    return lax.dot_general(a, b, (((1,), (0,)), ((), ())), preferred_element_type=F32)


def _mm_nt(a, b):
    return lax.dot_general(a, b, (((1,), (1,)), ((), ())), preferred_element_type=F32)


def _mm_tn(a, b):
    return lax.dot_general(a, b, (((0,), (0,)), ((), ())), preferred_element_type=F32)


def _rms_fwd(x, g):
    rstd = lax.rsqrt(jnp.mean(x * x, axis=-1, keepdims=True) + EPS)
    xhat = x * rstd
    return xhat * g, xhat, rstd


def _rms_bwd(dy, xhat, rstd, g):
    dyg = dy * g
    c = jnp.mean(dyg * xhat, axis=-1, keepdims=True)
    dx = rstd * (dyg - xhat * c)
    dg = jnp.sum(dy * xhat, axis=0, keepdims=True)
    return dx, dg


def _sigmoid(x):
    return pl.reciprocal(1.0 + jnp.exp(-x), approx=True)


def _log1p(x):
    u = 1.0 + x
    return jnp.where(u == 1.0, x, jnp.log(u) * x / (u - 1.0))


def _one_minus_sq_exp(x, ex):
    return -jnp.tanh(x) * (1.0 + ex * ex)


TINY = 1e-30


def _sqrt_pos(y):
    r = lax.rsqrt(jnp.maximum(y, TINY))
    return y * r, r


def _softplus(x):
    return jnp.maximum(x, 0.0) + _log1p(jnp.exp(-jnp.abs(x)))


GELU_C = 0.7978845608028654
GELU_K = 0.044715


def _gelu(x):
    t = jnp.tanh(GELU_C * (x + GELU_K * x * x * x))
    return 0.5 * x * (1.0 + t), t


def _gelu_grad(x, t):
    return 0.5 * (1.0 + t) + 0.5 * x * (1.0 - t * t) * GELU_C * (1.0 + 3.0 * GELU_K * x * x)


def _full(shape):
    return pl.BlockSpec(shape, lambda *_: (0,) * len(shape))


def _resident(shape):
    return pl.BlockSpec(shape, lambda *_: (0,) * len(shape), pipeline_mode=pl.Buffered(1))


def _exchange_copies(ins, outs, sems, modes):
    send_sems, recv_sems, local_sems = sems
    x, y, c = lax.axis_index("x"), lax.axis_index("y"), lax.axis_index("c")
    me = 4 * x + 2 * y + c

    def block(a, dev):
        return ins[a] if modes[a] == "gather" else ins[a].at[dev]

    local = [pltpu.make_async_copy(block(a, me), outs[a].at[me], local_sems.at[a]) for a in range(len(ins))]
    sends, recvs = [], []
    for a in range(len(ins)):
        for k in range(N_DEV - 1):
            bits = k + 1
            px = jnp.bitwise_xor(x, (bits >> 2) & 1)
            py = jnp.bitwise_xor(y, (bits >> 1) & 1)
            pc = jnp.bitwise_xor(c, bits & 1)
            peer = 4 * px + 2 * py + pc
            common = dict(src_ref=block(a, peer), send_sem=send_sems.at[a, k], recv_sem=recv_sems.at[a, k],
                          device_id=(px, py, pc), device_id_type=pl.DeviceIdType.MESH)
            sends.append(pltpu.make_async_remote_copy(dst_ref=outs[a].at[me], **common))
            recvs.append(pltpu.make_async_remote_copy(dst_ref=outs[a].at[peer], **common))
    return local, sends, recvs


def _exchange_start(ins, outs, sems, modes):
    local, sends, _ = _exchange_copies(ins, outs, sems, modes)
    for cp in local + sends:
        cp.start()


def _exchange_wait(ins, outs, sems, modes):
    local, sends, recvs = _exchange_copies(ins, outs, sems, modes)
    for cp in recvs:
        cp.wait_recv()
    for cp in sends:
        cp.wait_send()
    for cp in local:
        cp.wait()


def _exchange_shapes(arrays, modes):
    return [jax.ShapeDtypeStruct((N_DEV,) + a.shape if mode == "gather" else a.shape, a.dtype)
            for a, mode in zip(arrays, modes)]


def _exchange_sems(na):
    return [pltpu.SemaphoreType.DMA((na, N_DEV - 1)), pltpu.SemaphoreType.DMA((na, N_DEV - 1)),
            pltpu.SemaphoreType.DMA((na,))]


ANY_SPACE = pl.BlockSpec(memory_space=pl.ANY)


def _gather_two_level(arrays, name):
    na = len(arrays)

    def body(*refs):
        ins, outs = refs[:na], refs[na:2 * na]
        send_sems, recv_sems, local_sems = refs[2 * na:]
        x, y, c = lax.axis_index("x"), lax.axis_index("y"), lax.axis_index("c")
        me, sibling = (x, y, c), (x, y, 1 - c)
        chips = [(1 - x, y), (x, 1 - y), (1 - x, 1 - y)]

        def copy(a, k, block, to, src=None):
            slot = outs[a].at[4 * block[0] + 2 * block[1] + block[2]]
            return pltpu.make_async_remote_copy(
                src_ref=slot if src is None else src, dst_ref=slot, send_sem=send_sems.at[a, k],
                recv_sem=recv_sems.at[a, k], device_id=to, device_id_type=pl.DeviceIdType.MESH)

        local = [pltpu.make_async_copy(ins[a], outs[a].at[4 * x + 2 * y + c], local_sems.at[a]) for a in range(na)]
        first = []
        for a in range(na):
            first.append(copy(a, 0, me, sibling, src=ins[a]))
            first += [copy(a, 1 + j, me, (*chip, c), src=ins[a]) for j, chip in enumerate(chips)]
        for cp in local + first:
            cp.start()
        passed = []
        for j, chip in enumerate(chips):
            for a in range(na):
                copy(a, 1 + j, (*chip, c), me).wait_recv()
                passed.append(copy(a, 4 + j, (*chip, c), sibling))
                passed[-1].start()
        for a in range(na):
            copy(a, 0, sibling, me).wait_recv()
            for j, chip in enumerate(chips):
                copy(a, 4 + j, (*chip, 1 - c), me).wait_recv()
        for cp in first + passed:
            cp.wait_send()
        for cp in local:
            cp.wait()

    return pl.pallas_call(
        body, name=name, out_shape=_exchange_shapes(arrays, ["gather"] * na),
        in_specs=[ANY_SPACE] * na, out_specs=[ANY_SPACE] * na, scratch_shapes=_exchange_sems(na),
        compiler_params=pltpu.CompilerParams(has_side_effects=True),
    )(*arrays)


def _hosting_call(body, name, steps, in_specs, out_specs, out_shape, scratch_shapes, args, arrays, modes):
    n_in, n_out, n_scr, na = len(in_specs), len(out_specs), len(scratch_shapes), len(arrays)
    grid = steps if isinstance(steps, tuple) else (steps,)

    def hosting_body(*refs):
        cuts = [0]
        for n in (n_in, na, n_out, na, n_scr, 3):
            cuts.append(cuts[-1] + n)
        ins, x_ins, outs, x_outs, scr, sems = (refs[cuts[p]:cuts[p + 1]] for p in range(6))
        first, last = True, True
        for axis, n in enumerate(grid):
            first = first & (pl.program_id(axis) == 0)
            last = last & (pl.program_id(axis) == n - 1)

        @pl.when(first)
        def _():
            _exchange_start(x_ins, x_outs, sems, modes)

        body(*ins, *outs, *scr)

        @pl.when(last)
        def _():
            _exchange_wait(x_ins, x_outs, sems, modes)

    res = pl.pallas_call(
        hosting_body, name=name, grid=grid,
        in_specs=list(in_specs) + [ANY_SPACE] * na, out_specs=list(out_specs) + [ANY_SPACE] * na,
        out_shape=list(out_shape) + _exchange_shapes(arrays, modes),
        scratch_shapes=list(scratch_shapes) + _exchange_sems(na),
        compiler_params=_params(("arbitrary",) * len(grid)),
    )(*args, *arrays)
    return res[:n_out], res[n_out:]


def _frame_rows(src_hbm, buf, sem, i, steps, tm):
    def first():
        return pltpu.make_async_copy(src_hbm.at[pl.ds(0, tm - BLOCK)], buf.at[0, pl.ds(BLOCK, tm - BLOCK)], sem.at[0])

    def later(t, slot):
        return pltpu.make_async_copy(src_hbm.at[pl.ds(pl.multiple_of(t * tm - BLOCK, SUBLANES), tm)], buf.at[slot], sem.at[slot])

    slot = i % 2

    @pl.when(i == 0)
    def _():
        first().start()

    @pl.when(i + 1 < steps)
    def _():
        later(i + 1, 1 - slot).start()

    @pl.when(i == 0)
    def _():
        first().wait()

    @pl.when(i > 0)
    def _():
        later(i, slot).wait()

    return slot


def _frame_scratch(tm):
    return [pltpu.VMEM((2, tm, D_MODEL), F32), pltpu.SemaphoreType.DMA((2,))]


def _h0_tile(head_ref, x_hbm, buf, sem, i, steps, tm):
    slot = _frame_rows(x_hbm, buf, sem, i, steps, tm)

    @pl.when(i == 0)
    def _():
        buf[0, 0:BLOCK, :] = head_ref[...]

    return buf[slot]


N_BIAS = 3


def _attn_bias():
    key = np.arange(2 * BLOCK)[:, None]
    r = np.arange(GQA_GROUP * BLOCK)[None, :] % BLOCK
    band = (key > r) & (key <= r + BLOCK)
    out = [np.where(band & ((n - 1) * BLOCK + key >= PAD_ROWS), 0.0, NEG) for n in range(N_BIAS)]
    return jnp.asarray(np.stack(out), F32)


def _attn_probs(k2, q4, bias, sink_row):
    s = _mm_nt(k2, q4) * (HEAD_DIM ** -0.5) + bias
    m = jnp.maximum(jnp.max(s, axis=0, keepdims=True), sink_row)
    p = jnp.exp(s - m)
    es = jnp.exp(sink_row - m)
    inv = 1.0 / (jnp.sum(p, axis=0, keepdims=True) + es)
    return p * inv, es * inv


def _heads(ref, rows, first, count):
    return jnp.concatenate([ref[rows, (first + g) * HEAD_DIM:(first + g + 1) * HEAD_DIM] for g in range(count)], axis=0)


def _keys_of_block(prev_ref, cur_ref, b, kv):
    sl = slice(kv * HEAD_DIM, (kv + 1) * HEAD_DIM)
    before = prev_ref[:, sl] if b == 0 else cur_ref[(b - 1) * BLOCK:b * BLOCK, sl]
    return jnp.concatenate([before, cur_ref[b * BLOCK:(b + 1) * BLOCK, sl]], axis=0)


def _bias_of_block(bias_ref, block):
    return bias_ref[jnp.minimum(block, N_BIAS - 1)]


def _sink_row(sink_ref, kv):
    g = lax.broadcasted_iota(jnp.int32, (1, GQA_GROUP * BLOCK), 1) // BLOCK
    row = jnp.full((1, GQA_GROUP * BLOCK), sink_ref[0, kv * GQA_GROUP], F32)
    for i in range(1, GQA_GROUP):
        row = jnp.where(g == i, sink_ref[0, kv * GQA_GROUP + i], row)
    return row


def _from_head_major(pieces):
    return jnp.concatenate(pieces, axis=0).T


def _attn_specs(tm, tile_of):
    nbt = tm // BLOCK
    k_col, v_col = ATTN_WIDTH // KV_WIDTH, ATTN_WIDTH // KV_WIDTH + 1
    before = lambda i: jnp.maximum(tile_of(i) * nbt - 1, 0)
    return [pl.BlockSpec((tm, ATTN_WIDTH), lambda i: (tile_of(i), 0)),
            pl.BlockSpec((BLOCK, KV_WIDTH), lambda i: (before(i), k_col)),
            pl.BlockSpec((tm, KV_WIDTH), lambda i: (tile_of(i), k_col)),
            pl.BlockSpec((BLOCK, KV_WIDTH), lambda i: (before(i), v_col)),
            pl.BlockSpec((tm, KV_WIDTH), lambda i: (tile_of(i), v_col))]


def _in_proj_attn_fwd(head, x, g1, w_in, sinks, bias, carried, modes):
    rows = BLOCK + x.shape[0]
    tm = _row_tile(rows)
    steps, nbt = rows // tm, tm // BLOCK

    def body(head_ref, g_ref, w_ref, sink_ref, bias_ref, x_hbm, qkv_ref, zrec_ref, u_ref, o_ref, buf, sem, kv_before):
        i = pl.program_id(0)
        h = _h0_tile(head_ref, x_hbm, buf, sem, i, steps, tm)
        u, _, _ = _rms_fwd(h, g_ref[...])
        u = u.astype(BF16)
        u_ref[...] = u
        z = _mm(u, w_ref[...])
        qkv_ref[...] = z[:, :QKV_WIDTH].astype(BF16)
        zrec_ref[...] = z[:, QKV_WIDTH:]

        @pl.when(i == 0)
        def _():
            kv_before[...] = jnp.zeros_like(kv_before)

        kc_ref, vc_ref = (qkv_ref.at[:, pl.ds(ATTN_WIDTH + c * KV_WIDTH, KV_WIDTH)] for c in range(2))
        kp_ref, vp_ref = (kv_before.at[:, pl.ds(c * KV_WIDTH, KV_WIDTH)] for c in range(2))
        for b in range(nbt):
            blk = slice(b * BLOCK, (b + 1) * BLOCK)
            bias_t = _bias_of_block(bias_ref, i * nbt + b)
            pieces = []
            for kv in range(KV_HEADS):
                k2 = _keys_of_block(kp_ref, kc_ref, b, kv)
                v2 = _keys_of_block(vp_ref, vc_ref, b, kv)
                q4 = _heads(qkv_ref, blk, kv * GQA_GROUP, GQA_GROUP)
                pn, _ = _attn_probs(k2, q4, bias_t, _sink_row(sink_ref, kv))
                ot = _mm_tn(v2, pn.astype(BF16))
                pieces += [ot[:, g * BLOCK:(g + 1) * BLOCK] for g in range(GQA_GROUP)]
            o_ref[blk, :] = _from_head_major(pieces).astype(BF16)
        kv_before[...] = qkv_ref[tm - BLOCK:tm, ATTN_WIDTH:]

    wide = pl.BlockSpec((tm, D_MODEL), lambda i: (i, 0))
    return _hosting_call(
        body, "in_proj_attn_fwd", steps,
        [_full((BLOCK, D_MODEL)), _full((1, D_MODEL)), _resident((D_MODEL, IN_WIDTH)), pl.BlockSpec(memory_space=pltpu.SMEM),
         _resident((N_BIAS, 2 * BLOCK, GQA_GROUP * BLOCK)), ANY_SPACE],
        [pl.BlockSpec((tm, QKV_WIDTH), lambda i: (i, 0)), pl.BlockSpec((tm, 2 * LRU_WIDTH), lambda i: (i, 0)), wide,
         pl.BlockSpec((tm, ATTN_WIDTH), lambda i: (i, 0))],
        [jax.ShapeDtypeStruct((rows, QKV_WIDTH), BF16), jax.ShapeDtypeStruct((rows, 2 * LRU_WIDTH), F32),
         jax.ShapeDtypeStruct((rows, D_MODEL), BF16), jax.ShapeDtypeStruct((rows, ATTN_WIDTH), BF16)],
        _frame_scratch(tm) + [pltpu.VMEM((BLOCK, 2 * KV_WIDTH), BF16)],
        (head, g1, w_in, sinks, bias, x), carried, modes)


def _conv_taps(xbuf, tm):
    return [xbuf[pl.ds(SUBLANES - (CONV_WIDTH - 1 - j), tm), :] for j in range(CONV_WIDTH)]


def _lru_halves(xc):
    return [xc[:, h * LRU_HALF:(h + 1) * LRU_HALF].astype(BF16) for h in range(2)]


def _lru_gates(xc, wa_ref, ba_ref, wx_ref, bx_ref, lam_ref):
    halves = _lru_halves(xc)
    gate_r = jnp.concatenate([_mm(halves[h], wa_ref[h]) for h in range(2)], axis=1) + ba_ref[...]
    gate_i = jnp.concatenate([_mm(halves[h], wx_ref[h]) for h in range(2)], axis=1) + bx_ref[...]
    r = _sigmoid(gate_r)
    ig = _sigmoid(gate_i)
    log_a = (-LRU_C) * r * _softplus(-lam_ref[...])
    a = jnp.exp(log_a)
    mult, _ = _sqrt_pos(_one_minus_sq_exp(log_a, a))
    return r, ig, a, mult


KEPT_XC, KEPT_A, KEPT_MULT, KEPT_R, KEPT_I, N_KEPT = 0, 1, 2, 3, 4, 5


def _scan_tile(a_ref, u_ref, out_ref, carry, tm):
    row = lax.broadcasted_iota(jnp.int32, (SUBLANES, LRU_WIDTH), 0)

    def step(j, before):
        o = pl.multiple_of(j * SUBLANES, SUBLANES)
        a = a_ref[pl.ds(o, SUBLANES), :]
        u = u_ref[pl.ds(o, SUBLANES), :]
        for s in (1, 2, 4):
            keep = row >= s
            u = jnp.where(keep, a * pltpu.roll(u, s, 0) + u, u)
            a = jnp.where(keep, a * pltpu.roll(a, s, 0), a)
        out = a * before + u
        out_ref[pl.ds(o, SUBLANES), :] = out
        return out[SUBLANES - 1:SUBLANES, :]

    return lax.fori_loop(0, tm // SUBLANES, step, carry)


def _rec_fwd(zrec, conv_w, conv_b, wa_bd, b_a, wx_bd, b_x, lam, carried, modes):
    rows = zrec.shape[0]
    tm = _row_tile(rows)

    def body(xr_ref, yr_ref, cw_ref, cb_ref, wa_ref, ba_ref, wx_ref, bx_ref, lam_ref, rec_ref, h_ref, kept_ref,
             xbuf, a_s, u_s, carry):
        i = pl.program_id(0)

        @pl.when(i == 0)
        def _():
            xbuf[0:SUBLANES, :] = jnp.zeros((SUBLANES, LRU_WIDTH), F32)
            carry[...] = jnp.zeros_like(carry)

        @pl.when(i > 0)
        def _():
            xbuf[0:SUBLANES, :] = xbuf[tm:tm + SUBLANES, :]

        xbuf[SUBLANES:SUBLANES + tm, :] = xr_ref[...]
        taps = _conv_taps(xbuf, tm)
        xc = cb_ref[...] + sum(cw_ref[j:j + 1, :] * taps[j] for j in range(CONV_WIDTH))
        r, ig, a, mult = _lru_gates(xc, wa_ref, ba_ref, wx_ref, bx_ref, lam_ref)
        for k, val in ((KEPT_XC, xc), (KEPT_A, a), (KEPT_MULT, mult), (KEPT_R, r), (KEPT_I, ig)):
            kept_ref[:, k * LRU_WIDTH:(k + 1) * LRU_WIDTH] = val
        grow = i * tm + lax.broadcasted_iota(jnp.int32, (tm, LRU_WIDTH), 0)
        a_s[...] = a
        u_s[...] = jnp.where(grow >= PAD_ROWS, mult * (ig * xc), 0.0)
        carry[0:1, :] = _scan_tile(a_s, u_s, h_ref, carry[0:1, :], tm)
        gel, _ = _gelu(yr_ref[...])
        rec_ref[...] = (gel * h_ref[...]).astype(BF16)

    vec = _full((1, LRU_WIDTH))
    bd = _full((2, LRU_HALF, LRU_HALF))
    return _hosting_call(
        body, "rec_fwd", rows // tm,
        [pl.BlockSpec((tm, LRU_WIDTH), lambda i: (i, 0)), pl.BlockSpec((tm, LRU_WIDTH), lambda i: (i, 1)),
         _full((CONV_WIDTH, LRU_WIDTH)), vec, bd, vec, bd, vec, vec],
        [pl.BlockSpec((tm, LRU_WIDTH), lambda i: (i, 0))] * 2 + [pl.BlockSpec((tm, N_KEPT * LRU_WIDTH), lambda i: (i, 0))],
        [jax.ShapeDtypeStruct((rows, LRU_WIDTH), BF16), jax.ShapeDtypeStruct((rows, LRU_WIDTH), F32),
         jax.ShapeDtypeStruct((rows, N_KEPT * LRU_WIDTH), F32)],
        [pltpu.VMEM((tm + SUBLANES, LRU_WIDTH), F32), pltpu.VMEM((tm, LRU_WIDTH), F32),
         pltpu.VMEM((tm, LRU_WIDTH), F32), pltpu.VMEM((SUBLANES, LRU_WIDTH), F32)],
        (zrec, zrec, conv_w, conv_b, wa_bd, b_a, wx_bd, b_x, lam), carried, modes)


FF_COLS = 1024
FF_HALF = FF_CHUNK // 2


def _hidden_at(d, half):
    return half * (D_FF // 2) + d * FF_HALF


def _mix_and_ffn_up(attn, rec, w_out, head, x, g2, g3, w1_halves, carried, modes):
    rows = attn.shape[0]
    tm = _row_tile(rows)
    steps = rows // tm

    def body(attn_ref, rec_ref, w_ref, head_ref, g2_ref, g3_ref, wa_ref, wb_ref, x_hbm,
             mix_ref, h1_ref, act_ref, u_ref, buf, sem):
        h0 = _h0_tile(head_ref, x_hbm, buf, sem, pl.program_id(0), steps, tm)
        mix = _mm(attn_ref[...], w_ref[0:ATTN_WIDTH, :]) + _mm(rec_ref[...], w_ref[ATTN_WIDTH:, :])
        y, _, _ = _rms_fwd(mix, g2_ref[...])
        mix_ref[...] = mix
        h1_ref[...] = h0 + y
        u, _, _ = _rms_fwd(h1_ref[...], g3_ref[...])
        u = u.astype(BF16)
        u_ref[...] = u
        for half, w1_ref in enumerate((wa_ref, wb_ref)):
            for d in range(N_DEV):
                c = _hidden_at(d, half)
                a1 = jnp.maximum(_mm(u, w1_ref[d]), 0.0)
                act_ref[:, c:c + FF_HALF] = (a1 * a1).astype(BF16)

    half_in = pl.BlockSpec((tm, ATTN_WIDTH), lambda i: (i, 0))
    wide = pl.BlockSpec((tm, D_MODEL), lambda i: (i, 0))
    return _hosting_call(
        body, "mix_and_ffn_up", steps,
        [half_in, half_in, _resident((D_MODEL, D_MODEL)), _full((BLOCK, D_MODEL)), _full((1, D_MODEL)), _full((1, D_MODEL))]
        + [_resident((N_DEV, D_MODEL, FF_HALF))] * 2 + [ANY_SPACE],
        [wide, wide, pl.BlockSpec((tm, D_FF), lambda i: (i, 0)), wide],
        [jax.ShapeDtypeStruct((rows, D_MODEL), F32)] * 2
        + [jax.ShapeDtypeStruct((rows, D_FF), BF16), jax.ShapeDtypeStruct((rows, D_MODEL), BF16)],
        _frame_scratch(tm), (attn, rec, w_out, head, g2, g3, *w1_halves, x), carried, modes)


def _ffn_down_loss(act, w2_halves, h1, target, g4, carried, modes):
    rows = h1.shape[0]
    tm = _row_tile(rows)
    steps = rows // tm
    kh = D_FF // 2

    def body(act_ref, wa_ref, wb_ref, h_ref, g_ref, t_hbm, dy_ref, df_ref, dg_ref, loss_ref, buf, sem):
        i = pl.program_id(0)
        slot = _frame_rows(t_hbm, buf, sem, i, steps, tm)

        @pl.when(i == 0)
        def _():
            dg_ref[...] = jnp.zeros_like(dg_ref)
            loss_ref[...] = jnp.zeros_like(loss_ref)
            buf[0, 0:BLOCK, :] = jnp.zeros((BLOCK, D_MODEL), F32)

        g = g_ref[...]
        f = _mm(act_ref[:, :kh], wa_ref[...]) + _mm(act_ref[:, kh:], wb_ref[...])
        y, fhat, rstd = _rms_fwd(f, g)
        grow = i * tm + lax.broadcasted_iota(jnp.int32, (tm, D_MODEL), 0)
        err = jnp.where(grow >= BLOCK, h_ref[...] + y - buf[slot], 0.0)
        loss_ref[...] += (0.5 / D_MODEL) * jnp.sum(err * err)
        dy = err * (1.0 / D_MODEL)
        df, dg = _rms_bwd(dy, fhat, rstd, g)
        dy_ref[...] = dy
        df_ref[...] = df.astype(BF16)
        dg_ref[...] += dg

    wide = pl.BlockSpec((tm, D_MODEL), lambda i: (i, 0))
    return _hosting_call(
        body, "ffn_down_loss", steps,
        [pl.BlockSpec((tm, D_FF), lambda i: (i, 0)), _resident((kh, D_MODEL)), _resident((kh, D_MODEL)), wide,
         _full((1, D_MODEL)), ANY_SPACE],
        [wide, wide, _full((1, D_MODEL)), _full((SUBLANES, LANES))],
        [jax.ShapeDtypeStruct((rows, D_MODEL), F32), jax.ShapeDtypeStruct((rows, D_MODEL), BF16),
         jax.ShapeDtypeStruct((1, D_MODEL), F32), jax.ShapeDtypeStruct((SUBLANES, LANES), F32)],
        _frame_scratch(tm), (act, *w2_halves, h1, g4, target), carried, modes)


def _ffn_bwd_act(df, w2t_halves, act, carried, modes):
    rows = df.shape[0]
    tm = _row_tile(rows)

    def body(df_ref, wa_ref, wb_ref, act_ref, da_ref):
        df_t = df_ref[...]
        for half, w_ref in enumerate((wa_ref, wb_ref)):
            for d in range(N_DEV):
                cols = slice(_hidden_at(d, half), _hidden_at(d, half) + FF_HALF)
                dact = _mm(df_t, w_ref[d])
                relu_a1, _ = _sqrt_pos(act_ref[:, cols].astype(F32))
                da_ref[:, cols] = (dact * (2.0 * relu_a1)).astype(BF16)

    hidden = pl.BlockSpec((tm, D_FF), lambda i: (i, 0))
    return _hosting_call(
        body, "ffn_bwd_act", rows // tm,
        [pl.BlockSpec((tm, D_MODEL), lambda i: (i, 0))] + [_resident((N_DEV, D_MODEL, FF_HALF))] * 2 + [hidden],
        [hidden],
        [jax.ShapeDtypeStruct((rows, D_FF), BF16)],
        [], (df, *w2t_halves, act), carried, modes)


def _ffn_bwd_x(da, w1t_halves, h1, dy, g3, carried, modes):
    rows = h1.shape[0]
    tm = _row_tile(rows)
    kh = D_FF // 2

    def body(da_ref, wa_ref, wb_ref, h_ref, dy_ref, g_ref, dh_ref, dg_ref):
        @pl.when(pl.program_id(0) == 0)
        def _():
            dg_ref[...] = jnp.zeros_like(dg_ref)

        g = g_ref[...]
        _, xhat, rstd = _rms_fwd(h_ref[...], g)
        du = _mm(da_ref[:, :kh], wa_ref[...]) + _mm(da_ref[:, kh:], wb_ref[...])
        dx, dg = _rms_bwd(du, xhat, rstd, g)
        dh_ref[...] = dy_ref[...] + dx
        dg_ref[...] += dg

    wide = pl.BlockSpec((tm, D_MODEL), lambda i: (i, 0))
    return _hosting_call(
        body, "ffn_bwd_x", rows // tm,
        [pl.BlockSpec((tm, D_FF), lambda i: (i, 0)), _resident((kh, D_MODEL)), _resident((kh, D_MODEL)), wide, wide,
         _full((1, D_MODEL))],
        [wide, _full((1, D_MODEL))],
        [jax.ShapeDtypeStruct((rows, D_MODEL), F32), jax.ShapeDtypeStruct((1, D_MODEL), F32)],
        [], (da, *w1t_halves, h1, dy, g3), carried, modes)


def _ffn_bwd_weights(u2, da, act, df, carried, modes):
    rows = u2.shape[0]
    tb = _big_tile(rows)
    steps = rows // tb
    per = FF_COLS // FF_HALF

    def body(u_ref, da_ref, act_ref, df_ref, dw1_ref, dw2_ref, acc1, acc2):
        i = pl.program_id(1)

        @pl.when(i == 0)
        def _():
            acc1[...] = jnp.zeros_like(acc1)
            acc2[...] = jnp.zeros_like(acc2)

        acc1[...] += _mm_tn(u_ref[...], da_ref[...])
        acc2[...] += _mm_tn(act_ref[...], df_ref[...])

        @pl.when(i == steps - 1)
        def _():
            for p in range(per):
                c = p * FF_HALF
                dw1_ref[p] = acc1[:, c:c + FF_HALF].astype(BF16)
                dw2_ref[p] = acc2[c:c + FF_HALF, :].astype(BF16)

    wide = pl.BlockSpec((tb, D_MODEL), lambda j, i: (i, 0))
    chunk = pl.BlockSpec((tb, FF_COLS), lambda j, i: (i, j))
    return _hosting_call(
        body, "ffn_bwd_weights", (D_FF // FF_COLS, steps),
        [wide, chunk, chunk, wide],
        [pl.BlockSpec((None, per, D_MODEL, FF_HALF), lambda j, i: (j // 2, j % 2, 0, 0)),
         pl.BlockSpec((per, FF_HALF, D_MODEL), lambda j, i: (j % 2, j // 2, 0))],
        [jax.ShapeDtypeStruct((2, N_DEV, D_MODEL, FF_HALF), BF16), jax.ShapeDtypeStruct((N_DEV, FF_CHUNK, D_MODEL), BF16)],
        [pltpu.VMEM((D_MODEL, FF_COLS), F32), pltpu.VMEM((FF_COLS, D_MODEL), F32)],
        (u2, da, act, df), carried, modes)


def _out_proj_bwd(dh1, mix, g2, w_out_t, attn, rec, carried, modes):
    rows = dh1.shape[0]
    tm = _row_tile(rows)
    steps = rows // tm

    def body(dh_ref, mix_ref, g_ref, w_ref, attn_ref, rec_ref, dattn_ref, drec_ref, dw_ref, dg_ref, acc):
        i = pl.program_id(0)

        @pl.when(i == 0)
        def _():
            acc[...] = jnp.zeros_like(acc)
            dg_ref[...] = jnp.zeros_like(dg_ref)

        g = g_ref[...]
        _, xhat, rstd = _rms_fwd(mix_ref[...], g)
        dmix, dg = _rms_bwd(dh_ref[...], xhat, rstd, g)
        dmix = dmix.astype(BF16)
        dg_ref[...] += dg
        din = _mm(dmix, w_ref[...])
        dattn_ref[...] = din[:, :ATTN_WIDTH].astype(BF16)
        drec_ref[...] = din[:, ATTN_WIDTH:]
        acc[0:ATTN_WIDTH, :] += _mm_tn(attn_ref[...], dmix)
        acc[ATTN_WIDTH:, :] += _mm_tn(rec_ref[...], dmix)

        @pl.when(i == steps - 1)
        def _():
            dw_ref[...] = acc[...].astype(BF16)

    half = pl.BlockSpec((tm, ATTN_WIDTH), lambda i: (i, 0))
    wide = pl.BlockSpec((tm, D_MODEL), lambda i: (i, 0))
    return _hosting_call(
        body, "out_proj_bwd", steps,
        [wide, wide, _full((1, D_MODEL)), _resident((D_MODEL, D_MODEL)), half, half],
        [half, half, _full((D_MODEL, D_MODEL)), _full((1, D_MODEL))],
        [jax.ShapeDtypeStruct((rows, ATTN_WIDTH), BF16), jax.ShapeDtypeStruct((rows, LRU_WIDTH), F32),
         jax.ShapeDtypeStruct((D_MODEL, D_MODEL), BF16), jax.ShapeDtypeStruct((1, D_MODEL), F32)],
        [pltpu.VMEM((D_MODEL, D_MODEL), F32)],
        (dh1, mix, g2, w_out_t, attn, rec), carried, modes)


def _attn_bwd(qkv, dattn, sinks, bias, carried, modes):
    rows = qkv.shape[0]
    tm = _row_tile(rows)
    nbt, nt = tm // BLOCK, rows // tm

    def body(sink_ref, bias_ref, do_ref, q_ref, kp_ref, kc_ref, vp_ref, vc_ref, dq_ref, dkv_hbm, dsink_ref,
             dk_c, dv_c, stage, wsem):
        i = pl.program_id(0)
        slot = i % 2

        def first_write(s):
            return pltpu.make_async_copy(stage.at[s, pl.ds(BLOCK, tm - BLOCK)], dkv_hbm.at[pl.ds(0, tm - BLOCK)], wsem.at[s])

        def tile_write(t, s):
            return pltpu.make_async_copy(stage.at[s], dkv_hbm.at[pl.ds(pl.multiple_of(t * tm - BLOCK, BLOCK), tm)], wsem.at[s])

        def last_write(s):
            return pltpu.make_async_copy(stage.at[s, pl.ds(0, BLOCK)], dkv_hbm.at[pl.ds(rows - BLOCK, BLOCK)], wsem.at[s])

        def wait_write(t, s):
            @pl.when(t == 0)
            def _():
                first_write(s).wait()

            @pl.when(t > 0)
            def _():
                tile_write(t, s).wait()

        @pl.when(i == 0)
        def _():
            dk_c[...] = jnp.zeros_like(dk_c)
            dv_c[...] = jnp.zeros_like(dv_c)
            dsink_ref[...] = jnp.zeros_like(dsink_ref)

        @pl.when(i >= 2)
        def _():
            wait_write(i - 2, slot)

        @pl.when(i < nt)
        def _():
            dk_late, dv_late = dk_c[...], dv_c[...]
            dsink_rows = [jnp.zeros((1, LANES), F32)] * ATTN_HEADS
            for b in range(nbt):
                blk = slice(b * BLOCK, (b + 1) * BLOCK)
                bias_t = _bias_of_block(bias_ref, i * nbt + b)
                dq_parts, dk_parts, dv_parts = [], [], []
                for kv in range(KV_HEADS):
                    k2 = _keys_of_block(kp_ref, kc_ref, b, kv)
                    v2 = _keys_of_block(vp_ref, vc_ref, b, kv)
                    q4 = _heads(q_ref, blk, kv * GQA_GROUP, GQA_GROUP)
                    do4 = _heads(do_ref, blk, kv * GQA_GROUP, GQA_GROUP)
                    pn, psink = _attn_probs(k2, q4, bias_t, _sink_row(sink_ref, kv))
                    dpn = _mm_nt(v2, do4)
                    delta = jnp.sum(pn * dpn, axis=0, keepdims=True)
                    ds = ((pn * (dpn - delta)) * (HEAD_DIM ** -0.5)).astype(BF16)
                    dqt = _mm_tn(k2, ds)
                    dq_parts += [dqt[:, g * BLOCK:(g + 1) * BLOCK] for g in range(GQA_GROUP)]
                    dk_parts.append(_mm(ds, q4))
                    dv_parts.append(_mm(pn.astype(BF16), do4))
                    sd = psink * delta
                    for g in range(GQA_GROUP):
                        h = kv * GQA_GROUP + g
                        dsink_rows[h] = dsink_rows[h] - jnp.sum(sd[:, g * BLOCK:(g + 1) * BLOCK])
                dq_ref[blk, :] = _from_head_major(dq_parts).astype(BF16)
                dk2 = jnp.concatenate(dk_parts, axis=1)
                dv2 = jnp.concatenate(dv_parts, axis=1)
                stage[slot, blk, 0:KV_WIDTH] = (dk_late + dk2[0:BLOCK]).astype(BF16)
                stage[slot, blk, KV_WIDTH:] = (dv_late + dv2[0:BLOCK]).astype(BF16)
                dk_late, dv_late = dk2[BLOCK:], dv2[BLOCK:]
            dk_c[...] = dk_late
            dv_c[...] = dv_late
            dsink_ref[...] += jnp.concatenate(dsink_rows, axis=0)

            @pl.when(i == 0)
            def _():
                first_write(slot).start()

            @pl.when(i > 0)
            def _():
                tile_write(i, slot).start()

        @pl.when(i == nt)
        def _():
            stage[slot, 0:BLOCK, 0:KV_WIDTH] = dk_c[...].astype(BF16)
            stage[slot, 0:BLOCK, KV_WIDTH:] = dv_c[...].astype(BF16)
            last_write(slot).start()
            wait_write(i - 1, 1 - slot)
            last_write(slot).wait()

    tile_of = lambda i: jnp.minimum(i, nt - 1)
    tile = pl.BlockSpec((tm, ATTN_WIDTH), lambda i: (tile_of(i), 0))
    return _hosting_call(
        body, "attn_bwd", nt + 1,
        [pl.BlockSpec(memory_space=pltpu.SMEM), _resident((N_BIAS, 2 * BLOCK, GQA_GROUP * BLOCK)), tile]
        + _attn_specs(tm, tile_of),
        [tile, ANY_SPACE, _full((ATTN_HEADS, LANES))],
        [jax.ShapeDtypeStruct((rows, ATTN_WIDTH), BF16), jax.ShapeDtypeStruct((rows, 2 * KV_WIDTH), BF16),
         jax.ShapeDtypeStruct((ATTN_HEADS, LANES), F32)],
        [pltpu.VMEM((BLOCK, KV_WIDTH), F32), pltpu.VMEM((BLOCK, KV_WIDTH), F32),
         pltpu.VMEM((2, tm, 2 * KV_WIDTH), BF16), pltpu.SemaphoreType.DMA((2,))],
        (sinks, bias, dattn, qkv, qkv, qkv, qkv, qkv), carried, modes)


ROW_CONV_B, ROW_B_A, ROW_B_X, ROW_LAMBDA = 4, 5, 6, 7


def _rec_bwd(drec, zrec, h, kept, conv_w, wa_bd, wx_bd, lam, carried, modes):
    rows = zrec.shape[0]
    tm = _rec_tile(rows)
    nt = rows // tm
    per = tm // SUBLANES

    def body(drec_ref, xr_ref, yr_ref, h_ref, xc_ref, a_ref, mult_ref, r_ref, ig_ref, hhalo_ref, cw_ref, wa_ref, wx_ref,
             lam_ref, drz_ref, small_ref, dwa_ref, dwx_ref, hbuf, dbuf, dgr_s, dgi_s, dyr_s, carry):
        s = pl.program_id(0)
        i = nt - 1 - s

        @pl.when(s == 0)
        def _():
            small_ref[...] = jnp.zeros_like(small_ref)
            dwa_ref[...] = jnp.zeros_like(dwa_ref)
            dwx_ref[...] = jnp.zeros_like(dwx_ref)
            carry[...] = jnp.zeros_like(carry)
            dbuf[tm:tm + SUBLANES, :] = jnp.zeros((SUBLANES, LRU_WIDTH), F32)

        hbuf[0:SUBLANES, :] = jnp.where(i == 0, 0.0, hhalo_ref[...])
        hbuf[SUBLANES:SUBLANES + tm, :] = h_ref[...]

        row = lax.broadcasted_iota(jnp.int32, (SUBLANES, LRU_WIDTH), 0)
        log_a_scale = (-LRU_C) * _softplus(-lam_ref[...])
        zeros = jnp.zeros((SUBLANES, LRU_WIDTH), F32)

        def group(k, state):
            g_later, a_later, sum_dgr, sum_dgi, sum_lam = state
            o = pl.multiple_of((per - 1 - k) * SUBLANES, SUBLANES)
            rows8 = pl.ds(o, SUBLANES)
            yr, drec_t, h_t, a = yr_ref[rows8, :], drec_ref[rows8, :], h_ref[rows8, :], a_ref[rows8, :]
            gel, t = _gelu(yr)
            dyr_s[rows8, :] = drec_t * h_t * _gelu_grad(yr, t)
            u = drec_t * gel
            coef = jnp.where(row == SUBLANES - 1, a_later, pltpu.roll(a, SUBLANES - 1, 0))
            for sft in (1, 2, 4):
                keep = row < SUBLANES - sft
                u = jnp.where(keep, coef * pltpu.roll(u, SUBLANES - sft, 0) + u, u)
                coef = jnp.where(keep, coef * pltpu.roll(coef, SUBLANES - sft, 0), coef)
            g = coef * g_later + u
            du = jnp.where(i * tm + o + row >= PAD_ROWS, g, 0.0)
            h_before = jnp.where(row == 0, hbuf[rows8, :][SUBLANES - 1:SUBLANES, :], pltpu.roll(h_t, 1, 0))
            xc, mult, r, ig = xc_ref[rows8, :], mult_ref[rows8, :], r_ref[rows8, :], ig_ref[rows8, :]
            dbuf[rows8, :] = du * (mult * ig)
            dgi = (du * (mult * xc)) * (ig * (1.0 - ig))
            dgi_s[rows8, :] = dgi
            dlog_a = (g * h_before) * a - (du * (ig * xc)) * (a * a * pl.reciprocal(mult, approx=True))
            dgr = (dlog_a * log_a_scale) * (r * (1.0 - r))
            dgr_s[rows8, :] = dgr
            return g[0:1, :], a[0:1, :], sum_dgr + dgr, sum_dgi + dgi, sum_lam + dlog_a * r

        state = lax.fori_loop(0, per, group, (carry[0:1, :], carry[1:2, :], zeros, zeros, zeros))
        carry[0:1, :], carry[1:2, :] = state[0], state[1]
        sum_dgr, sum_dgi, sum_lam = (jnp.sum(v, axis=0, keepdims=True) for v in state[2:])
        dlam = sum_lam * (LRU_C * _sigmoid(-lam_ref[...]))

        dgr_b = [dgr_s[:, hh * LRU_HALF:(hh + 1) * LRU_HALF].astype(BF16) for hh in range(2)]
        dgi_b = [dgi_s[:, hh * LRU_HALF:(hh + 1) * LRU_HALF].astype(BF16) for hh in range(2)]
        halves = _lru_halves(xc_ref[...])
        for hh in range(2):
            dwa_ref[hh] += _mm_tn(halves[hh], dgr_b[hh])
            dwx_ref[hh] += _mm_tn(halves[hh], dgi_b[hh])
        dxc = dbuf[0:tm, :] + jnp.concatenate(
            [_mm_nt(dgr_b[hh], wa_ref[hh]) + _mm_nt(dgi_b[hh], wx_ref[hh]) for hh in range(2)], axis=1)

        dbuf[0:tm, :] = dxc
        sum_dxc = jnp.sum(dxc, axis=0, keepdims=True)
        ahead = [dbuf[pl.ds(CONV_WIDTH - 1 - j, tm), :] for j in range(CONV_WIDTH)]
        drz_ref[:, 0:LRU_WIDTH] = sum(cw_ref[j:j + 1, :] * ahead[j] for j in range(CONV_WIDTH)).astype(BF16)
        drz_ref[:, LRU_WIDTH:] = dyr_s[...].astype(BF16)
        upd = [jnp.sum(xr_ref[...] * ahead[j], axis=0, keepdims=True) for j in range(CONV_WIDTH)]
        dbuf[tm:tm + SUBLANES, :] = dbuf[0:SUBLANES, :]
        small_ref[...] += jnp.concatenate(upd + [sum_dxc, sum_dgr, sum_dgi, dlam], axis=0)

    rev = lambda s: nt - 1 - s
    halo = lambda s: jnp.maximum(rev(s) * per - 1, 0)
    cols = lambda k: pl.BlockSpec((tm, LRU_WIDTH), lambda s: (rev(s), k))
    halo0 = pl.BlockSpec((SUBLANES, LRU_WIDTH), lambda s: (halo(s), 0))
    bd = _full((2, LRU_HALF, LRU_HALF))
    big = pltpu.VMEM((tm + SUBLANES, LRU_WIDTH), F32)
    tile = pltpu.VMEM((tm, LRU_WIDTH), F32)
    kept_cols = [cols(k) for k in (KEPT_XC, KEPT_A, KEPT_MULT, KEPT_R, KEPT_I)]
    return _hosting_call(
        body, "rec_bwd", nt,
        [cols(0), cols(0), cols(1), cols(0)] + kept_cols
        + [halo0, _full((CONV_WIDTH, LRU_WIDTH)), bd, bd, _full((1, LRU_WIDTH))],
        [pl.BlockSpec((tm, 2 * LRU_WIDTH), lambda s: (rev(s), 0)), _full((SUBLANES, LRU_WIDTH)), bd, bd],
        [jax.ShapeDtypeStruct((rows, 2 * LRU_WIDTH), BF16), jax.ShapeDtypeStruct((SUBLANES, LRU_WIDTH), F32),
         jax.ShapeDtypeStruct((2, LRU_HALF, LRU_HALF), F32), jax.ShapeDtypeStruct((2, LRU_HALF, LRU_HALF), F32)],
        [big, big, tile, tile, tile, pltpu.VMEM((SUBLANES, LRU_WIDTH), F32)],
        (drec, zrec, zrec, h) + (kept,) * N_KEPT + (h, conv_w, wa_bd, wx_bd, lam), carried, modes)


DZ_CUTS = (0, ATTN_WIDTH, QKV_WIDTH, IN_WIDTH)


def _dz_specs(tm):
    return [pl.BlockSpec((tm, DZ_CUTS[p + 1] - DZ_CUTS[p]), lambda i: (i, 0)) for p in range(3)]


def _in_proj_bwd_x(head, x, g1, dh1, dq, dkv, drz, w_in_t, carried, modes):
    rows = dh1.shape[0]
    tm = _row_tile(rows)
    steps = rows // tm

    def body(head_ref, g_ref, dh1_ref, dq_ref, dkv_ref, drz_ref, w_ref, x_hbm, dh0_ref, dg_ref, buf, sem):
        i = pl.program_id(0)
        h0 = _h0_tile(head_ref, x_hbm, buf, sem, i, steps, tm)

        @pl.when(i == 0)
        def _():
            dg_ref[...] = jnp.zeros_like(dg_ref)

        g = g_ref[...]
        _, xhat, rstd = _rms_fwd(h0, g)
        parts = (dq_ref[...], dkv_ref[...], drz_ref[...])
        du = sum(_mm(parts[p], w_ref[DZ_CUTS[p]:DZ_CUTS[p + 1], :]) for p in range(3))
        dx, dg = _rms_bwd(du, xhat, rstd, g)
        dh0_ref[...] = dh1_ref[...] + dx
        dg_ref[...] += dg

    wide = pl.BlockSpec((tm, D_MODEL), lambda i: (i, 0))
    return _hosting_call(
        body, "in_proj_bwd_x", steps,
        [_full((BLOCK, D_MODEL)), _full((1, D_MODEL)), wide] + _dz_specs(tm) + [_resident((IN_WIDTH, D_MODEL)), ANY_SPACE],
        [wide, _full((1, D_MODEL))],
        [jax.ShapeDtypeStruct((rows, D_MODEL), F32), jax.ShapeDtypeStruct((1, D_MODEL), F32)],
        _frame_scratch(tm), (head, g1, dh1, dq, dkv, drz, w_in_t, x), carried, modes)


def _in_proj_bwd_w(u1, dq, dkv, drz, carried, modes):
    rows = u1.shape[0]
    tb = _big_tile(rows)
    steps = rows // tb

    def body(u_ref, dq_ref, dkv_ref, drz_ref, dw_ref, acc):
        i = pl.program_id(0)

        @pl.when(i == 0)
        def _():
            acc[...] = jnp.zeros_like(acc)

        u = u_ref[...]
        for p, ref in enumerate((dq_ref, dkv_ref, drz_ref)):
            acc[:, DZ_CUTS[p]:DZ_CUTS[p + 1]] += _mm_tn(u, ref[...])

        @pl.when(i == steps - 1)
        def _():
            dw_ref[...] = acc[...].astype(BF16)

    return _hosting_call(
        body, "in_proj_bwd_w", steps,
        [pl.BlockSpec((tb, D_MODEL), lambda i: (i, 0))] + _dz_specs(tb),
        [_full((D_MODEL, IN_WIDTH))],
        [jax.ShapeDtypeStruct((D_MODEL, IN_WIDTH), BF16)],
        [pltpu.VMEM((D_MODEL, IN_WIDTH), F32)], (u1, dq, dkv, drz), carried, modes)


def _adamw_math(w, m, v, g):
    nm = ADAM_B1 * m + (1.0 - ADAM_B1) * g
    nv = ADAM_B2 * v + (1.0 - ADAM_B2) * (g * g)
    m_hat = nm / (1.0 - ADAM_B1 ** ADAM_STEP)
    v_hat = nv / (1.0 - ADAM_B2 ** ADAM_STEP)
    return (-ADAM_LR) * (m_hat / (jnp.sqrt(v_hat) + ADAM_EPS) + ADAM_WD * w), nm, nv


SMALL_NAMES = ("conv_b", "b_a", "b_x", "lru_lambda", "attn_sinks", "g_post_mix", "g_pre_ffn", "g_post_ffn")
PACK_WIDTH = 1024


def _pack_rows(vals):
    assert len(SMALL_NAMES) == SUBLANES
    row = lax.broadcasted_iota(jnp.int32, (SUBLANES, PACK_WIDTH), 0)
    tile = jnp.zeros((SUBLANES, PACK_WIDTH), F32)
    for k, name in enumerate(SMALL_NAMES):
        a = vals[name].reshape(1, -1)
        tile = jnp.where(row == k, jnp.pad(a, ((0, 0), (0, PACK_WIDTH - a.shape[1]))), tile)
    return tile


def _adamw_small(weights, mom_m, mom_v, parts, loss_parts, others):
    names = list(SMALL_NAMES) + [name for name, _, _ in others]
    views = [(1, weights[name].size) for name in SMALL_NAMES] + [view for _, view, _ in others]
    n, n_pack = len(names), len(SMALL_NAMES)

    def body(*refs):
        w_refs, m_refs, v_refs = refs[:n], refs[n:2 * n], refs[2 * n:3 * n]
        p_ref, l_ref = refs[3 * n], refs[3 * n + 1]
        o_refs = refs[3 * n + 2:3 * n + 2 + len(others)]
        loss_ref, outs = refs[3 * n + 2 + len(others)], refs[3 * n + 3 + len(others):]
        for k, (_, c) in enumerate(views):
            if k < n_pack:
                g = p_ref[0, k:k + 1, 0:c]
                for s in range(1, N_DEV):
                    g = g + p_ref[s, k:k + 1, 0:c]
            else:
                g = o_refs[k - n_pack][0]
                for s in range(1, N_DEV):
                    g = g + o_refs[k - n_pack][s]
            g_ref, d_ref, nm_ref, nv_ref = outs[4 * k:4 * k + 4]
            g_ref[...] = g
            d_ref[...], nm_ref[...], nv_ref[...] = _adamw_math(w_refs[k][...], m_refs[k][...], v_refs[k][...], g)
        total = l_ref[0]
        for s in range(1, N_DEV):
            total = total + l_ref[s]
        loss_ref[...] = total

    args = [src[name].reshape(view) for src in (weights, mom_m, mom_v) for name, view in zip(names, views)]
    res = pl.pallas_call(
        body, name="adamw_small",
        out_shape=[jax.ShapeDtypeStruct(loss_parts.shape[1:], F32)]
                  + [jax.ShapeDtypeStruct(view, F32) for view in views for _ in range(4)],
        compiler_params=pltpu.CompilerParams(vmem_limit_bytes=VMEM_LIMIT),
    )(*args, parts, loss_parts, *[p for _, _, p in others])
    out = {name: tuple(t.reshape(weights[name].shape) for t in res[1 + 4 * k:5 + 4 * k]) for k, name in enumerate(names)}
    return res[0], out


def _adamw(w, m, v, parts, name, carried=(), modes=()):
    rows, cols = w.shape
    tr = next((t for t in (256, 128) if rows % t == 0), rows)
    parts = parts if isinstance(parts, (list, tuple)) else [parts]

    def body(w_ref, m_ref, v_ref, *refs):
        p_refs, (g_ref, d_ref, nm_ref, nv_ref) = refs[:len(parts)], refs[len(parts):]

        def total(p_ref):
            g = p_ref[0].astype(F32)
            for s in range(1, N_DEV):
                g = g + p_ref[s].astype(F32)
            return g

        g = jnp.concatenate([total(p_ref) for p_ref in p_refs], axis=1) if len(parts) > 1 else total(p_refs[0])
        g_ref[...] = g
        d_ref[...], nm_ref[...], nv_ref[...] = _adamw_math(w_ref[...], m_ref[...], v_ref[...], g)

    blk = pl.BlockSpec((tr, cols), lambda i: (i, 0))
    in_specs = [blk, blk, blk] + [pl.BlockSpec((N_DEV, tr, p.shape[2]), lambda i: (0, i, 0)) for p in parts]
    out_shape = [jax.ShapeDtypeStruct((rows, cols), F32)] * 4
    if carried:
        return _hosting_call(body, name, rows // tr, in_specs, [blk] * 4, out_shape, [], (w, m, v, *parts), carried, modes)
    return pl.pallas_call(
        body, name=name, grid=(rows // tr,), in_specs=in_specs, out_specs=[blk] * 4, out_shape=out_shape,
        compiler_params=_params(("parallel",)),
    )(w, m, v, *parts)


def _cols_from_shards(g):
    return jnp.transpose(g, (1, 0, 2)).reshape(g.shape[1], N_DEV * g.shape[2])


def _cols_to_shards(a):
    r, c = a.shape
    return jnp.transpose(a.reshape(r, N_DEV, c // N_DEV), (1, 0, 2))


def _block_diag(w):
    per = LRU_HALF // LRU_BLOCK
    w = w.reshape(2, per, LRU_BLOCK, LRU_BLOCK)
    eye = jnp.eye(per, dtype=w.dtype)
    return (w[:, :, :, None, :] * eye[None, :, None, :, None]).reshape(2, LRU_HALF, LRU_HALF)


def _block_diag_extract(t):
    per = LRU_HALF // LRU_BLOCK
    t = t.reshape(2, per, LRU_BLOCK, per, LRU_BLOCK)
    return jnp.stack([t[:, b, :, b, :] for b in range(per)], axis=1).reshape(LRU_BLOCKS, LRU_BLOCK, LRU_BLOCK)


def kernel(x, meta_tokens, g_pre_mix, w_in, conv_w, conv_b, w_a, b_a, w_x, b_x, lru_lambda, attn_sinks, w_out, g_post_mix, g_pre_ffn, w_ff1, w_ff2, g_post_ffn, loss_target, m_meta_tokens, m_g_pre_mix, m_w_in, m_conv_w, m_conv_b, m_w_a, m_b_a, m_w_x, m_b_x, m_lru_lambda, m_attn_sinks, m_w_out, m_g_post_mix, m_g_pre_ffn, m_w_ff1, m_w_ff2, m_g_post_ffn, v_meta_tokens, v_g_pre_mix, v_w_in, v_conv_w, v_conv_b, v_w_a, v_b_a, v_w_x, v_b_x, v_lru_lambda, v_attn_sinks, v_w_out, v_g_post_mix, v_g_pre_ffn, v_w_ff1, v_w_ff2, v_g_post_ffn):
    weights = dict(meta_tokens=meta_tokens, g_pre_mix=g_pre_mix, w_in=w_in, conv_w=conv_w, conv_b=conv_b, w_a=w_a,
                   b_a=b_a, w_x=w_x, b_x=b_x, lru_lambda=lru_lambda, attn_sinks=attn_sinks, w_out=w_out,
                   g_post_mix=g_post_mix, g_pre_ffn=g_pre_ffn, w_ff1=w_ff1, w_ff2=w_ff2, g_post_ffn=g_post_ffn)
    mom_m = dict(meta_tokens=m_meta_tokens, g_pre_mix=m_g_pre_mix, w_in=m_w_in, conv_w=m_conv_w, conv_b=m_conv_b,
                 w_a=m_w_a, b_a=m_b_a, w_x=m_w_x, b_x=m_b_x, lru_lambda=m_lru_lambda, attn_sinks=m_attn_sinks,
                 w_out=m_w_out, g_post_mix=m_g_post_mix, g_pre_ffn=m_g_pre_ffn, w_ff1=m_w_ff1, w_ff2=m_w_ff2,
                 g_post_ffn=m_g_post_ffn)
    mom_v = dict(meta_tokens=v_meta_tokens, g_pre_mix=v_g_pre_mix, w_in=v_w_in, conv_w=v_conv_w, conv_b=v_conv_b,
                 w_a=v_w_a, b_a=v_b_a, w_x=v_w_x, b_x=v_b_x, lru_lambda=v_lru_lambda, attn_sinks=v_attn_sinks,
                 w_out=v_w_out, g_post_mix=v_g_post_mix, g_pre_ffn=v_g_pre_ffn, w_ff1=v_w_ff1, w_ff2=v_w_ff2,
                 g_post_ffn=v_g_post_ffn)
    order = list(weights)

    (g_win, g_meta, g_cw) = _gather_two_level([w_in[0].astype(BF16), meta_tokens, conv_w[0]], "gather_first")
    w_in_full = _cols_from_shards(g_win)
    meta_full = _cols_from_shards(g_meta)
    conv_w_full = _cols_from_shards(g_cw)

    head = jnp.concatenate([jnp.zeros((PAD_ROWS, D_MODEL), F32), meta_full], axis=0)
    wa_bd = _block_diag(w_a[0]).astype(BF16)
    wx_bd = _block_diag(w_x[0]).astype(BF16)
    bias = _attn_bias()

    w1_shard = w_ff1[0].astype(BF16)
    (qkv, zrec, u1, attn), (g_wout, w1a) = _in_proj_attn_fwd(
        head, x[0], g_pre_mix, w_in_full, attn_sinks, bias,
        [w_out[0].astype(BF16), w1_shard[:, :FF_HALF]], ["gather"] * 2)
    (rec, h_lru, kept), (w1b,) = _rec_fwd(zrec, conv_w_full, conv_b, wa_bd, b_a, wx_bd, b_x, lru_lambda,
                                         [w1_shard[:, FF_HALF:]], ["gather"])
    w_out_full = g_wout.reshape(D_MODEL, D_MODEL)
    w2_shard = w_ff2[0].astype(BF16)
    (mix, h1, act, u2), (w2a, w2b) = _mix_and_ffn_up(
        attn, rec, w_out_full, head, x[0], g_post_mix, g_pre_ffn, (w1a, w1b),
        [w2_shard[:FF_HALF], w2_shard[FF_HALF:]], ["gather"] * 2)
    w2_halves = [w.reshape(D_FF // 2, D_MODEL) for w in (w2a, w2b)]
    (dy, df, dg_post_ffn, loss_acc), w2t_halves = _ffn_down_loss(
        act, w2_halves, h1, loss_target[0], g_post_ffn, [w2_shard[:FF_HALF].T, w2_shard[FF_HALF:].T], ["gather"] * 2)

    (da1,), (w1ta,) = _ffn_bwd_act(df, w2t_halves, act, [w1_shard[:, :FF_HALF].T], ["gather"])
    (dw1h, dw2g), (w1tb,) = _ffn_bwd_weights(u2, da1, act, df, [w1_shard[:, FF_HALF:].T], ["gather"])
    w1t_halves = [w.reshape(D_FF // 2, D_MODEL) for w in (w1ta, w1tb)]
    (dh1, dg_pre_ffn), (p_w1a,) = _ffn_bwd_x(da1, w1t_halves, h1, dy, g_pre_ffn, [dw1h[0]], ["scatter"])
    (dattn, drec, dw_out, dg_post_mix), (p_w1b,) = _out_proj_bwd(dh1, mix, g_post_mix, w_out_full.T, attn, rec,
                                                                [dw1h[1]], ["scatter"])
    (dq, dkv, dsinks), (p_w2,) = _attn_bwd(qkv, dattn, attn_sinks, bias, [dw2g], ["scatter"])
    (drz, rec_small, dwa_bd, dwx_bd), (p_wout,) = _rec_bwd(
        drec, zrec, h_lru, kept, conv_w_full, wa_bd, wx_bd, lru_lambda,
        [dw_out.reshape(N_DEV, D_MODEL // N_DEV, D_MODEL)], ["scatter"])
    small_grads = dict(
        conv_b=rec_small[ROW_CONV_B], b_a=rec_small[ROW_B_A], b_x=rec_small[ROW_B_X], lru_lambda=rec_small[ROW_LAMBDA],
        attn_sinks=dsinks[:, 0], g_post_mix=dg_post_mix, g_pre_ffn=dg_pre_ffn, g_post_ffn=dg_post_ffn)
    gate_rows = (LRU_BLOCKS * LRU_BLOCK, LRU_BLOCK)
    gate_dense = (LRU_BLOCKS * LRU_BLOCK * LRU_BLOCK // PACK_WIDTH, PACK_WIDTH)
    (dw_in,), (p_cw, p_small, p_wa, p_wx) = _in_proj_bwd_w(
        u1, dq, dkv, drz,
        [_cols_to_shards(rec_small[0:CONV_WIDTH]), _pack_rows(small_grads),
         _block_diag_extract(dwa_bd).reshape(gate_dense), _block_diag_extract(dwx_bd).reshape(gate_dense)],
        ["scatter", "gather", "gather", "gather"])
    p_wa, p_wx = (p.reshape((N_DEV,) + gate_rows) for p in (p_wa, p_wx))
    (dh0, dg_pre_mix), (p_win,) = _in_proj_bwd_x(
        head, x[0], g_pre_mix, dh1, dq, dkv, drz, w_in_full.T, [_cols_to_shards(dw_in)], ["scatter"])

    res = {}
    res["w_in"] = _adamw(w_in[0], m_w_in[0], v_w_in[0], p_win, "adamw_w_in")
    res["w_out"] = _adamw(w_out[0], m_w_out[0], v_w_out[0], p_wout, "adamw_w_out")
    res["w_ff1"], (p_meta, p_gpm, p_loss) = _adamw(
        w_ff1[0], m_w_ff1[0], v_w_ff1[0], [p_w1a, p_w1b], "adamw_w_ff1",
        [_cols_to_shards(dh0[PAD_ROWS:BLOCK]), dg_pre_mix, loss_acc], ["scatter", "gather", "gather"])
    res["w_ff2"] = _adamw(w_ff2[0], m_w_ff2[0], v_w_ff2[0], p_w2, "adamw_w_ff2")
    for name in ("w_in", "w_out", "w_ff1", "w_ff2"):
        res[name] = tuple(t[None] for t in res[name])
    others = [("g_pre_mix", g_pre_mix.shape, p_gpm), ("meta_tokens", meta_tokens.shape, p_meta),
              ("conv_w", conv_w.shape[1:], p_cw), ("w_a", gate_rows, p_wa), ("w_x", gate_rows, p_wx)]
    loss_total, small = _adamw_small(weights, mom_m, mom_v, p_small, p_loss, others)
    res.update(small)

    grad_x = dh0[BLOCK:][None]
    outs = [loss_total[0, 0], grad_x]
    for k in range(4):
        outs += [res[name][k] for name in order]
    return tuple(outs)
```

```python
import jax
import jax.numpy as jnp
import numpy as np
from jax import lax
from jax.experimental import pallas as pl
from jax.experimental.pallas import tpu as pltpu

F32 = jnp.float32
BF16 = jnp.bfloat16

D_MODEL = 1024
N_META = 16
HEAD_DIM = 64
ATTN_HEADS = 8
KV_HEADS = 2
GQA_GROUP = ATTN_HEADS // KV_HEADS
ATTN_WIDTH = ATTN_HEADS * HEAD_DIM
KV_WIDTH = KV_HEADS * HEAD_DIM
QKV_WIDTH = ATTN_WIDTH + 2 * KV_WIDTH
LRU_WIDTH = 512
LRU_BLOCKS = 8
LRU_BLOCK = 64
LRU_HALF = 256
LRU_C = 8.0
CONV_WIDTH = 4
BLOCK = 128
PAD_ROWS = BLOCK - N_META
IN_WIDTH = QKV_WIDTH + 2 * LRU_WIDTH
D_FF = 4096
EPS = 1e-6
NEG = -1e30
N_DEV = 8
FF_CHUNK = D_FF // N_DEV
SUBLANES = 8
LANES = 128

ADAM_LR = 0.001
ADAM_B1 = 0.9
ADAM_B2 = 0.999
ADAM_EPS = 1e-08
ADAM_WD = 0.01
ADAM_STEP = 10

VMEM_LIMIT = 56 * 1024 * 1024


def _row_tile(rows):
    for t in (640, 512, 256, 128):
        if rows % t == 0:
            return t
    raise ValueError(rows)


def _big_tile(rows):
    for t in (1664, 1024, 512, 256, 128):
        if rows % t == 0:
            return t
    raise ValueError(rows)


def _rec_tile(rows):
    for t in (640, 256, 128):
        if rows % t == 0:
            return t
    raise ValueError(rows)


def _params(semantics):
    return pltpu.CompilerParams(dimension_semantics=semantics, vmem_limit_bytes=VMEM_LIMIT)


def _mm(a, b):
    return lax.dot_general(a, b, (((1,), (0,)), ((), ())), preferred_element_type=F32)


def _mm_nt(a, b):
    return lax.dot_general(a, b, (((1,), (1,)), ((), ())), preferred_element_type=F32)


def _mm_tn(a, b):
    return lax.dot_general(a, b, (((0,), (0,)), ((), ())), preferred_element_type=F32)


def _rms_fwd(x, g):
    rstd = lax.rsqrt(jnp.mean(x * x, axis=-1, keepdims=True) + EPS)
    xhat = x * rstd
    return xhat * g, xhat, rstd


def _rms_bwd(dy, xhat, rstd, g):
    dyg = dy * g
    c = jnp.mean(dyg * xhat, axis=-1, keepdims=True)
    dx = rstd * (dyg - xhat * c)
    dg = jnp.sum(dy * xhat, axis=0, keepdims=True)
    return dx, dg


def _sigmoid(x):
    return pl.reciprocal(1.0 + jnp.exp(-x), approx=True)


def _log1p(x):
    u = 1.0 + x
    return jnp.where(u == 1.0, x, jnp.log(u) * x / (u - 1.0))


def _one_minus_sq_exp(x, ex):
    return -jnp.tanh(x) * (1.0 + ex * ex)


TINY = 1e-30


def _sqrt_pos(y):
    r = lax.rsqrt(jnp.maximum(y, TINY))
    return y * r, r


def _softplus(x):
    return jnp.maximum(x, 0.0) + _log1p(jnp.exp(-jnp.abs(x)))


GELU_C = 0.7978845608028654
GELU_K = 0.044715


def _gelu(x):
    t = jnp.tanh(GELU_C * (x + GELU_K * x * x * x))
    return 0.5 * x * (1.0 + t), t


def _gelu_grad(x, t):
    return 0.5 * (1.0 + t) + 0.5 * x * (1.0 - t * t) * GELU_C * (1.0 + 3.0 * GELU_K * x * x)


def _full(shape):
    return pl.BlockSpec(shape, lambda *_: (0,) * len(shape))


def _resident(shape):
    return pl.BlockSpec(shape, lambda *_: (0,) * len(shape), pipeline_mode=pl.Buffered(1))


def _exchange_copies(ins, outs, sems, modes):
    send_sems, recv_sems, local_sems = sems
    x, y, c = lax.axis_index("x"), lax.axis_index("y"), lax.axis_index("c")
    me = 4 * x + 2 * y + c

    def block(a, dev):
        return ins[a] if modes[a] == "gather" else ins[a].at[dev]

    local = [pltpu.make_async_copy(block(a, me), outs[a].at[me], local_sems.at[a]) for a in range(len(ins))]
    sends, recvs = [], []
    for a in range(len(ins)):
        for k in range(N_DEV - 1):
            bits = k + 1
            px = jnp.bitwise_xor(x, (bits >> 2) & 1)
            py = jnp.bitwise_xor(y, (bits >> 1) & 1)
            pc = jnp.bitwise_xor(c, bits & 1)
            peer = 4 * px + 2 * py + pc
            common = dict(src_ref=block(a, peer), send_sem=send_sems.at[a, k], recv_sem=recv_sems.at[a, k],
                          device_id=(px, py, pc), device_id_type=pl.DeviceIdType.MESH)
            sends.append(pltpu.make_async_remote_copy(dst_ref=outs[a].at[me], **common))
            recvs.append(pltpu.make_async_remote_copy(dst_ref=outs[a].at[peer], **common))
    return local, sends, recvs


def _exchange_start(ins, outs, sems, modes):
    local, sends, _ = _exchange_copies(ins, outs, sems, modes)
    for cp in local + sends:
        cp.start()


def _exchange_wait(ins, outs, sems, modes):
    local, sends, recvs = _exchange_copies(ins, outs, sems, modes)
    for cp in recvs:
        cp.wait_recv()
    for cp in sends:
        cp.wait_send()
    for cp in local:
        cp.wait()


def _exchange_shapes(arrays, modes):
    return [jax.ShapeDtypeStruct((N_DEV,) + a.shape if mode == "gather" else a.shape, a.dtype)
            for a, mode in zip(arrays, modes)]


def _exchange_sems(na):
    return [pltpu.SemaphoreType.DMA((na, N_DEV - 1)), pltpu.SemaphoreType.DMA((na, N_DEV - 1)),
            pltpu.SemaphoreType.DMA((na,))]


ANY_SPACE = pl.BlockSpec(memory_space=pl.ANY)


def _exchange(arrays, modes, name):
    na = len(arrays)

    def body(*refs):
        ins, outs, sems = refs[:na], refs[na:2 * na], refs[2 * na:]
        _exchange_start(ins, outs, sems, modes)
        _exchange_wait(ins, outs, sems, modes)

    return pl.pallas_call(
        body, name=name, out_shape=_exchange_shapes(arrays, modes),
        in_specs=[ANY_SPACE] * na, out_specs=[ANY_SPACE] * na, scratch_shapes=_exchange_sems(na),
        compiler_params=pltpu.CompilerParams(has_side_effects=True),
    )(*arrays)


def _gather_two_level(arrays, name):
    na = len(arrays)

    def body(*refs):
        ins, outs = refs[:na], refs[na:2 * na]
        send_sems, recv_sems, local_sems = refs[2 * na:]
        x, y, c = lax.axis_index("x"), lax.axis_index("y"), lax.axis_index("c")
        me, sibling = (x, y, c), (x, y, 1 - c)
        chips = [(1 - x, y), (x, 1 - y), (1 - x, 1 - y)]

        def copy(a, k, block, to, src=None):
            slot = outs[a].at[4 * block[0] + 2 * block[1] + block[2]]
            return pltpu.make_async_remote_copy(
                src_ref=slot if src is None else src, dst_ref=slot, send_sem=send_sems.at[a, k],
                recv_sem=recv_sems.at[a, k], device_id=to, device_id_type=pl.DeviceIdType.MESH)

        local = [pltpu.make_async_copy(ins[a], outs[a].at[4 * x + 2 * y + c], local_sems.at[a]) for a in range(na)]
        first = []
        for a in range(na):
            first.append(copy(a, 0, me, sibling, src=ins[a]))
            first += [copy(a, 1 + j, me, (*chip, c), src=ins[a]) for j, chip in enumerate(chips)]
        for cp in local + first:
            cp.start()
        passed = []
        for j, chip in enumerate(chips):
            for a in range(na):
                copy(a, 1 + j, (*chip, c), me).wait_recv()
                passed.append(copy(a, 4 + j, (*chip, c), sibling))
                passed[-1].start()
        for a in range(na):
            copy(a, 0, sibling, me).wait_recv()
            for j, chip in enumerate(chips):
                copy(a, 4 + j, (*chip, 1 - c), me).wait_recv()
        for cp in first + passed:
            cp.wait_send()
        for cp in local:
            cp.wait()

    return pl.pallas_call(
        body, name=name, out_shape=_exchange_shapes(arrays, ["gather"] * na),
        in_specs=[ANY_SPACE] * na, out_specs=[ANY_SPACE] * na, scratch_shapes=_exchange_sems(na),
        compiler_params=pltpu.CompilerParams(has_side_effects=True),
    )(*arrays)


def _hosting_call(body, name, steps, in_specs, out_specs, out_shape, scratch_shapes, args, arrays, modes):
    n_in, n_out, n_scr, na = len(in_specs), len(out_specs), len(scratch_shapes), len(arrays)
    grid = steps if isinstance(steps, tuple) else (steps,)

    def hosting_body(*refs):
        cuts = [0]
        for n in (n_in, na, n_out, na, n_scr, 3):
            cuts.append(cuts[-1] + n)
        ins, x_ins, outs, x_outs, scr, sems = (refs[cuts[p]:cuts[p + 1]] for p in range(6))
        first, last = True, True
        for axis, n in enumerate(grid):
            first = first & (pl.program_id(axis) == 0)
            last = last & (pl.program_id(axis) == n - 1)

        @pl.when(first)
        def _():
            _exchange_start(x_ins, x_outs, sems, modes)

        body(*ins, *outs, *scr)

        @pl.when(last)
        def _():
            _exchange_wait(x_ins, x_outs, sems, modes)

    res = pl.pallas_call(
        hosting_body, name=name, grid=grid,
        in_specs=list(in_specs) + [ANY_SPACE] * na, out_specs=list(out_specs) + [ANY_SPACE] * na,
        out_shape=list(out_shape) + _exchange_shapes(arrays, modes),
        scratch_shapes=list(scratch_shapes) + _exchange_sems(na),
        compiler_params=_params(("arbitrary",) * len(grid)),
    )(*args, *arrays)
    return res[:n_out], res[n_out:]


def _frame_rows(src_hbm, buf, sem, i, steps, tm):
    def first():
        return pltpu.make_async_copy(src_hbm.at[pl.ds(0, tm - BLOCK)], buf.at[0, pl.ds(BLOCK, tm - BLOCK)], sem.at[0])

    def later(t, slot):
        return pltpu.make_async_copy(src_hbm.at[pl.ds(pl.multiple_of(t * tm - BLOCK, SUBLANES), tm)], buf.at[slot], sem.at[slot])

    slot = i % 2

    @pl.when(i == 0)
    def _():
        first().start()

    @pl.when(i + 1 < steps)
    def _():
        later(i + 1, 1 - slot).start()

    @pl.when(i == 0)
    def _():
        first().wait()

    @pl.when(i > 0)
    def _():
        later(i, slot).wait()

    return slot


def _frame_scratch(tm):
    return [pltpu.VMEM((2, tm, D_MODEL), F32), pltpu.SemaphoreType.DMA((2,))]


def _h0_tile(head_ref, x_hbm, buf, sem, i, steps, tm):
    slot = _frame_rows(x_hbm, buf, sem, i, steps, tm)

    @pl.when(i == 0)
    def _():
        buf[0, 0:BLOCK, :] = head_ref[...]

    return buf[slot]


N_BIAS = 3


def _attn_bias():
    key = np.arange(2 * BLOCK)[:, None]
    r = np.arange(GQA_GROUP * BLOCK)[None, :] % BLOCK
    band = (key > r) & (key <= r + BLOCK)
    out = [np.where(band & ((n - 1) * BLOCK + key >= PAD_ROWS), 0.0, NEG) for n in range(N_BIAS)]
    return jnp.asarray(np.stack(out), F32)


def _attn_probs(k2, q4, bias, sink_row):
    s = _mm_nt(k2, q4) * (HEAD_DIM ** -0.5) + bias
    m = jnp.maximum(jnp.max(s, axis=0, keepdims=True), sink_row)
    p = jnp.exp(s - m)
    es = jnp.exp(sink_row - m)
    inv = 1.0 / (jnp.sum(p, axis=0, keepdims=True) + es)
    return p * inv, es * inv


def _heads(ref, rows, first, count):
    return jnp.concatenate([ref[rows, (first + g) * HEAD_DIM:(first + g + 1) * HEAD_DIM] for g in range(count)], axis=0)


def _keys_of_block(prev_ref, cur_ref, b, kv):
    sl = slice(kv * HEAD_DIM, (kv + 1) * HEAD_DIM)
    before = prev_ref[:, sl] if b == 0 else cur_ref[(b - 1) * BLOCK:b * BLOCK, sl]
    return jnp.concatenate([before, cur_ref[b * BLOCK:(b + 1) * BLOCK, sl]], axis=0)


def _bias_of_block(bias_ref, block):
    return bias_ref[jnp.minimum(block, N_BIAS - 1)]


def _sink_row(sink_ref, kv):
    g = lax.broadcasted_iota(jnp.int32, (1, GQA_GROUP * BLOCK), 1) // BLOCK
    row = jnp.full((1, GQA_GROUP * BLOCK), sink_ref[0, kv * GQA_GROUP], F32)
    for i in range(1, GQA_GROUP):
        row = jnp.where(g == i, sink_ref[0, kv * GQA_GROUP + i], row)
    return row


def _from_head_major(pieces):
    return jnp.concatenate(pieces, axis=0).T


def _attn_specs(tm, tile_of):
    nbt = tm // BLOCK
    k_col, v_col = ATTN_WIDTH // KV_WIDTH, ATTN_WIDTH // KV_WIDTH + 1
    before = lambda i: jnp.maximum(tile_of(i) * nbt - 1, 0)
    return [pl.BlockSpec((tm, ATTN_WIDTH), lambda i: (tile_of(i), 0)),
            pl.BlockSpec((BLOCK, KV_WIDTH), lambda i: (before(i), k_col)),
            pl.BlockSpec((tm, KV_WIDTH), lambda i: (tile_of(i), k_col)),
            pl.BlockSpec((BLOCK, KV_WIDTH), lambda i: (before(i), v_col)),
            pl.BlockSpec((tm, KV_WIDTH), lambda i: (tile_of(i), v_col))]


def _in_proj_attn_fwd(head, x, g1, w_in, sinks, bias, carried, modes):
    rows = BLOCK + x.shape[0]
    tm = _row_tile(rows)
    steps, nbt = rows // tm, tm // BLOCK

    def body(head_ref, g_ref, w_ref, sink_ref, bias_ref, x_hbm, qkv_ref, zrec_ref, u_ref, o_ref, buf, sem, kv_before):
        i = pl.program_id(0)
        h = _h0_tile(head_ref, x_hbm, buf, sem, i, steps, tm)
        u, _, _ = _rms_fwd(h, g_ref[...])
        u = u.astype(BF16)
        u_ref[...] = u
        z = _mm(u, w_ref[...])
        qkv_ref[...] = z[:, :QKV_WIDTH].astype(BF16)
        zrec_ref[...] = z[:, QKV_WIDTH:]

        @pl.when(i == 0)
        def _():
            kv_before[...] = jnp.zeros_like(kv_before)

        kc_ref, vc_ref = (qkv_ref.at[:, pl.ds(ATTN_WIDTH + c * KV_WIDTH, KV_WIDTH)] for c in range(2))
        kp_ref, vp_ref = (kv_before.at[:, pl.ds(c * KV_WIDTH, KV_WIDTH)] for c in range(2))
        for b in range(nbt):
            blk = slice(b * BLOCK, (b + 1) * BLOCK)
            bias_t = _bias_of_block(bias_ref, i * nbt + b)
            pieces = []
            for kv in range(KV_HEADS):
                k2 = _keys_of_block(kp_ref, kc_ref, b, kv)
                v2 = _keys_of_block(vp_ref, vc_ref, b, kv)
                q4 = _heads(qkv_ref, blk, kv * GQA_GROUP, GQA_GROUP)
                pn, _ = _attn_probs(k2, q4, bias_t, _sink_row(sink_ref, kv))
                ot = _mm_tn(v2, pn.astype(BF16))
                pieces += [ot[:, g * BLOCK:(g + 1) * BLOCK] for g in range(GQA_GROUP)]
            o_ref[blk, :] = _from_head_major(pieces).astype(BF16)
        kv_before[...] = qkv_ref[tm - BLOCK:tm, ATTN_WIDTH:]

    wide = pl.BlockSpec((tm, D_MODEL), lambda i: (i, 0))
    return _hosting_call(
        body, "in_proj_attn_fwd", steps,
        [_full((BLOCK, D_MODEL)), _full((1, D_MODEL)), _resident((D_MODEL, IN_WIDTH)), pl.BlockSpec(memory_space=pltpu.SMEM),
         _resident((N_BIAS, 2 * BLOCK, GQA_GROUP * BLOCK)), ANY_SPACE],
        [pl.BlockSpec((tm, QKV_WIDTH), lambda i: (i, 0)), pl.BlockSpec((tm, 2 * LRU_WIDTH), lambda i: (i, 0)), wide,
         pl.BlockSpec((tm, ATTN_WIDTH), lambda i: (i, 0))],
        [jax.ShapeDtypeStruct((rows, QKV_WIDTH), BF16), jax.ShapeDtypeStruct((rows, 2 * LRU_WIDTH), F32),
         jax.ShapeDtypeStruct((rows, D_MODEL), BF16), jax.ShapeDtypeStruct((rows, ATTN_WIDTH), BF16)],
        _frame_scratch(tm) + [pltpu.VMEM((BLOCK, 2 * KV_WIDTH), BF16)],
        (head, g1, w_in, sinks, bias, x), carried, modes)


def _conv_taps(xbuf, tm):
    return [xbuf[pl.ds(SUBLANES - (CONV_WIDTH - 1 - j), tm), :] for j in range(CONV_WIDTH)]


def _lru_halves(xc):
    return [xc[:, h * LRU_HALF:(h + 1) * LRU_HALF].astype(BF16) for h in range(2)]


def _lru_gates(xc, wa_ref, ba_ref, wx_ref, bx_ref, lam_ref):
    halves = _lru_halves(xc)
    gate_r = jnp.concatenate([_mm(halves[h], wa_ref[h]) for h in range(2)], axis=1) + ba_ref[...]
    gate_i = jnp.concatenate([_mm(halves[h], wx_ref[h]) for h in range(2)], axis=1) + bx_ref[...]
    r = _sigmoid(gate_r)
    ig = _sigmoid(gate_i)
    log_a = (-LRU_C) * r * _softplus(-lam_ref[...])
    a = jnp.exp(log_a)
    mult, _ = _sqrt_pos(_one_minus_sq_exp(log_a, a))
    return r, ig, a, mult


KEPT_XC, KEPT_A, KEPT_MULT, KEPT_R, KEPT_I, N_KEPT = 0, 1, 2, 3, 4, 5


def _scan_tile(a_ref, u_ref, out_ref, carry, tm):
    row = lax.broadcasted_iota(jnp.int32, (SUBLANES, LRU_WIDTH), 0)

    def step(j, before):
        o = pl.multiple_of(j * SUBLANES, SUBLANES)
        a = a_ref[pl.ds(o, SUBLANES), :]
        u = u_ref[pl.ds(o, SUBLANES), :]
        for s in (1, 2, 4):
            keep = row >= s
            u = jnp.where(keep, a * pltpu.roll(u, s, 0) + u, u)
            a = jnp.where(keep, a * pltpu.roll(a, s, 0), a)
        out = a * before + u
        out_ref[pl.ds(o, SUBLANES), :] = out
        return out[SUBLANES - 1:SUBLANES, :]

    return lax.fori_loop(0, tm // SUBLANES, step, carry)


def _rec_fwd(zrec, conv_w, conv_b, wa_bd, b_a, wx_bd, b_x, lam, carried, modes):
    rows = zrec.shape[0]
    tm = _row_tile(rows)

    def body(xr_ref, yr_ref, cw_ref, cb_ref, wa_ref, ba_ref, wx_ref, bx_ref, lam_ref, rec_ref, h_ref, kept_ref,
             xbuf, a_s, u_s, carry):
        i = pl.program_id(0)

        @pl.when(i == 0)
        def _():
            xbuf[0:SUBLANES, :] = jnp.zeros((SUBLANES, LRU_WIDTH), F32)
            carry[...] = jnp.zeros_like(carry)

        @pl.when(i > 0)
        def _():
            xbuf[0:SUBLANES, :] = xbuf[tm:tm + SUBLANES, :]

        xbuf[SUBLANES:SUBLANES + tm, :] = xr_ref[...]
        taps = _conv_taps(xbuf, tm)
        xc = cb_ref[...] + sum(cw_ref[j:j + 1, :] * taps[j] for j in range(CONV_WIDTH))
        r, ig, a, mult = _lru_gates(xc, wa_ref, ba_ref, wx_ref, bx_ref, lam_ref)
        for k, val in ((KEPT_XC, xc), (KEPT_A, a), (KEPT_MULT, mult), (KEPT_R, r), (KEPT_I, ig)):
            kept_ref[:, k * LRU_WIDTH:(k + 1) * LRU_WIDTH] = val
        grow = i * tm + lax.broadcasted_iota(jnp.int32, (tm, LRU_WIDTH), 0)
        a_s[...] = a
        u_s[...] = jnp.where(grow >= PAD_ROWS, mult * (ig * xc), 0.0)
        carry[0:1, :] = _scan_tile(a_s, u_s, h_ref, carry[0:1, :], tm)
        gel, _ = _gelu(yr_ref[...])
        rec_ref[...] = (gel * h_ref[...]).astype(BF16)

    vec = _full((1, LRU_WIDTH))
    bd = _full((2, LRU_HALF, LRU_HALF))
    return _hosting_call(
        body, "rec_fwd", rows // tm,
        [pl.BlockSpec((tm, LRU_WIDTH), lambda i: (i, 0)), pl.BlockSpec((tm, LRU_WIDTH), lambda i: (i, 1)),
         _full((CONV_WIDTH, LRU_WIDTH)), vec, bd, vec, bd, vec, vec],
        [pl.BlockSpec((tm, LRU_WIDTH), lambda i: (i, 0))] * 2 + [pl.BlockSpec((tm, N_KEPT * LRU_WIDTH), lambda i: (i, 0))],
        [jax.ShapeDtypeStruct((rows, LRU_WIDTH), BF16), jax.ShapeDtypeStruct((rows, LRU_WIDTH), F32),
         jax.ShapeDtypeStruct((rows, N_KEPT * LRU_WIDTH), F32)],
        [pltpu.VMEM((tm + SUBLANES, LRU_WIDTH), F32), pltpu.VMEM((tm, LRU_WIDTH), F32),
         pltpu.VMEM((tm, LRU_WIDTH), F32), pltpu.VMEM((SUBLANES, LRU_WIDTH), F32)],
        (zrec, zrec, conv_w, conv_b, wa_bd, b_a, wx_bd, b_x, lam), carried, modes)


FF_COLS = 1024
FF_HALF = FF_CHUNK // 2


def _hidden_at(d, half):
    return half * (D_FF // 2) + d * FF_HALF


def _mix_and_ffn_up(attn, rec, w_out, head, x, g2, g3, w1_halves, carried, modes):
    rows = attn.shape[0]
    tm = _row_tile(rows)
    steps = rows // tm

    def body(attn_ref, rec_ref, w_ref, head_ref, g2_ref, g3_ref, wa_ref, wb_ref, x_hbm,
             mix_ref, h1_ref, act_ref, u_ref, buf, sem):
        h0 = _h0_tile(head_ref, x_hbm, buf, sem, pl.program_id(0), steps, tm)
        mix = _mm(attn_ref[...], w_ref[0:ATTN_WIDTH, :]) + _mm(rec_ref[...], w_ref[ATTN_WIDTH:, :])
        y, _, _ = _rms_fwd(mix, g2_ref[...])
        mix_ref[...] = mix
        h1_ref[...] = h0 + y
        u, _, _ = _rms_fwd(h1_ref[...], g3_ref[...])
        u = u.astype(BF16)
        u_ref[...] = u
        for half, w1_ref in enumerate((wa_ref, wb_ref)):
            for d in range(N_DEV):
                c = _hidden_at(d, half)
                a1 = jnp.maximum(_mm(u, w1_ref[d]), 0.0)
                act_ref[:, c:c + FF_HALF] = (a1 * a1).astype(BF16)

    half_in = pl.BlockSpec((tm, ATTN_WIDTH), lambda i: (i, 0))
    wide = pl.BlockSpec((tm, D_MODEL), lambda i: (i, 0))
    return _hosting_call(
        body, "mix_and_ffn_up", steps,
        [half_in, half_in, _resident((D_MODEL, D_MODEL)), _full((BLOCK, D_MODEL)), _full((1, D_MODEL)), _full((1, D_MODEL))]
        + [_resident((N_DEV, D_MODEL, FF_HALF))] * 2 + [ANY_SPACE],
        [wide, wide, pl.BlockSpec((tm, D_FF), lambda i: (i, 0)), wide],
        [jax.ShapeDtypeStruct((rows, D_MODEL), F32)] * 2
        + [jax.ShapeDtypeStruct((rows, D_FF), BF16), jax.ShapeDtypeStruct((rows, D_MODEL), BF16)],
        _frame_scratch(tm), (attn, rec, w_out, head, g2, g3, *w1_halves, x), carried, modes)


def _ffn_down_loss(act, w2_halves, h1, target, g4, carried, modes):
    rows = h1.shape[0]
    tm = _row_tile(rows)
    steps = rows // tm
    kh = D_FF // 2

    def body(act_ref, wa_ref, wb_ref, h_ref, g_ref, t_hbm, dy_ref, df_ref, dg_ref, loss_ref, buf, sem):
        i = pl.program_id(0)
        slot = _frame_rows(t_hbm, buf, sem, i, steps, tm)

        @pl.when(i == 0)
        def _():
            dg_ref[...] = jnp.zeros_like(dg_ref)
            loss_ref[...] = jnp.zeros_like(loss_ref)
            buf[0, 0:BLOCK, :] = jnp.zeros((BLOCK, D_MODEL), F32)

        g = g_ref[...]
        f = _mm(act_ref[:, :kh], wa_ref[...]) + _mm(act_ref[:, kh:], wb_ref[...])
        y, fhat, rstd = _rms_fwd(f, g)
        grow = i * tm + lax.broadcasted_iota(jnp.int32, (tm, D_MODEL), 0)
        err = jnp.where(grow >= BLOCK, h_ref[...] + y - buf[slot], 0.0)
        loss_ref[...] += (0.5 / D_MODEL) * jnp.sum(err * err)
        dy = err * (1.0 / D_MODEL)
        df, dg = _rms_bwd(dy, fhat, rstd, g)
        dy_ref[...] = dy
        df_ref[...] = df.astype(BF16)
        dg_ref[...] += dg

    wide = pl.BlockSpec((tm, D_MODEL), lambda i: (i, 0))
    return _hosting_call(
        body, "ffn_down_loss", steps,
        [pl.BlockSpec((tm, D_FF), lambda i: (i, 0)), _resident((kh, D_MODEL)), _resident((kh, D_MODEL)), wide,
         _full((1, D_MODEL)), ANY_SPACE],
        [wide, wide, _full((1, D_MODEL)), _full((SUBLANES, LANES))],
        [jax.ShapeDtypeStruct((rows, D_MODEL), F32), jax.ShapeDtypeStruct((rows, D_MODEL), BF16),
         jax.ShapeDtypeStruct((1, D_MODEL), F32), jax.ShapeDtypeStruct((SUBLANES, LANES), F32)],
        _frame_scratch(tm), (act, *w2_halves, h1, g4, target), carried, modes)


def _ffn_bwd_act(df, w2t_halves, act, carried, modes):
    rows = df.shape[0]
    tm = _row_tile(rows)

    def body(df_ref, wa_ref, wb_ref, act_ref, da_ref):
        df_t = df_ref[...]
        for half, w_ref in enumerate((wa_ref, wb_ref)):
            for d in range(N_DEV):
                cols = slice(_hidden_at(d, half), _hidden_at(d, half) + FF_HALF)
                dact = _mm(df_t, w_ref[d])
                relu_a1, _ = _sqrt_pos(act_ref[:, cols].astype(F32))
                da_ref[:, cols] = (dact * (2.0 * relu_a1)).astype(BF16)

    hidden = pl.BlockSpec((tm, D_FF), lambda i: (i, 0))
    return _hosting_call(
        body, "ffn_bwd_act", rows // tm,
        [pl.BlockSpec((tm, D_MODEL), lambda i: (i, 0))] + [_resident((N_DEV, D_MODEL, FF_HALF))] * 2 + [hidden],
        [hidden],
        [jax.ShapeDtypeStruct((rows, D_FF), BF16)],
        [], (df, *w2t_halves, act), carried, modes)


def _ffn_bwd_x(da, w1t_halves, h1, dy, g3, carried, modes):
    rows = h1.shape[0]
    tm = _row_tile(rows)
    kh = D_FF // 2

    def body(da_ref, wa_ref, wb_ref, h_ref, dy_ref, g_ref, dh_ref, dg_ref):
        @pl.when(pl.program_id(0) == 0)
        def _():
            dg_ref[...] = jnp.zeros_like(dg_ref)

        g = g_ref[...]
        _, xhat, rstd = _rms_fwd(h_ref[...], g)
        du = _mm(da_ref[:, :kh], wa_ref[...]) + _mm(da_ref[:, kh:], wb_ref[...])
        dx, dg = _rms_bwd(du, xhat, rstd, g)
        dh_ref[...] = dy_ref[...] + dx
        dg_ref[...] += dg

    wide = pl.BlockSpec((tm, D_MODEL), lambda i: (i, 0))
    return _hosting_call(
        body, "ffn_bwd_x", rows // tm,
        [pl.BlockSpec((tm, D_FF), lambda i: (i, 0)), _resident((kh, D_MODEL)), _resident((kh, D_MODEL)), wide, wide,
         _full((1, D_MODEL))],
        [wide, _full((1, D_MODEL))],
        [jax.ShapeDtypeStruct((rows, D_MODEL), F32), jax.ShapeDtypeStruct((1, D_MODEL), F32)],
        [], (da, *w1t_halves, h1, dy, g3), carried, modes)


def _ffn_bwd_weights(u2, da, act, df, carried, modes):
    rows = u2.shape[0]
    tb = _big_tile(rows)
    steps = rows // tb
    per = FF_COLS // FF_HALF

    def body(u_ref, da_ref, act_ref, df_ref, dw1_ref, dw2_ref, acc1, acc2):
        i = pl.program_id(1)

        @pl.when(i == 0)
        def _():
            acc1[...] = jnp.zeros_like(acc1)
            acc2[...] = jnp.zeros_like(acc2)

        acc1[...] += _mm_tn(u_ref[...], da_ref[...])
        acc2[...] += _mm_tn(act_ref[...], df_ref[...])

        @pl.when(i == steps - 1)
        def _():
            for p in range(per):
                c = p * FF_HALF
                dw1_ref[p] = acc1[:, c:c + FF_HALF].astype(BF16)
                dw2_ref[p] = acc2[c:c + FF_HALF, :].astype(BF16)

    wide = pl.BlockSpec((tb, D_MODEL), lambda j, i: (i, 0))
    chunk = pl.BlockSpec((tb, FF_COLS), lambda j, i: (i, j))
    return _hosting_call(
        body, "ffn_bwd_weights", (D_FF // FF_COLS, steps),
        [wide, chunk, chunk, wide],
        [pl.BlockSpec((None, per, D_MODEL, FF_HALF), lambda j, i: (j // 2, j % 2, 0, 0)),
         pl.BlockSpec((per, FF_HALF, D_MODEL), lambda j, i: (j % 2, j // 2, 0))],
        [jax.ShapeDtypeStruct((2, N_DEV, D_MODEL, FF_HALF), BF16), jax.ShapeDtypeStruct((N_DEV, FF_CHUNK, D_MODEL), BF16)],
        [pltpu.VMEM((D_MODEL, FF_COLS), F32), pltpu.VMEM((FF_COLS, D_MODEL), F32)],
        (u2, da, act, df), carried, modes)


def _out_proj_bwd(dh1, mix, g2, w_out_t, attn, rec, carried, modes):
    rows = dh1.shape[0]
    tm = _row_tile(rows)
    steps = rows // tm

    def body(dh_ref, mix_ref, g_ref, w_ref, attn_ref, rec_ref, dattn_ref, drec_ref, dw_ref, dg_ref, acc):
        i = pl.program_id(0)

        @pl.when(i == 0)
        def _():
            acc[...] = jnp.zeros_like(acc)
            dg_ref[...] = jnp.zeros_like(dg_ref)

        g = g_ref[...]
        _, xhat, rstd = _rms_fwd(mix_ref[...], g)
        dmix, dg = _rms_bwd(dh_ref[...], xhat, rstd, g)
        dmix = dmix.astype(BF16)
        dg_ref[...] += dg
        din = _mm(dmix, w_ref[...])
        dattn_ref[...] = din[:, :ATTN_WIDTH].astype(BF16)
        drec_ref[...] = din[:, ATTN_WIDTH:]
        acc[0:ATTN_WIDTH, :] += _mm_tn(attn_ref[...], dmix)
        acc[ATTN_WIDTH:, :] += _mm_tn(rec_ref[...], dmix)

        @pl.when(i == steps - 1)
        def _():
            dw_ref[...] = acc[...].astype(BF16)

    half = pl.BlockSpec((tm, ATTN_WIDTH), lambda i: (i, 0))
    wide = pl.BlockSpec((tm, D_MODEL), lambda i: (i, 0))
    return _hosting_call(
        body, "out_proj_bwd", steps,
        [wide, wide, _full((1, D_MODEL)), _resident((D_MODEL, D_MODEL)), half, half],
        [half, half, _full((D_MODEL, D_MODEL)), _full((1, D_MODEL))],
        [jax.ShapeDtypeStruct((rows, ATTN_WIDTH), BF16), jax.ShapeDtypeStruct((rows, LRU_WIDTH), F32),
         jax.ShapeDtypeStruct((D_MODEL, D_MODEL), BF16), jax.ShapeDtypeStruct((1, D_MODEL), F32)],
        [pltpu.VMEM((D_MODEL, D_MODEL), F32)],
        (dh1, mix, g2, w_out_t, attn, rec), carried, modes)


def _attn_bwd(qkv, dattn, sinks, bias, carried, modes):
    rows = qkv.shape[0]
    tm = _row_tile(rows)
    nbt, nt = tm // BLOCK, rows // tm

    def body(sink_ref, bias_ref, do_ref, q_ref, kp_ref, kc_ref, vp_ref, vc_ref, dq_ref, dkv_hbm, dsink_ref,
             dk_c, dv_c, stage, wsem):
        i = pl.program_id(0)
        slot = i % 2

        def first_write(s):
            return pltpu.make_async_copy(stage.at[s, pl.ds(BLOCK, tm - BLOCK)], dkv_hbm.at[pl.ds(0, tm - BLOCK)], wsem.at[s])

        def tile_write(t, s):
            return pltpu.make_async_copy(stage.at[s], dkv_hbm.at[pl.ds(pl.multiple_of(t * tm - BLOCK, BLOCK), tm)], wsem.at[s])

        def last_write(s):
            return pltpu.make_async_copy(stage.at[s, pl.ds(0, BLOCK)], dkv_hbm.at[pl.ds(rows - BLOCK, BLOCK)], wsem.at[s])

        def wait_write(t, s):
            @pl.when(t == 0)
            def _():
                first_write(s).wait()

            @pl.when(t > 0)
            def _():
                tile_write(t, s).wait()

        @pl.when(i == 0)
        def _():
            dk_c[...] = jnp.zeros_like(dk_c)
            dv_c[...] = jnp.zeros_like(dv_c)
            dsink_ref[...] = jnp.zeros_like(dsink_ref)

        @pl.when(i >= 2)
        def _():
            wait_write(i - 2, slot)

        @pl.when(i < nt)
        def _():
            dk_late, dv_late = dk_c[...], dv_c[...]
            dsink_rows = [jnp.zeros((1, LANES), F32)] * ATTN_HEADS
            for b in range(nbt):
                blk = slice(b * BLOCK, (b + 1) * BLOCK)
                bias_t = _bias_of_block(bias_ref, i * nbt + b)
                dq_parts, dk_parts, dv_parts = [], [], []
                for kv in range(KV_HEADS):
                    k2 = _keys_of_block(kp_ref, kc_ref, b, kv)
                    v2 = _keys_of_block(vp_ref, vc_ref, b, kv)
                    q4 = _heads(q_ref, blk, kv * GQA_GROUP, GQA_GROUP)
                    do4 = _heads(do_ref, blk, kv * GQA_GROUP, GQA_GROUP)
                    pn, psink = _attn_probs(k2, q4, bias_t, _sink_row(sink_ref, kv))
                    dpn = _mm_nt(v2, do4)
                    delta = jnp.sum(pn * dpn, axis=0, keepdims=True)
                    ds = ((pn * (dpn - delta)) * (HEAD_DIM ** -0.5)).astype(BF16)
                    dqt = _mm_tn(k2, ds)
                    dq_parts += [dqt[:, g * BLOCK:(g + 1) * BLOCK] for g in range(GQA_GROUP)]
                    dk_parts.append(_mm(ds, q4))
                    dv_parts.append(_mm(pn.astype(BF16), do4))
                    sd = psink * delta
                    for g in range(GQA_GROUP):
                        h = kv * GQA_GROUP + g
                        dsink_rows[h] = dsink_rows[h] - jnp.sum(sd[:, g * BLOCK:(g + 1) * BLOCK])
                dq_ref[blk, :] = _from_head_major(dq_parts).astype(BF16)
                dk2 = jnp.concatenate(dk_parts, axis=1)
                dv2 = jnp.concatenate(dv_parts, axis=1)
                stage[slot, blk, 0:KV_WIDTH] = (dk_late + dk2[0:BLOCK]).astype(BF16)
                stage[slot, blk, KV_WIDTH:] = (dv_late + dv2[0:BLOCK]).astype(BF16)
                dk_late, dv_late = dk2[BLOCK:], dv2[BLOCK:]
            dk_c[...] = dk_late
            dv_c[...] = dv_late
            dsink_ref[...] += jnp.concatenate(dsink_rows, axis=0)

            @pl.when(i == 0)
            def _():
                first_write(slot).start()

            @pl.when(i > 0)
            def _():
                tile_write(i, slot).start()

        @pl.when(i == nt)
        def _():
            stage[slot, 0:BLOCK, 0:KV_WIDTH] = dk_c[...].astype(BF16)
            stage[slot, 0:BLOCK, KV_WIDTH:] = dv_c[...].astype(BF16)
            last_write(slot).start()
            wait_write(i - 1, 1 - slot)
            last_write(slot).wait()

    tile_of = lambda i: jnp.minimum(i, nt - 1)
    tile = pl.BlockSpec((tm, ATTN_WIDTH), lambda i: (tile_of(i), 0))
    return _hosting_call(
        body, "attn_bwd", nt + 1,
        [pl.BlockSpec(memory_space=pltpu.SMEM), _resident((N_BIAS, 2 * BLOCK, GQA_GROUP * BLOCK)), tile]
        + _attn_specs(tm, tile_of),
        [tile, ANY_SPACE, _full((ATTN_HEADS, LANES))],
        [jax.ShapeDtypeStruct((rows, ATTN_WIDTH), BF16), jax.ShapeDtypeStruct((rows, 2 * KV_WIDTH), BF16),
         jax.ShapeDtypeStruct((ATTN_HEADS, LANES), F32)],
        [pltpu.VMEM((BLOCK, KV_WIDTH), F32), pltpu.VMEM((BLOCK, KV_WIDTH), F32),
         pltpu.VMEM((2, tm, 2 * KV_WIDTH), BF16), pltpu.SemaphoreType.DMA((2,))],
        (sinks, bias, dattn, qkv, qkv, qkv, qkv, qkv), carried, modes)


ROW_CONV_B, ROW_B_A, ROW_B_X, ROW_LAMBDA = 4, 5, 6, 7


def _rec_bwd(drec, zrec, h, kept, conv_w, wa_bd, wx_bd, lam, carried, modes):
    rows = zrec.shape[0]
    tm = _rec_tile(rows)
    nt = rows // tm
    per = tm // SUBLANES

    def body(drec_ref, xr_ref, yr_ref, h_ref, xc_ref, a_ref, mult_ref, r_ref, ig_ref, hhalo_ref, cw_ref, wa_ref, wx_ref,
             lam_ref, drz_ref, small_ref, dwa_ref, dwx_ref, hbuf, dbuf, dgr_s, dgi_s, dyr_s, carry):
        s = pl.program_id(0)
        i = nt - 1 - s

        @pl.when(s == 0)
        def _():
            small_ref[...] = jnp.zeros_like(small_ref)
            dwa_ref[...] = jnp.zeros_like(dwa_ref)
            dwx_ref[...] = jnp.zeros_like(dwx_ref)
            carry[...] = jnp.zeros_like(carry)
            dbuf[tm:tm + SUBLANES, :] = jnp.zeros((SUBLANES, LRU_WIDTH), F32)

        hbuf[0:SUBLANES, :] = jnp.where(i == 0, 0.0, hhalo_ref[...])
        hbuf[SUBLANES:SUBLANES + tm, :] = h_ref[...]

        row = lax.broadcasted_iota(jnp.int32, (SUBLANES, LRU_WIDTH), 0)
        log_a_scale = (-LRU_C) * _softplus(-lam_ref[...])
        zeros = jnp.zeros((SUBLANES, LRU_WIDTH), F32)

        def group(k, state):
            g_later, a_later, sum_dgr, sum_dgi, sum_lam = state
            o = pl.multiple_of((per - 1 - k) * SUBLANES, SUBLANES)
            rows8 = pl.ds(o, SUBLANES)
            yr, drec_t, h_t, a = yr_ref[rows8, :], drec_ref[rows8, :], h_ref[rows8, :], a_ref[rows8, :]
            gel, t = _gelu(yr)
            dyr_s[rows8, :] = drec_t * h_t * _gelu_grad(yr, t)
            u = drec_t * gel
            coef = jnp.where(row == SUBLANES - 1, a_later, pltpu.roll(a, SUBLANES - 1, 0))
            for sft in (1, 2, 4):
                keep = row < SUBLANES - sft
                u = jnp.where(keep, coef * pltpu.roll(u, SUBLANES - sft, 0) + u, u)
                coef = jnp.where(keep, coef * pltpu.roll(coef, SUBLANES - sft, 0), coef)
            g = coef * g_later + u
            du = jnp.where(i * tm + o + row >= PAD_ROWS, g, 0.0)
            h_before = jnp.where(row == 0, hbuf[rows8, :][SUBLANES - 1:SUBLANES, :], pltpu.roll(h_t, 1, 0))
            xc, mult, r, ig = xc_ref[rows8, :], mult_ref[rows8, :], r_ref[rows8, :], ig_ref[rows8, :]
            dbuf[rows8, :] = du * (mult * ig)
            dgi = (du * (mult * xc)) * (ig * (1.0 - ig))
            dgi_s[rows8, :] = dgi
            dlog_a = (g * h_before) * a - (du * (ig * xc)) * (a * a * pl.reciprocal(mult, approx=True))
            dgr = (dlog_a * log_a_scale) * (r * (1.0 - r))
            dgr_s[rows8, :] = dgr
            return g[0:1, :], a[0:1, :], sum_dgr + dgr, sum_dgi + dgi, sum_lam + dlog_a * r

        state = lax.fori_loop(0, per, group, (carry[0:1, :], carry[1:2, :], zeros, zeros, zeros))
        carry[0:1, :], carry[1:2, :] = state[0], state[1]
        sum_dgr, sum_dgi, sum_lam = (jnp.sum(v, axis=0, keepdims=True) for v in state[2:])
        dlam = sum_lam * (LRU_C * _sigmoid(-lam_ref[...]))

        dgr_b = [dgr_s[:, hh * LRU_HALF:(hh + 1) * LRU_HALF].astype(BF16) for hh in range(2)]
        dgi_b = [dgi_s[:, hh * LRU_HALF:(hh + 1) * LRU_HALF].astype(BF16) for hh in range(2)]
        halves = _lru_halves(xc_ref[...])
        for hh in range(2):
            dwa_ref[hh] += _mm_tn(halves[hh], dgr_b[hh])
            dwx_ref[hh] += _mm_tn(halves[hh], dgi_b[hh])
        dxc = dbuf[0:tm, :] + jnp.concatenate(
            [_mm_nt(dgr_b[hh], wa_ref[hh]) + _mm_nt(dgi_b[hh], wx_ref[hh]) for hh in range(2)], axis=1)

        dbuf[0:tm, :] = dxc
        sum_dxc = jnp.sum(dxc, axis=0, keepdims=True)
        ahead = [dbuf[pl.ds(CONV_WIDTH - 1 - j, tm), :] for j in range(CONV_WIDTH)]
        drz_ref[:, 0:LRU_WIDTH] = sum(cw_ref[j:j + 1, :] * ahead[j] for j in range(CONV_WIDTH)).astype(BF16)
        drz_ref[:, LRU_WIDTH:] = dyr_s[...].astype(BF16)
        upd = [jnp.sum(xr_ref[...] * ahead[j], axis=0, keepdims=True) for j in range(CONV_WIDTH)]
        dbuf[tm:tm + SUBLANES, :] = dbuf[0:SUBLANES, :]
        small_ref[...] += jnp.concatenate(upd + [sum_dxc, sum_dgr, sum_dgi, dlam], axis=0)

    rev = lambda s: nt - 1 - s
    halo = lambda s: jnp.maximum(rev(s) * per - 1, 0)
    cols = lambda k: pl.BlockSpec((tm, LRU_WIDTH), lambda s: (rev(s), k))
    halo0 = pl.BlockSpec((SUBLANES, LRU_WIDTH), lambda s: (halo(s), 0))
    bd = _full((2, LRU_HALF, LRU_HALF))
    big = pltpu.VMEM((tm + SUBLANES, LRU_WIDTH), F32)
    tile = pltpu.VMEM((tm, LRU_WIDTH), F32)
    kept_cols = [cols(k) for k in (KEPT_XC, KEPT_A, KEPT_MULT, KEPT_R, KEPT_I)]
    return _hosting_call(
        body, "rec_bwd", nt,
        [cols(0), cols(0), cols(1), cols(0)] + kept_cols
        + [halo0, _full((CONV_WIDTH, LRU_WIDTH)), bd, bd, _full((1, LRU_WIDTH))],
        [pl.BlockSpec((tm, 2 * LRU_WIDTH), lambda s: (rev(s), 0)), _full((SUBLANES, LRU_WIDTH)), bd, bd],
        [jax.ShapeDtypeStruct((rows, 2 * LRU_WIDTH), BF16), jax.ShapeDtypeStruct((SUBLANES, LRU_WIDTH), F32),
         jax.ShapeDtypeStruct((2, LRU_HALF, LRU_HALF), F32), jax.ShapeDtypeStruct((2, LRU_HALF, LRU_HALF), F32)],
        [big, big, tile, tile, tile, pltpu.VMEM((SUBLANES, LRU_WIDTH), F32)],
        (drec, zrec, zrec, h) + (kept,) * N_KEPT + (h, conv_w, wa_bd, wx_bd, lam), carried, modes)


DZ_CUTS = (0, ATTN_WIDTH, QKV_WIDTH, IN_WIDTH)


def _dz_specs(tm):
    return [pl.BlockSpec((tm, DZ_CUTS[p + 1] - DZ_CUTS[p]), lambda i: (i, 0)) for p in range(3)]


def _in_proj_bwd_x(head, x, g1, dh1, dq, dkv, drz, w_in, carried, modes):
    rows = dh1.shape[0]
    tm = _row_tile(rows)
    steps = rows // tm

    def body(head_ref, g_ref, dh1_ref, dq_ref, dkv_ref, drz_ref, w_ref, x_hbm, dh0_ref, dg_ref, buf, sem):
        i = pl.program_id(0)
        h0 = _h0_tile(head_ref, x_hbm, buf, sem, i, steps, tm)

        @pl.when(i == 0)
        def _():
            dg_ref[...] = jnp.zeros_like(dg_ref)

        g = g_ref[...]
        _, xhat, rstd = _rms_fwd(h0, g)
        parts = (dq_ref[...], dkv_ref[...], drz_ref[...])
        du = sum(_mm_nt(parts[p], w_ref[:, DZ_CUTS[p]:DZ_CUTS[p + 1]]) for p in range(3))
        dx, dg = _rms_bwd(du, xhat, rstd, g)
        dh0_ref[...] = dh1_ref[...] + dx
        dg_ref[...] += dg

    wide = pl.BlockSpec((tm, D_MODEL), lambda i: (i, 0))
    return _hosting_call(
        body, "in_proj_bwd_x", steps,
        [_full((BLOCK, D_MODEL)), _full((1, D_MODEL)), wide] + _dz_specs(tm) + [_resident((D_MODEL, IN_WIDTH)), ANY_SPACE],
        [wide, _full((1, D_MODEL))],
        [jax.ShapeDtypeStruct((rows, D_MODEL), F32), jax.ShapeDtypeStruct((1, D_MODEL), F32)],
        _frame_scratch(tm), (head, g1, dh1, dq, dkv, drz, w_in, x), carried, modes)


def _in_proj_bwd_w(u1, dq, dkv, drz, carried, modes):
    rows = u1.shape[0]
    tb = _big_tile(rows)
    steps = rows // tb

    def body(u_ref, dq_ref, dkv_ref, drz_ref, dw_ref, acc):
        i = pl.program_id(0)

        @pl.when(i == 0)
        def _():
            acc[...] = jnp.zeros_like(acc)

        u = u_ref[...]
        for p, ref in enumerate((dq_ref, dkv_ref, drz_ref)):
            acc[:, DZ_CUTS[p]:DZ_CUTS[p + 1]] += _mm_tn(u, ref[...])

        @pl.when(i == steps - 1)
        def _():
            dw_ref[...] = acc[...].astype(BF16)

    return _hosting_call(
        body, "in_proj_bwd_w", steps,
        [pl.BlockSpec((tb, D_MODEL), lambda i: (i, 0))] + _dz_specs(tb),
        [_full((D_MODEL, IN_WIDTH))],
        [jax.ShapeDtypeStruct((D_MODEL, IN_WIDTH), BF16)],
        [pltpu.VMEM((D_MODEL, IN_WIDTH), F32)], (u1, dq, dkv, drz), carried, modes)


def _adamw_math(w, m, v, g):
    nm = ADAM_B1 * m + (1.0 - ADAM_B1) * g
    nv = ADAM_B2 * v + (1.0 - ADAM_B2) * (g * g)
    m_hat = nm / (1.0 - ADAM_B1 ** ADAM_STEP)
    v_hat = nv / (1.0 - ADAM_B2 ** ADAM_STEP)
    return (-ADAM_LR) * (m_hat / (jnp.sqrt(v_hat) + ADAM_EPS) + ADAM_WD * w), nm, nv


SMALL_NAMES = ("conv_b", "b_a", "b_x", "lru_lambda", "attn_sinks", "g_post_mix", "g_pre_ffn", "g_post_ffn")
PACK_WIDTH = 1024


def _pack_rows(vals):
    assert len(SMALL_NAMES) == SUBLANES
    row = lax.broadcasted_iota(jnp.int32, (SUBLANES, PACK_WIDTH), 0)
    tile = jnp.zeros((SUBLANES, PACK_WIDTH), F32)
    for k, name in enumerate(SMALL_NAMES):
        a = vals[name].reshape(1, -1)
        tile = jnp.where(row == k, jnp.pad(a, ((0, 0), (0, PACK_WIDTH - a.shape[1]))), tile)
    return tile


def _adamw_small(weights, mom_m, mom_v, parts, loss_parts, others):
    names = list(SMALL_NAMES) + [name for name, _, _ in others]
    views = [(1, weights[name].size) for name in SMALL_NAMES] + [view for _, view, _ in others]
    n, n_pack = len(names), len(SMALL_NAMES)

    def body(*refs):
        w_refs, m_refs, v_refs = refs[:n], refs[n:2 * n], refs[2 * n:3 * n]
        p_ref, l_ref = refs[3 * n], refs[3 * n + 1]
        o_refs = refs[3 * n + 2:3 * n + 2 + len(others)]
        loss_ref, outs = refs[3 * n + 2 + len(others)], refs[3 * n + 3 + len(others):]
        for k, (_, c) in enumerate(views):
            if k < n_pack:
                g = p_ref[0, k:k + 1, 0:c]
                for s in range(1, N_DEV):
                    g = g + p_ref[s, k:k + 1, 0:c]
            else:
                g = o_refs[k - n_pack][0]
                for s in range(1, N_DEV):
                    g = g + o_refs[k - n_pack][s]
            g_ref, d_ref, nm_ref, nv_ref = outs[4 * k:4 * k + 4]
            g_ref[...] = g
            d_ref[...], nm_ref[...], nv_ref[...] = _adamw_math(w_refs[k][...], m_refs[k][...], v_refs[k][...], g)
        total = l_ref[0]
        for s in range(1, N_DEV):
            total = total + l_ref[s]
        loss_ref[...] = total

    args = [src[name].reshape(view) for src in (weights, mom_m, mom_v) for name, view in zip(names, views)]
    res = pl.pallas_call(
        body, name="adamw_small",
        out_shape=[jax.ShapeDtypeStruct(loss_parts.shape[1:], F32)]
                  + [jax.ShapeDtypeStruct(view, F32) for view in views for _ in range(4)],
        compiler_params=pltpu.CompilerParams(vmem_limit_bytes=VMEM_LIMIT),
    )(*args, parts, loss_parts, *[p for _, _, p in others])
    out = {name: tuple(t.reshape(weights[name].shape) for t in res[1 + 4 * k:5 + 4 * k]) for k, name in enumerate(names)}
    return res[0], out


def _adamw(w, m, v, parts, name):
    rows, cols = w.shape
    tr = next((t for t in (256, 128) if rows % t == 0), rows)
    parts = parts if isinstance(parts, (list, tuple)) else [parts]

    def body(w_ref, m_ref, v_ref, *refs):
        p_refs, (g_ref, d_ref, nm_ref, nv_ref) = refs[:len(parts)], refs[len(parts):]

        def total(p_ref):
            g = p_ref[0].astype(F32)
            for s in range(1, N_DEV):
                g = g + p_ref[s].astype(F32)
            return g

        g = jnp.concatenate([total(p_ref) for p_ref in p_refs], axis=1) if len(parts) > 1 else total(p_refs[0])
        g_ref[...] = g
        d_ref[...], nm_ref[...], nv_ref[...] = _adamw_math(w_ref[...], m_ref[...], v_ref[...], g)

    blk = pl.BlockSpec((tr, cols), lambda i: (i, 0))
    return pl.pallas_call(
        body, name=name, grid=(rows // tr,),
        in_specs=[blk, blk, blk] + [pl.BlockSpec((N_DEV, tr, p.shape[2]), lambda i: (0, i, 0)) for p in parts],
        out_specs=[blk] * 4,
        out_shape=[jax.ShapeDtypeStruct((rows, cols), F32)] * 4,
        compiler_params=_params(("parallel",)),
    )(w, m, v, *parts)


def _cols_from_shards(g):
    return jnp.transpose(g, (1, 0, 2)).reshape(g.shape[1], N_DEV * g.shape[2])


def _cols_to_shards(a):
    r, c = a.shape
    return jnp.transpose(a.reshape(r, N_DEV, c // N_DEV), (1, 0, 2))


def _block_diag(w):
    per = LRU_HALF // LRU_BLOCK
    w = w.reshape(2, per, LRU_BLOCK, LRU_BLOCK)
    eye = jnp.eye(per, dtype=w.dtype)
    return (w[:, :, :, None, :] * eye[None, :, None, :, None]).reshape(2, LRU_HALF, LRU_HALF)


def _block_diag_extract(t):
    per = LRU_HALF // LRU_BLOCK
    t = t.reshape(2, per, LRU_BLOCK, per, LRU_BLOCK)
    return jnp.stack([t[:, b, :, b, :] for b in range(per)], axis=1).reshape(LRU_BLOCKS, LRU_BLOCK, LRU_BLOCK)


def kernel(x, meta_tokens, g_pre_mix, w_in, conv_w, conv_b, w_a, b_a, w_x, b_x, lru_lambda, attn_sinks, w_out, g_post_mix, g_pre_ffn, w_ff1, w_ff2, g_post_ffn, loss_target, m_meta_tokens, m_g_pre_mix, m_w_in, m_conv_w, m_conv_b, m_w_a, m_b_a, m_w_x, m_b_x, m_lru_lambda, m_attn_sinks, m_w_out, m_g_post_mix, m_g_pre_ffn, m_w_ff1, m_w_ff2, m_g_post_ffn, v_meta_tokens, v_g_pre_mix, v_w_in, v_conv_w, v_conv_b, v_w_a, v_b_a, v_w_x, v_b_x, v_lru_lambda, v_attn_sinks, v_w_out, v_g_post_mix, v_g_pre_ffn, v_w_ff1, v_w_ff2, v_g_post_ffn):
    weights = dict(meta_tokens=meta_tokens, g_pre_mix=g_pre_mix, w_in=w_in, conv_w=conv_w, conv_b=conv_b, w_a=w_a,
                   b_a=b_a, w_x=w_x, b_x=b_x, lru_lambda=lru_lambda, attn_sinks=attn_sinks, w_out=w_out,
                   g_post_mix=g_post_mix, g_pre_ffn=g_pre_ffn, w_ff1=w_ff1, w_ff2=w_ff2, g_post_ffn=g_post_ffn)
    mom_m = dict(meta_tokens=m_meta_tokens, g_pre_mix=m_g_pre_mix, w_in=m_w_in, conv_w=m_conv_w, conv_b=m_conv_b,
                 w_a=m_w_a, b_a=m_b_a, w_x=m_w_x, b_x=m_b_x, lru_lambda=m_lru_lambda, attn_sinks=m_attn_sinks,
                 w_out=m_w_out, g_post_mix=m_g_post_mix, g_pre_ffn=m_g_pre_ffn, w_ff1=m_w_ff1, w_ff2=m_w_ff2,
                 g_post_ffn=m_g_post_ffn)
    mom_v = dict(meta_tokens=v_meta_tokens, g_pre_mix=v_g_pre_mix, w_in=v_w_in, conv_w=v_conv_w, conv_b=v_conv_b,
                 w_a=v_w_a, b_a=v_b_a, w_x=v_w_x, b_x=v_b_x, lru_lambda=v_lru_lambda, attn_sinks=v_attn_sinks,
                 w_out=v_w_out, g_post_mix=v_g_post_mix, g_pre_ffn=v_g_pre_ffn, w_ff1=v_w_ff1, w_ff2=v_w_ff2,
                 g_post_ffn=v_g_post_ffn)
    order = list(weights)

    (g_win, g_meta, g_cw) = _gather_two_level([w_in[0].astype(BF16), meta_tokens, conv_w[0]], "gather_first")
    w_in_full = _cols_from_shards(g_win)
    meta_full = _cols_from_shards(g_meta)
    conv_w_full = _cols_from_shards(g_cw)

    head = jnp.concatenate([jnp.zeros((PAD_ROWS, D_MODEL), F32), meta_full], axis=0)
    wa_bd = _block_diag(w_a[0]).astype(BF16)
    wx_bd = _block_diag(w_x[0]).astype(BF16)
    bias = _attn_bias()

    w1_shard = w_ff1[0].astype(BF16)
    (qkv, zrec, u1, attn), (g_wout, w1a) = _in_proj_attn_fwd(
        head, x[0], g_pre_mix, w_in_full, attn_sinks, bias,
        [w_out[0].astype(BF16), w1_shard[:, :FF_HALF]], ["gather"] * 2)
    (rec, h_lru, kept), (w1b,) = _rec_fwd(zrec, conv_w_full, conv_b, wa_bd, b_a, wx_bd, b_x, lru_lambda,
                                         [w1_shard[:, FF_HALF:]], ["gather"])
    w_out_full = g_wout.reshape(D_MODEL, D_MODEL)
    w2_shard = w_ff2[0].astype(BF16)
    (mix, h1, act, u2), (w2a, w2b) = _mix_and_ffn_up(
        attn, rec, w_out_full, head, x[0], g_post_mix, g_pre_ffn, (w1a, w1b),
        [w2_shard[:FF_HALF], w2_shard[FF_HALF:]], ["gather"] * 2)
    w2_halves = [w.reshape(D_FF // 2, D_MODEL) for w in (w2a, w2b)]
    (dy, df, dg_post_ffn, loss_acc), w2t_halves = _ffn_down_loss(
        act, w2_halves, h1, loss_target[0], g_post_ffn, [w2_shard[:FF_HALF].T, w2_shard[FF_HALF:].T], ["gather"] * 2)

    (da1,), (w1ta,) = _ffn_bwd_act(df, w2t_halves, act, [w1_shard[:, :FF_HALF].T], ["gather"])
    (dw1h, dw2g), (w1tb,) = _ffn_bwd_weights(u2, da1, act, df, [w1_shard[:, FF_HALF:].T], ["gather"])
    w1t_halves = [w.reshape(D_FF // 2, D_MODEL) for w in (w1ta, w1tb)]
    (dh1, dg_pre_ffn), (p_w1a,) = _ffn_bwd_x(da1, w1t_halves, h1, dy, g_pre_ffn, [dw1h[0]], ["scatter"])
    (dattn, drec, dw_out, dg_post_mix), (p_w1b,) = _out_proj_bwd(dh1, mix, g_post_mix, w_out_full.T, attn, rec,
                                                                [dw1h[1]], ["scatter"])
    (dq, dkv, dsinks), (p_w2,) = _attn_bwd(qkv, dattn, attn_sinks, bias, [dw2g], ["scatter"])
    (drz, rec_small, dwa_bd, dwx_bd), (p_wout,) = _rec_bwd(
        drec, zrec, h_lru, kept, conv_w_full, wa_bd, wx_bd, lru_lambda,
        [dw_out.reshape(N_DEV, D_MODEL // N_DEV, D_MODEL)], ["scatter"])
    small_grads = dict(
        conv_b=rec_small[ROW_CONV_B], b_a=rec_small[ROW_B_A], b_x=rec_small[ROW_B_X], lru_lambda=rec_small[ROW_LAMBDA],
        attn_sinks=dsinks[:, 0], g_post_mix=dg_post_mix, g_pre_ffn=dg_pre_ffn, g_post_ffn=dg_post_ffn)
    gate_rows = (LRU_BLOCKS * LRU_BLOCK, LRU_BLOCK)
    gate_dense = (LRU_BLOCKS * LRU_BLOCK * LRU_BLOCK // PACK_WIDTH, PACK_WIDTH)
    (dw_in,), (p_cw, p_small, p_wa, p_wx) = _in_proj_bwd_w(
        u1, dq, dkv, drz,
        [_cols_to_shards(rec_small[0:CONV_WIDTH]), _pack_rows(small_grads),
         _block_diag_extract(dwa_bd).reshape(gate_dense), _block_diag_extract(dwx_bd).reshape(gate_dense)],
        ["scatter", "gather", "gather", "gather"])
    p_wa, p_wx = (p.reshape((N_DEV,) + gate_rows) for p in (p_wa, p_wx))
    (dh0, dg_pre_mix), (p_win,) = _in_proj_bwd_x(
        head, x[0], g_pre_mix, dh1, dq, dkv, drz, w_in_full, [_cols_to_shards(dw_in)], ["scatter"])
    p_meta, p_gpm, p_loss = _exchange([_cols_to_shards(dh0[PAD_ROWS:BLOCK]), dg_pre_mix, loss_acc],
                                      ["scatter", "gather", "gather"], "exchange_last")

    res = {}
    res["w_in"] = _adamw(w_in[0], m_w_in[0], v_w_in[0], p_win, "adamw_w_in")
    res["w_out"] = _adamw(w_out[0], m_w_out[0], v_w_out[0], p_wout, "adamw_w_out")
    res["w_ff1"] = _adamw(w_ff1[0], m_w_ff1[0], v_w_ff1[0], [p_w1a, p_w1b], "adamw_w_ff1")
    res["w_ff2"] = _adamw(w_ff2[0], m_w_ff2[0], v_w_ff2[0], p_w2, "adamw_w_ff2")
    for name in ("w_in", "w_out", "w_ff1", "w_ff2"):
        res[name] = tuple(t[None] for t in res[name])
    others = [("g_pre_mix", g_pre_mix.shape, p_gpm), ("meta_tokens", meta_tokens.shape, p_meta),
              ("conv_w", conv_w.shape[1:], p_cw), ("w_a", gate_rows, p_wa), ("w_x", gate_rows, p_wx)]
    loss_total, small = _adamw_small(weights, mom_m, mom_v, p_small, p_loss, others)
    res.update(small)

    grad_x = dh0[BLOCK:][None]
    outs = [loss_total[0, 0], grad_x]
    for k in range(4):
        outs += [res[name][k] for name in order]
    return tuple(outs)
```
